```python
import jax, jax.numpy as jnp
from jax import lax
import numpy as np

D_MODEL = 1024
BATCH = 8
SEQ = 2048
DEPTH = 1

CHUNK = 64
QUERY_BLOCK = 128
FOX_HEAD_DIM = 64
N_FOX_HEADS = (D_MODEL // 2) // FOX_HEAD_DIM
D_FOX = N_FOX_HEADS * FOX_HEAD_DIM
GDN_HEAD_DIM = 128
N_GDN_HEADS = (D_MODEL // 2) // GDN_HEAD_DIM
D_GDN = N_GDN_HEADS * GDN_HEAD_DIM
D_MIX = D_FOX + D_GDN
CONV_K = 4
D_FF = 4 * D_MODEL
EPS = 1e-6
PROJ_SIZES = (D_FOX, D_FOX, D_FOX, N_FOX_HEADS, D_GDN, D_GDN, D_GDN, N_GDN_HEADS, N_GDN_HEADS, D_GDN)
D_PROJ = sum(PROJ_SIZES)

kernel_name = "fox_gdn_hymba_sandwich_block"


def rms_norm(x, w):
    xf = x.astype(jnp.float32)
    y = xf * lax.rsqrt(jnp.mean(xf * xf, axis=-1, keepdims=True) + EPS)
    return (y * w.astype(jnp.float32)).astype(x.dtype)


def causal_depthwise_conv(x, w):
    k, c = w.shape
    return lax.conv_general_dilated(
        x, w[:, None, :].astype(x.dtype), window_strides=(1,), padding=[(k - 1, 0)],
        dimension_numbers=("NWC", "WIO", "NWC"), feature_group_count=c)


def forgetting_attention(q, k, v, log_f):
    b, s, h, d = q.shape
    scale = d ** -0.5
    cum = jnp.cumsum(log_f, axis=1).transpose(0, 2, 1)
    outs = []
    for i in range(s // QUERY_BLOCK):
        qs, qe = i * QUERY_BLOCK, (i + 1) * QUERY_BLOCK
        qb = q[:, qs:qe]
        kb = k[:, :qe]
        vb = v[:, :qe]
        logits = jnp.einsum("bqhd,bkhd->bhqk", qb, kb).astype(jnp.float32) * scale
        logits = logits + cum[:, :, qs:qe, None] - cum[:, :, None, :qe]
        mask = jnp.arange(qe)[None, :] <= (qs + jnp.arange(QUERY_BLOCK))[:, None]
        logits = jnp.where(mask, logits, -jnp.inf)
        p = jax.nn.softmax(logits, axis=-1)
        outs.append(jnp.einsum("bhqk,bkhd->bqhd", p.astype(v.dtype), vb))
    return jnp.concatenate(outs, axis=1)


def gated_delta_chunked(q, k, v, g, beta):
    b, s, h, dk = q.shape
    dv = v.shape[-1]
    n = s // CHUNK

    def to_chunks(t):
        return t.reshape(b, n, CHUNK, h, t.shape[-1]).transpose(0, 3, 1, 2, 4)

    q, k, v = to_chunks(q), to_chunks(k), to_chunks(v)
    g = g.reshape(b, n, CHUNK, h).transpose(0, 3, 1, 2)
    beta = beta.reshape(b, n, CHUNK, h).transpose(0, 3, 1, 2)
    gc = jnp.cumsum(g, axis=-1)
    incl = jnp.tril(jnp.ones((CHUNK, CHUNK), dtype=bool))
    strict = jnp.tril(jnp.ones((CHUNK, CHUNK), dtype=bool), k=-1)
    decay = jnp.exp(jnp.where(incl, gc[..., :, None] - gc[..., None, :], -jnp.inf))
    k_beta = k * beta[..., None]
    v_beta = v * beta[..., None]
    m = jnp.where(strict, jnp.einsum("bhncd,bhnkd->bhnck", k_beta, k) * decay, 0.0)
    eye = jnp.broadcast_to(jnp.eye(CHUNK, dtype=m.dtype), m.shape)
    t_inv = lax.linalg.triangular_solve(eye + m, eye, left_side=True, lower=True,
                                        unit_diagonal=True)
    u = jnp.einsum("bhnck,bhnkv->bhncv", t_inv, v_beta)
    w = jnp.einsum("bhnck,bhnkd->bhncd", t_inv, k_beta * jnp.exp(gc)[..., None])
    a_intra = jnp.einsum("bhncd,bhnkd->bhnck", q, k) * decay
    q_dec = q * jnp.exp(gc)[..., None]
    k_dec = k * jnp.exp(gc[..., -1:] - gc)[..., None]
    g_last = jnp.exp(gc[..., -1])

    def step(state, inp):
        qd, kd, uc, wc, ac, gl = inp
        v_new = uc - jnp.einsum("bhcd,bhdv->bhcv", wc, state)
        o = jnp.einsum("bhcd,bhdv->bhcv", qd, state) + jnp.einsum("bhck,bhkv->bhcv", ac, v_new)
        state = state * gl[..., None, None] + jnp.einsum("bhcd,bhcv->bhdv", kd, v_new)
        return state, o

    def chunk_major(t):
        return jnp.moveaxis(t, 2, 0)

    state0 = jnp.zeros((b, h, dk, dv), dtype=jnp.float32)
    _, o = lax.scan(step, state0, (chunk_major(q_dec), chunk_major(k_dec), chunk_major(u),
                                   chunk_major(w), chunk_major(a_intra), chunk_major(g_last)))
    return o.transpose(1, 0, 3, 2, 4).reshape(b, s, h, dv)


def _fwd_setup_inputs(seed: int = 0) -> dict:
    key = jax.random.key(seed)
    ks = jax.random.split(key, 16)
    f32 = jnp.float32

    def gain(k, n):
        return 1.0 + 0.1 * jax.random.normal(k, (n,), f32)

    x = jax.random.normal(ks[0], (BATCH, SEQ, D_MODEL), f32)
    pre_mix_norm = gain(ks[1], D_MODEL)
    w_in = jax.random.normal(ks[2], (D_MODEL, D_PROJ), f32) * D_MODEL ** -0.5
    fox_f_bias = 3.0 + 0.5 * jax.random.normal(ks[3], (N_FOX_HEADS,), f32)
    fox_out_norm = gain(ks[4], FOX_HEAD_DIM)
    gdn_conv_w = jax.random.normal(ks[5], (CONV_K, 3 * D_GDN), f32) * CONV_K ** -0.5
    gdn_a_log = jnp.log(jax.random.uniform(ks[6], (N_GDN_HEADS,), f32, 1.0, 16.0))
    dt = jnp.exp(jax.random.uniform(ks[7], (N_GDN_HEADS,), f32, np.log(1e-3), np.log(1e-1)))
    gdn_dt_bias = dt + jnp.log(-jnp.expm1(-dt))
    gdn_out_norm = gain(ks[8], GDN_HEAD_DIM)
    w_out = jax.random.normal(ks[9], (D_MIX, D_MODEL), f32) * D_MIX ** -0.5
    post_mix_norm = gain(ks[10], D_MODEL)
    pre_mlp_norm = gain(ks[11], D_MODEL)
    w_up = jax.random.normal(ks[12], (D_MODEL, D_FF), f32) * D_MODEL ** -0.5
    w_down = jax.random.normal(ks[13], (D_FF, D_MODEL), f32) * D_FF ** -0.5
    post_mlp_norm = gain(ks[14], D_MODEL)
    return {"x": x, "pre_mix_norm": pre_mix_norm, "w_in": w_in, "fox_f_bias": fox_f_bias,
            "fox_out_norm": fox_out_norm, "gdn_conv_w": gdn_conv_w, "gdn_a_log": gdn_a_log,
            "gdn_dt_bias": gdn_dt_bias, "gdn_out_norm": gdn_out_norm, "w_out": w_out,
            "post_mix_norm": post_mix_norm, "pre_mlp_norm": pre_mlp_norm, "w_up": w_up,
            "w_down": w_down, "post_mlp_norm": post_mlp_norm}


def _fwd_reference(x, pre_mix_norm, w_in, fox_f_bias, fox_out_norm, gdn_conv_w, gdn_a_log,
              gdn_dt_bias, gdn_out_norm, w_out, post_mix_norm, pre_mlp_norm, w_up, w_down,
              post_mlp_norm):
    b, s, _ = x.shape
    split_at = [int(v) for v in np.cumsum(PROJ_SIZES)[:-1]]
    for _layer in range(DEPTH):
        h = rms_norm(x, pre_mix_norm)
        proj = h @ w_in
        fq, fk, fv, ff, gq, gk, gv, gb, ga, gz = jnp.split(proj, split_at, axis=-1)

        fq = fq.reshape(b, s, N_FOX_HEADS, FOX_HEAD_DIM)
        fk = fk.reshape(b, s, N_FOX_HEADS, FOX_HEAD_DIM)
        fv = fv.reshape(b, s, N_FOX_HEADS, FOX_HEAD_DIM)
        log_f = jax.nn.log_sigmoid(ff.astype(jnp.float32) + fox_f_bias.astype(jnp.float32))
        fox_o = forgetting_attention(fq, fk, fv, log_f)
        fox_o = rms_norm(fox_o, fox_out_norm).reshape(b, s, D_FOX)

        qkv = jax.nn.silu(causal_depthwise_conv(jnp.concatenate([gq, gk, gv], axis=-1), gdn_conv_w))
        qkv = qkv.astype(jnp.float32)
        cq, ck, cv = jnp.split(qkv, [D_GDN, 2 * D_GDN], axis=-1)
        cq = cq.reshape(b, s, N_GDN_HEADS, GDN_HEAD_DIM)
        ck = ck.reshape(b, s, N_GDN_HEADS, GDN_HEAD_DIM)
        cv = cv.reshape(b, s, N_GDN_HEADS, GDN_HEAD_DIM)
        cq = cq * lax.rsqrt(jnp.sum(cq * cq, axis=-1, keepdims=True) + EPS) * GDN_HEAD_DIM ** -0.5
        ck = ck * lax.rsqrt(jnp.sum(ck * ck, axis=-1, keepdims=True) + EPS)
        beta = jax.nn.sigmoid(gb.astype(jnp.float32))
        g = -jnp.exp(gdn_a_log.astype(jnp.float32)) * jax.nn.softplus(
            ga.astype(jnp.float32) + gdn_dt_bias.astype(jnp.float32))
        gdn_o = gated_delta_chunked(cq, ck, cv, g, beta)
        gate = jax.nn.silu(gz.astype(jnp.float32)).reshape(b, s, N_GDN_HEADS, GDN_HEAD_DIM)
        gdn_o = (rms_norm(gdn_o, gdn_out_norm) * gate).astype(x.dtype).reshape(b, s, D_GDN)

        mixed = jnp.concatenate([fox_o, gdn_o], axis=-1) @ w_out
        x = x + rms_norm(mixed, post_mix_norm)

        h = rms_norm(x, pre_mlp_norm)
        y = jnp.square(jax.nn.relu(h @ w_up)) @ w_down
        x = x + rms_norm(y, post_mlp_norm)
    return x


import jax as _jax
import jax.numpy as _jnp

TWIN_FORMAT = 'train_step'
FWD_PARAMS = ['x', 'pre_mix_norm', 'w_in', 'fox_f_bias', 'fox_out_norm', 'gdn_conv_w', 'gdn_a_log', 'gdn_dt_bias', 'gdn_out_norm', 'w_out', 'post_mix_norm', 'pre_mlp_norm', 'w_up', 'w_down', 'post_mlp_norm']
TWIN_WEIGHTS = ['pre_mix_norm', 'w_in', 'fox_f_bias', 'fox_out_norm', 'gdn_conv_w', 'gdn_a_log', 'gdn_dt_bias', 'gdn_out_norm', 'w_out', 'post_mix_norm', 'pre_mlp_norm', 'w_up', 'w_down', 'post_mlp_norm']
TWIN_DIFF_INPUT = 'x'
TWIN_INPUTS = ['x', 'pre_mix_norm', 'w_in', 'fox_f_bias', 'fox_out_norm', 'gdn_conv_w', 'gdn_a_log', 'gdn_dt_bias', 'gdn_out_norm', 'w_out', 'post_mix_norm', 'pre_mlp_norm', 'w_up', 'w_down', 'post_mlp_norm', 'loss_target', 'm_pre_mix_norm', 'm_w_in', 'm_fox_f_bias', 'm_fox_out_norm', 'm_gdn_conv_w', 'm_gdn_a_log', 'm_gdn_dt_bias', 'm_gdn_out_norm', 'm_w_out', 'm_post_mix_norm', 'm_pre_mlp_norm', 'm_w_up', 'm_w_down', 'm_post_mlp_norm', 'v_pre_mix_norm', 'v_w_in', 'v_fox_f_bias', 'v_fox_out_norm', 'v_gdn_conv_w', 'v_gdn_a_log', 'v_gdn_dt_bias', 'v_gdn_out_norm', 'v_w_out', 'v_post_mix_norm', 'v_pre_mlp_norm', 'v_w_up', 'v_w_down', 'v_post_mlp_norm']
TWIN_OUTPUTS = ['loss', 'grad_x', 'grad_pre_mix_norm', 'grad_w_in', 'grad_fox_f_bias', 'grad_fox_out_norm', 'grad_gdn_conv_w', 'grad_gdn_a_log', 'grad_gdn_dt_bias', 'grad_gdn_out_norm', 'grad_w_out', 'grad_post_mix_norm', 'grad_pre_mlp_norm', 'grad_w_up', 'grad_w_down', 'grad_post_mlp_norm', 'delta_pre_mix_norm', 'delta_w_in', 'delta_fox_f_bias', 'delta_fox_out_norm', 'delta_gdn_conv_w', 'delta_gdn_a_log', 'delta_gdn_dt_bias', 'delta_gdn_out_norm', 'delta_w_out', 'delta_post_mix_norm', 'delta_pre_mlp_norm', 'delta_w_up', 'delta_w_down', 'delta_post_mlp_norm', 'new_m_pre_mix_norm', 'new_m_w_in', 'new_m_fox_f_bias', 'new_m_fox_out_norm', 'new_m_gdn_conv_w', 'new_m_gdn_a_log', 'new_m_gdn_dt_bias', 'new_m_gdn_out_norm', 'new_m_w_out', 'new_m_post_mix_norm', 'new_m_pre_mlp_norm', 'new_m_w_up', 'new_m_w_down', 'new_m_post_mlp_norm', 'new_v_pre_mix_norm', 'new_v_w_in', 'new_v_fox_f_bias', 'new_v_fox_out_norm', 'new_v_gdn_conv_w', 'new_v_gdn_a_log', 'new_v_gdn_dt_bias', 'new_v_gdn_out_norm', 'new_v_w_out', 'new_v_post_mix_norm', 'new_v_pre_mlp_norm', 'new_v_w_up', 'new_v_w_down', 'new_v_post_mlp_norm']
TWIN_LEAF_KINDS = {'loss': 'loss', 'grad_x': 'grad_x', 'grad_pre_mix_norm': 'grad_w', 'grad_w_in': 'grad_w', 'grad_fox_f_bias': 'grad_w', 'grad_fox_out_norm': 'grad_w', 'grad_gdn_conv_w': 'grad_w', 'grad_gdn_a_log': 'grad_w', 'grad_gdn_dt_bias': 'grad_w', 'grad_gdn_out_norm': 'grad_w', 'grad_w_out': 'grad_w', 'grad_post_mix_norm': 'grad_w', 'grad_pre_mlp_norm': 'grad_w', 'grad_w_up': 'grad_w', 'grad_w_down': 'grad_w', 'grad_post_mlp_norm': 'grad_w', 'delta_pre_mix_norm': 'delta_w', 'delta_w_in': 'delta_w', 'delta_fox_f_bias': 'delta_w', 'delta_fox_out_norm': 'delta_w', 'delta_gdn_conv_w': 'delta_w', 'delta_gdn_a_log': 'delta_w', 'delta_gdn_dt_bias': 'delta_w', 'delta_gdn_out_norm': 'delta_w', 'delta_w_out': 'delta_w', 'delta_post_mix_norm': 'delta_w', 'delta_pre_mlp_norm': 'delta_w', 'delta_w_up': 'delta_w', 'delta_w_down': 'delta_w', 'delta_post_mlp_norm': 'delta_w', 'new_m_pre_mix_norm': 'new_m', 'new_m_w_in': 'new_m', 'new_m_fox_f_bias': 'new_m', 'new_m_fox_out_norm': 'new_m', 'new_m_gdn_conv_w': 'new_m', 'new_m_gdn_a_log': 'new_m', 'new_m_gdn_dt_bias': 'new_m', 'new_m_gdn_out_norm': 'new_m', 'new_m_w_out': 'new_m', 'new_m_post_mix_norm': 'new_m', 'new_m_pre_mlp_norm': 'new_m', 'new_m_w_up': 'new_m', 'new_m_w_down': 'new_m', 'new_m_post_mlp_norm': 'new_m', 'new_v_pre_mix_norm': 'new_v', 'new_v_w_in': 'new_v', 'new_v_fox_f_bias': 'new_v', 'new_v_fox_out_norm': 'new_v', 'new_v_gdn_conv_w': 'new_v', 'new_v_gdn_a_log': 'new_v', 'new_v_gdn_dt_bias': 'new_v', 'new_v_gdn_out_norm': 'new_v', 'new_v_w_out': 'new_v', 'new_v_post_mix_norm': 'new_v', 'new_v_pre_mlp_norm': 'new_v', 'new_v_w_up': 'new_v', 'new_v_w_down': 'new_v', 'new_v_post_mlp_norm': 'new_v'}


def _forward(args):
    return _fwd_reference(*[args[k] for k in FWD_PARAMS])


def _output_shape():
    out = _jax.eval_shape(lambda: _forward(_fwd_setup_inputs(0)))
    return out.shape, out.dtype

N_MICROBATCH = 1
ADAM_LR = 0.001
ADAM_B1 = 0.9
ADAM_B2 = 0.999
ADAM_EPS = 1e-08
ADAM_WD = 0.01
ADAM_STEP = 10
PER_EXAMPLE_BATCH_AXIS = {'x': 0, 'loss_target': 0}
SHARED_INPUTS = []
_WEIGHT_DTYPES = {'pre_mix_norm': _jnp.float32, 'w_in': _jnp.float32, 'fox_f_bias': _jnp.float32, 'fox_out_norm': _jnp.float32, 'gdn_conv_w': _jnp.float32, 'gdn_a_log': _jnp.float32, 'gdn_dt_bias': _jnp.float32, 'gdn_out_norm': _jnp.float32, 'w_out': _jnp.float32, 'post_mix_norm': _jnp.float32, 'pre_mlp_norm': _jnp.float32, 'w_up': _jnp.float32, 'w_down': _jnp.float32, 'post_mlp_norm': _jnp.float32}
MOMENT_SCALE = {'pre_mix_norm': 5.652977e-01, 'w_in': 2.935551e-01, 'fox_f_bias': 3.440527e+00, 'fox_out_norm': 2.430064e+00, 'gdn_conv_w': 4.015850e-01, 'gdn_a_log': 1.435641e+00, 'gdn_dt_bias': 1.410651e+00, 'gdn_out_norm': 2.006316e+00, 'w_out': 7.952064e-01, 'post_mix_norm': 1.613662e+01, 'pre_mlp_norm': 4.912011e-01, 'w_up': 2.626319e-01, 'w_down': 9.879426e-01, 'post_mlp_norm': 1.653209e+01}


def _to_microbatches(a, axis):
    t = _jnp.moveaxis(a, axis, 0)
    t = t.reshape((N_MICROBATCH, t.shape[0] // N_MICROBATCH) + t.shape[1:])
    return _jnp.moveaxis(t, 1, axis + 1)


def setup_inputs(seed: int = 0) -> dict:
    inp = _fwd_setup_inputs(seed)
    key = _jax.random.fold_in(_jax.random.key(seed), 7919)
    shape, _ = _output_shape()
    out = dict(inp)
    out["loss_target"] = _jax.random.normal(_jax.random.fold_in(key, 0), shape, _jnp.float32)
    for i, name in enumerate(TWIN_WEIGHTS):
        w = inp[name].astype(_jnp.float32)
        if MOMENT_SCALE is None:
            s = _jnp.sqrt(_jnp.mean(_jnp.square(w)) + 1e-30)
        else:
            s = MOMENT_SCALE[name]
        km, kv = _jax.random.split(_jax.random.fold_in(key, i + 1))
        out[name] = w
        out["m_" + name] = s * _jax.random.normal(km, w.shape, _jnp.float32)
        out["v_" + name] = (s * s) * _jax.random.uniform(kv, w.shape, _jnp.float32, 0.5, 1.5)
    if N_MICROBATCH > 1:
        for name, axis in PER_EXAMPLE_BATCH_AXIS.items():
            out[name] = _to_microbatches(out[name], axis)
    return {'x': out['x'], 'pre_mix_norm': out['pre_mix_norm'], 'w_in': out['w_in'], 'fox_f_bias': out['fox_f_bias'], 'fox_out_norm': out['fox_out_norm'], 'gdn_conv_w': out['gdn_conv_w'], 'gdn_a_log': out['gdn_a_log'], 'gdn_dt_bias': out['gdn_dt_bias'], 'gdn_out_norm': out['gdn_out_norm'], 'w_out': out['w_out'], 'post_mix_norm': out['post_mix_norm'], 'pre_mlp_norm': out['pre_mlp_norm'], 'w_up': out['w_up'], 'w_down': out['w_down'], 'post_mlp_norm': out['post_mlp_norm'], 'loss_target': out['loss_target'], 'm_pre_mix_norm': out['m_pre_mix_norm'], 'm_w_in': out['m_w_in'], 'm_fox_f_bias': out['m_fox_f_bias'], 'm_fox_out_norm': out['m_fox_out_norm'], 'm_gdn_conv_w': out['m_gdn_conv_w'], 'm_gdn_a_log': out['m_gdn_a_log'], 'm_gdn_dt_bias': out['m_gdn_dt_bias'], 'm_gdn_out_norm': out['m_gdn_out_norm'], 'm_w_out': out['m_w_out'], 'm_post_mix_norm': out['m_post_mix_norm'], 'm_pre_mlp_norm': out['m_pre_mlp_norm'], 'm_w_up': out['m_w_up'], 'm_w_down': out['m_w_down'], 'm_post_mlp_norm': out['m_post_mlp_norm'], 'v_pre_mix_norm': out['v_pre_mix_norm'], 'v_w_in': out['v_w_in'], 'v_fox_f_bias': out['v_fox_f_bias'], 'v_fox_out_norm': out['v_fox_out_norm'], 'v_gdn_conv_w': out['v_gdn_conv_w'], 'v_gdn_a_log': out['v_gdn_a_log'], 'v_gdn_dt_bias': out['v_gdn_dt_bias'], 'v_gdn_out_norm': out['v_gdn_out_norm'], 'v_w_out': out['v_w_out'], 'v_post_mix_norm': out['v_post_mix_norm'], 'v_pre_mlp_norm': out['v_pre_mlp_norm'], 'v_w_up': out['v_w_up'], 'v_w_down': out['v_w_down'], 'v_post_mlp_norm': out['v_post_mlp_norm']}


def _loss(weights, diff, rest, loss_target):
    with _jax.named_scope("forward"):
        args = {**rest, TWIN_DIFF_INPUT: diff, **{k: w.astype(_WEIGHT_DTYPES[k]) for k, w in weights.items()}}
        y = _forward(args)
    with _jax.named_scope("loss_head"):
        err = _jnp.square(y.astype(_jnp.float32) - loss_target)
        return 0.5 * _jnp.sum(_jnp.mean(err, axis=-1)) if err.ndim else 0.5 * err


def _adamw(w, g, m, v):
    m = ADAM_B1 * m + (1.0 - ADAM_B1) * g
    v = ADAM_B2 * v + (1.0 - ADAM_B2) * _jnp.square(g)
    m_hat = m / (1.0 - ADAM_B1 ** ADAM_STEP)
    v_hat = v / (1.0 - ADAM_B2 ** ADAM_STEP)
    delta = -ADAM_LR * (m_hat / (_jnp.sqrt(v_hat) + ADAM_EPS) + ADAM_WD * w)
    return delta, m, v


def reference(x, pre_mix_norm, w_in, fox_f_bias, fox_out_norm, gdn_conv_w, gdn_a_log, gdn_dt_bias, gdn_out_norm, w_out, post_mix_norm, pre_mlp_norm, w_up, w_down, post_mlp_norm, loss_target, m_pre_mix_norm, m_w_in, m_fox_f_bias, m_fox_out_norm, m_gdn_conv_w, m_gdn_a_log, m_gdn_dt_bias, m_gdn_out_norm, m_w_out, m_post_mix_norm, m_pre_mlp_norm, m_w_up, m_w_down, m_post_mlp_norm, v_pre_mix_norm, v_w_in, v_fox_f_bias, v_fox_out_norm, v_gdn_conv_w, v_gdn_a_log, v_gdn_dt_bias, v_gdn_out_norm, v_w_out, v_post_mix_norm, v_pre_mlp_norm, v_w_up, v_w_down, v_post_mlp_norm):
    given = dict(x=x, pre_mix_norm=pre_mix_norm, w_in=w_in, fox_f_bias=fox_f_bias, fox_out_norm=fox_out_norm, gdn_conv_w=gdn_conv_w, gdn_a_log=gdn_a_log, gdn_dt_bias=gdn_dt_bias, gdn_out_norm=gdn_out_norm, w_out=w_out, post_mix_norm=post_mix_norm, pre_mlp_norm=pre_mlp_norm, w_up=w_up, w_down=w_down, post_mlp_norm=post_mlp_norm, loss_target=loss_target, m_pre_mix_norm=m_pre_mix_norm, m_w_in=m_w_in, m_fox_f_bias=m_fox_f_bias, m_fox_out_norm=m_fox_out_norm, m_gdn_conv_w=m_gdn_conv_w, m_gdn_a_log=m_gdn_a_log, m_gdn_dt_bias=m_gdn_dt_bias, m_gdn_out_norm=m_gdn_out_norm, m_w_out=m_w_out, m_post_mix_norm=m_post_mix_norm, m_pre_mlp_norm=m_pre_mlp_norm, m_w_up=m_w_up, m_w_down=m_w_down, m_post_mlp_norm=m_post_mlp_norm, v_pre_mix_norm=v_pre_mix_norm, v_w_in=v_w_in, v_fox_f_bias=v_fox_f_bias, v_fox_out_norm=v_fox_out_norm, v_gdn_conv_w=v_gdn_conv_w, v_gdn_a_log=v_gdn_a_log, v_gdn_dt_bias=v_gdn_dt_bias, v_gdn_out_norm=v_gdn_out_norm, v_w_out=v_w_out, v_post_mix_norm=v_post_mix_norm, v_pre_mlp_norm=v_pre_mlp_norm, v_w_up=v_w_up, v_w_down=v_w_down, v_post_mlp_norm=v_post_mlp_norm)
    weights = {n: given[n] for n in TWIN_WEIGHTS}
    shared = {n: given[n] for n in SHARED_INPUTS}
    per_example = {n: given[n] for n in ['x']}
    grad_fn = _jax.value_and_grad(_loss, argnums=(0, 1))

    def one_microbatch(ex, loss_target):
        ex = dict(ex)
        diff = ex.pop(TWIN_DIFF_INPUT)
        return grad_fn(weights, diff, {**shared, **ex}, loss_target)

    if N_MICROBATCH == 1:
        loss, (grad_w, grad_x) = one_microbatch(per_example, given["loss_target"])
    else:
        def body(carry, xs):
            loss_sum, grad_sum = carry
            l_k, (gw_k, gx_k) = one_microbatch(xs[0], xs[1])
            with _jax.named_scope("update"):
                return (loss_sum + l_k, _jax.tree.map(_jnp.add, grad_sum, gw_k)), gx_k

        init = (_jnp.zeros((), _jnp.float32), _jax.tree.map(_jnp.zeros_like, weights))
        (loss, grad_w), grad_x = _jax.lax.scan(body, init, (per_example, given["loss_target"]))
    with _jax.named_scope("update"):
        delta_w, new_m, new_v = {}, {}, {}
        for n in TWIN_WEIGHTS:
            delta_w[n], new_m[n], new_v[n] = _adamw(weights[n], grad_w[n], given["m_" + n], given["v_" + n])
    return (loss, grad_x, *[grad_w[n] for n in TWIN_WEIGHTS], *[delta_w[n] for n in TWIN_WEIGHTS],
            *[new_m[n] for n in TWIN_WEIGHTS], *[new_v[n] for n in TWIN_WEIGHTS])
```

```python
import functools

import jax
import jax.numpy as jnp
from jax import lax
from jax.experimental import pallas as pl
from jax.experimental.pallas import tpu as pltpu

F32 = jnp.float32
BF = jnp.bfloat16

D_MODEL = 1024
N_FOX_HEADS, FOX_HEAD_DIM = 8, 64
N_GDN_HEADS, GDN_HEAD_DIM = 4, 128
D_FOX = N_FOX_HEADS * FOX_HEAD_DIM
D_GDN = N_GDN_HEADS * GDN_HEAD_DIM
CHUNK = 64
CONV_K = 4
D_FF = 4 * D_MODEL
EPS = 1e-6
D_PROJ = 3600
N_DEV = 8

PROJ_W = 3712
COL_FOX, COL_GDN, COL_GZ, COL_SMALL = 0, 1536, 3072, 3584
LANES = 128
SM_FF, SM_GB, SM_GA = 0, 8, 12

ADAM_LR, ADAM_B1, ADAM_B2, ADAM_EPS, ADAM_WD, ADAM_STEP = 0.001, 0.9, 0.999, 1e-08, 0.01, 10

TOKEN_BLOCK = 256
FOX_SCALE = FOX_HEAD_DIM ** -0.5
GDN_QSCALE = GDN_HEAD_DIM ** -0.5
NEG_BIG = -1e30
VMEM_LIMIT = 56 * 1024 * 1024

VMEM_SPEC = pl.BlockSpec(memory_space=pltpu.VMEM)
HIGHEST = lax.Precision.HIGHEST


def _sds(shape, dtype=F32):
    return jax.ShapeDtypeStruct(shape, dtype)


def _params(*sem):
    return pltpu.CompilerParams(dimension_semantics=sem if sem else None, vmem_limit_bytes=VMEM_LIMIT)


def _mm(a, b):
    return jnp.dot(a.astype(BF), b.astype(BF), preferred_element_type=F32)


def _mm_nt(a, b):
    return lax.dot_general(a.astype(BF), b.astype(BF), (((1,), (1,)), ((), ())), preferred_element_type=F32)


def _mm_tn(a, b):
    return lax.dot_general(a.astype(BF), b.astype(BF), (((0,), (0,)), ((), ())), preferred_element_type=F32)


def _mm_exact(a, b):
    return jnp.dot(a, b, precision=HIGHEST, preferred_element_type=F32)


def _mm_tn_exact(a, b):
    return lax.dot_general(a, b, (((0,), (0,)), ((), ())), precision=HIGHEST, preferred_element_type=F32)


def _sigmoid(x):
    return 1.0 / (1.0 + jnp.exp(-x))


def _softplus(x):
    return jnp.maximum(x, 0.0) + jnp.log1p(jnp.exp(-jnp.abs(x)))


def _iota(shape, dim):
    return lax.broadcasted_iota(jnp.int32, shape, dim)


def _shift_down(x, s, row):
    return jnp.where(row >= s, pltpu.roll(x, s, 0), 0.0)


def _shift_up(x, s, row):
    n = x.shape[0]
    return jnp.where(row < n - s, pltpu.roll(x, n - s, 0), 0.0)


def _norm_proj(x, nw, w_al):
    t = x.shape[0]

    def body(x_ref, nw_ref, w_ref, proj_ref, h_ref):
        xv = x_ref[...]
        r = lax.rsqrt(jnp.mean(xv * xv, axis=-1, keepdims=True) + EPS)
        h = (xv * r * nw_ref[...]).astype(BF)
        h_ref[...] = h
        proj_ref[...] = jnp.dot(h, w_ref[...], preferred_element_type=F32)

    tm = TOKEN_BLOCK
    return pl.pallas_call(
        body, name="norm_proj", grid=(t // tm,),
        in_specs=[pl.BlockSpec((tm, D_MODEL), lambda i: (i, 0)), pl.BlockSpec((1, D_MODEL), lambda i: (0, 0)),
                  pl.BlockSpec((D_MODEL, PROJ_W), lambda i: (0, 0))],
        out_specs=[pl.BlockSpec((tm, PROJ_W), lambda i: (i, 0)), pl.BlockSpec((tm, D_MODEL), lambda i: (i, 0))],
        out_shape=[_sds((t, PROJ_W)), _sds((t, D_MODEL), BF)],
        compiler_params=_params("parallel"),
    )(x, nw, w_al)


def _expand_matrix(first_row, group):
    row = _iota((LANES, 512), 0)
    col = _iota((LANES, 512), 1)
    return (col // group + first_row == row).astype(F32)


def _small_prep(proj, fb, al, dtb):
    t = proj.shape[0]

    def body(sm_ref, fb_ref, al_ref, dtb_ref, ce_ref, cumt_ref, be_ref, ge_ref):
        s = sm_ref[...]
        z = s + fb_ref[...]
        cum = jnp.minimum(z, 0.0) - jnp.log1p(jnp.exp(-jnp.abs(z)))
        row = _iota((t, LANES), 0)
        step = 1
        while step < t:
            cum = cum + _shift_down(cum, step, row)
            step *= 2
        cumt_ref[...] = cum.T
        ce_ref[...] = _mm_exact(cum, _expand_matrix(SM_FF, FOX_HEAD_DIM))
        be_ref[...] = _mm_exact(_sigmoid(s), _expand_matrix(SM_GB, GDN_HEAD_DIM))
        g = -jnp.exp(al_ref[...]) * _softplus(s + dtb_ref[...])
        ge_ref[...] = _mm_exact(g, _expand_matrix(SM_GA, GDN_HEAD_DIM))

    vec = pl.BlockSpec((1, LANES), lambda i: (0, 0))
    return pl.pallas_call(
        body, name="small_prep", grid=(1,),
        in_specs=[pl.BlockSpec((t, LANES), lambda i: (0, COL_SMALL // LANES)), vec, vec, vec],
        out_specs=[pl.BlockSpec((t, 512), lambda i: (0, 0)), pl.BlockSpec((LANES, t), lambda i: (0, 0)),
                   pl.BlockSpec((t, 512), lambda i: (0, 0)), pl.BlockSpec((t, 512), lambda i: (0, 0))],
        out_shape=[_sds((t, 512)), _sds((LANES, t)), _sds((t, 512)), _sds((t, 512))],
        compiler_params=_params("arbitrary"),
    )(proj, fb, al, dtb)


def _fox_scores(qh, kb, ce_ref, cumt_ref, head, hh, i, tq):
    klen = (i + 1) * tq
    s = _mm_nt(qh, kb[:klen]) * FOX_SCALE
    cq = ce_ref[i * tq:(i + 1) * tq, FOX_HEAD_DIM * hh:FOX_HEAD_DIM * hh + 1]
    ck = cumt_ref[pl.ds(head, 1), 0:klen]
    s = s + cq - ck
    qi = _iota((tq, klen), 0) + i * tq
    ki = _iota((tq, klen), 1)
    return jnp.where(ki <= qi, s, NEG_BIG)


def _fox_fwd(proj, ce, cumt, fnw):
    t = proj.shape[0]
    tq = min(TOKEN_BLOCK, t // 2)
    nq = t // tq

    def body(q_ref, k_ref, v_ref, ce_ref, cumt_ref, fnw_ref, o_ref, lse_ref, fn_ref):
        j = pl.program_id(0)
        first = _iota((1, LANES), 1) < FOX_HEAD_DIM
        kb = k_ref[...].astype(BF)
        vb = v_ref[...].astype(BF)
        for i in range(nq):
            rows = slice(i * tq, (i + 1) * tq)
            klen = (i + 1) * tq
            q_i = q_ref[rows, :]
            o_acc = jnp.zeros((tq, LANES), F32)
            lse_acc = jnp.zeros((tq, LANES), F32)
            for hh in range(2):
                mh = first if hh == 0 else jnp.logical_not(first)
                qh = jnp.where(mh, q_i, 0.0).astype(BF)
                s = _fox_scores(qh, kb, ce_ref, cumt_ref, 2 * j + hh, hh, i, tq)
                m = jnp.max(s, axis=-1, keepdims=True)
                p = jnp.exp(s - m)
                l = jnp.sum(p, axis=-1, keepdims=True)
                o = jnp.dot(p.astype(BF), vb[:klen], preferred_element_type=F32) / l
                o_acc = jnp.where(mh, o, o_acc)
                lse_acc = jnp.where(mh, m + jnp.log(l), lse_acc)
            o_ref[rows, :] = o_acc
            lse_ref[rows, :] = lse_acc
            o2 = o_acc * o_acc
            s0 = jnp.sum(jnp.where(first, o2, 0.0), axis=-1, keepdims=True)
            s1 = jnp.sum(jnp.where(first, 0.0, o2), axis=-1, keepdims=True)
            r = lax.rsqrt(jnp.where(first, s0, s1) * (1.0 / FOX_HEAD_DIM) + EPS)
            fn_ref[rows, :] = (o_acc * r * fnw_ref[...]).astype(BF)

    blk = lambda off: pl.BlockSpec((t, LANES), lambda j: (0, off + j))
    return pl.pallas_call(
        body, name="fox_fwd", grid=(N_FOX_HEADS // 2,),
        in_specs=[blk(0), blk(4), blk(8), blk(0), pl.BlockSpec((LANES, t), lambda j: (0, 0)),
                  pl.BlockSpec((1, LANES), lambda j: (0, 0))],
        out_specs=[blk(0), blk(0), blk(0)],
        out_shape=[_sds((t, D_FOX)), _sds((t, D_FOX)), _sds((t, D_FOX), BF)],
        compiler_params=_params("parallel"),
    )(proj, proj, proj, ce, cumt, fnw)


def _fox_bwd(proj, ce, cumt, lse, o, do):
    t = proj.shape[0]
    tq = min(TOKEN_BLOCK, t // 2)
    nq = t // tq

    def body(q_ref, k_ref, v_ref, ce_ref, cumt_ref, lse_ref, o_ref, do_ref,
             dq_ref, dk_ref, dv_ref, dcq_ref, dckt_ref, dk_s, dv_s, dck_s):
        j = pl.program_id(0)
        first = _iota((1, LANES), 1) < FOX_HEAD_DIM
        kf = k_ref[...]
        kb = kf.astype(BF)
        vb = v_ref[...].astype(BF)
        dk_s[...] = jnp.zeros_like(dk_s)
        dv_s[...] = jnp.zeros_like(dv_s)
        dck_s[...] = jnp.zeros_like(dck_s)
        masks = [first, jnp.logical_not(first)]
        kmask = [jnp.where(mh, kf, 0.0).astype(BF) for mh in masks]
        for i in range(nq):
            rows = slice(i * tq, (i + 1) * tq)
            klen = (i + 1) * tq
            q_i = q_ref[rows, :]
            do_i = do_ref[rows, :]
            o_i = o_ref[rows, :]
            lse_i = lse_ref[rows, :]
            dq_acc = jnp.zeros((tq, LANES), F32)
            dcq_acc = jnp.zeros((tq, LANES), F32)
            for hh in range(2):
                mh = masks[hh]
                qh = jnp.where(mh, q_i, 0.0).astype(BF)
                doh = jnp.where(mh, do_i, 0.0)
                dohb = doh.astype(BF)
                delta = jnp.sum(doh * o_i, axis=-1, keepdims=True)
                s = _fox_scores(qh, kb, ce_ref, cumt_ref, 2 * j + hh, hh, i, tq)
                p = jnp.exp(s - lse_i[:, FOX_HEAD_DIM * hh:FOX_HEAD_DIM * hh + 1])
                dp = _mm_nt(dohb, vb[:klen])
                ds = p * (dp - delta)
                dsb = ds.astype(BF)
                dq_acc = dq_acc + jnp.dot(dsb, kmask[hh][:klen], preferred_element_type=F32) * FOX_SCALE
                dk_s[0:klen, :] += _mm_tn(dsb, qh) * FOX_SCALE
                dv_s[0:klen, :] += _mm_tn(p, dohb)
                dcq_acc = jnp.where(mh, jnp.sum(ds, axis=-1, keepdims=True), dcq_acc)
                dck_s[hh:hh + 1, 0:klen] += jnp.sum(ds, axis=0, keepdims=True)
            dq_ref[rows, :] = dq_acc
            dcq_ref[rows, :] = dcq_acc
        dk_ref[...] = dk_s[...]
        dv_ref[...] = dv_s[...]
        dckt_ref[...] = jnp.zeros_like(dckt_ref)
        dckt_ref[0:8, :] = dck_s[...]

    blk = lambda off: pl.BlockSpec((t, LANES), lambda j: (0, off + j))
    return pl.pallas_call(
        body, name="fox_bwd", grid=(N_FOX_HEADS // 2,),
        in_specs=[blk(0), blk(4), blk(8), blk(0), pl.BlockSpec((LANES, t), lambda j: (0, 0)), blk(0), blk(0), blk(0)],
        out_specs=[blk(0), blk(0), blk(0), blk(0), pl.BlockSpec((32, t), lambda j: (j, 0))],
        out_shape=[_sds((t, D_FOX))] * 4 + [_sds((LANES, t))],
        scratch_shapes=[pltpu.VMEM((t, LANES), F32), pltpu.VMEM((t, LANES), F32), pltpu.VMEM((8, t), F32)],
        compiler_params=_params("parallel"),
    )(proj, proj, proj, ce, cumt, lse, o, do)


def _conv(x, w, row):
    return (w[3:4, :] * x + w[2:3, :] * _shift_down(x, 1, row) + w[1:2, :] * _shift_down(x, 2, row)
            + w[0:1, :] * _shift_down(x, 3, row))


def _chunk_decay(gc_c):
    gi = gc_c[:, 0:CHUNK]
    gj = gc_c.T[0:CHUNK, :]
    ri = _iota((CHUNK, CHUNK), 0)
    cj = _iota((CHUNK, CHUNK), 1)
    return jnp.where(ri >= cj, jnp.exp(jnp.minimum(gi - gj, 0.0)), 0.0), ri > cj


def _gdn_specs(t):
    col = lambda off: pl.BlockSpec((t, LANES), lambda h: (0, off + h))
    cw = lambda off: pl.BlockSpec((CONV_K, LANES), lambda h: (0, off + h))
    mat = pl.BlockSpec((1, t // CHUNK, CHUNK, CHUNK), lambda h: (h, 0, 0, 0))
    return col, cw, mat


def _gdn_prep(proj, convw, be, ge):
    t = proj.shape[0]
    nch = t // CHUNK

    def body(xq_ref, xk_ref, xv_ref, wq_ref, wk_ref, wv_ref, be_ref, ge_ref,
             qn_ref, kn_ref, cv_ref, gc_ref, m_ref, a_ref):
        row = _iota((t, LANES), 0)

        def act(x_ref, w_ref):
            y = _conv(x_ref[...], w_ref[...], row)
            return y * _sigmoid(y)

        cq = act(xq_ref, wq_ref)
        ck = act(xk_ref, wk_ref)
        cv_ref[...] = act(xv_ref, wv_ref)
        qn_ref[...] = cq * lax.rsqrt(jnp.sum(cq * cq, axis=-1, keepdims=True) + EPS) * GDN_QSCALE
        kn_ref[...] = ck * lax.rsqrt(jnp.sum(ck * ck, axis=-1, keepdims=True) + EPS)
        gc = ge_ref[...]
        pos = row % CHUNK
        step = 1
        while step < CHUNK:
            gc = gc + jnp.where(pos >= step, pltpu.roll(gc, step, 0), 0.0)
            step *= 2
        gc_ref[...] = gc

        def chunk(n, carry):
            sl = pl.ds(pl.multiple_of(n * CHUNK, CHUNK), CHUNK)
            k_c = kn_ref[sl, :]
            decay, strict = _chunk_decay(gc_ref[sl, :])
            m_ref[0, n] = jnp.where(strict, _mm_nt(k_c * be_ref[sl, :], k_c) * decay, 0.0)
            a_ref[0, n] = _mm_nt(qn_ref[sl, :], k_c) * decay
            return carry

        lax.fori_loop(0, nch, chunk, 0)

    col, cw, mat = _gdn_specs(t)
    return pl.pallas_call(
        body, name="gdn_prep", grid=(N_GDN_HEADS,),
        in_specs=[col(12), col(16), col(20), cw(0), cw(4), cw(8), col(0), col(0)],
        out_specs=[col(0), col(0), col(0), col(0), mat, mat],
        out_shape=[_sds((t, D_GDN))] * 4 + [_sds((N_GDN_HEADS, nch, CHUNK, CHUNK))] * 2,
        compiler_params=_params("parallel"),
    )(proj, proj, proj, convw, convw, convw, be, ge)


def _tri_inverse(m2):
    n_prob = m2.shape[0]
    assert n_prob == LANES
    nb = CHUNK * CHUNK // LANES

    def body(m_ref, t_ref, ms, ts):
        for b in range(nb):
            ms[b * LANES:(b + 1) * LANES, :] = m_ref[:, b * LANES:(b + 1) * LANES].T
        cidx = _iota((CHUNK, LANES), 0)

        def outer(i, carry):
            def inner(jj, acc):
                mrow = ms[pl.ds(i * CHUNK + jj, 1), :]
                return acc - mrow * ts[pl.ds(pl.multiple_of(jj * CHUNK, CHUNK), CHUNK), :]

            acc = lax.fori_loop(0, i, inner, jnp.where(cidx == i, 1.0, 0.0).astype(F32))
            ts[pl.ds(pl.multiple_of(i * CHUNK, CHUNK), CHUNK), :] = acc
            return carry

        lax.fori_loop(0, CHUNK, outer, 0)
        for b in range(nb):
            t_ref[:, b * LANES:(b + 1) * LANES] = ts[b * LANES:(b + 1) * LANES, :].T

    return pl.pallas_call(
        body, name="tri_inverse", in_specs=[VMEM_SPEC], out_specs=VMEM_SPEC,
        out_shape=_sds((LANES, CHUNK * CHUNK)),
        scratch_shapes=[pltpu.VMEM((CHUNK * CHUNK, LANES), F32), pltpu.VMEM((CHUNK * CHUNK, LANES), F32)],
        compiler_params=_params(),
    )(m2)


def _gdn_chunk_terms(q, k, v, b, gcc):
    eg = jnp.exp(gcc)
    last = gcc[CHUNK - 1:CHUNK, :]
    egl = jnp.exp(last - gcc)
    gl = jnp.exp(last)
    kb = k * b
    return eg, egl, gl, kb, v * b, kb * eg, q * eg, k * egl


def _gdn_scan(qn, kn, cv, be, gc, tinv, amat):
    t = qn.shape[0]
    nch = t // CHUNK

    def body(q_ref, k_ref, v_ref, b_ref, gc_ref, t_ref, a_ref, o_ref, s_scr):
        s_scr[...] = jnp.zeros_like(s_scr)

        def chunk(n, carry):
            sl = pl.ds(pl.multiple_of(n * CHUNK, CHUNK), CHUNK)
            eg, egl, gl, kb, vb, kbg, qd, kd = _gdn_chunk_terms(q_ref[sl, :], k_ref[sl, :], v_ref[sl, :], b_ref[sl, :],
                                                                gc_ref[sl, :])
            tn = t_ref[0, n]
            s = s_scr[...]
            vn = _mm(tn, vb) - _mm(_mm(tn, kbg), s)
            o_ref[sl, :] = _mm(qd, s) + _mm(a_ref[0, n], vn)
            s_scr[...] = s * gl + _mm_tn(kd, vn)
            return carry

        lax.fori_loop(0, nch, chunk, 0)

    col, _, mat = _gdn_specs(t)
    return pl.pallas_call(
        body, name="gdn_scan", grid=(N_GDN_HEADS,),
        in_specs=[col(0)] * 5 + [mat, mat], out_specs=col(0), out_shape=_sds((t, D_GDN)),
        scratch_shapes=[pltpu.VMEM((GDN_HEAD_DIM, GDN_HEAD_DIM), F32)],
        compiler_params=_params("parallel"),
    )(qn, kn, cv, be, gc, tinv, amat)


def _gdn_bwd(qn, kn, cv, be, gc, tinv, amat, do):
    t = qn.shape[0]
    nch = t // CHUNK

    def body(q_ref, k_ref, v_ref, b_ref, gc_ref, t_ref, a_ref, do_ref,
             dq_ref, dk_ref, dv_ref, db_ref, dg_ref, s_all, vn_all, ds_scr):
        s_all[0] = jnp.zeros((GDN_HEAD_DIM, GDN_HEAD_DIM), F32)

        def fwd(n, carry):
            sl = pl.ds(pl.multiple_of(n * CHUNK, CHUNK), CHUNK)
            eg, egl, gl, kb, vb, kbg, qd, kd = _gdn_chunk_terms(q_ref[sl, :], k_ref[sl, :], v_ref[sl, :], b_ref[sl, :],
                                                                gc_ref[sl, :])
            tn = t_ref[0, n]
            s = s_all[n]
            vn = _mm(tn, vb) - _mm(_mm(tn, kbg), s)
            vn_all[sl, :] = vn

            @pl.when(n + 1 < nch)
            def _():
                s_all[n + 1] = s * gl + _mm_tn(kd, vn)

            return carry

        lax.fori_loop(0, nch, fwd, 0)
        ds_scr[...] = jnp.zeros_like(ds_scr)
        ones = jnp.ones((CHUNK, LANES), F32)
        ri = _iota((CHUNK, CHUNK), 0)
        cj = _iota((CHUNK, CHUNK), 1)
        upper = (cj >= ri).astype(F32)
        lastrow = _iota((CHUNK, LANES), 0) == CHUNK - 1

        def bwd(step, carry):
            n = nch - 1 - step
            sl = pl.ds(pl.multiple_of(n * CHUNK, CHUNK), CHUNK)
            q, k, v, b, gcc = q_ref[sl, :], k_ref[sl, :], v_ref[sl, :], b_ref[sl, :], gc_ref[sl, :]
            eg, egl, gl, kb, vb, kbg, qd, kd = _gdn_chunk_terms(q, k, v, b, gcc)
            do_c = do_ref[sl, :]
            tn = t_ref[0, n]
            an = a_ref[0, n]
            s = s_all[n]
            vn = vn_all[sl, :]
            dsp = ds_scr[...]
            w = _mm(tn, kbg)
            dvn = _mm_tn(an, do_c) + _mm(kd, dsp)
            da = _mm_nt(do_c, vn)
            dqd = _mm_nt(do_c, s)
            dkd = _mm_nt(vn, dsp)
            dgl = jnp.sum(jnp.sum(dsp * s, axis=-1, keepdims=True), axis=0, keepdims=True)
            ds_scr[...] = _mm_tn(qd, do_c) + gl * dsp - _mm_tn(w, dvn)
            dw = -_mm_nt(dvn, s)
            dt = _mm_nt(dvn, vb) + _mm_nt(dw, kbg)
            dvb = _mm_tn(tn, dvn)
            dkbg = _mm_tn(tn, dw)
            decay, strict = _chunk_decay(gcc)
            kk = _mm_nt(kb, k)
            qk = _mm_nt(q, k)
            dm = jnp.where(strict, -_mm_nt(_mm_tn(tn, dt), tn), 0.0)
            dkk = dm * decay
            dqk = da * decay
            gmat = dkk * kk + dqk * qk
            dq_ref[sl, :] = dqd * eg + _mm(dqk, k)
            dkb = _mm(dkk, k) + dkbg * eg
            dk_ref[sl, :] = dkd * egl + _mm_tn(dqk, q) + _mm_tn(dkk, kb) + dkb * b
            db = jnp.sum(dkb * k, axis=-1, keepdims=True) + jnp.sum(dvb * v, axis=-1, keepdims=True)
            db_ref[sl, :] = jnp.broadcast_to(db, (CHUNK, LANES))
            dv_ref[sl, :] = dvb * b
            dkd_kd = jnp.sum(dkd * kd, axis=-1, keepdims=True)
            dgc = (jnp.sum(gmat, axis=-1, keepdims=True) - _mm_tn_exact(gmat, ones)
                   + jnp.sum(dqd * qd, axis=-1, keepdims=True) + jnp.sum(dkbg * kbg, axis=-1, keepdims=True) - dkd_kd)
            extra = jnp.sum(dkd_kd, axis=0, keepdims=True) + dgl * gl
            dgc = dgc + jnp.where(lastrow, extra, 0.0)
            dg_ref[sl, :] = _mm_exact(upper, dgc)
            return carry

        lax.fori_loop(0, nch, bwd, 0)

    col, _, mat = _gdn_specs(t)
    return pl.pallas_call(
        body, name="gdn_bwd", grid=(N_GDN_HEADS,),
        in_specs=[col(0)] * 5 + [mat, mat, col(0)], out_specs=[col(0)] * 5, out_shape=[_sds((t, D_GDN))] * 5,
        scratch_shapes=[pltpu.VMEM((nch, GDN_HEAD_DIM, GDN_HEAD_DIM), F32), pltpu.VMEM((t, LANES), F32),
                        pltpu.VMEM((GDN_HEAD_DIM, GDN_HEAD_DIM), F32)],
        compiler_params=_params("parallel"),
    )(qn, kn, cv, be, gc, tinv, amat, do)


def _gdn_bwd_conv(proj, convw, dqn, dkn, dcv):
    t = proj.shape[0]

    def body(xq_ref, xk_ref, xv_ref, wq_ref, wk_ref, wv_ref, dq_ref, dk_ref, dv_ref,
             dxq_ref, dxk_ref, dxv_ref, dwq_ref, dwk_ref, dwv_ref):
        row = _iota((t, LANES), 0)

        def one(x_ref, w_ref, d_ref, dx_ref, dw_ref, scale):
            x = x_ref[...]
            w = w_ref[...]
            y = _conv(x, w, row)
            sg = _sigmoid(y)
            dc = d_ref[...]
            if scale is not None:
                c = y * sg
                r = lax.rsqrt(jnp.sum(c * c, axis=-1, keepdims=True) + EPS)
                ch = c * r
                dc = scale * r * (dc - ch * jnp.sum(dc * ch, axis=-1, keepdims=True))
            dy = dc * sg * (1.0 + y * (1.0 - sg))
            dx_ref[...] = (w[3:4, :] * dy + w[2:3, :] * _shift_up(dy, 1, row) + w[1:2, :] * _shift_up(dy, 2, row)
                           + w[0:1, :] * _shift_up(dy, 3, row))
            for jj in range(CONV_K):
                xs = x if jj == CONV_K - 1 else _shift_down(x, CONV_K - 1 - jj, row)
                dw_ref[jj:jj + 1, :] = jnp.sum(dy * xs, axis=0, keepdims=True)

        one(xq_ref, wq_ref, dq_ref, dxq_ref, dwq_ref, GDN_QSCALE)
        one(xk_ref, wk_ref, dk_ref, dxk_ref, dwk_ref, 1.0)
        one(xv_ref, wv_ref, dv_ref, dxv_ref, dwv_ref, None)

    col, cw, _ = _gdn_specs(t)
    return pl.pallas_call(
        body, name="gdn_bwd_conv", grid=(N_GDN_HEADS,),
        in_specs=[col(12), col(16), col(20), cw(0), cw(4), cw(8), col(0), col(0), col(0)],
        out_specs=[col(0), col(0), col(0), cw(0), cw(0), cw(0)],
        out_shape=[_sds((t, D_GDN))] * 3 + [_sds((CONV_K, D_GDN))] * 3,
        compiler_params=_params("parallel"),
    )(proj, proj, proj, convw, convw, convw, dqn, dkn, dcv)


def _mix_out(fox_n, gdn_o, proj, gnw, w_out, x, pmw, plw):
    t = x.shape[0]
    tm = TOKEN_BLOCK

    def body(fn_ref, go_ref, gz_ref, gnw_ref, w_ref, x_ref, pmw_ref, plw_ref, x1_ref, h2_ref, mixed_ref, omix_ref):
        omix_ref[:, 0:D_FOX] = fn_ref[...]
        for hd in range(N_GDN_HEADS):
            cs = slice(hd * LANES, (hd + 1) * LANES)
            go = go_ref[:, cs]
            r = lax.rsqrt(jnp.mean(go * go, axis=-1, keepdims=True) + EPS)
            gz = gz_ref[:, cs]
            omix_ref[:, D_FOX + hd * LANES:D_FOX + (hd + 1) * LANES] = (
                go * r * gnw_ref[...] * (gz * _sigmoid(gz))).astype(BF)
        mixed = jnp.dot(omix_ref[...], w_ref[...], preferred_element_type=F32)
        mixed_ref[...] = mixed
        r2 = lax.rsqrt(jnp.mean(mixed * mixed, axis=-1, keepdims=True) + EPS)
        x1 = x_ref[...] + mixed * r2 * pmw_ref[...]
        x1_ref[...] = x1
        r3 = lax.rsqrt(jnp.mean(x1 * x1, axis=-1, keepdims=True) + EPS)
        h2_ref[...] = (x1 * r3 * plw_ref[...]).astype(BF)

    tok = lambda w: pl.BlockSpec((tm, w), lambda i: (i, 0))
    vec = lambda w: pl.BlockSpec((1, w), lambda i: (0, 0))
    return pl.pallas_call(
        body, name="mix_out", grid=(t // tm,),
        in_specs=[tok(D_FOX), tok(D_GDN), pl.BlockSpec((tm, D_GDN), lambda i: (i, COL_GZ // D_GDN)), vec(LANES),
                  pl.BlockSpec((D_MODEL, D_MODEL), lambda i: (0, 0)), tok(D_MODEL), vec(D_MODEL), vec(D_MODEL)],
        out_specs=[tok(D_MODEL)] * 4,
        out_shape=[_sds((t, D_MODEL)), _sds((t, D_MODEL), BF), _sds((t, D_MODEL)), _sds((t, D_MODEL), BF)],
        compiler_params=_params("parallel"),
    )(fox_n, gdn_o, proj, gnw, w_out, x, pmw, plw)


def _out_bwd(dmixed, w_out, o_fox, gdn_o, proj, fnw, gnw):
    t = dmixed.shape[0]
    tm = TOKEN_BLOCK

    def body(dm_ref, w_ref, of_ref, go_ref, gz_ref, fnw_ref, gnw_ref, dof_ref, dgo_ref, dgz_ref, dfw_ref, dgw_ref):
        i = pl.program_id(0)

        @pl.when(i == 0)
        def _():
            dfw_ref[...] = jnp.zeros_like(dfw_ref)
            dgw_ref[...] = jnp.zeros_like(dgw_ref)

        domix = _mm_nt(dm_ref[...], w_ref[...])
        first = _iota((1, LANES), 1) < FOX_HEAD_DIM
        dfw = jnp.zeros((1, LANES), F32)
        dgw = jnp.zeros((1, LANES), F32)
        for pr in range(N_FOX_HEADS // 2):
            cs = slice(pr * LANES, (pr + 1) * LANES)
            o = of_ref[:, cs]
            dfn = domix[:, cs]
            o2 = o * o
            s0 = jnp.sum(jnp.where(first, o2, 0.0), axis=-1, keepdims=True)
            s1 = jnp.sum(jnp.where(first, 0.0, o2), axis=-1, keepdims=True)
            r = lax.rsqrt(jnp.where(first, s0, s1) * (1.0 / FOX_HEAD_DIM) + EPS)
            oh = o * r
            dfw = dfw + jnp.sum(dfn * oh, axis=0, keepdims=True)
            doh = dfn * fnw_ref[...]
            pr_ = doh * oh
            m0 = jnp.sum(jnp.where(first, pr_, 0.0), axis=-1, keepdims=True)
            m1 = jnp.sum(jnp.where(first, 0.0, pr_), axis=-1, keepdims=True)
            dof_ref[:, cs] = r * (doh - oh * jnp.where(first, m0, m1) * (1.0 / FOX_HEAD_DIM))
        for hd in range(N_GDN_HEADS):
            cs = slice(hd * LANES, (hd + 1) * LANES)
            go = go_ref[:, cs]
            gz = gz_ref[:, cs]
            dgated = domix[:, D_FOX + hd * LANES:D_FOX + (hd + 1) * LANES]
            r = lax.rsqrt(jnp.mean(go * go, axis=-1, keepdims=True) + EPS)
            goh = go * r
            sg = _sigmoid(gz)
            sz = gz * sg
            gn = goh * gnw_ref[...]
            dgn = dgated * sz
            dgz_ref[:, cs] = dgated * gn * sg * (1.0 + gz * (1.0 - sg))
            dgw = dgw + jnp.sum(dgn * goh, axis=0, keepdims=True)
            dgh = dgn * gnw_ref[...]
            dgo_ref[:, cs] = r * (dgh - goh * jnp.mean(dgh * goh, axis=-1, keepdims=True))
        dfw_ref[...] += dfw + pltpu.roll(dfw, FOX_HEAD_DIM, 1)
        dgw_ref[...] += dgw

    tok = lambda w: pl.BlockSpec((tm, w), lambda i: (i, 0))
    vec = lambda w: pl.BlockSpec((1, w), lambda i: (0, 0))
    return pl.pallas_call(
        body, name="out_bwd", grid=(t // tm,),
        in_specs=[tok(D_MODEL), pl.BlockSpec((D_MODEL, D_MODEL), lambda i: (0, 0)), tok(D_FOX), tok(D_GDN),
                  pl.BlockSpec((tm, D_GDN), lambda i: (i, COL_GZ // D_GDN)), vec(LANES), vec(LANES)],
        out_specs=[tok(D_FOX), tok(D_GDN), tok(D_GDN), vec(LANES), vec(LANES)],
        out_shape=[_sds((t, D_FOX)), _sds((t, D_GDN)), _sds((t, D_GDN)), _sds((1, LANES)), _sds((1, LANES))],
        compiler_params=_params("arbitrary"),
    )(dmixed, w_out, o_fox, gdn_o, proj, fnw, gnw)


def _mlp_up(h2, w_up):
    t = h2.shape[0]
    tm = TOKEN_BLOCK

    def body(h_ref, w_ref, up_ref):
        up_ref[...] = jnp.dot(h_ref[...], w_ref[...], preferred_element_type=F32)

    return pl.pallas_call(
        body, name="mlp_up", grid=(t // tm,),
        in_specs=[pl.BlockSpec((tm, D_MODEL), lambda i: (i, 0)), pl.BlockSpec((D_MODEL, D_FF), lambda i: (0, 0))],
        out_specs=pl.BlockSpec((tm, D_FF), lambda i: (i, 0)), out_shape=_sds((t, D_FF)),
        compiler_params=_params("parallel"),
    )(h2, w_up)


def _mlp_down_loss(up, w_down, x1, pw, target):
    t = up.shape[0]
    tm = TOKEN_BLOCK

    def body(up_ref, w_ref, x1_ref, pw_ref, tg_ref, dy_ref, dx2_ref, loss_ref, dpw_ref):
        i = pl.program_id(0)

        @pl.when(i == 0)
        def _():
            loss_ref[...] = jnp.zeros_like(loss_ref)
            dpw_ref[...] = jnp.zeros_like(dpw_ref)

        u = jnp.maximum(up_ref[...], 0.0)
        y = jnp.dot((u * u).astype(BF), w_ref[...], preferred_element_type=F32)
        r = lax.rsqrt(jnp.mean(y * y, axis=-1, keepdims=True) + EPS)
        yh = y * r
        pw = pw_ref[...]
        err = x1_ref[...] + yh * pw - tg_ref[...]
        part = jnp.sum(jnp.sum(err * err, axis=-1, keepdims=True), axis=0, keepdims=True) * (0.5 / D_MODEL)
        loss_ref[...] += jnp.broadcast_to(part, loss_ref.shape)
        dx2 = err * (1.0 / D_MODEL)
        dx2_ref[...] = dx2
        dpw_ref[...] += jnp.sum(dx2 * yh, axis=0, keepdims=True)
        dyh = dx2 * pw
        dy_ref[...] = (r * (dyh - yh * jnp.mean(dyh * yh, axis=-1, keepdims=True))).astype(BF)

    tok = lambda w: pl.BlockSpec((tm, w), lambda i: (i, 0))
    vec = lambda w: pl.BlockSpec((1, w), lambda i: (0, 0))
    return pl.pallas_call(
        body, name="mlp_down_loss", grid=(t // tm,),
        in_specs=[tok(D_FF), pl.BlockSpec((D_FF, D_MODEL), lambda i: (0, 0)), tok(D_MODEL), vec(D_MODEL), tok(D_MODEL)],
        out_specs=[tok(D_MODEL), tok(D_MODEL), vec(LANES), vec(D_MODEL)],
        out_shape=[_sds((t, D_MODEL), BF), _sds((t, D_MODEL)), _sds((1, LANES)), _sds((1, D_MODEL))],
        compiler_params=_params("arbitrary"),
    )(up, w_down, x1, pw, target)


def _mlp_bwd_act(dy, w_down, up):
    t = dy.shape[0]
    tm = TOKEN_BLOCK

    def body(dy_ref, w_ref, up_ref, dup_ref):
        da = lax.dot_general(dy_ref[...], w_ref[...], (((1,), (1,)), ((), ())), preferred_element_type=F32)
        dup_ref[...] = (da * (2.0 * jnp.maximum(up_ref[...], 0.0))).astype(BF)

    return pl.pallas_call(
        body, name="mlp_bwd_act", grid=(t // tm,),
        in_specs=[pl.BlockSpec((tm, D_MODEL), lambda i: (i, 0)), pl.BlockSpec((D_FF, D_MODEL), lambda i: (0, 0)),
                  pl.BlockSpec((tm, D_FF), lambda i: (i, 0))],
        out_specs=pl.BlockSpec((tm, D_FF), lambda i: (i, 0)), out_shape=_sds((t, D_FF), BF),
        compiler_params=_params("parallel"),
    )(dy, w_down, up)


def _mlp_bwd_in(dup, w_up, x1, plw, dx2, mixed, pmw):
    t = dup.shape[0]
    tm = TOKEN_BLOCK

    def body(dup_ref, w_ref, x1_ref, plw_ref, dx2_ref, mx_ref, pmw_ref, dx1_ref, dmixed_ref, dplw_ref, dpmw_ref):
        i = pl.program_id(0)

        @pl.when(i == 0)
        def _():
            dplw_ref[...] = jnp.zeros_like(dplw_ref)
            dpmw_ref[...] = jnp.zeros_like(dpmw_ref)

        dh = lax.dot_general(dup_ref[...], w_ref[...], (((1,), (1,)), ((), ())), preferred_element_type=F32)
        x1 = x1_ref[...]
        r = lax.rsqrt(jnp.mean(x1 * x1, axis=-1, keepdims=True) + EPS)
        xh = x1 * r
        dplw_ref[...] += jnp.sum(dh * xh, axis=0, keepdims=True)
        dxh = dh * plw_ref[...]
        dx1 = dx2_ref[...] + r * (dxh - xh * jnp.mean(dxh * xh, axis=-1, keepdims=True))
        dx1_ref[...] = dx1
        mx = mx_ref[...]
        r2 = lax.rsqrt(jnp.mean(mx * mx, axis=-1, keepdims=True) + EPS)
        mh = mx * r2
        dpmw_ref[...] += jnp.sum(dx1 * mh, axis=0, keepdims=True)
        dmh = dx1 * pmw_ref[...]
        dmixed_ref[...] = (r2 * (dmh - mh * jnp.mean(dmh * mh, axis=-1, keepdims=True))).astype(BF)

    tok = lambda w: pl.BlockSpec((tm, w), lambda i: (i, 0))
    vec = lambda w: pl.BlockSpec((1, w), lambda i: (0, 0))
    return pl.pallas_call(
        body, name="mlp_bwd_in", grid=(t // tm,),
        in_specs=[tok(D_FF), pl.BlockSpec((D_MODEL, D_FF), lambda i: (0, 0)), tok(D_MODEL), vec(D_MODEL), tok(D_MODEL),
                  tok(D_MODEL), vec(D_MODEL)],
        out_specs=[tok(D_MODEL), tok(D_MODEL), vec(D_MODEL), vec(D_MODEL)],
        out_shape=[_sds((t, D_MODEL)), _sds((t, D_MODEL), BF), _sds((1, D_MODEL)), _sds((1, D_MODEL))],
        compiler_params=_params("arbitrary"),
    )(dup, w_up, x1, plw, dx2, mixed, pmw)


def _wgrad(a, b, a_cols, b_cols, a_fn=None, pieces="a", name="wgrad"):
    t = a.shape[0]
    n_a, n_b = a.shape[1] // a_cols, b.shape[1] // b_cols

    def body(a_ref, b_ref, o_ref):
        av = a_ref[...]
        if a_fn is not None:
            av = a_fn(av)
        res = _mm_tn(av, b_ref[...]).astype(BF)
        if pieces is None:
            o_ref[...] = res
        else:
            o_ref[0] = res

    if pieces == "a":
        assert n_b == 1
        out_spec, out_shape = pl.BlockSpec((1, a_cols, b_cols), lambda i, j: (i, 0, 0)), (n_a, a_cols, b_cols)
    elif pieces == "b":
        assert n_a == 1
        out_spec, out_shape = pl.BlockSpec((1, a_cols, b_cols), lambda i, j: (j, 0, 0)), (n_b, a_cols, b_cols)
    else:
        assert n_a == 1
        out_spec, out_shape = pl.BlockSpec((a_cols, b_cols), lambda i, j: (0, j)), (a_cols, n_b * b_cols)
    return pl.pallas_call(
        body, name=name, grid=(n_a, n_b),
        in_specs=[pl.BlockSpec((t, a_cols), lambda i, j: (0, i)), pl.BlockSpec((t, b_cols), lambda i, j: (0, j))],
        out_specs=out_spec, out_shape=_sds(out_shape, BF),
        compiler_params=_params("parallel", "parallel"),
    )(a, b)


def _select_matrix(rows, fn):
    r = _iota((rows, LANES), 0)
    c = _iota((rows, LANES), 1)
    return (r == fn(c)).astype(F32)


def _small_bwd(proj, fb, al, dtb, dcq, dckt, dbe, dge):
    t = proj.shape[0]

    def body(sm_ref, fb_ref, al_ref, dtb_ref, dcq_ref, dckt_ref, dbe_ref, dge_ref, dsm_ref, dvec_ref):
        s = sm_ref[...]
        lane = _iota((1, LANES), 1)
        sel_f = _select_matrix(512, lambda c: jnp.where(c < 8, FOX_HEAD_DIM * c, -1))
        sel_k = _select_matrix(LANES, lambda c: jnp.where(c < 8, 32 * (c // 2) + c % 2, -1))
        dcum = _mm_exact(dcq_ref[...], sel_f) - _mm_exact(dckt_ref[...].T, sel_k)
        row = _iota((t, LANES), 0)
        step = 1
        while step < t:
            dcum = dcum + _shift_up(dcum, step, row)
            step *= 2
        dff = dcum * _sigmoid(-(s + fb_ref[...]))
        sel_b = _select_matrix(512, lambda c: jnp.where((c >= SM_GB) & (c < SM_GA), LANES * (c - SM_GB), -1))
        sel_g = _select_matrix(512, lambda c: jnp.where((c >= SM_GA) & (c < SM_GA + 4), LANES * (c - SM_GA), -1))
        beta = _sigmoid(s)
        dgb = _mm_exact(dbe_ref[...], sel_b) * beta * (1.0 - beta)
        dg = _mm_exact(dge_ref[...], sel_g)
        za = s + dtb_ref[...]
        nea = -jnp.exp(al_ref[...])
        dga = dg * nea * _sigmoid(za)
        is_f = lane < SM_GB
        is_b = (lane >= SM_GB) & (lane < SM_GA)
        is_a = (lane >= SM_GA) & (lane < SM_GA + 4)
        dsm_ref[...] = jnp.where(is_f, dff, jnp.where(is_b, dgb, jnp.where(is_a, dga, 0.0)))
        dvec_ref[...] = jnp.zeros_like(dvec_ref)
        dvec_ref[0:1, :] = jnp.sum(jnp.where(is_f, dff, 0.0), axis=0, keepdims=True)
        dvec_ref[1:2, :] = jnp.sum(jnp.where(is_a, dg * nea * _softplus(za), 0.0), axis=0, keepdims=True)
        dvec_ref[2:3, :] = jnp.sum(jnp.where(is_a, dga, 0.0), axis=0, keepdims=True)

    vec = pl.BlockSpec((1, LANES), lambda i: (0, 0))
    full = lambda r, c: pl.BlockSpec((r, c), lambda i: (0, 0))
    return pl.pallas_call(
        body, name="small_bwd", grid=(1,),
        in_specs=[pl.BlockSpec((t, LANES), lambda i: (0, COL_SMALL // LANES)), vec, vec, vec, full(t, 512),
                  full(LANES, t), full(t, 512), full(t, 512)],
        out_specs=[full(t, LANES), full(8, LANES)], out_shape=[_sds((t, LANES)), _sds((8, LANES))],
        compiler_params=_params("arbitrary"),
    )(proj, fb, al, dtb, dcq, dckt, dbe, dge)


def _in_bwd(dfox, dgdn, dgz, dsm, w_al, x, nw, dx1):
    t = x.shape[0]
    tm = TOKEN_BLOCK

    def body(*refs):
        parts, (w_ref, x_ref, nw_ref, dx1_ref, dp_ref, dx_ref, dnw_ref) = refs[:8], refs[8:]
        i = pl.program_id(0)

        @pl.when(i == 0)
        def _():
            dnw_ref[...] = jnp.zeros_like(dnw_ref)

        col = 0
        for part in parts:
            width = part.shape[1]
            dp_ref[:, col:col + width] = part[...].astype(BF)
            col += width
        dh = lax.dot_general(dp_ref[...], w_ref[...], (((1,), (1,)), ((), ())), preferred_element_type=F32)
        xv = x_ref[...]
        r = lax.rsqrt(jnp.mean(xv * xv, axis=-1, keepdims=True) + EPS)
        xh = xv * r
        dnw_ref[...] += jnp.sum(dh * xh, axis=0, keepdims=True)
        dxh = dh * nw_ref[...]
        dx_ref[...] = dx1_ref[...] + r * (dxh - xh * jnp.mean(dxh * xh, axis=-1, keepdims=True))

    tok = lambda w: pl.BlockSpec((tm, w), lambda i: (i, 0))
    vec = lambda w: pl.BlockSpec((1, w), lambda i: (0, 0))
    return pl.pallas_call(
        body, name="in_bwd", grid=(t // tm,),
        in_specs=[tok(D_FOX)] * 3 + [tok(D_GDN)] * 4 + [tok(LANES), pl.BlockSpec((D_MODEL, PROJ_W), lambda i: (0, 0)),
                                                        tok(D_MODEL), vec(D_MODEL), tok(D_MODEL)],
        out_specs=[tok(PROJ_W), tok(D_MODEL), vec(D_MODEL)],
        out_shape=[_sds((t, PROJ_W), BF), _sds((t, D_MODEL)), _sds((1, D_MODEL))],
        compiler_params=_params("arbitrary"),
    )(*dfox, *dgdn, dgz, dsm, w_al, x, nw, dx1)


def _row(v, width=None):
    v = v.reshape(1, -1).astype(F32)
    if width is not None and v.shape[1] < width:
        v = jnp.pad(v, ((0, 0), (0, width - v.shape[1])))
    return v


def _lane_vec(v, first):
    return jnp.zeros((1, LANES), F32).at[0, first:first + v.shape[0]].set(v.astype(F32))


def _local_step(x, target, w_al, w_out, w_up, w_down, convw, pre_mix_norm, fox_f_bias, fox_out_norm, gdn_a_log,
                gdn_dt_bias, gdn_out_norm, post_mix_norm, pre_mlp_norm, post_mlp_norm):
    t = x.shape[0]
    nch = t // CHUNK
    nw, pmw, plw, pw = _row(pre_mix_norm), _row(post_mix_norm), _row(pre_mlp_norm), _row(post_mlp_norm)
    fb, al, dtb = _lane_vec(fox_f_bias, SM_FF), _lane_vec(gdn_a_log, SM_GA), _lane_vec(gdn_dt_bias, SM_GA)
    fnw = _row(jnp.tile(fox_out_norm, 2))
    gnw = _row(gdn_out_norm)

    proj, h = _norm_proj(x, nw, w_al)
    ce, cumt, be, ge = _small_prep(proj, fb, al, dtb)
    o_fox, lse, fox_n = _fox_fwd(proj, ce, cumt, fnw)
    qn, kn, cv, gc, mmat, amat = _gdn_prep(proj, convw, be, ge)
    n_prob = N_GDN_HEADS * nch
    m2 = mmat.reshape(n_prob, CHUNK * CHUNK)
    if n_prob < LANES:
        m2 = jnp.pad(m2, ((0, LANES - n_prob), (0, 0)))
    tinv = _tri_inverse(m2)[:n_prob].reshape(N_GDN_HEADS, nch, CHUNK, CHUNK)
    gdn_o = _gdn_scan(qn, kn, cv, be, gc, tinv, amat)
    x1, h2, mixed, omix = _mix_out(fox_n, gdn_o, proj, gnw, w_out, x, pmw, plw)
    up = _mlp_up(h2, w_up)
    dy, dx2, loss, d_pw = _mlp_down_loss(up, w_down, x1, pw, target)

    dup = _mlp_bwd_act(dy, w_down, up)
    relu2 = lambda u: jnp.square(jnp.maximum(u, 0.0))
    g_down = _wgrad(up, dy, D_FF // N_DEV, D_MODEL, a_fn=relu2, pieces="a", name="wgrad_down")
    g_up = _wgrad(h2, dup, D_MODEL, D_FF // N_DEV, pieces="b", name="wgrad_up")
    dx1, dmixed, d_plw, d_pmw = _mlp_bwd_in(dup, w_up, x1, plw, dx2, mixed, pmw)
    g_out = _wgrad(omix, dmixed, D_MODEL // N_DEV, D_MODEL, pieces="a", name="wgrad_out")
    do_fox, dgo, dgz, d_fnw, d_gnw = _out_bwd(dmixed, w_out, o_fox, gdn_o, proj, fnw, gnw)
    dfq, dfk, dfv, dcq, dckt = _fox_bwd(proj, ce, cumt, lse, o_fox, do_fox)
    dqn, dkn, dcv, dbe, dge = _gdn_bwd(qn, kn, cv, be, gc, tinv, amat, dgo)
    dxq, dxk, dxv, dwq, dwk, dwv = _gdn_bwd_conv(proj, convw, dqn, dkn, dcv)
    dsm, dvec = _small_bwd(proj, fb, al, dtb, dcq, dckt, dbe, dge)
    dproj, grad_x, d_nw = _in_bwd((dfq, dfk, dfv), (dxq, dxk, dxv), dgz, dsm, w_al, x, nw, dx1)
    g_in = _wgrad(h, dproj, D_MODEL, LANES, pieces=None, name="wgrad_in")
    small = dict(pre_mix_norm=d_nw[0], fox_f_bias=dvec[0, SM_FF:SM_FF + N_FOX_HEADS],
                 fox_out_norm=d_fnw[0, :FOX_HEAD_DIM], gdn_conv_w=(dwq, dwk, dwv),
                 gdn_a_log=dvec[1, SM_GA:SM_GA + N_GDN_HEADS], gdn_dt_bias=dvec[2, SM_GA:SM_GA + N_GDN_HEADS],
                 gdn_out_norm=d_gnw[0], post_mix_norm=d_pmw[0], pre_mlp_norm=d_plw[0], post_mlp_norm=d_pw[0])
    return loss[0, 0], grad_x, g_in, g_out, g_up, g_down, small


MESH_IDS = pl.DeviceIdType.MESH
CHIP_FLIPS = ((0, 0), (1, 0), (0, 1), (1, 1))
ANY_SPEC = pl.BlockSpec(memory_space=pl.ANY)


def _place():
    return lax.axis_index("x"), lax.axis_index("y"), lax.axis_index("c")


def _all_gather(blocks):
    n = len(blocks)

    def body(*refs):
        ins, outs, (send_sems, recv_sems, local_sems) = refs[:n], refs[n:2 * n], refs[2 * n:]
        x, y, c = _place()
        sibling = (x, y, 1 - c)
        chips = [(x ^ fx, y ^ fy) for fx, fy in CHIP_FLIPS[1:]]

        def slot(out, px, py, pc):
            return out.at[4 * px + 2 * py + pc]

        def copy(a, k, block, to, src=None):
            return pltpu.make_async_remote_copy(
                src_ref=slot(outs[a], *block) if src is None else src, dst_ref=slot(outs[a], *block),
                send_sem=send_sems.at[a, k], recv_sem=recv_sems.at[a, k], device_id=to, device_id_type=MESH_IDS)

        pending = []
        for a in range(n):
            mine = pltpu.make_async_copy(ins[a], slot(outs[a], x, y, c), local_sems.at[a])
            mine.start()
            pending.append(mine)
        sends = []
        for a in range(n):
            first = [copy(a, 0, (x, y, c), sibling, src=ins[a])]
            first += [copy(a, 1 + j, (x, y, c), (*chip, c), src=ins[a]) for j, chip in enumerate(chips)]
            for cp in first:
                cp.start()
            sends += first
        for a in range(n):
            for j, chip in enumerate(chips):
                copy(a, 1 + j, (*chip, c), (x, y, c)).wait_recv()
                fwd = copy(a, 4 + j, (*chip, c), sibling)
                fwd.start()
                sends.append(fwd)
        for a in range(n):
            copy(a, 0, sibling, (x, y, c)).wait_recv()
            for j, chip in enumerate(chips):
                copy(a, 4 + j, (*chip, 1 - c), (x, y, c)).wait_recv()
        for cp in sends:
            cp.wait_send()
        for cp in pending:
            cp.wait()

    return pl.pallas_call(
        body, name="all_gather_weights", in_specs=[ANY_SPEC] * n, out_specs=[ANY_SPEC] * n,
        out_shape=[_sds((N_DEV,) + b.shape, b.dtype) for b in blocks],
        scratch_shapes=[pltpu.SemaphoreType.DMA((n, 7)), pltpu.SemaphoreType.DMA((n, 7)), pltpu.SemaphoreType.DMA((n,))],
        compiler_params=pltpu.CompilerParams(has_side_effects=True),
    )(*blocks)


def _adamw(w, g, m, v):
    m = ADAM_B1 * m + (1.0 - ADAM_B1) * g
    v = ADAM_B2 * v + (1.0 - ADAM_B2) * (g * g)
    m_hat = m / (1.0 - ADAM_B1 ** ADAM_STEP)
    v_hat = v / (1.0 - ADAM_B2 ** ADAM_STEP)
    return -ADAM_LR * (m_hat / (jnp.sqrt(v_hat) + ADAM_EPS) + ADAM_WD * w), m, v


def _reduce_scatter_adamw(g, w, m, v, name):
    _, r, c_ = g.shape
    rb = min(r, 256)

    def body(g_ref, w_ref, m_ref, v_ref, grad_ref, delta_ref, nm_ref, nv_ref,
             sib_buf, out_buf, ici_buf, send_a, recv_a, send_b, recv_b):
        x, y, c = _place()
        sibling = (x, y, 1 - c)
        chips = [(x ^ fx, y ^ fy) for fx, fy in CHIP_FLIPS]

        def piece(chip, core):
            return g_ref.at[4 * chip[0] + 2 * chip[1] + core]

        to_sibling = [pltpu.make_async_remote_copy(
            src_ref=piece(chip, 1 - c), dst_ref=sib_buf.at[j], send_sem=send_a.at[j], recv_sem=recv_a.at[j],
            device_id=sibling, device_id_type=MESH_IDS) for j, chip in enumerate(chips)]
        for cp in to_sibling:
            cp.start()
        to_owner = []
        for j in (1, 2, 3):
            to_sibling[j].wait_recv()
            own = piece(chips[j], c)
            for r0 in range(0, r, rb):
                rows = slice(r0, r0 + rb)
                out_buf[j, rows, :] = (own[rows, :].astype(F32) + sib_buf[j, rows, :].astype(F32)).astype(BF)
            cp = pltpu.make_async_remote_copy(
                src_ref=out_buf.at[j], dst_ref=ici_buf.at[j], send_sem=send_b.at[j], recv_sem=recv_b.at[j],
                device_id=(*chips[j], c), device_id_type=MESH_IDS)
            cp.start()
            to_owner.append(cp)
        to_sibling[0].wait_recv()
        for cp in to_owner:
            cp.wait_recv()
        own = piece(chips[0], c)
        for r0 in range(0, r, rb):
            rows = slice(r0, r0 + rb)
            total = own[rows, :].astype(F32) + sib_buf[0, rows, :].astype(F32)
            for j in (1, 2, 3):
                total = total + ici_buf[j, rows, :].astype(F32)
            grad_ref[rows, :] = total
            delta, nm, nv = _adamw(w_ref[rows, :], total, m_ref[rows, :], v_ref[rows, :])
            delta_ref[rows, :] = delta
            nm_ref[rows, :] = nm
            nv_ref[rows, :] = nv
        for cp in to_sibling + to_owner:
            cp.wait_send()

    return pl.pallas_call(
        body, name=name, in_specs=[VMEM_SPEC] * 4, out_specs=[VMEM_SPEC] * 4, out_shape=[_sds((r, c_))] * 4,
        scratch_shapes=[pltpu.VMEM((4, r, c_), BF), pltpu.VMEM((4, r, c_), BF), pltpu.VMEM((4, r, c_), BF),
                        pltpu.SemaphoreType.DMA((4,)), pltpu.SemaphoreType.DMA((4,)), pltpu.SemaphoreType.DMA((4,)),
                        pltpu.SemaphoreType.DMA((4,))],
        compiler_params=pltpu.CompilerParams(vmem_limit_bytes=VMEM_LIMIT, has_side_effects=True),
    )(g, w, m, v)


SMALL_ROWS = 16


def _all_reduce_small(packed):
    def body(p_ref, sum_ref, gath, send_sems, recv_sems):
        x, y, c = _place()
        me = 4 * x + 2 * y + c
        gath[me] = p_ref[...]
        copies = []
        for k in range(1, N_DEV):
            peer = (x ^ (k >> 2), y ^ ((k >> 1) & 1), c ^ (k & 1))
            cp = pltpu.make_async_remote_copy(
                src_ref=p_ref, dst_ref=gath.at[me], send_sem=send_sems.at[k], recv_sem=recv_sems.at[k],
                device_id=peer, device_id_type=MESH_IDS)
            cp.start()
            copies.append(cp)
        for cp in copies:
            cp.wait_recv()
        total = gath[0]
        for d in range(1, N_DEV):
            total = total + gath[d]
        sum_ref[...] = total
        for cp in copies:
            cp.wait_send()

    return pl.pallas_call(
        body, name="all_reduce_small", in_specs=[VMEM_SPEC], out_specs=VMEM_SPEC, out_shape=_sds(packed.shape),
        scratch_shapes=[pltpu.VMEM((N_DEV,) + packed.shape, F32), pltpu.SemaphoreType.DMA((N_DEV,)),
                        pltpu.SemaphoreType.DMA((N_DEV,))],
        compiler_params=pltpu.CompilerParams(has_side_effects=True),
    )(packed)


def _adamw_small(w, g, m, v):
    def body(w_ref, g_ref, m_ref, v_ref, delta_ref, nm_ref, nv_ref):
        delta_ref[...], nm_ref[...], nv_ref[...] = _adamw(w_ref[...], g_ref[...], m_ref[...], v_ref[...])

    return pl.pallas_call(body, name="adamw_small", in_specs=[VMEM_SPEC] * 4, out_specs=[VMEM_SPEC] * 3,
                          out_shape=[_sds(w.shape)] * 3)(w, g, m, v)


def _to_aligned(w_native):
    pad = jnp.zeros((w_native.shape[0], PROJ_W - D_PROJ), w_native.dtype)
    return jnp.concatenate([w_native[:, 0:1536], w_native[:, 1544:3080], w_native[:, 3088:3600], w_native[:, 1536:1544],
                            w_native[:, 3080:3088], pad], axis=1)


def _from_aligned(g_al):
    return jnp.concatenate([g_al[:, 0:1536], g_al[:, 3584:3592], g_al[:, 1536:3072], g_al[:, 3592:3600],
                            g_al[:, 3072:3584]], axis=1)


def _pieces_by_col(a):
    r = a.shape[0]
    return a.reshape(r, N_DEV, -1).transpose(1, 0, 2)


def _cols_from_pieces(p):
    return p.transpose(1, 0, 2).reshape(p.shape[1], -1)


SMALL_NORMS = ("pre_mix_norm", "post_mix_norm", "pre_mlp_norm", "post_mlp_norm")
SMALL_MISC = (("fox_out_norm", FOX_HEAD_DIM), ("gdn_out_norm", GDN_HEAD_DIM), ("fox_f_bias", N_FOX_HEADS),
              ("gdn_a_log", N_GDN_HEADS), ("gdn_dt_bias", N_GDN_HEADS))


def _pack_small(vals, conv):
    misc = jnp.concatenate([vals[n].astype(F32) for n, _ in SMALL_MISC])
    rows = [vals[n].astype(F32) for n in SMALL_NORMS] + [jnp.pad(misc, (0, D_MODEL - misc.shape[0]))]
    flat = conv.astype(F32).reshape(-1)
    n_rows = -(-flat.shape[0] // D_MODEL)
    flat = jnp.pad(flat, (0, n_rows * D_MODEL - flat.shape[0])).reshape(n_rows, D_MODEL)
    packed = jnp.concatenate([jnp.stack(rows), flat])
    return jnp.pad(packed, ((0, SMALL_ROWS - packed.shape[0]), (0, 0)))


def _unpack_small(packed, conv_shape):
    out = {n: packed[i] for i, n in enumerate(SMALL_NORMS)}
    off = 0
    for n, size in SMALL_MISC:
        out[n] = packed[4, off:off + size]
        off += size
    n_conv = conv_shape[0] * conv_shape[1]
    out["gdn_conv_w"] = packed[5:].reshape(-1)[:n_conv].reshape(conv_shape)
    return out


WEIGHT_ORDER = ("pre_mix_norm", "w_in", "fox_f_bias", "fox_out_norm", "gdn_conv_w", "gdn_a_log", "gdn_dt_bias",
                "gdn_out_norm", "w_out", "post_mix_norm", "pre_mlp_norm", "w_up", "w_down", "post_mlp_norm")


def kernel(x, pre_mix_norm, w_in, fox_f_bias, fox_out_norm, gdn_conv_w, gdn_a_log, gdn_dt_bias, gdn_out_norm, w_out, post_mix_norm, pre_mlp_norm, w_up, w_down, post_mlp_norm, loss_target, m_pre_mix_norm, m_w_in, m_fox_f_bias, m_fox_out_norm, m_gdn_conv_w, m_gdn_a_log, m_gdn_dt_bias, m_gdn_out_norm, m_w_out, m_post_mix_norm, m_pre_mlp_norm, m_w_up, m_w_down, m_post_mlp_norm, v_pre_mix_norm, v_w_in, v_fox_f_bias, v_fox_out_norm, v_gdn_conv_w, v_gdn_a_log, v_gdn_dt_bias, v_gdn_out_norm, v_w_out, v_post_mix_norm, v_pre_mlp_norm, v_w_up, v_w_down, v_post_mlp_norm):
    w = dict(pre_mix_norm=pre_mix_norm, w_in=w_in, fox_f_bias=fox_f_bias, fox_out_norm=fox_out_norm,
             gdn_conv_w=gdn_conv_w, gdn_a_log=gdn_a_log, gdn_dt_bias=gdn_dt_bias, gdn_out_norm=gdn_out_norm, w_out=w_out,
             post_mix_norm=post_mix_norm, pre_mlp_norm=pre_mlp_norm, w_up=w_up, w_down=w_down, post_mlp_norm=post_mlp_norm)
    mom = dict(pre_mix_norm=m_pre_mix_norm, w_in=m_w_in, fox_f_bias=m_fox_f_bias, fox_out_norm=m_fox_out_norm,
               gdn_conv_w=m_gdn_conv_w, gdn_a_log=m_gdn_a_log, gdn_dt_bias=m_gdn_dt_bias, gdn_out_norm=m_gdn_out_norm,
               w_out=m_w_out, post_mix_norm=m_post_mix_norm, pre_mlp_norm=m_pre_mlp_norm, w_up=m_w_up, w_down=m_w_down,
               post_mlp_norm=m_post_mlp_norm)
    var = dict(pre_mix_norm=v_pre_mix_norm, w_in=v_w_in, fox_f_bias=v_fox_f_bias, fox_out_norm=v_fox_out_norm,
               gdn_conv_w=v_gdn_conv_w, gdn_a_log=v_gdn_a_log, gdn_dt_bias=v_gdn_dt_bias, gdn_out_norm=v_gdn_out_norm,
               w_out=v_w_out, post_mix_norm=v_post_mix_norm, pre_mlp_norm=v_pre_mlp_norm, w_up=v_w_up, w_down=v_w_down,
               post_mlp_norm=v_post_mlp_norm)

    win_g, wout_g, wup_g, wdown_g, conv_g = _all_gather(
        [w_in.astype(BF), w_out.astype(BF), w_up.astype(BF), w_down.astype(BF), gdn_conv_w])
    w_al = _to_aligned(_cols_from_pieces(win_g))
    w_out_full = wout_g.reshape(D_MODEL, D_MODEL)
    w_up_full = _cols_from_pieces(wup_g)
    w_down_full = wdown_g.reshape(D_FF, D_MODEL)
    convw = _cols_from_pieces(conv_g)

    loss, grad_x, g_in, g_out, g_up, g_down, small = _local_step(
        x[0], loss_target[0], w_al, w_out_full, w_up_full, w_down_full, convw, pre_mix_norm, fox_f_bias, fox_out_norm,
        gdn_a_log, gdn_dt_bias, gdn_out_norm, post_mix_norm, pre_mlp_norm, post_mlp_norm)
    loss = lax.psum(loss, ("x", "y", "c"))

    grads, delta, new_m, new_v = {}, {}, {}, {}
    big = dict(w_in=_pieces_by_col(_from_aligned(g_in)), w_out=g_out, w_up=g_up, w_down=g_down)
    for n, g in big.items():
        grads[n], delta[n], new_m[n], new_v[n] = _reduce_scatter_adamw(g, w[n], mom[n], var[n], "reduce_scatter_" + n)

    dwq, dwk, dwv = small.pop("gdn_conv_w")
    total = _unpack_small(_all_reduce_small(_pack_small(small, jnp.concatenate([dwq, dwk, dwv], axis=1))),
                          (CONV_K, 3 * D_GDN))
    me = 4 * lax.axis_index("x") + 2 * lax.axis_index("y") + lax.axis_index("c")
    n_conv = gdn_conv_w.shape[1]
    total["gdn_conv_w"] = lax.dynamic_slice_in_dim(total["gdn_conv_w"], me * n_conv, n_conv, axis=1)
    grads.update(total)
    d_s, m_s, v_s = _adamw_small(_pack_small(w, gdn_conv_w), _pack_small(total, total["gdn_conv_w"]),
                                 _pack_small(mom, m_gdn_conv_w), _pack_small(var, v_gdn_conv_w))
    delta.update(_unpack_small(d_s, gdn_conv_w.shape))
    new_m.update(_unpack_small(m_s, gdn_conv_w.shape))
    new_v.update(_unpack_small(v_s, gdn_conv_w.shape))

    return (loss, grad_x[None], *[grads[n] for n in WEIGHT_ORDER], *[delta[n] for n in WEIGHT_ORDER],
            *[new_m[n] for n in WEIGHT_ORDER], *[new_v[n] for n in WEIGHT_ORDER])
```

```python
import jax
import jax.numpy as jnp
from jax import lax
from jax.experimental import pallas as pl
from jax.experimental.pallas import tpu as pltpu

F32 = jnp.float32
BF = jnp.bfloat16

D_MODEL = 1024
N_FOX_HEADS, FOX_HEAD_DIM = 8, 64
N_GDN_HEADS, GDN_HEAD_DIM = 4, 128
D_FOX = N_FOX_HEADS * FOX_HEAD_DIM
D_GDN = N_GDN_HEADS * GDN_HEAD_DIM
CHUNK = 64
CONV_K = 4
D_FF = 4 * D_MODEL
EPS = 1e-6
D_PROJ = 3600
N_DEV = 8

PROJ_W = 3712
COL_FOX, COL_GDN, COL_GZ, COL_SMALL = 0, 1536, 3072, 3584
LANES = 128
SM_FF, SM_GB, SM_GA = 0, 8, 12

ADAM_LR, ADAM_B1, ADAM_B2, ADAM_EPS, ADAM_WD, ADAM_STEP = 0.001, 0.9, 0.999, 1e-08, 0.01, 10

TOKEN_BLOCK = 256
FOX_SCALE = FOX_HEAD_DIM ** -0.5
GDN_QSCALE = GDN_HEAD_DIM ** -0.5
NEG_BIG = -1e30
VMEM_LIMIT = 56 * 1024 * 1024

VMEM_SPEC = pl.BlockSpec(memory_space=pltpu.VMEM)
HIGHEST = lax.Precision.HIGHEST


def _sds(shape, dtype=F32):
    return jax.ShapeDtypeStruct(shape, dtype)


def _params(*sem):
    return pltpu.CompilerParams(dimension_semantics=sem if sem else None, vmem_limit_bytes=VMEM_LIMIT)


def _mm(a, b):
    return jnp.dot(a.astype(BF), b.astype(BF), preferred_element_type=F32)


def _mm_nt(a, b):
    return lax.dot_general(a.astype(BF), b.astype(BF), (((1,), (1,)), ((), ())), preferred_element_type=F32)


def _mm_tn(a, b):
    return lax.dot_general(a.astype(BF), b.astype(BF), (((0,), (0,)), ((), ())), preferred_element_type=F32)


def _mm_exact(a, b):
    return jnp.dot(a, b, precision=HIGHEST, preferred_element_type=F32)


def _mm_tn_exact(a, b):
    return lax.dot_general(a, b, (((0,), (0,)), ((), ())), precision=HIGHEST, preferred_element_type=F32)


def _sigmoid(x):
    return 1.0 / (1.0 + jnp.exp(-x))


def _softplus(x):
    return jnp.maximum(x, 0.0) + jnp.log1p(jnp.exp(-jnp.abs(x)))


def _iota(shape, dim):
    return lax.broadcasted_iota(jnp.int32, shape, dim)


def _shift_down(x, s, row):
    return jnp.where(row >= s, pltpu.roll(x, s, 0), 0.0)


def _shift_up(x, s, row):
    n = x.shape[0]
    return jnp.where(row < n - s, pltpu.roll(x, n - s, 0), 0.0)


def _norm_proj(x, nw, wt_al):
    t = x.shape[0]

    def body(x_ref, nw_ref, w_ref, proj_ref, h_ref):
        xv = x_ref[...]
        r = lax.rsqrt(jnp.mean(xv * xv, axis=-1, keepdims=True) + EPS)
        h = (xv * r * nw_ref[...]).astype(BF)
        h_ref[...] = h
        proj_ref[...] = lax.dot_general(h, w_ref[...], (((1,), (1,)), ((), ())), preferred_element_type=F32)

    tm = TOKEN_BLOCK
    return pl.pallas_call(
        body, name="norm_proj", grid=(t // tm,),
        in_specs=[pl.BlockSpec((tm, D_MODEL), lambda i: (i, 0)), pl.BlockSpec((1, D_MODEL), lambda i: (0, 0)),
                  pl.BlockSpec((PROJ_W, D_MODEL), lambda i: (0, 0))],
        out_specs=[pl.BlockSpec((tm, PROJ_W), lambda i: (i, 0)), pl.BlockSpec((tm, D_MODEL), lambda i: (i, 0))],
        out_shape=[_sds((t, PROJ_W)), _sds((t, D_MODEL), BF)],
        compiler_params=_params("parallel"),
    )(x, nw, wt_al)


def _expand_matrix(first_row, group):
    row = _iota((LANES, 512), 0)
    col = _iota((LANES, 512), 1)
    return (col // group + first_row == row).astype(F32)


def _small_prep(proj, fb, al, dtb):
    t = proj.shape[0]

    def body(sm_ref, fb_ref, al_ref, dtb_ref, ce_ref, cumt_ref, be_ref, ge_ref):
        s = sm_ref[...]
        z = s + fb_ref[...]
        cum = jnp.minimum(z, 0.0) - jnp.log1p(jnp.exp(-jnp.abs(z)))
        row = _iota((t, LANES), 0)
        step = 1
        while step < t:
            cum = cum + _shift_down(cum, step, row)
            step *= 2
        cumt_ref[...] = cum.T
        ce_ref[...] = _mm_exact(cum, _expand_matrix(SM_FF, FOX_HEAD_DIM))
        be_ref[...] = _mm_exact(_sigmoid(s), _expand_matrix(SM_GB, GDN_HEAD_DIM))
        g = -jnp.exp(al_ref[...]) * _softplus(s + dtb_ref[...])
        ge_ref[...] = _mm_exact(g, _expand_matrix(SM_GA, GDN_HEAD_DIM))

    vec = pl.BlockSpec((1, LANES), lambda i: (0, 0))
    return pl.pallas_call(
        body, name="small_prep", grid=(1,),
        in_specs=[pl.BlockSpec((t, LANES), lambda i: (0, COL_SMALL // LANES)), vec, vec, vec],
        out_specs=[pl.BlockSpec((t, 512), lambda i: (0, 0)), pl.BlockSpec((LANES, t), lambda i: (0, 0)),
                   pl.BlockSpec((t, 512), lambda i: (0, 0)), pl.BlockSpec((t, 512), lambda i: (0, 0))],
        out_shape=[_sds((t, 512)), _sds((LANES, t)), _sds((t, 512)), _sds((t, 512))],
        compiler_params=_params("arbitrary"),
    )(proj, fb, al, dtb)


def _fox_scores(qh, kb, ce_ref, cumt_ref, head, hh, i, tq):
    klen = (i + 1) * tq
    s = _mm_nt(qh, kb[:klen]) * FOX_SCALE
    cq = ce_ref[i * tq:(i + 1) * tq, FOX_HEAD_DIM * hh:FOX_HEAD_DIM * hh + 1]
    ck = cumt_ref[pl.ds(head, 1), 0:klen]
    s = s + cq - ck
    qi = _iota((tq, klen), 0) + i * tq
    ki = _iota((tq, klen), 1)
    return jnp.where(ki <= qi, s, NEG_BIG)


def _fox_fwd(proj, ce, cumt, fnw):
    t = proj.shape[0]
    tq = min(TOKEN_BLOCK, t // 2)
    nq = t // tq

    def body(q_ref, k_ref, v_ref, ce_ref, cumt_ref, fnw_ref, o_ref, lse_ref, fn_ref):
        j = pl.program_id(0)
        first = _iota((1, LANES), 1) < FOX_HEAD_DIM
        kb = k_ref[...].astype(BF)
        vb = v_ref[...].astype(BF)
        for i in range(nq):
            rows = slice(i * tq, (i + 1) * tq)
            klen = (i + 1) * tq
            q_i = q_ref[rows, :]
            o_acc = jnp.zeros((tq, LANES), F32)
            lse_acc = jnp.zeros((tq, LANES), F32)
            for hh in range(2):
                mh = first if hh == 0 else jnp.logical_not(first)
                qh = jnp.where(mh, q_i, 0.0).astype(BF)
                s = _fox_scores(qh, kb, ce_ref, cumt_ref, 2 * j + hh, hh, i, tq)
                m = jnp.max(s, axis=-1, keepdims=True)
                p = jnp.exp(s - m)
                l = jnp.sum(p, axis=-1, keepdims=True)
                o = jnp.dot(p.astype(BF), vb[:klen], preferred_element_type=F32) / l
                o_acc = jnp.where(mh, o, o_acc)
                lse_acc = jnp.where(mh, m + jnp.log(l), lse_acc)
            o_ref[rows, :] = o_acc
            lse_ref[rows, :] = lse_acc
            o2 = o_acc * o_acc
            s0 = jnp.sum(jnp.where(first, o2, 0.0), axis=-1, keepdims=True)
            s1 = jnp.sum(jnp.where(first, 0.0, o2), axis=-1, keepdims=True)
            r = lax.rsqrt(jnp.where(first, s0, s1) * (1.0 / FOX_HEAD_DIM) + EPS)
            fn_ref[rows, :] = (o_acc * r * fnw_ref[...]).astype(BF)

    blk = lambda off: pl.BlockSpec((t, LANES), lambda j: (0, off + j))
    return pl.pallas_call(
        body, name="fox_fwd", grid=(N_FOX_HEADS // 2,),
        in_specs=[blk(0), blk(4), blk(8), blk(0), pl.BlockSpec((LANES, t), lambda j: (0, 0)),
                  pl.BlockSpec((1, LANES), lambda j: (0, 0))],
        out_specs=[blk(0), blk(0), blk(0)],
        out_shape=[_sds((t, D_FOX)), _sds((t, D_FOX)), _sds((t, D_FOX), BF)],
        compiler_params=_params("parallel"),
    )(proj, proj, proj, ce, cumt, fnw)


def _fox_bwd(proj, ce, cumt, lse, o, do):
    t = proj.shape[0]
    tq = min(TOKEN_BLOCK, t // 2)
    nq = t // tq

    def body(q_ref, k_ref, v_ref, ce_ref, cumt_ref, lse_ref, o_ref, do_ref,
             dq_ref, dk_ref, dv_ref, dcq_ref, dckt_ref, dk_s, dv_s, dck_s):
        j = pl.program_id(0)
        first = _iota((1, LANES), 1) < FOX_HEAD_DIM
        kf = k_ref[...]
        kb = kf.astype(BF)
        vb = v_ref[...].astype(BF)
        dk_s[...] = jnp.zeros_like(dk_s)
        dv_s[...] = jnp.zeros_like(dv_s)
        dck_s[...] = jnp.zeros_like(dck_s)
        masks = [first, jnp.logical_not(first)]
        kmask = [jnp.where(mh, kf, 0.0).astype(BF) for mh in masks]
        for i in range(nq):
            rows = slice(i * tq, (i + 1) * tq)
            klen = (i + 1) * tq
            q_i = q_ref[rows, :]
            do_i = do_ref[rows, :]
            o_i = o_ref[rows, :]
            lse_i = lse_ref[rows, :]
            dq_acc = jnp.zeros((tq, LANES), F32)
            dcq_acc = jnp.zeros((tq, LANES), F32)
            for hh in range(2):
                mh = masks[hh]
                qh = jnp.where(mh, q_i, 0.0).astype(BF)
                doh = jnp.where(mh, do_i, 0.0)
                dohb = doh.astype(BF)
                delta = jnp.sum(doh * o_i, axis=-1, keepdims=True)
                s = _fox_scores(qh, kb, ce_ref, cumt_ref, 2 * j + hh, hh, i, tq)
                p = jnp.exp(s - lse_i[:, FOX_HEAD_DIM * hh:FOX_HEAD_DIM * hh + 1])
                dp = _mm_nt(dohb, vb[:klen])
                ds = p * (dp - delta)
                dsb = ds.astype(BF)
                dq_acc = dq_acc + jnp.dot(dsb, kmask[hh][:klen], preferred_element_type=F32) * FOX_SCALE
                dk_s[0:klen, :] += _mm_tn(dsb, qh) * FOX_SCALE
                dv_s[0:klen, :] += _mm_tn(p, dohb)
                dcq_acc = jnp.where(mh, jnp.sum(ds, axis=-1, keepdims=True), dcq_acc)
                dck_s[hh:hh + 1, 0:klen] += jnp.sum(ds, axis=0, keepdims=True)
            dq_ref[rows, :] = dq_acc
            dcq_ref[rows, :] = dcq_acc
        dk_ref[...] = dk_s[...]
        dv_ref[...] = dv_s[...]
        dckt_ref[...] = jnp.zeros_like(dckt_ref)
        dckt_ref[0:8, :] = dck_s[...]

    blk = lambda off: pl.BlockSpec((t, LANES), lambda j: (0, off + j))
    return pl.pallas_call(
        body, name="fox_bwd", grid=(N_FOX_HEADS // 2,),
        in_specs=[blk(0), blk(4), blk(8), blk(0), pl.BlockSpec((LANES, t), lambda j: (0, 0)), blk(0), blk(0), blk(0)],
        out_specs=[blk(0), blk(0), blk(0), blk(0), pl.BlockSpec((32, t), lambda j: (j, 0))],
        out_shape=[_sds((t, D_FOX))] * 4 + [_sds((LANES, t))],
        scratch_shapes=[pltpu.VMEM((t, LANES), F32), pltpu.VMEM((t, LANES), F32), pltpu.VMEM((8, t), F32)],
        compiler_params=_params("parallel"),
    )(proj, proj, proj, ce, cumt, lse, o, do)


def _conv(x, w, row):
    return (w[3:4, :] * x + w[2:3, :] * _shift_down(x, 1, row) + w[1:2, :] * _shift_down(x, 2, row)
            + w[0:1, :] * _shift_down(x, 3, row))


def _chunk_decay(gc_c):
    gi = gc_c[:, 0:CHUNK]
    gj = gc_c.T[0:CHUNK, :]
    ri = _iota((CHUNK, CHUNK), 0)
    cj = _iota((CHUNK, CHUNK), 1)
    return jnp.where(ri >= cj, jnp.exp(jnp.minimum(gi - gj, 0.0)), 0.0), ri > cj


def _gdn_specs(t):
    col = lambda off: pl.BlockSpec((t, LANES), lambda h: (0, off + h))
    cw = lambda off: pl.BlockSpec((CONV_K, LANES), lambda h: (0, off + h))
    mat = pl.BlockSpec((1, t // CHUNK, CHUNK, CHUNK), lambda h: (h, 0, 0, 0))
    return col, cw, mat


def _gdn_prep(proj, convw, be, ge):
    t = proj.shape[0]
    nch = t // CHUNK

    def body(xq_ref, xk_ref, xv_ref, wq_ref, wk_ref, wv_ref, be_ref, ge_ref,
             qn_ref, kn_ref, cv_ref, gc_ref, m_ref, a_ref):
        row = _iota((t, LANES), 0)

        def act(x_ref, w_ref):
            y = _conv(x_ref[...], w_ref[...], row)
            return y * _sigmoid(y)

        cq = act(xq_ref, wq_ref)
        ck = act(xk_ref, wk_ref)
        cv_ref[...] = act(xv_ref, wv_ref)
        qn_ref[...] = cq * lax.rsqrt(jnp.sum(cq * cq, axis=-1, keepdims=True) + EPS) * GDN_QSCALE
        kn_ref[...] = ck * lax.rsqrt(jnp.sum(ck * ck, axis=-1, keepdims=True) + EPS)
        gc = ge_ref[...]
        pos = row % CHUNK
        step = 1
        while step < CHUNK:
            gc = gc + jnp.where(pos >= step, pltpu.roll(gc, step, 0), 0.0)
            step *= 2
        gc_ref[...] = gc

        def chunk(n, carry):
            sl = pl.ds(pl.multiple_of(n * CHUNK, CHUNK), CHUNK)
            k_c = kn_ref[sl, :]
            decay, strict = _chunk_decay(gc_ref[sl, :])
            m_ref[0, n] = jnp.where(strict, _mm_nt(k_c * be_ref[sl, :], k_c) * decay, 0.0)
            a_ref[0, n] = _mm_nt(qn_ref[sl, :], k_c) * decay
            return carry

        lax.fori_loop(0, nch, chunk, 0)

    col, cw, mat = _gdn_specs(t)
    return pl.pallas_call(
        body, name="gdn_prep", grid=(N_GDN_HEADS,),
        in_specs=[col(12), col(16), col(20), cw(0), cw(4), cw(8), col(0), col(0)],
        out_specs=[col(0), col(0), col(0), col(0), mat, mat],
        out_shape=[_sds((t, D_GDN))] * 4 + [_sds((N_GDN_HEADS, nch, CHUNK, CHUNK))] * 2,
        compiler_params=_params("parallel"),
    )(proj, proj, proj, convw, convw, convw, be, ge)


def _tri_inverse(m2):
    n_prob = m2.shape[0]
    assert n_prob == LANES
    nb = CHUNK * CHUNK // LANES

    def body(m_ref, t_ref, ms, ts):
        for b in range(nb):
            ms[b * LANES:(b + 1) * LANES, :] = m_ref[:, b * LANES:(b + 1) * LANES].T
        cidx = _iota((CHUNK, LANES), 0)

        def outer(i, carry):
            def inner(jj, acc):
                mrow = ms[pl.ds(i * CHUNK + jj, 1), :]
                return acc - mrow * ts[pl.ds(pl.multiple_of(jj * CHUNK, CHUNK), CHUNK), :]

            acc = lax.fori_loop(0, i, inner, jnp.where(cidx == i, 1.0, 0.0).astype(F32))
            ts[pl.ds(pl.multiple_of(i * CHUNK, CHUNK), CHUNK), :] = acc
            return carry

        lax.fori_loop(0, CHUNK, outer, 0)
        for b in range(nb):
            t_ref[:, b * LANES:(b + 1) * LANES] = ts[b * LANES:(b + 1) * LANES, :].T

    return pl.pallas_call(
        body, name="tri_inverse", in_specs=[VMEM_SPEC], out_specs=VMEM_SPEC,
        out_shape=_sds((LANES, CHUNK * CHUNK)),
        scratch_shapes=[pltpu.VMEM((CHUNK * CHUNK, LANES), F32), pltpu.VMEM((CHUNK * CHUNK, LANES), F32)],
        compiler_params=_params(),
    )(m2)


def _gdn_chunk_terms(q, k, v, b, gcc):
    eg = jnp.exp(gcc)
    last = gcc[CHUNK - 1:CHUNK, :]
    egl = jnp.exp(last - gcc)
    gl = jnp.exp(last)
    kb = k * b
    return eg, egl, gl, kb, v * b, kb * eg, q * eg, k * egl


GDN_BLOCK_CHUNKS = 4


def _gdn_block_specs(t, reverse):
    cb = GDN_BLOCK_CHUNKS
    nb = t // (cb * CHUNK)
    idx = (lambda i: nb - 1 - i) if reverse else (lambda i: i)
    tok = pl.BlockSpec((cb * CHUNK, D_GDN), lambda i: (idx(i), 0))
    mat = pl.BlockSpec((N_GDN_HEADS, cb, CHUNK, CHUNK), lambda i: (0, idx(i), 0, 0))
    state = pl.BlockSpec((N_GDN_HEADS, cb, GDN_HEAD_DIM, GDN_HEAD_DIM), lambda i: (0, idx(i), 0, 0))
    return nb, tok, mat, state


def _gdn_scan(qn, kn, cv, be, gc, tinv, amat):
    t = qn.shape[0]
    nch = t // CHUNK

    def body(q_ref, k_ref, v_ref, b_ref, gc_ref, t_ref, a_ref, o_ref, sall_ref, vn_ref, s_scr):
        @pl.when(pl.program_id(0) == 0)
        def _():
            s_scr[...] = jnp.zeros_like(s_scr)

        for hd in range(N_GDN_HEADS):
            cs = slice(hd * LANES, (hd + 1) * LANES)
            s = s_scr[hd]
            for cc in range(GDN_BLOCK_CHUNKS):
                rs = slice(cc * CHUNK, (cc + 1) * CHUNK)
                eg, egl, gl, kb, vb, kbg, qd, kd = _gdn_chunk_terms(q_ref[rs, cs], k_ref[rs, cs], v_ref[rs, cs],
                                                                    b_ref[rs, cs], gc_ref[rs, cs])
                tn = t_ref[hd, cc]
                sall_ref[hd, cc] = s
                vn = _mm(tn, vb) - _mm(_mm(tn, kbg), s)
                vn_ref[rs, cs] = vn
                o_ref[rs, cs] = _mm(qd, s) + _mm(a_ref[hd, cc], vn)
                s = s * gl + _mm_tn(kd, vn)
            s_scr[hd] = s

    nb, tok, mat, state = _gdn_block_specs(t, False)
    return pl.pallas_call(
        body, name="gdn_scan", grid=(nb,),
        in_specs=[tok] * 5 + [mat, mat], out_specs=[tok, state, tok],
        out_shape=[_sds((t, D_GDN)), _sds((N_GDN_HEADS, nch, GDN_HEAD_DIM, GDN_HEAD_DIM)), _sds((t, D_GDN))],
        scratch_shapes=[pltpu.VMEM((N_GDN_HEADS, GDN_HEAD_DIM, GDN_HEAD_DIM), F32)],
        compiler_params=_params("arbitrary"),
    )(qn, kn, cv, be, gc, tinv, amat)


def _gdn_bwd(qn, kn, cv, be, gc, tinv, amat, s_all, vn_all, do):
    t = qn.shape[0]

    def body(q_ref, k_ref, v_ref, b_ref, gc_ref, t_ref, a_ref, sall_ref, vn_ref, do_ref,
             dq_ref, dk_ref, dv_ref, db_ref, dg_ref, ds_scr):
        @pl.when(pl.program_id(0) == 0)
        def _():
            ds_scr[...] = jnp.zeros_like(ds_scr)

        ones = jnp.ones((CHUNK, LANES), F32)
        ri = _iota((CHUNK, CHUNK), 0)
        cj = _iota((CHUNK, CHUNK), 1)
        upper = (cj >= ri).astype(F32)
        lastrow = _iota((CHUNK, LANES), 0) == CHUNK - 1
        for hd in range(N_GDN_HEADS):
            cs = slice(hd * LANES, (hd + 1) * LANES)
            dsp = ds_scr[hd]
            for cc in reversed(range(GDN_BLOCK_CHUNKS)):
                rs = slice(cc * CHUNK, (cc + 1) * CHUNK)
                q, k, v, b, gcc = q_ref[rs, cs], k_ref[rs, cs], v_ref[rs, cs], b_ref[rs, cs], gc_ref[rs, cs]
                eg, egl, gl, kb, vb, kbg, qd, kd = _gdn_chunk_terms(q, k, v, b, gcc)
                do_c = do_ref[rs, cs]
                tn = t_ref[hd, cc]
                an = a_ref[hd, cc]
                s = sall_ref[hd, cc]
                vn = vn_ref[rs, cs]
                w = _mm(tn, kbg)
                dvn = _mm_tn(an, do_c) + _mm(kd, dsp)
                da = _mm_nt(do_c, vn)
                dqd = _mm_nt(do_c, s)
                dkd = _mm_nt(vn, dsp)
                dgl = jnp.sum(jnp.sum(dsp * s, axis=-1, keepdims=True), axis=0, keepdims=True)
                dsp = _mm_tn(qd, do_c) + gl * dsp - _mm_tn(w, dvn)
                dw = -_mm_nt(dvn, s)
                dt = _mm_nt(dvn, vb) + _mm_nt(dw, kbg)
                dvb = _mm_tn(tn, dvn)
                dkbg = _mm_tn(tn, dw)
                decay, strict = _chunk_decay(gcc)
                kk = _mm_nt(kb, k)
                qk = _mm_nt(q, k)
                dm = jnp.where(strict, -_mm_nt(_mm_tn(tn, dt), tn), 0.0)
                dkk = dm * decay
                dqk = da * decay
                gmat = dkk * kk + dqk * qk
                dq_ref[rs, cs] = dqd * eg + _mm(dqk, k)
                dkb = _mm(dkk, k) + dkbg * eg
                dk_ref[rs, cs] = dkd * egl + _mm_tn(dqk, q) + _mm_tn(dkk, kb) + dkb * b
                db = jnp.sum(dkb * k, axis=-1, keepdims=True) + jnp.sum(dvb * v, axis=-1, keepdims=True)
                db_ref[rs, cs] = jnp.broadcast_to(db, (CHUNK, LANES))
                dv_ref[rs, cs] = dvb * b
                dkd_kd = jnp.sum(dkd * kd, axis=-1, keepdims=True)
                dgc = (jnp.sum(gmat, axis=-1, keepdims=True) - _mm_tn_exact(gmat, ones)
                       + jnp.sum(dqd * qd, axis=-1, keepdims=True) + jnp.sum(dkbg * kbg, axis=-1, keepdims=True) - dkd_kd)
                extra = jnp.sum(dkd_kd, axis=0, keepdims=True) + dgl * gl
                dgc = dgc + jnp.where(lastrow, extra, 0.0)
                dg_ref[rs, cs] = _mm_exact(upper, dgc)
            ds_scr[hd] = dsp

    nb, tok, mat, state = _gdn_block_specs(t, True)
    return pl.pallas_call(
        body, name="gdn_bwd", grid=(nb,),
        in_specs=[tok] * 5 + [mat, mat, state, tok, tok], out_specs=[tok] * 5, out_shape=[_sds((t, D_GDN))] * 5,
        scratch_shapes=[pltpu.VMEM((N_GDN_HEADS, GDN_HEAD_DIM, GDN_HEAD_DIM), F32)],
        compiler_params=_params("arbitrary"),
    )(qn, kn, cv, be, gc, tinv, amat, s_all, vn_all, do)


def _gdn_bwd_conv(proj, convw, dqn, dkn, dcv):
    t = proj.shape[0]

    def body(xq_ref, xk_ref, xv_ref, wq_ref, wk_ref, wv_ref, dq_ref, dk_ref, dv_ref,
             dxq_ref, dxk_ref, dxv_ref, dwq_ref, dwk_ref, dwv_ref):
        row = _iota((t, LANES), 0)

        def one(x_ref, w_ref, d_ref, dx_ref, dw_ref, scale):
            x = x_ref[...]
            w = w_ref[...]
            y = _conv(x, w, row)
            sg = _sigmoid(y)
            dc = d_ref[...]
            if scale is not None:
                c = y * sg
                r = lax.rsqrt(jnp.sum(c * c, axis=-1, keepdims=True) + EPS)
                ch = c * r
                dc = scale * r * (dc - ch * jnp.sum(dc * ch, axis=-1, keepdims=True))
            dy = dc * sg * (1.0 + y * (1.0 - sg))
            dx_ref[...] = (w[3:4, :] * dy + w[2:3, :] * _shift_up(dy, 1, row) + w[1:2, :] * _shift_up(dy, 2, row)
                           + w[0:1, :] * _shift_up(dy, 3, row))
            for jj in range(CONV_K):
                xs = x if jj == CONV_K - 1 else _shift_down(x, CONV_K - 1 - jj, row)
                dw_ref[jj:jj + 1, :] = jnp.sum(dy * xs, axis=0, keepdims=True)

        one(xq_ref, wq_ref, dq_ref, dxq_ref, dwq_ref, GDN_QSCALE)
        one(xk_ref, wk_ref, dk_ref, dxk_ref, dwk_ref, 1.0)
        one(xv_ref, wv_ref, dv_ref, dxv_ref, dwv_ref, None)

    col, cw, _ = _gdn_specs(t)
    return pl.pallas_call(
        body, name="gdn_bwd_conv", grid=(N_GDN_HEADS,),
        in_specs=[col(12), col(16), col(20), cw(0), cw(4), cw(8), col(0), col(0), col(0)],
        out_specs=[col(0), col(0), col(0), cw(0), cw(0), cw(0)],
        out_shape=[_sds((t, D_GDN))] * 3 + [_sds((CONV_K, D_GDN))] * 3,
        compiler_params=_params("parallel"),
    )(proj, proj, proj, convw, convw, convw, dqn, dkn, dcv)


def _mix_out(fox_n, gdn_o, proj, gnw, w_out, x, pmw, plw):
    t = x.shape[0]
    tm = TOKEN_BLOCK

    def body(fn_ref, go_ref, gz_ref, gnw_ref, w_ref, x_ref, pmw_ref, plw_ref, x1_ref, h2_ref, mixed_ref, omix_ref,
             h2t_ref):
        omix_ref[:, 0:D_FOX] = fn_ref[...]
        for hd in range(N_GDN_HEADS):
            cs = slice(hd * LANES, (hd + 1) * LANES)
            go = go_ref[:, cs]
            r = lax.rsqrt(jnp.mean(go * go, axis=-1, keepdims=True) + EPS)
            gz = gz_ref[:, cs]
            omix_ref[:, D_FOX + hd * LANES:D_FOX + (hd + 1) * LANES] = (
                go * r * gnw_ref[...] * (gz * _sigmoid(gz))).astype(BF)
        mixed = jnp.dot(omix_ref[...], w_ref[...], preferred_element_type=F32)
        mixed_ref[...] = mixed
        r2 = lax.rsqrt(jnp.mean(mixed * mixed, axis=-1, keepdims=True) + EPS)
        x1 = x_ref[...] + mixed * r2 * pmw_ref[...]
        x1_ref[...] = x1
        r3 = lax.rsqrt(jnp.mean(x1 * x1, axis=-1, keepdims=True) + EPS)
        h2 = x1 * r3 * plw_ref[...]
        h2_ref[...] = h2.astype(BF)
        h2t_ref[...] = h2.T.astype(BF)

    tok = lambda w: pl.BlockSpec((tm, w), lambda i: (i, 0))
    vec = lambda w: pl.BlockSpec((1, w), lambda i: (0, 0))
    return pl.pallas_call(
        body, name="mix_out", grid=(t // tm,),
        in_specs=[tok(D_FOX), tok(D_GDN), pl.BlockSpec((tm, D_GDN), lambda i: (i, COL_GZ // D_GDN)), vec(LANES),
                  pl.BlockSpec((D_MODEL, D_MODEL), lambda i: (0, 0)), tok(D_MODEL), vec(D_MODEL), vec(D_MODEL)],
        out_specs=[tok(D_MODEL)] * 4 + [pl.BlockSpec((D_MODEL, tm), lambda i: (0, i))],
        out_shape=[_sds((t, D_MODEL)), _sds((t, D_MODEL), BF), _sds((t, D_MODEL)), _sds((t, D_MODEL), BF),
                   _sds((D_MODEL, t), BF)],
        compiler_params=_params("parallel"),
    )(fox_n, gdn_o, proj, gnw, w_out, x, pmw, plw)


def _out_bwd(dmixed, w_out, o_fox, gdn_o, proj, fnw, gnw):
    t = dmixed.shape[0]
    tm = TOKEN_BLOCK

    def body(dm_ref, w_ref, of_ref, go_ref, gz_ref, fnw_ref, gnw_ref, dof_ref, dgo_ref, dgz_ref, dfw_ref, dgw_ref):
        i = pl.program_id(0)

        @pl.when(i == 0)
        def _():
            dfw_ref[...] = jnp.zeros_like(dfw_ref)
            dgw_ref[...] = jnp.zeros_like(dgw_ref)

        domix = _mm_nt(dm_ref[...], w_ref[...])
        first = _iota((1, LANES), 1) < FOX_HEAD_DIM
        dfw = jnp.zeros((1, LANES), F32)
        dgw = jnp.zeros((1, LANES), F32)
        for pr in range(N_FOX_HEADS // 2):
            cs = slice(pr * LANES, (pr + 1) * LANES)
            o = of_ref[:, cs]
            dfn = domix[:, cs]
            o2 = o * o
            s0 = jnp.sum(jnp.where(first, o2, 0.0), axis=-1, keepdims=True)
            s1 = jnp.sum(jnp.where(first, 0.0, o2), axis=-1, keepdims=True)
            r = lax.rsqrt(jnp.where(first, s0, s1) * (1.0 / FOX_HEAD_DIM) + EPS)
            oh = o * r
            dfw = dfw + jnp.sum(dfn * oh, axis=0, keepdims=True)
            doh = dfn * fnw_ref[...]
            pr_ = doh * oh
            m0 = jnp.sum(jnp.where(first, pr_, 0.0), axis=-1, keepdims=True)
            m1 = jnp.sum(jnp.where(first, 0.0, pr_), axis=-1, keepdims=True)
            dof_ref[:, cs] = r * (doh - oh * jnp.where(first, m0, m1) * (1.0 / FOX_HEAD_DIM))
        for hd in range(N_GDN_HEADS):
            cs = slice(hd * LANES, (hd + 1) * LANES)
            go = go_ref[:, cs]
            gz = gz_ref[:, cs]
            dgated = domix[:, D_FOX + hd * LANES:D_FOX + (hd + 1) * LANES]
            r = lax.rsqrt(jnp.mean(go * go, axis=-1, keepdims=True) + EPS)
            goh = go * r
            sg = _sigmoid(gz)
            sz = gz * sg
            gn = goh * gnw_ref[...]
            dgn = dgated * sz
            dgz_ref[:, cs] = dgated * gn * sg * (1.0 + gz * (1.0 - sg))
            dgw = dgw + jnp.sum(dgn * goh, axis=0, keepdims=True)
            dgh = dgn * gnw_ref[...]
            dgo_ref[:, cs] = r * (dgh - goh * jnp.mean(dgh * goh, axis=-1, keepdims=True))
        dfw_ref[...] += dfw + pltpu.roll(dfw, FOX_HEAD_DIM, 1)
        dgw_ref[...] += dgw

    tok = lambda w: pl.BlockSpec((tm, w), lambda i: (i, 0))
    vec = lambda w: pl.BlockSpec((1, w), lambda i: (0, 0))
    return pl.pallas_call(
        body, name="out_bwd", grid=(t // tm,),
        in_specs=[tok(D_MODEL), pl.BlockSpec((D_MODEL, D_MODEL), lambda i: (0, 0)), tok(D_FOX), tok(D_GDN),
                  pl.BlockSpec((tm, D_GDN), lambda i: (i, COL_GZ // D_GDN)), vec(LANES), vec(LANES)],
        out_specs=[tok(D_FOX), tok(D_GDN), tok(D_GDN), vec(LANES), vec(LANES)],
        out_shape=[_sds((t, D_FOX)), _sds((t, D_GDN)), _sds((t, D_GDN)), _sds((1, LANES)), _sds((1, LANES))],
        compiler_params=_params("arbitrary"),
    )(dmixed, w_out, o_fox, gdn_o, proj, fnw, gnw)


def _mlp_up(h2, w_up):
    t = h2.shape[0]
    tm = TOKEN_BLOCK
    pc = D_FF // N_DEV

    def body(h_ref, w_ref, up_ref):
        h = h_ref[...]
        for p in range(N_DEV):
            up_ref[:, p * pc:(p + 1) * pc] = jnp.dot(h, w_ref[p], preferred_element_type=F32)

    return pl.pallas_call(
        body, name="mlp_up", grid=(t // tm,),
        in_specs=[pl.BlockSpec((tm, D_MODEL), lambda i: (i, 0)),
                  pl.BlockSpec((N_DEV, D_MODEL, pc), lambda i: (0, 0, 0))],
        out_specs=pl.BlockSpec((tm, D_FF), lambda i: (i, 0)), out_shape=_sds((t, D_FF)),
        compiler_params=_params("parallel"),
    )(h2, w_up)


def _mlp_down_loss(up, w_down, x1, pw, target):
    t = up.shape[0]
    tm = TOKEN_BLOCK

    def body(up_ref, w_ref, x1_ref, pw_ref, tg_ref, dy_ref, dx2_ref, loss_ref, dpw_ref):
        i = pl.program_id(0)

        @pl.when(i == 0)
        def _():
            loss_ref[...] = jnp.zeros_like(loss_ref)
            dpw_ref[...] = jnp.zeros_like(dpw_ref)

        u = jnp.maximum(up_ref[...], 0.0)
        y = jnp.dot((u * u).astype(BF), w_ref[...], preferred_element_type=F32)
        r = lax.rsqrt(jnp.mean(y * y, axis=-1, keepdims=True) + EPS)
        yh = y * r
        pw = pw_ref[...]
        err = x1_ref[...] + yh * pw - tg_ref[...]
        part = jnp.sum(jnp.sum(err * err, axis=-1, keepdims=True), axis=0, keepdims=True) * (0.5 / D_MODEL)
        loss_ref[...] += jnp.broadcast_to(part, loss_ref.shape)
        dx2 = err * (1.0 / D_MODEL)
        dx2_ref[...] = dx2
        dpw_ref[...] += jnp.sum(dx2 * yh, axis=0, keepdims=True)
        dyh = dx2 * pw
        dy_ref[...] = (r * (dyh - yh * jnp.mean(dyh * yh, axis=-1, keepdims=True))).astype(BF)

    tok = lambda w: pl.BlockSpec((tm, w), lambda i: (i, 0))
    vec = lambda w: pl.BlockSpec((1, w), lambda i: (0, 0))
    return pl.pallas_call(
        body, name="mlp_down_loss", grid=(t // tm,),
        in_specs=[tok(D_FF), pl.BlockSpec((D_FF, D_MODEL), lambda i: (0, 0)), tok(D_MODEL), vec(D_MODEL), tok(D_MODEL)],
        out_specs=[tok(D_MODEL), tok(D_MODEL), vec(LANES), vec(D_MODEL)],
        out_shape=[_sds((t, D_MODEL), BF), _sds((t, D_MODEL)), _sds((1, LANES)), _sds((1, D_MODEL))],
        compiler_params=_params("arbitrary"),
    )(up, w_down, x1, pw, target)


def _mlp_bwd_act(dy, w_down, up):
    t = dy.shape[0]
    tm = TOKEN_BLOCK

    def body(dy_ref, w_ref, up_ref, dup_ref):
        da = lax.dot_general(dy_ref[...], w_ref[...], (((1,), (1,)), ((), ())), preferred_element_type=F32)
        dup_ref[...] = (da * (2.0 * jnp.maximum(up_ref[...], 0.0))).astype(BF)

    return pl.pallas_call(
        body, name="mlp_bwd_act", grid=(t // tm,),
        in_specs=[pl.BlockSpec((tm, D_MODEL), lambda i: (i, 0)), pl.BlockSpec((D_FF, D_MODEL), lambda i: (0, 0)),
                  pl.BlockSpec((tm, D_FF), lambda i: (i, 0))],
        out_specs=pl.BlockSpec((tm, D_FF), lambda i: (i, 0)), out_shape=_sds((t, D_FF), BF),
        compiler_params=_params("parallel"),
    )(dy, w_down, up)


def _mlp_bwd_in(dup, w_up, x1, plw, dx2, mixed, pmw):
    t = dup.shape[0]
    tm = TOKEN_BLOCK

    def body(dup_ref, w_ref, x1_ref, plw_ref, dx2_ref, mx_ref, pmw_ref, dx1_ref, dmixed_ref, dplw_ref, dpmw_ref):
        i = pl.program_id(0)

        @pl.when(i == 0)
        def _():
            dplw_ref[...] = jnp.zeros_like(dplw_ref)
            dpmw_ref[...] = jnp.zeros_like(dpmw_ref)

        pc = D_FF // N_DEV
        dh = _mm_nt(dup_ref[:, 0:pc], w_ref[0])
        for p in range(1, N_DEV):
            dh = dh + _mm_nt(dup_ref[:, p * pc:(p + 1) * pc], w_ref[p])
        x1 = x1_ref[...]
        r = lax.rsqrt(jnp.mean(x1 * x1, axis=-1, keepdims=True) + EPS)
        xh = x1 * r
        dplw_ref[...] += jnp.sum(dh * xh, axis=0, keepdims=True)
        dxh = dh * plw_ref[...]
        dx1 = dx2_ref[...] + r * (dxh - xh * jnp.mean(dxh * xh, axis=-1, keepdims=True))
        dx1_ref[...] = dx1
        mx = mx_ref[...]
        r2 = lax.rsqrt(jnp.mean(mx * mx, axis=-1, keepdims=True) + EPS)
        mh = mx * r2
        dpmw_ref[...] += jnp.sum(dx1 * mh, axis=0, keepdims=True)
        dmh = dx1 * pmw_ref[...]
        dmixed_ref[...] = (r2 * (dmh - mh * jnp.mean(dmh * mh, axis=-1, keepdims=True))).astype(BF)

    tok = lambda w: pl.BlockSpec((tm, w), lambda i: (i, 0))
    vec = lambda w: pl.BlockSpec((1, w), lambda i: (0, 0))
    return pl.pallas_call(
        body, name="mlp_bwd_in", grid=(t // tm,),
        in_specs=[tok(D_FF), pl.BlockSpec((N_DEV, D_MODEL, D_FF // N_DEV), lambda i: (0, 0, 0)), tok(D_MODEL),
                  vec(D_MODEL), tok(D_MODEL), tok(D_MODEL), vec(D_MODEL)],
        out_specs=[tok(D_MODEL), tok(D_MODEL), vec(D_MODEL), vec(D_MODEL)],
        out_shape=[_sds((t, D_MODEL)), _sds((t, D_MODEL), BF), _sds((1, D_MODEL)), _sds((1, D_MODEL))],
        compiler_params=_params("arbitrary"),
    )(dup, w_up, x1, plw, dx2, mixed, pmw)


def _wgrad(a, b, a_cols, split=1, a_fn=None, a_block0=0, name="wgrad"):
    t, b_cols = b.shape
    n_a = (a.shape[1] - a_block0 * a_cols) // a_cols if a_block0 else a.shape[1] // a_cols

    def body(a_ref, b_ref, o_ref):
        av = a_ref[...]
        if a_fn is not None:
            av = a_fn(av)
        o_ref[...] = _mm_tn(av, b_ref[...]).astype(BF).reshape(o_ref.shape)

    return pl.pallas_call(
        body, name=name, grid=(n_a,),
        in_specs=[pl.BlockSpec((t, a_cols), lambda i: (0, i + a_block0)), pl.BlockSpec((t, b_cols), lambda i: (0, 0))],
        out_specs=pl.BlockSpec((split, a_cols // split, b_cols), lambda i: (i, 0, 0)),
        out_shape=_sds((n_a * split, a_cols // split, b_cols), BF),
        compiler_params=_params("parallel"),
    )(a, b)


def _wgrad_pre_t(at, b, b_cols, name):
    rows, t = at.shape
    n_b = b.shape[1] // b_cols

    def body(a_ref, b_ref, o_ref):
        o_ref[0] = jnp.dot(a_ref[...], b_ref[...], preferred_element_type=F32).astype(BF)

    return pl.pallas_call(
        body, name=name, grid=(n_b,),
        in_specs=[pl.BlockSpec((rows, t), lambda j: (0, 0)), pl.BlockSpec((t, b_cols), lambda j: (0, j))],
        out_specs=pl.BlockSpec((1, rows, b_cols), lambda j: (j, 0, 0)), out_shape=_sds((n_b, rows, b_cols), BF),
        compiler_params=_params("parallel"),
    )(at, b)


def _select_matrix(rows, fn):
    r = _iota((rows, LANES), 0)
    c = _iota((rows, LANES), 1)
    return (r == fn(c)).astype(F32)


def _small_bwd(proj, fb, al, dtb, dcq, dckt, dbe, dge):
    t = proj.shape[0]

    def body(sm_ref, fb_ref, al_ref, dtb_ref, dcq_ref, dckt_ref, dbe_ref, dge_ref, dsm_ref, dvec_ref):
        s = sm_ref[...]
        lane = _iota((1, LANES), 1)
        sel_f = _select_matrix(512, lambda c: jnp.where(c < 8, FOX_HEAD_DIM * c, -1))
        sel_k = _select_matrix(LANES, lambda c: jnp.where(c < 8, 32 * (c // 2) + c % 2, -1))
        dcum = _mm_exact(dcq_ref[...], sel_f) - _mm_exact(dckt_ref[...].T, sel_k)
        row = _iota((t, LANES), 0)
        step = 1
        while step < t:
            dcum = dcum + _shift_up(dcum, step, row)
            step *= 2
        dff = dcum * _sigmoid(-(s + fb_ref[...]))
        sel_b = _select_matrix(512, lambda c: jnp.where((c >= SM_GB) & (c < SM_GA), LANES * (c - SM_GB), -1))
        sel_g = _select_matrix(512, lambda c: jnp.where((c >= SM_GA) & (c < SM_GA + 4), LANES * (c - SM_GA), -1))
        beta = _sigmoid(s)
        dgb = _mm_exact(dbe_ref[...], sel_b) * beta * (1.0 - beta)
        dg = _mm_exact(dge_ref[...], sel_g)
        za = s + dtb_ref[...]
        nea = -jnp.exp(al_ref[...])
        dga = dg * nea * _sigmoid(za)
        is_f = lane < SM_GB
        is_b = (lane >= SM_GB) & (lane < SM_GA)
        is_a = (lane >= SM_GA) & (lane < SM_GA + 4)
        dsm_ref[...] = jnp.where(is_f, dff, jnp.where(is_b, dgb, jnp.where(is_a, dga, 0.0)))
        dvec_ref[...] = jnp.zeros_like(dvec_ref)
        dvec_ref[0:1, :] = jnp.sum(jnp.where(is_f, dff, 0.0), axis=0, keepdims=True)
        dvec_ref[1:2, :] = jnp.sum(jnp.where(is_a, dg * nea * _softplus(za), 0.0), axis=0, keepdims=True)
        dvec_ref[2:3, :] = jnp.sum(jnp.where(is_a, dga, 0.0), axis=0, keepdims=True)

    vec = pl.BlockSpec((1, LANES), lambda i: (0, 0))
    full = lambda r, c: pl.BlockSpec((r, c), lambda i: (0, 0))
    return pl.pallas_call(
        body, name="small_bwd", grid=(1,),
        in_specs=[pl.BlockSpec((t, LANES), lambda i: (0, COL_SMALL // LANES)), vec, vec, vec, full(t, 512),
                  full(LANES, t), full(t, 512), full(t, 512)],
        out_specs=[full(t, LANES), full(8, LANES)], out_shape=[_sds((t, LANES)), _sds((8, LANES))],
        compiler_params=_params("arbitrary"),
    )(proj, fb, al, dtb, dcq, dckt, dbe, dge)


def _in_bwd(dfox, dgdn, dgz, dsm, wt_al, x, nw, dx1):
    t = x.shape[0]
    tm = TOKEN_BLOCK

    def body(*refs):
        parts, (w_ref, x_ref, nw_ref, dx1_ref, dp_ref, dx_ref, dnw_ref) = refs[:8], refs[8:]
        i = pl.program_id(0)

        @pl.when(i == 0)
        def _():
            dnw_ref[...] = jnp.zeros_like(dnw_ref)

        col = 0
        for part in parts:
            width = part.shape[1]
            dp_ref[:, col:col + width] = part[...].astype(BF)
            col += width
        dh = jnp.dot(dp_ref[...], w_ref[...], preferred_element_type=F32)
        xv = x_ref[...]
        r = lax.rsqrt(jnp.mean(xv * xv, axis=-1, keepdims=True) + EPS)
        xh = xv * r
        dnw_ref[...] += jnp.sum(dh * xh, axis=0, keepdims=True)
        dxh = dh * nw_ref[...]
        dx_ref[...] = dx1_ref[...] + r * (dxh - xh * jnp.mean(dxh * xh, axis=-1, keepdims=True))

    tok = lambda w: pl.BlockSpec((tm, w), lambda i: (i, 0))
    vec = lambda w: pl.BlockSpec((1, w), lambda i: (0, 0))
    return pl.pallas_call(
        body, name="in_bwd", grid=(t // tm,),
        in_specs=[tok(D_FOX)] * 3 + [tok(D_GDN)] * 4 + [tok(LANES), pl.BlockSpec((PROJ_W, D_MODEL), lambda i: (0, 0)),
                                                        tok(D_MODEL), vec(D_MODEL), tok(D_MODEL)],
        out_specs=[tok(PROJ_W), tok(D_MODEL), vec(D_MODEL)],
        out_shape=[_sds((t, PROJ_W), BF), _sds((t, D_MODEL)), _sds((1, D_MODEL))],
        compiler_params=_params("arbitrary"),
    )(*dfox, *dgdn, dgz, dsm, wt_al, x, nw, dx1)


def _row(v, width=None):
    v = v.reshape(1, -1).astype(F32)
    if width is not None and v.shape[1] < width:
        v = jnp.pad(v, ((0, 0), (0, width - v.shape[1])))
    return v


def _lane_vec(v, first):
    return jnp.zeros((1, LANES), F32).at[0, first:first + v.shape[0]].set(v.astype(F32))


def _local_step(x, target, wt_al, w_out, mlp_weights, on_mlp_grads, convw, pre_mix_norm, fox_f_bias, fox_out_norm,
                gdn_a_log, gdn_dt_bias, gdn_out_norm, post_mix_norm, pre_mlp_norm, post_mlp_norm):
    t = x.shape[0]
    nch = t // CHUNK
    nw, pmw, plw, pw = _row(pre_mix_norm), _row(post_mix_norm), _row(pre_mlp_norm), _row(post_mlp_norm)
    fb, al, dtb = _lane_vec(fox_f_bias, SM_FF), _lane_vec(gdn_a_log, SM_GA), _lane_vec(gdn_dt_bias, SM_GA)
    fnw = _row(jnp.tile(fox_out_norm, 2))
    gnw = _row(gdn_out_norm)

    proj, h = _norm_proj(x, nw, wt_al)
    ce, cumt, be, ge = _small_prep(proj, fb, al, dtb)
    o_fox, lse, fox_n = _fox_fwd(proj, ce, cumt, fnw)
    qn, kn, cv, gc, mmat, amat = _gdn_prep(proj, convw, be, ge)
    n_prob = N_GDN_HEADS * nch
    m2 = mmat.reshape(n_prob, CHUNK * CHUNK)
    if n_prob < LANES:
        m2 = jnp.pad(m2, ((0, LANES - n_prob), (0, 0)))
    tinv = _tri_inverse(m2)[:n_prob].reshape(N_GDN_HEADS, nch, CHUNK, CHUNK)
    gdn_o, s_all, vn_all = _gdn_scan(qn, kn, cv, be, gc, tinv, amat)
    x1, h2, mixed, omix, h2t = _mix_out(fox_n, gdn_o, proj, gnw, w_out, x, pmw, plw)
    w_up, w_down = mlp_weights(h2)
    up = _mlp_up(h2, w_up)
    dy, dx2, loss, d_pw = _mlp_down_loss(up, w_down, x1, pw, target)

    dup = _mlp_bwd_act(dy, w_down, up)
    relu2 = lambda u: jnp.square(jnp.maximum(u, 0.0))
    g_down = _wgrad(up, dy, D_FF // N_DEV, a_fn=relu2, name="wgrad_down")
    g_up = _wgrad_pre_t(h2t, dup, D_FF // N_DEV, name="wgrad_up")
    mlp_handle, token = on_mlp_grads(g_up, g_down)
    dx1, dmixed, d_plw, d_pmw = _mlp_bwd_in(dup, w_up, x1, plw + token[0:1, 0:1], dx2, mixed, pmw)
    g_out = _wgrad(omix, dmixed, 512, split=4, name="wgrad_out")
    do_fox, dgo, dgz, d_fnw, d_gnw = _out_bwd(dmixed, w_out, o_fox, gdn_o, proj, fnw, gnw)
    dfq, dfk, dfv, dcq, dckt = _fox_bwd(proj, ce, cumt, lse, o_fox, do_fox)
    dqn, dkn, dcv, dbe, dge = _gdn_bwd(qn, kn, cv, be, gc, tinv, amat, s_all, vn_all, dgo)
    dxq, dxk, dxv, dwq, dwk, dwv = _gdn_bwd_conv(proj, convw, dqn, dkn, dcv)
    dsm, dvec = _small_bwd(proj, fb, al, dtb, dcq, dckt, dbe, dge)
    dproj, grad_x, d_nw = _in_bwd((dfq, dfk, dfv), (dxq, dxk, dxv), dgz, dsm, wt_al, x, nw, dx1)
    g_main = _wgrad(dproj, h, 512, name="wgrad_in")
    g_tail = _wgrad(dproj, h, LANES, a_block0=COL_SMALL // LANES, name="wgrad_in_small")
    g_in = jnp.concatenate([g_main.reshape(COL_SMALL, D_MODEL), g_tail[0]])
    small = dict(pre_mix_norm=d_nw[0], fox_f_bias=dvec[0, SM_FF:SM_FF + N_FOX_HEADS],
                 fox_out_norm=d_fnw[0, :FOX_HEAD_DIM], gdn_conv_w=(dwq, dwk, dwv),
                 gdn_a_log=dvec[1, SM_GA:SM_GA + N_GDN_HEADS], gdn_dt_bias=dvec[2, SM_GA:SM_GA + N_GDN_HEADS],
                 gdn_out_norm=d_gnw[0], post_mix_norm=d_pmw[0], pre_mlp_norm=d_plw[0], post_mlp_norm=d_pw[0])
    return loss[0, 0], grad_x, g_in, g_out, mlp_handle, small


MESH_IDS = pl.DeviceIdType.MESH
CHIP_FLIPS = ((0, 0), (1, 0), (0, 1), (1, 1))
ANY_SPEC = pl.BlockSpec(memory_space=pl.ANY)


def _place():
    return lax.axis_index("x"), lax.axis_index("y"), lax.axis_index("c")


def _all_gather(blocks):
    n = len(blocks)

    def body(*refs):
        ins, outs, (send_sems, recv_sems, local_sems) = refs[:n], refs[n:2 * n], refs[2 * n:]
        x, y, c = _place()
        sibling = (x, y, 1 - c)
        chips = [(x ^ fx, y ^ fy) for fx, fy in CHIP_FLIPS[1:]]

        def slot(out, px, py, pc):
            return out.at[4 * px + 2 * py + pc]

        def copy(a, k, block, to, src=None):
            return pltpu.make_async_remote_copy(
                src_ref=slot(outs[a], *block) if src is None else src, dst_ref=slot(outs[a], *block),
                send_sem=send_sems.at[a, k], recv_sem=recv_sems.at[a, k], device_id=to, device_id_type=MESH_IDS)

        pending = []
        for a in range(n):
            mine = pltpu.make_async_copy(ins[a], slot(outs[a], x, y, c), local_sems.at[a])
            mine.start()
            pending.append(mine)
        sends = []
        for a in range(n):
            first = [copy(a, 0, (x, y, c), sibling, src=ins[a])]
            first += [copy(a, 1 + j, (x, y, c), (*chip, c), src=ins[a]) for j, chip in enumerate(chips)]
            for cp in first:
                cp.start()
            sends += first
        for a in range(n):
            for j, chip in enumerate(chips):
                copy(a, 1 + j, (*chip, c), (x, y, c)).wait_recv()
                fwd = copy(a, 4 + j, (*chip, c), sibling)
                fwd.start()
                sends.append(fwd)
        for a in range(n):
            copy(a, 0, sibling, (x, y, c)).wait_recv()
            for j, chip in enumerate(chips):
                copy(a, 4 + j, (*chip, 1 - c), (x, y, c)).wait_recv()
        for cp in sends:
            cp.wait_send()
        for cp in pending:
            cp.wait()

    return pl.pallas_call(
        body, name="all_gather_weights", in_specs=[ANY_SPEC] * n, out_specs=[ANY_SPEC] * n,
        out_shape=[_sds((N_DEV,) + b.shape, b.dtype) for b in blocks],
        scratch_shapes=[pltpu.SemaphoreType.DMA((n, 7)), pltpu.SemaphoreType.DMA((n, 7)), pltpu.SemaphoreType.DMA((n,))],
        compiler_params=pltpu.CompilerParams(has_side_effects=True),
    )(*blocks)


def _adamw(w, g, m, v):
    m = ADAM_B1 * m + (1.0 - ADAM_B1) * g
    v = ADAM_B2 * v + (1.0 - ADAM_B2) * (g * g)
    m_hat = m / (1.0 - ADAM_B1 ** ADAM_STEP)
    v_hat = v / (1.0 - ADAM_B2 ** ADAM_STEP)
    return -ADAM_LR * (m_hat / (jnp.sqrt(v_hat) + ADAM_EPS) + ADAM_WD * w), m, v


def _reduce_scatter_adamw(g, w, m, v, name):
    _, r, c_ = g.shape
    rb = min(r, 256)

    def body(g_ref, w_ref, m_ref, v_ref, grad_ref, delta_ref, nm_ref, nv_ref,
             sib_buf, out_buf, ici_buf, send_a, recv_a, send_b, recv_b):
        x, y, c = _place()
        sibling = (x, y, 1 - c)
        chips = [(x ^ fx, y ^ fy) for fx, fy in CHIP_FLIPS]

        def piece(chip, core):
            return g_ref.at[4 * chip[0] + 2 * chip[1] + core]

        to_sibling = [pltpu.make_async_remote_copy(
            src_ref=piece(chip, 1 - c), dst_ref=sib_buf.at[j], send_sem=send_a.at[j], recv_sem=recv_a.at[j],
            device_id=sibling, device_id_type=MESH_IDS) for j, chip in enumerate(chips)]
        for cp in to_sibling:
            cp.start()
        to_owner = []
        for j in (1, 2, 3):
            to_sibling[j].wait_recv()
            own = piece(chips[j], c)
            for r0 in range(0, r, rb):
                rows = slice(r0, r0 + rb)
                out_buf[j, rows, :] = (own[rows, :].astype(F32) + sib_buf[j, rows, :].astype(F32)).astype(BF)
            cp = pltpu.make_async_remote_copy(
                src_ref=out_buf.at[j], dst_ref=ici_buf.at[j], send_sem=send_b.at[j], recv_sem=recv_b.at[j],
                device_id=(*chips[j], c), device_id_type=MESH_IDS)
            cp.start()
            to_owner.append(cp)
        to_sibling[0].wait_recv()
        for cp in to_owner:
            cp.wait_recv()
        own = piece(chips[0], c)
        for r0 in range(0, r, rb):
            rows = slice(r0, r0 + rb)
            total = own[rows, :].astype(F32) + sib_buf[0, rows, :].astype(F32)
            for j in (1, 2, 3):
                total = total + ici_buf[j, rows, :].astype(F32)
            grad_ref[rows, :] = total
            delta, nm, nv = _adamw(w_ref[rows, :], total, m_ref[rows, :], v_ref[rows, :])
            delta_ref[rows, :] = delta
            nm_ref[rows, :] = nm
            nv_ref[rows, :] = nv
        for cp in to_sibling + to_owner:
            cp.wait_send()

    return pl.pallas_call(
        body, name=name, in_specs=[VMEM_SPEC] * 4, out_specs=[VMEM_SPEC] * 4, out_shape=[_sds((r, c_))] * 4,
        scratch_shapes=[pltpu.VMEM((4, r, c_), BF), pltpu.VMEM((4, r, c_), BF), pltpu.VMEM((4, r, c_), BF),
                        pltpu.SemaphoreType.DMA((4,)), pltpu.SemaphoreType.DMA((4,)), pltpu.SemaphoreType.DMA((4,)),
                        pltpu.SemaphoreType.DMA((4,))],
        compiler_params=pltpu.CompilerParams(vmem_limit_bytes=VMEM_LIMIT, has_side_effects=True),
    )(g, w, m, v)


HBM_SPEC = pl.BlockSpec(memory_space=pltpu.HBM)
SEM_SPEC = pl.BlockSpec(memory_space=pltpu.SEMAPHORE)
DATAFLOW = pltpu.SideEffectType.DATAFLOW_SIDE_EFFECTING


def _peers():
    x, y, c = _place()
    return 4 * x + 2 * y + c, [(x ^ (k >> 2), y ^ ((k >> 1) & 1), c ^ (k & 1)) for k in range(1, N_DEV)]


def _peer_index(peer):
    return 4 * peer[0] + 2 * peer[1] + peer[2]


def _zones_with_own(srcs, pieces, name, after=None):
    n = len(srcs)
    extra = [] if after is None else [after]

    def body(*refs):
        ins, outs, sems = refs[:n], refs[-n - 1:-1], refs[-1]
        me, _ = _peers()
        copies = [pltpu.make_async_copy(ins[a].at[me] if pieces else ins[a], outs[a].at[me], sems.at[a]) for a in range(n)]
        for cp in copies:
            cp.start()
        for cp in copies:
            cp.wait()

    shapes = [s_.shape[1:] if pieces else s_.shape for s_ in srcs]
    return pl.pallas_call(
        body, name=name, in_specs=[ANY_SPEC] * (n + len(extra)), out_specs=[ANY_SPEC] * n,
        out_shape=[_sds((N_DEV,) + sh, s_.dtype) for sh, s_ in zip(shapes, srcs)],
        scratch_shapes=[pltpu.SemaphoreType.DMA((n,))],
    )(*srcs, *extra)


def _exchange_start(srcs, zones, pieces, name):
    n = len(srcs)

    def body(*refs):
        ins, zs = refs[:n], refs[n:2 * n]
        sems = refs[2 * n:4 * n]
        token = refs[-1]
        me, peers = _peers()
        for peer in peers:
            for a in range(n):
                pltpu.make_async_remote_copy(
                    src_ref=ins[a].at[_peer_index(peer)] if pieces else ins[a], dst_ref=zs[a].at[me],
                    send_sem=sems[2 * a], recv_sem=sems[2 * a + 1], device_id=peer, device_id_type=MESH_IDS).start()
        token[...] = jnp.zeros_like(token)

    hbm = lambda v: pltpu.with_memory_space_constraint(v, pltpu.HBM)
    out = pl.pallas_call(
        body, name=name,
        out_shape=tuple([pltpu.SemaphoreType.DMA(())] * (2 * n) + [pltpu.HBM(v.shape, v.dtype) for v in srcs]
                        + [pltpu.HBM(z.shape, z.dtype) for z in zones] + [_sds((8, LANES))]),
        in_specs=[HBM_SPEC] * (2 * n), out_specs=tuple([SEM_SPEC] * (2 * n) + [HBM_SPEC] * (2 * n) + [VMEM_SPEC]),
        input_output_aliases={i: 2 * n + i for i in range(2 * n)},
        compiler_params=pltpu.CompilerParams(has_side_effects=DATAFLOW),
    )(*[hbm(v) for v in srcs], *[hbm(z) for z in zones])
    return out[:2 * n], out[2 * n:3 * n], out[3 * n:4 * n], out[-1]


def _exchange_wait(sems, srcs, zones, after, name):
    n = len(srcs)

    def body(*refs):
        ins, zs, sm = refs[:n], refs[n:2 * n], refs[2 * n:4 * n]
        me, peers = _peers()
        for a in range(n):
            seven = zs[a].at[pl.ds(0, N_DEV - 1)]
            cp = pltpu.make_async_remote_copy(src_ref=seven, dst_ref=seven, send_sem=sm[2 * a], recv_sem=sm[2 * a + 1],
                                              device_id=peers[0], device_id_type=MESH_IDS)
            cp.wait_send()
            cp.wait_recv()

    out = pl.pallas_call(
        body, name=name, out_shape=tuple([pltpu.HBM(v.shape, v.dtype) for v in srcs] + [pltpu.HBM(z.shape, z.dtype) for z in zones]),
        in_specs=[HBM_SPEC] * (2 * n) + [SEM_SPEC] * (2 * n) + [ANY_SPEC], out_specs=tuple([HBM_SPEC] * (2 * n)),
        input_output_aliases={i: i for i in range(2 * n)},
        compiler_params=pltpu.CompilerParams(has_side_effects=DATAFLOW),
    )(*srcs, *zones, *sems, after)
    return out[n:]


def _sum_adamw(zone, w, m, v, name):
    _, r, c_ = zone.shape
    rb = min(r, 128)

    def body(z_ref, w_ref, m_ref, v_ref, grad_ref, delta_ref, nm_ref, nv_ref):
        total = z_ref[0].astype(F32)
        for d in range(1, N_DEV):
            total = total + z_ref[d].astype(F32)
        grad_ref[...] = total
        delta_ref[...], nm_ref[...], nv_ref[...] = _adamw(w_ref[...], total, m_ref[...], v_ref[...])

    blk = pl.BlockSpec((rb, c_), lambda i: (i, 0))
    return pl.pallas_call(
        body, name=name, grid=(r // rb,), in_specs=[pl.BlockSpec((N_DEV, rb, c_), lambda i: (0, i, 0)), blk, blk, blk],
        out_specs=[blk] * 4, out_shape=[_sds((r, c_))] * 4, compiler_params=_params("parallel"),
    )(zone, w, m, v)


SMALL_ROWS = 16


def _all_reduce_small(packed):
    def body(p_ref, sum_ref, gath, send_sems, recv_sems):
        x, y, c = _place()
        me = 4 * x + 2 * y + c
        gath[me] = p_ref[...]
        copies = []
        for k in range(1, N_DEV):
            peer = (x ^ (k >> 2), y ^ ((k >> 1) & 1), c ^ (k & 1))
            cp = pltpu.make_async_remote_copy(
                src_ref=p_ref, dst_ref=gath.at[me], send_sem=send_sems.at[k], recv_sem=recv_sems.at[k],
                device_id=peer, device_id_type=MESH_IDS)
            cp.start()
            copies.append(cp)
        for cp in copies:
            cp.wait_recv()
        total = gath[0]
        for d in range(1, N_DEV):
            total = total + gath[d]
        sum_ref[...] = total
        for cp in copies:
            cp.wait_send()

    return pl.pallas_call(
        body, name="all_reduce_small", in_specs=[VMEM_SPEC], out_specs=VMEM_SPEC, out_shape=_sds(packed.shape),
        scratch_shapes=[pltpu.VMEM((N_DEV,) + packed.shape, F32), pltpu.SemaphoreType.DMA((N_DEV,)),
                        pltpu.SemaphoreType.DMA((N_DEV,))],
        compiler_params=pltpu.CompilerParams(has_side_effects=True),
    )(packed)


def _adamw_small(w, g, m, v):
    def body(w_ref, g_ref, m_ref, v_ref, delta_ref, nm_ref, nv_ref):
        delta_ref[...], nm_ref[...], nv_ref[...] = _adamw(w_ref[...], g_ref[...], m_ref[...], v_ref[...])

    return pl.pallas_call(body, name="adamw_small", in_specs=[VMEM_SPEC] * 4, out_specs=[VMEM_SPEC] * 3,
                          out_shape=[_sds(w.shape)] * 3)(w, g, m, v)


NATIVE_ROWS = ((0, 1536), (1544, 3080), (3088, 3600), (1536, 1544), (3080, 3088))


def _to_aligned_rows(wt_native):
    pad = jnp.zeros((PROJ_W - D_PROJ, wt_native.shape[1]), wt_native.dtype)
    return jnp.concatenate([wt_native[lo:hi] for lo, hi in NATIVE_ROWS] + [pad])


def _from_aligned_rows(gt_al):
    return jnp.concatenate([gt_al[0:1536], gt_al[3584:3592], gt_al[1536:3072], gt_al[3592:3600], gt_al[3072:3584]])


def _cols_from_pieces(p):
    return p.transpose(1, 0, 2).reshape(p.shape[1], -1)


SMALL_NORMS = ("pre_mix_norm", "post_mix_norm", "pre_mlp_norm", "post_mlp_norm")
SMALL_MISC = (("fox_out_norm", FOX_HEAD_DIM), ("gdn_out_norm", GDN_HEAD_DIM), ("fox_f_bias", N_FOX_HEADS),
              ("gdn_a_log", N_GDN_HEADS), ("gdn_dt_bias", N_GDN_HEADS))


def _pack_small(vals, conv):
    misc = jnp.concatenate([vals[n].astype(F32) for n, _ in SMALL_MISC])
    rows = [vals[n].astype(F32) for n in SMALL_NORMS] + [jnp.pad(misc, (0, D_MODEL - misc.shape[0]))]
    flat = conv.astype(F32).reshape(-1)
    n_rows = -(-flat.shape[0] // D_MODEL)
    flat = jnp.pad(flat, (0, n_rows * D_MODEL - flat.shape[0])).reshape(n_rows, D_MODEL)
    packed = jnp.concatenate([jnp.stack(rows), flat])
    return jnp.pad(packed, ((0, SMALL_ROWS - packed.shape[0]), (0, 0)))


def _unpack_small(packed, conv_shape):
    out = {n: packed[i] for i, n in enumerate(SMALL_NORMS)}
    off = 0
    for n, size in SMALL_MISC:
        out[n] = packed[4, off:off + size]
        off += size
    n_conv = conv_shape[0] * conv_shape[1]
    out["gdn_conv_w"] = packed[5:].reshape(-1)[:n_conv].reshape(conv_shape)
    return out


WEIGHT_ORDER = ("pre_mix_norm", "w_in", "fox_f_bias", "fox_out_norm", "gdn_conv_w", "gdn_a_log", "gdn_dt_bias",
                "gdn_out_norm", "w_out", "post_mix_norm", "pre_mlp_norm", "w_up", "w_down", "post_mlp_norm")


def kernel(x, pre_mix_norm, w_in, fox_f_bias, fox_out_norm, gdn_conv_w, gdn_a_log, gdn_dt_bias, gdn_out_norm, w_out, post_mix_norm, pre_mlp_norm, w_up, w_down, post_mlp_norm, loss_target, m_pre_mix_norm, m_w_in, m_fox_f_bias, m_fox_out_norm, m_gdn_conv_w, m_gdn_a_log, m_gdn_dt_bias, m_gdn_out_norm, m_w_out, m_post_mix_norm, m_pre_mlp_norm, m_w_up, m_w_down, m_post_mlp_norm, v_pre_mix_norm, v_w_in, v_fox_f_bias, v_fox_out_norm, v_gdn_conv_w, v_gdn_a_log, v_gdn_dt_bias, v_gdn_out_norm, v_w_out, v_post_mix_norm, v_pre_mlp_norm, v_w_up, v_w_down, v_post_mlp_norm):
    w = dict(pre_mix_norm=pre_mix_norm, w_in=w_in, fox_f_bias=fox_f_bias, fox_out_norm=fox_out_norm,
             gdn_conv_w=gdn_conv_w, gdn_a_log=gdn_a_log, gdn_dt_bias=gdn_dt_bias, gdn_out_norm=gdn_out_norm, w_out=w_out,
             post_mix_norm=post_mix_norm, pre_mlp_norm=pre_mlp_norm, w_up=w_up, w_down=w_down, post_mlp_norm=post_mlp_norm)
    mom = dict(pre_mix_norm=m_pre_mix_norm, w_in=m_w_in, fox_f_bias=m_fox_f_bias, fox_out_norm=m_fox_out_norm,
               gdn_conv_w=m_gdn_conv_w, gdn_a_log=m_gdn_a_log, gdn_dt_bias=m_gdn_dt_bias, gdn_out_norm=m_gdn_out_norm,
               w_out=m_w_out, post_mix_norm=m_post_mix_norm, pre_mlp_norm=m_pre_mlp_norm, w_up=m_w_up, w_down=m_w_down,
               post_mlp_norm=m_post_mlp_norm)
    var = dict(pre_mix_norm=v_pre_mix_norm, w_in=v_w_in, fox_f_bias=v_fox_f_bias, fox_out_norm=v_fox_out_norm,
               gdn_conv_w=v_gdn_conv_w, gdn_a_log=v_gdn_a_log, gdn_dt_bias=v_gdn_dt_bias, gdn_out_norm=v_gdn_out_norm,
               w_out=v_w_out, post_mix_norm=v_post_mix_norm, pre_mlp_norm=v_pre_mlp_norm, w_up=v_w_up, w_down=v_w_down,
               post_mlp_norm=v_post_mlp_norm)

    win_g, wout_g, conv_g = _all_gather([w_in.T.astype(BF), w_out.astype(BF), gdn_conv_w])
    mlp_shards = [w_up.astype(BF), w_down.astype(BF)]
    ag_sems, mlp_shards, mlp_zones, ag_token = _exchange_start(
        mlp_shards, _zones_with_own(mlp_shards, False, "gather_mlp_own", after=win_g), False, "gather_mlp_start")
    wt_al = _to_aligned_rows(win_g.reshape(D_PROJ, D_MODEL))
    w_out_full = wout_g.reshape(D_MODEL, D_MODEL)
    convw = _cols_from_pieces(conv_g)

    def mlp_weights(after):
        up_g, down_g = _exchange_wait(ag_sems, mlp_shards, mlp_zones, after, "gather_mlp_wait")
        return up_g, down_g.reshape(D_FF, D_MODEL)

    def on_mlp_grads(g_up, g_down):
        srcs = [g_up, g_down]
        sems, srcs, zones, token = _exchange_start(srcs, _zones_with_own(srcs, True, "scatter_mlp_own"), True,
                                                   "scatter_mlp_start")
        return (sems, srcs, zones), token

    loss, grad_x, g_in, g_out, mlp_handle, small = _local_step(
        x[0], loss_target[0], wt_al, w_out_full, mlp_weights, on_mlp_grads, convw, pre_mix_norm + ag_token[0, 0],
        fox_f_bias, fox_out_norm, gdn_a_log, gdn_dt_bias, gdn_out_norm, post_mix_norm, pre_mlp_norm, post_mlp_norm)
    loss = lax.psum(loss, ("x", "y", "c"))

    grads, delta, new_m, new_v = {}, {}, {}, {}
    big = dict(w_out=g_out, w_in=_from_aligned_rows(g_in).reshape(N_DEV, D_PROJ // N_DEV, D_MODEL))
    for n, g in big.items():
        if n == "w_in":
            res = _reduce_scatter_adamw(g, w[n].T, mom[n].T, var[n].T, "reduce_scatter_" + n)
            grads[n], delta[n], new_m[n], new_v[n] = [r.T for r in res]
        else:
            grads[n], delta[n], new_m[n], new_v[n] = _reduce_scatter_adamw(g, w[n], mom[n], var[n], "reduce_scatter_" + n)
    zone_up, zone_down = _exchange_wait(*mlp_handle, grads["w_in"], "scatter_mlp_wait")
    for n, zone in (("w_up", zone_up), ("w_down", zone_down)):
        grads[n], delta[n], new_m[n], new_v[n] = _sum_adamw(zone, w[n], mom[n], var[n], "adamw_" + n)

    dwq, dwk, dwv = small.pop("gdn_conv_w")
    total = _unpack_small(_all_reduce_small(_pack_small(small, jnp.concatenate([dwq, dwk, dwv], axis=1))),
                          (CONV_K, 3 * D_GDN))
    me = 4 * lax.axis_index("x") + 2 * lax.axis_index("y") + lax.axis_index("c")
    n_conv = gdn_conv_w.shape[1]
    total["gdn_conv_w"] = lax.dynamic_slice_in_dim(total["gdn_conv_w"], me * n_conv, n_conv, axis=1)
    grads.update(total)
    d_s, m_s, v_s = _adamw_small(_pack_small(w, gdn_conv_w), _pack_small(total, total["gdn_conv_w"]),
                                 _pack_small(mom, m_gdn_conv_w), _pack_small(var, v_gdn_conv_w))
    delta.update(_unpack_small(d_s, gdn_conv_w.shape))
    new_m.update(_unpack_small(m_s, gdn_conv_w.shape))
    new_v.update(_unpack_small(v_s, gdn_conv_w.shape))

    return (loss, grad_x[None], *[grads[n] for n in WEIGHT_ORDER], *[delta[n] for n in WEIGHT_ORDER],
            *[new_m[n] for n in WEIGHT_ORDER], *[new_v[n] for n in WEIGHT_ORDER])
```

```python
import jax
import jax.numpy as jnp
from jax import lax
from jax.experimental import pallas as pl
from jax.experimental.pallas import tpu as pltpu

F32 = jnp.float32
BF = jnp.bfloat16

D_MODEL = 1024
N_FOX_HEADS, FOX_HEAD_DIM = 8, 64
N_GDN_HEADS, GDN_HEAD_DIM = 4, 128
D_FOX = N_FOX_HEADS * FOX_HEAD_DIM
D_GDN = N_GDN_HEADS * GDN_HEAD_DIM
CHUNK = 64
CONV_K = 4
D_FF = 4 * D_MODEL
EPS = 1e-6
D_PROJ = 3600
N_DEV = 8

PROJ_W = 3712
COL_FOX, COL_GDN, COL_GZ, COL_SMALL = 0, 1536, 3072, 3584
LANES = 128
SM_FF, SM_GB, SM_GA = 0, 8, 12

ADAM_LR, ADAM_B1, ADAM_B2, ADAM_EPS, ADAM_WD, ADAM_STEP = 0.001, 0.9, 0.999, 1e-08, 0.01, 10

TOKEN_BLOCK = 256
FOX_SCALE = FOX_HEAD_DIM ** -0.5
GDN_QSCALE = GDN_HEAD_DIM ** -0.5
NEG_BIG = -1e30
VMEM_LIMIT = 56 * 1024 * 1024

VMEM_SPEC = pl.BlockSpec(memory_space=pltpu.VMEM)
HIGHEST = lax.Precision.HIGHEST


def _sds(shape, dtype=F32):
    return jax.ShapeDtypeStruct(shape, dtype)


def _params(*sem):
    return pltpu.CompilerParams(dimension_semantics=sem if sem else None, vmem_limit_bytes=VMEM_LIMIT)


def _mm(a, b):
    return jnp.dot(a.astype(BF), b.astype(BF), preferred_element_type=F32)


def _mm_nt(a, b):
    return lax.dot_general(a.astype(BF), b.astype(BF), (((1,), (1,)), ((), ())), preferred_element_type=F32)


def _mm_tn(a, b):
    return lax.dot_general(a.astype(BF), b.astype(BF), (((0,), (0,)), ((), ())), preferred_element_type=F32)


def _mm_exact(a, b):
    return jnp.dot(a, b, precision=HIGHEST, preferred_element_type=F32)


def _mm_tn_exact(a, b):
    return lax.dot_general(a, b, (((0,), (0,)), ((), ())), precision=HIGHEST, preferred_element_type=F32)


def _sigmoid(x):
    return 1.0 / (1.0 + jnp.exp(-x))


def _softplus(x):
    return jnp.maximum(x, 0.0) + jnp.log1p(jnp.exp(-jnp.abs(x)))


def _iota(shape, dim):
    return lax.broadcasted_iota(jnp.int32, shape, dim)


def _shift_down(x, s, row):
    return jnp.where(row >= s, pltpu.roll(x, s, 0), 0.0)


def _shift_up(x, s, row):
    n = x.shape[0]
    return jnp.where(row < n - s, pltpu.roll(x, n - s, 0), 0.0)


def _norm_proj(x, nw, wt_al):
    t = x.shape[0]

    def body(x_ref, nw_ref, w_ref, proj_ref, h_ref):
        xv = x_ref[...]
        r = lax.rsqrt(jnp.mean(xv * xv, axis=-1, keepdims=True) + EPS)
        h = (xv * r * nw_ref[...]).astype(BF)
        h_ref[...] = h
        proj_ref[...] = lax.dot_general(h, w_ref[...], (((1,), (1,)), ((), ())), preferred_element_type=F32)

    tm = TOKEN_BLOCK
    return pl.pallas_call(
        body, name="norm_proj", grid=(t // tm,),
        in_specs=[pl.BlockSpec((tm, D_MODEL), lambda i: (i, 0)), pl.BlockSpec((1, D_MODEL), lambda i: (0, 0)),
                  pl.BlockSpec((PROJ_W, D_MODEL), lambda i: (0, 0))],
        out_specs=[pl.BlockSpec((tm, PROJ_W), lambda i: (i, 0)), pl.BlockSpec((tm, D_MODEL), lambda i: (i, 0))],
        out_shape=[_sds((t, PROJ_W)), _sds((t, D_MODEL), BF)],
        compiler_params=_params("parallel"),
    )(x, nw, wt_al)


def _expand_matrix(first_row, group):
    row = _iota((LANES, 512), 0)
    col = _iota((LANES, 512), 1)
    return (col // group + first_row == row).astype(F32)


def _small_prep(proj, fb, al, dtb):
    t = proj.shape[0]

    def body(sm_ref, fb_ref, al_ref, dtb_ref, ce_ref, cumt_ref, be_ref, ge_ref):
        s = sm_ref[...]
        z = s + fb_ref[...]
        cum = jnp.minimum(z, 0.0) - jnp.log1p(jnp.exp(-jnp.abs(z)))
        row = _iota((t, LANES), 0)
        step = 1
        while step < t:
            cum = cum + _shift_down(cum, step, row)
            step *= 2
        cumt_ref[...] = cum.T
        ce_ref[...] = _mm_exact(cum, _expand_matrix(SM_FF, FOX_HEAD_DIM))
        be_ref[...] = _mm_exact(_sigmoid(s), _expand_matrix(SM_GB, GDN_HEAD_DIM))
        g = -jnp.exp(al_ref[...]) * _softplus(s + dtb_ref[...])
        ge_ref[...] = _mm_exact(g, _expand_matrix(SM_GA, GDN_HEAD_DIM))

    vec = pl.BlockSpec((1, LANES), lambda i: (0, 0))
    return pl.pallas_call(
        body, name="small_prep", grid=(1,),
        in_specs=[pl.BlockSpec((t, LANES), lambda i: (0, COL_SMALL // LANES)), vec, vec, vec],
        out_specs=[pl.BlockSpec((t, 512), lambda i: (0, 0)), pl.BlockSpec((LANES, t), lambda i: (0, 0)),
                   pl.BlockSpec((t, 512), lambda i: (0, 0)), pl.BlockSpec((t, 512), lambda i: (0, 0))],
        out_shape=[_sds((t, 512)), _sds((LANES, t)), _sds((t, 512)), _sds((t, 512))],
        compiler_params=_params("arbitrary"),
    )(proj, fb, al, dtb)


def _fox_scores(qh, kb, ce_ref, cumt_ref, head, hh, i, tq):
    klen = (i + 1) * tq
    s = _mm_nt(qh, kb[:klen]) * FOX_SCALE
    cq = ce_ref[i * tq:(i + 1) * tq, FOX_HEAD_DIM * hh:FOX_HEAD_DIM * hh + 1]
    ck = cumt_ref[pl.ds(head, 1), 0:klen]
    s = s + cq - ck
    qi = _iota((tq, klen), 0) + i * tq
    ki = _iota((tq, klen), 1)
    return jnp.where(ki <= qi, s, NEG_BIG)


def _fox_fwd(proj, ce, cumt, fnw):
    t = proj.shape[0]
    tq = min(TOKEN_BLOCK, t // 2)
    nq = t // tq

    def body(q_ref, k_ref, v_ref, ce_ref, cumt_ref, fnw_ref, o_ref, lse_ref, fn_ref):
        j = pl.program_id(0)
        first = _iota((1, LANES), 1) < FOX_HEAD_DIM
        kb = k_ref[...].astype(BF)
        vb = v_ref[...].astype(BF)
        for i in range(nq):
            rows = slice(i * tq, (i + 1) * tq)
            klen = (i + 1) * tq
            q_i = q_ref[rows, :]
            o_acc = jnp.zeros((tq, LANES), F32)
            lse_acc = jnp.zeros((tq, LANES), F32)
            for hh in range(2):
                mh = first if hh == 0 else jnp.logical_not(first)
                qh = jnp.where(mh, q_i, 0.0).astype(BF)
                s = _fox_scores(qh, kb, ce_ref, cumt_ref, 2 * j + hh, hh, i, tq)
                m = jnp.max(s, axis=-1, keepdims=True)
                p = jnp.exp(s - m)
                l = jnp.sum(p, axis=-1, keepdims=True)
                o = jnp.dot(p.astype(BF), vb[:klen], preferred_element_type=F32) / l
                o_acc = jnp.where(mh, o, o_acc)
                lse_acc = jnp.where(mh, m + jnp.log(l), lse_acc)
            o_ref[rows, :] = o_acc
            lse_ref[rows, :] = lse_acc
            o2 = o_acc * o_acc
            s0 = jnp.sum(jnp.where(first, o2, 0.0), axis=-1, keepdims=True)
            s1 = jnp.sum(jnp.where(first, 0.0, o2), axis=-1, keepdims=True)
            r = lax.rsqrt(jnp.where(first, s0, s1) * (1.0 / FOX_HEAD_DIM) + EPS)
            fn_ref[rows, :] = (o_acc * r * fnw_ref[...]).astype(BF)

    blk = lambda off: pl.BlockSpec((t, LANES), lambda j: (0, off + j))
    return pl.pallas_call(
        body, name="fox_fwd", grid=(N_FOX_HEADS // 2,),
        in_specs=[blk(0), blk(4), blk(8), blk(0), pl.BlockSpec((LANES, t), lambda j: (0, 0)),
                  pl.BlockSpec((1, LANES), lambda j: (0, 0))],
        out_specs=[blk(0), blk(0), blk(0)],
        out_shape=[_sds((t, D_FOX)), _sds((t, D_FOX)), _sds((t, D_FOX), BF)],
        compiler_params=_params("parallel"),
    )(proj, proj, proj, ce, cumt, fnw)


def _fox_bwd(proj, ce, cumt, lse, o, do):
    t = proj.shape[0]
    tq = min(TOKEN_BLOCK, t // 2)
    nq = t // tq

    def body(q_ref, k_ref, v_ref, ce_ref, cumt_ref, lse_ref, o_ref, do_ref,
             dq_ref, dk_ref, dv_ref, dcq_ref, dckt_ref, dk_s, dv_s, dck_s):
        j = pl.program_id(0)
        first = _iota((1, LANES), 1) < FOX_HEAD_DIM
        kf = k_ref[...]
        kb = kf.astype(BF)
        vb = v_ref[...].astype(BF)
        dk_s[...] = jnp.zeros_like(dk_s)
        dv_s[...] = jnp.zeros_like(dv_s)
        dck_s[...] = jnp.zeros_like(dck_s)
        masks = [first, jnp.logical_not(first)]
        kmask = [jnp.where(mh, kf, 0.0).astype(BF) for mh in masks]
        for i in range(nq):
            rows = slice(i * tq, (i + 1) * tq)
            klen = (i + 1) * tq
            q_i = q_ref[rows, :]
            do_i = do_ref[rows, :]
            o_i = o_ref[rows, :]
            lse_i = lse_ref[rows, :]
            dq_acc = jnp.zeros((tq, LANES), F32)
            dcq_acc = jnp.zeros((tq, LANES), F32)
            for hh in range(2):
                mh = masks[hh]
                qh = jnp.where(mh, q_i, 0.0).astype(BF)
                doh = jnp.where(mh, do_i, 0.0)
                dohb = doh.astype(BF)
                delta = jnp.sum(doh * o_i, axis=-1, keepdims=True)
                s = _fox_scores(qh, kb, ce_ref, cumt_ref, 2 * j + hh, hh, i, tq)
                p = jnp.exp(s - lse_i[:, FOX_HEAD_DIM * hh:FOX_HEAD_DIM * hh + 1])
                dp = _mm_nt(dohb, vb[:klen])
                ds = p * (dp - delta)
                dsb = ds.astype(BF)
                dq_acc = dq_acc + jnp.dot(dsb, kmask[hh][:klen], preferred_element_type=F32) * FOX_SCALE
                dk_s[0:klen, :] += _mm_tn(dsb, qh) * FOX_SCALE
                dv_s[0:klen, :] += _mm_tn(p, dohb)
                dcq_acc = jnp.where(mh, jnp.sum(ds, axis=-1, keepdims=True), dcq_acc)
                dck_s[hh:hh + 1, 0:klen] += jnp.sum(ds, axis=0, keepdims=True)
            dq_ref[rows, :] = dq_acc
            dcq_ref[rows, :] = dcq_acc
        dk_ref[...] = dk_s[...]
        dv_ref[...] = dv_s[...]
        dckt_ref[...] = jnp.zeros_like(dckt_ref)
        dckt_ref[0:8, :] = dck_s[...]

    blk = lambda off: pl.BlockSpec((t, LANES), lambda j: (0, off + j))
    return pl.pallas_call(
        body, name="fox_bwd", grid=(N_FOX_HEADS // 2,),
        in_specs=[blk(0), blk(4), blk(8), blk(0), pl.BlockSpec((LANES, t), lambda j: (0, 0)), blk(0), blk(0), blk(0)],
        out_specs=[blk(0), blk(0), blk(0), blk(0), pl.BlockSpec((32, t), lambda j: (j, 0))],
        out_shape=[_sds((t, D_FOX))] * 4 + [_sds((LANES, t))],
        scratch_shapes=[pltpu.VMEM((t, LANES), F32), pltpu.VMEM((t, LANES), F32), pltpu.VMEM((8, t), F32)],
        compiler_params=_params("parallel"),
    )(proj, proj, proj, ce, cumt, lse, o, do)


def _conv(x, w, row):
    return (w[3:4, :] * x + w[2:3, :] * _shift_down(x, 1, row) + w[1:2, :] * _shift_down(x, 2, row)
            + w[0:1, :] * _shift_down(x, 3, row))


def _chunk_decay(gc_c):
    gi = gc_c[:, 0:CHUNK]
    gj = gc_c.T[0:CHUNK, :]
    ri = _iota((CHUNK, CHUNK), 0)
    cj = _iota((CHUNK, CHUNK), 1)
    return jnp.where(ri >= cj, jnp.exp(jnp.minimum(gi - gj, 0.0)), 0.0), ri > cj


def _gdn_specs(t):
    col = lambda off: pl.BlockSpec((t, LANES), lambda h: (0, off + h))
    cw = lambda off: pl.BlockSpec((CONV_K, LANES), lambda h: (0, off + h))
    mat = pl.BlockSpec((1, t // CHUNK, CHUNK, CHUNK), lambda h: (h, 0, 0, 0))
    return col, cw, mat


def _gdn_prep(proj, convw, be, ge):
    t = proj.shape[0]
    nch = t // CHUNK

    def body(xq_ref, xk_ref, xv_ref, wq_ref, wk_ref, wv_ref, be_ref, ge_ref,
             qn_ref, kn_ref, cv_ref, gc_ref, m_ref, a_ref):
        row = _iota((t, LANES), 0)

        def act(x_ref, w_ref):
            y = _conv(x_ref[...], w_ref[...], row)
            return y * _sigmoid(y)

        cq = act(xq_ref, wq_ref)
        ck = act(xk_ref, wk_ref)
        cv_ref[...] = act(xv_ref, wv_ref)
        qn_ref[...] = cq * lax.rsqrt(jnp.sum(cq * cq, axis=-1, keepdims=True) + EPS) * GDN_QSCALE
        kn_ref[...] = ck * lax.rsqrt(jnp.sum(ck * ck, axis=-1, keepdims=True) + EPS)
        gc = ge_ref[...]
        pos = row % CHUNK
        step = 1
        while step < CHUNK:
            gc = gc + jnp.where(pos >= step, pltpu.roll(gc, step, 0), 0.0)
            step *= 2
        gc_ref[...] = gc

        def chunk(n, carry):
            sl = pl.ds(pl.multiple_of(n * CHUNK, CHUNK), CHUNK)
            k_c = kn_ref[sl, :]
            decay, strict = _chunk_decay(gc_ref[sl, :])
            m_ref[0, n] = jnp.where(strict, _mm_nt(k_c * be_ref[sl, :], k_c) * decay, 0.0)
            a_ref[0, n] = _mm_nt(qn_ref[sl, :], k_c) * decay
            return carry

        lax.fori_loop(0, nch, chunk, 0)

    col, cw, mat = _gdn_specs(t)
    return pl.pallas_call(
        body, name="gdn_prep", grid=(N_GDN_HEADS,),
        in_specs=[col(12), col(16), col(20), cw(0), cw(4), cw(8), col(0), col(0)],
        out_specs=[col(0), col(0), col(0), col(0), mat, mat],
        out_shape=[_sds((t, D_GDN))] * 4 + [_sds((N_GDN_HEADS, nch, CHUNK, CHUNK))] * 2,
        compiler_params=_params("parallel"),
    )(proj, proj, proj, convw, convw, convw, be, ge)


def _tri_inverse(m2):
    n_prob = m2.shape[0]
    assert n_prob == LANES
    nb = CHUNK * CHUNK // LANES

    def body(m_ref, t_ref, ms, ts):
        for b in range(nb):
            ms[b * LANES:(b + 1) * LANES, :] = m_ref[:, b * LANES:(b + 1) * LANES].T
        cidx = _iota((CHUNK, LANES), 0)

        def outer(i, carry):
            def inner(jj, acc):
                mrow = ms[pl.ds(i * CHUNK + jj, 1), :]
                return acc - mrow * ts[pl.ds(pl.multiple_of(jj * CHUNK, CHUNK), CHUNK), :]

            acc = lax.fori_loop(0, i, inner, jnp.where(cidx == i, 1.0, 0.0).astype(F32))
            ts[pl.ds(pl.multiple_of(i * CHUNK, CHUNK), CHUNK), :] = acc
            return carry

        lax.fori_loop(0, CHUNK, outer, 0)
        for b in range(nb):
            t_ref[:, b * LANES:(b + 1) * LANES] = ts[b * LANES:(b + 1) * LANES, :].T

    return pl.pallas_call(
        body, name="tri_inverse", in_specs=[VMEM_SPEC], out_specs=VMEM_SPEC,
        out_shape=_sds((LANES, CHUNK * CHUNK)),
        scratch_shapes=[pltpu.VMEM((CHUNK * CHUNK, LANES), F32), pltpu.VMEM((CHUNK * CHUNK, LANES), F32)],
        compiler_params=_params(),
    )(m2)


def _gdn_chunk_terms(q, k, v, b, gcc):
    eg = jnp.exp(gcc)
    last = gcc[CHUNK - 1:CHUNK, :]
    egl = jnp.exp(last - gcc)
    gl = jnp.exp(last)
    kb = k * b
    return eg, egl, gl, kb, v * b, kb * eg, q * eg, k * egl


GDN_BLOCK_CHUNKS = 4


def _gdn_block_specs(t, reverse):
    cb = GDN_BLOCK_CHUNKS
    nb = t // (cb * CHUNK)
    idx = (lambda i: nb - 1 - i) if reverse else (lambda i: i)
    tok = pl.BlockSpec((cb * CHUNK, D_GDN), lambda i: (idx(i), 0))
    mat = pl.BlockSpec((N_GDN_HEADS, cb, CHUNK, CHUNK), lambda i: (0, idx(i), 0, 0))
    state = pl.BlockSpec((N_GDN_HEADS, cb, GDN_HEAD_DIM, GDN_HEAD_DIM), lambda i: (0, idx(i), 0, 0))
    return nb, tok, mat, state


def _gdn_scan(qn, kn, cv, be, gc, tinv, amat):
    t = qn.shape[0]
    nch = t // CHUNK

    def body(q_ref, k_ref, v_ref, b_ref, gc_ref, t_ref, a_ref, o_ref, sall_ref, vn_ref, s_scr):
        @pl.when(pl.program_id(0) == 0)
        def _():
            s_scr[...] = jnp.zeros_like(s_scr)

        for hd in range(N_GDN_HEADS):
            cs = slice(hd * LANES, (hd + 1) * LANES)
            s = s_scr[hd]
            for cc in range(GDN_BLOCK_CHUNKS):
                rs = slice(cc * CHUNK, (cc + 1) * CHUNK)
                eg, egl, gl, kb, vb, kbg, qd, kd = _gdn_chunk_terms(q_ref[rs, cs], k_ref[rs, cs], v_ref[rs, cs],
                                                                    b_ref[rs, cs], gc_ref[rs, cs])
                sall_ref[hd, cc] = s
                uw = _mm(t_ref[hd, cc], jnp.concatenate([vb, kbg], axis=1))
                ws_qs = _mm(jnp.concatenate([uw[:, LANES:], qd], axis=0), s)
                vn = uw[:, :LANES] - ws_qs[:CHUNK]
                vn_ref[rs, cs] = vn
                o_ref[rs, cs] = ws_qs[CHUNK:] + _mm(a_ref[hd, cc], vn)
                s = s * gl + _mm_tn(kd, vn)
            s_scr[hd] = s

    nb, tok, mat, state = _gdn_block_specs(t, False)
    return pl.pallas_call(
        body, name="gdn_scan", grid=(nb,),
        in_specs=[tok] * 5 + [mat, mat], out_specs=[tok, state, tok],
        out_shape=[_sds((t, D_GDN)), _sds((N_GDN_HEADS, nch, GDN_HEAD_DIM, GDN_HEAD_DIM)), _sds((t, D_GDN))],
        scratch_shapes=[pltpu.VMEM((N_GDN_HEADS, GDN_HEAD_DIM, GDN_HEAD_DIM), F32)],
        compiler_params=_params("arbitrary"),
    )(qn, kn, cv, be, gc, tinv, amat)


def _gdn_bwd(qn, kn, cv, be, gc, tinv, amat, s_all, vn_all, do):
    t = qn.shape[0]

    def body(q_ref, k_ref, v_ref, b_ref, gc_ref, t_ref, a_ref, sall_ref, vn_ref, do_ref,
             dq_ref, dk_ref, dv_ref, db_ref, dg_ref, ds_scr):
        @pl.when(pl.program_id(0) == 0)
        def _():
            ds_scr[...] = jnp.zeros_like(ds_scr)

        lastrow = _iota((CHUNK, LANES), 0) == CHUNK - 1
        for hd in range(N_GDN_HEADS):
            cs = slice(hd * LANES, (hd + 1) * LANES)
            dsp = ds_scr[hd]
            for cc in reversed(range(GDN_BLOCK_CHUNKS)):
                rs = slice(cc * CHUNK, (cc + 1) * CHUNK)
                q, k, v, b, gcc = q_ref[rs, cs], k_ref[rs, cs], v_ref[rs, cs], b_ref[rs, cs], gc_ref[rs, cs]
                eg, egl, gl, kb, vb, kbg, qd, kd = _gdn_chunk_terms(q, k, v, b, gcc)
                do_c = do_ref[rs, cs]
                tn = t_ref[hd, cc]
                an = a_ref[hd, cc]
                s = sall_ref[hd, cc]
                vn = vn_ref[rs, cs]
                w = _mm(tn, kbg)
                dvn = _mm_tn(an, do_c) + _mm(kd, dsp)
                do_dvn = jnp.concatenate([do_c, dvn], axis=0)
                by_s = _mm_nt(do_dvn, s)
                dqd, dw = by_s[:CHUNK], -by_s[CHUNK:]
                da = _mm_nt(do_c, vn)
                dkd = _mm_nt(vn, dsp)
                dgl = jnp.sum(jnp.sum(dsp * s, axis=-1, keepdims=True), axis=0, keepdims=True)
                dsp = _mm_tn(jnp.concatenate([qd, -w], axis=0), do_dvn) + gl * dsp
                dvn_dw = jnp.concatenate([dvn, dw], axis=1)
                dt = _mm_nt(dvn_dw, jnp.concatenate([vb, kbg], axis=1))
                by_t = _mm_tn(tn, dvn_dw)
                dvb, dkbg = by_t[:, :LANES], by_t[:, LANES:]
                decay, strict = _chunk_decay(gcc)
                by_k = _mm_nt(jnp.concatenate([kb, q], axis=0), k)
                kk, qk = by_k[:CHUNK], by_k[CHUNK:]
                dm = jnp.where(strict, -_mm_nt(_mm_tn(tn, dt), tn), 0.0)
                dkk = dm * decay
                dqk = da * decay
                gmat = dkk * kk + dqk * qk
                dqk_dkk = jnp.concatenate([dqk, dkk], axis=0)
                on_k = _mm(dqk_dkk, k)
                dq_ref[rs, cs] = dqd * eg + on_k[:CHUNK]
                dkb = on_k[CHUNK:] + dkbg * eg
                dk_ref[rs, cs] = dkd * egl + _mm_tn(dqk_dkk, jnp.concatenate([q, kb], axis=0)) + dkb * b
                db = jnp.sum(dkb * k, axis=-1, keepdims=True) + jnp.sum(dvb * v, axis=-1, keepdims=True)
                db_ref[rs, cs] = jnp.broadcast_to(db, (CHUNK, LANES))
                dv_ref[rs, cs] = dvb * b
                dkd_kd = jnp.sum(dkd * kd, axis=-1, keepdims=True)
                col_sums = jnp.sum(jnp.concatenate([gmat, jnp.zeros_like(gmat)], axis=1).T, axis=-1, keepdims=True)
                dgc = (jnp.sum(gmat, axis=-1, keepdims=True) - col_sums[:CHUNK]
                       + jnp.sum(dqd * qd, axis=-1, keepdims=True) + jnp.sum(dkbg * kbg, axis=-1, keepdims=True) - dkd_kd)
                extra = jnp.sum(dkd_kd, axis=0, keepdims=True) + dgl * gl
                dg_ref[rs, cs] = dgc + jnp.where(lastrow, extra, 0.0)
            ds_scr[hd] = dsp
        dg = dg_ref[...]
        row = _iota(dg.shape, 0)
        pos = row % CHUNK
        step = 1
        while step < CHUNK:
            dg = dg + jnp.where(pos < CHUNK - step, pltpu.roll(dg, dg.shape[0] - step, 0), 0.0)
            step *= 2
        dg_ref[...] = dg

    nb, tok, mat, state = _gdn_block_specs(t, True)
    return pl.pallas_call(
        body, name="gdn_bwd", grid=(nb,),
        in_specs=[tok] * 5 + [mat, mat, state, tok, tok], out_specs=[tok] * 5, out_shape=[_sds((t, D_GDN))] * 5,
        scratch_shapes=[pltpu.VMEM((N_GDN_HEADS, GDN_HEAD_DIM, GDN_HEAD_DIM), F32)],
        compiler_params=_params("arbitrary"),
    )(qn, kn, cv, be, gc, tinv, amat, s_all, vn_all, do)


def _gdn_bwd_conv(proj, convw, dqn, dkn, dcv):
    t = proj.shape[0]

    def body(xq_ref, xk_ref, xv_ref, wq_ref, wk_ref, wv_ref, dq_ref, dk_ref, dv_ref,
             dxq_ref, dxk_ref, dxv_ref, dwq_ref, dwk_ref, dwv_ref):
        row = _iota((t, LANES), 0)

        def one(x_ref, w_ref, d_ref, dx_ref, dw_ref, scale):
            x = x_ref[...]
            w = w_ref[...]
            y = _conv(x, w, row)
            sg = _sigmoid(y)
            dc = d_ref[...]
            if scale is not None:
                c = y * sg
                r = lax.rsqrt(jnp.sum(c * c, axis=-1, keepdims=True) + EPS)
                ch = c * r
                dc = scale * r * (dc - ch * jnp.sum(dc * ch, axis=-1, keepdims=True))
            dy = dc * sg * (1.0 + y * (1.0 - sg))
            dx_ref[...] = (w[3:4, :] * dy + w[2:3, :] * _shift_up(dy, 1, row) + w[1:2, :] * _shift_up(dy, 2, row)
                           + w[0:1, :] * _shift_up(dy, 3, row))
            for jj in range(CONV_K):
                xs = x if jj == CONV_K - 1 else _shift_down(x, CONV_K - 1 - jj, row)
                dw_ref[jj:jj + 1, :] = jnp.sum(dy * xs, axis=0, keepdims=True)

        one(xq_ref, wq_ref, dq_ref, dxq_ref, dwq_ref, GDN_QSCALE)
        one(xk_ref, wk_ref, dk_ref, dxk_ref, dwk_ref, 1.0)
        one(xv_ref, wv_ref, dv_ref, dxv_ref, dwv_ref, None)

    col, cw, _ = _gdn_specs(t)
    return pl.pallas_call(
        body, name="gdn_bwd_conv", grid=(N_GDN_HEADS,),
        in_specs=[col(12), col(16), col(20), cw(0), cw(4), cw(8), col(0), col(0), col(0)],
        out_specs=[col(0), col(0), col(0), cw(0), cw(0), cw(0)],
        out_shape=[_sds((t, D_GDN))] * 3 + [_sds((CONV_K, D_GDN))] * 3,
        compiler_params=_params("parallel"),
    )(proj, proj, proj, convw, convw, convw, dqn, dkn, dcv)


def _mix_out(fox_n, gdn_o, proj, gnw, w_out, x, pmw, plw):
    t = x.shape[0]
    tm = TOKEN_BLOCK

    def body(fn_ref, go_ref, gz_ref, gnw_ref, w_ref, x_ref, pmw_ref, plw_ref, x1_ref, h2_ref, mixed_ref, omix_ref,
             h2t_ref):
        omix_ref[:, 0:D_FOX] = fn_ref[...]
        for hd in range(N_GDN_HEADS):
            cs = slice(hd * LANES, (hd + 1) * LANES)
            go = go_ref[:, cs]
            r = lax.rsqrt(jnp.mean(go * go, axis=-1, keepdims=True) + EPS)
            gz = gz_ref[:, cs]
            omix_ref[:, D_FOX + hd * LANES:D_FOX + (hd + 1) * LANES] = (
                go * r * gnw_ref[...] * (gz * _sigmoid(gz))).astype(BF)
        mixed = jnp.dot(omix_ref[...], w_ref[...], preferred_element_type=F32)
        mixed_ref[...] = mixed
        r2 = lax.rsqrt(jnp.mean(mixed * mixed, axis=-1, keepdims=True) + EPS)
        x1 = x_ref[...] + mixed * r2 * pmw_ref[...]
        x1_ref[...] = x1
        r3 = lax.rsqrt(jnp.mean(x1 * x1, axis=-1, keepdims=True) + EPS)
        h2 = x1 * r3 * plw_ref[...]
        h2_ref[...] = h2.astype(BF)
        h2t_ref[...] = h2.T.astype(BF)

    tok = lambda w: pl.BlockSpec((tm, w), lambda i: (i, 0))
    vec = lambda w: pl.BlockSpec((1, w), lambda i: (0, 0))
    return pl.pallas_call(
        body, name="mix_out", grid=(t // tm,),
        in_specs=[tok(D_FOX), tok(D_GDN), pl.BlockSpec((tm, D_GDN), lambda i: (i, COL_GZ // D_GDN)), vec(LANES),
                  pl.BlockSpec((D_MODEL, D_MODEL), lambda i: (0, 0)), tok(D_MODEL), vec(D_MODEL), vec(D_MODEL)],
        out_specs=[tok(D_MODEL)] * 4 + [pl.BlockSpec((D_MODEL, tm), lambda i: (0, i))],
        out_shape=[_sds((t, D_MODEL)), _sds((t, D_MODEL), BF), _sds((t, D_MODEL)), _sds((t, D_MODEL), BF),
                   _sds((D_MODEL, t), BF)],
        compiler_params=_params("parallel"),
    )(fox_n, gdn_o, proj, gnw, w_out, x, pmw, plw)


def _out_bwd(dmixed, w_out, o_fox, gdn_o, proj, fnw, gnw):
    t = dmixed.shape[0]
    tm = TOKEN_BLOCK

    def body(dm_ref, w_ref, of_ref, go_ref, gz_ref, fnw_ref, gnw_ref, dof_ref, dgo_ref, dgz_ref, dfw_ref, dgw_ref):
        i = pl.program_id(0)

        @pl.when(i == 0)
        def _():
            dfw_ref[...] = jnp.zeros_like(dfw_ref)
            dgw_ref[...] = jnp.zeros_like(dgw_ref)

        domix = _mm_nt(dm_ref[...], w_ref[...])
        first = _iota((1, LANES), 1) < FOX_HEAD_DIM
        dfw = jnp.zeros((1, LANES), F32)
        dgw = jnp.zeros((1, LANES), F32)
        for pr in range(N_FOX_HEADS // 2):
            cs = slice(pr * LANES, (pr + 1) * LANES)
            o = of_ref[:, cs]
            dfn = domix[:, cs]
            o2 = o * o
            s0 = jnp.sum(jnp.where(first, o2, 0.0), axis=-1, keepdims=True)
            s1 = jnp.sum(jnp.where(first, 0.0, o2), axis=-1, keepdims=True)
            r = lax.rsqrt(jnp.where(first, s0, s1) * (1.0 / FOX_HEAD_DIM) + EPS)
            oh = o * r
            dfw = dfw + jnp.sum(dfn * oh, axis=0, keepdims=True)
            doh = dfn * fnw_ref[...]
            pr_ = doh * oh
            m0 = jnp.sum(jnp.where(first, pr_, 0.0), axis=-1, keepdims=True)
            m1 = jnp.sum(jnp.where(first, 0.0, pr_), axis=-1, keepdims=True)
            dof_ref[:, cs] = r * (doh - oh * jnp.where(first, m0, m1) * (1.0 / FOX_HEAD_DIM))
        for hd in range(N_GDN_HEADS):
            cs = slice(hd * LANES, (hd + 1) * LANES)
            go = go_ref[:, cs]
            gz = gz_ref[:, cs]
            dgated = domix[:, D_FOX + hd * LANES:D_FOX + (hd + 1) * LANES]
            r = lax.rsqrt(jnp.mean(go * go, axis=-1, keepdims=True) + EPS)
            goh = go * r
            sg = _sigmoid(gz)
            sz = gz * sg
            gn = goh * gnw_ref[...]
            dgn = dgated * sz
            dgz_ref[:, cs] = dgated * gn * sg * (1.0 + gz * (1.0 - sg))
            dgw = dgw + jnp.sum(dgn * goh, axis=0, keepdims=True)
            dgh = dgn * gnw_ref[...]
            dgo_ref[:, cs] = r * (dgh - goh * jnp.mean(dgh * goh, axis=-1, keepdims=True))
        dfw_ref[...] += dfw + pltpu.roll(dfw, FOX_HEAD_DIM, 1)
        dgw_ref[...] += dgw

    tok = lambda w: pl.BlockSpec((tm, w), lambda i: (i, 0))
    vec = lambda w: pl.BlockSpec((1, w), lambda i: (0, 0))
    return pl.pallas_call(
        body, name="out_bwd", grid=(t // tm,),
        in_specs=[tok(D_MODEL), pl.BlockSpec((D_MODEL, D_MODEL), lambda i: (0, 0)), tok(D_FOX), tok(D_GDN),
                  pl.BlockSpec((tm, D_GDN), lambda i: (i, COL_GZ // D_GDN)), vec(LANES), vec(LANES)],
        out_specs=[tok(D_FOX), tok(D_GDN), tok(D_GDN), vec(LANES), vec(LANES)],
        out_shape=[_sds((t, D_FOX)), _sds((t, D_GDN)), _sds((t, D_GDN)), _sds((1, LANES)), _sds((1, LANES))],
        compiler_params=_params("arbitrary"),
    )(dmixed, w_out, o_fox, gdn_o, proj, fnw, gnw)


def _mlp_up(h2, w_up):
    t = h2.shape[0]
    tm = TOKEN_BLOCK
    pc = D_FF // N_DEV

    def body(h_ref, w_ref, up_ref):
        h = h_ref[...]
        for p in range(N_DEV):
            up_ref[:, p * pc:(p + 1) * pc] = jnp.dot(h, w_ref[p], preferred_element_type=F32)

    return pl.pallas_call(
        body, name="mlp_up", grid=(t // tm,),
        in_specs=[pl.BlockSpec((tm, D_MODEL), lambda i: (i, 0)),
                  pl.BlockSpec((N_DEV, D_MODEL, pc), lambda i: (0, 0, 0))],
        out_specs=pl.BlockSpec((tm, D_FF), lambda i: (i, 0)), out_shape=_sds((t, D_FF)),
        compiler_params=_params("parallel"),
    )(h2, w_up)


def _mlp_down_loss(up, w_down, x1, pw, target):
    t = up.shape[0]
    tm = TOKEN_BLOCK

    def body(up_ref, w_ref, x1_ref, pw_ref, tg_ref, dy_ref, dx2_ref, loss_ref, dpw_ref):
        i = pl.program_id(0)

        @pl.when(i == 0)
        def _():
            loss_ref[...] = jnp.zeros_like(loss_ref)
            dpw_ref[...] = jnp.zeros_like(dpw_ref)

        u = jnp.maximum(up_ref[...], 0.0)
        y = jnp.dot((u * u).astype(BF), w_ref[...], preferred_element_type=F32)
        r = lax.rsqrt(jnp.mean(y * y, axis=-1, keepdims=True) + EPS)
        yh = y * r
        pw = pw_ref[...]
        err = x1_ref[...] + yh * pw - tg_ref[...]
        part = jnp.sum(jnp.sum(err * err, axis=-1, keepdims=True), axis=0, keepdims=True) * (0.5 / D_MODEL)
        loss_ref[...] += jnp.broadcast_to(part, loss_ref.shape)
        dx2 = err * (1.0 / D_MODEL)
        dx2_ref[...] = dx2
        dpw_ref[...] += jnp.sum(dx2 * yh, axis=0, keepdims=True)
        dyh = dx2 * pw
        dy_ref[...] = (r * (dyh - yh * jnp.mean(dyh * yh, axis=-1, keepdims=True))).astype(BF)

    tok = lambda w: pl.BlockSpec((tm, w), lambda i: (i, 0))
    vec = lambda w: pl.BlockSpec((1, w), lambda i: (0, 0))
    return pl.pallas_call(
        body, name="mlp_down_loss", grid=(t // tm,),
        in_specs=[tok(D_FF), pl.BlockSpec((D_FF, D_MODEL), lambda i: (0, 0)), tok(D_MODEL), vec(D_MODEL), tok(D_MODEL)],
        out_specs=[tok(D_MODEL), tok(D_MODEL), vec(LANES), vec(D_MODEL)],
        out_shape=[_sds((t, D_MODEL), BF), _sds((t, D_MODEL)), _sds((1, LANES)), _sds((1, D_MODEL))],
        compiler_params=_params("arbitrary"),
    )(up, w_down, x1, pw, target)


def _mlp_bwd_act(dy, w_down, up):
    t = dy.shape[0]
    tm = TOKEN_BLOCK

    def body(dy_ref, w_ref, up_ref, dup_ref):
        da = lax.dot_general(dy_ref[...], w_ref[...], (((1,), (1,)), ((), ())), preferred_element_type=F32)
        dup_ref[...] = (da * (2.0 * jnp.maximum(up_ref[...], 0.0))).astype(BF)

    return pl.pallas_call(
        body, name="mlp_bwd_act", grid=(t // tm,),
        in_specs=[pl.BlockSpec((tm, D_MODEL), lambda i: (i, 0)), pl.BlockSpec((D_FF, D_MODEL), lambda i: (0, 0)),
                  pl.BlockSpec((tm, D_FF), lambda i: (i, 0))],
        out_specs=pl.BlockSpec((tm, D_FF), lambda i: (i, 0)), out_shape=_sds((t, D_FF), BF),
        compiler_params=_params("parallel"),
    )(dy, w_down, up)


def _mlp_bwd_in(dup, w_up, x1, plw, dx2, mixed, pmw):
    t = dup.shape[0]
    tm = TOKEN_BLOCK

    def body(dup_ref, w_ref, x1_ref, plw_ref, dx2_ref, mx_ref, pmw_ref, dx1_ref, dmixed_ref, dplw_ref, dpmw_ref):
        i = pl.program_id(0)

        @pl.when(i == 0)
        def _():
            dplw_ref[...] = jnp.zeros_like(dplw_ref)
            dpmw_ref[...] = jnp.zeros_like(dpmw_ref)

        pc = D_FF // N_DEV
        dh = _mm_nt(dup_ref[:, 0:pc], w_ref[0])
        for p in range(1, N_DEV):
            dh = dh + _mm_nt(dup_ref[:, p * pc:(p + 1) * pc], w_ref[p])
        x1 = x1_ref[...]
        r = lax.rsqrt(jnp.mean(x1 * x1, axis=-1, keepdims=True) + EPS)
        xh = x1 * r
        dplw_ref[...] += jnp.sum(dh * xh, axis=0, keepdims=True)
        dxh = dh * plw_ref[...]
        dx1 = dx2_ref[...] + r * (dxh - xh * jnp.mean(dxh * xh, axis=-1, keepdims=True))
        dx1_ref[...] = dx1
        mx = mx_ref[...]
        r2 = lax.rsqrt(jnp.mean(mx * mx, axis=-1, keepdims=True) + EPS)
        mh = mx * r2
        dpmw_ref[...] += jnp.sum(dx1 * mh, axis=0, keepdims=True)
        dmh = dx1 * pmw_ref[...]
        dmixed_ref[...] = (r2 * (dmh - mh * jnp.mean(dmh * mh, axis=-1, keepdims=True))).astype(BF)

    tok = lambda w: pl.BlockSpec((tm, w), lambda i: (i, 0))
    vec = lambda w: pl.BlockSpec((1, w), lambda i: (0, 0))
    return pl.pallas_call(
        body, name="mlp_bwd_in", grid=(t // tm,),
        in_specs=[tok(D_FF), pl.BlockSpec((N_DEV, D_MODEL, D_FF // N_DEV), lambda i: (0, 0, 0)), tok(D_MODEL),
                  vec(D_MODEL), tok(D_MODEL), tok(D_MODEL), vec(D_MODEL)],
        out_specs=[tok(D_MODEL), tok(D_MODEL), vec(D_MODEL), vec(D_MODEL)],
        out_shape=[_sds((t, D_MODEL)), _sds((t, D_MODEL), BF), _sds((1, D_MODEL)), _sds((1, D_MODEL))],
        compiler_params=_params("arbitrary"),
    )(dup, w_up, x1, plw, dx2, mixed, pmw)


def _wgrad(a, b, a_cols, split=1, a_fn=None, a_block0=0, name="wgrad"):
    t, b_cols = b.shape
    n_a = (a.shape[1] - a_block0 * a_cols) // a_cols if a_block0 else a.shape[1] // a_cols

    def body(a_ref, b_ref, o_ref):
        av = a_ref[...]
        if a_fn is not None:
            av = a_fn(av)
        o_ref[...] = _mm_tn(av, b_ref[...]).astype(BF).reshape(o_ref.shape)

    return pl.pallas_call(
        body, name=name, grid=(n_a,),
        in_specs=[pl.BlockSpec((t, a_cols), lambda i: (0, i + a_block0)), pl.BlockSpec((t, b_cols), lambda i: (0, 0))],
        out_specs=pl.BlockSpec((split, a_cols // split, b_cols), lambda i: (i, 0, 0)),
        out_shape=_sds((n_a * split, a_cols // split, b_cols), BF),
        compiler_params=_params("parallel"),
    )(a, b)


def _wgrad_pre_t(at, b, b_cols, name):
    rows, t = at.shape
    n_b = b.shape[1] // b_cols

    def body(a_ref, b_ref, o_ref):
        o_ref[0] = jnp.dot(a_ref[...], b_ref[...], preferred_element_type=F32).astype(BF)

    return pl.pallas_call(
        body, name=name, grid=(n_b,),
        in_specs=[pl.BlockSpec((rows, t), lambda j: (0, 0)), pl.BlockSpec((t, b_cols), lambda j: (0, j))],
        out_specs=pl.BlockSpec((1, rows, b_cols), lambda j: (j, 0, 0)), out_shape=_sds((n_b, rows, b_cols), BF),
        compiler_params=_params("parallel"),
    )(at, b)


def _select_matrix(rows, fn):
    r = _iota((rows, LANES), 0)
    c = _iota((rows, LANES), 1)
    return (r == fn(c)).astype(F32)


def _small_bwd(proj, fb, al, dtb, dcq, dckt, dbe, dge):
    t = proj.shape[0]

    def body(sm_ref, fb_ref, al_ref, dtb_ref, dcq_ref, dckt_ref, dbe_ref, dge_ref, dsm_ref, dvec_ref):
        s = sm_ref[...]
        lane = _iota((1, LANES), 1)
        sel_f = _select_matrix(512, lambda c: jnp.where(c < 8, FOX_HEAD_DIM * c, -1))
        sel_k = _select_matrix(LANES, lambda c: jnp.where(c < 8, 32 * (c // 2) + c % 2, -1))
        dcum = _mm_exact(dcq_ref[...], sel_f) - _mm_exact(dckt_ref[...].T, sel_k)
        row = _iota((t, LANES), 0)
        step = 1
        while step < t:
            dcum = dcum + _shift_up(dcum, step, row)
            step *= 2
        dff = dcum * _sigmoid(-(s + fb_ref[...]))
        sel_b = _select_matrix(512, lambda c: jnp.where((c >= SM_GB) & (c < SM_GA), LANES * (c - SM_GB), -1))
        sel_g = _select_matrix(512, lambda c: jnp.where((c >= SM_GA) & (c < SM_GA + 4), LANES * (c - SM_GA), -1))
        beta = _sigmoid(s)
        dgb = _mm_exact(dbe_ref[...], sel_b) * beta * (1.0 - beta)
        dg = _mm_exact(dge_ref[...], sel_g)
        za = s + dtb_ref[...]
        nea = -jnp.exp(al_ref[...])
        dga = dg * nea * _sigmoid(za)
        is_f = lane < SM_GB
        is_b = (lane >= SM_GB) & (lane < SM_GA)
        is_a = (lane >= SM_GA) & (lane < SM_GA + 4)
        dsm_ref[...] = jnp.where(is_f, dff, jnp.where(is_b, dgb, jnp.where(is_a, dga, 0.0)))
        dvec_ref[...] = jnp.zeros_like(dvec_ref)
        dvec_ref[0:1, :] = jnp.sum(jnp.where(is_f, dff, 0.0), axis=0, keepdims=True)
        dvec_ref[1:2, :] = jnp.sum(jnp.where(is_a, dg * nea * _softplus(za), 0.0), axis=0, keepdims=True)
        dvec_ref[2:3, :] = jnp.sum(jnp.where(is_a, dga, 0.0), axis=0, keepdims=True)

    vec = pl.BlockSpec((1, LANES), lambda i: (0, 0))
    full = lambda r, c: pl.BlockSpec((r, c), lambda i: (0, 0))
    return pl.pallas_call(
        body, name="small_bwd", grid=(1,),
        in_specs=[pl.BlockSpec((t, LANES), lambda i: (0, COL_SMALL // LANES)), vec, vec, vec, full(t, 512),
                  full(LANES, t), full(t, 512), full(t, 512)],
        out_specs=[full(t, LANES), full(8, LANES)], out_shape=[_sds((t, LANES)), _sds((8, LANES))],
        compiler_params=_params("arbitrary"),
    )(proj, fb, al, dtb, dcq, dckt, dbe, dge)


def _in_bwd(dfox, dgdn, dgz, dsm, wt_al, x, nw, dx1):
    t = x.shape[0]
    tm = TOKEN_BLOCK

    def body(*refs):
        parts, (w_ref, x_ref, nw_ref, dx1_ref, dp_ref, dx_ref, dnw_ref) = refs[:8], refs[8:]
        i = pl.program_id(0)

        @pl.when(i == 0)
        def _():
            dnw_ref[...] = jnp.zeros_like(dnw_ref)

        col = 0
        for part in parts:
            width = part.shape[1]
            dp_ref[:, col:col + width] = part[...].astype(BF)
            col += width
        dh = jnp.dot(dp_ref[...], w_ref[...], preferred_element_type=F32)
        xv = x_ref[...]
        r = lax.rsqrt(jnp.mean(xv * xv, axis=-1, keepdims=True) + EPS)
        xh = xv * r
        dnw_ref[...] += jnp.sum(dh * xh, axis=0, keepdims=True)
        dxh = dh * nw_ref[...]
        dx_ref[...] = dx1_ref[...] + r * (dxh - xh * jnp.mean(dxh * xh, axis=-1, keepdims=True))

    tok = lambda w: pl.BlockSpec((tm, w), lambda i: (i, 0))
    vec = lambda w: pl.BlockSpec((1, w), lambda i: (0, 0))
    return pl.pallas_call(
        body, name="in_bwd", grid=(t // tm,),
        in_specs=[tok(D_FOX)] * 3 + [tok(D_GDN)] * 4 + [tok(LANES), pl.BlockSpec((PROJ_W, D_MODEL), lambda i: (0, 0)),
                                                        tok(D_MODEL), vec(D_MODEL), tok(D_MODEL)],
        out_specs=[tok(PROJ_W), tok(D_MODEL), vec(D_MODEL)],
        out_shape=[_sds((t, PROJ_W), BF), _sds((t, D_MODEL)), _sds((1, D_MODEL))],
        compiler_params=_params("arbitrary"),
    )(*dfox, *dgdn, dgz, dsm, wt_al, x, nw, dx1)


def _row(v, width=None):
    v = v.reshape(1, -1).astype(F32)
    if width is not None and v.shape[1] < width:
        v = jnp.pad(v, ((0, 0), (0, width - v.shape[1])))
    return v


def _lane_vec(v, first):
    return jnp.zeros((1, LANES), F32).at[0, first:first + v.shape[0]].set(v.astype(F32))


def _local_step(x, target, wt_al, w_out, mlp_weights, on_mlp_grads, convw, pre_mix_norm, fox_f_bias, fox_out_norm,
                gdn_a_log, gdn_dt_bias, gdn_out_norm, post_mix_norm, pre_mlp_norm, post_mlp_norm):
    t = x.shape[0]
    nch = t // CHUNK
    nw, pmw, plw, pw = _row(pre_mix_norm), _row(post_mix_norm), _row(pre_mlp_norm), _row(post_mlp_norm)
    fb, al, dtb = _lane_vec(fox_f_bias, SM_FF), _lane_vec(gdn_a_log, SM_GA), _lane_vec(gdn_dt_bias, SM_GA)
    fnw = _row(jnp.tile(fox_out_norm, 2))
    gnw = _row(gdn_out_norm)

    proj, h = _norm_proj(x, nw, wt_al)
    ce, cumt, be, ge = _small_prep(proj, fb, al, dtb)
    o_fox, lse, fox_n = _fox_fwd(proj, ce, cumt, fnw)
    qn, kn, cv, gc, mmat, amat = _gdn_prep(proj, convw, be, ge)
    n_prob = N_GDN_HEADS * nch
    m2 = mmat.reshape(n_prob, CHUNK * CHUNK)
    if n_prob < LANES:
        m2 = jnp.pad(m2, ((0, LANES - n_prob), (0, 0)))
    tinv = _tri_inverse(m2)[:n_prob].reshape(N_GDN_HEADS, nch, CHUNK, CHUNK)
    gdn_o, s_all, vn_all = _gdn_scan(qn, kn, cv, be, gc, tinv, amat)
    x1, h2, mixed, omix, h2t = _mix_out(fox_n, gdn_o, proj, gnw, w_out, x, pmw, plw)
    w_up, w_down = mlp_weights(h2)
    up = _mlp_up(h2, w_up)
    dy, dx2, loss, d_pw = _mlp_down_loss(up, w_down, x1, pw, target)

    dup = _mlp_bwd_act(dy, w_down, up)
    relu2 = lambda u: jnp.square(jnp.maximum(u, 0.0))
    g_down = _wgrad(up, dy, D_FF // N_DEV, a_fn=relu2, name="wgrad_down")
    g_up = _wgrad_pre_t(h2t, dup, D_FF // N_DEV, name="wgrad_up")
    mlp_handle, token = on_mlp_grads(g_up, g_down)
    dx1, dmixed, d_plw, d_pmw = _mlp_bwd_in(dup, w_up, x1, plw + token[0:1, 0:1], dx2, mixed, pmw)
    g_out = _wgrad(omix, dmixed, 512, split=4, name="wgrad_out")
    do_fox, dgo, dgz, d_fnw, d_gnw = _out_bwd(dmixed, w_out, o_fox, gdn_o, proj, fnw, gnw)
    dfq, dfk, dfv, dcq, dckt = _fox_bwd(proj, ce, cumt, lse, o_fox, do_fox)
    dqn, dkn, dcv, dbe, dge = _gdn_bwd(qn, kn, cv, be, gc, tinv, amat, s_all, vn_all, dgo)
    dxq, dxk, dxv, dwq, dwk, dwv = _gdn_bwd_conv(proj, convw, dqn, dkn, dcv)
    dsm, dvec = _small_bwd(proj, fb, al, dtb, dcq, dckt, dbe, dge)
    dproj, grad_x, d_nw = _in_bwd((dfq, dfk, dfv), (dxq, dxk, dxv), dgz, dsm, wt_al, x, nw, dx1)
    g_main = _wgrad(dproj, h, 512, name="wgrad_in")
    g_tail = _wgrad(dproj, h, LANES, a_block0=COL_SMALL // LANES, name="wgrad_in_small")
    g_in = jnp.concatenate([g_main.reshape(COL_SMALL, D_MODEL), g_tail[0]])
    small = dict(pre_mix_norm=d_nw[0], fox_f_bias=dvec[0, SM_FF:SM_FF + N_FOX_HEADS],
                 fox_out_norm=d_fnw[0, :FOX_HEAD_DIM], gdn_conv_w=(dwq, dwk, dwv),
                 gdn_a_log=dvec[1, SM_GA:SM_GA + N_GDN_HEADS], gdn_dt_bias=dvec[2, SM_GA:SM_GA + N_GDN_HEADS],
                 gdn_out_norm=d_gnw[0], post_mix_norm=d_pmw[0], pre_mlp_norm=d_plw[0], post_mlp_norm=d_pw[0])
    return loss[0, 0], grad_x, g_in, g_out, mlp_handle, small


MESH_IDS = pl.DeviceIdType.MESH
CHIP_FLIPS = ((0, 0), (1, 0), (0, 1), (1, 1))
ANY_SPEC = pl.BlockSpec(memory_space=pl.ANY)


def _place():
    return lax.axis_index("x"), lax.axis_index("y"), lax.axis_index("c")


def _all_gather(blocks):
    n = len(blocks)

    def body(*refs):
        ins, outs, (send_sems, recv_sems, local_sems) = refs[:n], refs[n:2 * n], refs[2 * n:]
        x, y, c = _place()
        sibling = (x, y, 1 - c)
        chips = [(x ^ fx, y ^ fy) for fx, fy in CHIP_FLIPS[1:]]

        def slot(out, px, py, pc):
            return out.at[4 * px + 2 * py + pc]

        def copy(a, k, block, to, src=None):
            return pltpu.make_async_remote_copy(
                src_ref=slot(outs[a], *block) if src is None else src, dst_ref=slot(outs[a], *block),
                send_sem=send_sems.at[a, k], recv_sem=recv_sems.at[a, k], device_id=to, device_id_type=MESH_IDS)

        pending = []
        for a in range(n):
            mine = pltpu.make_async_copy(ins[a], slot(outs[a], x, y, c), local_sems.at[a])
            mine.start()
            pending.append(mine)
        sends = []
        for a in range(n):
            first = [copy(a, 0, (x, y, c), sibling, src=ins[a])]
            first += [copy(a, 1 + j, (x, y, c), (*chip, c), src=ins[a]) for j, chip in enumerate(chips)]
            for cp in first:
                cp.start()
            sends += first
        for a in range(n):
            for j, chip in enumerate(chips):
                copy(a, 1 + j, (*chip, c), (x, y, c)).wait_recv()
                fwd = copy(a, 4 + j, (*chip, c), sibling)
                fwd.start()
                sends.append(fwd)
        for a in range(n):
            copy(a, 0, sibling, (x, y, c)).wait_recv()
            for j, chip in enumerate(chips):
                copy(a, 4 + j, (*chip, 1 - c), (x, y, c)).wait_recv()
        for cp in sends:
            cp.wait_send()
        for cp in pending:
            cp.wait()

    return pl.pallas_call(
        body, name="all_gather_weights", in_specs=[ANY_SPEC] * n, out_specs=[ANY_SPEC] * n,
        out_shape=[_sds((N_DEV,) + b.shape, b.dtype) for b in blocks],
        scratch_shapes=[pltpu.SemaphoreType.DMA((n, 7)), pltpu.SemaphoreType.DMA((n, 7)), pltpu.SemaphoreType.DMA((n,))],
        compiler_params=pltpu.CompilerParams(has_side_effects=True),
    )(*blocks)


def _adamw(w, g, m, v):
    m = ADAM_B1 * m + (1.0 - ADAM_B1) * g
    v = ADAM_B2 * v + (1.0 - ADAM_B2) * (g * g)
    m_hat = m / (1.0 - ADAM_B1 ** ADAM_STEP)
    v_hat = v / (1.0 - ADAM_B2 ** ADAM_STEP)
    return -ADAM_LR * (m_hat / (jnp.sqrt(v_hat) + ADAM_EPS) + ADAM_WD * w), m, v


def _reduce_scatter_adamw(g, w, m, v, name):
    _, r, c_ = g.shape
    rb = min(r, 256)

    def body(g_ref, w_ref, m_ref, v_ref, grad_ref, delta_ref, nm_ref, nv_ref,
             sib_buf, out_buf, ici_buf, send_a, recv_a, send_b, recv_b):
        x, y, c = _place()
        sibling = (x, y, 1 - c)
        chips = [(x ^ fx, y ^ fy) for fx, fy in CHIP_FLIPS]

        def piece(chip, core):
            return g_ref.at[4 * chip[0] + 2 * chip[1] + core]

        to_sibling = [pltpu.make_async_remote_copy(
            src_ref=piece(chip, 1 - c), dst_ref=sib_buf.at[j], send_sem=send_a.at[j], recv_sem=recv_a.at[j],
            device_id=sibling, device_id_type=MESH_IDS) for j, chip in enumerate(chips)]
        for cp in to_sibling:
            cp.start()
        to_owner = []
        for j in (1, 2, 3):
            to_sibling[j].wait_recv()
            own = piece(chips[j], c)
            for r0 in range(0, r, rb):
                rows = slice(r0, r0 + rb)
                out_buf[j, rows, :] = (own[rows, :].astype(F32) + sib_buf[j, rows, :].astype(F32)).astype(BF)
            cp = pltpu.make_async_remote_copy(
                src_ref=out_buf.at[j], dst_ref=ici_buf.at[j], send_sem=send_b.at[j], recv_sem=recv_b.at[j],
                device_id=(*chips[j], c), device_id_type=MESH_IDS)
            cp.start()
            to_owner.append(cp)
        to_sibling[0].wait_recv()
        for cp in to_owner:
            cp.wait_recv()
        own = piece(chips[0], c)
        for r0 in range(0, r, rb):
            rows = slice(r0, r0 + rb)
            total = own[rows, :].astype(F32) + sib_buf[0, rows, :].astype(F32)
            for j in (1, 2, 3):
                total = total + ici_buf[j, rows, :].astype(F32)
            grad_ref[rows, :] = total
            delta, nm, nv = _adamw(w_ref[rows, :], total, m_ref[rows, :], v_ref[rows, :])
            delta_ref[rows, :] = delta
            nm_ref[rows, :] = nm
            nv_ref[rows, :] = nv
        for cp in to_sibling + to_owner:
            cp.wait_send()

    return pl.pallas_call(
        body, name=name, in_specs=[VMEM_SPEC] * 4, out_specs=[VMEM_SPEC] * 4, out_shape=[_sds((r, c_))] * 4,
        scratch_shapes=[pltpu.VMEM((4, r, c_), BF), pltpu.VMEM((4, r, c_), BF), pltpu.VMEM((4, r, c_), BF),
                        pltpu.SemaphoreType.DMA((4,)), pltpu.SemaphoreType.DMA((4,)), pltpu.SemaphoreType.DMA((4,)),
                        pltpu.SemaphoreType.DMA((4,))],
        compiler_params=pltpu.CompilerParams(vmem_limit_bytes=VMEM_LIMIT, has_side_effects=True),
    )(g, w, m, v)


HBM_SPEC = pl.BlockSpec(memory_space=pltpu.HBM)
SEM_SPEC = pl.BlockSpec(memory_space=pltpu.SEMAPHORE)
DATAFLOW = pltpu.SideEffectType.DATAFLOW_SIDE_EFFECTING


def _peers():
    x, y, c = _place()
    return 4 * x + 2 * y + c, [(x ^ (k >> 2), y ^ ((k >> 1) & 1), c ^ (k & 1)) for k in range(1, N_DEV)]


def _peer_index(peer):
    return 4 * peer[0] + 2 * peer[1] + peer[2]


def _zones_with_own(srcs, pieces, name, after=None):
    n = len(srcs)
    extra = [] if after is None else [after]

    def body(me_ref, *refs):
        for a in range(n):
            refs[n + len(extra) + a][0] = refs[a][0] if pieces else refs[a][...]

    shapes = [s_.shape[1:] if pieces else s_.shape for s_ in srcs]
    mine = lambda sh: pl.BlockSpec((1,) + sh, lambda i, me_ref: (me_ref[0], 0, 0))
    in_specs = [mine(sh) if pieces else pl.BlockSpec(sh, lambda i, me_ref: (0, 0)) for sh in shapes]
    x, y, c = _place()
    return pl.pallas_call(
        body, name=name,
        grid_spec=pltpu.PrefetchScalarGridSpec(num_scalar_prefetch=1, grid=(1,), in_specs=in_specs + [ANY_SPEC] * len(extra),
                                               out_specs=[mine(sh) for sh in shapes]),
        out_shape=[_sds((N_DEV,) + sh, s_.dtype) for sh, s_ in zip(shapes, srcs)],
        compiler_params=_params("arbitrary"),
    )((4 * x + 2 * y + c).astype(jnp.int32).reshape(1), *srcs, *extra)


def _exchange_start(srcs, zones, pieces, name):
    n = len(srcs)

    def body(*refs):
        ins, zs = refs[:n], refs[n:2 * n]
        sems = refs[2 * n:4 * n]
        token = refs[-1]
        me, peers = _peers()
        for peer in peers:
            for a in range(n):
                pltpu.make_async_remote_copy(
                    src_ref=ins[a].at[_peer_index(peer)] if pieces else ins[a], dst_ref=zs[a].at[me],
                    send_sem=sems[2 * a], recv_sem=sems[2 * a + 1], device_id=peer, device_id_type=MESH_IDS).start()
        token[...] = jnp.zeros_like(token)

    hbm = lambda v: pltpu.with_memory_space_constraint(v, pltpu.HBM)
    out = pl.pallas_call(
        body, name=name,
        out_shape=tuple([pltpu.SemaphoreType.DMA(())] * (2 * n) + [pltpu.HBM(v.shape, v.dtype) for v in srcs]
                        + [pltpu.HBM(z.shape, z.dtype) for z in zones] + [_sds((8, LANES))]),
        in_specs=[HBM_SPEC] * (2 * n), out_specs=tuple([SEM_SPEC] * (2 * n) + [HBM_SPEC] * (2 * n) + [VMEM_SPEC]),
        input_output_aliases={i: 2 * n + i for i in range(2 * n)},
        compiler_params=pltpu.CompilerParams(has_side_effects=DATAFLOW),
    )(*[hbm(v) for v in srcs], *[hbm(z) for z in zones])
    return out[:2 * n], out[2 * n:3 * n], out[3 * n:4 * n], out[-1]


def _exchange_wait(sems, srcs, zones, after, name):
    n = len(srcs)

    def body(*refs):
        ins, zs, sm = refs[:n], refs[n:2 * n], refs[2 * n:4 * n]
        me, peers = _peers()
        for a in range(n):
            seven = zs[a].at[pl.ds(0, N_DEV - 1)]
            cp = pltpu.make_async_remote_copy(src_ref=seven, dst_ref=seven, send_sem=sm[2 * a], recv_sem=sm[2 * a + 1],
                                              device_id=peers[0], device_id_type=MESH_IDS)
            cp.wait_send()
            cp.wait_recv()

    out = pl.pallas_call(
        body, name=name, out_shape=tuple([pltpu.HBM(v.shape, v.dtype) for v in srcs] + [pltpu.HBM(z.shape, z.dtype) for z in zones]),
        in_specs=[HBM_SPEC] * (2 * n) + [SEM_SPEC] * (2 * n) + [ANY_SPEC], out_specs=tuple([HBM_SPEC] * (2 * n)),
        input_output_aliases={i: i for i in range(2 * n)},
        compiler_params=pltpu.CompilerParams(has_side_effects=DATAFLOW),
    )(*srcs, *zones, *sems, after)
    return out[n:]


def _sum_adamw(zone, w, m, v, name):
    _, r, c_ = zone.shape
    rb = min(r, 128)

    def body(z_ref, w_ref, m_ref, v_ref, grad_ref, delta_ref, nm_ref, nv_ref):
        total = z_ref[0].astype(F32)
        for d in range(1, N_DEV):
            total = total + z_ref[d].astype(F32)
        grad_ref[...] = total
        delta_ref[...], nm_ref[...], nv_ref[...] = _adamw(w_ref[...], total, m_ref[...], v_ref[...])

    blk = pl.BlockSpec((rb, c_), lambda i: (i, 0))
    return pl.pallas_call(
        body, name=name, grid=(r // rb,), in_specs=[pl.BlockSpec((N_DEV, rb, c_), lambda i: (0, i, 0)), blk, blk, blk],
        out_specs=[blk] * 4, out_shape=[_sds((r, c_))] * 4, compiler_params=_params("parallel"),
    )(zone, w, m, v)


SMALL_ROWS = 16


def _all_reduce_small(packed):
    def body(p_ref, sum_ref, gath, send_sems, recv_sems):
        x, y, c = _place()
        me = 4 * x + 2 * y + c
        gath[me] = p_ref[...]
        copies = []
        for k in range(1, N_DEV):
            peer = (x ^ (k >> 2), y ^ ((k >> 1) & 1), c ^ (k & 1))
            cp = pltpu.make_async_remote_copy(
                src_ref=p_ref, dst_ref=gath.at[me], send_sem=send_sems.at[k], recv_sem=recv_sems.at[k],
                device_id=peer, device_id_type=MESH_IDS)
            cp.start()
            copies.append(cp)
        for cp in copies:
            cp.wait_recv()
        total = gath[0]
        for d in range(1, N_DEV):
            total = total + gath[d]
        sum_ref[...] = total
        for cp in copies:
            cp.wait_send()

    return pl.pallas_call(
        body, name="all_reduce_small", in_specs=[VMEM_SPEC], out_specs=VMEM_SPEC, out_shape=_sds(packed.shape),
        scratch_shapes=[pltpu.VMEM((N_DEV,) + packed.shape, F32), pltpu.SemaphoreType.DMA((N_DEV,)),
                        pltpu.SemaphoreType.DMA((N_DEV,))],
        compiler_params=pltpu.CompilerParams(has_side_effects=True),
    )(packed)


def _adamw_small(w, g, m, v):
    def body(w_ref, g_ref, m_ref, v_ref, delta_ref, nm_ref, nv_ref):
        delta_ref[...], nm_ref[...], nv_ref[...] = _adamw(w_ref[...], g_ref[...], m_ref[...], v_ref[...])

    return pl.pallas_call(body, name="adamw_small", in_specs=[VMEM_SPEC] * 4, out_specs=[VMEM_SPEC] * 3,
                          out_shape=[_sds(w.shape)] * 3)(w, g, m, v)


NATIVE_ROWS = ((0, 1536), (1544, 3080), (3088, 3600), (1536, 1544), (3080, 3088))


def _to_aligned_rows(wt_native):
    pad = jnp.zeros((PROJ_W - D_PROJ, wt_native.shape[1]), wt_native.dtype)
    return jnp.concatenate([wt_native[lo:hi] for lo, hi in NATIVE_ROWS] + [pad])


def _from_aligned_rows(gt_al):
    return jnp.concatenate([gt_al[0:1536], gt_al[3584:3592], gt_al[1536:3072], gt_al[3592:3600], gt_al[3072:3584]])


def _cols_from_pieces(p):
    return p.transpose(1, 0, 2).reshape(p.shape[1], -1)


SMALL_NORMS = ("pre_mix_norm", "post_mix_norm", "pre_mlp_norm", "post_mlp_norm")
SMALL_MISC = (("fox_out_norm", FOX_HEAD_DIM), ("gdn_out_norm", GDN_HEAD_DIM), ("fox_f_bias", N_FOX_HEADS),
              ("gdn_a_log", N_GDN_HEADS), ("gdn_dt_bias", N_GDN_HEADS))


def _pack_small(vals, conv):
    misc = jnp.concatenate([vals[n].astype(F32) for n, _ in SMALL_MISC])
    rows = [vals[n].astype(F32) for n in SMALL_NORMS] + [jnp.pad(misc, (0, D_MODEL - misc.shape[0]))]
    flat = conv.astype(F32).reshape(-1)
    n_rows = -(-flat.shape[0] // D_MODEL)
    flat = jnp.pad(flat, (0, n_rows * D_MODEL - flat.shape[0])).reshape(n_rows, D_MODEL)
    packed = jnp.concatenate([jnp.stack(rows), flat])
    return jnp.pad(packed, ((0, SMALL_ROWS - packed.shape[0]), (0, 0)))


def _unpack_small(packed, conv_shape):
    out = {n: packed[i] for i, n in enumerate(SMALL_NORMS)}
    off = 0
    for n, size in SMALL_MISC:
        out[n] = packed[4, off:off + size]
        off += size
    n_conv = conv_shape[0] * conv_shape[1]
    out["gdn_conv_w"] = packed[5:].reshape(-1)[:n_conv].reshape(conv_shape)
    return out


WEIGHT_ORDER = ("pre_mix_norm", "w_in", "fox_f_bias", "fox_out_norm", "gdn_conv_w", "gdn_a_log", "gdn_dt_bias",
                "gdn_out_norm", "w_out", "post_mix_norm", "pre_mlp_norm", "w_up", "w_down", "post_mlp_norm")


def kernel(x, pre_mix_norm, w_in, fox_f_bias, fox_out_norm, gdn_conv_w, gdn_a_log, gdn_dt_bias, gdn_out_norm, w_out, post_mix_norm, pre_mlp_norm, w_up, w_down, post_mlp_norm, loss_target, m_pre_mix_norm, m_w_in, m_fox_f_bias, m_fox_out_norm, m_gdn_conv_w, m_gdn_a_log, m_gdn_dt_bias, m_gdn_out_norm, m_w_out, m_post_mix_norm, m_pre_mlp_norm, m_w_up, m_w_down, m_post_mlp_norm, v_pre_mix_norm, v_w_in, v_fox_f_bias, v_fox_out_norm, v_gdn_conv_w, v_gdn_a_log, v_gdn_dt_bias, v_gdn_out_norm, v_w_out, v_post_mix_norm, v_pre_mlp_norm, v_w_up, v_w_down, v_post_mlp_norm):
    w = dict(pre_mix_norm=pre_mix_norm, w_in=w_in, fox_f_bias=fox_f_bias, fox_out_norm=fox_out_norm,
             gdn_conv_w=gdn_conv_w, gdn_a_log=gdn_a_log, gdn_dt_bias=gdn_dt_bias, gdn_out_norm=gdn_out_norm, w_out=w_out,
             post_mix_norm=post_mix_norm, pre_mlp_norm=pre_mlp_norm, w_up=w_up, w_down=w_down, post_mlp_norm=post_mlp_norm)
    mom = dict(pre_mix_norm=m_pre_mix_norm, w_in=m_w_in, fox_f_bias=m_fox_f_bias, fox_out_norm=m_fox_out_norm,
               gdn_conv_w=m_gdn_conv_w, gdn_a_log=m_gdn_a_log, gdn_dt_bias=m_gdn_dt_bias, gdn_out_norm=m_gdn_out_norm,
               w_out=m_w_out, post_mix_norm=m_post_mix_norm, pre_mlp_norm=m_pre_mlp_norm, w_up=m_w_up, w_down=m_w_down,
               post_mlp_norm=m_post_mlp_norm)
    var = dict(pre_mix_norm=v_pre_mix_norm, w_in=v_w_in, fox_f_bias=v_fox_f_bias, fox_out_norm=v_fox_out_norm,
               gdn_conv_w=v_gdn_conv_w, gdn_a_log=v_gdn_a_log, gdn_dt_bias=v_gdn_dt_bias, gdn_out_norm=v_gdn_out_norm,
               w_out=v_w_out, post_mix_norm=v_post_mix_norm, pre_mlp_norm=v_pre_mlp_norm, w_up=v_w_up, w_down=v_w_down,
               post_mlp_norm=v_post_mlp_norm)

    win_g, wout_g, conv_g = _all_gather([w_in.T.astype(BF), w_out.astype(BF), gdn_conv_w])
    mlp_shards = [w_up.astype(BF), w_down.astype(BF)]
    ag_sems, mlp_shards, mlp_zones, ag_token = _exchange_start(
        mlp_shards, _zones_with_own(mlp_shards, False, "gather_mlp_own", after=win_g), False, "gather_mlp_start")
    wt_al = _to_aligned_rows(win_g.reshape(D_PROJ, D_MODEL))
    w_out_full = wout_g.reshape(D_MODEL, D_MODEL)
    convw = _cols_from_pieces(conv_g)

    def mlp_weights(after):
        up_g, down_g = _exchange_wait(ag_sems, mlp_shards, mlp_zones, after, "gather_mlp_wait")
        return up_g, down_g.reshape(D_FF, D_MODEL)

    def on_mlp_grads(g_up, g_down):
        srcs = [g_up, g_down]
        sems, srcs, zones, token = _exchange_start(srcs, _zones_with_own(srcs, True, "scatter_mlp_own"), True,
                                                   "scatter_mlp_start")
        return (sems, srcs, zones), token

    loss, grad_x, g_in, g_out, mlp_handle, small = _local_step(
        x[0], loss_target[0], wt_al, w_out_full, mlp_weights, on_mlp_grads, convw, pre_mix_norm + ag_token[0, 0],
        fox_f_bias, fox_out_norm, gdn_a_log, gdn_dt_bias, gdn_out_norm, post_mix_norm, pre_mlp_norm, post_mlp_norm)
    loss = lax.psum(loss, ("x", "y", "c"))

    grads, delta, new_m, new_v = {}, {}, {}, {}
    big = dict(w_out=g_out, w_in=_from_aligned_rows(g_in).reshape(N_DEV, D_PROJ // N_DEV, D_MODEL))
    for n, g in big.items():
        if n == "w_in":
            res = _reduce_scatter_adamw(g, w[n].T, mom[n].T, var[n].T, "reduce_scatter_" + n)
            grads[n], delta[n], new_m[n], new_v[n] = [r.T for r in res]
        else:
            grads[n], delta[n], new_m[n], new_v[n] = _reduce_scatter_adamw(g, w[n], mom[n], var[n], "reduce_scatter_" + n)
    zone_up, zone_down = _exchange_wait(*mlp_handle, grads["w_in"], "scatter_mlp_wait")
    for n, zone in (("w_up", zone_up), ("w_down", zone_down)):
        grads[n], delta[n], new_m[n], new_v[n] = _sum_adamw(zone, w[n], mom[n], var[n], "adamw_" + n)

    dwq, dwk, dwv = small.pop("gdn_conv_w")
    total = _unpack_small(_all_reduce_small(_pack_small(small, jnp.concatenate([dwq, dwk, dwv], axis=1))),
                          (CONV_K, 3 * D_GDN))
    me = 4 * lax.axis_index("x") + 2 * lax.axis_index("y") + lax.axis_index("c")
    n_conv = gdn_conv_w.shape[1]
    total["gdn_conv_w"] = lax.dynamic_slice_in_dim(total["gdn_conv_w"], me * n_conv, n_conv, axis=1)
    grads.update(total)
    d_s, m_s, v_s = _adamw_small(_pack_small(w, gdn_conv_w), _pack_small(total, total["gdn_conv_w"]),
                                 _pack_small(mom, m_gdn_conv_w), _pack_small(var, v_gdn_conv_w))
    delta.update(_unpack_small(d_s, gdn_conv_w.shape))
    new_m.update(_unpack_small(m_s, gdn_conv_w.shape))
    new_v.update(_unpack_small(v_s, gdn_conv_w.shape))

    return (loss, grad_x[None], *[grads[n] for n in WEIGHT_ORDER], *[delta[n] for n in WEIGHT_ORDER],
            *[new_m[n] for n in WEIGHT_ORDER], *[new_v[n] for n in WEIGHT_ORDER])
```

```python
import jax
import jax.numpy as jnp
from jax import lax
from jax.experimental import pallas as pl
from jax.experimental.pallas import tpu as pltpu

F32 = jnp.float32
BF = jnp.bfloat16

D_MODEL = 1024
N_FOX_HEADS, FOX_HEAD_DIM = 8, 64
N_GDN_HEADS, GDN_HEAD_DIM = 4, 128
D_FOX = N_FOX_HEADS * FOX_HEAD_DIM
D_GDN = N_GDN_HEADS * GDN_HEAD_DIM
CHUNK = 64
CONV_K = 4
D_FF = 4 * D_MODEL
EPS = 1e-6
D_PROJ = 3600
N_DEV = 8

PROJ_W = 3712
COL_FOX, COL_GDN, COL_GZ, COL_SMALL = 0, 1536, 3072, 3584
LANES = 128
SM_FF, SM_GB, SM_GA = 0, 8, 12

ADAM_LR, ADAM_B1, ADAM_B2, ADAM_EPS, ADAM_WD, ADAM_STEP = 0.001, 0.9, 0.999, 1e-08, 0.01, 10

TOKEN_BLOCK = 256
FOX_SCALE = FOX_HEAD_DIM ** -0.5
GDN_QSCALE = GDN_HEAD_DIM ** -0.5
NEG_BIG = -1e30
VMEM_LIMIT = 56 * 1024 * 1024

VMEM_SPEC = pl.BlockSpec(memory_space=pltpu.VMEM)
HIGHEST = lax.Precision.HIGHEST


def _sds(shape, dtype=F32):
    return jax.ShapeDtypeStruct(shape, dtype)


def _params(*sem):
    return pltpu.CompilerParams(dimension_semantics=sem if sem else None, vmem_limit_bytes=VMEM_LIMIT)


def _mm(a, b):
    return jnp.dot(a.astype(BF), b.astype(BF), preferred_element_type=F32)


def _mm_nt(a, b):
    return lax.dot_general(a.astype(BF), b.astype(BF), (((1,), (1,)), ((), ())), preferred_element_type=F32)


def _mm_tn(a, b):
    return lax.dot_general(a.astype(BF), b.astype(BF), (((0,), (0,)), ((), ())), preferred_element_type=F32)


def _mm_exact(a, b):
    return jnp.dot(a, b, precision=HIGHEST, preferred_element_type=F32)


def _mm_tn_exact(a, b):
    return lax.dot_general(a, b, (((0,), (0,)), ((), ())), precision=HIGHEST, preferred_element_type=F32)


def _sigmoid(x):
    return 1.0 / (1.0 + jnp.exp(-x))


def _softplus(x):
    return jnp.maximum(x, 0.0) + jnp.log1p(jnp.exp(-jnp.abs(x)))


def _iota(shape, dim):
    return lax.broadcasted_iota(jnp.int32, shape, dim)


def _shift_down(x, s, row):
    return jnp.where(row >= s, pltpu.roll(x, s, 0), 0.0)


def _shift_up(x, s, row):
    n = x.shape[0]
    return jnp.where(row < n - s, pltpu.roll(x, n - s, 0), 0.0)


def _norm_proj(x, nw, wt_al):
    t = x.shape[0]

    def body(x_ref, nw_ref, w_ref, proj_ref, h_ref):
        xv = x_ref[...]
        r = lax.rsqrt(jnp.mean(xv * xv, axis=-1, keepdims=True) + EPS)
        h = (xv * r * nw_ref[...]).astype(BF)
        h_ref[...] = h
        proj_ref[...] = lax.dot_general(h, w_ref[...], (((1,), (1,)), ((), ())), preferred_element_type=F32)

    tm = TOKEN_BLOCK
    return pl.pallas_call(
        body, name="norm_proj", grid=(t // tm,),
        in_specs=[pl.BlockSpec((tm, D_MODEL), lambda i: (i, 0)), pl.BlockSpec((1, D_MODEL), lambda i: (0, 0)),
                  pl.BlockSpec((PROJ_W, D_MODEL), lambda i: (0, 0))],
        out_specs=[pl.BlockSpec((tm, PROJ_W), lambda i: (i, 0)), pl.BlockSpec((tm, D_MODEL), lambda i: (i, 0))],
        out_shape=[_sds((t, PROJ_W)), _sds((t, D_MODEL), BF)],
        compiler_params=_params("parallel"),
    )(x, nw, wt_al)


def _expand_matrix(first_row, group):
    row = _iota((LANES, 512), 0)
    col = _iota((LANES, 512), 1)
    return (col // group + first_row == row).astype(F32)


def _small_prep(proj, fb, al, dtb):
    t = proj.shape[0]

    def body(sm_ref, fb_ref, al_ref, dtb_ref, ce_ref, cumt_ref, be_ref, ge_ref):
        s = sm_ref[...]
        z = s + fb_ref[...]
        cum = jnp.minimum(z, 0.0) - jnp.log1p(jnp.exp(-jnp.abs(z)))
        row = _iota((t, LANES), 0)
        step = 1
        while step < t:
            cum = cum + _shift_down(cum, step, row)
            step *= 2
        cumt_ref[...] = cum.T
        ce_ref[...] = _mm_exact(cum, _expand_matrix(SM_FF, FOX_HEAD_DIM))
        be_ref[...] = _mm_exact(_sigmoid(s), _expand_matrix(SM_GB, GDN_HEAD_DIM))
        g = -jnp.exp(al_ref[...]) * _softplus(s + dtb_ref[...])
        ge_ref[...] = _mm_exact(g, _expand_matrix(SM_GA, GDN_HEAD_DIM))

    vec = pl.BlockSpec((1, LANES), lambda i: (0, 0))
    return pl.pallas_call(
        body, name="small_prep", grid=(1,),
        in_specs=[pl.BlockSpec((t, LANES), lambda i: (0, COL_SMALL // LANES)), vec, vec, vec],
        out_specs=[pl.BlockSpec((t, 512), lambda i: (0, 0)), pl.BlockSpec((LANES, t), lambda i: (0, 0)),
                   pl.BlockSpec((t, 512), lambda i: (0, 0)), pl.BlockSpec((t, 512), lambda i: (0, 0))],
        out_shape=[_sds((t, 512)), _sds((LANES, t)), _sds((t, 512)), _sds((t, 512))],
        compiler_params=_params("arbitrary"),
    )(proj, fb, al, dtb)


def _fox_scores(qh, kb, ce_ref, cumt_ref, head, hh, i, tq):
    klen = (i + 1) * tq
    s = _mm_nt(qh, kb[:klen]) * FOX_SCALE
    cq = ce_ref[i * tq:(i + 1) * tq, FOX_HEAD_DIM * hh:FOX_HEAD_DIM * hh + 1]
    ck = cumt_ref[pl.ds(head, 1), 0:klen]
    s = s + cq - ck
    qi = _iota((tq, klen), 0) + i * tq
    ki = _iota((tq, klen), 1)
    return jnp.where(ki <= qi, s, NEG_BIG)


def _fox_fwd(proj, ce, cumt, fnw):
    t = proj.shape[0]
    tq = min(TOKEN_BLOCK, t // 2)
    nq = t // tq

    def body(q_ref, k_ref, v_ref, ce_ref, cumt_ref, fnw_ref, o_ref, lse_ref, fn_ref):
        j = pl.program_id(0)
        first = _iota((1, LANES), 1) < FOX_HEAD_DIM
        kb = k_ref[...].astype(BF)
        vb = v_ref[...].astype(BF)
        for i in range(nq):
            rows = slice(i * tq, (i + 1) * tq)
            klen = (i + 1) * tq
            q_i = q_ref[rows, :]
            o_acc = jnp.zeros((tq, LANES), F32)
            lse_acc = jnp.zeros((tq, LANES), F32)
            for hh in range(2):
                mh = first if hh == 0 else jnp.logical_not(first)
                qh = jnp.where(mh, q_i, 0.0).astype(BF)
                s = _fox_scores(qh, kb, ce_ref, cumt_ref, 2 * j + hh, hh, i, tq)
                m = jnp.max(s, axis=-1, keepdims=True)
                p = jnp.exp(s - m)
                l = jnp.sum(p, axis=-1, keepdims=True)
                o = jnp.dot(p.astype(BF), vb[:klen], preferred_element_type=F32) / l
                o_acc = jnp.where(mh, o, o_acc)
                lse_acc = jnp.where(mh, m + jnp.log(l), lse_acc)
            o_ref[rows, :] = o_acc
            lse_ref[rows, :] = lse_acc
            o2 = o_acc * o_acc
            s0 = jnp.sum(jnp.where(first, o2, 0.0), axis=-1, keepdims=True)
            s1 = jnp.sum(jnp.where(first, 0.0, o2), axis=-1, keepdims=True)
            r = lax.rsqrt(jnp.where(first, s0, s1) * (1.0 / FOX_HEAD_DIM) + EPS)
            fn_ref[rows, :] = (o_acc * r * fnw_ref[...]).astype(BF)

    blk = lambda off: pl.BlockSpec((t, LANES), lambda j: (0, off + j))
    return pl.pallas_call(
        body, name="fox_fwd", grid=(N_FOX_HEADS // 2,),
        in_specs=[blk(0), blk(4), blk(8), blk(0), pl.BlockSpec((LANES, t), lambda j: (0, 0)),
                  pl.BlockSpec((1, LANES), lambda j: (0, 0))],
        out_specs=[blk(0), blk(0), blk(0)],
        out_shape=[_sds((t, D_FOX)), _sds((t, D_FOX)), _sds((t, D_FOX), BF)],
        compiler_params=_params("parallel"),
    )(proj, proj, proj, ce, cumt, fnw)


def _fox_bwd(proj, ce, cumt, lse, o, do):
    t = proj.shape[0]
    tq = min(TOKEN_BLOCK, t // 2)
    nq = t // tq

    def body(q_ref, k_ref, v_ref, ce_ref, cumt_ref, lse_ref, o_ref, do_ref,
             dq_ref, dk_ref, dv_ref, dcq_ref, dckt_ref, dk_s, dv_s, dck_s):
        j = pl.program_id(0)
        first = _iota((1, LANES), 1) < FOX_HEAD_DIM
        kf = k_ref[...]
        kb = kf.astype(BF)
        vb = v_ref[...].astype(BF)
        dk_s[...] = jnp.zeros_like(dk_s)
        dv_s[...] = jnp.zeros_like(dv_s)
        dck_s[...] = jnp.zeros_like(dck_s)
        masks = [first, jnp.logical_not(first)]
        kmask = [jnp.where(mh, kf, 0.0).astype(BF) for mh in masks]
        for i in range(nq):
            rows = slice(i * tq, (i + 1) * tq)
            klen = (i + 1) * tq
            q_i = q_ref[rows, :]
            do_i = do_ref[rows, :]
            o_i = o_ref[rows, :]
            lse_i = lse_ref[rows, :]
            dq_acc = jnp.zeros((tq, LANES), F32)
            dcq_acc = jnp.zeros((tq, LANES), F32)
            for hh in range(2):
                mh = masks[hh]
                qh = jnp.where(mh, q_i, 0.0).astype(BF)
                doh = jnp.where(mh, do_i, 0.0)
                dohb = doh.astype(BF)
                delta = jnp.sum(doh * o_i, axis=-1, keepdims=True)
                s = _fox_scores(qh, kb, ce_ref, cumt_ref, 2 * j + hh, hh, i, tq)
                p = jnp.exp(s - lse_i[:, FOX_HEAD_DIM * hh:FOX_HEAD_DIM * hh + 1])
                dp = _mm_nt(dohb, vb[:klen])
                ds = p * (dp - delta)
                dsb = ds.astype(BF)
                dq_acc = dq_acc + jnp.dot(dsb, kmask[hh][:klen], preferred_element_type=F32) * FOX_SCALE
                dk_s[0:klen, :] += _mm_tn(dsb, qh) * FOX_SCALE
                dv_s[0:klen, :] += _mm_tn(p, dohb)
                dcq_acc = jnp.where(mh, jnp.sum(ds, axis=-1, keepdims=True), dcq_acc)
                dck_s[hh:hh + 1, 0:klen] += jnp.sum(ds, axis=0, keepdims=True)
            dq_ref[rows, :] = dq_acc
            dcq_ref[rows, :] = dcq_acc
        dk_ref[...] = dk_s[...]
        dv_ref[...] = dv_s[...]
        dckt_ref[...] = jnp.zeros_like(dckt_ref)
        dckt_ref[0:8, :] = dck_s[...]

    blk = lambda off: pl.BlockSpec((t, LANES), lambda j: (0, off + j))
    return pl.pallas_call(
        body, name="fox_bwd", grid=(N_FOX_HEADS // 2,),
        in_specs=[blk(0), blk(4), blk(8), blk(0), pl.BlockSpec((LANES, t), lambda j: (0, 0)), blk(0), blk(0), blk(0)],
        out_specs=[blk(0), blk(0), blk(0), blk(0), pl.BlockSpec((32, t), lambda j: (j, 0))],
        out_shape=[_sds((t, D_FOX))] * 4 + [_sds((LANES, t))],
        scratch_shapes=[pltpu.VMEM((t, LANES), F32), pltpu.VMEM((t, LANES), F32), pltpu.VMEM((8, t), F32)],
        compiler_params=_params("parallel"),
    )(proj, proj, proj, ce, cumt, lse, o, do)


def _conv(x, w, row):
    return (w[3:4, :] * x + w[2:3, :] * _shift_down(x, 1, row) + w[1:2, :] * _shift_down(x, 2, row)
            + w[0:1, :] * _shift_down(x, 3, row))


def _chunk_decay(gc_c):
    gi = gc_c[:, 0:CHUNK]
    gj = gc_c.T[0:CHUNK, :]
    ri = _iota((CHUNK, CHUNK), 0)
    cj = _iota((CHUNK, CHUNK), 1)
    return jnp.where(ri >= cj, jnp.exp(jnp.minimum(gi - gj, 0.0)), 0.0), ri > cj


def _gdn_specs(t):
    col = lambda off: pl.BlockSpec((t, LANES), lambda h: (0, off + h))
    cw = lambda off: pl.BlockSpec((CONV_K, LANES), lambda h: (0, off + h))
    mat = pl.BlockSpec((1, t // CHUNK, CHUNK, CHUNK), lambda h: (h, 0, 0, 0))
    return col, cw, mat


def _gdn_prep(proj, convw, be, ge):
    t = proj.shape[0]
    nch = t // CHUNK

    def body(xq_ref, xk_ref, xv_ref, wq_ref, wk_ref, wv_ref, be_ref, ge_ref,
             qn_ref, kn_ref, cv_ref, gc_ref, m_ref, a_ref):
        row = _iota((t, LANES), 0)

        def act(x_ref, w_ref):
            y = _conv(x_ref[...], w_ref[...], row)
            return y * _sigmoid(y)

        cq = act(xq_ref, wq_ref)
        ck = act(xk_ref, wk_ref)
        cv_ref[...] = act(xv_ref, wv_ref)
        qn_ref[...] = cq * lax.rsqrt(jnp.sum(cq * cq, axis=-1, keepdims=True) + EPS) * GDN_QSCALE
        kn_ref[...] = ck * lax.rsqrt(jnp.sum(ck * ck, axis=-1, keepdims=True) + EPS)
        gc = ge_ref[...]
        pos = row % CHUNK
        step = 1
        while step < CHUNK:
            gc = gc + jnp.where(pos >= step, pltpu.roll(gc, step, 0), 0.0)
            step *= 2
        gc_ref[...] = gc

        def chunk(n, carry):
            sl = pl.ds(pl.multiple_of(n * CHUNK, CHUNK), CHUNK)
            k_c = kn_ref[sl, :]
            decay, strict = _chunk_decay(gc_ref[sl, :])
            m_ref[0, n] = jnp.where(strict, _mm_nt(k_c * be_ref[sl, :], k_c) * decay, 0.0)
            a_ref[0, n] = _mm_nt(qn_ref[sl, :], k_c) * decay
            return carry

        lax.fori_loop(0, nch, chunk, 0)

    col, cw, mat = _gdn_specs(t)
    return pl.pallas_call(
        body, name="gdn_prep", grid=(N_GDN_HEADS,),
        in_specs=[col(12), col(16), col(20), cw(0), cw(4), cw(8), col(0), col(0)],
        out_specs=[col(0), col(0), col(0), col(0), mat, mat],
        out_shape=[_sds((t, D_GDN))] * 4 + [_sds((N_GDN_HEADS, nch, CHUNK, CHUNK))] * 2,
        compiler_params=_params("parallel"),
    )(proj, proj, proj, convw, convw, convw, be, ge)


def _tri_inverse(m2):
    n_prob = m2.shape[0]
    assert n_prob == LANES
    nb = CHUNK * CHUNK // LANES

    def body(m_ref, t_ref, ms, ts):
        for b in range(nb):
            ms[b * LANES:(b + 1) * LANES, :] = m_ref[:, b * LANES:(b + 1) * LANES].T
        cidx = _iota((CHUNK, LANES), 0)

        def outer(i, carry):
            def inner(jj, acc):
                mrow = ms[pl.ds(i * CHUNK + jj, 1), :]
                return acc - mrow * ts[pl.ds(pl.multiple_of(jj * CHUNK, CHUNK), CHUNK), :]

            acc = lax.fori_loop(0, i, inner, jnp.where(cidx == i, 1.0, 0.0).astype(F32))
            ts[pl.ds(pl.multiple_of(i * CHUNK, CHUNK), CHUNK), :] = acc
            return carry

        lax.fori_loop(0, CHUNK, outer, 0)
        for b in range(nb):
            t_ref[:, b * LANES:(b + 1) * LANES] = ts[b * LANES:(b + 1) * LANES, :].T

    return pl.pallas_call(
        body, name="tri_inverse", in_specs=[VMEM_SPEC], out_specs=VMEM_SPEC,
        out_shape=_sds((LANES, CHUNK * CHUNK)),
        scratch_shapes=[pltpu.VMEM((CHUNK * CHUNK, LANES), F32), pltpu.VMEM((CHUNK * CHUNK, LANES), F32)],
        compiler_params=_params(),
    )(m2)


def _gdn_chunk_terms(q, k, v, b, gcc):
    eg = jnp.exp(gcc)
    last = gcc[CHUNK - 1:CHUNK, :]
    egl = jnp.exp(last - gcc)
    gl = jnp.exp(last)
    kb = k * b
    return eg, egl, gl, kb, v * b, kb * eg, q * eg, k * egl


GDN_BLOCK_CHUNKS = 4


def _gdn_block_specs(t, reverse):
    cb = GDN_BLOCK_CHUNKS
    nb = t // (cb * CHUNK)
    idx = (lambda i: nb - 1 - i) if reverse else (lambda i: i)
    tok = pl.BlockSpec((cb * CHUNK, D_GDN), lambda i: (idx(i), 0))
    mat = pl.BlockSpec((N_GDN_HEADS, cb, CHUNK, CHUNK), lambda i: (0, idx(i), 0, 0))
    state = pl.BlockSpec((N_GDN_HEADS, cb, GDN_HEAD_DIM, GDN_HEAD_DIM), lambda i: (0, idx(i), 0, 0))
    return nb, tok, mat, state


def _gdn_scan(qn, kn, cv, be, gc, tinv, amat):
    t = qn.shape[0]
    nch = t // CHUNK

    def body(q_ref, k_ref, v_ref, b_ref, gc_ref, t_ref, a_ref, o_ref, sall_ref, vn_ref, s_scr):
        @pl.when(pl.program_id(0) == 0)
        def _():
            s_scr[...] = jnp.zeros_like(s_scr)

        for hd in range(N_GDN_HEADS):
            cs = slice(hd * LANES, (hd + 1) * LANES)
            s = s_scr[hd]
            for cc in range(GDN_BLOCK_CHUNKS):
                rs = slice(cc * CHUNK, (cc + 1) * CHUNK)
                eg, egl, gl, kb, vb, kbg, qd, kd = _gdn_chunk_terms(q_ref[rs, cs], k_ref[rs, cs], v_ref[rs, cs],
                                                                    b_ref[rs, cs], gc_ref[rs, cs])
                sall_ref[hd, cc] = s
                uw = _mm(t_ref[hd, cc], jnp.concatenate([vb, kbg], axis=1))
                ws_qs = _mm(jnp.concatenate([uw[:, LANES:], qd], axis=0), s)
                vn = uw[:, :LANES] - ws_qs[:CHUNK]
                vn_ref[rs, cs] = vn
                o_ref[rs, cs] = ws_qs[CHUNK:] + _mm(a_ref[hd, cc], vn)
                s = s * gl + _mm_tn(kd, vn)
            s_scr[hd] = s

    nb, tok, mat, state = _gdn_block_specs(t, False)
    return pl.pallas_call(
        body, name="gdn_scan", grid=(nb,),
        in_specs=[tok] * 5 + [mat, mat], out_specs=[tok, state, tok],
        out_shape=[_sds((t, D_GDN)), _sds((N_GDN_HEADS, nch, GDN_HEAD_DIM, GDN_HEAD_DIM)), _sds((t, D_GDN))],
        scratch_shapes=[pltpu.VMEM((N_GDN_HEADS, GDN_HEAD_DIM, GDN_HEAD_DIM), F32)],
        compiler_params=_params("arbitrary"),
    )(qn, kn, cv, be, gc, tinv, amat)


def _gdn_bwd(qn, kn, cv, be, gc, tinv, amat, s_all, vn_all, do):
    t = qn.shape[0]

    def body(q_ref, k_ref, v_ref, b_ref, gc_ref, t_ref, a_ref, sall_ref, vn_ref, do_ref,
             dq_ref, dk_ref, dv_ref, db_ref, dg_ref, ds_scr):
        @pl.when(pl.program_id(0) == 0)
        def _():
            ds_scr[...] = jnp.zeros_like(ds_scr)

        lastrow = _iota((CHUNK, LANES), 0) == CHUNK - 1
        for hd in range(N_GDN_HEADS):
            cs = slice(hd * LANES, (hd + 1) * LANES)
            dsp = ds_scr[hd]
            for cc in reversed(range(GDN_BLOCK_CHUNKS)):
                rs = slice(cc * CHUNK, (cc + 1) * CHUNK)
                q, k, v, b, gcc = q_ref[rs, cs], k_ref[rs, cs], v_ref[rs, cs], b_ref[rs, cs], gc_ref[rs, cs]
                eg, egl, gl, kb, vb, kbg, qd, kd = _gdn_chunk_terms(q, k, v, b, gcc)
                do_c = do_ref[rs, cs]
                tn = t_ref[hd, cc]
                an = a_ref[hd, cc]
                s = sall_ref[hd, cc]
                vn = vn_ref[rs, cs]
                w = _mm(tn, kbg)
                dvn = _mm_tn(an, do_c) + _mm(kd, dsp)
                do_dvn = jnp.concatenate([do_c, dvn], axis=0)
                by_s = _mm_nt(do_dvn, s)
                dqd, dw = by_s[:CHUNK], -by_s[CHUNK:]
                da = _mm_nt(do_c, vn)
                dkd = _mm_nt(vn, dsp)
                dgl = jnp.sum(jnp.sum(dsp * s, axis=-1, keepdims=True), axis=0, keepdims=True)
                dsp = _mm_tn(jnp.concatenate([qd, -w], axis=0), do_dvn) + gl * dsp
                dvn_dw = jnp.concatenate([dvn, dw], axis=1)
                dt = _mm_nt(dvn_dw, jnp.concatenate([vb, kbg], axis=1))
                by_t = _mm_tn(tn, dvn_dw)
                dvb, dkbg = by_t[:, :LANES], by_t[:, LANES:]
                decay, strict = _chunk_decay(gcc)
                by_k = _mm_nt(jnp.concatenate([kb, q], axis=0), k)
                kk, qk = by_k[:CHUNK], by_k[CHUNK:]
                dm = jnp.where(strict, -_mm_nt(_mm_tn(tn, dt), tn), 0.0)
                dkk = dm * decay
                dqk = da * decay
                gmat = dkk * kk + dqk * qk
                dqk_dkk = jnp.concatenate([dqk, dkk], axis=0)
                on_k = _mm(dqk_dkk, k)
                dq_ref[rs, cs] = dqd * eg + on_k[:CHUNK]
                dkb = on_k[CHUNK:] + dkbg * eg
                dk_ref[rs, cs] = dkd * egl + _mm_tn(dqk_dkk, jnp.concatenate([q, kb], axis=0)) + dkb * b
                db = jnp.sum(dkb * k, axis=-1, keepdims=True) + jnp.sum(dvb * v, axis=-1, keepdims=True)
                db_ref[rs, cs] = jnp.broadcast_to(db, (CHUNK, LANES))
                dv_ref[rs, cs] = dvb * b
                dkd_kd = jnp.sum(dkd * kd, axis=-1, keepdims=True)
                col_sums = jnp.sum(jnp.concatenate([gmat, jnp.zeros_like(gmat)], axis=1).T, axis=-1, keepdims=True)
                dgc = (jnp.sum(gmat, axis=-1, keepdims=True) - col_sums[:CHUNK]
                       + jnp.sum(dqd * qd, axis=-1, keepdims=True) + jnp.sum(dkbg * kbg, axis=-1, keepdims=True) - dkd_kd)
                extra = jnp.sum(dkd_kd, axis=0, keepdims=True) + dgl * gl
                dg_ref[rs, cs] = dgc + jnp.where(lastrow, extra, 0.0)
            ds_scr[hd] = dsp
        dg = dg_ref[...]
        row = _iota(dg.shape, 0)
        pos = row % CHUNK
        step = 1
        while step < CHUNK:
            dg = dg + jnp.where(pos < CHUNK - step, pltpu.roll(dg, dg.shape[0] - step, 0), 0.0)
            step *= 2
        dg_ref[...] = dg

    nb, tok, mat, state = _gdn_block_specs(t, True)
    return pl.pallas_call(
        body, name="gdn_bwd", grid=(nb,),
        in_specs=[tok] * 5 + [mat, mat, state, tok, tok], out_specs=[tok] * 5, out_shape=[_sds((t, D_GDN))] * 5,
        scratch_shapes=[pltpu.VMEM((N_GDN_HEADS, GDN_HEAD_DIM, GDN_HEAD_DIM), F32)],
        compiler_params=_params("arbitrary"),
    )(qn, kn, cv, be, gc, tinv, amat, s_all, vn_all, do)


def _gdn_bwd_conv(proj, convw, dqn, dkn, dcv):
    t = proj.shape[0]

    def body(xq_ref, xk_ref, xv_ref, wq_ref, wk_ref, wv_ref, dq_ref, dk_ref, dv_ref,
             dxq_ref, dxk_ref, dxv_ref, dwq_ref, dwk_ref, dwv_ref):
        row = _iota((t, LANES), 0)

        def one(x_ref, w_ref, d_ref, dx_ref, dw_ref, scale):
            x = x_ref[...]
            w = w_ref[...]
            y = _conv(x, w, row)
            sg = _sigmoid(y)
            dc = d_ref[...]
            if scale is not None:
                c = y * sg
                r = lax.rsqrt(jnp.sum(c * c, axis=-1, keepdims=True) + EPS)
                ch = c * r
                dc = scale * r * (dc - ch * jnp.sum(dc * ch, axis=-1, keepdims=True))
            dy = dc * sg * (1.0 + y * (1.0 - sg))
            dx_ref[...] = (w[3:4, :] * dy + w[2:3, :] * _shift_up(dy, 1, row) + w[1:2, :] * _shift_up(dy, 2, row)
                           + w[0:1, :] * _shift_up(dy, 3, row))
            for jj in range(CONV_K):
                xs = x if jj == CONV_K - 1 else _shift_down(x, CONV_K - 1 - jj, row)
                dw_ref[jj:jj + 1, :] = jnp.sum(dy * xs, axis=0, keepdims=True)

        one(xq_ref, wq_ref, dq_ref, dxq_ref, dwq_ref, GDN_QSCALE)
        one(xk_ref, wk_ref, dk_ref, dxk_ref, dwk_ref, 1.0)
        one(xv_ref, wv_ref, dv_ref, dxv_ref, dwv_ref, None)

    col, cw, _ = _gdn_specs(t)
    return pl.pallas_call(
        body, name="gdn_bwd_conv", grid=(N_GDN_HEADS,),
        in_specs=[col(12), col(16), col(20), cw(0), cw(4), cw(8), col(0), col(0), col(0)],
        out_specs=[col(0), col(0), col(0), cw(0), cw(0), cw(0)],
        out_shape=[_sds((t, D_GDN))] * 3 + [_sds((CONV_K, D_GDN))] * 3,
        compiler_params=_params("parallel"),
    )(proj, proj, proj, convw, convw, convw, dqn, dkn, dcv)


def _mix_out(fox_n, gdn_o, proj, gnw, w_out, x, pmw, plw):
    t = x.shape[0]
    tm = TOKEN_BLOCK

    def body(fn_ref, go_ref, gz_ref, gnw_ref, w_ref, x_ref, pmw_ref, plw_ref, x1_ref, h2_ref, mixed_ref, omix_ref,
             h2t_ref):
        omix_ref[:, 0:D_FOX] = fn_ref[...]
        for hd in range(N_GDN_HEADS):
            cs = slice(hd * LANES, (hd + 1) * LANES)
            go = go_ref[:, cs]
            r = lax.rsqrt(jnp.mean(go * go, axis=-1, keepdims=True) + EPS)
            gz = gz_ref[:, cs]
            omix_ref[:, D_FOX + hd * LANES:D_FOX + (hd + 1) * LANES] = (
                go * r * gnw_ref[...] * (gz * _sigmoid(gz))).astype(BF)
        mixed = jnp.dot(omix_ref[...], w_ref[...], preferred_element_type=F32)
        mixed_ref[...] = mixed
        r2 = lax.rsqrt(jnp.mean(mixed * mixed, axis=-1, keepdims=True) + EPS)
        x1 = x_ref[...] + mixed * r2 * pmw_ref[...]
        x1_ref[...] = x1
        r3 = lax.rsqrt(jnp.mean(x1 * x1, axis=-1, keepdims=True) + EPS)
        h2 = x1 * r3 * plw_ref[...]
        h2_ref[...] = h2.astype(BF)
        h2t_ref[...] = h2.T.astype(BF)

    tok = lambda w: pl.BlockSpec((tm, w), lambda i: (i, 0))
    vec = lambda w: pl.BlockSpec((1, w), lambda i: (0, 0))
    return pl.pallas_call(
        body, name="mix_out", grid=(t // tm,),
        in_specs=[tok(D_FOX), tok(D_GDN), pl.BlockSpec((tm, D_GDN), lambda i: (i, COL_GZ // D_GDN)), vec(LANES),
                  pl.BlockSpec((D_MODEL, D_MODEL), lambda i: (0, 0)), tok(D_MODEL), vec(D_MODEL), vec(D_MODEL)],
        out_specs=[tok(D_MODEL)] * 4 + [pl.BlockSpec((D_MODEL, tm), lambda i: (0, i))],
        out_shape=[_sds((t, D_MODEL)), _sds((t, D_MODEL), BF), _sds((t, D_MODEL)), _sds((t, D_MODEL), BF),
                   _sds((D_MODEL, t), BF)],
        compiler_params=_params("parallel"),
    )(fox_n, gdn_o, proj, gnw, w_out, x, pmw, plw)


def _out_bwd(dmixed, w_out, o_fox, gdn_o, proj, fnw, gnw):
    t = dmixed.shape[0]
    tm = TOKEN_BLOCK

    def body(dm_ref, w_ref, of_ref, go_ref, gz_ref, fnw_ref, gnw_ref, dof_ref, dgo_ref, dgz_ref, dfw_ref, dgw_ref):
        i = pl.program_id(0)

        @pl.when(i == 0)
        def _():
            dfw_ref[...] = jnp.zeros_like(dfw_ref)
            dgw_ref[...] = jnp.zeros_like(dgw_ref)

        domix = _mm_nt(dm_ref[...], w_ref[...])
        first = _iota((1, LANES), 1) < FOX_HEAD_DIM
        dfw = jnp.zeros((1, LANES), F32)
        dgw = jnp.zeros((1, LANES), F32)
        for pr in range(N_FOX_HEADS // 2):
            cs = slice(pr * LANES, (pr + 1) * LANES)
            o = of_ref[:, cs]
            dfn = domix[:, cs]
            o2 = o * o
            s0 = jnp.sum(jnp.where(first, o2, 0.0), axis=-1, keepdims=True)
            s1 = jnp.sum(jnp.where(first, 0.0, o2), axis=-1, keepdims=True)
            r = lax.rsqrt(jnp.where(first, s0, s1) * (1.0 / FOX_HEAD_DIM) + EPS)
            oh = o * r
            dfw = dfw + jnp.sum(dfn * oh, axis=0, keepdims=True)
            doh = dfn * fnw_ref[...]
            pr_ = doh * oh
            m0 = jnp.sum(jnp.where(first, pr_, 0.0), axis=-1, keepdims=True)
            m1 = jnp.sum(jnp.where(first, 0.0, pr_), axis=-1, keepdims=True)
            dof_ref[:, cs] = r * (doh - oh * jnp.where(first, m0, m1) * (1.0 / FOX_HEAD_DIM))
        for hd in range(N_GDN_HEADS):
            cs = slice(hd * LANES, (hd + 1) * LANES)
            go = go_ref[:, cs]
            gz = gz_ref[:, cs]
            dgated = domix[:, D_FOX + hd * LANES:D_FOX + (hd + 1) * LANES]
            r = lax.rsqrt(jnp.mean(go * go, axis=-1, keepdims=True) + EPS)
            goh = go * r
            sg = _sigmoid(gz)
            sz = gz * sg
            gn = goh * gnw_ref[...]
            dgn = dgated * sz
            dgz_ref[:, cs] = dgated * gn * sg * (1.0 + gz * (1.0 - sg))
            dgw = dgw + jnp.sum(dgn * goh, axis=0, keepdims=True)
            dgh = dgn * gnw_ref[...]
            dgo_ref[:, cs] = r * (dgh - goh * jnp.mean(dgh * goh, axis=-1, keepdims=True))
        dfw_ref[...] += dfw + pltpu.roll(dfw, FOX_HEAD_DIM, 1)
        dgw_ref[...] += dgw

    tok = lambda w: pl.BlockSpec((tm, w), lambda i: (i, 0))
    vec = lambda w: pl.BlockSpec((1, w), lambda i: (0, 0))
    return pl.pallas_call(
        body, name="out_bwd", grid=(t // tm,),
        in_specs=[tok(D_MODEL), pl.BlockSpec((D_MODEL, D_MODEL), lambda i: (0, 0)), tok(D_FOX), tok(D_GDN),
                  pl.BlockSpec((tm, D_GDN), lambda i: (i, COL_GZ // D_GDN)), vec(LANES), vec(LANES)],
        out_specs=[tok(D_FOX), tok(D_GDN), tok(D_GDN), vec(LANES), vec(LANES)],
        out_shape=[_sds((t, D_FOX)), _sds((t, D_GDN)), _sds((t, D_GDN)), _sds((1, LANES)), _sds((1, LANES))],
        compiler_params=_params("arbitrary"),
    )(dmixed, w_out, o_fox, gdn_o, proj, fnw, gnw)


def _mlp_up(h2, w_up):
    t = h2.shape[0]
    tm = TOKEN_BLOCK
    pc = D_FF // N_DEV

    def body(h_ref, w_ref, up_ref):
        h = h_ref[...]
        for p in range(N_DEV):
            up_ref[:, p * pc:(p + 1) * pc] = jnp.dot(h, w_ref[p], preferred_element_type=F32)

    return pl.pallas_call(
        body, name="mlp_up", grid=(t // tm,),
        in_specs=[pl.BlockSpec((tm, D_MODEL), lambda i: (i, 0)),
                  pl.BlockSpec((N_DEV, D_MODEL, pc), lambda i: (0, 0, 0))],
        out_specs=pl.BlockSpec((tm, D_FF), lambda i: (i, 0)), out_shape=_sds((t, D_FF)),
        compiler_params=_params("parallel"),
    )(h2, w_up)


def _mlp_down_loss(up, w_down, x1, pw, target):
    t = up.shape[0]
    tm = TOKEN_BLOCK

    def body(up_ref, w_ref, x1_ref, pw_ref, tg_ref, dy_ref, dx2_ref, loss_ref, dpw_ref):
        i = pl.program_id(0)

        @pl.when(i == 0)
        def _():
            loss_ref[...] = jnp.zeros_like(loss_ref)
            dpw_ref[...] = jnp.zeros_like(dpw_ref)

        u = jnp.maximum(up_ref[...], 0.0)
        y = jnp.dot((u * u).astype(BF), w_ref[...], preferred_element_type=F32)
        r = lax.rsqrt(jnp.mean(y * y, axis=-1, keepdims=True) + EPS)
        yh = y * r
        pw = pw_ref[...]
        err = x1_ref[...] + yh * pw - tg_ref[...]
        part = jnp.sum(jnp.sum(err * err, axis=-1, keepdims=True), axis=0, keepdims=True) * (0.5 / D_MODEL)
        loss_ref[...] += jnp.broadcast_to(part, loss_ref.shape)
        dx2 = err * (1.0 / D_MODEL)
        dx2_ref[...] = dx2
        dpw_ref[...] += jnp.sum(dx2 * yh, axis=0, keepdims=True)
        dyh = dx2 * pw
        dy_ref[...] = (r * (dyh - yh * jnp.mean(dyh * yh, axis=-1, keepdims=True))).astype(BF)

    tok = lambda w: pl.BlockSpec((tm, w), lambda i: (i, 0))
    vec = lambda w: pl.BlockSpec((1, w), lambda i: (0, 0))
    return pl.pallas_call(
        body, name="mlp_down_loss", grid=(t // tm,),
        in_specs=[tok(D_FF), pl.BlockSpec((D_FF, D_MODEL), lambda i: (0, 0)), tok(D_MODEL), vec(D_MODEL), tok(D_MODEL)],
        out_specs=[tok(D_MODEL), tok(D_MODEL), vec(LANES), vec(D_MODEL)],
        out_shape=[_sds((t, D_MODEL), BF), _sds((t, D_MODEL)), _sds((1, LANES)), _sds((1, D_MODEL))],
        compiler_params=_params("arbitrary"),
    )(up, w_down, x1, pw, target)


def _mlp_bwd_act(dy, w_down, up):
    t = dy.shape[0]
    tm = TOKEN_BLOCK

    def body(dy_ref, w_ref, up_ref, dup_ref):
        da = lax.dot_general(dy_ref[...], w_ref[...], (((1,), (1,)), ((), ())), preferred_element_type=F32)
        dup_ref[...] = (da * (2.0 * jnp.maximum(up_ref[...], 0.0))).astype(BF)

    return pl.pallas_call(
        body, name="mlp_bwd_act", grid=(t // tm,),
        in_specs=[pl.BlockSpec((tm, D_MODEL), lambda i: (i, 0)), pl.BlockSpec((D_FF, D_MODEL), lambda i: (0, 0)),
                  pl.BlockSpec((tm, D_FF), lambda i: (i, 0))],
        out_specs=pl.BlockSpec((tm, D_FF), lambda i: (i, 0)), out_shape=_sds((t, D_FF), BF),
        compiler_params=_params("parallel"),
    )(dy, w_down, up)


def _mlp_bwd_in(dup, w_up, x1, plw, dx2, mixed, pmw):
    t = dup.shape[0]
    tm = TOKEN_BLOCK

    def body(dup_ref, w_ref, x1_ref, plw_ref, dx2_ref, mx_ref, pmw_ref, dx1_ref, dmixed_ref, dplw_ref, dpmw_ref):
        i = pl.program_id(0)

        @pl.when(i == 0)
        def _():
            dplw_ref[...] = jnp.zeros_like(dplw_ref)
            dpmw_ref[...] = jnp.zeros_like(dpmw_ref)

        pc = D_FF // N_DEV
        dh = _mm_nt(dup_ref[:, 0:pc], w_ref[0])
        for p in range(1, N_DEV):
            dh = dh + _mm_nt(dup_ref[:, p * pc:(p + 1) * pc], w_ref[p])
        x1 = x1_ref[...]
        r = lax.rsqrt(jnp.mean(x1 * x1, axis=-1, keepdims=True) + EPS)
        xh = x1 * r
        dplw_ref[...] += jnp.sum(dh * xh, axis=0, keepdims=True)
        dxh = dh * plw_ref[...]
        dx1 = dx2_ref[...] + r * (dxh - xh * jnp.mean(dxh * xh, axis=-1, keepdims=True))
        dx1_ref[...] = dx1
        mx = mx_ref[...]
        r2 = lax.rsqrt(jnp.mean(mx * mx, axis=-1, keepdims=True) + EPS)
        mh = mx * r2
        dpmw_ref[...] += jnp.sum(dx1 * mh, axis=0, keepdims=True)
        dmh = dx1 * pmw_ref[...]
        dmixed_ref[...] = (r2 * (dmh - mh * jnp.mean(dmh * mh, axis=-1, keepdims=True))).astype(BF)

    tok = lambda w: pl.BlockSpec((tm, w), lambda i: (i, 0))
    vec = lambda w: pl.BlockSpec((1, w), lambda i: (0, 0))
    return pl.pallas_call(
        body, name="mlp_bwd_in", grid=(t // tm,),
        in_specs=[tok(D_FF), pl.BlockSpec((N_DEV, D_MODEL, D_FF // N_DEV), lambda i: (0, 0, 0)), tok(D_MODEL),
                  vec(D_MODEL), tok(D_MODEL), tok(D_MODEL), vec(D_MODEL)],
        out_specs=[tok(D_MODEL), tok(D_MODEL), vec(D_MODEL), vec(D_MODEL)],
        out_shape=[_sds((t, D_MODEL)), _sds((t, D_MODEL), BF), _sds((1, D_MODEL)), _sds((1, D_MODEL))],
        compiler_params=_params("arbitrary"),
    )(dup, w_up, x1, plw, dx2, mixed, pmw)


def _wgrad(a, b, a_cols, split=1, a_fn=None, a_block0=0, name="wgrad"):
    t, b_cols = b.shape
    n_a = (a.shape[1] - a_block0 * a_cols) // a_cols if a_block0 else a.shape[1] // a_cols

    def body(a_ref, b_ref, o_ref):
        av = a_ref[...]
        if a_fn is not None:
            av = a_fn(av)
        o_ref[...] = _mm_tn(av, b_ref[...]).astype(BF).reshape(o_ref.shape)

    return pl.pallas_call(
        body, name=name, grid=(n_a,),
        in_specs=[pl.BlockSpec((t, a_cols), lambda i: (0, i + a_block0)), pl.BlockSpec((t, b_cols), lambda i: (0, 0))],
        out_specs=pl.BlockSpec((split, a_cols // split, b_cols), lambda i: (i, 0, 0)),
        out_shape=_sds((n_a * split, a_cols // split, b_cols), BF),
        compiler_params=_params("parallel"),
    )(a, b)


def _wgrad_pre_t(at, b, b_cols, name):
    rows, t = at.shape
    n_b = b.shape[1] // b_cols

    def body(a_ref, b_ref, o_ref):
        o_ref[0] = jnp.dot(a_ref[...], b_ref[...], preferred_element_type=F32).astype(BF)

    return pl.pallas_call(
        body, name=name, grid=(n_b,),
        in_specs=[pl.BlockSpec((rows, t), lambda j: (0, 0)), pl.BlockSpec((t, b_cols), lambda j: (0, j))],
        out_specs=pl.BlockSpec((1, rows, b_cols), lambda j: (j, 0, 0)), out_shape=_sds((n_b, rows, b_cols), BF),
        compiler_params=_params("parallel"),
    )(at, b)


def _select_matrix(rows, fn):
    r = _iota((rows, LANES), 0)
    c = _iota((rows, LANES), 1)
    return (r == fn(c)).astype(F32)


def _small_bwd(proj, fb, al, dtb, dcq, dckt, dbe, dge):
    t = proj.shape[0]

    def body(sm_ref, fb_ref, al_ref, dtb_ref, dcq_ref, dckt_ref, dbe_ref, dge_ref, dsm_ref, dvec_ref):
        s = sm_ref[...]
        lane = _iota((1, LANES), 1)
        sel_f = _select_matrix(512, lambda c: jnp.where(c < 8, FOX_HEAD_DIM * c, -1))
        sel_k = _select_matrix(LANES, lambda c: jnp.where(c < 8, 32 * (c // 2) + c % 2, -1))
        dcum = _mm_exact(dcq_ref[...], sel_f) - _mm_exact(dckt_ref[...].T, sel_k)
        row = _iota((t, LANES), 0)
        step = 1
        while step < t:
            dcum = dcum + _shift_up(dcum, step, row)
            step *= 2
        dff = dcum * _sigmoid(-(s + fb_ref[...]))
        sel_b = _select_matrix(512, lambda c: jnp.where((c >= SM_GB) & (c < SM_GA), LANES * (c - SM_GB), -1))
        sel_g = _select_matrix(512, lambda c: jnp.where((c >= SM_GA) & (c < SM_GA + 4), LANES * (c - SM_GA), -1))
        beta = _sigmoid(s)
        dgb = _mm_exact(dbe_ref[...], sel_b) * beta * (1.0 - beta)
        dg = _mm_exact(dge_ref[...], sel_g)
        za = s + dtb_ref[...]
        nea = -jnp.exp(al_ref[...])
        dga = dg * nea * _sigmoid(za)
        is_f = lane < SM_GB
        is_b = (lane >= SM_GB) & (lane < SM_GA)
        is_a = (lane >= SM_GA) & (lane < SM_GA + 4)
        dsm_ref[...] = jnp.where(is_f, dff, jnp.where(is_b, dgb, jnp.where(is_a, dga, 0.0)))
        dvec_ref[...] = jnp.zeros_like(dvec_ref)
        dvec_ref[0:1, :] = jnp.sum(jnp.where(is_f, dff, 0.0), axis=0, keepdims=True)
        dvec_ref[1:2, :] = jnp.sum(jnp.where(is_a, dg * nea * _softplus(za), 0.0), axis=0, keepdims=True)
        dvec_ref[2:3, :] = jnp.sum(jnp.where(is_a, dga, 0.0), axis=0, keepdims=True)

    vec = pl.BlockSpec((1, LANES), lambda i: (0, 0))
    full = lambda r, c: pl.BlockSpec((r, c), lambda i: (0, 0))
    return pl.pallas_call(
        body, name="small_bwd", grid=(1,),
        in_specs=[pl.BlockSpec((t, LANES), lambda i: (0, COL_SMALL // LANES)), vec, vec, vec, full(t, 512),
                  full(LANES, t), full(t, 512), full(t, 512)],
        out_specs=[full(t, LANES), full(8, LANES)], out_shape=[_sds((t, LANES)), _sds((8, LANES))],
        compiler_params=_params("arbitrary"),
    )(proj, fb, al, dtb, dcq, dckt, dbe, dge)


def _in_bwd(dfox, dgdn, dgz, dsm, wt_al, x, nw, dx1):
    t = x.shape[0]
    tm = TOKEN_BLOCK

    def body(*refs):
        parts, (w_ref, x_ref, nw_ref, dx1_ref, dp_ref, dx_ref, dnw_ref) = refs[:8], refs[8:]
        i = pl.program_id(0)

        @pl.when(i == 0)
        def _():
            dnw_ref[...] = jnp.zeros_like(dnw_ref)

        col = 0
        for part in parts:
            width = part.shape[1]
            dp_ref[:, col:col + width] = part[...].astype(BF)
            col += width
        dh = jnp.dot(dp_ref[...], w_ref[...], preferred_element_type=F32)
        xv = x_ref[...]
        r = lax.rsqrt(jnp.mean(xv * xv, axis=-1, keepdims=True) + EPS)
        xh = xv * r
        dnw_ref[...] += jnp.sum(dh * xh, axis=0, keepdims=True)
        dxh = dh * nw_ref[...]
        dx_ref[...] = dx1_ref[...] + r * (dxh - xh * jnp.mean(dxh * xh, axis=-1, keepdims=True))

    tok = lambda w: pl.BlockSpec((tm, w), lambda i: (i, 0))
    vec = lambda w: pl.BlockSpec((1, w), lambda i: (0, 0))
    return pl.pallas_call(
        body, name="in_bwd", grid=(t // tm,),
        in_specs=[tok(D_FOX)] * 3 + [tok(D_GDN)] * 4 + [tok(LANES), pl.BlockSpec((PROJ_W, D_MODEL), lambda i: (0, 0)),
                                                        tok(D_MODEL), vec(D_MODEL), tok(D_MODEL)],
        out_specs=[tok(PROJ_W), tok(D_MODEL), vec(D_MODEL)],
        out_shape=[_sds((t, PROJ_W), BF), _sds((t, D_MODEL)), _sds((1, D_MODEL))],
        compiler_params=_params("arbitrary"),
    )(*dfox, *dgdn, dgz, dsm, wt_al, x, nw, dx1)


def _row(v, width=None):
    v = v.reshape(1, -1).astype(F32)
    if width is not None and v.shape[1] < width:
        v = jnp.pad(v, ((0, 0), (0, width - v.shape[1])))
    return v


def _lane_vec(v, first):
    return jnp.zeros((1, LANES), F32).at[0, first:first + v.shape[0]].set(v.astype(F32))


def _local_step(x, target, wt_al, late_weights, on_grads, convw, pre_mix_norm, fox_f_bias, fox_out_norm,
                gdn_a_log, gdn_dt_bias, gdn_out_norm, post_mix_norm, pre_mlp_norm, post_mlp_norm):
    t = x.shape[0]
    nch = t // CHUNK
    nw, pmw, plw, pw = _row(pre_mix_norm), _row(post_mix_norm), _row(pre_mlp_norm), _row(post_mlp_norm)
    fb, al, dtb = _lane_vec(fox_f_bias, SM_FF), _lane_vec(gdn_a_log, SM_GA), _lane_vec(gdn_dt_bias, SM_GA)
    fnw = _row(jnp.tile(fox_out_norm, 2))
    gnw = _row(gdn_out_norm)

    proj, h = _norm_proj(x, nw, wt_al)
    ce, cumt, be, ge = _small_prep(proj, fb, al, dtb)
    o_fox, lse, fox_n = _fox_fwd(proj, ce, cumt, fnw)
    qn, kn, cv, gc, mmat, amat = _gdn_prep(proj, convw, be, ge)
    n_prob = N_GDN_HEADS * nch
    m2 = mmat.reshape(n_prob, CHUNK * CHUNK)
    if n_prob < LANES:
        m2 = jnp.pad(m2, ((0, LANES - n_prob), (0, 0)))
    tinv = _tri_inverse(m2)[:n_prob].reshape(N_GDN_HEADS, nch, CHUNK, CHUNK)
    gdn_o, s_all, vn_all = _gdn_scan(qn, kn, cv, be, gc, tinv, amat)
    w_out = late_weights("w_out", gdn_o)
    x1, h2, mixed, omix, h2t = _mix_out(fox_n, gdn_o, proj, gnw, w_out, x, pmw, plw)
    w_up, w_down = late_weights("mlp", h2)
    up = _mlp_up(h2, w_up)
    dy, dx2, loss, d_pw = _mlp_down_loss(up, w_down, x1, pw, target)

    dup = _mlp_bwd_act(dy, w_down, up)
    relu2 = lambda u: jnp.square(jnp.maximum(u, 0.0))
    g_down = _wgrad(up, dy, D_FF // N_DEV, a_fn=relu2, name="wgrad_down")
    g_up = _wgrad_pre_t(h2t, dup, D_FF // N_DEV, name="wgrad_up")
    token = on_grads("mlp", (g_up, g_down))
    dx1, dmixed, d_plw, d_pmw = _mlp_bwd_in(dup, w_up, x1, plw + token[0:1, 0:1], dx2, mixed, pmw)
    token = on_grads("w_out", _wgrad(omix, dmixed, 512, split=4, name="wgrad_out"))
    do_fox, dgo, dgz, d_fnw, d_gnw = _out_bwd(dmixed, w_out, o_fox, gdn_o, proj, fnw + token[0:1, 0:1], gnw)
    dfq, dfk, dfv, dcq, dckt = _fox_bwd(proj, ce, cumt, lse, o_fox, do_fox)
    dqn, dkn, dcv, dbe, dge = _gdn_bwd(qn, kn, cv, be, gc, tinv, amat, s_all, vn_all, dgo)
    dxq, dxk, dxv, dwq, dwk, dwv = _gdn_bwd_conv(proj, convw, dqn, dkn, dcv)
    dsm, dvec = _small_bwd(proj, fb, al, dtb, dcq, dckt, dbe, dge)
    dproj, grad_x, d_nw = _in_bwd((dfq, dfk, dfv), (dxq, dxk, dxv), dgz, dsm, wt_al, x, nw, dx1)
    g_main = _wgrad(dproj, h, 512, name="wgrad_in")
    g_tail = _wgrad(dproj, h, LANES, a_block0=COL_SMALL // LANES, name="wgrad_in_small")
    on_grads("w_in", jnp.concatenate([g_main.reshape(COL_SMALL, D_MODEL), g_tail[0]]))
    small = dict(pre_mix_norm=d_nw[0], fox_f_bias=dvec[0, SM_FF:SM_FF + N_FOX_HEADS],
                 fox_out_norm=d_fnw[0, :FOX_HEAD_DIM], gdn_conv_w=(dwq, dwk, dwv),
                 gdn_a_log=dvec[1, SM_GA:SM_GA + N_GDN_HEADS], gdn_dt_bias=dvec[2, SM_GA:SM_GA + N_GDN_HEADS],
                 gdn_out_norm=d_gnw[0], post_mix_norm=d_pmw[0], pre_mlp_norm=d_plw[0], post_mlp_norm=d_pw[0])
    return loss[0, 0], grad_x, small


MESH_IDS = pl.DeviceIdType.MESH
CHIP_FLIPS = ((0, 0), (1, 0), (0, 1), (1, 1))
ANY_SPEC = pl.BlockSpec(memory_space=pl.ANY)


def _place():
    return lax.axis_index("x"), lax.axis_index("y"), lax.axis_index("c")


def _all_gather(blocks):
    n = len(blocks)

    def body(*refs):
        ins, outs, (send_sems, recv_sems, local_sems) = refs[:n], refs[n:2 * n], refs[2 * n:]
        x, y, c = _place()
        sibling = (x, y, 1 - c)
        chips = [(x ^ fx, y ^ fy) for fx, fy in CHIP_FLIPS[1:]]

        def slot(out, px, py, pc):
            return out.at[4 * px + 2 * py + pc]

        def copy(a, k, block, to, src=None):
            return pltpu.make_async_remote_copy(
                src_ref=slot(outs[a], *block) if src is None else src, dst_ref=slot(outs[a], *block),
                send_sem=send_sems.at[a, k], recv_sem=recv_sems.at[a, k], device_id=to, device_id_type=MESH_IDS)

        pending = []
        for a in range(n):
            mine = pltpu.make_async_copy(ins[a], slot(outs[a], x, y, c), local_sems.at[a])
            mine.start()
            pending.append(mine)
        sends = []
        for a in range(n):
            first = [copy(a, 0, (x, y, c), sibling, src=ins[a])]
            first += [copy(a, 1 + j, (x, y, c), (*chip, c), src=ins[a]) for j, chip in enumerate(chips)]
            for cp in first:
                cp.start()
            sends += first
        for a in range(n):
            for j, chip in enumerate(chips):
                copy(a, 1 + j, (*chip, c), (x, y, c)).wait_recv()
                fwd = copy(a, 4 + j, (*chip, c), sibling)
                fwd.start()
                sends.append(fwd)
        for a in range(n):
            copy(a, 0, sibling, (x, y, c)).wait_recv()
            for j, chip in enumerate(chips):
                copy(a, 4 + j, (*chip, 1 - c), (x, y, c)).wait_recv()
        for cp in sends:
            cp.wait_send()
        for cp in pending:
            cp.wait()

    return pl.pallas_call(
        body, name="all_gather_weights", in_specs=[ANY_SPEC] * n, out_specs=[ANY_SPEC] * n,
        out_shape=[_sds((N_DEV,) + b.shape, b.dtype) for b in blocks],
        scratch_shapes=[pltpu.SemaphoreType.DMA((n, 7)), pltpu.SemaphoreType.DMA((n, 7)), pltpu.SemaphoreType.DMA((n,))],
        compiler_params=pltpu.CompilerParams(has_side_effects=True),
    )(*blocks)


def _adamw(w, g, m, v):
    m = ADAM_B1 * m + (1.0 - ADAM_B1) * g
    v = ADAM_B2 * v + (1.0 - ADAM_B2) * (g * g)
    m_hat = m / (1.0 - ADAM_B1 ** ADAM_STEP)
    v_hat = v / (1.0 - ADAM_B2 ** ADAM_STEP)
    return -ADAM_LR * (m_hat / (jnp.sqrt(v_hat) + ADAM_EPS) + ADAM_WD * w), m, v


HBM_SPEC = pl.BlockSpec(memory_space=pltpu.HBM)
SEM_SPEC = pl.BlockSpec(memory_space=pltpu.SEMAPHORE)
DATAFLOW = pltpu.SideEffectType.DATAFLOW_SIDE_EFFECTING


def _peers():
    x, y, c = _place()
    return 4 * x + 2 * y + c, [(x ^ (k >> 2), y ^ ((k >> 1) & 1), c ^ (k & 1)) for k in range(1, N_DEV)]


def _peer_index(peer):
    return 4 * peer[0] + 2 * peer[1] + peer[2]


def _zones_with_own(srcs, pieces, name, after=None):
    n = len(srcs)
    extra = [] if after is None else [after]

    def body(me_ref, *refs):
        for a in range(n):
            refs[n + len(extra) + a][0] = refs[a][0] if pieces else refs[a][...]

    shapes = [s_.shape[1:] if pieces else s_.shape for s_ in srcs]
    mine = lambda sh: pl.BlockSpec((1,) + sh, lambda i, me_ref: (me_ref[0], 0, 0))
    in_specs = [mine(sh) if pieces else pl.BlockSpec(sh, lambda i, me_ref: (0, 0)) for sh in shapes]
    x, y, c = _place()
    return pl.pallas_call(
        body, name=name,
        grid_spec=pltpu.PrefetchScalarGridSpec(num_scalar_prefetch=1, grid=(1,), in_specs=in_specs + [ANY_SPEC] * len(extra),
                                               out_specs=[mine(sh) for sh in shapes]),
        out_shape=[_sds((N_DEV,) + sh, s_.dtype) for sh, s_ in zip(shapes, srcs)],
        compiler_params=_params("arbitrary"),
    )((4 * x + 2 * y + c).astype(jnp.int32).reshape(1), *srcs, *extra)


def _exchange_start(srcs, zones, pieces, name):
    n = len(srcs)

    def body(*refs):
        ins, zs = refs[:n], refs[n:2 * n]
        sems = refs[2 * n:4 * n]
        token = refs[-1]
        me, peers = _peers()
        for peer in peers:
            for a in range(n):
                pltpu.make_async_remote_copy(
                    src_ref=ins[a].at[_peer_index(peer)] if pieces else ins[a], dst_ref=zs[a].at[me],
                    send_sem=sems[2 * a], recv_sem=sems[2 * a + 1], device_id=peer, device_id_type=MESH_IDS).start()
        token[...] = jnp.zeros_like(token)

    hbm = lambda v: pltpu.with_memory_space_constraint(v, pltpu.HBM)
    out = pl.pallas_call(
        body, name=name,
        out_shape=tuple([pltpu.SemaphoreType.DMA(())] * (2 * n) + [pltpu.HBM(v.shape, v.dtype) for v in srcs]
                        + [pltpu.HBM(z.shape, z.dtype) for z in zones] + [_sds((8, LANES))]),
        in_specs=[HBM_SPEC] * (2 * n), out_specs=tuple([SEM_SPEC] * (2 * n) + [HBM_SPEC] * (2 * n) + [VMEM_SPEC]),
        input_output_aliases={i: 2 * n + i for i in range(2 * n)},
        compiler_params=pltpu.CompilerParams(has_side_effects=DATAFLOW),
    )(*[hbm(v) for v in srcs], *[hbm(z) for z in zones])
    return out[:2 * n], out[2 * n:3 * n], out[3 * n:4 * n], out[-1]


def _exchange_wait(sems, srcs, zones, after, name):
    n = len(srcs)

    def body(*refs):
        ins, zs, sm = refs[:n], refs[n:2 * n], refs[2 * n:4 * n]
        me, peers = _peers()
        for a in range(n):
            seven = zs[a].at[pl.ds(0, N_DEV - 1)]
            cp = pltpu.make_async_remote_copy(src_ref=seven, dst_ref=seven, send_sem=sm[2 * a], recv_sem=sm[2 * a + 1],
                                              device_id=peers[0], device_id_type=MESH_IDS)
            cp.wait_send()
            cp.wait_recv()

    out = pl.pallas_call(
        body, name=name, out_shape=tuple([pltpu.HBM(v.shape, v.dtype) for v in srcs] + [pltpu.HBM(z.shape, z.dtype) for z in zones]),
        in_specs=[HBM_SPEC] * (2 * n) + [SEM_SPEC] * (2 * n) + [ANY_SPEC], out_specs=tuple([HBM_SPEC] * (2 * n)),
        input_output_aliases={i: i for i in range(2 * n)},
        compiler_params=pltpu.CompilerParams(has_side_effects=DATAFLOW),
    )(*srcs, *zones, *sems, after)
    return out[n:]


def _sum_adamw(zone, w, m, v, name):
    _, r, c_ = zone.shape
    rb = 128 if r % 128 == 0 else r

    def body(z_ref, w_ref, m_ref, v_ref, grad_ref, delta_ref, nm_ref, nv_ref):
        total = z_ref[0].astype(F32)
        for d in range(1, N_DEV):
            total = total + z_ref[d].astype(F32)
        grad_ref[...] = total
        delta_ref[...], nm_ref[...], nv_ref[...] = _adamw(w_ref[...], total, m_ref[...], v_ref[...])

    blk = pl.BlockSpec((rb, c_), lambda i: (i, 0))
    return pl.pallas_call(
        body, name=name, grid=(r // rb,), in_specs=[pl.BlockSpec((N_DEV, rb, c_), lambda i: (0, i, 0)), blk, blk, blk],
        out_specs=[blk] * 4, out_shape=[_sds((r, c_))] * 4, compiler_params=_params("parallel"),
    )(zone, w, m, v)


SMALL_ROWS = 16


def _all_reduce_small(packed):
    def body(p_ref, sum_ref, gath, send_sems, recv_sems):
        x, y, c = _place()
        me = 4 * x + 2 * y + c
        gath[me] = p_ref[...]
        copies = []
        for k in range(1, N_DEV):
            peer = (x ^ (k >> 2), y ^ ((k >> 1) & 1), c ^ (k & 1))
            cp = pltpu.make_async_remote_copy(
                src_ref=p_ref, dst_ref=gath.at[me], send_sem=send_sems.at[k], recv_sem=recv_sems.at[k],
                device_id=peer, device_id_type=MESH_IDS)
            cp.start()
            copies.append(cp)
        for cp in copies:
            cp.wait_recv()
        total = gath[0]
        for d in range(1, N_DEV):
            total = total + gath[d]
        sum_ref[...] = total
        for cp in copies:
            cp.wait_send()

    return pl.pallas_call(
        body, name="all_reduce_small", in_specs=[VMEM_SPEC], out_specs=VMEM_SPEC, out_shape=_sds(packed.shape),
        scratch_shapes=[pltpu.VMEM((N_DEV,) + packed.shape, F32), pltpu.SemaphoreType.DMA((N_DEV,)),
                        pltpu.SemaphoreType.DMA((N_DEV,))],
        compiler_params=pltpu.CompilerParams(has_side_effects=True),
    )(packed)


def _adamw_small(w, g, m, v):
    def body(w_ref, g_ref, m_ref, v_ref, delta_ref, nm_ref, nv_ref):
        delta_ref[...], nm_ref[...], nv_ref[...] = _adamw(w_ref[...], g_ref[...], m_ref[...], v_ref[...])

    return pl.pallas_call(body, name="adamw_small", in_specs=[VMEM_SPEC] * 4, out_specs=[VMEM_SPEC] * 3,
                          out_shape=[_sds(w.shape)] * 3)(w, g, m, v)


NATIVE_ROWS = ((0, 1536), (1544, 3080), (3088, 3600), (1536, 1544), (3080, 3088))


def _to_aligned_rows(wt_native):
    pad = jnp.zeros((PROJ_W - D_PROJ, wt_native.shape[1]), wt_native.dtype)
    return jnp.concatenate([wt_native[lo:hi] for lo, hi in NATIVE_ROWS] + [pad])


def _from_aligned_rows(gt_al):
    return jnp.concatenate([gt_al[0:1536], gt_al[3584:3592], gt_al[1536:3072], gt_al[3592:3600], gt_al[3072:3584]])


def _cols_from_pieces(p):
    return p.transpose(1, 0, 2).reshape(p.shape[1], -1)


SMALL_NORMS = ("pre_mix_norm", "post_mix_norm", "pre_mlp_norm", "post_mlp_norm")
SMALL_MISC = (("fox_out_norm", FOX_HEAD_DIM), ("gdn_out_norm", GDN_HEAD_DIM), ("fox_f_bias", N_FOX_HEADS),
              ("gdn_a_log", N_GDN_HEADS), ("gdn_dt_bias", N_GDN_HEADS))


def _pack_small(vals, conv):
    misc = jnp.concatenate([vals[n].astype(F32) for n, _ in SMALL_MISC])
    rows = [vals[n].astype(F32) for n in SMALL_NORMS] + [jnp.pad(misc, (0, D_MODEL - misc.shape[0]))]
    flat = conv.astype(F32).reshape(-1)
    n_rows = -(-flat.shape[0] // D_MODEL)
    flat = jnp.pad(flat, (0, n_rows * D_MODEL - flat.shape[0])).reshape(n_rows, D_MODEL)
    packed = jnp.concatenate([jnp.stack(rows), flat])
    return jnp.pad(packed, ((0, SMALL_ROWS - packed.shape[0]), (0, 0)))


def _unpack_small(packed, conv_shape):
    out = {n: packed[i] for i, n in enumerate(SMALL_NORMS)}
    off = 0
    for n, size in SMALL_MISC:
        out[n] = packed[4, off:off + size]
        off += size
    n_conv = conv_shape[0] * conv_shape[1]
    out["gdn_conv_w"] = packed[5:].reshape(-1)[:n_conv].reshape(conv_shape)
    return out


WEIGHT_ORDER = ("pre_mix_norm", "w_in", "fox_f_bias", "fox_out_norm", "gdn_conv_w", "gdn_a_log", "gdn_dt_bias",
                "gdn_out_norm", "w_out", "post_mix_norm", "pre_mlp_norm", "w_up", "w_down", "post_mlp_norm")


def kernel(x, pre_mix_norm, w_in, fox_f_bias, fox_out_norm, gdn_conv_w, gdn_a_log, gdn_dt_bias, gdn_out_norm, w_out, post_mix_norm, pre_mlp_norm, w_up, w_down, post_mlp_norm, loss_target, m_pre_mix_norm, m_w_in, m_fox_f_bias, m_fox_out_norm, m_gdn_conv_w, m_gdn_a_log, m_gdn_dt_bias, m_gdn_out_norm, m_w_out, m_post_mix_norm, m_pre_mlp_norm, m_w_up, m_w_down, m_post_mlp_norm, v_pre_mix_norm, v_w_in, v_fox_f_bias, v_fox_out_norm, v_gdn_conv_w, v_gdn_a_log, v_gdn_dt_bias, v_gdn_out_norm, v_w_out, v_post_mix_norm, v_pre_mlp_norm, v_w_up, v_w_down, v_post_mlp_norm):
    w = dict(pre_mix_norm=pre_mix_norm, w_in=w_in, fox_f_bias=fox_f_bias, fox_out_norm=fox_out_norm,
             gdn_conv_w=gdn_conv_w, gdn_a_log=gdn_a_log, gdn_dt_bias=gdn_dt_bias, gdn_out_norm=gdn_out_norm, w_out=w_out,
             post_mix_norm=post_mix_norm, pre_mlp_norm=pre_mlp_norm, w_up=w_up, w_down=w_down, post_mlp_norm=post_mlp_norm)
    mom = dict(pre_mix_norm=m_pre_mix_norm, w_in=m_w_in, fox_f_bias=m_fox_f_bias, fox_out_norm=m_fox_out_norm,
               gdn_conv_w=m_gdn_conv_w, gdn_a_log=m_gdn_a_log, gdn_dt_bias=m_gdn_dt_bias, gdn_out_norm=m_gdn_out_norm,
               w_out=m_w_out, post_mix_norm=m_post_mix_norm, pre_mlp_norm=m_pre_mlp_norm, w_up=m_w_up, w_down=m_w_down,
               post_mlp_norm=m_post_mlp_norm)
    var = dict(pre_mix_norm=v_pre_mix_norm, w_in=v_w_in, fox_f_bias=v_fox_f_bias, fox_out_norm=v_fox_out_norm,
               gdn_conv_w=v_gdn_conv_w, gdn_a_log=v_gdn_a_log, gdn_dt_bias=v_gdn_dt_bias, gdn_out_norm=v_gdn_out_norm,
               w_out=v_w_out, post_mix_norm=v_post_mix_norm, pre_mlp_norm=v_pre_mlp_norm, w_up=v_w_up, w_down=v_w_down,
               post_mlp_norm=v_post_mlp_norm)

    win_g, conv_g = _all_gather([w_in.T.astype(BF), gdn_conv_w])
    wt_al = _to_aligned_rows(win_g.reshape(D_PROJ, D_MODEL))
    convw = _cols_from_pieces(conv_g)
    gathers, after = {}, win_g
    for name, shards in (("w_out", [w_out.astype(BF)]), ("mlp", [w_up.astype(BF), w_down.astype(BF)])):
        zones = _zones_with_own(shards, False, "gather_" + name + "_own", after=after)
        gathers[name] = _exchange_start(shards, zones, False, "gather_" + name + "_start")
        after = gathers[name][3]

    def late_weights(name, after):
        sems, shards, zones, _ = gathers[name]
        got = _exchange_wait(sems, shards, zones, after, "gather_" + name + "_wait")
        if name == "w_out":
            return got[0].reshape(D_MODEL, D_MODEL)
        return got[0], got[1].reshape(D_FF, D_MODEL)

    scatters = {}

    def on_grads(name, g):
        if name == "w_in":
            g = _from_aligned_rows(g).reshape(N_DEV, D_PROJ // N_DEV, D_MODEL)
        srcs = list(g) if name == "mlp" else [g]
        scatters[name] = _exchange_start(srcs, _zones_with_own(srcs, True, "scatter_" + name + "_own"), True,
                                         "scatter_" + name + "_start")
        return scatters[name][3]

    loss, grad_x, small = _local_step(
        x[0], loss_target[0], wt_al, late_weights, on_grads, convw, pre_mix_norm + after[0, 0],
        fox_f_bias, fox_out_norm, gdn_a_log, gdn_dt_bias, gdn_out_norm, post_mix_norm, pre_mlp_norm, post_mlp_norm)
    loss = lax.psum(loss, ("x", "y", "c"))

    grads, delta, new_m, new_v = {}, {}, {}, {}
    dwq, dwk, dwv = small.pop("gdn_conv_w")
    packed = _pack_small(small, jnp.concatenate([dwq, dwk, dwv], axis=1)) + scatters["w_in"][3][0, 0]
    total = _unpack_small(_all_reduce_small(packed), (CONV_K, 3 * D_GDN))
    after = total["pre_mix_norm"]
    for name, members in (("mlp", ("w_up", "w_down")), ("w_out", ("w_out",)), ("w_in", ("w_in",))):
        sems, srcs, zones, _ = scatters[name]
        zones = _exchange_wait(sems, srcs, zones, after, "scatter_" + name + "_wait")
        for n, zone in zip(members, zones):
            if n == "w_in":
                res = _sum_adamw(zone, w[n].T, mom[n].T, var[n].T, "adamw_" + n)
                grads[n], delta[n], new_m[n], new_v[n] = [r.T for r in res]
            else:
                grads[n], delta[n], new_m[n], new_v[n] = _sum_adamw(zone, w[n], mom[n], var[n], "adamw_" + n)
            after = grads[n]

    me = 4 * lax.axis_index("x") + 2 * lax.axis_index("y") + lax.axis_index("c")
    n_conv = gdn_conv_w.shape[1]
    total["gdn_conv_w"] = lax.dynamic_slice_in_dim(total["gdn_conv_w"], me * n_conv, n_conv, axis=1)
    grads.update(total)
    d_s, m_s, v_s = _adamw_small(_pack_small(w, gdn_conv_w), _pack_small(total, total["gdn_conv_w"]),
                                 _pack_small(mom, m_gdn_conv_w), _pack_small(var, v_gdn_conv_w))
    delta.update(_unpack_small(d_s, gdn_conv_w.shape))
    new_m.update(_unpack_small(m_s, gdn_conv_w.shape))
    new_v.update(_unpack_small(v_s, gdn_conv_w.shape))

    return (loss, grad_x[None], *[grads[n] for n in WEIGHT_ORDER], *[delta[n] for n in WEIGHT_ORDER],
            *[new_m[n] for n in WEIGHT_ORDER], *[new_v[n] for n in WEIGHT_ORDER])
```

```python
import jax
import jax.numpy as jnp
from jax import lax
from jax.experimental import pallas as pl
from jax.experimental.pallas import tpu as pltpu

F32 = jnp.float32
BF = jnp.bfloat16

D_MODEL = 1024
N_FOX_HEADS, FOX_HEAD_DIM = 8, 64
N_GDN_HEADS, GDN_HEAD_DIM = 4, 128
D_FOX = N_FOX_HEADS * FOX_HEAD_DIM
D_GDN = N_GDN_HEADS * GDN_HEAD_DIM
CHUNK = 64
CONV_K = 4
D_FF = 4 * D_MODEL
EPS = 1e-6
D_PROJ = 3600
N_DEV = 8

PROJ_W = 3712
COL_FOX, COL_GDN, COL_GZ, COL_SMALL = 0, 1536, 3072, 3584
LANES = 128
SM_FF, SM_GB, SM_GA = 0, 8, 12

ADAM_LR, ADAM_B1, ADAM_B2, ADAM_EPS, ADAM_WD, ADAM_STEP = 0.001, 0.9, 0.999, 1e-08, 0.01, 10

TOKEN_BLOCK = 256
FOX_SCALE = FOX_HEAD_DIM ** -0.5
GDN_QSCALE = GDN_HEAD_DIM ** -0.5
NEG_BIG = -1e30
VMEM_LIMIT = 56 * 1024 * 1024

VMEM_SPEC = pl.BlockSpec(memory_space=pltpu.VMEM)
HIGHEST = lax.Precision.HIGHEST


def _sds(shape, dtype=F32):
    return jax.ShapeDtypeStruct(shape, dtype)


def _params(*sem):
    return pltpu.CompilerParams(dimension_semantics=sem if sem else None, vmem_limit_bytes=VMEM_LIMIT)


def _mm(a, b):
    return jnp.dot(a.astype(BF), b.astype(BF), preferred_element_type=F32)


def _mm_nt(a, b):
    return lax.dot_general(a.astype(BF), b.astype(BF), (((1,), (1,)), ((), ())), preferred_element_type=F32)


def _mm_tn(a, b):
    return lax.dot_general(a.astype(BF), b.astype(BF), (((0,), (0,)), ((), ())), preferred_element_type=F32)


def _mm_exact(a, b):
    return jnp.dot(a, b, precision=HIGHEST, preferred_element_type=F32)


def _mm_tn_exact(a, b):
    return lax.dot_general(a, b, (((0,), (0,)), ((), ())), precision=HIGHEST, preferred_element_type=F32)


def _sigmoid(x):
    return 1.0 / (1.0 + jnp.exp(-x))


def _softplus(x):
    return jnp.maximum(x, 0.0) + jnp.log1p(jnp.exp(-jnp.abs(x)))


def _iota(shape, dim):
    return lax.broadcasted_iota(jnp.int32, shape, dim)


def _shift_down(x, s, row):
    return jnp.where(row >= s, pltpu.roll(x, s, 0), 0.0)


def _shift_up(x, s, row):
    n = x.shape[0]
    return jnp.where(row < n - s, pltpu.roll(x, n - s, 0), 0.0)


def _norm_proj(x, nw, wt_al):
    t = x.shape[0]

    def body(x_ref, nw_ref, w_ref, proj_ref, h_ref):
        xv = x_ref[...]
        r = lax.rsqrt(jnp.mean(xv * xv, axis=-1, keepdims=True) + EPS)
        h = (xv * r * nw_ref[...]).astype(BF)
        h_ref[...] = h
        proj_ref[...] = lax.dot_general(h, w_ref[...], (((1,), (1,)), ((), ())), preferred_element_type=F32)

    tm = TOKEN_BLOCK
    return pl.pallas_call(
        body, name="norm_proj", grid=(t // tm,),
        in_specs=[pl.BlockSpec((tm, D_MODEL), lambda i: (i, 0)), pl.BlockSpec((1, D_MODEL), lambda i: (0, 0)),
                  pl.BlockSpec((PROJ_W, D_MODEL), lambda i: (0, 0))],
        out_specs=[pl.BlockSpec((tm, PROJ_W), lambda i: (i, 0)), pl.BlockSpec((tm, D_MODEL), lambda i: (i, 0))],
        out_shape=[_sds((t, PROJ_W)), _sds((t, D_MODEL), BF)],
        compiler_params=_params("parallel"),
    )(x, nw, wt_al)


def _expand_matrix(first_row, group):
    row = _iota((LANES, 512), 0)
    col = _iota((LANES, 512), 1)
    return (col // group + first_row == row).astype(F32)


def _small_prep(proj, fb, al, dtb):
    t = proj.shape[0]

    def body(sm_ref, fb_ref, al_ref, dtb_ref, ce_ref, cumt_ref, be_ref, ge_ref):
        s = sm_ref[...]
        z = s + fb_ref[...]
        cum = jnp.minimum(z, 0.0) - jnp.log1p(jnp.exp(-jnp.abs(z)))
        row = _iota((t, LANES), 0)
        step = 1
        while step < t:
            cum = cum + _shift_down(cum, step, row)
            step *= 2
        cumt_ref[...] = cum.T
        ce_ref[...] = _mm_exact(cum, _expand_matrix(SM_FF, FOX_HEAD_DIM))
        be_ref[...] = _mm_exact(_sigmoid(s), _expand_matrix(SM_GB, GDN_HEAD_DIM))
        g = -jnp.exp(al_ref[...]) * _softplus(s + dtb_ref[...])
        ge_ref[...] = _mm_exact(g, _expand_matrix(SM_GA, GDN_HEAD_DIM))

    vec = pl.BlockSpec((1, LANES), lambda i: (0, 0))
    return pl.pallas_call(
        body, name="small_prep", grid=(1,),
        in_specs=[pl.BlockSpec((t, LANES), lambda i: (0, COL_SMALL // LANES)), vec, vec, vec],
        out_specs=[pl.BlockSpec((t, 512), lambda i: (0, 0)), pl.BlockSpec((LANES, t), lambda i: (0, 0)),
                   pl.BlockSpec((t, 512), lambda i: (0, 0)), pl.BlockSpec((t, 512), lambda i: (0, 0))],
        out_shape=[_sds((t, 512)), _sds((LANES, t)), _sds((t, 512)), _sds((t, 512))],
        compiler_params=_params("arbitrary"),
    )(proj, fb, al, dtb)


def _fox_scores(qh, kb, ce_ref, cumt_ref, head, hh, i, tq):
    klen = (i + 1) * tq
    s = _mm_nt(qh, kb[:klen]) * FOX_SCALE
    cq = ce_ref[i * tq:(i + 1) * tq, FOX_HEAD_DIM * hh:FOX_HEAD_DIM * hh + 1]
    ck = cumt_ref[pl.ds(head, 1), 0:klen]
    s = s + cq - ck
    qi = _iota((tq, klen), 0) + i * tq
    ki = _iota((tq, klen), 1)
    return jnp.where(ki <= qi, s, NEG_BIG)


def _fox_fwd(proj, ce, cumt, fnw):
    t = proj.shape[0]
    tq = min(TOKEN_BLOCK, t // 2)
    nq = t // tq

    def body(q_ref, k_ref, v_ref, ce_ref, cumt_ref, fnw_ref, o_ref, lse_ref, fn_ref):
        j = pl.program_id(0)
        first = _iota((1, LANES), 1) < FOX_HEAD_DIM
        kb = k_ref[...].astype(BF)
        vb = v_ref[...].astype(BF)
        for i in range(nq):
            rows = slice(i * tq, (i + 1) * tq)
            klen = (i + 1) * tq
            q_i = q_ref[rows, :]
            o_acc = jnp.zeros((tq, LANES), F32)
            lse_acc = jnp.zeros((tq, LANES), F32)
            for hh in range(2):
                mh = first if hh == 0 else jnp.logical_not(first)
                qh = jnp.where(mh, q_i, 0.0).astype(BF)
                s = _fox_scores(qh, kb, ce_ref, cumt_ref, 2 * j + hh, hh, i, tq)
                m = jnp.max(s, axis=-1, keepdims=True)
                p = jnp.exp(s - m)
                l = jnp.sum(p, axis=-1, keepdims=True)
                o = jnp.dot(p.astype(BF), vb[:klen], preferred_element_type=F32) / l
                o_acc = jnp.where(mh, o, o_acc)
                lse_acc = jnp.where(mh, m + jnp.log(l), lse_acc)
            o_ref[rows, :] = o_acc
            lse_ref[rows, :] = lse_acc
            o2 = o_acc * o_acc
            s0 = jnp.sum(jnp.where(first, o2, 0.0), axis=-1, keepdims=True)
            s1 = jnp.sum(jnp.where(first, 0.0, o2), axis=-1, keepdims=True)
            r = lax.rsqrt(jnp.where(first, s0, s1) * (1.0 / FOX_HEAD_DIM) + EPS)
            fn_ref[rows, :] = (o_acc * r * fnw_ref[...]).astype(BF)

    blk = lambda off: pl.BlockSpec((t, LANES), lambda j: (0, off + j))
    return pl.pallas_call(
        body, name="fox_fwd", grid=(N_FOX_HEADS // 2,),
        in_specs=[blk(0), blk(4), blk(8), blk(0), pl.BlockSpec((LANES, t), lambda j: (0, 0)),
                  pl.BlockSpec((1, LANES), lambda j: (0, 0))],
        out_specs=[blk(0), blk(0), blk(0)],
        out_shape=[_sds((t, D_FOX)), _sds((t, D_FOX)), _sds((t, D_FOX), BF)],
        compiler_params=_params("parallel"),
    )(proj, proj, proj, ce, cumt, fnw)


def _fox_bwd(proj, ce, cumt, lse, o, do):
    t = proj.shape[0]
    tq = min(TOKEN_BLOCK, t // 2)
    nq = t // tq

    def body(q_ref, k_ref, v_ref, ce_ref, cumt_ref, lse_ref, o_ref, do_ref,
             dq_ref, dk_ref, dv_ref, dcq_ref, dckt_ref, dk_s, dv_s, dck_s):
        j = pl.program_id(0)
        first = _iota((1, LANES), 1) < FOX_HEAD_DIM
        kf = k_ref[...]
        kb = kf.astype(BF)
        vb = v_ref[...].astype(BF)
        dk_s[...] = jnp.zeros_like(dk_s)
        dv_s[...] = jnp.zeros_like(dv_s)
        dck_s[...] = jnp.zeros_like(dck_s)
        masks = [first, jnp.logical_not(first)]
        kmask = [jnp.where(mh, kf, 0.0).astype(BF) for mh in masks]
        for i in range(nq):
            rows = slice(i * tq, (i + 1) * tq)
            klen = (i + 1) * tq
            q_i = q_ref[rows, :]
            do_i = do_ref[rows, :]
            o_i = o_ref[rows, :]
            lse_i = lse_ref[rows, :]
            dq_acc = jnp.zeros((tq, LANES), F32)
            dcq_acc = jnp.zeros((tq, LANES), F32)
            for hh in range(2):
                mh = masks[hh]
                qh = jnp.where(mh, q_i, 0.0).astype(BF)
                doh = jnp.where(mh, do_i, 0.0)
                dohb = doh.astype(BF)
                delta = jnp.sum(doh * o_i, axis=-1, keepdims=True)
                s = _fox_scores(qh, kb, ce_ref, cumt_ref, 2 * j + hh, hh, i, tq)
                p = jnp.exp(s - lse_i[:, FOX_HEAD_DIM * hh:FOX_HEAD_DIM * hh + 1])
                dp = _mm_nt(dohb, vb[:klen])
                ds = p * (dp - delta)
                dsb = ds.astype(BF)
                dq_acc = dq_acc + jnp.dot(dsb, kmask[hh][:klen], preferred_element_type=F32) * FOX_SCALE
                dk_s[0:klen, :] += _mm_tn(dsb, qh) * FOX_SCALE
                dv_s[0:klen, :] += _mm_tn(p, dohb)
                dcq_acc = jnp.where(mh, jnp.sum(ds, axis=-1, keepdims=True), dcq_acc)
                dck_s[hh:hh + 1, 0:klen] += jnp.sum(ds, axis=0, keepdims=True)
            dq_ref[rows, :] = dq_acc
            dcq_ref[rows, :] = dcq_acc
        dk_ref[...] = dk_s[...]
        dv_ref[...] = dv_s[...]
        dckt_ref[...] = jnp.zeros_like(dckt_ref)
        dckt_ref[0:8, :] = dck_s[...]

    blk = lambda off: pl.BlockSpec((t, LANES), lambda j: (0, off + j))
    return pl.pallas_call(
        body, name="fox_bwd", grid=(N_FOX_HEADS // 2,),
        in_specs=[blk(0), blk(4), blk(8), blk(0), pl.BlockSpec((LANES, t), lambda j: (0, 0)), blk(0), blk(0), blk(0)],
        out_specs=[blk(0), blk(0), blk(0), blk(0), pl.BlockSpec((32, t), lambda j: (j, 0))],
        out_shape=[_sds((t, D_FOX))] * 4 + [_sds((LANES, t))],
        scratch_shapes=[pltpu.VMEM((t, LANES), F32), pltpu.VMEM((t, LANES), F32), pltpu.VMEM((8, t), F32)],
        compiler_params=_params("parallel"),
    )(proj, proj, proj, ce, cumt, lse, o, do)


def _conv(x, w, row):
    return (w[3:4, :] * x + w[2:3, :] * _shift_down(x, 1, row) + w[1:2, :] * _shift_down(x, 2, row)
            + w[0:1, :] * _shift_down(x, 3, row))


def _chunk_decay(gc_c):
    gi = gc_c[:, 0:CHUNK]
    gj = gc_c.T[0:CHUNK, :]
    ri = _iota((CHUNK, CHUNK), 0)
    cj = _iota((CHUNK, CHUNK), 1)
    return jnp.where(ri >= cj, jnp.exp(jnp.minimum(gi - gj, 0.0)), 0.0), ri > cj


def _gdn_specs(t):
    col = lambda off: pl.BlockSpec((t, LANES), lambda h: (0, off + h))
    cw = lambda off: pl.BlockSpec((CONV_K, LANES), lambda h: (0, off + h))
    mat = pl.BlockSpec((1, t // CHUNK, CHUNK, CHUNK), lambda h: (h, 0, 0, 0))
    return col, cw, mat


def _gdn_prep(proj, convw, be, ge):
    t = proj.shape[0]
    nch = t // CHUNK

    def body(xq_ref, xk_ref, xv_ref, wq_ref, wk_ref, wv_ref, be_ref, ge_ref,
             qn_ref, kn_ref, cv_ref, gc_ref, m_ref, a_ref):
        row = _iota((t, LANES), 0)

        def act(x_ref, w_ref):
            y = _conv(x_ref[...], w_ref[...], row)
            return y * _sigmoid(y)

        cq = act(xq_ref, wq_ref)
        ck = act(xk_ref, wk_ref)
        cv_ref[...] = act(xv_ref, wv_ref)
        qn_ref[...] = cq * lax.rsqrt(jnp.sum(cq * cq, axis=-1, keepdims=True) + EPS) * GDN_QSCALE
        kn_ref[...] = ck * lax.rsqrt(jnp.sum(ck * ck, axis=-1, keepdims=True) + EPS)
        gc = ge_ref[...]
        pos = row % CHUNK
        step = 1
        while step < CHUNK:
            gc = gc + jnp.where(pos >= step, pltpu.roll(gc, step, 0), 0.0)
            step *= 2
        gc_ref[...] = gc

        def chunk(n, carry):
            sl = pl.ds(pl.multiple_of(n * CHUNK, CHUNK), CHUNK)
            k_c = kn_ref[sl, :]
            decay, strict = _chunk_decay(gc_ref[sl, :])
            m_ref[0, n] = jnp.where(strict, _mm_nt(k_c * be_ref[sl, :], k_c) * decay, 0.0)
            a_ref[0, n] = _mm_nt(qn_ref[sl, :], k_c) * decay
            return carry

        lax.fori_loop(0, nch, chunk, 0)

    col, cw, mat = _gdn_specs(t)
    return pl.pallas_call(
        body, name="gdn_prep", grid=(N_GDN_HEADS,),
        in_specs=[col(12), col(16), col(20), cw(0), cw(4), cw(8), col(0), col(0)],
        out_specs=[col(0), col(0), col(0), col(0), mat, mat],
        out_shape=[_sds((t, D_GDN))] * 4 + [_sds((N_GDN_HEADS, nch, CHUNK, CHUNK))] * 2,
        compiler_params=_params("parallel"),
    )(proj, proj, proj, convw, convw, convw, be, ge)


def _tri_inverse(m2):
    n_prob = m2.shape[0]
    assert n_prob == LANES
    nb = CHUNK * CHUNK // LANES

    def body(m_ref, t_ref, ms, ts):
        for b in range(nb):
            ms[b * LANES:(b + 1) * LANES, :] = m_ref[:, b * LANES:(b + 1) * LANES].T
        cidx = _iota((CHUNK, LANES), 0)

        def outer(i, carry):
            def inner(jj, acc):
                mrow = ms[pl.ds(i * CHUNK + jj, 1), :]
                return acc - mrow * ts[pl.ds(pl.multiple_of(jj * CHUNK, CHUNK), CHUNK), :]

            acc = lax.fori_loop(0, i, inner, jnp.where(cidx == i, 1.0, 0.0).astype(F32))
            ts[pl.ds(pl.multiple_of(i * CHUNK, CHUNK), CHUNK), :] = acc
            return carry

        lax.fori_loop(0, CHUNK, outer, 0)
        for b in range(nb):
            t_ref[:, b * LANES:(b + 1) * LANES] = ts[b * LANES:(b + 1) * LANES, :].T

    return pl.pallas_call(
        body, name="tri_inverse", in_specs=[VMEM_SPEC], out_specs=VMEM_SPEC,
        out_shape=_sds((LANES, CHUNK * CHUNK)),
        scratch_shapes=[pltpu.VMEM((CHUNK * CHUNK, LANES), F32), pltpu.VMEM((CHUNK * CHUNK, LANES), F32)],
        compiler_params=_params(),
    )(m2)


def _gdn_chunk_terms(q, k, v, b, gcc):
    eg = jnp.exp(gcc)
    last = gcc[CHUNK - 1:CHUNK, :]
    egl = jnp.exp(last - gcc)
    gl = jnp.exp(last)
    kb = k * b
    return eg, egl, gl, kb, v * b, kb * eg, q * eg, k * egl


GDN_BLOCK_CHUNKS = 4


def _gdn_block_specs(t, reverse):
    cb = GDN_BLOCK_CHUNKS
    nb = t // (cb * CHUNK)
    idx = (lambda i: nb - 1 - i) if reverse else (lambda i: i)
    tok = pl.BlockSpec((cb * CHUNK, D_GDN), lambda i: (idx(i), 0))
    mat = pl.BlockSpec((N_GDN_HEADS, cb, CHUNK, CHUNK), lambda i: (0, idx(i), 0, 0))
    state = pl.BlockSpec((N_GDN_HEADS, cb, GDN_HEAD_DIM, GDN_HEAD_DIM), lambda i: (0, idx(i), 0, 0))
    return nb, tok, mat, state


def _gdn_scan(qn, kn, cv, be, gc, tinv, amat):
    t = qn.shape[0]
    nch = t // CHUNK

    def body(q_ref, k_ref, v_ref, b_ref, gc_ref, t_ref, a_ref, o_ref, sall_ref, vn_ref, s_scr):
        @pl.when(pl.program_id(0) == 0)
        def _():
            s_scr[...] = jnp.zeros_like(s_scr)

        for hd in range(N_GDN_HEADS):
            cs = slice(hd * LANES, (hd + 1) * LANES)
            s = s_scr[hd]
            for cc in range(GDN_BLOCK_CHUNKS):
                rs = slice(cc * CHUNK, (cc + 1) * CHUNK)
                eg, egl, gl, kb, vb, kbg, qd, kd = _gdn_chunk_terms(q_ref[rs, cs], k_ref[rs, cs], v_ref[rs, cs],
                                                                    b_ref[rs, cs], gc_ref[rs, cs])
                sall_ref[hd, cc] = s
                uw = _mm(t_ref[hd, cc], jnp.concatenate([vb, kbg], axis=1))
                ws_qs = _mm(jnp.concatenate([uw[:, LANES:], qd], axis=0), s)
                vn = uw[:, :LANES] - ws_qs[:CHUNK]
                vn_ref[rs, cs] = vn
                o_ref[rs, cs] = ws_qs[CHUNK:] + _mm(a_ref[hd, cc], vn)
                s = s * gl + _mm_tn(kd, vn)
            s_scr[hd] = s

    nb, tok, mat, state = _gdn_block_specs(t, False)
    return pl.pallas_call(
        body, name="gdn_scan", grid=(nb,),
        in_specs=[tok] * 5 + [mat, mat], out_specs=[tok, state, tok],
        out_shape=[_sds((t, D_GDN)), _sds((N_GDN_HEADS, nch, GDN_HEAD_DIM, GDN_HEAD_DIM)), _sds((t, D_GDN))],
        scratch_shapes=[pltpu.VMEM((N_GDN_HEADS, GDN_HEAD_DIM, GDN_HEAD_DIM), F32)],
        compiler_params=_params("arbitrary"),
    )(qn, kn, cv, be, gc, tinv, amat)


def _gdn_bwd(qn, kn, cv, be, gc, tinv, amat, s_all, vn_all, do):
    t = qn.shape[0]

    def body(q_ref, k_ref, v_ref, b_ref, gc_ref, t_ref, a_ref, sall_ref, vn_ref, do_ref,
             dq_ref, dk_ref, dv_ref, db_ref, dg_ref, ds_scr):
        @pl.when(pl.program_id(0) == 0)
        def _():
            ds_scr[...] = jnp.zeros_like(ds_scr)

        lastrow = _iota((CHUNK, LANES), 0) == CHUNK - 1
        for hd in range(N_GDN_HEADS):
            cs = slice(hd * LANES, (hd + 1) * LANES)
            dsp = ds_scr[hd]
            for cc in reversed(range(GDN_BLOCK_CHUNKS)):
                rs = slice(cc * CHUNK, (cc + 1) * CHUNK)
                q, k, v, b, gcc = q_ref[rs, cs], k_ref[rs, cs], v_ref[rs, cs], b_ref[rs, cs], gc_ref[rs, cs]
                eg, egl, gl, kb, vb, kbg, qd, kd = _gdn_chunk_terms(q, k, v, b, gcc)
                do_c = do_ref[rs, cs]
                tn = t_ref[hd, cc]
                an = a_ref[hd, cc]
                s = sall_ref[hd, cc]
                vn = vn_ref[rs, cs]
                w = _mm(tn, kbg)
                dvn = _mm_tn(an, do_c) + _mm(kd, dsp)
                do_dvn = jnp.concatenate([do_c, dvn], axis=0)
                by_s = _mm_nt(do_dvn, s)
                dqd, dw = by_s[:CHUNK], -by_s[CHUNK:]
                da = _mm_nt(do_c, vn)
                dkd = _mm_nt(vn, dsp)
                dgl = jnp.sum(jnp.sum(dsp * s, axis=-1, keepdims=True), axis=0, keepdims=True)
                dsp = _mm_tn(jnp.concatenate([qd, -w], axis=0), do_dvn) + gl * dsp
                dvn_dw = jnp.concatenate([dvn, dw], axis=1)
                dt = _mm_nt(dvn_dw, jnp.concatenate([vb, kbg], axis=1))
                by_t = _mm_tn(tn, dvn_dw)
                dvb, dkbg = by_t[:, :LANES], by_t[:, LANES:]
                decay, strict = _chunk_decay(gcc)
                by_k = _mm_nt(jnp.concatenate([kb, q], axis=0), k)
                kk, qk = by_k[:CHUNK], by_k[CHUNK:]
                dm = jnp.where(strict, -_mm_nt(_mm_tn(tn, dt), tn), 0.0)
                dkk = dm * decay
                dqk = da * decay
                gmat = dkk * kk + dqk * qk
                dqk_dkk = jnp.concatenate([dqk, dkk], axis=0)
                on_k = _mm(dqk_dkk, k)
                dq_ref[rs, cs] = dqd * eg + on_k[:CHUNK]
                dkb = on_k[CHUNK:] + dkbg * eg
                dk_ref[rs, cs] = dkd * egl + _mm_tn(dqk_dkk, jnp.concatenate([q, kb], axis=0)) + dkb * b
                db = jnp.sum(dkb * k, axis=-1, keepdims=True) + jnp.sum(dvb * v, axis=-1, keepdims=True)
                db_ref[rs, cs] = jnp.broadcast_to(db, (CHUNK, LANES))
                dv_ref[rs, cs] = dvb * b
                dkd_kd = jnp.sum(dkd * kd, axis=-1, keepdims=True)
                col_sums = jnp.sum(jnp.concatenate([gmat, jnp.zeros_like(gmat)], axis=1).T, axis=-1, keepdims=True)
                dgc = (jnp.sum(gmat, axis=-1, keepdims=True) - col_sums[:CHUNK]
                       + jnp.sum(dqd * qd, axis=-1, keepdims=True) + jnp.sum(dkbg * kbg, axis=-1, keepdims=True) - dkd_kd)
                extra = jnp.sum(dkd_kd, axis=0, keepdims=True) + dgl * gl
                dg_ref[rs, cs] = dgc + jnp.where(lastrow, extra, 0.0)
            ds_scr[hd] = dsp
        dg = dg_ref[...]
        row = _iota(dg.shape, 0)
        pos = row % CHUNK
        step = 1
        while step < CHUNK:
            dg = dg + jnp.where(pos < CHUNK - step, pltpu.roll(dg, dg.shape[0] - step, 0), 0.0)
            step *= 2
        dg_ref[...] = dg

    nb, tok, mat, state = _gdn_block_specs(t, True)
    return pl.pallas_call(
        body, name="gdn_bwd", grid=(nb,),
        in_specs=[tok] * 5 + [mat, mat, state, tok, tok], out_specs=[tok] * 5, out_shape=[_sds((t, D_GDN))] * 5,
        scratch_shapes=[pltpu.VMEM((N_GDN_HEADS, GDN_HEAD_DIM, GDN_HEAD_DIM), F32)],
        compiler_params=_params("arbitrary"),
    )(qn, kn, cv, be, gc, tinv, amat, s_all, vn_all, do)


def _gdn_bwd_conv(proj, convw, dqn, dkn, dcv):
    t = proj.shape[0]

    def body(xq_ref, xk_ref, xv_ref, wq_ref, wk_ref, wv_ref, dq_ref, dk_ref, dv_ref,
             dxq_ref, dxk_ref, dxv_ref, dwq_ref, dwk_ref, dwv_ref):
        row = _iota((t, LANES), 0)

        def one(x_ref, w_ref, d_ref, dx_ref, dw_ref, scale):
            x = x_ref[...]
            w = w_ref[...]
            y = _conv(x, w, row)
            sg = _sigmoid(y)
            dc = d_ref[...]
            if scale is not None:
                c = y * sg
                r = lax.rsqrt(jnp.sum(c * c, axis=-1, keepdims=True) + EPS)
                ch = c * r
                dc = scale * r * (dc - ch * jnp.sum(dc * ch, axis=-1, keepdims=True))
            dy = dc * sg * (1.0 + y * (1.0 - sg))
            dx_ref[...] = (w[3:4, :] * dy + w[2:3, :] * _shift_up(dy, 1, row) + w[1:2, :] * _shift_up(dy, 2, row)
                           + w[0:1, :] * _shift_up(dy, 3, row))
            for jj in range(CONV_K):
                xs = x if jj == CONV_K - 1 else _shift_down(x, CONV_K - 1 - jj, row)
                dw_ref[jj:jj + 1, :] = jnp.sum(dy * xs, axis=0, keepdims=True)

        one(xq_ref, wq_ref, dq_ref, dxq_ref, dwq_ref, GDN_QSCALE)
        one(xk_ref, wk_ref, dk_ref, dxk_ref, dwk_ref, 1.0)
        one(xv_ref, wv_ref, dv_ref, dxv_ref, dwv_ref, None)

    col, cw, _ = _gdn_specs(t)
    return pl.pallas_call(
        body, name="gdn_bwd_conv", grid=(N_GDN_HEADS,),
        in_specs=[col(12), col(16), col(20), cw(0), cw(4), cw(8), col(0), col(0), col(0)],
        out_specs=[col(0), col(0), col(0), cw(0), cw(0), cw(0)],
        out_shape=[_sds((t, D_GDN))] * 3 + [_sds((CONV_K, D_GDN))] * 3,
        compiler_params=_params("parallel"),
    )(proj, proj, proj, convw, convw, convw, dqn, dkn, dcv)


def _mix_out(fox_n, gdn_o, proj, gnw, w_out, x, pmw, plw):
    t = x.shape[0]
    tm = TOKEN_BLOCK

    def body(fn_ref, go_ref, gz_ref, gnw_ref, w_ref, x_ref, pmw_ref, plw_ref, x1_ref, h2_ref, mixed_ref, omix_ref,
             h2t_ref):
        omix_ref[:, 0:D_FOX] = fn_ref[...]
        for hd in range(N_GDN_HEADS):
            cs = slice(hd * LANES, (hd + 1) * LANES)
            go = go_ref[:, cs]
            r = lax.rsqrt(jnp.mean(go * go, axis=-1, keepdims=True) + EPS)
            gz = gz_ref[:, cs]
            omix_ref[:, D_FOX + hd * LANES:D_FOX + (hd + 1) * LANES] = (
                go * r * gnw_ref[...] * (gz * _sigmoid(gz))).astype(BF)
        mixed = jnp.dot(omix_ref[...], w_ref[...], preferred_element_type=F32)
        mixed_ref[...] = mixed
        r2 = lax.rsqrt(jnp.mean(mixed * mixed, axis=-1, keepdims=True) + EPS)
        x1 = x_ref[...] + mixed * r2 * pmw_ref[...]
        x1_ref[...] = x1
        r3 = lax.rsqrt(jnp.mean(x1 * x1, axis=-1, keepdims=True) + EPS)
        h2 = x1 * r3 * plw_ref[...]
        h2_ref[...] = h2.astype(BF)
        h2t_ref[...] = h2.T.astype(BF)

    tok = lambda w: pl.BlockSpec((tm, w), lambda i: (i, 0))
    vec = lambda w: pl.BlockSpec((1, w), lambda i: (0, 0))
    return pl.pallas_call(
        body, name="mix_out", grid=(t // tm,),
        in_specs=[tok(D_FOX), tok(D_GDN), pl.BlockSpec((tm, D_GDN), lambda i: (i, COL_GZ // D_GDN)), vec(LANES),
                  pl.BlockSpec((D_MODEL, D_MODEL), lambda i: (0, 0)), tok(D_MODEL), vec(D_MODEL), vec(D_MODEL)],
        out_specs=[tok(D_MODEL)] * 4 + [pl.BlockSpec((D_MODEL, tm), lambda i: (0, i))],
        out_shape=[_sds((t, D_MODEL)), _sds((t, D_MODEL), BF), _sds((t, D_MODEL)), _sds((t, D_MODEL), BF),
                   _sds((D_MODEL, t), BF)],
        compiler_params=_params("parallel"),
    )(fox_n, gdn_o, proj, gnw, w_out, x, pmw, plw)


def _out_bwd(dmixed, w_out, o_fox, gdn_o, proj, fnw, gnw):
    t = dmixed.shape[0]
    tm = TOKEN_BLOCK

    def body(dm_ref, w_ref, of_ref, go_ref, gz_ref, fnw_ref, gnw_ref, dof_ref, dgo_ref, dgz_ref, dfw_ref, dgw_ref):
        i = pl.program_id(0)

        @pl.when(i == 0)
        def _():
            dfw_ref[...] = jnp.zeros_like(dfw_ref)
            dgw_ref[...] = jnp.zeros_like(dgw_ref)

        domix = _mm_nt(dm_ref[...], w_ref[...])
        first = _iota((1, LANES), 1) < FOX_HEAD_DIM
        dfw = jnp.zeros((1, LANES), F32)
        dgw = jnp.zeros((1, LANES), F32)
        for pr in range(N_FOX_HEADS // 2):
            cs = slice(pr * LANES, (pr + 1) * LANES)
            o = of_ref[:, cs]
            dfn = domix[:, cs]
            o2 = o * o
            s0 = jnp.sum(jnp.where(first, o2, 0.0), axis=-1, keepdims=True)
            s1 = jnp.sum(jnp.where(first, 0.0, o2), axis=-1, keepdims=True)
            r = lax.rsqrt(jnp.where(first, s0, s1) * (1.0 / FOX_HEAD_DIM) + EPS)
            oh = o * r
            dfw = dfw + jnp.sum(dfn * oh, axis=0, keepdims=True)
            doh = dfn * fnw_ref[...]
            pr_ = doh * oh
            m0 = jnp.sum(jnp.where(first, pr_, 0.0), axis=-1, keepdims=True)
            m1 = jnp.sum(jnp.where(first, 0.0, pr_), axis=-1, keepdims=True)
            dof_ref[:, cs] = r * (doh - oh * jnp.where(first, m0, m1) * (1.0 / FOX_HEAD_DIM))
        for hd in range(N_GDN_HEADS):
            cs = slice(hd * LANES, (hd + 1) * LANES)
            go = go_ref[:, cs]
            gz = gz_ref[:, cs]
            dgated = domix[:, D_FOX + hd * LANES:D_FOX + (hd + 1) * LANES]
            r = lax.rsqrt(jnp.mean(go * go, axis=-1, keepdims=True) + EPS)
            goh = go * r
            sg = _sigmoid(gz)
            sz = gz * sg
            gn = goh * gnw_ref[...]
            dgn = dgated * sz
            dgz_ref[:, cs] = dgated * gn * sg * (1.0 + gz * (1.0 - sg))
            dgw = dgw + jnp.sum(dgn * goh, axis=0, keepdims=True)
            dgh = dgn * gnw_ref[...]
            dgo_ref[:, cs] = r * (dgh - goh * jnp.mean(dgh * goh, axis=-1, keepdims=True))
        dfw_ref[...] += dfw + pltpu.roll(dfw, FOX_HEAD_DIM, 1)
        dgw_ref[...] += dgw

    tok = lambda w: pl.BlockSpec((tm, w), lambda i: (i, 0))
    vec = lambda w: pl.BlockSpec((1, w), lambda i: (0, 0))
    return pl.pallas_call(
        body, name="out_bwd", grid=(t // tm,),
        in_specs=[tok(D_MODEL), pl.BlockSpec((D_MODEL, D_MODEL), lambda i: (0, 0)), tok(D_FOX), tok(D_GDN),
                  pl.BlockSpec((tm, D_GDN), lambda i: (i, COL_GZ // D_GDN)), vec(LANES), vec(LANES)],
        out_specs=[tok(D_FOX), tok(D_GDN), tok(D_GDN), vec(LANES), vec(LANES)],
        out_shape=[_sds((t, D_FOX)), _sds((t, D_GDN)), _sds((t, D_GDN)), _sds((1, LANES)), _sds((1, LANES))],
        compiler_params=_params("arbitrary"),
    )(dmixed, w_out, o_fox, gdn_o, proj, fnw, gnw)


def _mlp_up(h2, w_up):
    t = h2.shape[0]
    tm = TOKEN_BLOCK
    pc = D_FF // N_DEV

    def body(h_ref, w_ref, up_ref):
        h = h_ref[...]
        for p in range(N_DEV):
            up_ref[:, p * pc:(p + 1) * pc] = jnp.dot(h, w_ref[p], preferred_element_type=F32)

    return pl.pallas_call(
        body, name="mlp_up", grid=(t // tm,),
        in_specs=[pl.BlockSpec((tm, D_MODEL), lambda i: (i, 0)),
                  pl.BlockSpec((N_DEV, D_MODEL, pc), lambda i: (0, 0, 0))],
        out_specs=pl.BlockSpec((tm, D_FF), lambda i: (i, 0)), out_shape=_sds((t, D_FF)),
        compiler_params=_params("parallel"),
    )(h2, w_up)


def _mlp_down_loss(up, w_down, x1, pw, target):
    t = up.shape[0]
    tm = TOKEN_BLOCK

    def body(up_ref, w_ref, x1_ref, pw_ref, tg_ref, dy_ref, dx2_ref, loss_ref, dpw_ref):
        i = pl.program_id(0)

        @pl.when(i == 0)
        def _():
            loss_ref[...] = jnp.zeros_like(loss_ref)
            dpw_ref[...] = jnp.zeros_like(dpw_ref)

        u = jnp.maximum(up_ref[...], 0.0)
        y = jnp.dot((u * u).astype(BF), w_ref[...], preferred_element_type=F32)
        r = lax.rsqrt(jnp.mean(y * y, axis=-1, keepdims=True) + EPS)
        yh = y * r
        pw = pw_ref[...]
        err = x1_ref[...] + yh * pw - tg_ref[...]
        part = jnp.sum(jnp.sum(err * err, axis=-1, keepdims=True), axis=0, keepdims=True) * (0.5 / D_MODEL)
        loss_ref[...] += jnp.broadcast_to(part, loss_ref.shape)
        dx2 = err * (1.0 / D_MODEL)
        dx2_ref[...] = dx2
        dpw_ref[...] += jnp.sum(dx2 * yh, axis=0, keepdims=True)
        dyh = dx2 * pw
        dy_ref[...] = (r * (dyh - yh * jnp.mean(dyh * yh, axis=-1, keepdims=True))).astype(BF)

    tok = lambda w: pl.BlockSpec((tm, w), lambda i: (i, 0))
    vec = lambda w: pl.BlockSpec((1, w), lambda i: (0, 0))
    return pl.pallas_call(
        body, name="mlp_down_loss", grid=(t // tm,),
        in_specs=[tok(D_FF), pl.BlockSpec((D_FF, D_MODEL), lambda i: (0, 0)), tok(D_MODEL), vec(D_MODEL), tok(D_MODEL)],
        out_specs=[tok(D_MODEL), tok(D_MODEL), vec(LANES), vec(D_MODEL)],
        out_shape=[_sds((t, D_MODEL), BF), _sds((t, D_MODEL)), _sds((1, LANES)), _sds((1, D_MODEL))],
        compiler_params=_params("arbitrary"),
    )(up, w_down, x1, pw, target)


def _mlp_bwd_act(dy, w_down, up):
    t = dy.shape[0]
    tm = TOKEN_BLOCK

    def body(dy_ref, w_ref, up_ref, dup_ref):
        da = lax.dot_general(dy_ref[...], w_ref[...], (((1,), (1,)), ((), ())), preferred_element_type=F32)
        dup_ref[...] = (da * (2.0 * jnp.maximum(up_ref[...], 0.0))).astype(BF)

    return pl.pallas_call(
        body, name="mlp_bwd_act", grid=(t // tm,),
        in_specs=[pl.BlockSpec((tm, D_MODEL), lambda i: (i, 0)), pl.BlockSpec((D_FF, D_MODEL), lambda i: (0, 0)),
                  pl.BlockSpec((tm, D_FF), lambda i: (i, 0))],
        out_specs=pl.BlockSpec((tm, D_FF), lambda i: (i, 0)), out_shape=_sds((t, D_FF), BF),
        compiler_params=_params("parallel"),
    )(dy, w_down, up)


def _mlp_bwd_in(dup, w_up, x1, plw, dx2, mixed, pmw):
    t = dup.shape[0]
    tm = TOKEN_BLOCK

    def body(dup_ref, w_ref, x1_ref, plw_ref, dx2_ref, mx_ref, pmw_ref, dx1_ref, dmixed_ref, dplw_ref, dpmw_ref):
        i = pl.program_id(0)

        @pl.when(i == 0)
        def _():
            dplw_ref[...] = jnp.zeros_like(dplw_ref)
            dpmw_ref[...] = jnp.zeros_like(dpmw_ref)

        pc = D_FF // N_DEV
        dh = _mm_nt(dup_ref[:, 0:pc], w_ref[0])
        for p in range(1, N_DEV):
            dh = dh + _mm_nt(dup_ref[:, p * pc:(p + 1) * pc], w_ref[p])
        x1 = x1_ref[...]
        r = lax.rsqrt(jnp.mean(x1 * x1, axis=-1, keepdims=True) + EPS)
        xh = x1 * r
        dplw_ref[...] += jnp.sum(dh * xh, axis=0, keepdims=True)
        dxh = dh * plw_ref[...]
        dx1 = dx2_ref[...] + r * (dxh - xh * jnp.mean(dxh * xh, axis=-1, keepdims=True))
        dx1_ref[...] = dx1
        mx = mx_ref[...]
        r2 = lax.rsqrt(jnp.mean(mx * mx, axis=-1, keepdims=True) + EPS)
        mh = mx * r2
        dpmw_ref[...] += jnp.sum(dx1 * mh, axis=0, keepdims=True)
        dmh = dx1 * pmw_ref[...]
        dmixed_ref[...] = (r2 * (dmh - mh * jnp.mean(dmh * mh, axis=-1, keepdims=True))).astype(BF)

    tok = lambda w: pl.BlockSpec((tm, w), lambda i: (i, 0))
    vec = lambda w: pl.BlockSpec((1, w), lambda i: (0, 0))
    return pl.pallas_call(
        body, name="mlp_bwd_in", grid=(t // tm,),
        in_specs=[tok(D_FF), pl.BlockSpec((N_DEV, D_MODEL, D_FF // N_DEV), lambda i: (0, 0, 0)), tok(D_MODEL),
                  vec(D_MODEL), tok(D_MODEL), tok(D_MODEL), vec(D_MODEL)],
        out_specs=[tok(D_MODEL), tok(D_MODEL), vec(D_MODEL), vec(D_MODEL)],
        out_shape=[_sds((t, D_MODEL)), _sds((t, D_MODEL), BF), _sds((1, D_MODEL)), _sds((1, D_MODEL))],
        compiler_params=_params("arbitrary"),
    )(dup, w_up, x1, plw, dx2, mixed, pmw)


def _wgrad(a, b, a_cols, split=1, a_fn=None, a_block0=0, name="wgrad"):
    t, b_cols = b.shape
    n_a = (a.shape[1] - a_block0 * a_cols) // a_cols if a_block0 else a.shape[1] // a_cols

    def body(a_ref, b_ref, o_ref):
        av = a_ref[...]
        if a_fn is not None:
            av = a_fn(av)
        o_ref[...] = _mm_tn(av, b_ref[...]).astype(BF).reshape(o_ref.shape)

    return pl.pallas_call(
        body, name=name, grid=(n_a,),
        in_specs=[pl.BlockSpec((t, a_cols), lambda i: (0, i + a_block0)), pl.BlockSpec((t, b_cols), lambda i: (0, 0))],
        out_specs=pl.BlockSpec((split, a_cols // split, b_cols), lambda i: (i, 0, 0)),
        out_shape=_sds((n_a * split, a_cols // split, b_cols), BF),
        compiler_params=_params("parallel"),
    )(a, b)


def _wgrad_pre_t(at, b, b_cols, name):
    rows, t = at.shape
    n_b = b.shape[1] // b_cols

    def body(a_ref, b_ref, o_ref):
        o_ref[0] = jnp.dot(a_ref[...], b_ref[...], preferred_element_type=F32).astype(BF)

    return pl.pallas_call(
        body, name=name, grid=(n_b,),
        in_specs=[pl.BlockSpec((rows, t), lambda j: (0, 0)), pl.BlockSpec((t, b_cols), lambda j: (0, j))],
        out_specs=pl.BlockSpec((1, rows, b_cols), lambda j: (j, 0, 0)), out_shape=_sds((n_b, rows, b_cols), BF),
        compiler_params=_params("parallel"),
    )(at, b)


def _select_matrix(rows, fn):
    r = _iota((rows, LANES), 0)
    c = _iota((rows, LANES), 1)
    return (r == fn(c)).astype(F32)


def _small_bwd(proj, fb, al, dtb, dcq, dckt, dbe, dge):
    t = proj.shape[0]

    def body(sm_ref, fb_ref, al_ref, dtb_ref, dcq_ref, dckt_ref, dbe_ref, dge_ref, dsm_ref, dvec_ref):
        s = sm_ref[...]
        lane = _iota((1, LANES), 1)
        sel_f = _select_matrix(512, lambda c: jnp.where(c < 8, FOX_HEAD_DIM * c, -1))
        sel_k = _select_matrix(LANES, lambda c: jnp.where(c < 8, 32 * (c // 2) + c % 2, -1))
        dcum = _mm_exact(dcq_ref[...], sel_f) - _mm_exact(dckt_ref[...].T, sel_k)
        row = _iota((t, LANES), 0)
        step = 1
        while step < t:
            dcum = dcum + _shift_up(dcum, step, row)
            step *= 2
        dff = dcum * _sigmoid(-(s + fb_ref[...]))
        sel_b = _select_matrix(512, lambda c: jnp.where((c >= SM_GB) & (c < SM_GA), LANES * (c - SM_GB), -1))
        sel_g = _select_matrix(512, lambda c: jnp.where((c >= SM_GA) & (c < SM_GA + 4), LANES * (c - SM_GA), -1))
        beta = _sigmoid(s)
        dgb = _mm_exact(dbe_ref[...], sel_b) * beta * (1.0 - beta)
        dg = _mm_exact(dge_ref[...], sel_g)
        za = s + dtb_ref[...]
        nea = -jnp.exp(al_ref[...])
        dga = dg * nea * _sigmoid(za)
        is_f = lane < SM_GB
        is_b = (lane >= SM_GB) & (lane < SM_GA)
        is_a = (lane >= SM_GA) & (lane < SM_GA + 4)
        dsm_ref[...] = jnp.where(is_f, dff, jnp.where(is_b, dgb, jnp.where(is_a, dga, 0.0)))
        dvec_ref[...] = jnp.zeros_like(dvec_ref)
        dvec_ref[0:1, :] = jnp.sum(jnp.where(is_f, dff, 0.0), axis=0, keepdims=True)
        dvec_ref[1:2, :] = jnp.sum(jnp.where(is_a, dg * nea * _softplus(za), 0.0), axis=0, keepdims=True)
        dvec_ref[2:3, :] = jnp.sum(jnp.where(is_a, dga, 0.0), axis=0, keepdims=True)

    vec = pl.BlockSpec((1, LANES), lambda i: (0, 0))
    full = lambda r, c: pl.BlockSpec((r, c), lambda i: (0, 0))
    return pl.pallas_call(
        body, name="small_bwd", grid=(1,),
        in_specs=[pl.BlockSpec((t, LANES), lambda i: (0, COL_SMALL // LANES)), vec, vec, vec, full(t, 512),
                  full(LANES, t), full(t, 512), full(t, 512)],
        out_specs=[full(t, LANES), full(8, LANES)], out_shape=[_sds((t, LANES)), _sds((8, LANES))],
        compiler_params=_params("arbitrary"),
    )(proj, fb, al, dtb, dcq, dckt, dbe, dge)


def _pack_dproj(dfox, dgdn, dgz, dsm):
    t = dgz.shape[0]
    tm = TOKEN_BLOCK

    def body(*refs):
        parts, dp_ref = refs[:8], refs[8]
        col = 0
        for part in parts:
            width = part.shape[1]
            dp_ref[:, col:col + width] = part[...].astype(BF)
            col += width

    tok = lambda w: pl.BlockSpec((tm, w), lambda i: (i, 0))
    return pl.pallas_call(
        body, name="pack_dproj", grid=(t // tm,), in_specs=[tok(D_FOX)] * 3 + [tok(D_GDN)] * 4 + [tok(LANES)],
        out_specs=tok(PROJ_W), out_shape=_sds((t, PROJ_W), BF), compiler_params=_params("parallel"),
    )(*dfox, *dgdn, dgz, dsm)


def _in_bwd(dproj, wt_al, x, nw, dx1):
    t = x.shape[0]
    tm = TOKEN_BLOCK

    def body(dp_ref, w_ref, x_ref, nw_ref, dx1_ref, dx_ref, dnw_ref):
        i = pl.program_id(0)

        @pl.when(i == 0)
        def _():
            dnw_ref[...] = jnp.zeros_like(dnw_ref)

        dh = jnp.dot(dp_ref[...], w_ref[...], preferred_element_type=F32)
        xv = x_ref[...]
        r = lax.rsqrt(jnp.mean(xv * xv, axis=-1, keepdims=True) + EPS)
        xh = xv * r
        dnw_ref[...] += jnp.sum(dh * xh, axis=0, keepdims=True)
        dxh = dh * nw_ref[...]
        dx_ref[...] = dx1_ref[...] + r * (dxh - xh * jnp.mean(dxh * xh, axis=-1, keepdims=True))

    tok = lambda w: pl.BlockSpec((tm, w), lambda i: (i, 0))
    vec = lambda w: pl.BlockSpec((1, w), lambda i: (0, 0))
    return pl.pallas_call(
        body, name="in_bwd", grid=(t // tm,),
        in_specs=[tok(PROJ_W), pl.BlockSpec((PROJ_W, D_MODEL), lambda i: (0, 0)), tok(D_MODEL), vec(D_MODEL),
                  tok(D_MODEL)],
        out_specs=[tok(D_MODEL), vec(D_MODEL)], out_shape=[_sds((t, D_MODEL)), _sds((1, D_MODEL))],
        compiler_params=_params("arbitrary"),
    )(dproj, wt_al, x, nw, dx1)


def _row(v, width=None):
    v = v.reshape(1, -1).astype(F32)
    if width is not None and v.shape[1] < width:
        v = jnp.pad(v, ((0, 0), (0, width - v.shape[1])))
    return v


def _lane_vec(v, first):
    return jnp.zeros((1, LANES), F32).at[0, first:first + v.shape[0]].set(v.astype(F32))


def _local_step(x, target, wt_al, late_weights, on_grads, convw, pre_mix_norm, fox_f_bias, fox_out_norm,
                gdn_a_log, gdn_dt_bias, gdn_out_norm, post_mix_norm, pre_mlp_norm, post_mlp_norm):
    t = x.shape[0]
    nch = t // CHUNK
    nw, pmw, plw, pw = _row(pre_mix_norm), _row(post_mix_norm), _row(pre_mlp_norm), _row(post_mlp_norm)
    fb, al, dtb = _lane_vec(fox_f_bias, SM_FF), _lane_vec(gdn_a_log, SM_GA), _lane_vec(gdn_dt_bias, SM_GA)
    fnw = _row(jnp.tile(fox_out_norm, 2))
    gnw = _row(gdn_out_norm)

    proj, h = _norm_proj(x, nw, wt_al)
    ce, cumt, be, ge = _small_prep(proj, fb, al, dtb)
    o_fox, lse, fox_n = _fox_fwd(proj, ce, cumt, fnw)
    qn, kn, cv, gc, mmat, amat = _gdn_prep(proj, convw, be, ge)
    n_prob = N_GDN_HEADS * nch
    m2 = mmat.reshape(n_prob, CHUNK * CHUNK)
    if n_prob < LANES:
        m2 = jnp.pad(m2, ((0, LANES - n_prob), (0, 0)))
    tinv = _tri_inverse(m2)[:n_prob].reshape(N_GDN_HEADS, nch, CHUNK, CHUNK)
    gdn_o, s_all, vn_all = _gdn_scan(qn, kn, cv, be, gc, tinv, amat)
    w_out = late_weights("w_out", gdn_o)
    x1, h2, mixed, omix, h2t = _mix_out(fox_n, gdn_o, proj, gnw, w_out, x, pmw, plw)
    w_up, w_down = late_weights("mlp", h2)
    up = _mlp_up(h2, w_up)
    dy, dx2, loss, d_pw = _mlp_down_loss(up, w_down, x1, pw, target)

    dup = _mlp_bwd_act(dy, w_down, up)
    relu2 = lambda u: jnp.square(jnp.maximum(u, 0.0))
    g_down = _wgrad(up, dy, D_FF // N_DEV, a_fn=relu2, name="wgrad_down")
    g_up = _wgrad_pre_t(h2t, dup, D_FF // N_DEV, name="wgrad_up")
    token = on_grads("mlp", (g_up, g_down))
    dx1, dmixed, d_plw, d_pmw = _mlp_bwd_in(dup, w_up, x1, plw + token[0:1, 0:1], dx2, mixed, pmw)
    token = on_grads("w_out", _wgrad(omix, dmixed, 512, split=4, name="wgrad_out"))
    do_fox, dgo, dgz, d_fnw, d_gnw = _out_bwd(dmixed, w_out, o_fox, gdn_o, proj, fnw + token[0:1, 0:1], gnw)
    dfq, dfk, dfv, dcq, dckt = _fox_bwd(proj, ce, cumt, lse, o_fox, do_fox)
    dqn, dkn, dcv, dbe, dge = _gdn_bwd(qn, kn, cv, be, gc, tinv, amat, s_all, vn_all, dgo)
    dxq, dxk, dxv, dwq, dwk, dwv = _gdn_bwd_conv(proj, convw, dqn, dkn, dcv)
    dsm, dvec = _small_bwd(proj, fb, al, dtb, dcq, dckt, dbe, dge)
    dproj = _pack_dproj((dfq, dfk, dfv), (dxq, dxk, dxv), dgz, dsm)
    g_main = _wgrad(dproj, h, 512, name="wgrad_in")
    g_tail = _wgrad(dproj, h, LANES, a_block0=COL_SMALL // LANES, name="wgrad_in_small")
    token = on_grads("w_in", jnp.concatenate([g_main.reshape(COL_SMALL, D_MODEL), g_tail[0]]))
    grad_x, d_nw = _in_bwd(dproj, wt_al, x, nw + token[0:1, 0:1], dx1)
    small = dict(loss=loss[0, 0:1], pre_mix_norm=d_nw[0], fox_f_bias=dvec[0, SM_FF:SM_FF + N_FOX_HEADS],
                 fox_out_norm=d_fnw[0, :FOX_HEAD_DIM], gdn_conv_w=(dwq, dwk, dwv),
                 gdn_a_log=dvec[1, SM_GA:SM_GA + N_GDN_HEADS], gdn_dt_bias=dvec[2, SM_GA:SM_GA + N_GDN_HEADS],
                 gdn_out_norm=d_gnw[0], post_mix_norm=d_pmw[0], pre_mlp_norm=d_plw[0], post_mlp_norm=d_pw[0])
    return grad_x, small


MESH_IDS = pl.DeviceIdType.MESH
CHIP_FLIPS = ((0, 0), (1, 0), (0, 1), (1, 1))
ANY_SPEC = pl.BlockSpec(memory_space=pl.ANY)


def _place():
    return lax.axis_index("x"), lax.axis_index("y"), lax.axis_index("c")


def _all_gather(blocks):
    n = len(blocks)

    def body(*refs):
        ins, outs, (send_sems, recv_sems, local_sems) = refs[:n], refs[n:2 * n], refs[2 * n:]
        x, y, c = _place()
        sibling = (x, y, 1 - c)
        chips = [(x ^ fx, y ^ fy) for fx, fy in CHIP_FLIPS[1:]]

        def slot(out, px, py, pc):
            return out.at[4 * px + 2 * py + pc]

        def copy(a, k, block, to, src=None):
            return pltpu.make_async_remote_copy(
                src_ref=slot(outs[a], *block) if src is None else src, dst_ref=slot(outs[a], *block),
                send_sem=send_sems.at[a, k], recv_sem=recv_sems.at[a, k], device_id=to, device_id_type=MESH_IDS)

        pending = []
        for a in range(n):
            mine = pltpu.make_async_copy(ins[a], slot(outs[a], x, y, c), local_sems.at[a])
            mine.start()
            pending.append(mine)
        sends = []
        for a in range(n):
            first = [copy(a, 0, (x, y, c), sibling, src=ins[a])]
            first += [copy(a, 1 + j, (x, y, c), (*chip, c), src=ins[a]) for j, chip in enumerate(chips)]
            for cp in first:
                cp.start()
            sends += first
        for a in range(n):
            for j, chip in enumerate(chips):
                copy(a, 1 + j, (*chip, c), (x, y, c)).wait_recv()
                fwd = copy(a, 4 + j, (*chip, c), sibling)
                fwd.start()
                sends.append(fwd)
        for a in range(n):
            copy(a, 0, sibling, (x, y, c)).wait_recv()
            for j, chip in enumerate(chips):
                copy(a, 4 + j, (*chip, 1 - c), (x, y, c)).wait_recv()
        for cp in sends:
            cp.wait_send()
        for cp in pending:
            cp.wait()

    return pl.pallas_call(
        body, name="all_gather_weights", in_specs=[ANY_SPEC] * n, out_specs=[ANY_SPEC] * n,
        out_shape=[_sds((N_DEV,) + b.shape, b.dtype) for b in blocks],
        scratch_shapes=[pltpu.SemaphoreType.DMA((n, 7)), pltpu.SemaphoreType.DMA((n, 7)), pltpu.SemaphoreType.DMA((n,))],
        compiler_params=pltpu.CompilerParams(has_side_effects=True),
    )(*blocks)


def _adamw(w, g, m, v):
    m = ADAM_B1 * m + (1.0 - ADAM_B1) * g
    v = ADAM_B2 * v + (1.0 - ADAM_B2) * (g * g)
    m_hat = m / (1.0 - ADAM_B1 ** ADAM_STEP)
    v_hat = v / (1.0 - ADAM_B2 ** ADAM_STEP)
    return -ADAM_LR * (m_hat / (jnp.sqrt(v_hat) + ADAM_EPS) + ADAM_WD * w), m, v


HBM_SPEC = pl.BlockSpec(memory_space=pltpu.HBM)
SEM_SPEC = pl.BlockSpec(memory_space=pltpu.SEMAPHORE)
DATAFLOW = pltpu.SideEffectType.DATAFLOW_SIDE_EFFECTING


def _peers():
    x, y, c = _place()
    return 4 * x + 2 * y + c, [(x ^ (k >> 2), y ^ ((k >> 1) & 1), c ^ (k & 1)) for k in range(1, N_DEV)]


def _peer_index(peer):
    return 4 * peer[0] + 2 * peer[1] + peer[2]


def _zones_with_own(srcs, pieces, name, after=None):
    n = len(srcs)
    extra = [] if after is None else [after]

    def body(me_ref, *refs):
        for a in range(n):
            refs[n + len(extra) + a][0] = refs[a][0] if pieces else refs[a][...]

    shapes = [s_.shape[1:] if pieces else s_.shape for s_ in srcs]
    mine = lambda sh: pl.BlockSpec((1,) + sh, lambda i, me_ref: (me_ref[0], 0, 0))
    in_specs = [mine(sh) if pieces else pl.BlockSpec(sh, lambda i, me_ref: (0, 0)) for sh in shapes]
    x, y, c = _place()
    return pl.pallas_call(
        body, name=name,
        grid_spec=pltpu.PrefetchScalarGridSpec(num_scalar_prefetch=1, grid=(1,), in_specs=in_specs + [ANY_SPEC] * len(extra),
                                               out_specs=[mine(sh) for sh in shapes]),
        out_shape=[_sds((N_DEV,) + sh, s_.dtype) for sh, s_ in zip(shapes, srcs)],
        compiler_params=_params("arbitrary"),
    )((4 * x + 2 * y + c).astype(jnp.int32).reshape(1), *srcs, *extra)


def _exchange_start(srcs, zones, pieces, name):
    n = len(srcs)

    def body(*refs):
        ins, zs = refs[:n], refs[n:2 * n]
        sems = refs[2 * n:4 * n]
        token = refs[-1]
        me, peers = _peers()
        for peer in peers:
            for a in range(n):
                pltpu.make_async_remote_copy(
                    src_ref=ins[a].at[_peer_index(peer)] if pieces else ins[a], dst_ref=zs[a].at[me],
                    send_sem=sems[2 * a], recv_sem=sems[2 * a + 1], device_id=peer, device_id_type=MESH_IDS).start()
        token[...] = jnp.zeros_like(token)

    hbm = lambda v: pltpu.with_memory_space_constraint(v, pltpu.HBM)
    out = pl.pallas_call(
        body, name=name,
        out_shape=tuple([pltpu.SemaphoreType.DMA(())] * (2 * n) + [pltpu.HBM(v.shape, v.dtype) for v in srcs]
                        + [pltpu.HBM(z.shape, z.dtype) for z in zones] + [_sds((8, LANES))]),
        in_specs=[HBM_SPEC] * (2 * n), out_specs=tuple([SEM_SPEC] * (2 * n) + [HBM_SPEC] * (2 * n) + [VMEM_SPEC]),
        input_output_aliases={i: 2 * n + i for i in range(2 * n)},
        compiler_params=pltpu.CompilerParams(has_side_effects=DATAFLOW),
    )(*[hbm(v) for v in srcs], *[hbm(z) for z in zones])
    return out[:2 * n], out[2 * n:3 * n], out[3 * n:4 * n], out[-1]


def _exchange_wait(sems, srcs, zones, after, name):
    n = len(srcs)
    after = list(after) if isinstance(after, (list, tuple)) else [after]

    def body(*refs):
        ins, zs, sm = refs[:n], refs[n:2 * n], refs[2 * n:4 * n]
        me, peers = _peers()
        for a in range(n):
            seven = zs[a].at[pl.ds(0, N_DEV - 1)]
            cp = pltpu.make_async_remote_copy(src_ref=seven, dst_ref=seven, send_sem=sm[2 * a], recv_sem=sm[2 * a + 1],
                                              device_id=peers[0], device_id_type=MESH_IDS)
            cp.wait_send()
            cp.wait_recv()

    out = pl.pallas_call(
        body, name=name, out_shape=tuple([pltpu.HBM(v.shape, v.dtype) for v in srcs] + [pltpu.HBM(z.shape, z.dtype) for z in zones]),
        in_specs=[HBM_SPEC] * (2 * n) + [SEM_SPEC] * (2 * n) + [ANY_SPEC] * len(after),
        out_specs=tuple([HBM_SPEC] * (2 * n)), input_output_aliases={i: i for i in range(2 * n)},
        compiler_params=pltpu.CompilerParams(has_side_effects=DATAFLOW),
    )(*srcs, *zones, *sems, *after)
    return out[n:]


def _sum_adamw(zone, w, m, v, name):
    _, r, c_ = zone.shape
    rb = 128 if r % 128 == 0 else r

    def body(z_ref, w_ref, m_ref, v_ref, grad_ref, delta_ref, nm_ref, nv_ref):
        total = z_ref[0].astype(F32)
        for d in range(1, N_DEV):
            total = total + z_ref[d].astype(F32)
        grad_ref[...] = total
        delta_ref[...], nm_ref[...], nv_ref[...] = _adamw(w_ref[...], total, m_ref[...], v_ref[...])

    blk = pl.BlockSpec((rb, c_), lambda i: (i, 0))
    return pl.pallas_call(
        body, name=name, grid=(r // rb,), in_specs=[pl.BlockSpec((N_DEV, rb, c_), lambda i: (0, i, 0)), blk, blk, blk],
        out_specs=[blk] * 4, out_shape=[_sds((r, c_))] * 4, compiler_params=_params("parallel"),
    )(zone, w, m, v)


SMALL_ROWS = 16


def _sum_small(zone):
    def body(z_ref, sum_ref):
        total = z_ref[0]
        for d in range(1, N_DEV):
            total = total + z_ref[d]
        sum_ref[...] = total

    return pl.pallas_call(body, name="sum_small", in_specs=[VMEM_SPEC], out_specs=VMEM_SPEC,
                          out_shape=_sds(zone.shape[1:]))(zone)


def _adamw_small(w, g, m, v):
    def body(w_ref, g_ref, m_ref, v_ref, delta_ref, nm_ref, nv_ref):
        delta_ref[...], nm_ref[...], nv_ref[...] = _adamw(w_ref[...], g_ref[...], m_ref[...], v_ref[...])

    return pl.pallas_call(body, name="adamw_small", in_specs=[VMEM_SPEC] * 4, out_specs=[VMEM_SPEC] * 3,
                          out_shape=[_sds(w.shape)] * 3)(w, g, m, v)


NATIVE_ROWS = ((0, 1536), (1544, 3080), (3088, 3600), (1536, 1544), (3080, 3088))


def _to_aligned_rows(wt_native):
    pad = jnp.zeros((PROJ_W - D_PROJ, wt_native.shape[1]), wt_native.dtype)
    return jnp.concatenate([wt_native[lo:hi] for lo, hi in NATIVE_ROWS] + [pad])


def _from_aligned_rows(gt_al):
    return jnp.concatenate([gt_al[0:1536], gt_al[3584:3592], gt_al[1536:3072], gt_al[3592:3600], gt_al[3072:3584]])


def _cols_from_pieces(p):
    return p.transpose(1, 0, 2).reshape(p.shape[1], -1)


SMALL_NORMS = ("pre_mix_norm", "post_mix_norm", "pre_mlp_norm", "post_mlp_norm")
SMALL_MISC = (("fox_out_norm", FOX_HEAD_DIM), ("gdn_out_norm", GDN_HEAD_DIM), ("fox_f_bias", N_FOX_HEADS),
              ("gdn_a_log", N_GDN_HEADS), ("gdn_dt_bias", N_GDN_HEADS), ("loss", 1))


def _pack_small(vals, conv):
    misc = jnp.concatenate([vals[n].astype(F32) if n in vals else jnp.zeros((size,), F32) for n, size in SMALL_MISC])
    rows = [vals[n].astype(F32) for n in SMALL_NORMS] + [jnp.pad(misc, (0, D_MODEL - misc.shape[0]))]
    flat = conv.astype(F32).reshape(-1)
    n_rows = -(-flat.shape[0] // D_MODEL)
    flat = jnp.pad(flat, (0, n_rows * D_MODEL - flat.shape[0])).reshape(n_rows, D_MODEL)
    packed = jnp.concatenate([jnp.stack(rows), flat])
    return jnp.pad(packed, ((0, SMALL_ROWS - packed.shape[0]), (0, 0)))


def _unpack_small(packed, conv_shape):
    out = {n: packed[i] for i, n in enumerate(SMALL_NORMS)}
    off = 0
    for n, size in SMALL_MISC:
        out[n] = packed[4, off:off + size]
        off += size
    n_conv = conv_shape[0] * conv_shape[1]
    out["gdn_conv_w"] = packed[5:].reshape(-1)[:n_conv].reshape(conv_shape)
    return out


WEIGHT_ORDER = ("pre_mix_norm", "w_in", "fox_f_bias", "fox_out_norm", "gdn_conv_w", "gdn_a_log", "gdn_dt_bias",
                "gdn_out_norm", "w_out", "post_mix_norm", "pre_mlp_norm", "w_up", "w_down", "post_mlp_norm")


def kernel(x, pre_mix_norm, w_in, fox_f_bias, fox_out_norm, gdn_conv_w, gdn_a_log, gdn_dt_bias, gdn_out_norm, w_out, post_mix_norm, pre_mlp_norm, w_up, w_down, post_mlp_norm, loss_target, m_pre_mix_norm, m_w_in, m_fox_f_bias, m_fox_out_norm, m_gdn_conv_w, m_gdn_a_log, m_gdn_dt_bias, m_gdn_out_norm, m_w_out, m_post_mix_norm, m_pre_mlp_norm, m_w_up, m_w_down, m_post_mlp_norm, v_pre_mix_norm, v_w_in, v_fox_f_bias, v_fox_out_norm, v_gdn_conv_w, v_gdn_a_log, v_gdn_dt_bias, v_gdn_out_norm, v_w_out, v_post_mix_norm, v_pre_mlp_norm, v_w_up, v_w_down, v_post_mlp_norm):
    w = dict(pre_mix_norm=pre_mix_norm, w_in=w_in, fox_f_bias=fox_f_bias, fox_out_norm=fox_out_norm,
             gdn_conv_w=gdn_conv_w, gdn_a_log=gdn_a_log, gdn_dt_bias=gdn_dt_bias, gdn_out_norm=gdn_out_norm, w_out=w_out,
             post_mix_norm=post_mix_norm, pre_mlp_norm=pre_mlp_norm, w_up=w_up, w_down=w_down, post_mlp_norm=post_mlp_norm)
    mom = dict(pre_mix_norm=m_pre_mix_norm, w_in=m_w_in, fox_f_bias=m_fox_f_bias, fox_out_norm=m_fox_out_norm,
               gdn_conv_w=m_gdn_conv_w, gdn_a_log=m_gdn_a_log, gdn_dt_bias=m_gdn_dt_bias, gdn_out_norm=m_gdn_out_norm,
               w_out=m_w_out, post_mix_norm=m_post_mix_norm, pre_mlp_norm=m_pre_mlp_norm, w_up=m_w_up, w_down=m_w_down,
               post_mlp_norm=m_post_mlp_norm)
    var = dict(pre_mix_norm=v_pre_mix_norm, w_in=v_w_in, fox_f_bias=v_fox_f_bias, fox_out_norm=v_fox_out_norm,
               gdn_conv_w=v_gdn_conv_w, gdn_a_log=v_gdn_a_log, gdn_dt_bias=v_gdn_dt_bias, gdn_out_norm=v_gdn_out_norm,
               w_out=v_w_out, post_mix_norm=v_post_mix_norm, pre_mlp_norm=v_pre_mlp_norm, w_up=v_w_up, w_down=v_w_down,
               post_mlp_norm=v_post_mlp_norm)

    win_g, conv_g = _all_gather([w_in.T.astype(BF), gdn_conv_w])
    wt_al = _to_aligned_rows(win_g.reshape(D_PROJ, D_MODEL))
    convw = _cols_from_pieces(conv_g)
    gathers, after = {}, win_g
    for name, shards in (("w_out", [w_out.astype(BF)]), ("mlp", [w_up.astype(BF), w_down.astype(BF)])):
        zones = _zones_with_own(shards, False, "gather_" + name + "_own", after=after)
        gathers[name] = _exchange_start(shards, zones, False, "gather_" + name + "_start")
        after = gathers[name][3]

    def late_weights(name, after):
        sems, shards, zones, _ = gathers[name]
        got = _exchange_wait(sems, shards, zones, after, "gather_" + name + "_wait")
        if name == "w_out":
            return got[0].reshape(D_MODEL, D_MODEL)
        return got[0], got[1].reshape(D_FF, D_MODEL)

    scatters = {}

    def on_grads(name, g):
        if name == "w_in":
            g = _from_aligned_rows(g).reshape(N_DEV, D_PROJ // N_DEV, D_MODEL)
        srcs = list(g) if name == "mlp" else [g]
        scatters[name] = _exchange_start(srcs, _zones_with_own(srcs, True, "scatter_" + name + "_own"), True,
                                         "scatter_" + name + "_start")
        return scatters[name][3]

    grad_x, small = _local_step(
        x[0], loss_target[0], wt_al, late_weights, on_grads, convw, pre_mix_norm + after[0, 0],
        fox_f_bias, fox_out_norm, gdn_a_log, gdn_dt_bias, gdn_out_norm, post_mix_norm, pre_mlp_norm, post_mlp_norm)
    dwq, dwk, dwv = small.pop("gdn_conv_w")
    packed = [_pack_small(small, jnp.concatenate([dwq, dwk, dwv], axis=1))]
    scatters["small"] = _exchange_start(packed, _zones_with_own(packed, False, "small_own"), False, "small_start")

    grads, delta, new_m, new_v = {}, {}, {}, {}
    after = scatters["small"][3]
    for name, members in (("mlp", ("w_up", "w_down")), ("w_out", ("w_out",)), ("small", ()), ("w_in", ("w_in",))):
        sems, srcs, zones, _ = scatters[name]
        zones = _exchange_wait(sems, srcs, zones, after, "scatter_" + name + "_wait")
        if name == "small":
            total = _unpack_small(_sum_small(zones[0]), (CONV_K, 3 * D_GDN))
            after = total["pre_mix_norm"]
        for n, zone in zip(members, zones):
            if n == "w_in":
                res = _sum_adamw(zone, w[n].T, mom[n].T, var[n].T, "adamw_" + n)
                grads[n], delta[n], new_m[n], new_v[n] = [r.T for r in res]
            else:
                grads[n], delta[n], new_m[n], new_v[n] = _sum_adamw(zone, w[n], mom[n], var[n], "adamw_" + n)
        if members:
            after = [grads[n] for n in members]

    loss = total.pop("loss")[0]
    me = 4 * lax.axis_index("x") + 2 * lax.axis_index("y") + lax.axis_index("c")
    n_conv = gdn_conv_w.shape[1]
    total["gdn_conv_w"] = lax.dynamic_slice_in_dim(total["gdn_conv_w"], me * n_conv, n_conv, axis=1)
    grads.update(total)
    d_s, m_s, v_s = _adamw_small(_pack_small(w, gdn_conv_w), _pack_small(total, total["gdn_conv_w"]),
                                 _pack_small(mom, m_gdn_conv_w), _pack_small(var, v_gdn_conv_w))
    delta.update(_unpack_small(d_s, gdn_conv_w.shape))
    new_m.update(_unpack_small(m_s, gdn_conv_w.shape))
    new_v.update(_unpack_small(v_s, gdn_conv_w.shape))

    return (loss, grad_x[None], *[grads[n] for n in WEIGHT_ORDER], *[delta[n] for n in WEIGHT_ORDER],
            *[new_m[n] for n in WEIGHT_ORDER], *[new_v[n] for n in WEIGHT_ORDER])
```

```python
import jax
import jax.numpy as jnp
from jax import lax
from jax.experimental import pallas as pl
from jax.experimental.pallas import tpu as pltpu

F32 = jnp.float32
BF = jnp.bfloat16

D_MODEL = 1024
N_FOX_HEADS, FOX_HEAD_DIM = 8, 64
N_GDN_HEADS, GDN_HEAD_DIM = 4, 128
D_FOX = N_FOX_HEADS * FOX_HEAD_DIM
D_GDN = N_GDN_HEADS * GDN_HEAD_DIM
CHUNK = 64
CONV_K = 4
D_FF = 4 * D_MODEL
EPS = 1e-6
D_PROJ = 3600
N_DEV = 8

PROJ_W = 3712
COL_FOX, COL_GDN, COL_GZ, COL_SMALL = 0, 1536, 3072, 3584
LANES = 128
SM_FF, SM_GB, SM_GA = 0, 8, 12

ADAM_LR, ADAM_B1, ADAM_B2, ADAM_EPS, ADAM_WD, ADAM_STEP = 0.001, 0.9, 0.999, 1e-08, 0.01, 10

TOKEN_BLOCK = 256
FOX_SCALE = FOX_HEAD_DIM ** -0.5
GDN_QSCALE = GDN_HEAD_DIM ** -0.5
NEG_BIG = -1e30
VMEM_LIMIT = 56 * 1024 * 1024

VMEM_SPEC = pl.BlockSpec(memory_space=pltpu.VMEM)
HIGHEST = lax.Precision.HIGHEST


def _sds(shape, dtype=F32):
    return jax.ShapeDtypeStruct(shape, dtype)


def _params(*sem):
    return pltpu.CompilerParams(dimension_semantics=sem if sem else None, vmem_limit_bytes=VMEM_LIMIT)


def _mm(a, b):
    return jnp.dot(a.astype(BF), b.astype(BF), preferred_element_type=F32)


def _mm_nt(a, b):
    return lax.dot_general(a.astype(BF), b.astype(BF), (((1,), (1,)), ((), ())), preferred_element_type=F32)


def _mm_tn(a, b):
    return lax.dot_general(a.astype(BF), b.astype(BF), (((0,), (0,)), ((), ())), preferred_element_type=F32)


def _mm_exact(a, b):
    return jnp.dot(a, b, precision=HIGHEST, preferred_element_type=F32)


def _mm_tn_exact(a, b):
    return lax.dot_general(a, b, (((0,), (0,)), ((), ())), precision=HIGHEST, preferred_element_type=F32)


def _sigmoid(x):
    return 1.0 / (1.0 + jnp.exp(-x))


def _softplus(x):
    return jnp.maximum(x, 0.0) + jnp.log1p(jnp.exp(-jnp.abs(x)))


def _iota(shape, dim):
    return lax.broadcasted_iota(jnp.int32, shape, dim)


def _shift_down(x, s, row):
    return jnp.where(row >= s, pltpu.roll(x, s, 0), 0.0)


def _shift_up(x, s, row):
    n = x.shape[0]
    return jnp.where(row < n - s, pltpu.roll(x, n - s, 0), 0.0)


def _norm_proj(x, nw, wt_al):
    t = x.shape[0]

    def body(x_ref, nw_ref, w_ref, proj_ref, h_ref):
        xv = x_ref[...]
        r = lax.rsqrt(jnp.mean(xv * xv, axis=-1, keepdims=True) + EPS)
        h = (xv * r * nw_ref[...]).astype(BF)
        h_ref[...] = h
        proj_ref[...] = lax.dot_general(h, w_ref[...], (((1,), (1,)), ((), ())), preferred_element_type=F32)

    tm = TOKEN_BLOCK
    return pl.pallas_call(
        body, name="norm_proj", grid=(t // tm,),
        in_specs=[pl.BlockSpec((tm, D_MODEL), lambda i: (i, 0)), pl.BlockSpec((1, D_MODEL), lambda i: (0, 0)),
                  pl.BlockSpec((PROJ_W, D_MODEL), lambda i: (0, 0))],
        out_specs=[pl.BlockSpec((tm, PROJ_W), lambda i: (i, 0)), pl.BlockSpec((tm, D_MODEL), lambda i: (i, 0))],
        out_shape=[_sds((t, PROJ_W)), _sds((t, D_MODEL), BF)],
        compiler_params=_params("parallel"),
    )(x, nw, wt_al)


def _expand_matrix(first_row, group):
    row = _iota((LANES, 512), 0)
    col = _iota((LANES, 512), 1)
    return (col // group + first_row == row).astype(F32)


def _small_prep(proj, fb, al, dtb):
    t = proj.shape[0]

    def body(sm_ref, fb_ref, al_ref, dtb_ref, ce_ref, cumt_ref, be_ref, ge_ref):
        s = sm_ref[...]
        z = s + fb_ref[...]
        cum = jnp.minimum(z, 0.0) - jnp.log1p(jnp.exp(-jnp.abs(z)))
        row = _iota((t, LANES), 0)
        step = 1
        while step < t:
            cum = cum + _shift_down(cum, step, row)
            step *= 2
        cumt_ref[...] = cum.T
        ce_ref[...] = _mm_exact(cum, _expand_matrix(SM_FF, FOX_HEAD_DIM))
        be_ref[...] = _mm_exact(_sigmoid(s), _expand_matrix(SM_GB, GDN_HEAD_DIM))
        g = -jnp.exp(al_ref[...]) * _softplus(s + dtb_ref[...])
        ge_ref[...] = _mm_exact(g, _expand_matrix(SM_GA, GDN_HEAD_DIM))

    vec = pl.BlockSpec((1, LANES), lambda i: (0, 0))
    return pl.pallas_call(
        body, name="small_prep", grid=(1,),
        in_specs=[pl.BlockSpec((t, LANES), lambda i: (0, COL_SMALL // LANES)), vec, vec, vec],
        out_specs=[pl.BlockSpec((t, 512), lambda i: (0, 0)), pl.BlockSpec((LANES, t), lambda i: (0, 0)),
                   pl.BlockSpec((t, 512), lambda i: (0, 0)), pl.BlockSpec((t, 512), lambda i: (0, 0))],
        out_shape=[_sds((t, 512)), _sds((LANES, t)), _sds((t, 512)), _sds((t, 512))],
        compiler_params=_params("arbitrary"),
    )(proj, fb, al, dtb)


def _fox_scores(qh, kb, ce_ref, cumt_ref, head, hh, i, tq):
    klen = (i + 1) * tq
    s = _mm_nt(qh, kb[:klen]) * FOX_SCALE
    cq = ce_ref[i * tq:(i + 1) * tq, FOX_HEAD_DIM * hh:FOX_HEAD_DIM * hh + 1]
    ck = cumt_ref[pl.ds(head, 1), 0:klen]
    s = s + cq - ck
    qi = _iota((tq, klen), 0) + i * tq
    ki = _iota((tq, klen), 1)
    return jnp.where(ki <= qi, s, NEG_BIG)


def _fox_fwd(proj, ce, cumt, fnw):
    t = proj.shape[0]
    tq = min(TOKEN_BLOCK, t // 2)
    nq = t // tq

    def body(q_ref, k_ref, v_ref, ce_ref, cumt_ref, fnw_ref, o_ref, lse_ref, fn_ref):
        j = pl.program_id(0)
        first = _iota((1, LANES), 1) < FOX_HEAD_DIM
        kb = k_ref[...].astype(BF)
        vb = v_ref[...].astype(BF)
        for i in range(nq):
            rows = slice(i * tq, (i + 1) * tq)
            klen = (i + 1) * tq
            q_i = q_ref[rows, :]
            o_acc = jnp.zeros((tq, LANES), F32)
            lse_acc = jnp.zeros((tq, LANES), F32)
            for hh in range(2):
                mh = first if hh == 0 else jnp.logical_not(first)
                qh = jnp.where(mh, q_i, 0.0).astype(BF)
                s = _fox_scores(qh, kb, ce_ref, cumt_ref, 2 * j + hh, hh, i, tq)
                m = jnp.max(s, axis=-1, keepdims=True)
                p = jnp.exp(s - m)
                l = jnp.sum(p, axis=-1, keepdims=True)
                o = jnp.dot(p.astype(BF), vb[:klen], preferred_element_type=F32) / l
                o_acc = jnp.where(mh, o, o_acc)
                lse_acc = jnp.where(mh, m + jnp.log(l), lse_acc)
            o_ref[rows, :] = o_acc
            lse_ref[rows, :] = lse_acc
            o2 = o_acc * o_acc
            s0 = jnp.sum(jnp.where(first, o2, 0.0), axis=-1, keepdims=True)
            s1 = jnp.sum(jnp.where(first, 0.0, o2), axis=-1, keepdims=True)
            r = lax.rsqrt(jnp.where(first, s0, s1) * (1.0 / FOX_HEAD_DIM) + EPS)
            fn_ref[rows, :] = (o_acc * r * fnw_ref[...]).astype(BF)

    blk = lambda off: pl.BlockSpec((t, LANES), lambda j: (0, off + j))
    return pl.pallas_call(
        body, name="fox_fwd", grid=(N_FOX_HEADS // 2,),
        in_specs=[blk(0), blk(4), blk(8), blk(0), pl.BlockSpec((LANES, t), lambda j: (0, 0)),
                  pl.BlockSpec((1, LANES), lambda j: (0, 0))],
        out_specs=[blk(0), blk(0), blk(0)],
        out_shape=[_sds((t, D_FOX)), _sds((t, D_FOX)), _sds((t, D_FOX), BF)],
        compiler_params=_params("parallel"),
    )(proj, proj, proj, ce, cumt, fnw)


def _fox_bwd(proj, ce, cumt, lse, o, do):
    t = proj.shape[0]
    tq = min(TOKEN_BLOCK, t // 2)
    nq = t // tq

    def body(q_ref, k_ref, v_ref, ce_ref, cumt_ref, lse_ref, o_ref, do_ref,
             dq_ref, dk_ref, dv_ref, dcq_ref, dckt_ref, dk_s, dv_s, dck_s):
        j = pl.program_id(0)
        first = _iota((1, LANES), 1) < FOX_HEAD_DIM
        kf = k_ref[...]
        kb = kf.astype(BF)
        vb = v_ref[...].astype(BF)
        dk_s[...] = jnp.zeros_like(dk_s)
        dv_s[...] = jnp.zeros_like(dv_s)
        dck_s[...] = jnp.zeros_like(dck_s)
        masks = [first, jnp.logical_not(first)]
        kmask = [jnp.where(mh, kf, 0.0).astype(BF) for mh in masks]
        for i in range(nq):
            rows = slice(i * tq, (i + 1) * tq)
            klen = (i + 1) * tq
            q_i = q_ref[rows, :]
            do_i = do_ref[rows, :]
            o_i = o_ref[rows, :]
            lse_i = lse_ref[rows, :]
            dq_acc = jnp.zeros((tq, LANES), F32)
            dcq_acc = jnp.zeros((tq, LANES), F32)
            for hh in range(2):
                mh = masks[hh]
                qh = jnp.where(mh, q_i, 0.0).astype(BF)
                doh = jnp.where(mh, do_i, 0.0)
                dohb = doh.astype(BF)
                delta = jnp.sum(doh * o_i, axis=-1, keepdims=True)
                s = _fox_scores(qh, kb, ce_ref, cumt_ref, 2 * j + hh, hh, i, tq)
                p = jnp.exp(s - lse_i[:, FOX_HEAD_DIM * hh:FOX_HEAD_DIM * hh + 1])
                dp = _mm_nt(dohb, vb[:klen])
                ds = p * (dp - delta)
                dsb = ds.astype(BF)
                dq_acc = dq_acc + jnp.dot(dsb, kmask[hh][:klen], preferred_element_type=F32) * FOX_SCALE
                dk_s[0:klen, :] += _mm_tn(dsb, qh) * FOX_SCALE
                dv_s[0:klen, :] += _mm_tn(p, dohb)
                dcq_acc = jnp.where(mh, jnp.sum(ds, axis=-1, keepdims=True), dcq_acc)
                dck_s[hh:hh + 1, 0:klen] += jnp.sum(ds, axis=0, keepdims=True)
            dq_ref[rows, :] = dq_acc
            dcq_ref[rows, :] = dcq_acc
        dk_ref[...] = dk_s[...]
        dv_ref[...] = dv_s[...]
        dckt_ref[...] = jnp.zeros_like(dckt_ref)
        dckt_ref[0:8, :] = dck_s[...]

    blk = lambda off: pl.BlockSpec((t, LANES), lambda j: (0, off + j))
    return pl.pallas_call(
        body, name="fox_bwd", grid=(N_FOX_HEADS // 2,),
        in_specs=[blk(0), blk(4), blk(8), blk(0), pl.BlockSpec((LANES, t), lambda j: (0, 0)), blk(0), blk(0), blk(0)],
        out_specs=[blk(0), blk(0), blk(0), blk(0), pl.BlockSpec((32, t), lambda j: (j, 0))],
        out_shape=[_sds((t, D_FOX))] * 4 + [_sds((LANES, t))],
        scratch_shapes=[pltpu.VMEM((t, LANES), F32), pltpu.VMEM((t, LANES), F32), pltpu.VMEM((8, t), F32)],
        compiler_params=_params("parallel"),
    )(proj, proj, proj, ce, cumt, lse, o, do)


def _conv(x, w, row):
    return (w[3:4, :] * x + w[2:3, :] * _shift_down(x, 1, row) + w[1:2, :] * _shift_down(x, 2, row)
            + w[0:1, :] * _shift_down(x, 3, row))


def _chunk_decay(gc_c):
    gi = gc_c[:, 0:CHUNK]
    gj = gc_c.T[0:CHUNK, :]
    ri = _iota((CHUNK, CHUNK), 0)
    cj = _iota((CHUNK, CHUNK), 1)
    return jnp.where(ri >= cj, jnp.exp(jnp.minimum(gi - gj, 0.0)), 0.0), ri > cj


def _gdn_specs(t):
    col = lambda off: pl.BlockSpec((t, LANES), lambda h: (0, off + h))
    cw = lambda off: pl.BlockSpec((CONV_K, LANES), lambda h: (0, off + h))
    mat = pl.BlockSpec((1, t // CHUNK, CHUNK, CHUNK), lambda h: (h, 0, 0, 0))
    return col, cw, mat


def _gdn_prep(proj, convw, be, ge):
    t = proj.shape[0]
    nch = t // CHUNK

    def body(xq_ref, xk_ref, xv_ref, wq_ref, wk_ref, wv_ref, be_ref, ge_ref,
             qn_ref, kn_ref, cv_ref, gc_ref, m_ref, a_ref):
        row = _iota((t, LANES), 0)

        def act(x_ref, w_ref):
            y = _conv(x_ref[...], w_ref[...], row)
            return y * _sigmoid(y)

        cq = act(xq_ref, wq_ref)
        ck = act(xk_ref, wk_ref)
        cv_ref[...] = act(xv_ref, wv_ref)
        qn_ref[...] = cq * lax.rsqrt(jnp.sum(cq * cq, axis=-1, keepdims=True) + EPS) * GDN_QSCALE
        kn_ref[...] = ck * lax.rsqrt(jnp.sum(ck * ck, axis=-1, keepdims=True) + EPS)
        gc = ge_ref[...]
        pos = row % CHUNK
        step = 1
        while step < CHUNK:
            gc = gc + jnp.where(pos >= step, pltpu.roll(gc, step, 0), 0.0)
            step *= 2
        gc_ref[...] = gc

        def chunk(n, carry):
            sl = pl.ds(pl.multiple_of(n * CHUNK, CHUNK), CHUNK)
            k_c = kn_ref[sl, :]
            decay, strict = _chunk_decay(gc_ref[sl, :])
            m_ref[0, n] = jnp.where(strict, _mm_nt(k_c * be_ref[sl, :], k_c) * decay, 0.0)
            a_ref[0, n] = _mm_nt(qn_ref[sl, :], k_c) * decay
            return carry

        lax.fori_loop(0, nch, chunk, 0)

    col, cw, mat = _gdn_specs(t)
    return pl.pallas_call(
        body, name="gdn_prep", grid=(N_GDN_HEADS,),
        in_specs=[col(12), col(16), col(20), cw(0), cw(4), cw(8), col(0), col(0)],
        out_specs=[col(0), col(0), col(0), col(0), mat, mat],
        out_shape=[_sds((t, D_GDN))] * 4 + [_sds((N_GDN_HEADS, nch, CHUNK, CHUNK))] * 2,
        compiler_params=_params("parallel"),
    )(proj, proj, proj, convw, convw, convw, be, ge)


def _tri_inverse(m2):
    n_prob = m2.shape[0]
    assert n_prob == LANES
    nb = CHUNK * CHUNK // LANES

    def body(m_ref, t_ref, ms, ts):
        for b in range(nb):
            ms[b * LANES:(b + 1) * LANES, :] = m_ref[:, b * LANES:(b + 1) * LANES].T
        cidx = _iota((CHUNK, LANES), 0)

        def outer(i, carry):
            def inner(jj, acc):
                mrow = ms[pl.ds(i * CHUNK + jj, 1), :]
                return acc - mrow * ts[pl.ds(pl.multiple_of(jj * CHUNK, CHUNK), CHUNK), :]

            acc = lax.fori_loop(0, i, inner, jnp.where(cidx == i, 1.0, 0.0).astype(F32))
            ts[pl.ds(pl.multiple_of(i * CHUNK, CHUNK), CHUNK), :] = acc
            return carry

        lax.fori_loop(0, CHUNK, outer, 0)
        for b in range(nb):
            t_ref[:, b * LANES:(b + 1) * LANES] = ts[b * LANES:(b + 1) * LANES, :].T

    return pl.pallas_call(
        body, name="tri_inverse", in_specs=[VMEM_SPEC], out_specs=VMEM_SPEC,
        out_shape=_sds((LANES, CHUNK * CHUNK)),
        scratch_shapes=[pltpu.VMEM((CHUNK * CHUNK, LANES), F32), pltpu.VMEM((CHUNK * CHUNK, LANES), F32)],
        compiler_params=_params(),
    )(m2)


def _gdn_chunk_terms(q, k, v, b, gcc):
    eg = jnp.exp(gcc)
    last = gcc[CHUNK - 1:CHUNK, :]
    egl = jnp.exp(last - gcc)
    gl = jnp.exp(last)
    kb = k * b
    return eg, egl, gl, kb, v * b, kb * eg, q * eg, k * egl


GDN_BLOCK_CHUNKS = 4


def _gdn_block_specs(t, reverse):
    cb = GDN_BLOCK_CHUNKS
    nb = t // (cb * CHUNK)
    idx = (lambda i: nb - 1 - i) if reverse else (lambda i: i)
    tok = pl.BlockSpec((cb * CHUNK, D_GDN), lambda i: (idx(i), 0))
    mat = pl.BlockSpec((N_GDN_HEADS, cb, CHUNK, CHUNK), lambda i: (0, idx(i), 0, 0))
    state = pl.BlockSpec((N_GDN_HEADS, cb, GDN_HEAD_DIM, GDN_HEAD_DIM), lambda i: (0, idx(i), 0, 0))
    return nb, tok, mat, state


def _gdn_scan(qn, kn, cv, be, gc, tinv, amat):
    t = qn.shape[0]
    nch = t // CHUNK

    def body(q_ref, k_ref, v_ref, b_ref, gc_ref, t_ref, a_ref, o_ref, sall_ref, vn_ref, s_scr):
        @pl.when(pl.program_id(0) == 0)
        def _():
            s_scr[...] = jnp.zeros_like(s_scr)

        heads = range(N_GDN_HEADS)
        cols = [slice(hd * LANES, (hd + 1) * LANES) for hd in heads]
        s = [s_scr[hd] for hd in heads]
        for cc in range(GDN_BLOCK_CHUNKS):
            rs = slice(cc * CHUNK, (cc + 1) * CHUNK)
            terms = [_gdn_chunk_terms(q_ref[rs, cs], k_ref[rs, cs], v_ref[rs, cs], b_ref[rs, cs], gc_ref[rs, cs])
                     for cs in cols]
            for hd in heads:
                sall_ref[hd, cc] = s[hd]
            uw = [_mm(t_ref[hd, cc], jnp.concatenate([terms[hd][4], terms[hd][5]], axis=1)) for hd in heads]
            ws_qs = [_mm(jnp.concatenate([uw[hd][:, LANES:], terms[hd][6]], axis=0), s[hd]) for hd in heads]
            vn = [uw[hd][:, :LANES] - ws_qs[hd][:CHUNK] for hd in heads]
            a_vn = [_mm(a_ref[hd, cc], vn[hd]) for hd in heads]
            kd_vn = [_mm_tn(terms[hd][7], vn[hd]) for hd in heads]
            for hd in heads:
                vn_ref[rs, cols[hd]] = vn[hd]
                o_ref[rs, cols[hd]] = ws_qs[hd][CHUNK:] + a_vn[hd]
                s[hd] = s[hd] * terms[hd][2] + kd_vn[hd]
        for hd in heads:
            s_scr[hd] = s[hd]

    nb, tok, mat, state = _gdn_block_specs(t, False)
    return pl.pallas_call(
        body, name="gdn_scan", grid=(nb,),
        in_specs=[tok] * 5 + [mat, mat], out_specs=[tok, state, tok],
        out_shape=[_sds((t, D_GDN)), _sds((N_GDN_HEADS, nch, GDN_HEAD_DIM, GDN_HEAD_DIM)), _sds((t, D_GDN))],
        scratch_shapes=[pltpu.VMEM((N_GDN_HEADS, GDN_HEAD_DIM, GDN_HEAD_DIM), F32)],
        compiler_params=_params("arbitrary"),
    )(qn, kn, cv, be, gc, tinv, amat)


def _gdn_bwd(qn, kn, cv, be, gc, tinv, amat, s_all, vn_all, do):
    t = qn.shape[0]

    def body(q_ref, k_ref, v_ref, b_ref, gc_ref, t_ref, a_ref, sall_ref, vn_ref, do_ref,
             dq_ref, dk_ref, dv_ref, db_ref, dg_ref, ds_scr):
        @pl.when(pl.program_id(0) == 0)
        def _():
            ds_scr[...] = jnp.zeros_like(ds_scr)

        lastrow = _iota((CHUNK, LANES), 0) == CHUNK - 1
        heads = range(N_GDN_HEADS)
        cols = [slice(hd * LANES, (hd + 1) * LANES) for hd in heads]
        each = lambda fn: [fn(hd) for hd in heads]
        rows_cat = lambda x, y: jnp.concatenate([x, y], axis=0)
        lane_cat = lambda x, y: jnp.concatenate([x, y], axis=1)
        dsp = each(lambda hd: ds_scr[hd])
        for cc in reversed(range(GDN_BLOCK_CHUNKS)):
            rs = slice(cc * CHUNK, (cc + 1) * CHUNK)
            q = each(lambda hd: q_ref[rs, cols[hd]])
            k = each(lambda hd: k_ref[rs, cols[hd]])
            v = each(lambda hd: v_ref[rs, cols[hd]])
            b = each(lambda hd: b_ref[rs, cols[hd]])
            gcc = each(lambda hd: gc_ref[rs, cols[hd]])
            do_c = each(lambda hd: do_ref[rs, cols[hd]])
            vn = each(lambda hd: vn_ref[rs, cols[hd]])
            tn = each(lambda hd: t_ref[hd, cc])
            st = each(lambda hd: sall_ref[hd, cc])
            terms = each(lambda hd: _gdn_chunk_terms(q[hd], k[hd], v[hd], b[hd], gcc[hd]))
            eg, egl, gl, kb, vb, kbg, qd, kd = [[terms[hd][i] for hd in heads] for i in range(8)]
            w = each(lambda hd: _mm(tn[hd], kbg[hd]))
            a_do = each(lambda hd: _mm_tn(a_ref[hd, cc], do_c[hd]))
            kd_ds = each(lambda hd: _mm(kd[hd], dsp[hd]))
            da = each(lambda hd: _mm_nt(do_c[hd], vn[hd]))
            dkd = each(lambda hd: _mm_nt(vn[hd], dsp[hd]))
            by_k = each(lambda hd: _mm_nt(rows_cat(kb[hd], q[hd]), k[hd]))
            dgl = each(lambda hd: jnp.sum(jnp.sum(dsp[hd] * st[hd], axis=-1, keepdims=True), axis=0, keepdims=True))
            dvn = each(lambda hd: a_do[hd] + kd_ds[hd])
            do_dvn = each(lambda hd: rows_cat(do_c[hd], dvn[hd]))
            by_s = each(lambda hd: _mm_nt(do_dvn[hd], st[hd]))
            dqd = each(lambda hd: by_s[hd][:CHUNK])
            dvn_dw = each(lambda hd: lane_cat(dvn[hd], -by_s[hd][CHUNK:]))
            dsp = each(lambda hd: _mm_tn(rows_cat(qd[hd], -w[hd]), do_dvn[hd]) + gl[hd] * dsp[hd])
            dt = each(lambda hd: _mm_nt(dvn_dw[hd], lane_cat(vb[hd], kbg[hd])))
            by_t = each(lambda hd: _mm_tn(tn[hd], dvn_dw[hd]))
            tt_dt = each(lambda hd: _mm_tn(tn[hd], dt[hd]))
            dm_raw = each(lambda hd: _mm_nt(tt_dt[hd], tn[hd]))
            masks = each(lambda hd: _chunk_decay(gcc[hd]))
            dkk = each(lambda hd: jnp.where(masks[hd][1], -dm_raw[hd], 0.0) * masks[hd][0])
            dqk = each(lambda hd: da[hd] * masks[hd][0])
            dqk_dkk = each(lambda hd: rows_cat(dqk[hd], dkk[hd]))
            on_k = each(lambda hd: _mm(dqk_dkk[hd], k[hd]))
            dk_mm = each(lambda hd: _mm_tn(dqk_dkk[hd], rows_cat(q[hd], kb[hd])))
            for hd in heads:
                cs = cols[hd]
                dvb, dkbg = by_t[hd][:, :LANES], by_t[hd][:, LANES:]
                gmat = dkk[hd] * by_k[hd][:CHUNK] + dqk[hd] * by_k[hd][CHUNK:]
                dq_ref[rs, cs] = dqd[hd] * eg[hd] + on_k[hd][:CHUNK]
                dkb = on_k[hd][CHUNK:] + dkbg * eg[hd]
                dk_ref[rs, cs] = dkd[hd] * egl[hd] + dk_mm[hd] + dkb * b[hd]
                db = jnp.sum(dkb * k[hd], axis=-1, keepdims=True) + jnp.sum(dvb * v[hd], axis=-1, keepdims=True)
                db_ref[rs, cs] = jnp.broadcast_to(db, (CHUNK, LANES))
                dv_ref[rs, cs] = dvb * b[hd]
                dkd_kd = jnp.sum(dkd[hd] * kd[hd], axis=-1, keepdims=True)
                col_sums = jnp.sum(lane_cat(gmat, jnp.zeros_like(gmat)).T, axis=-1, keepdims=True)
                dgc = (jnp.sum(gmat, axis=-1, keepdims=True) - col_sums[:CHUNK]
                       + jnp.sum(dqd[hd] * qd[hd], axis=-1, keepdims=True)
                       + jnp.sum(dkbg * kbg[hd], axis=-1, keepdims=True) - dkd_kd)
                extra = jnp.sum(dkd_kd, axis=0, keepdims=True) + dgl[hd] * gl[hd]
                dg_ref[rs, cs] = dgc + jnp.where(lastrow, extra, 0.0)
        for hd in heads:
            ds_scr[hd] = dsp[hd]
        dg = dg_ref[...]
        row = _iota(dg.shape, 0)
        pos = row % CHUNK
        step = 1
        while step < CHUNK:
            dg = dg + jnp.where(pos < CHUNK - step, pltpu.roll(dg, dg.shape[0] - step, 0), 0.0)
            step *= 2
        dg_ref[...] = dg

    nb, tok, mat, state = _gdn_block_specs(t, True)
    return pl.pallas_call(
        body, name="gdn_bwd", grid=(nb,),
        in_specs=[tok] * 5 + [mat, mat, state, tok, tok], out_specs=[tok] * 5, out_shape=[_sds((t, D_GDN))] * 5,
        scratch_shapes=[pltpu.VMEM((N_GDN_HEADS, GDN_HEAD_DIM, GDN_HEAD_DIM), F32)],
        compiler_params=_params("arbitrary"),
    )(qn, kn, cv, be, gc, tinv, amat, s_all, vn_all, do)


def _gdn_bwd_conv(proj, convw, dqn, dkn, dcv):
    t = proj.shape[0]

    def body(xq_ref, xk_ref, xv_ref, wq_ref, wk_ref, wv_ref, dq_ref, dk_ref, dv_ref,
             dxq_ref, dxk_ref, dxv_ref, dwq_ref, dwk_ref, dwv_ref):
        row = _iota((t, LANES), 0)

        def one(x_ref, w_ref, d_ref, dx_ref, dw_ref, scale):
            x = x_ref[...]
            w = w_ref[...]
            y = _conv(x, w, row)
            sg = _sigmoid(y)
            dc = d_ref[...]
            if scale is not None:
                c = y * sg
                r = lax.rsqrt(jnp.sum(c * c, axis=-1, keepdims=True) + EPS)
                ch = c * r
                dc = scale * r * (dc - ch * jnp.sum(dc * ch, axis=-1, keepdims=True))
            dy = dc * sg * (1.0 + y * (1.0 - sg))
            dx_ref[...] = (w[3:4, :] * dy + w[2:3, :] * _shift_up(dy, 1, row) + w[1:2, :] * _shift_up(dy, 2, row)
                           + w[0:1, :] * _shift_up(dy, 3, row))
            for jj in range(CONV_K):
                xs = x if jj == CONV_K - 1 else _shift_down(x, CONV_K - 1 - jj, row)
                dw_ref[jj:jj + 1, :] = jnp.sum(dy * xs, axis=0, keepdims=True)

        one(xq_ref, wq_ref, dq_ref, dxq_ref, dwq_ref, GDN_QSCALE)
        one(xk_ref, wk_ref, dk_ref, dxk_ref, dwk_ref, 1.0)
        one(xv_ref, wv_ref, dv_ref, dxv_ref, dwv_ref, None)

    col, cw, _ = _gdn_specs(t)
    return pl.pallas_call(
        body, name="gdn_bwd_conv", grid=(N_GDN_HEADS,),
        in_specs=[col(12), col(16), col(20), cw(0), cw(4), cw(8), col(0), col(0), col(0)],
        out_specs=[col(0), col(0), col(0), cw(0), cw(0), cw(0)],
        out_shape=[_sds((t, D_GDN))] * 3 + [_sds((CONV_K, D_GDN))] * 3,
        compiler_params=_params("parallel"),
    )(proj, proj, proj, convw, convw, convw, dqn, dkn, dcv)


def _mix_out(fox_n, gdn_o, proj, gnw, w_out, x, pmw, plw):
    t = x.shape[0]
    tm = TOKEN_BLOCK

    def body(fn_ref, go_ref, gz_ref, gnw_ref, w_ref, x_ref, pmw_ref, plw_ref, x1_ref, h2_ref, mixed_ref, omix_ref,
             h2t_ref):
        omix_ref[:, 0:D_FOX] = fn_ref[...]
        for hd in range(N_GDN_HEADS):
            cs = slice(hd * LANES, (hd + 1) * LANES)
            go = go_ref[:, cs]
            r = lax.rsqrt(jnp.mean(go * go, axis=-1, keepdims=True) + EPS)
            gz = gz_ref[:, cs]
            omix_ref[:, D_FOX + hd * LANES:D_FOX + (hd + 1) * LANES] = (
                go * r * gnw_ref[...] * (gz * _sigmoid(gz))).astype(BF)
        mixed = jnp.dot(omix_ref[...], w_ref[...], preferred_element_type=F32)
        mixed_ref[...] = mixed
        r2 = lax.rsqrt(jnp.mean(mixed * mixed, axis=-1, keepdims=True) + EPS)
        x1 = x_ref[...] + mixed * r2 * pmw_ref[...]
        x1_ref[...] = x1
        r3 = lax.rsqrt(jnp.mean(x1 * x1, axis=-1, keepdims=True) + EPS)
        h2 = x1 * r3 * plw_ref[...]
        h2_ref[...] = h2.astype(BF)
        h2t_ref[...] = h2.T.astype(BF)

    tok = lambda w: pl.BlockSpec((tm, w), lambda i: (i, 0))
    vec = lambda w: pl.BlockSpec((1, w), lambda i: (0, 0))
    return pl.pallas_call(
        body, name="mix_out", grid=(t // tm,),
        in_specs=[tok(D_FOX), tok(D_GDN), pl.BlockSpec((tm, D_GDN), lambda i: (i, COL_GZ // D_GDN)), vec(LANES),
                  pl.BlockSpec((D_MODEL, D_MODEL), lambda i: (0, 0)), tok(D_MODEL), vec(D_MODEL), vec(D_MODEL)],
        out_specs=[tok(D_MODEL)] * 4 + [pl.BlockSpec((D_MODEL, tm), lambda i: (0, i))],
        out_shape=[_sds((t, D_MODEL)), _sds((t, D_MODEL), BF), _sds((t, D_MODEL)), _sds((t, D_MODEL), BF),
                   _sds((D_MODEL, t), BF)],
        compiler_params=_params("parallel"),
    )(fox_n, gdn_o, proj, gnw, w_out, x, pmw, plw)


def _out_bwd(dmixed, w_out, o_fox, gdn_o, proj, fnw, gnw):
    t = dmixed.shape[0]
    tm = TOKEN_BLOCK

    def body(dm_ref, w_ref, of_ref, go_ref, gz_ref, fnw_ref, gnw_ref, dof_ref, dgo_ref, dgz_ref, dfw_ref, dgw_ref):
        i = pl.program_id(0)

        @pl.when(i == 0)
        def _():
            dfw_ref[...] = jnp.zeros_like(dfw_ref)
            dgw_ref[...] = jnp.zeros_like(dgw_ref)

        domix = _mm_nt(dm_ref[...], w_ref[...])
        first = _iota((1, LANES), 1) < FOX_HEAD_DIM
        dfw = jnp.zeros((1, LANES), F32)
        dgw = jnp.zeros((1, LANES), F32)
        for pr in range(N_FOX_HEADS // 2):
            cs = slice(pr * LANES, (pr + 1) * LANES)
            o = of_ref[:, cs]
            dfn = domix[:, cs]
            o2 = o * o
            s0 = jnp.sum(jnp.where(first, o2, 0.0), axis=-1, keepdims=True)
            s1 = jnp.sum(jnp.where(first, 0.0, o2), axis=-1, keepdims=True)
            r = lax.rsqrt(jnp.where(first, s0, s1) * (1.0 / FOX_HEAD_DIM) + EPS)
            oh = o * r
            dfw = dfw + jnp.sum(dfn * oh, axis=0, keepdims=True)
            doh = dfn * fnw_ref[...]
            pr_ = doh * oh
            m0 = jnp.sum(jnp.where(first, pr_, 0.0), axis=-1, keepdims=True)
            m1 = jnp.sum(jnp.where(first, 0.0, pr_), axis=-1, keepdims=True)
            dof_ref[:, cs] = r * (doh - oh * jnp.where(first, m0, m1) * (1.0 / FOX_HEAD_DIM))
        for hd in range(N_GDN_HEADS):
            cs = slice(hd * LANES, (hd + 1) * LANES)
            go = go_ref[:, cs]
            gz = gz_ref[:, cs]
            dgated = domix[:, D_FOX + hd * LANES:D_FOX + (hd + 1) * LANES]
            r = lax.rsqrt(jnp.mean(go * go, axis=-1, keepdims=True) + EPS)
            goh = go * r
            sg = _sigmoid(gz)
            sz = gz * sg
            gn = goh * gnw_ref[...]
            dgn = dgated * sz
            dgz_ref[:, cs] = dgated * gn * sg * (1.0 + gz * (1.0 - sg))
            dgw = dgw + jnp.sum(dgn * goh, axis=0, keepdims=True)
            dgh = dgn * gnw_ref[...]
            dgo_ref[:, cs] = r * (dgh - goh * jnp.mean(dgh * goh, axis=-1, keepdims=True))
        dfw_ref[...] += dfw + pltpu.roll(dfw, FOX_HEAD_DIM, 1)
        dgw_ref[...] += dgw

    tok = lambda w: pl.BlockSpec((tm, w), lambda i: (i, 0))
    vec = lambda w: pl.BlockSpec((1, w), lambda i: (0, 0))
    return pl.pallas_call(
        body, name="out_bwd", grid=(t // tm,),
        in_specs=[tok(D_MODEL), pl.BlockSpec((D_MODEL, D_MODEL), lambda i: (0, 0)), tok(D_FOX), tok(D_GDN),
                  pl.BlockSpec((tm, D_GDN), lambda i: (i, COL_GZ // D_GDN)), vec(LANES), vec(LANES)],
        out_specs=[tok(D_FOX), tok(D_GDN), tok(D_GDN), vec(LANES), vec(LANES)],
        out_shape=[_sds((t, D_FOX)), _sds((t, D_GDN)), _sds((t, D_GDN)), _sds((1, LANES)), _sds((1, LANES))],
        compiler_params=_params("arbitrary"),
    )(dmixed, w_out, o_fox, gdn_o, proj, fnw, gnw)


def _mlp_up(h2, w_up):
    t = h2.shape[0]
    tm = TOKEN_BLOCK
    pc = D_FF // N_DEV

    def body(h_ref, w_ref, up_ref):
        h = h_ref[...]
        for p in range(N_DEV):
            up_ref[:, p * pc:(p + 1) * pc] = jnp.dot(h, w_ref[p], preferred_element_type=F32)

    return pl.pallas_call(
        body, name="mlp_up", grid=(t // tm,),
        in_specs=[pl.BlockSpec((tm, D_MODEL), lambda i: (i, 0)),
                  pl.BlockSpec((N_DEV, D_MODEL, pc), lambda i: (0, 0, 0))],
        out_specs=pl.BlockSpec((tm, D_FF), lambda i: (i, 0)), out_shape=_sds((t, D_FF)),
        compiler_params=_params("parallel"),
    )(h2, w_up)


def _mlp_down_loss(up, w_down, x1, pw, target):
    t = up.shape[0]
    tm = TOKEN_BLOCK

    def body(up_ref, w_ref, x1_ref, pw_ref, tg_ref, dy_ref, dx2_ref, loss_ref, dpw_ref):
        i = pl.program_id(0)

        @pl.when(i == 0)
        def _():
            loss_ref[...] = jnp.zeros_like(loss_ref)
            dpw_ref[...] = jnp.zeros_like(dpw_ref)

        u = jnp.maximum(up_ref[...], 0.0)
        y = jnp.dot((u * u).astype(BF), w_ref[...], preferred_element_type=F32)
        r = lax.rsqrt(jnp.mean(y * y, axis=-1, keepdims=True) + EPS)
        yh = y * r
        pw = pw_ref[...]
        err = x1_ref[...] + yh * pw - tg_ref[...]
        part = jnp.sum(jnp.sum(err * err, axis=-1, keepdims=True), axis=0, keepdims=True) * (0.5 / D_MODEL)
        loss_ref[...] += jnp.broadcast_to(part, loss_ref.shape)
        dx2 = err * (1.0 / D_MODEL)
        dx2_ref[...] = dx2
        dpw_ref[...] += jnp.sum(dx2 * yh, axis=0, keepdims=True)
        dyh = dx2 * pw
        dy_ref[...] = (r * (dyh - yh * jnp.mean(dyh * yh, axis=-1, keepdims=True))).astype(BF)

    tok = lambda w: pl.BlockSpec((tm, w), lambda i: (i, 0))
    vec = lambda w: pl.BlockSpec((1, w), lambda i: (0, 0))
    return pl.pallas_call(
        body, name="mlp_down_loss", grid=(t // tm,),
        in_specs=[tok(D_FF), pl.BlockSpec((D_FF, D_MODEL), lambda i: (0, 0)), tok(D_MODEL), vec(D_MODEL), tok(D_MODEL)],
        out_specs=[tok(D_MODEL), tok(D_MODEL), vec(LANES), vec(D_MODEL)],
        out_shape=[_sds((t, D_MODEL), BF), _sds((t, D_MODEL)), _sds((1, LANES)), _sds((1, D_MODEL))],
        compiler_params=_params("arbitrary"),
    )(up, w_down, x1, pw, target)


def _mlp_bwd_act(dy, w_down, up):
    t = dy.shape[0]
    tm = TOKEN_BLOCK

    def body(dy_ref, w_ref, up_ref, dup_ref):
        da = lax.dot_general(dy_ref[...], w_ref[...], (((1,), (1,)), ((), ())), preferred_element_type=F32)
        dup_ref[...] = (da * (2.0 * jnp.maximum(up_ref[...], 0.0))).astype(BF)

    return pl.pallas_call(
        body, name="mlp_bwd_act", grid=(t // tm,),
        in_specs=[pl.BlockSpec((tm, D_MODEL), lambda i: (i, 0)), pl.BlockSpec((D_FF, D_MODEL), lambda i: (0, 0)),
                  pl.BlockSpec((tm, D_FF), lambda i: (i, 0))],
        out_specs=pl.BlockSpec((tm, D_FF), lambda i: (i, 0)), out_shape=_sds((t, D_FF), BF),
        compiler_params=_params("parallel"),
    )(dy, w_down, up)


def _mlp_bwd_in(dup, w_up, x1, plw, dx2, mixed, pmw):
    t = dup.shape[0]
    tm = TOKEN_BLOCK

    def body(dup_ref, w_ref, x1_ref, plw_ref, dx2_ref, mx_ref, pmw_ref, dx1_ref, dmixed_ref, dplw_ref, dpmw_ref):
        i = pl.program_id(0)

        @pl.when(i == 0)
        def _():
            dplw_ref[...] = jnp.zeros_like(dplw_ref)
            dpmw_ref[...] = jnp.zeros_like(dpmw_ref)

        pc = D_FF // N_DEV
        dh = _mm_nt(dup_ref[:, 0:pc], w_ref[0])
        for p in range(1, N_DEV):
            dh = dh + _mm_nt(dup_ref[:, p * pc:(p + 1) * pc], w_ref[p])
        x1 = x1_ref[...]
        r = lax.rsqrt(jnp.mean(x1 * x1, axis=-1, keepdims=True) + EPS)
        xh = x1 * r
        dplw_ref[...] += jnp.sum(dh * xh, axis=0, keepdims=True)
        dxh = dh * plw_ref[...]
        dx1 = dx2_ref[...] + r * (dxh - xh * jnp.mean(dxh * xh, axis=-1, keepdims=True))
        dx1_ref[...] = dx1
        mx = mx_ref[...]
        r2 = lax.rsqrt(jnp.mean(mx * mx, axis=-1, keepdims=True) + EPS)
        mh = mx * r2
        dpmw_ref[...] += jnp.sum(dx1 * mh, axis=0, keepdims=True)
        dmh = dx1 * pmw_ref[...]
        dmixed_ref[...] = (r2 * (dmh - mh * jnp.mean(dmh * mh, axis=-1, keepdims=True))).astype(BF)

    tok = lambda w: pl.BlockSpec((tm, w), lambda i: (i, 0))
    vec = lambda w: pl.BlockSpec((1, w), lambda i: (0, 0))
    return pl.pallas_call(
        body, name="mlp_bwd_in", grid=(t // tm,),
        in_specs=[tok(D_FF), pl.BlockSpec((N_DEV, D_MODEL, D_FF // N_DEV), lambda i: (0, 0, 0)), tok(D_MODEL),
                  vec(D_MODEL), tok(D_MODEL), tok(D_MODEL), vec(D_MODEL)],
        out_specs=[tok(D_MODEL), tok(D_MODEL), vec(D_MODEL), vec(D_MODEL)],
        out_shape=[_sds((t, D_MODEL)), _sds((t, D_MODEL), BF), _sds((1, D_MODEL)), _sds((1, D_MODEL))],
        compiler_params=_params("arbitrary"),
    )(dup, w_up, x1, plw, dx2, mixed, pmw)


def _wgrad(a, b, a_cols, split=1, a_fn=None, a_block0=0, name="wgrad"):
    t, b_cols = b.shape
    n_a = (a.shape[1] - a_block0 * a_cols) // a_cols if a_block0 else a.shape[1] // a_cols

    def body(a_ref, b_ref, o_ref):
        av = a_ref[...]
        if a_fn is not None:
            av = a_fn(av)
        o_ref[...] = _mm_tn(av, b_ref[...]).astype(BF).reshape(o_ref.shape)

    return pl.pallas_call(
        body, name=name, grid=(n_a,),
        in_specs=[pl.BlockSpec((t, a_cols), lambda i: (0, i + a_block0)), pl.BlockSpec((t, b_cols), lambda i: (0, 0))],
        out_specs=pl.BlockSpec((split, a_cols // split, b_cols), lambda i: (i, 0, 0)),
        out_shape=_sds((n_a * split, a_cols // split, b_cols), BF),
        compiler_params=_params("parallel"),
    )(a, b)


def _wgrad_pre_t(at, b, b_cols, name):
    rows, t = at.shape
    n_b = b.shape[1] // b_cols

    def body(a_ref, b_ref, o_ref):
        o_ref[0] = jnp.dot(a_ref[...], b_ref[...], preferred_element_type=F32).astype(BF)

    return pl.pallas_call(
        body, name=name, grid=(n_b,),
        in_specs=[pl.BlockSpec((rows, t), lambda j: (0, 0)), pl.BlockSpec((t, b_cols), lambda j: (0, j))],
        out_specs=pl.BlockSpec((1, rows, b_cols), lambda j: (j, 0, 0)), out_shape=_sds((n_b, rows, b_cols), BF),
        compiler_params=_params("parallel"),
    )(at, b)


def _select_matrix(rows, fn):
    r = _iota((rows, LANES), 0)
    c = _iota((rows, LANES), 1)
    return (r == fn(c)).astype(F32)


def _small_bwd(proj, fb, al, dtb, dcq, dckt, dbe, dge):
    t = proj.shape[0]

    def body(sm_ref, fb_ref, al_ref, dtb_ref, dcq_ref, dckt_ref, dbe_ref, dge_ref, dsm_ref, dvec_ref):
        s = sm_ref[...]
        lane = _iota((1, LANES), 1)
        sel_f = _select_matrix(512, lambda c: jnp.where(c < 8, FOX_HEAD_DIM * c, -1))
        sel_k = _select_matrix(LANES, lambda c: jnp.where(c < 8, 32 * (c // 2) + c % 2, -1))
        dcum = _mm_exact(dcq_ref[...], sel_f) - _mm_exact(dckt_ref[...].T, sel_k)
        row = _iota((t, LANES), 0)
        step = 1
        while step < t:
            dcum = dcum + _shift_up(dcum, step, row)
            step *= 2
        dff = dcum * _sigmoid(-(s + fb_ref[...]))
        sel_b = _select_matrix(512, lambda c: jnp.where((c >= SM_GB) & (c < SM_GA), LANES * (c - SM_GB), -1))
        sel_g = _select_matrix(512, lambda c: jnp.where((c >= SM_GA) & (c < SM_GA + 4), LANES * (c - SM_GA), -1))
        beta = _sigmoid(s)
        dgb = _mm_exact(dbe_ref[...], sel_b) * beta * (1.0 - beta)
        dg = _mm_exact(dge_ref[...], sel_g)
        za = s + dtb_ref[...]
        nea = -jnp.exp(al_ref[...])
        dga = dg * nea * _sigmoid(za)
        is_f = lane < SM_GB
        is_b = (lane >= SM_GB) & (lane < SM_GA)
        is_a = (lane >= SM_GA) & (lane < SM_GA + 4)
        dsm_ref[...] = jnp.where(is_f, dff, jnp.where(is_b, dgb, jnp.where(is_a, dga, 0.0)))
        dvec_ref[...] = jnp.zeros_like(dvec_ref)
        dvec_ref[0:1, :] = jnp.sum(jnp.where(is_f, dff, 0.0), axis=0, keepdims=True)
        dvec_ref[1:2, :] = jnp.sum(jnp.where(is_a, dg * nea * _softplus(za), 0.0), axis=0, keepdims=True)
        dvec_ref[2:3, :] = jnp.sum(jnp.where(is_a, dga, 0.0), axis=0, keepdims=True)

    vec = pl.BlockSpec((1, LANES), lambda i: (0, 0))
    full = lambda r, c: pl.BlockSpec((r, c), lambda i: (0, 0))
    return pl.pallas_call(
        body, name="small_bwd", grid=(1,),
        in_specs=[pl.BlockSpec((t, LANES), lambda i: (0, COL_SMALL // LANES)), vec, vec, vec, full(t, 512),
                  full(LANES, t), full(t, 512), full(t, 512)],
        out_specs=[full(t, LANES), full(8, LANES)], out_shape=[_sds((t, LANES)), _sds((8, LANES))],
        compiler_params=_params("arbitrary"),
    )(proj, fb, al, dtb, dcq, dckt, dbe, dge)


def _pack_dproj(dfox, dgdn, dgz, dsm):
    t = dgz.shape[0]
    tm = TOKEN_BLOCK

    def body(*refs):
        parts, dp_ref = refs[:8], refs[8]
        col = 0
        for part in parts:
            width = part.shape[1]
            dp_ref[:, col:col + width] = part[...].astype(BF)
            col += width

    tok = lambda w: pl.BlockSpec((tm, w), lambda i: (i, 0))
    return pl.pallas_call(
        body, name="pack_dproj", grid=(t // tm,), in_specs=[tok(D_FOX)] * 3 + [tok(D_GDN)] * 4 + [tok(LANES)],
        out_specs=tok(PROJ_W), out_shape=_sds((t, PROJ_W), BF), compiler_params=_params("parallel"),
    )(*dfox, *dgdn, dgz, dsm)


def _in_bwd(dproj, wt_al, x, nw, dx1):
    t = x.shape[0]
    tm = TOKEN_BLOCK

    def body(dp_ref, w_ref, x_ref, nw_ref, dx1_ref, dx_ref, dnw_ref):
        i = pl.program_id(0)

        @pl.when(i == 0)
        def _():
            dnw_ref[...] = jnp.zeros_like(dnw_ref)

        dh = jnp.dot(dp_ref[...], w_ref[...], preferred_element_type=F32)
        xv = x_ref[...]
        r = lax.rsqrt(jnp.mean(xv * xv, axis=-1, keepdims=True) + EPS)
        xh = xv * r
        dnw_ref[...] += jnp.sum(dh * xh, axis=0, keepdims=True)
        dxh = dh * nw_ref[...]
        dx_ref[...] = dx1_ref[...] + r * (dxh - xh * jnp.mean(dxh * xh, axis=-1, keepdims=True))

    tok = lambda w: pl.BlockSpec((tm, w), lambda i: (i, 0))
    vec = lambda w: pl.BlockSpec((1, w), lambda i: (0, 0))
    return pl.pallas_call(
        body, name="in_bwd", grid=(t // tm,),
        in_specs=[tok(PROJ_W), pl.BlockSpec((PROJ_W, D_MODEL), lambda i: (0, 0)), tok(D_MODEL), vec(D_MODEL),
                  tok(D_MODEL)],
        out_specs=[tok(D_MODEL), vec(D_MODEL)], out_shape=[_sds((t, D_MODEL)), _sds((1, D_MODEL))],
        compiler_params=_params("arbitrary"),
    )(dproj, wt_al, x, nw, dx1)


def _row(v, width=None):
    v = v.reshape(1, -1).astype(F32)
    if width is not None and v.shape[1] < width:
        v = jnp.pad(v, ((0, 0), (0, width - v.shape[1])))
    return v


def _lane_vec(v, first):
    return jnp.zeros((1, LANES), F32).at[0, first:first + v.shape[0]].set(v.astype(F32))


def _local_step(x, target, wt_al, late_weights, on_grads, convw, pre_mix_norm, fox_f_bias, fox_out_norm,
                gdn_a_log, gdn_dt_bias, gdn_out_norm, post_mix_norm, pre_mlp_norm, post_mlp_norm):
    t = x.shape[0]
    nch = t // CHUNK
    nw, pmw, plw, pw = _row(pre_mix_norm), _row(post_mix_norm), _row(pre_mlp_norm), _row(post_mlp_norm)
    fb, al, dtb = _lane_vec(fox_f_bias, SM_FF), _lane_vec(gdn_a_log, SM_GA), _lane_vec(gdn_dt_bias, SM_GA)
    fnw = _row(jnp.tile(fox_out_norm, 2))
    gnw = _row(gdn_out_norm)

    proj, h = _norm_proj(x, nw, wt_al)
    ce, cumt, be, ge = _small_prep(proj, fb, al, dtb)
    o_fox, lse, fox_n = _fox_fwd(proj, ce, cumt, fnw)
    qn, kn, cv, gc, mmat, amat = _gdn_prep(proj, convw, be, ge)
    n_prob = N_GDN_HEADS * nch
    m2 = mmat.reshape(n_prob, CHUNK * CHUNK)
    if n_prob < LANES:
        m2 = jnp.pad(m2, ((0, LANES - n_prob), (0, 0)))
    tinv = _tri_inverse(m2)[:n_prob].reshape(N_GDN_HEADS, nch, CHUNK, CHUNK)
    gdn_o, s_all, vn_all = _gdn_scan(qn, kn, cv, be, gc, tinv, amat)
    w_out = late_weights("w_out", gdn_o)
    x1, h2, mixed, omix, h2t = _mix_out(fox_n, gdn_o, proj, gnw, w_out, x, pmw, plw)
    w_up, w_down = late_weights("mlp", h2)
    up = _mlp_up(h2, w_up)
    dy, dx2, loss, d_pw = _mlp_down_loss(up, w_down, x1, pw, target)

    dup = _mlp_bwd_act(dy, w_down, up)
    relu2 = lambda u: jnp.square(jnp.maximum(u, 0.0))
    g_down = _wgrad(up, dy, D_FF // N_DEV, a_fn=relu2, name="wgrad_down")
    g_up = _wgrad_pre_t(h2t, dup, D_FF // N_DEV, name="wgrad_up")
    token = on_grads("mlp", (g_up, g_down))
    dx1, dmixed, d_plw, d_pmw = _mlp_bwd_in(dup, w_up, x1, plw + token[0:1, 0:1], dx2, mixed, pmw)
    token = on_grads("w_out", _wgrad(omix, dmixed, 512, split=4, name="wgrad_out"))
    do_fox, dgo, dgz, d_fnw, d_gnw = _out_bwd(dmixed, w_out, o_fox, gdn_o, proj, fnw + token[0:1, 0:1], gnw)
    dfq, dfk, dfv, dcq, dckt = _fox_bwd(proj, ce, cumt, lse, o_fox, do_fox)
    dqn, dkn, dcv, dbe, dge = _gdn_bwd(qn, kn, cv, be, gc, tinv, amat, s_all, vn_all, dgo)
    dxq, dxk, dxv, dwq, dwk, dwv = _gdn_bwd_conv(proj, convw, dqn, dkn, dcv)
    dsm, dvec = _small_bwd(proj, fb, al, dtb, dcq, dckt, dbe, dge)
    dproj = _pack_dproj((dfq, dfk, dfv), (dxq, dxk, dxv), dgz, dsm)
    g_main = _wgrad(dproj, h, 512, name="wgrad_in")
    g_tail = _wgrad(dproj, h, LANES, a_block0=COL_SMALL // LANES, name="wgrad_in_small")
    token = on_grads("w_in", jnp.concatenate([g_main.reshape(COL_SMALL, D_MODEL), g_tail[0]]))
    grad_x, d_nw = _in_bwd(dproj, wt_al, x, nw + token[0:1, 0:1], dx1)
    small = dict(loss=loss[0, 0:1], pre_mix_norm=d_nw[0], fox_f_bias=dvec[0, SM_FF:SM_FF + N_FOX_HEADS],
                 fox_out_norm=d_fnw[0, :FOX_HEAD_DIM], gdn_conv_w=(dwq, dwk, dwv),
                 gdn_a_log=dvec[1, SM_GA:SM_GA + N_GDN_HEADS], gdn_dt_bias=dvec[2, SM_GA:SM_GA + N_GDN_HEADS],
                 gdn_out_norm=d_gnw[0], post_mix_norm=d_pmw[0], pre_mlp_norm=d_plw[0], post_mlp_norm=d_pw[0])
    return grad_x, small


MESH_IDS = pl.DeviceIdType.MESH
CHIP_FLIPS = ((0, 0), (1, 0), (0, 1), (1, 1))
ANY_SPEC = pl.BlockSpec(memory_space=pl.ANY)


def _place():
    return lax.axis_index("x"), lax.axis_index("y"), lax.axis_index("c")


def _all_gather(blocks):
    n = len(blocks)

    def body(*refs):
        ins, outs, (send_sems, recv_sems, local_sems) = refs[:n], refs[n:2 * n], refs[2 * n:]
        x, y, c = _place()
        sibling = (x, y, 1 - c)
        chips = [(x ^ fx, y ^ fy) for fx, fy in CHIP_FLIPS[1:]]

        def slot(out, px, py, pc):
            return out.at[4 * px + 2 * py + pc]

        def copy(a, k, block, to, src=None):
            return pltpu.make_async_remote_copy(
                src_ref=slot(outs[a], *block) if src is None else src, dst_ref=slot(outs[a], *block),
                send_sem=send_sems.at[a, k], recv_sem=recv_sems.at[a, k], device_id=to, device_id_type=MESH_IDS)

        pending = []
        for a in range(n):
            mine = pltpu.make_async_copy(ins[a], slot(outs[a], x, y, c), local_sems.at[a])
            mine.start()
            pending.append(mine)
        sends = []
        for a in range(n):
            first = [copy(a, 0, (x, y, c), sibling, src=ins[a])]
            first += [copy(a, 1 + j, (x, y, c), (*chip, c), src=ins[a]) for j, chip in enumerate(chips)]
            for cp in first:
                cp.start()
            sends += first
        for a in range(n):
            for j, chip in enumerate(chips):
                copy(a, 1 + j, (*chip, c), (x, y, c)).wait_recv()
                fwd = copy(a, 4 + j, (*chip, c), sibling)
                fwd.start()
                sends.append(fwd)
        for a in range(n):
            copy(a, 0, sibling, (x, y, c)).wait_recv()
            for j, chip in enumerate(chips):
                copy(a, 4 + j, (*chip, 1 - c), (x, y, c)).wait_recv()
        for cp in sends:
            cp.wait_send()
        for cp in pending:
            cp.wait()

    return pl.pallas_call(
        body, name="all_gather_weights", in_specs=[ANY_SPEC] * n, out_specs=[ANY_SPEC] * n,
        out_shape=[_sds((N_DEV,) + b.shape, b.dtype) for b in blocks],
        scratch_shapes=[pltpu.SemaphoreType.DMA((n, 7)), pltpu.SemaphoreType.DMA((n, 7)), pltpu.SemaphoreType.DMA((n,))],
        compiler_params=pltpu.CompilerParams(has_side_effects=True),
    )(*blocks)


def _adamw(w, g, m, v):
    m = ADAM_B1 * m + (1.0 - ADAM_B1) * g
    v = ADAM_B2 * v + (1.0 - ADAM_B2) * (g * g)
    m_hat = m / (1.0 - ADAM_B1 ** ADAM_STEP)
    v_hat = v / (1.0 - ADAM_B2 ** ADAM_STEP)
    return -ADAM_LR * (m_hat / (jnp.sqrt(v_hat) + ADAM_EPS) + ADAM_WD * w), m, v


HBM_SPEC = pl.BlockSpec(memory_space=pltpu.HBM)
SEM_SPEC = pl.BlockSpec(memory_space=pltpu.SEMAPHORE)
DATAFLOW = pltpu.SideEffectType.DATAFLOW_SIDE_EFFECTING


def _peers():
    x, y, c = _place()
    return 4 * x + 2 * y + c, [(x ^ (k >> 2), y ^ ((k >> 1) & 1), c ^ (k & 1)) for k in range(1, N_DEV)]


def _peer_index(peer):
    return 4 * peer[0] + 2 * peer[1] + peer[2]


def _zones_with_own(srcs, pieces, name, after=None):
    n = len(srcs)
    extra = [] if after is None else [after]

    def body(me_ref, *refs):
        for a in range(n):
            refs[n + len(extra) + a][0] = refs[a][0] if pieces else refs[a][...]

    shapes = [s_.shape[1:] if pieces else s_.shape for s_ in srcs]
    mine = lambda sh: pl.BlockSpec((1,) + sh, lambda i, me_ref: (me_ref[0], 0, 0))
    in_specs = [mine(sh) if pieces else pl.BlockSpec(sh, lambda i, me_ref: (0, 0)) for sh in shapes]
    x, y, c = _place()
    return pl.pallas_call(
        body, name=name,
        grid_spec=pltpu.PrefetchScalarGridSpec(num_scalar_prefetch=1, grid=(1,), in_specs=in_specs + [ANY_SPEC] * len(extra),
                                               out_specs=[mine(sh) for sh in shapes]),
        out_shape=[_sds((N_DEV,) + sh, s_.dtype) for sh, s_ in zip(shapes, srcs)],
        compiler_params=_params("arbitrary"),
    )((4 * x + 2 * y + c).astype(jnp.int32).reshape(1), *srcs, *extra)


def _exchange_start(srcs, zones, pieces, name):
    n = len(srcs)

    def body(*refs):
        ins, zs = refs[:n], refs[n:2 * n]
        sems = refs[2 * n:4 * n]
        token = refs[-1]
        me, peers = _peers()
        for peer in peers:
            for a in range(n):
                pltpu.make_async_remote_copy(
                    src_ref=ins[a].at[_peer_index(peer)] if pieces else ins[a], dst_ref=zs[a].at[me],
                    send_sem=sems[2 * a], recv_sem=sems[2 * a + 1], device_id=peer, device_id_type=MESH_IDS).start()
        token[...] = jnp.zeros_like(token)

    hbm = lambda v: pltpu.with_memory_space_constraint(v, pltpu.HBM)
    out = pl.pallas_call(
        body, name=name,
        out_shape=tuple([pltpu.SemaphoreType.DMA(())] * (2 * n) + [pltpu.HBM(v.shape, v.dtype) for v in srcs]
                        + [pltpu.HBM(z.shape, z.dtype) for z in zones] + [_sds((8, LANES))]),
        in_specs=[HBM_SPEC] * (2 * n), out_specs=tuple([SEM_SPEC] * (2 * n) + [HBM_SPEC] * (2 * n) + [VMEM_SPEC]),
        input_output_aliases={i: 2 * n + i for i in range(2 * n)},
        compiler_params=pltpu.CompilerParams(has_side_effects=DATAFLOW),
    )(*[hbm(v) for v in srcs], *[hbm(z) for z in zones])
    return out[:2 * n], out[2 * n:3 * n], out[3 * n:4 * n], out[-1]


def _exchange_wait(sems, srcs, zones, after, name):
    n = len(srcs)
    after = list(after) if isinstance(after, (list, tuple)) else [after]

    def body(*refs):
        ins, zs, sm = refs[:n], refs[n:2 * n], refs[2 * n:4 * n]
        me, peers = _peers()
        for a in range(n):
            seven = zs[a].at[pl.ds(0, N_DEV - 1)]
            cp = pltpu.make_async_remote_copy(src_ref=seven, dst_ref=seven, send_sem=sm[2 * a], recv_sem=sm[2 * a + 1],
                                              device_id=peers[0], device_id_type=MESH_IDS)
            cp.wait_send()
            cp.wait_recv()

    out = pl.pallas_call(
        body, name=name, out_shape=tuple([pltpu.HBM(v.shape, v.dtype) for v in srcs] + [pltpu.HBM(z.shape, z.dtype) for z in zones]),
        in_specs=[HBM_SPEC] * (2 * n) + [SEM_SPEC] * (2 * n) + [ANY_SPEC] * len(after),
        out_specs=tuple([HBM_SPEC] * (2 * n)), input_output_aliases={i: i for i in range(2 * n)},
        compiler_params=pltpu.CompilerParams(has_side_effects=DATAFLOW),
    )(*srcs, *zones, *sems, *after)
    return out[n:]


def _sum_adamw(zone, w, m, v, name):
    _, r, c_ = zone.shape
    rb = 128 if r % 128 == 0 else r

    def body(z_ref, w_ref, m_ref, v_ref, grad_ref, delta_ref, nm_ref, nv_ref):
        total = z_ref[0].astype(F32)
        for d in range(1, N_DEV):
            total = total + z_ref[d].astype(F32)
        grad_ref[...] = total
        delta_ref[...], nm_ref[...], nv_ref[...] = _adamw(w_ref[...], total, m_ref[...], v_ref[...])

    blk = pl.BlockSpec((rb, c_), lambda i: (i, 0))
    return pl.pallas_call(
        body, name=name, grid=(r // rb,), in_specs=[pl.BlockSpec((N_DEV, rb, c_), lambda i: (0, i, 0)), blk, blk, blk],
        out_specs=[blk] * 4, out_shape=[_sds((r, c_))] * 4, compiler_params=_params("parallel"),
    )(zone, w, m, v)


SMALL_ROWS = 16


def _sum_small(zone):
    def body(z_ref, sum_ref):
        total = z_ref[0]
        for d in range(1, N_DEV):
            total = total + z_ref[d]
        sum_ref[...] = total

    return pl.pallas_call(body, name="sum_small", in_specs=[VMEM_SPEC], out_specs=VMEM_SPEC,
                          out_shape=_sds(zone.shape[1:]))(zone)


def _adamw_small(w, g, m, v):
    def body(w_ref, g_ref, m_ref, v_ref, delta_ref, nm_ref, nv_ref):
        delta_ref[...], nm_ref[...], nv_ref[...] = _adamw(w_ref[...], g_ref[...], m_ref[...], v_ref[...])

    return pl.pallas_call(body, name="adamw_small", in_specs=[VMEM_SPEC] * 4, out_specs=[VMEM_SPEC] * 3,
                          out_shape=[_sds(w.shape)] * 3)(w, g, m, v)


NATIVE_ROWS = ((0, 1536), (1544, 3080), (3088, 3600), (1536, 1544), (3080, 3088))


def _to_aligned_rows(wt_native):
    pad = jnp.zeros((PROJ_W - D_PROJ, wt_native.shape[1]), wt_native.dtype)
    return jnp.concatenate([wt_native[lo:hi] for lo, hi in NATIVE_ROWS] + [pad])


def _from_aligned_rows(gt_al):
    return jnp.concatenate([gt_al[0:1536], gt_al[3584:3592], gt_al[1536:3072], gt_al[3592:3600], gt_al[3072:3584]])


def _cols_from_pieces(p):
    return p.transpose(1, 0, 2).reshape(p.shape[1], -1)


SMALL_NORMS = ("pre_mix_norm", "post_mix_norm", "pre_mlp_norm", "post_mlp_norm")
SMALL_MISC = (("fox_out_norm", FOX_HEAD_DIM), ("gdn_out_norm", GDN_HEAD_DIM), ("fox_f_bias", N_FOX_HEADS),
              ("gdn_a_log", N_GDN_HEADS), ("gdn_dt_bias", N_GDN_HEADS), ("loss", 1))


def _pack_small(vals, conv):
    misc = jnp.concatenate([vals[n].astype(F32) if n in vals else jnp.zeros((size,), F32) for n, size in SMALL_MISC])
    rows = [vals[n].astype(F32) for n in SMALL_NORMS] + [jnp.pad(misc, (0, D_MODEL - misc.shape[0]))]
    flat = conv.astype(F32).reshape(-1)
    n_rows = -(-flat.shape[0] // D_MODEL)
    flat = jnp.pad(flat, (0, n_rows * D_MODEL - flat.shape[0])).reshape(n_rows, D_MODEL)
    packed = jnp.concatenate([jnp.stack(rows), flat])
    return jnp.pad(packed, ((0, SMALL_ROWS - packed.shape[0]), (0, 0)))


def _unpack_small(packed, conv_shape):
    out = {n: packed[i] for i, n in enumerate(SMALL_NORMS)}
    off = 0
    for n, size in SMALL_MISC:
        out[n] = packed[4, off:off + size]
        off += size
    n_conv = conv_shape[0] * conv_shape[1]
    out["gdn_conv_w"] = packed[5:].reshape(-1)[:n_conv].reshape(conv_shape)
    return out


WEIGHT_ORDER = ("pre_mix_norm", "w_in", "fox_f_bias", "fox_out_norm", "gdn_conv_w", "gdn_a_log", "gdn_dt_bias",
                "gdn_out_norm", "w_out", "post_mix_norm", "pre_mlp_norm", "w_up", "w_down", "post_mlp_norm")


def kernel(x, pre_mix_norm, w_in, fox_f_bias, fox_out_norm, gdn_conv_w, gdn_a_log, gdn_dt_bias, gdn_out_norm, w_out, post_mix_norm, pre_mlp_norm, w_up, w_down, post_mlp_norm, loss_target, m_pre_mix_norm, m_w_in, m_fox_f_bias, m_fox_out_norm, m_gdn_conv_w, m_gdn_a_log, m_gdn_dt_bias, m_gdn_out_norm, m_w_out, m_post_mix_norm, m_pre_mlp_norm, m_w_up, m_w_down, m_post_mlp_norm, v_pre_mix_norm, v_w_in, v_fox_f_bias, v_fox_out_norm, v_gdn_conv_w, v_gdn_a_log, v_gdn_dt_bias, v_gdn_out_norm, v_w_out, v_post_mix_norm, v_pre_mlp_norm, v_w_up, v_w_down, v_post_mlp_norm):
    w = dict(pre_mix_norm=pre_mix_norm, w_in=w_in, fox_f_bias=fox_f_bias, fox_out_norm=fox_out_norm,
             gdn_conv_w=gdn_conv_w, gdn_a_log=gdn_a_log, gdn_dt_bias=gdn_dt_bias, gdn_out_norm=gdn_out_norm, w_out=w_out,
             post_mix_norm=post_mix_norm, pre_mlp_norm=pre_mlp_norm, w_up=w_up, w_down=w_down, post_mlp_norm=post_mlp_norm)
    mom = dict(pre_mix_norm=m_pre_mix_norm, w_in=m_w_in, fox_f_bias=m_fox_f_bias, fox_out_norm=m_fox_out_norm,
               gdn_conv_w=m_gdn_conv_w, gdn_a_log=m_gdn_a_log, gdn_dt_bias=m_gdn_dt_bias, gdn_out_norm=m_gdn_out_norm,
               w_out=m_w_out, post_mix_norm=m_post_mix_norm, pre_mlp_norm=m_pre_mlp_norm, w_up=m_w_up, w_down=m_w_down,
               post_mlp_norm=m_post_mlp_norm)
    var = dict(pre_mix_norm=v_pre_mix_norm, w_in=v_w_in, fox_f_bias=v_fox_f_bias, fox_out_norm=v_fox_out_norm,
               gdn_conv_w=v_gdn_conv_w, gdn_a_log=v_gdn_a_log, gdn_dt_bias=v_gdn_dt_bias, gdn_out_norm=v_gdn_out_norm,
               w_out=v_w_out, post_mix_norm=v_post_mix_norm, pre_mlp_norm=v_pre_mlp_norm, w_up=v_w_up, w_down=v_w_down,
               post_mlp_norm=v_post_mlp_norm)

    win_g, conv_g = _all_gather([w_in.T.astype(BF), gdn_conv_w])
    wt_al = _to_aligned_rows(win_g.reshape(D_PROJ, D_MODEL))
    convw = _cols_from_pieces(conv_g)
    gathers, after = {}, win_g
    for name, shards in (("w_out", [w_out.astype(BF)]), ("mlp", [w_up.astype(BF), w_down.astype(BF)])):
        zones = _zones_with_own(shards, False, "gather_" + name + "_own", after=after)
        gathers[name] = _exchange_start(shards, zones, False, "gather_" + name + "_start")
        after = gathers[name][3]

    def late_weights(name, after):
        sems, shards, zones, _ = gathers[name]
        got = _exchange_wait(sems, shards, zones, after, "gather_" + name + "_wait")
        if name == "w_out":
            return got[0].reshape(D_MODEL, D_MODEL)
        return got[0], got[1].reshape(D_FF, D_MODEL)

    scatters = {}

    def on_grads(name, g):
        if name == "w_in":
            g = _from_aligned_rows(g).reshape(N_DEV, D_PROJ // N_DEV, D_MODEL)
        srcs = list(g) if name == "mlp" else [g]
        scatters[name] = _exchange_start(srcs, _zones_with_own(srcs, True, "scatter_" + name + "_own"), True,
                                         "scatter_" + name + "_start")
        return scatters[name][3]

    grad_x, small = _local_step(
        x[0], loss_target[0], wt_al, late_weights, on_grads, convw, pre_mix_norm + after[0, 0],
        fox_f_bias, fox_out_norm, gdn_a_log, gdn_dt_bias, gdn_out_norm, post_mix_norm, pre_mlp_norm, post_mlp_norm)
    dwq, dwk, dwv = small.pop("gdn_conv_w")
    packed = [_pack_small(small, jnp.concatenate([dwq, dwk, dwv], axis=1))]
    scatters["small"] = _exchange_start(packed, _zones_with_own(packed, False, "small_own"), False, "small_start")

    grads, delta, new_m, new_v = {}, {}, {}, {}
    after = scatters["small"][3]
    for name, members in (("mlp", ("w_up", "w_down")), ("w_out", ("w_out",)), ("small", ()), ("w_in", ("w_in",))):
        sems, srcs, zones, _ = scatters[name]
        zones = _exchange_wait(sems, srcs, zones, after, "scatter_" + name + "_wait")
        if name == "small":
            total = _unpack_small(_sum_small(zones[0]), (CONV_K, 3 * D_GDN))
            after = total["pre_mix_norm"]
        for n, zone in zip(members, zones):
            if n == "w_in":
                res = _sum_adamw(zone, w[n].T, mom[n].T, var[n].T, "adamw_" + n)
                grads[n], delta[n], new_m[n], new_v[n] = [r.T for r in res]
            else:
                grads[n], delta[n], new_m[n], new_v[n] = _sum_adamw(zone, w[n], mom[n], var[n], "adamw_" + n)
        if members:
            after = [grads[n] for n in members]

    loss = total.pop("loss")[0]
    me = 4 * lax.axis_index("x") + 2 * lax.axis_index("y") + lax.axis_index("c")
    n_conv = gdn_conv_w.shape[1]
    total["gdn_conv_w"] = lax.dynamic_slice_in_dim(total["gdn_conv_w"], me * n_conv, n_conv, axis=1)
    grads.update(total)
    d_s, m_s, v_s = _adamw_small(_pack_small(w, gdn_conv_w), _pack_small(total, total["gdn_conv_w"]),
                                 _pack_small(mom, m_gdn_conv_w), _pack_small(var, v_gdn_conv_w))
    delta.update(_unpack_small(d_s, gdn_conv_w.shape))
    new_m.update(_unpack_small(m_s, gdn_conv_w.shape))
    new_v.update(_unpack_small(v_s, gdn_conv_w.shape))

    return (loss, grad_x[None], *[grads[n] for n in WEIGHT_ORDER], *[delta[n] for n in WEIGHT_ORDER],
            *[new_m[n] for n in WEIGHT_ORDER], *[new_v[n] for n in WEIGHT_ORDER])
```

```python
import jax
import jax.numpy as jnp
from jax import lax
from jax.experimental import pallas as pl
from jax.experimental.pallas import tpu as pltpu

F32 = jnp.float32
BF = jnp.bfloat16

D_MODEL = 1024
N_FOX_HEADS, FOX_HEAD_DIM = 8, 64
N_GDN_HEADS, GDN_HEAD_DIM = 4, 128
D_FOX = N_FOX_HEADS * FOX_HEAD_DIM
D_GDN = N_GDN_HEADS * GDN_HEAD_DIM
CHUNK = 64
CONV_K = 4
D_FF = 4 * D_MODEL
EPS = 1e-6
D_PROJ = 3600
N_DEV = 8

PROJ_W = 3712
COL_FOX, COL_GDN, COL_GZ, COL_SMALL = 0, 1536, 3072, 3584
LANES = 128
SM_FF, SM_GB, SM_GA = 0, 8, 12

ADAM_LR, ADAM_B1, ADAM_B2, ADAM_EPS, ADAM_WD, ADAM_STEP = 0.001, 0.9, 0.999, 1e-08, 0.01, 10

TOKEN_BLOCK = 256
FOX_SCALE = FOX_HEAD_DIM ** -0.5
GDN_QSCALE = GDN_HEAD_DIM ** -0.5
NEG_BIG = -1e30
VMEM_LIMIT = 56 * 1024 * 1024

VMEM_SPEC = pl.BlockSpec(memory_space=pltpu.VMEM)
HIGHEST = lax.Precision.HIGHEST


def _sds(shape, dtype=F32):
    return jax.ShapeDtypeStruct(shape, dtype)


def _params(*sem):
    return pltpu.CompilerParams(dimension_semantics=sem if sem else None, vmem_limit_bytes=VMEM_LIMIT)


def _mm(a, b):
    return jnp.dot(a.astype(BF), b.astype(BF), preferred_element_type=F32)


def _mm_nt(a, b):
    return lax.dot_general(a.astype(BF), b.astype(BF), (((1,), (1,)), ((), ())), preferred_element_type=F32)


def _mm_tn(a, b):
    return lax.dot_general(a.astype(BF), b.astype(BF), (((0,), (0,)), ((), ())), preferred_element_type=F32)


def _mm_exact(a, b):
    return jnp.dot(a, b, precision=HIGHEST, preferred_element_type=F32)


def _mm_tn_exact(a, b):
    return lax.dot_general(a, b, (((0,), (0,)), ((), ())), precision=HIGHEST, preferred_element_type=F32)


def _sigmoid(x):
    return 1.0 / (1.0 + jnp.exp(-x))


def _softplus(x):
    return jnp.maximum(x, 0.0) + jnp.log1p(jnp.exp(-jnp.abs(x)))


def _iota(shape, dim):
    return lax.broadcasted_iota(jnp.int32, shape, dim)


def _shift_down(x, s, row):
    return jnp.where(row >= s, pltpu.roll(x, s, 0), 0.0)


def _shift_up(x, s, row):
    n = x.shape[0]
    return jnp.where(row < n - s, pltpu.roll(x, n - s, 0), 0.0)


def _norm_proj(x, nw, wt_al):
    t = x.shape[0]

    def body(x_ref, nw_ref, w_ref, proj_ref, h_ref):
        xv = x_ref[...]
        r = lax.rsqrt(jnp.mean(xv * xv, axis=-1, keepdims=True) + EPS)
        h = (xv * r * nw_ref[...]).astype(BF)
        h_ref[...] = h
        proj_ref[...] = lax.dot_general(h, w_ref[...], (((1,), (1,)), ((), ())), preferred_element_type=F32)

    tm = TOKEN_BLOCK
    return pl.pallas_call(
        body, name="norm_proj", grid=(t // tm,),
        in_specs=[pl.BlockSpec((tm, D_MODEL), lambda i: (i, 0)), pl.BlockSpec((1, D_MODEL), lambda i: (0, 0)),
                  pl.BlockSpec((PROJ_W, D_MODEL), lambda i: (0, 0))],
        out_specs=[pl.BlockSpec((tm, PROJ_W), lambda i: (i, 0)), pl.BlockSpec((tm, D_MODEL), lambda i: (i, 0))],
        out_shape=[_sds((t, PROJ_W)), _sds((t, D_MODEL), BF)],
        compiler_params=_params("parallel"),
    )(x, nw, wt_al)


def _expand_matrix(first_row, group):
    row = _iota((LANES, 512), 0)
    col = _iota((LANES, 512), 1)
    return (col // group + first_row == row).astype(F32)


def _small_prep(proj, fb, al, dtb):
    t = proj.shape[0]

    def body(sm_ref, fb_ref, al_ref, dtb_ref, ce_ref, cumt_ref, be_ref, ge_ref):
        s = sm_ref[...]
        z = s + fb_ref[...]
        cum = jnp.minimum(z, 0.0) - jnp.log1p(jnp.exp(-jnp.abs(z)))
        row = _iota((t, LANES), 0)
        step = 1
        while step < t:
            cum = cum + _shift_down(cum, step, row)
            step *= 2
        cumt_ref[...] = cum.T
        ce_ref[...] = _mm_exact(cum, _expand_matrix(SM_FF, FOX_HEAD_DIM))
        be_ref[...] = _mm_exact(_sigmoid(s), _expand_matrix(SM_GB, GDN_HEAD_DIM))
        g = -jnp.exp(al_ref[...]) * _softplus(s + dtb_ref[...])
        ge_ref[...] = _mm_exact(g, _expand_matrix(SM_GA, GDN_HEAD_DIM))

    vec = pl.BlockSpec((1, LANES), lambda i: (0, 0))
    return pl.pallas_call(
        body, name="small_prep", grid=(1,),
        in_specs=[pl.BlockSpec((t, LANES), lambda i: (0, COL_SMALL // LANES)), vec, vec, vec],
        out_specs=[pl.BlockSpec((t, 512), lambda i: (0, 0)), pl.BlockSpec((LANES, t), lambda i: (0, 0)),
                   pl.BlockSpec((t, 512), lambda i: (0, 0)), pl.BlockSpec((t, 512), lambda i: (0, 0))],
        out_shape=[_sds((t, 512)), _sds((LANES, t)), _sds((t, 512)), _sds((t, 512))],
        compiler_params=_params("arbitrary"),
    )(proj, fb, al, dtb)


def _fox_scores(qh, kb, ce_ref, cumt_ref, head, hh, i, tq):
    klen = (i + 1) * tq
    s = _mm_nt(qh, kb[:klen]) * FOX_SCALE
    cq = ce_ref[i * tq:(i + 1) * tq, FOX_HEAD_DIM * hh:FOX_HEAD_DIM * hh + 1]
    ck = cumt_ref[pl.ds(head, 1), 0:klen]
    s = s + cq - ck
    qi = _iota((tq, klen), 0) + i * tq
    ki = _iota((tq, klen), 1)
    return jnp.where(ki <= qi, s, NEG_BIG)


def _fox_fwd(proj, ce, cumt, fnw):
    t = proj.shape[0]
    tq = min(TOKEN_BLOCK, t // 2)
    nq = t // tq

    def body(q_ref, k_ref, v_ref, ce_ref, cumt_ref, fnw_ref, o_ref, lse_ref, fn_ref):
        j = pl.program_id(0)
        first = _iota((1, LANES), 1) < FOX_HEAD_DIM
        kb = k_ref[...].astype(BF)
        vb = v_ref[...].astype(BF)
        for i in range(nq):
            rows = slice(i * tq, (i + 1) * tq)
            klen = (i + 1) * tq
            q_i = q_ref[rows, :]
            o_acc = jnp.zeros((tq, LANES), F32)
            lse_acc = jnp.zeros((tq, LANES), F32)
            for hh in range(2):
                mh = first if hh == 0 else jnp.logical_not(first)
                qh = jnp.where(mh, q_i, 0.0).astype(BF)
                s = _fox_scores(qh, kb, ce_ref, cumt_ref, 2 * j + hh, hh, i, tq)
                m = jnp.max(s, axis=-1, keepdims=True)
                p = jnp.exp(s - m)
                l = jnp.sum(p, axis=-1, keepdims=True)
                o = jnp.dot(p.astype(BF), vb[:klen], preferred_element_type=F32) / l
                o_acc = jnp.where(mh, o, o_acc)
                lse_acc = jnp.where(mh, m + jnp.log(l), lse_acc)
            o_ref[rows, :] = o_acc
            lse_ref[rows, :] = lse_acc
            o2 = o_acc * o_acc
            s0 = jnp.sum(jnp.where(first, o2, 0.0), axis=-1, keepdims=True)
            s1 = jnp.sum(jnp.where(first, 0.0, o2), axis=-1, keepdims=True)
            r = lax.rsqrt(jnp.where(first, s0, s1) * (1.0 / FOX_HEAD_DIM) + EPS)
            fn_ref[rows, :] = (o_acc * r * fnw_ref[...]).astype(BF)

    blk = lambda off: pl.BlockSpec((t, LANES), lambda j: (0, off + j))
    return pl.pallas_call(
        body, name="fox_fwd", grid=(N_FOX_HEADS // 2,),
        in_specs=[blk(0), blk(4), blk(8), blk(0), pl.BlockSpec((LANES, t), lambda j: (0, 0)),
                  pl.BlockSpec((1, LANES), lambda j: (0, 0))],
        out_specs=[blk(0), blk(0), blk(0)],
        out_shape=[_sds((t, D_FOX)), _sds((t, D_FOX)), _sds((t, D_FOX), BF)],
        compiler_params=_params("parallel"),
    )(proj, proj, proj, ce, cumt, fnw)


def _fox_bwd(proj, ce, cumt, lse, o, do):
    t = proj.shape[0]
    tq = min(TOKEN_BLOCK, t // 2)
    nq = t // tq

    def body(q_ref, k_ref, v_ref, ce_ref, cumt_ref, lse_ref, o_ref, do_ref,
             dq_ref, dk_ref, dv_ref, dcq_ref, dckt_ref, dk_s, dv_s, dck_s):
        j = pl.program_id(0)
        first = _iota((1, LANES), 1) < FOX_HEAD_DIM
        kf = k_ref[...]
        kb = kf.astype(BF)
        vb = v_ref[...].astype(BF)
        dk_s[...] = jnp.zeros_like(dk_s)
        dv_s[...] = jnp.zeros_like(dv_s)
        dck_s[...] = jnp.zeros_like(dck_s)
        masks = [first, jnp.logical_not(first)]
        kmask = [jnp.where(mh, kf, 0.0).astype(BF) for mh in masks]
        for i in range(nq):
            rows = slice(i * tq, (i + 1) * tq)
            klen = (i + 1) * tq
            q_i = q_ref[rows, :]
            do_i = do_ref[rows, :]
            o_i = o_ref[rows, :]
            lse_i = lse_ref[rows, :]
            dq_acc = jnp.zeros((tq, LANES), F32)
            dcq_acc = jnp.zeros((tq, LANES), F32)
            for hh in range(2):
                mh = masks[hh]
                qh = jnp.where(mh, q_i, 0.0).astype(BF)
                doh = jnp.where(mh, do_i, 0.0)
                dohb = doh.astype(BF)
                delta = jnp.sum(doh * o_i, axis=-1, keepdims=True)
                s = _fox_scores(qh, kb, ce_ref, cumt_ref, 2 * j + hh, hh, i, tq)
                p = jnp.exp(s - lse_i[:, FOX_HEAD_DIM * hh:FOX_HEAD_DIM * hh + 1])
                dp = _mm_nt(dohb, vb[:klen])
                ds = p * (dp - delta)
                dsb = ds.astype(BF)
                dq_acc = dq_acc + jnp.dot(dsb, kmask[hh][:klen], preferred_element_type=F32) * FOX_SCALE
                dk_s[0:klen, :] += _mm_tn(dsb, qh) * FOX_SCALE
                dv_s[0:klen, :] += _mm_tn(p, dohb)
                dcq_acc = jnp.where(mh, jnp.sum(ds, axis=-1, keepdims=True), dcq_acc)
                dck_s[hh:hh + 1, 0:klen] += jnp.sum(ds, axis=0, keepdims=True)
            dq_ref[rows, :] = dq_acc
            dcq_ref[rows, :] = dcq_acc
        dk_ref[...] = dk_s[...]
        dv_ref[...] = dv_s[...]
        dckt_ref[...] = jnp.zeros_like(dckt_ref)
        dckt_ref[0:8, :] = dck_s[...]

    blk = lambda off: pl.BlockSpec((t, LANES), lambda j: (0, off + j))
    return pl.pallas_call(
        body, name="fox_bwd", grid=(N_FOX_HEADS // 2,),
        in_specs=[blk(0), blk(4), blk(8), blk(0), pl.BlockSpec((LANES, t), lambda j: (0, 0)), blk(0), blk(0), blk(0)],
        out_specs=[blk(0), blk(0), blk(0), blk(0), pl.BlockSpec((32, t), lambda j: (j, 0))],
        out_shape=[_sds((t, D_FOX))] * 4 + [_sds((LANES, t))],
        scratch_shapes=[pltpu.VMEM((t, LANES), F32), pltpu.VMEM((t, LANES), F32), pltpu.VMEM((8, t), F32)],
        compiler_params=_params("parallel"),
    )(proj, proj, proj, ce, cumt, lse, o, do)


def _conv(x, w, row):
    return (w[3:4, :] * x + w[2:3, :] * _shift_down(x, 1, row) + w[1:2, :] * _shift_down(x, 2, row)
            + w[0:1, :] * _shift_down(x, 3, row))


def _chunk_decay(gc_c):
    gi = gc_c[:, 0:CHUNK]
    gj = gc_c.T[0:CHUNK, :]
    ri = _iota((CHUNK, CHUNK), 0)
    cj = _iota((CHUNK, CHUNK), 1)
    return jnp.where(ri >= cj, jnp.exp(jnp.minimum(gi - gj, 0.0)), 0.0), ri > cj


def _gdn_specs(t):
    col = lambda off: pl.BlockSpec((t, LANES), lambda h: (0, off + h))
    cw = lambda off: pl.BlockSpec((CONV_K, LANES), lambda h: (0, off + h))
    mat = pl.BlockSpec((1, t // CHUNK, CHUNK, CHUNK), lambda h: (h, 0, 0, 0))
    return col, cw, mat


def _gdn_prep(proj, convw, be, ge):
    t = proj.shape[0]
    nch = t // CHUNK

    def body(xq_ref, xk_ref, xv_ref, wq_ref, wk_ref, wv_ref, be_ref, ge_ref,
             qn_ref, kn_ref, cv_ref, gc_ref, m_ref, a_ref):
        row = _iota((t, LANES), 0)

        def act(x_ref, w_ref):
            y = _conv(x_ref[...], w_ref[...], row)
            return y * _sigmoid(y)

        cq = act(xq_ref, wq_ref)
        ck = act(xk_ref, wk_ref)
        cv_ref[...] = act(xv_ref, wv_ref)
        qn_ref[...] = cq * lax.rsqrt(jnp.sum(cq * cq, axis=-1, keepdims=True) + EPS) * GDN_QSCALE
        kn_ref[...] = ck * lax.rsqrt(jnp.sum(ck * ck, axis=-1, keepdims=True) + EPS)
        gc = ge_ref[...]
        pos = row % CHUNK
        step = 1
        while step < CHUNK:
            gc = gc + jnp.where(pos >= step, pltpu.roll(gc, step, 0), 0.0)
            step *= 2
        gc_ref[...] = gc

        def chunk(n, carry):
            sl = pl.ds(pl.multiple_of(n * CHUNK, CHUNK), CHUNK)
            k_c = kn_ref[sl, :]
            decay, strict = _chunk_decay(gc_ref[sl, :])
            m_ref[0, n] = jnp.where(strict, _mm_nt(k_c * be_ref[sl, :], k_c) * decay, 0.0)
            a_ref[0, n] = _mm_nt(qn_ref[sl, :], k_c) * decay
            return carry

        lax.fori_loop(0, nch, chunk, 0)

    col, cw, mat = _gdn_specs(t)
    return pl.pallas_call(
        body, name="gdn_prep", grid=(N_GDN_HEADS,),
        in_specs=[col(12), col(16), col(20), cw(0), cw(4), cw(8), col(0), col(0)],
        out_specs=[col(0), col(0), col(0), col(0), mat, mat],
        out_shape=[_sds((t, D_GDN))] * 4 + [_sds((N_GDN_HEADS, nch, CHUNK, CHUNK))] * 2,
        compiler_params=_params("parallel"),
    )(proj, proj, proj, convw, convw, convw, be, ge)


def _tri_inverse(m3):
    assert m3.shape == (LANES, CHUNK, CHUNK)

    def body(m_ref, t_ref, ms, ts):
        for i in range(CHUNK):
            ms[i * CHUNK:(i + 1) * CHUNK, :] = m_ref[:, i, :].T
        cidx = _iota((CHUNK, LANES), 0)

        def outer(i, carry):
            def inner(jj, acc):
                mrow = ms[pl.ds(i * CHUNK + jj, 1), :]
                return acc - mrow * ts[pl.ds(pl.multiple_of(jj * CHUNK, CHUNK), CHUNK), :]

            acc = lax.fori_loop(0, i, inner, jnp.where(cidx == i, 1.0, 0.0).astype(F32))
            ts[pl.ds(pl.multiple_of(i * CHUNK, CHUNK), CHUNK), :] = acc
            return carry

        lax.fori_loop(0, CHUNK, outer, 0)
        for i in range(CHUNK):
            t_ref[:, i, :] = ts[i * CHUNK:(i + 1) * CHUNK, :].T

    return pl.pallas_call(
        body, name="tri_inverse", in_specs=[VMEM_SPEC], out_specs=VMEM_SPEC,
        out_shape=_sds((LANES, CHUNK, CHUNK)),
        scratch_shapes=[pltpu.VMEM((CHUNK * CHUNK, LANES), F32), pltpu.VMEM((CHUNK * CHUNK, LANES), F32)],
        compiler_params=_params(),
    )(m3)


def _gdn_chunk_terms(q, k, v, b, gcc):
    eg = jnp.exp(gcc)
    last = gcc[CHUNK - 1:CHUNK, :]
    egl = jnp.exp(last - gcc)
    gl = jnp.exp(last)
    kb = k * b
    return eg, egl, gl, kb, v * b, kb * eg, q * eg, k * egl


GDN_BLOCK_CHUNKS = 4


def _gdn_block_specs(t, reverse):
    cb = GDN_BLOCK_CHUNKS
    nb = t // (cb * CHUNK)
    idx = (lambda i: nb - 1 - i) if reverse else (lambda i: i)
    tok = pl.BlockSpec((cb * CHUNK, D_GDN), lambda i: (idx(i), 0))
    mat = pl.BlockSpec((N_GDN_HEADS, cb, CHUNK, CHUNK), lambda i: (0, idx(i), 0, 0))
    state = pl.BlockSpec((N_GDN_HEADS, cb, GDN_HEAD_DIM, GDN_HEAD_DIM), lambda i: (0, idx(i), 0, 0))
    return nb, tok, mat, state


def _gdn_scan(qn, kn, cv, be, gc, tinv, amat):
    t = qn.shape[0]
    nch = t // CHUNK

    def body(q_ref, k_ref, v_ref, b_ref, gc_ref, t_ref, a_ref, o_ref, sall_ref, vn_ref, s_scr):
        @pl.when(pl.program_id(0) == 0)
        def _():
            s_scr[...] = jnp.zeros_like(s_scr)

        heads = range(N_GDN_HEADS)
        cols = [slice(hd * LANES, (hd + 1) * LANES) for hd in heads]
        s = [s_scr[hd] for hd in heads]
        for cc in range(GDN_BLOCK_CHUNKS):
            rs = slice(cc * CHUNK, (cc + 1) * CHUNK)
            terms = [_gdn_chunk_terms(q_ref[rs, cs], k_ref[rs, cs], v_ref[rs, cs], b_ref[rs, cs], gc_ref[rs, cs])
                     for cs in cols]
            for hd in heads:
                sall_ref[hd, cc] = s[hd]
            uw = [_mm(t_ref[hd, cc], jnp.concatenate([terms[hd][4], terms[hd][5]], axis=1)) for hd in heads]
            ws_qs = [_mm(jnp.concatenate([uw[hd][:, LANES:], terms[hd][6]], axis=0), s[hd]) for hd in heads]
            vn = [uw[hd][:, :LANES] - ws_qs[hd][:CHUNK] for hd in heads]
            a_vn = [_mm(a_ref[hd, cc], vn[hd]) for hd in heads]
            kd_vn = [_mm_tn(terms[hd][7], vn[hd]) for hd in heads]
            for hd in heads:
                vn_ref[rs, cols[hd]] = vn[hd]
                o_ref[rs, cols[hd]] = ws_qs[hd][CHUNK:] + a_vn[hd]
                s[hd] = s[hd] * terms[hd][2] + kd_vn[hd]
        for hd in heads:
            s_scr[hd] = s[hd]

    nb, tok, mat, state = _gdn_block_specs(t, False)
    return pl.pallas_call(
        body, name="gdn_scan", grid=(nb,),
        in_specs=[tok] * 5 + [mat, mat], out_specs=[tok, state, tok],
        out_shape=[_sds((t, D_GDN)), _sds((N_GDN_HEADS, nch, GDN_HEAD_DIM, GDN_HEAD_DIM)), _sds((t, D_GDN))],
        scratch_shapes=[pltpu.VMEM((N_GDN_HEADS, GDN_HEAD_DIM, GDN_HEAD_DIM), F32)],
        compiler_params=_params("arbitrary"),
    )(qn, kn, cv, be, gc, tinv, amat)


def _gdn_bwd(qn, kn, cv, be, gc, tinv, amat, s_all, vn_all, do):
    t = qn.shape[0]

    def body(q_ref, k_ref, v_ref, b_ref, gc_ref, t_ref, a_ref, sall_ref, vn_ref, do_ref,
             dq_ref, dk_ref, dv_ref, db_ref, dg_ref, ds_scr):
        @pl.when(pl.program_id(0) == 0)
        def _():
            ds_scr[...] = jnp.zeros_like(ds_scr)

        lastrow = _iota((CHUNK, LANES), 0) == CHUNK - 1
        heads = range(N_GDN_HEADS)
        cols = [slice(hd * LANES, (hd + 1) * LANES) for hd in heads]
        each = lambda fn: [fn(hd) for hd in heads]
        rows_cat = lambda x, y: jnp.concatenate([x, y], axis=0)
        lane_cat = lambda x, y: jnp.concatenate([x, y], axis=1)
        dsp = each(lambda hd: ds_scr[hd])
        for cc in reversed(range(GDN_BLOCK_CHUNKS)):
            rs = slice(cc * CHUNK, (cc + 1) * CHUNK)
            q = each(lambda hd: q_ref[rs, cols[hd]])
            k = each(lambda hd: k_ref[rs, cols[hd]])
            v = each(lambda hd: v_ref[rs, cols[hd]])
            b = each(lambda hd: b_ref[rs, cols[hd]])
            gcc = each(lambda hd: gc_ref[rs, cols[hd]])
            do_c = each(lambda hd: do_ref[rs, cols[hd]])
            vn = each(lambda hd: vn_ref[rs, cols[hd]])
            tn = each(lambda hd: t_ref[hd, cc])
            st = each(lambda hd: sall_ref[hd, cc])
            terms = each(lambda hd: _gdn_chunk_terms(q[hd], k[hd], v[hd], b[hd], gcc[hd]))
            eg, egl, gl, kb, vb, kbg, qd, kd = [[terms[hd][i] for hd in heads] for i in range(8)]
            w = each(lambda hd: _mm(tn[hd], kbg[hd]))
            a_do = each(lambda hd: _mm_tn(a_ref[hd, cc], do_c[hd]))
            kd_ds = each(lambda hd: _mm(kd[hd], dsp[hd]))
            da = each(lambda hd: _mm_nt(do_c[hd], vn[hd]))
            dkd = each(lambda hd: _mm_nt(vn[hd], dsp[hd]))
            by_k = each(lambda hd: _mm_nt(rows_cat(kb[hd], q[hd]), k[hd]))
            dgl = each(lambda hd: jnp.sum(jnp.sum(dsp[hd] * st[hd], axis=-1, keepdims=True), axis=0, keepdims=True))
            dvn = each(lambda hd: a_do[hd] + kd_ds[hd])
            do_dvn = each(lambda hd: rows_cat(do_c[hd], dvn[hd]))
            by_s = each(lambda hd: _mm_nt(do_dvn[hd], st[hd]))
            dqd = each(lambda hd: by_s[hd][:CHUNK])
            dvn_dw = each(lambda hd: lane_cat(dvn[hd], -by_s[hd][CHUNK:]))
            dsp = each(lambda hd: _mm_tn(rows_cat(qd[hd], -w[hd]), do_dvn[hd]) + gl[hd] * dsp[hd])
            dt = each(lambda hd: _mm_nt(dvn_dw[hd], lane_cat(vb[hd], kbg[hd])))
            by_t = each(lambda hd: _mm_tn(tn[hd], dvn_dw[hd]))
            tt_dt = each(lambda hd: _mm_tn(tn[hd], dt[hd]))
            dm_raw = each(lambda hd: _mm_nt(tt_dt[hd], tn[hd]))
            masks = each(lambda hd: _chunk_decay(gcc[hd]))
            dkk = each(lambda hd: jnp.where(masks[hd][1], -dm_raw[hd], 0.0) * masks[hd][0])
            dqk = each(lambda hd: da[hd] * masks[hd][0])
            dqk_dkk = each(lambda hd: rows_cat(dqk[hd], dkk[hd]))
            on_k = each(lambda hd: _mm(dqk_dkk[hd], k[hd]))
            dk_mm = each(lambda hd: _mm_tn(dqk_dkk[hd], rows_cat(q[hd], kb[hd])))
            for hd in heads:
                cs = cols[hd]
                dvb, dkbg = by_t[hd][:, :LANES], by_t[hd][:, LANES:]
                gmat = dkk[hd] * by_k[hd][:CHUNK] + dqk[hd] * by_k[hd][CHUNK:]
                dq_ref[rs, cs] = dqd[hd] * eg[hd] + on_k[hd][:CHUNK]
                dkb = on_k[hd][CHUNK:] + dkbg * eg[hd]
                dk_ref[rs, cs] = dkd[hd] * egl[hd] + dk_mm[hd] + dkb * b[hd]
                db = jnp.sum(dkb * k[hd], axis=-1, keepdims=True) + jnp.sum(dvb * v[hd], axis=-1, keepdims=True)
                db_ref[rs, cs] = jnp.broadcast_to(db, (CHUNK, LANES))
                dv_ref[rs, cs] = dvb * b[hd]
                dkd_kd = jnp.sum(dkd[hd] * kd[hd], axis=-1, keepdims=True)
                col_sums = jnp.sum(lane_cat(gmat, jnp.zeros_like(gmat)).T, axis=-1, keepdims=True)
                dgc = (jnp.sum(gmat, axis=-1, keepdims=True) - col_sums[:CHUNK]
                       + jnp.sum(dqd[hd] * qd[hd], axis=-1, keepdims=True)
                       + jnp.sum(dkbg * kbg[hd], axis=-1, keepdims=True) - dkd_kd)
                extra = jnp.sum(dkd_kd, axis=0, keepdims=True) + dgl[hd] * gl[hd]
                dg_ref[rs, cs] = dgc + jnp.where(lastrow, extra, 0.0)
        for hd in heads:
            ds_scr[hd] = dsp[hd]
        dg = dg_ref[...]
        row = _iota(dg.shape, 0)
        pos = row % CHUNK
        step = 1
        while step < CHUNK:
            dg = dg + jnp.where(pos < CHUNK - step, pltpu.roll(dg, dg.shape[0] - step, 0), 0.0)
            step *= 2
        dg_ref[...] = dg

    nb, tok, mat, state = _gdn_block_specs(t, True)
    return pl.pallas_call(
        body, name="gdn_bwd", grid=(nb,),
        in_specs=[tok] * 5 + [mat, mat, state, tok, tok], out_specs=[tok] * 5, out_shape=[_sds((t, D_GDN))] * 5,
        scratch_shapes=[pltpu.VMEM((N_GDN_HEADS, GDN_HEAD_DIM, GDN_HEAD_DIM), F32)],
        compiler_params=_params("arbitrary"),
    )(qn, kn, cv, be, gc, tinv, amat, s_all, vn_all, do)


def _gdn_bwd_conv(proj, convw, dqn, dkn, dcv):
    t = proj.shape[0]

    def body(xq_ref, xk_ref, xv_ref, wq_ref, wk_ref, wv_ref, dq_ref, dk_ref, dv_ref,
             dxq_ref, dxk_ref, dxv_ref, dwq_ref, dwk_ref, dwv_ref):
        row = _iota((t, LANES), 0)

        def one(x_ref, w_ref, d_ref, dx_ref, dw_ref, scale):
            x = x_ref[...]
            w = w_ref[...]
            y = _conv(x, w, row)
            sg = _sigmoid(y)
            dc = d_ref[...]
            if scale is not None:
                c = y * sg
                r = lax.rsqrt(jnp.sum(c * c, axis=-1, keepdims=True) + EPS)
                ch = c * r
                dc = scale * r * (dc - ch * jnp.sum(dc * ch, axis=-1, keepdims=True))
            dy = dc * sg * (1.0 + y * (1.0 - sg))
            dx_ref[...] = (w[3:4, :] * dy + w[2:3, :] * _shift_up(dy, 1, row) + w[1:2, :] * _shift_up(dy, 2, row)
                           + w[0:1, :] * _shift_up(dy, 3, row))
            for jj in range(CONV_K):
                xs = x if jj == CONV_K - 1 else _shift_down(x, CONV_K - 1 - jj, row)
                dw_ref[jj:jj + 1, :] = jnp.sum(dy * xs, axis=0, keepdims=True)

        one(xq_ref, wq_ref, dq_ref, dxq_ref, dwq_ref, GDN_QSCALE)
        one(xk_ref, wk_ref, dk_ref, dxk_ref, dwk_ref, 1.0)
        one(xv_ref, wv_ref, dv_ref, dxv_ref, dwv_ref, None)

    col, cw, _ = _gdn_specs(t)
    return pl.pallas_call(
        body, name="gdn_bwd_conv", grid=(N_GDN_HEADS,),
        in_specs=[col(12), col(16), col(20), cw(0), cw(4), cw(8), col(0), col(0), col(0)],
        out_specs=[col(0), col(0), col(0), cw(0), cw(0), cw(0)],
        out_shape=[_sds((t, D_GDN))] * 3 + [_sds((CONV_K, D_GDN))] * 3,
        compiler_params=_params("parallel"),
    )(proj, proj, proj, convw, convw, convw, dqn, dkn, dcv)


def _mix_out(fox_n, gdn_o, proj, gnw, w_out, x, pmw, plw):
    t = x.shape[0]
    tm = TOKEN_BLOCK

    def body(fn_ref, go_ref, gz_ref, gnw_ref, w_ref, x_ref, pmw_ref, plw_ref, x1_ref, h2_ref, mixed_ref, omix_ref,
             h2t_ref):
        omix_ref[:, 0:D_FOX] = fn_ref[...]
        for hd in range(N_GDN_HEADS):
            cs = slice(hd * LANES, (hd + 1) * LANES)
            go = go_ref[:, cs]
            r = lax.rsqrt(jnp.mean(go * go, axis=-1, keepdims=True) + EPS)
            gz = gz_ref[:, cs]
            omix_ref[:, D_FOX + hd * LANES:D_FOX + (hd + 1) * LANES] = (
                go * r * gnw_ref[...] * (gz * _sigmoid(gz))).astype(BF)
        mixed = jnp.dot(omix_ref[...], w_ref[...], preferred_element_type=F32)
        mixed_ref[...] = mixed
        r2 = lax.rsqrt(jnp.mean(mixed * mixed, axis=-1, keepdims=True) + EPS)
        x1 = x_ref[...] + mixed * r2 * pmw_ref[...]
        x1_ref[...] = x1
        r3 = lax.rsqrt(jnp.mean(x1 * x1, axis=-1, keepdims=True) + EPS)
        h2 = x1 * r3 * plw_ref[...]
        h2_ref[...] = h2.astype(BF)
        h2t_ref[...] = h2.T.astype(BF)

    tok = lambda w: pl.BlockSpec((tm, w), lambda i: (i, 0))
    vec = lambda w: pl.BlockSpec((1, w), lambda i: (0, 0))
    return pl.pallas_call(
        body, name="mix_out", grid=(t // tm,),
        in_specs=[tok(D_FOX), tok(D_GDN), pl.BlockSpec((tm, D_GDN), lambda i: (i, COL_GZ // D_GDN)), vec(LANES),
                  pl.BlockSpec((D_MODEL, D_MODEL), lambda i: (0, 0)), tok(D_MODEL), vec(D_MODEL), vec(D_MODEL)],
        out_specs=[tok(D_MODEL)] * 4 + [pl.BlockSpec((D_MODEL, tm), lambda i: (0, i))],
        out_shape=[_sds((t, D_MODEL)), _sds((t, D_MODEL), BF), _sds((t, D_MODEL)), _sds((t, D_MODEL), BF),
                   _sds((D_MODEL, t), BF)],
        compiler_params=_params("parallel"),
    )(fox_n, gdn_o, proj, gnw, w_out, x, pmw, plw)


def _out_bwd(dmixed, w_out, o_fox, gdn_o, proj, fnw, gnw):
    t = dmixed.shape[0]
    tm = TOKEN_BLOCK

    def body(dm_ref, w_ref, of_ref, go_ref, gz_ref, fnw_ref, gnw_ref, dof_ref, dgo_ref, dgz_ref, dfw_ref, dgw_ref):
        i = pl.program_id(0)

        @pl.when(i == 0)
        def _():
            dfw_ref[...] = jnp.zeros_like(dfw_ref)
            dgw_ref[...] = jnp.zeros_like(dgw_ref)

        domix = _mm_nt(dm_ref[...], w_ref[...])
        first = _iota((1, LANES), 1) < FOX_HEAD_DIM
        dfw = jnp.zeros((1, LANES), F32)
        dgw = jnp.zeros((1, LANES), F32)
        for pr in range(N_FOX_HEADS // 2):
            cs = slice(pr * LANES, (pr + 1) * LANES)
            o = of_ref[:, cs]
            dfn = domix[:, cs]
            o2 = o * o
            s0 = jnp.sum(jnp.where(first, o2, 0.0), axis=-1, keepdims=True)
            s1 = jnp.sum(jnp.where(first, 0.0, o2), axis=-1, keepdims=True)
            r = lax.rsqrt(jnp.where(first, s0, s1) * (1.0 / FOX_HEAD_DIM) + EPS)
            oh = o * r
            dfw = dfw + jnp.sum(dfn * oh, axis=0, keepdims=True)
            doh = dfn * fnw_ref[...]
            pr_ = doh * oh
            m0 = jnp.sum(jnp.where(first, pr_, 0.0), axis=-1, keepdims=True)
            m1 = jnp.sum(jnp.where(first, 0.0, pr_), axis=-1, keepdims=True)
            dof_ref[:, cs] = r * (doh - oh * jnp.where(first, m0, m1) * (1.0 / FOX_HEAD_DIM))
        for hd in range(N_GDN_HEADS):
            cs = slice(hd * LANES, (hd + 1) * LANES)
            go = go_ref[:, cs]
            gz = gz_ref[:, cs]
            dgated = domix[:, D_FOX + hd * LANES:D_FOX + (hd + 1) * LANES]
            r = lax.rsqrt(jnp.mean(go * go, axis=-1, keepdims=True) + EPS)
            goh = go * r
            sg = _sigmoid(gz)
            sz = gz * sg
            gn = goh * gnw_ref[...]
            dgn = dgated * sz
            dgz_ref[:, cs] = dgated * gn * sg * (1.0 + gz * (1.0 - sg))
            dgw = dgw + jnp.sum(dgn * goh, axis=0, keepdims=True)
            dgh = dgn * gnw_ref[...]
            dgo_ref[:, cs] = r * (dgh - goh * jnp.mean(dgh * goh, axis=-1, keepdims=True))
        dfw_ref[...] += dfw + pltpu.roll(dfw, FOX_HEAD_DIM, 1)
        dgw_ref[...] += dgw

    tok = lambda w: pl.BlockSpec((tm, w), lambda i: (i, 0))
    vec = lambda w: pl.BlockSpec((1, w), lambda i: (0, 0))
    return pl.pallas_call(
        body, name="out_bwd", grid=(t // tm,),
        in_specs=[tok(D_MODEL), pl.BlockSpec((D_MODEL, D_MODEL), lambda i: (0, 0)), tok(D_FOX), tok(D_GDN),
                  pl.BlockSpec((tm, D_GDN), lambda i: (i, COL_GZ // D_GDN)), vec(LANES), vec(LANES)],
        out_specs=[tok(D_FOX), tok(D_GDN), tok(D_GDN), vec(LANES), vec(LANES)],
        out_shape=[_sds((t, D_FOX)), _sds((t, D_GDN)), _sds((t, D_GDN)), _sds((1, LANES)), _sds((1, LANES))],
        compiler_params=_params("arbitrary"),
    )(dmixed, w_out, o_fox, gdn_o, proj, fnw, gnw)


def _mlp_up(h2, w_up):
    t = h2.shape[0]
    tm = TOKEN_BLOCK
    pc = D_FF // N_DEV

    def body(h_ref, w_ref, up_ref):
        h = h_ref[...]
        for p in range(N_DEV):
            up_ref[:, p * pc:(p + 1) * pc] = jnp.dot(h, w_ref[p], preferred_element_type=F32).astype(BF)

    return pl.pallas_call(
        body, name="mlp_up", grid=(t // tm,),
        in_specs=[pl.BlockSpec((tm, D_MODEL), lambda i: (i, 0)),
                  pl.BlockSpec((N_DEV, D_MODEL, pc), lambda i: (0, 0, 0))],
        out_specs=pl.BlockSpec((tm, D_FF), lambda i: (i, 0)), out_shape=_sds((t, D_FF), BF),
        compiler_params=_params("parallel"),
    )(h2, w_up)


def _mlp_down_loss(up, w_down, x1, pw, target):
    t = up.shape[0]
    tm = TOKEN_BLOCK

    def body(up_ref, w_ref, x1_ref, pw_ref, tg_ref, dy_ref, dx2_ref, loss_ref, dpw_ref):
        i = pl.program_id(0)

        @pl.when(i == 0)
        def _():
            loss_ref[...] = jnp.zeros_like(loss_ref)
            dpw_ref[...] = jnp.zeros_like(dpw_ref)

        u = jnp.maximum(up_ref[...].astype(F32), 0.0)
        y = jnp.dot((u * u).astype(BF), w_ref[...], preferred_element_type=F32)
        r = lax.rsqrt(jnp.mean(y * y, axis=-1, keepdims=True) + EPS)
        yh = y * r
        pw = pw_ref[...]
        err = x1_ref[...] + yh * pw - tg_ref[...]
        part = jnp.sum(jnp.sum(err * err, axis=-1, keepdims=True), axis=0, keepdims=True) * (0.5 / D_MODEL)
        loss_ref[...] += jnp.broadcast_to(part, loss_ref.shape)
        dx2 = err * (1.0 / D_MODEL)
        dx2_ref[...] = dx2
        dpw_ref[...] += jnp.sum(dx2 * yh, axis=0, keepdims=True)
        dyh = dx2 * pw
        dy_ref[...] = (r * (dyh - yh * jnp.mean(dyh * yh, axis=-1, keepdims=True))).astype(BF)

    tok = lambda w: pl.BlockSpec((tm, w), lambda i: (i, 0))
    vec = lambda w: pl.BlockSpec((1, w), lambda i: (0, 0))
    return pl.pallas_call(
        body, name="mlp_down_loss", grid=(t // tm,),
        in_specs=[tok(D_FF), pl.BlockSpec((D_FF, D_MODEL), lambda i: (0, 0)), tok(D_MODEL), vec(D_MODEL), tok(D_MODEL)],
        out_specs=[tok(D_MODEL), tok(D_MODEL), vec(LANES), vec(D_MODEL)],
        out_shape=[_sds((t, D_MODEL), BF), _sds((t, D_MODEL)), _sds((1, LANES)), _sds((1, D_MODEL))],
        compiler_params=_params("arbitrary"),
    )(up, w_down, x1, pw, target)


def _mlp_bwd_act(dy, w_down, up):
    t = dy.shape[0]
    tm = TOKEN_BLOCK

    def body(dy_ref, w_ref, up_ref, dup_ref):
        da = lax.dot_general(dy_ref[...], w_ref[...], (((1,), (1,)), ((), ())), preferred_element_type=F32)
        dup_ref[...] = (da * (2.0 * jnp.maximum(up_ref[...].astype(F32), 0.0))).astype(BF)

    return pl.pallas_call(
        body, name="mlp_bwd_act", grid=(t // tm,),
        in_specs=[pl.BlockSpec((tm, D_MODEL), lambda i: (i, 0)), pl.BlockSpec((D_FF, D_MODEL), lambda i: (0, 0)),
                  pl.BlockSpec((tm, D_FF), lambda i: (i, 0))],
        out_specs=pl.BlockSpec((tm, D_FF), lambda i: (i, 0)), out_shape=_sds((t, D_FF), BF),
        compiler_params=_params("parallel"),
    )(dy, w_down, up)


def _mlp_bwd_in(dup, w_up, x1, plw, dx2, mixed, pmw):
    t = dup.shape[0]
    tm = TOKEN_BLOCK

    def body(dup_ref, w_ref, x1_ref, plw_ref, dx2_ref, mx_ref, pmw_ref, dx1_ref, dmixed_ref, dplw_ref, dpmw_ref):
        i = pl.program_id(0)

        @pl.when(i == 0)
        def _():
            dplw_ref[...] = jnp.zeros_like(dplw_ref)
            dpmw_ref[...] = jnp.zeros_like(dpmw_ref)

        pc = D_FF // N_DEV
        dh = _mm_nt(dup_ref[:, 0:pc], w_ref[0])
        for p in range(1, N_DEV):
            dh = dh + _mm_nt(dup_ref[:, p * pc:(p + 1) * pc], w_ref[p])
        x1 = x1_ref[...]
        r = lax.rsqrt(jnp.mean(x1 * x1, axis=-1, keepdims=True) + EPS)
        xh = x1 * r
        dplw_ref[...] += jnp.sum(dh * xh, axis=0, keepdims=True)
        dxh = dh * plw_ref[...]
        dx1 = dx2_ref[...] + r * (dxh - xh * jnp.mean(dxh * xh, axis=-1, keepdims=True))
        dx1_ref[...] = dx1
        mx = mx_ref[...]
        r2 = lax.rsqrt(jnp.mean(mx * mx, axis=-1, keepdims=True) + EPS)
        mh = mx * r2
        dpmw_ref[...] += jnp.sum(dx1 * mh, axis=0, keepdims=True)
        dmh = dx1 * pmw_ref[...]
        dmixed_ref[...] = (r2 * (dmh - mh * jnp.mean(dmh * mh, axis=-1, keepdims=True))).astype(BF)

    tok = lambda w: pl.BlockSpec((tm, w), lambda i: (i, 0))
    vec = lambda w: pl.BlockSpec((1, w), lambda i: (0, 0))
    return pl.pallas_call(
        body, name="mlp_bwd_in", grid=(t // tm,),
        in_specs=[tok(D_FF), pl.BlockSpec((N_DEV, D_MODEL, D_FF // N_DEV), lambda i: (0, 0, 0)), tok(D_MODEL),
                  vec(D_MODEL), tok(D_MODEL), tok(D_MODEL), vec(D_MODEL)],
        out_specs=[tok(D_MODEL), tok(D_MODEL), vec(D_MODEL), vec(D_MODEL)],
        out_shape=[_sds((t, D_MODEL)), _sds((t, D_MODEL), BF), _sds((1, D_MODEL)), _sds((1, D_MODEL))],
        compiler_params=_params("arbitrary"),
    )(dup, w_up, x1, plw, dx2, mixed, pmw)


def _wgrad(a, b, a_cols, split=1, a_fn=None, a_block0=0, name="wgrad"):
    t, b_cols = b.shape
    n_a = (a.shape[1] - a_block0 * a_cols) // a_cols if a_block0 else a.shape[1] // a_cols

    def body(a_ref, b_ref, o_ref):
        av = a_ref[...]
        if a_fn is not None:
            av = a_fn(av)
        o_ref[...] = _mm_tn(av, b_ref[...]).astype(BF).reshape(o_ref.shape)

    return pl.pallas_call(
        body, name=name, grid=(n_a,),
        in_specs=[pl.BlockSpec((t, a_cols), lambda i: (0, i + a_block0)), pl.BlockSpec((t, b_cols), lambda i: (0, 0))],
        out_specs=pl.BlockSpec((split, a_cols // split, b_cols), lambda i: (i, 0, 0)),
        out_shape=_sds((n_a * split, a_cols // split, b_cols), BF),
        compiler_params=_params("parallel"),
    )(a, b)


def _wgrad_pre_t(at, b, b_cols, name):
    rows, t = at.shape
    n_b = b.shape[1] // b_cols

    def body(a_ref, b_ref, o_ref):
        o_ref[0] = jnp.dot(a_ref[...], b_ref[...], preferred_element_type=F32).astype(BF)

    return pl.pallas_call(
        body, name=name, grid=(n_b,),
        in_specs=[pl.BlockSpec((rows, t), lambda j: (0, 0)), pl.BlockSpec((t, b_cols), lambda j: (0, j))],
        out_specs=pl.BlockSpec((1, rows, b_cols), lambda j: (j, 0, 0)), out_shape=_sds((n_b, rows, b_cols), BF),
        compiler_params=_params("parallel"),
    )(at, b)


def _select_matrix(rows, fn):
    r = _iota((rows, LANES), 0)
    c = _iota((rows, LANES), 1)
    return (r == fn(c)).astype(F32)


def _small_bwd(proj, fb, al, dtb, dcq, dckt, dbe, dge):
    t = proj.shape[0]

    def body(sm_ref, fb_ref, al_ref, dtb_ref, dcq_ref, dckt_ref, dbe_ref, dge_ref, dsm_ref, dvec_ref):
        s = sm_ref[...]
        lane = _iota((1, LANES), 1)
        sel_f = _select_matrix(512, lambda c: jnp.where(c < 8, FOX_HEAD_DIM * c, -1))
        sel_k = _select_matrix(LANES, lambda c: jnp.where(c < 8, 32 * (c // 2) + c % 2, -1))
        dcum = _mm_exact(dcq_ref[...], sel_f) - _mm_exact(dckt_ref[...].T, sel_k)
        row = _iota((t, LANES), 0)
        step = 1
        while step < t:
            dcum = dcum + _shift_up(dcum, step, row)
            step *= 2
        dff = dcum * _sigmoid(-(s + fb_ref[...]))
        sel_b = _select_matrix(512, lambda c: jnp.where((c >= SM_GB) & (c < SM_GA), LANES * (c - SM_GB), -1))
        sel_g = _select_matrix(512, lambda c: jnp.where((c >= SM_GA) & (c < SM_GA + 4), LANES * (c - SM_GA), -1))
        beta = _sigmoid(s)
        dgb = _mm_exact(dbe_ref[...], sel_b) * beta * (1.0 - beta)
        dg = _mm_exact(dge_ref[...], sel_g)
        za = s + dtb_ref[...]
        nea = -jnp.exp(al_ref[...])
        dga = dg * nea * _sigmoid(za)
        is_f = lane < SM_GB
        is_b = (lane >= SM_GB) & (lane < SM_GA)
        is_a = (lane >= SM_GA) & (lane < SM_GA + 4)
        dsm_ref[...] = jnp.where(is_f, dff, jnp.where(is_b, dgb, jnp.where(is_a, dga, 0.0)))
        dvec_ref[...] = jnp.zeros_like(dvec_ref)
        dvec_ref[0:1, :] = jnp.sum(jnp.where(is_f, dff, 0.0), axis=0, keepdims=True)
        dvec_ref[1:2, :] = jnp.sum(jnp.where(is_a, dg * nea * _softplus(za), 0.0), axis=0, keepdims=True)
        dvec_ref[2:3, :] = jnp.sum(jnp.where(is_a, dga, 0.0), axis=0, keepdims=True)

    vec = pl.BlockSpec((1, LANES), lambda i: (0, 0))
    full = lambda r, c: pl.BlockSpec((r, c), lambda i: (0, 0))
    return pl.pallas_call(
        body, name="small_bwd", grid=(1,),
        in_specs=[pl.BlockSpec((t, LANES), lambda i: (0, COL_SMALL // LANES)), vec, vec, vec, full(t, 512),
                  full(LANES, t), full(t, 512), full(t, 512)],
        out_specs=[full(t, LANES), full(8, LANES)], out_shape=[_sds((t, LANES)), _sds((8, LANES))],
        compiler_params=_params("arbitrary"),
    )(proj, fb, al, dtb, dcq, dckt, dbe, dge)


def _pack_dproj(dfox, dgdn, dgz, dsm):
    t = dgz.shape[0]
    tm = TOKEN_BLOCK

    def body(*refs):
        parts, dp_ref = refs[:8], refs[8]
        col = 0
        for part in parts:
            width = part.shape[1]
            dp_ref[:, col:col + width] = part[...].astype(BF)
            col += width

    tok = lambda w: pl.BlockSpec((tm, w), lambda i: (i, 0))
    return pl.pallas_call(
        body, name="pack_dproj", grid=(t // tm,), in_specs=[tok(D_FOX)] * 3 + [tok(D_GDN)] * 4 + [tok(LANES)],
        out_specs=tok(PROJ_W), out_shape=_sds((t, PROJ_W), BF), compiler_params=_params("parallel"),
    )(*dfox, *dgdn, dgz, dsm)


def _in_bwd(dproj, wt_al, x, nw, dx1):
    t = x.shape[0]
    tm = TOKEN_BLOCK

    def body(dp_ref, w_ref, x_ref, nw_ref, dx1_ref, dx_ref, dnw_ref):
        i = pl.program_id(0)

        @pl.when(i == 0)
        def _():
            dnw_ref[...] = jnp.zeros_like(dnw_ref)

        dh = jnp.dot(dp_ref[...], w_ref[...], preferred_element_type=F32)
        xv = x_ref[...]
        r = lax.rsqrt(jnp.mean(xv * xv, axis=-1, keepdims=True) + EPS)
        xh = xv * r
        dnw_ref[...] += jnp.sum(dh * xh, axis=0, keepdims=True)
        dxh = dh * nw_ref[...]
        dx_ref[...] = dx1_ref[...] + r * (dxh - xh * jnp.mean(dxh * xh, axis=-1, keepdims=True))

    tok = lambda w: pl.BlockSpec((tm, w), lambda i: (i, 0))
    vec = lambda w: pl.BlockSpec((1, w), lambda i: (0, 0))
    return pl.pallas_call(
        body, name="in_bwd", grid=(t // tm,),
        in_specs=[tok(PROJ_W), pl.BlockSpec((PROJ_W, D_MODEL), lambda i: (0, 0)), tok(D_MODEL), vec(D_MODEL),
                  tok(D_MODEL)],
        out_specs=[tok(D_MODEL), vec(D_MODEL)], out_shape=[_sds((t, D_MODEL)), _sds((1, D_MODEL))],
        compiler_params=_params("arbitrary"),
    )(dproj, wt_al, x, nw, dx1)


def _row(v, width=None):
    v = v.reshape(1, -1).astype(F32)
    if width is not None and v.shape[1] < width:
        v = jnp.pad(v, ((0, 0), (0, width - v.shape[1])))
    return v


def _lane_vec(v, first):
    return jnp.zeros((1, LANES), F32).at[0, first:first + v.shape[0]].set(v.astype(F32))


def _local_step(x, target, wt_al, late_weights, on_grads, convw, pre_mix_norm, fox_f_bias, fox_out_norm,
                gdn_a_log, gdn_dt_bias, gdn_out_norm, post_mix_norm, pre_mlp_norm, post_mlp_norm):
    t = x.shape[0]
    nch = t // CHUNK
    nw, pmw, plw, pw = _row(pre_mix_norm), _row(post_mix_norm), _row(pre_mlp_norm), _row(post_mlp_norm)
    fb, al, dtb = _lane_vec(fox_f_bias, SM_FF), _lane_vec(gdn_a_log, SM_GA), _lane_vec(gdn_dt_bias, SM_GA)
    fnw = _row(jnp.tile(fox_out_norm, 2))
    gnw = _row(gdn_out_norm)

    proj, h = _norm_proj(x, nw, wt_al)
    ce, cumt, be, ge = _small_prep(proj, fb, al, dtb)
    o_fox, lse, fox_n = _fox_fwd(proj, ce, cumt, fnw)
    qn, kn, cv, gc, mmat, amat = _gdn_prep(proj, convw, be, ge)
    n_prob = N_GDN_HEADS * nch
    m3 = mmat.reshape(n_prob, CHUNK, CHUNK)
    if n_prob < LANES:
        m3 = jnp.pad(m3, ((0, LANES - n_prob), (0, 0), (0, 0)))
    tinv = _tri_inverse(m3)[:n_prob].reshape(N_GDN_HEADS, nch, CHUNK, CHUNK)
    gdn_o, s_all, vn_all = _gdn_scan(qn, kn, cv, be, gc, tinv, amat)
    w_out = late_weights("w_out", gdn_o)
    x1, h2, mixed, omix, h2t = _mix_out(fox_n, gdn_o, proj, gnw, w_out, x, pmw, plw)
    w_up, w_down = late_weights("mlp", h2)
    up = _mlp_up(h2, w_up)
    dy, dx2, loss, d_pw = _mlp_down_loss(up, w_down, x1, pw, target)

    dup = _mlp_bwd_act(dy, w_down, up)
    relu2 = lambda u: jnp.square(jnp.maximum(u.astype(F32), 0.0))
    g_down = _wgrad(up, dy, D_FF // N_DEV, a_fn=relu2, name="wgrad_down")
    g_up = _wgrad_pre_t(h2t, dup, D_FF // N_DEV, name="wgrad_up")
    token = on_grads("mlp", (g_up, g_down))
    dx1, dmixed, d_plw, d_pmw = _mlp_bwd_in(dup, w_up, x1, plw + token[0:1, 0:1], dx2, mixed, pmw)
    token = on_grads("w_out", _wgrad(omix, dmixed, 512, split=4, name="wgrad_out"))
    do_fox, dgo, dgz, d_fnw, d_gnw = _out_bwd(dmixed, w_out, o_fox, gdn_o, proj, fnw + token[0:1, 0:1], gnw)
    dfq, dfk, dfv, dcq, dckt = _fox_bwd(proj, ce, cumt, lse, o_fox, do_fox)
    dqn, dkn, dcv, dbe, dge = _gdn_bwd(qn, kn, cv, be, gc, tinv, amat, s_all, vn_all, dgo)
    dxq, dxk, dxv, dwq, dwk, dwv = _gdn_bwd_conv(proj, convw, dqn, dkn, dcv)
    dsm, dvec = _small_bwd(proj, fb, al, dtb, dcq, dckt, dbe, dge)
    dproj = _pack_dproj((dfq, dfk, dfv), (dxq, dxk, dxv), dgz, dsm)
    g_main = _wgrad(dproj, h, 512, name="wgrad_in")
    g_tail = _wgrad(dproj, h, LANES, a_block0=COL_SMALL // LANES, name="wgrad_in_small")
    token = on_grads("w_in", jnp.concatenate([g_main.reshape(COL_SMALL, D_MODEL), g_tail[0]]))
    grad_x, d_nw = _in_bwd(dproj, wt_al, x, nw + token[0:1, 0:1], dx1)
    small = dict(norms=(d_nw, d_pmw, d_plw, d_pw), fox_out_norm=d_fnw, gdn_out_norm=d_gnw, loss=loss, vectors=dvec,
                 conv=(dwq, dwk, dwv))
    return grad_x, small


MESH_IDS = pl.DeviceIdType.MESH
CHIP_FLIPS = ((0, 0), (1, 0), (0, 1), (1, 1))
ANY_SPEC = pl.BlockSpec(memory_space=pl.ANY)


def _place():
    return lax.axis_index("x"), lax.axis_index("y"), lax.axis_index("c")


def _all_gather(blocks):
    n = len(blocks)

    def body(*refs):
        ins, outs, (send_sems, recv_sems, local_sems) = refs[:n], refs[n:2 * n], refs[2 * n:]
        x, y, c = _place()
        sibling = (x, y, 1 - c)
        chips = [(x ^ fx, y ^ fy) for fx, fy in CHIP_FLIPS[1:]]

        def slot(out, px, py, pc):
            return out.at[4 * px + 2 * py + pc]

        def copy(a, k, block, to, src=None):
            return pltpu.make_async_remote_copy(
                src_ref=slot(outs[a], *block) if src is None else src, dst_ref=slot(outs[a], *block),
                send_sem=send_sems.at[a, k], recv_sem=recv_sems.at[a, k], device_id=to, device_id_type=MESH_IDS)

        pending = []
        for a in range(n):
            mine = pltpu.make_async_copy(ins[a], slot(outs[a], x, y, c), local_sems.at[a])
            mine.start()
            pending.append(mine)
        sends = []
        for a in range(n):
            first = [copy(a, 0, (x, y, c), sibling, src=ins[a])]
            first += [copy(a, 1 + j, (x, y, c), (*chip, c), src=ins[a]) for j, chip in enumerate(chips)]
            for cp in first:
                cp.start()
            sends += first
        for a in range(n):
            for j, chip in enumerate(chips):
                copy(a, 1 + j, (*chip, c), (x, y, c)).wait_recv()
                fwd = copy(a, 4 + j, (*chip, c), sibling)
                fwd.start()
                sends.append(fwd)
        for a in range(n):
            copy(a, 0, sibling, (x, y, c)).wait_recv()
            for j, chip in enumerate(chips):
                copy(a, 4 + j, (*chip, 1 - c), (x, y, c)).wait_recv()
        for cp in sends:
            cp.wait_send()
        for cp in pending:
            cp.wait()

    return pl.pallas_call(
        body, name="all_gather_weights", in_specs=[ANY_SPEC] * n, out_specs=[ANY_SPEC] * n,
        out_shape=[_sds((N_DEV,) + b.shape, b.dtype) for b in blocks],
        scratch_shapes=[pltpu.SemaphoreType.DMA((n, 7)), pltpu.SemaphoreType.DMA((n, 7)), pltpu.SemaphoreType.DMA((n,))],
        compiler_params=pltpu.CompilerParams(has_side_effects=True),
    )(*blocks)


def _adamw(w, g, m, v):
    m = ADAM_B1 * m + (1.0 - ADAM_B1) * g
    v = ADAM_B2 * v + (1.0 - ADAM_B2) * (g * g)
    m_hat = m / (1.0 - ADAM_B1 ** ADAM_STEP)
    v_hat = v / (1.0 - ADAM_B2 ** ADAM_STEP)
    return -ADAM_LR * (m_hat / (jnp.sqrt(v_hat) + ADAM_EPS) + ADAM_WD * w), m, v


HBM_SPEC = pl.BlockSpec(memory_space=pltpu.HBM)
SEM_SPEC = pl.BlockSpec(memory_space=pltpu.SEMAPHORE)
DATAFLOW = pltpu.SideEffectType.DATAFLOW_SIDE_EFFECTING


def _peers():
    x, y, c = _place()
    return 4 * x + 2 * y + c, [(x ^ (k >> 2), y ^ ((k >> 1) & 1), c ^ (k & 1)) for k in range(1, N_DEV)]


def _peer_index(peer):
    return 4 * peer[0] + 2 * peer[1] + peer[2]


def _zones_with_own(srcs, pieces, name, after=None, dtype=None):
    n = len(srcs)
    extra = [] if after is None else [after]
    dtypes = [s_.dtype if pieces or dtype is None else dtype for s_ in srcs]

    def body(me_ref, *refs):
        outs = refs[n + len(extra):]
        for a in range(n):
            if pieces:
                outs[a][0] = refs[a][0]
            else:
                val = refs[a][...].astype(dtypes[a])
                outs[a][0] = val
                outs[n + a][...] = val

    shapes = [s_.shape[1:] if pieces else s_.shape for s_ in srcs]
    mine = lambda sh: pl.BlockSpec((1,) + sh, lambda i, me_ref: (me_ref[0], 0, 0))
    whole = lambda sh: pl.BlockSpec(sh, lambda i, me_ref: (0, 0))
    in_specs = [mine(sh) if pieces else whole(sh) for sh in shapes]
    out_specs = [mine(sh) for sh in shapes] + ([] if pieces else [whole(sh) for sh in shapes])
    out_shape = [_sds((N_DEV,) + sh, dt) for sh, dt in zip(shapes, dtypes)]
    out_shape += [] if pieces else [_sds(sh, dt) for sh, dt in zip(shapes, dtypes)]
    x, y, c = _place()
    out = pl.pallas_call(
        body, name=name,
        grid_spec=pltpu.PrefetchScalarGridSpec(num_scalar_prefetch=1, grid=(1,), in_specs=in_specs + [ANY_SPEC] * len(extra),
                                               out_specs=out_specs),
        out_shape=out_shape, compiler_params=_params("arbitrary"),
    )((4 * x + 2 * y + c).astype(jnp.int32).reshape(1), *srcs, *extra)
    return out[:n], (list(srcs) if pieces else out[n:])


def _exchange_start(srcs, zones, pieces, name):
    n = len(srcs)

    def body(*refs):
        ins, zs = refs[:n], refs[n:2 * n]
        sems = refs[2 * n:4 * n]
        token = refs[-1]
        me, peers = _peers()
        for peer in peers:
            for a in range(n):
                pltpu.make_async_remote_copy(
                    src_ref=ins[a].at[_peer_index(peer)] if pieces else ins[a], dst_ref=zs[a].at[me],
                    send_sem=sems[2 * a], recv_sem=sems[2 * a + 1], device_id=peer, device_id_type=MESH_IDS).start()
        token[...] = jnp.zeros_like(token)

    hbm = lambda v: pltpu.with_memory_space_constraint(v, pltpu.HBM)
    out = pl.pallas_call(
        body, name=name,
        out_shape=tuple([pltpu.SemaphoreType.DMA(())] * (2 * n) + [pltpu.HBM(v.shape, v.dtype) for v in srcs]
                        + [pltpu.HBM(z.shape, z.dtype) for z in zones] + [_sds((8, LANES))]),
        in_specs=[HBM_SPEC] * (2 * n), out_specs=tuple([SEM_SPEC] * (2 * n) + [HBM_SPEC] * (2 * n) + [VMEM_SPEC]),
        input_output_aliases={i: 2 * n + i for i in range(2 * n)},
        compiler_params=pltpu.CompilerParams(has_side_effects=DATAFLOW),
    )(*[hbm(v) for v in srcs], *[hbm(z) for z in zones])
    return out[:2 * n], out[2 * n:3 * n], out[3 * n:4 * n], out[-1]


def _exchange_wait(sems, srcs, zones, after, name):
    n = len(srcs)
    after = list(after) if isinstance(after, (list, tuple)) else [after]

    def body(*refs):
        ins, zs, sm = refs[:n], refs[n:2 * n], refs[2 * n:4 * n]
        me, peers = _peers()
        for a in range(n):
            seven = zs[a].at[pl.ds(0, N_DEV - 1)]
            cp = pltpu.make_async_remote_copy(src_ref=seven, dst_ref=seven, send_sem=sm[2 * a], recv_sem=sm[2 * a + 1],
                                              device_id=peers[0], device_id_type=MESH_IDS)
            cp.wait_send()
            cp.wait_recv()

    out = pl.pallas_call(
        body, name=name, out_shape=tuple([pltpu.HBM(v.shape, v.dtype) for v in srcs] + [pltpu.HBM(z.shape, z.dtype) for z in zones]),
        in_specs=[HBM_SPEC] * (2 * n) + [SEM_SPEC] * (2 * n) + [ANY_SPEC] * len(after),
        out_specs=tuple([HBM_SPEC] * (2 * n)), input_output_aliases={i: i for i in range(2 * n)},
        compiler_params=pltpu.CompilerParams(has_side_effects=DATAFLOW),
    )(*srcs, *zones, *sems, *after)
    return out[n:]


def _sum_adamw(zone, w, m, v, name):
    _, r, c_ = zone.shape
    rb = 128 if r % 128 == 0 else r

    def body(z_ref, w_ref, m_ref, v_ref, grad_ref, delta_ref, nm_ref, nv_ref):
        total = z_ref[0].astype(F32)
        for d in range(1, N_DEV):
            total = total + z_ref[d].astype(F32)
        grad_ref[...] = total
        delta_ref[...], nm_ref[...], nv_ref[...] = _adamw(w_ref[...], total, m_ref[...], v_ref[...])

    blk = pl.BlockSpec((rb, c_), lambda i: (i, 0))
    return pl.pallas_call(
        body, name=name, grid=(r // rb,), in_specs=[pl.BlockSpec((N_DEV, rb, c_), lambda i: (0, i, 0)), blk, blk, blk],
        out_specs=[blk] * 4, out_shape=[_sds((r, c_))] * 4, compiler_params=_params("parallel"),
    )(zone, w, m, v)


SMALL_NORMS = ("pre_mix_norm", "post_mix_norm", "pre_mlp_norm", "post_mlp_norm")
SMALL_ORDER = SMALL_NORMS + ("fox_out_norm", "gdn_out_norm", "fox_f_bias", "gdn_a_log", "gdn_dt_bias", "gdn_conv_w")
CONV_SLAB_ROWS, CONV_SLAB_LANES = 8, 256


def _small_pack(small):
    def body(n0, n1, n2, n3, fnw_ref, gnw_ref, loss_ref, vec_ref, out_ref):
        out_ref[...] = jnp.zeros_like(out_ref)
        for i, ref in enumerate((n0, n1, n2, n3)):
            out_ref[i:i + 1, :] = ref[...]
        out_ref[4:5, 0:LANES] = fnw_ref[...]
        out_ref[4:5, LANES:2 * LANES] = gnw_ref[...]
        out_ref[4:5, 2 * LANES:3 * LANES] = loss_ref[...]
        out_ref[5:8, 0:LANES] = vec_ref[0:3, :]

    return pl.pallas_call(body, name="small_pack", in_specs=[VMEM_SPEC] * 8, out_specs=VMEM_SPEC,
                          out_shape=_sds((8, D_MODEL)))(*small["norms"], small["fox_out_norm"], small["gdn_out_norm"],
                                                        small["loss"], small["vectors"])


def _conv_slabs(dconv):
    blocks = dconv.reshape(CONV_K, N_DEV, -1).transpose(1, 0, 2)
    blocks = jnp.pad(blocks, ((0, 0), (0, CONV_SLAB_ROWS - CONV_K), (0, CONV_SLAB_LANES - blocks.shape[2])))
    return blocks.reshape(N_DEV * CONV_SLAB_ROWS, CONV_SLAB_LANES)


def _small_update(zone, conv_zone, w, m, v):
    n = len(SMALL_ORDER)
    n_conv = w["gdn_conv_w"].shape[1]

    def body(me_ref, z_ref, zc_ref, *refs):
        params, loss_ref, outs, (tot, totc) = refs[:3 * n], refs[3 * n], refs[3 * n + 1:7 * n + 1], refs[-2:]
        total, total_c = z_ref[0], zc_ref[0]
        for d in range(1, N_DEV):
            total, total_c = total + z_ref[d], total_c + zc_ref[d]
        tot[...] = total
        totc[...] = total_c
        loss_ref[...] = tot[4, 2 * LANES:2 * LANES + 1]
        mine = totc[pl.ds(pl.multiple_of(me_ref[0] * CONV_SLAB_ROWS, CONV_SLAB_ROWS), CONV_SLAB_ROWS), :]
        g = dict(zip(SMALL_NORMS, (tot[0], tot[1], tot[2], tot[3])))
        g.update(fox_out_norm=tot[4, 0:FOX_HEAD_DIM], gdn_out_norm=tot[4, LANES:LANES + GDN_HEAD_DIM],
                 fox_f_bias=tot[5, SM_FF:SM_FF + N_FOX_HEADS], gdn_a_log=tot[6, SM_GA:SM_GA + N_GDN_HEADS],
                 gdn_dt_bias=tot[7, SM_GA:SM_GA + N_GDN_HEADS], gdn_conv_w=mine[0:CONV_K, 0:n_conv])
        for i, name in enumerate(SMALL_ORDER):
            w_ref, m_ref, v_ref = params[3 * i:3 * i + 3]
            outs[4 * i][...] = g[name]
            outs[4 * i + 1][...], outs[4 * i + 2][...], outs[4 * i + 3][...] = _adamw(w_ref[...], g[name], m_ref[...],
                                                                                     v_ref[...])

    x, y, c = _place()
    operands = [a[name] for name in SMALL_ORDER for a in (w, m, v)]
    out = pl.pallas_call(
        body, name="small_update",
        in_specs=[pl.BlockSpec(memory_space=pltpu.SMEM)] + [VMEM_SPEC] * (2 + 3 * n), out_specs=[VMEM_SPEC] * (1 + 4 * n),
        out_shape=[_sds((1,))] + [_sds(w[name].shape) for name in SMALL_ORDER for _ in range(4)],
        scratch_shapes=[pltpu.VMEM(zone.shape[1:], F32), pltpu.VMEM(conv_zone.shape[1:], F32)],
    )((4 * x + 2 * y + c).astype(jnp.int32).reshape(1), zone, conv_zone, *operands)
    return out[0][0], {name: out[1 + 4 * i:5 + 4 * i] for i, name in enumerate(SMALL_ORDER)}


NATIVE_ROWS = ((0, 1536), (1544, 3080), (3088, 3600), (1536, 1544), (3080, 3088))


def _to_aligned_rows(wt_native):
    pad = jnp.zeros((PROJ_W - D_PROJ, wt_native.shape[1]), wt_native.dtype)
    return jnp.concatenate([wt_native[lo:hi] for lo, hi in NATIVE_ROWS] + [pad])


def _from_aligned_rows(gt_al):
    return jnp.concatenate([gt_al[0:1536], gt_al[3584:3592], gt_al[1536:3072], gt_al[3592:3600], gt_al[3072:3584]])


def _cols_from_pieces(p):
    return p.transpose(1, 0, 2).reshape(p.shape[1], -1)


WEIGHT_ORDER = ("pre_mix_norm", "w_in", "fox_f_bias", "fox_out_norm", "gdn_conv_w", "gdn_a_log", "gdn_dt_bias",
                "gdn_out_norm", "w_out", "post_mix_norm", "pre_mlp_norm", "w_up", "w_down", "post_mlp_norm")


def kernel(x, pre_mix_norm, w_in, fox_f_bias, fox_out_norm, gdn_conv_w, gdn_a_log, gdn_dt_bias, gdn_out_norm, w_out, post_mix_norm, pre_mlp_norm, w_up, w_down, post_mlp_norm, loss_target, m_pre_mix_norm, m_w_in, m_fox_f_bias, m_fox_out_norm, m_gdn_conv_w, m_gdn_a_log, m_gdn_dt_bias, m_gdn_out_norm, m_w_out, m_post_mix_norm, m_pre_mlp_norm, m_w_up, m_w_down, m_post_mlp_norm, v_pre_mix_norm, v_w_in, v_fox_f_bias, v_fox_out_norm, v_gdn_conv_w, v_gdn_a_log, v_gdn_dt_bias, v_gdn_out_norm, v_w_out, v_post_mix_norm, v_pre_mlp_norm, v_w_up, v_w_down, v_post_mlp_norm):
    w = dict(pre_mix_norm=pre_mix_norm, w_in=w_in, fox_f_bias=fox_f_bias, fox_out_norm=fox_out_norm,
             gdn_conv_w=gdn_conv_w, gdn_a_log=gdn_a_log, gdn_dt_bias=gdn_dt_bias, gdn_out_norm=gdn_out_norm, w_out=w_out,
             post_mix_norm=post_mix_norm, pre_mlp_norm=pre_mlp_norm, w_up=w_up, w_down=w_down, post_mlp_norm=post_mlp_norm)
    mom = dict(pre_mix_norm=m_pre_mix_norm, w_in=m_w_in, fox_f_bias=m_fox_f_bias, fox_out_norm=m_fox_out_norm,
               gdn_conv_w=m_gdn_conv_w, gdn_a_log=m_gdn_a_log, gdn_dt_bias=m_gdn_dt_bias, gdn_out_norm=m_gdn_out_norm,
               w_out=m_w_out, post_mix_norm=m_post_mix_norm, pre_mlp_norm=m_pre_mlp_norm, w_up=m_w_up, w_down=m_w_down,
               post_mlp_norm=m_post_mlp_norm)
    var = dict(pre_mix_norm=v_pre_mix_norm, w_in=v_w_in, fox_f_bias=v_fox_f_bias, fox_out_norm=v_fox_out_norm,
               gdn_conv_w=v_gdn_conv_w, gdn_a_log=v_gdn_a_log, gdn_dt_bias=v_gdn_dt_bias, gdn_out_norm=v_gdn_out_norm,
               w_out=v_w_out, post_mix_norm=v_post_mix_norm, pre_mlp_norm=v_pre_mlp_norm, w_up=v_w_up, w_down=v_w_down,
               post_mlp_norm=v_post_mlp_norm)

    win_g, conv_g = _all_gather([w_in.T.astype(BF), gdn_conv_w])
    wt_al = _to_aligned_rows(win_g.reshape(D_PROJ, D_MODEL))
    convw = _cols_from_pieces(conv_g)
    gathers, after = {}, win_g
    for name, shards in (("w_out", [w_out]), ("mlp", [w_up, w_down])):
        zones, shards = _zones_with_own(shards, False, "gather_" + name + "_own", after=after, dtype=BF)
        gathers[name] = _exchange_start(shards, zones, False, "gather_" + name + "_start")
        after = gathers[name][3]

    def late_weights(name, after):
        sems, shards, zones, _ = gathers[name]
        got = _exchange_wait(sems, shards, zones, after, "gather_" + name + "_wait")
        if name == "w_out":
            return got[0].reshape(D_MODEL, D_MODEL)
        return got[0], got[1].reshape(D_FF, D_MODEL)

    scatters = {}

    def on_grads(name, g):
        if name == "w_in":
            g = _from_aligned_rows(g).reshape(N_DEV, D_PROJ // N_DEV, D_MODEL)
        srcs = list(g) if name == "mlp" else [g]
        zones, _ = _zones_with_own(srcs, True, "scatter_" + name + "_own")
        scatters[name] = _exchange_start(srcs, zones, True, "scatter_" + name + "_start")
        return scatters[name][3]

    grad_x, small = _local_step(
        x[0], loss_target[0], wt_al, late_weights, on_grads, convw, pre_mix_norm + after[0, 0],
        fox_f_bias, fox_out_norm, gdn_a_log, gdn_dt_bias, gdn_out_norm, post_mix_norm, pre_mlp_norm, post_mlp_norm)
    slabs = [_small_pack(small), _conv_slabs(jnp.concatenate(small["conv"], axis=1))]
    zones, slabs = _zones_with_own(slabs, False, "small_own")
    scatters["small"] = _exchange_start(slabs, zones, False, "small_start")

    grads, delta, new_m, new_v = {}, {}, {}, {}
    after = scatters["small"][3]
    for name, members in (("mlp", ("w_up", "w_down")), ("w_out", ("w_out",)), ("small", ()), ("w_in", ("w_in",))):
        sems, srcs, zones, _ = scatters[name]
        zones = _exchange_wait(sems, srcs, zones, after, "scatter_" + name + "_wait")
        if name == "small":
            loss, updated = _small_update(zones[0], zones[1], w, mom, var)
            for n, res in updated.items():
                grads[n], delta[n], new_m[n], new_v[n] = res
            after = grads["pre_mix_norm"]
        for n, zone in zip(members, zones):
            if n == "w_in":
                res = _sum_adamw(zone, w[n].T, mom[n].T, var[n].T, "adamw_" + n)
                grads[n], delta[n], new_m[n], new_v[n] = [r.T for r in res]
            else:
                grads[n], delta[n], new_m[n], new_v[n] = _sum_adamw(zone, w[n], mom[n], var[n], "adamw_" + n)
        if members:
            after = [grads[n] for n in members]

    return (loss, grad_x[None], *[grads[n] for n in WEIGHT_ORDER], *[delta[n] for n in WEIGHT_ORDER],
            *[new_m[n] for n in WEIGHT_ORDER], *[new_v[n] for n in WEIGHT_ORDER])
```

```python
import jax
import jax.numpy as jnp
from jax import lax
from jax.experimental import pallas as pl
from jax.experimental.pallas import tpu as pltpu

F32 = jnp.float32
BF = jnp.bfloat16

D_MODEL = 1024
N_FOX_HEADS, FOX_HEAD_DIM = 8, 64
N_GDN_HEADS, GDN_HEAD_DIM = 4, 128
D_FOX = N_FOX_HEADS * FOX_HEAD_DIM
D_GDN = N_GDN_HEADS * GDN_HEAD_DIM
CHUNK = 64
CONV_K = 4
D_FF = 4 * D_MODEL
EPS = 1e-6
D_PROJ = 3600
N_DEV = 8

PROJ_W = 3712
COL_FOX, COL_GDN, COL_GZ, COL_SMALL = 0, 1536, 3072, 3584
LANES = 128
SM_FF, SM_GB, SM_GA = 0, 8, 12

ADAM_LR, ADAM_B1, ADAM_B2, ADAM_EPS, ADAM_WD, ADAM_STEP = 0.001, 0.9, 0.999, 1e-08, 0.01, 10

TOKEN_BLOCK = 256
MATMUL_BLOCK = 512
FOX_SCALE = FOX_HEAD_DIM ** -0.5
GDN_QSCALE = GDN_HEAD_DIM ** -0.5
NEG_BIG = -1e30
VMEM_LIMIT = 56 * 1024 * 1024

VMEM_SPEC = pl.BlockSpec(memory_space=pltpu.VMEM)


def _sds(shape, dtype=F32):
    return jax.ShapeDtypeStruct(shape, dtype)


def _params(*sem):
    return pltpu.CompilerParams(dimension_semantics=sem if sem else None, vmem_limit_bytes=VMEM_LIMIT)


def _mm(a, b):
    return jnp.dot(a.astype(BF), b.astype(BF), preferred_element_type=F32)


def _mm_nt(a, b):
    return lax.dot_general(a.astype(BF), b.astype(BF), (((1,), (1,)), ((), ())), preferred_element_type=F32)


def _mm_tn(a, b):
    return lax.dot_general(a.astype(BF), b.astype(BF), (((0,), (0,)), ((), ())), preferred_element_type=F32)


def _sigmoid(x):
    return 1.0 / (1.0 + jnp.exp(-x))


def _softplus(x):
    return jnp.maximum(x, 0.0) + jnp.log1p(jnp.exp(-jnp.abs(x)))


def _iota(shape, dim):
    return lax.broadcasted_iota(jnp.int32, shape, dim)


def _shift_down(x, s, row):
    return jnp.where(row >= s, pltpu.roll(x, s, 0), 0.0)


def _shift_up(x, s, row):
    n = x.shape[0]
    return jnp.where(row < n - s, pltpu.roll(x, n - s, 0), 0.0)


def _norm_proj(x, nw, wt_al):
    t = x.shape[0]

    def body(x_ref, nw_ref, w_ref, proj_ref, h_ref):
        xv = x_ref[...]
        r = lax.rsqrt(jnp.mean(xv * xv, axis=-1, keepdims=True) + EPS)
        h = (xv * r * nw_ref[...]).astype(BF)
        h_ref[...] = h
        proj_ref[...] = lax.dot_general(h, w_ref[...], (((1,), (1,)), ((), ())), preferred_element_type=F32)

    tm = min(MATMUL_BLOCK, t)
    return pl.pallas_call(
        body, name="norm_proj", grid=(t // tm,),
        in_specs=[pl.BlockSpec((tm, D_MODEL), lambda i: (i, 0)), pl.BlockSpec((1, D_MODEL), lambda i: (0, 0)),
                  pl.BlockSpec((PROJ_W, D_MODEL), lambda i: (0, 0))],
        out_specs=[pl.BlockSpec((tm, PROJ_W), lambda i: (i, 0)), pl.BlockSpec((tm, D_MODEL), lambda i: (i, 0))],
        out_shape=[_sds((t, PROJ_W)), _sds((t, D_MODEL), BF)],
        compiler_params=_params("parallel"),
    )(x, nw, wt_al)


def _lane_column(x, lane):
    return jnp.sum(jnp.where(_iota((1, LANES), 1) == lane, x, 0.0), axis=-1, keepdims=True)


def _small_prep(proj, fb, al, dtb):
    t = proj.shape[0]

    def body(sm_ref, fb_ref, al_ref, dtb_ref, cum_ref, cumt_ref, beta_ref, g_ref):
        s = sm_ref[...]
        z = s + fb_ref[...]
        cum = jnp.minimum(z, 0.0) - jnp.log1p(jnp.exp(-jnp.abs(z)))
        row = _iota((t, LANES), 0)
        step = 1
        while step < t:
            cum = cum + _shift_down(cum, step, row)
            step *= 2
        cum_ref[...] = cum
        cumt_ref[...] = cum.T
        beta_ref[...] = _sigmoid(s)
        g_ref[...] = -jnp.exp(al_ref[...]) * _softplus(s + dtb_ref[...])

    vec = pl.BlockSpec((1, LANES), lambda i: (0, 0))
    tok = pl.BlockSpec((t, LANES), lambda i: (0, 0))
    return pl.pallas_call(
        body, name="small_prep", grid=(1,),
        in_specs=[pl.BlockSpec((t, LANES), lambda i: (0, COL_SMALL // LANES)), vec, vec, vec],
        out_specs=[tok, pl.BlockSpec((LANES, t), lambda i: (0, 0)), tok, tok],
        out_shape=[_sds((t, LANES)), _sds((LANES, t)), _sds((t, LANES)), _sds((t, LANES))],
        compiler_params=_params("arbitrary"),
    )(proj, fb, al, dtb)


def _fox_scores(qh, kb, cq_all, cumt_ref, head, i, tq):
    klen = (i + 1) * tq
    s = _mm_nt(qh, kb[:klen]) * FOX_SCALE
    cq = cq_all[i * tq:(i + 1) * tq]
    ck = cumt_ref[pl.ds(head, 1), 0:klen]
    s = s + cq - ck
    qi = _iota((tq, klen), 0) + i * tq
    ki = _iota((tq, klen), 1)
    return jnp.where(ki <= qi, s, NEG_BIG)


def _fox_fwd(proj, cum, cumt, fnw):
    t = proj.shape[0]
    tq = min(TOKEN_BLOCK, t // 2)
    nq = t // tq

    def body(q_ref, k_ref, v_ref, cum_ref, cumt_ref, fnw_ref, o_ref, lse_ref, fn_ref):
        j = pl.program_id(0)
        first = _iota((1, LANES), 1) < FOX_HEAD_DIM
        kb = k_ref[...].astype(BF)
        vb = v_ref[...].astype(BF)
        cq_all = [_lane_column(cum_ref[...], 2 * j + hh) for hh in range(2)]
        for i in range(nq):
            rows = slice(i * tq, (i + 1) * tq)
            klen = (i + 1) * tq
            q_i = q_ref[rows, :]
            o_acc = jnp.zeros((tq, LANES), F32)
            lse_acc = jnp.zeros((tq, LANES), F32)
            for hh in range(2):
                mh = first if hh == 0 else jnp.logical_not(first)
                qh = jnp.where(mh, q_i, 0.0).astype(BF)
                s = _fox_scores(qh, kb, cq_all[hh], cumt_ref, 2 * j + hh, i, tq)
                m = jnp.max(s, axis=-1, keepdims=True)
                p = jnp.exp(s - m)
                l = jnp.sum(p, axis=-1, keepdims=True)
                o = jnp.dot(p.astype(BF), vb[:klen], preferred_element_type=F32) / l
                o_acc = jnp.where(mh, o, o_acc)
                lse_acc = jnp.where(mh, m + jnp.log(l), lse_acc)
            o_ref[rows, :] = o_acc
            lse_ref[rows, :] = lse_acc
            o2 = o_acc * o_acc
            s0 = jnp.sum(jnp.where(first, o2, 0.0), axis=-1, keepdims=True)
            s1 = jnp.sum(jnp.where(first, 0.0, o2), axis=-1, keepdims=True)
            r = lax.rsqrt(jnp.where(first, s0, s1) * (1.0 / FOX_HEAD_DIM) + EPS)
            fn_ref[rows, :] = (o_acc * r * fnw_ref[...]).astype(BF)

    blk = lambda off: pl.BlockSpec((t, LANES), lambda j: (0, off + j))
    return pl.pallas_call(
        body, name="fox_fwd", grid=(N_FOX_HEADS // 2,),
        in_specs=[blk(0), blk(4), blk(8), pl.BlockSpec((t, LANES), lambda j: (0, 0)),
                  pl.BlockSpec((LANES, t), lambda j: (0, 0)), pl.BlockSpec((1, LANES), lambda j: (0, 0))],
        out_specs=[blk(0), blk(0), blk(0)],
        out_shape=[_sds((t, D_FOX)), _sds((t, D_FOX)), _sds((t, D_FOX), BF)],
        compiler_params=_params("parallel"),
    )(proj, proj, proj, cum, cumt, fnw)


def _fox_bwd(proj, cum, cumt, lse, o, do):
    t = proj.shape[0]
    tq = min(TOKEN_BLOCK, t // 2)
    nq = t // tq

    def body(q_ref, k_ref, v_ref, cum_ref, cumt_ref, lse_ref, o_ref, do_ref,
             dq_ref, dk_ref, dv_ref, dcq_ref, dckt_ref, dk_s, dv_s):
        j = pl.program_id(0)

        @pl.when(j == 0)
        def _():
            dcq_ref[...] = jnp.zeros_like(dcq_ref)
            dckt_ref[...] = jnp.zeros_like(dckt_ref)

        lane = _iota((1, LANES), 1)
        first = lane < FOX_HEAD_DIM
        kf = k_ref[...]
        kb = kf.astype(BF)
        vb = v_ref[...].astype(BF)
        dk_s[...] = jnp.zeros_like(dk_s)
        dv_s[...] = jnp.zeros_like(dv_s)
        masks = [first, jnp.logical_not(first)]
        kmask = [jnp.where(mh, kf, 0.0).astype(BF) for mh in masks]
        cq_all = [_lane_column(cum_ref[...], 2 * j + hh) for hh in range(2)]
        for i in range(nq):
            rows = slice(i * tq, (i + 1) * tq)
            klen = (i + 1) * tq
            q_i = q_ref[rows, :]
            do_i = do_ref[rows, :]
            o_i = o_ref[rows, :]
            lse_i = lse_ref[rows, :]
            dq_acc = jnp.zeros((tq, LANES), F32)
            dcq_acc = jnp.zeros((tq, LANES), F32)
            for hh in range(2):
                mh = masks[hh]
                head = 2 * j + hh
                qh = jnp.where(mh, q_i, 0.0).astype(BF)
                doh = jnp.where(mh, do_i, 0.0)
                dohb = doh.astype(BF)
                delta = jnp.sum(doh * o_i, axis=-1, keepdims=True)
                s = _fox_scores(qh, kb, cq_all[hh], cumt_ref, head, i, tq)
                p = jnp.exp(s - lse_i[:, FOX_HEAD_DIM * hh:FOX_HEAD_DIM * hh + 1])
                dp = _mm_nt(dohb, vb[:klen])
                ds = p * (dp - delta)
                dsb = ds.astype(BF)
                dq_acc = dq_acc + jnp.dot(dsb, kmask[hh][:klen], preferred_element_type=F32) * FOX_SCALE
                dk_s[0:klen, :] += _mm_tn(dsb, qh) * FOX_SCALE
                dv_s[0:klen, :] += _mm_tn(p, dohb)
                dcq_acc = jnp.where(lane == head, jnp.sum(ds, axis=-1, keepdims=True), dcq_acc)
                dckt_ref[pl.ds(head, 1), 0:klen] += jnp.sum(ds, axis=0, keepdims=True)
            dq_ref[rows, :] = dq_acc
            dcq_ref[rows, :] += dcq_acc
        dk_ref[...] = dk_s[...]
        dv_ref[...] = dv_s[...]

    blk = lambda off: pl.BlockSpec((t, LANES), lambda j: (0, off + j))
    return pl.pallas_call(
        body, name="fox_bwd", grid=(N_FOX_HEADS // 2,),
        in_specs=[blk(0), blk(4), blk(8), pl.BlockSpec((t, LANES), lambda j: (0, 0)),
                  pl.BlockSpec((LANES, t), lambda j: (0, 0)), blk(0), blk(0), blk(0)],
        out_specs=[blk(0), blk(0), blk(0), pl.BlockSpec((t, LANES), lambda j: (0, 0)),
                   pl.BlockSpec((LANES, t), lambda j: (0, 0))],
        out_shape=[_sds((t, D_FOX))] * 3 + [_sds((t, LANES)), _sds((LANES, t))],
        scratch_shapes=[pltpu.VMEM((t, LANES), F32), pltpu.VMEM((t, LANES), F32)],
        compiler_params=_params("arbitrary"),
    )(proj, proj, proj, cum, cumt, lse, o, do)


def _conv(x, w, row):
    return (w[3:4, :] * x + w[2:3, :] * _shift_down(x, 1, row) + w[1:2, :] * _shift_down(x, 2, row)
            + w[0:1, :] * _shift_down(x, 3, row))


def _chunk_decay(gc_c):
    gi = gc_c[:, 0:CHUNK]
    gj = gc_c.T[0:CHUNK, :]
    ri = _iota((CHUNK, CHUNK), 0)
    cj = _iota((CHUNK, CHUNK), 1)
    return jnp.where(ri >= cj, jnp.exp(jnp.minimum(gi - gj, 0.0)), 0.0), ri > cj


def _gdn_specs(t):
    col = lambda off: pl.BlockSpec((t, LANES), lambda h: (0, off + h))
    cw = lambda off: pl.BlockSpec((CONV_K, LANES), lambda h: (0, off + h))
    mat = pl.BlockSpec((1, t // CHUNK, CHUNK, CHUNK), lambda h: (h, 0, 0, 0))
    return col, cw, mat


def _gdn_prep(proj, convw, beta, g):
    t = proj.shape[0]
    nch = t // CHUNK

    def body(xq_ref, xk_ref, xv_ref, wq_ref, wk_ref, wv_ref, beta_ref, g_ref,
             qn_ref, kn_ref, cv_ref, gc_ref, be_ref, m_ref, a_ref):
        row = _iota((t, LANES), 0)
        hd = pl.program_id(0)
        be_ref[...] = jnp.broadcast_to(_lane_column(beta_ref[...], SM_GB + hd), (t, LANES))

        def act(x_ref, w_ref):
            y = _conv(x_ref[...], w_ref[...], row)
            return y * _sigmoid(y)

        cq = act(xq_ref, wq_ref)
        ck = act(xk_ref, wk_ref)
        cv_ref[...] = act(xv_ref, wv_ref)
        qn_ref[...] = cq * lax.rsqrt(jnp.sum(cq * cq, axis=-1, keepdims=True) + EPS) * GDN_QSCALE
        kn_ref[...] = ck * lax.rsqrt(jnp.sum(ck * ck, axis=-1, keepdims=True) + EPS)
        gc = jnp.broadcast_to(_lane_column(g_ref[...], SM_GA + hd), (t, LANES))
        pos = row % CHUNK
        step = 1
        while step < CHUNK:
            gc = gc + jnp.where(pos >= step, pltpu.roll(gc, step, 0), 0.0)
            step *= 2
        gc_ref[...] = gc

        def chunk(n, carry):
            sl = pl.ds(pl.multiple_of(n * CHUNK, CHUNK), CHUNK)
            k_c = kn_ref[sl, :]
            decay, strict = _chunk_decay(gc_ref[sl, :])
            m_ref[0, n] = jnp.where(strict, _mm_nt(k_c * be_ref[sl, :], k_c) * decay, 0.0)
            a_ref[0, n] = _mm_nt(qn_ref[sl, :], k_c) * decay
            return carry

        lax.fori_loop(0, nch, chunk, 0)

    col, cw, mat = _gdn_specs(t)
    return pl.pallas_call(
        body, name="gdn_prep", grid=(N_GDN_HEADS,),
        in_specs=[col(12), col(16), col(20), cw(0), cw(4), cw(8)] + [pl.BlockSpec((t, LANES), lambda h: (0, 0))] * 2,
        out_specs=[col(0), col(0), col(0), col(0), col(0), mat, mat],
        out_shape=[_sds((t, D_GDN))] * 5 + [_sds((N_GDN_HEADS, nch, CHUNK, CHUNK))] * 2,
        compiler_params=_params("parallel"),
    )(proj, proj, proj, convw, convw, convw, beta, g)


def _tri_inverse(m3):
    assert m3.shape == (LANES, CHUNK, CHUNK)

    def body(m_ref, t_ref, ms, ts):
        for i in range(CHUNK):
            ms[i * CHUNK:(i + 1) * CHUNK, :] = m_ref[:, i, :].T
        cidx = _iota((CHUNK, LANES), 0)

        def outer(i, carry):
            def inner(jj, acc):
                mrow = ms[pl.ds(i * CHUNK + jj, 1), :]
                return acc - mrow * ts[pl.ds(pl.multiple_of(jj * CHUNK, CHUNK), CHUNK), :]

            acc = lax.fori_loop(0, i, inner, jnp.where(cidx == i, 1.0, 0.0).astype(F32))
            ts[pl.ds(pl.multiple_of(i * CHUNK, CHUNK), CHUNK), :] = acc
            return carry

        lax.fori_loop(0, CHUNK, outer, 0)
        for i in range(CHUNK):
            t_ref[:, i, :] = ts[i * CHUNK:(i + 1) * CHUNK, :].T

    return pl.pallas_call(
        body, name="tri_inverse", in_specs=[VMEM_SPEC], out_specs=VMEM_SPEC,
        out_shape=_sds((LANES, CHUNK, CHUNK)),
        scratch_shapes=[pltpu.VMEM((CHUNK * CHUNK, LANES), F32), pltpu.VMEM((CHUNK * CHUNK, LANES), F32)],
        compiler_params=_params(),
    )(m3)


def _gdn_chunk_terms(q, k, v, b, gcc):
    eg = jnp.exp(gcc)
    last = gcc[CHUNK - 1:CHUNK, :]
    egl = jnp.exp(last - gcc)
    gl = jnp.exp(last)
    kb = k * b
    return eg, egl, gl, kb, v * b, kb * eg, q * eg, k * egl


GDN_BLOCK_CHUNKS = 4


def _gdn_block_specs(t, reverse):
    cb = GDN_BLOCK_CHUNKS
    nb = t // (cb * CHUNK)
    idx = (lambda i: nb - 1 - i) if reverse else (lambda i: i)
    tok = pl.BlockSpec((cb * CHUNK, D_GDN), lambda i: (idx(i), 0))
    mat = pl.BlockSpec((N_GDN_HEADS, cb, CHUNK, CHUNK), lambda i: (0, idx(i), 0, 0))
    state = pl.BlockSpec((N_GDN_HEADS, cb, GDN_HEAD_DIM, GDN_HEAD_DIM), lambda i: (0, idx(i), 0, 0))
    return nb, tok, mat, state


def _gdn_scan(qn, kn, cv, be, gc, tinv, amat):
    t = qn.shape[0]
    nch = t // CHUNK

    def body(q_ref, k_ref, v_ref, b_ref, gc_ref, t_ref, a_ref, o_ref, sall_ref, vn_ref, s_scr):
        @pl.when(pl.program_id(0) == 0)
        def _():
            s_scr[...] = jnp.zeros_like(s_scr)

        heads = range(N_GDN_HEADS)
        cols = [slice(hd * LANES, (hd + 1) * LANES) for hd in heads]
        s = [s_scr[hd] for hd in heads]
        for cc in range(GDN_BLOCK_CHUNKS):
            rs = slice(cc * CHUNK, (cc + 1) * CHUNK)
            terms = [_gdn_chunk_terms(q_ref[rs, cs], k_ref[rs, cs], v_ref[rs, cs], b_ref[rs, cs], gc_ref[rs, cs])
                     for cs in cols]
            for hd in heads:
                sall_ref[hd, cc] = s[hd]
            uw = [_mm(t_ref[hd, cc], jnp.concatenate([terms[hd][4], terms[hd][5]], axis=1)) for hd in heads]
            ws_qs = [_mm(jnp.concatenate([uw[hd][:, LANES:], terms[hd][6]], axis=0), s[hd]) for hd in heads]
            vn = [uw[hd][:, :LANES] - ws_qs[hd][:CHUNK] for hd in heads]
            a_vn = [_mm(a_ref[hd, cc], vn[hd]) for hd in heads]
            kd_vn = [_mm_tn(terms[hd][7], vn[hd]) for hd in heads]
            for hd in heads:
                vn_ref[rs, cols[hd]] = vn[hd]
                o_ref[rs, cols[hd]] = ws_qs[hd][CHUNK:] + a_vn[hd]
                s[hd] = s[hd] * terms[hd][2] + kd_vn[hd]
        for hd in heads:
            s_scr[hd] = s[hd]

    nb, tok, mat, state = _gdn_block_specs(t, False)
    return pl.pallas_call(
        body, name="gdn_scan", grid=(nb,),
        in_specs=[tok] * 5 + [mat, mat], out_specs=[tok, state, tok],
        out_shape=[_sds((t, D_GDN)), _sds((N_GDN_HEADS, nch, GDN_HEAD_DIM, GDN_HEAD_DIM)), _sds((t, D_GDN))],
        scratch_shapes=[pltpu.VMEM((N_GDN_HEADS, GDN_HEAD_DIM, GDN_HEAD_DIM), F32)],
        compiler_params=_params("arbitrary"),
    )(qn, kn, cv, be, gc, tinv, amat)


def _gdn_bwd(qn, kn, cv, be, gc, tinv, amat, s_all, vn_all, do):
    t = qn.shape[0]

    def body(q_ref, k_ref, v_ref, b_ref, gc_ref, t_ref, a_ref, sall_ref, vn_ref, do_ref,
             dq_ref, dk_ref, dv_ref, db_ref, dg_ref, ds_scr):
        @pl.when(pl.program_id(0) == 0)
        def _():
            ds_scr[...] = jnp.zeros_like(ds_scr)

        lastrow = _iota((CHUNK, LANES), 0) == CHUNK - 1
        heads = range(N_GDN_HEADS)
        cols = [slice(hd * LANES, (hd + 1) * LANES) for hd in heads]
        each = lambda fn: [fn(hd) for hd in heads]
        rows_cat = lambda x, y: jnp.concatenate([x, y], axis=0)
        lane_cat = lambda x, y: jnp.concatenate([x, y], axis=1)
        dsp = each(lambda hd: ds_scr[hd])
        for cc in reversed(range(GDN_BLOCK_CHUNKS)):
            rs = slice(cc * CHUNK, (cc + 1) * CHUNK)
            q = each(lambda hd: q_ref[rs, cols[hd]])
            k = each(lambda hd: k_ref[rs, cols[hd]])
            v = each(lambda hd: v_ref[rs, cols[hd]])
            b = each(lambda hd: b_ref[rs, cols[hd]])
            gcc = each(lambda hd: gc_ref[rs, cols[hd]])
            do_c = each(lambda hd: do_ref[rs, cols[hd]])
            vn = each(lambda hd: vn_ref[rs, cols[hd]])
            tn = each(lambda hd: t_ref[hd, cc])
            st = each(lambda hd: sall_ref[hd, cc])
            terms = each(lambda hd: _gdn_chunk_terms(q[hd], k[hd], v[hd], b[hd], gcc[hd]))
            eg, egl, gl, kb, vb, kbg, qd, kd = [[terms[hd][i] for hd in heads] for i in range(8)]
            w = each(lambda hd: _mm(tn[hd], kbg[hd]))
            a_do = each(lambda hd: _mm_tn(a_ref[hd, cc], do_c[hd]))
            kd_ds = each(lambda hd: _mm(kd[hd], dsp[hd]))
            da = each(lambda hd: _mm_nt(do_c[hd], vn[hd]))
            dkd = each(lambda hd: _mm_nt(vn[hd], dsp[hd]))
            by_k = each(lambda hd: _mm_nt(rows_cat(kb[hd], q[hd]), k[hd]))
            dgl = each(lambda hd: jnp.sum(jnp.sum(dsp[hd] * st[hd], axis=-1, keepdims=True), axis=0, keepdims=True))
            dvn = each(lambda hd: a_do[hd] + kd_ds[hd])
            do_dvn = each(lambda hd: rows_cat(do_c[hd], dvn[hd]))
            by_s = each(lambda hd: _mm_nt(do_dvn[hd], st[hd]))
            dqd = each(lambda hd: by_s[hd][:CHUNK])
            dvn_dw = each(lambda hd: lane_cat(dvn[hd], -by_s[hd][CHUNK:]))
            dsp = each(lambda hd: _mm_tn(rows_cat(qd[hd], -w[hd]), do_dvn[hd]) + gl[hd] * dsp[hd])
            dt = each(lambda hd: _mm_nt(dvn_dw[hd], lane_cat(vb[hd], kbg[hd])))
            by_t = each(lambda hd: _mm_tn(tn[hd], dvn_dw[hd]))
            tt_dt = each(lambda hd: _mm_tn(tn[hd], dt[hd]))
            dm_raw = each(lambda hd: _mm_nt(tt_dt[hd], tn[hd]))
            masks = each(lambda hd: _chunk_decay(gcc[hd]))
            dkk = each(lambda hd: jnp.where(masks[hd][1], -dm_raw[hd], 0.0) * masks[hd][0])
            dqk = each(lambda hd: da[hd] * masks[hd][0])
            dqk_dkk = each(lambda hd: rows_cat(dqk[hd], dkk[hd]))
            on_k = each(lambda hd: _mm(dqk_dkk[hd], k[hd]))
            dk_mm = each(lambda hd: _mm_tn(dqk_dkk[hd], rows_cat(q[hd], kb[hd])))
            for hd in heads:
                cs = cols[hd]
                dvb, dkbg = by_t[hd][:, :LANES], by_t[hd][:, LANES:]
                gmat = dkk[hd] * by_k[hd][:CHUNK] + dqk[hd] * by_k[hd][CHUNK:]
                dq_ref[rs, cs] = dqd[hd] * eg[hd] + on_k[hd][:CHUNK]
                dkb = on_k[hd][CHUNK:] + dkbg * eg[hd]
                dk_ref[rs, cs] = dkd[hd] * egl[hd] + dk_mm[hd] + dkb * b[hd]
                db = jnp.sum(dkb * k[hd], axis=-1, keepdims=True) + jnp.sum(dvb * v[hd], axis=-1, keepdims=True)
                db_ref[rs, cs] = jnp.broadcast_to(db, (CHUNK, LANES))
                dv_ref[rs, cs] = dvb * b[hd]
                dkd_kd = jnp.sum(dkd[hd] * kd[hd], axis=-1, keepdims=True)
                col_sums = jnp.sum(lane_cat(gmat, jnp.zeros_like(gmat)).T, axis=-1, keepdims=True)
                dgc = (jnp.sum(gmat, axis=-1, keepdims=True) - col_sums[:CHUNK]
                       + jnp.sum(dqd[hd] * qd[hd], axis=-1, keepdims=True)
                       + jnp.sum(dkbg * kbg[hd], axis=-1, keepdims=True) - dkd_kd)
                extra = jnp.sum(dkd_kd, axis=0, keepdims=True) + dgl[hd] * gl[hd]
                dg_ref[rs, cs] = dgc + jnp.where(lastrow, extra, 0.0)
        for hd in heads:
            ds_scr[hd] = dsp[hd]
        dg = dg_ref[...]
        row = _iota(dg.shape, 0)
        pos = row % CHUNK
        step = 1
        while step < CHUNK:
            dg = dg + jnp.where(pos < CHUNK - step, pltpu.roll(dg, dg.shape[0] - step, 0), 0.0)
            step *= 2
        dg_ref[...] = dg

    nb, tok, mat, state = _gdn_block_specs(t, True)
    return pl.pallas_call(
        body, name="gdn_bwd", grid=(nb,),
        in_specs=[tok] * 5 + [mat, mat, state, tok, tok], out_specs=[tok] * 5, out_shape=[_sds((t, D_GDN))] * 5,
        scratch_shapes=[pltpu.VMEM((N_GDN_HEADS, GDN_HEAD_DIM, GDN_HEAD_DIM), F32)],
        compiler_params=_params("arbitrary"),
    )(qn, kn, cv, be, gc, tinv, amat, s_all, vn_all, do)


def _gdn_bwd_conv(proj, convw, dqn, dkn, dcv):
    t = proj.shape[0]

    def body(xq_ref, xk_ref, xv_ref, wq_ref, wk_ref, wv_ref, dq_ref, dk_ref, dv_ref,
             dxq_ref, dxk_ref, dxv_ref, dwq_ref, dwk_ref, dwv_ref):
        row = _iota((t, LANES), 0)

        def one(x_ref, w_ref, d_ref, dx_ref, dw_ref, scale):
            x = x_ref[...]
            w = w_ref[...]
            y = _conv(x, w, row)
            sg = _sigmoid(y)
            dc = d_ref[...]
            if scale is not None:
                c = y * sg
                r = lax.rsqrt(jnp.sum(c * c, axis=-1, keepdims=True) + EPS)
                ch = c * r
                dc = scale * r * (dc - ch * jnp.sum(dc * ch, axis=-1, keepdims=True))
            dy = dc * sg * (1.0 + y * (1.0 - sg))
            dx_ref[...] = (w[3:4, :] * dy + w[2:3, :] * _shift_up(dy, 1, row) + w[1:2, :] * _shift_up(dy, 2, row)
                           + w[0:1, :] * _shift_up(dy, 3, row))
            for jj in range(CONV_K):
                xs = x if jj == CONV_K - 1 else _shift_down(x, CONV_K - 1 - jj, row)
                dw_ref[jj:jj + 1, :] = jnp.sum(dy * xs, axis=0, keepdims=True)

        one(xq_ref, wq_ref, dq_ref, dxq_ref, dwq_ref, GDN_QSCALE)
        one(xk_ref, wk_ref, dk_ref, dxk_ref, dwk_ref, 1.0)
        one(xv_ref, wv_ref, dv_ref, dxv_ref, dwv_ref, None)

    col, cw, _ = _gdn_specs(t)
    return pl.pallas_call(
        body, name="gdn_bwd_conv", grid=(N_GDN_HEADS,),
        in_specs=[col(12), col(16), col(20), cw(0), cw(4), cw(8), col(0), col(0), col(0)],
        out_specs=[col(0), col(0), col(0), cw(0), cw(0), cw(0)],
        out_shape=[_sds((t, D_GDN))] * 3 + [_sds((CONV_K, D_GDN))] * 3,
        compiler_params=_params("parallel"),
    )(proj, proj, proj, convw, convw, convw, dqn, dkn, dcv)


def _mix_out(fox_n, gdn_o, proj, gnw, w_out, x, pmw, plw):
    t = x.shape[0]
    tm = min(MATMUL_BLOCK, t)

    def body(fn_ref, go_ref, gz_ref, gnw_ref, w_ref, x_ref, pmw_ref, plw_ref, x1_ref, h2_ref, mixed_ref, omix_ref,
             h2t_ref):
        omix_ref[:, 0:D_FOX] = fn_ref[...]
        for hd in range(N_GDN_HEADS):
            cs = slice(hd * LANES, (hd + 1) * LANES)
            go = go_ref[:, cs]
            r = lax.rsqrt(jnp.mean(go * go, axis=-1, keepdims=True) + EPS)
            gz = gz_ref[:, cs]
            omix_ref[:, D_FOX + hd * LANES:D_FOX + (hd + 1) * LANES] = (
                go * r * gnw_ref[...] * (gz * _sigmoid(gz))).astype(BF)
        mixed = jnp.dot(omix_ref[...], w_ref[...], preferred_element_type=F32)
        mixed_ref[...] = mixed
        r2 = lax.rsqrt(jnp.mean(mixed * mixed, axis=-1, keepdims=True) + EPS)
        x1 = x_ref[...] + mixed * r2 * pmw_ref[...]
        x1_ref[...] = x1
        r3 = lax.rsqrt(jnp.mean(x1 * x1, axis=-1, keepdims=True) + EPS)
        h2 = x1 * r3 * plw_ref[...]
        h2_ref[...] = h2.astype(BF)
        h2t_ref[...] = h2.T.astype(BF)

    tok = lambda w: pl.BlockSpec((tm, w), lambda i: (i, 0))
    vec = lambda w: pl.BlockSpec((1, w), lambda i: (0, 0))
    return pl.pallas_call(
        body, name="mix_out", grid=(t // tm,),
        in_specs=[tok(D_FOX), tok(D_GDN), pl.BlockSpec((tm, D_GDN), lambda i: (i, COL_GZ // D_GDN)), vec(LANES),
                  pl.BlockSpec((D_MODEL, D_MODEL), lambda i: (0, 0)), tok(D_MODEL), vec(D_MODEL), vec(D_MODEL)],
        out_specs=[tok(D_MODEL)] * 4 + [pl.BlockSpec((D_MODEL, tm), lambda i: (0, i))],
        out_shape=[_sds((t, D_MODEL)), _sds((t, D_MODEL), BF), _sds((t, D_MODEL)), _sds((t, D_MODEL), BF),
                   _sds((D_MODEL, t), BF)],
        compiler_params=_params("parallel"),
    )(fox_n, gdn_o, proj, gnw, w_out, x, pmw, plw)


def _out_bwd(dmixed, w_out, o_fox, gdn_o, proj, fnw, gnw):
    t = dmixed.shape[0]
    tm = min(MATMUL_BLOCK, t)

    def body(dm_ref, w_ref, of_ref, go_ref, gz_ref, fnw_ref, gnw_ref, dof_ref, dgo_ref, dgz_ref, dfw_ref, dgw_ref):
        i = pl.program_id(0)

        @pl.when(i == 0)
        def _():
            dfw_ref[...] = jnp.zeros_like(dfw_ref)
            dgw_ref[...] = jnp.zeros_like(dgw_ref)

        domix = _mm_nt(dm_ref[...], w_ref[...])
        first = _iota((1, LANES), 1) < FOX_HEAD_DIM
        dfw = jnp.zeros((1, LANES), F32)
        dgw = jnp.zeros((1, LANES), F32)
        for pr in range(N_FOX_HEADS // 2):
            cs = slice(pr * LANES, (pr + 1) * LANES)
            o = of_ref[:, cs]
            dfn = domix[:, cs]
            o2 = o * o
            s0 = jnp.sum(jnp.where(first, o2, 0.0), axis=-1, keepdims=True)
            s1 = jnp.sum(jnp.where(first, 0.0, o2), axis=-1, keepdims=True)
            r = lax.rsqrt(jnp.where(first, s0, s1) * (1.0 / FOX_HEAD_DIM) + EPS)
            oh = o * r
            dfw = dfw + jnp.sum(dfn * oh, axis=0, keepdims=True)
            doh = dfn * fnw_ref[...]
            pr_ = doh * oh
            m0 = jnp.sum(jnp.where(first, pr_, 0.0), axis=-1, keepdims=True)
            m1 = jnp.sum(jnp.where(first, 0.0, pr_), axis=-1, keepdims=True)
            dof_ref[:, cs] = r * (doh - oh * jnp.where(first, m0, m1) * (1.0 / FOX_HEAD_DIM))
        for hd in range(N_GDN_HEADS):
            cs = slice(hd * LANES, (hd + 1) * LANES)
            go = go_ref[:, cs]
            gz = gz_ref[:, cs]
            dgated = domix[:, D_FOX + hd * LANES:D_FOX + (hd + 1) * LANES]
            r = lax.rsqrt(jnp.mean(go * go, axis=-1, keepdims=True) + EPS)
            goh = go * r
            sg = _sigmoid(gz)
            sz = gz * sg
            gn = goh * gnw_ref[...]
            dgn = dgated * sz
            dgz_ref[:, cs] = dgated * gn * sg * (1.0 + gz * (1.0 - sg))
            dgw = dgw + jnp.sum(dgn * goh, axis=0, keepdims=True)
            dgh = dgn * gnw_ref[...]
            dgo_ref[:, cs] = r * (dgh - goh * jnp.mean(dgh * goh, axis=-1, keepdims=True))
        dfw_ref[...] += dfw + pltpu.roll(dfw, FOX_HEAD_DIM, 1)
        dgw_ref[...] += dgw

    tok = lambda w: pl.BlockSpec((tm, w), lambda i: (i, 0))
    vec = lambda w: pl.BlockSpec((1, w), lambda i: (0, 0))
    return pl.pallas_call(
        body, name="out_bwd", grid=(t // tm,),
        in_specs=[tok(D_MODEL), pl.BlockSpec((D_MODEL, D_MODEL), lambda i: (0, 0)), tok(D_FOX), tok(D_GDN),
                  pl.BlockSpec((tm, D_GDN), lambda i: (i, COL_GZ // D_GDN)), vec(LANES), vec(LANES)],
        out_specs=[tok(D_FOX), tok(D_GDN), tok(D_GDN), vec(LANES), vec(LANES)],
        out_shape=[_sds((t, D_FOX)), _sds((t, D_GDN)), _sds((t, D_GDN)), _sds((1, LANES)), _sds((1, LANES))],
        compiler_params=_params("arbitrary"),
    )(dmixed, w_out, o_fox, gdn_o, proj, fnw, gnw)


def _mlp_up(h2, w_up):
    t = h2.shape[0]
    tm = min(MATMUL_BLOCK, t)
    pc = D_FF // N_DEV

    def body(h_ref, w_ref, up_ref):
        h = h_ref[...]
        for p in range(N_DEV):
            up_ref[:, p * pc:(p + 1) * pc] = jnp.dot(h, w_ref[p], preferred_element_type=F32).astype(BF)

    return pl.pallas_call(
        body, name="mlp_up", grid=(t // tm,),
        in_specs=[pl.BlockSpec((tm, D_MODEL), lambda i: (i, 0)),
                  pl.BlockSpec((N_DEV, D_MODEL, pc), lambda i: (0, 0, 0))],
        out_specs=pl.BlockSpec((tm, D_FF), lambda i: (i, 0)), out_shape=_sds((t, D_FF), BF),
        compiler_params=_params("parallel"),
    )(h2, w_up)


def _mlp_down_loss(up, w_down, x1, pw, target):
    t = up.shape[0]
    tm = min(MATMUL_BLOCK, t)

    def body(up_ref, w_ref, x1_ref, pw_ref, tg_ref, dy_ref, dx2_ref, loss_ref, dpw_ref):
        i = pl.program_id(0)

        @pl.when(i == 0)
        def _():
            loss_ref[...] = jnp.zeros_like(loss_ref)
            dpw_ref[...] = jnp.zeros_like(dpw_ref)

        u = jnp.maximum(up_ref[...].astype(F32), 0.0)
        y = jnp.dot((u * u).astype(BF), w_ref[...], preferred_element_type=F32)
        r = lax.rsqrt(jnp.mean(y * y, axis=-1, keepdims=True) + EPS)
        yh = y * r
        pw = pw_ref[...]
        err = x1_ref[...] + yh * pw - tg_ref[...]
        part = jnp.sum(jnp.sum(err * err, axis=-1, keepdims=True), axis=0, keepdims=True) * (0.5 / D_MODEL)
        loss_ref[...] += jnp.broadcast_to(part, loss_ref.shape)
        dx2 = err * (1.0 / D_MODEL)
        dx2_ref[...] = dx2
        dpw_ref[...] += jnp.sum(dx2 * yh, axis=0, keepdims=True)
        dyh = dx2 * pw
        dy_ref[...] = (r * (dyh - yh * jnp.mean(dyh * yh, axis=-1, keepdims=True))).astype(BF)

    tok = lambda w: pl.BlockSpec((tm, w), lambda i: (i, 0))
    vec = lambda w: pl.BlockSpec((1, w), lambda i: (0, 0))
    return pl.pallas_call(
        body, name="mlp_down_loss", grid=(t // tm,),
        in_specs=[tok(D_FF), pl.BlockSpec((D_FF, D_MODEL), lambda i: (0, 0)), tok(D_MODEL), vec(D_MODEL), tok(D_MODEL)],
        out_specs=[tok(D_MODEL), tok(D_MODEL), vec(LANES), vec(D_MODEL)],
        out_shape=[_sds((t, D_MODEL), BF), _sds((t, D_MODEL)), _sds((1, LANES)), _sds((1, D_MODEL))],
        compiler_params=_params("arbitrary"),
    )(up, w_down, x1, pw, target)


def _mlp_bwd_act(dy, w_down, up):
    t = dy.shape[0]
    tm = min(MATMUL_BLOCK, t)

    def body(dy_ref, w_ref, up_ref, dup_ref):
        da = lax.dot_general(dy_ref[...], w_ref[...], (((1,), (1,)), ((), ())), preferred_element_type=F32)
        dup_ref[...] = (da * (2.0 * jnp.maximum(up_ref[...].astype(F32), 0.0))).astype(BF)

    return pl.pallas_call(
        body, name="mlp_bwd_act", grid=(t // tm,),
        in_specs=[pl.BlockSpec((tm, D_MODEL), lambda i: (i, 0)), pl.BlockSpec((D_FF, D_MODEL), lambda i: (0, 0)),
                  pl.BlockSpec((tm, D_FF), lambda i: (i, 0))],
        out_specs=pl.BlockSpec((tm, D_FF), lambda i: (i, 0)), out_shape=_sds((t, D_FF), BF),
        compiler_params=_params("parallel"),
    )(dy, w_down, up)


def _mlp_bwd_in(dup, w_up, x1, plw, dx2, mixed, pmw):
    t = dup.shape[0]
    tm = min(MATMUL_BLOCK, t)

    def body(dup_ref, w_ref, x1_ref, plw_ref, dx2_ref, mx_ref, pmw_ref, dx1_ref, dmixed_ref, dplw_ref, dpmw_ref):
        i = pl.program_id(0)

        @pl.when(i == 0)
        def _():
            dplw_ref[...] = jnp.zeros_like(dplw_ref)
            dpmw_ref[...] = jnp.zeros_like(dpmw_ref)

        pc = D_FF // N_DEV
        dh = _mm_nt(dup_ref[:, 0:pc], w_ref[0])
        for p in range(1, N_DEV):
            dh = dh + _mm_nt(dup_ref[:, p * pc:(p + 1) * pc], w_ref[p])
        x1 = x1_ref[...]
        r = lax.rsqrt(jnp.mean(x1 * x1, axis=-1, keepdims=True) + EPS)
        xh = x1 * r
        dplw_ref[...] += jnp.sum(dh * xh, axis=0, keepdims=True)
        dxh = dh * plw_ref[...]
        dx1 = dx2_ref[...] + r * (dxh - xh * jnp.mean(dxh * xh, axis=-1, keepdims=True))
        dx1_ref[...] = dx1
        mx = mx_ref[...]
        r2 = lax.rsqrt(jnp.mean(mx * mx, axis=-1, keepdims=True) + EPS)
        mh = mx * r2
        dpmw_ref[...] += jnp.sum(dx1 * mh, axis=0, keepdims=True)
        dmh = dx1 * pmw_ref[...]
        dmixed_ref[...] = (r2 * (dmh - mh * jnp.mean(dmh * mh, axis=-1, keepdims=True))).astype(BF)

    tok = lambda w: pl.BlockSpec((tm, w), lambda i: (i, 0))
    vec = lambda w: pl.BlockSpec((1, w), lambda i: (0, 0))
    return pl.pallas_call(
        body, name="mlp_bwd_in", grid=(t // tm,),
        in_specs=[tok(D_FF), pl.BlockSpec((N_DEV, D_MODEL, D_FF // N_DEV), lambda i: (0, 0, 0)), tok(D_MODEL),
                  vec(D_MODEL), tok(D_MODEL), tok(D_MODEL), vec(D_MODEL)],
        out_specs=[tok(D_MODEL), tok(D_MODEL), vec(D_MODEL), vec(D_MODEL)],
        out_shape=[_sds((t, D_MODEL)), _sds((t, D_MODEL), BF), _sds((1, D_MODEL)), _sds((1, D_MODEL))],
        compiler_params=_params("arbitrary"),
    )(dup, w_up, x1, plw, dx2, mixed, pmw)


def _wgrad(a, b, a_cols, split=1, a_fn=None, a_block0=0, name="wgrad"):
    t, b_cols = b.shape
    n_a = (a.shape[1] - a_block0 * a_cols) // a_cols if a_block0 else a.shape[1] // a_cols

    def body(a_ref, b_ref, o_ref):
        av = a_ref[...]
        if a_fn is not None:
            av = a_fn(av)
        o_ref[...] = _mm_tn(av, b_ref[...]).astype(BF).reshape(o_ref.shape)

    return pl.pallas_call(
        body, name=name, grid=(n_a,),
        in_specs=[pl.BlockSpec((t, a_cols), lambda i: (0, i + a_block0)), pl.BlockSpec((t, b_cols), lambda i: (0, 0))],
        out_specs=pl.BlockSpec((split, a_cols // split, b_cols), lambda i: (i, 0, 0)),
        out_shape=_sds((n_a * split, a_cols // split, b_cols), BF),
        compiler_params=_params("parallel"),
    )(a, b)


def _wgrad_pre_t(at, b, b_cols, name):
    rows, t = at.shape
    n_b = b.shape[1] // b_cols

    def body(a_ref, b_ref, o_ref):
        o_ref[0] = jnp.dot(a_ref[...], b_ref[...], preferred_element_type=F32).astype(BF)

    return pl.pallas_call(
        body, name=name, grid=(n_b,),
        in_specs=[pl.BlockSpec((rows, t), lambda j: (0, 0)), pl.BlockSpec((t, b_cols), lambda j: (0, j))],
        out_specs=pl.BlockSpec((1, rows, b_cols), lambda j: (j, 0, 0)), out_shape=_sds((n_b, rows, b_cols), BF),
        compiler_params=_params("parallel"),
    )(at, b)


def _small_bwd(proj, fb, al, dtb, dcq, dckt, dbe, dge):
    t = proj.shape[0]

    def body(sm_ref, fb_ref, al_ref, dtb_ref, dcq_ref, dckt_ref, dbe_ref, dge_ref, dsm_ref, dvec_ref):
        s = sm_ref[...]
        lane = _iota((1, LANES), 1)
        dcum = dcq_ref[...] - dckt_ref[...].T
        row = _iota((t, LANES), 0)
        step = 1
        while step < t:
            dcum = dcum + _shift_up(dcum, step, row)
            step *= 2
        dff = dcum * _sigmoid(-(s + fb_ref[...]))
        dbeta = jnp.zeros((t, LANES), F32)
        dg = jnp.zeros((t, LANES), F32)
        for hd in range(N_GDN_HEADS):
            dbeta = jnp.where(lane == SM_GB + hd, dbe_ref[:, hd * LANES:hd * LANES + 1], dbeta)
            dg = jnp.where(lane == SM_GA + hd, dge_ref[:, hd * LANES:hd * LANES + 1], dg)
        beta = _sigmoid(s)
        dgb = dbeta * beta * (1.0 - beta)
        za = s + dtb_ref[...]
        nea = -jnp.exp(al_ref[...])
        dga = dg * nea * _sigmoid(za)
        is_f = lane < SM_GB
        is_b = (lane >= SM_GB) & (lane < SM_GA)
        is_a = (lane >= SM_GA) & (lane < SM_GA + 4)
        dsm_ref[...] = jnp.where(is_f, dff, jnp.where(is_b, dgb, jnp.where(is_a, dga, 0.0)))
        dvec_ref[...] = jnp.zeros_like(dvec_ref)
        dvec_ref[0:1, :] = jnp.sum(jnp.where(is_f, dff, 0.0), axis=0, keepdims=True)
        dvec_ref[1:2, :] = jnp.sum(jnp.where(is_a, dg * nea * _softplus(za), 0.0), axis=0, keepdims=True)
        dvec_ref[2:3, :] = jnp.sum(jnp.where(is_a, dga, 0.0), axis=0, keepdims=True)

    vec = pl.BlockSpec((1, LANES), lambda i: (0, 0))
    full = lambda r, c: pl.BlockSpec((r, c), lambda i: (0, 0))
    return pl.pallas_call(
        body, name="small_bwd", grid=(1,),
        in_specs=[pl.BlockSpec((t, LANES), lambda i: (0, COL_SMALL // LANES)), vec, vec, vec, full(t, LANES),
                  full(LANES, t), full(t, 512), full(t, 512)],
        out_specs=[full(t, LANES), full(8, LANES)], out_shape=[_sds((t, LANES)), _sds((8, LANES))],
        compiler_params=_params("arbitrary"),
    )(proj, fb, al, dtb, dcq, dckt, dbe, dge)


def _pack_dproj(dfox, dgdn, dgz, dsm):
    t = dgz.shape[0]
    tm = min(MATMUL_BLOCK, t)

    def body(*refs):
        parts, dp_ref = refs[:8], refs[8]
        col = 0
        for part in parts:
            width = part.shape[1]
            dp_ref[:, col:col + width] = part[...].astype(BF)
            col += width

    tok = lambda w: pl.BlockSpec((tm, w), lambda i: (i, 0))
    return pl.pallas_call(
        body, name="pack_dproj", grid=(t // tm,), in_specs=[tok(D_FOX)] * 3 + [tok(D_GDN)] * 4 + [tok(LANES)],
        out_specs=tok(PROJ_W), out_shape=_sds((t, PROJ_W), BF), compiler_params=_params("parallel"),
    )(*dfox, *dgdn, dgz, dsm)


def _in_bwd(dproj, wt_al, x, nw, dx1):
    t = x.shape[0]
    tm = min(MATMUL_BLOCK, t)

    def body(dp_ref, w_ref, x_ref, nw_ref, dx1_ref, dx_ref, dnw_ref):
        i = pl.program_id(0)

        @pl.when(i == 0)
        def _():
            dnw_ref[...] = jnp.zeros_like(dnw_ref)

        dh = jnp.dot(dp_ref[...], w_ref[...], preferred_element_type=F32)
        xv = x_ref[...]
        r = lax.rsqrt(jnp.mean(xv * xv, axis=-1, keepdims=True) + EPS)
        xh = xv * r
        dnw_ref[...] += jnp.sum(dh * xh, axis=0, keepdims=True)
        dxh = dh * nw_ref[...]
        dx_ref[...] = dx1_ref[...] + r * (dxh - xh * jnp.mean(dxh * xh, axis=-1, keepdims=True))

    tok = lambda w: pl.BlockSpec((tm, w), lambda i: (i, 0))
    vec = lambda w: pl.BlockSpec((1, w), lambda i: (0, 0))
    return pl.pallas_call(
        body, name="in_bwd", grid=(t // tm,),
        in_specs=[tok(PROJ_W), pl.BlockSpec((PROJ_W, D_MODEL), lambda i: (0, 0)), tok(D_MODEL), vec(D_MODEL),
                  tok(D_MODEL)],
        out_specs=[tok(D_MODEL), vec(D_MODEL)], out_shape=[_sds((t, D_MODEL)), _sds((1, D_MODEL))],
        compiler_params=_params("arbitrary"),
    )(dproj, wt_al, x, nw, dx1)


def _row(v, width=None):
    v = v.reshape(1, -1).astype(F32)
    if width is not None and v.shape[1] < width:
        v = jnp.pad(v, ((0, 0), (0, width - v.shape[1])))
    return v


def _lane_vec(v, first):
    return jnp.zeros((1, LANES), F32).at[0, first:first + v.shape[0]].set(v.astype(F32))


def _local_step(x, target, wt_al, late_weights, on_grads, convw, pre_mix_norm, fox_f_bias, fox_out_norm,
                gdn_a_log, gdn_dt_bias, gdn_out_norm, post_mix_norm, pre_mlp_norm, post_mlp_norm):
    t = x.shape[0]
    nch = t // CHUNK
    nw, pmw, plw, pw = _row(pre_mix_norm), _row(post_mix_norm), _row(pre_mlp_norm), _row(post_mlp_norm)
    fb, al, dtb = _lane_vec(fox_f_bias, SM_FF), _lane_vec(gdn_a_log, SM_GA), _lane_vec(gdn_dt_bias, SM_GA)
    fnw = _row(jnp.tile(fox_out_norm, 2))
    gnw = _row(gdn_out_norm)

    proj, h = _norm_proj(x, nw, wt_al)
    cum, cumt, beta, g = _small_prep(proj, fb, al, dtb)
    o_fox, lse, fox_n = _fox_fwd(proj, cum, cumt, fnw)
    qn, kn, cv, gc, be, mmat, amat = _gdn_prep(proj, convw, beta, g)
    n_prob = N_GDN_HEADS * nch
    m3 = mmat.reshape(n_prob, CHUNK, CHUNK)
    if n_prob < LANES:
        m3 = jnp.pad(m3, ((0, LANES - n_prob), (0, 0), (0, 0)))
    tinv = _tri_inverse(m3)[:n_prob].reshape(N_GDN_HEADS, nch, CHUNK, CHUNK)
    gdn_o, s_all, vn_all = _gdn_scan(qn, kn, cv, be, gc, tinv, amat)
    w_out = late_weights("w_out", gdn_o)
    x1, h2, mixed, omix, h2t = _mix_out(fox_n, gdn_o, proj, gnw, w_out, x, pmw, plw)
    w_up, w_down = late_weights("mlp", h2)
    up = _mlp_up(h2, w_up)
    dy, dx2, loss, d_pw = _mlp_down_loss(up, w_down, x1, pw, target)

    dup = _mlp_bwd_act(dy, w_down, up)
    relu2 = lambda u: jnp.square(jnp.maximum(u.astype(F32), 0.0))
    g_down = _wgrad(up, dy, D_FF // N_DEV, a_fn=relu2, name="wgrad_down")
    g_up = _wgrad_pre_t(h2t, dup, D_FF // N_DEV, name="wgrad_up")
    token = on_grads("mlp", (g_up, g_down))
    dx1, dmixed, d_plw, d_pmw = _mlp_bwd_in(dup, w_up, x1, plw + token[0:1, 0:1], dx2, mixed, pmw)
    token = on_grads("w_out", _wgrad(omix, dmixed, 512, split=4, name="wgrad_out"))
    do_fox, dgo, dgz, d_fnw, d_gnw = _out_bwd(dmixed, w_out, o_fox, gdn_o, proj, fnw + token[0:1, 0:1], gnw)
    dfq, dfk, dfv, dcq, dckt = _fox_bwd(proj, cum, cumt, lse, o_fox, do_fox)
    dqn, dkn, dcv, dbe, dge = _gdn_bwd(qn, kn, cv, be, gc, tinv, amat, s_all, vn_all, dgo)
    dxq, dxk, dxv, dwq, dwk, dwv = _gdn_bwd_conv(proj, convw, dqn, dkn, dcv)
    dsm, dvec = _small_bwd(proj, fb, al, dtb, dcq, dckt, dbe, dge)
    dproj = _pack_dproj((dfq, dfk, dfv), (dxq, dxk, dxv), dgz, dsm)
    g_main = _wgrad(dproj, h, 512, name="wgrad_in")
    g_tail = _wgrad(dproj, h, LANES, a_block0=COL_SMALL // LANES, name="wgrad_in_small")
    token = on_grads("w_in", jnp.concatenate([g_main.reshape(COL_SMALL, D_MODEL), g_tail[0]]))
    grad_x, d_nw = _in_bwd(dproj, wt_al, x, nw + token[0:1, 0:1], dx1)
    small = dict(norms=(d_nw, d_pmw, d_plw, d_pw), fox_out_norm=d_fnw, gdn_out_norm=d_gnw, loss=loss, vectors=dvec,
                 conv=(dwq, dwk, dwv))
    return grad_x, small


MESH_IDS = pl.DeviceIdType.MESH
CHIP_FLIPS = ((0, 0), (1, 0), (0, 1), (1, 1))
ANY_SPEC = pl.BlockSpec(memory_space=pl.ANY)


def _place():
    return lax.axis_index("x"), lax.axis_index("y"), lax.axis_index("c")


def _all_gather(blocks):
    n = len(blocks)

    def body(*refs):
        ins, outs, (send_sems, recv_sems, local_sems) = refs[:n], refs[n:2 * n], refs[2 * n:]
        x, y, c = _place()
        sibling = (x, y, 1 - c)
        chips = [(x ^ fx, y ^ fy) for fx, fy in CHIP_FLIPS[1:]]

        def slot(out, px, py, pc):
            return out.at[4 * px + 2 * py + pc]

        def copy(a, k, block, to, src=None):
            return pltpu.make_async_remote_copy(
                src_ref=slot(outs[a], *block) if src is None else src, dst_ref=slot(outs[a], *block),
                send_sem=send_sems.at[a, k], recv_sem=recv_sems.at[a, k], device_id=to, device_id_type=MESH_IDS)

        pending = []
        for a in range(n):
            mine = pltpu.make_async_copy(ins[a], slot(outs[a], x, y, c), local_sems.at[a])
            mine.start()
            pending.append(mine)
        sends = []
        for a in range(n):
            first = [copy(a, 0, (x, y, c), sibling, src=ins[a])]
            first += [copy(a, 1 + j, (x, y, c), (*chip, c), src=ins[a]) for j, chip in enumerate(chips)]
            for cp in first:
                cp.start()
            sends += first
        for a in range(n):
            for j, chip in enumerate(chips):
                copy(a, 1 + j, (*chip, c), (x, y, c)).wait_recv()
                fwd = copy(a, 4 + j, (*chip, c), sibling)
                fwd.start()
                sends.append(fwd)
        for a in range(n):
            copy(a, 0, sibling, (x, y, c)).wait_recv()
            for j, chip in enumerate(chips):
                copy(a, 4 + j, (*chip, 1 - c), (x, y, c)).wait_recv()
        for cp in sends:
            cp.wait_send()
        for cp in pending:
            cp.wait()

    return pl.pallas_call(
        body, name="all_gather_weights", in_specs=[ANY_SPEC] * n, out_specs=[ANY_SPEC] * n,
        out_shape=[_sds((N_DEV,) + b.shape, b.dtype) for b in blocks],
        scratch_shapes=[pltpu.SemaphoreType.DMA((n, 7)), pltpu.SemaphoreType.DMA((n, 7)), pltpu.SemaphoreType.DMA((n,))],
        compiler_params=pltpu.CompilerParams(has_side_effects=True),
    )(*blocks)


def _adamw(w, g, m, v):
    m = ADAM_B1 * m + (1.0 - ADAM_B1) * g
    v = ADAM_B2 * v + (1.0 - ADAM_B2) * (g * g)
    m_hat = m / (1.0 - ADAM_B1 ** ADAM_STEP)
    v_hat = v / (1.0 - ADAM_B2 ** ADAM_STEP)
    return -ADAM_LR * (m_hat / (jnp.sqrt(v_hat) + ADAM_EPS) + ADAM_WD * w), m, v


HBM_SPEC = pl.BlockSpec(memory_space=pltpu.HBM)
SEM_SPEC = pl.BlockSpec(memory_space=pltpu.SEMAPHORE)
DATAFLOW = pltpu.SideEffectType.DATAFLOW_SIDE_EFFECTING


def _peers():
    x, y, c = _place()
    return 4 * x + 2 * y + c, [(x ^ (k >> 2), y ^ ((k >> 1) & 1), c ^ (k & 1)) for k in range(1, N_DEV)]


def _peer_index(peer):
    return 4 * peer[0] + 2 * peer[1] + peer[2]


def _zones_with_own(srcs, pieces, name, after=None, dtype=None):
    n = len(srcs)
    extra = [] if after is None else [after]
    dtypes = [s_.dtype if pieces or dtype is None else dtype for s_ in srcs]

    def body(me_ref, *refs):
        outs = refs[n + len(extra):]
        for a in range(n):
            if pieces:
                outs[a][0] = refs[a][0]
            else:
                val = refs[a][...].astype(dtypes[a])
                outs[a][0] = val
                outs[n + a][...] = val

    shapes = [s_.shape[1:] if pieces else s_.shape for s_ in srcs]
    mine = lambda sh: pl.BlockSpec((1,) + sh, lambda i, me_ref: (me_ref[0], 0, 0))
    whole = lambda sh: pl.BlockSpec(sh, lambda i, me_ref: (0, 0))
    in_specs = [mine(sh) if pieces else whole(sh) for sh in shapes]
    out_specs = [mine(sh) for sh in shapes] + ([] if pieces else [whole(sh) for sh in shapes])
    out_shape = [_sds((N_DEV,) + sh, dt) for sh, dt in zip(shapes, dtypes)]
    out_shape += [] if pieces else [_sds(sh, dt) for sh, dt in zip(shapes, dtypes)]
    x, y, c = _place()
    out = pl.pallas_call(
        body, name=name,
        grid_spec=pltpu.PrefetchScalarGridSpec(num_scalar_prefetch=1, grid=(1,), in_specs=in_specs + [ANY_SPEC] * len(extra),
                                               out_specs=out_specs),
        out_shape=out_shape, compiler_params=_params("arbitrary"),
    )((4 * x + 2 * y + c).astype(jnp.int32).reshape(1), *srcs, *extra)
    return out[:n], (list(srcs) if pieces else out[n:])


def _exchange_start(srcs, zones, pieces, name):
    n = len(srcs)

    def body(*refs):
        ins, zs = refs[:n], refs[n:2 * n]
        sems = refs[2 * n:4 * n]
        token = refs[-1]
        me, peers = _peers()
        for peer in peers:
            for a in range(n):
                pltpu.make_async_remote_copy(
                    src_ref=ins[a].at[_peer_index(peer)] if pieces else ins[a], dst_ref=zs[a].at[me],
                    send_sem=sems[2 * a], recv_sem=sems[2 * a + 1], device_id=peer, device_id_type=MESH_IDS).start()
        token[...] = jnp.zeros_like(token)

    hbm = lambda v: pltpu.with_memory_space_constraint(v, pltpu.HBM)
    out = pl.pallas_call(
        body, name=name,
        out_shape=tuple([pltpu.SemaphoreType.DMA(())] * (2 * n) + [pltpu.HBM(v.shape, v.dtype) for v in srcs]
                        + [pltpu.HBM(z.shape, z.dtype) for z in zones] + [_sds((8, LANES))]),
        in_specs=[HBM_SPEC] * (2 * n), out_specs=tuple([SEM_SPEC] * (2 * n) + [HBM_SPEC] * (2 * n) + [VMEM_SPEC]),
        input_output_aliases={i: 2 * n + i for i in range(2 * n)},
        compiler_params=pltpu.CompilerParams(has_side_effects=DATAFLOW),
    )(*[hbm(v) for v in srcs], *[hbm(z) for z in zones])
    return out[:2 * n], out[2 * n:3 * n], out[3 * n:4 * n], out[-1]


def _exchange_wait(sems, srcs, zones, after, name):
    n = len(srcs)
    after = list(after) if isinstance(after, (list, tuple)) else [after]

    def body(*refs):
        ins, zs, sm = refs[:n], refs[n:2 * n], refs[2 * n:4 * n]
        me, peers = _peers()
        for a in range(n):
            seven = zs[a].at[pl.ds(0, N_DEV - 1)]
            cp = pltpu.make_async_remote_copy(src_ref=seven, dst_ref=seven, send_sem=sm[2 * a], recv_sem=sm[2 * a + 1],
                                              device_id=peers[0], device_id_type=MESH_IDS)
            cp.wait_send()
            cp.wait_recv()

    out = pl.pallas_call(
        body, name=name, out_shape=tuple([pltpu.HBM(v.shape, v.dtype) for v in srcs] + [pltpu.HBM(z.shape, z.dtype) for z in zones]),
        in_specs=[HBM_SPEC] * (2 * n) + [SEM_SPEC] * (2 * n) + [ANY_SPEC] * len(after),
        out_specs=tuple([HBM_SPEC] * (2 * n)), input_output_aliases={i: i for i in range(2 * n)},
        compiler_params=pltpu.CompilerParams(has_side_effects=DATAFLOW),
    )(*srcs, *zones, *sems, *after)
    return out[n:]


def _sum_adamw(zone, w, m, v, name):
    _, r, c_ = zone.shape
    rb = 128 if r % 128 == 0 else r

    def body(z_ref, w_ref, m_ref, v_ref, grad_ref, delta_ref, nm_ref, nv_ref):
        total = z_ref[0].astype(F32)
        for d in range(1, N_DEV):
            total = total + z_ref[d].astype(F32)
        grad_ref[...] = total
        delta_ref[...], nm_ref[...], nv_ref[...] = _adamw(w_ref[...], total, m_ref[...], v_ref[...])

    blk = pl.BlockSpec((rb, c_), lambda i: (i, 0))
    return pl.pallas_call(
        body, name=name, grid=(r // rb,), in_specs=[pl.BlockSpec((N_DEV, rb, c_), lambda i: (0, i, 0)), blk, blk, blk],
        out_specs=[blk] * 4, out_shape=[_sds((r, c_))] * 4, compiler_params=_params("parallel"),
    )(zone, w, m, v)


SMALL_NORMS = ("pre_mix_norm", "post_mix_norm", "pre_mlp_norm", "post_mlp_norm")
SMALL_ORDER = SMALL_NORMS + ("fox_out_norm", "gdn_out_norm", "fox_f_bias", "gdn_a_log", "gdn_dt_bias", "gdn_conv_w")
CONV_SLAB_ROWS, CONV_SLAB_LANES = 8, 256


def _small_pack(small):
    def body(n0, n1, n2, n3, fnw_ref, gnw_ref, loss_ref, vec_ref, out_ref):
        out_ref[...] = jnp.zeros_like(out_ref)
        for i, ref in enumerate((n0, n1, n2, n3)):
            out_ref[i:i + 1, :] = ref[...]
        out_ref[4:5, 0:LANES] = fnw_ref[...]
        out_ref[4:5, LANES:2 * LANES] = gnw_ref[...]
        out_ref[4:5, 2 * LANES:3 * LANES] = loss_ref[...]
        out_ref[5:8, 0:LANES] = vec_ref[0:3, :]

    return pl.pallas_call(body, name="small_pack", in_specs=[VMEM_SPEC] * 8, out_specs=VMEM_SPEC,
                          out_shape=_sds((8, D_MODEL)))(*small["norms"], small["fox_out_norm"], small["gdn_out_norm"],
                                                        small["loss"], small["vectors"])


def _conv_slabs(dconv):
    blocks = dconv.reshape(CONV_K, N_DEV, -1).transpose(1, 0, 2)
    blocks = jnp.pad(blocks, ((0, 0), (0, CONV_SLAB_ROWS - CONV_K), (0, CONV_SLAB_LANES - blocks.shape[2])))
    return blocks.reshape(N_DEV * CONV_SLAB_ROWS, CONV_SLAB_LANES)


def _small_update(zone, conv_zone, w, m, v):
    n = len(SMALL_ORDER)
    n_conv = w["gdn_conv_w"].shape[1]

    def body(me_ref, z_ref, zc_ref, *refs):
        params, loss_ref, outs, (tot, totc) = refs[:3 * n], refs[3 * n], refs[3 * n + 1:7 * n + 1], refs[-2:]
        total, total_c = z_ref[0], zc_ref[0]
        for d in range(1, N_DEV):
            total, total_c = total + z_ref[d], total_c + zc_ref[d]
        tot[...] = total
        totc[...] = total_c
        loss_ref[...] = tot[4, 2 * LANES:2 * LANES + 1]
        mine = totc[pl.ds(pl.multiple_of(me_ref[0] * CONV_SLAB_ROWS, CONV_SLAB_ROWS), CONV_SLAB_ROWS), :]
        g = dict(zip(SMALL_NORMS, (tot[0], tot[1], tot[2], tot[3])))
        g.update(fox_out_norm=tot[4, 0:FOX_HEAD_DIM], gdn_out_norm=tot[4, LANES:LANES + GDN_HEAD_DIM],
                 fox_f_bias=tot[5, SM_FF:SM_FF + N_FOX_HEADS], gdn_a_log=tot[6, SM_GA:SM_GA + N_GDN_HEADS],
                 gdn_dt_bias=tot[7, SM_GA:SM_GA + N_GDN_HEADS], gdn_conv_w=mine[0:CONV_K, 0:n_conv])
        for i, name in enumerate(SMALL_ORDER):
            w_ref, m_ref, v_ref = params[3 * i:3 * i + 3]
            outs[4 * i][...] = g[name]
            outs[4 * i + 1][...], outs[4 * i + 2][...], outs[4 * i + 3][...] = _adamw(w_ref[...], g[name], m_ref[...],
                                                                                     v_ref[...])

    x, y, c = _place()
    operands = [a[name] for name in SMALL_ORDER for a in (w, m, v)]
    out = pl.pallas_call(
        body, name="small_update",
        in_specs=[pl.BlockSpec(memory_space=pltpu.SMEM)] + [VMEM_SPEC] * (2 + 3 * n), out_specs=[VMEM_SPEC] * (1 + 4 * n),
        out_shape=[_sds((1,))] + [_sds(w[name].shape) for name in SMALL_ORDER for _ in range(4)],
        scratch_shapes=[pltpu.VMEM(zone.shape[1:], F32), pltpu.VMEM(conv_zone.shape[1:], F32)],
    )((4 * x + 2 * y + c).astype(jnp.int32).reshape(1), zone, conv_zone, *operands)
    return out[0][0], {name: out[1 + 4 * i:5 + 4 * i] for i, name in enumerate(SMALL_ORDER)}


NATIVE_ROWS = ((0, 1536), (1544, 3080), (3088, 3600), (1536, 1544), (3080, 3088))


def _to_aligned_rows(wt_native):
    pad = jnp.zeros((PROJ_W - D_PROJ, wt_native.shape[1]), wt_native.dtype)
    return jnp.concatenate([wt_native[lo:hi] for lo, hi in NATIVE_ROWS] + [pad])


def _from_aligned_rows(gt_al):
    return jnp.concatenate([gt_al[0:1536], gt_al[3584:3592], gt_al[1536:3072], gt_al[3592:3600], gt_al[3072:3584]])


def _cols_from_pieces(p):
    return p.transpose(1, 0, 2).reshape(p.shape[1], -1)


WEIGHT_ORDER = ("pre_mix_norm", "w_in", "fox_f_bias", "fox_out_norm", "gdn_conv_w", "gdn_a_log", "gdn_dt_bias",
                "gdn_out_norm", "w_out", "post_mix_norm", "pre_mlp_norm", "w_up", "w_down", "post_mlp_norm")


def kernel(x, pre_mix_norm, w_in, fox_f_bias, fox_out_norm, gdn_conv_w, gdn_a_log, gdn_dt_bias, gdn_out_norm, w_out, post_mix_norm, pre_mlp_norm, w_up, w_down, post_mlp_norm, loss_target, m_pre_mix_norm, m_w_in, m_fox_f_bias, m_fox_out_norm, m_gdn_conv_w, m_gdn_a_log, m_gdn_dt_bias, m_gdn_out_norm, m_w_out, m_post_mix_norm, m_pre_mlp_norm, m_w_up, m_w_down, m_post_mlp_norm, v_pre_mix_norm, v_w_in, v_fox_f_bias, v_fox_out_norm, v_gdn_conv_w, v_gdn_a_log, v_gdn_dt_bias, v_gdn_out_norm, v_w_out, v_post_mix_norm, v_pre_mlp_norm, v_w_up, v_w_down, v_post_mlp_norm):
    w = dict(pre_mix_norm=pre_mix_norm, w_in=w_in, fox_f_bias=fox_f_bias, fox_out_norm=fox_out_norm,
             gdn_conv_w=gdn_conv_w, gdn_a_log=gdn_a_log, gdn_dt_bias=gdn_dt_bias, gdn_out_norm=gdn_out_norm, w_out=w_out,
             post_mix_norm=post_mix_norm, pre_mlp_norm=pre_mlp_norm, w_up=w_up, w_down=w_down, post_mlp_norm=post_mlp_norm)
    mom = dict(pre_mix_norm=m_pre_mix_norm, w_in=m_w_in, fox_f_bias=m_fox_f_bias, fox_out_norm=m_fox_out_norm,
               gdn_conv_w=m_gdn_conv_w, gdn_a_log=m_gdn_a_log, gdn_dt_bias=m_gdn_dt_bias, gdn_out_norm=m_gdn_out_norm,
               w_out=m_w_out, post_mix_norm=m_post_mix_norm, pre_mlp_norm=m_pre_mlp_norm, w_up=m_w_up, w_down=m_w_down,
               post_mlp_norm=m_post_mlp_norm)
    var = dict(pre_mix_norm=v_pre_mix_norm, w_in=v_w_in, fox_f_bias=v_fox_f_bias, fox_out_norm=v_fox_out_norm,
               gdn_conv_w=v_gdn_conv_w, gdn_a_log=v_gdn_a_log, gdn_dt_bias=v_gdn_dt_bias, gdn_out_norm=v_gdn_out_norm,
               w_out=v_w_out, post_mix_norm=v_post_mix_norm, pre_mlp_norm=v_pre_mlp_norm, w_up=v_w_up, w_down=v_w_down,
               post_mlp_norm=v_post_mlp_norm)

    win_g, conv_g = _all_gather([w_in.T.astype(BF), gdn_conv_w])
    wt_al = _to_aligned_rows(win_g.reshape(D_PROJ, D_MODEL))
    convw = _cols_from_pieces(conv_g)
    gathers, after = {}, win_g
    for name, shards in (("w_out", [w_out]), ("mlp", [w_up, w_down])):
        zones, shards = _zones_with_own(shards, False, "gather_" + name + "_own", after=after, dtype=BF)
        gathers[name] = _exchange_start(shards, zones, False, "gather_" + name + "_start")
        after = gathers[name][3]

    def late_weights(name, after):
        sems, shards, zones, _ = gathers[name]
        got = _exchange_wait(sems, shards, zones, after, "gather_" + name + "_wait")
        if name == "w_out":
            return got[0].reshape(D_MODEL, D_MODEL)
        return got[0], got[1].reshape(D_FF, D_MODEL)

    scatters = {}

    def on_grads(name, g):
        if name == "w_in":
            g = _from_aligned_rows(g).reshape(N_DEV, D_PROJ // N_DEV, D_MODEL)
        srcs = list(g) if name == "mlp" else [g]
        zones, _ = _zones_with_own(srcs, True, "scatter_" + name + "_own")
        scatters[name] = _exchange_start(srcs, zones, True, "scatter_" + name + "_start")
        return scatters[name][3]

    grad_x, small = _local_step(
        x[0], loss_target[0], wt_al, late_weights, on_grads, convw, pre_mix_norm + after[0, 0],
        fox_f_bias, fox_out_norm, gdn_a_log, gdn_dt_bias, gdn_out_norm, post_mix_norm, pre_mlp_norm, post_mlp_norm)
    slabs = [_small_pack(small), _conv_slabs(jnp.concatenate(small["conv"], axis=1))]
    zones, slabs = _zones_with_own(slabs, False, "small_own")
    scatters["small"] = _exchange_start(slabs, zones, False, "small_start")

    grads, delta, new_m, new_v = {}, {}, {}, {}
    after = scatters["small"][3]
    for name, members in (("mlp", ("w_up", "w_down")), ("w_out", ("w_out",)), ("small", ()), ("w_in", ("w_in",))):
        sems, srcs, zones, _ = scatters[name]
        zones = _exchange_wait(sems, srcs, zones, after, "scatter_" + name + "_wait")
        if name == "small":
            loss, updated = _small_update(zones[0], zones[1], w, mom, var)
            for n, res in updated.items():
                grads[n], delta[n], new_m[n], new_v[n] = res
            after = grads["pre_mix_norm"]
        for n, zone in zip(members, zones):
            if n == "w_in":
                res = _sum_adamw(zone, w[n].T, mom[n].T, var[n].T, "adamw_" + n)
                grads[n], delta[n], new_m[n], new_v[n] = [r.T for r in res]
            else:
                grads[n], delta[n], new_m[n], new_v[n] = _sum_adamw(zone, w[n], mom[n], var[n], "adamw_" + n)
        if members:
            after = [grads[n] for n in members]

    return (loss, grad_x[None], *[grads[n] for n in WEIGHT_ORDER], *[delta[n] for n in WEIGHT_ORDER],
            *[new_m[n] for n in WEIGHT_ORDER], *[new_v[n] for n in WEIGHT_ORDER])
```

```python
import jax
import jax.numpy as jnp
from jax import lax
from jax.experimental import pallas as pl
from jax.experimental.pallas import tpu as pltpu

F32 = jnp.float32
BF = jnp.bfloat16

D_MODEL = 1024
N_FOX_HEADS, FOX_HEAD_DIM = 8, 64
N_GDN_HEADS, GDN_HEAD_DIM = 4, 128
D_FOX = N_FOX_HEADS * FOX_HEAD_DIM
D_GDN = N_GDN_HEADS * GDN_HEAD_DIM
CHUNK = 64
CONV_K = 4
D_FF = 4 * D_MODEL
EPS = 1e-6
D_PROJ = 3600
N_DEV = 8

PROJ_W = 3712
COL_FOX, COL_GDN, COL_GZ, COL_SMALL = 0, 1536, 3072, 3584
LANES = 128
SM_FF, SM_GB, SM_GA = 0, 8, 12

ADAM_LR, ADAM_B1, ADAM_B2, ADAM_EPS, ADAM_WD, ADAM_STEP = 0.001, 0.9, 0.999, 1e-08, 0.01, 10

TOKEN_BLOCK = 256
MATMUL_BLOCK = 512
FOX_SCALE = FOX_HEAD_DIM ** -0.5
GDN_QSCALE = GDN_HEAD_DIM ** -0.5
NEG_BIG = -1e30
VMEM_LIMIT = 56 * 1024 * 1024

VMEM_SPEC = pl.BlockSpec(memory_space=pltpu.VMEM)


def _sds(shape, dtype=F32):
    return jax.ShapeDtypeStruct(shape, dtype)


def _params(*sem):
    return pltpu.CompilerParams(dimension_semantics=sem if sem else None, vmem_limit_bytes=VMEM_LIMIT)


def _mm(a, b):
    return jnp.dot(a.astype(BF), b.astype(BF), preferred_element_type=F32)


def _mm_nt(a, b):
    return lax.dot_general(a.astype(BF), b.astype(BF), (((1,), (1,)), ((), ())), preferred_element_type=F32)


def _mm_tn(a, b):
    return lax.dot_general(a.astype(BF), b.astype(BF), (((0,), (0,)), ((), ())), preferred_element_type=F32)


def _sigmoid(x):
    return 1.0 / (1.0 + jnp.exp(-x))


def _softplus(x):
    return jnp.maximum(x, 0.0) + jnp.log1p(jnp.exp(-jnp.abs(x)))


def _iota(shape, dim):
    return lax.broadcasted_iota(jnp.int32, shape, dim)


def _shift_down(x, s, row):
    return jnp.where(row >= s, pltpu.roll(x, s, 0), 0.0)


def _shift_up(x, s, row):
    n = x.shape[0]
    return jnp.where(row < n - s, pltpu.roll(x, n - s, 0), 0.0)


def _norm_proj(x, nw, wt_al):
    t = x.shape[0]

    def body(x_ref, nw_ref, w_ref, proj_ref, h_ref):
        xv = x_ref[...]
        r = lax.rsqrt(jnp.mean(xv * xv, axis=-1, keepdims=True) + EPS)
        h = (xv * r * nw_ref[...]).astype(BF)
        h_ref[...] = h
        proj_ref[...] = lax.dot_general(h, w_ref[...], (((1,), (1,)), ((), ())), preferred_element_type=F32)

    tm = min(MATMUL_BLOCK, t)
    return pl.pallas_call(
        body, name="norm_proj", grid=(t // tm,),
        in_specs=[pl.BlockSpec((tm, D_MODEL), lambda i: (i, 0)), pl.BlockSpec((1, D_MODEL), lambda i: (0, 0)),
                  pl.BlockSpec((PROJ_W, D_MODEL), lambda i: (0, 0))],
        out_specs=[pl.BlockSpec((tm, PROJ_W), lambda i: (i, 0)), pl.BlockSpec((tm, D_MODEL), lambda i: (i, 0))],
        out_shape=[_sds((t, PROJ_W)), _sds((t, D_MODEL), BF)],
        compiler_params=_params("parallel"),
    )(x, nw, wt_al)


def _lane_column(x, lane):
    return jnp.sum(jnp.where(_iota((1, LANES), 1) == lane, x, 0.0), axis=-1, keepdims=True)


def _small_prep(proj, fb, al, dtb):
    t = proj.shape[0]

    def body(sm_ref, fb_ref, al_ref, dtb_ref, cumt_ref, beta_ref, g_ref):
        s = sm_ref[...]
        z = s + fb_ref[...]
        cum = jnp.minimum(z, 0.0) - jnp.log1p(jnp.exp(-jnp.abs(z)))
        row = _iota((t, LANES), 0)
        step = 1
        while step < t:
            cum = cum + _shift_down(cum, step, row)
            step *= 2
        cumt_ref[...] = cum.T
        beta_ref[...] = _sigmoid(s)
        g_ref[...] = -jnp.exp(al_ref[...]) * _softplus(s + dtb_ref[...])

    vec = pl.BlockSpec((1, LANES), lambda i: (0, 0))
    tok = pl.BlockSpec((t, LANES), lambda i: (0, 0))
    return pl.pallas_call(
        body, name="small_prep", grid=(1,),
        in_specs=[pl.BlockSpec((t, LANES), lambda i: (0, COL_SMALL // LANES)), vec, vec, vec],
        out_specs=[pl.BlockSpec((LANES, t), lambda i: (0, 0)), tok, tok],
        out_shape=[_sds((LANES, t)), _sds((t, LANES)), _sds((t, LANES))],
        compiler_params=_params("arbitrary"),
    )(proj, fb, al, dtb)


def _fox_stack(x, first):
    return jnp.concatenate([jnp.where(first, x, 0.0), jnp.where(first, 0.0, x)], axis=0).astype(BF)


def _fox_unstack(y, first):
    n = y.shape[0] // 2
    return jnp.where(first, y[:n], y[n:])


def _fox_logits(q2_i, kb, cumt_ref, pair, i, tq):
    klen = (i + 1) * tq
    s = lax.dot_general(q2_i, kb[:klen], (((1,), (1,)), ((), ())), preferred_element_type=F32)
    upper = _iota((2 * tq, 1), 0) < tq
    s = s - jnp.where(upper, cumt_ref[pl.ds(2 * pair, 1), 0:klen], cumt_ref[pl.ds(2 * pair + 1, 1), 0:klen])
    causal = _iota((2 * tq, tq), 1) <= _iota((2 * tq, tq), 0) % tq
    parts = [(s[:, :klen - tq], 0, klen - tq)] if i else []
    return parts + [(jnp.where(causal, s[:, klen - tq:], NEG_BIG), klen - tq, klen)]


def _fox_fwd(proj, cumt, fnw):
    t = proj.shape[0]
    tq = min(TOKEN_BLOCK, t // 2)
    nq = t // tq

    def body(q_ref, k_ref, v_ref, cumt_ref, fnw_ref, o_ref, lse_ref, fn_ref):
        j = pl.program_id(0)
        first = _iota((1, LANES), 1) < FOX_HEAD_DIM
        kb = k_ref[...].astype(BF)
        vb = v_ref[...].astype(BF)
        for i in range(nq):
            rows = slice(i * tq, (i + 1) * tq)
            q2 = _fox_stack(q_ref[rows, :] * FOX_SCALE, first)
            parts = _fox_logits(q2, kb, cumt_ref, j, i, tq)
            m = jnp.max(parts[-1][0], axis=-1, keepdims=True)
            if i:
                m = jnp.maximum(m, jnp.max(parts[0][0], axis=-1, keepdims=True))
            l = jnp.zeros((2 * tq, 1), F32)
            o = jnp.zeros((2 * tq, LANES), F32)
            for s, lo, hi in parts:
                p = jnp.exp(s - m)
                l = l + jnp.sum(p, axis=-1, keepdims=True)
                o = o + jnp.dot(p.astype(BF), vb[lo:hi], preferred_element_type=F32)
            o_acc = _fox_unstack(o / l, first)
            lse_acc = _fox_unstack(jnp.broadcast_to(m + jnp.log(l), (2 * tq, LANES)), first)
            o_ref[rows, :] = o_acc
            lse_ref[rows, :] = lse_acc
            o2 = o_acc * o_acc
            s0 = jnp.sum(jnp.where(first, o2, 0.0), axis=-1, keepdims=True)
            s1 = jnp.sum(jnp.where(first, 0.0, o2), axis=-1, keepdims=True)
            r = lax.rsqrt(jnp.where(first, s0, s1) * (1.0 / FOX_HEAD_DIM) + EPS)
            fn_ref[rows, :] = (o_acc * r * fnw_ref[...]).astype(BF)

    blk = lambda off: pl.BlockSpec((t, LANES), lambda j: (0, off + j))
    return pl.pallas_call(
        body, name="fox_fwd", grid=(N_FOX_HEADS // 2,),
        in_specs=[blk(0), blk(4), blk(8), pl.BlockSpec((LANES, t), lambda j: (0, 0)),
                  pl.BlockSpec((1, LANES), lambda j: (0, 0))],
        out_specs=[blk(0), blk(0), blk(0)],
        out_shape=[_sds((t, D_FOX)), _sds((t, D_FOX)), _sds((t, D_FOX), BF)],
        compiler_params=_params("parallel"),
    )(proj, proj, proj, cumt, fnw)


def _fox_bwd(proj, cumt, lse, o, do):
    t = proj.shape[0]
    tq = min(TOKEN_BLOCK, t // 2)
    nq = t // tq

    def body(q_ref, k_ref, v_ref, cumt_ref, lse_ref, o_ref, do_ref,
             dq_ref, dk_ref, dv_ref, dcq_ref, dckt_ref, dk_s, dv_s):
        j = pl.program_id(0)

        @pl.when(j == 0)
        def _():
            dcq_ref[...] = jnp.zeros_like(dcq_ref)
            dckt_ref[...] = jnp.zeros_like(dckt_ref)

        lane = _iota((1, LANES), 1)

        first = _iota((1, LANES), 1) < FOX_HEAD_DIM
        kb = k_ref[...].astype(BF)
        vb = v_ref[...].astype(BF)
        dk_s[...] = jnp.zeros_like(dk_s)
        dv_s[...] = jnp.zeros_like(dv_s)
        for i in range(nq):
            rows = slice(i * tq, (i + 1) * tq)
            do_i = do_ref[rows, :]
            prod = do_i * o_ref[rows, :]
            lse_i = lse_ref[rows, :]
            q2 = _fox_stack(q_ref[rows, :] * FOX_SCALE, first)
            do2 = _fox_stack(do_i, first)
            delta = jnp.concatenate([jnp.sum(jnp.where(first, prod, 0.0), axis=-1, keepdims=True),
                                     jnp.sum(jnp.where(first, 0.0, prod), axis=-1, keepdims=True)], axis=0)
            lse2 = jnp.concatenate([lse_i[:, 0:1], lse_i[:, FOX_HEAD_DIM:FOX_HEAD_DIM + 1]], axis=0)
            dq2 = jnp.zeros((2 * tq, LANES), F32)
            dcq2 = jnp.zeros((2 * tq, 1), F32)
            for s, lo, hi in _fox_logits(q2, kb, cumt_ref, j, i, tq):
                p = jnp.exp(s - lse2)
                ds = p * (_mm_nt(do2, vb[lo:hi]) - delta)
                dsb = ds.astype(BF)
                dq2 = dq2 + jnp.dot(dsb, kb[lo:hi], preferred_element_type=F32)
                dk_s[lo:hi, :] += _mm_tn(dsb, q2)
                dv_s[lo:hi, :] += _mm_tn(p, do2)
                dcq2 = dcq2 + jnp.sum(ds, axis=-1, keepdims=True)
                dckt_ref[pl.ds(2 * j, 1), lo:hi] += jnp.sum(ds[:tq], axis=0, keepdims=True)
                dckt_ref[pl.ds(2 * j + 1, 1), lo:hi] += jnp.sum(ds[tq:], axis=0, keepdims=True)
            dq_ref[rows, :] = _fox_unstack(dq2, first) * FOX_SCALE
            dcq_ref[rows, :] += jnp.where(lane == 2 * j, dcq2[:tq], jnp.where(lane == 2 * j + 1, dcq2[tq:], 0.0))
        dk_ref[...] = dk_s[...]
        dv_ref[...] = dv_s[...]

    blk = lambda off: pl.BlockSpec((t, LANES), lambda j: (0, off + j))
    rows128 = pl.BlockSpec((LANES, t), lambda j: (0, 0))
    return pl.pallas_call(
        body, name="fox_bwd", grid=(N_FOX_HEADS // 2,),
        in_specs=[blk(0), blk(4), blk(8), rows128, blk(0), blk(0), blk(0)],
        out_specs=[blk(0), blk(0), blk(0), pl.BlockSpec((t, LANES), lambda j: (0, 0)), rows128],
        out_shape=[_sds((t, D_FOX))] * 3 + [_sds((t, LANES)), _sds((LANES, t))],
        scratch_shapes=[pltpu.VMEM((t, LANES), F32), pltpu.VMEM((t, LANES), F32)],
        compiler_params=_params("arbitrary"),
    )(proj, proj, proj, cumt, lse, o, do)


def _conv(x, w, row):
    return (w[3:4, :] * x + w[2:3, :] * _shift_down(x, 1, row) + w[1:2, :] * _shift_down(x, 2, row)
            + w[0:1, :] * _shift_down(x, 3, row))


def _chunk_decay(gc_c):
    gi = gc_c[:, 0:CHUNK]
    gj = gc_c.T[0:CHUNK, :]
    ri = _iota((CHUNK, CHUNK), 0)
    cj = _iota((CHUNK, CHUNK), 1)
    return jnp.where(ri >= cj, jnp.exp(jnp.minimum(gi - gj, 0.0)), 0.0), ri > cj


def _gdn_specs(t):
    col = lambda off: pl.BlockSpec((t, LANES), lambda h: (0, off + h))
    cw = lambda off: pl.BlockSpec((CONV_K, LANES), lambda h: (0, off + h))
    mat = pl.BlockSpec((1, t // CHUNK, CHUNK, CHUNK), lambda h: (h, 0, 0, 0))
    return col, cw, mat


def _gdn_prep(proj, convw, beta, g):
    t = proj.shape[0]
    nch = t // CHUNK

    def body(xq_ref, xk_ref, xv_ref, wq_ref, wk_ref, wv_ref, beta_ref, g_ref,
             qn_ref, kn_ref, cv_ref, gc_ref, be_ref, m_ref, a_ref):
        row = _iota((t, LANES), 0)
        hd = pl.program_id(0)
        be_ref[...] = jnp.broadcast_to(_lane_column(beta_ref[...], SM_GB + hd), (t, LANES))

        def act(x_ref, w_ref):
            y = _conv(x_ref[...], w_ref[...], row)
            return y * _sigmoid(y)

        cq = act(xq_ref, wq_ref)
        ck = act(xk_ref, wk_ref)
        cv_ref[...] = act(xv_ref, wv_ref)
        qn_ref[...] = cq * lax.rsqrt(jnp.sum(cq * cq, axis=-1, keepdims=True) + EPS) * GDN_QSCALE
        kn_ref[...] = ck * lax.rsqrt(jnp.sum(ck * ck, axis=-1, keepdims=True) + EPS)
        gc = jnp.broadcast_to(_lane_column(g_ref[...], SM_GA + hd), (t, LANES))
        pos = row % CHUNK
        step = 1
        while step < CHUNK:
            gc = gc + jnp.where(pos >= step, pltpu.roll(gc, step, 0), 0.0)
            step *= 2
        gc_ref[...] = gc

        def chunk(n, carry):
            sl = pl.ds(pl.multiple_of(n * CHUNK, CHUNK), CHUNK)
            k_c = kn_ref[sl, :]
            decay, strict = _chunk_decay(gc_ref[sl, :])
            m_ref[0, n] = jnp.where(strict, _mm_nt(k_c * be_ref[sl, :], k_c) * decay, 0.0)
            a_ref[0, n] = _mm_nt(qn_ref[sl, :], k_c) * decay
            return carry

        lax.fori_loop(0, nch, chunk, 0)

    col, cw, mat = _gdn_specs(t)
    return pl.pallas_call(
        body, name="gdn_prep", grid=(N_GDN_HEADS,),
        in_specs=[col(12), col(16), col(20), cw(0), cw(4), cw(8)] + [pl.BlockSpec((t, LANES), lambda h: (0, 0))] * 2,
        out_specs=[col(0), col(0), col(0), col(0), col(0), mat, mat],
        out_shape=[_sds((t, D_GDN))] * 5 + [_sds((N_GDN_HEADS, nch, CHUNK, CHUNK))] * 2,
        compiler_params=_params("parallel"),
    )(proj, proj, proj, convw, convw, convw, beta, g)


def _tri_inverse(m3):
    assert m3.shape == (LANES, CHUNK, CHUNK)

    def body(m_ref, t_ref, ms, ts):
        for i in range(CHUNK):
            ms[i * CHUNK:(i + 1) * CHUNK, :] = m_ref[:, i, :].T
        cidx = _iota((CHUNK, LANES), 0)

        def outer(i, carry):
            def inner(jj, acc):
                mrow = ms[pl.ds(i * CHUNK + jj, 1), :]
                return acc - mrow * ts[pl.ds(pl.multiple_of(jj * CHUNK, CHUNK), CHUNK), :]

            acc = lax.fori_loop(0, i, inner, jnp.where(cidx == i, 1.0, 0.0).astype(F32))
            ts[pl.ds(pl.multiple_of(i * CHUNK, CHUNK), CHUNK), :] = acc
            return carry

        lax.fori_loop(0, CHUNK, outer, 0)
        for i in range(CHUNK):
            t_ref[:, i, :] = ts[i * CHUNK:(i + 1) * CHUNK, :].T

    return pl.pallas_call(
        body, name="tri_inverse", in_specs=[VMEM_SPEC], out_specs=VMEM_SPEC,
        out_shape=_sds((LANES, CHUNK, CHUNK)),
        scratch_shapes=[pltpu.VMEM((CHUNK * CHUNK, LANES), F32), pltpu.VMEM((CHUNK * CHUNK, LANES), F32)],
        compiler_params=_params(),
    )(m3)


def _gdn_chunk_terms(q, k, v, b, gcc):
    eg = jnp.exp(gcc)
    last = gcc[CHUNK - 1:CHUNK, :]
    egl = jnp.exp(last - gcc)
    gl = jnp.exp(last)
    kb = k * b
    return eg, egl, gl, kb, v * b, kb * eg, q * eg, k * egl


GDN_BLOCK_CHUNKS = 4


def _gdn_block_specs(t, reverse):
    cb = GDN_BLOCK_CHUNKS
    nb = t // (cb * CHUNK)
    idx = (lambda i: nb - 1 - i) if reverse else (lambda i: i)
    tok = pl.BlockSpec((cb * CHUNK, D_GDN), lambda i: (idx(i), 0))
    mat = pl.BlockSpec((N_GDN_HEADS, cb, CHUNK, CHUNK), lambda i: (0, idx(i), 0, 0))
    state = pl.BlockSpec((N_GDN_HEADS, cb, GDN_HEAD_DIM, GDN_HEAD_DIM), lambda i: (0, idx(i), 0, 0))
    return nb, tok, mat, state


def _gdn_scan(qn, kn, cv, be, gc, tinv, amat):
    t = qn.shape[0]
    nch = t // CHUNK

    def body(q_ref, k_ref, v_ref, b_ref, gc_ref, t_ref, a_ref, o_ref, sall_ref, vn_ref, s_scr):
        @pl.when(pl.program_id(0) == 0)
        def _():
            s_scr[...] = jnp.zeros_like(s_scr)

        heads = range(N_GDN_HEADS)
        cols = [slice(hd * LANES, (hd + 1) * LANES) for hd in heads]
        s = [s_scr[hd] for hd in heads]
        for cc in range(GDN_BLOCK_CHUNKS):
            rs = slice(cc * CHUNK, (cc + 1) * CHUNK)
            terms = [_gdn_chunk_terms(q_ref[rs, cs], k_ref[rs, cs], v_ref[rs, cs], b_ref[rs, cs], gc_ref[rs, cs])
                     for cs in cols]
            for hd in heads:
                sall_ref[hd, cc] = s[hd]
            uw = [_mm(t_ref[hd, cc], jnp.concatenate([terms[hd][4], terms[hd][5]], axis=1)) for hd in heads]
            ws_qs = [_mm(jnp.concatenate([uw[hd][:, LANES:], terms[hd][6]], axis=0), s[hd]) for hd in heads]
            vn = [uw[hd][:, :LANES] - ws_qs[hd][:CHUNK] for hd in heads]
            a_vn = [_mm(a_ref[hd, cc], vn[hd]) for hd in heads]
            kd_vn = [_mm_tn(terms[hd][7], vn[hd]) for hd in heads]
            for hd in heads:
                vn_ref[rs, cols[hd]] = vn[hd]
                o_ref[rs, cols[hd]] = ws_qs[hd][CHUNK:] + a_vn[hd]
                s[hd] = s[hd] * terms[hd][2] + kd_vn[hd]
        for hd in heads:
            s_scr[hd] = s[hd]

    nb, tok, mat, state = _gdn_block_specs(t, False)
    return pl.pallas_call(
        body, name="gdn_scan", grid=(nb,),
        in_specs=[tok] * 5 + [mat, mat], out_specs=[tok, state, tok],
        out_shape=[_sds((t, D_GDN)), _sds((N_GDN_HEADS, nch, GDN_HEAD_DIM, GDN_HEAD_DIM)), _sds((t, D_GDN))],
        scratch_shapes=[pltpu.VMEM((N_GDN_HEADS, GDN_HEAD_DIM, GDN_HEAD_DIM), F32)],
        compiler_params=_params("arbitrary"),
    )(qn, kn, cv, be, gc, tinv, amat)


def _gdn_bwd(qn, kn, cv, be, gc, tinv, amat, s_all, vn_all, do):
    t = qn.shape[0]

    def body(q_ref, k_ref, v_ref, b_ref, gc_ref, t_ref, a_ref, sall_ref, vn_ref, do_ref,
             dq_ref, dk_ref, dv_ref, db_ref, dg_ref, ds_scr):
        @pl.when(pl.program_id(0) == 0)
        def _():
            ds_scr[...] = jnp.zeros_like(ds_scr)

        lastrow = _iota((CHUNK, LANES), 0) == CHUNK - 1
        heads = range(N_GDN_HEADS)
        cols = [slice(hd * LANES, (hd + 1) * LANES) for hd in heads]
        each = lambda fn: [fn(hd) for hd in heads]
        rows_cat = lambda x, y: jnp.concatenate([x, y], axis=0)
        lane_cat = lambda x, y: jnp.concatenate([x, y], axis=1)
        dsp = each(lambda hd: ds_scr[hd])
        for cc in reversed(range(GDN_BLOCK_CHUNKS)):
            rs = slice(cc * CHUNK, (cc + 1) * CHUNK)
            q = each(lambda hd: q_ref[rs, cols[hd]])
            k = each(lambda hd: k_ref[rs, cols[hd]])
            v = each(lambda hd: v_ref[rs, cols[hd]])
            b = each(lambda hd: b_ref[rs, cols[hd]])
            gcc = each(lambda hd: gc_ref[rs, cols[hd]])
            do_c = each(lambda hd: do_ref[rs, cols[hd]])
            vn = each(lambda hd: vn_ref[rs, cols[hd]])
            tn = each(lambda hd: t_ref[hd, cc])
            st = each(lambda hd: sall_ref[hd, cc])
            terms = each(lambda hd: _gdn_chunk_terms(q[hd], k[hd], v[hd], b[hd], gcc[hd]))
            eg, egl, gl, kb, vb, kbg, qd, kd = [[terms[hd][i] for hd in heads] for i in range(8)]
            w = each(lambda hd: _mm(tn[hd], kbg[hd]))
            a_do = each(lambda hd: _mm_tn(a_ref[hd, cc], do_c[hd]))
            kd_ds = each(lambda hd: _mm(kd[hd], dsp[hd]))
            da = each(lambda hd: _mm_nt(do_c[hd], vn[hd]))
            dkd = each(lambda hd: _mm_nt(vn[hd], dsp[hd]))
            by_k = each(lambda hd: _mm_nt(rows_cat(kb[hd], q[hd]), k[hd]))
            dgl = each(lambda hd: jnp.sum(jnp.sum(dsp[hd] * st[hd], axis=-1, keepdims=True), axis=0, keepdims=True))
            dvn = each(lambda hd: a_do[hd] + kd_ds[hd])
            do_dvn = each(lambda hd: rows_cat(do_c[hd], dvn[hd]))
            by_s = each(lambda hd: _mm_nt(do_dvn[hd], st[hd]))
            dqd = each(lambda hd: by_s[hd][:CHUNK])
            dvn_dw = each(lambda hd: lane_cat(dvn[hd], -by_s[hd][CHUNK:]))
            dsp = each(lambda hd: _mm_tn(rows_cat(qd[hd], -w[hd]), do_dvn[hd]) + gl[hd] * dsp[hd])
            dt = each(lambda hd: _mm_nt(dvn_dw[hd], lane_cat(vb[hd], kbg[hd])))
            by_t = each(lambda hd: _mm_tn(tn[hd], dvn_dw[hd]))
            tt_dt = each(lambda hd: _mm_tn(tn[hd], dt[hd]))
            dm_raw = each(lambda hd: _mm_nt(tt_dt[hd], tn[hd]))
            masks = each(lambda hd: _chunk_decay(gcc[hd]))
            dkk = each(lambda hd: jnp.where(masks[hd][1], -dm_raw[hd], 0.0) * masks[hd][0])
            dqk = each(lambda hd: da[hd] * masks[hd][0])
            dqk_dkk = each(lambda hd: rows_cat(dqk[hd], dkk[hd]))
            on_k = each(lambda hd: _mm(dqk_dkk[hd], k[hd]))
            dk_mm = each(lambda hd: _mm_tn(dqk_dkk[hd], rows_cat(q[hd], kb[hd])))
            for hd in heads:
                cs = cols[hd]
                dvb, dkbg = by_t[hd][:, :LANES], by_t[hd][:, LANES:]
                gmat = dkk[hd] * by_k[hd][:CHUNK] + dqk[hd] * by_k[hd][CHUNK:]
                dq_ref[rs, cs] = dqd[hd] * eg[hd] + on_k[hd][:CHUNK]
                dkb = on_k[hd][CHUNK:] + dkbg * eg[hd]
                dk_ref[rs, cs] = dkd[hd] * egl[hd] + dk_mm[hd] + dkb * b[hd]
                db = jnp.sum(dkb * k[hd], axis=-1, keepdims=True) + jnp.sum(dvb * v[hd], axis=-1, keepdims=True)
                db_ref[rs, cs] = jnp.broadcast_to(db, (CHUNK, LANES))
                dv_ref[rs, cs] = dvb * b[hd]
                dkd_kd = jnp.sum(dkd[hd] * kd[hd], axis=-1, keepdims=True)
                col_sums = jnp.sum(lane_cat(gmat, jnp.zeros_like(gmat)).T, axis=-1, keepdims=True)
                dgc = (jnp.sum(gmat, axis=-1, keepdims=True) - col_sums[:CHUNK]
                       + jnp.sum(dqd[hd] * qd[hd], axis=-1, keepdims=True)
                       + jnp.sum(dkbg * kbg[hd], axis=-1, keepdims=True) - dkd_kd)
                extra = jnp.sum(dkd_kd, axis=0, keepdims=True) + dgl[hd] * gl[hd]
                dg_ref[rs, cs] = dgc + jnp.where(lastrow, extra, 0.0)
        for hd in heads:
            ds_scr[hd] = dsp[hd]
        dg = dg_ref[...]
        row = _iota(dg.shape, 0)
        pos = row % CHUNK
        step = 1
        while step < CHUNK:
            dg = dg + jnp.where(pos < CHUNK - step, pltpu.roll(dg, dg.shape[0] - step, 0), 0.0)
            step *= 2
        dg_ref[...] = dg

    nb, tok, mat, state = _gdn_block_specs(t, True)
    return pl.pallas_call(
        body, name="gdn_bwd", grid=(nb,),
        in_specs=[tok] * 5 + [mat, mat, state, tok, tok], out_specs=[tok] * 5, out_shape=[_sds((t, D_GDN))] * 5,
        scratch_shapes=[pltpu.VMEM((N_GDN_HEADS, GDN_HEAD_DIM, GDN_HEAD_DIM), F32)],
        compiler_params=_params("arbitrary"),
    )(qn, kn, cv, be, gc, tinv, amat, s_all, vn_all, do)


def _gdn_bwd_conv(proj, convw, dqn, dkn, dcv):
    t = proj.shape[0]

    def body(xq_ref, xk_ref, xv_ref, wq_ref, wk_ref, wv_ref, dq_ref, dk_ref, dv_ref,
             dxq_ref, dxk_ref, dxv_ref, dwq_ref, dwk_ref, dwv_ref):
        row = _iota((t, LANES), 0)

        def one(x_ref, w_ref, d_ref, dx_ref, dw_ref, scale):
            x = x_ref[...]
            w = w_ref[...]
            y = _conv(x, w, row)
            sg = _sigmoid(y)
            dc = d_ref[...]
            if scale is not None:
                c = y * sg
                r = lax.rsqrt(jnp.sum(c * c, axis=-1, keepdims=True) + EPS)
                ch = c * r
                dc = scale * r * (dc - ch * jnp.sum(dc * ch, axis=-1, keepdims=True))
            dy = dc * sg * (1.0 + y * (1.0 - sg))
            dx_ref[...] = (w[3:4, :] * dy + w[2:3, :] * _shift_up(dy, 1, row) + w[1:2, :] * _shift_up(dy, 2, row)
                           + w[0:1, :] * _shift_up(dy, 3, row))
            for jj in range(CONV_K):
                xs = x if jj == CONV_K - 1 else _shift_down(x, CONV_K - 1 - jj, row)
                dw_ref[jj:jj + 1, :] = jnp.sum(dy * xs, axis=0, keepdims=True)

        one(xq_ref, wq_ref, dq_ref, dxq_ref, dwq_ref, GDN_QSCALE)
        one(xk_ref, wk_ref, dk_ref, dxk_ref, dwk_ref, 1.0)
        one(xv_ref, wv_ref, dv_ref, dxv_ref, dwv_ref, None)

    col, cw, _ = _gdn_specs(t)
    return pl.pallas_call(
        body, name="gdn_bwd_conv", grid=(N_GDN_HEADS,),
        in_specs=[col(12), col(16), col(20), cw(0), cw(4), cw(8), col(0), col(0), col(0)],
        out_specs=[col(0), col(0), col(0), cw(0), cw(0), cw(0)],
        out_shape=[_sds((t, D_GDN))] * 3 + [_sds((CONV_K, D_GDN))] * 3,
        compiler_params=_params("parallel"),
    )(proj, proj, proj, convw, convw, convw, dqn, dkn, dcv)


def _mix_out(fox_n, gdn_o, proj, gnw, w_out, x, pmw, plw):
    t = x.shape[0]
    tm = min(MATMUL_BLOCK, t)

    def body(fn_ref, go_ref, gz_ref, gnw_ref, w_ref, x_ref, pmw_ref, plw_ref, x1_ref, h2_ref, mixed_ref, omix_ref,
             h2t_ref):
        omix_ref[:, 0:D_FOX] = fn_ref[...]
        for hd in range(N_GDN_HEADS):
            cs = slice(hd * LANES, (hd + 1) * LANES)
            go = go_ref[:, cs]
            r = lax.rsqrt(jnp.mean(go * go, axis=-1, keepdims=True) + EPS)
            gz = gz_ref[:, cs]
            omix_ref[:, D_FOX + hd * LANES:D_FOX + (hd + 1) * LANES] = (
                go * r * gnw_ref[...] * (gz * _sigmoid(gz))).astype(BF)
        mixed = jnp.dot(omix_ref[...], w_ref[...], preferred_element_type=F32)
        mixed_ref[...] = mixed
        r2 = lax.rsqrt(jnp.mean(mixed * mixed, axis=-1, keepdims=True) + EPS)
        x1 = x_ref[...] + mixed * r2 * pmw_ref[...]
        x1_ref[...] = x1
        r3 = lax.rsqrt(jnp.mean(x1 * x1, axis=-1, keepdims=True) + EPS)
        h2 = x1 * r3 * plw_ref[...]
        h2_ref[...] = h2.astype(BF)
        h2t_ref[...] = h2.T.astype(BF)

    tok = lambda w: pl.BlockSpec((tm, w), lambda i: (i, 0))
    vec = lambda w: pl.BlockSpec((1, w), lambda i: (0, 0))
    return pl.pallas_call(
        body, name="mix_out", grid=(t // tm,),
        in_specs=[tok(D_FOX), tok(D_GDN), pl.BlockSpec((tm, D_GDN), lambda i: (i, COL_GZ // D_GDN)), vec(LANES),
                  pl.BlockSpec((D_MODEL, D_MODEL), lambda i: (0, 0)), tok(D_MODEL), vec(D_MODEL), vec(D_MODEL)],
        out_specs=[tok(D_MODEL)] * 4 + [pl.BlockSpec((D_MODEL, tm), lambda i: (0, i))],
        out_shape=[_sds((t, D_MODEL)), _sds((t, D_MODEL), BF), _sds((t, D_MODEL)), _sds((t, D_MODEL), BF),
                   _sds((D_MODEL, t), BF)],
        compiler_params=_params("parallel"),
    )(fox_n, gdn_o, proj, gnw, w_out, x, pmw, plw)


def _out_bwd(dmixed, w_out, o_fox, gdn_o, proj, fnw, gnw):
    t = dmixed.shape[0]
    tm = min(MATMUL_BLOCK, t)

    def body(dm_ref, w_ref, of_ref, go_ref, gz_ref, fnw_ref, gnw_ref, dof_ref, dgo_ref, dgz_ref, dfw_ref, dgw_ref):
        i = pl.program_id(0)

        @pl.when(i == 0)
        def _():
            dfw_ref[...] = jnp.zeros_like(dfw_ref)
            dgw_ref[...] = jnp.zeros_like(dgw_ref)

        domix = _mm_nt(dm_ref[...], w_ref[...])
        first = _iota((1, LANES), 1) < FOX_HEAD_DIM
        dfw = jnp.zeros((1, LANES), F32)
        dgw = jnp.zeros((1, LANES), F32)
        for pr in range(N_FOX_HEADS // 2):
            cs = slice(pr * LANES, (pr + 1) * LANES)
            o = of_ref[:, cs]
            dfn = domix[:, cs]
            o2 = o * o
            s0 = jnp.sum(jnp.where(first, o2, 0.0), axis=-1, keepdims=True)
            s1 = jnp.sum(jnp.where(first, 0.0, o2), axis=-1, keepdims=True)
            r = lax.rsqrt(jnp.where(first, s0, s1) * (1.0 / FOX_HEAD_DIM) + EPS)
            oh = o * r
            dfw = dfw + jnp.sum(dfn * oh, axis=0, keepdims=True)
            doh = dfn * fnw_ref[...]
            pr_ = doh * oh
            m0 = jnp.sum(jnp.where(first, pr_, 0.0), axis=-1, keepdims=True)
            m1 = jnp.sum(jnp.where(first, 0.0, pr_), axis=-1, keepdims=True)
            dof_ref[:, cs] = r * (doh - oh * jnp.where(first, m0, m1) * (1.0 / FOX_HEAD_DIM))
        for hd in range(N_GDN_HEADS):
            cs = slice(hd * LANES, (hd + 1) * LANES)
            go = go_ref[:, cs]
            gz = gz_ref[:, cs]
            dgated = domix[:, D_FOX + hd * LANES:D_FOX + (hd + 1) * LANES]
            r = lax.rsqrt(jnp.mean(go * go, axis=-1, keepdims=True) + EPS)
            goh = go * r
            sg = _sigmoid(gz)
            sz = gz * sg
            gn = goh * gnw_ref[...]
            dgn = dgated * sz
            dgz_ref[:, cs] = dgated * gn * sg * (1.0 + gz * (1.0 - sg))
            dgw = dgw + jnp.sum(dgn * goh, axis=0, keepdims=True)
            dgh = dgn * gnw_ref[...]
            dgo_ref[:, cs] = r * (dgh - goh * jnp.mean(dgh * goh, axis=-1, keepdims=True))
        dfw_ref[...] += dfw + pltpu.roll(dfw, FOX_HEAD_DIM, 1)
        dgw_ref[...] += dgw

    tok = lambda w: pl.BlockSpec((tm, w), lambda i: (i, 0))
    vec = lambda w: pl.BlockSpec((1, w), lambda i: (0, 0))
    return pl.pallas_call(
        body, name="out_bwd", grid=(t // tm,),
        in_specs=[tok(D_MODEL), pl.BlockSpec((D_MODEL, D_MODEL), lambda i: (0, 0)), tok(D_FOX), tok(D_GDN),
                  pl.BlockSpec((tm, D_GDN), lambda i: (i, COL_GZ // D_GDN)), vec(LANES), vec(LANES)],
        out_specs=[tok(D_FOX), tok(D_GDN), tok(D_GDN), vec(LANES), vec(LANES)],
        out_shape=[_sds((t, D_FOX)), _sds((t, D_GDN)), _sds((t, D_GDN)), _sds((1, LANES)), _sds((1, LANES))],
        compiler_params=_params("arbitrary"),
    )(dmixed, w_out, o_fox, gdn_o, proj, fnw, gnw)


def _mlp_up(h2, w_up):
    t = h2.shape[0]
    tm = min(MATMUL_BLOCK, t)
    pc = D_FF // N_DEV

    def body(h_ref, w_ref, up_ref):
        h = h_ref[...]
        for p in range(N_DEV):
            up_ref[:, p * pc:(p + 1) * pc] = jnp.dot(h, w_ref[p], preferred_element_type=F32).astype(BF)

    return pl.pallas_call(
        body, name="mlp_up", grid=(t // tm,),
        in_specs=[pl.BlockSpec((tm, D_MODEL), lambda i: (i, 0)),
                  pl.BlockSpec((N_DEV, D_MODEL, pc), lambda i: (0, 0, 0))],
        out_specs=pl.BlockSpec((tm, D_FF), lambda i: (i, 0)), out_shape=_sds((t, D_FF), BF),
        compiler_params=_params("parallel"),
    )(h2, w_up)


def _mlp_down_loss(up, w_down, x1, pw, target):
    t = up.shape[0]
    tm = min(MATMUL_BLOCK, t)

    def body(up_ref, w_ref, x1_ref, pw_ref, tg_ref, dy_ref, dx2_ref, loss_ref, dpw_ref):
        i = pl.program_id(0)

        @pl.when(i == 0)
        def _():
            loss_ref[...] = jnp.zeros_like(loss_ref)
            dpw_ref[...] = jnp.zeros_like(dpw_ref)

        u = jnp.maximum(up_ref[...].astype(F32), 0.0)
        y = jnp.dot((u * u).astype(BF), w_ref[...], preferred_element_type=F32)
        r = lax.rsqrt(jnp.mean(y * y, axis=-1, keepdims=True) + EPS)
        yh = y * r
        pw = pw_ref[...]
        err = x1_ref[...] + yh * pw - tg_ref[...]
        part = jnp.sum(jnp.sum(err * err, axis=-1, keepdims=True), axis=0, keepdims=True) * (0.5 / D_MODEL)
        loss_ref[...] += jnp.broadcast_to(part, loss_ref.shape)
        dx2 = err * (1.0 / D_MODEL)
        dx2_ref[...] = dx2
        dpw_ref[...] += jnp.sum(dx2 * yh, axis=0, keepdims=True)
        dyh = dx2 * pw
        dy_ref[...] = (r * (dyh - yh * jnp.mean(dyh * yh, axis=-1, keepdims=True))).astype(BF)

    tok = lambda w: pl.BlockSpec((tm, w), lambda i: (i, 0))
    vec = lambda w: pl.BlockSpec((1, w), lambda i: (0, 0))
    return pl.pallas_call(
        body, name="mlp_down_loss", grid=(t // tm,),
        in_specs=[tok(D_FF), pl.BlockSpec((D_FF, D_MODEL), lambda i: (0, 0)), tok(D_MODEL), vec(D_MODEL), tok(D_MODEL)],
        out_specs=[tok(D_MODEL), tok(D_MODEL), vec(LANES), vec(D_MODEL)],
        out_shape=[_sds((t, D_MODEL), BF), _sds((t, D_MODEL)), _sds((1, LANES)), _sds((1, D_MODEL))],
        compiler_params=_params("arbitrary"),
    )(up, w_down, x1, pw, target)


def _mlp_bwd_act(dy, w_down, up):
    t = dy.shape[0]
    tm = min(MATMUL_BLOCK, t)

    def body(dy_ref, w_ref, up_ref, dup_ref):
        da = lax.dot_general(dy_ref[...], w_ref[...], (((1,), (1,)), ((), ())), preferred_element_type=F32)
        dup_ref[...] = (da * (2.0 * jnp.maximum(up_ref[...].astype(F32), 0.0))).astype(BF)

    return pl.pallas_call(
        body, name="mlp_bwd_act", grid=(t // tm,),
        in_specs=[pl.BlockSpec((tm, D_MODEL), lambda i: (i, 0)), pl.BlockSpec((D_FF, D_MODEL), lambda i: (0, 0)),
                  pl.BlockSpec((tm, D_FF), lambda i: (i, 0))],
        out_specs=pl.BlockSpec((tm, D_FF), lambda i: (i, 0)), out_shape=_sds((t, D_FF), BF),
        compiler_params=_params("parallel"),
    )(dy, w_down, up)


def _mlp_bwd_in(dup, w_up, x1, plw, dx2, mixed, pmw):
    t = dup.shape[0]
    tm = min(MATMUL_BLOCK, t)

    def body(dup_ref, w_ref, x1_ref, plw_ref, dx2_ref, mx_ref, pmw_ref, dx1_ref, dmixed_ref, dplw_ref, dpmw_ref):
        i = pl.program_id(0)

        @pl.when(i == 0)
        def _():
            dplw_ref[...] = jnp.zeros_like(dplw_ref)
            dpmw_ref[...] = jnp.zeros_like(dpmw_ref)

        pc = D_FF // N_DEV
        dh = _mm_nt(dup_ref[:, 0:pc], w_ref[0])
        for p in range(1, N_DEV):
            dh = dh + _mm_nt(dup_ref[:, p * pc:(p + 1) * pc], w_ref[p])
        x1 = x1_ref[...]
        r = lax.rsqrt(jnp.mean(x1 * x1, axis=-1, keepdims=True) + EPS)
        xh = x1 * r
        dplw_ref[...] += jnp.sum(dh * xh, axis=0, keepdims=True)
        dxh = dh * plw_ref[...]
        dx1 = dx2_ref[...] + r * (dxh - xh * jnp.mean(dxh * xh, axis=-1, keepdims=True))
        dx1_ref[...] = dx1
        mx = mx_ref[...]
        r2 = lax.rsqrt(jnp.mean(mx * mx, axis=-1, keepdims=True) + EPS)
        mh = mx * r2
        dpmw_ref[...] += jnp.sum(dx1 * mh, axis=0, keepdims=True)
        dmh = dx1 * pmw_ref[...]
        dmixed_ref[...] = (r2 * (dmh - mh * jnp.mean(dmh * mh, axis=-1, keepdims=True))).astype(BF)

    tok = lambda w: pl.BlockSpec((tm, w), lambda i: (i, 0))
    vec = lambda w: pl.BlockSpec((1, w), lambda i: (0, 0))
    return pl.pallas_call(
        body, name="mlp_bwd_in", grid=(t // tm,),
        in_specs=[tok(D_FF), pl.BlockSpec((N_DEV, D_MODEL, D_FF // N_DEV), lambda i: (0, 0, 0)), tok(D_MODEL),
                  vec(D_MODEL), tok(D_MODEL), tok(D_MODEL), vec(D_MODEL)],
        out_specs=[tok(D_MODEL), tok(D_MODEL), vec(D_MODEL), vec(D_MODEL)],
        out_shape=[_sds((t, D_MODEL)), _sds((t, D_MODEL), BF), _sds((1, D_MODEL)), _sds((1, D_MODEL))],
        compiler_params=_params("arbitrary"),
    )(dup, w_up, x1, plw, dx2, mixed, pmw)


def _wgrad(a, b, a_cols, split=1, a_fn=None, a_block0=0, name="wgrad"):
    t, b_cols = b.shape
    n_a = (a.shape[1] - a_block0 * a_cols) // a_cols if a_block0 else a.shape[1] // a_cols

    def body(a_ref, b_ref, o_ref):
        av = a_ref[...]
        if a_fn is not None:
            av = a_fn(av)
        o_ref[...] = _mm_tn(av, b_ref[...]).astype(BF).reshape(o_ref.shape)

    return pl.pallas_call(
        body, name=name, grid=(n_a,),
        in_specs=[pl.BlockSpec((t, a_cols), lambda i: (0, i + a_block0)), pl.BlockSpec((t, b_cols), lambda i: (0, 0))],
        out_specs=pl.BlockSpec((split, a_cols // split, b_cols), lambda i: (i, 0, 0)),
        out_shape=_sds((n_a * split, a_cols // split, b_cols), BF),
        compiler_params=_params("parallel"),
    )(a, b)


def _wgrad_pre_t(at, b, b_cols, name):
    rows, t = at.shape
    n_b = b.shape[1] // b_cols

    def body(a_ref, b_ref, o_ref):
        o_ref[0] = jnp.dot(a_ref[...], b_ref[...], preferred_element_type=F32).astype(BF)

    return pl.pallas_call(
        body, name=name, grid=(n_b,),
        in_specs=[pl.BlockSpec((rows, t), lambda j: (0, 0)), pl.BlockSpec((t, b_cols), lambda j: (0, j))],
        out_specs=pl.BlockSpec((1, rows, b_cols), lambda j: (j, 0, 0)), out_shape=_sds((n_b, rows, b_cols), BF),
        compiler_params=_params("parallel"),
    )(at, b)


def _small_bwd(proj, fb, al, dtb, dcq, dckt, dbe, dge):
    t = proj.shape[0]

    def body(sm_ref, fb_ref, al_ref, dtb_ref, dcq_ref, dckt_ref, dbe_ref, dge_ref, dsm_ref, dvec_ref):
        s = sm_ref[...]
        lane = _iota((1, LANES), 1)
        dcum = dcq_ref[...] - dckt_ref[...].T
        row = _iota((t, LANES), 0)
        step = 1
        while step < t:
            dcum = dcum + _shift_up(dcum, step, row)
            step *= 2
        dff = dcum * _sigmoid(-(s + fb_ref[...]))
        dbeta = jnp.zeros((t, LANES), F32)
        dg = jnp.zeros((t, LANES), F32)
        for hd in range(N_GDN_HEADS):
            dbeta = jnp.where(lane == SM_GB + hd, dbe_ref[:, hd * LANES:hd * LANES + 1], dbeta)
            dg = jnp.where(lane == SM_GA + hd, dge_ref[:, hd * LANES:hd * LANES + 1], dg)
        beta = _sigmoid(s)
        dgb = dbeta * beta * (1.0 - beta)
        za = s + dtb_ref[...]
        nea = -jnp.exp(al_ref[...])
        dga = dg * nea * _sigmoid(za)
        is_f = lane < SM_GB
        is_b = (lane >= SM_GB) & (lane < SM_GA)
        is_a = (lane >= SM_GA) & (lane < SM_GA + 4)
        dsm_ref[...] = jnp.where(is_f, dff, jnp.where(is_b, dgb, jnp.where(is_a, dga, 0.0)))
        dvec_ref[...] = jnp.zeros_like(dvec_ref)
        dvec_ref[0:1, :] = jnp.sum(jnp.where(is_f, dff, 0.0), axis=0, keepdims=True)
        dvec_ref[1:2, :] = jnp.sum(jnp.where(is_a, dg * nea * _softplus(za), 0.0), axis=0, keepdims=True)
        dvec_ref[2:3, :] = jnp.sum(jnp.where(is_a, dga, 0.0), axis=0, keepdims=True)

    vec = pl.BlockSpec((1, LANES), lambda i: (0, 0))
    full = lambda r, c: pl.BlockSpec((r, c), lambda i: (0, 0))
    return pl.pallas_call(
        body, name="small_bwd", grid=(1,),
        in_specs=[pl.BlockSpec((t, LANES), lambda i: (0, COL_SMALL // LANES)), vec, vec, vec, full(t, LANES),
                  full(LANES, t), full(t, 512), full(t, 512)],
        out_specs=[full(t, LANES), full(8, LANES)], out_shape=[_sds((t, LANES)), _sds((8, LANES))],
        compiler_params=_params("arbitrary"),
    )(proj, fb, al, dtb, dcq, dckt, dbe, dge)


def _pack_dproj(dfox, dgdn, dgz, dsm):
    t = dgz.shape[0]
    tm = min(MATMUL_BLOCK, t)

    def body(*refs):
        parts, dp_ref = refs[:8], refs[8]
        col = 0
        for part in parts:
            width = part.shape[1]
            dp_ref[:, col:col + width] = part[...].astype(BF)
            col += width

    tok = lambda w: pl.BlockSpec((tm, w), lambda i: (i, 0))
    return pl.pallas_call(
        body, name="pack_dproj", grid=(t // tm,), in_specs=[tok(D_FOX)] * 3 + [tok(D_GDN)] * 4 + [tok(LANES)],
        out_specs=tok(PROJ_W), out_shape=_sds((t, PROJ_W), BF), compiler_params=_params("parallel"),
    )(*dfox, *dgdn, dgz, dsm)


def _in_bwd(dproj, wt_al, x, nw, dx1):
    t = x.shape[0]
    tm = min(MATMUL_BLOCK, t)

    def body(dp_ref, w_ref, x_ref, nw_ref, dx1_ref, dx_ref, dnw_ref):
        i = pl.program_id(0)

        @pl.when(i == 0)
        def _():
            dnw_ref[...] = jnp.zeros_like(dnw_ref)

        dh = jnp.dot(dp_ref[...], w_ref[...], preferred_element_type=F32)
        xv = x_ref[...]
        r = lax.rsqrt(jnp.mean(xv * xv, axis=-1, keepdims=True) + EPS)
        xh = xv * r
        dnw_ref[...] += jnp.sum(dh * xh, axis=0, keepdims=True)
        dxh = dh * nw_ref[...]
        dx_ref[...] = dx1_ref[...] + r * (dxh - xh * jnp.mean(dxh * xh, axis=-1, keepdims=True))

    tok = lambda w: pl.BlockSpec((tm, w), lambda i: (i, 0))
    vec = lambda w: pl.BlockSpec((1, w), lambda i: (0, 0))
    return pl.pallas_call(
        body, name="in_bwd", grid=(t // tm,),
        in_specs=[tok(PROJ_W), pl.BlockSpec((PROJ_W, D_MODEL), lambda i: (0, 0)), tok(D_MODEL), vec(D_MODEL),
                  tok(D_MODEL)],
        out_specs=[tok(D_MODEL), vec(D_MODEL)], out_shape=[_sds((t, D_MODEL)), _sds((1, D_MODEL))],
        compiler_params=_params("arbitrary"),
    )(dproj, wt_al, x, nw, dx1)


def _row(v, width=None):
    v = v.reshape(1, -1).astype(F32)
    if width is not None and v.shape[1] < width:
        v = jnp.pad(v, ((0, 0), (0, width - v.shape[1])))
    return v


def _lane_vec(v, first):
    return jnp.pad(v.astype(F32), (first, LANES - first - v.shape[0])).reshape(1, LANES)


def _local_step(x, target, wt_al, late_weights, on_grads, convw, pre_mix_norm, fox_f_bias, fox_out_norm,
                gdn_a_log, gdn_dt_bias, gdn_out_norm, post_mix_norm, pre_mlp_norm, post_mlp_norm):
    t = x.shape[0]
    nch = t // CHUNK
    nw, pmw, plw, pw = _row(pre_mix_norm), _row(post_mix_norm), _row(pre_mlp_norm), _row(post_mlp_norm)
    fb, al, dtb = _lane_vec(fox_f_bias, SM_FF), _lane_vec(gdn_a_log, SM_GA), _lane_vec(gdn_dt_bias, SM_GA)
    fnw = _row(jnp.tile(fox_out_norm, 2))
    gnw = _row(gdn_out_norm)

    proj, h = _norm_proj(x, nw, wt_al)
    cumt, beta, g = _small_prep(proj, fb, al, dtb)
    o_fox, lse, fox_n = _fox_fwd(proj, cumt, fnw)
    qn, kn, cv, gc, be, mmat, amat = _gdn_prep(proj, convw, beta, g)
    n_prob = N_GDN_HEADS * nch
    m3 = mmat.reshape(n_prob, CHUNK, CHUNK)
    if n_prob < LANES:
        m3 = jnp.pad(m3, ((0, LANES - n_prob), (0, 0), (0, 0)))
    tinv = _tri_inverse(m3)[:n_prob].reshape(N_GDN_HEADS, nch, CHUNK, CHUNK)
    gdn_o, s_all, vn_all = _gdn_scan(qn, kn, cv, be, gc, tinv, amat)
    w_out = late_weights("w_out", gdn_o)
    x1, h2, mixed, omix, h2t = _mix_out(fox_n, gdn_o, proj, gnw, w_out, x, pmw, plw)
    w_up, w_down = late_weights("mlp", h2)
    up = _mlp_up(h2, w_up)
    dy, dx2, loss, d_pw = _mlp_down_loss(up, w_down, x1, pw, target)

    dup = _mlp_bwd_act(dy, w_down, up)
    relu2 = lambda u: jnp.square(jnp.maximum(u.astype(F32), 0.0))
    g_down = _wgrad(up, dy, D_FF // N_DEV, a_fn=relu2, name="wgrad_down")
    g_up = _wgrad_pre_t(h2t, dup, D_FF // N_DEV, name="wgrad_up")
    token = on_grads("mlp", (g_up, g_down))
    dx1, dmixed, d_plw, d_pmw = _mlp_bwd_in(dup, w_up, x1, plw + token[0:1, 0:1], dx2, mixed, pmw)
    token = on_grads("w_out", _wgrad(omix, dmixed, 512, split=4, name="wgrad_out"))
    do_fox, dgo, dgz, d_fnw, d_gnw = _out_bwd(dmixed, w_out, o_fox, gdn_o, proj, fnw + token[0:1, 0:1], gnw)
    dfq, dfk, dfv, dcq, dckt = _fox_bwd(proj, cumt, lse, o_fox, do_fox)
    dqn, dkn, dcv, dbe, dge = _gdn_bwd(qn, kn, cv, be, gc, tinv, amat, s_all, vn_all, dgo)
    dxq, dxk, dxv, dwq, dwk, dwv = _gdn_bwd_conv(proj, convw, dqn, dkn, dcv)
    dsm, dvec = _small_bwd(proj, fb, al, dtb, dcq, dckt, dbe, dge)
    dproj = _pack_dproj((dfq, dfk, dfv), (dxq, dxk, dxv), dgz, dsm)
    g_main = _wgrad(dproj, h, 512, name="wgrad_in")
    g_tail = _wgrad(dproj, h, LANES, a_block0=COL_SMALL // LANES, name="wgrad_in_small")
    token = on_grads("w_in", jnp.concatenate([g_main.reshape(COL_SMALL, D_MODEL), g_tail[0]]))
    grad_x, d_nw = _in_bwd(dproj, wt_al, x, nw + token[0:1, 0:1], dx1)
    small = dict(norms=(d_nw, d_pmw, d_plw, d_pw), fox_out_norm=d_fnw, gdn_out_norm=d_gnw, loss=loss, vectors=dvec,
                 conv=(dwq, dwk, dwv))
    return grad_x, small


MESH_IDS = pl.DeviceIdType.MESH
CHIP_FLIPS = ((0, 0), (1, 0), (0, 1), (1, 1))
ANY_SPEC = pl.BlockSpec(memory_space=pl.ANY)


def _place():
    return lax.axis_index("x"), lax.axis_index("y"), lax.axis_index("c")


def _all_gather(blocks):
    n = len(blocks)

    def body(*refs):
        ins, outs, (send_sems, recv_sems, local_sems) = refs[:n], refs[n:2 * n], refs[2 * n:]
        x, y, c = _place()
        sibling = (x, y, 1 - c)
        chips = [(x ^ fx, y ^ fy) for fx, fy in CHIP_FLIPS[1:]]

        def slot(out, px, py, pc):
            return out.at[4 * px + 2 * py + pc]

        def copy(a, k, block, to, src=None):
            return pltpu.make_async_remote_copy(
                src_ref=slot(outs[a], *block) if src is None else src, dst_ref=slot(outs[a], *block),
                send_sem=send_sems.at[a, k], recv_sem=recv_sems.at[a, k], device_id=to, device_id_type=MESH_IDS)

        pending = []
        for a in range(n):
            mine = pltpu.make_async_copy(ins[a], slot(outs[a], x, y, c), local_sems.at[a])
            mine.start()
            pending.append(mine)
        sends = []
        for a in range(n):
            first = [copy(a, 0, (x, y, c), sibling, src=ins[a])]
            first += [copy(a, 1 + j, (x, y, c), (*chip, c), src=ins[a]) for j, chip in enumerate(chips)]
            for cp in first:
                cp.start()
            sends += first
        for a in range(n):
            for j, chip in enumerate(chips):
                copy(a, 1 + j, (*chip, c), (x, y, c)).wait_recv()
                fwd = copy(a, 4 + j, (*chip, c), sibling)
                fwd.start()
                sends.append(fwd)
        for a in range(n):
            copy(a, 0, sibling, (x, y, c)).wait_recv()
            for j, chip in enumerate(chips):
                copy(a, 4 + j, (*chip, 1 - c), (x, y, c)).wait_recv()
        for cp in sends:
            cp.wait_send()
        for cp in pending:
            cp.wait()

    return pl.pallas_call(
        body, name="all_gather_weights", in_specs=[ANY_SPEC] * n, out_specs=[ANY_SPEC] * n,
        out_shape=[_sds((N_DEV,) + b.shape, b.dtype) for b in blocks],
        scratch_shapes=[pltpu.SemaphoreType.DMA((n, 7)), pltpu.SemaphoreType.DMA((n, 7)), pltpu.SemaphoreType.DMA((n,))],
        compiler_params=pltpu.CompilerParams(has_side_effects=True),
    )(*blocks)


def _adamw(w, g, m, v):
    m = ADAM_B1 * m + (1.0 - ADAM_B1) * g
    v = ADAM_B2 * v + (1.0 - ADAM_B2) * (g * g)
    m_hat = m / (1.0 - ADAM_B1 ** ADAM_STEP)
    v_hat = v / (1.0 - ADAM_B2 ** ADAM_STEP)
    return -ADAM_LR * (m_hat / (jnp.sqrt(v_hat) + ADAM_EPS) + ADAM_WD * w), m, v


HBM_SPEC = pl.BlockSpec(memory_space=pltpu.HBM)
SEM_SPEC = pl.BlockSpec(memory_space=pltpu.SEMAPHORE)
DATAFLOW = pltpu.SideEffectType.DATAFLOW_SIDE_EFFECTING


def _peers():
    x, y, c = _place()
    return 4 * x + 2 * y + c, [(x ^ (k >> 2), y ^ ((k >> 1) & 1), c ^ (k & 1)) for k in range(1, N_DEV)]


def _peer_index(peer):
    return 4 * peer[0] + 2 * peer[1] + peer[2]


def _zones_with_own(srcs, pieces, name, after=None, dtype=None):
    n = len(srcs)
    extra = [] if after is None else [after]
    dtypes = [s_.dtype if pieces or dtype is None else dtype for s_ in srcs]

    def body(me_ref, *refs):
        outs = refs[n + len(extra):]
        for a in range(n):
            if pieces:
                outs[a][0] = refs[a][0]
            else:
                val = refs[a][...].astype(dtypes[a])
                outs[a][0] = val
                outs[n + a][...] = val

    shapes = [s_.shape[1:] if pieces else s_.shape for s_ in srcs]
    mine = lambda sh: pl.BlockSpec((1,) + sh, lambda i, me_ref: (me_ref[0], 0, 0))
    whole = lambda sh: pl.BlockSpec(sh, lambda i, me_ref: (0, 0))
    in_specs = [mine(sh) if pieces else whole(sh) for sh in shapes]
    out_specs = [mine(sh) for sh in shapes] + ([] if pieces else [whole(sh) for sh in shapes])
    out_shape = [_sds((N_DEV,) + sh, dt) for sh, dt in zip(shapes, dtypes)]
    out_shape += [] if pieces else [_sds(sh, dt) for sh, dt in zip(shapes, dtypes)]
    x, y, c = _place()
    out = pl.pallas_call(
        body, name=name,
        grid_spec=pltpu.PrefetchScalarGridSpec(num_scalar_prefetch=1, grid=(1,), in_specs=in_specs + [ANY_SPEC] * len(extra),
                                               out_specs=out_specs),
        out_shape=out_shape, compiler_params=_params("arbitrary"),
    )((4 * x + 2 * y + c).astype(jnp.int32).reshape(1), *srcs, *extra)
    return out[:n], (list(srcs) if pieces else out[n:])


def _exchange_start(srcs, zones, pieces, name):
    n = len(srcs)

    def body(*refs):
        ins, zs = refs[:n], refs[n:2 * n]
        sems = refs[2 * n:4 * n]
        token = refs[-1]
        me, peers = _peers()
        for peer in peers:
            for a in range(n):
                pltpu.make_async_remote_copy(
                    src_ref=ins[a].at[_peer_index(peer)] if pieces else ins[a], dst_ref=zs[a].at[me],
                    send_sem=sems[2 * a], recv_sem=sems[2 * a + 1], device_id=peer, device_id_type=MESH_IDS).start()
        token[...] = jnp.zeros_like(token)

    hbm = lambda v: pltpu.with_memory_space_constraint(v, pltpu.HBM)
    out = pl.pallas_call(
        body, name=name,
        out_shape=tuple([pltpu.SemaphoreType.DMA(())] * (2 * n) + [pltpu.HBM(v.shape, v.dtype) for v in srcs]
                        + [pltpu.HBM(z.shape, z.dtype) for z in zones] + [_sds((8, LANES))]),
        in_specs=[HBM_SPEC] * (2 * n), out_specs=tuple([SEM_SPEC] * (2 * n) + [HBM_SPEC] * (2 * n) + [VMEM_SPEC]),
        input_output_aliases={i: 2 * n + i for i in range(2 * n)},
        compiler_params=pltpu.CompilerParams(has_side_effects=DATAFLOW),
    )(*[hbm(v) for v in srcs], *[hbm(z) for z in zones])
    return out[:2 * n], out[2 * n:3 * n], out[3 * n:4 * n], out[-1]


def _exchange_wait(sems, srcs, zones, after, name):
    n = len(srcs)
    after = list(after) if isinstance(after, (list, tuple)) else [after]

    def body(*refs):
        ins, zs, sm = refs[:n], refs[n:2 * n], refs[2 * n:4 * n]
        me, peers = _peers()
        for a in range(n):
            seven = zs[a].at[pl.ds(0, N_DEV - 1)]
            cp = pltpu.make_async_remote_copy(src_ref=seven, dst_ref=seven, send_sem=sm[2 * a], recv_sem=sm[2 * a + 1],
                                              device_id=peers[0], device_id_type=MESH_IDS)
            cp.wait_send()
            cp.wait_recv()

    out = pl.pallas_call(
        body, name=name, out_shape=tuple([pltpu.HBM(v.shape, v.dtype) for v in srcs] + [pltpu.HBM(z.shape, z.dtype) for z in zones]),
        in_specs=[HBM_SPEC] * (2 * n) + [SEM_SPEC] * (2 * n) + [ANY_SPEC] * len(after),
        out_specs=tuple([HBM_SPEC] * (2 * n)), input_output_aliases={i: i for i in range(2 * n)},
        compiler_params=pltpu.CompilerParams(has_side_effects=DATAFLOW),
    )(*srcs, *zones, *sems, *after)
    return out[n:]


def _sum_adamw(zone, w, m, v, name):
    _, r, c_ = zone.shape
    rb = next((b for b in (256, 128) if r % b == 0), r)

    def body(z_ref, w_ref, m_ref, v_ref, grad_ref, delta_ref, nm_ref, nv_ref):
        total = z_ref[0].astype(F32)
        for d in range(1, N_DEV):
            total = total + z_ref[d].astype(F32)
        grad_ref[...] = total
        delta_ref[...], nm_ref[...], nv_ref[...] = _adamw(w_ref[...], total, m_ref[...], v_ref[...])

    blk = pl.BlockSpec((rb, c_), lambda i: (i, 0))
    return pl.pallas_call(
        body, name=name, grid=(r // rb,), in_specs=[pl.BlockSpec((N_DEV, rb, c_), lambda i: (0, i, 0)), blk, blk, blk],
        out_specs=[blk] * 4, out_shape=[_sds((r, c_))] * 4, compiler_params=_params("parallel"),
    )(zone, w, m, v)


SMALL_NORMS = ("pre_mix_norm", "post_mix_norm", "pre_mlp_norm", "post_mlp_norm")
SMALL_ORDER = SMALL_NORMS + ("fox_out_norm", "gdn_out_norm", "fox_f_bias", "gdn_a_log", "gdn_dt_bias", "gdn_conv_w")
CONV_SLAB_ROWS, CONV_SLAB_LANES = 8, 256


def _small_pack(small):
    def body(n0, n1, n2, n3, fnw_ref, gnw_ref, loss_ref, vec_ref, out_ref):
        out_ref[...] = jnp.zeros_like(out_ref)
        for i, ref in enumerate((n0, n1, n2, n3)):
            out_ref[i:i + 1, :] = ref[...]
        out_ref[4:5, 0:LANES] = fnw_ref[...]
        out_ref[4:5, LANES:2 * LANES] = gnw_ref[...]
        out_ref[4:5, 2 * LANES:3 * LANES] = loss_ref[...]
        out_ref[5:8, 0:LANES] = vec_ref[0:3, :]

    return pl.pallas_call(body, name="small_pack", in_specs=[VMEM_SPEC] * 8, out_specs=VMEM_SPEC,
                          out_shape=_sds((8, D_MODEL)))(*small["norms"], small["fox_out_norm"], small["gdn_out_norm"],
                                                        small["loss"], small["vectors"])


def _conv_slabs(dconv):
    blocks = dconv.reshape(CONV_K, N_DEV, -1).transpose(1, 0, 2)
    blocks = jnp.pad(blocks, ((0, 0), (0, CONV_SLAB_ROWS - CONV_K), (0, CONV_SLAB_LANES - blocks.shape[2])))
    return blocks.reshape(N_DEV * CONV_SLAB_ROWS, CONV_SLAB_LANES)


def _small_update(zone, conv_zone, w, m, v):
    n = len(SMALL_ORDER)
    n_conv = w["gdn_conv_w"].shape[1]

    def body(me_ref, z_ref, zc_ref, *refs):
        params, loss_ref, outs, (tot, totc) = refs[:3 * n], refs[3 * n], refs[3 * n + 1:7 * n + 1], refs[-2:]
        total, total_c = z_ref[0], zc_ref[0]
        for d in range(1, N_DEV):
            total, total_c = total + z_ref[d], total_c + zc_ref[d]
        tot[...] = total
        totc[...] = total_c
        loss_ref[...] = tot[4, 2 * LANES:2 * LANES + 1]
        mine = totc[pl.ds(pl.multiple_of(me_ref[0] * CONV_SLAB_ROWS, CONV_SLAB_ROWS), CONV_SLAB_ROWS), :]
        g = dict(zip(SMALL_NORMS, (tot[0], tot[1], tot[2], tot[3])))
        g.update(fox_out_norm=tot[4, 0:FOX_HEAD_DIM], gdn_out_norm=tot[4, LANES:LANES + GDN_HEAD_DIM],
                 fox_f_bias=tot[5, SM_FF:SM_FF + N_FOX_HEADS], gdn_a_log=tot[6, SM_GA:SM_GA + N_GDN_HEADS],
                 gdn_dt_bias=tot[7, SM_GA:SM_GA + N_GDN_HEADS], gdn_conv_w=mine[0:CONV_K, 0:n_conv])
        for i, name in enumerate(SMALL_ORDER):
            w_ref, m_ref, v_ref = params[3 * i:3 * i + 3]
            outs[4 * i][...] = g[name]
            outs[4 * i + 1][...], outs[4 * i + 2][...], outs[4 * i + 3][...] = _adamw(w_ref[...], g[name], m_ref[...],
                                                                                     v_ref[...])

    x, y, c = _place()
    operands = [a[name] for name in SMALL_ORDER for a in (w, m, v)]
    out = pl.pallas_call(
        body, name="small_update",
        in_specs=[pl.BlockSpec(memory_space=pltpu.SMEM)] + [VMEM_SPEC] * (2 + 3 * n), out_specs=[VMEM_SPEC] * (1 + 4 * n),
        out_shape=[_sds((1,))] + [_sds(w[name].shape) for name in SMALL_ORDER for _ in range(4)],
        scratch_shapes=[pltpu.VMEM(zone.shape[1:], F32), pltpu.VMEM(conv_zone.shape[1:], F32)],
    )((4 * x + 2 * y + c).astype(jnp.int32).reshape(1), zone, conv_zone, *operands)
    return out[0][0], {name: out[1 + 4 * i:5 + 4 * i] for i, name in enumerate(SMALL_ORDER)}


NATIVE_ROWS = ((0, 1536), (1544, 3080), (3088, 3600), (1536, 1544), (3080, 3088))


def _to_aligned_rows(wt_native):
    pad = jnp.zeros((PROJ_W - D_PROJ, wt_native.shape[1]), wt_native.dtype)
    return jnp.concatenate([wt_native[lo:hi] for lo, hi in NATIVE_ROWS] + [pad])


def _from_aligned_rows(gt_al):
    return jnp.concatenate([gt_al[0:1536], gt_al[3584:3592], gt_al[1536:3072], gt_al[3592:3600], gt_al[3072:3584]])


def _cols_from_pieces(p):
    return p.transpose(1, 0, 2).reshape(p.shape[1], -1)


WEIGHT_ORDER = ("pre_mix_norm", "w_in", "fox_f_bias", "fox_out_norm", "gdn_conv_w", "gdn_a_log", "gdn_dt_bias",
                "gdn_out_norm", "w_out", "post_mix_norm", "pre_mlp_norm", "w_up", "w_down", "post_mlp_norm")


def kernel(x, pre_mix_norm, w_in, fox_f_bias, fox_out_norm, gdn_conv_w, gdn_a_log, gdn_dt_bias, gdn_out_norm, w_out, post_mix_norm, pre_mlp_norm, w_up, w_down, post_mlp_norm, loss_target, m_pre_mix_norm, m_w_in, m_fox_f_bias, m_fox_out_norm, m_gdn_conv_w, m_gdn_a_log, m_gdn_dt_bias, m_gdn_out_norm, m_w_out, m_post_mix_norm, m_pre_mlp_norm, m_w_up, m_w_down, m_post_mlp_norm, v_pre_mix_norm, v_w_in, v_fox_f_bias, v_fox_out_norm, v_gdn_conv_w, v_gdn_a_log, v_gdn_dt_bias, v_gdn_out_norm, v_w_out, v_post_mix_norm, v_pre_mlp_norm, v_w_up, v_w_down, v_post_mlp_norm):
    w = dict(pre_mix_norm=pre_mix_norm, w_in=w_in, fox_f_bias=fox_f_bias, fox_out_norm=fox_out_norm,
             gdn_conv_w=gdn_conv_w, gdn_a_log=gdn_a_log, gdn_dt_bias=gdn_dt_bias, gdn_out_norm=gdn_out_norm, w_out=w_out,
             post_mix_norm=post_mix_norm, pre_mlp_norm=pre_mlp_norm, w_up=w_up, w_down=w_down, post_mlp_norm=post_mlp_norm)
    mom = dict(pre_mix_norm=m_pre_mix_norm, w_in=m_w_in, fox_f_bias=m_fox_f_bias, fox_out_norm=m_fox_out_norm,
               gdn_conv_w=m_gdn_conv_w, gdn_a_log=m_gdn_a_log, gdn_dt_bias=m_gdn_dt_bias, gdn_out_norm=m_gdn_out_norm,
               w_out=m_w_out, post_mix_norm=m_post_mix_norm, pre_mlp_norm=m_pre_mlp_norm, w_up=m_w_up, w_down=m_w_down,
               post_mlp_norm=m_post_mlp_norm)
    var = dict(pre_mix_norm=v_pre_mix_norm, w_in=v_w_in, fox_f_bias=v_fox_f_bias, fox_out_norm=v_fox_out_norm,
               gdn_conv_w=v_gdn_conv_w, gdn_a_log=v_gdn_a_log, gdn_dt_bias=v_gdn_dt_bias, gdn_out_norm=v_gdn_out_norm,
               w_out=v_w_out, post_mix_norm=v_post_mix_norm, pre_mlp_norm=v_pre_mlp_norm, w_up=v_w_up, w_down=v_w_down,
               post_mlp_norm=v_post_mlp_norm)

    win_g, conv_g = _all_gather([w_in.T.astype(BF), gdn_conv_w])
    wt_al = _to_aligned_rows(win_g.reshape(D_PROJ, D_MODEL))
    convw = _cols_from_pieces(conv_g)
    gathers, after = {}, win_g
    for name, shards in (("w_out", [w_out]), ("mlp", [w_up, w_down])):
        zones, shards = _zones_with_own(shards, False, "gather_" + name + "_own", after=after, dtype=BF)
        gathers[name] = _exchange_start(shards, zones, False, "gather_" + name + "_start")
        after = gathers[name][3]

    def late_weights(name, after):
        sems, shards, zones, _ = gathers[name]
        got = _exchange_wait(sems, shards, zones, after, "gather_" + name + "_wait")
        if name == "w_out":
            return got[0].reshape(D_MODEL, D_MODEL)
        return got[0], got[1].reshape(D_FF, D_MODEL)

    scatters = {}

    def on_grads(name, g):
        if name == "w_in":
            g = _from_aligned_rows(g).reshape(N_DEV, D_PROJ // N_DEV, D_MODEL)
        srcs = list(g) if name == "mlp" else [g]
        zones, _ = _zones_with_own(srcs, True, "scatter_" + name + "_own")
        scatters[name] = _exchange_start(srcs, zones, True, "scatter_" + name + "_start")
        return scatters[name][3]

    grad_x, small = _local_step(
        x[0], loss_target[0], wt_al, late_weights, on_grads, convw, pre_mix_norm + after[0, 0],
        fox_f_bias, fox_out_norm, gdn_a_log, gdn_dt_bias, gdn_out_norm, post_mix_norm, pre_mlp_norm, post_mlp_norm)
    slabs = [_small_pack(small), _conv_slabs(jnp.concatenate(small["conv"], axis=1))]
    zones, slabs = _zones_with_own(slabs, False, "small_own")
    scatters["small"] = _exchange_start(slabs, zones, False, "small_start")

    grads, delta, new_m, new_v = {}, {}, {}, {}
    after = scatters["small"][3]
    for name, members in (("mlp", ("w_up", "w_down")), ("w_out", ("w_out",)), ("small", ()), ("w_in", ("w_in",))):
        sems, srcs, zones, _ = scatters[name]
        zones = _exchange_wait(sems, srcs, zones, after, "scatter_" + name + "_wait")
        if name == "small":
            loss, updated = _small_update(zones[0], zones[1], w, mom, var)
            for n, res in updated.items():
                grads[n], delta[n], new_m[n], new_v[n] = res
            after = grads["pre_mix_norm"]
        for n, zone in zip(members, zones):
            if n == "w_in":
                res = _sum_adamw(zone, w[n].T, mom[n].T, var[n].T, "adamw_" + n)
                grads[n], delta[n], new_m[n], new_v[n] = [r.T for r in res]
            else:
                grads[n], delta[n], new_m[n], new_v[n] = _sum_adamw(zone, w[n], mom[n], var[n], "adamw_" + n)
        if members:
            after = [grads[n] for n in members]

    return (loss, grad_x[None], *[grads[n] for n in WEIGHT_ORDER], *[delta[n] for n in WEIGHT_ORDER],
            *[new_m[n] for n in WEIGHT_ORDER], *[new_v[n] for n in WEIGHT_ORDER])
```

```python
import jax
import jax.numpy as jnp
from jax import lax
from jax.experimental import pallas as pl
from jax.experimental.pallas import tpu as pltpu

F32 = jnp.float32
BF = jnp.bfloat16

D_MODEL = 1024
N_FOX_HEADS, FOX_HEAD_DIM = 8, 64
N_GDN_HEADS, GDN_HEAD_DIM = 4, 128
D_FOX = N_FOX_HEADS * FOX_HEAD_DIM
D_GDN = N_GDN_HEADS * GDN_HEAD_DIM
CHUNK = 64
CONV_K = 4
D_FF = 4 * D_MODEL
EPS = 1e-6
D_PROJ = 3600
N_DEV = 8

PROJ_W = 3712
COL_FOX, COL_GDN, COL_GZ, COL_SMALL = 0, 1536, 3072, 3584
LANES = 128
SM_FF, SM_GB, SM_GA = 0, 8, 12

ADAM_LR, ADAM_B1, ADAM_B2, ADAM_EPS, ADAM_WD, ADAM_STEP = 0.001, 0.9, 0.999, 1e-08, 0.01, 10

TOKEN_BLOCK = 256
MATMUL_BLOCK = 512
FOX_SCALE = FOX_HEAD_DIM ** -0.5
GDN_QSCALE = GDN_HEAD_DIM ** -0.5
NEG_BIG = -1e30
VMEM_LIMIT = 56 * 1024 * 1024

VMEM_SPEC = pl.BlockSpec(memory_space=pltpu.VMEM)


def _sds(shape, dtype=F32):
    return jax.ShapeDtypeStruct(shape, dtype)


def _params(*sem):
    return pltpu.CompilerParams(dimension_semantics=sem if sem else None, vmem_limit_bytes=VMEM_LIMIT)


def _mm(a, b):
    return jnp.dot(a.astype(BF), b.astype(BF), preferred_element_type=F32)


def _mm_nt(a, b):
    return lax.dot_general(a.astype(BF), b.astype(BF), (((1,), (1,)), ((), ())), preferred_element_type=F32)


def _mm_tn(a, b):
    return lax.dot_general(a.astype(BF), b.astype(BF), (((0,), (0,)), ((), ())), preferred_element_type=F32)


def _sigmoid(x):
    return 1.0 / (1.0 + jnp.exp(-x))


def _softplus(x):
    return jnp.maximum(x, 0.0) + jnp.log1p(jnp.exp(-jnp.abs(x)))


def _iota(shape, dim):
    return lax.broadcasted_iota(jnp.int32, shape, dim)


def _shift_down(x, s, row):
    return jnp.where(row >= s, pltpu.roll(x, s, 0), 0.0)


def _shift_up(x, s, row):
    n = x.shape[0]
    return jnp.where(row < n - s, pltpu.roll(x, n - s, 0), 0.0)


def _norm_proj(x, nw, wt_al):
    t = x.shape[0]

    def body(x_ref, nw_ref, w_ref, proj_ref, h_ref):
        xv = x_ref[...]
        r = lax.rsqrt(jnp.mean(xv * xv, axis=-1, keepdims=True) + EPS)
        h = (xv * r * nw_ref[...]).astype(BF)
        h_ref[...] = h
        proj_ref[...] = lax.dot_general(h, w_ref[...], (((1,), (1,)), ((), ())), preferred_element_type=F32)

    tm = min(MATMUL_BLOCK, t)
    return pl.pallas_call(
        body, name="norm_proj", grid=(t // tm,),
        in_specs=[pl.BlockSpec((tm, D_MODEL), lambda i: (i, 0)), pl.BlockSpec((1, D_MODEL), lambda i: (0, 0)),
                  pl.BlockSpec((PROJ_W, D_MODEL), lambda i: (0, 0))],
        out_specs=[pl.BlockSpec((tm, PROJ_W), lambda i: (i, 0)), pl.BlockSpec((tm, D_MODEL), lambda i: (i, 0))],
        out_shape=[_sds((t, PROJ_W)), _sds((t, D_MODEL), BF)],
        compiler_params=_params("parallel"),
    )(x, nw, wt_al)


def _lane_column(x, lane):
    return jnp.sum(jnp.where(_iota((1, LANES), 1) == lane, x, 0.0), axis=-1, keepdims=True)


def _small_prep(proj, fb, al, dtb):
    t = proj.shape[0]

    def body(sm_ref, fb_ref, al_ref, dtb_ref, cumt_ref, beta_ref, g_ref):
        s = sm_ref[...]
        z = s + fb_ref[...]
        cum = jnp.minimum(z, 0.0) - jnp.log1p(jnp.exp(-jnp.abs(z)))
        row = _iota((t, LANES), 0)
        step = 1
        while step < t:
            cum = cum + _shift_down(cum, step, row)
            step *= 2
        cumt_ref[...] = cum.T
        beta_ref[...] = _sigmoid(s)
        g_ref[...] = -jnp.exp(al_ref[...]) * _softplus(s + dtb_ref[...])

    vec = pl.BlockSpec((1, LANES), lambda i: (0, 0))
    tok = pl.BlockSpec((t, LANES), lambda i: (0, 0))
    return pl.pallas_call(
        body, name="small_prep", grid=(1,),
        in_specs=[pl.BlockSpec((t, LANES), lambda i: (0, COL_SMALL // LANES)), vec, vec, vec],
        out_specs=[pl.BlockSpec((LANES, t), lambda i: (0, 0)), tok, tok],
        out_shape=[_sds((LANES, t)), _sds((t, LANES)), _sds((t, LANES))],
        compiler_params=_params("arbitrary"),
    )(proj, fb, al, dtb)


def _fox_stack(x, first):
    return jnp.concatenate([jnp.where(first, x, 0.0), jnp.where(first, 0.0, x)], axis=0).astype(BF)


def _fox_unstack(y, first):
    n = y.shape[0] // 2
    return jnp.where(first, y[:n], y[n:])


def _fox_logits(q2_i, kb, cumt_ref, pair, i, tq):
    klen = (i + 1) * tq
    s = lax.dot_general(q2_i, kb[:klen], (((1,), (1,)), ((), ())), preferred_element_type=F32)
    upper = _iota((2 * tq, 1), 0) < tq
    s = s - jnp.where(upper, cumt_ref[pl.ds(2 * pair, 1), 0:klen], cumt_ref[pl.ds(2 * pair + 1, 1), 0:klen])
    causal = _iota((2 * tq, tq), 1) <= _iota((2 * tq, tq), 0) % tq
    parts = [(s[:, :klen - tq], 0, klen - tq)] if i else []
    return parts + [(jnp.where(causal, s[:, klen - tq:], NEG_BIG), klen - tq, klen)]


def _fox_fwd(proj, cumt, fnw):
    t = proj.shape[0]
    tq = min(TOKEN_BLOCK, t // 2)
    nq = t // tq

    def body(q_ref, k_ref, v_ref, cumt_ref, fnw_ref, o_ref, lse_ref, fn_ref):
        j = pl.program_id(0)
        first = _iota((1, LANES), 1) < FOX_HEAD_DIM
        kb = k_ref[...].astype(BF)
        vb = v_ref[...].astype(BF)
        for i in range(nq):
            rows = slice(i * tq, (i + 1) * tq)
            q2 = _fox_stack(q_ref[rows, :] * FOX_SCALE, first)
            parts = _fox_logits(q2, kb, cumt_ref, j, i, tq)
            m = jnp.max(parts[-1][0], axis=-1, keepdims=True)
            if i:
                m = jnp.maximum(m, jnp.max(parts[0][0], axis=-1, keepdims=True))
            l = jnp.zeros((2 * tq, 1), F32)
            o = jnp.zeros((2 * tq, LANES), F32)
            for s, lo, hi in parts:
                p = jnp.exp(s - m)
                l = l + jnp.sum(p, axis=-1, keepdims=True)
                o = o + jnp.dot(p.astype(BF), vb[lo:hi], preferred_element_type=F32)
            o_acc = _fox_unstack(o / l, first)
            lse_acc = _fox_unstack(jnp.broadcast_to(m + jnp.log(l), (2 * tq, LANES)), first)
            o_ref[rows, :] = o_acc
            lse_ref[rows, :] = lse_acc
            o2 = o_acc * o_acc
            s0 = jnp.sum(jnp.where(first, o2, 0.0), axis=-1, keepdims=True)
            s1 = jnp.sum(jnp.where(first, 0.0, o2), axis=-1, keepdims=True)
            r = lax.rsqrt(jnp.where(first, s0, s1) * (1.0 / FOX_HEAD_DIM) + EPS)
            fn_ref[rows, :] = (o_acc * r * fnw_ref[...]).astype(BF)

    blk = lambda off: pl.BlockSpec((t, LANES), lambda j: (0, off + j))
    return pl.pallas_call(
        body, name="fox_fwd", grid=(N_FOX_HEADS // 2,),
        in_specs=[blk(0), blk(4), blk(8), pl.BlockSpec((LANES, t), lambda j: (0, 0)),
                  pl.BlockSpec((1, LANES), lambda j: (0, 0))],
        out_specs=[blk(0), blk(0), blk(0)],
        out_shape=[_sds((t, D_FOX)), _sds((t, D_FOX)), _sds((t, D_FOX), BF)],
        compiler_params=_params("parallel"),
    )(proj, proj, proj, cumt, fnw)


def _fox_bwd(proj, cumt, lse, o, do):
    t = proj.shape[0]
    tq = min(TOKEN_BLOCK, t // 2)
    nq = t // tq

    def body(q_ref, k_ref, v_ref, cumt_ref, lse_ref, o_ref, do_ref,
             dq_ref, dk_ref, dv_ref, dcq_ref, dckt_ref, dk_s, dv_s):
        j = pl.program_id(0)

        @pl.when(j == 0)
        def _():
            dcq_ref[...] = jnp.zeros_like(dcq_ref)
            dckt_ref[...] = jnp.zeros_like(dckt_ref)

        lane = _iota((1, LANES), 1)

        first = _iota((1, LANES), 1) < FOX_HEAD_DIM
        kb = k_ref[...].astype(BF)
        vb = v_ref[...].astype(BF)
        dk_s[...] = jnp.zeros_like(dk_s)
        dv_s[...] = jnp.zeros_like(dv_s)
        for i in range(nq):
            rows = slice(i * tq, (i + 1) * tq)
            do_i = do_ref[rows, :]
            prod = do_i * o_ref[rows, :]
            lse_i = lse_ref[rows, :]
            q2 = _fox_stack(q_ref[rows, :] * FOX_SCALE, first)
            do2 = _fox_stack(do_i, first)
            delta = jnp.concatenate([jnp.sum(jnp.where(first, prod, 0.0), axis=-1, keepdims=True),
                                     jnp.sum(jnp.where(first, 0.0, prod), axis=-1, keepdims=True)], axis=0)
            lse2 = jnp.concatenate([lse_i[:, 0:1], lse_i[:, FOX_HEAD_DIM:FOX_HEAD_DIM + 1]], axis=0)
            dq2 = jnp.zeros((2 * tq, LANES), F32)
            dcq2 = jnp.zeros((2 * tq, 1), F32)
            for s, lo, hi in _fox_logits(q2, kb, cumt_ref, j, i, tq):
                p = jnp.exp(s - lse2)
                ds = p * (_mm_nt(do2, vb[lo:hi]) - delta)
                dsb = ds.astype(BF)
                dq2 = dq2 + jnp.dot(dsb, kb[lo:hi], preferred_element_type=F32)
                dk_s[lo:hi, :] += _mm_tn(dsb, q2)
                dv_s[lo:hi, :] += _mm_tn(p, do2)
                dcq2 = dcq2 + jnp.sum(ds, axis=-1, keepdims=True)
                dckt_ref[pl.ds(2 * j, 1), lo:hi] += jnp.sum(ds[:tq], axis=0, keepdims=True)
                dckt_ref[pl.ds(2 * j + 1, 1), lo:hi] += jnp.sum(ds[tq:], axis=0, keepdims=True)
            dq_ref[rows, :] = _fox_unstack(dq2, first) * FOX_SCALE
            dcq_ref[rows, :] += jnp.where(lane == 2 * j, dcq2[:tq], jnp.where(lane == 2 * j + 1, dcq2[tq:], 0.0))
        dk_ref[...] = dk_s[...]
        dv_ref[...] = dv_s[...]

    blk = lambda off: pl.BlockSpec((t, LANES), lambda j: (0, off + j))
    rows128 = pl.BlockSpec((LANES, t), lambda j: (0, 0))
    return pl.pallas_call(
        body, name="fox_bwd", grid=(N_FOX_HEADS // 2,),
        in_specs=[blk(0), blk(4), blk(8), rows128, blk(0), blk(0), blk(0)],
        out_specs=[blk(0), blk(0), blk(0), pl.BlockSpec((t, LANES), lambda j: (0, 0)), rows128],
        out_shape=[_sds((t, D_FOX))] * 3 + [_sds((t, LANES)), _sds((LANES, t))],
        scratch_shapes=[pltpu.VMEM((t, LANES), F32), pltpu.VMEM((t, LANES), F32)],
        compiler_params=_params("arbitrary"),
    )(proj, proj, proj, cumt, lse, o, do)


def _conv(x, w, row):
    return (w[3:4, :] * x + w[2:3, :] * _shift_down(x, 1, row) + w[1:2, :] * _shift_down(x, 2, row)
            + w[0:1, :] * _shift_down(x, 3, row))


def _chunk_decay(gc_c):
    gi = gc_c[:, 0:CHUNK]
    gj = gc_c.T[0:CHUNK, :]
    ri = _iota((CHUNK, CHUNK), 0)
    cj = _iota((CHUNK, CHUNK), 1)
    return jnp.where(ri >= cj, jnp.exp(jnp.minimum(gi - gj, 0.0)), 0.0), ri > cj


def _gdn_specs(t):
    col = lambda off: pl.BlockSpec((t, LANES), lambda h: (0, off + h))
    cw = lambda off: pl.BlockSpec((CONV_K, LANES), lambda h: (0, off + h))
    mat = pl.BlockSpec((1, t // CHUNK, CHUNK, CHUNK), lambda h: (h, 0, 0, 0))
    return col, cw, mat


def _gdn_prep(proj, convw, beta, g):
    t = proj.shape[0]
    nch = t // CHUNK

    def body(xq_ref, xk_ref, xv_ref, wq_ref, wk_ref, wv_ref, beta_ref, g_ref,
             qn_ref, kn_ref, cv_ref, gc_ref, be_ref, m_ref, a_ref):
        row = _iota((t, LANES), 0)
        hd = pl.program_id(0)
        be_ref[...] = jnp.broadcast_to(_lane_column(beta_ref[...], SM_GB + hd), (t, LANES))

        def act(x_ref, w_ref):
            y = _conv(x_ref[...], w_ref[...], row)
            return y * _sigmoid(y)

        cq = act(xq_ref, wq_ref)
        ck = act(xk_ref, wk_ref)
        cv_ref[...] = act(xv_ref, wv_ref)
        qn_ref[...] = cq * lax.rsqrt(jnp.sum(cq * cq, axis=-1, keepdims=True) + EPS) * GDN_QSCALE
        kn_ref[...] = ck * lax.rsqrt(jnp.sum(ck * ck, axis=-1, keepdims=True) + EPS)
        gc = jnp.broadcast_to(_lane_column(g_ref[...], SM_GA + hd), (t, LANES))
        pos = row % CHUNK
        step = 1
        while step < CHUNK:
            gc = gc + jnp.where(pos >= step, pltpu.roll(gc, step, 0), 0.0)
            step *= 2
        gc_ref[...] = gc

        group = 4 if nch % 4 == 0 else 1

        def chunks(gi, carry):
            ns = [gi * group + c for c in range(group)]
            sls = [pl.ds(pl.multiple_of(n * CHUNK, CHUNK), CHUNK) for n in ns]
            ks = [kn_ref[sl, :] for sl in sls]
            kk = [_mm_nt(k_c * be_ref[sl, :], k_c) for k_c, sl in zip(ks, sls)]
            qk = [_mm_nt(qn_ref[sl, :], k_c) for k_c, sl in zip(ks, sls)]
            for c, n in enumerate(ns):
                decay, strict = _chunk_decay(gc_ref[sls[c], :])
                m_ref[0, n] = jnp.where(strict, kk[c] * decay, 0.0)
                a_ref[0, n] = qk[c] * decay
            return carry

        lax.fori_loop(0, nch // group, chunks, 0)

    col, cw, mat = _gdn_specs(t)
    return pl.pallas_call(
        body, name="gdn_prep", grid=(N_GDN_HEADS,),
        in_specs=[col(12), col(16), col(20), cw(0), cw(4), cw(8)] + [pl.BlockSpec((t, LANES), lambda h: (0, 0))] * 2,
        out_specs=[col(0), col(0), col(0), col(0), col(0), mat, mat],
        out_shape=[_sds((t, D_GDN))] * 5 + [_sds((N_GDN_HEADS, nch, CHUNK, CHUNK))] * 2,
        compiler_params=_params("parallel"),
    )(proj, proj, proj, convw, convw, convw, beta, g)


def _tri_inverse(m3):
    assert m3.shape == (LANES, CHUNK, CHUNK)

    def body(m_ref, t_ref, ms, ts):
        for i in range(CHUNK):
            ms[i * CHUNK:(i + 1) * CHUNK, :] = m_ref[:, i, :].T
        cidx = _iota((CHUNK, LANES), 0)

        def outer(i, carry):
            def inner(jj, acc):
                mrow = ms[pl.ds(i * CHUNK + jj, 1), :]
                return acc - mrow * ts[pl.ds(pl.multiple_of(jj * CHUNK, CHUNK), CHUNK), :]

            acc = lax.fori_loop(0, i, inner, jnp.where(cidx == i, 1.0, 0.0).astype(F32))
            ts[pl.ds(pl.multiple_of(i * CHUNK, CHUNK), CHUNK), :] = acc
            return carry

        lax.fori_loop(0, CHUNK, outer, 0)
        for i in range(CHUNK):
            t_ref[:, i, :] = ts[i * CHUNK:(i + 1) * CHUNK, :].T

    return pl.pallas_call(
        body, name="tri_inverse", in_specs=[VMEM_SPEC], out_specs=VMEM_SPEC,
        out_shape=_sds((LANES, CHUNK, CHUNK)),
        scratch_shapes=[pltpu.VMEM((CHUNK * CHUNK, LANES), F32), pltpu.VMEM((CHUNK * CHUNK, LANES), F32)],
        compiler_params=_params(),
    )(m3)


def _gdn_chunk_terms(q, k, v, b, gcc):
    eg = jnp.exp(gcc)
    last = gcc[CHUNK - 1:CHUNK, :]
    egl = jnp.exp(last - gcc)
    gl = jnp.exp(last)
    kb = k * b
    return eg, egl, gl, kb, v * b, kb * eg, q * eg, k * egl


GDN_BLOCK_CHUNKS = 4


def _gdn_block_specs(t, reverse):
    cb = GDN_BLOCK_CHUNKS
    nb = t // (cb * CHUNK)
    idx = (lambda i: nb - 1 - i) if reverse else (lambda i: i)
    tok = pl.BlockSpec((cb * CHUNK, D_GDN), lambda i: (idx(i), 0))
    mat = pl.BlockSpec((N_GDN_HEADS, cb, CHUNK, CHUNK), lambda i: (0, idx(i), 0, 0))
    state = pl.BlockSpec((N_GDN_HEADS, cb, GDN_HEAD_DIM, GDN_HEAD_DIM), lambda i: (0, idx(i), 0, 0))
    return nb, tok, mat, state


def _gdn_scan(qn, kn, cv, be, gc, tinv, amat):
    t = qn.shape[0]
    nch = t // CHUNK

    def body(q_ref, k_ref, v_ref, b_ref, gc_ref, t_ref, a_ref, o_ref, sall_ref, vn_ref, s_scr):
        @pl.when(pl.program_id(0) == 0)
        def _():
            s_scr[...] = jnp.zeros_like(s_scr)

        heads = range(N_GDN_HEADS)
        cols = [slice(hd * LANES, (hd + 1) * LANES) for hd in heads]
        s = [s_scr[hd] for hd in heads]
        for cc in range(GDN_BLOCK_CHUNKS):
            rs = slice(cc * CHUNK, (cc + 1) * CHUNK)
            terms = [_gdn_chunk_terms(q_ref[rs, cs], k_ref[rs, cs], v_ref[rs, cs], b_ref[rs, cs], gc_ref[rs, cs])
                     for cs in cols]
            for hd in heads:
                sall_ref[hd, cc] = s[hd]
            uw = [_mm(t_ref[hd, cc], jnp.concatenate([terms[hd][4], terms[hd][5]], axis=1)) for hd in heads]
            ws_qs = [_mm(jnp.concatenate([uw[hd][:, LANES:], terms[hd][6]], axis=0), s[hd]) for hd in heads]
            vn = [uw[hd][:, :LANES] - ws_qs[hd][:CHUNK] for hd in heads]
            a_vn = [_mm(a_ref[hd, cc], vn[hd]) for hd in heads]
            kd_vn = [_mm_tn(terms[hd][7], vn[hd]) for hd in heads]
            for hd in heads:
                vn_ref[rs, cols[hd]] = vn[hd]
                o_ref[rs, cols[hd]] = ws_qs[hd][CHUNK:] + a_vn[hd]
                s[hd] = s[hd] * terms[hd][2] + kd_vn[hd]
        for hd in heads:
            s_scr[hd] = s[hd]

    nb, tok, mat, state = _gdn_block_specs(t, False)
    return pl.pallas_call(
        body, name="gdn_scan", grid=(nb,),
        in_specs=[tok] * 5 + [mat, mat], out_specs=[tok, state, tok],
        out_shape=[_sds((t, D_GDN)), _sds((N_GDN_HEADS, nch, GDN_HEAD_DIM, GDN_HEAD_DIM)), _sds((t, D_GDN))],
        scratch_shapes=[pltpu.VMEM((N_GDN_HEADS, GDN_HEAD_DIM, GDN_HEAD_DIM), F32)],
        compiler_params=_params("arbitrary"),
    )(qn, kn, cv, be, gc, tinv, amat)


def _gdn_bwd(qn, kn, cv, be, gc, tinv, amat, s_all, vn_all, do):
    t = qn.shape[0]

    def body(q_ref, k_ref, v_ref, b_ref, gc_ref, t_ref, a_ref, sall_ref, vn_ref, do_ref,
             dq_ref, dk_ref, dv_ref, db_ref, dg_ref, ds_scr):
        @pl.when(pl.program_id(0) == 0)
        def _():
            ds_scr[...] = jnp.zeros_like(ds_scr)

        lastrow = _iota((CHUNK, LANES), 0) == CHUNK - 1
        heads = range(N_GDN_HEADS)
        cols = [slice(hd * LANES, (hd + 1) * LANES) for hd in heads]
        each = lambda fn: [fn(hd) for hd in heads]
        rows_cat = lambda x, y: jnp.concatenate([x, y], axis=0)
        lane_cat = lambda x, y: jnp.concatenate([x, y], axis=1)
        dsp = each(lambda hd: ds_scr[hd])
        for cc in reversed(range(GDN_BLOCK_CHUNKS)):
            rs = slice(cc * CHUNK, (cc + 1) * CHUNK)
            q = each(lambda hd: q_ref[rs, cols[hd]])
            k = each(lambda hd: k_ref[rs, cols[hd]])
            v = each(lambda hd: v_ref[rs, cols[hd]])
            b = each(lambda hd: b_ref[rs, cols[hd]])
            gcc = each(lambda hd: gc_ref[rs, cols[hd]])
            do_c = each(lambda hd: do_ref[rs, cols[hd]])
            vn = each(lambda hd: vn_ref[rs, cols[hd]])
            tn = each(lambda hd: t_ref[hd, cc])
            st = each(lambda hd: sall_ref[hd, cc])
            terms = each(lambda hd: _gdn_chunk_terms(q[hd], k[hd], v[hd], b[hd], gcc[hd]))
            eg, egl, gl, kb, vb, kbg, qd, kd = [[terms[hd][i] for hd in heads] for i in range(8)]
            w = each(lambda hd: _mm(tn[hd], kbg[hd]))
            a_do = each(lambda hd: _mm_tn(a_ref[hd, cc], do_c[hd]))
            kd_ds = each(lambda hd: _mm(kd[hd], dsp[hd]))
            da = each(lambda hd: _mm_nt(do_c[hd], vn[hd]))
            dkd = each(lambda hd: _mm_nt(vn[hd], dsp[hd]))
            by_k = each(lambda hd: _mm_nt(rows_cat(kb[hd], q[hd]), k[hd]))
            dgl = each(lambda hd: jnp.sum(jnp.sum(dsp[hd] * st[hd], axis=-1, keepdims=True), axis=0, keepdims=True))
            dvn = each(lambda hd: a_do[hd] + kd_ds[hd])
            do_dvn = each(lambda hd: rows_cat(do_c[hd], dvn[hd]))
            by_s = each(lambda hd: _mm_nt(do_dvn[hd], st[hd]))
            dqd = each(lambda hd: by_s[hd][:CHUNK])
            dvn_dw = each(lambda hd: lane_cat(dvn[hd], -by_s[hd][CHUNK:]))
            dsp = each(lambda hd: _mm_tn(rows_cat(qd[hd], -w[hd]), do_dvn[hd]) + gl[hd] * dsp[hd])
            dt = each(lambda hd: _mm_nt(dvn_dw[hd], lane_cat(vb[hd], kbg[hd])))
            by_t = each(lambda hd: _mm_tn(tn[hd], dvn_dw[hd]))
            tt_dt = each(lambda hd: _mm_tn(tn[hd], dt[hd]))
            dm_raw = each(lambda hd: _mm_nt(tt_dt[hd], tn[hd]))
            masks = each(lambda hd: _chunk_decay(gcc[hd]))
            dkk = each(lambda hd: jnp.where(masks[hd][1], -dm_raw[hd], 0.0) * masks[hd][0])
            dqk = each(lambda hd: da[hd] * masks[hd][0])
            dqk_dkk = each(lambda hd: rows_cat(dqk[hd], dkk[hd]))
            on_k = each(lambda hd: _mm(dqk_dkk[hd], k[hd]))
            dk_mm = each(lambda hd: _mm_tn(dqk_dkk[hd], rows_cat(q[hd], kb[hd])))
            for hd in heads:
                cs = cols[hd]
                dvb, dkbg = by_t[hd][:, :LANES], by_t[hd][:, LANES:]
                gmat = dkk[hd] * by_k[hd][:CHUNK] + dqk[hd] * by_k[hd][CHUNK:]
                dq_ref[rs, cs] = dqd[hd] * eg[hd] + on_k[hd][:CHUNK]
                dkb = on_k[hd][CHUNK:] + dkbg * eg[hd]
                dk_ref[rs, cs] = dkd[hd] * egl[hd] + dk_mm[hd] + dkb * b[hd]
                db = jnp.sum(dkb * k[hd], axis=-1, keepdims=True) + jnp.sum(dvb * v[hd], axis=-1, keepdims=True)
                db_ref[rs, cs] = jnp.broadcast_to(db, (CHUNK, LANES))
                dv_ref[rs, cs] = dvb * b[hd]
                dkd_kd = jnp.sum(dkd[hd] * kd[hd], axis=-1, keepdims=True)
                col_sums = jnp.sum(lane_cat(gmat, jnp.zeros_like(gmat)).T, axis=-1, keepdims=True)
                dgc = (jnp.sum(gmat, axis=-1, keepdims=True) - col_sums[:CHUNK]
                       + jnp.sum(dqd[hd] * qd[hd], axis=-1, keepdims=True)
                       + jnp.sum(dkbg * kbg[hd], axis=-1, keepdims=True) - dkd_kd)
                extra = jnp.sum(dkd_kd, axis=0, keepdims=True) + dgl[hd] * gl[hd]
                dg_ref[rs, cs] = dgc + jnp.where(lastrow, extra, 0.0)
        for hd in heads:
            ds_scr[hd] = dsp[hd]
        dg = dg_ref[...]
        row = _iota(dg.shape, 0)
        pos = row % CHUNK
        step = 1
        while step < CHUNK:
            dg = dg + jnp.where(pos < CHUNK - step, pltpu.roll(dg, dg.shape[0] - step, 0), 0.0)
            step *= 2
        dg_ref[...] = dg

    nb, tok, mat, state = _gdn_block_specs(t, True)
    return pl.pallas_call(
        body, name="gdn_bwd", grid=(nb,),
        in_specs=[tok] * 5 + [mat, mat, state, tok, tok], out_specs=[tok] * 5, out_shape=[_sds((t, D_GDN))] * 5,
        scratch_shapes=[pltpu.VMEM((N_GDN_HEADS, GDN_HEAD_DIM, GDN_HEAD_DIM), F32)],
        compiler_params=_params("arbitrary"),
    )(qn, kn, cv, be, gc, tinv, amat, s_all, vn_all, do)


def _gdn_bwd_conv(proj, convw, dqn, dkn, dcv):
    t = proj.shape[0]

    def body(xq_ref, xk_ref, xv_ref, wq_ref, wk_ref, wv_ref, dq_ref, dk_ref, dv_ref,
             dxq_ref, dxk_ref, dxv_ref, dwq_ref, dwk_ref, dwv_ref):
        row = _iota((t, LANES), 0)

        def one(x_ref, w_ref, d_ref, dx_ref, dw_ref, scale):
            x = x_ref[...]
            w = w_ref[...]
            y = _conv(x, w, row)
            sg = _sigmoid(y)
            dc = d_ref[...]
            if scale is not None:
                c = y * sg
                r = lax.rsqrt(jnp.sum(c * c, axis=-1, keepdims=True) + EPS)
                ch = c * r
                dc = scale * r * (dc - ch * jnp.sum(dc * ch, axis=-1, keepdims=True))
            dy = dc * sg * (1.0 + y * (1.0 - sg))
            dx_ref[...] = (w[3:4, :] * dy + w[2:3, :] * _shift_up(dy, 1, row) + w[1:2, :] * _shift_up(dy, 2, row)
                           + w[0:1, :] * _shift_up(dy, 3, row))
            for jj in range(CONV_K):
                xs = x if jj == CONV_K - 1 else _shift_down(x, CONV_K - 1 - jj, row)
                dw_ref[jj:jj + 1, :] = jnp.sum(dy * xs, axis=0, keepdims=True)

        one(xq_ref, wq_ref, dq_ref, dxq_ref, dwq_ref, GDN_QSCALE)
        one(xk_ref, wk_ref, dk_ref, dxk_ref, dwk_ref, 1.0)
        one(xv_ref, wv_ref, dv_ref, dxv_ref, dwv_ref, None)

    col, cw, _ = _gdn_specs(t)
    return pl.pallas_call(
        body, name="gdn_bwd_conv", grid=(N_GDN_HEADS,),
        in_specs=[col(12), col(16), col(20), cw(0), cw(4), cw(8), col(0), col(0), col(0)],
        out_specs=[col(0), col(0), col(0), cw(0), cw(0), cw(0)],
        out_shape=[_sds((t, D_GDN))] * 3 + [_sds((CONV_K, D_GDN))] * 3,
        compiler_params=_params("parallel"),
    )(proj, proj, proj, convw, convw, convw, dqn, dkn, dcv)


def _mix_out(fox_n, gdn_o, proj, gnw, w_out, x, pmw, plw):
    t = x.shape[0]
    tm = min(MATMUL_BLOCK, t)

    def body(fn_ref, go_ref, gz_ref, gnw_ref, w_ref, x_ref, pmw_ref, plw_ref, x1_ref, h2_ref, mixed_ref, omix_ref,
             h2t_ref):
        omix_ref[:, 0:D_FOX] = fn_ref[...]
        for hd in range(N_GDN_HEADS):
            cs = slice(hd * LANES, (hd + 1) * LANES)
            go = go_ref[:, cs]
            r = lax.rsqrt(jnp.mean(go * go, axis=-1, keepdims=True) + EPS)
            gz = gz_ref[:, cs]
            omix_ref[:, D_FOX + hd * LANES:D_FOX + (hd + 1) * LANES] = (
                go * r * gnw_ref[...] * (gz * _sigmoid(gz))).astype(BF)
        mixed = jnp.dot(omix_ref[...], w_ref[...], preferred_element_type=F32)
        mixed_ref[...] = mixed
        r2 = lax.rsqrt(jnp.mean(mixed * mixed, axis=-1, keepdims=True) + EPS)
        x1 = x_ref[...] + mixed * r2 * pmw_ref[...]
        x1_ref[...] = x1
        r3 = lax.rsqrt(jnp.mean(x1 * x1, axis=-1, keepdims=True) + EPS)
        h2 = x1 * r3 * plw_ref[...]
        h2_ref[...] = h2.astype(BF)
        h2t_ref[...] = h2.T.astype(BF)

    tok = lambda w: pl.BlockSpec((tm, w), lambda i: (i, 0))
    vec = lambda w: pl.BlockSpec((1, w), lambda i: (0, 0))
    return pl.pallas_call(
        body, name="mix_out", grid=(t // tm,),
        in_specs=[tok(D_FOX), tok(D_GDN), pl.BlockSpec((tm, D_GDN), lambda i: (i, COL_GZ // D_GDN)), vec(LANES),
                  pl.BlockSpec((D_MODEL, D_MODEL), lambda i: (0, 0)), tok(D_MODEL), vec(D_MODEL), vec(D_MODEL)],
        out_specs=[tok(D_MODEL)] * 4 + [pl.BlockSpec((D_MODEL, tm), lambda i: (0, i))],
        out_shape=[_sds((t, D_MODEL)), _sds((t, D_MODEL), BF), _sds((t, D_MODEL)), _sds((t, D_MODEL), BF),
                   _sds((D_MODEL, t), BF)],
        compiler_params=_params("parallel"),
    )(fox_n, gdn_o, proj, gnw, w_out, x, pmw, plw)


def _out_bwd(dmixed, w_out, o_fox, gdn_o, proj, fnw, gnw):
    t = dmixed.shape[0]
    tm = min(MATMUL_BLOCK, t)

    def body(dm_ref, w_ref, of_ref, go_ref, gz_ref, fnw_ref, gnw_ref, dof_ref, dgo_ref, dgz_ref, dfw_ref, dgw_ref):
        i = pl.program_id(0)

        @pl.when(i == 0)
        def _():
            dfw_ref[...] = jnp.zeros_like(dfw_ref)
            dgw_ref[...] = jnp.zeros_like(dgw_ref)

        domix = _mm_nt(dm_ref[...], w_ref[...])
        first = _iota((1, LANES), 1) < FOX_HEAD_DIM
        dfw = jnp.zeros((1, LANES), F32)
        dgw = jnp.zeros((1, LANES), F32)
        for pr in range(N_FOX_HEADS // 2):
            cs = slice(pr * LANES, (pr + 1) * LANES)
            o = of_ref[:, cs]
            dfn = domix[:, cs]
            o2 = o * o
            s0 = jnp.sum(jnp.where(first, o2, 0.0), axis=-1, keepdims=True)
            s1 = jnp.sum(jnp.where(first, 0.0, o2), axis=-1, keepdims=True)
            r = lax.rsqrt(jnp.where(first, s0, s1) * (1.0 / FOX_HEAD_DIM) + EPS)
            oh = o * r
            dfw = dfw + jnp.sum(dfn * oh, axis=0, keepdims=True)
            doh = dfn * fnw_ref[...]
            pr_ = doh * oh
            m0 = jnp.sum(jnp.where(first, pr_, 0.0), axis=-1, keepdims=True)
            m1 = jnp.sum(jnp.where(first, 0.0, pr_), axis=-1, keepdims=True)
            dof_ref[:, cs] = r * (doh - oh * jnp.where(first, m0, m1) * (1.0 / FOX_HEAD_DIM))
        for hd in range(N_GDN_HEADS):
            cs = slice(hd * LANES, (hd + 1) * LANES)
            go = go_ref[:, cs]
            gz = gz_ref[:, cs]
            dgated = domix[:, D_FOX + hd * LANES:D_FOX + (hd + 1) * LANES]
            r = lax.rsqrt(jnp.mean(go * go, axis=-1, keepdims=True) + EPS)
            goh = go * r
            sg = _sigmoid(gz)
            sz = gz * sg
            gn = goh * gnw_ref[...]
            dgn = dgated * sz
            dgz_ref[:, cs] = dgated * gn * sg * (1.0 + gz * (1.0 - sg))
            dgw = dgw + jnp.sum(dgn * goh, axis=0, keepdims=True)
            dgh = dgn * gnw_ref[...]
            dgo_ref[:, cs] = r * (dgh - goh * jnp.mean(dgh * goh, axis=-1, keepdims=True))
        dfw_ref[...] += dfw + pltpu.roll(dfw, FOX_HEAD_DIM, 1)
        dgw_ref[...] += dgw

    tok = lambda w: pl.BlockSpec((tm, w), lambda i: (i, 0))
    vec = lambda w: pl.BlockSpec((1, w), lambda i: (0, 0))
    return pl.pallas_call(
        body, name="out_bwd", grid=(t // tm,),
        in_specs=[tok(D_MODEL), pl.BlockSpec((D_MODEL, D_MODEL), lambda i: (0, 0)), tok(D_FOX), tok(D_GDN),
                  pl.BlockSpec((tm, D_GDN), lambda i: (i, COL_GZ // D_GDN)), vec(LANES), vec(LANES)],
        out_specs=[tok(D_FOX), tok(D_GDN), tok(D_GDN), vec(LANES), vec(LANES)],
        out_shape=[_sds((t, D_FOX)), _sds((t, D_GDN)), _sds((t, D_GDN)), _sds((1, LANES)), _sds((1, LANES))],
        compiler_params=_params("arbitrary"),
    )(dmixed, w_out, o_fox, gdn_o, proj, fnw, gnw)


def _mlp_up(h2, w_up):
    t = h2.shape[0]
    tm = min(MATMUL_BLOCK, t)
    pc = D_FF // N_DEV

    def body(h_ref, w_ref, up_ref):
        h = h_ref[...]
        for p in range(N_DEV):
            up_ref[:, p * pc:(p + 1) * pc] = jnp.dot(h, w_ref[p], preferred_element_type=F32).astype(BF)

    return pl.pallas_call(
        body, name="mlp_up", grid=(t // tm,),
        in_specs=[pl.BlockSpec((tm, D_MODEL), lambda i: (i, 0)),
                  pl.BlockSpec((N_DEV, D_MODEL, pc), lambda i: (0, 0, 0))],
        out_specs=pl.BlockSpec((tm, D_FF), lambda i: (i, 0)), out_shape=_sds((t, D_FF), BF),
        compiler_params=_params("parallel"),
    )(h2, w_up)


def _mlp_down_loss(up, w_down, x1, pw, target):
    t = up.shape[0]
    tm = min(MATMUL_BLOCK, t)

    def body(up_ref, w_ref, x1_ref, pw_ref, tg_ref, dy_ref, dx2_ref, loss_ref, dpw_ref):
        i = pl.program_id(0)

        @pl.when(i == 0)
        def _():
            loss_ref[...] = jnp.zeros_like(loss_ref)
            dpw_ref[...] = jnp.zeros_like(dpw_ref)

        u = jnp.maximum(up_ref[...].astype(F32), 0.0)
        y = jnp.dot((u * u).astype(BF), w_ref[...], preferred_element_type=F32)
        r = lax.rsqrt(jnp.mean(y * y, axis=-1, keepdims=True) + EPS)
        yh = y * r
        pw = pw_ref[...]
        err = x1_ref[...] + yh * pw - tg_ref[...]
        part = jnp.sum(jnp.sum(err * err, axis=-1, keepdims=True), axis=0, keepdims=True) * (0.5 / D_MODEL)
        loss_ref[...] += jnp.broadcast_to(part, loss_ref.shape)
        dx2 = err * (1.0 / D_MODEL)
        dx2_ref[...] = dx2
        dpw_ref[...] += jnp.sum(dx2 * yh, axis=0, keepdims=True)
        dyh = dx2 * pw
        dy_ref[...] = (r * (dyh - yh * jnp.mean(dyh * yh, axis=-1, keepdims=True))).astype(BF)

    tok = lambda w: pl.BlockSpec((tm, w), lambda i: (i, 0))
    vec = lambda w: pl.BlockSpec((1, w), lambda i: (0, 0))
    return pl.pallas_call(
        body, name="mlp_down_loss", grid=(t // tm,),
        in_specs=[tok(D_FF), pl.BlockSpec((D_FF, D_MODEL), lambda i: (0, 0)), tok(D_MODEL), vec(D_MODEL), tok(D_MODEL)],
        out_specs=[tok(D_MODEL), tok(D_MODEL), vec(LANES), vec(D_MODEL)],
        out_shape=[_sds((t, D_MODEL), BF), _sds((t, D_MODEL)), _sds((1, LANES)), _sds((1, D_MODEL))],
        compiler_params=_params("arbitrary"),
    )(up, w_down, x1, pw, target)


def _mlp_bwd_act(dy, w_down, up):
    t = dy.shape[0]
    tm = min(MATMUL_BLOCK, t)

    def body(dy_ref, w_ref, up_ref, dup_ref):
        da = lax.dot_general(dy_ref[...], w_ref[...], (((1,), (1,)), ((), ())), preferred_element_type=F32)
        dup_ref[...] = (da * (2.0 * jnp.maximum(up_ref[...].astype(F32), 0.0))).astype(BF)

    return pl.pallas_call(
        body, name="mlp_bwd_act", grid=(t // tm,),
        in_specs=[pl.BlockSpec((tm, D_MODEL), lambda i: (i, 0)), pl.BlockSpec((D_FF, D_MODEL), lambda i: (0, 0)),
                  pl.BlockSpec((tm, D_FF), lambda i: (i, 0))],
        out_specs=pl.BlockSpec((tm, D_FF), lambda i: (i, 0)), out_shape=_sds((t, D_FF), BF),
        compiler_params=_params("parallel"),
    )(dy, w_down, up)


def _mlp_bwd_in(dup, w_up, x1, plw, dx2, mixed, pmw):
    t = dup.shape[0]
    tm = min(MATMUL_BLOCK, t)

    def body(dup_ref, w_ref, x1_ref, plw_ref, dx2_ref, mx_ref, pmw_ref, dx1_ref, dmixed_ref, dplw_ref, dpmw_ref):
        i = pl.program_id(0)

        @pl.when(i == 0)
        def _():
            dplw_ref[...] = jnp.zeros_like(dplw_ref)
            dpmw_ref[...] = jnp.zeros_like(dpmw_ref)

        pc = D_FF // N_DEV
        dh = _mm_nt(dup_ref[:, 0:pc], w_ref[0])
        for p in range(1, N_DEV):
            dh = dh + _mm_nt(dup_ref[:, p * pc:(p + 1) * pc], w_ref[p])
        x1 = x1_ref[...]
        r = lax.rsqrt(jnp.mean(x1 * x1, axis=-1, keepdims=True) + EPS)
        xh = x1 * r
        dplw_ref[...] += jnp.sum(dh * xh, axis=0, keepdims=True)
        dxh = dh * plw_ref[...]
        dx1 = dx2_ref[...] + r * (dxh - xh * jnp.mean(dxh * xh, axis=-1, keepdims=True))
        dx1_ref[...] = dx1
        mx = mx_ref[...]
        r2 = lax.rsqrt(jnp.mean(mx * mx, axis=-1, keepdims=True) + EPS)
        mh = mx * r2
        dpmw_ref[...] += jnp.sum(dx1 * mh, axis=0, keepdims=True)
        dmh = dx1 * pmw_ref[...]
        dmixed_ref[...] = (r2 * (dmh - mh * jnp.mean(dmh * mh, axis=-1, keepdims=True))).astype(BF)

    tok = lambda w: pl.BlockSpec((tm, w), lambda i: (i, 0))
    vec = lambda w: pl.BlockSpec((1, w), lambda i: (0, 0))
    return pl.pallas_call(
        body, name="mlp_bwd_in", grid=(t // tm,),
        in_specs=[tok(D_FF), pl.BlockSpec((N_DEV, D_MODEL, D_FF // N_DEV), lambda i: (0, 0, 0)), tok(D_MODEL),
                  vec(D_MODEL), tok(D_MODEL), tok(D_MODEL), vec(D_MODEL)],
        out_specs=[tok(D_MODEL), tok(D_MODEL), vec(D_MODEL), vec(D_MODEL)],
        out_shape=[_sds((t, D_MODEL)), _sds((t, D_MODEL), BF), _sds((1, D_MODEL)), _sds((1, D_MODEL))],
        compiler_params=_params("arbitrary"),
    )(dup, w_up, x1, plw, dx2, mixed, pmw)


def _wgrad(a, b, a_cols, split=1, a_fn=None, a_block0=0, name="wgrad"):
    t, b_cols = b.shape
    n_a = (a.shape[1] - a_block0 * a_cols) // a_cols if a_block0 else a.shape[1] // a_cols

    def body(a_ref, b_ref, o_ref):
        av = a_ref[...]
        if a_fn is not None:
            av = a_fn(av)
        o_ref[...] = _mm_tn(av, b_ref[...]).astype(BF).reshape(o_ref.shape)

    return pl.pallas_call(
        body, name=name, grid=(n_a,),
        in_specs=[pl.BlockSpec((t, a_cols), lambda i: (0, i + a_block0)), pl.BlockSpec((t, b_cols), lambda i: (0, 0))],
        out_specs=pl.BlockSpec((split, a_cols // split, b_cols), lambda i: (i, 0, 0)),
        out_shape=_sds((n_a * split, a_cols // split, b_cols), BF),
        compiler_params=_params("parallel"),
    )(a, b)


def _wgrad_pre_t(at, b, b_cols, name):
    rows, t = at.shape
    n_b = b.shape[1] // b_cols

    def body(a_ref, b_ref, o_ref):
        o_ref[0] = jnp.dot(a_ref[...], b_ref[...], preferred_element_type=F32).astype(BF)

    return pl.pallas_call(
        body, name=name, grid=(n_b,),
        in_specs=[pl.BlockSpec((rows, t), lambda j: (0, 0)), pl.BlockSpec((t, b_cols), lambda j: (0, j))],
        out_specs=pl.BlockSpec((1, rows, b_cols), lambda j: (j, 0, 0)), out_shape=_sds((n_b, rows, b_cols), BF),
        compiler_params=_params("parallel"),
    )(at, b)


def _small_bwd(proj, fb, al, dtb, dcq, dckt, dbe, dge):
    t = proj.shape[0]

    def body(sm_ref, fb_ref, al_ref, dtb_ref, dcq_ref, dckt_ref, dbe_ref, dge_ref, dsm_ref, dvec_ref):
        s = sm_ref[...]
        lane = _iota((1, LANES), 1)
        dcum = dcq_ref[...] - dckt_ref[...].T
        row = _iota((t, LANES), 0)
        step = 1
        while step < t:
            dcum = dcum + _shift_up(dcum, step, row)
            step *= 2
        dff = dcum * _sigmoid(-(s + fb_ref[...]))
        dbeta = jnp.zeros((t, LANES), F32)
        dg = jnp.zeros((t, LANES), F32)
        for hd in range(N_GDN_HEADS):
            dbeta = jnp.where(lane == SM_GB + hd, dbe_ref[:, hd * LANES:hd * LANES + 1], dbeta)
            dg = jnp.where(lane == SM_GA + hd, dge_ref[:, hd * LANES:hd * LANES + 1], dg)
        beta = _sigmoid(s)
        dgb = dbeta * beta * (1.0 - beta)
        za = s + dtb_ref[...]
        nea = -jnp.exp(al_ref[...])
        dga = dg * nea * _sigmoid(za)
        is_f = lane < SM_GB
        is_b = (lane >= SM_GB) & (lane < SM_GA)
        is_a = (lane >= SM_GA) & (lane < SM_GA + 4)
        dsm_ref[...] = jnp.where(is_f, dff, jnp.where(is_b, dgb, jnp.where(is_a, dga, 0.0)))
        dvec_ref[...] = jnp.zeros_like(dvec_ref)
        dvec_ref[0:1, :] = jnp.sum(jnp.where(is_f, dff, 0.0), axis=0, keepdims=True)
        dvec_ref[1:2, :] = jnp.sum(jnp.where(is_a, dg * nea * _softplus(za), 0.0), axis=0, keepdims=True)
        dvec_ref[2:3, :] = jnp.sum(jnp.where(is_a, dga, 0.0), axis=0, keepdims=True)

    vec = pl.BlockSpec((1, LANES), lambda i: (0, 0))
    full = lambda r, c: pl.BlockSpec((r, c), lambda i: (0, 0))
    return pl.pallas_call(
        body, name="small_bwd", grid=(1,),
        in_specs=[pl.BlockSpec((t, LANES), lambda i: (0, COL_SMALL // LANES)), vec, vec, vec, full(t, LANES),
                  full(LANES, t), full(t, 512), full(t, 512)],
        out_specs=[full(t, LANES), full(8, LANES)], out_shape=[_sds((t, LANES)), _sds((8, LANES))],
        compiler_params=_params("arbitrary"),
    )(proj, fb, al, dtb, dcq, dckt, dbe, dge)


def _pack_dproj(dfox, dgdn, dgz, dsm):
    t = dgz.shape[0]
    tm = min(MATMUL_BLOCK, t)

    def body(*refs):
        parts, dp_ref = refs[:8], refs[8]
        col = 0
        for part in parts:
            width = part.shape[1]
            dp_ref[:, col:col + width] = part[...].astype(BF)
            col += width

    tok = lambda w: pl.BlockSpec((tm, w), lambda i: (i, 0))
    return pl.pallas_call(
        body, name="pack_dproj", grid=(t // tm,), in_specs=[tok(D_FOX)] * 3 + [tok(D_GDN)] * 4 + [tok(LANES)],
        out_specs=tok(PROJ_W), out_shape=_sds((t, PROJ_W), BF), compiler_params=_params("parallel"),
    )(*dfox, *dgdn, dgz, dsm)


def _in_bwd(dproj, wt_al, x, nw, dx1):
    t = x.shape[0]
    tm = min(MATMUL_BLOCK, t)

    def body(dp_ref, w_ref, x_ref, nw_ref, dx1_ref, dx_ref, dnw_ref):
        i = pl.program_id(0)

        @pl.when(i == 0)
        def _():
            dnw_ref[...] = jnp.zeros_like(dnw_ref)

        dh = jnp.dot(dp_ref[...], w_ref[...], preferred_element_type=F32)
        xv = x_ref[...]
        r = lax.rsqrt(jnp.mean(xv * xv, axis=-1, keepdims=True) + EPS)
        xh = xv * r
        dnw_ref[...] += jnp.sum(dh * xh, axis=0, keepdims=True)
        dxh = dh * nw_ref[...]
        dx_ref[...] = dx1_ref[...] + r * (dxh - xh * jnp.mean(dxh * xh, axis=-1, keepdims=True))

    tok = lambda w: pl.BlockSpec((tm, w), lambda i: (i, 0))
    vec = lambda w: pl.BlockSpec((1, w), lambda i: (0, 0))
    return pl.pallas_call(
        body, name="in_bwd", grid=(t // tm,),
        in_specs=[tok(PROJ_W), pl.BlockSpec((PROJ_W, D_MODEL), lambda i: (0, 0)), tok(D_MODEL), vec(D_MODEL),
                  tok(D_MODEL)],
        out_specs=[tok(D_MODEL), vec(D_MODEL)], out_shape=[_sds((t, D_MODEL)), _sds((1, D_MODEL))],
        compiler_params=_params("arbitrary"),
    )(dproj, wt_al, x, nw, dx1)


def _row(v, width=None):
    v = v.reshape(1, -1).astype(F32)
    if width is not None and v.shape[1] < width:
        v = jnp.pad(v, ((0, 0), (0, width - v.shape[1])))
    return v


def _lane_vec(v, first):
    return jnp.pad(v.astype(F32), (first, LANES - first - v.shape[0])).reshape(1, LANES)


def _local_step(x, target, wt_al, late_weights, on_grads, convw, pre_mix_norm, fox_f_bias, fox_out_norm,
                gdn_a_log, gdn_dt_bias, gdn_out_norm, post_mix_norm, pre_mlp_norm, post_mlp_norm):
    t = x.shape[0]
    nch = t // CHUNK
    nw, pmw, plw, pw = _row(pre_mix_norm), _row(post_mix_norm), _row(pre_mlp_norm), _row(post_mlp_norm)
    fb, al, dtb = _lane_vec(fox_f_bias, SM_FF), _lane_vec(gdn_a_log, SM_GA), _lane_vec(gdn_dt_bias, SM_GA)
    fnw = _row(jnp.tile(fox_out_norm, 2))
    gnw = _row(gdn_out_norm)

    proj, h = _norm_proj(x, nw, wt_al)
    cumt, beta, g = _small_prep(proj, fb, al, dtb)
    o_fox, lse, fox_n = _fox_fwd(proj, cumt, fnw)
    qn, kn, cv, gc, be, mmat, amat = _gdn_prep(proj, convw, beta, g)
    n_prob = N_GDN_HEADS * nch
    m3 = mmat.reshape(n_prob, CHUNK, CHUNK)
    if n_prob < LANES:
        m3 = jnp.pad(m3, ((0, LANES - n_prob), (0, 0), (0, 0)))
    tinv = _tri_inverse(m3)[:n_prob].reshape(N_GDN_HEADS, nch, CHUNK, CHUNK)
    gdn_o, s_all, vn_all = _gdn_scan(qn, kn, cv, be, gc, tinv, amat)
    w_out = late_weights("w_out", gdn_o)
    x1, h2, mixed, omix, h2t = _mix_out(fox_n, gdn_o, proj, gnw, w_out, x, pmw, plw)
    w_up, w_down = late_weights("mlp", h2)
    up = _mlp_up(h2, w_up)
    dy, dx2, loss, d_pw = _mlp_down_loss(up, w_down, x1, pw, target)

    dup = _mlp_bwd_act(dy, w_down, up)
    relu2 = lambda u: jnp.square(jnp.maximum(u.astype(F32), 0.0))
    g_down = _wgrad(up, dy, D_FF // N_DEV, a_fn=relu2, name="wgrad_down")
    g_up = _wgrad_pre_t(h2t, dup, D_FF // N_DEV, name="wgrad_up")
    token = on_grads("mlp", (g_up, g_down))
    dx1, dmixed, d_plw, d_pmw = _mlp_bwd_in(dup, w_up, x1, plw + token[0:1, 0:1], dx2, mixed, pmw)
    token = on_grads("w_out", _wgrad(omix, dmixed, 512, split=4, name="wgrad_out"))
    do_fox, dgo, dgz, d_fnw, d_gnw = _out_bwd(dmixed, w_out, o_fox, gdn_o, proj, fnw + token[0:1, 0:1], gnw)
    dfq, dfk, dfv, dcq, dckt = _fox_bwd(proj, cumt, lse, o_fox, do_fox)
    dqn, dkn, dcv, dbe, dge = _gdn_bwd(qn, kn, cv, be, gc, tinv, amat, s_all, vn_all, dgo)
    dxq, dxk, dxv, dwq, dwk, dwv = _gdn_bwd_conv(proj, convw, dqn, dkn, dcv)
    dsm, dvec = _small_bwd(proj, fb, al, dtb, dcq, dckt, dbe, dge)
    dproj = _pack_dproj((dfq, dfk, dfv), (dxq, dxk, dxv), dgz, dsm)
    g_main = _wgrad(dproj, h, 512, name="wgrad_in")
    g_tail = _wgrad(dproj, h, LANES, a_block0=COL_SMALL // LANES, name="wgrad_in_small")
    token = on_grads("w_in", jnp.concatenate([g_main.reshape(COL_SMALL, D_MODEL), g_tail[0]]))
    grad_x, d_nw = _in_bwd(dproj, wt_al, x, nw + token[0:1, 0:1], dx1)
    small = dict(norms=(d_nw, d_pmw, d_plw, d_pw), fox_out_norm=d_fnw, gdn_out_norm=d_gnw, loss=loss, vectors=dvec,
                 conv=(dwq, dwk, dwv))
    return grad_x, small


MESH_IDS = pl.DeviceIdType.MESH
CHIP_FLIPS = ((0, 0), (1, 0), (0, 1), (1, 1))
ANY_SPEC = pl.BlockSpec(memory_space=pl.ANY)


def _place():
    return lax.axis_index("x"), lax.axis_index("y"), lax.axis_index("c")


def _all_gather(blocks):
    n = len(blocks)

    def body(*refs):
        ins, outs, (send_sems, recv_sems, local_sems) = refs[:n], refs[n:2 * n], refs[2 * n:]
        x, y, c = _place()
        sibling = (x, y, 1 - c)
        chips = [(x ^ fx, y ^ fy) for fx, fy in CHIP_FLIPS[1:]]

        def slot(out, px, py, pc):
            return out.at[4 * px + 2 * py + pc]

        def copy(a, k, block, to, src=None):
            return pltpu.make_async_remote_copy(
                src_ref=slot(outs[a], *block) if src is None else src, dst_ref=slot(outs[a], *block),
                send_sem=send_sems.at[a, k], recv_sem=recv_sems.at[a, k], device_id=to, device_id_type=MESH_IDS)

        pending = []
        for a in range(n):
            mine = pltpu.make_async_copy(ins[a], slot(outs[a], x, y, c), local_sems.at[a])
            mine.start()
            pending.append(mine)
        sends = []
        for a in range(n):
            first = [copy(a, 0, (x, y, c), sibling, src=ins[a])]
            first += [copy(a, 1 + j, (x, y, c), (*chip, c), src=ins[a]) for j, chip in enumerate(chips)]
            for cp in first:
                cp.start()
            sends += first
        for a in range(n):
            for j, chip in enumerate(chips):
                copy(a, 1 + j, (*chip, c), (x, y, c)).wait_recv()
                fwd = copy(a, 4 + j, (*chip, c), sibling)
                fwd.start()
                sends.append(fwd)
        for a in range(n):
            copy(a, 0, sibling, (x, y, c)).wait_recv()
            for j, chip in enumerate(chips):
                copy(a, 4 + j, (*chip, 1 - c), (x, y, c)).wait_recv()
        for cp in sends:
            cp.wait_send()
        for cp in pending:
            cp.wait()

    return pl.pallas_call(
        body, name="all_gather_weights", in_specs=[ANY_SPEC] * n, out_specs=[ANY_SPEC] * n,
        out_shape=[_sds((N_DEV,) + b.shape, b.dtype) for b in blocks],
        scratch_shapes=[pltpu.SemaphoreType.DMA((n, 7)), pltpu.SemaphoreType.DMA((n, 7)), pltpu.SemaphoreType.DMA((n,))],
        compiler_params=pltpu.CompilerParams(has_side_effects=True),
    )(*blocks)


def _adamw(w, g, m, v):
    m = ADAM_B1 * m + (1.0 - ADAM_B1) * g
    v = ADAM_B2 * v + (1.0 - ADAM_B2) * (g * g)
    m_hat = m / (1.0 - ADAM_B1 ** ADAM_STEP)
    v_hat = v / (1.0 - ADAM_B2 ** ADAM_STEP)
    return -ADAM_LR * (m_hat / (jnp.sqrt(v_hat) + ADAM_EPS) + ADAM_WD * w), m, v


def _pair_reduce(g, name):
    _, r, c_ = g.shape
    n = len(CHIP_FLIPS)

    def body(g_ref, out_ref, sib_buf, send_sems, recv_sems):
        x, y, c = _place()
        chips = [(x ^ fx, y ^ fy) for fx, fy in CHIP_FLIPS]
        piece = lambda chip, core: g_ref.at[4 * chip[0] + 2 * chip[1] + core]
        copies = [pltpu.make_async_remote_copy(
            src_ref=piece(chip, 1 - c), dst_ref=sib_buf.at[j], send_sem=send_sems.at[j], recv_sem=recv_sems.at[j],
            device_id=(x, y, 1 - c), device_id_type=MESH_IDS) for j, chip in enumerate(chips)]
        for cp in copies:
            cp.start()
        for j, chip in enumerate(chips):
            copies[j].wait_recv()
            out_ref[j] = (piece(chip, c)[...].astype(F32) + sib_buf[j].astype(F32)).astype(BF)
        for cp in copies:
            cp.wait_send()

    return pl.pallas_call(
        body, name=name, in_specs=[VMEM_SPEC], out_specs=VMEM_SPEC, out_shape=_sds((n, r, c_), BF),
        scratch_shapes=[pltpu.VMEM((n, r, c_), BF), pltpu.SemaphoreType.DMA((n,)), pltpu.SemaphoreType.DMA((n,))],
        compiler_params=pltpu.CompilerParams(vmem_limit_bytes=VMEM_LIMIT, has_side_effects=True),
    )(g)


HBM_SPEC = pl.BlockSpec(memory_space=pltpu.HBM)
SEM_SPEC = pl.BlockSpec(memory_space=pltpu.SEMAPHORE)
DATAFLOW = pltpu.SideEffectType.DATAFLOW_SIDE_EFFECTING


def _peers():
    x, y, c = _place()
    return 4 * x + 2 * y + c, [(x ^ (k >> 2), y ^ ((k >> 1) & 1), c ^ (k & 1)) for k in range(1, N_DEV)]


def _peer_index(peer):
    return 4 * peer[0] + 2 * peer[1] + peer[2]


def _zones_with_own(srcs, pieces, name, after=None, dtype=None, chips=False):
    n = len(srcs)
    slots = len(CHIP_FLIPS) if chips else N_DEV
    extra = [] if after is None else [after]
    dtypes = [s_.dtype if pieces or dtype is None else dtype for s_ in srcs]

    def body(me_ref, *refs):
        outs = refs[n + len(extra):]
        for a in range(n):
            if pieces:
                outs[a][0] = refs[a][0]
            else:
                val = refs[a][...].astype(dtypes[a])
                outs[a][0] = val
                outs[n + a][...] = val

    shapes = [s_.shape[1:] if pieces else s_.shape for s_ in srcs]
    mine = lambda sh: pl.BlockSpec((1,) + sh, lambda i, me_ref: (me_ref[0], 0, 0))
    whole = lambda sh: pl.BlockSpec(sh, lambda i, me_ref: (0, 0))
    in_specs = [mine(sh) if pieces else whole(sh) for sh in shapes]
    out_specs = [mine(sh) for sh in shapes] + ([] if pieces else [whole(sh) for sh in shapes])
    out_shape = [_sds((slots,) + sh, dt) for sh, dt in zip(shapes, dtypes)]
    out_shape += [] if pieces else [_sds(sh, dt) for sh, dt in zip(shapes, dtypes)]
    x, y, c = _place()
    own = 0 * x if chips else 4 * x + 2 * y + c
    out = pl.pallas_call(
        body, name=name,
        grid_spec=pltpu.PrefetchScalarGridSpec(num_scalar_prefetch=1, grid=(1,), in_specs=in_specs + [ANY_SPEC] * len(extra),
                                               out_specs=out_specs),
        out_shape=out_shape, compiler_params=_params("arbitrary"),
    )(own.astype(jnp.int32).reshape(1), *srcs, *extra)
    return out[:n], (list(srcs) if pieces else out[n:])


def _exchange_start(srcs, zones, pieces, name, chips=False):
    n = len(srcs)

    def body(*refs):
        ins, zs = refs[:n], refs[n:2 * n]
        sems = refs[2 * n:4 * n]
        token = refs[-1]
        me, peers = _peers()
        x, y, c = _place()
        if chips:
            routes = [((x ^ fx, y ^ fy, c), j, j) for j, (fx, fy) in enumerate(CHIP_FLIPS) if j]
        else:
            routes = [(peer, _peer_index(peer) if pieces else None, me) for peer in peers]
        for peer, src_slot, dst_slot in routes:
            for a in range(n):
                pltpu.make_async_remote_copy(
                    src_ref=ins[a] if src_slot is None else ins[a].at[src_slot], dst_ref=zs[a].at[dst_slot],
                    send_sem=sems[2 * a], recv_sem=sems[2 * a + 1], device_id=peer, device_id_type=MESH_IDS).start()
        token[...] = jnp.zeros_like(token)

    hbm = lambda v: pltpu.with_memory_space_constraint(v, pltpu.HBM)
    out = pl.pallas_call(
        body, name=name,
        out_shape=tuple([pltpu.SemaphoreType.DMA(())] * (2 * n) + [pltpu.HBM(v.shape, v.dtype) for v in srcs]
                        + [pltpu.HBM(z.shape, z.dtype) for z in zones] + [_sds((8, LANES))]),
        in_specs=[HBM_SPEC] * (2 * n), out_specs=tuple([SEM_SPEC] * (2 * n) + [HBM_SPEC] * (2 * n) + [VMEM_SPEC]),
        input_output_aliases={i: 2 * n + i for i in range(2 * n)},
        compiler_params=pltpu.CompilerParams(has_side_effects=DATAFLOW),
    )(*[hbm(v) for v in srcs], *[hbm(z) for z in zones])
    return out[:2 * n], out[2 * n:3 * n], out[3 * n:4 * n], out[-1]


def _exchange_wait(sems, srcs, zones, after, name, chips=False):
    n = len(srcs)
    after = list(after) if isinstance(after, (list, tuple)) else [after]
    n_copies = len(CHIP_FLIPS) - 1 if chips else N_DEV - 1

    def body(*refs):
        ins, zs, sm = refs[:n], refs[n:2 * n], refs[2 * n:4 * n]
        me, peers = _peers()
        for a in range(n):
            seven = zs[a].at[pl.ds(0, n_copies)]
            cp = pltpu.make_async_remote_copy(src_ref=seven, dst_ref=seven, send_sem=sm[2 * a], recv_sem=sm[2 * a + 1],
                                              device_id=peers[0], device_id_type=MESH_IDS)
            cp.wait_send()
            cp.wait_recv()

    out = pl.pallas_call(
        body, name=name, out_shape=tuple([pltpu.HBM(v.shape, v.dtype) for v in srcs] + [pltpu.HBM(z.shape, z.dtype) for z in zones]),
        in_specs=[HBM_SPEC] * (2 * n) + [SEM_SPEC] * (2 * n) + [ANY_SPEC] * len(after),
        out_specs=tuple([HBM_SPEC] * (2 * n)), input_output_aliases={i: i for i in range(2 * n)},
        compiler_params=pltpu.CompilerParams(has_side_effects=DATAFLOW),
    )(*srcs, *zones, *sems, *after)
    return out[n:]


def _sum_adamw(zone, w, m, v, name):
    n_slots, r, c_ = zone.shape
    rb = next((b for b in (256, 128) if r % b == 0), r)

    def body(z_ref, w_ref, m_ref, v_ref, grad_ref, delta_ref, nm_ref, nv_ref):
        total = z_ref[0].astype(F32)
        for d in range(1, n_slots):
            total = total + z_ref[d].astype(F32)
        grad_ref[...] = total
        delta_ref[...], nm_ref[...], nv_ref[...] = _adamw(w_ref[...], total, m_ref[...], v_ref[...])

    blk = pl.BlockSpec((rb, c_), lambda i: (i, 0))
    return pl.pallas_call(
        body, name=name, grid=(r // rb,), in_specs=[pl.BlockSpec((n_slots, rb, c_), lambda i: (0, i, 0)), blk, blk, blk],
        out_specs=[blk] * 4, out_shape=[_sds((r, c_))] * 4, compiler_params=_params("parallel"),
    )(zone, w, m, v)


SMALL_NORMS = ("pre_mix_norm", "post_mix_norm", "pre_mlp_norm", "post_mlp_norm")
SMALL_ORDER = SMALL_NORMS + ("fox_out_norm", "gdn_out_norm", "fox_f_bias", "gdn_a_log", "gdn_dt_bias", "gdn_conv_w")
CONV_SLAB_ROWS, CONV_SLAB_LANES = 8, 256


def _small_pack(small):
    def body(n0, n1, n2, n3, fnw_ref, gnw_ref, loss_ref, vec_ref, out_ref):
        out_ref[...] = jnp.zeros_like(out_ref)
        for i, ref in enumerate((n0, n1, n2, n3)):
            out_ref[i:i + 1, :] = ref[...]
        out_ref[4:5, 0:LANES] = fnw_ref[...]
        out_ref[4:5, LANES:2 * LANES] = gnw_ref[...]
        out_ref[4:5, 2 * LANES:3 * LANES] = loss_ref[...]
        out_ref[5:8, 0:LANES] = vec_ref[0:3, :]

    return pl.pallas_call(body, name="small_pack", in_specs=[VMEM_SPEC] * 8, out_specs=VMEM_SPEC,
                          out_shape=_sds((8, D_MODEL)))(*small["norms"], small["fox_out_norm"], small["gdn_out_norm"],
                                                        small["loss"], small["vectors"])


def _conv_slabs(dconv):
    blocks = dconv.reshape(CONV_K, N_DEV, -1).transpose(1, 0, 2)
    blocks = jnp.pad(blocks, ((0, 0), (0, CONV_SLAB_ROWS - CONV_K), (0, CONV_SLAB_LANES - blocks.shape[2])))
    return blocks.reshape(N_DEV * CONV_SLAB_ROWS, CONV_SLAB_LANES)


def _small_update(zone, conv_zone, w, m, v):
    n = len(SMALL_ORDER)
    n_conv = w["gdn_conv_w"].shape[1]

    def body(me_ref, z_ref, zc_ref, *refs):
        params, loss_ref, outs, (tot, totc) = refs[:3 * n], refs[3 * n], refs[3 * n + 1:7 * n + 1], refs[-2:]
        total, total_c = z_ref[0], zc_ref[0]
        for d in range(1, N_DEV):
            total, total_c = total + z_ref[d], total_c + zc_ref[d]
        tot[...] = total
        totc[...] = total_c
        loss_ref[...] = tot[4, 2 * LANES:2 * LANES + 1]
        mine = totc[pl.ds(pl.multiple_of(me_ref[0] * CONV_SLAB_ROWS, CONV_SLAB_ROWS), CONV_SLAB_ROWS), :]
        g = dict(zip(SMALL_NORMS, (tot[0], tot[1], tot[2], tot[3])))
        g.update(fox_out_norm=tot[4, 0:FOX_HEAD_DIM], gdn_out_norm=tot[4, LANES:LANES + GDN_HEAD_DIM],
                 fox_f_bias=tot[5, SM_FF:SM_FF + N_FOX_HEADS], gdn_a_log=tot[6, SM_GA:SM_GA + N_GDN_HEADS],
                 gdn_dt_bias=tot[7, SM_GA:SM_GA + N_GDN_HEADS], gdn_conv_w=mine[0:CONV_K, 0:n_conv])
        for i, name in enumerate(SMALL_ORDER):
            w_ref, m_ref, v_ref = params[3 * i:3 * i + 3]
            outs[4 * i][...] = g[name]
            outs[4 * i + 1][...], outs[4 * i + 2][...], outs[4 * i + 3][...] = _adamw(w_ref[...], g[name], m_ref[...],
                                                                                     v_ref[...])

    x, y, c = _place()
    operands = [a[name] for name in SMALL_ORDER for a in (w, m, v)]
    out = pl.pallas_call(
        body, name="small_update",
        in_specs=[pl.BlockSpec(memory_space=pltpu.SMEM)] + [VMEM_SPEC] * (2 + 3 * n), out_specs=[VMEM_SPEC] * (1 + 4 * n),
        out_shape=[_sds((1,))] + [_sds(w[name].shape) for name in SMALL_ORDER for _ in range(4)],
        scratch_shapes=[pltpu.VMEM(zone.shape[1:], F32), pltpu.VMEM(conv_zone.shape[1:], F32)],
    )((4 * x + 2 * y + c).astype(jnp.int32).reshape(1), zone, conv_zone, *operands)
    return out[0][0], {name: out[1 + 4 * i:5 + 4 * i] for i, name in enumerate(SMALL_ORDER)}


NATIVE_ROWS = ((0, 1536), (1544, 3080), (3088, 3600), (1536, 1544), (3080, 3088))


def _to_aligned_rows(wt_native):
    pad = jnp.zeros((PROJ_W - D_PROJ, wt_native.shape[1]), wt_native.dtype)
    return jnp.concatenate([wt_native[lo:hi] for lo, hi in NATIVE_ROWS] + [pad])


def _from_aligned_rows(gt_al):
    return jnp.concatenate([gt_al[0:1536], gt_al[3584:3592], gt_al[1536:3072], gt_al[3592:3600], gt_al[3072:3584]])


def _cols_from_pieces(p):
    return p.transpose(1, 0, 2).reshape(p.shape[1], -1)


WEIGHT_ORDER = ("pre_mix_norm", "w_in", "fox_f_bias", "fox_out_norm", "gdn_conv_w", "gdn_a_log", "gdn_dt_bias",
                "gdn_out_norm", "w_out", "post_mix_norm", "pre_mlp_norm", "w_up", "w_down", "post_mlp_norm")


def kernel(x, pre_mix_norm, w_in, fox_f_bias, fox_out_norm, gdn_conv_w, gdn_a_log, gdn_dt_bias, gdn_out_norm, w_out, post_mix_norm, pre_mlp_norm, w_up, w_down, post_mlp_norm, loss_target, m_pre_mix_norm, m_w_in, m_fox_f_bias, m_fox_out_norm, m_gdn_conv_w, m_gdn_a_log, m_gdn_dt_bias, m_gdn_out_norm, m_w_out, m_post_mix_norm, m_pre_mlp_norm, m_w_up, m_w_down, m_post_mlp_norm, v_pre_mix_norm, v_w_in, v_fox_f_bias, v_fox_out_norm, v_gdn_conv_w, v_gdn_a_log, v_gdn_dt_bias, v_gdn_out_norm, v_w_out, v_post_mix_norm, v_pre_mlp_norm, v_w_up, v_w_down, v_post_mlp_norm):
    w = dict(pre_mix_norm=pre_mix_norm, w_in=w_in, fox_f_bias=fox_f_bias, fox_out_norm=fox_out_norm,
             gdn_conv_w=gdn_conv_w, gdn_a_log=gdn_a_log, gdn_dt_bias=gdn_dt_bias, gdn_out_norm=gdn_out_norm, w_out=w_out,
             post_mix_norm=post_mix_norm, pre_mlp_norm=pre_mlp_norm, w_up=w_up, w_down=w_down, post_mlp_norm=post_mlp_norm)
    mom = dict(pre_mix_norm=m_pre_mix_norm, w_in=m_w_in, fox_f_bias=m_fox_f_bias, fox_out_norm=m_fox_out_norm,
               gdn_conv_w=m_gdn_conv_w, gdn_a_log=m_gdn_a_log, gdn_dt_bias=m_gdn_dt_bias, gdn_out_norm=m_gdn_out_norm,
               w_out=m_w_out, post_mix_norm=m_post_mix_norm, pre_mlp_norm=m_pre_mlp_norm, w_up=m_w_up, w_down=m_w_down,
               post_mlp_norm=m_post_mlp_norm)
    var = dict(pre_mix_norm=v_pre_mix_norm, w_in=v_w_in, fox_f_bias=v_fox_f_bias, fox_out_norm=v_fox_out_norm,
               gdn_conv_w=v_gdn_conv_w, gdn_a_log=v_gdn_a_log, gdn_dt_bias=v_gdn_dt_bias, gdn_out_norm=v_gdn_out_norm,
               w_out=v_w_out, post_mix_norm=v_post_mix_norm, pre_mlp_norm=v_pre_mlp_norm, w_up=v_w_up, w_down=v_w_down,
               post_mlp_norm=v_post_mlp_norm)

    win_g, conv_g = _all_gather([w_in.T.astype(BF), gdn_conv_w])
    wt_al = _to_aligned_rows(win_g.reshape(D_PROJ, D_MODEL))
    convw = _cols_from_pieces(conv_g)
    gathers, after = {}, win_g
    for name, shards in (("w_out", [w_out]), ("mlp", [w_up, w_down])):
        zones, shards = _zones_with_own(shards, False, "gather_" + name + "_own", after=after, dtype=BF)
        gathers[name] = _exchange_start(shards, zones, False, "gather_" + name + "_start")
        after = gathers[name][3]

    def late_weights(name, after):
        sems, shards, zones, _ = gathers[name]
        got = _exchange_wait(sems, shards, zones, after, "gather_" + name + "_wait")
        if name == "w_out":
            return got[0].reshape(D_MODEL, D_MODEL)
        return got[0], got[1].reshape(D_FF, D_MODEL)

    scatters = {}

    def on_grads(name, g):
        chips = name == "w_in"
        if name == "w_in":
            g = _pair_reduce(_from_aligned_rows(g).reshape(N_DEV, D_PROJ // N_DEV, D_MODEL), "pair_reduce_w_in")
        srcs = list(g) if name == "mlp" else [g]
        zones, _ = _zones_with_own(srcs, True, "scatter_" + name + "_own", chips=chips)
        scatters[name] = _exchange_start(srcs, zones, True, "scatter_" + name + "_start", chips=chips)
        return scatters[name][3]

    grad_x, small = _local_step(
        x[0], loss_target[0], wt_al, late_weights, on_grads, convw, pre_mix_norm + after[0, 0],
        fox_f_bias, fox_out_norm, gdn_a_log, gdn_dt_bias, gdn_out_norm, post_mix_norm, pre_mlp_norm, post_mlp_norm)
    slabs = [_small_pack(small), _conv_slabs(jnp.concatenate(small["conv"], axis=1))]
    zones, slabs = _zones_with_own(slabs, False, "small_own")
    scatters["small"] = _exchange_start(slabs, zones, False, "small_start")

    grads, delta, new_m, new_v = {}, {}, {}, {}
    after = scatters["small"][3]
    for name, members in (("mlp", ("w_up", "w_down")), ("w_out", ("w_out",)), ("small", ()), ("w_in", ("w_in",))):
        sems, srcs, zones, _ = scatters[name]
        zones = _exchange_wait(sems, srcs, zones, after, "scatter_" + name + "_wait", chips=name == "w_in")
        if name == "small":
            loss, updated = _small_update(zones[0], zones[1], w, mom, var)
            for n, res in updated.items():
                grads[n], delta[n], new_m[n], new_v[n] = res
            after = grads["pre_mix_norm"]
        for n, zone in zip(members, zones):
            if n == "w_in":
                res = _sum_adamw(zone, w[n].T, mom[n].T, var[n].T, "adamw_" + n)
                grads[n], delta[n], new_m[n], new_v[n] = [r.T for r in res]
            else:
                grads[n], delta[n], new_m[n], new_v[n] = _sum_adamw(zone, w[n], mom[n], var[n], "adamw_" + n)
        if members:
            after = [grads[n] for n in members]

    return (loss, grad_x[None], *[grads[n] for n in WEIGHT_ORDER], *[delta[n] for n in WEIGHT_ORDER],
            *[new_m[n] for n in WEIGHT_ORDER], *[new_v[n] for n in WEIGHT_ORDER])
```

```python
import jax
import jax.numpy as jnp
from jax import lax
from jax.experimental import pallas as pl
from jax.experimental.pallas import tpu as pltpu

F32 = jnp.float32
BF = jnp.bfloat16

D_MODEL = 1024
N_FOX_HEADS, FOX_HEAD_DIM = 8, 64
N_GDN_HEADS, GDN_HEAD_DIM = 4, 128
D_FOX = N_FOX_HEADS * FOX_HEAD_DIM
D_GDN = N_GDN_HEADS * GDN_HEAD_DIM
CHUNK = 64
CONV_K = 4
D_FF = 4 * D_MODEL
EPS = 1e-6
D_PROJ = 3600
N_DEV = 8

PROJ_W = 3712
COL_FOX, COL_GDN, COL_GZ, COL_SMALL = 0, 1536, 3072, 3584
LANES = 128
SM_FF, SM_GB, SM_GA = 0, 8, 12

ADAM_LR, ADAM_B1, ADAM_B2, ADAM_EPS, ADAM_WD, ADAM_STEP = 0.001, 0.9, 0.999, 1e-08, 0.01, 10

TOKEN_BLOCK = 256
MATMUL_BLOCK = 512
FOX_SCALE = FOX_HEAD_DIM ** -0.5
GDN_QSCALE = GDN_HEAD_DIM ** -0.5
NEG_BIG = -1e30
VMEM_LIMIT = 56 * 1024 * 1024

VMEM_SPEC = pl.BlockSpec(memory_space=pltpu.VMEM)


def _sds(shape, dtype=F32):
    return jax.ShapeDtypeStruct(shape, dtype)


def _params(*sem):
    return pltpu.CompilerParams(dimension_semantics=sem if sem else None, vmem_limit_bytes=VMEM_LIMIT)


def _mm(a, b):
    return jnp.dot(a.astype(BF), b.astype(BF), preferred_element_type=F32)


def _mm_nt(a, b):
    return lax.dot_general(a.astype(BF), b.astype(BF), (((1,), (1,)), ((), ())), preferred_element_type=F32)


def _mm_tn(a, b):
    return lax.dot_general(a.astype(BF), b.astype(BF), (((0,), (0,)), ((), ())), preferred_element_type=F32)


def _sigmoid(x):
    return 1.0 / (1.0 + jnp.exp(-x))


def _softplus(x):
    return jnp.maximum(x, 0.0) + jnp.log1p(jnp.exp(-jnp.abs(x)))


def _iota(shape, dim):
    return lax.broadcasted_iota(jnp.int32, shape, dim)


def _shift_down(x, s, row):
    return jnp.where(row >= s, pltpu.roll(x, s, 0), 0.0)


def _shift_up(x, s, row):
    n = x.shape[0]
    return jnp.where(row < n - s, pltpu.roll(x, n - s, 0), 0.0)


def _norm_proj(x, nw, wt_al):
    t = x.shape[0]

    def body(x_ref, nw_ref, w_ref, proj_ref, h_ref):
        xv = x_ref[...]
        r = lax.rsqrt(jnp.mean(xv * xv, axis=-1, keepdims=True) + EPS)
        h = (xv * r * nw_ref[...]).astype(BF)
        h_ref[...] = h
        proj_ref[...] = lax.dot_general(h, w_ref[...], (((1,), (1,)), ((), ())), preferred_element_type=F32)

    tm = min(MATMUL_BLOCK, t)
    return pl.pallas_call(
        body, name="norm_proj", grid=(t // tm,),
        in_specs=[pl.BlockSpec((tm, D_MODEL), lambda i: (i, 0)), pl.BlockSpec((1, D_MODEL), lambda i: (0, 0)),
                  pl.BlockSpec((PROJ_W, D_MODEL), lambda i: (0, 0))],
        out_specs=[pl.BlockSpec((tm, PROJ_W), lambda i: (i, 0)), pl.BlockSpec((tm, D_MODEL), lambda i: (i, 0))],
        out_shape=[_sds((t, PROJ_W)), _sds((t, D_MODEL), BF)],
        compiler_params=_params("parallel"),
    )(x, nw, wt_al)


def _lane_column(x, lane):
    return jnp.sum(jnp.where(_iota((1, LANES), 1) == lane, x, 0.0), axis=-1, keepdims=True)


def _small_prep(proj, fb, al, dtb):
    t = proj.shape[0]

    def body(sm_ref, fb_ref, al_ref, dtb_ref, cumt_ref, beta_ref, g_ref):
        s = sm_ref[...]
        z = s + fb_ref[...]
        cum = jnp.minimum(z, 0.0) - jnp.log1p(jnp.exp(-jnp.abs(z)))
        row = _iota((t, LANES), 0)
        step = 1
        while step < t:
            cum = cum + _shift_down(cum, step, row)
            step *= 2
        cumt_ref[...] = cum.T
        beta_ref[...] = _sigmoid(s)
        g_ref[...] = -jnp.exp(al_ref[...]) * _softplus(s + dtb_ref[...])

    vec = pl.BlockSpec((1, LANES), lambda i: (0, 0))
    tok = pl.BlockSpec((t, LANES), lambda i: (0, 0))
    return pl.pallas_call(
        body, name="small_prep", grid=(1,),
        in_specs=[pl.BlockSpec((t, LANES), lambda i: (0, COL_SMALL // LANES)), vec, vec, vec],
        out_specs=[pl.BlockSpec((LANES, t), lambda i: (0, 0)), tok, tok],
        out_shape=[_sds((LANES, t)), _sds((t, LANES)), _sds((t, LANES))],
        compiler_params=_params("arbitrary"),
    )(proj, fb, al, dtb)


def _fox_stack(x, first):
    return jnp.concatenate([jnp.where(first, x, 0.0), jnp.where(first, 0.0, x)], axis=0).astype(BF)


def _fox_unstack(y, first):
    n = y.shape[0] // 2
    return jnp.where(first, y[:n], y[n:])


def _fox_logits(q2_i, kb, cumt_ref, pair, i, tq):
    klen = (i + 1) * tq
    s = lax.dot_general(q2_i, kb[:klen], (((1,), (1,)), ((), ())), preferred_element_type=F32)
    upper = _iota((2 * tq, 1), 0) < tq
    s = s - jnp.where(upper, cumt_ref[pl.ds(2 * pair, 1), 0:klen], cumt_ref[pl.ds(2 * pair + 1, 1), 0:klen])
    causal = _iota((2 * tq, tq), 1) <= _iota((2 * tq, tq), 0) % tq
    parts = [(s[:, :klen - tq], 0, klen - tq)] if i else []
    return parts + [(jnp.where(causal, s[:, klen - tq:], NEG_BIG), klen - tq, klen)]


def _fox_fwd(proj, cumt, fnw):
    t = proj.shape[0]
    tq = min(TOKEN_BLOCK, t // 2)
    nq = t // tq

    def body(q_ref, k_ref, v_ref, cumt_ref, fnw_ref, o_ref, lse_ref, fn_ref):
        j = pl.program_id(0)
        first = _iota((1, LANES), 1) < FOX_HEAD_DIM
        kb = k_ref[...].astype(BF)
        vb = v_ref[...].astype(BF)
        for i in range(nq):
            rows = slice(i * tq, (i + 1) * tq)
            q2 = _fox_stack(q_ref[rows, :] * FOX_SCALE, first)
            parts = _fox_logits(q2, kb, cumt_ref, j, i, tq)
            m = jnp.max(parts[-1][0], axis=-1, keepdims=True)
            if i:
                m = jnp.maximum(m, jnp.max(parts[0][0], axis=-1, keepdims=True))
            l = jnp.zeros((2 * tq, 1), F32)
            o = jnp.zeros((2 * tq, LANES), F32)
            for s, lo, hi in parts:
                p = jnp.exp(s - m)
                l = l + jnp.sum(p, axis=-1, keepdims=True)
                o = o + jnp.dot(p.astype(BF), vb[lo:hi], preferred_element_type=F32)
            o_acc = _fox_unstack(o / l, first)
            lse_acc = _fox_unstack(jnp.broadcast_to(m + jnp.log(l), (2 * tq, LANES)), first)
            o_ref[rows, :] = o_acc
            lse_ref[rows, :] = lse_acc
            o2 = o_acc * o_acc
            s0 = jnp.sum(jnp.where(first, o2, 0.0), axis=-1, keepdims=True)
            s1 = jnp.sum(jnp.where(first, 0.0, o2), axis=-1, keepdims=True)
            r = lax.rsqrt(jnp.where(first, s0, s1) * (1.0 / FOX_HEAD_DIM) + EPS)
            fn_ref[rows, :] = (o_acc * r * fnw_ref[...]).astype(BF)

    blk = lambda off: pl.BlockSpec((t, LANES), lambda j: (0, off + j))
    return pl.pallas_call(
        body, name="fox_fwd", grid=(N_FOX_HEADS // 2,),
        in_specs=[blk(0), blk(4), blk(8), pl.BlockSpec((LANES, t), lambda j: (0, 0)),
                  pl.BlockSpec((1, LANES), lambda j: (0, 0))],
        out_specs=[blk(0), blk(0), blk(0)],
        out_shape=[_sds((t, D_FOX)), _sds((t, D_FOX)), _sds((t, D_FOX), BF)],
        compiler_params=_params("parallel"),
    )(proj, proj, proj, cumt, fnw)


def _fox_bwd(proj, cumt, lse, o, do):
    t = proj.shape[0]
    tq = min(TOKEN_BLOCK, t // 2)
    nq = t // tq

    def body(q_ref, k_ref, v_ref, cumt_ref, lse_ref, o_ref, do_ref,
             dq_ref, dk_ref, dv_ref, dcq_ref, dckt_ref, dk_s, dv_s):
        j = pl.program_id(0)

        @pl.when(j == 0)
        def _():
            dcq_ref[...] = jnp.zeros_like(dcq_ref)
            dckt_ref[...] = jnp.zeros_like(dckt_ref)

        lane = _iota((1, LANES), 1)

        first = _iota((1, LANES), 1) < FOX_HEAD_DIM
        kb = k_ref[...].astype(BF)
        vb = v_ref[...].astype(BF)
        dk_s[...] = jnp.zeros_like(dk_s)
        dv_s[...] = jnp.zeros_like(dv_s)
        for i in range(nq):
            rows = slice(i * tq, (i + 1) * tq)
            do_i = do_ref[rows, :]
            prod = do_i * o_ref[rows, :]
            lse_i = lse_ref[rows, :]
            q2 = _fox_stack(q_ref[rows, :] * FOX_SCALE, first)
            do2 = _fox_stack(do_i, first)
            delta = jnp.concatenate([jnp.sum(jnp.where(first, prod, 0.0), axis=-1, keepdims=True),
                                     jnp.sum(jnp.where(first, 0.0, prod), axis=-1, keepdims=True)], axis=0)
            lse2 = jnp.concatenate([lse_i[:, 0:1], lse_i[:, FOX_HEAD_DIM:FOX_HEAD_DIM + 1]], axis=0)
            dq2 = jnp.zeros((2 * tq, LANES), F32)
            dcq2 = jnp.zeros((2 * tq, 1), F32)
            for s, lo, hi in _fox_logits(q2, kb, cumt_ref, j, i, tq):
                p = jnp.exp(s - lse2)
                ds = p * (_mm_nt(do2, vb[lo:hi]) - delta)
                dsb = ds.astype(BF)
                dq2 = dq2 + jnp.dot(dsb, kb[lo:hi], preferred_element_type=F32)
                dk_s[lo:hi, :] += _mm_tn(dsb, q2)
                dv_s[lo:hi, :] += _mm_tn(p, do2)
                dcq2 = dcq2 + jnp.sum(ds, axis=-1, keepdims=True)
                dckt_ref[pl.ds(2 * j, 1), lo:hi] += jnp.sum(ds[:tq], axis=0, keepdims=True)
                dckt_ref[pl.ds(2 * j + 1, 1), lo:hi] += jnp.sum(ds[tq:], axis=0, keepdims=True)
            dq_ref[rows, :] = _fox_unstack(dq2, first) * FOX_SCALE
            dcq_ref[rows, :] += jnp.where(lane == 2 * j, dcq2[:tq], jnp.where(lane == 2 * j + 1, dcq2[tq:], 0.0))
        dk_ref[...] = dk_s[...]
        dv_ref[...] = dv_s[...]

    blk = lambda off: pl.BlockSpec((t, LANES), lambda j: (0, off + j))
    rows128 = pl.BlockSpec((LANES, t), lambda j: (0, 0))
    return pl.pallas_call(
        body, name="fox_bwd", grid=(N_FOX_HEADS // 2,),
        in_specs=[blk(0), blk(4), blk(8), rows128, blk(0), blk(0), blk(0)],
        out_specs=[blk(0), blk(0), blk(0), pl.BlockSpec((t, LANES), lambda j: (0, 0)), rows128],
        out_shape=[_sds((t, D_FOX))] * 3 + [_sds((t, LANES)), _sds((LANES, t))],
        scratch_shapes=[pltpu.VMEM((t, LANES), F32), pltpu.VMEM((t, LANES), F32)],
        compiler_params=_params("arbitrary"),
    )(proj, proj, proj, cumt, lse, o, do)


def _conv(x, w, row):
    return (w[3:4, :] * x + w[2:3, :] * _shift_down(x, 1, row) + w[1:2, :] * _shift_down(x, 2, row)
            + w[0:1, :] * _shift_down(x, 3, row))


def _chunk_decay(gc_c):
    gi = gc_c[:, 0:CHUNK]
    gj = gc_c.T[0:CHUNK, :]
    ri = _iota((CHUNK, CHUNK), 0)
    cj = _iota((CHUNK, CHUNK), 1)
    return jnp.where(ri >= cj, jnp.exp(jnp.minimum(gi - gj, 0.0)), 0.0), ri > cj


def _gdn_specs(t):
    col = lambda off: pl.BlockSpec((t, LANES), lambda h: (0, off + h))
    cw = lambda off: pl.BlockSpec((CONV_K, LANES), lambda h: (0, off + h))
    mat = pl.BlockSpec((1, t // CHUNK, CHUNK, CHUNK), lambda h: (h, 0, 0, 0))
    return col, cw, mat


def _gdn_prep(proj, convw, beta, g):
    t = proj.shape[0]
    nch = t // CHUNK

    def body(xq_ref, xk_ref, xv_ref, wq_ref, wk_ref, wv_ref, beta_ref, g_ref,
             qn_ref, kn_ref, cv_ref, gc_ref, be_ref, m_ref, a_ref):
        row = _iota((t, LANES), 0)
        hd = pl.program_id(0)
        be_ref[...] = jnp.broadcast_to(_lane_column(beta_ref[...], SM_GB + hd), (t, LANES))

        def act(x_ref, w_ref):
            y = _conv(x_ref[...], w_ref[...], row)
            return y * _sigmoid(y)

        cq = act(xq_ref, wq_ref)
        ck = act(xk_ref, wk_ref)
        cv_ref[...] = act(xv_ref, wv_ref)
        qn_ref[...] = cq * lax.rsqrt(jnp.sum(cq * cq, axis=-1, keepdims=True) + EPS) * GDN_QSCALE
        kn_ref[...] = ck * lax.rsqrt(jnp.sum(ck * ck, axis=-1, keepdims=True) + EPS)
        gc = jnp.broadcast_to(_lane_column(g_ref[...], SM_GA + hd), (t, LANES))
        pos = row % CHUNK
        step = 1
        while step < CHUNK:
            gc = gc + jnp.where(pos >= step, pltpu.roll(gc, step, 0), 0.0)
            step *= 2
        gc_ref[...] = gc

        group = 4 if nch % 4 == 0 else 1

        def chunks(gi, carry):
            ns = [gi * group + c for c in range(group)]
            sls = [pl.ds(pl.multiple_of(n * CHUNK, CHUNK), CHUNK) for n in ns]
            ks = [kn_ref[sl, :] for sl in sls]
            kk = [_mm_nt(k_c * be_ref[sl, :], k_c) for k_c, sl in zip(ks, sls)]
            qk = [_mm_nt(qn_ref[sl, :], k_c) for k_c, sl in zip(ks, sls)]
            for c, n in enumerate(ns):
                decay, strict = _chunk_decay(gc_ref[sls[c], :])
                m_ref[0, n] = jnp.where(strict, kk[c] * decay, 0.0)
                a_ref[0, n] = qk[c] * decay
            return carry

        lax.fori_loop(0, nch // group, chunks, 0)

    col, cw, mat = _gdn_specs(t)
    return pl.pallas_call(
        body, name="gdn_prep", grid=(N_GDN_HEADS,),
        in_specs=[col(12), col(16), col(20), cw(0), cw(4), cw(8)] + [pl.BlockSpec((t, LANES), lambda h: (0, 0))] * 2,
        out_specs=[col(0), col(0), col(0), col(0), col(0), mat, mat],
        out_shape=[_sds((t, D_GDN))] * 5 + [_sds((N_GDN_HEADS, nch, CHUNK, CHUNK))] * 2,
        compiler_params=_params("parallel"),
    )(proj, proj, proj, convw, convw, convw, beta, g)


def _tri_inverse(m3):
    assert m3.shape == (LANES, CHUNK, CHUNK)

    def body(m_ref, t_ref, ms, ts):
        for i in range(CHUNK):
            ms[i * CHUNK:(i + 1) * CHUNK, :] = m_ref[:, i, :].T
        cidx = _iota((CHUNK, LANES), 0)

        def outer(i, carry):
            def inner(jj, acc):
                mrow = ms[pl.ds(i * CHUNK + jj, 1), :]
                return acc - mrow * ts[pl.ds(pl.multiple_of(jj * CHUNK, CHUNK), CHUNK), :]

            acc = lax.fori_loop(0, i, inner, jnp.where(cidx == i, 1.0, 0.0).astype(F32))
            ts[pl.ds(pl.multiple_of(i * CHUNK, CHUNK), CHUNK), :] = acc
            return carry

        lax.fori_loop(0, CHUNK, outer, 0)
        for i in range(CHUNK):
            t_ref[:, i, :] = ts[i * CHUNK:(i + 1) * CHUNK, :].T

    return pl.pallas_call(
        body, name="tri_inverse", in_specs=[VMEM_SPEC], out_specs=VMEM_SPEC,
        out_shape=_sds((LANES, CHUNK, CHUNK)),
        scratch_shapes=[pltpu.VMEM((CHUNK * CHUNK, LANES), F32), pltpu.VMEM((CHUNK * CHUNK, LANES), F32)],
        compiler_params=_params(),
    )(m3)


def _gdn_chunk_terms(q, k, v, b, gcc):
    eg = jnp.exp(gcc)
    last = gcc[CHUNK - 1:CHUNK, :]
    egl = jnp.exp(last - gcc)
    gl = jnp.exp(last)
    kb = k * b
    return eg, egl, gl, kb, v * b, kb * eg, q * eg, k * egl


GDN_BLOCK_CHUNKS = 4


def _gdn_block_specs(t, reverse):
    cb = GDN_BLOCK_CHUNKS
    nb = t // (cb * CHUNK)
    idx = (lambda i: nb - 1 - i) if reverse else (lambda i: i)
    tok = pl.BlockSpec((cb * CHUNK, D_GDN), lambda i: (idx(i), 0))
    mat = pl.BlockSpec((N_GDN_HEADS, cb, CHUNK, CHUNK), lambda i: (0, idx(i), 0, 0))
    state = pl.BlockSpec((N_GDN_HEADS, cb, GDN_HEAD_DIM, GDN_HEAD_DIM), lambda i: (0, idx(i), 0, 0))
    return nb, tok, mat, state


def _gdn_scan(qn, kn, cv, be, gc, tinv, amat):
    t = qn.shape[0]
    nch = t // CHUNK

    def body(q_ref, k_ref, v_ref, b_ref, gc_ref, t_ref, a_ref, o_ref, sall_ref, vn_ref, s_scr):
        @pl.when(pl.program_id(0) == 0)
        def _():
            s_scr[...] = jnp.zeros_like(s_scr)

        heads = range(N_GDN_HEADS)
        cols = [slice(hd * LANES, (hd + 1) * LANES) for hd in heads]
        s = [s_scr[hd] for hd in heads]
        for cc in range(GDN_BLOCK_CHUNKS):
            rs = slice(cc * CHUNK, (cc + 1) * CHUNK)
            terms = [_gdn_chunk_terms(q_ref[rs, cs], k_ref[rs, cs], v_ref[rs, cs], b_ref[rs, cs], gc_ref[rs, cs])
                     for cs in cols]
            for hd in heads:
                sall_ref[hd, cc] = s[hd]
            uw = [_mm(t_ref[hd, cc], jnp.concatenate([terms[hd][4], terms[hd][5]], axis=1)) for hd in heads]
            ws_qs = [_mm(jnp.concatenate([uw[hd][:, LANES:], terms[hd][6]], axis=0), s[hd]) for hd in heads]
            vn = [uw[hd][:, :LANES] - ws_qs[hd][:CHUNK] for hd in heads]
            a_vn = [_mm(a_ref[hd, cc], vn[hd]) for hd in heads]
            kd_vn = [_mm_tn(terms[hd][7], vn[hd]) for hd in heads]
            for hd in heads:
                vn_ref[rs, cols[hd]] = vn[hd]
                o_ref[rs, cols[hd]] = ws_qs[hd][CHUNK:] + a_vn[hd]
                s[hd] = s[hd] * terms[hd][2] + kd_vn[hd]
        for hd in heads:
            s_scr[hd] = s[hd]

    nb, tok, mat, state = _gdn_block_specs(t, False)
    return pl.pallas_call(
        body, name="gdn_scan", grid=(nb,),
        in_specs=[tok] * 5 + [mat, mat], out_specs=[tok, state, tok],
        out_shape=[_sds((t, D_GDN)), _sds((N_GDN_HEADS, nch, GDN_HEAD_DIM, GDN_HEAD_DIM)), _sds((t, D_GDN))],
        scratch_shapes=[pltpu.VMEM((N_GDN_HEADS, GDN_HEAD_DIM, GDN_HEAD_DIM), F32)],
        compiler_params=_params("arbitrary"),
    )(qn, kn, cv, be, gc, tinv, amat)


def _gdn_bwd(qn, kn, cv, be, gc, tinv, amat, s_all, vn_all, do):
    t = qn.shape[0]

    def body(q_ref, k_ref, v_ref, b_ref, gc_ref, t_ref, a_ref, sall_ref, vn_ref, do_ref,
             dq_ref, dk_ref, dv_ref, db_ref, dg_ref, ds_scr):
        @pl.when(pl.program_id(0) == 0)
        def _():
            ds_scr[...] = jnp.zeros_like(ds_scr)

        lastrow = _iota((CHUNK, LANES), 0) == CHUNK - 1
        heads = range(N_GDN_HEADS)
        cols = [slice(hd * LANES, (hd + 1) * LANES) for hd in heads]
        each = lambda fn: [fn(hd) for hd in heads]
        rows_cat = lambda x, y: jnp.concatenate([x, y], axis=0)
        lane_cat = lambda x, y: jnp.concatenate([x, y], axis=1)
        dsp = each(lambda hd: ds_scr[hd])
        for cc in reversed(range(GDN_BLOCK_CHUNKS)):
            rs = slice(cc * CHUNK, (cc + 1) * CHUNK)
            q = each(lambda hd: q_ref[rs, cols[hd]])
            k = each(lambda hd: k_ref[rs, cols[hd]])
            v = each(lambda hd: v_ref[rs, cols[hd]])
            b = each(lambda hd: b_ref[rs, cols[hd]])
            gcc = each(lambda hd: gc_ref[rs, cols[hd]])
            do_c = each(lambda hd: do_ref[rs, cols[hd]])
            vn = each(lambda hd: vn_ref[rs, cols[hd]])
            tn = each(lambda hd: t_ref[hd, cc])
            st = each(lambda hd: sall_ref[hd, cc])
            terms = each(lambda hd: _gdn_chunk_terms(q[hd], k[hd], v[hd], b[hd], gcc[hd]))
            eg, egl, gl, kb, vb, kbg, qd, kd = [[terms[hd][i] for hd in heads] for i in range(8)]
            w = each(lambda hd: _mm(tn[hd], kbg[hd]))
            a_do = each(lambda hd: _mm_tn(a_ref[hd, cc], do_c[hd]))
            kd_ds = each(lambda hd: _mm(kd[hd], dsp[hd]))
            da = each(lambda hd: _mm_nt(do_c[hd], vn[hd]))
            dkd = each(lambda hd: _mm_nt(vn[hd], dsp[hd]))
            by_k = each(lambda hd: _mm_nt(rows_cat(kb[hd], q[hd]), k[hd]))
            dgl = each(lambda hd: jnp.sum(jnp.sum(dsp[hd] * st[hd], axis=-1, keepdims=True), axis=0, keepdims=True))
            dvn = each(lambda hd: a_do[hd] + kd_ds[hd])
            do_dvn = each(lambda hd: rows_cat(do_c[hd], dvn[hd]))
            by_s = each(lambda hd: _mm_nt(do_dvn[hd], st[hd]))
            dqd = each(lambda hd: by_s[hd][:CHUNK])
            dvn_dw = each(lambda hd: lane_cat(dvn[hd], -by_s[hd][CHUNK:]))
            dsp = each(lambda hd: _mm_tn(rows_cat(qd[hd], -w[hd]), do_dvn[hd]) + gl[hd] * dsp[hd])
            dt = each(lambda hd: _mm_nt(dvn_dw[hd], lane_cat(vb[hd], kbg[hd])))
            by_t = each(lambda hd: _mm_tn(tn[hd], dvn_dw[hd]))
            tt_dt = each(lambda hd: _mm_tn(tn[hd], dt[hd]))
            dm_raw = each(lambda hd: _mm_nt(tt_dt[hd], tn[hd]))
            masks = each(lambda hd: _chunk_decay(gcc[hd]))
            dkk = each(lambda hd: jnp.where(masks[hd][1], -dm_raw[hd], 0.0) * masks[hd][0])
            dqk = each(lambda hd: da[hd] * masks[hd][0])
            dqk_dkk = each(lambda hd: rows_cat(dqk[hd], dkk[hd]))
            on_k = each(lambda hd: _mm(dqk_dkk[hd], k[hd]))
            dk_mm = each(lambda hd: _mm_tn(dqk_dkk[hd], rows_cat(q[hd], kb[hd])))
            for hd in heads:
                cs = cols[hd]
                dvb, dkbg = by_t[hd][:, :LANES], by_t[hd][:, LANES:]
                gmat = dkk[hd] * by_k[hd][:CHUNK] + dqk[hd] * by_k[hd][CHUNK:]
                dq_ref[rs, cs] = dqd[hd] * eg[hd] + on_k[hd][:CHUNK]
                dkb = on_k[hd][CHUNK:] + dkbg * eg[hd]
                dk_ref[rs, cs] = dkd[hd] * egl[hd] + dk_mm[hd] + dkb * b[hd]
                db = jnp.sum(dkb * k[hd], axis=-1, keepdims=True) + jnp.sum(dvb * v[hd], axis=-1, keepdims=True)
                db_ref[rs, cs] = jnp.broadcast_to(db, (CHUNK, LANES))
                dv_ref[rs, cs] = dvb * b[hd]
                dkd_kd = jnp.sum(dkd[hd] * kd[hd], axis=-1, keepdims=True)
                col_sums = jnp.sum(lane_cat(gmat, jnp.zeros_like(gmat)).T, axis=-1, keepdims=True)
                dgc = (jnp.sum(gmat, axis=-1, keepdims=True) - col_sums[:CHUNK]
                       + jnp.sum(dqd[hd] * qd[hd], axis=-1, keepdims=True)
                       + jnp.sum(dkbg * kbg[hd], axis=-1, keepdims=True) - dkd_kd)
                extra = jnp.sum(dkd_kd, axis=0, keepdims=True) + dgl[hd] * gl[hd]
                dg_ref[rs, cs] = dgc + jnp.where(lastrow, extra, 0.0)
        for hd in heads:
            ds_scr[hd] = dsp[hd]
        dg = dg_ref[...]
        row = _iota(dg.shape, 0)
        pos = row % CHUNK
        step = 1
        while step < CHUNK:
            dg = dg + jnp.where(pos < CHUNK - step, pltpu.roll(dg, dg.shape[0] - step, 0), 0.0)
            step *= 2
        dg_ref[...] = dg

    nb, tok, mat, state = _gdn_block_specs(t, True)
    return pl.pallas_call(
        body, name="gdn_bwd", grid=(nb,),
        in_specs=[tok] * 5 + [mat, mat, state, tok, tok], out_specs=[tok] * 5, out_shape=[_sds((t, D_GDN))] * 5,
        scratch_shapes=[pltpu.VMEM((N_GDN_HEADS, GDN_HEAD_DIM, GDN_HEAD_DIM), F32)],
        compiler_params=_params("arbitrary"),
    )(qn, kn, cv, be, gc, tinv, amat, s_all, vn_all, do)


def _gdn_bwd_conv(proj, convw, dqn, dkn, dcv):
    t = proj.shape[0]

    def body(xq_ref, xk_ref, xv_ref, wq_ref, wk_ref, wv_ref, dq_ref, dk_ref, dv_ref,
             dxq_ref, dxk_ref, dxv_ref, dwq_ref, dwk_ref, dwv_ref):
        row = _iota((t, LANES), 0)

        def one(x_ref, w_ref, d_ref, dx_ref, dw_ref, scale):
            x = x_ref[...]
            w = w_ref[...]
            y = _conv(x, w, row)
            sg = _sigmoid(y)
            dc = d_ref[...]
            if scale is not None:
                c = y * sg
                r = lax.rsqrt(jnp.sum(c * c, axis=-1, keepdims=True) + EPS)
                ch = c * r
                dc = scale * r * (dc - ch * jnp.sum(dc * ch, axis=-1, keepdims=True))
            dy = dc * sg * (1.0 + y * (1.0 - sg))
            dx_ref[...] = (w[3:4, :] * dy + w[2:3, :] * _shift_up(dy, 1, row) + w[1:2, :] * _shift_up(dy, 2, row)
                           + w[0:1, :] * _shift_up(dy, 3, row))
            for jj in range(CONV_K):
                xs = x if jj == CONV_K - 1 else _shift_down(x, CONV_K - 1 - jj, row)
                dw_ref[jj:jj + 1, :] = jnp.sum(dy * xs, axis=0, keepdims=True)

        one(xq_ref, wq_ref, dq_ref, dxq_ref, dwq_ref, GDN_QSCALE)
        one(xk_ref, wk_ref, dk_ref, dxk_ref, dwk_ref, 1.0)
        one(xv_ref, wv_ref, dv_ref, dxv_ref, dwv_ref, None)

    col, cw, _ = _gdn_specs(t)
    return pl.pallas_call(
        body, name="gdn_bwd_conv", grid=(N_GDN_HEADS,),
        in_specs=[col(12), col(16), col(20), cw(0), cw(4), cw(8), col(0), col(0), col(0)],
        out_specs=[col(0), col(0), col(0), cw(0), cw(0), cw(0)],
        out_shape=[_sds((t, D_GDN))] * 3 + [_sds((CONV_K, D_GDN))] * 3,
        compiler_params=_params("parallel"),
    )(proj, proj, proj, convw, convw, convw, dqn, dkn, dcv)


def _mix_out(fox_n, gdn_o, proj, gnw, w_out, x, pmw, plw):
    t = x.shape[0]
    tm = min(MATMUL_BLOCK, t)

    def body(fn_ref, go_ref, gz_ref, gnw_ref, w_ref, x_ref, pmw_ref, plw_ref, x1_ref, h2_ref, mixed_ref, omix_ref,
             h2t_ref):
        omix_ref[:, 0:D_FOX] = fn_ref[...]
        for hd in range(N_GDN_HEADS):
            cs = slice(hd * LANES, (hd + 1) * LANES)
            go = go_ref[:, cs]
            r = lax.rsqrt(jnp.mean(go * go, axis=-1, keepdims=True) + EPS)
            gz = gz_ref[:, cs]
            omix_ref[:, D_FOX + hd * LANES:D_FOX + (hd + 1) * LANES] = (
                go * r * gnw_ref[...] * (gz * _sigmoid(gz))).astype(BF)
        mixed = jnp.dot(omix_ref[...], w_ref[...], preferred_element_type=F32)
        mixed_ref[...] = mixed
        r2 = lax.rsqrt(jnp.mean(mixed * mixed, axis=-1, keepdims=True) + EPS)
        x1 = x_ref[...] + mixed * r2 * pmw_ref[...]
        x1_ref[...] = x1
        r3 = lax.rsqrt(jnp.mean(x1 * x1, axis=-1, keepdims=True) + EPS)
        h2 = x1 * r3 * plw_ref[...]
        h2_ref[...] = h2.astype(BF)
        h2t_ref[...] = h2.T.astype(BF)

    tok = lambda w: pl.BlockSpec((tm, w), lambda i: (i, 0))
    vec = lambda w: pl.BlockSpec((1, w), lambda i: (0, 0))
    return pl.pallas_call(
        body, name="mix_out", grid=(t // tm,),
        in_specs=[tok(D_FOX), tok(D_GDN), pl.BlockSpec((tm, D_GDN), lambda i: (i, COL_GZ // D_GDN)), vec(LANES),
                  pl.BlockSpec((D_MODEL, D_MODEL), lambda i: (0, 0)), tok(D_MODEL), vec(D_MODEL), vec(D_MODEL)],
        out_specs=[tok(D_MODEL)] * 4 + [pl.BlockSpec((D_MODEL, tm), lambda i: (0, i))],
        out_shape=[_sds((t, D_MODEL)), _sds((t, D_MODEL), BF), _sds((t, D_MODEL)), _sds((t, D_MODEL), BF),
                   _sds((D_MODEL, t), BF)],
        compiler_params=_params("parallel"),
    )(fox_n, gdn_o, proj, gnw, w_out, x, pmw, plw)


def _out_bwd(dmixed, w_out, o_fox, gdn_o, proj, fnw, gnw):
    t = dmixed.shape[0]
    tm = min(MATMUL_BLOCK, t)

    def body(dm_ref, w_ref, of_ref, go_ref, gz_ref, fnw_ref, gnw_ref, dof_ref, dgo_ref, dgz_ref, dfw_ref, dgw_ref):
        i = pl.program_id(0)

        @pl.when(i == 0)
        def _():
            dfw_ref[...] = jnp.zeros_like(dfw_ref)
            dgw_ref[...] = jnp.zeros_like(dgw_ref)

        domix = _mm_nt(dm_ref[...], w_ref[...])
        first = _iota((1, LANES), 1) < FOX_HEAD_DIM
        dfw = jnp.zeros((1, LANES), F32)
        dgw = jnp.zeros((1, LANES), F32)
        for pr in range(N_FOX_HEADS // 2):
            cs = slice(pr * LANES, (pr + 1) * LANES)
            o = of_ref[:, cs]
            dfn = domix[:, cs]
            o2 = o * o
            s0 = jnp.sum(jnp.where(first, o2, 0.0), axis=-1, keepdims=True)
            s1 = jnp.sum(jnp.where(first, 0.0, o2), axis=-1, keepdims=True)
            r = lax.rsqrt(jnp.where(first, s0, s1) * (1.0 / FOX_HEAD_DIM) + EPS)
            oh = o * r
            dfw = dfw + jnp.sum(dfn * oh, axis=0, keepdims=True)
            doh = dfn * fnw_ref[...]
            pr_ = doh * oh
            m0 = jnp.sum(jnp.where(first, pr_, 0.0), axis=-1, keepdims=True)
            m1 = jnp.sum(jnp.where(first, 0.0, pr_), axis=-1, keepdims=True)
            dof_ref[:, cs] = r * (doh - oh * jnp.where(first, m0, m1) * (1.0 / FOX_HEAD_DIM))
        for hd in range(N_GDN_HEADS):
            cs = slice(hd * LANES, (hd + 1) * LANES)
            go = go_ref[:, cs]
            gz = gz_ref[:, cs]
            dgated = domix[:, D_FOX + hd * LANES:D_FOX + (hd + 1) * LANES]
            r = lax.rsqrt(jnp.mean(go * go, axis=-1, keepdims=True) + EPS)
            goh = go * r
            sg = _sigmoid(gz)
            sz = gz * sg
            gn = goh * gnw_ref[...]
            dgn = dgated * sz
            dgz_ref[:, cs] = dgated * gn * sg * (1.0 + gz * (1.0 - sg))
            dgw = dgw + jnp.sum(dgn * goh, axis=0, keepdims=True)
            dgh = dgn * gnw_ref[...]
            dgo_ref[:, cs] = r * (dgh - goh * jnp.mean(dgh * goh, axis=-1, keepdims=True))
        dfw_ref[...] += dfw + pltpu.roll(dfw, FOX_HEAD_DIM, 1)
        dgw_ref[...] += dgw

    tok = lambda w: pl.BlockSpec((tm, w), lambda i: (i, 0))
    vec = lambda w: pl.BlockSpec((1, w), lambda i: (0, 0))
    return pl.pallas_call(
        body, name="out_bwd", grid=(t // tm,),
        in_specs=[tok(D_MODEL), pl.BlockSpec((D_MODEL, D_MODEL), lambda i: (0, 0)), tok(D_FOX), tok(D_GDN),
                  pl.BlockSpec((tm, D_GDN), lambda i: (i, COL_GZ // D_GDN)), vec(LANES), vec(LANES)],
        out_specs=[tok(D_FOX), tok(D_GDN), tok(D_GDN), vec(LANES), vec(LANES)],
        out_shape=[_sds((t, D_FOX)), _sds((t, D_GDN)), _sds((t, D_GDN)), _sds((1, LANES)), _sds((1, LANES))],
        compiler_params=_params("arbitrary"),
    )(dmixed, w_out, o_fox, gdn_o, proj, fnw, gnw)


def _mlp_up(h2, w_up):
    t = h2.shape[0]
    tm = min(MATMUL_BLOCK, t)
    pc = D_FF // N_DEV

    def body(h_ref, w_ref, up_ref):
        h = h_ref[...]
        for p in range(N_DEV):
            up_ref[:, p * pc:(p + 1) * pc] = jnp.dot(h, w_ref[p], preferred_element_type=F32).astype(BF)

    return pl.pallas_call(
        body, name="mlp_up", grid=(t // tm,),
        in_specs=[pl.BlockSpec((tm, D_MODEL), lambda i: (i, 0)),
                  pl.BlockSpec((N_DEV, D_MODEL, pc), lambda i: (0, 0, 0))],
        out_specs=pl.BlockSpec((tm, D_FF), lambda i: (i, 0)), out_shape=_sds((t, D_FF), BF),
        compiler_params=_params("parallel"),
    )(h2, w_up)


def _mlp_down_loss(up, w_down, x1, pw, target):
    t = up.shape[0]
    tm = min(MATMUL_BLOCK, t)

    def body(up_ref, w_ref, x1_ref, pw_ref, tg_ref, dy_ref, dx2_ref, loss_ref, dpw_ref):
        i = pl.program_id(0)

        @pl.when(i == 0)
        def _():
            loss_ref[...] = jnp.zeros_like(loss_ref)
            dpw_ref[...] = jnp.zeros_like(dpw_ref)

        u = jnp.maximum(up_ref[...].astype(F32), 0.0)
        y = jnp.dot((u * u).astype(BF), w_ref[...], preferred_element_type=F32)
        r = lax.rsqrt(jnp.mean(y * y, axis=-1, keepdims=True) + EPS)
        yh = y * r
        pw = pw_ref[...]
        err = x1_ref[...] + yh * pw - tg_ref[...]
        part = jnp.sum(jnp.sum(err * err, axis=-1, keepdims=True), axis=0, keepdims=True) * (0.5 / D_MODEL)
        loss_ref[...] += jnp.broadcast_to(part, loss_ref.shape)
        dx2 = err * (1.0 / D_MODEL)
        dx2_ref[...] = dx2
        dpw_ref[...] += jnp.sum(dx2 * yh, axis=0, keepdims=True)
        dyh = dx2 * pw
        dy_ref[...] = (r * (dyh - yh * jnp.mean(dyh * yh, axis=-1, keepdims=True))).astype(BF)

    tok = lambda w: pl.BlockSpec((tm, w), lambda i: (i, 0))
    vec = lambda w: pl.BlockSpec((1, w), lambda i: (0, 0))
    return pl.pallas_call(
        body, name="mlp_down_loss", grid=(t // tm,),
        in_specs=[tok(D_FF), pl.BlockSpec((D_FF, D_MODEL), lambda i: (0, 0)), tok(D_MODEL), vec(D_MODEL), tok(D_MODEL)],
        out_specs=[tok(D_MODEL), tok(D_MODEL), vec(LANES), vec(D_MODEL)],
        out_shape=[_sds((t, D_MODEL), BF), _sds((t, D_MODEL)), _sds((1, LANES)), _sds((1, D_MODEL))],
        compiler_params=_params("arbitrary"),
    )(up, w_down, x1, pw, target)


def _mlp_bwd_act(dy, w_down, up):
    t = dy.shape[0]
    tm = min(MATMUL_BLOCK, t)

    def body(dy_ref, w_ref, up_ref, dup_ref):
        da = lax.dot_general(dy_ref[...], w_ref[...], (((1,), (1,)), ((), ())), preferred_element_type=F32)
        dup_ref[...] = (da * (2.0 * jnp.maximum(up_ref[...].astype(F32), 0.0))).astype(BF)

    return pl.pallas_call(
        body, name="mlp_bwd_act", grid=(t // tm,),
        in_specs=[pl.BlockSpec((tm, D_MODEL), lambda i: (i, 0)), pl.BlockSpec((D_FF, D_MODEL), lambda i: (0, 0)),
                  pl.BlockSpec((tm, D_FF), lambda i: (i, 0))],
        out_specs=pl.BlockSpec((tm, D_FF), lambda i: (i, 0)), out_shape=_sds((t, D_FF), BF),
        compiler_params=_params("parallel"),
    )(dy, w_down, up)


def _mlp_bwd_in(dup, w_up, x1, plw, dx2, mixed, pmw):
    t = dup.shape[0]
    tm = min(MATMUL_BLOCK, t)

    def body(dup_ref, w_ref, x1_ref, plw_ref, dx2_ref, mx_ref, pmw_ref, dx1_ref, dmixed_ref, dplw_ref, dpmw_ref):
        i = pl.program_id(0)

        @pl.when(i == 0)
        def _():
            dplw_ref[...] = jnp.zeros_like(dplw_ref)
            dpmw_ref[...] = jnp.zeros_like(dpmw_ref)

        pc = D_FF // N_DEV
        dh = _mm_nt(dup_ref[:, 0:pc], w_ref[0])
        for p in range(1, N_DEV):
            dh = dh + _mm_nt(dup_ref[:, p * pc:(p + 1) * pc], w_ref[p])
        x1 = x1_ref[...]
        r = lax.rsqrt(jnp.mean(x1 * x1, axis=-1, keepdims=True) + EPS)
        xh = x1 * r
        dplw_ref[...] += jnp.sum(dh * xh, axis=0, keepdims=True)
        dxh = dh * plw_ref[...]
        dx1 = dx2_ref[...] + r * (dxh - xh * jnp.mean(dxh * xh, axis=-1, keepdims=True))
        dx1_ref[...] = dx1
        mx = mx_ref[...]
        r2 = lax.rsqrt(jnp.mean(mx * mx, axis=-1, keepdims=True) + EPS)
        mh = mx * r2
        dpmw_ref[...] += jnp.sum(dx1 * mh, axis=0, keepdims=True)
        dmh = dx1 * pmw_ref[...]
        dmixed_ref[...] = (r2 * (dmh - mh * jnp.mean(dmh * mh, axis=-1, keepdims=True))).astype(BF)

    tok = lambda w: pl.BlockSpec((tm, w), lambda i: (i, 0))
    vec = lambda w: pl.BlockSpec((1, w), lambda i: (0, 0))
    return pl.pallas_call(
        body, name="mlp_bwd_in", grid=(t // tm,),
        in_specs=[tok(D_FF), pl.BlockSpec((N_DEV, D_MODEL, D_FF // N_DEV), lambda i: (0, 0, 0)), tok(D_MODEL),
                  vec(D_MODEL), tok(D_MODEL), tok(D_MODEL), vec(D_MODEL)],
        out_specs=[tok(D_MODEL), tok(D_MODEL), vec(D_MODEL), vec(D_MODEL)],
        out_shape=[_sds((t, D_MODEL)), _sds((t, D_MODEL), BF), _sds((1, D_MODEL)), _sds((1, D_MODEL))],
        compiler_params=_params("arbitrary"),
    )(dup, w_up, x1, plw, dx2, mixed, pmw)


def _wgrad(a, b, a_cols, split=1, a_fn=None, a_block0=0, name="wgrad"):
    t, b_cols = b.shape
    n_a = (a.shape[1] - a_block0 * a_cols) // a_cols if a_block0 else a.shape[1] // a_cols

    def body(a_ref, b_ref, o_ref):
        av = a_ref[...]
        if a_fn is not None:
            av = a_fn(av)
        o_ref[...] = _mm_tn(av, b_ref[...]).astype(BF).reshape(o_ref.shape)

    return pl.pallas_call(
        body, name=name, grid=(n_a,),
        in_specs=[pl.BlockSpec((t, a_cols), lambda i: (0, i + a_block0)), pl.BlockSpec((t, b_cols), lambda i: (0, 0))],
        out_specs=pl.BlockSpec((split, a_cols // split, b_cols), lambda i: (i, 0, 0)),
        out_shape=_sds((n_a * split, a_cols // split, b_cols), BF),
        compiler_params=_params("parallel"),
    )(a, b)


def _wgrad_pre_t(at, b, b_cols, name):
    rows, t = at.shape
    n_b = b.shape[1] // b_cols

    def body(a_ref, b_ref, o_ref):
        o_ref[0] = jnp.dot(a_ref[...], b_ref[...], preferred_element_type=F32).astype(BF)

    return pl.pallas_call(
        body, name=name, grid=(n_b,),
        in_specs=[pl.BlockSpec((rows, t), lambda j: (0, 0)), pl.BlockSpec((t, b_cols), lambda j: (0, j))],
        out_specs=pl.BlockSpec((1, rows, b_cols), lambda j: (j, 0, 0)), out_shape=_sds((n_b, rows, b_cols), BF),
        compiler_params=_params("parallel"),
    )(at, b)


def _small_bwd(proj, fb, al, dtb, dcq, dckt, dbe, dge):
    t = proj.shape[0]

    def body(sm_ref, fb_ref, al_ref, dtb_ref, dcq_ref, dckt_ref, dbe_ref, dge_ref, dsm_ref, dvec_ref):
        s = sm_ref[...]
        lane = _iota((1, LANES), 1)
        dcum = dcq_ref[...] - dckt_ref[...].T
        row = _iota((t, LANES), 0)
        step = 1
        while step < t:
            dcum = dcum + _shift_up(dcum, step, row)
            step *= 2
        dff = dcum * _sigmoid(-(s + fb_ref[...]))
        dbeta = jnp.zeros((t, LANES), F32)
        dg = jnp.zeros((t, LANES), F32)
        for hd in range(N_GDN_HEADS):
            dbeta = jnp.where(lane == SM_GB + hd, dbe_ref[:, hd * LANES:hd * LANES + 1], dbeta)
            dg = jnp.where(lane == SM_GA + hd, dge_ref[:, hd * LANES:hd * LANES + 1], dg)
        beta = _sigmoid(s)
        dgb = dbeta * beta * (1.0 - beta)
        za = s + dtb_ref[...]
        nea = -jnp.exp(al_ref[...])
        dga = dg * nea * _sigmoid(za)
        is_f = lane < SM_GB
        is_b = (lane >= SM_GB) & (lane < SM_GA)
        is_a = (lane >= SM_GA) & (lane < SM_GA + 4)
        dsm_ref[...] = jnp.where(is_f, dff, jnp.where(is_b, dgb, jnp.where(is_a, dga, 0.0)))
        dvec_ref[...] = jnp.zeros_like(dvec_ref)
        dvec_ref[0:1, :] = jnp.sum(jnp.where(is_f, dff, 0.0), axis=0, keepdims=True)
        dvec_ref[1:2, :] = jnp.sum(jnp.where(is_a, dg * nea * _softplus(za), 0.0), axis=0, keepdims=True)
        dvec_ref[2:3, :] = jnp.sum(jnp.where(is_a, dga, 0.0), axis=0, keepdims=True)

    vec = pl.BlockSpec((1, LANES), lambda i: (0, 0))
    full = lambda r, c: pl.BlockSpec((r, c), lambda i: (0, 0))
    return pl.pallas_call(
        body, name="small_bwd", grid=(1,),
        in_specs=[pl.BlockSpec((t, LANES), lambda i: (0, COL_SMALL // LANES)), vec, vec, vec, full(t, LANES),
                  full(LANES, t), full(t, 512), full(t, 512)],
        out_specs=[full(t, LANES), full(8, LANES)], out_shape=[_sds((t, LANES)), _sds((8, LANES))],
        compiler_params=_params("arbitrary"),
    )(proj, fb, al, dtb, dcq, dckt, dbe, dge)


def _pack_dproj(dfox, dgdn, dgz, dsm):
    t = dgz.shape[0]
    tm = min(MATMUL_BLOCK, t)

    def body(*refs):
        parts, dp_ref = refs[:8], refs[8]
        col = 0
        for part in parts:
            width = part.shape[1]
            dp_ref[:, col:col + width] = part[...].astype(BF)
            col += width

    tok = lambda w: pl.BlockSpec((tm, w), lambda i: (i, 0))
    return pl.pallas_call(
        body, name="pack_dproj", grid=(t // tm,), in_specs=[tok(D_FOX)] * 3 + [tok(D_GDN)] * 4 + [tok(LANES)],
        out_specs=tok(PROJ_W), out_shape=_sds((t, PROJ_W), BF), compiler_params=_params("parallel"),
    )(*dfox, *dgdn, dgz, dsm)


def _in_bwd(dproj, wt_al, x, nw, dx1):
    t = x.shape[0]
    tm = min(MATMUL_BLOCK, t)

    def body(dp_ref, w_ref, x_ref, nw_ref, dx1_ref, dx_ref, dnw_ref):
        i = pl.program_id(0)

        @pl.when(i == 0)
        def _():
            dnw_ref[...] = jnp.zeros_like(dnw_ref)

        dh = jnp.dot(dp_ref[...], w_ref[...], preferred_element_type=F32)
        xv = x_ref[...]
        r = lax.rsqrt(jnp.mean(xv * xv, axis=-1, keepdims=True) + EPS)
        xh = xv * r
        dnw_ref[...] += jnp.sum(dh * xh, axis=0, keepdims=True)
        dxh = dh * nw_ref[...]
        dx_ref[...] = dx1_ref[...] + r * (dxh - xh * jnp.mean(dxh * xh, axis=-1, keepdims=True))

    tok = lambda w: pl.BlockSpec((tm, w), lambda i: (i, 0))
    vec = lambda w: pl.BlockSpec((1, w), lambda i: (0, 0))
    return pl.pallas_call(
        body, name="in_bwd", grid=(t // tm,),
        in_specs=[tok(PROJ_W), pl.BlockSpec((PROJ_W, D_MODEL), lambda i: (0, 0)), tok(D_MODEL), vec(D_MODEL),
                  tok(D_MODEL)],
        out_specs=[tok(D_MODEL), vec(D_MODEL)], out_shape=[_sds((t, D_MODEL)), _sds((1, D_MODEL))],
        compiler_params=_params("arbitrary"),
    )(dproj, wt_al, x, nw, dx1)


def _row(v, width=None):
    v = v.reshape(1, -1).astype(F32)
    if width is not None and v.shape[1] < width:
        v = jnp.pad(v, ((0, 0), (0, width - v.shape[1])))
    return v


def _lane_vec(v, first):
    return jnp.pad(v.astype(F32), (first, LANES - first - v.shape[0])).reshape(1, LANES)


def _local_step(x, target, wt_al, late_weights, on_grads, convw, pre_mix_norm, fox_f_bias, fox_out_norm,
                gdn_a_log, gdn_dt_bias, gdn_out_norm, post_mix_norm, pre_mlp_norm, post_mlp_norm):
    t = x.shape[0]
    nch = t // CHUNK
    nw, pmw, plw, pw = _row(pre_mix_norm), _row(post_mix_norm), _row(pre_mlp_norm), _row(post_mlp_norm)
    fb, al, dtb = _lane_vec(fox_f_bias, SM_FF), _lane_vec(gdn_a_log, SM_GA), _lane_vec(gdn_dt_bias, SM_GA)
    fnw = _row(jnp.tile(fox_out_norm, 2))
    gnw = _row(gdn_out_norm)

    proj, h = _norm_proj(x, nw, wt_al)
    cumt, beta, g = _small_prep(proj, fb, al, dtb)
    o_fox, lse, fox_n = _fox_fwd(proj, cumt, fnw)
    qn, kn, cv, gc, be, mmat, amat = _gdn_prep(proj, convw, beta, g)
    n_prob = N_GDN_HEADS * nch
    m3 = mmat.reshape(n_prob, CHUNK, CHUNK)
    if n_prob < LANES:
        m3 = jnp.pad(m3, ((0, LANES - n_prob), (0, 0), (0, 0)))
    tinv = _tri_inverse(m3)[:n_prob].reshape(N_GDN_HEADS, nch, CHUNK, CHUNK)
    token = late_weights("mlp_relay", tinv)
    gdn_o, s_all, vn_all = _gdn_scan(qn, kn, cv, be, gc, tinv, amat)
    w_out = late_weights("w_out", gdn_o)
    x1, h2, mixed, omix, h2t = _mix_out(fox_n, gdn_o, proj, gnw + token[0:1, 0:1], w_out, x, pmw, plw)
    w_up, w_down = late_weights("mlp", h2)
    up = _mlp_up(h2, w_up)
    dy, dx2, loss, d_pw = _mlp_down_loss(up, w_down, x1, pw, target)

    dup = _mlp_bwd_act(dy, w_down, up)
    relu2 = lambda u: jnp.square(jnp.maximum(u.astype(F32), 0.0))
    g_down = _wgrad(up, dy, D_FF // N_DEV, a_fn=relu2, name="wgrad_down")
    g_up = _wgrad_pre_t(h2t, dup, D_FF // N_DEV, name="wgrad_up")
    token = on_grads("mlp", (g_up, g_down))
    dx1, dmixed, d_plw, d_pmw = _mlp_bwd_in(dup, w_up, x1, plw + token[0:1, 0:1], dx2, mixed, pmw)
    token = on_grads("w_out", _wgrad(omix, dmixed, 512, split=4, name="wgrad_out"))
    do_fox, dgo, dgz, d_fnw, d_gnw = _out_bwd(dmixed, w_out, o_fox, gdn_o, proj, fnw + token[0:1, 0:1], gnw)
    dfq, dfk, dfv, dcq, dckt = _fox_bwd(proj, cumt, lse, o_fox, do_fox)
    dqn, dkn, dcv, dbe, dge = _gdn_bwd(qn, kn, cv, be, gc, tinv, amat, s_all, vn_all, dgo)
    dxq, dxk, dxv, dwq, dwk, dwv = _gdn_bwd_conv(proj, convw, dqn, dkn, dcv)
    dsm, dvec = _small_bwd(proj, fb, al, dtb, dcq, dckt, dbe, dge)
    dproj = _pack_dproj((dfq, dfk, dfv), (dxq, dxk, dxv), dgz, dsm)
    g_main = _wgrad(dproj, h, 512, name="wgrad_in")
    g_tail = _wgrad(dproj, h, LANES, a_block0=COL_SMALL // LANES, name="wgrad_in_small")
    token = on_grads("w_in", jnp.concatenate([g_main.reshape(COL_SMALL, D_MODEL), g_tail[0]]))
    grad_x, d_nw = _in_bwd(dproj, wt_al, x, nw + token[0:1, 0:1], dx1)
    small = dict(norms=(d_nw, d_pmw, d_plw, d_pw), fox_out_norm=d_fnw, gdn_out_norm=d_gnw, loss=loss, vectors=dvec,
                 conv=(dwq, dwk, dwv))
    return grad_x, small


MESH_IDS = pl.DeviceIdType.MESH
CHIP_FLIPS = ((0, 0), (1, 0), (0, 1), (1, 1))
ANY_SPEC = pl.BlockSpec(memory_space=pl.ANY)


def _place():
    return lax.axis_index("x"), lax.axis_index("y"), lax.axis_index("c")


def _all_gather(blocks):
    n = len(blocks)

    def body(*refs):
        ins, outs, (send_sems, recv_sems, local_sems) = refs[:n], refs[n:2 * n], refs[2 * n:]
        x, y, c = _place()
        sibling = (x, y, 1 - c)
        chips = [(x ^ fx, y ^ fy) for fx, fy in CHIP_FLIPS[1:]]

        def slot(out, px, py, pc):
            return out.at[4 * px + 2 * py + pc]

        def copy(a, k, block, to, src=None):
            return pltpu.make_async_remote_copy(
                src_ref=slot(outs[a], *block) if src is None else src, dst_ref=slot(outs[a], *block),
                send_sem=send_sems.at[a, k], recv_sem=recv_sems.at[a, k], device_id=to, device_id_type=MESH_IDS)

        pending = []
        for a in range(n):
            mine = pltpu.make_async_copy(ins[a], slot(outs[a], x, y, c), local_sems.at[a])
            mine.start()
            pending.append(mine)
        sends = []
        for a in range(n):
            first = [copy(a, 0, (x, y, c), sibling, src=ins[a])]
            first += [copy(a, 1 + j, (x, y, c), (*chip, c), src=ins[a]) for j, chip in enumerate(chips)]
            for cp in first:
                cp.start()
            sends += first
        for a in range(n):
            for j, chip in enumerate(chips):
                copy(a, 1 + j, (*chip, c), (x, y, c)).wait_recv()
                fwd = copy(a, 4 + j, (*chip, c), sibling)
                fwd.start()
                sends.append(fwd)
        for a in range(n):
            copy(a, 0, sibling, (x, y, c)).wait_recv()
            for j, chip in enumerate(chips):
                copy(a, 4 + j, (*chip, 1 - c), (x, y, c)).wait_recv()
        for cp in sends:
            cp.wait_send()
        for cp in pending:
            cp.wait()

    return pl.pallas_call(
        body, name="all_gather_weights", in_specs=[ANY_SPEC] * n, out_specs=[ANY_SPEC] * n,
        out_shape=[_sds((N_DEV,) + b.shape, b.dtype) for b in blocks],
        scratch_shapes=[pltpu.SemaphoreType.DMA((n, 7)), pltpu.SemaphoreType.DMA((n, 7)), pltpu.SemaphoreType.DMA((n,))],
        compiler_params=pltpu.CompilerParams(has_side_effects=True),
    )(*blocks)


def _adamw(w, g, m, v):
    m = ADAM_B1 * m + (1.0 - ADAM_B1) * g
    v = ADAM_B2 * v + (1.0 - ADAM_B2) * (g * g)
    m_hat = m / (1.0 - ADAM_B1 ** ADAM_STEP)
    v_hat = v / (1.0 - ADAM_B2 ** ADAM_STEP)
    return -ADAM_LR * (m_hat / (jnp.sqrt(v_hat) + ADAM_EPS) + ADAM_WD * w), m, v


def _pair_reduce(g, name):
    _, r, c_ = g.shape
    n = len(CHIP_FLIPS)

    def body(g_ref, out_ref, sib_buf, send_sems, recv_sems):
        x, y, c = _place()
        chips = [(x ^ fx, y ^ fy) for fx, fy in CHIP_FLIPS]
        piece = lambda chip, core: g_ref.at[4 * chip[0] + 2 * chip[1] + core]
        copies = [pltpu.make_async_remote_copy(
            src_ref=piece(chip, 1 - c), dst_ref=sib_buf.at[j], send_sem=send_sems.at[j], recv_sem=recv_sems.at[j],
            device_id=(x, y, 1 - c), device_id_type=MESH_IDS) for j, chip in enumerate(chips)]
        for cp in copies:
            cp.start()
        for j, chip in enumerate(chips):
            copies[j].wait_recv()
            out_ref[j] = (piece(chip, c)[...].astype(F32) + sib_buf[j].astype(F32)).astype(BF)
        for cp in copies:
            cp.wait_send()

    return pl.pallas_call(
        body, name=name, in_specs=[VMEM_SPEC], out_specs=VMEM_SPEC, out_shape=_sds((n, r, c_), BF),
        scratch_shapes=[pltpu.VMEM((n, r, c_), BF), pltpu.SemaphoreType.DMA((n,)), pltpu.SemaphoreType.DMA((n,))],
        compiler_params=pltpu.CompilerParams(vmem_limit_bytes=VMEM_LIMIT, has_side_effects=True),
    )(g)


HBM_SPEC = pl.BlockSpec(memory_space=pltpu.HBM)
SEM_SPEC = pl.BlockSpec(memory_space=pltpu.SEMAPHORE)
DATAFLOW = pltpu.SideEffectType.DATAFLOW_SIDE_EFFECTING


def _peers():
    x, y, c = _place()
    return 4 * x + 2 * y + c, [(x ^ (k >> 2), y ^ ((k >> 1) & 1), c ^ (k & 1)) for k in range(1, N_DEV)]


def _peer_index(peer):
    return 4 * peer[0] + 2 * peer[1] + peer[2]


def _zones_with_own(srcs, pieces, name, after=None, dtype=None, chips=False):
    n = len(srcs)
    slots = len(CHIP_FLIPS) if chips else N_DEV
    extra = [] if after is None else [after]
    dtypes = [s_.dtype if pieces or dtype is None else dtype for s_ in srcs]

    def body(me_ref, *refs):
        outs = refs[n + len(extra):]
        for a in range(n):
            if pieces:
                outs[a][0] = refs[a][0]
            else:
                val = refs[a][...].astype(dtypes[a])
                outs[a][0] = val
                outs[n + a][...] = val

    shapes = [s_.shape[1:] if pieces else s_.shape for s_ in srcs]
    mine = lambda sh: pl.BlockSpec((1,) + sh, lambda i, me_ref: (me_ref[0], 0, 0))
    whole = lambda sh: pl.BlockSpec(sh, lambda i, me_ref: (0, 0))
    in_specs = [mine(sh) if pieces else whole(sh) for sh in shapes]
    out_specs = [mine(sh) for sh in shapes] + ([] if pieces else [whole(sh) for sh in shapes])
    out_shape = [_sds((slots,) + sh, dt) for sh, dt in zip(shapes, dtypes)]
    out_shape += [] if pieces else [_sds(sh, dt) for sh, dt in zip(shapes, dtypes)]
    x, y, c = _place()
    own = 0 * x if chips else 4 * x + 2 * y + c
    out = pl.pallas_call(
        body, name=name,
        grid_spec=pltpu.PrefetchScalarGridSpec(num_scalar_prefetch=1, grid=(1,), in_specs=in_specs + [ANY_SPEC] * len(extra),
                                               out_specs=out_specs),
        out_shape=out_shape, compiler_params=_params("arbitrary"),
    )(own.astype(jnp.int32).reshape(1), *srcs, *extra)
    return out[:n], (list(srcs) if pieces else out[n:])


def _exchange_start(srcs, zones, pieces, name, chips=False):
    n = len(srcs)

    def body(*refs):
        ins, zs = refs[:n], refs[n:2 * n]
        sems = refs[2 * n:4 * n]
        token = refs[-1]
        me, peers = _peers()
        x, y, c = _place()
        if chips and pieces:
            routes = [((x ^ fx, y ^ fy, c), j, j) for j, (fx, fy) in enumerate(CHIP_FLIPS) if j]
        elif chips:
            routes = [((x ^ fx, y ^ fy, c), None, me) for fx, fy in CHIP_FLIPS[1:]]
        else:
            routes = [(peer, _peer_index(peer) if pieces else None, me) for peer in peers]
        for peer, src_slot, dst_slot in routes:
            for a in range(n):
                pltpu.make_async_remote_copy(
                    src_ref=ins[a] if src_slot is None else ins[a].at[src_slot], dst_ref=zs[a].at[dst_slot],
                    send_sem=sems[2 * a], recv_sem=sems[2 * a + 1], device_id=peer, device_id_type=MESH_IDS).start()
        token[...] = jnp.zeros_like(token)

    hbm = lambda v: pltpu.with_memory_space_constraint(v, pltpu.HBM)
    out = pl.pallas_call(
        body, name=name,
        out_shape=tuple([pltpu.SemaphoreType.DMA(())] * (2 * n) + [pltpu.HBM(v.shape, v.dtype) for v in srcs]
                        + [pltpu.HBM(z.shape, z.dtype) for z in zones] + [_sds((8, LANES))]),
        in_specs=[HBM_SPEC] * (2 * n), out_specs=tuple([SEM_SPEC] * (2 * n) + [HBM_SPEC] * (2 * n) + [VMEM_SPEC]),
        input_output_aliases={i: 2 * n + i for i in range(2 * n)},
        compiler_params=pltpu.CompilerParams(has_side_effects=DATAFLOW),
    )(*[hbm(v) for v in srcs], *[hbm(z) for z in zones])
    return out[:2 * n], out[2 * n:3 * n], out[3 * n:4 * n], out[-1]


def _relay_start(zones, name):
    n = len(zones)

    def body(*refs):
        zs, sems, token = refs[:n], refs[n:3 * n], refs[-1]
        x, y, c = _place()
        for fx, fy in CHIP_FLIPS:
            slot = 4 * (x ^ fx) + 2 * (y ^ fy) + c
            for a in range(n):
                pltpu.make_async_remote_copy(
                    src_ref=zs[a].at[slot], dst_ref=zs[a].at[slot], send_sem=sems[2 * a], recv_sem=sems[2 * a + 1],
                    device_id=(x, y, 1 - c), device_id_type=MESH_IDS).start()
        token[...] = jnp.zeros_like(token)

    out = pl.pallas_call(
        body, name=name,
        out_shape=tuple([pltpu.SemaphoreType.DMA(())] * (2 * n) + [pltpu.HBM(z.shape, z.dtype) for z in zones]
                        + [_sds((8, LANES))]),
        in_specs=[HBM_SPEC] * n, out_specs=tuple([SEM_SPEC] * (2 * n) + [HBM_SPEC] * n + [VMEM_SPEC]),
        input_output_aliases={i: 2 * n + i for i in range(n)},
        compiler_params=pltpu.CompilerParams(has_side_effects=DATAFLOW),
    )(*[pltpu.with_memory_space_constraint(z, pltpu.HBM) for z in zones])
    return out[:2 * n], [], out[2 * n:3 * n], out[-1]


def _exchange_wait(sems, srcs, zones, after, name, chips=False, n_copies=None):
    n, n_src = len(zones), len(srcs)
    after = list(after) if isinstance(after, (list, tuple)) else [after]
    n_copies = n_copies or (len(CHIP_FLIPS) - 1 if chips else N_DEV - 1)

    def body(*refs):
        zs, sm = refs[n_src:n_src + n], refs[n_src + n:n_src + 3 * n]
        me, peers = _peers()
        for a in range(n):
            seven = zs[a].at[pl.ds(0, n_copies)]
            cp = pltpu.make_async_remote_copy(src_ref=seven, dst_ref=seven, send_sem=sm[2 * a], recv_sem=sm[2 * a + 1],
                                              device_id=peers[0], device_id_type=MESH_IDS)
            cp.wait_send()
            cp.wait_recv()

    out = pl.pallas_call(
        body, name=name, out_shape=tuple([pltpu.HBM(v.shape, v.dtype) for v in srcs] + [pltpu.HBM(z.shape, z.dtype) for z in zones]),
        in_specs=[HBM_SPEC] * (n_src + n) + [SEM_SPEC] * (2 * n) + [ANY_SPEC] * len(after),
        out_specs=tuple([HBM_SPEC] * (n_src + n)), input_output_aliases={i: i for i in range(n_src + n)},
        compiler_params=pltpu.CompilerParams(has_side_effects=DATAFLOW),
    )(*srcs, *zones, *sems, *after)
    return out[n_src:]


def _sum_adamw(zone, w, m, v, name):
    n_slots, r, c_ = zone.shape
    rb = next((b for b in (256, 128) if r % b == 0), r)

    def body(z_ref, w_ref, m_ref, v_ref, grad_ref, delta_ref, nm_ref, nv_ref):
        total = z_ref[0].astype(F32)
        for d in range(1, n_slots):
            total = total + z_ref[d].astype(F32)
        grad_ref[...] = total
        delta_ref[...], nm_ref[...], nv_ref[...] = _adamw(w_ref[...], total, m_ref[...], v_ref[...])

    blk = pl.BlockSpec((rb, c_), lambda i: (i, 0))
    return pl.pallas_call(
        body, name=name, grid=(r // rb,), in_specs=[pl.BlockSpec((n_slots, rb, c_), lambda i: (0, i, 0)), blk, blk, blk],
        out_specs=[blk] * 4, out_shape=[_sds((r, c_))] * 4, compiler_params=_params("parallel"),
    )(zone, w, m, v)


SMALL_NORMS = ("pre_mix_norm", "post_mix_norm", "pre_mlp_norm", "post_mlp_norm")
SMALL_ORDER = SMALL_NORMS + ("fox_out_norm", "gdn_out_norm", "fox_f_bias", "gdn_a_log", "gdn_dt_bias", "gdn_conv_w")
CONV_SLAB_ROWS, CONV_SLAB_LANES = 8, 256


def _small_pack(small):
    def body(n0, n1, n2, n3, fnw_ref, gnw_ref, loss_ref, vec_ref, out_ref):
        out_ref[...] = jnp.zeros_like(out_ref)
        for i, ref in enumerate((n0, n1, n2, n3)):
            out_ref[i:i + 1, :] = ref[...]
        out_ref[4:5, 0:LANES] = fnw_ref[...]
        out_ref[4:5, LANES:2 * LANES] = gnw_ref[...]
        out_ref[4:5, 2 * LANES:3 * LANES] = loss_ref[...]
        out_ref[5:8, 0:LANES] = vec_ref[0:3, :]

    return pl.pallas_call(body, name="small_pack", in_specs=[VMEM_SPEC] * 8, out_specs=VMEM_SPEC,
                          out_shape=_sds((8, D_MODEL)))(*small["norms"], small["fox_out_norm"], small["gdn_out_norm"],
                                                        small["loss"], small["vectors"])


def _conv_slabs(dconv):
    blocks = dconv.reshape(CONV_K, N_DEV, -1).transpose(1, 0, 2)
    blocks = jnp.pad(blocks, ((0, 0), (0, CONV_SLAB_ROWS - CONV_K), (0, CONV_SLAB_LANES - blocks.shape[2])))
    return blocks.reshape(N_DEV * CONV_SLAB_ROWS, CONV_SLAB_LANES)


def _small_update(zone, conv_zone, w, m, v):
    n = len(SMALL_ORDER)
    n_conv = w["gdn_conv_w"].shape[1]

    def body(me_ref, z_ref, zc_ref, *refs):
        params, loss_ref, outs, (tot, totc) = refs[:3 * n], refs[3 * n], refs[3 * n + 1:7 * n + 1], refs[-2:]
        total, total_c = z_ref[0], zc_ref[0]
        for d in range(1, N_DEV):
            total, total_c = total + z_ref[d], total_c + zc_ref[d]
        tot[...] = total
        totc[...] = total_c
        loss_ref[...] = tot[4, 2 * LANES:2 * LANES + 1]
        mine = totc[pl.ds(pl.multiple_of(me_ref[0] * CONV_SLAB_ROWS, CONV_SLAB_ROWS), CONV_SLAB_ROWS), :]
        g = dict(zip(SMALL_NORMS, (tot[0], tot[1], tot[2], tot[3])))
        g.update(fox_out_norm=tot[4, 0:FOX_HEAD_DIM], gdn_out_norm=tot[4, LANES:LANES + GDN_HEAD_DIM],
                 fox_f_bias=tot[5, SM_FF:SM_FF + N_FOX_HEADS], gdn_a_log=tot[6, SM_GA:SM_GA + N_GDN_HEADS],
                 gdn_dt_bias=tot[7, SM_GA:SM_GA + N_GDN_HEADS], gdn_conv_w=mine[0:CONV_K, 0:n_conv])
        for i, name in enumerate(SMALL_ORDER):
            w_ref, m_ref, v_ref = params[3 * i:3 * i + 3]
            outs[4 * i][...] = g[name]
            outs[4 * i + 1][...], outs[4 * i + 2][...], outs[4 * i + 3][...] = _adamw(w_ref[...], g[name], m_ref[...],
                                                                                     v_ref[...])

    x, y, c = _place()
    operands = [a[name] for name in SMALL_ORDER for a in (w, m, v)]
    out = pl.pallas_call(
        body, name="small_update",
        in_specs=[pl.BlockSpec(memory_space=pltpu.SMEM)] + [VMEM_SPEC] * (2 + 3 * n), out_specs=[VMEM_SPEC] * (1 + 4 * n),
        out_shape=[_sds((1,))] + [_sds(w[name].shape) for name in SMALL_ORDER for _ in range(4)],
        scratch_shapes=[pltpu.VMEM(zone.shape[1:], F32), pltpu.VMEM(conv_zone.shape[1:], F32)],
    )((4 * x + 2 * y + c).astype(jnp.int32).reshape(1), zone, conv_zone, *operands)
    return out[0][0], {name: out[1 + 4 * i:5 + 4 * i] for i, name in enumerate(SMALL_ORDER)}


NATIVE_ROWS = ((0, 1536), (1544, 3080), (3088, 3600), (1536, 1544), (3080, 3088))


def _to_aligned_rows(wt_native):
    pad = jnp.zeros((PROJ_W - D_PROJ, wt_native.shape[1]), wt_native.dtype)
    return jnp.concatenate([wt_native[lo:hi] for lo, hi in NATIVE_ROWS] + [pad])


def _from_aligned_rows(gt_al):
    return jnp.concatenate([gt_al[0:1536], gt_al[3584:3592], gt_al[1536:3072], gt_al[3592:3600], gt_al[3072:3584]])


def _cols_from_pieces(p):
    return p.transpose(1, 0, 2).reshape(p.shape[1], -1)


WEIGHT_ORDER = ("pre_mix_norm", "w_in", "fox_f_bias", "fox_out_norm", "gdn_conv_w", "gdn_a_log", "gdn_dt_bias",
                "gdn_out_norm", "w_out", "post_mix_norm", "pre_mlp_norm", "w_up", "w_down", "post_mlp_norm")


def kernel(x, pre_mix_norm, w_in, fox_f_bias, fox_out_norm, gdn_conv_w, gdn_a_log, gdn_dt_bias, gdn_out_norm, w_out, post_mix_norm, pre_mlp_norm, w_up, w_down, post_mlp_norm, loss_target, m_pre_mix_norm, m_w_in, m_fox_f_bias, m_fox_out_norm, m_gdn_conv_w, m_gdn_a_log, m_gdn_dt_bias, m_gdn_out_norm, m_w_out, m_post_mix_norm, m_pre_mlp_norm, m_w_up, m_w_down, m_post_mlp_norm, v_pre_mix_norm, v_w_in, v_fox_f_bias, v_fox_out_norm, v_gdn_conv_w, v_gdn_a_log, v_gdn_dt_bias, v_gdn_out_norm, v_w_out, v_post_mix_norm, v_pre_mlp_norm, v_w_up, v_w_down, v_post_mlp_norm):
    w = dict(pre_mix_norm=pre_mix_norm, w_in=w_in, fox_f_bias=fox_f_bias, fox_out_norm=fox_out_norm,
             gdn_conv_w=gdn_conv_w, gdn_a_log=gdn_a_log, gdn_dt_bias=gdn_dt_bias, gdn_out_norm=gdn_out_norm, w_out=w_out,
             post_mix_norm=post_mix_norm, pre_mlp_norm=pre_mlp_norm, w_up=w_up, w_down=w_down, post_mlp_norm=post_mlp_norm)
    mom = dict(pre_mix_norm=m_pre_mix_norm, w_in=m_w_in, fox_f_bias=m_fox_f_bias, fox_out_norm=m_fox_out_norm,
               gdn_conv_w=m_gdn_conv_w, gdn_a_log=m_gdn_a_log, gdn_dt_bias=m_gdn_dt_bias, gdn_out_norm=m_gdn_out_norm,
               w_out=m_w_out, post_mix_norm=m_post_mix_norm, pre_mlp_norm=m_pre_mlp_norm, w_up=m_w_up, w_down=m_w_down,
               post_mlp_norm=m_post_mlp_norm)
    var = dict(pre_mix_norm=v_pre_mix_norm, w_in=v_w_in, fox_f_bias=v_fox_f_bias, fox_out_norm=v_fox_out_norm,
               gdn_conv_w=v_gdn_conv_w, gdn_a_log=v_gdn_a_log, gdn_dt_bias=v_gdn_dt_bias, gdn_out_norm=v_gdn_out_norm,
               w_out=v_w_out, post_mix_norm=v_post_mix_norm, pre_mlp_norm=v_pre_mlp_norm, w_up=v_w_up, w_down=v_w_down,
               post_mlp_norm=v_post_mlp_norm)

    win_g, conv_g = _all_gather([w_in.T.astype(BF), gdn_conv_w])
    wt_al = _to_aligned_rows(win_g.reshape(D_PROJ, D_MODEL))
    convw = _cols_from_pieces(conv_g)
    gathers, after = {}, win_g
    for name, shards in (("w_out", [w_out]), ("mlp", [w_up, w_down])):
        zones, shards = _zones_with_own(shards, False, "gather_" + name + "_own", after=after, dtype=BF)
        gathers[name] = _exchange_start(shards, zones, False, "gather_" + name + "_start", chips=name == "mlp")
        after = gathers[name][3]

    def late_weights(name, after):
        if name == "mlp_relay":
            sems, shards, zones, _ = gathers["mlp"]
            zones = _exchange_wait(sems, shards, zones, after, "gather_mlp_wait", chips=True)
            gathers["mlp"] = _relay_start(zones, "gather_mlp_relay")
            return gathers["mlp"][3]
        sems, shards, zones, _ = gathers[name]
        got = _exchange_wait(sems, shards, zones, after, "gather_" + name + "_done",
                             n_copies=len(CHIP_FLIPS) if name == "mlp" else None)
        if name == "w_out":
            return got[0].reshape(D_MODEL, D_MODEL)
        return got[0], got[1].reshape(D_FF, D_MODEL)

    scatters = {}

    def on_grads(name, g):
        chips = name == "w_in"
        if name == "w_in":
            g = _pair_reduce(_from_aligned_rows(g).reshape(N_DEV, D_PROJ // N_DEV, D_MODEL), "pair_reduce_w_in")
        srcs = list(g) if name == "mlp" else [g]
        zones, _ = _zones_with_own(srcs, True, "scatter_" + name + "_own", chips=chips)
        scatters[name] = _exchange_start(srcs, zones, True, "scatter_" + name + "_start", chips=chips)
        return scatters[name][3]

    grad_x, small = _local_step(
        x[0], loss_target[0], wt_al, late_weights, on_grads, convw, pre_mix_norm + after[0, 0],
        fox_f_bias, fox_out_norm, gdn_a_log, gdn_dt_bias, gdn_out_norm, post_mix_norm, pre_mlp_norm, post_mlp_norm)
    slabs = [_small_pack(small), _conv_slabs(jnp.concatenate(small["conv"], axis=1))]
    zones, slabs = _zones_with_own(slabs, False, "small_own")
    scatters["small"] = _exchange_start(slabs, zones, False, "small_start")

    grads, delta, new_m, new_v = {}, {}, {}, {}
    after = scatters["small"][3]
    for name, members in (("mlp", ("w_up", "w_down")), ("w_out", ("w_out",)), ("small", ()), ("w_in", ("w_in",))):
        sems, srcs, zones, _ = scatters[name]
        zones = _exchange_wait(sems, srcs, zones, after, "scatter_" + name + "_wait", chips=name == "w_in")
        if name == "small":
            loss, updated = _small_update(zones[0], zones[1], w, mom, var)
            for n, res in updated.items():
                grads[n], delta[n], new_m[n], new_v[n] = res
            after = grads["pre_mix_norm"]
        for n, zone in zip(members, zones):
            if n == "w_in":
                res = _sum_adamw(zone, w[n].T, mom[n].T, var[n].T, "adamw_" + n)
                grads[n], delta[n], new_m[n], new_v[n] = [r.T for r in res]
            else:
                grads[n], delta[n], new_m[n], new_v[n] = _sum_adamw(zone, w[n], mom[n], var[n], "adamw_" + n)
        if members:
            after = [grads[n] for n in members]

    return (loss, grad_x[None], *[grads[n] for n in WEIGHT_ORDER], *[delta[n] for n in WEIGHT_ORDER],
            *[new_m[n] for n in WEIGHT_ORDER], *[new_v[n] for n in WEIGHT_ORDER])
```

```python
import jax
import jax.numpy as jnp
from jax import lax
from jax.experimental import pallas as pl
from jax.experimental.pallas import tpu as pltpu

F32 = jnp.float32
BF = jnp.bfloat16

D_MODEL = 1024
N_FOX_HEADS, FOX_HEAD_DIM = 8, 64
N_GDN_HEADS, GDN_HEAD_DIM = 4, 128
D_FOX = N_FOX_HEADS * FOX_HEAD_DIM
D_GDN = N_GDN_HEADS * GDN_HEAD_DIM
CHUNK = 64
CONV_K = 4
D_FF = 4 * D_MODEL
EPS = 1e-6
D_PROJ = 3600
N_DEV = 8

PROJ_W = 3712
COL_FOX, COL_GDN, COL_GZ, COL_SMALL = 0, 1536, 3072, 3584
LANES = 128
SM_FF, SM_GB, SM_GA = 0, 8, 12

ADAM_LR, ADAM_B1, ADAM_B2, ADAM_EPS, ADAM_WD, ADAM_STEP = 0.001, 0.9, 0.999, 1e-08, 0.01, 10

TOKEN_BLOCK = 256
MATMUL_BLOCK = 512
FOX_SCALE = FOX_HEAD_DIM ** -0.5
GDN_QSCALE = GDN_HEAD_DIM ** -0.5
NEG_BIG = -1e30
VMEM_LIMIT = 56 * 1024 * 1024

VMEM_SPEC = pl.BlockSpec(memory_space=pltpu.VMEM)


def _sds(shape, dtype=F32):
    return jax.ShapeDtypeStruct(shape, dtype)


def _params(*sem):
    return pltpu.CompilerParams(dimension_semantics=sem if sem else None, vmem_limit_bytes=VMEM_LIMIT)


def _mm(a, b):
    return jnp.dot(a.astype(BF), b.astype(BF), preferred_element_type=F32)


def _mm_nt(a, b):
    return lax.dot_general(a.astype(BF), b.astype(BF), (((1,), (1,)), ((), ())), preferred_element_type=F32)


def _mm_tn(a, b):
    return lax.dot_general(a.astype(BF), b.astype(BF), (((0,), (0,)), ((), ())), preferred_element_type=F32)


def _sigmoid(x):
    return 1.0 / (1.0 + jnp.exp(-x))


def _softplus(x):
    return jnp.maximum(x, 0.0) + jnp.log1p(jnp.exp(-jnp.abs(x)))


def _iota(shape, dim):
    return lax.broadcasted_iota(jnp.int32, shape, dim)


def _shift_down(x, s, row):
    return jnp.where(row >= s, pltpu.roll(x, s, 0), 0.0)


def _shift_up(x, s, row):
    n = x.shape[0]
    return jnp.where(row < n - s, pltpu.roll(x, n - s, 0), 0.0)


def _norm_proj(x, nw, wt_al):
    t = x.shape[0]

    def body(x_ref, nw_ref, w_ref, proj_ref, h_ref):
        xv = x_ref[...]
        r = lax.rsqrt(jnp.mean(xv * xv, axis=-1, keepdims=True) + EPS)
        h = (xv * r * nw_ref[...]).astype(BF)
        h_ref[...] = h
        proj_ref[...] = lax.dot_general(h, w_ref[...], (((1,), (1,)), ((), ())), preferred_element_type=F32)

    tm = min(MATMUL_BLOCK, t)
    return pl.pallas_call(
        body, name="norm_proj", grid=(t // tm,),
        in_specs=[pl.BlockSpec((tm, D_MODEL), lambda i: (i, 0)), pl.BlockSpec((1, D_MODEL), lambda i: (0, 0)),
                  pl.BlockSpec((PROJ_W, D_MODEL), lambda i: (0, 0))],
        out_specs=[pl.BlockSpec((tm, PROJ_W), lambda i: (i, 0)), pl.BlockSpec((tm, D_MODEL), lambda i: (i, 0))],
        out_shape=[_sds((t, PROJ_W)), _sds((t, D_MODEL), BF)],
        compiler_params=_params("parallel"),
    )(x, nw, wt_al)


def _lane_column(x, lane):
    return jnp.sum(jnp.where(_iota((1, LANES), 1) == lane, x, 0.0), axis=-1, keepdims=True)


def _small_prep(proj, fb, al, dtb):
    t = proj.shape[0]

    def body(sm_ref, fb_ref, al_ref, dtb_ref, cumt_ref, beta_ref, g_ref):
        s = sm_ref[...]
        z = s + fb_ref[...]
        cum = jnp.minimum(z, 0.0) - jnp.log1p(jnp.exp(-jnp.abs(z)))
        row = _iota((t, LANES), 0)
        step = 1
        while step < t:
            cum = cum + _shift_down(cum, step, row)
            step *= 2
        cumt_ref[...] = cum.T
        beta_ref[...] = _sigmoid(s)
        g_ref[...] = -jnp.exp(al_ref[...]) * _softplus(s + dtb_ref[...])

    vec = pl.BlockSpec((1, LANES), lambda i: (0, 0))
    tok = pl.BlockSpec((t, LANES), lambda i: (0, 0))
    return pl.pallas_call(
        body, name="small_prep", grid=(1,),
        in_specs=[pl.BlockSpec((t, LANES), lambda i: (0, COL_SMALL // LANES)), vec, vec, vec],
        out_specs=[pl.BlockSpec((LANES, t), lambda i: (0, 0)), tok, tok],
        out_shape=[_sds((LANES, t)), _sds((t, LANES)), _sds((t, LANES))],
        compiler_params=_params("arbitrary"),
    )(proj, fb, al, dtb)


def _fox_stack(x, first):
    return jnp.concatenate([jnp.where(first, x, 0.0), jnp.where(first, 0.0, x)], axis=0).astype(BF)


def _fox_unstack(y, first):
    n = y.shape[0] // 2
    return jnp.where(first, y[:n], y[n:])


def _fox_logits(q2_i, kb, cumt_ref, pair, i, tq):
    klen = (i + 1) * tq
    s = lax.dot_general(q2_i, kb[:klen], (((1,), (1,)), ((), ())), preferred_element_type=F32)
    upper = _iota((2 * tq, 1), 0) < tq
    s = s - jnp.where(upper, cumt_ref[pl.ds(2 * pair, 1), 0:klen], cumt_ref[pl.ds(2 * pair + 1, 1), 0:klen])
    causal = _iota((2 * tq, tq), 1) <= _iota((2 * tq, tq), 0) % tq
    parts = [(s[:, :klen - tq], 0, klen - tq)] if i else []
    return parts + [(jnp.where(causal, s[:, klen - tq:], NEG_BIG), klen - tq, klen)]


def _fox_fwd(proj, cumt, fnw):
    t = proj.shape[0]
    tq = min(TOKEN_BLOCK, t // 2)
    nq = t // tq

    def body(q_ref, k_ref, v_ref, cumt_ref, fnw_ref, o_ref, lse_ref, fn_ref):
        j = pl.program_id(0)
        first = _iota((1, LANES), 1) < FOX_HEAD_DIM
        kb = k_ref[...].astype(BF)
        vb = v_ref[...].astype(BF)
        for i in range(nq):
            rows = slice(i * tq, (i + 1) * tq)
            q2 = _fox_stack(q_ref[rows, :] * FOX_SCALE, first)
            parts = _fox_logits(q2, kb, cumt_ref, j, i, tq)
            m = jnp.max(parts[-1][0], axis=-1, keepdims=True)
            if i:
                m = jnp.maximum(m, jnp.max(parts[0][0], axis=-1, keepdims=True))
            l = jnp.zeros((2 * tq, 1), F32)
            o = jnp.zeros((2 * tq, LANES), F32)
            for s, lo, hi in parts:
                p = jnp.exp(s - m)
                l = l + jnp.sum(p, axis=-1, keepdims=True)
                o = o + jnp.dot(p.astype(BF), vb[lo:hi], preferred_element_type=F32)
            o_acc = _fox_unstack(o / l, first)
            lse_acc = _fox_unstack(jnp.broadcast_to(m + jnp.log(l), (2 * tq, LANES)), first)
            o_ref[rows, :] = o_acc
            lse_ref[rows, :] = lse_acc
            o2 = o_acc * o_acc
            s0 = jnp.sum(jnp.where(first, o2, 0.0), axis=-1, keepdims=True)
            s1 = jnp.sum(jnp.where(first, 0.0, o2), axis=-1, keepdims=True)
            r = lax.rsqrt(jnp.where(first, s0, s1) * (1.0 / FOX_HEAD_DIM) + EPS)
            fn_ref[rows, :] = (o_acc * r * fnw_ref[...]).astype(BF)

    blk = lambda off: pl.BlockSpec((t, LANES), lambda j: (0, off + j))
    return pl.pallas_call(
        body, name="fox_fwd", grid=(N_FOX_HEADS // 2,),
        in_specs=[blk(0), blk(4), blk(8), pl.BlockSpec((LANES, t), lambda j: (0, 0)),
                  pl.BlockSpec((1, LANES), lambda j: (0, 0))],
        out_specs=[blk(0), blk(0), blk(0)],
        out_shape=[_sds((t, D_FOX)), _sds((t, D_FOX)), _sds((t, D_FOX), BF)],
        compiler_params=_params("parallel"),
    )(proj, proj, proj, cumt, fnw)


def _fox_bwd(proj, cumt, lse, o, do):
    t = proj.shape[0]
    tq = min(TOKEN_BLOCK, t // 2)
    nq = t // tq

    def body(q_ref, k_ref, v_ref, cumt_ref, lse_ref, o_ref, do_ref,
             dq_ref, dk_ref, dv_ref, dcq_ref, dckt_ref, dk_s, dv_s):
        j = pl.program_id(0)

        @pl.when(j == 0)
        def _():
            dcq_ref[...] = jnp.zeros_like(dcq_ref)
            dckt_ref[...] = jnp.zeros_like(dckt_ref)

        lane = _iota((1, LANES), 1)

        first = _iota((1, LANES), 1) < FOX_HEAD_DIM
        kb = k_ref[...].astype(BF)
        vb = v_ref[...].astype(BF)
        dk_s[...] = jnp.zeros_like(dk_s)
        dv_s[...] = jnp.zeros_like(dv_s)
        for i in range(nq):
            rows = slice(i * tq, (i + 1) * tq)
            do_i = do_ref[rows, :]
            prod = do_i * o_ref[rows, :]
            lse_i = lse_ref[rows, :]
            q2 = _fox_stack(q_ref[rows, :] * FOX_SCALE, first)
            do2 = _fox_stack(do_i, first)
            delta = jnp.concatenate([jnp.sum(jnp.where(first, prod, 0.0), axis=-1, keepdims=True),
                                     jnp.sum(jnp.where(first, 0.0, prod), axis=-1, keepdims=True)], axis=0)
            lse2 = jnp.concatenate([lse_i[:, 0:1], lse_i[:, FOX_HEAD_DIM:FOX_HEAD_DIM + 1]], axis=0)
            dq2 = jnp.zeros((2 * tq, LANES), F32)
            dcq2 = jnp.zeros((2 * tq, 1), F32)
            for s, lo, hi in _fox_logits(q2, kb, cumt_ref, j, i, tq):
                p = jnp.exp(s - lse2)
                ds = p * (_mm_nt(do2, vb[lo:hi]) - delta)
                dsb = ds.astype(BF)
                dq2 = dq2 + jnp.dot(dsb, kb[lo:hi], preferred_element_type=F32)
                dk_s[lo:hi, :] += _mm_tn(dsb, q2)
                dv_s[lo:hi, :] += _mm_tn(p, do2)
                dcq2 = dcq2 + jnp.sum(ds, axis=-1, keepdims=True)
                dckt_ref[pl.ds(2 * j, 1), lo:hi] += jnp.sum(ds[:tq], axis=0, keepdims=True)
                dckt_ref[pl.ds(2 * j + 1, 1), lo:hi] += jnp.sum(ds[tq:], axis=0, keepdims=True)
            dq_ref[rows, :] = (_fox_unstack(dq2, first) * FOX_SCALE).astype(BF)
            dcq_ref[rows, :] += jnp.where(lane == 2 * j, dcq2[:tq], jnp.where(lane == 2 * j + 1, dcq2[tq:], 0.0))
        dk_ref[...] = dk_s[...].astype(BF)
        dv_ref[...] = dv_s[...].astype(BF)

    blk = lambda off: pl.BlockSpec((t, LANES), lambda j: (0, off + j))
    rows128 = pl.BlockSpec((LANES, t), lambda j: (0, 0))
    return pl.pallas_call(
        body, name="fox_bwd", grid=(N_FOX_HEADS // 2,),
        in_specs=[blk(0), blk(4), blk(8), rows128, blk(0), blk(0), blk(0)],
        out_specs=[blk(0), blk(0), blk(0), pl.BlockSpec((t, LANES), lambda j: (0, 0)), rows128],
        out_shape=[_sds((t, D_FOX), BF)] * 3 + [_sds((t, LANES)), _sds((LANES, t))],
        scratch_shapes=[pltpu.VMEM((t, LANES), F32), pltpu.VMEM((t, LANES), F32)],
        compiler_params=_params("arbitrary"),
    )(proj, proj, proj, cumt, lse, o, do)


def _conv(x, w, row):
    return (w[3:4, :] * x + w[2:3, :] * _shift_down(x, 1, row) + w[1:2, :] * _shift_down(x, 2, row)
            + w[0:1, :] * _shift_down(x, 3, row))


def _chunk_decay(gc_c):
    gi = gc_c[:, 0:CHUNK]
    gj = gc_c.T[0:CHUNK, :]
    ri = _iota((CHUNK, CHUNK), 0)
    cj = _iota((CHUNK, CHUNK), 1)
    return jnp.where(ri >= cj, jnp.exp(jnp.minimum(gi - gj, 0.0)), 0.0), ri > cj


def _gdn_specs(t):
    col = lambda off: pl.BlockSpec((t, LANES), lambda h: (0, off + h))
    cw = lambda off: pl.BlockSpec((CONV_K, LANES), lambda h: (0, off + h))
    mat = pl.BlockSpec((1, t // CHUNK, CHUNK, CHUNK), lambda h: (h, 0, 0, 0))
    return col, cw, mat


def _gdn_prep(proj, convw, beta, g):
    t = proj.shape[0]
    nch = t // CHUNK

    def body(xq_ref, xk_ref, xv_ref, wq_ref, wk_ref, wv_ref, beta_ref, g_ref,
             qn_ref, kn_ref, cv_ref, gc_ref, be_ref, m_ref, a_ref):
        row = _iota((t, LANES), 0)
        hd = pl.program_id(0)
        be_ref[...] = jnp.broadcast_to(_lane_column(beta_ref[...], SM_GB + hd), (t, LANES))

        def act(x_ref, w_ref):
            y = _conv(x_ref[...], w_ref[...], row)
            return y * _sigmoid(y)

        cq = act(xq_ref, wq_ref)
        ck = act(xk_ref, wk_ref)
        cv_ref[...] = act(xv_ref, wv_ref)
        qn_ref[...] = cq * lax.rsqrt(jnp.sum(cq * cq, axis=-1, keepdims=True) + EPS) * GDN_QSCALE
        kn_ref[...] = ck * lax.rsqrt(jnp.sum(ck * ck, axis=-1, keepdims=True) + EPS)
        gc = jnp.broadcast_to(_lane_column(g_ref[...], SM_GA + hd), (t, LANES))
        pos = row % CHUNK
        step = 1
        while step < CHUNK:
            gc = gc + jnp.where(pos >= step, pltpu.roll(gc, step, 0), 0.0)
            step *= 2
        gc_ref[...] = gc

        group = 4 if nch % 4 == 0 else 1

        def chunks(gi, carry):
            ns = [gi * group + c for c in range(group)]
            sls = [pl.ds(pl.multiple_of(n * CHUNK, CHUNK), CHUNK) for n in ns]
            ks = [kn_ref[sl, :] for sl in sls]
            kk = [_mm_nt(k_c * be_ref[sl, :], k_c) for k_c, sl in zip(ks, sls)]
            qk = [_mm_nt(qn_ref[sl, :], k_c) for k_c, sl in zip(ks, sls)]
            for c, n in enumerate(ns):
                decay, strict = _chunk_decay(gc_ref[sls[c], :])
                m_ref[0, n] = jnp.where(strict, kk[c] * decay, 0.0)
                a_ref[0, n] = qk[c] * decay
            return carry

        lax.fori_loop(0, nch // group, chunks, 0)

    col, cw, mat = _gdn_specs(t)
    return pl.pallas_call(
        body, name="gdn_prep", grid=(N_GDN_HEADS,),
        in_specs=[col(12), col(16), col(20), cw(0), cw(4), cw(8)] + [pl.BlockSpec((t, LANES), lambda h: (0, 0))] * 2,
        out_specs=[col(0), col(0), col(0), col(0), col(0), mat, mat],
        out_shape=[_sds((t, D_GDN))] * 5 + [_sds((N_GDN_HEADS, nch, CHUNK, CHUNK))] * 2,
        compiler_params=_params("parallel"),
    )(proj, proj, proj, convw, convw, convw, beta, g)


def _tri_inverse(m3):
    assert m3.shape == (LANES, CHUNK, CHUNK)

    def body(m_ref, t_ref, ms, ts):
        for i in range(CHUNK):
            ms[i * CHUNK:(i + 1) * CHUNK, :] = m_ref[:, i, :].T
        cidx = _iota((CHUNK, LANES), 0)

        def outer(i, carry):
            def inner(jj, acc):
                mrow = ms[pl.ds(i * CHUNK + jj, 1), :]
                return acc - mrow * ts[pl.ds(pl.multiple_of(jj * CHUNK, CHUNK), CHUNK), :]

            acc = lax.fori_loop(0, i, inner, jnp.where(cidx == i, 1.0, 0.0).astype(F32))
            ts[pl.ds(pl.multiple_of(i * CHUNK, CHUNK), CHUNK), :] = acc
            return carry

        lax.fori_loop(0, CHUNK, outer, 0)
        for i in range(CHUNK):
            t_ref[:, i, :] = ts[i * CHUNK:(i + 1) * CHUNK, :].T

    return pl.pallas_call(
        body, name="tri_inverse", in_specs=[VMEM_SPEC], out_specs=VMEM_SPEC,
        out_shape=_sds((LANES, CHUNK, CHUNK)),
        scratch_shapes=[pltpu.VMEM((CHUNK * CHUNK, LANES), F32), pltpu.VMEM((CHUNK * CHUNK, LANES), F32)],
        compiler_params=_params(),
    )(m3)


def _gdn_chunk_terms(q, k, v, b, gcc):
    eg = jnp.exp(gcc)
    last = gcc[CHUNK - 1:CHUNK, :]
    egl = jnp.exp(last - gcc)
    gl = jnp.exp(last)
    kb = k * b
    return eg, egl, gl, kb, v * b, kb * eg, q * eg, k * egl


GDN_BLOCK_CHUNKS = 4


def _gdn_block_specs(t, reverse):
    cb = GDN_BLOCK_CHUNKS
    nb = t // (cb * CHUNK)
    idx = (lambda i: nb - 1 - i) if reverse else (lambda i: i)
    tok = pl.BlockSpec((cb * CHUNK, D_GDN), lambda i: (idx(i), 0))
    mat = pl.BlockSpec((N_GDN_HEADS, cb, CHUNK, CHUNK), lambda i: (0, idx(i), 0, 0))
    state = pl.BlockSpec((N_GDN_HEADS, cb, GDN_HEAD_DIM, GDN_HEAD_DIM), lambda i: (0, idx(i), 0, 0))
    return nb, tok, mat, state


def _gdn_scan(qn, kn, cv, be, gc, tinv, amat):
    t = qn.shape[0]
    nch = t // CHUNK

    def body(q_ref, k_ref, v_ref, b_ref, gc_ref, t_ref, a_ref, o_ref, sall_ref, vn_ref, s_scr):
        @pl.when(pl.program_id(0) == 0)
        def _():
            s_scr[...] = jnp.zeros_like(s_scr)

        heads = range(N_GDN_HEADS)
        cols = [slice(hd * LANES, (hd + 1) * LANES) for hd in heads]
        s = [s_scr[hd] for hd in heads]
        for cc in range(GDN_BLOCK_CHUNKS):
            rs = slice(cc * CHUNK, (cc + 1) * CHUNK)
            terms = [_gdn_chunk_terms(q_ref[rs, cs], k_ref[rs, cs], v_ref[rs, cs], b_ref[rs, cs], gc_ref[rs, cs])
                     for cs in cols]
            for hd in heads:
                sall_ref[hd, cc] = s[hd]
            uw = [_mm(t_ref[hd, cc], jnp.concatenate([terms[hd][4], terms[hd][5]], axis=1)) for hd in heads]
            ws_qs = [_mm(jnp.concatenate([uw[hd][:, LANES:], terms[hd][6]], axis=0), s[hd]) for hd in heads]
            vn = [uw[hd][:, :LANES] - ws_qs[hd][:CHUNK] for hd in heads]
            a_vn = [_mm(a_ref[hd, cc], vn[hd]) for hd in heads]
            kd_vn = [_mm_tn(terms[hd][7], vn[hd]) for hd in heads]
            for hd in heads:
                vn_ref[rs, cols[hd]] = vn[hd]
                o_ref[rs, cols[hd]] = ws_qs[hd][CHUNK:] + a_vn[hd]
                s[hd] = s[hd] * terms[hd][2] + kd_vn[hd]
        for hd in heads:
            s_scr[hd] = s[hd]

    nb, tok, mat, state = _gdn_block_specs(t, False)
    return pl.pallas_call(
        body, name="gdn_scan", grid=(nb,),
        in_specs=[tok] * 5 + [mat, mat], out_specs=[tok, state, tok],
        out_shape=[_sds((t, D_GDN)), _sds((N_GDN_HEADS, nch, GDN_HEAD_DIM, GDN_HEAD_DIM)), _sds((t, D_GDN))],
        scratch_shapes=[pltpu.VMEM((N_GDN_HEADS, GDN_HEAD_DIM, GDN_HEAD_DIM), F32)],
        compiler_params=_params("arbitrary"),
    )(qn, kn, cv, be, gc, tinv, amat)


def _gdn_bwd(qn, kn, cv, be, gc, tinv, amat, s_all, vn_all, do):
    t = qn.shape[0]

    def body(q_ref, k_ref, v_ref, b_ref, gc_ref, t_ref, a_ref, sall_ref, vn_ref, do_ref,
             dq_ref, dk_ref, dv_ref, db_ref, dg_ref, ds_scr):
        @pl.when(pl.program_id(0) == 0)
        def _():
            ds_scr[...] = jnp.zeros_like(ds_scr)

        lastrow = _iota((CHUNK, LANES), 0) == CHUNK - 1
        heads = range(N_GDN_HEADS)
        cols = [slice(hd * LANES, (hd + 1) * LANES) for hd in heads]
        each = lambda fn: [fn(hd) for hd in heads]
        rows_cat = lambda x, y: jnp.concatenate([x, y], axis=0)
        lane_cat = lambda x, y: jnp.concatenate([x, y], axis=1)
        dsp = each(lambda hd: ds_scr[hd])
        for cc in reversed(range(GDN_BLOCK_CHUNKS)):
            rs = slice(cc * CHUNK, (cc + 1) * CHUNK)
            q = each(lambda hd: q_ref[rs, cols[hd]])
            k = each(lambda hd: k_ref[rs, cols[hd]])
            v = each(lambda hd: v_ref[rs, cols[hd]])
            b = each(lambda hd: b_ref[rs, cols[hd]])
            gcc = each(lambda hd: gc_ref[rs, cols[hd]])
            do_c = each(lambda hd: do_ref[rs, cols[hd]])
            vn = each(lambda hd: vn_ref[rs, cols[hd]])
            tn = each(lambda hd: t_ref[hd, cc])
            st = each(lambda hd: sall_ref[hd, cc])
            terms = each(lambda hd: _gdn_chunk_terms(q[hd], k[hd], v[hd], b[hd], gcc[hd]))
            eg, egl, gl, kb, vb, kbg, qd, kd = [[terms[hd][i] for hd in heads] for i in range(8)]
            w = each(lambda hd: _mm(tn[hd], kbg[hd]))
            a_do = each(lambda hd: _mm_tn(a_ref[hd, cc], do_c[hd]))
            kd_ds = each(lambda hd: _mm(kd[hd], dsp[hd]))
            da = each(lambda hd: _mm_nt(do_c[hd], vn[hd]))
            dkd = each(lambda hd: _mm_nt(vn[hd], dsp[hd]))
            by_k = each(lambda hd: _mm_nt(rows_cat(kb[hd], q[hd]), k[hd]))
            dgl = each(lambda hd: jnp.sum(jnp.sum(dsp[hd] * st[hd], axis=-1, keepdims=True), axis=0, keepdims=True))
            dvn = each(lambda hd: a_do[hd] + kd_ds[hd])
            do_dvn = each(lambda hd: rows_cat(do_c[hd], dvn[hd]))
            by_s = each(lambda hd: _mm_nt(do_dvn[hd], st[hd]))
            dqd = each(lambda hd: by_s[hd][:CHUNK])
            dvn_dw = each(lambda hd: lane_cat(dvn[hd], -by_s[hd][CHUNK:]))
            dsp = each(lambda hd: _mm_tn(rows_cat(qd[hd], -w[hd]), do_dvn[hd]) + gl[hd] * dsp[hd])
            dt = each(lambda hd: _mm_nt(dvn_dw[hd], lane_cat(vb[hd], kbg[hd])))
            by_t = each(lambda hd: _mm_tn(tn[hd], dvn_dw[hd]))
            tt_dt = each(lambda hd: _mm_tn(tn[hd], dt[hd]))
            dm_raw = each(lambda hd: _mm_nt(tt_dt[hd], tn[hd]))
            masks = each(lambda hd: _chunk_decay(gcc[hd]))
            dkk = each(lambda hd: jnp.where(masks[hd][1], -dm_raw[hd], 0.0) * masks[hd][0])
            dqk = each(lambda hd: da[hd] * masks[hd][0])
            dqk_dkk = each(lambda hd: rows_cat(dqk[hd], dkk[hd]))
            on_k = each(lambda hd: _mm(dqk_dkk[hd], k[hd]))
            dk_mm = each(lambda hd: _mm_tn(dqk_dkk[hd], rows_cat(q[hd], kb[hd])))
            for hd in heads:
                cs = cols[hd]
                dvb, dkbg = by_t[hd][:, :LANES], by_t[hd][:, LANES:]
                gmat = dkk[hd] * by_k[hd][:CHUNK] + dqk[hd] * by_k[hd][CHUNK:]
                dq_ref[rs, cs] = dqd[hd] * eg[hd] + on_k[hd][:CHUNK]
                dkb = on_k[hd][CHUNK:] + dkbg * eg[hd]
                dk_ref[rs, cs] = dkd[hd] * egl[hd] + dk_mm[hd] + dkb * b[hd]
                db = jnp.sum(dkb * k[hd], axis=-1, keepdims=True) + jnp.sum(dvb * v[hd], axis=-1, keepdims=True)
                db_ref[rs, cs] = jnp.broadcast_to(db, (CHUNK, LANES))
                dv_ref[rs, cs] = dvb * b[hd]
                dkd_kd = jnp.sum(dkd[hd] * kd[hd], axis=-1, keepdims=True)
                col_sums = jnp.sum(lane_cat(gmat, jnp.zeros_like(gmat)).T, axis=-1, keepdims=True)
                dgc = (jnp.sum(gmat, axis=-1, keepdims=True) - col_sums[:CHUNK]
                       + jnp.sum(dqd[hd] * qd[hd], axis=-1, keepdims=True)
                       + jnp.sum(dkbg * kbg[hd], axis=-1, keepdims=True) - dkd_kd)
                extra = jnp.sum(dkd_kd, axis=0, keepdims=True) + dgl[hd] * gl[hd]
                dg_ref[rs, cs] = dgc + jnp.where(lastrow, extra, 0.0)
        for hd in heads:
            ds_scr[hd] = dsp[hd]
        dg = dg_ref[...]
        row = _iota(dg.shape, 0)
        pos = row % CHUNK
        step = 1
        while step < CHUNK:
            dg = dg + jnp.where(pos < CHUNK - step, pltpu.roll(dg, dg.shape[0] - step, 0), 0.0)
            step *= 2
        dg_ref[...] = dg

    nb, tok, mat, state = _gdn_block_specs(t, True)
    return pl.pallas_call(
        body, name="gdn_bwd", grid=(nb,),
        in_specs=[tok] * 5 + [mat, mat, state, tok, tok], out_specs=[tok] * 5, out_shape=[_sds((t, D_GDN))] * 5,
        scratch_shapes=[pltpu.VMEM((N_GDN_HEADS, GDN_HEAD_DIM, GDN_HEAD_DIM), F32)],
        compiler_params=_params("arbitrary"),
    )(qn, kn, cv, be, gc, tinv, amat, s_all, vn_all, do)


def _gdn_bwd_conv(proj, convw, dqn, dkn, dcv):
    t = proj.shape[0]

    def body(xq_ref, xk_ref, xv_ref, wq_ref, wk_ref, wv_ref, dq_ref, dk_ref, dv_ref,
             dxq_ref, dxk_ref, dxv_ref, dwq_ref, dwk_ref, dwv_ref):
        row = _iota((t, LANES), 0)

        def one(x_ref, w_ref, d_ref, dx_ref, dw_ref, scale):
            x = x_ref[...]
            w = w_ref[...]
            y = _conv(x, w, row)
            sg = _sigmoid(y)
            dc = d_ref[...]
            if scale is not None:
                c = y * sg
                r = lax.rsqrt(jnp.sum(c * c, axis=-1, keepdims=True) + EPS)
                ch = c * r
                dc = scale * r * (dc - ch * jnp.sum(dc * ch, axis=-1, keepdims=True))
            dy = dc * sg * (1.0 + y * (1.0 - sg))
            dx_ref[...] = (w[3:4, :] * dy + w[2:3, :] * _shift_up(dy, 1, row) + w[1:2, :] * _shift_up(dy, 2, row)
                           + w[0:1, :] * _shift_up(dy, 3, row)).astype(BF)
            for jj in range(CONV_K):
                xs = x if jj == CONV_K - 1 else _shift_down(x, CONV_K - 1 - jj, row)
                dw_ref[jj:jj + 1, :] = jnp.sum(dy * xs, axis=0, keepdims=True)

        one(xq_ref, wq_ref, dq_ref, dxq_ref, dwq_ref, GDN_QSCALE)
        one(xk_ref, wk_ref, dk_ref, dxk_ref, dwk_ref, 1.0)
        one(xv_ref, wv_ref, dv_ref, dxv_ref, dwv_ref, None)

    col, cw, _ = _gdn_specs(t)
    return pl.pallas_call(
        body, name="gdn_bwd_conv", grid=(N_GDN_HEADS,),
        in_specs=[col(12), col(16), col(20), cw(0), cw(4), cw(8), col(0), col(0), col(0)],
        out_specs=[col(0), col(0), col(0), cw(0), cw(0), cw(0)],
        out_shape=[_sds((t, D_GDN), BF)] * 3 + [_sds((CONV_K, D_GDN))] * 3,
        compiler_params=_params("parallel"),
    )(proj, proj, proj, convw, convw, convw, dqn, dkn, dcv)


def _mix_out(fox_n, gdn_o, proj, gnw, w_out, x, pmw, plw):
    t = x.shape[0]
    tm = min(MATMUL_BLOCK, t)

    def body(fn_ref, go_ref, gz_ref, gnw_ref, w_ref, x_ref, pmw_ref, plw_ref, x1_ref, h2_ref, mixed_ref, omix_ref,
             h2t_ref):
        omix_ref[:, 0:D_FOX] = fn_ref[...]
        for hd in range(N_GDN_HEADS):
            cs = slice(hd * LANES, (hd + 1) * LANES)
            go = go_ref[:, cs]
            r = lax.rsqrt(jnp.mean(go * go, axis=-1, keepdims=True) + EPS)
            gz = gz_ref[:, cs]
            omix_ref[:, D_FOX + hd * LANES:D_FOX + (hd + 1) * LANES] = (
                go * r * gnw_ref[...] * (gz * _sigmoid(gz))).astype(BF)
        mixed = jnp.dot(omix_ref[...], w_ref[...], preferred_element_type=F32)
        mixed_ref[...] = mixed
        r2 = lax.rsqrt(jnp.mean(mixed * mixed, axis=-1, keepdims=True) + EPS)
        x1 = x_ref[...] + mixed * r2 * pmw_ref[...]
        x1_ref[...] = x1
        r3 = lax.rsqrt(jnp.mean(x1 * x1, axis=-1, keepdims=True) + EPS)
        h2 = x1 * r3 * plw_ref[...]
        h2_ref[...] = h2.astype(BF)
        h2t_ref[...] = h2.T.astype(BF)

    tok = lambda w: pl.BlockSpec((tm, w), lambda i: (i, 0))
    vec = lambda w: pl.BlockSpec((1, w), lambda i: (0, 0))
    return pl.pallas_call(
        body, name="mix_out", grid=(t // tm,),
        in_specs=[tok(D_FOX), tok(D_GDN), pl.BlockSpec((tm, D_GDN), lambda i: (i, COL_GZ // D_GDN)), vec(LANES),
                  pl.BlockSpec((D_MODEL, D_MODEL), lambda i: (0, 0)), tok(D_MODEL), vec(D_MODEL), vec(D_MODEL)],
        out_specs=[tok(D_MODEL)] * 4 + [pl.BlockSpec((D_MODEL, tm), lambda i: (0, i))],
        out_shape=[_sds((t, D_MODEL)), _sds((t, D_MODEL), BF), _sds((t, D_MODEL)), _sds((t, D_MODEL), BF),
                   _sds((D_MODEL, t), BF)],
        compiler_params=_params("parallel"),
    )(fox_n, gdn_o, proj, gnw, w_out, x, pmw, plw)


def _out_bwd(dmixed, w_out, o_fox, gdn_o, proj, fnw, gnw):
    t = dmixed.shape[0]
    tm = min(MATMUL_BLOCK, t)

    def body(dm_ref, w_ref, of_ref, go_ref, gz_ref, fnw_ref, gnw_ref, dof_ref, dgo_ref, dgz_ref, dfw_ref, dgw_ref):
        i = pl.program_id(0)

        @pl.when(i == 0)
        def _():
            dfw_ref[...] = jnp.zeros_like(dfw_ref)
            dgw_ref[...] = jnp.zeros_like(dgw_ref)

        domix = _mm_nt(dm_ref[...], w_ref[...])
        first = _iota((1, LANES), 1) < FOX_HEAD_DIM
        dfw = jnp.zeros((1, LANES), F32)
        dgw = jnp.zeros((1, LANES), F32)
        for pr in range(N_FOX_HEADS // 2):
            cs = slice(pr * LANES, (pr + 1) * LANES)
            o = of_ref[:, cs]
            dfn = domix[:, cs]
            o2 = o * o
            s0 = jnp.sum(jnp.where(first, o2, 0.0), axis=-1, keepdims=True)
            s1 = jnp.sum(jnp.where(first, 0.0, o2), axis=-1, keepdims=True)
            r = lax.rsqrt(jnp.where(first, s0, s1) * (1.0 / FOX_HEAD_DIM) + EPS)
            oh = o * r
            dfw = dfw + jnp.sum(dfn * oh, axis=0, keepdims=True)
            doh = dfn * fnw_ref[...]
            pr_ = doh * oh
            m0 = jnp.sum(jnp.where(first, pr_, 0.0), axis=-1, keepdims=True)
            m1 = jnp.sum(jnp.where(first, 0.0, pr_), axis=-1, keepdims=True)
            dof_ref[:, cs] = r * (doh - oh * jnp.where(first, m0, m1) * (1.0 / FOX_HEAD_DIM))
        for hd in range(N_GDN_HEADS):
            cs = slice(hd * LANES, (hd + 1) * LANES)
            go = go_ref[:, cs]
            gz = gz_ref[:, cs]
            dgated = domix[:, D_FOX + hd * LANES:D_FOX + (hd + 1) * LANES]
            r = lax.rsqrt(jnp.mean(go * go, axis=-1, keepdims=True) + EPS)
            goh = go * r
            sg = _sigmoid(gz)
            sz = gz * sg
            gn = goh * gnw_ref[...]
            dgn = dgated * sz
            dgz_ref[:, cs] = (dgated * gn * sg * (1.0 + gz * (1.0 - sg))).astype(BF)
            dgw = dgw + jnp.sum(dgn * goh, axis=0, keepdims=True)
            dgh = dgn * gnw_ref[...]
            dgo_ref[:, cs] = r * (dgh - goh * jnp.mean(dgh * goh, axis=-1, keepdims=True))
        dfw_ref[...] += dfw + pltpu.roll(dfw, FOX_HEAD_DIM, 1)
        dgw_ref[...] += dgw

    tok = lambda w: pl.BlockSpec((tm, w), lambda i: (i, 0))
    vec = lambda w: pl.BlockSpec((1, w), lambda i: (0, 0))
    return pl.pallas_call(
        body, name="out_bwd", grid=(t // tm,),
        in_specs=[tok(D_MODEL), pl.BlockSpec((D_MODEL, D_MODEL), lambda i: (0, 0)), tok(D_FOX), tok(D_GDN),
                  pl.BlockSpec((tm, D_GDN), lambda i: (i, COL_GZ // D_GDN)), vec(LANES), vec(LANES)],
        out_specs=[tok(D_FOX), tok(D_GDN), tok(D_GDN), vec(LANES), vec(LANES)],
        out_shape=[_sds((t, D_FOX)), _sds((t, D_GDN)), _sds((t, D_GDN), BF), _sds((1, LANES)), _sds((1, LANES))],
        compiler_params=_params("arbitrary"),
    )(dmixed, w_out, o_fox, gdn_o, proj, fnw, gnw)


def _mlp_up(h2, w_upt):
    t = h2.shape[0]
    tm = min(MATMUL_BLOCK, t)

    def body(h_ref, w_ref, up_ref):
        up_ref[...] = lax.dot_general(h_ref[...], w_ref[...], (((1,), (1,)), ((), ())),
                                      preferred_element_type=F32).astype(BF)

    return pl.pallas_call(
        body, name="mlp_up", grid=(t // tm,),
        in_specs=[pl.BlockSpec((tm, D_MODEL), lambda i: (i, 0)), pl.BlockSpec((D_FF, D_MODEL), lambda i: (0, 0))],
        out_specs=pl.BlockSpec((tm, D_FF), lambda i: (i, 0)), out_shape=_sds((t, D_FF), BF),
        compiler_params=_params("parallel"),
    )(h2, w_upt)


def _mlp_down_loss(up, w_down, x1, pw, target):
    t = up.shape[0]
    tm = min(MATMUL_BLOCK, t)

    def body(up_ref, w_ref, x1_ref, pw_ref, tg_ref, dy_ref, dx2_ref, loss_ref, dpw_ref):
        i = pl.program_id(0)

        @pl.when(i == 0)
        def _():
            loss_ref[...] = jnp.zeros_like(loss_ref)
            dpw_ref[...] = jnp.zeros_like(dpw_ref)

        u = jnp.maximum(up_ref[...].astype(F32), 0.0)
        y = jnp.dot((u * u).astype(BF), w_ref[...], preferred_element_type=F32)
        r = lax.rsqrt(jnp.mean(y * y, axis=-1, keepdims=True) + EPS)
        yh = y * r
        pw = pw_ref[...]
        err = x1_ref[...] + yh * pw - tg_ref[...]
        part = jnp.sum(jnp.sum(err * err, axis=-1, keepdims=True), axis=0, keepdims=True) * (0.5 / D_MODEL)
        loss_ref[...] += jnp.broadcast_to(part, loss_ref.shape)
        dx2 = err * (1.0 / D_MODEL)
        dx2_ref[...] = dx2
        dpw_ref[...] += jnp.sum(dx2 * yh, axis=0, keepdims=True)
        dyh = dx2 * pw
        dy_ref[...] = (r * (dyh - yh * jnp.mean(dyh * yh, axis=-1, keepdims=True))).astype(BF)

    tok = lambda w: pl.BlockSpec((tm, w), lambda i: (i, 0))
    vec = lambda w: pl.BlockSpec((1, w), lambda i: (0, 0))
    return pl.pallas_call(
        body, name="mlp_down_loss", grid=(t // tm,),
        in_specs=[tok(D_FF), pl.BlockSpec((D_FF, D_MODEL), lambda i: (0, 0)), tok(D_MODEL), vec(D_MODEL), tok(D_MODEL)],
        out_specs=[tok(D_MODEL), tok(D_MODEL), vec(LANES), vec(D_MODEL)],
        out_shape=[_sds((t, D_MODEL), BF), _sds((t, D_MODEL)), _sds((1, LANES)), _sds((1, D_MODEL))],
        compiler_params=_params("arbitrary"),
    )(up, w_down, x1, pw, target)


def _mlp_bwd_act(dy, w_down, up):
    t = dy.shape[0]
    tm = min(MATMUL_BLOCK, t)

    def body(dy_ref, w_ref, up_ref, dup_ref):
        da = lax.dot_general(dy_ref[...], w_ref[...], (((1,), (1,)), ((), ())), preferred_element_type=F32)
        dup_ref[...] = (da * (2.0 * jnp.maximum(up_ref[...].astype(F32), 0.0))).astype(BF)

    return pl.pallas_call(
        body, name="mlp_bwd_act", grid=(t // tm,),
        in_specs=[pl.BlockSpec((tm, D_MODEL), lambda i: (i, 0)), pl.BlockSpec((D_FF, D_MODEL), lambda i: (0, 0)),
                  pl.BlockSpec((tm, D_FF), lambda i: (i, 0))],
        out_specs=pl.BlockSpec((tm, D_FF), lambda i: (i, 0)), out_shape=_sds((t, D_FF), BF),
        compiler_params=_params("parallel"),
    )(dy, w_down, up)


def _mlp_bwd_in(dup, w_up, x1, plw, dx2, mixed, pmw):
    t = dup.shape[0]
    tm = min(MATMUL_BLOCK, t)

    def body(dup_ref, w_ref, x1_ref, plw_ref, dx2_ref, mx_ref, pmw_ref, dx1_ref, dmixed_ref, dplw_ref, dpmw_ref):
        i = pl.program_id(0)

        @pl.when(i == 0)
        def _():
            dplw_ref[...] = jnp.zeros_like(dplw_ref)
            dpmw_ref[...] = jnp.zeros_like(dpmw_ref)

        dh = jnp.dot(dup_ref[...], w_ref[...], preferred_element_type=F32)
        x1 = x1_ref[...]
        r = lax.rsqrt(jnp.mean(x1 * x1, axis=-1, keepdims=True) + EPS)
        xh = x1 * r
        dplw_ref[...] += jnp.sum(dh * xh, axis=0, keepdims=True)
        dxh = dh * plw_ref[...]
        dx1 = dx2_ref[...] + r * (dxh - xh * jnp.mean(dxh * xh, axis=-1, keepdims=True))
        dx1_ref[...] = dx1
        mx = mx_ref[...]
        r2 = lax.rsqrt(jnp.mean(mx * mx, axis=-1, keepdims=True) + EPS)
        mh = mx * r2
        dpmw_ref[...] += jnp.sum(dx1 * mh, axis=0, keepdims=True)
        dmh = dx1 * pmw_ref[...]
        dmixed_ref[...] = (r2 * (dmh - mh * jnp.mean(dmh * mh, axis=-1, keepdims=True))).astype(BF)

    tok = lambda w: pl.BlockSpec((tm, w), lambda i: (i, 0))
    vec = lambda w: pl.BlockSpec((1, w), lambda i: (0, 0))
    return pl.pallas_call(
        body, name="mlp_bwd_in", grid=(t // tm,),
        in_specs=[tok(D_FF), pl.BlockSpec((D_FF, D_MODEL), lambda i: (0, 0)), tok(D_MODEL),
                  vec(D_MODEL), tok(D_MODEL), tok(D_MODEL), vec(D_MODEL)],
        out_specs=[tok(D_MODEL), tok(D_MODEL), vec(D_MODEL), vec(D_MODEL)],
        out_shape=[_sds((t, D_MODEL)), _sds((t, D_MODEL), BF), _sds((1, D_MODEL)), _sds((1, D_MODEL))],
        compiler_params=_params("arbitrary"),
    )(dup, w_up, x1, plw, dx2, mixed, pmw)


def _wgrad(a, b, a_cols, split=1, a_fn=None, a_block0=0, name="wgrad"):
    t, b_cols = b.shape
    n_a = (a.shape[1] - a_block0 * a_cols) // a_cols if a_block0 else a.shape[1] // a_cols

    def body(a_ref, b_ref, o_ref):
        av = a_ref[...]
        if a_fn is not None:
            av = a_fn(av)
        o_ref[...] = _mm_tn(av, b_ref[...]).astype(BF).reshape(o_ref.shape)

    return pl.pallas_call(
        body, name=name, grid=(n_a,),
        in_specs=[pl.BlockSpec((t, a_cols), lambda i: (0, i + a_block0)), pl.BlockSpec((t, b_cols), lambda i: (0, 0))],
        out_specs=pl.BlockSpec((split, a_cols // split, b_cols), lambda i: (i, 0, 0)),
        out_shape=_sds((n_a * split, a_cols // split, b_cols), BF),
        compiler_params=_params("parallel"),
    )(a, b)


def _wgrad_pre_t(at, b, b_cols, name):
    rows, t = at.shape
    n_b = b.shape[1] // b_cols

    def body(a_ref, b_ref, o_ref):
        o_ref[0] = jnp.dot(a_ref[...], b_ref[...], preferred_element_type=F32).astype(BF)

    return pl.pallas_call(
        body, name=name, grid=(n_b,),
        in_specs=[pl.BlockSpec((rows, t), lambda j: (0, 0)), pl.BlockSpec((t, b_cols), lambda j: (0, j))],
        out_specs=pl.BlockSpec((1, rows, b_cols), lambda j: (j, 0, 0)), out_shape=_sds((n_b, rows, b_cols), BF),
        compiler_params=_params("parallel"),
    )(at, b)


def _small_bwd(proj, fb, al, dtb, dcq, dckt, dbe, dge):
    t = proj.shape[0]

    def body(sm_ref, fb_ref, al_ref, dtb_ref, dcq_ref, dckt_ref, dbe_ref, dge_ref, dsm_ref, dvec_ref):
        s = sm_ref[...]
        lane = _iota((1, LANES), 1)
        dcum = dcq_ref[...] - dckt_ref[...].T
        row = _iota((t, LANES), 0)
        step = 1
        while step < t:
            dcum = dcum + _shift_up(dcum, step, row)
            step *= 2
        dff = dcum * _sigmoid(-(s + fb_ref[...]))
        dbeta = jnp.zeros((t, LANES), F32)
        dg = jnp.zeros((t, LANES), F32)
        for hd in range(N_GDN_HEADS):
            dbeta = jnp.where(lane == SM_GB + hd, dbe_ref[:, hd * LANES:hd * LANES + 1], dbeta)
            dg = jnp.where(lane == SM_GA + hd, dge_ref[:, hd * LANES:hd * LANES + 1], dg)
        beta = _sigmoid(s)
        dgb = dbeta * beta * (1.0 - beta)
        za = s + dtb_ref[...]
        nea = -jnp.exp(al_ref[...])
        dga = dg * nea * _sigmoid(za)
        is_f = lane < SM_GB
        is_b = (lane >= SM_GB) & (lane < SM_GA)
        is_a = (lane >= SM_GA) & (lane < SM_GA + 4)
        dsm_ref[...] = jnp.where(is_f, dff, jnp.where(is_b, dgb, jnp.where(is_a, dga, 0.0))).astype(BF)
        dvec_ref[...] = jnp.zeros_like(dvec_ref)
        dvec_ref[0:1, :] = jnp.sum(jnp.where(is_f, dff, 0.0), axis=0, keepdims=True)
        dvec_ref[1:2, :] = jnp.sum(jnp.where(is_a, dg * nea * _softplus(za), 0.0), axis=0, keepdims=True)
        dvec_ref[2:3, :] = jnp.sum(jnp.where(is_a, dga, 0.0), axis=0, keepdims=True)

    vec = pl.BlockSpec((1, LANES), lambda i: (0, 0))
    full = lambda r, c: pl.BlockSpec((r, c), lambda i: (0, 0))
    return pl.pallas_call(
        body, name="small_bwd", grid=(1,),
        in_specs=[pl.BlockSpec((t, LANES), lambda i: (0, COL_SMALL // LANES)), vec, vec, vec, full(t, LANES),
                  full(LANES, t), full(t, 512), full(t, 512)],
        out_specs=[full(t, LANES), full(8, LANES)], out_shape=[_sds((t, LANES), BF), _sds((8, LANES))],
        compiler_params=_params("arbitrary"),
    )(proj, fb, al, dtb, dcq, dckt, dbe, dge)


def _pack_dproj(dfox, dgdn, dgz, dsm):
    t = dgz.shape[0]
    tm = min(MATMUL_BLOCK, t)

    def body(*refs):
        parts, dp_ref = refs[:8], refs[8]
        col = 0
        for part in parts:
            width = part.shape[1]
            dp_ref[:, col:col + width] = part[...].astype(BF)
            col += width

    tok = lambda w: pl.BlockSpec((tm, w), lambda i: (i, 0))
    return pl.pallas_call(
        body, name="pack_dproj", grid=(t // tm,), in_specs=[tok(D_FOX)] * 3 + [tok(D_GDN)] * 4 + [tok(LANES)],
        out_specs=tok(PROJ_W), out_shape=_sds((t, PROJ_W), BF), compiler_params=_params("parallel"),
    )(*dfox, *dgdn, dgz, dsm)


def _in_bwd(dproj, wt_al, x, nw, dx1):
    t = x.shape[0]
    tm = min(MATMUL_BLOCK, t)

    def body(dp_ref, w_ref, x_ref, nw_ref, dx1_ref, dx_ref, dnw_ref):
        i = pl.program_id(0)

        @pl.when(i == 0)
        def _():
            dnw_ref[...] = jnp.zeros_like(dnw_ref)

        dh = jnp.dot(dp_ref[...], w_ref[...], preferred_element_type=F32)
        xv = x_ref[...]
        r = lax.rsqrt(jnp.mean(xv * xv, axis=-1, keepdims=True) + EPS)
        xh = xv * r
        dnw_ref[...] += jnp.sum(dh * xh, axis=0, keepdims=True)
        dxh = dh * nw_ref[...]
        dx_ref[...] = dx1_ref[...] + r * (dxh - xh * jnp.mean(dxh * xh, axis=-1, keepdims=True))

    tok = lambda w: pl.BlockSpec((tm, w), lambda i: (i, 0))
    vec = lambda w: pl.BlockSpec((1, w), lambda i: (0, 0))
    return pl.pallas_call(
        body, name="in_bwd", grid=(t // tm,),
        in_specs=[tok(PROJ_W), pl.BlockSpec((PROJ_W, D_MODEL), lambda i: (0, 0)), tok(D_MODEL), vec(D_MODEL),
                  tok(D_MODEL)],
        out_specs=[tok(D_MODEL), vec(D_MODEL)], out_shape=[_sds((t, D_MODEL)), _sds((1, D_MODEL))],
        compiler_params=_params("arbitrary"),
    )(dproj, wt_al, x, nw, dx1)


def _row(v, width=None):
    v = v.reshape(1, -1).astype(F32)
    if width is not None and v.shape[1] < width:
        v = jnp.pad(v, ((0, 0), (0, width - v.shape[1])))
    return v


def _lane_vec(v, first):
    return jnp.pad(v.astype(F32), (first, LANES - first - v.shape[0])).reshape(1, LANES)


def _local_step(x, target, wt_al, late_weights, on_grads, convw, pre_mix_norm, fox_f_bias, fox_out_norm,
                gdn_a_log, gdn_dt_bias, gdn_out_norm, post_mix_norm, pre_mlp_norm, post_mlp_norm):
    t = x.shape[0]
    nch = t // CHUNK
    nw, pmw, plw, pw = _row(pre_mix_norm), _row(post_mix_norm), _row(pre_mlp_norm), _row(post_mlp_norm)
    fb, al, dtb = _lane_vec(fox_f_bias, SM_FF), _lane_vec(gdn_a_log, SM_GA), _lane_vec(gdn_dt_bias, SM_GA)
    fnw = _row(jnp.tile(fox_out_norm, 2))
    gnw = _row(gdn_out_norm)

    proj, h = _norm_proj(x, nw, wt_al)
    cumt, beta, g = _small_prep(proj, fb, al, dtb)
    o_fox, lse, fox_n = _fox_fwd(proj, cumt, fnw)
    qn, kn, cv, gc, be, mmat, amat = _gdn_prep(proj, convw, beta, g)
    n_prob = N_GDN_HEADS * nch
    m3 = mmat.reshape(n_prob, CHUNK, CHUNK)
    if n_prob < LANES:
        m3 = jnp.pad(m3, ((0, LANES - n_prob), (0, 0), (0, 0)))
    tinv = _tri_inverse(m3)[:n_prob].reshape(N_GDN_HEADS, nch, CHUNK, CHUNK)
    token = late_weights("mlp_relay", tinv)
    gdn_o, s_all, vn_all = _gdn_scan(qn, kn, cv, be, gc, tinv, amat)
    w_out = late_weights("w_out", gdn_o)
    x1, h2, mixed, omix, h2t = _mix_out(fox_n, gdn_o, proj, gnw + token[0:1, 0:1], w_out, x, pmw, plw)
    w_up, w_down = late_weights("mlp", h2)
    up = _mlp_up(h2, w_up)
    dy, dx2, loss, d_pw = _mlp_down_loss(up, w_down, x1, pw, target)

    dup = _mlp_bwd_act(dy, w_down, up)
    relu2 = lambda u: jnp.square(jnp.maximum(u.astype(F32), 0.0))
    g_down = _wgrad(up, dy, D_FF // N_DEV, a_fn=relu2, name="wgrad_down")
    g_up = _wgrad_pre_t(h2t, dup, D_FF // N_DEV, name="wgrad_up")
    token = on_grads("mlp", (g_up, g_down))
    dx1, dmixed, d_plw, d_pmw = _mlp_bwd_in(dup, w_up, x1, plw + token[0:1, 0:1], dx2, mixed, pmw)
    token = on_grads("w_out", _wgrad(omix, dmixed, 512, split=4, name="wgrad_out"))
    do_fox, dgo, dgz, d_fnw, d_gnw = _out_bwd(dmixed, w_out, o_fox, gdn_o, proj, fnw + token[0:1, 0:1], gnw)
    dfq, dfk, dfv, dcq, dckt = _fox_bwd(proj, cumt, lse, o_fox, do_fox)
    dqn, dkn, dcv, dbe, dge = _gdn_bwd(qn, kn, cv, be, gc, tinv, amat, s_all, vn_all, dgo)
    dxq, dxk, dxv, dwq, dwk, dwv = _gdn_bwd_conv(proj, convw, dqn, dkn, dcv)
    dsm, dvec = _small_bwd(proj, fb, al, dtb, dcq, dckt, dbe, dge)
    dproj = _pack_dproj((dfq, dfk, dfv), (dxq, dxk, dxv), dgz, dsm)
    g_main = _wgrad(dproj, h, 512, name="wgrad_in")
    g_tail = _wgrad(dproj, h, LANES, a_block0=COL_SMALL // LANES, name="wgrad_in_small")
    token = on_grads("w_in", jnp.concatenate([g_main.reshape(COL_SMALL, D_MODEL), g_tail[0]]))
    grad_x, d_nw = _in_bwd(dproj, wt_al, x, nw + token[0:1, 0:1], dx1)
    small = dict(norms=(d_nw, d_pmw, d_plw, d_pw), fox_out_norm=d_fnw, gdn_out_norm=d_gnw, loss=loss, vectors=dvec,
                 conv=(dwq, dwk, dwv))
    return grad_x, small


MESH_IDS = pl.DeviceIdType.MESH
CHIP_FLIPS = ((0, 0), (1, 0), (0, 1), (1, 1))
ANY_SPEC = pl.BlockSpec(memory_space=pl.ANY)


def _place():
    return lax.axis_index("x"), lax.axis_index("y"), lax.axis_index("c")


def _all_gather(blocks):
    n = len(blocks)

    def body(*refs):
        ins, outs, (send_sems, recv_sems, local_sems) = refs[:n], refs[n:2 * n], refs[2 * n:]
        x, y, c = _place()
        sibling = (x, y, 1 - c)
        chips = [(x ^ fx, y ^ fy) for fx, fy in CHIP_FLIPS[1:]]

        def slot(out, px, py, pc):
            return out.at[4 * px + 2 * py + pc]

        def copy(a, k, block, to, src=None):
            return pltpu.make_async_remote_copy(
                src_ref=slot(outs[a], *block) if src is None else src, dst_ref=slot(outs[a], *block),
                send_sem=send_sems.at[a, k], recv_sem=recv_sems.at[a, k], device_id=to, device_id_type=MESH_IDS)

        pending = []
        for a in range(n):
            mine = pltpu.make_async_copy(ins[a], slot(outs[a], x, y, c), local_sems.at[a])
            mine.start()
            pending.append(mine)
        sends = []
        for a in range(n):
            first = [copy(a, 0, (x, y, c), sibling, src=ins[a])]
            first += [copy(a, 1 + j, (x, y, c), (*chip, c), src=ins[a]) for j, chip in enumerate(chips)]
            for cp in first:
                cp.start()
            sends += first
        for a in range(n):
            for j, chip in enumerate(chips):
                copy(a, 1 + j, (*chip, c), (x, y, c)).wait_recv()
                fwd = copy(a, 4 + j, (*chip, c), sibling)
                fwd.start()
                sends.append(fwd)
        for a in range(n):
            copy(a, 0, sibling, (x, y, c)).wait_recv()
            for j, chip in enumerate(chips):
                copy(a, 4 + j, (*chip, 1 - c), (x, y, c)).wait_recv()
        for cp in sends:
            cp.wait_send()
        for cp in pending:
            cp.wait()

    return pl.pallas_call(
        body, name="all_gather_weights", in_specs=[ANY_SPEC] * n, out_specs=[ANY_SPEC] * n,
        out_shape=[_sds((N_DEV,) + b.shape, b.dtype) for b in blocks],
        scratch_shapes=[pltpu.SemaphoreType.DMA((n, 7)), pltpu.SemaphoreType.DMA((n, 7)), pltpu.SemaphoreType.DMA((n,))],
        compiler_params=pltpu.CompilerParams(has_side_effects=True),
    )(*blocks)


def _adamw(w, g, m, v):
    m = ADAM_B1 * m + (1.0 - ADAM_B1) * g
    v = ADAM_B2 * v + (1.0 - ADAM_B2) * (g * g)
    m_hat = m / (1.0 - ADAM_B1 ** ADAM_STEP)
    v_hat = v / (1.0 - ADAM_B2 ** ADAM_STEP)
    return -ADAM_LR * (m_hat / (jnp.sqrt(v_hat) + ADAM_EPS) + ADAM_WD * w), m, v


def _pair_reduce(g, name):
    _, r, c_ = g.shape
    n = len(CHIP_FLIPS)

    def body(g_ref, out_ref, sib_buf, send_sems, recv_sems):
        x, y, c = _place()
        chips = [(x ^ fx, y ^ fy) for fx, fy in CHIP_FLIPS]
        piece = lambda chip, core: g_ref.at[4 * chip[0] + 2 * chip[1] + core]
        copies = [pltpu.make_async_remote_copy(
            src_ref=piece(chip, 1 - c), dst_ref=sib_buf.at[j], send_sem=send_sems.at[j], recv_sem=recv_sems.at[j],
            device_id=(x, y, 1 - c), device_id_type=MESH_IDS) for j, chip in enumerate(chips)]
        for cp in copies:
            cp.start()
        for j, chip in enumerate(chips):
            copies[j].wait_recv()
            out_ref[j] = (piece(chip, c)[...].astype(F32) + sib_buf[j].astype(F32)).astype(BF)
        for cp in copies:
            cp.wait_send()

    return pl.pallas_call(
        body, name=name, in_specs=[VMEM_SPEC], out_specs=VMEM_SPEC, out_shape=_sds((n, r, c_), BF),
        scratch_shapes=[pltpu.VMEM((n, r, c_), BF), pltpu.SemaphoreType.DMA((n,)), pltpu.SemaphoreType.DMA((n,))],
        compiler_params=pltpu.CompilerParams(vmem_limit_bytes=VMEM_LIMIT, has_side_effects=True),
    )(g)


HBM_SPEC = pl.BlockSpec(memory_space=pltpu.HBM)
SEM_SPEC = pl.BlockSpec(memory_space=pltpu.SEMAPHORE)
DATAFLOW = pltpu.SideEffectType.DATAFLOW_SIDE_EFFECTING


def _peers():
    x, y, c = _place()
    return 4 * x + 2 * y + c, [(x ^ (k >> 2), y ^ ((k >> 1) & 1), c ^ (k & 1)) for k in range(1, N_DEV)]


def _peer_index(peer):
    return 4 * peer[0] + 2 * peer[1] + peer[2]


def _zones_with_own(srcs, pieces, name, after=None, dtype=None, chips=False):
    n = len(srcs)
    slots = len(CHIP_FLIPS) if chips else N_DEV
    extra = [] if after is None else [after]
    dtypes = [s_.dtype if pieces or dtype is None else dtype for s_ in srcs]

    def body(me_ref, *refs):
        outs = refs[n + len(extra):]
        for a in range(n):
            if pieces:
                outs[a][0] = refs[a][0]
            else:
                val = refs[a][...].astype(dtypes[a])
                outs[a][0] = val
                outs[n + a][...] = val

    shapes = [s_.shape[1:] if pieces else s_.shape for s_ in srcs]
    mine = lambda sh: pl.BlockSpec((1,) + sh, lambda i, me_ref: (me_ref[0], 0, 0))
    whole = lambda sh: pl.BlockSpec(sh, lambda i, me_ref: (0, 0))
    in_specs = [mine(sh) if pieces else whole(sh) for sh in shapes]
    out_specs = [mine(sh) for sh in shapes] + ([] if pieces else [whole(sh) for sh in shapes])
    out_shape = [_sds((slots,) + sh, dt) for sh, dt in zip(shapes, dtypes)]
    out_shape += [] if pieces else [_sds(sh, dt) for sh, dt in zip(shapes, dtypes)]
    x, y, c = _place()
    own = 0 * x if chips else 4 * x + 2 * y + c
    out = pl.pallas_call(
        body, name=name,
        grid_spec=pltpu.PrefetchScalarGridSpec(num_scalar_prefetch=1, grid=(1,), in_specs=in_specs + [ANY_SPEC] * len(extra),
                                               out_specs=out_specs),
        out_shape=out_shape, compiler_params=_params("arbitrary"),
    )(own.astype(jnp.int32).reshape(1), *srcs, *extra)
    return out[:n], (list(srcs) if pieces else out[n:])


def _exchange_start(srcs, zones, pieces, name, chips=False):
    n = len(srcs)

    def body(*refs):
        ins, zs = refs[:n], refs[n:2 * n]
        sems = refs[2 * n:4 * n]
        token = refs[-1]
        me, peers = _peers()
        x, y, c = _place()
        if chips and pieces:
            routes = [((x ^ fx, y ^ fy, c), j, j) for j, (fx, fy) in enumerate(CHIP_FLIPS) if j]
        elif chips:
            routes = [((x ^ fx, y ^ fy, c), None, me) for fx, fy in CHIP_FLIPS[1:]]
        else:
            routes = [(peer, _peer_index(peer) if pieces else None, me) for peer in peers]
        for peer, src_slot, dst_slot in routes:
            for a in range(n):
                pltpu.make_async_remote_copy(
                    src_ref=ins[a] if src_slot is None else ins[a].at[src_slot], dst_ref=zs[a].at[dst_slot],
                    send_sem=sems[2 * a], recv_sem=sems[2 * a + 1], device_id=peer, device_id_type=MESH_IDS).start()
        token[...] = jnp.zeros_like(token)

    hbm = lambda v: pltpu.with_memory_space_constraint(v, pltpu.HBM)
    out = pl.pallas_call(
        body, name=name,
        out_shape=tuple([pltpu.SemaphoreType.DMA(())] * (2 * n) + [pltpu.HBM(v.shape, v.dtype) for v in srcs]
                        + [pltpu.HBM(z.shape, z.dtype) for z in zones] + [_sds((8, LANES))]),
        in_specs=[HBM_SPEC] * (2 * n), out_specs=tuple([SEM_SPEC] * (2 * n) + [HBM_SPEC] * (2 * n) + [VMEM_SPEC]),
        input_output_aliases={i: 2 * n + i for i in range(2 * n)},
        compiler_params=pltpu.CompilerParams(has_side_effects=DATAFLOW),
    )(*[hbm(v) for v in srcs], *[hbm(z) for z in zones])
    return out[:2 * n], out[2 * n:3 * n], out[3 * n:4 * n], out[-1]


def _relay_start(zones, name):
    n = len(zones)

    def body(*refs):
        zs, sems, token = refs[:n], refs[n:3 * n], refs[-1]
        x, y, c = _place()
        for fx, fy in CHIP_FLIPS:
            slot = 4 * (x ^ fx) + 2 * (y ^ fy) + c
            for a in range(n):
                pltpu.make_async_remote_copy(
                    src_ref=zs[a].at[slot], dst_ref=zs[a].at[slot], send_sem=sems[2 * a], recv_sem=sems[2 * a + 1],
                    device_id=(x, y, 1 - c), device_id_type=MESH_IDS).start()
        token[...] = jnp.zeros_like(token)

    out = pl.pallas_call(
        body, name=name,
        out_shape=tuple([pltpu.SemaphoreType.DMA(())] * (2 * n) + [pltpu.HBM(z.shape, z.dtype) for z in zones]
                        + [_sds((8, LANES))]),
        in_specs=[HBM_SPEC] * n, out_specs=tuple([SEM_SPEC] * (2 * n) + [HBM_SPEC] * n + [VMEM_SPEC]),
        input_output_aliases={i: 2 * n + i for i in range(n)},
        compiler_params=pltpu.CompilerParams(has_side_effects=DATAFLOW),
    )(*[pltpu.with_memory_space_constraint(z, pltpu.HBM) for z in zones])
    return out[:2 * n], [], out[2 * n:3 * n], out[-1]


def _exchange_wait(sems, srcs, zones, after, name, chips=False, n_copies=None):
    n, n_src = len(zones), len(srcs)
    after = list(after) if isinstance(after, (list, tuple)) else [after]
    n_copies = n_copies or (len(CHIP_FLIPS) - 1 if chips else N_DEV - 1)

    def body(*refs):
        zs, sm = refs[n_src:n_src + n], refs[n_src + n:n_src + 3 * n]
        me, peers = _peers()
        for a in range(n):
            seven = zs[a].at[pl.ds(0, n_copies)]
            cp = pltpu.make_async_remote_copy(src_ref=seven, dst_ref=seven, send_sem=sm[2 * a], recv_sem=sm[2 * a + 1],
                                              device_id=peers[0], device_id_type=MESH_IDS)
            cp.wait_send()
            cp.wait_recv()

    out = pl.pallas_call(
        body, name=name, out_shape=tuple([pltpu.HBM(v.shape, v.dtype) for v in srcs] + [pltpu.HBM(z.shape, z.dtype) for z in zones]),
        in_specs=[HBM_SPEC] * (n_src + n) + [SEM_SPEC] * (2 * n) + [ANY_SPEC] * len(after),
        out_specs=tuple([HBM_SPEC] * (n_src + n)), input_output_aliases={i: i for i in range(n_src + n)},
        compiler_params=pltpu.CompilerParams(has_side_effects=DATAFLOW),
    )(*srcs, *zones, *sems, *after)
    return out[n_src:]


def _sum_adamw(zone, w, m, v, name):
    n_slots, r, c_ = zone.shape
    rb = next((b for b in (256, 128) if r % b == 0), r)

    def body(z_ref, w_ref, m_ref, v_ref, grad_ref, delta_ref, nm_ref, nv_ref):
        total = z_ref[0].astype(F32)
        for d in range(1, n_slots):
            total = total + z_ref[d].astype(F32)
        grad_ref[...] = total
        delta_ref[...], nm_ref[...], nv_ref[...] = _adamw(w_ref[...], total, m_ref[...], v_ref[...])

    blk = pl.BlockSpec((rb, c_), lambda i: (i, 0))
    return pl.pallas_call(
        body, name=name, grid=(r // rb,), in_specs=[pl.BlockSpec((n_slots, rb, c_), lambda i: (0, i, 0)), blk, blk, blk],
        out_specs=[blk] * 4, out_shape=[_sds((r, c_))] * 4, compiler_params=_params("parallel"),
    )(zone, w, m, v)


SMALL_NORMS = ("pre_mix_norm", "post_mix_norm", "pre_mlp_norm", "post_mlp_norm")
SMALL_ORDER = SMALL_NORMS + ("fox_out_norm", "gdn_out_norm", "fox_f_bias", "gdn_a_log", "gdn_dt_bias", "gdn_conv_w")
CONV_SLAB_ROWS, CONV_SLAB_LANES = 8, 256


def _small_pack(small):
    def body(n0, n1, n2, n3, fnw_ref, gnw_ref, loss_ref, vec_ref, out_ref):
        out_ref[...] = jnp.zeros_like(out_ref)
        for i, ref in enumerate((n0, n1, n2, n3)):
            out_ref[i:i + 1, :] = ref[...]
        out_ref[4:5, 0:LANES] = fnw_ref[...]
        out_ref[4:5, LANES:2 * LANES] = gnw_ref[...]
        out_ref[4:5, 2 * LANES:3 * LANES] = loss_ref[...]
        out_ref[5:8, 0:LANES] = vec_ref[0:3, :]

    return pl.pallas_call(body, name="small_pack", in_specs=[VMEM_SPEC] * 8, out_specs=VMEM_SPEC,
                          out_shape=_sds((8, D_MODEL)))(*small["norms"], small["fox_out_norm"], small["gdn_out_norm"],
                                                        small["loss"], small["vectors"])


def _conv_slabs(dconv):
    blocks = dconv.reshape(CONV_K, N_DEV, -1).transpose(1, 0, 2)
    blocks = jnp.pad(blocks, ((0, 0), (0, CONV_SLAB_ROWS - CONV_K), (0, CONV_SLAB_LANES - blocks.shape[2])))
    return blocks.reshape(N_DEV * CONV_SLAB_ROWS, CONV_SLAB_LANES)


def _small_update(zone, conv_zone, w, m, v):
    n = len(SMALL_ORDER)
    n_conv = w["gdn_conv_w"].shape[1]

    def body(me_ref, z_ref, zc_ref, *refs):
        params, loss_ref, outs, (tot, totc) = refs[:3 * n], refs[3 * n], refs[3 * n + 1:7 * n + 1], refs[-2:]
        total, total_c = z_ref[0], zc_ref[0]
        for d in range(1, N_DEV):
            total, total_c = total + z_ref[d], total_c + zc_ref[d]
        tot[...] = total
        totc[...] = total_c
        loss_ref[...] = tot[4, 2 * LANES:2 * LANES + 1]
        mine = totc[pl.ds(pl.multiple_of(me_ref[0] * CONV_SLAB_ROWS, CONV_SLAB_ROWS), CONV_SLAB_ROWS), :]
        g = dict(zip(SMALL_NORMS, (tot[0], tot[1], tot[2], tot[3])))
        g.update(fox_out_norm=tot[4, 0:FOX_HEAD_DIM], gdn_out_norm=tot[4, LANES:LANES + GDN_HEAD_DIM],
                 fox_f_bias=tot[5, SM_FF:SM_FF + N_FOX_HEADS], gdn_a_log=tot[6, SM_GA:SM_GA + N_GDN_HEADS],
                 gdn_dt_bias=tot[7, SM_GA:SM_GA + N_GDN_HEADS], gdn_conv_w=mine[0:CONV_K, 0:n_conv])
        for i, name in enumerate(SMALL_ORDER):
            w_ref, m_ref, v_ref = params[3 * i:3 * i + 3]
            outs[4 * i][...] = g[name]
            outs[4 * i + 1][...], outs[4 * i + 2][...], outs[4 * i + 3][...] = _adamw(w_ref[...], g[name], m_ref[...],
                                                                                     v_ref[...])

    x, y, c = _place()
    operands = [a[name] for name in SMALL_ORDER for a in (w, m, v)]
    out = pl.pallas_call(
        body, name="small_update",
        in_specs=[pl.BlockSpec(memory_space=pltpu.SMEM)] + [VMEM_SPEC] * (2 + 3 * n), out_specs=[VMEM_SPEC] * (1 + 4 * n),
        out_shape=[_sds((1,))] + [_sds(w[name].shape) for name in SMALL_ORDER for _ in range(4)],
        scratch_shapes=[pltpu.VMEM(zone.shape[1:], F32), pltpu.VMEM(conv_zone.shape[1:], F32)],
    )((4 * x + 2 * y + c).astype(jnp.int32).reshape(1), zone, conv_zone, *operands)
    return out[0][0], {name: out[1 + 4 * i:5 + 4 * i] for i, name in enumerate(SMALL_ORDER)}


NATIVE_ROWS = ((0, 1536), (1544, 3080), (3088, 3600), (1536, 1544), (3080, 3088))


def _to_aligned_rows(wt_native):
    pad = jnp.zeros((PROJ_W - D_PROJ, wt_native.shape[1]), wt_native.dtype)
    return jnp.concatenate([wt_native[lo:hi] for lo, hi in NATIVE_ROWS] + [pad])


def _from_aligned_rows(gt_al):
    return jnp.concatenate([gt_al[0:1536], gt_al[3584:3592], gt_al[1536:3072], gt_al[3592:3600], gt_al[3072:3584]])


def _cols_from_pieces(p):
    return p.transpose(1, 0, 2).reshape(p.shape[1], -1)


WEIGHT_ORDER = ("pre_mix_norm", "w_in", "fox_f_bias", "fox_out_norm", "gdn_conv_w", "gdn_a_log", "gdn_dt_bias",
                "gdn_out_norm", "w_out", "post_mix_norm", "pre_mlp_norm", "w_up", "w_down", "post_mlp_norm")


def kernel(x, pre_mix_norm, w_in, fox_f_bias, fox_out_norm, gdn_conv_w, gdn_a_log, gdn_dt_bias, gdn_out_norm, w_out, post_mix_norm, pre_mlp_norm, w_up, w_down, post_mlp_norm, loss_target, m_pre_mix_norm, m_w_in, m_fox_f_bias, m_fox_out_norm, m_gdn_conv_w, m_gdn_a_log, m_gdn_dt_bias, m_gdn_out_norm, m_w_out, m_post_mix_norm, m_pre_mlp_norm, m_w_up, m_w_down, m_post_mlp_norm, v_pre_mix_norm, v_w_in, v_fox_f_bias, v_fox_out_norm, v_gdn_conv_w, v_gdn_a_log, v_gdn_dt_bias, v_gdn_out_norm, v_w_out, v_post_mix_norm, v_pre_mlp_norm, v_w_up, v_w_down, v_post_mlp_norm):
    w = dict(pre_mix_norm=pre_mix_norm, w_in=w_in, fox_f_bias=fox_f_bias, fox_out_norm=fox_out_norm,
             gdn_conv_w=gdn_conv_w, gdn_a_log=gdn_a_log, gdn_dt_bias=gdn_dt_bias, gdn_out_norm=gdn_out_norm, w_out=w_out,
             post_mix_norm=post_mix_norm, pre_mlp_norm=pre_mlp_norm, w_up=w_up, w_down=w_down, post_mlp_norm=post_mlp_norm)
    mom = dict(pre_mix_norm=m_pre_mix_norm, w_in=m_w_in, fox_f_bias=m_fox_f_bias, fox_out_norm=m_fox_out_norm,
               gdn_conv_w=m_gdn_conv_w, gdn_a_log=m_gdn_a_log, gdn_dt_bias=m_gdn_dt_bias, gdn_out_norm=m_gdn_out_norm,
               w_out=m_w_out, post_mix_norm=m_post_mix_norm, pre_mlp_norm=m_pre_mlp_norm, w_up=m_w_up, w_down=m_w_down,
               post_mlp_norm=m_post_mlp_norm)
    var = dict(pre_mix_norm=v_pre_mix_norm, w_in=v_w_in, fox_f_bias=v_fox_f_bias, fox_out_norm=v_fox_out_norm,
               gdn_conv_w=v_gdn_conv_w, gdn_a_log=v_gdn_a_log, gdn_dt_bias=v_gdn_dt_bias, gdn_out_norm=v_gdn_out_norm,
               w_out=v_w_out, post_mix_norm=v_post_mix_norm, pre_mlp_norm=v_pre_mlp_norm, w_up=v_w_up, w_down=v_w_down,
               post_mlp_norm=v_post_mlp_norm)

    win_g, conv_g = _all_gather([w_in.T.astype(BF), gdn_conv_w])
    wt_al = _to_aligned_rows(win_g.reshape(D_PROJ, D_MODEL))
    convw = _cols_from_pieces(conv_g)
    gathers, after = {}, win_g
    for name, shards in (("w_out", [w_out]), ("mlp", [w_up.T, w_down])):
        zones, shards = _zones_with_own(shards, False, "gather_" + name + "_own", after=after, dtype=BF)
        gathers[name] = _exchange_start(shards, zones, False, "gather_" + name + "_start", chips=name == "mlp")
        after = gathers[name][3]

    def late_weights(name, after):
        if name == "mlp_relay":
            sems, shards, zones, _ = gathers["mlp"]
            zones = _exchange_wait(sems, shards, zones, after, "gather_mlp_wait", chips=True)
            gathers["mlp"] = _relay_start(zones, "gather_mlp_relay")
            return gathers["mlp"][3]
        sems, shards, zones, _ = gathers[name]
        got = _exchange_wait(sems, shards, zones, after, "gather_" + name + "_done",
                             n_copies=len(CHIP_FLIPS) if name == "mlp" else None)
        if name == "w_out":
            return got[0].reshape(D_MODEL, D_MODEL)
        return got[0].reshape(D_FF, D_MODEL), got[1].reshape(D_FF, D_MODEL)

    scatters = {}

    def on_grads(name, g):
        chips = name == "w_in"
        if name == "w_in":
            g = _pair_reduce(_from_aligned_rows(g).reshape(N_DEV, D_PROJ // N_DEV, D_MODEL), "pair_reduce_w_in")
        srcs = list(g) if name == "mlp" else [g]
        zones, _ = _zones_with_own(srcs, True, "scatter_" + name + "_own", chips=chips)
        scatters[name] = _exchange_start(srcs, zones, True, "scatter_" + name + "_start", chips=chips)
        return scatters[name][3]

    grad_x, small = _local_step(
        x[0], loss_target[0], wt_al, late_weights, on_grads, convw, pre_mix_norm + after[0, 0],
        fox_f_bias, fox_out_norm, gdn_a_log, gdn_dt_bias, gdn_out_norm, post_mix_norm, pre_mlp_norm, post_mlp_norm)
    slabs = [_small_pack(small), _conv_slabs(jnp.concatenate(small["conv"], axis=1))]
    zones, slabs = _zones_with_own(slabs, False, "small_own")
    scatters["small"] = _exchange_start(slabs, zones, False, "small_start")

    grads, delta, new_m, new_v = {}, {}, {}, {}
    after = scatters["small"][3]
    for name, members in (("mlp", ("w_up", "w_down")), ("w_out", ("w_out",)), ("small", ()), ("w_in", ("w_in",))):
        sems, srcs, zones, _ = scatters[name]
        zones = _exchange_wait(sems, srcs, zones, after, "scatter_" + name + "_wait", chips=name == "w_in")
        if name == "small":
            loss, updated = _small_update(zones[0], zones[1], w, mom, var)
            for n, res in updated.items():
                grads[n], delta[n], new_m[n], new_v[n] = res
            after = grads["pre_mix_norm"]
        for n, zone in zip(members, zones):
            if n == "w_in":
                res = _sum_adamw(zone, w[n].T, mom[n].T, var[n].T, "adamw_" + n)
                grads[n], delta[n], new_m[n], new_v[n] = [r.T for r in res]
            else:
                grads[n], delta[n], new_m[n], new_v[n] = _sum_adamw(zone, w[n], mom[n], var[n], "adamw_" + n)
        if members:
            after = [grads[n] for n in members]

    return (loss, grad_x[None], *[grads[n] for n in WEIGHT_ORDER], *[delta[n] for n in WEIGHT_ORDER],
            *[new_m[n] for n in WEIGHT_ORDER], *[new_v[n] for n in WEIGHT_ORDER])
```

```python
import jax
import jax.numpy as jnp
from jax import lax
from jax.experimental import pallas as pl
from jax.experimental.pallas import tpu as pltpu

F32 = jnp.float32
BF = jnp.bfloat16

D_MODEL = 1024
N_FOX_HEADS, FOX_HEAD_DIM = 8, 64
N_GDN_HEADS, GDN_HEAD_DIM = 4, 128
D_FOX = N_FOX_HEADS * FOX_HEAD_DIM
D_GDN = N_GDN_HEADS * GDN_HEAD_DIM
CHUNK = 64
CONV_K = 4
D_FF = 4 * D_MODEL
EPS = 1e-6
D_PROJ = 3600
N_DEV = 8

PROJ_W = 3712
COL_FOX, COL_GDN, COL_GZ, COL_SMALL = 0, 1536, 3072, 3584
LANES = 128
SM_FF, SM_GB, SM_GA = 0, 8, 12

ADAM_LR, ADAM_B1, ADAM_B2, ADAM_EPS, ADAM_WD, ADAM_STEP = 0.001, 0.9, 0.999, 1e-08, 0.01, 10

TOKEN_BLOCK = 256
MATMUL_BLOCK = 512
FOX_SCALE = FOX_HEAD_DIM ** -0.5
GDN_QSCALE = GDN_HEAD_DIM ** -0.5
NEG_BIG = -1e30
VMEM_LIMIT = 56 * 1024 * 1024

VMEM_SPEC = pl.BlockSpec(memory_space=pltpu.VMEM)


def _sds(shape, dtype=F32):
    return jax.ShapeDtypeStruct(shape, dtype)


def _params(*sem):
    return pltpu.CompilerParams(dimension_semantics=sem if sem else None, vmem_limit_bytes=VMEM_LIMIT)


def _mm(a, b):
    return jnp.dot(a.astype(BF), b.astype(BF), preferred_element_type=F32)


def _mm_nt(a, b):
    return lax.dot_general(a.astype(BF), b.astype(BF), (((1,), (1,)), ((), ())), preferred_element_type=F32)


def _mm_tn(a, b):
    return lax.dot_general(a.astype(BF), b.astype(BF), (((0,), (0,)), ((), ())), preferred_element_type=F32)


def _sigmoid(x):
    return 1.0 / (1.0 + jnp.exp(-x))


def _softplus(x):
    return jnp.maximum(x, 0.0) + jnp.log1p(jnp.exp(-jnp.abs(x)))


def _iota(shape, dim):
    return lax.broadcasted_iota(jnp.int32, shape, dim)


def _shift_down(x, s, row):
    return jnp.where(row >= s, pltpu.roll(x, s, 0), 0.0)


def _shift_up(x, s, row):
    n = x.shape[0]
    return jnp.where(row < n - s, pltpu.roll(x, n - s, 0), 0.0)


def _norm_proj(x, nw, wt_al):
    t = x.shape[0]

    def body(x_ref, nw_ref, w_ref, proj_ref, h_ref):
        xv = x_ref[...]
        r = lax.rsqrt(jnp.mean(xv * xv, axis=-1, keepdims=True) + EPS)
        h = (xv * r * nw_ref[...]).astype(BF)
        h_ref[...] = h
        proj_ref[...] = lax.dot_general(h, w_ref[...], (((1,), (1,)), ((), ())), preferred_element_type=F32)

    tm = min(MATMUL_BLOCK, t)
    return pl.pallas_call(
        body, name="norm_proj", grid=(t // tm,),
        in_specs=[pl.BlockSpec((tm, D_MODEL), lambda i: (i, 0)), pl.BlockSpec((1, D_MODEL), lambda i: (0, 0)),
                  pl.BlockSpec((PROJ_W, D_MODEL), lambda i: (0, 0))],
        out_specs=[pl.BlockSpec((tm, PROJ_W), lambda i: (i, 0)), pl.BlockSpec((tm, D_MODEL), lambda i: (i, 0))],
        out_shape=[_sds((t, PROJ_W)), _sds((t, D_MODEL), BF)],
        compiler_params=_params("parallel"),
    )(x, nw, wt_al)


def _lane_column(x, lane):
    return jnp.sum(jnp.where(_iota((1, LANES), 1) == lane, x, 0.0), axis=-1, keepdims=True)


def _small_prep(proj, fb, al, dtb):
    t = proj.shape[0]

    def body(sm_ref, fb_ref, al_ref, dtb_ref, cumt_ref, beta_ref, g_ref):
        s = sm_ref[...]
        z = s + fb_ref[...]
        cum = jnp.minimum(z, 0.0) - jnp.log1p(jnp.exp(-jnp.abs(z)))
        row = _iota((t, LANES), 0)
        step = 1
        while step < t:
            cum = cum + _shift_down(cum, step, row)
            step *= 2
        cumt_ref[...] = cum.T
        beta_ref[...] = _sigmoid(s)
        g_ref[...] = -jnp.exp(al_ref[...]) * _softplus(s + dtb_ref[...])

    vec = pl.BlockSpec((1, LANES), lambda i: (0, 0))
    tok = pl.BlockSpec((t, LANES), lambda i: (0, 0))
    return pl.pallas_call(
        body, name="small_prep", grid=(1,),
        in_specs=[pl.BlockSpec((t, LANES), lambda i: (0, COL_SMALL // LANES)), vec, vec, vec],
        out_specs=[pl.BlockSpec((LANES, t), lambda i: (0, 0)), tok, tok],
        out_shape=[_sds((LANES, t)), _sds((t, LANES)), _sds((t, LANES))],
        compiler_params=_params("arbitrary"),
    )(proj, fb, al, dtb)


def _fox_stack(x, first):
    return jnp.concatenate([jnp.where(first, x, 0.0), jnp.where(first, 0.0, x)], axis=0).astype(BF)


def _fox_unstack(y, first):
    n = y.shape[0] // 2
    return jnp.where(first, y[:n], y[n:])


def _fox_logits(q2_i, kb, cumt_ref, pair, i, tq):
    klen = (i + 1) * tq
    s = lax.dot_general(q2_i, kb[:klen], (((1,), (1,)), ((), ())), preferred_element_type=F32)
    upper = _iota((2 * tq, 1), 0) < tq
    s = s - jnp.where(upper, cumt_ref[pl.ds(2 * pair, 1), 0:klen], cumt_ref[pl.ds(2 * pair + 1, 1), 0:klen])
    causal = _iota((2 * tq, tq), 1) <= _iota((2 * tq, tq), 0) % tq
    parts = [(s[:, :klen - tq], 0, klen - tq)] if i else []
    return parts + [(jnp.where(causal, s[:, klen - tq:], NEG_BIG), klen - tq, klen)]


def _fox_fwd(proj, cumt, fnw):
    t = proj.shape[0]
    tq = min(TOKEN_BLOCK, t // 2)
    nq = t // tq

    def body(q_ref, k_ref, v_ref, cumt_ref, fnw_ref, o_ref, lse_ref, fn_ref):
        j = pl.program_id(0)
        first = _iota((1, LANES), 1) < FOX_HEAD_DIM
        kb = k_ref[...].astype(BF)
        vb = v_ref[...].astype(BF)
        for i in range(nq):
            rows = slice(i * tq, (i + 1) * tq)
            q2 = _fox_stack(q_ref[rows, :] * FOX_SCALE, first)
            parts = _fox_logits(q2, kb, cumt_ref, j, i, tq)
            m = jnp.max(parts[-1][0], axis=-1, keepdims=True)
            if i:
                m = jnp.maximum(m, jnp.max(parts[0][0], axis=-1, keepdims=True))
            l = jnp.zeros((2 * tq, 1), F32)
            o = jnp.zeros((2 * tq, LANES), F32)
            for s, lo, hi in parts:
                p = jnp.exp(s - m)
                l = l + jnp.sum(p, axis=-1, keepdims=True)
                o = o + jnp.dot(p.astype(BF), vb[lo:hi], preferred_element_type=F32)
            o_acc = _fox_unstack(o / l, first)
            lse_acc = _fox_unstack(jnp.broadcast_to(m + jnp.log(l), (2 * tq, LANES)), first)
            o_ref[rows, :] = o_acc
            lse_ref[rows, :] = lse_acc
            o2 = o_acc * o_acc
            s0 = jnp.sum(jnp.where(first, o2, 0.0), axis=-1, keepdims=True)
            s1 = jnp.sum(jnp.where(first, 0.0, o2), axis=-1, keepdims=True)
            r = lax.rsqrt(jnp.where(first, s0, s1) * (1.0 / FOX_HEAD_DIM) + EPS)
            fn_ref[rows, :] = (o_acc * r * fnw_ref[...]).astype(BF)

    blk = lambda off: pl.BlockSpec((t, LANES), lambda j: (0, off + j))
    return pl.pallas_call(
        body, name="fox_fwd", grid=(N_FOX_HEADS // 2,),
        in_specs=[blk(0), blk(4), blk(8), pl.BlockSpec((LANES, t), lambda j: (0, 0)),
                  pl.BlockSpec((1, LANES), lambda j: (0, 0))],
        out_specs=[blk(0), blk(0), blk(0)],
        out_shape=[_sds((t, D_FOX)), _sds((t, D_FOX)), _sds((t, D_FOX), BF)],
        compiler_params=_params("parallel"),
    )(proj, proj, proj, cumt, fnw)


def _fox_bwd(proj, cumt, lse, o, do):
    t = proj.shape[0]
    tq = min(TOKEN_BLOCK, t // 2)
    nq = t // tq

    def body(q_ref, k_ref, v_ref, cumt_ref, lse_ref, o_ref, do_ref,
             dq_ref, dk_ref, dv_ref, dcq_ref, dckt_ref, dk_s, dv_s):
        j = pl.program_id(0)

        @pl.when(j == 0)
        def _():
            dcq_ref[...] = jnp.zeros_like(dcq_ref)
            dckt_ref[...] = jnp.zeros_like(dckt_ref)

        lane = _iota((1, LANES), 1)

        first = _iota((1, LANES), 1) < FOX_HEAD_DIM
        kb = k_ref[...].astype(BF)
        vb = v_ref[...].astype(BF)
        dk_s[...] = jnp.zeros_like(dk_s)
        dv_s[...] = jnp.zeros_like(dv_s)
        for i in range(nq):
            rows = slice(i * tq, (i + 1) * tq)
            do_i = do_ref[rows, :]
            prod = do_i * o_ref[rows, :]
            lse_i = lse_ref[rows, :]
            q2 = _fox_stack(q_ref[rows, :] * FOX_SCALE, first)
            do2 = _fox_stack(do_i, first)
            delta = jnp.concatenate([jnp.sum(jnp.where(first, prod, 0.0), axis=-1, keepdims=True),
                                     jnp.sum(jnp.where(first, 0.0, prod), axis=-1, keepdims=True)], axis=0)
            lse2 = jnp.concatenate([lse_i[:, 0:1], lse_i[:, FOX_HEAD_DIM:FOX_HEAD_DIM + 1]], axis=0)
            dq2 = jnp.zeros((2 * tq, LANES), F32)
            dcq2 = jnp.zeros((2 * tq, 1), F32)
            for s, lo, hi in _fox_logits(q2, kb, cumt_ref, j, i, tq):
                p = jnp.exp(s - lse2)
                ds = p * (_mm_nt(do2, vb[lo:hi]) - delta)
                dsb = ds.astype(BF)
                dq2 = dq2 + jnp.dot(dsb, kb[lo:hi], preferred_element_type=F32)
                dk_s[lo:hi, :] += _mm_tn(dsb, q2)
                dv_s[lo:hi, :] += _mm_tn(p, do2)
                dcq2 = dcq2 + jnp.sum(ds, axis=-1, keepdims=True)
                dckt_ref[pl.ds(2 * j, 1), lo:hi] += jnp.sum(ds[:tq], axis=0, keepdims=True)
                dckt_ref[pl.ds(2 * j + 1, 1), lo:hi] += jnp.sum(ds[tq:], axis=0, keepdims=True)
            dq_ref[rows, :] = (_fox_unstack(dq2, first) * FOX_SCALE).astype(BF)
            dcq_ref[rows, :] += jnp.where(lane == 2 * j, dcq2[:tq], jnp.where(lane == 2 * j + 1, dcq2[tq:], 0.0))
        dk_ref[...] = dk_s[...].astype(BF)
        dv_ref[...] = dv_s[...].astype(BF)

    blk = lambda off: pl.BlockSpec((t, LANES), lambda j: (0, off + j))
    rows128 = pl.BlockSpec((LANES, t), lambda j: (0, 0))
    return pl.pallas_call(
        body, name="fox_bwd", grid=(N_FOX_HEADS // 2,),
        in_specs=[blk(0), blk(4), blk(8), rows128, blk(0), blk(0), blk(0)],
        out_specs=[blk(0), blk(0), blk(0), pl.BlockSpec((t, LANES), lambda j: (0, 0)), rows128],
        out_shape=[_sds((t, D_FOX), BF)] * 3 + [_sds((t, LANES)), _sds((LANES, t))],
        scratch_shapes=[pltpu.VMEM((t, LANES), F32), pltpu.VMEM((t, LANES), F32)],
        compiler_params=_params("arbitrary"),
    )(proj, proj, proj, cumt, lse, o, do)


def _conv(x, w, row):
    return (w[3:4, :] * x + w[2:3, :] * _shift_down(x, 1, row) + w[1:2, :] * _shift_down(x, 2, row)
            + w[0:1, :] * _shift_down(x, 3, row))


def _chunk_decay(gc_c):
    gi = gc_c[:, 0:CHUNK]
    gj = gc_c.T[0:CHUNK, :]
    ri = _iota((CHUNK, CHUNK), 0)
    cj = _iota((CHUNK, CHUNK), 1)
    return jnp.where(ri >= cj, jnp.exp(jnp.minimum(gi - gj, 0.0)), 0.0), ri > cj


def _gdn_specs(t):
    col = lambda off: pl.BlockSpec((t, LANES), lambda h: (0, off + h))
    cw = lambda off: pl.BlockSpec((CONV_K, LANES), lambda h: (0, off + h))
    mat = pl.BlockSpec((1, t // CHUNK, CHUNK, CHUNK), lambda h: (h, 0, 0, 0))
    return col, cw, mat


def _gdn_prep(proj, convw, beta, g):
    t = proj.shape[0]
    nch = t // CHUNK

    def body(xq_ref, xk_ref, xv_ref, wq_ref, wk_ref, wv_ref, beta_ref, g_ref,
             qn_ref, kn_ref, cv_ref, gc_ref, be_ref, m_ref, a_ref):
        row = _iota((t, LANES), 0)
        hd = pl.program_id(0)
        be_ref[...] = jnp.broadcast_to(_lane_column(beta_ref[...], SM_GB + hd), (t, LANES))

        def act(x_ref, w_ref):
            y = _conv(x_ref[...], w_ref[...], row)
            return y * _sigmoid(y)

        cq = act(xq_ref, wq_ref)
        ck = act(xk_ref, wk_ref)
        cv_ref[...] = act(xv_ref, wv_ref)
        qn_ref[...] = cq * lax.rsqrt(jnp.sum(cq * cq, axis=-1, keepdims=True) + EPS) * GDN_QSCALE
        kn_ref[...] = ck * lax.rsqrt(jnp.sum(ck * ck, axis=-1, keepdims=True) + EPS)
        gc = jnp.broadcast_to(_lane_column(g_ref[...], SM_GA + hd), (t, LANES))
        pos = row % CHUNK
        step = 1
        while step < CHUNK:
            gc = gc + jnp.where(pos >= step, pltpu.roll(gc, step, 0), 0.0)
            step *= 2
        gc_ref[...] = gc

        group = 4 if nch % 4 == 0 else 1

        def chunks(gi, carry):
            ns = [gi * group + c for c in range(group)]
            sls = [pl.ds(pl.multiple_of(n * CHUNK, CHUNK), CHUNK) for n in ns]
            ks = [kn_ref[sl, :] for sl in sls]
            kk = [_mm_nt(k_c * be_ref[sl, :], k_c) for k_c, sl in zip(ks, sls)]
            qk = [_mm_nt(qn_ref[sl, :], k_c) for k_c, sl in zip(ks, sls)]
            for c, n in enumerate(ns):
                decay, strict = _chunk_decay(gc_ref[sls[c], :])
                m_ref[0, n] = jnp.where(strict, kk[c] * decay, 0.0)
                a_ref[0, n] = qk[c] * decay
            return carry

        lax.fori_loop(0, nch // group, chunks, 0)

    col, cw, mat = _gdn_specs(t)
    return pl.pallas_call(
        body, name="gdn_prep", grid=(N_GDN_HEADS,),
        in_specs=[col(12), col(16), col(20), cw(0), cw(4), cw(8)] + [pl.BlockSpec((t, LANES), lambda h: (0, 0))] * 2,
        out_specs=[col(0), col(0), col(0), col(0), col(0), mat, mat],
        out_shape=[_sds((t, D_GDN))] * 5 + [_sds((N_GDN_HEADS, nch, CHUNK, CHUNK))] * 2,
        compiler_params=_params("parallel"),
    )(proj, proj, proj, convw, convw, convw, beta, g)


def _tri_inverse(m3):
    assert m3.shape == (LANES, CHUNK, CHUNK)

    def body(m_ref, t_ref, ms, ts):
        for i in range(CHUNK):
            ms[i * CHUNK:(i + 1) * CHUNK, :] = m_ref[:, i, :].T
        cidx = _iota((CHUNK, LANES), 0)

        for i in range(CHUNK):
            def inner(jj, acc):
                mrow = ms[pl.ds(i * CHUNK + jj, 1), :]
                return acc - mrow * ts[pl.ds(pl.multiple_of(jj * CHUNK, CHUNK), CHUNK), :]

            acc = lax.fori_loop(0, i, inner, jnp.where(cidx == i, 1.0, 0.0).astype(F32), unroll=4)
            ts[i * CHUNK:(i + 1) * CHUNK, :] = acc
        for i in range(CHUNK):
            t_ref[:, i, :] = ts[i * CHUNK:(i + 1) * CHUNK, :].T

    return pl.pallas_call(
        body, name="tri_inverse", in_specs=[VMEM_SPEC], out_specs=VMEM_SPEC,
        out_shape=_sds((LANES, CHUNK, CHUNK)),
        scratch_shapes=[pltpu.VMEM((CHUNK * CHUNK, LANES), F32), pltpu.VMEM((CHUNK * CHUNK, LANES), F32)],
        compiler_params=_params(),
    )(m3)


def _gdn_chunk_terms(q, k, v, b, gcc):
    eg = jnp.exp(gcc)
    last = gcc[CHUNK - 1:CHUNK, :]
    egl = jnp.exp(last - gcc)
    gl = jnp.exp(last)
    kb = k * b
    return eg, egl, gl, kb, v * b, kb * eg, q * eg, k * egl


GDN_BLOCK_CHUNKS = 4


def _gdn_block_specs(t, reverse):
    cb = GDN_BLOCK_CHUNKS
    nb = t // (cb * CHUNK)
    idx = (lambda i: nb - 1 - i) if reverse else (lambda i: i)
    tok = pl.BlockSpec((cb * CHUNK, D_GDN), lambda i: (idx(i), 0))
    mat = pl.BlockSpec((N_GDN_HEADS, cb, CHUNK, CHUNK), lambda i: (0, idx(i), 0, 0))
    state = pl.BlockSpec((N_GDN_HEADS, cb, GDN_HEAD_DIM, GDN_HEAD_DIM), lambda i: (0, idx(i), 0, 0))
    return nb, tok, mat, state


def _gdn_scan(qn, kn, cv, be, gc, tinv, amat):
    t = qn.shape[0]
    nch = t // CHUNK

    def body(q_ref, k_ref, v_ref, b_ref, gc_ref, t_ref, a_ref, o_ref, sall_ref, vn_ref, s_scr):
        @pl.when(pl.program_id(0) == 0)
        def _():
            s_scr[...] = jnp.zeros_like(s_scr)

        heads = range(N_GDN_HEADS)
        cols = [slice(hd * LANES, (hd + 1) * LANES) for hd in heads]
        s = [s_scr[hd] for hd in heads]
        for cc in range(GDN_BLOCK_CHUNKS):
            rs = slice(cc * CHUNK, (cc + 1) * CHUNK)
            terms = [_gdn_chunk_terms(q_ref[rs, cs], k_ref[rs, cs], v_ref[rs, cs], b_ref[rs, cs], gc_ref[rs, cs])
                     for cs in cols]
            for hd in heads:
                sall_ref[hd, cc] = s[hd]
            uw = [_mm(t_ref[hd, cc], jnp.concatenate([terms[hd][4], terms[hd][5]], axis=1)) for hd in heads]
            ws_qs = [_mm(jnp.concatenate([uw[hd][:, LANES:], terms[hd][6]], axis=0), s[hd]) for hd in heads]
            vn = [uw[hd][:, :LANES] - ws_qs[hd][:CHUNK] for hd in heads]
            a_vn = [_mm(a_ref[hd, cc], vn[hd]) for hd in heads]
            kd_vn = [_mm_tn(terms[hd][7], vn[hd]) for hd in heads]
            for hd in heads:
                vn_ref[rs, cols[hd]] = vn[hd]
                o_ref[rs, cols[hd]] = ws_qs[hd][CHUNK:] + a_vn[hd]
                s[hd] = s[hd] * terms[hd][2] + kd_vn[hd]
        for hd in heads:
            s_scr[hd] = s[hd]

    nb, tok, mat, state = _gdn_block_specs(t, False)
    return pl.pallas_call(
        body, name="gdn_scan", grid=(nb,),
        in_specs=[tok] * 5 + [mat, mat], out_specs=[tok, state, tok],
        out_shape=[_sds((t, D_GDN)), _sds((N_GDN_HEADS, nch, GDN_HEAD_DIM, GDN_HEAD_DIM)), _sds((t, D_GDN))],
        scratch_shapes=[pltpu.VMEM((N_GDN_HEADS, GDN_HEAD_DIM, GDN_HEAD_DIM), F32)],
        compiler_params=_params("arbitrary"),
    )(qn, kn, cv, be, gc, tinv, amat)


def _gdn_bwd(qn, kn, cv, be, gc, tinv, amat, s_all, vn_all, do):
    t = qn.shape[0]

    def body(q_ref, k_ref, v_ref, b_ref, gc_ref, t_ref, a_ref, sall_ref, vn_ref, do_ref,
             dq_ref, dk_ref, dv_ref, db_ref, dg_ref, ds_scr):
        @pl.when(pl.program_id(0) == 0)
        def _():
            ds_scr[...] = jnp.zeros_like(ds_scr)

        lastrow = _iota((CHUNK, LANES), 0) == CHUNK - 1
        heads = range(N_GDN_HEADS)
        cols = [slice(hd * LANES, (hd + 1) * LANES) for hd in heads]
        each = lambda fn: [fn(hd) for hd in heads]
        rows_cat = lambda x, y: jnp.concatenate([x, y], axis=0)
        lane_cat = lambda x, y: jnp.concatenate([x, y], axis=1)
        dsp = each(lambda hd: ds_scr[hd])
        for cc in reversed(range(GDN_BLOCK_CHUNKS)):
            rs = slice(cc * CHUNK, (cc + 1) * CHUNK)
            q = each(lambda hd: q_ref[rs, cols[hd]])
            k = each(lambda hd: k_ref[rs, cols[hd]])
            v = each(lambda hd: v_ref[rs, cols[hd]])
            b = each(lambda hd: b_ref[rs, cols[hd]])
            gcc = each(lambda hd: gc_ref[rs, cols[hd]])
            do_c = each(lambda hd: do_ref[rs, cols[hd]])
            vn = each(lambda hd: vn_ref[rs, cols[hd]])
            tn = each(lambda hd: t_ref[hd, cc])
            st = each(lambda hd: sall_ref[hd, cc])
            terms = each(lambda hd: _gdn_chunk_terms(q[hd], k[hd], v[hd], b[hd], gcc[hd]))
            eg, egl, gl, kb, vb, kbg, qd, kd = [[terms[hd][i] for hd in heads] for i in range(8)]
            w = each(lambda hd: _mm(tn[hd], kbg[hd]))
            a_do = each(lambda hd: _mm_tn(a_ref[hd, cc], do_c[hd]))
            kd_ds = each(lambda hd: _mm(kd[hd], dsp[hd]))
            da = each(lambda hd: _mm_nt(do_c[hd], vn[hd]))
            dkd = each(lambda hd: _mm_nt(vn[hd], dsp[hd]))
            by_k = each(lambda hd: _mm_nt(rows_cat(kb[hd], q[hd]), k[hd]))
            dgl = each(lambda hd: jnp.sum(jnp.sum(dsp[hd] * st[hd], axis=-1, keepdims=True), axis=0, keepdims=True))
            dvn = each(lambda hd: a_do[hd] + kd_ds[hd])
            do_dvn = each(lambda hd: rows_cat(do_c[hd], dvn[hd]))
            by_s = each(lambda hd: _mm_nt(do_dvn[hd], st[hd]))
            dqd = each(lambda hd: by_s[hd][:CHUNK])
            dvn_dw = each(lambda hd: lane_cat(dvn[hd], -by_s[hd][CHUNK:]))
            dsp = each(lambda hd: _mm_tn(rows_cat(qd[hd], -w[hd]), do_dvn[hd]) + gl[hd] * dsp[hd])
            dt = each(lambda hd: _mm_nt(dvn_dw[hd], lane_cat(vb[hd], kbg[hd])))
            by_t = each(lambda hd: _mm_tn(tn[hd], dvn_dw[hd]))
            tt_dt = each(lambda hd: _mm_tn(tn[hd], dt[hd]))
            dm_raw = each(lambda hd: _mm_nt(tt_dt[hd], tn[hd]))
            masks = each(lambda hd: _chunk_decay(gcc[hd]))
            dkk = each(lambda hd: jnp.where(masks[hd][1], -dm_raw[hd], 0.0) * masks[hd][0])
            dqk = each(lambda hd: da[hd] * masks[hd][0])
            dqk_dkk = each(lambda hd: rows_cat(dqk[hd], dkk[hd]))
            on_k = each(lambda hd: _mm(dqk_dkk[hd], k[hd]))
            dk_mm = each(lambda hd: _mm_tn(dqk_dkk[hd], rows_cat(q[hd], kb[hd])))
            for hd in heads:
                cs = cols[hd]
                dvb, dkbg = by_t[hd][:, :LANES], by_t[hd][:, LANES:]
                gmat = dkk[hd] * by_k[hd][:CHUNK] + dqk[hd] * by_k[hd][CHUNK:]
                dq_ref[rs, cs] = dqd[hd] * eg[hd] + on_k[hd][:CHUNK]
                dkb = on_k[hd][CHUNK:] + dkbg * eg[hd]
                dk_ref[rs, cs] = dkd[hd] * egl[hd] + dk_mm[hd] + dkb * b[hd]
                db = jnp.sum(dkb * k[hd], axis=-1, keepdims=True) + jnp.sum(dvb * v[hd], axis=-1, keepdims=True)
                db_ref[rs, cs] = jnp.broadcast_to(db, (CHUNK, LANES))
                dv_ref[rs, cs] = dvb * b[hd]
                dkd_kd = jnp.sum(dkd[hd] * kd[hd], axis=-1, keepdims=True)
                col_sums = jnp.sum(lane_cat(gmat, jnp.zeros_like(gmat)).T, axis=-1, keepdims=True)
                dgc = (jnp.sum(gmat, axis=-1, keepdims=True) - col_sums[:CHUNK]
                       + jnp.sum(dqd[hd] * qd[hd], axis=-1, keepdims=True)
                       + jnp.sum(dkbg * kbg[hd], axis=-1, keepdims=True) - dkd_kd)
                extra = jnp.sum(dkd_kd, axis=0, keepdims=True) + dgl[hd] * gl[hd]
                dg_ref[rs, cs] = dgc + jnp.where(lastrow, extra, 0.0)
        for hd in heads:
            ds_scr[hd] = dsp[hd]
        dg = dg_ref[...]
        row = _iota(dg.shape, 0)
        pos = row % CHUNK
        step = 1
        while step < CHUNK:
            dg = dg + jnp.where(pos < CHUNK - step, pltpu.roll(dg, dg.shape[0] - step, 0), 0.0)
            step *= 2
        dg_ref[...] = dg

    nb, tok, mat, state = _gdn_block_specs(t, True)
    return pl.pallas_call(
        body, name="gdn_bwd", grid=(nb,),
        in_specs=[tok] * 5 + [mat, mat, state, tok, tok], out_specs=[tok] * 5, out_shape=[_sds((t, D_GDN))] * 5,
        scratch_shapes=[pltpu.VMEM((N_GDN_HEADS, GDN_HEAD_DIM, GDN_HEAD_DIM), F32)],
        compiler_params=_params("arbitrary"),
    )(qn, kn, cv, be, gc, tinv, amat, s_all, vn_all, do)


def _gdn_bwd_conv(proj, convw, dqn, dkn, dcv):
    t = proj.shape[0]

    def body(xq_ref, xk_ref, xv_ref, wq_ref, wk_ref, wv_ref, dq_ref, dk_ref, dv_ref,
             dxq_ref, dxk_ref, dxv_ref, dwq_ref, dwk_ref, dwv_ref):
        row = _iota((t, LANES), 0)

        def one(x_ref, w_ref, d_ref, dx_ref, dw_ref, scale):
            x = x_ref[...]
            w = w_ref[...]
            y = _conv(x, w, row)
            sg = _sigmoid(y)
            dc = d_ref[...]
            if scale is not None:
                c = y * sg
                r = lax.rsqrt(jnp.sum(c * c, axis=-1, keepdims=True) + EPS)
                ch = c * r
                dc = scale * r * (dc - ch * jnp.sum(dc * ch, axis=-1, keepdims=True))
            dy = dc * sg * (1.0 + y * (1.0 - sg))
            dx_ref[...] = (w[3:4, :] * dy + w[2:3, :] * _shift_up(dy, 1, row) + w[1:2, :] * _shift_up(dy, 2, row)
                           + w[0:1, :] * _shift_up(dy, 3, row)).astype(BF)
            for jj in range(CONV_K):
                xs = x if jj == CONV_K - 1 else _shift_down(x, CONV_K - 1 - jj, row)
                dw_ref[jj:jj + 1, :] = jnp.sum(dy * xs, axis=0, keepdims=True)

        one(xq_ref, wq_ref, dq_ref, dxq_ref, dwq_ref, GDN_QSCALE)
        one(xk_ref, wk_ref, dk_ref, dxk_ref, dwk_ref, 1.0)
        one(xv_ref, wv_ref, dv_ref, dxv_ref, dwv_ref, None)

    col, cw, _ = _gdn_specs(t)
    return pl.pallas_call(
        body, name="gdn_bwd_conv", grid=(N_GDN_HEADS,),
        in_specs=[col(12), col(16), col(20), cw(0), cw(4), cw(8), col(0), col(0), col(0)],
        out_specs=[col(0), col(0), col(0), cw(0), cw(0), cw(0)],
        out_shape=[_sds((t, D_GDN), BF)] * 3 + [_sds((CONV_K, D_GDN))] * 3,
        compiler_params=_params("parallel"),
    )(proj, proj, proj, convw, convw, convw, dqn, dkn, dcv)


def _mix_out(fox_n, gdn_o, proj, gnw, w_out, x, pmw, plw):
    t = x.shape[0]
    tm = min(MATMUL_BLOCK, t)

    def body(fn_ref, go_ref, gz_ref, gnw_ref, w_ref, x_ref, pmw_ref, plw_ref, x1_ref, h2_ref, mixed_ref, omix_ref,
             h2t_ref):
        omix_ref[:, 0:D_FOX] = fn_ref[...]
        for hd in range(N_GDN_HEADS):
            cs = slice(hd * LANES, (hd + 1) * LANES)
            go = go_ref[:, cs]
            r = lax.rsqrt(jnp.mean(go * go, axis=-1, keepdims=True) + EPS)
            gz = gz_ref[:, cs]
            omix_ref[:, D_FOX + hd * LANES:D_FOX + (hd + 1) * LANES] = (
                go * r * gnw_ref[...] * (gz * _sigmoid(gz))).astype(BF)
        mixed = jnp.dot(omix_ref[...], w_ref[...], preferred_element_type=F32)
        mixed_ref[...] = mixed
        r2 = lax.rsqrt(jnp.mean(mixed * mixed, axis=-1, keepdims=True) + EPS)
        x1 = x_ref[...] + mixed * r2 * pmw_ref[...]
        x1_ref[...] = x1
        r3 = lax.rsqrt(jnp.mean(x1 * x1, axis=-1, keepdims=True) + EPS)
        h2 = x1 * r3 * plw_ref[...]
        h2_ref[...] = h2.astype(BF)
        h2t_ref[...] = h2.T.astype(BF)

    tok = lambda w: pl.BlockSpec((tm, w), lambda i: (i, 0))
    vec = lambda w: pl.BlockSpec((1, w), lambda i: (0, 0))
    return pl.pallas_call(
        body, name="mix_out", grid=(t // tm,),
        in_specs=[tok(D_FOX), tok(D_GDN), pl.BlockSpec((tm, D_GDN), lambda i: (i, COL_GZ // D_GDN)), vec(LANES),
                  pl.BlockSpec((D_MODEL, D_MODEL), lambda i: (0, 0)), tok(D_MODEL), vec(D_MODEL), vec(D_MODEL)],
        out_specs=[tok(D_MODEL)] * 4 + [pl.BlockSpec((D_MODEL, tm), lambda i: (0, i))],
        out_shape=[_sds((t, D_MODEL)), _sds((t, D_MODEL), BF), _sds((t, D_MODEL)), _sds((t, D_MODEL), BF),
                   _sds((D_MODEL, t), BF)],
        compiler_params=_params("parallel"),
    )(fox_n, gdn_o, proj, gnw, w_out, x, pmw, plw)


def _out_bwd(dmixed, w_out, o_fox, gdn_o, proj, fnw, gnw):
    t = dmixed.shape[0]
    tm = min(MATMUL_BLOCK, t)

    def body(dm_ref, w_ref, of_ref, go_ref, gz_ref, fnw_ref, gnw_ref, dof_ref, dgo_ref, dgz_ref, dfw_ref, dgw_ref):
        i = pl.program_id(0)

        @pl.when(i == 0)
        def _():
            dfw_ref[...] = jnp.zeros_like(dfw_ref)
            dgw_ref[...] = jnp.zeros_like(dgw_ref)

        domix = _mm_nt(dm_ref[...], w_ref[...])
        first = _iota((1, LANES), 1) < FOX_HEAD_DIM
        dfw = jnp.zeros((1, LANES), F32)
        dgw = jnp.zeros((1, LANES), F32)
        for pr in range(N_FOX_HEADS // 2):
            cs = slice(pr * LANES, (pr + 1) * LANES)
            o = of_ref[:, cs]
            dfn = domix[:, cs]
            o2 = o * o
            s0 = jnp.sum(jnp.where(first, o2, 0.0), axis=-1, keepdims=True)
            s1 = jnp.sum(jnp.where(first, 0.0, o2), axis=-1, keepdims=True)
            r = lax.rsqrt(jnp.where(first, s0, s1) * (1.0 / FOX_HEAD_DIM) + EPS)
            oh = o * r
            dfw = dfw + jnp.sum(dfn * oh, axis=0, keepdims=True)
            doh = dfn * fnw_ref[...]
            pr_ = doh * oh
            m0 = jnp.sum(jnp.where(first, pr_, 0.0), axis=-1, keepdims=True)
            m1 = jnp.sum(jnp.where(first, 0.0, pr_), axis=-1, keepdims=True)
            dof_ref[:, cs] = r * (doh - oh * jnp.where(first, m0, m1) * (1.0 / FOX_HEAD_DIM))
        for hd in range(N_GDN_HEADS):
            cs = slice(hd * LANES, (hd + 1) * LANES)
            go = go_ref[:, cs]
            gz = gz_ref[:, cs]
            dgated = domix[:, D_FOX + hd * LANES:D_FOX + (hd + 1) * LANES]
            r = lax.rsqrt(jnp.mean(go * go, axis=-1, keepdims=True) + EPS)
            goh = go * r
            sg = _sigmoid(gz)
            sz = gz * sg
            gn = goh * gnw_ref[...]
            dgn = dgated * sz
            dgz_ref[:, cs] = (dgated * gn * sg * (1.0 + gz * (1.0 - sg))).astype(BF)
            dgw = dgw + jnp.sum(dgn * goh, axis=0, keepdims=True)
            dgh = dgn * gnw_ref[...]
            dgo_ref[:, cs] = r * (dgh - goh * jnp.mean(dgh * goh, axis=-1, keepdims=True))
        dfw_ref[...] += dfw + pltpu.roll(dfw, FOX_HEAD_DIM, 1)
        dgw_ref[...] += dgw

    tok = lambda w: pl.BlockSpec((tm, w), lambda i: (i, 0))
    vec = lambda w: pl.BlockSpec((1, w), lambda i: (0, 0))
    return pl.pallas_call(
        body, name="out_bwd", grid=(t // tm,),
        in_specs=[tok(D_MODEL), pl.BlockSpec((D_MODEL, D_MODEL), lambda i: (0, 0)), tok(D_FOX), tok(D_GDN),
                  pl.BlockSpec((tm, D_GDN), lambda i: (i, COL_GZ // D_GDN)), vec(LANES), vec(LANES)],
        out_specs=[tok(D_FOX), tok(D_GDN), tok(D_GDN), vec(LANES), vec(LANES)],
        out_shape=[_sds((t, D_FOX)), _sds((t, D_GDN)), _sds((t, D_GDN), BF), _sds((1, LANES)), _sds((1, LANES))],
        compiler_params=_params("arbitrary"),
    )(dmixed, w_out, o_fox, gdn_o, proj, fnw, gnw)


def _mlp_up(h2, w_upt):
    t = h2.shape[0]
    tm = min(MATMUL_BLOCK, t)

    def body(h_ref, w_ref, up_ref):
        up_ref[...] = lax.dot_general(h_ref[...], w_ref[...], (((1,), (1,)), ((), ())),
                                      preferred_element_type=F32).astype(BF)

    return pl.pallas_call(
        body, name="mlp_up", grid=(t // tm,),
        in_specs=[pl.BlockSpec((tm, D_MODEL), lambda i: (i, 0)), pl.BlockSpec((D_FF, D_MODEL), lambda i: (0, 0))],
        out_specs=pl.BlockSpec((tm, D_FF), lambda i: (i, 0)), out_shape=_sds((t, D_FF), BF),
        compiler_params=_params("parallel"),
    )(h2, w_upt)


def _mlp_down_loss(up, w_down, x1, pw, target):
    t = up.shape[0]
    tm = min(MATMUL_BLOCK, t)

    def body(up_ref, w_ref, x1_ref, pw_ref, tg_ref, dy_ref, dx2_ref, loss_ref, dpw_ref):
        i = pl.program_id(0)

        @pl.when(i == 0)
        def _():
            loss_ref[...] = jnp.zeros_like(loss_ref)
            dpw_ref[...] = jnp.zeros_like(dpw_ref)

        u = jnp.maximum(up_ref[...].astype(F32), 0.0)
        y = jnp.dot((u * u).astype(BF), w_ref[...], preferred_element_type=F32)
        r = lax.rsqrt(jnp.mean(y * y, axis=-1, keepdims=True) + EPS)
        yh = y * r
        pw = pw_ref[...]
        err = x1_ref[...] + yh * pw - tg_ref[...]
        part = jnp.sum(jnp.sum(err * err, axis=-1, keepdims=True), axis=0, keepdims=True) * (0.5 / D_MODEL)
        loss_ref[...] += jnp.broadcast_to(part, loss_ref.shape)
        dx2 = err * (1.0 / D_MODEL)
        dx2_ref[...] = dx2
        dpw_ref[...] += jnp.sum(dx2 * yh, axis=0, keepdims=True)
        dyh = dx2 * pw
        dy_ref[...] = (r * (dyh - yh * jnp.mean(dyh * yh, axis=-1, keepdims=True))).astype(BF)

    tok = lambda w: pl.BlockSpec((tm, w), lambda i: (i, 0))
    vec = lambda w: pl.BlockSpec((1, w), lambda i: (0, 0))
    return pl.pallas_call(
        body, name="mlp_down_loss", grid=(t // tm,),
        in_specs=[tok(D_FF), pl.BlockSpec((D_FF, D_MODEL), lambda i: (0, 0)), tok(D_MODEL), vec(D_MODEL), tok(D_MODEL)],
        out_specs=[tok(D_MODEL), tok(D_MODEL), vec(LANES), vec(D_MODEL)],
        out_shape=[_sds((t, D_MODEL), BF), _sds((t, D_MODEL)), _sds((1, LANES)), _sds((1, D_MODEL))],
        compiler_params=_params("arbitrary"),
    )(up, w_down, x1, pw, target)


def _mlp_bwd_act(dy, w_down, up):
    t = dy.shape[0]
    tm = min(MATMUL_BLOCK, t)

    def body(dy_ref, w_ref, up_ref, dup_ref):
        da = lax.dot_general(dy_ref[...], w_ref[...], (((1,), (1,)), ((), ())), preferred_element_type=F32)
        dup_ref[...] = (da * (2.0 * jnp.maximum(up_ref[...].astype(F32), 0.0))).astype(BF)

    return pl.pallas_call(
        body, name="mlp_bwd_act", grid=(t // tm,),
        in_specs=[pl.BlockSpec((tm, D_MODEL), lambda i: (i, 0)), pl.BlockSpec((D_FF, D_MODEL), lambda i: (0, 0)),
                  pl.BlockSpec((tm, D_FF), lambda i: (i, 0))],
        out_specs=pl.BlockSpec((tm, D_FF), lambda i: (i, 0)), out_shape=_sds((t, D_FF), BF),
        compiler_params=_params("parallel"),
    )(dy, w_down, up)


def _mlp_bwd_in(dup, w_up, x1, plw, dx2, mixed, pmw):
    t = dup.shape[0]
    tm = min(MATMUL_BLOCK, t)

    def body(dup_ref, w_ref, x1_ref, plw_ref, dx2_ref, mx_ref, pmw_ref, dx1_ref, dmixed_ref, dplw_ref, dpmw_ref):
        i = pl.program_id(0)

        @pl.when(i == 0)
        def _():
            dplw_ref[...] = jnp.zeros_like(dplw_ref)
            dpmw_ref[...] = jnp.zeros_like(dpmw_ref)

        dh = jnp.dot(dup_ref[...], w_ref[...], preferred_element_type=F32)
        x1 = x1_ref[...]
        r = lax.rsqrt(jnp.mean(x1 * x1, axis=-1, keepdims=True) + EPS)
        xh = x1 * r
        dplw_ref[...] += jnp.sum(dh * xh, axis=0, keepdims=True)
        dxh = dh * plw_ref[...]
        dx1 = dx2_ref[...] + r * (dxh - xh * jnp.mean(dxh * xh, axis=-1, keepdims=True))
        dx1_ref[...] = dx1
        mx = mx_ref[...]
        r2 = lax.rsqrt(jnp.mean(mx * mx, axis=-1, keepdims=True) + EPS)
        mh = mx * r2
        dpmw_ref[...] += jnp.sum(dx1 * mh, axis=0, keepdims=True)
        dmh = dx1 * pmw_ref[...]
        dmixed_ref[...] = (r2 * (dmh - mh * jnp.mean(dmh * mh, axis=-1, keepdims=True))).astype(BF)

    tok = lambda w: pl.BlockSpec((tm, w), lambda i: (i, 0))
    vec = lambda w: pl.BlockSpec((1, w), lambda i: (0, 0))
    return pl.pallas_call(
        body, name="mlp_bwd_in", grid=(t // tm,),
        in_specs=[tok(D_FF), pl.BlockSpec((D_FF, D_MODEL), lambda i: (0, 0)), tok(D_MODEL),
                  vec(D_MODEL), tok(D_MODEL), tok(D_MODEL), vec(D_MODEL)],
        out_specs=[tok(D_MODEL), tok(D_MODEL), vec(D_MODEL), vec(D_MODEL)],
        out_shape=[_sds((t, D_MODEL)), _sds((t, D_MODEL), BF), _sds((1, D_MODEL)), _sds((1, D_MODEL))],
        compiler_params=_params("arbitrary"),
    )(dup, w_up, x1, plw, dx2, mixed, pmw)


def _wgrad(a, b, a_cols, split=1, a_fn=None, a_block0=0, name="wgrad"):
    t, b_cols = b.shape
    n_a = (a.shape[1] - a_block0 * a_cols) // a_cols if a_block0 else a.shape[1] // a_cols

    def body(a_ref, b_ref, o_ref):
        av = a_ref[...]
        if a_fn is not None:
            av = a_fn(av)
        o_ref[...] = _mm_tn(av, b_ref[...]).astype(BF).reshape(o_ref.shape)

    return pl.pallas_call(
        body, name=name, grid=(n_a,),
        in_specs=[pl.BlockSpec((t, a_cols), lambda i: (0, i + a_block0)), pl.BlockSpec((t, b_cols), lambda i: (0, 0))],
        out_specs=pl.BlockSpec((split, a_cols // split, b_cols), lambda i: (i, 0, 0)),
        out_shape=_sds((n_a * split, a_cols // split, b_cols), BF),
        compiler_params=_params("parallel"),
    )(a, b)


def _wgrad_pre_t(at, b, b_cols, name):
    rows, t = at.shape
    n_b = b.shape[1] // b_cols

    def body(a_ref, b_ref, o_ref):
        o_ref[0] = jnp.dot(a_ref[...], b_ref[...], preferred_element_type=F32).astype(BF)

    return pl.pallas_call(
        body, name=name, grid=(n_b,),
        in_specs=[pl.BlockSpec((rows, t), lambda j: (0, 0)), pl.BlockSpec((t, b_cols), lambda j: (0, j))],
        out_specs=pl.BlockSpec((1, rows, b_cols), lambda j: (j, 0, 0)), out_shape=_sds((n_b, rows, b_cols), BF),
        compiler_params=_params("parallel"),
    )(at, b)


def _small_bwd(proj, fb, al, dtb, dcq, dckt, dbe, dge):
    t = proj.shape[0]

    def body(sm_ref, fb_ref, al_ref, dtb_ref, dcq_ref, dckt_ref, dbe_ref, dge_ref, dsm_ref, dvec_ref):
        s = sm_ref[...]
        lane = _iota((1, LANES), 1)
        dcum = dcq_ref[...] - dckt_ref[...].T
        row = _iota((t, LANES), 0)
        step = 1
        while step < t:
            dcum = dcum + _shift_up(dcum, step, row)
            step *= 2
        dff = dcum * _sigmoid(-(s + fb_ref[...]))
        dbeta = jnp.zeros((t, LANES), F32)
        dg = jnp.zeros((t, LANES), F32)
        for hd in range(N_GDN_HEADS):
            dbeta = jnp.where(lane == SM_GB + hd, dbe_ref[:, hd * LANES:hd * LANES + 1], dbeta)
            dg = jnp.where(lane == SM_GA + hd, dge_ref[:, hd * LANES:hd * LANES + 1], dg)
        beta = _sigmoid(s)
        dgb = dbeta * beta * (1.0 - beta)
        za = s + dtb_ref[...]
        nea = -jnp.exp(al_ref[...])
        dga = dg * nea * _sigmoid(za)
        is_f = lane < SM_GB
        is_b = (lane >= SM_GB) & (lane < SM_GA)
        is_a = (lane >= SM_GA) & (lane < SM_GA + 4)
        dsm_ref[...] = jnp.where(is_f, dff, jnp.where(is_b, dgb, jnp.where(is_a, dga, 0.0))).astype(BF)
        dvec_ref[...] = jnp.zeros_like(dvec_ref)
        dvec_ref[0:1, :] = jnp.sum(jnp.where(is_f, dff, 0.0), axis=0, keepdims=True)
        dvec_ref[1:2, :] = jnp.sum(jnp.where(is_a, dg * nea * _softplus(za), 0.0), axis=0, keepdims=True)
        dvec_ref[2:3, :] = jnp.sum(jnp.where(is_a, dga, 0.0), axis=0, keepdims=True)

    vec = pl.BlockSpec((1, LANES), lambda i: (0, 0))
    full = lambda r, c: pl.BlockSpec((r, c), lambda i: (0, 0))
    return pl.pallas_call(
        body, name="small_bwd", grid=(1,),
        in_specs=[pl.BlockSpec((t, LANES), lambda i: (0, COL_SMALL // LANES)), vec, vec, vec, full(t, LANES),
                  full(LANES, t), full(t, 512), full(t, 512)],
        out_specs=[full(t, LANES), full(8, LANES)], out_shape=[_sds((t, LANES), BF), _sds((8, LANES))],
        compiler_params=_params("arbitrary"),
    )(proj, fb, al, dtb, dcq, dckt, dbe, dge)


def _pack_dproj(dfox, dgdn, dgz, dsm):
    t = dgz.shape[0]
    tm = min(MATMUL_BLOCK, t)

    def body(*refs):
        parts, dp_ref = refs[:8], refs[8]
        col = 0
        for part in parts:
            width = part.shape[1]
            dp_ref[:, col:col + width] = part[...].astype(BF)
            col += width

    tok = lambda w: pl.BlockSpec((tm, w), lambda i: (i, 0))
    return pl.pallas_call(
        body, name="pack_dproj", grid=(t // tm,), in_specs=[tok(D_FOX)] * 3 + [tok(D_GDN)] * 4 + [tok(LANES)],
        out_specs=tok(PROJ_W), out_shape=_sds((t, PROJ_W), BF), compiler_params=_params("parallel"),
    )(*dfox, *dgdn, dgz, dsm)


def _in_bwd(dproj, wt_al, x, nw, dx1):
    t = x.shape[0]
    tm = min(MATMUL_BLOCK, t)

    def body(dp_ref, w_ref, x_ref, nw_ref, dx1_ref, dx_ref, dnw_ref):
        i = pl.program_id(0)

        @pl.when(i == 0)
        def _():
            dnw_ref[...] = jnp.zeros_like(dnw_ref)

        dh = jnp.dot(dp_ref[...], w_ref[...], preferred_element_type=F32)
        xv = x_ref[...]
        r = lax.rsqrt(jnp.mean(xv * xv, axis=-1, keepdims=True) + EPS)
        xh = xv * r
        dnw_ref[...] += jnp.sum(dh * xh, axis=0, keepdims=True)
        dxh = dh * nw_ref[...]
        dx_ref[...] = dx1_ref[...] + r * (dxh - xh * jnp.mean(dxh * xh, axis=-1, keepdims=True))

    tok = lambda w: pl.BlockSpec((tm, w), lambda i: (i, 0))
    vec = lambda w: pl.BlockSpec((1, w), lambda i: (0, 0))
    return pl.pallas_call(
        body, name="in_bwd", grid=(t // tm,),
        in_specs=[tok(PROJ_W), pl.BlockSpec((PROJ_W, D_MODEL), lambda i: (0, 0)), tok(D_MODEL), vec(D_MODEL),
                  tok(D_MODEL)],
        out_specs=[tok(D_MODEL), vec(D_MODEL)], out_shape=[_sds((t, D_MODEL)), _sds((1, D_MODEL))],
        compiler_params=_params("arbitrary"),
    )(dproj, wt_al, x, nw, dx1)


def _row(v, width=None):
    v = v.reshape(1, -1).astype(F32)
    if width is not None and v.shape[1] < width:
        v = jnp.pad(v, ((0, 0), (0, width - v.shape[1])))
    return v


def _lane_vec(v, first):
    return jnp.pad(v.astype(F32), (first, LANES - first - v.shape[0])).reshape(1, LANES)


def _local_step(x, target, wt_al, late_weights, on_grads, convw, pre_mix_norm, fox_f_bias, fox_out_norm,
                gdn_a_log, gdn_dt_bias, gdn_out_norm, post_mix_norm, pre_mlp_norm, post_mlp_norm):
    t = x.shape[0]
    nch = t // CHUNK
    nw, pmw, plw, pw = _row(pre_mix_norm), _row(post_mix_norm), _row(pre_mlp_norm), _row(post_mlp_norm)
    fb, al, dtb = _lane_vec(fox_f_bias, SM_FF), _lane_vec(gdn_a_log, SM_GA), _lane_vec(gdn_dt_bias, SM_GA)
    fnw = _row(jnp.tile(fox_out_norm, 2))
    gnw = _row(gdn_out_norm)

    proj, h = _norm_proj(x, nw, wt_al)
    cumt, beta, g = _small_prep(proj, fb, al, dtb)
    o_fox, lse, fox_n = _fox_fwd(proj, cumt, fnw)
    qn, kn, cv, gc, be, mmat, amat = _gdn_prep(proj, convw, beta, g)
    n_prob = N_GDN_HEADS * nch
    m3 = mmat.reshape(n_prob, CHUNK, CHUNK)
    if n_prob < LANES:
        m3 = jnp.pad(m3, ((0, LANES - n_prob), (0, 0), (0, 0)))
    tinv = _tri_inverse(m3)[:n_prob].reshape(N_GDN_HEADS, nch, CHUNK, CHUNK)
    token = late_weights("mlp_relay", tinv)
    gdn_o, s_all, vn_all = _gdn_scan(qn, kn, cv, be, gc, tinv, amat)
    w_out = late_weights("w_out", gdn_o)
    x1, h2, mixed, omix, h2t = _mix_out(fox_n, gdn_o, proj, gnw + token[0:1, 0:1], w_out, x, pmw, plw)
    w_up, w_down = late_weights("mlp", h2)
    up = _mlp_up(h2, w_up)
    dy, dx2, loss, d_pw = _mlp_down_loss(up, w_down, x1, pw, target)

    dup = _mlp_bwd_act(dy, w_down, up)
    relu2 = lambda u: jnp.square(jnp.maximum(u.astype(F32), 0.0))
    g_down = _wgrad(up, dy, D_FF // N_DEV, a_fn=relu2, name="wgrad_down")
    g_up = _wgrad_pre_t(h2t, dup, D_FF // N_DEV, name="wgrad_up")
    token = on_grads("mlp", (g_up, g_down))
    dx1, dmixed, d_plw, d_pmw = _mlp_bwd_in(dup, w_up, x1, plw + token[0:1, 0:1], dx2, mixed, pmw)
    token = on_grads("w_out", _wgrad(omix, dmixed, 512, split=4, name="wgrad_out"))
    do_fox, dgo, dgz, d_fnw, d_gnw = _out_bwd(dmixed, w_out, o_fox, gdn_o, proj, fnw + token[0:1, 0:1], gnw)
    dfq, dfk, dfv, dcq, dckt = _fox_bwd(proj, cumt, lse, o_fox, do_fox)
    dqn, dkn, dcv, dbe, dge = _gdn_bwd(qn, kn, cv, be, gc, tinv, amat, s_all, vn_all, dgo)
    dxq, dxk, dxv, dwq, dwk, dwv = _gdn_bwd_conv(proj, convw, dqn, dkn, dcv)
    dsm, dvec = _small_bwd(proj, fb, al, dtb, dcq, dckt, dbe, dge)
    dproj = _pack_dproj((dfq, dfk, dfv), (dxq, dxk, dxv), dgz, dsm)
    g_main = _wgrad(dproj, h, 512, name="wgrad_in")
    g_tail = _wgrad(dproj, h, LANES, a_block0=COL_SMALL // LANES, name="wgrad_in_small")
    token = on_grads("w_in", jnp.concatenate([g_main.reshape(COL_SMALL, D_MODEL), g_tail[0]]))
    grad_x, d_nw = _in_bwd(dproj, wt_al, x, nw + token[0:1, 0:1], dx1)
    small = dict(norms=(d_nw, d_pmw, d_plw, d_pw), fox_out_norm=d_fnw, gdn_out_norm=d_gnw, loss=loss, vectors=dvec,
                 conv=(dwq, dwk, dwv))
    return grad_x, small


MESH_IDS = pl.DeviceIdType.MESH
CHIP_FLIPS = ((0, 0), (1, 0), (0, 1), (1, 1))
ANY_SPEC = pl.BlockSpec(memory_space=pl.ANY)


def _place():
    return lax.axis_index("x"), lax.axis_index("y"), lax.axis_index("c")


def _all_gather(blocks):
    n = len(blocks)

    def body(*refs):
        ins, outs, (send_sems, recv_sems, local_sems) = refs[:n], refs[n:2 * n], refs[2 * n:]
        x, y, c = _place()
        sibling = (x, y, 1 - c)
        chips = [(x ^ fx, y ^ fy) for fx, fy in CHIP_FLIPS[1:]]

        def slot(out, px, py, pc):
            return out.at[4 * px + 2 * py + pc]

        def copy(a, k, block, to, src=None):
            return pltpu.make_async_remote_copy(
                src_ref=slot(outs[a], *block) if src is None else src, dst_ref=slot(outs[a], *block),
                send_sem=send_sems.at[a, k], recv_sem=recv_sems.at[a, k], device_id=to, device_id_type=MESH_IDS)

        pending = []
        for a in range(n):
            mine = pltpu.make_async_copy(ins[a], slot(outs[a], x, y, c), local_sems.at[a])
            mine.start()
            pending.append(mine)
        sends = []
        for a in range(n):
            first = [copy(a, 0, (x, y, c), sibling, src=ins[a])]
            first += [copy(a, 1 + j, (x, y, c), (*chip, c), src=ins[a]) for j, chip in enumerate(chips)]
            for cp in first:
                cp.start()
            sends += first
        for a in range(n):
            for j, chip in enumerate(chips):
                copy(a, 1 + j, (*chip, c), (x, y, c)).wait_recv()
                fwd = copy(a, 4 + j, (*chip, c), sibling)
                fwd.start()
                sends.append(fwd)
        for a in range(n):
            copy(a, 0, sibling, (x, y, c)).wait_recv()
            for j, chip in enumerate(chips):
                copy(a, 4 + j, (*chip, 1 - c), (x, y, c)).wait_recv()
        for cp in sends:
            cp.wait_send()
        for cp in pending:
            cp.wait()

    return pl.pallas_call(
        body, name="all_gather_weights", in_specs=[ANY_SPEC] * n, out_specs=[ANY_SPEC] * n,
        out_shape=[_sds((N_DEV,) + b.shape, b.dtype) for b in blocks],
        scratch_shapes=[pltpu.SemaphoreType.DMA((n, 7)), pltpu.SemaphoreType.DMA((n, 7)), pltpu.SemaphoreType.DMA((n,))],
        compiler_params=pltpu.CompilerParams(has_side_effects=True),
    )(*blocks)


def _adamw(w, g, m, v):
    m = ADAM_B1 * m + (1.0 - ADAM_B1) * g
    v = ADAM_B2 * v + (1.0 - ADAM_B2) * (g * g)
    m_hat = m / (1.0 - ADAM_B1 ** ADAM_STEP)
    v_hat = v / (1.0 - ADAM_B2 ** ADAM_STEP)
    return -ADAM_LR * (m_hat / (jnp.sqrt(v_hat) + ADAM_EPS) + ADAM_WD * w), m, v


def _pair_reduce(g, name):
    _, r, c_ = g.shape
    n = len(CHIP_FLIPS)

    def body(g_ref, out_ref, sib_buf, send_sems, recv_sems):
        x, y, c = _place()
        chips = [(x ^ fx, y ^ fy) for fx, fy in CHIP_FLIPS]
        piece = lambda chip, core: g_ref.at[4 * chip[0] + 2 * chip[1] + core]
        copies = [pltpu.make_async_remote_copy(
            src_ref=piece(chip, 1 - c), dst_ref=sib_buf.at[j], send_sem=send_sems.at[j], recv_sem=recv_sems.at[j],
            device_id=(x, y, 1 - c), device_id_type=MESH_IDS) for j, chip in enumerate(chips)]
        for cp in copies:
            cp.start()
        for j, chip in enumerate(chips):
            copies[j].wait_recv()
            out_ref[j] = (piece(chip, c)[...].astype(F32) + sib_buf[j].astype(F32)).astype(BF)
        for cp in copies:
            cp.wait_send()

    return pl.pallas_call(
        body, name=name, in_specs=[VMEM_SPEC], out_specs=VMEM_SPEC, out_shape=_sds((n, r, c_), BF),
        scratch_shapes=[pltpu.VMEM((n, r, c_), BF), pltpu.SemaphoreType.DMA((n,)), pltpu.SemaphoreType.DMA((n,))],
        compiler_params=pltpu.CompilerParams(vmem_limit_bytes=VMEM_LIMIT, has_side_effects=True),
    )(g)


HBM_SPEC = pl.BlockSpec(memory_space=pltpu.HBM)
SEM_SPEC = pl.BlockSpec(memory_space=pltpu.SEMAPHORE)
DATAFLOW = pltpu.SideEffectType.DATAFLOW_SIDE_EFFECTING


def _peers():
    x, y, c = _place()
    return 4 * x + 2 * y + c, [(x ^ (k >> 2), y ^ ((k >> 1) & 1), c ^ (k & 1)) for k in range(1, N_DEV)]


def _peer_index(peer):
    return 4 * peer[0] + 2 * peer[1] + peer[2]


def _zones_with_own(srcs, pieces, name, after=None, dtype=None, chips=False):
    n = len(srcs)
    slots = len(CHIP_FLIPS) if chips else N_DEV
    extra = [] if after is None else [after]
    dtypes = [s_.dtype if pieces or dtype is None else dtype for s_ in srcs]

    def body(me_ref, *refs):
        outs = refs[n + len(extra):]
        for a in range(n):
            if pieces:
                outs[a][0] = refs[a][0]
            else:
                val = refs[a][...].astype(dtypes[a])
                outs[a][0] = val
                outs[n + a][...] = val

    shapes = [s_.shape[1:] if pieces else s_.shape for s_ in srcs]
    mine = lambda sh: pl.BlockSpec((1,) + sh, lambda i, me_ref: (me_ref[0], 0, 0))
    whole = lambda sh: pl.BlockSpec(sh, lambda i, me_ref: (0, 0))
    in_specs = [mine(sh) if pieces else whole(sh) for sh in shapes]
    out_specs = [mine(sh) for sh in shapes] + ([] if pieces else [whole(sh) for sh in shapes])
    out_shape = [_sds((slots,) + sh, dt) for sh, dt in zip(shapes, dtypes)]
    out_shape += [] if pieces else [_sds(sh, dt) for sh, dt in zip(shapes, dtypes)]
    x, y, c = _place()
    own = 0 * x if chips else 4 * x + 2 * y + c
    out = pl.pallas_call(
        body, name=name,
        grid_spec=pltpu.PrefetchScalarGridSpec(num_scalar_prefetch=1, grid=(1,), in_specs=in_specs + [ANY_SPEC] * len(extra),
                                               out_specs=out_specs),
        out_shape=out_shape, compiler_params=_params("arbitrary"),
    )(own.astype(jnp.int32).reshape(1), *srcs, *extra)
    return out[:n], (list(srcs) if pieces else out[n:])


def _exchange_start(srcs, zones, pieces, name, chips=False):
    n = len(srcs)

    def body(*refs):
        ins, zs = refs[:n], refs[n:2 * n]
        sems = refs[2 * n:4 * n]
        token = refs[-1]
        me, peers = _peers()
        x, y, c = _place()
        if chips and pieces:
            routes = [((x ^ fx, y ^ fy, c), j, j) for j, (fx, fy) in enumerate(CHIP_FLIPS) if j]
        elif chips:
            routes = [((x ^ fx, y ^ fy, c), None, me) for fx, fy in CHIP_FLIPS[1:]]
        else:
            routes = [(peer, _peer_index(peer) if pieces else None, me) for peer in peers]
        for peer, src_slot, dst_slot in routes:
            for a in range(n):
                pltpu.make_async_remote_copy(
                    src_ref=ins[a] if src_slot is None else ins[a].at[src_slot], dst_ref=zs[a].at[dst_slot],
                    send_sem=sems[2 * a], recv_sem=sems[2 * a + 1], device_id=peer, device_id_type=MESH_IDS).start()
        token[...] = jnp.zeros_like(token)

    hbm = lambda v: pltpu.with_memory_space_constraint(v, pltpu.HBM)
    out = pl.pallas_call(
        body, name=name,
        out_shape=tuple([pltpu.SemaphoreType.DMA(())] * (2 * n) + [pltpu.HBM(v.shape, v.dtype) for v in srcs]
                        + [pltpu.HBM(z.shape, z.dtype) for z in zones] + [_sds((8, LANES))]),
        in_specs=[HBM_SPEC] * (2 * n), out_specs=tuple([SEM_SPEC] * (2 * n) + [HBM_SPEC] * (2 * n) + [VMEM_SPEC]),
        input_output_aliases={i: 2 * n + i for i in range(2 * n)},
        compiler_params=pltpu.CompilerParams(has_side_effects=DATAFLOW),
    )(*[hbm(v) for v in srcs], *[hbm(z) for z in zones])
    return out[:2 * n], out[2 * n:3 * n], out[3 * n:4 * n], out[-1]


def _relay_start(zones, name):
    n = len(zones)

    def body(*refs):
        zs, sems, token = refs[:n], refs[n:3 * n], refs[-1]
        x, y, c = _place()
        for fx, fy in CHIP_FLIPS:
            slot = 4 * (x ^ fx) + 2 * (y ^ fy) + c
            for a in range(n):
                pltpu.make_async_remote_copy(
                    src_ref=zs[a].at[slot], dst_ref=zs[a].at[slot], send_sem=sems[2 * a], recv_sem=sems[2 * a + 1],
                    device_id=(x, y, 1 - c), device_id_type=MESH_IDS).start()
        token[...] = jnp.zeros_like(token)

    out = pl.pallas_call(
        body, name=name,
        out_shape=tuple([pltpu.SemaphoreType.DMA(())] * (2 * n) + [pltpu.HBM(z.shape, z.dtype) for z in zones]
                        + [_sds((8, LANES))]),
        in_specs=[HBM_SPEC] * n, out_specs=tuple([SEM_SPEC] * (2 * n) + [HBM_SPEC] * n + [VMEM_SPEC]),
        input_output_aliases={i: 2 * n + i for i in range(n)},
        compiler_params=pltpu.CompilerParams(has_side_effects=DATAFLOW),
    )(*[pltpu.with_memory_space_constraint(z, pltpu.HBM) for z in zones])
    return out[:2 * n], [], out[2 * n:3 * n], out[-1]


def _exchange_wait(sems, srcs, zones, after, name, chips=False, n_copies=None):
    n, n_src = len(zones), len(srcs)
    after = list(after) if isinstance(after, (list, tuple)) else [after]
    n_copies = n_copies or (len(CHIP_FLIPS) - 1 if chips else N_DEV - 1)

    def body(*refs):
        zs, sm = refs[n_src:n_src + n], refs[n_src + n:n_src + 3 * n]
        me, peers = _peers()
        for a in range(n):
            seven = zs[a].at[pl.ds(0, n_copies)]
            cp = pltpu.make_async_remote_copy(src_ref=seven, dst_ref=seven, send_sem=sm[2 * a], recv_sem=sm[2 * a + 1],
                                              device_id=peers[0], device_id_type=MESH_IDS)
            cp.wait_send()
            cp.wait_recv()

    out = pl.pallas_call(
        body, name=name, out_shape=tuple([pltpu.HBM(v.shape, v.dtype) for v in srcs] + [pltpu.HBM(z.shape, z.dtype) for z in zones]),
        in_specs=[HBM_SPEC] * (n_src + n) + [SEM_SPEC] * (2 * n) + [ANY_SPEC] * len(after),
        out_specs=tuple([HBM_SPEC] * (n_src + n)), input_output_aliases={i: i for i in range(n_src + n)},
        compiler_params=pltpu.CompilerParams(has_side_effects=DATAFLOW),
    )(*srcs, *zones, *sems, *after)
    return out[n_src:]


def _sum_adamw(zone, w, m, v, name):
    n_slots, r, c_ = zone.shape
    rb = next((b for b in (256, 128) if r % b == 0), r)

    def body(z_ref, w_ref, m_ref, v_ref, grad_ref, delta_ref, nm_ref, nv_ref):
        total = z_ref[0].astype(F32)
        for d in range(1, n_slots):
            total = total + z_ref[d].astype(F32)
        grad_ref[...] = total
        delta_ref[...], nm_ref[...], nv_ref[...] = _adamw(w_ref[...], total, m_ref[...], v_ref[...])

    blk = pl.BlockSpec((rb, c_), lambda i: (i, 0))
    return pl.pallas_call(
        body, name=name, grid=(r // rb,), in_specs=[pl.BlockSpec((n_slots, rb, c_), lambda i: (0, i, 0)), blk, blk, blk],
        out_specs=[blk] * 4, out_shape=[_sds((r, c_))] * 4, compiler_params=_params("parallel"),
    )(zone, w, m, v)


SMALL_NORMS = ("pre_mix_norm", "post_mix_norm", "pre_mlp_norm", "post_mlp_norm")
SMALL_ORDER = SMALL_NORMS + ("fox_out_norm", "gdn_out_norm", "fox_f_bias", "gdn_a_log", "gdn_dt_bias", "gdn_conv_w")
CONV_SLAB_ROWS, CONV_SLAB_LANES = 8, 256


def _small_pack(small):
    def body(n0, n1, n2, n3, fnw_ref, gnw_ref, loss_ref, vec_ref, out_ref):
        out_ref[...] = jnp.zeros_like(out_ref)
        for i, ref in enumerate((n0, n1, n2, n3)):
            out_ref[i:i + 1, :] = ref[...]
        out_ref[4:5, 0:LANES] = fnw_ref[...]
        out_ref[4:5, LANES:2 * LANES] = gnw_ref[...]
        out_ref[4:5, 2 * LANES:3 * LANES] = loss_ref[...]
        out_ref[5:8, 0:LANES] = vec_ref[0:3, :]

    return pl.pallas_call(body, name="small_pack", in_specs=[VMEM_SPEC] * 8, out_specs=VMEM_SPEC,
                          out_shape=_sds((8, D_MODEL)))(*small["norms"], small["fox_out_norm"], small["gdn_out_norm"],
                                                        small["loss"], small["vectors"])


def _conv_slabs(dconv):
    blocks = dconv.reshape(CONV_K, N_DEV, -1).transpose(1, 0, 2)
    blocks = jnp.pad(blocks, ((0, 0), (0, CONV_SLAB_ROWS - CONV_K), (0, CONV_SLAB_LANES - blocks.shape[2])))
    return blocks.reshape(N_DEV * CONV_SLAB_ROWS, CONV_SLAB_LANES)


def _small_update(zone, conv_zone, w, m, v):
    n = len(SMALL_ORDER)
    n_conv = w["gdn_conv_w"].shape[1]

    def body(me_ref, z_ref, zc_ref, *refs):
        params, loss_ref, outs, (tot, totc) = refs[:3 * n], refs[3 * n], refs[3 * n + 1:7 * n + 1], refs[-2:]
        total, total_c = z_ref[0], zc_ref[0]
        for d in range(1, N_DEV):
            total, total_c = total + z_ref[d], total_c + zc_ref[d]
        tot[...] = total
        totc[...] = total_c
        loss_ref[...] = tot[4, 2 * LANES:2 * LANES + 1]
        mine = totc[pl.ds(pl.multiple_of(me_ref[0] * CONV_SLAB_ROWS, CONV_SLAB_ROWS), CONV_SLAB_ROWS), :]
        g = dict(zip(SMALL_NORMS, (tot[0], tot[1], tot[2], tot[3])))
        g.update(fox_out_norm=tot[4, 0:FOX_HEAD_DIM], gdn_out_norm=tot[4, LANES:LANES + GDN_HEAD_DIM],
                 fox_f_bias=tot[5, SM_FF:SM_FF + N_FOX_HEADS], gdn_a_log=tot[6, SM_GA:SM_GA + N_GDN_HEADS],
                 gdn_dt_bias=tot[7, SM_GA:SM_GA + N_GDN_HEADS], gdn_conv_w=mine[0:CONV_K, 0:n_conv])
        for i, name in enumerate(SMALL_ORDER):
            w_ref, m_ref, v_ref = params[3 * i:3 * i + 3]
            outs[4 * i][...] = g[name]
            outs[4 * i + 1][...], outs[4 * i + 2][...], outs[4 * i + 3][...] = _adamw(w_ref[...], g[name], m_ref[...],
                                                                                     v_ref[...])

    x, y, c = _place()
    operands = [a[name] for name in SMALL_ORDER for a in (w, m, v)]
    out = pl.pallas_call(
        body, name="small_update",
        in_specs=[pl.BlockSpec(memory_space=pltpu.SMEM)] + [VMEM_SPEC] * (2 + 3 * n), out_specs=[VMEM_SPEC] * (1 + 4 * n),
        out_shape=[_sds((1,))] + [_sds(w[name].shape) for name in SMALL_ORDER for _ in range(4)],
        scratch_shapes=[pltpu.VMEM(zone.shape[1:], F32), pltpu.VMEM(conv_zone.shape[1:], F32)],
    )((4 * x + 2 * y + c).astype(jnp.int32).reshape(1), zone, conv_zone, *operands)
    return out[0][0], {name: out[1 + 4 * i:5 + 4 * i] for i, name in enumerate(SMALL_ORDER)}


NATIVE_ROWS = ((0, 1536), (1544, 3080), (3088, 3600), (1536, 1544), (3080, 3088))


def _to_aligned_rows(wt_native):
    pad = jnp.zeros((PROJ_W - D_PROJ, wt_native.shape[1]), wt_native.dtype)
    return jnp.concatenate([wt_native[lo:hi] for lo, hi in NATIVE_ROWS] + [pad])


def _from_aligned_rows(gt_al):
    return jnp.concatenate([gt_al[0:1536], gt_al[3584:3592], gt_al[1536:3072], gt_al[3592:3600], gt_al[3072:3584]])


def _cols_from_pieces(p):
    return p.transpose(1, 0, 2).reshape(p.shape[1], -1)


WEIGHT_ORDER = ("pre_mix_norm", "w_in", "fox_f_bias", "fox_out_norm", "gdn_conv_w", "gdn_a_log", "gdn_dt_bias",
                "gdn_out_norm", "w_out", "post_mix_norm", "pre_mlp_norm", "w_up", "w_down", "post_mlp_norm")


def kernel(x, pre_mix_norm, w_in, fox_f_bias, fox_out_norm, gdn_conv_w, gdn_a_log, gdn_dt_bias, gdn_out_norm, w_out, post_mix_norm, pre_mlp_norm, w_up, w_down, post_mlp_norm, loss_target, m_pre_mix_norm, m_w_in, m_fox_f_bias, m_fox_out_norm, m_gdn_conv_w, m_gdn_a_log, m_gdn_dt_bias, m_gdn_out_norm, m_w_out, m_post_mix_norm, m_pre_mlp_norm, m_w_up, m_w_down, m_post_mlp_norm, v_pre_mix_norm, v_w_in, v_fox_f_bias, v_fox_out_norm, v_gdn_conv_w, v_gdn_a_log, v_gdn_dt_bias, v_gdn_out_norm, v_w_out, v_post_mix_norm, v_pre_mlp_norm, v_w_up, v_w_down, v_post_mlp_norm):
    w = dict(pre_mix_norm=pre_mix_norm, w_in=w_in, fox_f_bias=fox_f_bias, fox_out_norm=fox_out_norm,
             gdn_conv_w=gdn_conv_w, gdn_a_log=gdn_a_log, gdn_dt_bias=gdn_dt_bias, gdn_out_norm=gdn_out_norm, w_out=w_out,
             post_mix_norm=post_mix_norm, pre_mlp_norm=pre_mlp_norm, w_up=w_up, w_down=w_down, post_mlp_norm=post_mlp_norm)
    mom = dict(pre_mix_norm=m_pre_mix_norm, w_in=m_w_in, fox_f_bias=m_fox_f_bias, fox_out_norm=m_fox_out_norm,
               gdn_conv_w=m_gdn_conv_w, gdn_a_log=m_gdn_a_log, gdn_dt_bias=m_gdn_dt_bias, gdn_out_norm=m_gdn_out_norm,
               w_out=m_w_out, post_mix_norm=m_post_mix_norm, pre_mlp_norm=m_pre_mlp_norm, w_up=m_w_up, w_down=m_w_down,
               post_mlp_norm=m_post_mlp_norm)
    var = dict(pre_mix_norm=v_pre_mix_norm, w_in=v_w_in, fox_f_bias=v_fox_f_bias, fox_out_norm=v_fox_out_norm,
               gdn_conv_w=v_gdn_conv_w, gdn_a_log=v_gdn_a_log, gdn_dt_bias=v_gdn_dt_bias, gdn_out_norm=v_gdn_out_norm,
               w_out=v_w_out, post_mix_norm=v_post_mix_norm, pre_mlp_norm=v_pre_mlp_norm, w_up=v_w_up, w_down=v_w_down,
               post_mlp_norm=v_post_mlp_norm)

    win_g, conv_g = _all_gather([w_in.T.astype(BF), gdn_conv_w])
    wt_al = _to_aligned_rows(win_g.reshape(D_PROJ, D_MODEL))
    convw = _cols_from_pieces(conv_g)
    gathers, after = {}, win_g
    for name, shards in (("w_out", [w_out]), ("mlp", [w_up.T, w_down])):
        zones, shards = _zones_with_own(shards, False, "gather_" + name + "_own", after=after, dtype=BF)
        gathers[name] = _exchange_start(shards, zones, False, "gather_" + name + "_start", chips=name == "mlp")
        after = gathers[name][3]

    def late_weights(name, after):
        if name == "mlp_relay":
            sems, shards, zones, _ = gathers["mlp"]
            zones = _exchange_wait(sems, shards, zones, after, "gather_mlp_wait", chips=True)
            gathers["mlp"] = _relay_start(zones, "gather_mlp_relay")
            return gathers["mlp"][3]
        sems, shards, zones, _ = gathers[name]
        got = _exchange_wait(sems, shards, zones, after, "gather_" + name + "_done",
                             n_copies=len(CHIP_FLIPS) if name == "mlp" else None)
        if name == "w_out":
            return got[0].reshape(D_MODEL, D_MODEL)
        return got[0].reshape(D_FF, D_MODEL), got[1].reshape(D_FF, D_MODEL)

    scatters = {}

    def on_grads(name, g):
        chips = name == "w_in"
        if name == "w_in":
            g = _pair_reduce(_from_aligned_rows(g).reshape(N_DEV, D_PROJ // N_DEV, D_MODEL), "pair_reduce_w_in")
        srcs = list(g) if name == "mlp" else [g]
        zones, _ = _zones_with_own(srcs, True, "scatter_" + name + "_own", chips=chips)
        scatters[name] = _exchange_start(srcs, zones, True, "scatter_" + name + "_start", chips=chips)
        return scatters[name][3]

    grad_x, small = _local_step(
        x[0], loss_target[0], wt_al, late_weights, on_grads, convw, pre_mix_norm + after[0, 0],
        fox_f_bias, fox_out_norm, gdn_a_log, gdn_dt_bias, gdn_out_norm, post_mix_norm, pre_mlp_norm, post_mlp_norm)
    slabs = [_small_pack(small), _conv_slabs(jnp.concatenate(small["conv"], axis=1))]
    zones, slabs = _zones_with_own(slabs, False, "small_own")
    scatters["small"] = _exchange_start(slabs, zones, False, "small_start")

    grads, delta, new_m, new_v = {}, {}, {}, {}
    after = scatters["small"][3]
    for name, members in (("mlp", ("w_up", "w_down")), ("w_out", ("w_out",)), ("small", ()), ("w_in", ("w_in",))):
        sems, srcs, zones, _ = scatters[name]
        zones = _exchange_wait(sems, srcs, zones, after, "scatter_" + name + "_wait", chips=name == "w_in")
        if name == "small":
            loss, updated = _small_update(zones[0], zones[1], w, mom, var)
            for n, res in updated.items():
                grads[n], delta[n], new_m[n], new_v[n] = res
            after = grads["pre_mix_norm"]
        for n, zone in zip(members, zones):
            if n == "w_in":
                res = _sum_adamw(zone, w[n].T, mom[n].T, var[n].T, "adamw_" + n)
                grads[n], delta[n], new_m[n], new_v[n] = [r.T for r in res]
            else:
                grads[n], delta[n], new_m[n], new_v[n] = _sum_adamw(zone, w[n], mom[n], var[n], "adamw_" + n)
        if members:
            after = [grads[n] for n in members]

    return (loss, grad_x[None], *[grads[n] for n in WEIGHT_ORDER], *[delta[n] for n in WEIGHT_ORDER],
            *[new_m[n] for n in WEIGHT_ORDER], *[new_v[n] for n in WEIGHT_ORDER])
```

```python
import jax
import jax.numpy as jnp
from jax import lax
from jax.experimental import pallas as pl
from jax.experimental.pallas import tpu as pltpu

F32 = jnp.float32
BF = jnp.bfloat16

D_MODEL = 1024
N_FOX_HEADS, FOX_HEAD_DIM = 8, 64
N_GDN_HEADS, GDN_HEAD_DIM = 4, 128
D_FOX = N_FOX_HEADS * FOX_HEAD_DIM
D_GDN = N_GDN_HEADS * GDN_HEAD_DIM
CHUNK = 64
CONV_K = 4
D_FF = 4 * D_MODEL
EPS = 1e-6
D_PROJ = 3600
N_DEV = 8

PROJ_W = 3712
COL_FOX, COL_GDN, COL_GZ, COL_SMALL = 0, 1536, 3072, 3584
LANES = 128
SM_FF, SM_GB, SM_GA = 0, 8, 12

ADAM_LR, ADAM_B1, ADAM_B2, ADAM_EPS, ADAM_WD, ADAM_STEP = 0.001, 0.9, 0.999, 1e-08, 0.01, 10

TOKEN_BLOCK = 256
MATMUL_BLOCK = 512
FOX_SCALE = FOX_HEAD_DIM ** -0.5
GDN_QSCALE = GDN_HEAD_DIM ** -0.5
NEG_BIG = -1e30
VMEM_LIMIT = 56 * 1024 * 1024

VMEM_SPEC = pl.BlockSpec(memory_space=pltpu.VMEM)


def _sds(shape, dtype=F32):
    return jax.ShapeDtypeStruct(shape, dtype)


def _params(*sem):
    return pltpu.CompilerParams(dimension_semantics=sem if sem else None, vmem_limit_bytes=VMEM_LIMIT)


def _mm(a, b):
    return jnp.dot(a.astype(BF), b.astype(BF), preferred_element_type=F32)


def _mm_nt(a, b):
    return lax.dot_general(a.astype(BF), b.astype(BF), (((1,), (1,)), ((), ())), preferred_element_type=F32)


def _mm_tn(a, b):
    return lax.dot_general(a.astype(BF), b.astype(BF), (((0,), (0,)), ((), ())), preferred_element_type=F32)


def _sigmoid(x):
    return 1.0 / (1.0 + jnp.exp(-x))


def _softplus(x):
    return jnp.maximum(x, 0.0) + jnp.log1p(jnp.exp(-jnp.abs(x)))


def _iota(shape, dim):
    return lax.broadcasted_iota(jnp.int32, shape, dim)


def _shift_down(x, s, row):
    return jnp.where(row >= s, pltpu.roll(x, s, 0), 0.0)


def _shift_up(x, s, row):
    n = x.shape[0]
    return jnp.where(row < n - s, pltpu.roll(x, n - s, 0), 0.0)


def _norm_proj(x, nw, wt_al):
    t = x.shape[0]

    def body(x_ref, nw_ref, w_ref, proj_ref, h_ref):
        xv = x_ref[...]
        r = lax.rsqrt(jnp.mean(xv * xv, axis=-1, keepdims=True) + EPS)
        h = (xv * r * nw_ref[...]).astype(BF)
        h_ref[...] = h
        proj_ref[...] = lax.dot_general(h, w_ref[...], (((1,), (1,)), ((), ())), preferred_element_type=F32)

    tm = min(MATMUL_BLOCK, t)
    return pl.pallas_call(
        body, name="norm_proj", grid=(t // tm,),
        in_specs=[pl.BlockSpec((tm, D_MODEL), lambda i: (i, 0)), pl.BlockSpec((1, D_MODEL), lambda i: (0, 0)),
                  pl.BlockSpec((PROJ_W, D_MODEL), lambda i: (0, 0))],
        out_specs=[pl.BlockSpec((tm, PROJ_W), lambda i: (i, 0)), pl.BlockSpec((tm, D_MODEL), lambda i: (i, 0))],
        out_shape=[_sds((t, PROJ_W)), _sds((t, D_MODEL), BF)],
        compiler_params=_params("parallel"),
    )(x, nw, wt_al)


def _lane_column(x, lane):
    return jnp.sum(jnp.where(_iota((1, LANES), 1) == lane, x, 0.0), axis=-1, keepdims=True)


def _small_prep(proj, fb, al, dtb):
    t = proj.shape[0]

    def body(sm_ref, fb_ref, al_ref, dtb_ref, cumt_ref, beta_ref, g_ref):
        s = sm_ref[...]
        z = s + fb_ref[...]
        cum = jnp.minimum(z, 0.0) - jnp.log1p(jnp.exp(-jnp.abs(z)))
        row = _iota((t, LANES), 0)
        step = 1
        while step < t:
            cum = cum + _shift_down(cum, step, row)
            step *= 2
        cumt_ref[...] = cum.T
        beta_ref[...] = _sigmoid(s)
        g_ref[...] = -jnp.exp(al_ref[...]) * _softplus(s + dtb_ref[...])

    vec = pl.BlockSpec((1, LANES), lambda i: (0, 0))
    tok = pl.BlockSpec((t, LANES), lambda i: (0, 0))
    return pl.pallas_call(
        body, name="small_prep", grid=(1,),
        in_specs=[pl.BlockSpec((t, LANES), lambda i: (0, COL_SMALL // LANES)), vec, vec, vec],
        out_specs=[pl.BlockSpec((LANES, t), lambda i: (0, 0)), tok, tok],
        out_shape=[_sds((LANES, t)), _sds((t, LANES)), _sds((t, LANES))],
        compiler_params=_params("arbitrary"),
    )(proj, fb, al, dtb)


def _fox_stack(x, first):
    return jnp.concatenate([jnp.where(first, x, 0.0), jnp.where(first, 0.0, x)], axis=0).astype(BF)


def _fox_unstack(y, first):
    n = y.shape[0] // 2
    return jnp.where(first, y[:n], y[n:])


def _fox_logits(q2_i, kb, cumt_ref, pair, i, tq):
    klen = (i + 1) * tq
    s = lax.dot_general(q2_i, kb[:klen], (((1,), (1,)), ((), ())), preferred_element_type=F32)
    upper = _iota((2 * tq, 1), 0) < tq
    s = s - jnp.where(upper, cumt_ref[pl.ds(2 * pair, 1), 0:klen], cumt_ref[pl.ds(2 * pair + 1, 1), 0:klen])
    causal = _iota((2 * tq, tq), 1) <= _iota((2 * tq, tq), 0) % tq
    parts = [(s[:, :klen - tq], 0, klen - tq)] if i else []
    return parts + [(jnp.where(causal, s[:, klen - tq:], NEG_BIG), klen - tq, klen)]


def _fox_fwd(proj, cumt, fnw):
    t = proj.shape[0]
    tq = min(TOKEN_BLOCK, t // 2)
    nq = t // tq

    def body(q_ref, k_ref, v_ref, cumt_ref, fnw_ref, o_ref, lse_ref, fn_ref):
        j = pl.program_id(0)
        first = _iota((1, LANES), 1) < FOX_HEAD_DIM
        kb = k_ref[...].astype(BF)
        vb = v_ref[...].astype(BF)
        for i in range(nq):
            rows = slice(i * tq, (i + 1) * tq)
            q2 = _fox_stack(q_ref[rows, :] * FOX_SCALE, first)
            parts = _fox_logits(q2, kb, cumt_ref, j, i, tq)
            m = jnp.max(parts[-1][0], axis=-1, keepdims=True)
            if i:
                m = jnp.maximum(m, jnp.max(parts[0][0], axis=-1, keepdims=True))
            l = jnp.zeros((2 * tq, 1), F32)
            o = jnp.zeros((2 * tq, LANES), F32)
            for s, lo, hi in parts:
                p = jnp.exp(s - m)
                l = l + jnp.sum(p, axis=-1, keepdims=True)
                o = o + jnp.dot(p.astype(BF), vb[lo:hi], preferred_element_type=F32)
            o_acc = _fox_unstack(o / l, first)
            lse_acc = _fox_unstack(jnp.broadcast_to(m + jnp.log(l), (2 * tq, LANES)), first)
            o_ref[rows, :] = o_acc
            lse_ref[rows, :] = lse_acc
            o2 = o_acc * o_acc
            s0 = jnp.sum(jnp.where(first, o2, 0.0), axis=-1, keepdims=True)
            s1 = jnp.sum(jnp.where(first, 0.0, o2), axis=-1, keepdims=True)
            r = lax.rsqrt(jnp.where(first, s0, s1) * (1.0 / FOX_HEAD_DIM) + EPS)
            fn_ref[rows, :] = (o_acc * r * fnw_ref[...]).astype(BF)

    blk = lambda off: pl.BlockSpec((t, LANES), lambda j: (0, off + j))
    return pl.pallas_call(
        body, name="fox_fwd", grid=(N_FOX_HEADS // 2,),
        in_specs=[blk(0), blk(4), blk(8), pl.BlockSpec((LANES, t), lambda j: (0, 0)),
                  pl.BlockSpec((1, LANES), lambda j: (0, 0))],
        out_specs=[blk(0), blk(0), blk(0)],
        out_shape=[_sds((t, D_FOX)), _sds((t, D_FOX)), _sds((t, D_FOX), BF)],
        compiler_params=_params("parallel"),
    )(proj, proj, proj, cumt, fnw)


def _fox_bwd(proj, cumt, lse, o, do):
    t = proj.shape[0]
    tq = min(TOKEN_BLOCK, t // 2)
    nq = t // tq

    def body(q_ref, k_ref, v_ref, cumt_ref, lse_ref, o_ref, do_ref,
             dq_ref, dk_ref, dv_ref, dcq_ref, dckt_ref, dk_s, dv_s):
        j = pl.program_id(0)

        @pl.when(j == 0)
        def _():
            dcq_ref[...] = jnp.zeros_like(dcq_ref)
            dckt_ref[...] = jnp.zeros_like(dckt_ref)

        lane = _iota((1, LANES), 1)

        first = _iota((1, LANES), 1) < FOX_HEAD_DIM
        kb = k_ref[...].astype(BF)
        vb = v_ref[...].astype(BF)
        dk_s[...] = jnp.zeros_like(dk_s)
        dv_s[...] = jnp.zeros_like(dv_s)
        for i in range(nq):
            rows = slice(i * tq, (i + 1) * tq)
            do_i = do_ref[rows, :]
            prod = do_i * o_ref[rows, :]
            lse_i = lse_ref[rows, :]
            q2 = _fox_stack(q_ref[rows, :] * FOX_SCALE, first)
            do2 = _fox_stack(do_i, first)
            delta = jnp.concatenate([jnp.sum(jnp.where(first, prod, 0.0), axis=-1, keepdims=True),
                                     jnp.sum(jnp.where(first, 0.0, prod), axis=-1, keepdims=True)], axis=0)
            lse2 = jnp.concatenate([lse_i[:, 0:1], lse_i[:, FOX_HEAD_DIM:FOX_HEAD_DIM + 1]], axis=0)
            dq2 = jnp.zeros((2 * tq, LANES), F32)
            dcq2 = jnp.zeros((2 * tq, 1), F32)
            for s, lo, hi in _fox_logits(q2, kb, cumt_ref, j, i, tq):
                p = jnp.exp(s - lse2)
                ds = p * (_mm_nt(do2, vb[lo:hi]) - delta)
                dsb = ds.astype(BF)
                dq2 = dq2 + jnp.dot(dsb, kb[lo:hi], preferred_element_type=F32)
                dk_s[lo:hi, :] += _mm_tn(dsb, q2)
                dv_s[lo:hi, :] += _mm_tn(p, do2)
                dcq2 = dcq2 + jnp.sum(ds, axis=-1, keepdims=True)
                dckt_ref[pl.ds(2 * j, 1), lo:hi] += jnp.sum(ds[:tq], axis=0, keepdims=True)
                dckt_ref[pl.ds(2 * j + 1, 1), lo:hi] += jnp.sum(ds[tq:], axis=0, keepdims=True)
            dq_ref[rows, :] = (_fox_unstack(dq2, first) * FOX_SCALE).astype(BF)
            dcq_ref[rows, :] += jnp.where(lane == 2 * j, dcq2[:tq], jnp.where(lane == 2 * j + 1, dcq2[tq:], 0.0))
        dk_ref[...] = dk_s[...].astype(BF)
        dv_ref[...] = dv_s[...].astype(BF)

    blk = lambda off: pl.BlockSpec((t, LANES), lambda j: (0, off + j))
    rows128 = pl.BlockSpec((LANES, t), lambda j: (0, 0))
    return pl.pallas_call(
        body, name="fox_bwd", grid=(N_FOX_HEADS // 2,),
        in_specs=[blk(0), blk(4), blk(8), rows128, blk(0), blk(0), blk(0)],
        out_specs=[blk(0), blk(0), blk(0), pl.BlockSpec((t, LANES), lambda j: (0, 0)), rows128],
        out_shape=[_sds((t, D_FOX), BF)] * 3 + [_sds((t, LANES)), _sds((LANES, t))],
        scratch_shapes=[pltpu.VMEM((t, LANES), F32), pltpu.VMEM((t, LANES), F32)],
        compiler_params=_params("arbitrary"),
    )(proj, proj, proj, cumt, lse, o, do)


def _conv(x, w, row):
    return (w[3:4, :] * x + w[2:3, :] * _shift_down(x, 1, row) + w[1:2, :] * _shift_down(x, 2, row)
            + w[0:1, :] * _shift_down(x, 3, row))


def _chunk_decay(gc_c):
    gi = gc_c[:, 0:CHUNK]
    gj = gc_c.T[0:CHUNK, :]
    ri = _iota((CHUNK, CHUNK), 0)
    cj = _iota((CHUNK, CHUNK), 1)
    return jnp.where(ri >= cj, jnp.exp(jnp.minimum(gi - gj, 0.0)), 0.0), ri > cj


def _gdn_specs(t):
    col = lambda off: pl.BlockSpec((t, LANES), lambda h: (0, off + h))
    cw = lambda off: pl.BlockSpec((CONV_K, LANES), lambda h: (0, off + h))
    mat = pl.BlockSpec((1, t // CHUNK, CHUNK, CHUNK), lambda h: (h, 0, 0, 0))
    return col, cw, mat


def _gdn_prep(proj, convw, beta, g):
    t = proj.shape[0]
    nch = t // CHUNK

    def body(xq_ref, xk_ref, xv_ref, wq_ref, wk_ref, wv_ref, beta_ref, g_ref,
             qn_ref, kn_ref, cv_ref, gc_ref, be_ref, m_ref, a_ref):
        row = _iota((t, LANES), 0)
        hd = pl.program_id(0)
        be_ref[...] = jnp.broadcast_to(_lane_column(beta_ref[...], SM_GB + hd), (t, LANES))

        def act(x_ref, w_ref):
            y = _conv(x_ref[...], w_ref[...], row)
            return y * _sigmoid(y)

        cq = act(xq_ref, wq_ref)
        ck = act(xk_ref, wk_ref)
        cv_ref[...] = act(xv_ref, wv_ref)
        qn_ref[...] = cq * lax.rsqrt(jnp.sum(cq * cq, axis=-1, keepdims=True) + EPS) * GDN_QSCALE
        kn_ref[...] = ck * lax.rsqrt(jnp.sum(ck * ck, axis=-1, keepdims=True) + EPS)
        gc = jnp.broadcast_to(_lane_column(g_ref[...], SM_GA + hd), (t, LANES))
        pos = row % CHUNK
        step = 1
        while step < CHUNK:
            gc = gc + jnp.where(pos >= step, pltpu.roll(gc, step, 0), 0.0)
            step *= 2
        gc_ref[...] = gc

        group = 4 if nch % 4 == 0 else 1

        def chunks(gi, carry):
            ns = [gi * group + c for c in range(group)]
            sls = [pl.ds(pl.multiple_of(n * CHUNK, CHUNK), CHUNK) for n in ns]
            ks = [kn_ref[sl, :] for sl in sls]
            kk = [_mm_nt(k_c * be_ref[sl, :], k_c) for k_c, sl in zip(ks, sls)]
            qk = [_mm_nt(qn_ref[sl, :], k_c) for k_c, sl in zip(ks, sls)]
            for c, n in enumerate(ns):
                decay, strict = _chunk_decay(gc_ref[sls[c], :])
                m_ref[0, n] = jnp.where(strict, kk[c] * decay, 0.0)
                a_ref[0, n] = qk[c] * decay
            return carry

        lax.fori_loop(0, nch // group, chunks, 0)

    col, cw, mat = _gdn_specs(t)
    return pl.pallas_call(
        body, name="gdn_prep", grid=(N_GDN_HEADS,),
        in_specs=[col(12), col(16), col(20), cw(0), cw(4), cw(8)] + [pl.BlockSpec((t, LANES), lambda h: (0, 0))] * 2,
        out_specs=[col(0), col(0), col(0), col(0), col(0), mat, mat],
        out_shape=[_sds((t, D_GDN))] * 5 + [_sds((N_GDN_HEADS, nch, CHUNK, CHUNK))] * 2,
        compiler_params=_params("parallel"),
    )(proj, proj, proj, convw, convw, convw, beta, g)


def _tri_inverse(m3):
    assert m3.shape == (LANES, CHUNK, CHUNK)

    def body(m_ref, t_ref, ms, ts):
        for i in range(CHUNK):
            ms[i * CHUNK:(i + 1) * CHUNK, :] = m_ref[:, i, :].T
        cidx = _iota((CHUNK, LANES), 0)

        def outer(i, carry):
            def inner(jj, acc):
                mrow = ms[pl.ds(i * CHUNK + jj, 1), :]
                return acc - mrow * ts[pl.ds(pl.multiple_of(jj * CHUNK, CHUNK), CHUNK), :]

            acc = lax.fori_loop(0, i, inner, jnp.where(cidx == i, 1.0, 0.0).astype(F32))
            ts[pl.ds(pl.multiple_of(i * CHUNK, CHUNK), CHUNK), :] = acc
            return carry

        lax.fori_loop(0, CHUNK, outer, 0)
        for i in range(CHUNK):
            t_ref[:, i, :] = ts[i * CHUNK:(i + 1) * CHUNK, :].T

    return pl.pallas_call(
        body, name="tri_inverse", in_specs=[VMEM_SPEC], out_specs=VMEM_SPEC,
        out_shape=_sds((LANES, CHUNK, CHUNK)),
        scratch_shapes=[pltpu.VMEM((CHUNK * CHUNK, LANES), F32), pltpu.VMEM((CHUNK * CHUNK, LANES), F32)],
        compiler_params=_params(),
    )(m3)


def _gdn_chunk_terms(q, k, v, b, gcc):
    eg = jnp.exp(gcc)
    last = gcc[CHUNK - 1:CHUNK, :]
    egl = jnp.exp(last - gcc)
    gl = jnp.exp(last)
    kb = k * b
    return eg, egl, gl, kb, v * b, kb * eg, q * eg, k * egl


GDN_BLOCK_CHUNKS = 4


def _gdn_block_specs(t, reverse):
    cb = GDN_BLOCK_CHUNKS
    nb = t // (cb * CHUNK)
    idx = (lambda i: nb - 1 - i) if reverse else (lambda i: i)
    tok = pl.BlockSpec((cb * CHUNK, D_GDN), lambda i: (idx(i), 0))
    mat = pl.BlockSpec((N_GDN_HEADS, cb, CHUNK, CHUNK), lambda i: (0, idx(i), 0, 0))
    state = pl.BlockSpec((N_GDN_HEADS, cb, GDN_HEAD_DIM, GDN_HEAD_DIM), lambda i: (0, idx(i), 0, 0))
    return nb, tok, mat, state


def _gdn_scan(qn, kn, cv, be, gc, tinv, amat):
    t = qn.shape[0]
    nch = t // CHUNK

    def body(q_ref, k_ref, v_ref, b_ref, gc_ref, t_ref, a_ref, o_ref, sall_ref, vn_ref, s_scr):
        @pl.when(pl.program_id(0) == 0)
        def _():
            s_scr[...] = jnp.zeros_like(s_scr)

        heads = range(N_GDN_HEADS)
        cols = [slice(hd * LANES, (hd + 1) * LANES) for hd in heads]
        s = [s_scr[hd] for hd in heads]
        for cc in range(GDN_BLOCK_CHUNKS):
            rs = slice(cc * CHUNK, (cc + 1) * CHUNK)
            terms = [_gdn_chunk_terms(q_ref[rs, cs], k_ref[rs, cs], v_ref[rs, cs], b_ref[rs, cs], gc_ref[rs, cs])
                     for cs in cols]
            for hd in heads:
                sall_ref[hd, cc] = s[hd]
            uw = [_mm(t_ref[hd, cc], jnp.concatenate([terms[hd][4], terms[hd][5]], axis=1)) for hd in heads]
            ws_qs = [_mm(jnp.concatenate([uw[hd][:, LANES:], terms[hd][6]], axis=0), s[hd]) for hd in heads]
            vn = [uw[hd][:, :LANES] - ws_qs[hd][:CHUNK] for hd in heads]
            a_vn = [_mm(a_ref[hd, cc], vn[hd]) for hd in heads]
            kd_vn = [_mm_tn(terms[hd][7], vn[hd]) for hd in heads]
            for hd in heads:
                vn_ref[rs, cols[hd]] = vn[hd]
                o_ref[rs, cols[hd]] = ws_qs[hd][CHUNK:] + a_vn[hd]
                s[hd] = s[hd] * terms[hd][2] + kd_vn[hd]
        for hd in heads:
            s_scr[hd] = s[hd]

    nb, tok, mat, state = _gdn_block_specs(t, False)
    return pl.pallas_call(
        body, name="gdn_scan", grid=(nb,),
        in_specs=[tok] * 5 + [mat, mat], out_specs=[tok, state, tok],
        out_shape=[_sds((t, D_GDN)), _sds((N_GDN_HEADS, nch, GDN_HEAD_DIM, GDN_HEAD_DIM)), _sds((t, D_GDN))],
        scratch_shapes=[pltpu.VMEM((N_GDN_HEADS, GDN_HEAD_DIM, GDN_HEAD_DIM), F32)],
        compiler_params=_params("arbitrary"),
    )(qn, kn, cv, be, gc, tinv, amat)


def _gdn_bwd(qn, kn, cv, be, gc, tinv, amat, s_all, vn_all, do):
    t = qn.shape[0]

    def body(q_ref, k_ref, v_ref, b_ref, gc_ref, t_ref, a_ref, sall_ref, vn_ref, do_ref,
             dq_ref, dk_ref, dv_ref, db_ref, dg_ref, ds_scr):
        @pl.when(pl.program_id(0) == 0)
        def _():
            ds_scr[...] = jnp.zeros_like(ds_scr)

        lastrow = _iota((CHUNK, LANES), 0) == CHUNK - 1
        heads = range(N_GDN_HEADS)
        cols = [slice(hd * LANES, (hd + 1) * LANES) for hd in heads]
        each = lambda fn: [fn(hd) for hd in heads]
        rows_cat = lambda x, y: jnp.concatenate([x, y], axis=0)
        lane_cat = lambda x, y: jnp.concatenate([x, y], axis=1)
        dsp = each(lambda hd: ds_scr[hd])
        for cc in reversed(range(GDN_BLOCK_CHUNKS)):
            rs = slice(cc * CHUNK, (cc + 1) * CHUNK)
            q = each(lambda hd: q_ref[rs, cols[hd]])
            k = each(lambda hd: k_ref[rs, cols[hd]])
            v = each(lambda hd: v_ref[rs, cols[hd]])
            b = each(lambda hd: b_ref[rs, cols[hd]])
            gcc = each(lambda hd: gc_ref[rs, cols[hd]])
            do_c = each(lambda hd: do_ref[rs, cols[hd]])
            vn = each(lambda hd: vn_ref[rs, cols[hd]])
            tn = each(lambda hd: t_ref[hd, cc])
            st = each(lambda hd: sall_ref[hd, cc])
            terms = each(lambda hd: _gdn_chunk_terms(q[hd], k[hd], v[hd], b[hd], gcc[hd]))
            eg, egl, gl, kb, vb, kbg, qd, kd = [[terms[hd][i] for hd in heads] for i in range(8)]
            w = each(lambda hd: _mm(tn[hd], kbg[hd]))
            a_do = each(lambda hd: _mm_tn(a_ref[hd, cc], do_c[hd]))
            kd_ds = each(lambda hd: _mm(kd[hd], dsp[hd]))
            da = each(lambda hd: _mm_nt(do_c[hd], vn[hd]))
            dkd = each(lambda hd: _mm_nt(vn[hd], dsp[hd]))
            by_k = each(lambda hd: _mm_nt(rows_cat(kb[hd], q[hd]), k[hd]))
            dgl = each(lambda hd: jnp.sum(jnp.sum(dsp[hd] * st[hd], axis=-1, keepdims=True), axis=0, keepdims=True))
            dvn = each(lambda hd: a_do[hd] + kd_ds[hd])
            do_dvn = each(lambda hd: rows_cat(do_c[hd], dvn[hd]))
            by_s = each(lambda hd: _mm_nt(do_dvn[hd], st[hd]))
            dqd = each(lambda hd: by_s[hd][:CHUNK])
            dvn_dw = each(lambda hd: lane_cat(dvn[hd], -by_s[hd][CHUNK:]))
            dsp = each(lambda hd: _mm_tn(rows_cat(qd[hd], -w[hd]), do_dvn[hd]) + gl[hd] * dsp[hd])
            dt = each(lambda hd: _mm_nt(dvn_dw[hd], lane_cat(vb[hd], kbg[hd])))
            by_t = each(lambda hd: _mm_tn(tn[hd], dvn_dw[hd]))
            tt_dt = each(lambda hd: _mm_tn(tn[hd], dt[hd]))
            dm_raw = each(lambda hd: _mm_nt(tt_dt[hd], tn[hd]))
            masks = each(lambda hd: _chunk_decay(gcc[hd]))
            dkk = each(lambda hd: jnp.where(masks[hd][1], -dm_raw[hd], 0.0) * masks[hd][0])
            dqk = each(lambda hd: da[hd] * masks[hd][0])
            dqk_dkk = each(lambda hd: rows_cat(dqk[hd], dkk[hd]))
            on_k = each(lambda hd: _mm(dqk_dkk[hd], k[hd]))
            dk_mm = each(lambda hd: _mm_tn(dqk_dkk[hd], rows_cat(q[hd], kb[hd])))
            for hd in heads:
                cs = cols[hd]
                dvb, dkbg = by_t[hd][:, :LANES], by_t[hd][:, LANES:]
                gmat = dkk[hd] * by_k[hd][:CHUNK] + dqk[hd] * by_k[hd][CHUNK:]
                dq_ref[rs, cs] = dqd[hd] * eg[hd] + on_k[hd][:CHUNK]
                dkb = on_k[hd][CHUNK:] + dkbg * eg[hd]
                dk_ref[rs, cs] = dkd[hd] * egl[hd] + dk_mm[hd] + dkb * b[hd]
                db = jnp.sum(dkb * k[hd], axis=-1, keepdims=True) + jnp.sum(dvb * v[hd], axis=-1, keepdims=True)
                db_ref[rs, cs] = jnp.broadcast_to(db, (CHUNK, LANES))
                dv_ref[rs, cs] = dvb * b[hd]
                dkd_kd = jnp.sum(dkd[hd] * kd[hd], axis=-1, keepdims=True)
                col_sums = jnp.sum(lane_cat(gmat, jnp.zeros_like(gmat)).T, axis=-1, keepdims=True)
                dgc = (jnp.sum(gmat, axis=-1, keepdims=True) - col_sums[:CHUNK]
                       + jnp.sum(dqd[hd] * qd[hd], axis=-1, keepdims=True)
                       + jnp.sum(dkbg * kbg[hd], axis=-1, keepdims=True) - dkd_kd)
                extra = jnp.sum(dkd_kd, axis=0, keepdims=True) + dgl[hd] * gl[hd]
                dg_ref[rs, cs] = dgc + jnp.where(lastrow, extra, 0.0)
        for hd in heads:
            ds_scr[hd] = dsp[hd]
        dg = dg_ref[...]
        row = _iota(dg.shape, 0)
        pos = row % CHUNK
        step = 1
        while step < CHUNK:
            dg = dg + jnp.where(pos < CHUNK - step, pltpu.roll(dg, dg.shape[0] - step, 0), 0.0)
            step *= 2
        dg_ref[...] = dg

    nb, tok, mat, state = _gdn_block_specs(t, True)
    return pl.pallas_call(
        body, name="gdn_bwd", grid=(nb,),
        in_specs=[tok] * 5 + [mat, mat, state, tok, tok], out_specs=[tok] * 5, out_shape=[_sds((t, D_GDN))] * 5,
        scratch_shapes=[pltpu.VMEM((N_GDN_HEADS, GDN_HEAD_DIM, GDN_HEAD_DIM), F32)],
        compiler_params=_params("arbitrary"),
    )(qn, kn, cv, be, gc, tinv, amat, s_all, vn_all, do)


def _gdn_bwd_conv(proj, convw, dqn, dkn, dcv):
    t = proj.shape[0]

    def body(xq_ref, xk_ref, xv_ref, wq_ref, wk_ref, wv_ref, dq_ref, dk_ref, dv_ref,
             dxq_ref, dxk_ref, dxv_ref, dwq_ref, dwk_ref, dwv_ref):
        row = _iota((t, LANES), 0)

        def one(x_ref, w_ref, d_ref, dx_ref, dw_ref, scale):
            x = x_ref[...]
            w = w_ref[...]
            y = _conv(x, w, row)
            sg = _sigmoid(y)
            dc = d_ref[...]
            if scale is not None:
                c = y * sg
                r = lax.rsqrt(jnp.sum(c * c, axis=-1, keepdims=True) + EPS)
                ch = c * r
                dc = scale * r * (dc - ch * jnp.sum(dc * ch, axis=-1, keepdims=True))
            dy = dc * sg * (1.0 + y * (1.0 - sg))
            dx_ref[...] = (w[3:4, :] * dy + w[2:3, :] * _shift_up(dy, 1, row) + w[1:2, :] * _shift_up(dy, 2, row)
                           + w[0:1, :] * _shift_up(dy, 3, row)).astype(BF)
            for jj in range(CONV_K):
                xs = x if jj == CONV_K - 1 else _shift_down(x, CONV_K - 1 - jj, row)
                dw_ref[jj:jj + 1, :] = jnp.sum(dy * xs, axis=0, keepdims=True)

        one(xq_ref, wq_ref, dq_ref, dxq_ref, dwq_ref, GDN_QSCALE)
        one(xk_ref, wk_ref, dk_ref, dxk_ref, dwk_ref, 1.0)
        one(xv_ref, wv_ref, dv_ref, dxv_ref, dwv_ref, None)

    col, cw, _ = _gdn_specs(t)
    return pl.pallas_call(
        body, name="gdn_bwd_conv", grid=(N_GDN_HEADS,),
        in_specs=[col(12), col(16), col(20), cw(0), cw(4), cw(8), col(0), col(0), col(0)],
        out_specs=[col(0), col(0), col(0), cw(0), cw(0), cw(0)],
        out_shape=[_sds((t, D_GDN), BF)] * 3 + [_sds((CONV_K, D_GDN))] * 3,
        compiler_params=_params("parallel"),
    )(proj, proj, proj, convw, convw, convw, dqn, dkn, dcv)


def _mix_out(fox_n, gdn_o, proj, gnw, w_out, x, pmw, plw):
    t = x.shape[0]
    tm = min(MATMUL_BLOCK, t)

    def body(fn_ref, go_ref, gz_ref, gnw_ref, w_ref, x_ref, pmw_ref, plw_ref, x1_ref, h2_ref, mixed_ref, omix_ref,
             h2t_ref):
        omix_ref[:, 0:D_FOX] = fn_ref[...]
        for hd in range(N_GDN_HEADS):
            cs = slice(hd * LANES, (hd + 1) * LANES)
            go = go_ref[:, cs]
            r = lax.rsqrt(jnp.mean(go * go, axis=-1, keepdims=True) + EPS)
            gz = gz_ref[:, cs]
            omix_ref[:, D_FOX + hd * LANES:D_FOX + (hd + 1) * LANES] = (
                go * r * gnw_ref[...] * (gz * _sigmoid(gz))).astype(BF)
        mixed = jnp.dot(omix_ref[...], w_ref[...], preferred_element_type=F32)
        mixed_ref[...] = mixed
        r2 = lax.rsqrt(jnp.mean(mixed * mixed, axis=-1, keepdims=True) + EPS)
        x1 = x_ref[...] + mixed * r2 * pmw_ref[...]
        x1_ref[...] = x1
        r3 = lax.rsqrt(jnp.mean(x1 * x1, axis=-1, keepdims=True) + EPS)
        h2 = x1 * r3 * plw_ref[...]
        h2_ref[...] = h2.astype(BF)
        h2t_ref[...] = h2.T.astype(BF)

    tok = lambda w: pl.BlockSpec((tm, w), lambda i: (i, 0))
    vec = lambda w: pl.BlockSpec((1, w), lambda i: (0, 0))
    return pl.pallas_call(
        body, name="mix_out", grid=(t // tm,),
        in_specs=[tok(D_FOX), tok(D_GDN), pl.BlockSpec((tm, D_GDN), lambda i: (i, COL_GZ // D_GDN)), vec(LANES),
                  pl.BlockSpec((D_MODEL, D_MODEL), lambda i: (0, 0)), tok(D_MODEL), vec(D_MODEL), vec(D_MODEL)],
        out_specs=[tok(D_MODEL)] * 4 + [pl.BlockSpec((D_MODEL, tm), lambda i: (0, i))],
        out_shape=[_sds((t, D_MODEL)), _sds((t, D_MODEL), BF), _sds((t, D_MODEL)), _sds((t, D_MODEL), BF),
                   _sds((D_MODEL, t), BF)],
        compiler_params=_params("parallel"),
    )(fox_n, gdn_o, proj, gnw, w_out, x, pmw, plw)


def _out_bwd(dmixed, w_out, o_fox, gdn_o, proj, fnw, gnw):
    t = dmixed.shape[0]
    tm = min(MATMUL_BLOCK, t)

    def body(dm_ref, w_ref, of_ref, go_ref, gz_ref, fnw_ref, gnw_ref, dof_ref, dgo_ref, dgz_ref, dfw_ref, dgw_ref):
        i = pl.program_id(0)

        @pl.when(i == 0)
        def _():
            dfw_ref[...] = jnp.zeros_like(dfw_ref)
            dgw_ref[...] = jnp.zeros_like(dgw_ref)

        domix = _mm_nt(dm_ref[...], w_ref[...])
        first = _iota((1, LANES), 1) < FOX_HEAD_DIM
        dfw = jnp.zeros((1, LANES), F32)
        dgw = jnp.zeros((1, LANES), F32)
        for pr in range(N_FOX_HEADS // 2):
            cs = slice(pr * LANES, (pr + 1) * LANES)
            o = of_ref[:, cs]
            dfn = domix[:, cs]
            o2 = o * o
            s0 = jnp.sum(jnp.where(first, o2, 0.0), axis=-1, keepdims=True)
            s1 = jnp.sum(jnp.where(first, 0.0, o2), axis=-1, keepdims=True)
            r = lax.rsqrt(jnp.where(first, s0, s1) * (1.0 / FOX_HEAD_DIM) + EPS)
            oh = o * r
            dfw = dfw + jnp.sum(dfn * oh, axis=0, keepdims=True)
            doh = dfn * fnw_ref[...]
            pr_ = doh * oh
            m0 = jnp.sum(jnp.where(first, pr_, 0.0), axis=-1, keepdims=True)
            m1 = jnp.sum(jnp.where(first, 0.0, pr_), axis=-1, keepdims=True)
            dof_ref[:, cs] = r * (doh - oh * jnp.where(first, m0, m1) * (1.0 / FOX_HEAD_DIM))
        for hd in range(N_GDN_HEADS):
            cs = slice(hd * LANES, (hd + 1) * LANES)
            go = go_ref[:, cs]
            gz = gz_ref[:, cs]
            dgated = domix[:, D_FOX + hd * LANES:D_FOX + (hd + 1) * LANES]
            r = lax.rsqrt(jnp.mean(go * go, axis=-1, keepdims=True) + EPS)
            goh = go * r
            sg = _sigmoid(gz)
            sz = gz * sg
            gn = goh * gnw_ref[...]
            dgn = dgated * sz
            dgz_ref[:, cs] = (dgated * gn * sg * (1.0 + gz * (1.0 - sg))).astype(BF)
            dgw = dgw + jnp.sum(dgn * goh, axis=0, keepdims=True)
            dgh = dgn * gnw_ref[...]
            dgo_ref[:, cs] = r * (dgh - goh * jnp.mean(dgh * goh, axis=-1, keepdims=True))
        dfw_ref[...] += dfw + pltpu.roll(dfw, FOX_HEAD_DIM, 1)
        dgw_ref[...] += dgw

    tok = lambda w: pl.BlockSpec((tm, w), lambda i: (i, 0))
    vec = lambda w: pl.BlockSpec((1, w), lambda i: (0, 0))
    return pl.pallas_call(
        body, name="out_bwd", grid=(t // tm,),
        in_specs=[tok(D_MODEL), pl.BlockSpec((D_MODEL, D_MODEL), lambda i: (0, 0)), tok(D_FOX), tok(D_GDN),
                  pl.BlockSpec((tm, D_GDN), lambda i: (i, COL_GZ // D_GDN)), vec(LANES), vec(LANES)],
        out_specs=[tok(D_FOX), tok(D_GDN), tok(D_GDN), vec(LANES), vec(LANES)],
        out_shape=[_sds((t, D_FOX)), _sds((t, D_GDN)), _sds((t, D_GDN), BF), _sds((1, LANES)), _sds((1, LANES))],
        compiler_params=_params("arbitrary"),
    )(dmixed, w_out, o_fox, gdn_o, proj, fnw, gnw)


def _mlp_up(h2, w_upt):
    t = h2.shape[0]
    tm = min(MATMUL_BLOCK, t)

    def body(h_ref, w_ref, up_ref):
        up_ref[...] = lax.dot_general(h_ref[...], w_ref[...], (((1,), (1,)), ((), ())),
                                      preferred_element_type=F32).astype(BF)

    return pl.pallas_call(
        body, name="mlp_up", grid=(t // tm,),
        in_specs=[pl.BlockSpec((tm, D_MODEL), lambda i: (i, 0)), pl.BlockSpec((D_FF, D_MODEL), lambda i: (0, 0))],
        out_specs=pl.BlockSpec((tm, D_FF), lambda i: (i, 0)), out_shape=_sds((t, D_FF), BF),
        compiler_params=_params("parallel"),
    )(h2, w_upt)


def _mlp_down_loss(up, w_down, x1, pw, target):
    t = up.shape[0]
    tm = min(MATMUL_BLOCK, t)

    def body(up_ref, w_ref, x1_ref, pw_ref, tg_ref, dy_ref, dx2_ref, loss_ref, dpw_ref):
        i = pl.program_id(0)

        @pl.when(i == 0)
        def _():
            loss_ref[...] = jnp.zeros_like(loss_ref)
            dpw_ref[...] = jnp.zeros_like(dpw_ref)

        u = jnp.maximum(up_ref[...].astype(F32), 0.0)
        y = jnp.dot((u * u).astype(BF), w_ref[...], preferred_element_type=F32)
        r = lax.rsqrt(jnp.mean(y * y, axis=-1, keepdims=True) + EPS)
        yh = y * r
        pw = pw_ref[...]
        err = x1_ref[...] + yh * pw - tg_ref[...]
        part = jnp.sum(jnp.sum(err * err, axis=-1, keepdims=True), axis=0, keepdims=True) * (0.5 / D_MODEL)
        loss_ref[...] += jnp.broadcast_to(part, loss_ref.shape)
        dx2 = err * (1.0 / D_MODEL)
        dx2_ref[...] = dx2
        dpw_ref[...] += jnp.sum(dx2 * yh, axis=0, keepdims=True)
        dyh = dx2 * pw
        dy_ref[...] = (r * (dyh - yh * jnp.mean(dyh * yh, axis=-1, keepdims=True))).astype(BF)

    tok = lambda w: pl.BlockSpec((tm, w), lambda i: (i, 0))
    vec = lambda w: pl.BlockSpec((1, w), lambda i: (0, 0))
    return pl.pallas_call(
        body, name="mlp_down_loss", grid=(t // tm,),
        in_specs=[tok(D_FF), pl.BlockSpec((D_FF, D_MODEL), lambda i: (0, 0)), tok(D_MODEL), vec(D_MODEL), tok(D_MODEL)],
        out_specs=[tok(D_MODEL), tok(D_MODEL), vec(LANES), vec(D_MODEL)],
        out_shape=[_sds((t, D_MODEL), BF), _sds((t, D_MODEL)), _sds((1, LANES)), _sds((1, D_MODEL))],
        compiler_params=_params("arbitrary"),
    )(up, w_down, x1, pw, target)


def _mlp_bwd_act(dy, w_down, up):
    t = dy.shape[0]
    tm = min(MATMUL_BLOCK, t)

    def body(dy_ref, w_ref, up_ref, dup_ref):
        da = lax.dot_general(dy_ref[...], w_ref[...], (((1,), (1,)), ((), ())), preferred_element_type=F32)
        dup_ref[...] = (da * (2.0 * jnp.maximum(up_ref[...].astype(F32), 0.0))).astype(BF)

    return pl.pallas_call(
        body, name="mlp_bwd_act", grid=(t // tm,),
        in_specs=[pl.BlockSpec((tm, D_MODEL), lambda i: (i, 0)), pl.BlockSpec((D_FF, D_MODEL), lambda i: (0, 0)),
                  pl.BlockSpec((tm, D_FF), lambda i: (i, 0))],
        out_specs=pl.BlockSpec((tm, D_FF), lambda i: (i, 0)), out_shape=_sds((t, D_FF), BF),
        compiler_params=_params("parallel"),
    )(dy, w_down, up)


def _mlp_bwd_in(dup, w_up, x1, plw, dx2, mixed, pmw):
    t = dup.shape[0]
    tm = min(MATMUL_BLOCK, t)

    def body(dup_ref, w_ref, x1_ref, plw_ref, dx2_ref, mx_ref, pmw_ref, dx1_ref, dmixed_ref, dplw_ref, dpmw_ref):
        i = pl.program_id(0)

        @pl.when(i == 0)
        def _():
            dplw_ref[...] = jnp.zeros_like(dplw_ref)
            dpmw_ref[...] = jnp.zeros_like(dpmw_ref)

        dh = jnp.dot(dup_ref[...], w_ref[...], preferred_element_type=F32)
        x1 = x1_ref[...]
        r = lax.rsqrt(jnp.mean(x1 * x1, axis=-1, keepdims=True) + EPS)
        xh = x1 * r
        dplw_ref[...] += jnp.sum(dh * xh, axis=0, keepdims=True)
        dxh = dh * plw_ref[...]
        dx1 = dx2_ref[...] + r * (dxh - xh * jnp.mean(dxh * xh, axis=-1, keepdims=True))
        dx1_ref[...] = dx1
        mx = mx_ref[...]
        r2 = lax.rsqrt(jnp.mean(mx * mx, axis=-1, keepdims=True) + EPS)
        mh = mx * r2
        dpmw_ref[...] += jnp.sum(dx1 * mh, axis=0, keepdims=True)
        dmh = dx1 * pmw_ref[...]
        dmixed_ref[...] = (r2 * (dmh - mh * jnp.mean(dmh * mh, axis=-1, keepdims=True))).astype(BF)

    tok = lambda w: pl.BlockSpec((tm, w), lambda i: (i, 0))
    vec = lambda w: pl.BlockSpec((1, w), lambda i: (0, 0))
    return pl.pallas_call(
        body, name="mlp_bwd_in", grid=(t // tm,),
        in_specs=[tok(D_FF), pl.BlockSpec((D_FF, D_MODEL), lambda i: (0, 0)), tok(D_MODEL),
                  vec(D_MODEL), tok(D_MODEL), tok(D_MODEL), vec(D_MODEL)],
        out_specs=[tok(D_MODEL), tok(D_MODEL), vec(D_MODEL), vec(D_MODEL)],
        out_shape=[_sds((t, D_MODEL)), _sds((t, D_MODEL), BF), _sds((1, D_MODEL)), _sds((1, D_MODEL))],
        compiler_params=_params("arbitrary"),
    )(dup, w_up, x1, plw, dx2, mixed, pmw)


def _wgrad(a, b, a_cols, split=1, a_fn=None, a_block0=0, name="wgrad"):
    t, b_cols = b.shape
    n_a = (a.shape[1] - a_block0 * a_cols) // a_cols if a_block0 else a.shape[1] // a_cols

    def body(a_ref, b_ref, o_ref):
        av = a_ref[...]
        if a_fn is not None:
            av = a_fn(av)
        o_ref[...] = _mm_tn(av, b_ref[...]).astype(BF).reshape(o_ref.shape)

    return pl.pallas_call(
        body, name=name, grid=(n_a,),
        in_specs=[pl.BlockSpec((t, a_cols), lambda i: (0, i + a_block0)), pl.BlockSpec((t, b_cols), lambda i: (0, 0))],
        out_specs=pl.BlockSpec((split, a_cols // split, b_cols), lambda i: (i, 0, 0)),
        out_shape=_sds((n_a * split, a_cols // split, b_cols), BF),
        compiler_params=_params("parallel"),
    )(a, b)


def _wgrad_pre_t(at, b, b_cols, name):
    rows, t = at.shape
    n_b = b.shape[1] // b_cols

    def body(a_ref, b_ref, o_ref):
        o_ref[0] = jnp.dot(a_ref[...], b_ref[...], preferred_element_type=F32).astype(BF)

    return pl.pallas_call(
        body, name=name, grid=(n_b,),
        in_specs=[pl.BlockSpec((rows, t), lambda j: (0, 0)), pl.BlockSpec((t, b_cols), lambda j: (0, j))],
        out_specs=pl.BlockSpec((1, rows, b_cols), lambda j: (j, 0, 0)), out_shape=_sds((n_b, rows, b_cols), BF),
        compiler_params=_params("parallel"),
    )(at, b)


def _small_bwd(proj, fb, al, dtb, dcq, dckt, dbe, dge):
    t = proj.shape[0]

    def body(sm_ref, fb_ref, al_ref, dtb_ref, dcq_ref, dckt_ref, dbe_ref, dge_ref, dsm_ref, dvec_ref):
        s = sm_ref[...]
        lane = _iota((1, LANES), 1)
        dcum = dcq_ref[...] - dckt_ref[...].T
        row = _iota((t, LANES), 0)
        step = 1
        while step < t:
            dcum = dcum + _shift_up(dcum, step, row)
            step *= 2
        dff = dcum * _sigmoid(-(s + fb_ref[...]))
        dbeta = jnp.zeros((t, LANES), F32)
        dg = jnp.zeros((t, LANES), F32)
        for hd in range(N_GDN_HEADS):
            dbeta = jnp.where(lane == SM_GB + hd, dbe_ref[:, hd * LANES:hd * LANES + 1], dbeta)
            dg = jnp.where(lane == SM_GA + hd, dge_ref[:, hd * LANES:hd * LANES + 1], dg)
        beta = _sigmoid(s)
        dgb = dbeta * beta * (1.0 - beta)
        za = s + dtb_ref[...]
        nea = -jnp.exp(al_ref[...])
        dga = dg * nea * _sigmoid(za)
        is_f = lane < SM_GB
        is_b = (lane >= SM_GB) & (lane < SM_GA)
        is_a = (lane >= SM_GA) & (lane < SM_GA + 4)
        dsm_ref[...] = jnp.where(is_f, dff, jnp.where(is_b, dgb, jnp.where(is_a, dga, 0.0))).astype(BF)
        dvec_ref[...] = jnp.zeros_like(dvec_ref)
        dvec_ref[0:1, :] = jnp.sum(jnp.where(is_f, dff, 0.0), axis=0, keepdims=True)
        dvec_ref[1:2, :] = jnp.sum(jnp.where(is_a, dg * nea * _softplus(za), 0.0), axis=0, keepdims=True)
        dvec_ref[2:3, :] = jnp.sum(jnp.where(is_a, dga, 0.0), axis=0, keepdims=True)

    vec = pl.BlockSpec((1, LANES), lambda i: (0, 0))
    full = lambda r, c: pl.BlockSpec((r, c), lambda i: (0, 0))
    return pl.pallas_call(
        body, name="small_bwd", grid=(1,),
        in_specs=[pl.BlockSpec((t, LANES), lambda i: (0, COL_SMALL // LANES)), vec, vec, vec, full(t, LANES),
                  full(LANES, t), full(t, 512), full(t, 512)],
        out_specs=[full(t, LANES), full(8, LANES)], out_shape=[_sds((t, LANES), BF), _sds((8, LANES))],
        compiler_params=_params("arbitrary"),
    )(proj, fb, al, dtb, dcq, dckt, dbe, dge)


def _pack_dproj(dfox, dgdn, dgz, dsm):
    t = dgz.shape[0]
    tm = min(MATMUL_BLOCK, t)

    def body(*refs):
        parts, dp_ref = refs[:8], refs[8]
        col = 0
        for part in parts:
            width = part.shape[1]
            dp_ref[:, col:col + width] = part[...].astype(BF)
            col += width

    tok = lambda w: pl.BlockSpec((tm, w), lambda i: (i, 0))
    return pl.pallas_call(
        body, name="pack_dproj", grid=(t // tm,), in_specs=[tok(D_FOX)] * 3 + [tok(D_GDN)] * 4 + [tok(LANES)],
        out_specs=tok(PROJ_W), out_shape=_sds((t, PROJ_W), BF), compiler_params=_params("parallel"),
    )(*dfox, *dgdn, dgz, dsm)


def _in_bwd(dproj, wt_al, x, nw, dx1):
    t = x.shape[0]
    tm = min(MATMUL_BLOCK, t)

    def body(dp_ref, w_ref, x_ref, nw_ref, dx1_ref, dx_ref, dnw_ref):
        i = pl.program_id(0)

        @pl.when(i == 0)
        def _():
            dnw_ref[...] = jnp.zeros_like(dnw_ref)

        dh = jnp.dot(dp_ref[...], w_ref[...], preferred_element_type=F32)
        xv = x_ref[...]
        r = lax.rsqrt(jnp.mean(xv * xv, axis=-1, keepdims=True) + EPS)
        xh = xv * r
        dnw_ref[...] += jnp.sum(dh * xh, axis=0, keepdims=True)
        dxh = dh * nw_ref[...]
        dx_ref[...] = dx1_ref[...] + r * (dxh - xh * jnp.mean(dxh * xh, axis=-1, keepdims=True))

    tok = lambda w: pl.BlockSpec((tm, w), lambda i: (i, 0))
    vec = lambda w: pl.BlockSpec((1, w), lambda i: (0, 0))
    return pl.pallas_call(
        body, name="in_bwd", grid=(t // tm,),
        in_specs=[tok(PROJ_W), pl.BlockSpec((PROJ_W, D_MODEL), lambda i: (0, 0)), tok(D_MODEL), vec(D_MODEL),
                  tok(D_MODEL)],
        out_specs=[tok(D_MODEL), vec(D_MODEL)], out_shape=[_sds((t, D_MODEL)), _sds((1, D_MODEL))],
        compiler_params=_params("arbitrary"),
    )(dproj, wt_al, x, nw, dx1)


def _row(v, width=None):
    v = v.reshape(1, -1).astype(F32)
    if width is not None and v.shape[1] < width:
        v = jnp.pad(v, ((0, 0), (0, width - v.shape[1])))
    return v


def _lane_vec(v, first):
    return jnp.pad(v.astype(F32), (first, LANES - first - v.shape[0])).reshape(1, LANES)


def _local_step(x, target, wt_al, late_weights, on_grads, convw, pre_mix_norm, fox_f_bias, fox_out_norm,
                gdn_a_log, gdn_dt_bias, gdn_out_norm, post_mix_norm, pre_mlp_norm, post_mlp_norm):
    t = x.shape[0]
    nch = t // CHUNK
    nw, pmw, plw, pw = _row(pre_mix_norm), _row(post_mix_norm), _row(pre_mlp_norm), _row(post_mlp_norm)
    fb, al, dtb = _lane_vec(fox_f_bias, SM_FF), _lane_vec(gdn_a_log, SM_GA), _lane_vec(gdn_dt_bias, SM_GA)
    fnw = _row(jnp.tile(fox_out_norm, 2))
    gnw = _row(gdn_out_norm)

    proj, h = _norm_proj(x, nw, wt_al)
    cumt, beta, g = _small_prep(proj, fb, al, dtb)
    o_fox, lse, fox_n = _fox_fwd(proj, cumt, fnw)
    qn, kn, cv, gc, be, mmat, amat = _gdn_prep(proj, convw, beta, g)
    n_prob = N_GDN_HEADS * nch
    m3 = mmat.reshape(n_prob, CHUNK, CHUNK)
    if n_prob < LANES:
        m3 = jnp.pad(m3, ((0, LANES - n_prob), (0, 0), (0, 0)))
    tinv = _tri_inverse(m3)[:n_prob].reshape(N_GDN_HEADS, nch, CHUNK, CHUNK)
    token = late_weights("mlp_relay", tinv)
    gdn_o, s_all, vn_all = _gdn_scan(qn, kn, cv, be, gc, tinv, amat)
    w_out = late_weights("w_out", gdn_o)
    x1, h2, mixed, omix, h2t = _mix_out(fox_n, gdn_o, proj, gnw + token[0:1, 0:1], w_out, x, pmw, plw)
    w_up, w_down = late_weights("mlp", h2)
    up = _mlp_up(h2, w_up)
    dy, dx2, loss, d_pw = _mlp_down_loss(up, w_down, x1, pw, target)

    dup = _mlp_bwd_act(dy, w_down, up)
    relu2 = lambda u: jnp.square(jnp.maximum(u.astype(F32), 0.0))
    g_down = _wgrad(up, dy, D_FF // N_DEV, a_fn=relu2, name="wgrad_down")
    g_up = _wgrad_pre_t(h2t, dup, D_FF // N_DEV, name="wgrad_up")
    token = on_grads("mlp", (g_up, g_down))
    dx1, dmixed, d_plw, d_pmw = _mlp_bwd_in(dup, w_up, x1, plw + token[0:1, 0:1], dx2, mixed, pmw)
    token = on_grads("w_out", _wgrad(omix, dmixed, 512, split=4, name="wgrad_out"))
    do_fox, dgo, dgz, d_fnw, d_gnw = _out_bwd(dmixed, w_out, o_fox, gdn_o, proj, fnw + token[0:1, 0:1], gnw)
    dfq, dfk, dfv, dcq, dckt = _fox_bwd(proj, cumt, lse, o_fox, do_fox)
    dqn, dkn, dcv, dbe, dge = _gdn_bwd(qn, kn, cv, be, gc, tinv, amat, s_all, vn_all, dgo)
    dxq, dxk, dxv, dwq, dwk, dwv = _gdn_bwd_conv(proj, convw, dqn, dkn, dcv)
    dsm, dvec = _small_bwd(proj, fb, al, dtb, dcq, dckt, dbe, dge)
    dproj = _pack_dproj((dfq, dfk, dfv), (dxq, dxk, dxv), dgz, dsm)
    g_main = _wgrad(dproj, h, WGRAD_IN_ROWS, name="wgrad_in")
    g_tail = _wgrad(dproj, h, LANES, a_block0=COL_SMALL // LANES, name="wgrad_in_small")
    token = on_grads("w_in", (g_main, g_tail))
    grad_x, d_nw = _in_bwd(dproj, wt_al, x, nw + token[0:1, 0:1], dx1)
    small = dict(norms=(d_nw, d_pmw, d_plw, d_pw), fox_out_norm=d_fnw, gdn_out_norm=d_gnw, loss=loss, vectors=dvec,
                 conv=(dwq, dwk, dwv))
    return grad_x, small


MESH_IDS = pl.DeviceIdType.MESH
CHIP_FLIPS = ((0, 0), (1, 0), (0, 1), (1, 1))
ANY_SPEC = pl.BlockSpec(memory_space=pl.ANY)


def _place():
    return lax.axis_index("x"), lax.axis_index("y"), lax.axis_index("c")


def _all_gather(blocks):
    n = len(blocks)

    def body(*refs):
        ins, outs, (send_sems, recv_sems, local_sems) = refs[:n], refs[n:2 * n], refs[2 * n:]
        x, y, c = _place()
        sibling = (x, y, 1 - c)
        chips = [(x ^ fx, y ^ fy) for fx, fy in CHIP_FLIPS[1:]]

        def slot(out, px, py, pc):
            return out.at[4 * px + 2 * py + pc]

        def copy(a, k, block, to, src=None):
            return pltpu.make_async_remote_copy(
                src_ref=slot(outs[a], *block) if src is None else src, dst_ref=slot(outs[a], *block),
                send_sem=send_sems.at[a, k], recv_sem=recv_sems.at[a, k], device_id=to, device_id_type=MESH_IDS)

        pending = []
        for a in range(n):
            mine = pltpu.make_async_copy(ins[a], slot(outs[a], x, y, c), local_sems.at[a])
            mine.start()
            pending.append(mine)
        sends = []
        for a in range(n):
            first = [copy(a, 0, (x, y, c), sibling, src=ins[a])]
            first += [copy(a, 1 + j, (x, y, c), (*chip, c), src=ins[a]) for j, chip in enumerate(chips)]
            for cp in first:
                cp.start()
            sends += first
        for a in range(n):
            for j, chip in enumerate(chips):
                copy(a, 1 + j, (*chip, c), (x, y, c)).wait_recv()
                fwd = copy(a, 4 + j, (*chip, c), sibling)
                fwd.start()
                sends.append(fwd)
        for a in range(n):
            copy(a, 0, sibling, (x, y, c)).wait_recv()
            for j, chip in enumerate(chips):
                copy(a, 4 + j, (*chip, 1 - c), (x, y, c)).wait_recv()
        for cp in sends:
            cp.wait_send()
        for cp in pending:
            cp.wait()

    return pl.pallas_call(
        body, name="all_gather_weights", in_specs=[ANY_SPEC] * n, out_specs=[ANY_SPEC] * n,
        out_shape=[_sds((N_DEV,) + b.shape, b.dtype) for b in blocks],
        scratch_shapes=[pltpu.SemaphoreType.DMA((n, 7)), pltpu.SemaphoreType.DMA((n, 7)), pltpu.SemaphoreType.DMA((n,))],
        compiler_params=pltpu.CompilerParams(has_side_effects=True),
    )(*blocks)


def _adamw(w, g, m, v):
    m = ADAM_B1 * m + (1.0 - ADAM_B1) * g
    v = ADAM_B2 * v + (1.0 - ADAM_B2) * (g * g)
    m_hat = m / (1.0 - ADAM_B1 ** ADAM_STEP)
    v_hat = v / (1.0 - ADAM_B2 ** ADAM_STEP)
    return -ADAM_LR * (m_hat / (jnp.sqrt(v_hat) + ADAM_EPS) + ADAM_WD * w), m, v


def _pair_reduce(g, name):
    _, r, c_ = g.shape
    n = len(CHIP_FLIPS)

    def body(g_ref, out_ref, sib_buf, send_sems, recv_sems):
        x, y, c = _place()
        chips = [(x ^ fx, y ^ fy) for fx, fy in CHIP_FLIPS]
        piece = lambda chip, core: g_ref.at[4 * chip[0] + 2 * chip[1] + core]
        copies = [pltpu.make_async_remote_copy(
            src_ref=piece(chip, 1 - c), dst_ref=sib_buf.at[j], send_sem=send_sems.at[j], recv_sem=recv_sems.at[j],
            device_id=(x, y, 1 - c), device_id_type=MESH_IDS) for j, chip in enumerate(chips)]
        for cp in copies:
            cp.start()
        for j, chip in enumerate(chips):
            copies[j].wait_recv()
            out_ref[j] = (piece(chip, c)[...].astype(F32) + sib_buf[j].astype(F32)).astype(BF)
        for cp in copies:
            cp.wait_send()

    return pl.pallas_call(
        body, name=name, in_specs=[VMEM_SPEC], out_specs=VMEM_SPEC, out_shape=_sds((n, r, c_), BF),
        scratch_shapes=[pltpu.VMEM((n, r, c_), BF), pltpu.SemaphoreType.DMA((n,)), pltpu.SemaphoreType.DMA((n,))],
        compiler_params=pltpu.CompilerParams(vmem_limit_bytes=VMEM_LIMIT, has_side_effects=True),
    )(g)


HBM_SPEC = pl.BlockSpec(memory_space=pltpu.HBM)
SEM_SPEC = pl.BlockSpec(memory_space=pltpu.SEMAPHORE)
DATAFLOW = pltpu.SideEffectType.DATAFLOW_SIDE_EFFECTING


def _peers():
    x, y, c = _place()
    return 4 * x + 2 * y + c, [(x ^ (k >> 2), y ^ ((k >> 1) & 1), c ^ (k & 1)) for k in range(1, N_DEV)]


def _peer_index(peer):
    return 4 * peer[0] + 2 * peer[1] + peer[2]


def _zones_with_own(srcs, pieces, name, after=None, dtype=None, chips=False):
    n = len(srcs)
    slots = len(CHIP_FLIPS) if chips else N_DEV
    extra = [] if after is None else [after]
    dtypes = [s_.dtype if pieces or dtype is None else dtype for s_ in srcs]

    def body(me_ref, *refs):
        outs = refs[n + len(extra):]
        for a in range(n):
            if pieces:
                outs[a][0] = refs[a][0]
            else:
                val = refs[a][...].astype(dtypes[a])
                outs[a][0] = val
                outs[n + a][...] = val

    shapes = [s_.shape[1:] if pieces else s_.shape for s_ in srcs]
    mine = lambda sh: pl.BlockSpec((1,) + sh, lambda i, me_ref: (me_ref[0], 0, 0))
    whole = lambda sh: pl.BlockSpec(sh, lambda i, me_ref: (0, 0))
    in_specs = [mine(sh) if pieces else whole(sh) for sh in shapes]
    out_specs = [mine(sh) for sh in shapes] + ([] if pieces else [whole(sh) for sh in shapes])
    out_shape = [_sds((slots,) + sh, dt) for sh, dt in zip(shapes, dtypes)]
    out_shape += [] if pieces else [_sds(sh, dt) for sh, dt in zip(shapes, dtypes)]
    x, y, c = _place()
    own = 0 * x if chips else 4 * x + 2 * y + c
    out = pl.pallas_call(
        body, name=name,
        grid_spec=pltpu.PrefetchScalarGridSpec(num_scalar_prefetch=1, grid=(1,), in_specs=in_specs + [ANY_SPEC] * len(extra),
                                               out_specs=out_specs),
        out_shape=out_shape, compiler_params=_params("arbitrary"),
    )(own.astype(jnp.int32).reshape(1), *srcs, *extra)
    return out[:n], (list(srcs) if pieces else out[n:])


def _exchange_start(srcs, zones, pieces, name, chips=False):
    n = len(srcs)

    def body(*refs):
        ins, zs = refs[:n], refs[n:2 * n]
        sems = refs[2 * n:4 * n]
        token = refs[-1]
        me, peers = _peers()
        x, y, c = _place()
        if chips and pieces:
            routes = [((x ^ fx, y ^ fy, c), j, j) for j, (fx, fy) in enumerate(CHIP_FLIPS) if j]
        elif chips:
            routes = [((x ^ fx, y ^ fy, c), None, me) for fx, fy in CHIP_FLIPS[1:]]
        else:
            routes = [(peer, _peer_index(peer) if pieces else None, me) for peer in peers]
        for peer, src_slot, dst_slot in routes:
            for a in range(n):
                pltpu.make_async_remote_copy(
                    src_ref=ins[a] if src_slot is None else ins[a].at[src_slot], dst_ref=zs[a].at[dst_slot],
                    send_sem=sems[2 * a], recv_sem=sems[2 * a + 1], device_id=peer, device_id_type=MESH_IDS).start()
        token[...] = jnp.zeros_like(token)

    hbm = lambda v: pltpu.with_memory_space_constraint(v, pltpu.HBM)
    out = pl.pallas_call(
        body, name=name,
        out_shape=tuple([pltpu.SemaphoreType.DMA(())] * (2 * n) + [pltpu.HBM(v.shape, v.dtype) for v in srcs]
                        + [pltpu.HBM(z.shape, z.dtype) for z in zones] + [_sds((8, LANES))]),
        in_specs=[HBM_SPEC] * (2 * n), out_specs=tuple([SEM_SPEC] * (2 * n) + [HBM_SPEC] * (2 * n) + [VMEM_SPEC]),
        input_output_aliases={i: 2 * n + i for i in range(2 * n)},
        compiler_params=pltpu.CompilerParams(has_side_effects=DATAFLOW),
    )(*[hbm(v) for v in srcs], *[hbm(z) for z in zones])
    return out[:2 * n], out[2 * n:3 * n], out[3 * n:4 * n], out[-1]


def _relay_start(zones, name):
    n = len(zones)

    def body(*refs):
        zs, sems, token = refs[:n], refs[n:3 * n], refs[-1]
        x, y, c = _place()
        for fx, fy in CHIP_FLIPS:
            slot = 4 * (x ^ fx) + 2 * (y ^ fy) + c
            for a in range(n):
                pltpu.make_async_remote_copy(
                    src_ref=zs[a].at[slot], dst_ref=zs[a].at[slot], send_sem=sems[2 * a], recv_sem=sems[2 * a + 1],
                    device_id=(x, y, 1 - c), device_id_type=MESH_IDS).start()
        token[...] = jnp.zeros_like(token)

    out = pl.pallas_call(
        body, name=name,
        out_shape=tuple([pltpu.SemaphoreType.DMA(())] * (2 * n) + [pltpu.HBM(z.shape, z.dtype) for z in zones]
                        + [_sds((8, LANES))]),
        in_specs=[HBM_SPEC] * n, out_specs=tuple([SEM_SPEC] * (2 * n) + [HBM_SPEC] * n + [VMEM_SPEC]),
        input_output_aliases={i: 2 * n + i for i in range(n)},
        compiler_params=pltpu.CompilerParams(has_side_effects=DATAFLOW),
    )(*[pltpu.with_memory_space_constraint(z, pltpu.HBM) for z in zones])
    return out[:2 * n], [], out[2 * n:3 * n], out[-1]


def _exchange_wait(sems, srcs, zones, after, name, chips=False, n_copies=None):
    n, n_src = len(zones), len(srcs)
    after = list(after) if isinstance(after, (list, tuple)) else [after]
    n_copies = n_copies or (len(CHIP_FLIPS) - 1 if chips else N_DEV - 1)

    def body(*refs):
        zs, sm = refs[n_src:n_src + n], refs[n_src + n:n_src + 3 * n]
        me, peers = _peers()
        for a in range(n):
            seven = zs[a].at[pl.ds(0, n_copies)]
            cp = pltpu.make_async_remote_copy(src_ref=seven, dst_ref=seven, send_sem=sm[2 * a], recv_sem=sm[2 * a + 1],
                                              device_id=peers[0], device_id_type=MESH_IDS)
            cp.wait_send()
            cp.wait_recv()

    out = pl.pallas_call(
        body, name=name, out_shape=tuple([pltpu.HBM(v.shape, v.dtype) for v in srcs] + [pltpu.HBM(z.shape, z.dtype) for z in zones]),
        in_specs=[HBM_SPEC] * (n_src + n) + [SEM_SPEC] * (2 * n) + [ANY_SPEC] * len(after),
        out_specs=tuple([HBM_SPEC] * (n_src + n)), input_output_aliases={i: i for i in range(n_src + n)},
        compiler_params=pltpu.CompilerParams(has_side_effects=DATAFLOW),
    )(*srcs, *zones, *sems, *after)
    return out[n_src:]


def _sum_adamw(zone, w, m, v, name):
    n_slots, r, c_ = zone.shape
    rb = next((b for b in (256, 128) if r % b == 0), r)

    def body(z_ref, w_ref, m_ref, v_ref, grad_ref, delta_ref, nm_ref, nv_ref):
        total = z_ref[0].astype(F32)
        for d in range(1, n_slots):
            total = total + z_ref[d].astype(F32)
        grad_ref[...] = total
        delta_ref[...], nm_ref[...], nv_ref[...] = _adamw(w_ref[...], total, m_ref[...], v_ref[...])

    blk = pl.BlockSpec((rb, c_), lambda i: (i, 0))
    return pl.pallas_call(
        body, name=name, grid=(r // rb,), in_specs=[pl.BlockSpec((n_slots, rb, c_), lambda i: (0, i, 0)), blk, blk, blk],
        out_specs=[blk] * 4, out_shape=[_sds((r, c_))] * 4, compiler_params=_params("parallel"),
    )(zone, w, m, v)


SMALL_NORMS = ("pre_mix_norm", "post_mix_norm", "pre_mlp_norm", "post_mlp_norm")
SMALL_ORDER = SMALL_NORMS + ("fox_out_norm", "gdn_out_norm", "fox_f_bias", "gdn_a_log", "gdn_dt_bias", "gdn_conv_w")
CONV_SLAB_ROWS, CONV_SLAB_LANES = 8, 256


def _small_pack(small):
    def body(n0, n1, n2, n3, fnw_ref, gnw_ref, loss_ref, vec_ref, out_ref):
        out_ref[...] = jnp.zeros_like(out_ref)
        for i, ref in enumerate((n0, n1, n2, n3)):
            out_ref[i:i + 1, :] = ref[...]
        out_ref[4:5, 0:LANES] = fnw_ref[...]
        out_ref[4:5, LANES:2 * LANES] = gnw_ref[...]
        out_ref[4:5, 2 * LANES:3 * LANES] = loss_ref[...]
        out_ref[5:8, 0:LANES] = vec_ref[0:3, :]

    return pl.pallas_call(body, name="small_pack", in_specs=[VMEM_SPEC] * 8, out_specs=VMEM_SPEC,
                          out_shape=_sds((8, D_MODEL)))(*small["norms"], small["fox_out_norm"], small["gdn_out_norm"],
                                                        small["loss"], small["vectors"])


def _conv_slabs(dconv):
    blocks = dconv.reshape(CONV_K, N_DEV, -1).transpose(1, 0, 2)
    blocks = jnp.pad(blocks, ((0, 0), (0, CONV_SLAB_ROWS - CONV_K), (0, CONV_SLAB_LANES - blocks.shape[2])))
    return blocks.reshape(N_DEV * CONV_SLAB_ROWS, CONV_SLAB_LANES)


def _small_update(zone, conv_zone, w, m, v):
    n = len(SMALL_ORDER)
    n_conv = w["gdn_conv_w"].shape[1]

    def body(me_ref, z_ref, zc_ref, *refs):
        params, loss_ref, outs, (tot, totc) = refs[:3 * n], refs[3 * n], refs[3 * n + 1:7 * n + 1], refs[-2:]
        total, total_c = z_ref[0], zc_ref[0]
        for d in range(1, N_DEV):
            total, total_c = total + z_ref[d], total_c + zc_ref[d]
        tot[...] = total
        totc[...] = total_c
        loss_ref[...] = tot[4, 2 * LANES:2 * LANES + 1]
        mine = totc[pl.ds(pl.multiple_of(me_ref[0] * CONV_SLAB_ROWS, CONV_SLAB_ROWS), CONV_SLAB_ROWS), :]
        g = dict(zip(SMALL_NORMS, (tot[0], tot[1], tot[2], tot[3])))
        g.update(fox_out_norm=tot[4, 0:FOX_HEAD_DIM], gdn_out_norm=tot[4, LANES:LANES + GDN_HEAD_DIM],
                 fox_f_bias=tot[5, SM_FF:SM_FF + N_FOX_HEADS], gdn_a_log=tot[6, SM_GA:SM_GA + N_GDN_HEADS],
                 gdn_dt_bias=tot[7, SM_GA:SM_GA + N_GDN_HEADS], gdn_conv_w=mine[0:CONV_K, 0:n_conv])
        for i, name in enumerate(SMALL_ORDER):
            w_ref, m_ref, v_ref = params[3 * i:3 * i + 3]
            outs[4 * i][...] = g[name]
            outs[4 * i + 1][...], outs[4 * i + 2][...], outs[4 * i + 3][...] = _adamw(w_ref[...], g[name], m_ref[...],
                                                                                     v_ref[...])

    x, y, c = _place()
    operands = [a[name] for name in SMALL_ORDER for a in (w, m, v)]
    out = pl.pallas_call(
        body, name="small_update",
        in_specs=[pl.BlockSpec(memory_space=pltpu.SMEM)] + [VMEM_SPEC] * (2 + 3 * n), out_specs=[VMEM_SPEC] * (1 + 4 * n),
        out_shape=[_sds((1,))] + [_sds(w[name].shape) for name in SMALL_ORDER for _ in range(4)],
        scratch_shapes=[pltpu.VMEM(zone.shape[1:], F32), pltpu.VMEM(conv_zone.shape[1:], F32)],
    )((4 * x + 2 * y + c).astype(jnp.int32).reshape(1), zone, conv_zone, *operands)
    return out[0][0], {name: out[1 + 4 * i:5 + 4 * i] for i, name in enumerate(SMALL_ORDER)}


NATIVE_ROWS = ((0, 1536), (1544, 3080), (3088, 3600), (1536, 1544), (3080, 3088))


W_IN_PIECE = D_PROJ // N_DEV
WGRAD_IN_ROWS = 512
SHUFFLE_LANES = 256


def _to_aligned_moves():
    moves, o = [], 0
    for lo, hi in NATIVE_ROWS:
        r = lo
        while r < hi:
            d = r // W_IN_PIECE
            k = min(hi, (d + 1) * W_IN_PIECE) - r
            moves.append((0, d, r - d * W_IN_PIECE, 0, o, k))
            r, o = r + k, o + k
    return moves


def _from_aligned_moves():
    moves = []
    for _, d, a, _, o, k in _to_aligned_moves():
        while k:
            n = min(k, WGRAD_IN_ROWS - o % WGRAD_IN_ROWS) if o < COL_SMALL else k
            moves.append((0, o // WGRAD_IN_ROWS, o % WGRAD_IN_ROWS, d, a, n) if o < COL_SMALL else
                         (1, 0, o - COL_SMALL, d, a, n))
            o, a, k = o + n, a + n, k - n
    return moves


def _shuffle_rows(srcs, moves, out_shape, name):
    c = srcs[0].shape[-1]

    def body(*refs):
        s_refs, o_ref, s_f, o_f = refs[:len(srcs)], refs[len(srcs)], refs[len(srcs) + 1:-1], refs[-1]
        for s_ref, f in zip(s_refs, s_f):
            f[...] = s_ref[...].astype(F32)
        o_f[...] = jnp.zeros_like(o_f)
        for i, ss, so, ds, do, k in moves:
            o_f[ds, pl.ds(do, k), :] = s_f[i][ss, pl.ds(so, k), :]
        o_ref[...] = o_f[...].astype(BF)

    blk = lambda shape: pl.BlockSpec(tuple(shape[:-1]) + (SHUFFLE_LANES,), lambda j: (0, 0, j))
    scratch = lambda shape: pltpu.VMEM(tuple(shape[:-1]) + (SHUFFLE_LANES,), F32)
    return pl.pallas_call(
        body, name=name, grid=(c // SHUFFLE_LANES,), in_specs=[blk(s.shape) for s in srcs], out_specs=blk(out_shape),
        out_shape=_sds(out_shape, BF), scratch_shapes=[scratch(s.shape) for s in srcs] + [scratch(out_shape)],
        compiler_params=_params("parallel"),
    )(*srcs)


def _cols_from_pieces(p):
    return p.transpose(1, 0, 2).reshape(p.shape[1], -1)


WEIGHT_ORDER = ("pre_mix_norm", "w_in", "fox_f_bias", "fox_out_norm", "gdn_conv_w", "gdn_a_log", "gdn_dt_bias",
                "gdn_out_norm", "w_out", "post_mix_norm", "pre_mlp_norm", "w_up", "w_down", "post_mlp_norm")


def kernel(x, pre_mix_norm, w_in, fox_f_bias, fox_out_norm, gdn_conv_w, gdn_a_log, gdn_dt_bias, gdn_out_norm, w_out, post_mix_norm, pre_mlp_norm, w_up, w_down, post_mlp_norm, loss_target, m_pre_mix_norm, m_w_in, m_fox_f_bias, m_fox_out_norm, m_gdn_conv_w, m_gdn_a_log, m_gdn_dt_bias, m_gdn_out_norm, m_w_out, m_post_mix_norm, m_pre_mlp_norm, m_w_up, m_w_down, m_post_mlp_norm, v_pre_mix_norm, v_w_in, v_fox_f_bias, v_fox_out_norm, v_gdn_conv_w, v_gdn_a_log, v_gdn_dt_bias, v_gdn_out_norm, v_w_out, v_post_mix_norm, v_pre_mlp_norm, v_w_up, v_w_down, v_post_mlp_norm):
    w = dict(pre_mix_norm=pre_mix_norm, w_in=w_in, fox_f_bias=fox_f_bias, fox_out_norm=fox_out_norm,
             gdn_conv_w=gdn_conv_w, gdn_a_log=gdn_a_log, gdn_dt_bias=gdn_dt_bias, gdn_out_norm=gdn_out_norm, w_out=w_out,
             post_mix_norm=post_mix_norm, pre_mlp_norm=pre_mlp_norm, w_up=w_up, w_down=w_down, post_mlp_norm=post_mlp_norm)
    mom = dict(pre_mix_norm=m_pre_mix_norm, w_in=m_w_in, fox_f_bias=m_fox_f_bias, fox_out_norm=m_fox_out_norm,
               gdn_conv_w=m_gdn_conv_w, gdn_a_log=m_gdn_a_log, gdn_dt_bias=m_gdn_dt_bias, gdn_out_norm=m_gdn_out_norm,
               w_out=m_w_out, post_mix_norm=m_post_mix_norm, pre_mlp_norm=m_pre_mlp_norm, w_up=m_w_up, w_down=m_w_down,
               post_mlp_norm=m_post_mlp_norm)
    var = dict(pre_mix_norm=v_pre_mix_norm, w_in=v_w_in, fox_f_bias=v_fox_f_bias, fox_out_norm=v_fox_out_norm,
               gdn_conv_w=v_gdn_conv_w, gdn_a_log=v_gdn_a_log, gdn_dt_bias=v_gdn_dt_bias, gdn_out_norm=v_gdn_out_norm,
               w_out=v_w_out, post_mix_norm=v_post_mix_norm, pre_mlp_norm=v_pre_mlp_norm, w_up=v_w_up, w_down=v_w_down,
               post_mlp_norm=v_post_mlp_norm)

    win_g, conv_g = _all_gather([w_in.T.astype(BF), gdn_conv_w])
    wt_al = _shuffle_rows([win_g], _to_aligned_moves(), (1, PROJ_W, D_MODEL), "w_in_to_aligned")[0]
    convw = _cols_from_pieces(conv_g)
    gathers, after = {}, win_g
    for name, shards in (("w_out", [w_out]), ("mlp", [w_up.T, w_down])):
        zones, shards = _zones_with_own(shards, False, "gather_" + name + "_own", after=after, dtype=BF)
        gathers[name] = _exchange_start(shards, zones, False, "gather_" + name + "_start", chips=name == "mlp")
        after = gathers[name][3]

    def late_weights(name, after):
        if name == "mlp_relay":
            sems, shards, zones, _ = gathers["mlp"]
            zones = _exchange_wait(sems, shards, zones, after, "gather_mlp_wait", chips=True)
            gathers["mlp"] = _relay_start(zones, "gather_mlp_relay")
            return gathers["mlp"][3]
        sems, shards, zones, _ = gathers[name]
        got = _exchange_wait(sems, shards, zones, after, "gather_" + name + "_done",
                             n_copies=len(CHIP_FLIPS) if name == "mlp" else None)
        if name == "w_out":
            return got[0].reshape(D_MODEL, D_MODEL)
        return got[0].reshape(D_FF, D_MODEL), got[1].reshape(D_FF, D_MODEL)

    scatters = {}

    def on_grads(name, g):
        chips = name == "w_in"
        if name == "w_in":
            g = _shuffle_rows(list(g), _from_aligned_moves(), (N_DEV, W_IN_PIECE, D_MODEL), "w_in_grad_from_aligned")
            g = _pair_reduce(g, "pair_reduce_w_in")
        srcs = list(g) if name == "mlp" else [g]
        zones, _ = _zones_with_own(srcs, True, "scatter_" + name + "_own", chips=chips)
        scatters[name] = _exchange_start(srcs, zones, True, "scatter_" + name + "_start", chips=chips)
        return scatters[name][3]

    grad_x, small = _local_step(
        x[0], loss_target[0], wt_al, late_weights, on_grads, convw, pre_mix_norm + after[0, 0],
        fox_f_bias, fox_out_norm, gdn_a_log, gdn_dt_bias, gdn_out_norm, post_mix_norm, pre_mlp_norm, post_mlp_norm)
    slabs = [_small_pack(small), _conv_slabs(jnp.concatenate(small["conv"], axis=1))]
    zones, slabs = _zones_with_own(slabs, False, "small_own")
    scatters["small"] = _exchange_start(slabs, zones, False, "small_start")

    grads, delta, new_m, new_v = {}, {}, {}, {}
    after = scatters["small"][3]
    for name, members in (("mlp", ("w_up", "w_down")), ("w_out", ("w_out",)), ("small", ()), ("w_in", ("w_in",))):
        sems, srcs, zones, _ = scatters[name]
        zones = _exchange_wait(sems, srcs, zones, after, "scatter_" + name + "_wait", chips=name == "w_in")
        if name == "small":
            loss, updated = _small_update(zones[0], zones[1], w, mom, var)
            for n, res in updated.items():
                grads[n], delta[n], new_m[n], new_v[n] = res
            after = grads["pre_mix_norm"]
        for n, zone in zip(members, zones):
            if n == "w_in":
                res = _sum_adamw(zone, w[n].T, mom[n].T, var[n].T, "adamw_" + n)
                grads[n], delta[n], new_m[n], new_v[n] = [r.T for r in res]
            else:
                grads[n], delta[n], new_m[n], new_v[n] = _sum_adamw(zone, w[n], mom[n], var[n], "adamw_" + n)
        if members:
            after = [grads[n] for n in members]

    return (loss, grad_x[None], *[grads[n] for n in WEIGHT_ORDER], *[delta[n] for n in WEIGHT_ORDER],
            *[new_m[n] for n in WEIGHT_ORDER], *[new_v[n] for n in WEIGHT_ORDER])
```

```python
import jax
import jax.numpy as jnp
from jax import lax
from jax.experimental import pallas as pl
from jax.experimental.pallas import tpu as pltpu

F32 = jnp.float32
BF = jnp.bfloat16

D_MODEL = 1024
N_FOX_HEADS, FOX_HEAD_DIM = 8, 64
N_GDN_HEADS, GDN_HEAD_DIM = 4, 128
D_FOX = N_FOX_HEADS * FOX_HEAD_DIM
D_GDN = N_GDN_HEADS * GDN_HEAD_DIM
CHUNK = 64
CONV_K = 4
D_FF = 4 * D_MODEL
EPS = 1e-6
D_PROJ = 3600
N_DEV = 8

PROJ_W = 3712
COL_FOX, COL_GDN, COL_GZ, COL_SMALL = 0, 1536, 3072, 3584
LANES = 128
SM_FF, SM_GB, SM_GA = 0, 8, 12

ADAM_LR, ADAM_B1, ADAM_B2, ADAM_EPS, ADAM_WD, ADAM_STEP = 0.001, 0.9, 0.999, 1e-08, 0.01, 10

TOKEN_BLOCK = 256
MATMUL_BLOCK = 512
FOX_SCALE = FOX_HEAD_DIM ** -0.5
GDN_QSCALE = GDN_HEAD_DIM ** -0.5
NEG_BIG = -1e30
VMEM_LIMIT = 56 * 1024 * 1024

VMEM_SPEC = pl.BlockSpec(memory_space=pltpu.VMEM)
ANY_SPEC = pl.BlockSpec(memory_space=pl.ANY)


def _sds(shape, dtype=F32):
    return jax.ShapeDtypeStruct(shape, dtype)


def _params(*sem):
    return pltpu.CompilerParams(dimension_semantics=sem if sem else None, vmem_limit_bytes=VMEM_LIMIT)


def _ordered(body):
    def ordered(_, *refs):
        body(*refs)

    return ordered


def _mm(a, b):
    return jnp.dot(a.astype(BF), b.astype(BF), preferred_element_type=F32)


def _mm_nt(a, b):
    return lax.dot_general(a.astype(BF), b.astype(BF), (((1,), (1,)), ((), ())), preferred_element_type=F32)


def _mm_tn(a, b):
    return lax.dot_general(a.astype(BF), b.astype(BF), (((0,), (0,)), ((), ())), preferred_element_type=F32)


def _sigmoid(x):
    return 1.0 / (1.0 + jnp.exp(-x))


def _softplus(x):
    return jnp.maximum(x, 0.0) + jnp.log1p(jnp.exp(-jnp.abs(x)))


def _iota(shape, dim):
    return lax.broadcasted_iota(jnp.int32, shape, dim)


def _shift_down(x, s, row):
    return jnp.where(row >= s, pltpu.roll(x, s, 0), 0.0)


def _shift_up(x, s, row):
    n = x.shape[0]
    return jnp.where(row < n - s, pltpu.roll(x, n - s, 0), 0.0)


def _norm_proj(x, nw, wt_al, after):
    t = x.shape[0]

    def body(x_ref, nw_ref, w_ref, proj_ref, h_ref):
        xv = x_ref[...]
        r = lax.rsqrt(jnp.mean(xv * xv, axis=-1, keepdims=True) + EPS)
        h = (xv * r * nw_ref[...]).astype(BF)
        h_ref[...] = h
        proj_ref[...] = lax.dot_general(h, w_ref[...], (((1,), (1,)), ((), ())), preferred_element_type=F32)

    tm = min(MATMUL_BLOCK, t)
    return pl.pallas_call(
        _ordered(body), name="norm_proj", grid=(t // tm,),
        in_specs=[ANY_SPEC,pl.BlockSpec((tm, D_MODEL), lambda i: (i, 0)), pl.BlockSpec((1, D_MODEL), lambda i: (0, 0)),
                  pl.BlockSpec((PROJ_W, D_MODEL), lambda i: (0, 0))],
        out_specs=[pl.BlockSpec((tm, PROJ_W), lambda i: (i, 0)), pl.BlockSpec((tm, D_MODEL), lambda i: (i, 0))],
        out_shape=[_sds((t, PROJ_W)), _sds((t, D_MODEL), BF)],
        compiler_params=_params("parallel"),
    )(after, x, nw, wt_al)


def _lane_column(x, lane):
    return jnp.sum(jnp.where(_iota((1, LANES), 1) == lane, x, 0.0), axis=-1, keepdims=True)


def _small_prep(proj, fb, al, dtb):
    t = proj.shape[0]

    def body(sm_ref, fb_ref, al_ref, dtb_ref, cumt_ref, beta_ref, g_ref):
        s = sm_ref[...]
        z = s + fb_ref[...]
        cum = jnp.minimum(z, 0.0) - jnp.log1p(jnp.exp(-jnp.abs(z)))
        row = _iota((t, LANES), 0)
        step = 1
        while step < t:
            cum = cum + _shift_down(cum, step, row)
            step *= 2
        cumt_ref[...] = cum.T
        beta_ref[...] = _sigmoid(s)
        g_ref[...] = -jnp.exp(al_ref[...]) * _softplus(s + dtb_ref[...])

    vec = pl.BlockSpec((1, LANES), lambda i: (0, 0))
    tok = pl.BlockSpec((t, LANES), lambda i: (0, 0))
    return pl.pallas_call(
        body, name="small_prep", grid=(1,),
        in_specs=[pl.BlockSpec((t, LANES), lambda i: (0, COL_SMALL // LANES)), vec, vec, vec],
        out_specs=[pl.BlockSpec((LANES, t), lambda i: (0, 0)), tok, tok],
        out_shape=[_sds((LANES, t)), _sds((t, LANES)), _sds((t, LANES))],
        compiler_params=_params("arbitrary"),
    )(proj, fb, al, dtb)


def _fox_stack(x, first):
    return jnp.concatenate([jnp.where(first, x, 0.0), jnp.where(first, 0.0, x)], axis=0).astype(BF)


def _fox_unstack(y, first):
    n = y.shape[0] // 2
    return jnp.where(first, y[:n], y[n:])


def _fox_logits(q2_i, kb, cumt_ref, pair, i, tq):
    klen = (i + 1) * tq
    s = lax.dot_general(q2_i, kb[:klen], (((1,), (1,)), ((), ())), preferred_element_type=F32)
    upper = _iota((2 * tq, 1), 0) < tq
    s = s - jnp.where(upper, cumt_ref[pl.ds(2 * pair, 1), 0:klen], cumt_ref[pl.ds(2 * pair + 1, 1), 0:klen])
    causal = _iota((2 * tq, tq), 1) <= _iota((2 * tq, tq), 0) % tq
    parts = [(s[:, :klen - tq], 0, klen - tq)] if i else []
    return parts + [(jnp.where(causal, s[:, klen - tq:], NEG_BIG), klen - tq, klen)]


def _fox_fwd(proj, cumt, fnw):
    t = proj.shape[0]
    tq = min(TOKEN_BLOCK, t // 2)
    nq = t // tq

    def body(q_ref, k_ref, v_ref, cumt_ref, fnw_ref, o_ref, lse_ref, fn_ref):
        j = pl.program_id(0)
        first = _iota((1, LANES), 1) < FOX_HEAD_DIM
        kb = k_ref[...].astype(BF)
        vb = v_ref[...].astype(BF)
        for i in range(nq):
            rows = slice(i * tq, (i + 1) * tq)
            q2 = _fox_stack(q_ref[rows, :] * FOX_SCALE, first)
            parts = _fox_logits(q2, kb, cumt_ref, j, i, tq)
            m = jnp.max(parts[-1][0], axis=-1, keepdims=True)
            if i:
                m = jnp.maximum(m, jnp.max(parts[0][0], axis=-1, keepdims=True))
            l = jnp.zeros((2 * tq, 1), F32)
            o = jnp.zeros((2 * tq, LANES), F32)
            for s, lo, hi in parts:
                p = jnp.exp(s - m)
                l = l + jnp.sum(p, axis=-1, keepdims=True)
                o = o + jnp.dot(p.astype(BF), vb[lo:hi], preferred_element_type=F32)
            o_acc = _fox_unstack(o / l, first)
            lse_acc = _fox_unstack(jnp.broadcast_to(m + jnp.log(l), (2 * tq, LANES)), first)
            o_ref[rows, :] = o_acc
            lse_ref[rows, :] = lse_acc
            o2 = o_acc * o_acc
            s0 = jnp.sum(jnp.where(first, o2, 0.0), axis=-1, keepdims=True)
            s1 = jnp.sum(jnp.where(first, 0.0, o2), axis=-1, keepdims=True)
            r = lax.rsqrt(jnp.where(first, s0, s1) * (1.0 / FOX_HEAD_DIM) + EPS)
            fn_ref[rows, :] = (o_acc * r * fnw_ref[...]).astype(BF)

    blk = lambda off: pl.BlockSpec((t, LANES), lambda j: (0, off + j))
    return pl.pallas_call(
        body, name="fox_fwd", grid=(N_FOX_HEADS // 2,),
        in_specs=[blk(0), blk(4), blk(8), pl.BlockSpec((LANES, t), lambda j: (0, 0)),
                  pl.BlockSpec((1, LANES), lambda j: (0, 0))],
        out_specs=[blk(0), blk(0), blk(0)],
        out_shape=[_sds((t, D_FOX)), _sds((t, D_FOX)), _sds((t, D_FOX), BF)],
        compiler_params=_params("parallel"),
    )(proj, proj, proj, cumt, fnw)


def _fox_bwd(proj, cumt, lse, o, do):
    t = proj.shape[0]
    tq = min(TOKEN_BLOCK, t // 2)
    nq = t // tq

    def body(q_ref, k_ref, v_ref, cumt_ref, lse_ref, o_ref, do_ref,
             dq_ref, dk_ref, dv_ref, dcq_ref, dckt_ref, dk_s, dv_s):
        j = pl.program_id(0)

        @pl.when(j == 0)
        def _():
            dcq_ref[...] = jnp.zeros_like(dcq_ref)
            dckt_ref[...] = jnp.zeros_like(dckt_ref)

        lane = _iota((1, LANES), 1)

        first = _iota((1, LANES), 1) < FOX_HEAD_DIM
        kb = k_ref[...].astype(BF)
        vb = v_ref[...].astype(BF)
        dk_s[...] = jnp.zeros_like(dk_s)
        dv_s[...] = jnp.zeros_like(dv_s)
        for i in range(nq):
            rows = slice(i * tq, (i + 1) * tq)
            do_i = do_ref[rows, :]
            prod = do_i * o_ref[rows, :]
            lse_i = lse_ref[rows, :]
            q2 = _fox_stack(q_ref[rows, :] * FOX_SCALE, first)
            do2 = _fox_stack(do_i, first)
            delta = jnp.concatenate([jnp.sum(jnp.where(first, prod, 0.0), axis=-1, keepdims=True),
                                     jnp.sum(jnp.where(first, 0.0, prod), axis=-1, keepdims=True)], axis=0)
            lse2 = jnp.concatenate([lse_i[:, 0:1], lse_i[:, FOX_HEAD_DIM:FOX_HEAD_DIM + 1]], axis=0)
            dq2 = jnp.zeros((2 * tq, LANES), F32)
            dcq2 = jnp.zeros((2 * tq, 1), F32)
            for s, lo, hi in _fox_logits(q2, kb, cumt_ref, j, i, tq):
                p = jnp.exp(s - lse2)
                ds = p * (_mm_nt(do2, vb[lo:hi]) - delta)
                dsb = ds.astype(BF)
                dq2 = dq2 + jnp.dot(dsb, kb[lo:hi], preferred_element_type=F32)
                dk_s[lo:hi, :] += _mm_tn(dsb, q2)
                dv_s[lo:hi, :] += _mm_tn(p, do2)
                dcq2 = dcq2 + jnp.sum(ds, axis=-1, keepdims=True)
                dckt_ref[pl.ds(2 * j, 1), lo:hi] += jnp.sum(ds[:tq], axis=0, keepdims=True)
                dckt_ref[pl.ds(2 * j + 1, 1), lo:hi] += jnp.sum(ds[tq:], axis=0, keepdims=True)
            dq_ref[rows, :] = (_fox_unstack(dq2, first) * FOX_SCALE).astype(BF)
            dcq_ref[rows, :] += jnp.where(lane == 2 * j, dcq2[:tq], jnp.where(lane == 2 * j + 1, dcq2[tq:], 0.0))
        dk_ref[...] = dk_s[...].astype(BF)
        dv_ref[...] = dv_s[...].astype(BF)

    blk = lambda off: pl.BlockSpec((t, LANES), lambda j: (0, off + j))
    rows128 = pl.BlockSpec((LANES, t), lambda j: (0, 0))
    return pl.pallas_call(
        body, name="fox_bwd", grid=(N_FOX_HEADS // 2,),
        in_specs=[blk(0), blk(4), blk(8), rows128, blk(0), blk(0), blk(0)],
        out_specs=[blk(0), blk(0), blk(0), pl.BlockSpec((t, LANES), lambda j: (0, 0)), rows128],
        out_shape=[_sds((t, D_FOX), BF)] * 3 + [_sds((t, LANES)), _sds((LANES, t))],
        scratch_shapes=[pltpu.VMEM((t, LANES), F32), pltpu.VMEM((t, LANES), F32)],
        compiler_params=_params("arbitrary"),
    )(proj, proj, proj, cumt, lse, o, do)


def _conv(x, w, row):
    return (w[3:4, :] * x + w[2:3, :] * _shift_down(x, 1, row) + w[1:2, :] * _shift_down(x, 2, row)
            + w[0:1, :] * _shift_down(x, 3, row))


def _chunk_decay(gc_c):
    gi = gc_c[:, 0:CHUNK]
    gj = gc_c.T[0:CHUNK, :]
    ri = _iota((CHUNK, CHUNK), 0)
    cj = _iota((CHUNK, CHUNK), 1)
    return jnp.where(ri >= cj, jnp.exp(jnp.minimum(gi - gj, 0.0)), 0.0), ri > cj


def _gdn_specs(t):
    col = lambda off: pl.BlockSpec((t, LANES), lambda h: (0, off + h))
    cw = lambda off: pl.BlockSpec((CONV_K, LANES), lambda h: (0, off + h))
    mat = pl.BlockSpec((1, t // CHUNK, CHUNK, CHUNK), lambda h: (h, 0, 0, 0))
    return col, cw, mat


def _gdn_prep(proj, convw, beta, g):
    t = proj.shape[0]
    nch = t // CHUNK

    def body(xq_ref, xk_ref, xv_ref, wq_ref, wk_ref, wv_ref, beta_ref, g_ref,
             qn_ref, kn_ref, cv_ref, gc_ref, be_ref, m_ref, a_ref):
        row = _iota((t, LANES), 0)
        hd = pl.program_id(0)
        be_ref[...] = jnp.broadcast_to(_lane_column(beta_ref[...], SM_GB + hd), (t, LANES))

        def act(x_ref, w_ref):
            y = _conv(x_ref[...], w_ref[...], row)
            return y * _sigmoid(y)

        cq = act(xq_ref, wq_ref)
        ck = act(xk_ref, wk_ref)
        cv_ref[...] = act(xv_ref, wv_ref)
        qn_ref[...] = cq * lax.rsqrt(jnp.sum(cq * cq, axis=-1, keepdims=True) + EPS) * GDN_QSCALE
        kn_ref[...] = ck * lax.rsqrt(jnp.sum(ck * ck, axis=-1, keepdims=True) + EPS)
        gc = jnp.broadcast_to(_lane_column(g_ref[...], SM_GA + hd), (t, LANES))
        pos = row % CHUNK
        step = 1
        while step < CHUNK:
            gc = gc + jnp.where(pos >= step, pltpu.roll(gc, step, 0), 0.0)
            step *= 2
        gc_ref[...] = gc

        group = 4 if nch % 4 == 0 else 1

        def chunks(gi, carry):
            ns = [gi * group + c for c in range(group)]
            sls = [pl.ds(pl.multiple_of(n * CHUNK, CHUNK), CHUNK) for n in ns]
            ks = [kn_ref[sl, :] for sl in sls]
            kk = [_mm_nt(k_c * be_ref[sl, :], k_c) for k_c, sl in zip(ks, sls)]
            qk = [_mm_nt(qn_ref[sl, :], k_c) for k_c, sl in zip(ks, sls)]
            for c, n in enumerate(ns):
                decay, strict = _chunk_decay(gc_ref[sls[c], :])
                m_ref[0, n] = jnp.where(strict, kk[c] * decay, 0.0)
                a_ref[0, n] = qk[c] * decay
            return carry

        lax.fori_loop(0, nch // group, chunks, 0)

    col, cw, mat = _gdn_specs(t)
    return pl.pallas_call(
        body, name="gdn_prep", grid=(N_GDN_HEADS,),
        in_specs=[col(12), col(16), col(20), cw(0), cw(4), cw(8)] + [pl.BlockSpec((t, LANES), lambda h: (0, 0))] * 2,
        out_specs=[col(0), col(0), col(0), col(0), col(0), mat, mat],
        out_shape=[_sds((t, D_GDN))] * 5 + [_sds((N_GDN_HEADS, nch, CHUNK, CHUNK))] * 2,
        compiler_params=_params("parallel"),
    )(proj, proj, proj, convw, convw, convw, beta, g)


def _tri_inverse(m3):
    assert m3.shape == (LANES, CHUNK, CHUNK)

    def body(m_ref, t_ref, ms, ts):
        for i in range(CHUNK):
            ms[i * CHUNK:(i + 1) * CHUNK, :] = m_ref[:, i, :].T
        cidx = _iota((CHUNK, LANES), 0)

        def outer(i, carry):
            def inner(jj, acc):
                mrow = ms[pl.ds(i * CHUNK + jj, 1), :]
                return acc - mrow * ts[pl.ds(pl.multiple_of(jj * CHUNK, CHUNK), CHUNK), :]

            acc = lax.fori_loop(0, i, inner, jnp.where(cidx == i, 1.0, 0.0).astype(F32))
            ts[pl.ds(pl.multiple_of(i * CHUNK, CHUNK), CHUNK), :] = acc
            return carry

        lax.fori_loop(0, CHUNK, outer, 0)
        for i in range(CHUNK):
            t_ref[:, i, :] = ts[i * CHUNK:(i + 1) * CHUNK, :].T

    return pl.pallas_call(
        body, name="tri_inverse", in_specs=[VMEM_SPEC], out_specs=VMEM_SPEC,
        out_shape=_sds((LANES, CHUNK, CHUNK)),
        scratch_shapes=[pltpu.VMEM((CHUNK * CHUNK, LANES), F32), pltpu.VMEM((CHUNK * CHUNK, LANES), F32)],
        compiler_params=_params(),
    )(m3)


def _gdn_chunk_terms(q, k, v, b, gcc):
    eg = jnp.exp(gcc)
    last = gcc[CHUNK - 1:CHUNK, :]
    egl = jnp.exp(last - gcc)
    gl = jnp.exp(last)
    kb = k * b
    return eg, egl, gl, kb, v * b, kb * eg, q * eg, k * egl


GDN_BLOCK_CHUNKS = 4


def _gdn_block_specs(t, reverse):
    cb = GDN_BLOCK_CHUNKS
    nb = t // (cb * CHUNK)
    idx = (lambda i: nb - 1 - i) if reverse else (lambda i: i)
    tok = pl.BlockSpec((cb * CHUNK, D_GDN), lambda i: (idx(i), 0))
    mat = pl.BlockSpec((N_GDN_HEADS, cb, CHUNK, CHUNK), lambda i: (0, idx(i), 0, 0))
    state = pl.BlockSpec((N_GDN_HEADS, cb, GDN_HEAD_DIM, GDN_HEAD_DIM), lambda i: (0, idx(i), 0, 0))
    return nb, tok, mat, state


def _gdn_scan(qn, kn, cv, be, gc, tinv, amat):
    t = qn.shape[0]
    nch = t // CHUNK

    def body(q_ref, k_ref, v_ref, b_ref, gc_ref, t_ref, a_ref, o_ref, sall_ref, vn_ref, s_scr):
        @pl.when(pl.program_id(0) == 0)
        def _():
            s_scr[...] = jnp.zeros_like(s_scr)

        heads = range(N_GDN_HEADS)
        cols = [slice(hd * LANES, (hd + 1) * LANES) for hd in heads]
        s = [s_scr[hd] for hd in heads]
        for cc in range(GDN_BLOCK_CHUNKS):
            rs = slice(cc * CHUNK, (cc + 1) * CHUNK)
            terms = [_gdn_chunk_terms(q_ref[rs, cs], k_ref[rs, cs], v_ref[rs, cs], b_ref[rs, cs], gc_ref[rs, cs])
                     for cs in cols]
            for hd in heads:
                sall_ref[hd, cc] = s[hd]
            uw = [_mm(t_ref[hd, cc], jnp.concatenate([terms[hd][4], terms[hd][5]], axis=1)) for hd in heads]
            ws_qs = [_mm(jnp.concatenate([uw[hd][:, LANES:], terms[hd][6]], axis=0), s[hd]) for hd in heads]
            vn = [uw[hd][:, :LANES] - ws_qs[hd][:CHUNK] for hd in heads]
            a_vn = [_mm(a_ref[hd, cc], vn[hd]) for hd in heads]
            kd_vn = [_mm_tn(terms[hd][7], vn[hd]) for hd in heads]
            for hd in heads:
                vn_ref[rs, cols[hd]] = vn[hd]
                o_ref[rs, cols[hd]] = ws_qs[hd][CHUNK:] + a_vn[hd]
                s[hd] = s[hd] * terms[hd][2] + kd_vn[hd]
        for hd in heads:
            s_scr[hd] = s[hd]

    nb, tok, mat, state = _gdn_block_specs(t, False)
    return pl.pallas_call(
        body, name="gdn_scan", grid=(nb,),
        in_specs=[tok] * 5 + [mat, mat], out_specs=[tok, state, tok],
        out_shape=[_sds((t, D_GDN)), _sds((N_GDN_HEADS, nch, GDN_HEAD_DIM, GDN_HEAD_DIM)), _sds((t, D_GDN))],
        scratch_shapes=[pltpu.VMEM((N_GDN_HEADS, GDN_HEAD_DIM, GDN_HEAD_DIM), F32)],
        compiler_params=_params("arbitrary"),
    )(qn, kn, cv, be, gc, tinv, amat)


def _gdn_bwd(qn, kn, cv, be, gc, tinv, amat, s_all, vn_all, do):
    t = qn.shape[0]

    def body(q_ref, k_ref, v_ref, b_ref, gc_ref, t_ref, a_ref, sall_ref, vn_ref, do_ref,
             dq_ref, dk_ref, dv_ref, db_ref, dg_ref, ds_scr):
        @pl.when(pl.program_id(0) == 0)
        def _():
            ds_scr[...] = jnp.zeros_like(ds_scr)

        lastrow = _iota((CHUNK, LANES), 0) == CHUNK - 1
        heads = range(N_GDN_HEADS)
        cols = [slice(hd * LANES, (hd + 1) * LANES) for hd in heads]
        each = lambda fn: [fn(hd) for hd in heads]
        rows_cat = lambda x, y: jnp.concatenate([x, y], axis=0)
        lane_cat = lambda x, y: jnp.concatenate([x, y], axis=1)
        dsp = each(lambda hd: ds_scr[hd])
        for cc in reversed(range(GDN_BLOCK_CHUNKS)):
            rs = slice(cc * CHUNK, (cc + 1) * CHUNK)
            q = each(lambda hd: q_ref[rs, cols[hd]])
            k = each(lambda hd: k_ref[rs, cols[hd]])
            v = each(lambda hd: v_ref[rs, cols[hd]])
            b = each(lambda hd: b_ref[rs, cols[hd]])
            gcc = each(lambda hd: gc_ref[rs, cols[hd]])
            do_c = each(lambda hd: do_ref[rs, cols[hd]])
            vn = each(lambda hd: vn_ref[rs, cols[hd]])
            tn = each(lambda hd: t_ref[hd, cc])
            st = each(lambda hd: sall_ref[hd, cc])
            terms = each(lambda hd: _gdn_chunk_terms(q[hd], k[hd], v[hd], b[hd], gcc[hd]))
            eg, egl, gl, kb, vb, kbg, qd, kd = [[terms[hd][i] for hd in heads] for i in range(8)]
            w = each(lambda hd: _mm(tn[hd], kbg[hd]))
            a_do = each(lambda hd: _mm_tn(a_ref[hd, cc], do_c[hd]))
            kd_ds = each(lambda hd: _mm(kd[hd], dsp[hd]))
            da = each(lambda hd: _mm_nt(do_c[hd], vn[hd]))
            dkd = each(lambda hd: _mm_nt(vn[hd], dsp[hd]))
            by_k = each(lambda hd: _mm_nt(rows_cat(kb[hd], q[hd]), k[hd]))
            dgl = each(lambda hd: jnp.sum(jnp.sum(dsp[hd] * st[hd], axis=-1, keepdims=True), axis=0, keepdims=True))
            dvn = each(lambda hd: a_do[hd] + kd_ds[hd])
            do_dvn = each(lambda hd: rows_cat(do_c[hd], dvn[hd]))
            by_s = each(lambda hd: _mm_nt(do_dvn[hd], st[hd]))
            dqd = each(lambda hd: by_s[hd][:CHUNK])
            dvn_dw = each(lambda hd: lane_cat(dvn[hd], -by_s[hd][CHUNK:]))
            dsp = each(lambda hd: _mm_tn(rows_cat(qd[hd], -w[hd]), do_dvn[hd]) + gl[hd] * dsp[hd])
            dt = each(lambda hd: _mm_nt(dvn_dw[hd], lane_cat(vb[hd], kbg[hd])))
            by_t = each(lambda hd: _mm_tn(tn[hd], dvn_dw[hd]))
            tt_dt = each(lambda hd: _mm_tn(tn[hd], dt[hd]))
            dm_raw = each(lambda hd: _mm_nt(tt_dt[hd], tn[hd]))
            masks = each(lambda hd: _chunk_decay(gcc[hd]))
            dkk = each(lambda hd: jnp.where(masks[hd][1], -dm_raw[hd], 0.0) * masks[hd][0])
            dqk = each(lambda hd: da[hd] * masks[hd][0])
            dqk_dkk = each(lambda hd: rows_cat(dqk[hd], dkk[hd]))
            on_k = each(lambda hd: _mm(dqk_dkk[hd], k[hd]))
            dk_mm = each(lambda hd: _mm_tn(dqk_dkk[hd], rows_cat(q[hd], kb[hd])))
            for hd in heads:
                cs = cols[hd]
                dvb, dkbg = by_t[hd][:, :LANES], by_t[hd][:, LANES:]
                gmat = dkk[hd] * by_k[hd][:CHUNK] + dqk[hd] * by_k[hd][CHUNK:]
                dq_ref[rs, cs] = dqd[hd] * eg[hd] + on_k[hd][:CHUNK]
                dkb = on_k[hd][CHUNK:] + dkbg * eg[hd]
                dk_ref[rs, cs] = dkd[hd] * egl[hd] + dk_mm[hd] + dkb * b[hd]
                db = jnp.sum(dkb * k[hd], axis=-1, keepdims=True) + jnp.sum(dvb * v[hd], axis=-1, keepdims=True)
                db_ref[rs, cs] = jnp.broadcast_to(db, (CHUNK, LANES))
                dv_ref[rs, cs] = dvb * b[hd]
                dkd_kd = jnp.sum(dkd[hd] * kd[hd], axis=-1, keepdims=True)
                col_sums = jnp.sum(lane_cat(gmat, jnp.zeros_like(gmat)).T, axis=-1, keepdims=True)
                dgc = (jnp.sum(gmat, axis=-1, keepdims=True) - col_sums[:CHUNK]
                       + jnp.sum(dqd[hd] * qd[hd], axis=-1, keepdims=True)
                       + jnp.sum(dkbg * kbg[hd], axis=-1, keepdims=True) - dkd_kd)
                extra = jnp.sum(dkd_kd, axis=0, keepdims=True) + dgl[hd] * gl[hd]
                dg_ref[rs, cs] = dgc + jnp.where(lastrow, extra, 0.0)
        for hd in heads:
            ds_scr[hd] = dsp[hd]
        dg = dg_ref[...]
        row = _iota(dg.shape, 0)
        pos = row % CHUNK
        step = 1
        while step < CHUNK:
            dg = dg + jnp.where(pos < CHUNK - step, pltpu.roll(dg, dg.shape[0] - step, 0), 0.0)
            step *= 2
        dg_ref[...] = dg

    nb, tok, mat, state = _gdn_block_specs(t, True)
    return pl.pallas_call(
        body, name="gdn_bwd", grid=(nb,),
        in_specs=[tok] * 5 + [mat, mat, state, tok, tok], out_specs=[tok] * 5, out_shape=[_sds((t, D_GDN))] * 5,
        scratch_shapes=[pltpu.VMEM((N_GDN_HEADS, GDN_HEAD_DIM, GDN_HEAD_DIM), F32)],
        compiler_params=_params("arbitrary"),
    )(qn, kn, cv, be, gc, tinv, amat, s_all, vn_all, do)


def _gdn_bwd_conv(proj, convw, dqn, dkn, dcv):
    t = proj.shape[0]

    def body(xq_ref, xk_ref, xv_ref, wq_ref, wk_ref, wv_ref, dq_ref, dk_ref, dv_ref,
             dxq_ref, dxk_ref, dxv_ref, dwq_ref, dwk_ref, dwv_ref):
        row = _iota((t, LANES), 0)

        def one(x_ref, w_ref, d_ref, dx_ref, dw_ref, scale):
            x = x_ref[...]
            w = w_ref[...]
            y = _conv(x, w, row)
            sg = _sigmoid(y)
            dc = d_ref[...]
            if scale is not None:
                c = y * sg
                r = lax.rsqrt(jnp.sum(c * c, axis=-1, keepdims=True) + EPS)
                ch = c * r
                dc = scale * r * (dc - ch * jnp.sum(dc * ch, axis=-1, keepdims=True))
            dy = dc * sg * (1.0 + y * (1.0 - sg))
            dx_ref[...] = (w[3:4, :] * dy + w[2:3, :] * _shift_up(dy, 1, row) + w[1:2, :] * _shift_up(dy, 2, row)
                           + w[0:1, :] * _shift_up(dy, 3, row)).astype(BF)
            for jj in range(CONV_K):
                xs = x if jj == CONV_K - 1 else _shift_down(x, CONV_K - 1 - jj, row)
                dw_ref[jj:jj + 1, :] = jnp.sum(dy * xs, axis=0, keepdims=True)

        one(xq_ref, wq_ref, dq_ref, dxq_ref, dwq_ref, GDN_QSCALE)
        one(xk_ref, wk_ref, dk_ref, dxk_ref, dwk_ref, 1.0)
        one(xv_ref, wv_ref, dv_ref, dxv_ref, dwv_ref, None)

    col, cw, _ = _gdn_specs(t)
    return pl.pallas_call(
        body, name="gdn_bwd_conv", grid=(N_GDN_HEADS,),
        in_specs=[col(12), col(16), col(20), cw(0), cw(4), cw(8), col(0), col(0), col(0)],
        out_specs=[col(0), col(0), col(0), cw(0), cw(0), cw(0)],
        out_shape=[_sds((t, D_GDN), BF)] * 3 + [_sds((CONV_K, D_GDN))] * 3,
        compiler_params=_params("parallel"),
    )(proj, proj, proj, convw, convw, convw, dqn, dkn, dcv)


def _mix_out(fox_n, gdn_o, proj, gnw, w_out, x, pmw, plw, after):
    t = x.shape[0]
    tm = min(MATMUL_BLOCK, t)

    def body(fn_ref, go_ref, gz_ref, gnw_ref, w_ref, x_ref, pmw_ref, plw_ref, x1_ref, h2_ref, mixed_ref, omix_ref,
             h2t_ref):
        omix_ref[:, 0:D_FOX] = fn_ref[...]
        for hd in range(N_GDN_HEADS):
            cs = slice(hd * LANES, (hd + 1) * LANES)
            go = go_ref[:, cs]
            r = lax.rsqrt(jnp.mean(go * go, axis=-1, keepdims=True) + EPS)
            gz = gz_ref[:, cs]
            omix_ref[:, D_FOX + hd * LANES:D_FOX + (hd + 1) * LANES] = (
                go * r * gnw_ref[...] * (gz * _sigmoid(gz))).astype(BF)
        mixed = jnp.dot(omix_ref[...], w_ref[...], preferred_element_type=F32)
        mixed_ref[...] = mixed
        r2 = lax.rsqrt(jnp.mean(mixed * mixed, axis=-1, keepdims=True) + EPS)
        x1 = x_ref[...] + mixed * r2 * pmw_ref[...]
        x1_ref[...] = x1
        r3 = lax.rsqrt(jnp.mean(x1 * x1, axis=-1, keepdims=True) + EPS)
        h2 = x1 * r3 * plw_ref[...]
        h2_ref[...] = h2.astype(BF)
        h2t_ref[...] = h2.T.astype(BF)

    tok = lambda w: pl.BlockSpec((tm, w), lambda i: (i, 0))
    vec = lambda w: pl.BlockSpec((1, w), lambda i: (0, 0))
    return pl.pallas_call(
        _ordered(body), name="mix_out", grid=(t // tm,),
        in_specs=[ANY_SPEC,tok(D_FOX), tok(D_GDN), pl.BlockSpec((tm, D_GDN), lambda i: (i, COL_GZ // D_GDN)), vec(LANES),
                  pl.BlockSpec((D_MODEL, D_MODEL), lambda i: (0, 0)), tok(D_MODEL), vec(D_MODEL), vec(D_MODEL)],
        out_specs=[tok(D_MODEL)] * 4 + [pl.BlockSpec((D_MODEL, tm), lambda i: (0, i))],
        out_shape=[_sds((t, D_MODEL)), _sds((t, D_MODEL), BF), _sds((t, D_MODEL)), _sds((t, D_MODEL), BF),
                   _sds((D_MODEL, t), BF)],
        compiler_params=_params("parallel"),
    )(after, fox_n, gdn_o, proj, gnw, w_out, x, pmw, plw)


def _out_bwd(dmixed, w_out, o_fox, gdn_o, proj, fnw, gnw, after):
    t = dmixed.shape[0]
    tm = min(MATMUL_BLOCK, t)

    def body(dm_ref, w_ref, of_ref, go_ref, gz_ref, fnw_ref, gnw_ref, dof_ref, dgo_ref, dgz_ref, dfw_ref, dgw_ref):
        i = pl.program_id(0)

        @pl.when(i == 0)
        def _():
            dfw_ref[...] = jnp.zeros_like(dfw_ref)
            dgw_ref[...] = jnp.zeros_like(dgw_ref)

        domix = _mm_nt(dm_ref[...], w_ref[...])
        first = _iota((1, LANES), 1) < FOX_HEAD_DIM
        dfw = jnp.zeros((1, LANES), F32)
        dgw = jnp.zeros((1, LANES), F32)
        for pr in range(N_FOX_HEADS // 2):
            cs = slice(pr * LANES, (pr + 1) * LANES)
            o = of_ref[:, cs]
            dfn = domix[:, cs]
            o2 = o * o
            s0 = jnp.sum(jnp.where(first, o2, 0.0), axis=-1, keepdims=True)
            s1 = jnp.sum(jnp.where(first, 0.0, o2), axis=-1, keepdims=True)
            r = lax.rsqrt(jnp.where(first, s0, s1) * (1.0 / FOX_HEAD_DIM) + EPS)
            oh = o * r
            dfw = dfw + jnp.sum(dfn * oh, axis=0, keepdims=True)
            doh = dfn * fnw_ref[...]
            pr_ = doh * oh
            m0 = jnp.sum(jnp.where(first, pr_, 0.0), axis=-1, keepdims=True)
            m1 = jnp.sum(jnp.where(first, 0.0, pr_), axis=-1, keepdims=True)
            dof_ref[:, cs] = r * (doh - oh * jnp.where(first, m0, m1) * (1.0 / FOX_HEAD_DIM))
        for hd in range(N_GDN_HEADS):
            cs = slice(hd * LANES, (hd + 1) * LANES)
            go = go_ref[:, cs]
            gz = gz_ref[:, cs]
            dgated = domix[:, D_FOX + hd * LANES:D_FOX + (hd + 1) * LANES]
            r = lax.rsqrt(jnp.mean(go * go, axis=-1, keepdims=True) + EPS)
            goh = go * r
            sg = _sigmoid(gz)
            sz = gz * sg
            gn = goh * gnw_ref[...]
            dgn = dgated * sz
            dgz_ref[:, cs] = (dgated * gn * sg * (1.0 + gz * (1.0 - sg))).astype(BF)
            dgw = dgw + jnp.sum(dgn * goh, axis=0, keepdims=True)
            dgh = dgn * gnw_ref[...]
            dgo_ref[:, cs] = r * (dgh - goh * jnp.mean(dgh * goh, axis=-1, keepdims=True))
        dfw_ref[...] += dfw + pltpu.roll(dfw, FOX_HEAD_DIM, 1)
        dgw_ref[...] += dgw

    tok = lambda w: pl.BlockSpec((tm, w), lambda i: (i, 0))
    vec = lambda w: pl.BlockSpec((1, w), lambda i: (0, 0))
    return pl.pallas_call(
        _ordered(body), name="out_bwd", grid=(t // tm,),
        in_specs=[ANY_SPEC,tok(D_MODEL), pl.BlockSpec((D_MODEL, D_MODEL), lambda i: (0, 0)), tok(D_FOX), tok(D_GDN),
                  pl.BlockSpec((tm, D_GDN), lambda i: (i, COL_GZ // D_GDN)), vec(LANES), vec(LANES)],
        out_specs=[tok(D_FOX), tok(D_GDN), tok(D_GDN), vec(LANES), vec(LANES)],
        out_shape=[_sds((t, D_FOX)), _sds((t, D_GDN)), _sds((t, D_GDN), BF), _sds((1, LANES)), _sds((1, LANES))],
        compiler_params=_params("arbitrary"),
    )(after, dmixed, w_out, o_fox, gdn_o, proj, fnw, gnw)


def _mlp_up(h2, w_upt):
    t = h2.shape[0]
    tm = min(MATMUL_BLOCK, t)

    def body(h_ref, w_ref, up_ref):
        up_ref[...] = lax.dot_general(h_ref[...], w_ref[...], (((1,), (1,)), ((), ())),
                                      preferred_element_type=F32).astype(BF)

    return pl.pallas_call(
        body, name="mlp_up", grid=(t // tm,),
        in_specs=[pl.BlockSpec((tm, D_MODEL), lambda i: (i, 0)), pl.BlockSpec((D_FF, D_MODEL), lambda i: (0, 0))],
        out_specs=pl.BlockSpec((tm, D_FF), lambda i: (i, 0)), out_shape=_sds((t, D_FF), BF),
        compiler_params=_params("parallel"),
    )(h2, w_upt)


def _mlp_down_loss(up, w_down, x1, pw, target):
    t = up.shape[0]
    tm = min(MATMUL_BLOCK, t)

    def body(up_ref, w_ref, x1_ref, pw_ref, tg_ref, dy_ref, dx2_ref, loss_ref, dpw_ref):
        i = pl.program_id(0)

        @pl.when(i == 0)
        def _():
            loss_ref[...] = jnp.zeros_like(loss_ref)
            dpw_ref[...] = jnp.zeros_like(dpw_ref)

        u = jnp.maximum(up_ref[...].astype(F32), 0.0)
        y = jnp.dot((u * u).astype(BF), w_ref[...], preferred_element_type=F32)
        r = lax.rsqrt(jnp.mean(y * y, axis=-1, keepdims=True) + EPS)
        yh = y * r
        pw = pw_ref[...]
        err = x1_ref[...] + yh * pw - tg_ref[...]
        part = jnp.sum(jnp.sum(err * err, axis=-1, keepdims=True), axis=0, keepdims=True) * (0.5 / D_MODEL)
        loss_ref[...] += jnp.broadcast_to(part, loss_ref.shape)
        dx2 = err * (1.0 / D_MODEL)
        dx2_ref[...] = dx2
        dpw_ref[...] += jnp.sum(dx2 * yh, axis=0, keepdims=True)
        dyh = dx2 * pw
        dy_ref[...] = (r * (dyh - yh * jnp.mean(dyh * yh, axis=-1, keepdims=True))).astype(BF)

    tok = lambda w: pl.BlockSpec((tm, w), lambda i: (i, 0))
    vec = lambda w: pl.BlockSpec((1, w), lambda i: (0, 0))
    return pl.pallas_call(
        body, name="mlp_down_loss", grid=(t // tm,),
        in_specs=[tok(D_FF), pl.BlockSpec((D_FF, D_MODEL), lambda i: (0, 0)), tok(D_MODEL), vec(D_MODEL), tok(D_MODEL)],
        out_specs=[tok(D_MODEL), tok(D_MODEL), vec(LANES), vec(D_MODEL)],
        out_shape=[_sds((t, D_MODEL), BF), _sds((t, D_MODEL)), _sds((1, LANES)), _sds((1, D_MODEL))],
        compiler_params=_params("arbitrary"),
    )(up, w_down, x1, pw, target)


def _mlp_bwd_act(dy, w_down, up):
    t = dy.shape[0]
    tm = min(MATMUL_BLOCK, t)

    def body(dy_ref, w_ref, up_ref, dup_ref):
        da = lax.dot_general(dy_ref[...], w_ref[...], (((1,), (1,)), ((), ())), preferred_element_type=F32)
        dup_ref[...] = (da * (2.0 * jnp.maximum(up_ref[...].astype(F32), 0.0))).astype(BF)

    return pl.pallas_call(
        body, name="mlp_bwd_act", grid=(t // tm,),
        in_specs=[pl.BlockSpec((tm, D_MODEL), lambda i: (i, 0)), pl.BlockSpec((D_FF, D_MODEL), lambda i: (0, 0)),
                  pl.BlockSpec((tm, D_FF), lambda i: (i, 0))],
        out_specs=pl.BlockSpec((tm, D_FF), lambda i: (i, 0)), out_shape=_sds((t, D_FF), BF),
        compiler_params=_params("parallel"),
    )(dy, w_down, up)


def _mlp_bwd_in(dup, w_up, x1, plw, dx2, mixed, pmw, after):
    t = dup.shape[0]
    tm = min(MATMUL_BLOCK, t)

    def body(dup_ref, w_ref, x1_ref, plw_ref, dx2_ref, mx_ref, pmw_ref, dx1_ref, dmixed_ref, dplw_ref, dpmw_ref):
        i = pl.program_id(0)

        @pl.when(i == 0)
        def _():
            dplw_ref[...] = jnp.zeros_like(dplw_ref)
            dpmw_ref[...] = jnp.zeros_like(dpmw_ref)

        dh = jnp.dot(dup_ref[...], w_ref[...], preferred_element_type=F32)
        x1 = x1_ref[...]
        r = lax.rsqrt(jnp.mean(x1 * x1, axis=-1, keepdims=True) + EPS)
        xh = x1 * r
        dplw_ref[...] += jnp.sum(dh * xh, axis=0, keepdims=True)
        dxh = dh * plw_ref[...]
        dx1 = dx2_ref[...] + r * (dxh - xh * jnp.mean(dxh * xh, axis=-1, keepdims=True))
        dx1_ref[...] = dx1
        mx = mx_ref[...]
        r2 = lax.rsqrt(jnp.mean(mx * mx, axis=-1, keepdims=True) + EPS)
        mh = mx * r2
        dpmw_ref[...] += jnp.sum(dx1 * mh, axis=0, keepdims=True)
        dmh = dx1 * pmw_ref[...]
        dmixed_ref[...] = (r2 * (dmh - mh * jnp.mean(dmh * mh, axis=-1, keepdims=True))).astype(BF)

    tok = lambda w: pl.BlockSpec((tm, w), lambda i: (i, 0))
    vec = lambda w: pl.BlockSpec((1, w), lambda i: (0, 0))
    return pl.pallas_call(
        _ordered(body), name="mlp_bwd_in", grid=(t // tm,),
        in_specs=[ANY_SPEC,tok(D_FF), pl.BlockSpec((D_FF, D_MODEL), lambda i: (0, 0)), tok(D_MODEL),
                  vec(D_MODEL), tok(D_MODEL), tok(D_MODEL), vec(D_MODEL)],
        out_specs=[tok(D_MODEL), tok(D_MODEL), vec(D_MODEL), vec(D_MODEL)],
        out_shape=[_sds((t, D_MODEL)), _sds((t, D_MODEL), BF), _sds((1, D_MODEL)), _sds((1, D_MODEL))],
        compiler_params=_params("arbitrary"),
    )(after, dup, w_up, x1, plw, dx2, mixed, pmw)


def _wgrad(a, b, a_cols, split=1, a_fn=None, a_block0=0, name="wgrad"):
    t, b_cols = b.shape
    n_a = (a.shape[1] - a_block0 * a_cols) // a_cols if a_block0 else a.shape[1] // a_cols

    def body(a_ref, b_ref, o_ref):
        av = a_ref[...]
        if a_fn is not None:
            av = a_fn(av)
        o_ref[...] = _mm_tn(av, b_ref[...]).astype(BF).reshape(o_ref.shape)

    return pl.pallas_call(
        body, name=name, grid=(n_a,),
        in_specs=[pl.BlockSpec((t, a_cols), lambda i: (0, i + a_block0)), pl.BlockSpec((t, b_cols), lambda i: (0, 0))],
        out_specs=pl.BlockSpec((split, a_cols // split, b_cols), lambda i: (i, 0, 0)),
        out_shape=_sds((n_a * split, a_cols // split, b_cols), BF),
        compiler_params=_params("parallel"),
    )(a, b)


def _wgrad_pre_t(at, b, b_cols, name):
    rows, t = at.shape
    n_b = b.shape[1] // b_cols

    def body(a_ref, b_ref, o_ref):
        o_ref[0] = jnp.dot(a_ref[...], b_ref[...], preferred_element_type=F32).astype(BF)

    return pl.pallas_call(
        body, name=name, grid=(n_b,),
        in_specs=[pl.BlockSpec((rows, t), lambda j: (0, 0)), pl.BlockSpec((t, b_cols), lambda j: (0, j))],
        out_specs=pl.BlockSpec((1, rows, b_cols), lambda j: (j, 0, 0)), out_shape=_sds((n_b, rows, b_cols), BF),
        compiler_params=_params("parallel"),
    )(at, b)


def _small_bwd(proj, fb, al, dtb, dcq, dckt, dbe, dge):
    t = proj.shape[0]

    def body(sm_ref, fb_ref, al_ref, dtb_ref, dcq_ref, dckt_ref, dbe_ref, dge_ref, dsm_ref, dvec_ref):
        s = sm_ref[...]
        lane = _iota((1, LANES), 1)
        dcum = dcq_ref[...] - dckt_ref[...].T
        row = _iota((t, LANES), 0)
        step = 1
        while step < t:
            dcum = dcum + _shift_up(dcum, step, row)
            step *= 2
        dff = dcum * _sigmoid(-(s + fb_ref[...]))
        dbeta = jnp.zeros((t, LANES), F32)
        dg = jnp.zeros((t, LANES), F32)
        for hd in range(N_GDN_HEADS):
            dbeta = jnp.where(lane == SM_GB + hd, dbe_ref[:, hd * LANES:hd * LANES + 1], dbeta)
            dg = jnp.where(lane == SM_GA + hd, dge_ref[:, hd * LANES:hd * LANES + 1], dg)
        beta = _sigmoid(s)
        dgb = dbeta * beta * (1.0 - beta)
        za = s + dtb_ref[...]
        nea = -jnp.exp(al_ref[...])
        dga = dg * nea * _sigmoid(za)
        is_f = lane < SM_GB
        is_b = (lane >= SM_GB) & (lane < SM_GA)
        is_a = (lane >= SM_GA) & (lane < SM_GA + 4)
        dsm_ref[...] = jnp.where(is_f, dff, jnp.where(is_b, dgb, jnp.where(is_a, dga, 0.0))).astype(BF)
        dvec_ref[...] = jnp.zeros_like(dvec_ref)
        dvec_ref[0:1, :] = jnp.sum(jnp.where(is_f, dff, 0.0), axis=0, keepdims=True)
        dvec_ref[1:2, :] = jnp.sum(jnp.where(is_a, dg * nea * _softplus(za), 0.0), axis=0, keepdims=True)
        dvec_ref[2:3, :] = jnp.sum(jnp.where(is_a, dga, 0.0), axis=0, keepdims=True)

    vec = pl.BlockSpec((1, LANES), lambda i: (0, 0))
    full = lambda r, c: pl.BlockSpec((r, c), lambda i: (0, 0))
    return pl.pallas_call(
        body, name="small_bwd", grid=(1,),
        in_specs=[pl.BlockSpec((t, LANES), lambda i: (0, COL_SMALL // LANES)), vec, vec, vec, full(t, LANES),
                  full(LANES, t), full(t, 512), full(t, 512)],
        out_specs=[full(t, LANES), full(8, LANES)], out_shape=[_sds((t, LANES), BF), _sds((8, LANES))],
        compiler_params=_params("arbitrary"),
    )(proj, fb, al, dtb, dcq, dckt, dbe, dge)


def _pack_dproj(dfox, dgdn, dgz, dsm):
    t = dgz.shape[0]
    tm = min(MATMUL_BLOCK, t)

    def body(*refs):
        parts, dp_ref = refs[:8], refs[8]
        col = 0
        for part in parts:
            width = part.shape[1]
            dp_ref[:, col:col + width] = part[...].astype(BF)
            col += width

    tok = lambda w: pl.BlockSpec((tm, w), lambda i: (i, 0))
    return pl.pallas_call(
        body, name="pack_dproj", grid=(t // tm,), in_specs=[tok(D_FOX)] * 3 + [tok(D_GDN)] * 4 + [tok(LANES)],
        out_specs=tok(PROJ_W), out_shape=_sds((t, PROJ_W), BF), compiler_params=_params("parallel"),
    )(*dfox, *dgdn, dgz, dsm)


def _in_bwd(dproj, wt_al, x, nw, dx1, after):
    t = x.shape[0]
    tm = min(MATMUL_BLOCK, t)

    def body(dp_ref, w_ref, x_ref, nw_ref, dx1_ref, dx_ref, dnw_ref):
        i = pl.program_id(0)

        @pl.when(i == 0)
        def _():
            dnw_ref[...] = jnp.zeros_like(dnw_ref)

        dh = jnp.dot(dp_ref[...], w_ref[...], preferred_element_type=F32)
        xv = x_ref[...]
        r = lax.rsqrt(jnp.mean(xv * xv, axis=-1, keepdims=True) + EPS)
        xh = xv * r
        dnw_ref[...] += jnp.sum(dh * xh, axis=0, keepdims=True)
        dxh = dh * nw_ref[...]
        dx_ref[...] = dx1_ref[...] + r * (dxh - xh * jnp.mean(dxh * xh, axis=-1, keepdims=True))

    tok = lambda w: pl.BlockSpec((tm, w), lambda i: (i, 0))
    vec = lambda w: pl.BlockSpec((1, w), lambda i: (0, 0))
    return pl.pallas_call(
        _ordered(body), name="in_bwd", grid=(t // tm,),
        in_specs=[ANY_SPEC,tok(PROJ_W), pl.BlockSpec((PROJ_W, D_MODEL), lambda i: (0, 0)), tok(D_MODEL), vec(D_MODEL),
                  tok(D_MODEL)],
        out_specs=[tok(D_MODEL), vec(D_MODEL)], out_shape=[_sds((t, D_MODEL)), _sds((1, D_MODEL))],
        compiler_params=_params("arbitrary"),
    )(after, dproj, wt_al, x, nw, dx1)


def _row(v, width=None):
    v = v.reshape(1, -1).astype(F32)
    if width is not None and v.shape[1] < width:
        v = jnp.pad(v, ((0, 0), (0, width - v.shape[1])))
    return v


def _lane_vec(v, first):
    return jnp.pad(v.astype(F32), (first, LANES - first - v.shape[0])).reshape(1, LANES)


def _local_step(x, target, wt_al, started, late_weights, on_grads, convw, pre_mix_norm, fox_f_bias, fox_out_norm,
                gdn_a_log, gdn_dt_bias, gdn_out_norm, post_mix_norm, pre_mlp_norm, post_mlp_norm):
    t = x.shape[0]
    nch = t // CHUNK
    nw, pmw, plw, pw = _row(pre_mix_norm), _row(post_mix_norm), _row(pre_mlp_norm), _row(post_mlp_norm)
    fb, al, dtb = _lane_vec(fox_f_bias, SM_FF), _lane_vec(gdn_a_log, SM_GA), _lane_vec(gdn_dt_bias, SM_GA)
    fnw = _row(jnp.tile(fox_out_norm, 2))
    gnw = _row(gdn_out_norm)

    proj, h = _norm_proj(x, nw, wt_al, started)
    cumt, beta, g = _small_prep(proj, fb, al, dtb)
    o_fox, lse, fox_n = _fox_fwd(proj, cumt, fnw)
    qn, kn, cv, gc, be, mmat, amat = _gdn_prep(proj, convw, beta, g)
    n_prob = N_GDN_HEADS * nch
    m3 = mmat.reshape(n_prob, CHUNK, CHUNK)
    if n_prob < LANES:
        m3 = jnp.pad(m3, ((0, LANES - n_prob), (0, 0), (0, 0)))
    tinv = _tri_inverse(m3)[:n_prob].reshape(N_GDN_HEADS, nch, CHUNK, CHUNK)
    token = late_weights("mlp_relay", tinv)
    gdn_o, s_all, vn_all = _gdn_scan(qn, kn, cv, be, gc, tinv, amat)
    w_out = late_weights("w_out", gdn_o)
    x1, h2, mixed, omix, h2t = _mix_out(fox_n, gdn_o, proj, gnw, w_out, x, pmw, plw, token)
    w_up, w_down = late_weights("mlp", h2)
    up = _mlp_up(h2, w_up)
    dy, dx2, loss, d_pw = _mlp_down_loss(up, w_down, x1, pw, target)

    dup = _mlp_bwd_act(dy, w_down, up)
    relu2 = lambda u: jnp.square(jnp.maximum(u.astype(F32), 0.0))
    g_down = _wgrad(up, dy, D_FF // N_DEV, a_fn=relu2, name="wgrad_down")
    g_up = _wgrad_pre_t(h2t, dup, D_FF // N_DEV, name="wgrad_up")
    token = on_grads("mlp", (g_up, g_down))
    dx1, dmixed, d_plw, d_pmw = _mlp_bwd_in(dup, w_up, x1, plw, dx2, mixed, pmw, token)
    token = on_grads("w_out", _wgrad(omix, dmixed, 512, split=4, name="wgrad_out"))
    do_fox, dgo, dgz, d_fnw, d_gnw = _out_bwd(dmixed, w_out, o_fox, gdn_o, proj, fnw, gnw, token)
    dfq, dfk, dfv, dcq, dckt = _fox_bwd(proj, cumt, lse, o_fox, do_fox)
    dqn, dkn, dcv, dbe, dge = _gdn_bwd(qn, kn, cv, be, gc, tinv, amat, s_all, vn_all, dgo)
    dxq, dxk, dxv, dwq, dwk, dwv = _gdn_bwd_conv(proj, convw, dqn, dkn, dcv)
    dsm, dvec = _small_bwd(proj, fb, al, dtb, dcq, dckt, dbe, dge)
    dproj = _pack_dproj((dfq, dfk, dfv), (dxq, dxk, dxv), dgz, dsm)
    g_main = _wgrad(dproj, h, WGRAD_IN_ROWS, name="wgrad_in")
    g_tail = _wgrad(dproj, h, LANES, a_block0=COL_SMALL // LANES, name="wgrad_in_small")
    token = on_grads("w_in", (g_main, g_tail))
    grad_x, d_nw = _in_bwd(dproj, wt_al, x, nw, dx1, token)
    small = dict(norms=(d_nw, d_pmw, d_plw, d_pw), fox_out_norm=d_fnw, gdn_out_norm=d_gnw, loss=loss, vectors=dvec,
                 conv=(dwq, dwk, dwv))
    return grad_x, small


MESH_IDS = pl.DeviceIdType.MESH
CHIP_FLIPS = ((0, 0), (1, 0), (0, 1), (1, 1))


def _place():
    return lax.axis_index("x"), lax.axis_index("y"), lax.axis_index("c")


def _all_gather(blocks):
    n = len(blocks)

    def body(*refs):
        ins, outs, (send_sems, recv_sems, local_sems) = refs[:n], refs[n:2 * n], refs[2 * n:]
        x, y, c = _place()
        sibling = (x, y, 1 - c)
        chips = [(x ^ fx, y ^ fy) for fx, fy in CHIP_FLIPS[1:]]

        def slot(out, px, py, pc):
            return out.at[4 * px + 2 * py + pc]

        def copy(a, k, block, to, src=None):
            return pltpu.make_async_remote_copy(
                src_ref=slot(outs[a], *block) if src is None else src, dst_ref=slot(outs[a], *block),
                send_sem=send_sems.at[a, k], recv_sem=recv_sems.at[a, k], device_id=to, device_id_type=MESH_IDS)

        pending = []
        for a in range(n):
            mine = pltpu.make_async_copy(ins[a], slot(outs[a], x, y, c), local_sems.at[a])
            mine.start()
            pending.append(mine)
        sends = []
        for a in range(n):
            first = [copy(a, 0, (x, y, c), sibling, src=ins[a])]
            first += [copy(a, 1 + j, (x, y, c), (*chip, c), src=ins[a]) for j, chip in enumerate(chips)]
            for cp in first:
                cp.start()
            sends += first
        for a in range(n):
            for j, chip in enumerate(chips):
                copy(a, 1 + j, (*chip, c), (x, y, c)).wait_recv()
                fwd = copy(a, 4 + j, (*chip, c), sibling)
                fwd.start()
                sends.append(fwd)
        for a in range(n):
            copy(a, 0, sibling, (x, y, c)).wait_recv()
            for j, chip in enumerate(chips):
                copy(a, 4 + j, (*chip, 1 - c), (x, y, c)).wait_recv()
        for cp in sends:
            cp.wait_send()
        for cp in pending:
            cp.wait()

    return pl.pallas_call(
        body, name="all_gather_weights", in_specs=[ANY_SPEC] * n, out_specs=[ANY_SPEC] * n,
        out_shape=[_sds((N_DEV,) + b.shape, b.dtype) for b in blocks],
        scratch_shapes=[pltpu.SemaphoreType.DMA((n, 7)), pltpu.SemaphoreType.DMA((n, 7)), pltpu.SemaphoreType.DMA((n,))],
        compiler_params=pltpu.CompilerParams(has_side_effects=True),
    )(*blocks)


def _adamw(w, g, m, v):
    m = ADAM_B1 * m + (1.0 - ADAM_B1) * g
    v = ADAM_B2 * v + (1.0 - ADAM_B2) * (g * g)
    m_hat = m / (1.0 - ADAM_B1 ** ADAM_STEP)
    v_hat = v / (1.0 - ADAM_B2 ** ADAM_STEP)
    return -ADAM_LR * (m_hat / (jnp.sqrt(v_hat) + ADAM_EPS) + ADAM_WD * w), m, v


def _pair_reduce(g, name):
    _, r, c_ = g.shape
    n = len(CHIP_FLIPS)

    def body(g_ref, out_ref, sib_buf, send_sems, recv_sems):
        x, y, c = _place()
        chips = [(x ^ fx, y ^ fy) for fx, fy in CHIP_FLIPS]
        piece = lambda chip, core: g_ref.at[4 * chip[0] + 2 * chip[1] + core]
        copies = [pltpu.make_async_remote_copy(
            src_ref=piece(chip, 1 - c), dst_ref=sib_buf.at[j], send_sem=send_sems.at[j], recv_sem=recv_sems.at[j],
            device_id=(x, y, 1 - c), device_id_type=MESH_IDS) for j, chip in enumerate(chips)]
        for cp in copies:
            cp.start()
        for j, chip in enumerate(chips):
            copies[j].wait_recv()
            out_ref[j] = (piece(chip, c)[...].astype(F32) + sib_buf[j].astype(F32)).astype(BF)
        for cp in copies:
            cp.wait_send()

    return pl.pallas_call(
        body, name=name, in_specs=[VMEM_SPEC], out_specs=VMEM_SPEC, out_shape=_sds((n, r, c_), BF),
        scratch_shapes=[pltpu.VMEM((n, r, c_), BF), pltpu.SemaphoreType.DMA((n,)), pltpu.SemaphoreType.DMA((n,))],
        compiler_params=pltpu.CompilerParams(vmem_limit_bytes=VMEM_LIMIT, has_side_effects=True),
    )(g)


HBM_SPEC = pl.BlockSpec(memory_space=pltpu.HBM)
SEM_SPEC = pl.BlockSpec(memory_space=pltpu.SEMAPHORE)
DATAFLOW = pltpu.SideEffectType.DATAFLOW_SIDE_EFFECTING


def _peers():
    x, y, c = _place()
    return 4 * x + 2 * y + c, [(x ^ (k >> 2), y ^ ((k >> 1) & 1), c ^ (k & 1)) for k in range(1, N_DEV)]


def _peer_index(peer):
    return 4 * peer[0] + 2 * peer[1] + peer[2]


def _zones_with_own(srcs, pieces, name, after=None, dtype=None, chips=False):
    n = len(srcs)
    slots = len(CHIP_FLIPS) if chips else N_DEV
    extra = [] if after is None else [after]
    dtypes = [s_.dtype if pieces or dtype is None else dtype for s_ in srcs]

    def body(me_ref, *refs):
        outs = refs[n + len(extra):]
        for a in range(n):
            if pieces:
                outs[a][0] = refs[a][0]
            else:
                val = refs[a][...].astype(dtypes[a])
                outs[a][0] = val
                outs[n + a][...] = val

    shapes = [s_.shape[1:] if pieces else s_.shape for s_ in srcs]
    mine = lambda sh: pl.BlockSpec((1,) + sh, lambda i, me_ref: (me_ref[0], 0, 0))
    whole = lambda sh: pl.BlockSpec(sh, lambda i, me_ref: (0, 0))
    in_specs = [mine(sh) if pieces else whole(sh) for sh in shapes]
    out_specs = [mine(sh) for sh in shapes] + ([] if pieces else [whole(sh) for sh in shapes])
    out_shape = [_sds((slots,) + sh, dt) for sh, dt in zip(shapes, dtypes)]
    out_shape += [] if pieces else [_sds(sh, dt) for sh, dt in zip(shapes, dtypes)]
    x, y, c = _place()
    own = 0 * x if chips else 4 * x + 2 * y + c
    out = pl.pallas_call(
        body, name=name,
        grid_spec=pltpu.PrefetchScalarGridSpec(num_scalar_prefetch=1, grid=(1,), in_specs=in_specs + [ANY_SPEC] * len(extra),
                                               out_specs=out_specs),
        out_shape=out_shape, compiler_params=_params("arbitrary"),
    )(own.astype(jnp.int32).reshape(1), *srcs, *extra)
    return out[:n], (list(srcs) if pieces else out[n:])


def _exchange_start(srcs, zones, pieces, name, chips=False):
    n = len(srcs)

    def body(*refs):
        ins, zs = refs[:n], refs[n:2 * n]
        sems = refs[2 * n:4 * n]
        token = refs[-1]
        me, peers = _peers()
        x, y, c = _place()
        if chips and pieces:
            routes = [((x ^ fx, y ^ fy, c), j, j) for j, (fx, fy) in enumerate(CHIP_FLIPS) if j]
        elif chips:
            routes = [((x ^ fx, y ^ fy, c), None, me) for fx, fy in CHIP_FLIPS[1:]]
        else:
            routes = [(peer, _peer_index(peer) if pieces else None, me) for peer in peers]
        for peer, src_slot, dst_slot in routes:
            for a in range(n):
                pltpu.make_async_remote_copy(
                    src_ref=ins[a] if src_slot is None else ins[a].at[src_slot], dst_ref=zs[a].at[dst_slot],
                    send_sem=sems[2 * a], recv_sem=sems[2 * a + 1], device_id=peer, device_id_type=MESH_IDS).start()
        token[...] = jnp.zeros_like(token)

    hbm = lambda v: pltpu.with_memory_space_constraint(v, pltpu.HBM)
    out = pl.pallas_call(
        body, name=name,
        out_shape=tuple([pltpu.SemaphoreType.DMA(())] * (2 * n) + [pltpu.HBM(v.shape, v.dtype) for v in srcs]
                        + [pltpu.HBM(z.shape, z.dtype) for z in zones] + [_sds((8, LANES))]),
        in_specs=[HBM_SPEC] * (2 * n), out_specs=tuple([SEM_SPEC] * (2 * n) + [HBM_SPEC] * (2 * n) + [VMEM_SPEC]),
        input_output_aliases={i: 2 * n + i for i in range(2 * n)},
        compiler_params=pltpu.CompilerParams(has_side_effects=DATAFLOW),
    )(*[hbm(v) for v in srcs], *[hbm(z) for z in zones])
    return out[:2 * n], out[2 * n:3 * n], out[3 * n:4 * n], out[-1]


def _relay_start(zones, name):
    n = len(zones)

    def body(*refs):
        zs, sems, token = refs[:n], refs[n:3 * n], refs[-1]
        x, y, c = _place()
        for fx, fy in CHIP_FLIPS:
            slot = 4 * (x ^ fx) + 2 * (y ^ fy) + c
            for a in range(n):
                pltpu.make_async_remote_copy(
                    src_ref=zs[a].at[slot], dst_ref=zs[a].at[slot], send_sem=sems[2 * a], recv_sem=sems[2 * a + 1],
                    device_id=(x, y, 1 - c), device_id_type=MESH_IDS).start()
        token[...] = jnp.zeros_like(token)

    out = pl.pallas_call(
        body, name=name,
        out_shape=tuple([pltpu.SemaphoreType.DMA(())] * (2 * n) + [pltpu.HBM(z.shape, z.dtype) for z in zones]
                        + [_sds((8, LANES))]),
        in_specs=[HBM_SPEC] * n, out_specs=tuple([SEM_SPEC] * (2 * n) + [HBM_SPEC] * n + [VMEM_SPEC]),
        input_output_aliases={i: 2 * n + i for i in range(n)},
        compiler_params=pltpu.CompilerParams(has_side_effects=DATAFLOW),
    )(*[pltpu.with_memory_space_constraint(z, pltpu.HBM) for z in zones])
    return out[:2 * n], [], out[2 * n:3 * n], out[-1]


def _exchange_wait(sems, srcs, zones, after, name, chips=False, n_copies=None):
    n, n_src = len(zones), len(srcs)
    after = list(after) if isinstance(after, (list, tuple)) else [after]
    n_copies = n_copies or (len(CHIP_FLIPS) - 1 if chips else N_DEV - 1)

    def body(*refs):
        zs, sm = refs[n_src:n_src + n], refs[n_src + n:n_src + 3 * n]
        me, peers = _peers()
        for a in range(n):
            seven = zs[a].at[pl.ds(0, n_copies)]
            cp = pltpu.make_async_remote_copy(src_ref=seven, dst_ref=seven, send_sem=sm[2 * a], recv_sem=sm[2 * a + 1],
                                              device_id=peers[0], device_id_type=MESH_IDS)
            cp.wait_send()
            cp.wait_recv()

    out = pl.pallas_call(
        body, name=name, out_shape=tuple([pltpu.HBM(v.shape, v.dtype) for v in srcs] + [pltpu.HBM(z.shape, z.dtype) for z in zones]),
        in_specs=[HBM_SPEC] * (n_src + n) + [SEM_SPEC] * (2 * n) + [ANY_SPEC] * len(after),
        out_specs=tuple([HBM_SPEC] * (n_src + n)), input_output_aliases={i: i for i in range(n_src + n)},
        compiler_params=pltpu.CompilerParams(has_side_effects=DATAFLOW),
    )(*srcs, *zones, *sems, *after)
    return out[n_src:]


def _sum_adamw(zone, w, m, v, name):
    n_slots, r, c_ = zone.shape
    rb = next((b for b in (256, 128) if r % b == 0), r)

    def body(z_ref, w_ref, m_ref, v_ref, grad_ref, delta_ref, nm_ref, nv_ref):
        total = z_ref[0].astype(F32)
        for d in range(1, n_slots):
            total = total + z_ref[d].astype(F32)
        grad_ref[...] = total
        delta_ref[...], nm_ref[...], nv_ref[...] = _adamw(w_ref[...], total, m_ref[...], v_ref[...])

    blk = pl.BlockSpec((rb, c_), lambda i: (i, 0))
    return pl.pallas_call(
        body, name=name, grid=(r // rb,), in_specs=[pl.BlockSpec((n_slots, rb, c_), lambda i: (0, i, 0)), blk, blk, blk],
        out_specs=[blk] * 4, out_shape=[_sds((r, c_))] * 4, compiler_params=_params("parallel"),
    )(zone, w, m, v)


SMALL_NORMS = ("pre_mix_norm", "post_mix_norm", "pre_mlp_norm", "post_mlp_norm")
SMALL_ORDER = SMALL_NORMS + ("fox_out_norm", "gdn_out_norm", "fox_f_bias", "gdn_a_log", "gdn_dt_bias", "gdn_conv_w")
CONV_SLAB_ROWS, CONV_SLAB_LANES = 8, 256


def _small_pack(small):
    def body(n0, n1, n2, n3, fnw_ref, gnw_ref, loss_ref, vec_ref, out_ref):
        out_ref[...] = jnp.zeros_like(out_ref)
        for i, ref in enumerate((n0, n1, n2, n3)):
            out_ref[i:i + 1, :] = ref[...]
        out_ref[4:5, 0:LANES] = fnw_ref[...]
        out_ref[4:5, LANES:2 * LANES] = gnw_ref[...]
        out_ref[4:5, 2 * LANES:3 * LANES] = loss_ref[...]
        out_ref[5:8, 0:LANES] = vec_ref[0:3, :]

    return pl.pallas_call(body, name="small_pack", in_specs=[VMEM_SPEC] * 8, out_specs=VMEM_SPEC,
                          out_shape=_sds((8, D_MODEL)))(*small["norms"], small["fox_out_norm"], small["gdn_out_norm"],
                                                        small["loss"], small["vectors"])


def _conv_slabs(dconv):
    blocks = dconv.reshape(CONV_K, N_DEV, -1).transpose(1, 0, 2)
    blocks = jnp.pad(blocks, ((0, 0), (0, CONV_SLAB_ROWS - CONV_K), (0, CONV_SLAB_LANES - blocks.shape[2])))
    return blocks.reshape(N_DEV * CONV_SLAB_ROWS, CONV_SLAB_LANES)


def _small_update(zone, conv_zone, w, m, v):
    n = len(SMALL_ORDER)
    n_conv = w["gdn_conv_w"].shape[1]

    def body(me_ref, z_ref, zc_ref, *refs):
        params, loss_ref, outs, (tot, totc) = refs[:3 * n], refs[3 * n], refs[3 * n + 1:7 * n + 1], refs[-2:]
        total, total_c = z_ref[0], zc_ref[0]
        for d in range(1, N_DEV):
            total, total_c = total + z_ref[d], total_c + zc_ref[d]
        tot[...] = total
        totc[...] = total_c
        loss_ref[...] = tot[4, 2 * LANES:2 * LANES + 1]
        mine = totc[pl.ds(pl.multiple_of(me_ref[0] * CONV_SLAB_ROWS, CONV_SLAB_ROWS), CONV_SLAB_ROWS), :]
        g = dict(zip(SMALL_NORMS, (tot[0], tot[1], tot[2], tot[3])))
        g.update(fox_out_norm=tot[4, 0:FOX_HEAD_DIM], gdn_out_norm=tot[4, LANES:LANES + GDN_HEAD_DIM],
                 fox_f_bias=tot[5, SM_FF:SM_FF + N_FOX_HEADS], gdn_a_log=tot[6, SM_GA:SM_GA + N_GDN_HEADS],
                 gdn_dt_bias=tot[7, SM_GA:SM_GA + N_GDN_HEADS], gdn_conv_w=mine[0:CONV_K, 0:n_conv])
        for i, name in enumerate(SMALL_ORDER):
            w_ref, m_ref, v_ref = params[3 * i:3 * i + 3]
            outs[4 * i][...] = g[name]
            outs[4 * i + 1][...], outs[4 * i + 2][...], outs[4 * i + 3][...] = _adamw(w_ref[...], g[name], m_ref[...],
                                                                                     v_ref[...])

    x, y, c = _place()
    operands = [a[name] for name in SMALL_ORDER for a in (w, m, v)]
    out = pl.pallas_call(
        body, name="small_update",
        in_specs=[pl.BlockSpec(memory_space=pltpu.SMEM)] + [VMEM_SPEC] * (2 + 3 * n), out_specs=[VMEM_SPEC] * (1 + 4 * n),
        out_shape=[_sds((1,))] + [_sds(w[name].shape) for name in SMALL_ORDER for _ in range(4)],
        scratch_shapes=[pltpu.VMEM(zone.shape[1:], F32), pltpu.VMEM(conv_zone.shape[1:], F32)],
    )((4 * x + 2 * y + c).astype(jnp.int32).reshape(1), zone, conv_zone, *operands)
    return out[0][0], {name: out[1 + 4 * i:5 + 4 * i] for i, name in enumerate(SMALL_ORDER)}


NATIVE_ROWS = ((0, 1536), (1544, 3080), (3088, 3600), (1536, 1544), (3080, 3088))


W_IN_PIECE = D_PROJ // N_DEV
WGRAD_IN_ROWS = 512
SHUFFLE_LANES = 256


def _to_aligned_moves():
    moves, o = [], 0
    for lo, hi in NATIVE_ROWS:
        r = lo
        while r < hi:
            d = r // W_IN_PIECE
            k = min(hi, (d + 1) * W_IN_PIECE) - r
            moves.append((0, d, r - d * W_IN_PIECE, 0, o, k))
            r, o = r + k, o + k
    return moves


def _from_aligned_moves():
    moves = []
    for _, d, a, _, o, k in _to_aligned_moves():
        while k:
            n = min(k, WGRAD_IN_ROWS - o % WGRAD_IN_ROWS) if o < COL_SMALL else k
            moves.append((0, o // WGRAD_IN_ROWS, o % WGRAD_IN_ROWS, d, a, n) if o < COL_SMALL else
                         (1, 0, o - COL_SMALL, d, a, n))
            o, a, k = o + n, a + n, k - n
    return moves


def _shuffle_rows(srcs, moves, out_shape, name):
    c = srcs[0].shape[-1]

    def body(*refs):
        s_refs, o_ref, s_f, o_f = refs[:len(srcs)], refs[len(srcs)], refs[len(srcs) + 1:-1], refs[-1]
        for s_ref, f in zip(s_refs, s_f):
            f[...] = s_ref[...].astype(F32)
        o_f[...] = jnp.zeros_like(o_f)
        for i, ss, so, ds, do, k in moves:
            o_f[ds, pl.ds(do, k), :] = s_f[i][ss, pl.ds(so, k), :]
        o_ref[...] = o_f[...].astype(BF)

    blk = lambda shape: pl.BlockSpec(tuple(shape[:-1]) + (SHUFFLE_LANES,), lambda j: (0, 0, j))
    scratch = lambda shape: pltpu.VMEM(tuple(shape[:-1]) + (SHUFFLE_LANES,), F32)
    return pl.pallas_call(
        body, name=name, grid=(c // SHUFFLE_LANES,), in_specs=[blk(s.shape) for s in srcs], out_specs=blk(out_shape),
        out_shape=_sds(out_shape, BF), scratch_shapes=[scratch(s.shape) for s in srcs] + [scratch(out_shape)],
        compiler_params=_params("parallel"),
    )(*srcs)


def _cols_from_pieces(p):
    return p.transpose(1, 0, 2).reshape(p.shape[1], -1)


WEIGHT_ORDER = ("pre_mix_norm", "w_in", "fox_f_bias", "fox_out_norm", "gdn_conv_w", "gdn_a_log", "gdn_dt_bias",
                "gdn_out_norm", "w_out", "post_mix_norm", "pre_mlp_norm", "w_up", "w_down", "post_mlp_norm")


def kernel(x, pre_mix_norm, w_in, fox_f_bias, fox_out_norm, gdn_conv_w, gdn_a_log, gdn_dt_bias, gdn_out_norm, w_out, post_mix_norm, pre_mlp_norm, w_up, w_down, post_mlp_norm, loss_target, m_pre_mix_norm, m_w_in, m_fox_f_bias, m_fox_out_norm, m_gdn_conv_w, m_gdn_a_log, m_gdn_dt_bias, m_gdn_out_norm, m_w_out, m_post_mix_norm, m_pre_mlp_norm, m_w_up, m_w_down, m_post_mlp_norm, v_pre_mix_norm, v_w_in, v_fox_f_bias, v_fox_out_norm, v_gdn_conv_w, v_gdn_a_log, v_gdn_dt_bias, v_gdn_out_norm, v_w_out, v_post_mix_norm, v_pre_mlp_norm, v_w_up, v_w_down, v_post_mlp_norm):
    w = dict(pre_mix_norm=pre_mix_norm, w_in=w_in, fox_f_bias=fox_f_bias, fox_out_norm=fox_out_norm,
             gdn_conv_w=gdn_conv_w, gdn_a_log=gdn_a_log, gdn_dt_bias=gdn_dt_bias, gdn_out_norm=gdn_out_norm, w_out=w_out,
             post_mix_norm=post_mix_norm, pre_mlp_norm=pre_mlp_norm, w_up=w_up, w_down=w_down, post_mlp_norm=post_mlp_norm)
    mom = dict(pre_mix_norm=m_pre_mix_norm, w_in=m_w_in, fox_f_bias=m_fox_f_bias, fox_out_norm=m_fox_out_norm,
               gdn_conv_w=m_gdn_conv_w, gdn_a_log=m_gdn_a_log, gdn_dt_bias=m_gdn_dt_bias, gdn_out_norm=m_gdn_out_norm,
               w_out=m_w_out, post_mix_norm=m_post_mix_norm, pre_mlp_norm=m_pre_mlp_norm, w_up=m_w_up, w_down=m_w_down,
               post_mlp_norm=m_post_mlp_norm)
    var = dict(pre_mix_norm=v_pre_mix_norm, w_in=v_w_in, fox_f_bias=v_fox_f_bias, fox_out_norm=v_fox_out_norm,
               gdn_conv_w=v_gdn_conv_w, gdn_a_log=v_gdn_a_log, gdn_dt_bias=v_gdn_dt_bias, gdn_out_norm=v_gdn_out_norm,
               w_out=v_w_out, post_mix_norm=v_post_mix_norm, pre_mlp_norm=v_pre_mlp_norm, w_up=v_w_up, w_down=v_w_down,
               post_mlp_norm=v_post_mlp_norm)

    win_g, conv_g = _all_gather([w_in.T.astype(BF), gdn_conv_w])
    wt_al = _shuffle_rows([win_g], _to_aligned_moves(), (1, PROJ_W, D_MODEL), "w_in_to_aligned")[0]
    convw = _cols_from_pieces(conv_g)
    gathers, after = {}, win_g
    for name, shards in (("w_out", [w_out]), ("mlp", [w_up.T, w_down])):
        zones, shards = _zones_with_own(shards, False, "gather_" + name + "_own", after=after, dtype=BF)
        gathers[name] = _exchange_start(shards, zones, False, "gather_" + name + "_start", chips=name == "mlp")
        after = gathers[name][3]

    def late_weights(name, after):
        if name == "mlp_relay":
            sems, shards, zones, _ = gathers["mlp"]
            zones = _exchange_wait(sems, shards, zones, after, "gather_mlp_wait", chips=True)
            gathers["mlp"] = _relay_start(zones, "gather_mlp_relay")
            return gathers["mlp"][3]
        sems, shards, zones, _ = gathers[name]
        got = _exchange_wait(sems, shards, zones, after, "gather_" + name + "_done",
                             n_copies=len(CHIP_FLIPS) if name == "mlp" else None)
        if name == "w_out":
            return got[0].reshape(D_MODEL, D_MODEL)
        return got[0].reshape(D_FF, D_MODEL), got[1].reshape(D_FF, D_MODEL)

    scatters = {}

    def on_grads(name, g):
        chips = name == "w_in"
        if name == "w_in":
            g = _shuffle_rows(list(g), _from_aligned_moves(), (N_DEV, W_IN_PIECE, D_MODEL), "w_in_grad_from_aligned")
            g = _pair_reduce(g, "pair_reduce_w_in")
        srcs = list(g) if name == "mlp" else [g]
        zones, _ = _zones_with_own(srcs, True, "scatter_" + name + "_own", chips=chips)
        scatters[name] = _exchange_start(srcs, zones, True, "scatter_" + name + "_start", chips=chips)
        return scatters[name][3]

    grad_x, small = _local_step(
        x[0], loss_target[0], wt_al, after, late_weights, on_grads, convw, pre_mix_norm,
        fox_f_bias, fox_out_norm, gdn_a_log, gdn_dt_bias, gdn_out_norm, post_mix_norm, pre_mlp_norm, post_mlp_norm)
    slabs = [_small_pack(small), _conv_slabs(jnp.concatenate(small["conv"], axis=1))]
    zones, slabs = _zones_with_own(slabs, False, "small_own")
    scatters["small"] = _exchange_start(slabs, zones, False, "small_start")

    grads, delta, new_m, new_v = {}, {}, {}, {}
    after = scatters["small"][3]
    for name, members in (("mlp", ("w_up", "w_down")), ("w_out", ("w_out",)), ("small", ()), ("w_in", ("w_in",))):
        sems, srcs, zones, _ = scatters[name]
        zones = _exchange_wait(sems, srcs, zones, after, "scatter_" + name + "_wait", chips=name == "w_in")
        if name == "small":
            loss, updated = _small_update(zones[0], zones[1], w, mom, var)
            for n, res in updated.items():
                grads[n], delta[n], new_m[n], new_v[n] = res
            after = grads["pre_mix_norm"]
        for n, zone in zip(members, zones):
            if n == "w_in":
                res = _sum_adamw(zone, w[n].T, mom[n].T, var[n].T, "adamw_" + n)
                grads[n], delta[n], new_m[n], new_v[n] = [r.T for r in res]
            else:
                grads[n], delta[n], new_m[n], new_v[n] = _sum_adamw(zone, w[n], mom[n], var[n], "adamw_" + n)
        if members:
            after = [grads[n] for n in members]

    return (loss, grad_x[None], *[grads[n] for n in WEIGHT_ORDER], *[delta[n] for n in WEIGHT_ORDER],
            *[new_m[n] for n in WEIGHT_ORDER], *[new_v[n] for n in WEIGHT_ORDER])
```

```python
import jax
import jax.numpy as jnp
from jax import lax
from jax.experimental import pallas as pl
from jax.experimental.pallas import tpu as pltpu

F32 = jnp.float32
BF = jnp.bfloat16

D_MODEL = 1024
N_FOX_HEADS, FOX_HEAD_DIM = 8, 64
N_GDN_HEADS, GDN_HEAD_DIM = 4, 128
D_FOX = N_FOX_HEADS * FOX_HEAD_DIM
D_GDN = N_GDN_HEADS * GDN_HEAD_DIM
CHUNK = 64
CONV_K = 4
D_FF = 4 * D_MODEL
EPS = 1e-6
D_PROJ = 3600
N_DEV = 8

PROJ_W = 3712
COL_FOX, COL_GDN, COL_GZ, COL_SMALL = 0, 1536, 3072, 3584
LANES = 128
QKV = 3 * LANES
SM_FF, SM_GB, SM_GA = 0, 8, 12

ADAM_LR, ADAM_B1, ADAM_B2, ADAM_EPS, ADAM_WD, ADAM_STEP = 0.001, 0.9, 0.999, 1e-08, 0.01, 10

TOKEN_BLOCK = 256
MATMUL_BLOCK = 512
FOX_SCALE = FOX_HEAD_DIM ** -0.5
GDN_QSCALE = GDN_HEAD_DIM ** -0.5
NEG_BIG = -1e30
VMEM_LIMIT = 56 * 1024 * 1024

VMEM_SPEC = pl.BlockSpec(memory_space=pltpu.VMEM)
ANY_SPEC = pl.BlockSpec(memory_space=pl.ANY)


def _sds(shape, dtype=F32):
    return jax.ShapeDtypeStruct(shape, dtype)


def _params(*sem):
    return pltpu.CompilerParams(dimension_semantics=sem if sem else None, vmem_limit_bytes=VMEM_LIMIT)


def _ordered(body):
    def ordered(_, *refs):
        body(*refs)

    return ordered


def _mm(a, b):
    return jnp.dot(a.astype(BF), b.astype(BF), preferred_element_type=F32)


def _mm_nt(a, b):
    return lax.dot_general(a.astype(BF), b.astype(BF), (((1,), (1,)), ((), ())), preferred_element_type=F32)


def _mm_tn(a, b):
    return lax.dot_general(a.astype(BF), b.astype(BF), (((0,), (0,)), ((), ())), preferred_element_type=F32)


def _sigmoid(x):
    return 1.0 / (1.0 + jnp.exp(-x))


def _softplus(x):
    return jnp.maximum(x, 0.0) + jnp.log1p(jnp.exp(-jnp.abs(x)))


def _iota(shape, dim):
    return lax.broadcasted_iota(jnp.int32, shape, dim)


def _shift_down(x, s, row):
    return jnp.where(row >= s, pltpu.roll(x, s, 0), 0.0)


def _shift_up(x, s, row):
    n = x.shape[0]
    return jnp.where(row < n - s, pltpu.roll(x, n - s, 0), 0.0)


def _norm_proj(x, nw, wt_al, after):
    t = x.shape[0]

    def body(x_ref, nw_ref, w_ref, proj_ref, h_ref):
        xv = x_ref[...]
        r = lax.rsqrt(jnp.mean(xv * xv, axis=-1, keepdims=True) + EPS)
        h = (xv * r * nw_ref[...]).astype(BF)
        h_ref[...] = h
        proj_ref[...] = lax.dot_general(h, w_ref[...], (((1,), (1,)), ((), ())), preferred_element_type=F32)

    tm = min(MATMUL_BLOCK, t)
    return pl.pallas_call(
        _ordered(body), name="norm_proj", grid=(t // tm,),
        in_specs=[ANY_SPEC, pl.BlockSpec((tm, D_MODEL), lambda i: (i, 0)), pl.BlockSpec((1, D_MODEL), lambda i: (0, 0)),
                  pl.BlockSpec((PROJ_W, D_MODEL), lambda i: (0, 0))],
        out_specs=[pl.BlockSpec((tm, PROJ_W), lambda i: (i, 0)), pl.BlockSpec((tm, D_MODEL), lambda i: (i, 0))],
        out_shape=[_sds((t, PROJ_W)), _sds((t, D_MODEL), BF)],
        compiler_params=_params("parallel"),
    )(after, x, nw, wt_al)


def _lane_column(x, lane):
    return jnp.sum(jnp.where(_iota((1, LANES), 1) == lane, x, 0.0), axis=-1, keepdims=True)


def _small_prep(proj, fb, al, dtb):
    t = proj.shape[0]

    def body(sm_ref, fb_ref, al_ref, dtb_ref, cumt_ref, beta_ref, g_ref):
        s = sm_ref[...]
        z = s + fb_ref[...]
        cum = jnp.minimum(z, 0.0) - jnp.log1p(jnp.exp(-jnp.abs(z)))
        row = _iota((t, LANES), 0)
        step = 1
        while step < t:
            cum = cum + _shift_down(cum, step, row)
            step *= 2
        cumt_ref[...] = cum.T
        beta_ref[...] = _sigmoid(s)
        g_ref[...] = -jnp.exp(al_ref[...]) * _softplus(s + dtb_ref[...])

    vec = pl.BlockSpec((1, LANES), lambda i: (0, 0))
    tok = pl.BlockSpec((t, LANES), lambda i: (0, 0))
    return pl.pallas_call(
        body, name="small_prep", grid=(1,),
        in_specs=[pl.BlockSpec((t, LANES), lambda i: (0, COL_SMALL // LANES)), vec, vec, vec],
        out_specs=[pl.BlockSpec((LANES, t), lambda i: (0, 0)), tok, tok],
        out_shape=[_sds((LANES, t)), _sds((t, LANES)), _sds((t, LANES))],
        compiler_params=_params("arbitrary"),
    )(proj, fb, al, dtb)


def _fox_stack(x, first):
    return jnp.concatenate([jnp.where(first, x, 0.0), jnp.where(first, 0.0, x)], axis=0).astype(BF)


def _fox_unstack(y, first):
    n = y.shape[0] // 2
    return jnp.where(first, y[:n], y[n:])


def _fox_logits(q2_i, kb, cumt_ref, pair, i, tq):
    klen = (i + 1) * tq
    s = lax.dot_general(q2_i, kb[:klen], (((1,), (1,)), ((), ())), preferred_element_type=F32)
    upper = _iota((2 * tq, 1), 0) < tq
    s = s - jnp.where(upper, cumt_ref[pl.ds(2 * pair, 1), 0:klen], cumt_ref[pl.ds(2 * pair + 1, 1), 0:klen])
    causal = _iota((2 * tq, tq), 1) <= _iota((2 * tq, tq), 0) % tq
    parts = [(s[:, :klen - tq], 0, klen - tq)] if i else []
    return parts + [(jnp.where(causal, s[:, klen - tq:], NEG_BIG), klen - tq, klen)]


def _fox_fwd(proj, cumt, fnw):
    t = proj.shape[0]
    tq = min(TOKEN_BLOCK, t // 2)
    nq = t // tq

    def body(q_ref, k_ref, v_ref, cumt_ref, fnw_ref, o_ref, lse_ref, fn_ref):
        j = pl.program_id(0)
        first = _iota((1, LANES), 1) < FOX_HEAD_DIM
        kb = k_ref[...].astype(BF)
        vb = v_ref[...].astype(BF)
        for i in range(nq):
            rows = slice(i * tq, (i + 1) * tq)
            q2 = _fox_stack(q_ref[rows, :] * FOX_SCALE, first)
            parts = _fox_logits(q2, kb, cumt_ref, j, i, tq)
            m = jnp.max(parts[-1][0], axis=-1, keepdims=True)
            if i:
                m = jnp.maximum(m, jnp.max(parts[0][0], axis=-1, keepdims=True))
            l = jnp.zeros((2 * tq, 1), F32)
            o = jnp.zeros((2 * tq, LANES), F32)
            for s, lo, hi in parts:
                p = jnp.exp(s - m)
                l = l + jnp.sum(p, axis=-1, keepdims=True)
                o = o + jnp.dot(p.astype(BF), vb[lo:hi], preferred_element_type=F32)
            o_acc = _fox_unstack(o / l, first)
            lse_acc = _fox_unstack(jnp.broadcast_to(m + jnp.log(l), (2 * tq, LANES)), first)
            o_ref[rows, :] = o_acc
            lse_ref[rows, :] = lse_acc
            o2 = o_acc * o_acc
            s0 = jnp.sum(jnp.where(first, o2, 0.0), axis=-1, keepdims=True)
            s1 = jnp.sum(jnp.where(first, 0.0, o2), axis=-1, keepdims=True)
            r = lax.rsqrt(jnp.where(first, s0, s1) * (1.0 / FOX_HEAD_DIM) + EPS)
            fn_ref[rows, :] = (o_acc * r * fnw_ref[...]).astype(BF)

    qkv = lambda k: pl.BlockSpec((t, LANES), lambda j: (0, COL_FOX // LANES + 3 * j + k))
    pair = pl.BlockSpec((t, LANES), lambda j: (0, j))
    return pl.pallas_call(
        body, name="fox_fwd", grid=(N_FOX_HEADS // 2,),
        in_specs=[qkv(0), qkv(1), qkv(2), pl.BlockSpec((LANES, t), lambda j: (0, 0)),
                  pl.BlockSpec((1, LANES), lambda j: (0, 0))],
        out_specs=[pair, pair, pair],
        out_shape=[_sds((t, D_FOX)), _sds((t, D_FOX)), _sds((t, D_FOX), BF)],
        compiler_params=_params("parallel"),
    )(proj, proj, proj, cumt, fnw)


def _fox_bwd(proj, cumt, lse, o, do, dproj):
    t = proj.shape[0]
    tq = min(TOKEN_BLOCK, t // 2)
    nq = t // tq

    def body(q_ref, k_ref, v_ref, cumt_ref, lse_ref, o_ref, do_ref, _, dqkv_ref, dcq_ref, dckt_ref, dk_s, dv_s):
        j = pl.program_id(0)

        @pl.when(j == 0)
        def _():
            dcq_ref[...] = jnp.zeros_like(dcq_ref)
            dckt_ref[...] = jnp.zeros_like(dckt_ref)

        lane = _iota((1, LANES), 1)

        first = _iota((1, LANES), 1) < FOX_HEAD_DIM
        kb = k_ref[...].astype(BF)
        vb = v_ref[...].astype(BF)
        dk_s[...] = jnp.zeros_like(dk_s)
        dv_s[...] = jnp.zeros_like(dv_s)
        for i in range(nq):
            rows = slice(i * tq, (i + 1) * tq)
            do_i = do_ref[rows, :]
            prod = do_i * o_ref[rows, :]
            lse_i = lse_ref[rows, :]
            q2 = _fox_stack(q_ref[rows, :] * FOX_SCALE, first)
            do2 = _fox_stack(do_i, first)
            delta = jnp.concatenate([jnp.sum(jnp.where(first, prod, 0.0), axis=-1, keepdims=True),
                                     jnp.sum(jnp.where(first, 0.0, prod), axis=-1, keepdims=True)], axis=0)
            lse2 = jnp.concatenate([lse_i[:, 0:1], lse_i[:, FOX_HEAD_DIM:FOX_HEAD_DIM + 1]], axis=0)
            dq2 = jnp.zeros((2 * tq, LANES), F32)
            dcq2 = jnp.zeros((2 * tq, 1), F32)
            for s, lo, hi in _fox_logits(q2, kb, cumt_ref, j, i, tq):
                p = jnp.exp(s - lse2)
                ds = p * (_mm_nt(do2, vb[lo:hi]) - delta)
                dsb = ds.astype(BF)
                dq2 = dq2 + jnp.dot(dsb, kb[lo:hi], preferred_element_type=F32)
                dk_s[lo:hi, :] += _mm_tn(dsb, q2)
                dv_s[lo:hi, :] += _mm_tn(p, do2)
                dcq2 = dcq2 + jnp.sum(ds, axis=-1, keepdims=True)
                dckt_ref[pl.ds(2 * j, 1), lo:hi] += jnp.sum(ds[:tq], axis=0, keepdims=True)
                dckt_ref[pl.ds(2 * j + 1, 1), lo:hi] += jnp.sum(ds[tq:], axis=0, keepdims=True)
            dqkv_ref[rows, 0:LANES] = (_fox_unstack(dq2, first) * FOX_SCALE).astype(BF)
            dcq_ref[rows, :] += jnp.where(lane == 2 * j, dcq2[:tq], jnp.where(lane == 2 * j + 1, dcq2[tq:], 0.0))
        dqkv_ref[:, LANES:2 * LANES] = dk_s[...].astype(BF)
        dqkv_ref[:, 2 * LANES:QKV] = dv_s[...].astype(BF)

    qkv = lambda k: pl.BlockSpec((t, LANES), lambda j: (0, COL_FOX // LANES + 3 * j + k))
    pair = pl.BlockSpec((t, LANES), lambda j: (0, j))
    rows128 = pl.BlockSpec((LANES, t), lambda j: (0, 0))
    return pl.pallas_call(
        body, name="fox_bwd", grid=(N_FOX_HEADS // 2,),
        in_specs=[qkv(0), qkv(1), qkv(2), rows128, pair, pair, pair, ANY_SPEC],
        out_specs=[pl.BlockSpec((t, QKV), lambda j: (0, COL_FOX // QKV + j)),
                   pl.BlockSpec((t, LANES), lambda j: (0, 0)), rows128],
        out_shape=[_sds(dproj.shape, BF), _sds((t, LANES)), _sds((LANES, t))],
        scratch_shapes=[pltpu.VMEM((t, LANES), F32), pltpu.VMEM((t, LANES), F32)],
        input_output_aliases={7: 0}, compiler_params=_params("arbitrary"),
    )(proj, proj, proj, cumt, lse, o, do, dproj)


def _conv(x, w, row):
    return (w[3:4, :] * x + w[2:3, :] * _shift_down(x, 1, row) + w[1:2, :] * _shift_down(x, 2, row)
            + w[0:1, :] * _shift_down(x, 3, row))


def _chunk_decay(gc_c):
    gi = gc_c[:, 0:CHUNK]
    gj = gc_c.T[0:CHUNK, :]
    ri = _iota((CHUNK, CHUNK), 0)
    cj = _iota((CHUNK, CHUNK), 1)
    return jnp.where(ri >= cj, jnp.exp(jnp.minimum(gi - gj, 0.0)), 0.0), ri > cj


def _gdn_specs(t):
    col = lambda off: pl.BlockSpec((t, LANES), lambda h: (0, off + h))
    cw = lambda off: pl.BlockSpec((CONV_K, LANES), lambda h: (0, off + h))
    mat = pl.BlockSpec((1, t // CHUNK, CHUNK, CHUNK), lambda h: (h, 0, 0, 0))
    qkv = lambda k: pl.BlockSpec((t, LANES), lambda h: (0, COL_GDN // LANES + 3 * h + k))
    return col, cw, mat, qkv


def _gdn_prep(proj, convw, beta, g):
    t = proj.shape[0]
    nch = t // CHUNK

    def body(xq_ref, xk_ref, xv_ref, wq_ref, wk_ref, wv_ref, beta_ref, g_ref,
             qn_ref, kn_ref, cv_ref, gc_ref, be_ref, m_ref, a_ref):
        row = _iota((t, LANES), 0)
        hd = pl.program_id(0)
        be_ref[...] = jnp.broadcast_to(_lane_column(beta_ref[...], SM_GB + hd), (t, LANES))

        def act(x_ref, w_ref):
            y = _conv(x_ref[...], w_ref[...], row)
            return y * _sigmoid(y)

        cq = act(xq_ref, wq_ref)
        ck = act(xk_ref, wk_ref)
        cv_ref[...] = act(xv_ref, wv_ref)
        qn_ref[...] = cq * lax.rsqrt(jnp.sum(cq * cq, axis=-1, keepdims=True) + EPS) * GDN_QSCALE
        kn_ref[...] = ck * lax.rsqrt(jnp.sum(ck * ck, axis=-1, keepdims=True) + EPS)
        gc = jnp.broadcast_to(_lane_column(g_ref[...], SM_GA + hd), (t, LANES))
        pos = row % CHUNK
        step = 1
        while step < CHUNK:
            gc = gc + jnp.where(pos >= step, pltpu.roll(gc, step, 0), 0.0)
            step *= 2
        gc_ref[...] = gc

        group = 4 if nch % 4 == 0 else 1

        def chunks(gi, carry):
            ns = [gi * group + c for c in range(group)]
            sls = [pl.ds(pl.multiple_of(n * CHUNK, CHUNK), CHUNK) for n in ns]
            ks = [kn_ref[sl, :] for sl in sls]
            kk = [_mm_nt(k_c * be_ref[sl, :], k_c) for k_c, sl in zip(ks, sls)]
            qk = [_mm_nt(qn_ref[sl, :], k_c) for k_c, sl in zip(ks, sls)]
            for c, n in enumerate(ns):
                decay, strict = _chunk_decay(gc_ref[sls[c], :])
                m_ref[0, n] = jnp.where(strict, kk[c] * decay, 0.0)
                a_ref[0, n] = qk[c] * decay
            return carry

        lax.fori_loop(0, nch // group, chunks, 0)

    col, cw, mat, qkv = _gdn_specs(t)
    return pl.pallas_call(
        body, name="gdn_prep", grid=(N_GDN_HEADS,),
        in_specs=[qkv(0), qkv(1), qkv(2), cw(0), cw(4), cw(8)] + [pl.BlockSpec((t, LANES), lambda h: (0, 0))] * 2,
        out_specs=[col(0), col(0), col(0), col(0), col(0), mat, mat],
        out_shape=[_sds((t, D_GDN))] * 5 + [_sds((N_GDN_HEADS, nch, CHUNK, CHUNK))] * 2,
        compiler_params=_params("parallel"),
    )(proj, proj, proj, convw, convw, convw, beta, g)


def _tri_inverse(m3):
    assert m3.shape == (LANES, CHUNK, CHUNK)

    def body(m_ref, t_ref, ms, ts):
        for i in range(CHUNK):
            ms[i * CHUNK:(i + 1) * CHUNK, :] = m_ref[:, i, :].T
        cidx = _iota((CHUNK, LANES), 0)

        def outer(i, carry):
            def inner(jj, acc):
                mrow = ms[pl.ds(i * CHUNK + jj, 1), :]
                return acc - mrow * ts[pl.ds(pl.multiple_of(jj * CHUNK, CHUNK), CHUNK), :]

            acc = lax.fori_loop(0, i, inner, jnp.where(cidx == i, 1.0, 0.0).astype(F32))
            ts[pl.ds(pl.multiple_of(i * CHUNK, CHUNK), CHUNK), :] = acc
            return carry

        lax.fori_loop(0, CHUNK, outer, 0)
        for i in range(CHUNK):
            t_ref[:, i, :] = ts[i * CHUNK:(i + 1) * CHUNK, :].T

    return pl.pallas_call(
        body, name="tri_inverse", in_specs=[VMEM_SPEC], out_specs=VMEM_SPEC,
        out_shape=_sds((LANES, CHUNK, CHUNK)),
        scratch_shapes=[pltpu.VMEM((CHUNK * CHUNK, LANES), F32), pltpu.VMEM((CHUNK * CHUNK, LANES), F32)],
        compiler_params=_params(),
    )(m3)


def _gdn_chunk_terms(q, k, v, b, gcc):
    eg = jnp.exp(gcc)
    last = gcc[CHUNK - 1:CHUNK, :]
    egl = jnp.exp(last - gcc)
    gl = jnp.exp(last)
    kb = k * b
    return eg, egl, gl, kb, v * b, kb * eg, q * eg, k * egl


GDN_BLOCK_CHUNKS = 4


def _gdn_block_specs(t, reverse):
    cb = GDN_BLOCK_CHUNKS
    nb = t // (cb * CHUNK)
    idx = (lambda i: nb - 1 - i) if reverse else (lambda i: i)
    tok = pl.BlockSpec((cb * CHUNK, D_GDN), lambda i: (idx(i), 0))
    mat = pl.BlockSpec((N_GDN_HEADS, cb, CHUNK, CHUNK), lambda i: (0, idx(i), 0, 0))
    state = pl.BlockSpec((N_GDN_HEADS, cb, GDN_HEAD_DIM, GDN_HEAD_DIM), lambda i: (0, idx(i), 0, 0))
    return nb, tok, mat, state


def _gdn_scan(qn, kn, cv, be, gc, tinv, amat):
    t = qn.shape[0]
    nch = t // CHUNK

    def body(q_ref, k_ref, v_ref, b_ref, gc_ref, t_ref, a_ref, o_ref, sall_ref, vn_ref, s_scr):
        @pl.when(pl.program_id(0) == 0)
        def _():
            s_scr[...] = jnp.zeros_like(s_scr)

        heads = range(N_GDN_HEADS)
        cols = [slice(hd * LANES, (hd + 1) * LANES) for hd in heads]
        s = [s_scr[hd] for hd in heads]
        for cc in range(GDN_BLOCK_CHUNKS):
            rs = slice(cc * CHUNK, (cc + 1) * CHUNK)
            terms = [_gdn_chunk_terms(q_ref[rs, cs], k_ref[rs, cs], v_ref[rs, cs], b_ref[rs, cs], gc_ref[rs, cs])
                     for cs in cols]
            for hd in heads:
                sall_ref[hd, cc] = s[hd]
            uw = [_mm(t_ref[hd, cc], jnp.concatenate([terms[hd][4], terms[hd][5]], axis=1)) for hd in heads]
            ws_qs = [_mm(jnp.concatenate([uw[hd][:, LANES:], terms[hd][6]], axis=0), s[hd]) for hd in heads]
            vn = [uw[hd][:, :LANES] - ws_qs[hd][:CHUNK] for hd in heads]
            a_vn = [_mm(a_ref[hd, cc], vn[hd]) for hd in heads]
            kd_vn = [_mm_tn(terms[hd][7], vn[hd]) for hd in heads]
            for hd in heads:
                vn_ref[rs, cols[hd]] = vn[hd]
                o_ref[rs, cols[hd]] = ws_qs[hd][CHUNK:] + a_vn[hd]
                s[hd] = s[hd] * terms[hd][2] + kd_vn[hd]
        for hd in heads:
            s_scr[hd] = s[hd]

    nb, tok, mat, state = _gdn_block_specs(t, False)
    return pl.pallas_call(
        body, name="gdn_scan", grid=(nb,),
        in_specs=[tok] * 5 + [mat, mat], out_specs=[tok, state, tok],
        out_shape=[_sds((t, D_GDN)), _sds((N_GDN_HEADS, nch, GDN_HEAD_DIM, GDN_HEAD_DIM)), _sds((t, D_GDN))],
        scratch_shapes=[pltpu.VMEM((N_GDN_HEADS, GDN_HEAD_DIM, GDN_HEAD_DIM), F32)],
        compiler_params=_params("arbitrary"),
    )(qn, kn, cv, be, gc, tinv, amat)


def _gdn_bwd(qn, kn, cv, be, gc, tinv, amat, s_all, vn_all, do):
    t = qn.shape[0]

    def body(q_ref, k_ref, v_ref, b_ref, gc_ref, t_ref, a_ref, sall_ref, vn_ref, do_ref,
             dq_ref, dk_ref, dv_ref, db_ref, dg_ref, ds_scr):
        @pl.when(pl.program_id(0) == 0)
        def _():
            ds_scr[...] = jnp.zeros_like(ds_scr)

        lastrow = _iota((CHUNK, LANES), 0) == CHUNK - 1
        heads = range(N_GDN_HEADS)
        cols = [slice(hd * LANES, (hd + 1) * LANES) for hd in heads]
        each = lambda fn: [fn(hd) for hd in heads]
        rows_cat = lambda x, y: jnp.concatenate([x, y], axis=0)
        lane_cat = lambda x, y: jnp.concatenate([x, y], axis=1)
        dsp = each(lambda hd: ds_scr[hd])
        for cc in reversed(range(GDN_BLOCK_CHUNKS)):
            rs = slice(cc * CHUNK, (cc + 1) * CHUNK)
            q = each(lambda hd: q_ref[rs, cols[hd]])
            k = each(lambda hd: k_ref[rs, cols[hd]])
            v = each(lambda hd: v_ref[rs, cols[hd]])
            b = each(lambda hd: b_ref[rs, cols[hd]])
            gcc = each(lambda hd: gc_ref[rs, cols[hd]])
            do_c = each(lambda hd: do_ref[rs, cols[hd]])
            vn = each(lambda hd: vn_ref[rs, cols[hd]])
            tn = each(lambda hd: t_ref[hd, cc])
            st = each(lambda hd: sall_ref[hd, cc])
            terms = each(lambda hd: _gdn_chunk_terms(q[hd], k[hd], v[hd], b[hd], gcc[hd]))
            eg, egl, gl, kb, vb, kbg, qd, kd = [[terms[hd][i] for hd in heads] for i in range(8)]
            w = each(lambda hd: _mm(tn[hd], kbg[hd]))
            a_do = each(lambda hd: _mm_tn(a_ref[hd, cc], do_c[hd]))
            kd_ds = each(lambda hd: _mm(kd[hd], dsp[hd]))
            da = each(lambda hd: _mm_nt(do_c[hd], vn[hd]))
            dkd = each(lambda hd: _mm_nt(vn[hd], dsp[hd]))
            by_k = each(lambda hd: _mm_nt(rows_cat(kb[hd], q[hd]), k[hd]))
            dgl = each(lambda hd: jnp.sum(jnp.sum(dsp[hd] * st[hd], axis=-1, keepdims=True), axis=0, keepdims=True))
            dvn = each(lambda hd: a_do[hd] + kd_ds[hd])
            do_dvn = each(lambda hd: rows_cat(do_c[hd], dvn[hd]))
            by_s = each(lambda hd: _mm_nt(do_dvn[hd], st[hd]))
            dqd = each(lambda hd: by_s[hd][:CHUNK])
            dvn_dw = each(lambda hd: lane_cat(dvn[hd], -by_s[hd][CHUNK:]))
            dsp = each(lambda hd: _mm_tn(rows_cat(qd[hd], -w[hd]), do_dvn[hd]) + gl[hd] * dsp[hd])
            dt = each(lambda hd: _mm_nt(dvn_dw[hd], lane_cat(vb[hd], kbg[hd])))
            by_t = each(lambda hd: _mm_tn(tn[hd], dvn_dw[hd]))
            tt_dt = each(lambda hd: _mm_tn(tn[hd], dt[hd]))
            dm_raw = each(lambda hd: _mm_nt(tt_dt[hd], tn[hd]))
            masks = each(lambda hd: _chunk_decay(gcc[hd]))
            dkk = each(lambda hd: jnp.where(masks[hd][1], -dm_raw[hd], 0.0) * masks[hd][0])
            dqk = each(lambda hd: da[hd] * masks[hd][0])
            dqk_dkk = each(lambda hd: rows_cat(dqk[hd], dkk[hd]))
            on_k = each(lambda hd: _mm(dqk_dkk[hd], k[hd]))
            dk_mm = each(lambda hd: _mm_tn(dqk_dkk[hd], rows_cat(q[hd], kb[hd])))
            for hd in heads:
                cs = cols[hd]
                dvb, dkbg = by_t[hd][:, :LANES], by_t[hd][:, LANES:]
                gmat = dkk[hd] * by_k[hd][:CHUNK] + dqk[hd] * by_k[hd][CHUNK:]
                dq_ref[rs, cs] = dqd[hd] * eg[hd] + on_k[hd][:CHUNK]
                dkb = on_k[hd][CHUNK:] + dkbg * eg[hd]
                dk_ref[rs, cs] = dkd[hd] * egl[hd] + dk_mm[hd] + dkb * b[hd]
                db = jnp.sum(dkb * k[hd], axis=-1, keepdims=True) + jnp.sum(dvb * v[hd], axis=-1, keepdims=True)
                db_ref[rs, cs] = jnp.broadcast_to(db, (CHUNK, LANES))
                dv_ref[rs, cs] = dvb * b[hd]
                dkd_kd = jnp.sum(dkd[hd] * kd[hd], axis=-1, keepdims=True)
                col_sums = jnp.sum(lane_cat(gmat, jnp.zeros_like(gmat)).T, axis=-1, keepdims=True)
                dgc = (jnp.sum(gmat, axis=-1, keepdims=True) - col_sums[:CHUNK]
                       + jnp.sum(dqd[hd] * qd[hd], axis=-1, keepdims=True)
                       + jnp.sum(dkbg * kbg[hd], axis=-1, keepdims=True) - dkd_kd)
                extra = jnp.sum(dkd_kd, axis=0, keepdims=True) + dgl[hd] * gl[hd]
                dg_ref[rs, cs] = dgc + jnp.where(lastrow, extra, 0.0)
        for hd in heads:
            ds_scr[hd] = dsp[hd]
        dg = dg_ref[...]
        row = _iota(dg.shape, 0)
        pos = row % CHUNK
        step = 1
        while step < CHUNK:
            dg = dg + jnp.where(pos < CHUNK - step, pltpu.roll(dg, dg.shape[0] - step, 0), 0.0)
            step *= 2
        dg_ref[...] = dg

    nb, tok, mat, state = _gdn_block_specs(t, True)
    return pl.pallas_call(
        body, name="gdn_bwd", grid=(nb,),
        in_specs=[tok] * 5 + [mat, mat, state, tok, tok], out_specs=[tok] * 5, out_shape=[_sds((t, D_GDN))] * 5,
        scratch_shapes=[pltpu.VMEM((N_GDN_HEADS, GDN_HEAD_DIM, GDN_HEAD_DIM), F32)],
        compiler_params=_params("arbitrary"),
    )(qn, kn, cv, be, gc, tinv, amat, s_all, vn_all, do)


def _gdn_bwd_conv(proj, convw, dqn, dkn, dcv, dproj):
    t = proj.shape[0]

    def body(xq_ref, xk_ref, xv_ref, wq_ref, wk_ref, wv_ref, dq_ref, dk_ref, dv_ref, _,
             dqkv_ref, dwq_ref, dwk_ref, dwv_ref):
        row = _iota((t, LANES), 0)

        def one(x_ref, w_ref, d_ref, k, dw_ref, scale):
            x = x_ref[...]
            w = w_ref[...]
            y = _conv(x, w, row)
            sg = _sigmoid(y)
            dc = d_ref[...]
            if scale is not None:
                c = y * sg
                r = lax.rsqrt(jnp.sum(c * c, axis=-1, keepdims=True) + EPS)
                ch = c * r
                dc = scale * r * (dc - ch * jnp.sum(dc * ch, axis=-1, keepdims=True))
            dy = dc * sg * (1.0 + y * (1.0 - sg))
            dqkv_ref[:, k * LANES:(k + 1) * LANES] = (
                w[3:4, :] * dy + w[2:3, :] * _shift_up(dy, 1, row) + w[1:2, :] * _shift_up(dy, 2, row)
                + w[0:1, :] * _shift_up(dy, 3, row)).astype(BF)
            for jj in range(CONV_K):
                xs = x if jj == CONV_K - 1 else _shift_down(x, CONV_K - 1 - jj, row)
                dw_ref[jj:jj + 1, :] = jnp.sum(dy * xs, axis=0, keepdims=True)

        one(xq_ref, wq_ref, dq_ref, 0, dwq_ref, GDN_QSCALE)
        one(xk_ref, wk_ref, dk_ref, 1, dwk_ref, 1.0)
        one(xv_ref, wv_ref, dv_ref, 2, dwv_ref, None)

    col, cw, _, qkv = _gdn_specs(t)
    return pl.pallas_call(
        body, name="gdn_bwd_conv", grid=(N_GDN_HEADS,),
        in_specs=[qkv(0), qkv(1), qkv(2), cw(0), cw(4), cw(8), col(0), col(0), col(0), ANY_SPEC],
        out_specs=[pl.BlockSpec((t, QKV), lambda h: (0, COL_GDN // QKV + h)), cw(0), cw(0), cw(0)],
        out_shape=[_sds(dproj.shape, BF)] + [_sds((CONV_K, D_GDN))] * 3,
        input_output_aliases={9: 0}, compiler_params=_params("parallel"),
    )(proj, proj, proj, convw, convw, convw, dqn, dkn, dcv, dproj)


def _mix_out(fox_n, gdn_o, proj, gnw, w_out, x, pmw, plw, after):
    t = x.shape[0]
    tm = min(MATMUL_BLOCK, t)

    def body(fn_ref, go_ref, gz_ref, gnw_ref, w_ref, x_ref, pmw_ref, plw_ref, x1_ref, h2_ref, mixed_ref, omix_ref,
             h2t_ref):
        omix_ref[:, 0:D_FOX] = fn_ref[...]
        for hd in range(N_GDN_HEADS):
            cs = slice(hd * LANES, (hd + 1) * LANES)
            go = go_ref[:, cs]
            r = lax.rsqrt(jnp.mean(go * go, axis=-1, keepdims=True) + EPS)
            gz = gz_ref[:, cs]
            omix_ref[:, D_FOX + hd * LANES:D_FOX + (hd + 1) * LANES] = (
                go * r * gnw_ref[...] * (gz * _sigmoid(gz))).astype(BF)
        mixed = jnp.dot(omix_ref[...], w_ref[...], preferred_element_type=F32)
        mixed_ref[...] = mixed
        r2 = lax.rsqrt(jnp.mean(mixed * mixed, axis=-1, keepdims=True) + EPS)
        x1 = x_ref[...] + mixed * r2 * pmw_ref[...]
        x1_ref[...] = x1
        r3 = lax.rsqrt(jnp.mean(x1 * x1, axis=-1, keepdims=True) + EPS)
        h2 = x1 * r3 * plw_ref[...]
        h2_ref[...] = h2.astype(BF)
        h2t_ref[...] = h2.T.astype(BF)

    tok = lambda w: pl.BlockSpec((tm, w), lambda i: (i, 0))
    vec = lambda w: pl.BlockSpec((1, w), lambda i: (0, 0))
    return pl.pallas_call(
        _ordered(body), name="mix_out", grid=(t // tm,),
        in_specs=[ANY_SPEC, tok(D_FOX), tok(D_GDN), pl.BlockSpec((tm, D_GDN), lambda i: (i, COL_GZ // D_GDN)), vec(LANES),
                  pl.BlockSpec((D_MODEL, D_MODEL), lambda i: (0, 0)), tok(D_MODEL), vec(D_MODEL), vec(D_MODEL)],
        out_specs=[tok(D_MODEL)] * 4 + [pl.BlockSpec((D_MODEL, tm), lambda i: (0, i))],
        out_shape=[_sds((t, D_MODEL)), _sds((t, D_MODEL), BF), _sds((t, D_MODEL)), _sds((t, D_MODEL), BF),
                   _sds((D_MODEL, t), BF)],
        compiler_params=_params("parallel"),
    )(after, fox_n, gdn_o, proj, gnw, w_out, x, pmw, plw)


def _out_bwd(dmixed, w_out, o_fox, gdn_o, proj, fnw, gnw, after):
    t = dmixed.shape[0]
    tm = min(MATMUL_BLOCK, t)

    def body(dm_ref, w_ref, of_ref, go_ref, gz_ref, fnw_ref, gnw_ref, dof_ref, dgo_ref, dgz_ref, dfw_ref, dgw_ref):
        i = pl.program_id(0)

        @pl.when(i == 0)
        def _():
            dfw_ref[...] = jnp.zeros_like(dfw_ref)
            dgw_ref[...] = jnp.zeros_like(dgw_ref)

        domix = _mm_nt(dm_ref[...], w_ref[...])
        first = _iota((1, LANES), 1) < FOX_HEAD_DIM
        dfw = jnp.zeros((1, LANES), F32)
        dgw = jnp.zeros((1, LANES), F32)
        for pr in range(N_FOX_HEADS // 2):
            cs = slice(pr * LANES, (pr + 1) * LANES)
            o = of_ref[:, cs]
            dfn = domix[:, cs]
            o2 = o * o
            s0 = jnp.sum(jnp.where(first, o2, 0.0), axis=-1, keepdims=True)
            s1 = jnp.sum(jnp.where(first, 0.0, o2), axis=-1, keepdims=True)
            r = lax.rsqrt(jnp.where(first, s0, s1) * (1.0 / FOX_HEAD_DIM) + EPS)
            oh = o * r
            dfw = dfw + jnp.sum(dfn * oh, axis=0, keepdims=True)
            doh = dfn * fnw_ref[...]
            pr_ = doh * oh
            m0 = jnp.sum(jnp.where(first, pr_, 0.0), axis=-1, keepdims=True)
            m1 = jnp.sum(jnp.where(first, 0.0, pr_), axis=-1, keepdims=True)
            dof_ref[:, cs] = r * (doh - oh * jnp.where(first, m0, m1) * (1.0 / FOX_HEAD_DIM))
        for hd in range(N_GDN_HEADS):
            cs = slice(hd * LANES, (hd + 1) * LANES)
            go = go_ref[:, cs]
            gz = gz_ref[:, cs]
            dgated = domix[:, D_FOX + hd * LANES:D_FOX + (hd + 1) * LANES]
            r = lax.rsqrt(jnp.mean(go * go, axis=-1, keepdims=True) + EPS)
            goh = go * r
            sg = _sigmoid(gz)
            sz = gz * sg
            gn = goh * gnw_ref[...]
            dgn = dgated * sz
            dgz_ref[:, cs] = (dgated * gn * sg * (1.0 + gz * (1.0 - sg))).astype(BF)
            dgw = dgw + jnp.sum(dgn * goh, axis=0, keepdims=True)
            dgh = dgn * gnw_ref[...]
            dgo_ref[:, cs] = r * (dgh - goh * jnp.mean(dgh * goh, axis=-1, keepdims=True))
        dfw_ref[...] += dfw + pltpu.roll(dfw, FOX_HEAD_DIM, 1)
        dgw_ref[...] += dgw

    tok = lambda w: pl.BlockSpec((tm, w), lambda i: (i, 0))
    vec = lambda w: pl.BlockSpec((1, w), lambda i: (0, 0))
    return pl.pallas_call(
        _ordered(body), name="out_bwd", grid=(t // tm,),
        in_specs=[ANY_SPEC, tok(D_MODEL), pl.BlockSpec((D_MODEL, D_MODEL), lambda i: (0, 0)), tok(D_FOX), tok(D_GDN),
                  pl.BlockSpec((tm, D_GDN), lambda i: (i, COL_GZ // D_GDN)), vec(LANES), vec(LANES)],
        out_specs=[tok(D_FOX), tok(D_GDN), pl.BlockSpec((tm, D_GDN), lambda i: (i, COL_GZ // D_GDN)), vec(LANES),
                   vec(LANES)],
        out_shape=[_sds((t, D_FOX)), _sds((t, D_GDN)), _sds((t, PROJ_W), BF), _sds((1, LANES)), _sds((1, LANES))],
        compiler_params=_params("arbitrary"),
    )(after, dmixed, w_out, o_fox, gdn_o, proj, fnw, gnw)


def _mlp_up(h2, w_upt):
    t = h2.shape[0]
    tm = min(MATMUL_BLOCK, t)

    def body(h_ref, w_ref, up_ref):
        up_ref[...] = lax.dot_general(h_ref[...], w_ref[...], (((1,), (1,)), ((), ())),
                                      preferred_element_type=F32).astype(BF)

    return pl.pallas_call(
        body, name="mlp_up", grid=(t // tm,),
        in_specs=[pl.BlockSpec((tm, D_MODEL), lambda i: (i, 0)), pl.BlockSpec((D_FF, D_MODEL), lambda i: (0, 0))],
        out_specs=pl.BlockSpec((tm, D_FF), lambda i: (i, 0)), out_shape=_sds((t, D_FF), BF),
        compiler_params=_params("parallel"),
    )(h2, w_upt)


def _mlp_down_loss(up, w_down, x1, pw, target):
    t = up.shape[0]
    tm = min(MATMUL_BLOCK, t)

    def body(up_ref, w_ref, x1_ref, pw_ref, tg_ref, dy_ref, dx2_ref, loss_ref, dpw_ref):
        i = pl.program_id(0)

        @pl.when(i == 0)
        def _():
            loss_ref[...] = jnp.zeros_like(loss_ref)
            dpw_ref[...] = jnp.zeros_like(dpw_ref)

        u = jnp.maximum(up_ref[...].astype(F32), 0.0)
        y = jnp.dot((u * u).astype(BF), w_ref[...], preferred_element_type=F32)
        r = lax.rsqrt(jnp.mean(y * y, axis=-1, keepdims=True) + EPS)
        yh = y * r
        pw = pw_ref[...]
        err = x1_ref[...] + yh * pw - tg_ref[...]
        part = jnp.sum(jnp.sum(err * err, axis=-1, keepdims=True), axis=0, keepdims=True) * (0.5 / D_MODEL)
        loss_ref[...] += jnp.broadcast_to(part, loss_ref.shape)
        dx2 = err * (1.0 / D_MODEL)
        dx2_ref[...] = dx2
        dpw_ref[...] += jnp.sum(dx2 * yh, axis=0, keepdims=True)
        dyh = dx2 * pw
        dy_ref[...] = (r * (dyh - yh * jnp.mean(dyh * yh, axis=-1, keepdims=True))).astype(BF)

    tok = lambda w: pl.BlockSpec((tm, w), lambda i: (i, 0))
    vec = lambda w: pl.BlockSpec((1, w), lambda i: (0, 0))
    return pl.pallas_call(
        body, name="mlp_down_loss", grid=(t // tm,),
        in_specs=[tok(D_FF), pl.BlockSpec((D_FF, D_MODEL), lambda i: (0, 0)), tok(D_MODEL), vec(D_MODEL), tok(D_MODEL)],
        out_specs=[tok(D_MODEL), tok(D_MODEL), vec(LANES), vec(D_MODEL)],
        out_shape=[_sds((t, D_MODEL), BF), _sds((t, D_MODEL)), _sds((1, LANES)), _sds((1, D_MODEL))],
        compiler_params=_params("arbitrary"),
    )(up, w_down, x1, pw, target)


def _mlp_bwd_act(dy, w_down, up):
    t = dy.shape[0]
    tm = min(MATMUL_BLOCK, t)

    def body(dy_ref, w_ref, up_ref, dup_ref):
        da = lax.dot_general(dy_ref[...], w_ref[...], (((1,), (1,)), ((), ())), preferred_element_type=F32)
        dup_ref[...] = (da * (2.0 * jnp.maximum(up_ref[...].astype(F32), 0.0))).astype(BF)

    return pl.pallas_call(
        body, name="mlp_bwd_act", grid=(t // tm,),
        in_specs=[pl.BlockSpec((tm, D_MODEL), lambda i: (i, 0)), pl.BlockSpec((D_FF, D_MODEL), lambda i: (0, 0)),
                  pl.BlockSpec((tm, D_FF), lambda i: (i, 0))],
        out_specs=pl.BlockSpec((tm, D_FF), lambda i: (i, 0)), out_shape=_sds((t, D_FF), BF),
        compiler_params=_params("parallel"),
    )(dy, w_down, up)


def _mlp_bwd_in(dup, w_up, x1, plw, dx2, mixed, pmw, after):
    t = dup.shape[0]
    tm = min(MATMUL_BLOCK, t)

    def body(dup_ref, w_ref, x1_ref, plw_ref, dx2_ref, mx_ref, pmw_ref, dx1_ref, dmixed_ref, dplw_ref, dpmw_ref):
        i = pl.program_id(0)

        @pl.when(i == 0)
        def _():
            dplw_ref[...] = jnp.zeros_like(dplw_ref)
            dpmw_ref[...] = jnp.zeros_like(dpmw_ref)

        dh = jnp.dot(dup_ref[...], w_ref[...], preferred_element_type=F32)
        x1 = x1_ref[...]
        r = lax.rsqrt(jnp.mean(x1 * x1, axis=-1, keepdims=True) + EPS)
        xh = x1 * r
        dplw_ref[...] += jnp.sum(dh * xh, axis=0, keepdims=True)
        dxh = dh * plw_ref[...]
        dx1 = dx2_ref[...] + r * (dxh - xh * jnp.mean(dxh * xh, axis=-1, keepdims=True))
        dx1_ref[...] = dx1
        mx = mx_ref[...]
        r2 = lax.rsqrt(jnp.mean(mx * mx, axis=-1, keepdims=True) + EPS)
        mh = mx * r2
        dpmw_ref[...] += jnp.sum(dx1 * mh, axis=0, keepdims=True)
        dmh = dx1 * pmw_ref[...]
        dmixed_ref[...] = (r2 * (dmh - mh * jnp.mean(dmh * mh, axis=-1, keepdims=True))).astype(BF)

    tok = lambda w: pl.BlockSpec((tm, w), lambda i: (i, 0))
    vec = lambda w: pl.BlockSpec((1, w), lambda i: (0, 0))
    return pl.pallas_call(
        _ordered(body), name="mlp_bwd_in", grid=(t // tm,),
        in_specs=[ANY_SPEC, tok(D_FF), pl.BlockSpec((D_FF, D_MODEL), lambda i: (0, 0)), tok(D_MODEL),
                  vec(D_MODEL), tok(D_MODEL), tok(D_MODEL), vec(D_MODEL)],
        out_specs=[tok(D_MODEL), tok(D_MODEL), vec(D_MODEL), vec(D_MODEL)],
        out_shape=[_sds((t, D_MODEL)), _sds((t, D_MODEL), BF), _sds((1, D_MODEL)), _sds((1, D_MODEL))],
        compiler_params=_params("arbitrary"),
    )(after, dup, w_up, x1, plw, dx2, mixed, pmw)


def _wgrad(a, b, a_cols, split=1, a_fn=None, a_block0=0, name="wgrad"):
    t, b_cols = b.shape
    n_a = (a.shape[1] - a_block0 * a_cols) // a_cols if a_block0 else a.shape[1] // a_cols

    def body(a_ref, b_ref, o_ref):
        av = a_ref[...]
        if a_fn is not None:
            av = a_fn(av)
        o_ref[...] = _mm_tn(av, b_ref[...]).astype(BF).reshape(o_ref.shape)

    return pl.pallas_call(
        body, name=name, grid=(n_a,),
        in_specs=[pl.BlockSpec((t, a_cols), lambda i: (0, i + a_block0)), pl.BlockSpec((t, b_cols), lambda i: (0, 0))],
        out_specs=pl.BlockSpec((split, a_cols // split, b_cols), lambda i: (i, 0, 0)),
        out_shape=_sds((n_a * split, a_cols // split, b_cols), BF),
        compiler_params=_params("parallel"),
    )(a, b)


def _wgrad_pre_t(at, b, b_cols, name):
    rows, t = at.shape
    n_b = b.shape[1] // b_cols

    def body(a_ref, b_ref, o_ref):
        o_ref[0] = jnp.dot(a_ref[...], b_ref[...], preferred_element_type=F32).astype(BF)

    return pl.pallas_call(
        body, name=name, grid=(n_b,),
        in_specs=[pl.BlockSpec((rows, t), lambda j: (0, 0)), pl.BlockSpec((t, b_cols), lambda j: (0, j))],
        out_specs=pl.BlockSpec((1, rows, b_cols), lambda j: (j, 0, 0)), out_shape=_sds((n_b, rows, b_cols), BF),
        compiler_params=_params("parallel"),
    )(at, b)


def _small_bwd(proj, fb, al, dtb, dcq, dckt, dbe, dge, dproj):
    t = proj.shape[0]

    def body(sm_ref, fb_ref, al_ref, dtb_ref, dcq_ref, dckt_ref, dbe_ref, dge_ref, _, dsm_ref, dvec_ref):
        s = sm_ref[...]
        lane = _iota((1, LANES), 1)
        dcum = dcq_ref[...] - dckt_ref[...].T
        row = _iota((t, LANES), 0)
        step = 1
        while step < t:
            dcum = dcum + _shift_up(dcum, step, row)
            step *= 2
        dff = dcum * _sigmoid(-(s + fb_ref[...]))
        dbeta = jnp.zeros((t, LANES), F32)
        dg = jnp.zeros((t, LANES), F32)
        for hd in range(N_GDN_HEADS):
            dbeta = jnp.where(lane == SM_GB + hd, dbe_ref[:, hd * LANES:hd * LANES + 1], dbeta)
            dg = jnp.where(lane == SM_GA + hd, dge_ref[:, hd * LANES:hd * LANES + 1], dg)
        beta = _sigmoid(s)
        dgb = dbeta * beta * (1.0 - beta)
        za = s + dtb_ref[...]
        nea = -jnp.exp(al_ref[...])
        dga = dg * nea * _sigmoid(za)
        is_f = lane < SM_GB
        is_b = (lane >= SM_GB) & (lane < SM_GA)
        is_a = (lane >= SM_GA) & (lane < SM_GA + 4)
        dsm_ref[...] = jnp.where(is_f, dff, jnp.where(is_b, dgb, jnp.where(is_a, dga, 0.0))).astype(BF)
        dvec_ref[...] = jnp.zeros_like(dvec_ref)
        dvec_ref[0:1, :] = jnp.sum(jnp.where(is_f, dff, 0.0), axis=0, keepdims=True)
        dvec_ref[1:2, :] = jnp.sum(jnp.where(is_a, dg * nea * _softplus(za), 0.0), axis=0, keepdims=True)
        dvec_ref[2:3, :] = jnp.sum(jnp.where(is_a, dga, 0.0), axis=0, keepdims=True)

    vec = pl.BlockSpec((1, LANES), lambda i: (0, 0))
    full = lambda r, c: pl.BlockSpec((r, c), lambda i: (0, 0))
    small = pl.BlockSpec((t, LANES), lambda i: (0, COL_SMALL // LANES))
    return pl.pallas_call(
        body, name="small_bwd", grid=(1,),
        in_specs=[small, vec, vec, vec, full(t, LANES), full(LANES, t), full(t, 512), full(t, 512), ANY_SPEC],
        out_specs=[small, full(8, LANES)], out_shape=[_sds(dproj.shape, BF), _sds((8, LANES))],
        input_output_aliases={8: 0}, compiler_params=_params("arbitrary"),
    )(proj, fb, al, dtb, dcq, dckt, dbe, dge, dproj)


def _in_bwd(dproj, wt_al, x, nw, dx1, after):
    t = x.shape[0]
    tm = min(MATMUL_BLOCK, t)

    def body(dp_ref, w_ref, x_ref, nw_ref, dx1_ref, dx_ref, dnw_ref):
        i = pl.program_id(0)

        @pl.when(i == 0)
        def _():
            dnw_ref[...] = jnp.zeros_like(dnw_ref)

        dh = jnp.dot(dp_ref[...], w_ref[...], preferred_element_type=F32)
        xv = x_ref[...]
        r = lax.rsqrt(jnp.mean(xv * xv, axis=-1, keepdims=True) + EPS)
        xh = xv * r
        dnw_ref[...] += jnp.sum(dh * xh, axis=0, keepdims=True)
        dxh = dh * nw_ref[...]
        dx_ref[...] = dx1_ref[...] + r * (dxh - xh * jnp.mean(dxh * xh, axis=-1, keepdims=True))

    tok = lambda w: pl.BlockSpec((tm, w), lambda i: (i, 0))
    vec = lambda w: pl.BlockSpec((1, w), lambda i: (0, 0))
    return pl.pallas_call(
        _ordered(body), name="in_bwd", grid=(t // tm,),
        in_specs=[ANY_SPEC, tok(PROJ_W), pl.BlockSpec((PROJ_W, D_MODEL), lambda i: (0, 0)), tok(D_MODEL), vec(D_MODEL),
                  tok(D_MODEL)],
        out_specs=[tok(D_MODEL), vec(D_MODEL)], out_shape=[_sds((t, D_MODEL)), _sds((1, D_MODEL))],
        compiler_params=_params("arbitrary"),
    )(after, dproj, wt_al, x, nw, dx1)


def _row(v, width=None):
    v = v.reshape(1, -1).astype(F32)
    if width is not None and v.shape[1] < width:
        v = jnp.pad(v, ((0, 0), (0, width - v.shape[1])))
    return v


def _lane_vec(v, first):
    return jnp.pad(v.astype(F32), (first, LANES - first - v.shape[0])).reshape(1, LANES)


def _local_step(x, target, wt_al, started, late_weights, on_grads, convw, pre_mix_norm, fox_f_bias, fox_out_norm,
                gdn_a_log, gdn_dt_bias, gdn_out_norm, post_mix_norm, pre_mlp_norm, post_mlp_norm):
    t = x.shape[0]
    nch = t // CHUNK
    nw, pmw, plw, pw = _row(pre_mix_norm), _row(post_mix_norm), _row(pre_mlp_norm), _row(post_mlp_norm)
    fb, al, dtb = _lane_vec(fox_f_bias, SM_FF), _lane_vec(gdn_a_log, SM_GA), _lane_vec(gdn_dt_bias, SM_GA)
    fnw = _row(jnp.tile(fox_out_norm, 2))
    gnw = _row(gdn_out_norm)

    proj, h = _norm_proj(x, nw, wt_al, started)
    cumt, beta, g = _small_prep(proj, fb, al, dtb)
    o_fox, lse, fox_n = _fox_fwd(proj, cumt, fnw)
    qn, kn, cv, gc, be, mmat, amat = _gdn_prep(proj, convw, beta, g)
    n_prob = N_GDN_HEADS * nch
    m3 = mmat.reshape(n_prob, CHUNK, CHUNK)
    if n_prob < LANES:
        m3 = jnp.pad(m3, ((0, LANES - n_prob), (0, 0), (0, 0)))
    tinv = _tri_inverse(m3)[:n_prob].reshape(N_GDN_HEADS, nch, CHUNK, CHUNK)
    token = late_weights("mlp_relay", tinv)
    gdn_o, s_all, vn_all = _gdn_scan(qn, kn, cv, be, gc, tinv, amat)
    w_out = late_weights("w_out", gdn_o)
    x1, h2, mixed, omix, h2t = _mix_out(fox_n, gdn_o, proj, gnw, w_out, x, pmw, plw, token)
    w_up, w_down = late_weights("mlp", h2)
    up = _mlp_up(h2, w_up)
    dy, dx2, loss, d_pw = _mlp_down_loss(up, w_down, x1, pw, target)

    dup = _mlp_bwd_act(dy, w_down, up)
    relu2 = lambda u: jnp.square(jnp.maximum(u.astype(F32), 0.0))
    g_down = _wgrad(up, dy, D_FF // N_DEV, a_fn=relu2, name="wgrad_down")
    g_up = _wgrad_pre_t(h2t, dup, D_FF // N_DEV, name="wgrad_up")
    token = on_grads("mlp", (g_up, g_down))
    dx1, dmixed, d_plw, d_pmw = _mlp_bwd_in(dup, w_up, x1, plw, dx2, mixed, pmw, token)
    token = on_grads("w_out", _wgrad(omix, dmixed, 512, split=4, name="wgrad_out"))
    do_fox, dgo, dproj, d_fnw, d_gnw = _out_bwd(dmixed, w_out, o_fox, gdn_o, proj, fnw, gnw, token)
    dproj, dcq, dckt = _fox_bwd(proj, cumt, lse, o_fox, do_fox, dproj)
    dqn, dkn, dcv, dbe, dge = _gdn_bwd(qn, kn, cv, be, gc, tinv, amat, s_all, vn_all, dgo)
    dproj, dwq, dwk, dwv = _gdn_bwd_conv(proj, convw, dqn, dkn, dcv, dproj)
    dproj, dvec = _small_bwd(proj, fb, al, dtb, dcq, dckt, dbe, dge, dproj)
    g_main = _wgrad(dproj, h, WGRAD_IN_ROWS, name="wgrad_in")
    g_tail = _wgrad(dproj, h, LANES, a_block0=COL_SMALL // LANES, name="wgrad_in_small")
    token = on_grads("w_in", (g_main, g_tail))
    grad_x, d_nw = _in_bwd(dproj, wt_al, x, nw, dx1, token)
    small = dict(norms=(d_nw, d_pmw, d_plw, d_pw), fox_out_norm=d_fnw, gdn_out_norm=d_gnw, loss=loss, vectors=dvec,
                 conv=(dwq, dwk, dwv))
    return grad_x, small


MESH_IDS = pl.DeviceIdType.MESH
CHIP_FLIPS = ((0, 0), (1, 0), (0, 1), (1, 1))


def _place():
    return lax.axis_index("x"), lax.axis_index("y"), lax.axis_index("c")


def _all_gather(blocks):
    n = len(blocks)

    def body(*refs):
        ins, outs, (send_sems, recv_sems, local_sems) = refs[:n], refs[n:2 * n], refs[2 * n:]
        x, y, c = _place()
        sibling = (x, y, 1 - c)
        chips = [(x ^ fx, y ^ fy) for fx, fy in CHIP_FLIPS[1:]]

        def slot(out, px, py, pc):
            return out.at[4 * px + 2 * py + pc]

        def copy(a, k, block, to, src=None):
            return pltpu.make_async_remote_copy(
                src_ref=slot(outs[a], *block) if src is None else src, dst_ref=slot(outs[a], *block),
                send_sem=send_sems.at[a, k], recv_sem=recv_sems.at[a, k], device_id=to, device_id_type=MESH_IDS)

        pending = []
        for a in range(n):
            mine = pltpu.make_async_copy(ins[a], slot(outs[a], x, y, c), local_sems.at[a])
            mine.start()
            pending.append(mine)
        sends = []
        for a in range(n):
            first = [copy(a, 0, (x, y, c), sibling, src=ins[a])]
            first += [copy(a, 1 + j, (x, y, c), (*chip, c), src=ins[a]) for j, chip in enumerate(chips)]
            for cp in first:
                cp.start()
            sends += first
        for a in range(n):
            for j, chip in enumerate(chips):
                copy(a, 1 + j, (*chip, c), (x, y, c)).wait_recv()
                fwd = copy(a, 4 + j, (*chip, c), sibling)
                fwd.start()
                sends.append(fwd)
        for a in range(n):
            copy(a, 0, sibling, (x, y, c)).wait_recv()
            for j, chip in enumerate(chips):
                copy(a, 4 + j, (*chip, 1 - c), (x, y, c)).wait_recv()
        for cp in sends:
            cp.wait_send()
        for cp in pending:
            cp.wait()

    return pl.pallas_call(
        body, name="all_gather_weights", in_specs=[ANY_SPEC] * n, out_specs=[ANY_SPEC] * n,
        out_shape=[_sds((N_DEV,) + b.shape, b.dtype) for b in blocks],
        scratch_shapes=[pltpu.SemaphoreType.DMA((n, 7)), pltpu.SemaphoreType.DMA((n, 7)), pltpu.SemaphoreType.DMA((n,))],
        compiler_params=pltpu.CompilerParams(has_side_effects=True),
    )(*blocks)


def _adamw(w, g, m, v):
    m = ADAM_B1 * m + (1.0 - ADAM_B1) * g
    v = ADAM_B2 * v + (1.0 - ADAM_B2) * (g * g)
    m_hat = m / (1.0 - ADAM_B1 ** ADAM_STEP)
    v_hat = v / (1.0 - ADAM_B2 ** ADAM_STEP)
    return -ADAM_LR * (m_hat / (jnp.sqrt(v_hat) + ADAM_EPS) + ADAM_WD * w), m, v


def _pair_reduce(g, name):
    _, r, c_ = g.shape
    n = len(CHIP_FLIPS)

    def body(g_ref, out_ref, sib_buf, send_sems, recv_sems):
        x, y, c = _place()
        chips = [(x ^ fx, y ^ fy) for fx, fy in CHIP_FLIPS]
        piece = lambda chip, core: g_ref.at[4 * chip[0] + 2 * chip[1] + core]
        copies = [pltpu.make_async_remote_copy(
            src_ref=piece(chip, 1 - c), dst_ref=sib_buf.at[j], send_sem=send_sems.at[j], recv_sem=recv_sems.at[j],
            device_id=(x, y, 1 - c), device_id_type=MESH_IDS) for j, chip in enumerate(chips)]
        for cp in copies:
            cp.start()
        for j, chip in enumerate(chips):
            copies[j].wait_recv()
            out_ref[j] = (piece(chip, c)[...].astype(F32) + sib_buf[j].astype(F32)).astype(BF)
        for cp in copies:
            cp.wait_send()

    return pl.pallas_call(
        body, name=name, in_specs=[VMEM_SPEC], out_specs=VMEM_SPEC, out_shape=_sds((n, r, c_), BF),
        scratch_shapes=[pltpu.VMEM((n, r, c_), BF), pltpu.SemaphoreType.DMA((n,)), pltpu.SemaphoreType.DMA((n,))],
        compiler_params=pltpu.CompilerParams(vmem_limit_bytes=VMEM_LIMIT, has_side_effects=True),
    )(g)


HBM_SPEC = pl.BlockSpec(memory_space=pltpu.HBM)
SEM_SPEC = pl.BlockSpec(memory_space=pltpu.SEMAPHORE)
DATAFLOW = pltpu.SideEffectType.DATAFLOW_SIDE_EFFECTING


def _peers():
    x, y, c = _place()
    return 4 * x + 2 * y + c, [(x ^ (k >> 2), y ^ ((k >> 1) & 1), c ^ (k & 1)) for k in range(1, N_DEV)]


def _peer_index(peer):
    return 4 * peer[0] + 2 * peer[1] + peer[2]


def _zones_with_own(srcs, pieces, name, after=None, dtype=None, chips=False):
    n = len(srcs)
    slots = len(CHIP_FLIPS) if chips else N_DEV
    extra = [] if after is None else [after]
    dtypes = [s_.dtype if pieces or dtype is None else dtype for s_ in srcs]

    def body(me_ref, *refs):
        outs = refs[n + len(extra):]
        for a in range(n):
            if pieces:
                outs[a][0] = refs[a][0]
            else:
                val = refs[a][...].astype(dtypes[a])
                outs[a][0] = val
                outs[n + a][...] = val

    shapes = [s_.shape[1:] if pieces else s_.shape for s_ in srcs]
    mine = lambda sh: pl.BlockSpec((1,) + sh, lambda i, me_ref: (me_ref[0], 0, 0))
    whole = lambda sh: pl.BlockSpec(sh, lambda i, me_ref: (0, 0))
    in_specs = [mine(sh) if pieces else whole(sh) for sh in shapes]
    out_specs = [mine(sh) for sh in shapes] + ([] if pieces else [whole(sh) for sh in shapes])
    out_shape = [_sds((slots,) + sh, dt) for sh, dt in zip(shapes, dtypes)]
    out_shape += [] if pieces else [_sds(sh, dt) for sh, dt in zip(shapes, dtypes)]
    x, y, c = _place()
    own = 0 * x if chips else 4 * x + 2 * y + c
    out = pl.pallas_call(
        body, name=name,
        grid_spec=pltpu.PrefetchScalarGridSpec(num_scalar_prefetch=1, grid=(1,), in_specs=in_specs + [ANY_SPEC] * len(extra),
                                               out_specs=out_specs),
        out_shape=out_shape, compiler_params=_params("arbitrary"),
    )(own.astype(jnp.int32).reshape(1), *srcs, *extra)
    return out[:n], (list(srcs) if pieces else out[n:])


def _exchange_start(srcs, zones, pieces, name, chips=False):
    n = len(srcs)

    def body(*refs):
        ins, zs = refs[:n], refs[n:2 * n]
        sems = refs[2 * n:4 * n]
        token = refs[-1]
        me, peers = _peers()
        x, y, c = _place()
        if chips and pieces:
            routes = [((x ^ fx, y ^ fy, c), j, j) for j, (fx, fy) in enumerate(CHIP_FLIPS) if j]
        elif chips:
            routes = [((x ^ fx, y ^ fy, c), None, me) for fx, fy in CHIP_FLIPS[1:]]
        else:
            routes = [(peer, _peer_index(peer) if pieces else None, me) for peer in peers]
        for peer, src_slot, dst_slot in routes:
            for a in range(n):
                pltpu.make_async_remote_copy(
                    src_ref=ins[a] if src_slot is None else ins[a].at[src_slot], dst_ref=zs[a].at[dst_slot],
                    send_sem=sems[2 * a], recv_sem=sems[2 * a + 1], device_id=peer, device_id_type=MESH_IDS).start()
        token[...] = jnp.zeros_like(token)

    hbm = lambda v: pltpu.with_memory_space_constraint(v, pltpu.HBM)
    out = pl.pallas_call(
        body, name=name,
        out_shape=tuple([pltpu.SemaphoreType.DMA(())] * (2 * n) + [pltpu.HBM(v.shape, v.dtype) for v in srcs]
                        + [pltpu.HBM(z.shape, z.dtype) for z in zones] + [_sds((8, LANES))]),
        in_specs=[HBM_SPEC] * (2 * n), out_specs=tuple([SEM_SPEC] * (2 * n) + [HBM_SPEC] * (2 * n) + [VMEM_SPEC]),
        input_output_aliases={i: 2 * n + i for i in range(2 * n)},
        compiler_params=pltpu.CompilerParams(has_side_effects=DATAFLOW),
    )(*[hbm(v) for v in srcs], *[hbm(z) for z in zones])
    return out[:2 * n], out[2 * n:3 * n], out[3 * n:4 * n], out[-1]


def _relay_start(zones, name):
    n = len(zones)

    def body(*refs):
        zs, sems, token = refs[:n], refs[n:3 * n], refs[-1]
        x, y, c = _place()
        for fx, fy in CHIP_FLIPS:
            slot = 4 * (x ^ fx) + 2 * (y ^ fy) + c
            for a in range(n):
                pltpu.make_async_remote_copy(
                    src_ref=zs[a].at[slot], dst_ref=zs[a].at[slot], send_sem=sems[2 * a], recv_sem=sems[2 * a + 1],
                    device_id=(x, y, 1 - c), device_id_type=MESH_IDS).start()
        token[...] = jnp.zeros_like(token)

    out = pl.pallas_call(
        body, name=name,
        out_shape=tuple([pltpu.SemaphoreType.DMA(())] * (2 * n) + [pltpu.HBM(z.shape, z.dtype) for z in zones]
                        + [_sds((8, LANES))]),
        in_specs=[HBM_SPEC] * n, out_specs=tuple([SEM_SPEC] * (2 * n) + [HBM_SPEC] * n + [VMEM_SPEC]),
        input_output_aliases={i: 2 * n + i for i in range(n)},
        compiler_params=pltpu.CompilerParams(has_side_effects=DATAFLOW),
    )(*[pltpu.with_memory_space_constraint(z, pltpu.HBM) for z in zones])
    return out[:2 * n], [], out[2 * n:3 * n], out[-1]


def _exchange_wait(sems, srcs, zones, after, name, chips=False, n_copies=None):
    n, n_src = len(zones), len(srcs)
    after = list(after) if isinstance(after, (list, tuple)) else [after]
    n_copies = n_copies or (len(CHIP_FLIPS) - 1 if chips else N_DEV - 1)

    def body(*refs):
        zs, sm = refs[n_src:n_src + n], refs[n_src + n:n_src + 3 * n]
        me, peers = _peers()
        for a in range(n):
            seven = zs[a].at[pl.ds(0, n_copies)]
            cp = pltpu.make_async_remote_copy(src_ref=seven, dst_ref=seven, send_sem=sm[2 * a], recv_sem=sm[2 * a + 1],
                                              device_id=peers[0], device_id_type=MESH_IDS)
            cp.wait_send()
            cp.wait_recv()

    out = pl.pallas_call(
        body, name=name, out_shape=tuple([pltpu.HBM(v.shape, v.dtype) for v in srcs] + [pltpu.HBM(z.shape, z.dtype) for z in zones]),
        in_specs=[HBM_SPEC] * (n_src + n) + [SEM_SPEC] * (2 * n) + [ANY_SPEC] * len(after),
        out_specs=tuple([HBM_SPEC] * (n_src + n)), input_output_aliases={i: i for i in range(n_src + n)},
        compiler_params=pltpu.CompilerParams(has_side_effects=DATAFLOW),
    )(*srcs, *zones, *sems, *after)
    return out[n_src:]


def _sum_adamw(zone, w, m, v, name):
    n_slots, r, c_ = zone.shape
    rb = next((b for b in (256, 128) if r % b == 0), r)

    def body(z_ref, w_ref, m_ref, v_ref, grad_ref, delta_ref, nm_ref, nv_ref):
        total = z_ref[0].astype(F32)
        for d in range(1, n_slots):
            total = total + z_ref[d].astype(F32)
        grad_ref[...] = total
        delta_ref[...], nm_ref[...], nv_ref[...] = _adamw(w_ref[...], total, m_ref[...], v_ref[...])

    blk = pl.BlockSpec((rb, c_), lambda i: (i, 0))
    return pl.pallas_call(
        body, name=name, grid=(r // rb,), in_specs=[pl.BlockSpec((n_slots, rb, c_), lambda i: (0, i, 0)), blk, blk, blk],
        out_specs=[blk] * 4, out_shape=[_sds((r, c_))] * 4, compiler_params=_params("parallel"),
    )(zone, w, m, v)


SMALL_NORMS = ("pre_mix_norm", "post_mix_norm", "pre_mlp_norm", "post_mlp_norm")
SMALL_ORDER = SMALL_NORMS + ("fox_out_norm", "gdn_out_norm", "fox_f_bias", "gdn_a_log", "gdn_dt_bias", "gdn_conv_w")
CONV_SLAB_ROWS, CONV_SLAB_LANES = 8, 256


def _small_pack(small):
    def body(n0, n1, n2, n3, fnw_ref, gnw_ref, loss_ref, vec_ref, out_ref):
        out_ref[...] = jnp.zeros_like(out_ref)
        for i, ref in enumerate((n0, n1, n2, n3)):
            out_ref[i:i + 1, :] = ref[...]
        out_ref[4:5, 0:LANES] = fnw_ref[...]
        out_ref[4:5, LANES:2 * LANES] = gnw_ref[...]
        out_ref[4:5, 2 * LANES:3 * LANES] = loss_ref[...]
        out_ref[5:8, 0:LANES] = vec_ref[0:3, :]

    return pl.pallas_call(body, name="small_pack", in_specs=[VMEM_SPEC] * 8, out_specs=VMEM_SPEC,
                          out_shape=_sds((8, D_MODEL)))(*small["norms"], small["fox_out_norm"], small["gdn_out_norm"],
                                                        small["loss"], small["vectors"])


def _conv_slabs(dconv):
    blocks = dconv.reshape(CONV_K, N_DEV, -1).transpose(1, 0, 2)
    blocks = jnp.pad(blocks, ((0, 0), (0, CONV_SLAB_ROWS - CONV_K), (0, CONV_SLAB_LANES - blocks.shape[2])))
    return blocks.reshape(N_DEV * CONV_SLAB_ROWS, CONV_SLAB_LANES)


def _small_update(zone, conv_zone, w, m, v):
    n = len(SMALL_ORDER)
    n_conv = w["gdn_conv_w"].shape[1]

    def body(me_ref, z_ref, zc_ref, *refs):
        params, loss_ref, outs, (tot, totc) = refs[:3 * n], refs[3 * n], refs[3 * n + 1:7 * n + 1], refs[-2:]
        total, total_c = z_ref[0], zc_ref[0]
        for d in range(1, N_DEV):
            total, total_c = total + z_ref[d], total_c + zc_ref[d]
        tot[...] = total
        totc[...] = total_c
        loss_ref[...] = tot[4, 2 * LANES:2 * LANES + 1]
        mine = totc[pl.ds(pl.multiple_of(me_ref[0] * CONV_SLAB_ROWS, CONV_SLAB_ROWS), CONV_SLAB_ROWS), :]
        g = dict(zip(SMALL_NORMS, (tot[0], tot[1], tot[2], tot[3])))
        g.update(fox_out_norm=tot[4, 0:FOX_HEAD_DIM], gdn_out_norm=tot[4, LANES:LANES + GDN_HEAD_DIM],
                 fox_f_bias=tot[5, SM_FF:SM_FF + N_FOX_HEADS], gdn_a_log=tot[6, SM_GA:SM_GA + N_GDN_HEADS],
                 gdn_dt_bias=tot[7, SM_GA:SM_GA + N_GDN_HEADS], gdn_conv_w=mine[0:CONV_K, 0:n_conv])
        for i, name in enumerate(SMALL_ORDER):
            w_ref, m_ref, v_ref = params[3 * i:3 * i + 3]
            outs[4 * i][...] = g[name]
            outs[4 * i + 1][...], outs[4 * i + 2][...], outs[4 * i + 3][...] = _adamw(w_ref[...], g[name], m_ref[...],
                                                                                     v_ref[...])

    x, y, c = _place()
    operands = [a[name] for name in SMALL_ORDER for a in (w, m, v)]
    out = pl.pallas_call(
        body, name="small_update",
        in_specs=[pl.BlockSpec(memory_space=pltpu.SMEM)] + [VMEM_SPEC] * (2 + 3 * n), out_specs=[VMEM_SPEC] * (1 + 4 * n),
        out_shape=[_sds((1,))] + [_sds(w[name].shape) for name in SMALL_ORDER for _ in range(4)],
        scratch_shapes=[pltpu.VMEM(zone.shape[1:], F32), pltpu.VMEM(conv_zone.shape[1:], F32)],
    )((4 * x + 2 * y + c).astype(jnp.int32).reshape(1), zone, conv_zone, *operands)
    return out[0][0], {name: out[1 + 4 * i:5 + 4 * i] for i, name in enumerate(SMALL_ORDER)}


def _native_rows():
    groups = []
    for first, n_groups in ((0, N_FOX_HEADS // 2), (D_FOX * 3 + N_FOX_HEADS, N_GDN_HEADS)):
        for g in range(n_groups):
            groups += [(first + part * n_groups * LANES + g * LANES, first + part * n_groups * LANES + (g + 1) * LANES)
                       for part in range(3)]
    return tuple(groups) + ((3088, 3600), (1536, 1544), (3080, 3088))


NATIVE_ROWS = _native_rows()


W_IN_PIECE = D_PROJ // N_DEV
WGRAD_IN_ROWS = 512
SHUFFLE_LANES = 256


def _to_aligned_moves():
    moves, o = [], 0
    for lo, hi in NATIVE_ROWS:
        r = lo
        while r < hi:
            d = r // W_IN_PIECE
            k = min(hi, (d + 1) * W_IN_PIECE) - r
            moves.append((0, d, r - d * W_IN_PIECE, 0, o, k))
            r, o = r + k, o + k
    return moves


def _from_aligned_moves():
    moves = []
    for _, d, a, _, o, k in _to_aligned_moves():
        while k:
            n = min(k, WGRAD_IN_ROWS - o % WGRAD_IN_ROWS) if o < COL_SMALL else k
            moves.append((0, o // WGRAD_IN_ROWS, o % WGRAD_IN_ROWS, d, a, n) if o < COL_SMALL else
                         (1, 0, o - COL_SMALL, d, a, n))
            o, a, k = o + n, a + n, k - n
    return moves


def _shuffle_rows(srcs, moves, out_shape, name):
    c = srcs[0].shape[-1]

    def body(*refs):
        s_refs, o_ref, s_f, o_f = refs[:len(srcs)], refs[len(srcs)], refs[len(srcs) + 1:-1], refs[-1]
        for s_ref, f in zip(s_refs, s_f):
            f[...] = s_ref[...].astype(F32)
        o_f[...] = jnp.zeros_like(o_f)
        for i, ss, so, ds, do, k in moves:
            o_f[ds, pl.ds(do, k), :] = s_f[i][ss, pl.ds(so, k), :]
        o_ref[...] = o_f[...].astype(BF)

    blk = lambda shape: pl.BlockSpec(tuple(shape[:-1]) + (SHUFFLE_LANES,), lambda j: (0, 0, j))
    scratch = lambda shape: pltpu.VMEM(tuple(shape[:-1]) + (SHUFFLE_LANES,), F32)
    return pl.pallas_call(
        body, name=name, grid=(c // SHUFFLE_LANES,), in_specs=[blk(s.shape) for s in srcs], out_specs=blk(out_shape),
        out_shape=_sds(out_shape, BF), scratch_shapes=[scratch(s.shape) for s in srcs] + [scratch(out_shape)],
        compiler_params=_params("parallel"),
    )(*srcs)


def _cols_from_pieces(p):
    return p.transpose(1, 0, 2).reshape(p.shape[1], -1)


WEIGHT_ORDER = ("pre_mix_norm", "w_in", "fox_f_bias", "fox_out_norm", "gdn_conv_w", "gdn_a_log", "gdn_dt_bias",
                "gdn_out_norm", "w_out", "post_mix_norm", "pre_mlp_norm", "w_up", "w_down", "post_mlp_norm")


def kernel(x, pre_mix_norm, w_in, fox_f_bias, fox_out_norm, gdn_conv_w, gdn_a_log, gdn_dt_bias, gdn_out_norm, w_out, post_mix_norm, pre_mlp_norm, w_up, w_down, post_mlp_norm, loss_target, m_pre_mix_norm, m_w_in, m_fox_f_bias, m_fox_out_norm, m_gdn_conv_w, m_gdn_a_log, m_gdn_dt_bias, m_gdn_out_norm, m_w_out, m_post_mix_norm, m_pre_mlp_norm, m_w_up, m_w_down, m_post_mlp_norm, v_pre_mix_norm, v_w_in, v_fox_f_bias, v_fox_out_norm, v_gdn_conv_w, v_gdn_a_log, v_gdn_dt_bias, v_gdn_out_norm, v_w_out, v_post_mix_norm, v_pre_mlp_norm, v_w_up, v_w_down, v_post_mlp_norm):
    w = dict(pre_mix_norm=pre_mix_norm, w_in=w_in, fox_f_bias=fox_f_bias, fox_out_norm=fox_out_norm,
             gdn_conv_w=gdn_conv_w, gdn_a_log=gdn_a_log, gdn_dt_bias=gdn_dt_bias, gdn_out_norm=gdn_out_norm, w_out=w_out,
             post_mix_norm=post_mix_norm, pre_mlp_norm=pre_mlp_norm, w_up=w_up, w_down=w_down, post_mlp_norm=post_mlp_norm)
    mom = dict(pre_mix_norm=m_pre_mix_norm, w_in=m_w_in, fox_f_bias=m_fox_f_bias, fox_out_norm=m_fox_out_norm,
               gdn_conv_w=m_gdn_conv_w, gdn_a_log=m_gdn_a_log, gdn_dt_bias=m_gdn_dt_bias, gdn_out_norm=m_gdn_out_norm,
               w_out=m_w_out, post_mix_norm=m_post_mix_norm, pre_mlp_norm=m_pre_mlp_norm, w_up=m_w_up, w_down=m_w_down,
               post_mlp_norm=m_post_mlp_norm)
    var = dict(pre_mix_norm=v_pre_mix_norm, w_in=v_w_in, fox_f_bias=v_fox_f_bias, fox_out_norm=v_fox_out_norm,
               gdn_conv_w=v_gdn_conv_w, gdn_a_log=v_gdn_a_log, gdn_dt_bias=v_gdn_dt_bias, gdn_out_norm=v_gdn_out_norm,
               w_out=v_w_out, post_mix_norm=v_post_mix_norm, pre_mlp_norm=v_pre_mlp_norm, w_up=v_w_up, w_down=v_w_down,
               post_mlp_norm=v_post_mlp_norm)

    win_g, conv_g = _all_gather([w_in.T.astype(BF), gdn_conv_w])
    wt_al = _shuffle_rows([win_g], _to_aligned_moves(), (1, PROJ_W, D_MODEL), "w_in_to_aligned")[0]
    convw = _cols_from_pieces(conv_g)
    gathers, after = {}, win_g
    for name, shards in (("w_out", [w_out]), ("mlp", [w_up.T, w_down])):
        zones, shards = _zones_with_own(shards, False, "gather_" + name + "_own", after=after, dtype=BF)
        gathers[name] = _exchange_start(shards, zones, False, "gather_" + name + "_start", chips=name == "mlp")
        after = gathers[name][3]

    def late_weights(name, after):
        if name == "mlp_relay":
            sems, shards, zones, _ = gathers["mlp"]
            zones = _exchange_wait(sems, shards, zones, after, "gather_mlp_wait", chips=True)
            gathers["mlp"] = _relay_start(zones, "gather_mlp_relay")
            return gathers["mlp"][3]
        sems, shards, zones, _ = gathers[name]
        got = _exchange_wait(sems, shards, zones, after, "gather_" + name + "_done",
                             n_copies=len(CHIP_FLIPS) if name == "mlp" else None)
        if name == "w_out":
            return got[0].reshape(D_MODEL, D_MODEL)
        return got[0].reshape(D_FF, D_MODEL), got[1].reshape(D_FF, D_MODEL)

    scatters = {}

    def on_grads(name, g):
        chips = name == "w_in"
        if name == "w_in":
            g = _shuffle_rows(list(g), _from_aligned_moves(), (N_DEV, W_IN_PIECE, D_MODEL), "w_in_grad_from_aligned")
            g = _pair_reduce(g, "pair_reduce_w_in")
        srcs = list(g) if name == "mlp" else [g]
        zones, _ = _zones_with_own(srcs, True, "scatter_" + name + "_own", chips=chips)
        scatters[name] = _exchange_start(srcs, zones, True, "scatter_" + name + "_start", chips=chips)
        return scatters[name][3]

    grad_x, small = _local_step(
        x[0], loss_target[0], wt_al, after, late_weights, on_grads, convw, pre_mix_norm,
        fox_f_bias, fox_out_norm, gdn_a_log, gdn_dt_bias, gdn_out_norm, post_mix_norm, pre_mlp_norm, post_mlp_norm)
    slabs = [_small_pack(small), _conv_slabs(jnp.concatenate(small["conv"], axis=1))]
    zones, slabs = _zones_with_own(slabs, False, "small_own")
    scatters["small"] = _exchange_start(slabs, zones, False, "small_start")

    grads, delta, new_m, new_v = {}, {}, {}, {}
    after = scatters["small"][3]
    for name, members in (("mlp", ("w_up", "w_down")), ("w_out", ("w_out",)), ("small", ()), ("w_in", ("w_in",))):
        sems, srcs, zones, _ = scatters[name]
        zones = _exchange_wait(sems, srcs, zones, after, "scatter_" + name + "_wait", chips=name == "w_in")
        if name == "small":
            loss, updated = _small_update(zones[0], zones[1], w, mom, var)
            for n, res in updated.items():
                grads[n], delta[n], new_m[n], new_v[n] = res
            after = grads["pre_mix_norm"]
        for n, zone in zip(members, zones):
            if n == "w_in":
                res = _sum_adamw(zone, w[n].T, mom[n].T, var[n].T, "adamw_" + n)
                grads[n], delta[n], new_m[n], new_v[n] = [r.T for r in res]
            else:
                grads[n], delta[n], new_m[n], new_v[n] = _sum_adamw(zone, w[n], mom[n], var[n], "adamw_" + n)
        if members:
            after = [grads[n] for n in members]

    return (loss, grad_x[None], *[grads[n] for n in WEIGHT_ORDER], *[delta[n] for n in WEIGHT_ORDER],
            *[new_m[n] for n in WEIGHT_ORDER], *[new_v[n] for n in WEIGHT_ORDER])
```

```python
import jax
import jax.numpy as jnp
from jax import lax
from jax.experimental import pallas as pl
from jax.experimental.pallas import tpu as pltpu

F32 = jnp.float32
BF = jnp.bfloat16

D_MODEL = 1024
N_FOX_HEADS, FOX_HEAD_DIM = 8, 64
N_GDN_HEADS, GDN_HEAD_DIM = 4, 128
D_FOX = N_FOX_HEADS * FOX_HEAD_DIM
D_GDN = N_GDN_HEADS * GDN_HEAD_DIM
CHUNK = 64
CONV_K = 4
D_FF = 4 * D_MODEL
EPS = 1e-6
D_PROJ = 3600
N_DEV = 8

PROJ_W = 3712
COL_FOX, COL_GDN, COL_GZ, COL_SMALL = 0, 1536, 3072, 3584
LANES = 128
QKV = 3 * LANES
SM_FF, SM_GB, SM_GA = 0, 8, 12

ADAM_LR, ADAM_B1, ADAM_B2, ADAM_EPS, ADAM_WD, ADAM_STEP = 0.001, 0.9, 0.999, 1e-08, 0.01, 10

TOKEN_BLOCK = 256
MATMUL_BLOCK = 512
FOX_SCALE = FOX_HEAD_DIM ** -0.5
GDN_QSCALE = GDN_HEAD_DIM ** -0.5
NEG_BIG = -1e30
VMEM_LIMIT = 56 * 1024 * 1024

VMEM_SPEC = pl.BlockSpec(memory_space=pltpu.VMEM)
ANY_SPEC = pl.BlockSpec(memory_space=pl.ANY)


def _sds(shape, dtype=F32):
    return jax.ShapeDtypeStruct(shape, dtype)


def _params(*sem):
    return pltpu.CompilerParams(dimension_semantics=sem if sem else None, vmem_limit_bytes=VMEM_LIMIT)


def _ordered(body):
    def ordered(_, *refs):
        body(*refs)

    return ordered


def _mm(a, b):
    return jnp.dot(a.astype(BF), b.astype(BF), preferred_element_type=F32)


def _mm_nt(a, b):
    return lax.dot_general(a.astype(BF), b.astype(BF), (((1,), (1,)), ((), ())), preferred_element_type=F32)


def _mm_tn(a, b):
    return lax.dot_general(a.astype(BF), b.astype(BF), (((0,), (0,)), ((), ())), preferred_element_type=F32)


def _sigmoid(x):
    return 1.0 / (1.0 + jnp.exp(-x))


def _softplus(x):
    return jnp.maximum(x, 0.0) + jnp.log1p(jnp.exp(-jnp.abs(x)))


def _iota(shape, dim):
    return lax.broadcasted_iota(jnp.int32, shape, dim)


SUBLANES = 8


def _shift_down(x, s, row):
    y = pltpu.roll(x, s, 0)
    edge = -(-s // SUBLANES) * SUBLANES
    head = jnp.zeros_like(y[:edge]) if s == edge else jnp.where(row[:edge] >= s, y[:edge], 0.0)
    return head if edge == x.shape[0] else jnp.concatenate([head, y[edge:]], axis=0)


def _shift_up(x, s, row):
    n = x.shape[0]
    y = pltpu.roll(x, n - s, 0)
    edge = n - -(-s // SUBLANES) * SUBLANES
    tail = jnp.zeros_like(y[edge:]) if n - s == edge else jnp.where(row[edge:] < n - s, y[edge:], 0.0)
    return tail if edge == 0 else jnp.concatenate([y[:edge], tail], axis=0)


def _norm_proj(x, nw, wt_al, after):
    t = x.shape[0]

    def body(x_ref, nw_ref, w_ref, proj_ref, h_ref):
        xv = x_ref[...]
        r = lax.rsqrt(jnp.mean(xv * xv, axis=-1, keepdims=True) + EPS)
        h = (xv * r * nw_ref[...]).astype(BF)
        h_ref[...] = h
        proj_ref[...] = lax.dot_general(h, w_ref[...], (((1,), (1,)), ((), ())), preferred_element_type=F32)

    tm = min(MATMUL_BLOCK, t)
    return pl.pallas_call(
        _ordered(body), name="norm_proj", grid=(t // tm,),
        in_specs=[ANY_SPEC, pl.BlockSpec((tm, D_MODEL), lambda i: (i, 0)), pl.BlockSpec((1, D_MODEL), lambda i: (0, 0)),
                  pl.BlockSpec((PROJ_W, D_MODEL), lambda i: (0, 0))],
        out_specs=[pl.BlockSpec((tm, PROJ_W), lambda i: (i, 0)), pl.BlockSpec((tm, D_MODEL), lambda i: (i, 0))],
        out_shape=[_sds((t, PROJ_W)), _sds((t, D_MODEL), BF)],
        compiler_params=_params("parallel"),
    )(after, x, nw, wt_al)


def _lane_column(x, lane):
    return jnp.sum(jnp.where(_iota((1, LANES), 1) == lane, x, 0.0), axis=-1, keepdims=True)


def _small_prep(proj, fb, al, dtb):
    t = proj.shape[0]

    def body(sm_ref, fb_ref, al_ref, dtb_ref, cumt_ref, beta_ref, g_ref):
        s = sm_ref[...]
        z = s + fb_ref[...]
        cum = jnp.minimum(z, 0.0) - jnp.log1p(jnp.exp(-jnp.abs(z)))
        row = _iota((t, LANES), 0)
        step = 1
        while step < t:
            cum = cum + _shift_down(cum, step, row)
            step *= 2
        cumt_ref[...] = cum.T
        beta_ref[...] = _sigmoid(s)
        g_ref[...] = -jnp.exp(al_ref[...]) * _softplus(s + dtb_ref[...])

    vec = pl.BlockSpec((1, LANES), lambda i: (0, 0))
    tok = pl.BlockSpec((t, LANES), lambda i: (0, 0))
    return pl.pallas_call(
        body, name="small_prep", grid=(1,),
        in_specs=[pl.BlockSpec((t, LANES), lambda i: (0, COL_SMALL // LANES)), vec, vec, vec],
        out_specs=[pl.BlockSpec((LANES, t), lambda i: (0, 0)), tok, tok],
        out_shape=[_sds((LANES, t)), _sds((t, LANES)), _sds((t, LANES))],
        compiler_params=_params("arbitrary"),
    )(proj, fb, al, dtb)


def _fox_stack(x, first):
    return jnp.concatenate([jnp.where(first, x, 0.0), jnp.where(first, 0.0, x)], axis=0).astype(BF)


def _fox_unstack(y, first):
    n = y.shape[0] // 2
    return jnp.where(first, y[:n], y[n:])


def _fox_logits(q2_i, kb, cumt_ref, pair, i, tq):
    klen = (i + 1) * tq
    s = lax.dot_general(q2_i, kb[:klen], (((1,), (1,)), ((), ())), preferred_element_type=F32)
    upper = _iota((2 * tq, 1), 0) < tq
    s = s - jnp.where(upper, cumt_ref[pl.ds(2 * pair, 1), 0:klen], cumt_ref[pl.ds(2 * pair + 1, 1), 0:klen])
    causal = _iota((2 * tq, tq), 1) <= _iota((2 * tq, tq), 0) % tq
    parts = [(s[:, :klen - tq], 0, klen - tq)] if i else []
    return parts + [(jnp.where(causal, s[:, klen - tq:], NEG_BIG), klen - tq, klen)]


def _fox_fwd(proj, cumt, fnw):
    t = proj.shape[0]
    tq = min(TOKEN_BLOCK, t // 2)
    nq = t // tq

    def body(q_ref, k_ref, v_ref, cumt_ref, fnw_ref, o_ref, lse_ref, fn_ref):
        j = pl.program_id(0)
        first = _iota((1, LANES), 1) < FOX_HEAD_DIM
        kb = k_ref[...].astype(BF)
        vb = v_ref[...].astype(BF)
        for i in range(nq):
            rows = slice(i * tq, (i + 1) * tq)
            q2 = _fox_stack(q_ref[rows, :] * FOX_SCALE, first)
            parts = _fox_logits(q2, kb, cumt_ref, j, i, tq)
            m = jnp.max(parts[-1][0], axis=-1, keepdims=True)
            if i:
                m = jnp.maximum(m, jnp.max(parts[0][0], axis=-1, keepdims=True))
            l = jnp.zeros((2 * tq, 1), F32)
            o = jnp.zeros((2 * tq, LANES), F32)
            for s, lo, hi in parts:
                p = jnp.exp(s - m)
                l = l + jnp.sum(p, axis=-1, keepdims=True)
                o = o + jnp.dot(p.astype(BF), vb[lo:hi], preferred_element_type=F32)
            o_acc = _fox_unstack(o / l, first)
            lse_acc = _fox_unstack(jnp.broadcast_to(m + jnp.log(l), (2 * tq, LANES)), first)
            o_ref[rows, :] = o_acc
            lse_ref[rows, :] = lse_acc
            o2 = o_acc * o_acc
            s0 = jnp.sum(jnp.where(first, o2, 0.0), axis=-1, keepdims=True)
            s1 = jnp.sum(jnp.where(first, 0.0, o2), axis=-1, keepdims=True)
            r = lax.rsqrt(jnp.where(first, s0, s1) * (1.0 / FOX_HEAD_DIM) + EPS)
            fn_ref[rows, :] = (o_acc * r * fnw_ref[...]).astype(BF)

    qkv = lambda k: pl.BlockSpec((t, LANES), lambda j: (0, COL_FOX // LANES + 3 * j + k))
    pair = pl.BlockSpec((t, LANES), lambda j: (0, j))
    return pl.pallas_call(
        body, name="fox_fwd", grid=(N_FOX_HEADS // 2,),
        in_specs=[qkv(0), qkv(1), qkv(2), pl.BlockSpec((LANES, t), lambda j: (0, 0)),
                  pl.BlockSpec((1, LANES), lambda j: (0, 0))],
        out_specs=[pair, pair, pair],
        out_shape=[_sds((t, D_FOX)), _sds((t, D_FOX)), _sds((t, D_FOX), BF)],
        compiler_params=_params("parallel"),
    )(proj, proj, proj, cumt, fnw)


def _fox_bwd(proj, cumt, lse, o, do, dproj):
    t = proj.shape[0]
    tq = min(TOKEN_BLOCK, t // 2)
    nq = t // tq

    def body(q_ref, k_ref, v_ref, cumt_ref, lse_ref, o_ref, do_ref, _, dqkv_ref, dcq_ref, dckt_ref, dk_s, dv_s):
        j = pl.program_id(0)

        @pl.when(j == 0)
        def _():
            dcq_ref[...] = jnp.zeros_like(dcq_ref)
            dckt_ref[...] = jnp.zeros_like(dckt_ref)

        lane = _iota((1, LANES), 1)

        first = _iota((1, LANES), 1) < FOX_HEAD_DIM
        kb = k_ref[...].astype(BF)
        vb = v_ref[...].astype(BF)
        dk_s[...] = jnp.zeros_like(dk_s)
        dv_s[...] = jnp.zeros_like(dv_s)
        for i in range(nq):
            rows = slice(i * tq, (i + 1) * tq)
            do_i = do_ref[rows, :]
            prod = do_i * o_ref[rows, :]
            lse_i = lse_ref[rows, :]
            q2 = _fox_stack(q_ref[rows, :] * FOX_SCALE, first)
            do2 = _fox_stack(do_i, first)
            delta = jnp.concatenate([jnp.sum(jnp.where(first, prod, 0.0), axis=-1, keepdims=True),
                                     jnp.sum(jnp.where(first, 0.0, prod), axis=-1, keepdims=True)], axis=0)
            lse2 = jnp.concatenate([lse_i[:, 0:1], lse_i[:, FOX_HEAD_DIM:FOX_HEAD_DIM + 1]], axis=0)
            dq2 = jnp.zeros((2 * tq, LANES), F32)
            dcq2 = jnp.zeros((2 * tq, 1), F32)
            for s, lo, hi in _fox_logits(q2, kb, cumt_ref, j, i, tq):
                p = jnp.exp(s - lse2)
                ds = p * (_mm_nt(do2, vb[lo:hi]) - delta)
                dsb = ds.astype(BF)
                dq2 = dq2 + jnp.dot(dsb, kb[lo:hi], preferred_element_type=F32)
                dk_s[lo:hi, :] += _mm_tn(dsb, q2)
                dv_s[lo:hi, :] += _mm_tn(p, do2)
                dcq2 = dcq2 + jnp.sum(ds, axis=-1, keepdims=True)
                dckt_ref[pl.ds(2 * j, 1), lo:hi] += jnp.sum(ds[:tq], axis=0, keepdims=True)
                dckt_ref[pl.ds(2 * j + 1, 1), lo:hi] += jnp.sum(ds[tq:], axis=0, keepdims=True)
            dqkv_ref[rows, 0:LANES] = (_fox_unstack(dq2, first) * FOX_SCALE).astype(BF)
            dcq_ref[rows, :] += jnp.where(lane == 2 * j, dcq2[:tq], jnp.where(lane == 2 * j + 1, dcq2[tq:], 0.0))
        dqkv_ref[:, LANES:2 * LANES] = dk_s[...].astype(BF)
        dqkv_ref[:, 2 * LANES:QKV] = dv_s[...].astype(BF)

    qkv = lambda k: pl.BlockSpec((t, LANES), lambda j: (0, COL_FOX // LANES + 3 * j + k))
    pair = pl.BlockSpec((t, LANES), lambda j: (0, j))
    rows128 = pl.BlockSpec((LANES, t), lambda j: (0, 0))
    return pl.pallas_call(
        body, name="fox_bwd", grid=(N_FOX_HEADS // 2,),
        in_specs=[qkv(0), qkv(1), qkv(2), rows128, pair, pair, pair, ANY_SPEC],
        out_specs=[pl.BlockSpec((t, QKV), lambda j: (0, COL_FOX // QKV + j)),
                   pl.BlockSpec((t, LANES), lambda j: (0, 0)), rows128],
        out_shape=[_sds(dproj.shape, BF), _sds((t, LANES)), _sds((LANES, t))],
        scratch_shapes=[pltpu.VMEM((t, LANES), F32), pltpu.VMEM((t, LANES), F32)],
        input_output_aliases={7: 0}, compiler_params=_params("arbitrary"),
    )(proj, proj, proj, cumt, lse, o, do, dproj)


def _conv(x, w, row):
    return (w[3:4, :] * x + w[2:3, :] * _shift_down(x, 1, row) + w[1:2, :] * _shift_down(x, 2, row)
            + w[0:1, :] * _shift_down(x, 3, row))


def _chunk_decay(gc_c):
    gi = gc_c[:, 0:CHUNK]
    gj = gc_c.T[0:CHUNK, :]
    ri = _iota((CHUNK, CHUNK), 0)
    cj = _iota((CHUNK, CHUNK), 1)
    return jnp.where(ri >= cj, jnp.exp(jnp.minimum(gi - gj, 0.0)), 0.0), ri > cj


def _gdn_specs(t):
    col = lambda off: pl.BlockSpec((t, LANES), lambda h: (0, off + h))
    cw = lambda off: pl.BlockSpec((CONV_K, LANES), lambda h: (0, off + h))
    mat = pl.BlockSpec((1, t // CHUNK, CHUNK, CHUNK), lambda h: (h, 0, 0, 0))
    qkv = lambda k: pl.BlockSpec((t, LANES), lambda h: (0, COL_GDN // LANES + 3 * h + k))
    return col, cw, mat, qkv


def _gdn_prep(proj, convw, beta, g):
    t = proj.shape[0]
    nch = t // CHUNK

    def body(xq_ref, xk_ref, xv_ref, wq_ref, wk_ref, wv_ref, beta_ref, g_ref,
             qn_ref, kn_ref, cv_ref, gc_ref, be_ref, m_ref, a_ref):
        row = _iota((t, LANES), 0)
        hd = pl.program_id(0)
        be_ref[...] = jnp.broadcast_to(_lane_column(beta_ref[...], SM_GB + hd), (t, LANES))

        def act(x_ref, w_ref):
            y = _conv(x_ref[...], w_ref[...], row)
            return y * _sigmoid(y)

        cq = act(xq_ref, wq_ref)
        ck = act(xk_ref, wk_ref)
        cv_ref[...] = act(xv_ref, wv_ref)
        qn_ref[...] = cq * lax.rsqrt(jnp.sum(cq * cq, axis=-1, keepdims=True) + EPS) * GDN_QSCALE
        kn_ref[...] = ck * lax.rsqrt(jnp.sum(ck * ck, axis=-1, keepdims=True) + EPS)
        gc = jnp.broadcast_to(_lane_column(g_ref[...], SM_GA + hd), (t, LANES))
        pos = row % CHUNK
        step = 1
        while step < CHUNK:
            gc = gc + jnp.where(pos >= step, pltpu.roll(gc, step, 0), 0.0)
            step *= 2
        gc_ref[...] = gc

        group = 4 if nch % 4 == 0 else 1

        def chunks(gi, carry):
            ns = [gi * group + c for c in range(group)]
            sls = [pl.ds(pl.multiple_of(n * CHUNK, CHUNK), CHUNK) for n in ns]
            ks = [kn_ref[sl, :] for sl in sls]
            kk = [_mm_nt(k_c * be_ref[sl, :], k_c) for k_c, sl in zip(ks, sls)]
            qk = [_mm_nt(qn_ref[sl, :], k_c) for k_c, sl in zip(ks, sls)]
            for c, n in enumerate(ns):
                decay, strict = _chunk_decay(gc_ref[sls[c], :])
                m_ref[0, n] = jnp.where(strict, kk[c] * decay, 0.0)
                a_ref[0, n] = qk[c] * decay
            return carry

        lax.fori_loop(0, nch // group, chunks, 0)

    col, cw, mat, qkv = _gdn_specs(t)
    return pl.pallas_call(
        body, name="gdn_prep", grid=(N_GDN_HEADS,),
        in_specs=[qkv(0), qkv(1), qkv(2), cw(0), cw(4), cw(8)] + [pl.BlockSpec((t, LANES), lambda h: (0, 0))] * 2,
        out_specs=[col(0), col(0), col(0), col(0), col(0), mat, mat],
        out_shape=[_sds((t, D_GDN))] * 5 + [_sds((N_GDN_HEADS, nch, CHUNK, CHUNK))] * 2,
        compiler_params=_params("parallel"),
    )(proj, proj, proj, convw, convw, convw, beta, g)


def _tri_inverse(m3):
    assert m3.shape == (LANES, CHUNK, CHUNK)

    def body(m_ref, t_ref, ms, ts):
        for i in range(CHUNK):
            ms[i * CHUNK:(i + 1) * CHUNK, :] = m_ref[:, i, :].T
        cidx = _iota((CHUNK, LANES), 0)

        def outer(i, carry):
            def inner(jj, acc):
                mrow = ms[pl.ds(i * CHUNK + jj, 1), :]
                return acc - mrow * ts[pl.ds(pl.multiple_of(jj * CHUNK, CHUNK), CHUNK), :]

            acc = lax.fori_loop(0, i, inner, jnp.where(cidx == i, 1.0, 0.0).astype(F32))
            ts[pl.ds(pl.multiple_of(i * CHUNK, CHUNK), CHUNK), :] = acc
            return carry

        lax.fori_loop(0, CHUNK, outer, 0)
        for i in range(CHUNK):
            t_ref[:, i, :] = ts[i * CHUNK:(i + 1) * CHUNK, :].T

    return pl.pallas_call(
        body, name="tri_inverse", in_specs=[VMEM_SPEC], out_specs=VMEM_SPEC,
        out_shape=_sds((LANES, CHUNK, CHUNK)),
        scratch_shapes=[pltpu.VMEM((CHUNK * CHUNK, LANES), F32), pltpu.VMEM((CHUNK * CHUNK, LANES), F32)],
        compiler_params=_params(),
    )(m3)


def _gdn_chunk_terms(q, k, v, b, gcc):
    eg = jnp.exp(gcc)
    last = gcc[CHUNK - 1:CHUNK, :]
    egl = jnp.exp(last - gcc)
    gl = jnp.exp(last)
    kb = k * b
    return eg, egl, gl, kb, v * b, kb * eg, q * eg, k * egl


GDN_BLOCK_CHUNKS = 4


def _gdn_block_specs(t, reverse):
    cb = GDN_BLOCK_CHUNKS
    nb = t // (cb * CHUNK)
    idx = (lambda i: nb - 1 - i) if reverse else (lambda i: i)
    tok = pl.BlockSpec((cb * CHUNK, D_GDN), lambda i: (idx(i), 0))
    mat = pl.BlockSpec((N_GDN_HEADS, cb, CHUNK, CHUNK), lambda i: (0, idx(i), 0, 0))
    state = pl.BlockSpec((N_GDN_HEADS, cb, GDN_HEAD_DIM, GDN_HEAD_DIM), lambda i: (0, idx(i), 0, 0))
    return nb, tok, mat, state


def _gdn_scan(qn, kn, cv, be, gc, tinv, amat):
    t = qn.shape[0]
    nch = t // CHUNK

    def body(q_ref, k_ref, v_ref, b_ref, gc_ref, t_ref, a_ref, o_ref, sall_ref, vn_ref, s_scr):
        @pl.when(pl.program_id(0) == 0)
        def _():
            s_scr[...] = jnp.zeros_like(s_scr)

        heads = range(N_GDN_HEADS)
        cols = [slice(hd * LANES, (hd + 1) * LANES) for hd in heads]
        s = [s_scr[hd] for hd in heads]
        for cc in range(GDN_BLOCK_CHUNKS):
            rs = slice(cc * CHUNK, (cc + 1) * CHUNK)
            terms = [_gdn_chunk_terms(q_ref[rs, cs], k_ref[rs, cs], v_ref[rs, cs], b_ref[rs, cs], gc_ref[rs, cs])
                     for cs in cols]
            for hd in heads:
                sall_ref[hd, cc] = s[hd]
            uw = [_mm(t_ref[hd, cc], jnp.concatenate([terms[hd][4], terms[hd][5]], axis=1)) for hd in heads]
            ws_qs = [_mm(jnp.concatenate([uw[hd][:, LANES:], terms[hd][6]], axis=0), s[hd]) for hd in heads]
            vn = [uw[hd][:, :LANES] - ws_qs[hd][:CHUNK] for hd in heads]
            a_vn = [_mm(a_ref[hd, cc], vn[hd]) for hd in heads]
            kd_vn = [_mm_tn(terms[hd][7], vn[hd]) for hd in heads]
            for hd in heads:
                vn_ref[rs, cols[hd]] = vn[hd]
                o_ref[rs, cols[hd]] = ws_qs[hd][CHUNK:] + a_vn[hd]
                s[hd] = s[hd] * terms[hd][2] + kd_vn[hd]
        for hd in heads:
            s_scr[hd] = s[hd]

    nb, tok, mat, state = _gdn_block_specs(t, False)
    return pl.pallas_call(
        body, name="gdn_scan", grid=(nb,),
        in_specs=[tok] * 5 + [mat, mat], out_specs=[tok, state, tok],
        out_shape=[_sds((t, D_GDN)), _sds((N_GDN_HEADS, nch, GDN_HEAD_DIM, GDN_HEAD_DIM)), _sds((t, D_GDN))],
        scratch_shapes=[pltpu.VMEM((N_GDN_HEADS, GDN_HEAD_DIM, GDN_HEAD_DIM), F32)],
        compiler_params=_params("arbitrary"),
    )(qn, kn, cv, be, gc, tinv, amat)


def _gdn_bwd(qn, kn, cv, be, gc, tinv, amat, s_all, vn_all, do):
    t = qn.shape[0]

    def body(q_ref, k_ref, v_ref, b_ref, gc_ref, t_ref, a_ref, sall_ref, vn_ref, do_ref,
             dq_ref, dk_ref, dv_ref, db_ref, dg_ref, ds_scr):
        @pl.when(pl.program_id(0) == 0)
        def _():
            ds_scr[...] = jnp.zeros_like(ds_scr)

        lastrow = _iota((CHUNK, LANES), 0) == CHUNK - 1
        heads = range(N_GDN_HEADS)
        cols = [slice(hd * LANES, (hd + 1) * LANES) for hd in heads]
        each = lambda fn: [fn(hd) for hd in heads]
        rows_cat = lambda x, y: jnp.concatenate([x, y], axis=0)
        lane_cat = lambda x, y: jnp.concatenate([x, y], axis=1)
        dsp = each(lambda hd: ds_scr[hd])
        for cc in reversed(range(GDN_BLOCK_CHUNKS)):
            rs = slice(cc * CHUNK, (cc + 1) * CHUNK)
            q = each(lambda hd: q_ref[rs, cols[hd]])
            k = each(lambda hd: k_ref[rs, cols[hd]])
            v = each(lambda hd: v_ref[rs, cols[hd]])
            b = each(lambda hd: b_ref[rs, cols[hd]])
            gcc = each(lambda hd: gc_ref[rs, cols[hd]])
            do_c = each(lambda hd: do_ref[rs, cols[hd]])
            vn = each(lambda hd: vn_ref[rs, cols[hd]])
            tn = each(lambda hd: t_ref[hd, cc])
            st = each(lambda hd: sall_ref[hd, cc])
            terms = each(lambda hd: _gdn_chunk_terms(q[hd], k[hd], v[hd], b[hd], gcc[hd]))
            eg, egl, gl, kb, vb, kbg, qd, kd = [[terms[hd][i] for hd in heads] for i in range(8)]
            w = each(lambda hd: _mm(tn[hd], kbg[hd]))
            a_do = each(lambda hd: _mm_tn(a_ref[hd, cc], do_c[hd]))
            kd_ds = each(lambda hd: _mm(kd[hd], dsp[hd]))
            da = each(lambda hd: _mm_nt(do_c[hd], vn[hd]))
            dkd = each(lambda hd: _mm_nt(vn[hd], dsp[hd]))
            by_k = each(lambda hd: _mm_nt(rows_cat(kb[hd], q[hd]), k[hd]))
            dgl = each(lambda hd: jnp.sum(jnp.sum(dsp[hd] * st[hd], axis=-1, keepdims=True), axis=0, keepdims=True))
            dvn = each(lambda hd: a_do[hd] + kd_ds[hd])
            do_dvn = each(lambda hd: rows_cat(do_c[hd], dvn[hd]))
            by_s = each(lambda hd: _mm_nt(do_dvn[hd], st[hd]))
            dqd = each(lambda hd: by_s[hd][:CHUNK])
            dvn_dw = each(lambda hd: lane_cat(dvn[hd], -by_s[hd][CHUNK:]))
            dsp = each(lambda hd: _mm_tn(rows_cat(qd[hd], -w[hd]), do_dvn[hd]) + gl[hd] * dsp[hd])
            dt = each(lambda hd: _mm_nt(dvn_dw[hd], lane_cat(vb[hd], kbg[hd])))
            by_t = each(lambda hd: _mm_tn(tn[hd], dvn_dw[hd]))
            tt_dt = each(lambda hd: _mm_tn(tn[hd], dt[hd]))
            dm_raw = each(lambda hd: _mm_nt(tt_dt[hd], tn[hd]))
            masks = each(lambda hd: _chunk_decay(gcc[hd]))
            dkk = each(lambda hd: jnp.where(masks[hd][1], -dm_raw[hd], 0.0) * masks[hd][0])
            dqk = each(lambda hd: da[hd] * masks[hd][0])
            dqk_dkk = each(lambda hd: rows_cat(dqk[hd], dkk[hd]))
            on_k = each(lambda hd: _mm(dqk_dkk[hd], k[hd]))
            dk_mm = each(lambda hd: _mm_tn(dqk_dkk[hd], rows_cat(q[hd], kb[hd])))
            for hd in heads:
                cs = cols[hd]
                dvb, dkbg = by_t[hd][:, :LANES], by_t[hd][:, LANES:]
                gmat = dkk[hd] * by_k[hd][:CHUNK] + dqk[hd] * by_k[hd][CHUNK:]
                dq_ref[rs, cs] = dqd[hd] * eg[hd] + on_k[hd][:CHUNK]
                dkb = on_k[hd][CHUNK:] + dkbg * eg[hd]
                dk_ref[rs, cs] = dkd[hd] * egl[hd] + dk_mm[hd] + dkb * b[hd]
                db = jnp.sum(dkb * k[hd], axis=-1, keepdims=True) + jnp.sum(dvb * v[hd], axis=-1, keepdims=True)
                db_ref[rs, cs] = jnp.broadcast_to(db, (CHUNK, LANES))
                dv_ref[rs, cs] = dvb * b[hd]
                dkd_kd = jnp.sum(dkd[hd] * kd[hd], axis=-1, keepdims=True)
                col_sums = jnp.sum(lane_cat(gmat, jnp.zeros_like(gmat)).T, axis=-1, keepdims=True)
                dgc = (jnp.sum(gmat, axis=-1, keepdims=True) - col_sums[:CHUNK]
                       + jnp.sum(dqd[hd] * qd[hd], axis=-1, keepdims=True)
                       + jnp.sum(dkbg * kbg[hd], axis=-1, keepdims=True) - dkd_kd)
                extra = jnp.sum(dkd_kd, axis=0, keepdims=True) + dgl[hd] * gl[hd]
                dg_ref[rs, cs] = dgc + jnp.where(lastrow, extra, 0.0)
        for hd in heads:
            ds_scr[hd] = dsp[hd]
        dg = dg_ref[...]
        row = _iota(dg.shape, 0)
        pos = row % CHUNK
        step = 1
        while step < CHUNK:
            dg = dg + jnp.where(pos < CHUNK - step, pltpu.roll(dg, dg.shape[0] - step, 0), 0.0)
            step *= 2
        dg_ref[...] = dg

    nb, tok, mat, state = _gdn_block_specs(t, True)
    return pl.pallas_call(
        body, name="gdn_bwd", grid=(nb,),
        in_specs=[tok] * 5 + [mat, mat, state, tok, tok], out_specs=[tok] * 5, out_shape=[_sds((t, D_GDN))] * 5,
        scratch_shapes=[pltpu.VMEM((N_GDN_HEADS, GDN_HEAD_DIM, GDN_HEAD_DIM), F32)],
        compiler_params=_params("arbitrary"),
    )(qn, kn, cv, be, gc, tinv, amat, s_all, vn_all, do)


def _gdn_bwd_conv(proj, convw, dqn, dkn, dcv, dproj):
    t = proj.shape[0]

    def body(xq_ref, xk_ref, xv_ref, wq_ref, wk_ref, wv_ref, dq_ref, dk_ref, dv_ref, _,
             dqkv_ref, dwq_ref, dwk_ref, dwv_ref):
        row = _iota((t, LANES), 0)

        def one(x_ref, w_ref, d_ref, k, dw_ref, scale):
            x = x_ref[...]
            w = w_ref[...]
            y = _conv(x, w, row)
            sg = _sigmoid(y)
            dc = d_ref[...]
            if scale is not None:
                c = y * sg
                r = lax.rsqrt(jnp.sum(c * c, axis=-1, keepdims=True) + EPS)
                ch = c * r
                dc = scale * r * (dc - ch * jnp.sum(dc * ch, axis=-1, keepdims=True))
            dy = dc * sg * (1.0 + y * (1.0 - sg))
            dys = [dy] + [_shift_up(dy, s, row) for s in range(1, CONV_K)]
            dqkv_ref[:, k * LANES:(k + 1) * LANES] = (
                w[3:4, :] * dys[0] + w[2:3, :] * dys[1] + w[1:2, :] * dys[2] + w[0:1, :] * dys[3]).astype(BF)
            for jj in range(CONV_K):
                dw_ref[jj:jj + 1, :] = jnp.sum(dys[CONV_K - 1 - jj] * x, axis=0, keepdims=True)

        one(xq_ref, wq_ref, dq_ref, 0, dwq_ref, GDN_QSCALE)
        one(xk_ref, wk_ref, dk_ref, 1, dwk_ref, 1.0)
        one(xv_ref, wv_ref, dv_ref, 2, dwv_ref, None)

    col, cw, _, qkv = _gdn_specs(t)
    return pl.pallas_call(
        body, name="gdn_bwd_conv", grid=(N_GDN_HEADS,),
        in_specs=[qkv(0), qkv(1), qkv(2), cw(0), cw(4), cw(8), col(0), col(0), col(0), ANY_SPEC],
        out_specs=[pl.BlockSpec((t, QKV), lambda h: (0, COL_GDN // QKV + h)), cw(0), cw(0), cw(0)],
        out_shape=[_sds(dproj.shape, BF)] + [_sds((CONV_K, D_GDN))] * 3,
        input_output_aliases={9: 0}, compiler_params=_params("parallel"),
    )(proj, proj, proj, convw, convw, convw, dqn, dkn, dcv, dproj)


def _mix_out(fox_n, gdn_o, proj, gnw, w_out, x, pmw, plw, after):
    t = x.shape[0]
    tm = min(MATMUL_BLOCK, t)

    def body(fn_ref, go_ref, gz_ref, gnw_ref, w_ref, x_ref, pmw_ref, plw_ref, x1_ref, h2_ref, mixed_ref, omix_ref,
             h2t_ref):
        omix_ref[:, 0:D_FOX] = fn_ref[...]
        for hd in range(N_GDN_HEADS):
            cs = slice(hd * LANES, (hd + 1) * LANES)
            go = go_ref[:, cs]
            r = lax.rsqrt(jnp.mean(go * go, axis=-1, keepdims=True) + EPS)
            gz = gz_ref[:, cs]
            omix_ref[:, D_FOX + hd * LANES:D_FOX + (hd + 1) * LANES] = (
                go * r * gnw_ref[...] * (gz * _sigmoid(gz))).astype(BF)
        mixed = jnp.dot(omix_ref[...], w_ref[...], preferred_element_type=F32)
        mixed_ref[...] = mixed
        r2 = lax.rsqrt(jnp.mean(mixed * mixed, axis=-1, keepdims=True) + EPS)
        x1 = x_ref[...] + mixed * r2 * pmw_ref[...]
        x1_ref[...] = x1
        r3 = lax.rsqrt(jnp.mean(x1 * x1, axis=-1, keepdims=True) + EPS)
        h2 = x1 * r3 * plw_ref[...]
        h2_ref[...] = h2.astype(BF)
        h2t_ref[...] = h2.T.astype(BF)

    tok = lambda w: pl.BlockSpec((tm, w), lambda i: (i, 0))
    vec = lambda w: pl.BlockSpec((1, w), lambda i: (0, 0))
    return pl.pallas_call(
        _ordered(body), name="mix_out", grid=(t // tm,),
        in_specs=[ANY_SPEC, tok(D_FOX), tok(D_GDN), pl.BlockSpec((tm, D_GDN), lambda i: (i, COL_GZ // D_GDN)), vec(LANES),
                  pl.BlockSpec((D_MODEL, D_MODEL), lambda i: (0, 0)), tok(D_MODEL), vec(D_MODEL), vec(D_MODEL)],
        out_specs=[tok(D_MODEL)] * 4 + [pl.BlockSpec((D_MODEL, tm), lambda i: (0, i))],
        out_shape=[_sds((t, D_MODEL)), _sds((t, D_MODEL), BF), _sds((t, D_MODEL)), _sds((t, D_MODEL), BF),
                   _sds((D_MODEL, t), BF)],
        compiler_params=_params("parallel"),
    )(after, fox_n, gdn_o, proj, gnw, w_out, x, pmw, plw)


def _out_bwd(dmixed, w_out, o_fox, gdn_o, proj, fnw, gnw, after):
    t = dmixed.shape[0]
    tm = min(MATMUL_BLOCK, t)

    def body(dm_ref, w_ref, of_ref, go_ref, gz_ref, fnw_ref, gnw_ref, dof_ref, dgo_ref, dgz_ref, dfw_ref, dgw_ref):
        i = pl.program_id(0)

        @pl.when(i == 0)
        def _():
            dfw_ref[...] = jnp.zeros_like(dfw_ref)
            dgw_ref[...] = jnp.zeros_like(dgw_ref)

        domix = _mm_nt(dm_ref[...], w_ref[...])
        first = _iota((1, LANES), 1) < FOX_HEAD_DIM
        dfw = jnp.zeros((1, LANES), F32)
        dgw = jnp.zeros((1, LANES), F32)
        for pr in range(N_FOX_HEADS // 2):
            cs = slice(pr * LANES, (pr + 1) * LANES)
            o = of_ref[:, cs]
            dfn = domix[:, cs]
            o2 = o * o
            s0 = jnp.sum(jnp.where(first, o2, 0.0), axis=-1, keepdims=True)
            s1 = jnp.sum(jnp.where(first, 0.0, o2), axis=-1, keepdims=True)
            r = lax.rsqrt(jnp.where(first, s0, s1) * (1.0 / FOX_HEAD_DIM) + EPS)
            oh = o * r
            dfw = dfw + jnp.sum(dfn * oh, axis=0, keepdims=True)
            doh = dfn * fnw_ref[...]
            pr_ = doh * oh
            m0 = jnp.sum(jnp.where(first, pr_, 0.0), axis=-1, keepdims=True)
            m1 = jnp.sum(jnp.where(first, 0.0, pr_), axis=-1, keepdims=True)
            dof_ref[:, cs] = r * (doh - oh * jnp.where(first, m0, m1) * (1.0 / FOX_HEAD_DIM))
        for hd in range(N_GDN_HEADS):
            cs = slice(hd * LANES, (hd + 1) * LANES)
            go = go_ref[:, cs]
            gz = gz_ref[:, cs]
            dgated = domix[:, D_FOX + hd * LANES:D_FOX + (hd + 1) * LANES]
            r = lax.rsqrt(jnp.mean(go * go, axis=-1, keepdims=True) + EPS)
            goh = go * r
            sg = _sigmoid(gz)
            sz = gz * sg
            gn = goh * gnw_ref[...]
            dgn = dgated * sz
            dgz_ref[:, cs] = (dgated * gn * sg * (1.0 + gz * (1.0 - sg))).astype(BF)
            dgw = dgw + jnp.sum(dgn * goh, axis=0, keepdims=True)
            dgh = dgn * gnw_ref[...]
            dgo_ref[:, cs] = r * (dgh - goh * jnp.mean(dgh * goh, axis=-1, keepdims=True))
        dfw_ref[...] += dfw + pltpu.roll(dfw, FOX_HEAD_DIM, 1)
        dgw_ref[...] += dgw

    tok = lambda w: pl.BlockSpec((tm, w), lambda i: (i, 0))
    vec = lambda w: pl.BlockSpec((1, w), lambda i: (0, 0))
    return pl.pallas_call(
        _ordered(body), name="out_bwd", grid=(t // tm,),
        in_specs=[ANY_SPEC, tok(D_MODEL), pl.BlockSpec((D_MODEL, D_MODEL), lambda i: (0, 0)), tok(D_FOX), tok(D_GDN),
                  pl.BlockSpec((tm, D_GDN), lambda i: (i, COL_GZ // D_GDN)), vec(LANES), vec(LANES)],
        out_specs=[tok(D_FOX), tok(D_GDN), pl.BlockSpec((tm, D_GDN), lambda i: (i, COL_GZ // D_GDN)), vec(LANES),
                   vec(LANES)],
        out_shape=[_sds((t, D_FOX)), _sds((t, D_GDN)), _sds((t, PROJ_W), BF), _sds((1, LANES)), _sds((1, LANES))],
        compiler_params=_params("arbitrary"),
    )(after, dmixed, w_out, o_fox, gdn_o, proj, fnw, gnw)


def _mlp_up(h2, w_upt):
    t = h2.shape[0]
    tm = min(MATMUL_BLOCK, t)

    def body(h_ref, w_ref, up_ref):
        up_ref[...] = lax.dot_general(h_ref[...], w_ref[...], (((1,), (1,)), ((), ())),
                                      preferred_element_type=F32).astype(BF)

    return pl.pallas_call(
        body, name="mlp_up", grid=(t // tm,),
        in_specs=[pl.BlockSpec((tm, D_MODEL), lambda i: (i, 0)), pl.BlockSpec((D_FF, D_MODEL), lambda i: (0, 0))],
        out_specs=pl.BlockSpec((tm, D_FF), lambda i: (i, 0)), out_shape=_sds((t, D_FF), BF),
        compiler_params=_params("parallel"),
    )(h2, w_upt)


def _mlp_down_loss(up, w_down, x1, pw, target):
    t = up.shape[0]
    tm = min(MATMUL_BLOCK, t)

    def body(up_ref, w_ref, x1_ref, pw_ref, tg_ref, dy_ref, dx2_ref, loss_ref, dpw_ref):
        i = pl.program_id(0)

        @pl.when(i == 0)
        def _():
            loss_ref[...] = jnp.zeros_like(loss_ref)
            dpw_ref[...] = jnp.zeros_like(dpw_ref)

        u = jnp.maximum(up_ref[...].astype(F32), 0.0)
        y = jnp.dot((u * u).astype(BF), w_ref[...], preferred_element_type=F32)
        r = lax.rsqrt(jnp.mean(y * y, axis=-1, keepdims=True) + EPS)
        yh = y * r
        pw = pw_ref[...]
        err = x1_ref[...] + yh * pw - tg_ref[...]
        part = jnp.sum(jnp.sum(err * err, axis=-1, keepdims=True), axis=0, keepdims=True) * (0.5 / D_MODEL)
        loss_ref[...] += jnp.broadcast_to(part, loss_ref.shape)
        dx2 = err * (1.0 / D_MODEL)
        dx2_ref[...] = dx2
        dpw_ref[...] += jnp.sum(dx2 * yh, axis=0, keepdims=True)
        dyh = dx2 * pw
        dy_ref[...] = (r * (dyh - yh * jnp.mean(dyh * yh, axis=-1, keepdims=True))).astype(BF)

    tok = lambda w: pl.BlockSpec((tm, w), lambda i: (i, 0))
    vec = lambda w: pl.BlockSpec((1, w), lambda i: (0, 0))
    return pl.pallas_call(
        body, name="mlp_down_loss", grid=(t // tm,),
        in_specs=[tok(D_FF), pl.BlockSpec((D_FF, D_MODEL), lambda i: (0, 0)), tok(D_MODEL), vec(D_MODEL), tok(D_MODEL)],
        out_specs=[tok(D_MODEL), tok(D_MODEL), vec(LANES), vec(D_MODEL)],
        out_shape=[_sds((t, D_MODEL), BF), _sds((t, D_MODEL)), _sds((1, LANES)), _sds((1, D_MODEL))],
        compiler_params=_params("arbitrary"),
    )(up, w_down, x1, pw, target)


def _mlp_bwd_act(dy, w_down, up):
    t = dy.shape[0]
    tm = min(MATMUL_BLOCK, t)

    def body(dy_ref, w_ref, up_ref, dup_ref):
        da = lax.dot_general(dy_ref[...], w_ref[...], (((1,), (1,)), ((), ())), preferred_element_type=F32)
        dup_ref[...] = (da * (2.0 * jnp.maximum(up_ref[...].astype(F32), 0.0))).astype(BF)

    return pl.pallas_call(
        body, name="mlp_bwd_act", grid=(t // tm,),
        in_specs=[pl.BlockSpec((tm, D_MODEL), lambda i: (i, 0)), pl.BlockSpec((D_FF, D_MODEL), lambda i: (0, 0)),
                  pl.BlockSpec((tm, D_FF), lambda i: (i, 0))],
        out_specs=pl.BlockSpec((tm, D_FF), lambda i: (i, 0)), out_shape=_sds((t, D_FF), BF),
        compiler_params=_params("parallel"),
    )(dy, w_down, up)


def _mlp_bwd_in(dup, w_up, x1, plw, dx2, mixed, pmw, after):
    t = dup.shape[0]
    tm = min(MATMUL_BLOCK, t)

    def body(dup_ref, w_ref, x1_ref, plw_ref, dx2_ref, mx_ref, pmw_ref, dx1_ref, dmixed_ref, dplw_ref, dpmw_ref):
        i = pl.program_id(0)

        @pl.when(i == 0)
        def _():
            dplw_ref[...] = jnp.zeros_like(dplw_ref)
            dpmw_ref[...] = jnp.zeros_like(dpmw_ref)

        dh = jnp.dot(dup_ref[...], w_ref[...], preferred_element_type=F32)
        x1 = x1_ref[...]
        r = lax.rsqrt(jnp.mean(x1 * x1, axis=-1, keepdims=True) + EPS)
        xh = x1 * r
        dplw_ref[...] += jnp.sum(dh * xh, axis=0, keepdims=True)
        dxh = dh * plw_ref[...]
        dx1 = dx2_ref[...] + r * (dxh - xh * jnp.mean(dxh * xh, axis=-1, keepdims=True))
        dx1_ref[...] = dx1
        mx = mx_ref[...]
        r2 = lax.rsqrt(jnp.mean(mx * mx, axis=-1, keepdims=True) + EPS)
        mh = mx * r2
        dpmw_ref[...] += jnp.sum(dx1 * mh, axis=0, keepdims=True)
        dmh = dx1 * pmw_ref[...]
        dmixed_ref[...] = (r2 * (dmh - mh * jnp.mean(dmh * mh, axis=-1, keepdims=True))).astype(BF)

    tok = lambda w: pl.BlockSpec((tm, w), lambda i: (i, 0))
    vec = lambda w: pl.BlockSpec((1, w), lambda i: (0, 0))
    return pl.pallas_call(
        _ordered(body), name="mlp_bwd_in", grid=(t // tm,),
        in_specs=[ANY_SPEC, tok(D_FF), pl.BlockSpec((D_FF, D_MODEL), lambda i: (0, 0)), tok(D_MODEL),
                  vec(D_MODEL), tok(D_MODEL), tok(D_MODEL), vec(D_MODEL)],
        out_specs=[tok(D_MODEL), tok(D_MODEL), vec(D_MODEL), vec(D_MODEL)],
        out_shape=[_sds((t, D_MODEL)), _sds((t, D_MODEL), BF), _sds((1, D_MODEL)), _sds((1, D_MODEL))],
        compiler_params=_params("arbitrary"),
    )(after, dup, w_up, x1, plw, dx2, mixed, pmw)


def _wgrad(a, b, a_cols, split=1, a_fn=None, a_block0=0, name="wgrad"):
    t, b_cols = b.shape
    n_a = (a.shape[1] - a_block0 * a_cols) // a_cols if a_block0 else a.shape[1] // a_cols

    def body(a_ref, b_ref, o_ref):
        av = a_ref[...]
        if a_fn is not None:
            av = a_fn(av)
        o_ref[...] = _mm_tn(av, b_ref[...]).astype(BF).reshape(o_ref.shape)

    return pl.pallas_call(
        body, name=name, grid=(n_a,),
        in_specs=[pl.BlockSpec((t, a_cols), lambda i: (0, i + a_block0)), pl.BlockSpec((t, b_cols), lambda i: (0, 0))],
        out_specs=pl.BlockSpec((split, a_cols // split, b_cols), lambda i: (i, 0, 0)),
        out_shape=_sds((n_a * split, a_cols // split, b_cols), BF),
        compiler_params=_params("parallel"),
    )(a, b)


def _wgrad_pre_t(at, b, b_cols, name):
    rows, t = at.shape
    n_b = b.shape[1] // b_cols

    def body(a_ref, b_ref, o_ref):
        o_ref[0] = jnp.dot(a_ref[...], b_ref[...], preferred_element_type=F32).astype(BF)

    return pl.pallas_call(
        body, name=name, grid=(n_b,),
        in_specs=[pl.BlockSpec((rows, t), lambda j: (0, 0)), pl.BlockSpec((t, b_cols), lambda j: (0, j))],
        out_specs=pl.BlockSpec((1, rows, b_cols), lambda j: (j, 0, 0)), out_shape=_sds((n_b, rows, b_cols), BF),
        compiler_params=_params("parallel"),
    )(at, b)


def _small_bwd(proj, fb, al, dtb, dcq, dckt, dbe, dge, dproj):
    t = proj.shape[0]

    def body(sm_ref, fb_ref, al_ref, dtb_ref, dcq_ref, dckt_ref, dbe_ref, dge_ref, _, dsm_ref, dvec_ref):
        s = sm_ref[...]
        lane = _iota((1, LANES), 1)
        dcum = dcq_ref[...] - dckt_ref[...].T
        row = _iota((t, LANES), 0)
        step = 1
        while step < t:
            dcum = dcum + _shift_up(dcum, step, row)
            step *= 2
        dff = dcum * _sigmoid(-(s + fb_ref[...]))
        dbeta = jnp.zeros((t, LANES), F32)
        dg = jnp.zeros((t, LANES), F32)
        for hd in range(N_GDN_HEADS):
            dbeta = jnp.where(lane == SM_GB + hd, dbe_ref[:, hd * LANES:hd * LANES + 1], dbeta)
            dg = jnp.where(lane == SM_GA + hd, dge_ref[:, hd * LANES:hd * LANES + 1], dg)
        beta = _sigmoid(s)
        dgb = dbeta * beta * (1.0 - beta)
        za = s + dtb_ref[...]
        nea = -jnp.exp(al_ref[...])
        dga = dg * nea * _sigmoid(za)
        is_f = lane < SM_GB
        is_b = (lane >= SM_GB) & (lane < SM_GA)
        is_a = (lane >= SM_GA) & (lane < SM_GA + 4)
        dsm_ref[...] = jnp.where(is_f, dff, jnp.where(is_b, dgb, jnp.where(is_a, dga, 0.0))).astype(BF)
        dvec_ref[...] = jnp.zeros_like(dvec_ref)
        dvec_ref[0:1, :] = jnp.sum(jnp.where(is_f, dff, 0.0), axis=0, keepdims=True)
        dvec_ref[1:2, :] = jnp.sum(jnp.where(is_a, dg * nea * _softplus(za), 0.0), axis=0, keepdims=True)
        dvec_ref[2:3, :] = jnp.sum(jnp.where(is_a, dga, 0.0), axis=0, keepdims=True)

    vec = pl.BlockSpec((1, LANES), lambda i: (0, 0))
    full = lambda r, c: pl.BlockSpec((r, c), lambda i: (0, 0))
    small = pl.BlockSpec((t, LANES), lambda i: (0, COL_SMALL // LANES))
    return pl.pallas_call(
        body, name="small_bwd", grid=(1,),
        in_specs=[small, vec, vec, vec, full(t, LANES), full(LANES, t), full(t, 512), full(t, 512), ANY_SPEC],
        out_specs=[small, full(8, LANES)], out_shape=[_sds(dproj.shape, BF), _sds((8, LANES))],
        input_output_aliases={8: 0}, compiler_params=_params("arbitrary"),
    )(proj, fb, al, dtb, dcq, dckt, dbe, dge, dproj)


def _in_bwd(dproj, wt_al, x, nw, dx1, after):
    t = x.shape[0]
    tm = min(MATMUL_BLOCK, t)

    def body(dp_ref, w_ref, x_ref, nw_ref, dx1_ref, dx_ref, dnw_ref):
        i = pl.program_id(0)

        @pl.when(i == 0)
        def _():
            dnw_ref[...] = jnp.zeros_like(dnw_ref)

        dh = jnp.dot(dp_ref[...], w_ref[...], preferred_element_type=F32)
        xv = x_ref[...]
        r = lax.rsqrt(jnp.mean(xv * xv, axis=-1, keepdims=True) + EPS)
        xh = xv * r
        dnw_ref[...] += jnp.sum(dh * xh, axis=0, keepdims=True)
        dxh = dh * nw_ref[...]
        dx_ref[...] = dx1_ref[...] + r * (dxh - xh * jnp.mean(dxh * xh, axis=-1, keepdims=True))

    tok = lambda w: pl.BlockSpec((tm, w), lambda i: (i, 0))
    vec = lambda w: pl.BlockSpec((1, w), lambda i: (0, 0))
    return pl.pallas_call(
        _ordered(body), name="in_bwd", grid=(t // tm,),
        in_specs=[ANY_SPEC, tok(PROJ_W), pl.BlockSpec((PROJ_W, D_MODEL), lambda i: (0, 0)), tok(D_MODEL), vec(D_MODEL),
                  tok(D_MODEL)],
        out_specs=[tok(D_MODEL), vec(D_MODEL)], out_shape=[_sds((t, D_MODEL)), _sds((1, D_MODEL))],
        compiler_params=_params("arbitrary"),
    )(after, dproj, wt_al, x, nw, dx1)


def _row(v, width=None):
    v = v.reshape(1, -1).astype(F32)
    if width is not None and v.shape[1] < width:
        v = jnp.pad(v, ((0, 0), (0, width - v.shape[1])))
    return v


def _lane_vec(v, first):
    return jnp.pad(v.astype(F32), (first, LANES - first - v.shape[0])).reshape(1, LANES)


def _local_step(x, target, wt_al, started, late_weights, on_grads, convw, pre_mix_norm, fox_f_bias, fox_out_norm,
                gdn_a_log, gdn_dt_bias, gdn_out_norm, post_mix_norm, pre_mlp_norm, post_mlp_norm):
    t = x.shape[0]
    nch = t // CHUNK
    nw, pmw, plw, pw = _row(pre_mix_norm), _row(post_mix_norm), _row(pre_mlp_norm), _row(post_mlp_norm)
    fb, al, dtb = _lane_vec(fox_f_bias, SM_FF), _lane_vec(gdn_a_log, SM_GA), _lane_vec(gdn_dt_bias, SM_GA)
    fnw = _row(jnp.tile(fox_out_norm, 2))
    gnw = _row(gdn_out_norm)

    proj, h = _norm_proj(x, nw, wt_al, started)
    cumt, beta, g = _small_prep(proj, fb, al, dtb)
    o_fox, lse, fox_n = _fox_fwd(proj, cumt, fnw)
    qn, kn, cv, gc, be, mmat, amat = _gdn_prep(proj, convw, beta, g)
    n_prob = N_GDN_HEADS * nch
    m3 = mmat.reshape(n_prob, CHUNK, CHUNK)
    if n_prob < LANES:
        m3 = jnp.pad(m3, ((0, LANES - n_prob), (0, 0), (0, 0)))
    tinv = _tri_inverse(m3)[:n_prob].reshape(N_GDN_HEADS, nch, CHUNK, CHUNK)
    token = late_weights("mlp_relay", tinv)
    gdn_o, s_all, vn_all = _gdn_scan(qn, kn, cv, be, gc, tinv, amat)
    w_out = late_weights("w_out", gdn_o)
    x1, h2, mixed, omix, h2t = _mix_out(fox_n, gdn_o, proj, gnw, w_out, x, pmw, plw, token)
    w_up, w_down = late_weights("mlp", h2)
    up = _mlp_up(h2, w_up)
    dy, dx2, loss, d_pw = _mlp_down_loss(up, w_down, x1, pw, target)

    dup = _mlp_bwd_act(dy, w_down, up)
    relu2 = lambda u: jnp.square(jnp.maximum(u.astype(F32), 0.0))
    g_down = _wgrad(up, dy, D_FF // N_DEV, a_fn=relu2, name="wgrad_down")
    g_up = _wgrad_pre_t(h2t, dup, D_FF // N_DEV, name="wgrad_up")
    token = on_grads("mlp", (g_up, g_down))
    dx1, dmixed, d_plw, d_pmw = _mlp_bwd_in(dup, w_up, x1, plw, dx2, mixed, pmw, token)
    token = on_grads("w_out", _wgrad(omix, dmixed, 512, split=4, name="wgrad_out"))
    do_fox, dgo, dproj, d_fnw, d_gnw = _out_bwd(dmixed, w_out, o_fox, gdn_o, proj, fnw, gnw, token)
    dproj, dcq, dckt = _fox_bwd(proj, cumt, lse, o_fox, do_fox, dproj)
    dqn, dkn, dcv, dbe, dge = _gdn_bwd(qn, kn, cv, be, gc, tinv, amat, s_all, vn_all, dgo)
    dproj, dwq, dwk, dwv = _gdn_bwd_conv(proj, convw, dqn, dkn, dcv, dproj)
    dproj, dvec = _small_bwd(proj, fb, al, dtb, dcq, dckt, dbe, dge, dproj)
    g_main = _wgrad(dproj, h, WGRAD_IN_ROWS, name="wgrad_in")
    g_tail = _wgrad(dproj, h, LANES, a_block0=COL_SMALL // LANES, name="wgrad_in_small")
    token = on_grads("w_in", (g_main, g_tail))
    grad_x, d_nw = _in_bwd(dproj, wt_al, x, nw, dx1, token)
    small = dict(norms=(d_nw, d_pmw, d_plw, d_pw), fox_out_norm=d_fnw, gdn_out_norm=d_gnw, loss=loss, vectors=dvec,
                 conv=(dwq, dwk, dwv))
    return grad_x, small


MESH_IDS = pl.DeviceIdType.MESH
CHIP_FLIPS = ((0, 0), (1, 0), (0, 1), (1, 1))


def _place():
    return lax.axis_index("x"), lax.axis_index("y"), lax.axis_index("c")


def _all_gather(blocks):
    n = len(blocks)

    def body(*refs):
        ins, outs, (send_sems, recv_sems, local_sems) = refs[:n], refs[n:2 * n], refs[2 * n:]
        x, y, c = _place()
        sibling = (x, y, 1 - c)
        chips = [(x ^ fx, y ^ fy) for fx, fy in CHIP_FLIPS[1:]]

        def slot(out, px, py, pc):
            return out.at[4 * px + 2 * py + pc]

        def copy(a, k, block, to, src=None):
            return pltpu.make_async_remote_copy(
                src_ref=slot(outs[a], *block) if src is None else src, dst_ref=slot(outs[a], *block),
                send_sem=send_sems.at[a, k], recv_sem=recv_sems.at[a, k], device_id=to, device_id_type=MESH_IDS)

        pending = []
        for a in range(n):
            mine = pltpu.make_async_copy(ins[a], slot(outs[a], x, y, c), local_sems.at[a])
            mine.start()
            pending.append(mine)
        sends = []
        for a in range(n):
            first = [copy(a, 0, (x, y, c), sibling, src=ins[a])]
            first += [copy(a, 1 + j, (x, y, c), (*chip, c), src=ins[a]) for j, chip in enumerate(chips)]
            for cp in first:
                cp.start()
            sends += first
        for a in range(n):
            for j, chip in enumerate(chips):
                copy(a, 1 + j, (*chip, c), (x, y, c)).wait_recv()
                fwd = copy(a, 4 + j, (*chip, c), sibling)
                fwd.start()
                sends.append(fwd)
        for a in range(n):
            copy(a, 0, sibling, (x, y, c)).wait_recv()
            for j, chip in enumerate(chips):
                copy(a, 4 + j, (*chip, 1 - c), (x, y, c)).wait_recv()
        for cp in sends:
            cp.wait_send()
        for cp in pending:
            cp.wait()

    return pl.pallas_call(
        body, name="all_gather_weights", in_specs=[ANY_SPEC] * n, out_specs=[ANY_SPEC] * n,
        out_shape=[_sds((N_DEV,) + b.shape, b.dtype) for b in blocks],
        scratch_shapes=[pltpu.SemaphoreType.DMA((n, 7)), pltpu.SemaphoreType.DMA((n, 7)), pltpu.SemaphoreType.DMA((n,))],
        compiler_params=pltpu.CompilerParams(has_side_effects=True),
    )(*blocks)


def _adamw(w, g, m, v):
    m = ADAM_B1 * m + (1.0 - ADAM_B1) * g
    v = ADAM_B2 * v + (1.0 - ADAM_B2) * (g * g)
    m_hat = m / (1.0 - ADAM_B1 ** ADAM_STEP)
    v_hat = v / (1.0 - ADAM_B2 ** ADAM_STEP)
    return -ADAM_LR * (m_hat / (jnp.sqrt(v_hat) + ADAM_EPS) + ADAM_WD * w), m, v


def _pair_reduce(g, name):
    _, r, c_ = g.shape
    n = len(CHIP_FLIPS)

    def body(g_ref, out_ref, sib_buf, send_sems, recv_sems):
        x, y, c = _place()
        chips = [(x ^ fx, y ^ fy) for fx, fy in CHIP_FLIPS]
        piece = lambda chip, core: g_ref.at[4 * chip[0] + 2 * chip[1] + core]
        copies = [pltpu.make_async_remote_copy(
            src_ref=piece(chip, 1 - c), dst_ref=sib_buf.at[j], send_sem=send_sems.at[j], recv_sem=recv_sems.at[j],
            device_id=(x, y, 1 - c), device_id_type=MESH_IDS) for j, chip in enumerate(chips)]
        for cp in copies:
            cp.start()
        for j, chip in enumerate(chips):
            copies[j].wait_recv()
            out_ref[j] = (piece(chip, c)[...].astype(F32) + sib_buf[j].astype(F32)).astype(BF)
        for cp in copies:
            cp.wait_send()

    return pl.pallas_call(
        body, name=name, in_specs=[VMEM_SPEC], out_specs=VMEM_SPEC, out_shape=_sds((n, r, c_), BF),
        scratch_shapes=[pltpu.VMEM((n, r, c_), BF), pltpu.SemaphoreType.DMA((n,)), pltpu.SemaphoreType.DMA((n,))],
        compiler_params=pltpu.CompilerParams(vmem_limit_bytes=VMEM_LIMIT, has_side_effects=True),
    )(g)


HBM_SPEC = pl.BlockSpec(memory_space=pltpu.HBM)
SEM_SPEC = pl.BlockSpec(memory_space=pltpu.SEMAPHORE)
DATAFLOW = pltpu.SideEffectType.DATAFLOW_SIDE_EFFECTING


def _peers():
    x, y, c = _place()
    return 4 * x + 2 * y + c, [(x ^ (k >> 2), y ^ ((k >> 1) & 1), c ^ (k & 1)) for k in range(1, N_DEV)]


def _peer_index(peer):
    return 4 * peer[0] + 2 * peer[1] + peer[2]


def _zones_with_own(srcs, pieces, name, after=None, dtype=None, chips=False):
    n = len(srcs)
    slots = len(CHIP_FLIPS) if chips else N_DEV
    extra = [] if after is None else [after]
    dtypes = [s_.dtype if pieces or dtype is None else dtype for s_ in srcs]

    def body(me_ref, *refs):
        outs = refs[n + len(extra):]
        for a in range(n):
            if pieces:
                outs[a][0] = refs[a][0]
            else:
                val = refs[a][...].astype(dtypes[a])
                outs[a][0] = val
                outs[n + a][...] = val

    shapes = [s_.shape[1:] if pieces else s_.shape for s_ in srcs]
    mine = lambda sh: pl.BlockSpec((1,) + sh, lambda i, me_ref: (me_ref[0], 0, 0))
    whole = lambda sh: pl.BlockSpec(sh, lambda i, me_ref: (0, 0))
    in_specs = [mine(sh) if pieces else whole(sh) for sh in shapes]
    out_specs = [mine(sh) for sh in shapes] + ([] if pieces else [whole(sh) for sh in shapes])
    out_shape = [_sds((slots,) + sh, dt) for sh, dt in zip(shapes, dtypes)]
    out_shape += [] if pieces else [_sds(sh, dt) for sh, dt in zip(shapes, dtypes)]
    x, y, c = _place()
    own = 0 * x if chips else 4 * x + 2 * y + c
    out = pl.pallas_call(
        body, name=name,
        grid_spec=pltpu.PrefetchScalarGridSpec(num_scalar_prefetch=1, grid=(1,), in_specs=in_specs + [ANY_SPEC] * len(extra),
                                               out_specs=out_specs),
        out_shape=out_shape, compiler_params=_params("arbitrary"),
    )(own.astype(jnp.int32).reshape(1), *srcs, *extra)
    return out[:n], (list(srcs) if pieces else out[n:])


def _exchange_start(srcs, zones, pieces, name, chips=False):
    n = len(srcs)

    def body(*refs):
        ins, zs = refs[:n], refs[n:2 * n]
        sems = refs[2 * n:4 * n]
        token = refs[-1]
        me, peers = _peers()
        x, y, c = _place()
        if chips and pieces:
            routes = [((x ^ fx, y ^ fy, c), j, j) for j, (fx, fy) in enumerate(CHIP_FLIPS) if j]
        elif chips:
            routes = [((x ^ fx, y ^ fy, c), None, me) for fx, fy in CHIP_FLIPS[1:]]
        else:
            routes = [(peer, _peer_index(peer) if pieces else None, me) for peer in peers]
        for peer, src_slot, dst_slot in routes:
            for a in range(n):
                pltpu.make_async_remote_copy(
                    src_ref=ins[a] if src_slot is None else ins[a].at[src_slot], dst_ref=zs[a].at[dst_slot],
                    send_sem=sems[2 * a], recv_sem=sems[2 * a + 1], device_id=peer, device_id_type=MESH_IDS).start()
        token[...] = jnp.zeros_like(token)

    hbm = lambda v: pltpu.with_memory_space_constraint(v, pltpu.HBM)
    out = pl.pallas_call(
        body, name=name,
        out_shape=tuple([pltpu.SemaphoreType.DMA(())] * (2 * n) + [pltpu.HBM(v.shape, v.dtype) for v in srcs]
                        + [pltpu.HBM(z.shape, z.dtype) for z in zones] + [_sds((8, LANES))]),
        in_specs=[HBM_SPEC] * (2 * n), out_specs=tuple([SEM_SPEC] * (2 * n) + [HBM_SPEC] * (2 * n) + [VMEM_SPEC]),
        input_output_aliases={i: 2 * n + i for i in range(2 * n)},
        compiler_params=pltpu.CompilerParams(has_side_effects=DATAFLOW),
    )(*[hbm(v) for v in srcs], *[hbm(z) for z in zones])
    return out[:2 * n], out[2 * n:3 * n], out[3 * n:4 * n], out[-1]


def _relay_start(zones, name):
    n = len(zones)

    def body(*refs):
        zs, sems, token = refs[:n], refs[n:3 * n], refs[-1]
        x, y, c = _place()
        for fx, fy in CHIP_FLIPS:
            slot = 4 * (x ^ fx) + 2 * (y ^ fy) + c
            for a in range(n):
                pltpu.make_async_remote_copy(
                    src_ref=zs[a].at[slot], dst_ref=zs[a].at[slot], send_sem=sems[2 * a], recv_sem=sems[2 * a + 1],
                    device_id=(x, y, 1 - c), device_id_type=MESH_IDS).start()
        token[...] = jnp.zeros_like(token)

    out = pl.pallas_call(
        body, name=name,
        out_shape=tuple([pltpu.SemaphoreType.DMA(())] * (2 * n) + [pltpu.HBM(z.shape, z.dtype) for z in zones]
                        + [_sds((8, LANES))]),
        in_specs=[HBM_SPEC] * n, out_specs=tuple([SEM_SPEC] * (2 * n) + [HBM_SPEC] * n + [VMEM_SPEC]),
        input_output_aliases={i: 2 * n + i for i in range(n)},
        compiler_params=pltpu.CompilerParams(has_side_effects=DATAFLOW),
    )(*[pltpu.with_memory_space_constraint(z, pltpu.HBM) for z in zones])
    return out[:2 * n], [], out[2 * n:3 * n], out[-1]


def _exchange_wait(sems, srcs, zones, after, name, chips=False, n_copies=None):
    n, n_src = len(zones), len(srcs)
    after = list(after) if isinstance(after, (list, tuple)) else [after]
    n_copies = n_copies or (len(CHIP_FLIPS) - 1 if chips else N_DEV - 1)

    def body(*refs):
        zs, sm = refs[n_src:n_src + n], refs[n_src + n:n_src + 3 * n]
        me, peers = _peers()
        for a in range(n):
            seven = zs[a].at[pl.ds(0, n_copies)]
            cp = pltpu.make_async_remote_copy(src_ref=seven, dst_ref=seven, send_sem=sm[2 * a], recv_sem=sm[2 * a + 1],
                                              device_id=peers[0], device_id_type=MESH_IDS)
            cp.wait_send()
            cp.wait_recv()

    out = pl.pallas_call(
        body, name=name, out_shape=tuple([pltpu.HBM(v.shape, v.dtype) for v in srcs] + [pltpu.HBM(z.shape, z.dtype) for z in zones]),
        in_specs=[HBM_SPEC] * (n_src + n) + [SEM_SPEC] * (2 * n) + [ANY_SPEC] * len(after),
        out_specs=tuple([HBM_SPEC] * (n_src + n)), input_output_aliases={i: i for i in range(n_src + n)},
        compiler_params=pltpu.CompilerParams(has_side_effects=DATAFLOW),
    )(*srcs, *zones, *sems, *after)
    return out[n_src:]


def _sum_adamw(zone, w, m, v, name):
    n_slots, r, c_ = zone.shape
    rb = next((b for b in (256, 128) if r % b == 0), r)

    def body(z_ref, w_ref, m_ref, v_ref, grad_ref, delta_ref, nm_ref, nv_ref):
        total = z_ref[0].astype(F32)
        for d in range(1, n_slots):
            total = total + z_ref[d].astype(F32)
        grad_ref[...] = total
        delta_ref[...], nm_ref[...], nv_ref[...] = _adamw(w_ref[...], total, m_ref[...], v_ref[...])

    blk = pl.BlockSpec((rb, c_), lambda i: (i, 0))
    return pl.pallas_call(
        body, name=name, grid=(r // rb,), in_specs=[pl.BlockSpec((n_slots, rb, c_), lambda i: (0, i, 0)), blk, blk, blk],
        out_specs=[blk] * 4, out_shape=[_sds((r, c_))] * 4, compiler_params=_params("parallel"),
    )(zone, w, m, v)


SMALL_NORMS = ("pre_mix_norm", "post_mix_norm", "pre_mlp_norm", "post_mlp_norm")
SMALL_ORDER = SMALL_NORMS + ("fox_out_norm", "gdn_out_norm", "fox_f_bias", "gdn_a_log", "gdn_dt_bias", "gdn_conv_w")
CONV_SLAB_ROWS, CONV_SLAB_LANES = 8, 256


def _small_pack(small):
    def body(n0, n1, n2, n3, fnw_ref, gnw_ref, loss_ref, vec_ref, out_ref):
        out_ref[...] = jnp.zeros_like(out_ref)
        for i, ref in enumerate((n0, n1, n2, n3)):
            out_ref[i:i + 1, :] = ref[...]
        out_ref[4:5, 0:LANES] = fnw_ref[...]
        out_ref[4:5, LANES:2 * LANES] = gnw_ref[...]
        out_ref[4:5, 2 * LANES:3 * LANES] = loss_ref[...]
        out_ref[5:8, 0:LANES] = vec_ref[0:3, :]

    return pl.pallas_call(body, name="small_pack", in_specs=[VMEM_SPEC] * 8, out_specs=VMEM_SPEC,
                          out_shape=_sds((8, D_MODEL)))(*small["norms"], small["fox_out_norm"], small["gdn_out_norm"],
                                                        small["loss"], small["vectors"])


def _conv_slabs(dconv):
    blocks = dconv.reshape(CONV_K, N_DEV, -1).transpose(1, 0, 2)
    blocks = jnp.pad(blocks, ((0, 0), (0, CONV_SLAB_ROWS - CONV_K), (0, CONV_SLAB_LANES - blocks.shape[2])))
    return blocks.reshape(N_DEV * CONV_SLAB_ROWS, CONV_SLAB_LANES)


def _small_update(zone, conv_zone, w, m, v):
    n = len(SMALL_ORDER)
    n_conv = w["gdn_conv_w"].shape[1]

    def body(me_ref, z_ref, zc_ref, *refs):
        params, loss_ref, outs, (tot, totc) = refs[:3 * n], refs[3 * n], refs[3 * n + 1:7 * n + 1], refs[-2:]
        total, total_c = z_ref[0], zc_ref[0]
        for d in range(1, N_DEV):
            total, total_c = total + z_ref[d], total_c + zc_ref[d]
        tot[...] = total
        totc[...] = total_c
        loss_ref[...] = tot[4, 2 * LANES:2 * LANES + 1]
        mine = totc[pl.ds(pl.multiple_of(me_ref[0] * CONV_SLAB_ROWS, CONV_SLAB_ROWS), CONV_SLAB_ROWS), :]
        g = dict(zip(SMALL_NORMS, (tot[0], tot[1], tot[2], tot[3])))
        g.update(fox_out_norm=tot[4, 0:FOX_HEAD_DIM], gdn_out_norm=tot[4, LANES:LANES + GDN_HEAD_DIM],
                 fox_f_bias=tot[5, SM_FF:SM_FF + N_FOX_HEADS], gdn_a_log=tot[6, SM_GA:SM_GA + N_GDN_HEADS],
                 gdn_dt_bias=tot[7, SM_GA:SM_GA + N_GDN_HEADS], gdn_conv_w=mine[0:CONV_K, 0:n_conv])
        for i, name in enumerate(SMALL_ORDER):
            w_ref, m_ref, v_ref = params[3 * i:3 * i + 3]
            outs[4 * i][...] = g[name]
            outs[4 * i + 1][...], outs[4 * i + 2][...], outs[4 * i + 3][...] = _adamw(w_ref[...], g[name], m_ref[...],
                                                                                     v_ref[...])

    x, y, c = _place()
    operands = [a[name] for name in SMALL_ORDER for a in (w, m, v)]
    out = pl.pallas_call(
        body, name="small_update",
        in_specs=[pl.BlockSpec(memory_space=pltpu.SMEM)] + [VMEM_SPEC] * (2 + 3 * n), out_specs=[VMEM_SPEC] * (1 + 4 * n),
        out_shape=[_sds((1,))] + [_sds(w[name].shape) for name in SMALL_ORDER for _ in range(4)],
        scratch_shapes=[pltpu.VMEM(zone.shape[1:], F32), pltpu.VMEM(conv_zone.shape[1:], F32)],
    )((4 * x + 2 * y + c).astype(jnp.int32).reshape(1), zone, conv_zone, *operands)
    return out[0][0], {name: out[1 + 4 * i:5 + 4 * i] for i, name in enumerate(SMALL_ORDER)}


def _native_rows():
    groups = []
    for first, n_groups in ((0, N_FOX_HEADS // 2), (D_FOX * 3 + N_FOX_HEADS, N_GDN_HEADS)):
        for g in range(n_groups):
            groups += [(first + part * n_groups * LANES + g * LANES, first + part * n_groups * LANES + (g + 1) * LANES)
                       for part in range(3)]
    return tuple(groups) + ((3088, 3600), (1536, 1544), (3080, 3088))


NATIVE_ROWS = _native_rows()


W_IN_PIECE = D_PROJ // N_DEV
WGRAD_IN_ROWS = 512
SHUFFLE_LANES = 256


def _to_aligned_moves():
    moves, o = [], 0
    for lo, hi in NATIVE_ROWS:
        r = lo
        while r < hi:
            d = r // W_IN_PIECE
            k = min(hi, (d + 1) * W_IN_PIECE) - r
            moves.append((0, d, r - d * W_IN_PIECE, 0, o, k))
            r, o = r + k, o + k
    return moves


def _from_aligned_moves():
    moves = []
    for _, d, a, _, o, k in _to_aligned_moves():
        while k:
            n = min(k, WGRAD_IN_ROWS - o % WGRAD_IN_ROWS) if o < COL_SMALL else k
            moves.append((0, o // WGRAD_IN_ROWS, o % WGRAD_IN_ROWS, d, a, n) if o < COL_SMALL else
                         (1, 0, o - COL_SMALL, d, a, n))
            o, a, k = o + n, a + n, k - n
    return moves


def _shuffle_rows(srcs, moves, out_shape, name):
    c = srcs[0].shape[-1]

    def body(*refs):
        s_refs, o_ref, s_f, o_f = refs[:len(srcs)], refs[len(srcs)], refs[len(srcs) + 1:-1], refs[-1]
        for s_ref, f in zip(s_refs, s_f):
            f[...] = s_ref[...].astype(F32)
        o_f[...] = jnp.zeros_like(o_f)
        for i, ss, so, ds, do, k in moves:
            o_f[ds, pl.ds(do, k), :] = s_f[i][ss, pl.ds(so, k), :]
        o_ref[...] = o_f[...].astype(BF)

    blk = lambda shape: pl.BlockSpec(tuple(shape[:-1]) + (SHUFFLE_LANES,), lambda j: (0, 0, j))
    scratch = lambda shape: pltpu.VMEM(tuple(shape[:-1]) + (SHUFFLE_LANES,), F32)
    return pl.pallas_call(
        body, name=name, grid=(c // SHUFFLE_LANES,), in_specs=[blk(s.shape) for s in srcs], out_specs=blk(out_shape),
        out_shape=_sds(out_shape, BF), scratch_shapes=[scratch(s.shape) for s in srcs] + [scratch(out_shape)],
        compiler_params=_params("parallel"),
    )(*srcs)


def _cols_from_pieces(p):
    return p.transpose(1, 0, 2).reshape(p.shape[1], -1)


WEIGHT_ORDER = ("pre_mix_norm", "w_in", "fox_f_bias", "fox_out_norm", "gdn_conv_w", "gdn_a_log", "gdn_dt_bias",
                "gdn_out_norm", "w_out", "post_mix_norm", "pre_mlp_norm", "w_up", "w_down", "post_mlp_norm")


def kernel(x, pre_mix_norm, w_in, fox_f_bias, fox_out_norm, gdn_conv_w, gdn_a_log, gdn_dt_bias, gdn_out_norm, w_out, post_mix_norm, pre_mlp_norm, w_up, w_down, post_mlp_norm, loss_target, m_pre_mix_norm, m_w_in, m_fox_f_bias, m_fox_out_norm, m_gdn_conv_w, m_gdn_a_log, m_gdn_dt_bias, m_gdn_out_norm, m_w_out, m_post_mix_norm, m_pre_mlp_norm, m_w_up, m_w_down, m_post_mlp_norm, v_pre_mix_norm, v_w_in, v_fox_f_bias, v_fox_out_norm, v_gdn_conv_w, v_gdn_a_log, v_gdn_dt_bias, v_gdn_out_norm, v_w_out, v_post_mix_norm, v_pre_mlp_norm, v_w_up, v_w_down, v_post_mlp_norm):
    w = dict(pre_mix_norm=pre_mix_norm, w_in=w_in, fox_f_bias=fox_f_bias, fox_out_norm=fox_out_norm,
             gdn_conv_w=gdn_conv_w, gdn_a_log=gdn_a_log, gdn_dt_bias=gdn_dt_bias, gdn_out_norm=gdn_out_norm, w_out=w_out,
             post_mix_norm=post_mix_norm, pre_mlp_norm=pre_mlp_norm, w_up=w_up, w_down=w_down, post_mlp_norm=post_mlp_norm)
    mom = dict(pre_mix_norm=m_pre_mix_norm, w_in=m_w_in, fox_f_bias=m_fox_f_bias, fox_out_norm=m_fox_out_norm,
               gdn_conv_w=m_gdn_conv_w, gdn_a_log=m_gdn_a_log, gdn_dt_bias=m_gdn_dt_bias, gdn_out_norm=m_gdn_out_norm,
               w_out=m_w_out, post_mix_norm=m_post_mix_norm, pre_mlp_norm=m_pre_mlp_norm, w_up=m_w_up, w_down=m_w_down,
               post_mlp_norm=m_post_mlp_norm)
    var = dict(pre_mix_norm=v_pre_mix_norm, w_in=v_w_in, fox_f_bias=v_fox_f_bias, fox_out_norm=v_fox_out_norm,
               gdn_conv_w=v_gdn_conv_w, gdn_a_log=v_gdn_a_log, gdn_dt_bias=v_gdn_dt_bias, gdn_out_norm=v_gdn_out_norm,
               w_out=v_w_out, post_mix_norm=v_post_mix_norm, pre_mlp_norm=v_pre_mlp_norm, w_up=v_w_up, w_down=v_w_down,
               post_mlp_norm=v_post_mlp_norm)

    win_g, conv_g = _all_gather([w_in.T.astype(BF), gdn_conv_w])
    wt_al = _shuffle_rows([win_g], _to_aligned_moves(), (1, PROJ_W, D_MODEL), "w_in_to_aligned")[0]
    convw = _cols_from_pieces(conv_g)
    gathers, after = {}, win_g
    for name, shards in (("w_out", [w_out]), ("mlp", [w_up.T, w_down])):
        zones, shards = _zones_with_own(shards, False, "gather_" + name + "_own", after=after, dtype=BF)
        gathers[name] = _exchange_start(shards, zones, False, "gather_" + name + "_start", chips=name == "mlp")
        after = gathers[name][3]

    def late_weights(name, after):
        if name == "mlp_relay":
            sems, shards, zones, _ = gathers["mlp"]
            zones = _exchange_wait(sems, shards, zones, after, "gather_mlp_wait", chips=True)
            gathers["mlp"] = _relay_start(zones, "gather_mlp_relay")
            return gathers["mlp"][3]
        sems, shards, zones, _ = gathers[name]
        got = _exchange_wait(sems, shards, zones, after, "gather_" + name + "_done",
                             n_copies=len(CHIP_FLIPS) if name == "mlp" else None)
        if name == "w_out":
            return got[0].reshape(D_MODEL, D_MODEL)
        return got[0].reshape(D_FF, D_MODEL), got[1].reshape(D_FF, D_MODEL)

    scatters = {}

    def on_grads(name, g):
        chips = name == "w_in"
        if name == "w_in":
            g = _shuffle_rows(list(g), _from_aligned_moves(), (N_DEV, W_IN_PIECE, D_MODEL), "w_in_grad_from_aligned")
            g = _pair_reduce(g, "pair_reduce_w_in")
        srcs = list(g) if name == "mlp" else [g]
        zones, _ = _zones_with_own(srcs, True, "scatter_" + name + "_own", chips=chips)
        scatters[name] = _exchange_start(srcs, zones, True, "scatter_" + name + "_start", chips=chips)
        return scatters[name][3]

    grad_x, small = _local_step(
        x[0], loss_target[0], wt_al, after, late_weights, on_grads, convw, pre_mix_norm,
        fox_f_bias, fox_out_norm, gdn_a_log, gdn_dt_bias, gdn_out_norm, post_mix_norm, pre_mlp_norm, post_mlp_norm)
    slabs = [_small_pack(small), _conv_slabs(jnp.concatenate(small["conv"], axis=1))]
    zones, slabs = _zones_with_own(slabs, False, "small_own")
    scatters["small"] = _exchange_start(slabs, zones, False, "small_start")

    grads, delta, new_m, new_v = {}, {}, {}, {}
    after = scatters["small"][3]
    for name, members in (("mlp", ("w_up", "w_down")), ("w_out", ("w_out",)), ("small", ()), ("w_in", ("w_in",))):
        sems, srcs, zones, _ = scatters[name]
        zones = _exchange_wait(sems, srcs, zones, after, "scatter_" + name + "_wait", chips=name == "w_in")
        if name == "small":
            loss, updated = _small_update(zones[0], zones[1], w, mom, var)
            for n, res in updated.items():
                grads[n], delta[n], new_m[n], new_v[n] = res
            after = grads["pre_mix_norm"]
        for n, zone in zip(members, zones):
            if n == "w_in":
                res = _sum_adamw(zone, w[n].T, mom[n].T, var[n].T, "adamw_" + n)
                grads[n], delta[n], new_m[n], new_v[n] = [r.T for r in res]
            else:
                grads[n], delta[n], new_m[n], new_v[n] = _sum_adamw(zone, w[n], mom[n], var[n], "adamw_" + n)
        if members:
            after = [grads[n] for n in members]

    return (loss, grad_x[None], *[grads[n] for n in WEIGHT_ORDER], *[delta[n] for n in WEIGHT_ORDER],
            *[new_m[n] for n in WEIGHT_ORDER], *[new_v[n] for n in WEIGHT_ORDER])
```

```python
import jax
import jax.numpy as jnp
from jax import lax
from jax.experimental import pallas as pl
from jax.experimental.pallas import tpu as pltpu

F32 = jnp.float32
BF = jnp.bfloat16

D_MODEL = 1024
N_FOX_HEADS, FOX_HEAD_DIM = 8, 64
N_GDN_HEADS, GDN_HEAD_DIM = 4, 128
D_FOX = N_FOX_HEADS * FOX_HEAD_DIM
D_GDN = N_GDN_HEADS * GDN_HEAD_DIM
CHUNK = 64
CONV_K = 4
D_FF = 4 * D_MODEL
EPS = 1e-6
D_PROJ = 3600
N_DEV = 8

PROJ_W = 3712
COL_FOX, COL_GDN, COL_GZ, COL_SMALL = 0, 1536, 3072, 3584
LANES = 128
QKV = 3 * LANES
SM_FF, SM_GB, SM_GA = 0, 8, 12

ADAM_LR, ADAM_B1, ADAM_B2, ADAM_EPS, ADAM_WD, ADAM_STEP = 0.001, 0.9, 0.999, 1e-08, 0.01, 10

TOKEN_BLOCK = 256
MATMUL_BLOCK = 512
FOX_SCALE = FOX_HEAD_DIM ** -0.5
GDN_QSCALE = GDN_HEAD_DIM ** -0.5
NEG_BIG = -1e30
VMEM_LIMIT = 56 * 1024 * 1024

VMEM_SPEC = pl.BlockSpec(memory_space=pltpu.VMEM)
ANY_SPEC = pl.BlockSpec(memory_space=pl.ANY)


def _sds(shape, dtype=F32):
    return jax.ShapeDtypeStruct(shape, dtype)


def _params(*sem):
    return pltpu.CompilerParams(dimension_semantics=sem if sem else None, vmem_limit_bytes=VMEM_LIMIT)


def _ordered(body):
    def ordered(_, *refs):
        body(*refs)

    return ordered


def _mm(a, b):
    return jnp.dot(a.astype(BF), b.astype(BF), preferred_element_type=F32)


def _mm_nt(a, b):
    return lax.dot_general(a.astype(BF), b.astype(BF), (((1,), (1,)), ((), ())), preferred_element_type=F32)


def _mm_tn(a, b):
    return lax.dot_general(a.astype(BF), b.astype(BF), (((0,), (0,)), ((), ())), preferred_element_type=F32)


def _sigmoid(x):
    return 1.0 / (1.0 + jnp.exp(-x))


def _softplus(x):
    return jnp.maximum(x, 0.0) + jnp.log1p(jnp.exp(-jnp.abs(x)))


def _iota(shape, dim):
    return lax.broadcasted_iota(jnp.int32, shape, dim)


def _shift_down(x, s, row):
    return jnp.where(row >= s, pltpu.roll(x, s, 0), 0.0)


def _shift_up(x, s, row):
    n = x.shape[0]
    return jnp.where(row < n - s, pltpu.roll(x, n - s, 0), 0.0)


def _norm_proj(x, nw, wt_al, after):
    t = x.shape[0]

    def body(x_ref, nw_ref, w_ref, proj_ref, h_ref):
        xv = x_ref[...]
        r = lax.rsqrt(jnp.mean(xv * xv, axis=-1, keepdims=True) + EPS)
        h = (xv * r * nw_ref[...]).astype(BF)
        h_ref[...] = h
        proj_ref[...] = lax.dot_general(h, w_ref[...], (((1,), (1,)), ((), ())), preferred_element_type=F32)

    tm = min(MATMUL_BLOCK, t)
    return pl.pallas_call(
        _ordered(body), name="norm_proj", grid=(t // tm,),
        in_specs=[ANY_SPEC, pl.BlockSpec((tm, D_MODEL), lambda i: (i, 0)), pl.BlockSpec((1, D_MODEL), lambda i: (0, 0)),
                  pl.BlockSpec((PROJ_W, D_MODEL), lambda i: (0, 0))],
        out_specs=[pl.BlockSpec((tm, PROJ_W), lambda i: (i, 0)), pl.BlockSpec((tm, D_MODEL), lambda i: (i, 0))],
        out_shape=[_sds((t, PROJ_W)), _sds((t, D_MODEL), BF)],
        compiler_params=_params("parallel"),
    )(after, x, nw, wt_al)


def _lane_column(x, lane):
    return jnp.sum(jnp.where(_iota((1, LANES), 1) == lane, x, 0.0), axis=-1, keepdims=True)


def _small_prep(proj, fb, al, dtb):
    t = proj.shape[0]

    def body(sm_ref, fb_ref, al_ref, dtb_ref, cumt_ref, beta_ref, g_ref):
        s = sm_ref[...]
        z = s + fb_ref[...]
        cum = jnp.minimum(z, 0.0) - jnp.log1p(jnp.exp(-jnp.abs(z)))
        row = _iota((t, LANES), 0)
        step = 1
        while step < t:
            cum = cum + _shift_down(cum, step, row)
            step *= 2
        cumt_ref[...] = cum.T
        beta_ref[...] = _sigmoid(s)
        g_ref[...] = -jnp.exp(al_ref[...]) * _softplus(s + dtb_ref[...])

    vec = pl.BlockSpec((1, LANES), lambda i: (0, 0))
    tok = pl.BlockSpec((t, LANES), lambda i: (0, 0))
    return pl.pallas_call(
        body, name="small_prep", grid=(1,),
        in_specs=[pl.BlockSpec((t, LANES), lambda i: (0, COL_SMALL // LANES)), vec, vec, vec],
        out_specs=[pl.BlockSpec((LANES, t), lambda i: (0, 0)), tok, tok],
        out_shape=[_sds((LANES, t)), _sds((t, LANES)), _sds((t, LANES))],
        compiler_params=_params("arbitrary"),
    )(proj, fb, al, dtb)


def _fox_stack(x, first):
    return jnp.concatenate([jnp.where(first, x, 0.0), jnp.where(first, 0.0, x)], axis=0).astype(BF)


def _fox_unstack(y, first):
    n = y.shape[0] // 2
    return jnp.where(first, y[:n], y[n:])


def _fox_logits(q2_i, kb, cumt_ref, pair, i, tq):
    klen = (i + 1) * tq
    s = lax.dot_general(q2_i, kb[:klen], (((1,), (1,)), ((), ())), preferred_element_type=F32)
    upper = _iota((2 * tq, 1), 0) < tq
    s = s - jnp.where(upper, cumt_ref[pl.ds(2 * pair, 1), 0:klen], cumt_ref[pl.ds(2 * pair + 1, 1), 0:klen])
    causal = _iota((2 * tq, tq), 1) <= _iota((2 * tq, tq), 0) % tq
    parts = [(s[:, :klen - tq], 0, klen - tq)] if i else []
    return parts + [(jnp.where(causal, s[:, klen - tq:], NEG_BIG), klen - tq, klen)]


def _fox_fwd(proj, cumt, fnw):
    t = proj.shape[0]
    tq = min(TOKEN_BLOCK, t // 2)
    nq = t // tq

    def body(q_ref, k_ref, v_ref, cumt_ref, fnw_ref, o_ref, lse_ref, fn_ref):
        j = pl.program_id(0)
        first = _iota((1, LANES), 1) < FOX_HEAD_DIM
        kb = k_ref[...].astype(BF)
        vb = v_ref[...].astype(BF)
        for i in range(nq):
            rows = slice(i * tq, (i + 1) * tq)
            q2 = _fox_stack(q_ref[rows, :] * FOX_SCALE, first)
            parts = _fox_logits(q2, kb, cumt_ref, j, i, tq)
            m = jnp.max(parts[-1][0], axis=-1, keepdims=True)
            if i:
                m = jnp.maximum(m, jnp.max(parts[0][0], axis=-1, keepdims=True))
            l = jnp.zeros((2 * tq, 1), F32)
            o = jnp.zeros((2 * tq, LANES), F32)
            for s, lo, hi in parts:
                p = jnp.exp(s - m)
                l = l + jnp.sum(p, axis=-1, keepdims=True)
                o = o + jnp.dot(p.astype(BF), vb[lo:hi], preferred_element_type=F32)
            o_acc = _fox_unstack(o / l, first)
            lse_acc = _fox_unstack(jnp.broadcast_to(m + jnp.log(l), (2 * tq, LANES)), first)
            o_ref[rows, :] = o_acc
            lse_ref[rows, :] = lse_acc
            o2 = o_acc * o_acc
            s0 = jnp.sum(jnp.where(first, o2, 0.0), axis=-1, keepdims=True)
            s1 = jnp.sum(jnp.where(first, 0.0, o2), axis=-1, keepdims=True)
            r = lax.rsqrt(jnp.where(first, s0, s1) * (1.0 / FOX_HEAD_DIM) + EPS)
            fn_ref[rows, :] = (o_acc * r * fnw_ref[...]).astype(BF)

    qkv = lambda k: pl.BlockSpec((t, LANES), lambda j: (0, COL_FOX // LANES + 3 * j + k))
    pair = pl.BlockSpec((t, LANES), lambda j: (0, j))
    return pl.pallas_call(
        body, name="fox_fwd", grid=(N_FOX_HEADS // 2,),
        in_specs=[qkv(0), qkv(1), qkv(2), pl.BlockSpec((LANES, t), lambda j: (0, 0)),
                  pl.BlockSpec((1, LANES), lambda j: (0, 0))],
        out_specs=[pair, pair, pair],
        out_shape=[_sds((t, D_FOX)), _sds((t, D_FOX)), _sds((t, D_FOX), BF)],
        compiler_params=_params("parallel"),
    )(proj, proj, proj, cumt, fnw)


def _fox_bwd(proj, cumt, lse, o, do, dproj):
    t = proj.shape[0]
    tq = min(TOKEN_BLOCK, t // 2)
    nq = t // tq

    def body(q_ref, k_ref, v_ref, cumt_ref, lse_ref, o_ref, do_ref, _, dqkv_ref, dcq_ref, dckt_ref, dk_s, dv_s):
        j = pl.program_id(0)

        @pl.when(j == 0)
        def _():
            dcq_ref[...] = jnp.zeros_like(dcq_ref)
            dckt_ref[...] = jnp.zeros_like(dckt_ref)

        lane = _iota((1, LANES), 1)

        first = _iota((1, LANES), 1) < FOX_HEAD_DIM
        kb = k_ref[...].astype(BF)
        vb = v_ref[...].astype(BF)
        dk_s[...] = jnp.zeros_like(dk_s)
        dv_s[...] = jnp.zeros_like(dv_s)
        for i in range(nq):
            rows = slice(i * tq, (i + 1) * tq)
            do_i = do_ref[rows, :]
            prod = do_i * o_ref[rows, :]
            lse_i = lse_ref[rows, :]
            q2 = _fox_stack(q_ref[rows, :] * FOX_SCALE, first)
            do2 = _fox_stack(do_i, first)
            delta = jnp.concatenate([jnp.sum(jnp.where(first, prod, 0.0), axis=-1, keepdims=True),
                                     jnp.sum(jnp.where(first, 0.0, prod), axis=-1, keepdims=True)], axis=0)
            lse2 = jnp.concatenate([lse_i[:, 0:1], lse_i[:, FOX_HEAD_DIM:FOX_HEAD_DIM + 1]], axis=0)
            dq2 = jnp.zeros((2 * tq, LANES), F32)
            dcq2 = jnp.zeros((2 * tq, 1), F32)
            for s, lo, hi in _fox_logits(q2, kb, cumt_ref, j, i, tq):
                p = jnp.exp(s - lse2)
                ds = p * (_mm_nt(do2, vb[lo:hi]) - delta)
                dsb = ds.astype(BF)
                dq2 = dq2 + jnp.dot(dsb, kb[lo:hi], preferred_element_type=F32)
                dk_s[lo:hi, :] += _mm_tn(dsb, q2)
                dv_s[lo:hi, :] += _mm_tn(p, do2)
                dcq2 = dcq2 + jnp.sum(ds, axis=-1, keepdims=True)
                dckt_ref[pl.ds(2 * j, 1), lo:hi] += jnp.sum(ds[:tq], axis=0, keepdims=True)
                dckt_ref[pl.ds(2 * j + 1, 1), lo:hi] += jnp.sum(ds[tq:], axis=0, keepdims=True)
            dqkv_ref[rows, 0:LANES] = (_fox_unstack(dq2, first) * FOX_SCALE).astype(BF)
            dcq_ref[rows, :] += jnp.where(lane == 2 * j, dcq2[:tq], jnp.where(lane == 2 * j + 1, dcq2[tq:], 0.0))
        dqkv_ref[:, LANES:2 * LANES] = dk_s[...].astype(BF)
        dqkv_ref[:, 2 * LANES:QKV] = dv_s[...].astype(BF)

    qkv = lambda k: pl.BlockSpec((t, LANES), lambda j: (0, COL_FOX // LANES + 3 * j + k))
    pair = pl.BlockSpec((t, LANES), lambda j: (0, j))
    rows128 = pl.BlockSpec((LANES, t), lambda j: (0, 0))
    return pl.pallas_call(
        body, name="fox_bwd", grid=(N_FOX_HEADS // 2,),
        in_specs=[qkv(0), qkv(1), qkv(2), rows128, pair, pair, pair, ANY_SPEC],
        out_specs=[pl.BlockSpec((t, QKV), lambda j: (0, COL_FOX // QKV + j)),
                   pl.BlockSpec((t, LANES), lambda j: (0, 0)), rows128],
        out_shape=[_sds(dproj.shape, BF), _sds((t, LANES)), _sds((LANES, t))],
        scratch_shapes=[pltpu.VMEM((t, LANES), F32), pltpu.VMEM((t, LANES), F32)],
        input_output_aliases={7: 0}, compiler_params=_params("arbitrary"),
    )(proj, proj, proj, cumt, lse, o, do, dproj)


def _conv(x, w, row):
    return (w[3:4, :] * x + w[2:3, :] * _shift_down(x, 1, row) + w[1:2, :] * _shift_down(x, 2, row)
            + w[0:1, :] * _shift_down(x, 3, row))


def _chunk_decay(gc_c):
    gi = gc_c[:, 0:CHUNK]
    gj = gc_c.T[0:CHUNK, :]
    ri = _iota((CHUNK, CHUNK), 0)
    cj = _iota((CHUNK, CHUNK), 1)
    return jnp.where(ri >= cj, jnp.exp(jnp.minimum(gi - gj, 0.0)), 0.0), ri > cj


def _gdn_specs(t):
    col = lambda off: pl.BlockSpec((t, LANES), lambda h: (0, off + h))
    cw = lambda off: pl.BlockSpec((CONV_K, LANES), lambda h: (0, off + h))
    mat = pl.BlockSpec((1, t // CHUNK, CHUNK, CHUNK), lambda h: (h, 0, 0, 0))
    qkv = lambda k: pl.BlockSpec((t, LANES), lambda h: (0, COL_GDN // LANES + 3 * h + k))
    return col, cw, mat, qkv


def _gdn_prep(proj, convw, beta, g):
    t = proj.shape[0]
    nch = t // CHUNK

    def body(xq_ref, xk_ref, xv_ref, wq_ref, wk_ref, wv_ref, beta_ref, g_ref,
             qn_ref, kn_ref, cv_ref, gc_ref, be_ref, m_ref, a_ref):
        row = _iota((t, LANES), 0)
        hd = pl.program_id(0)
        be_ref[...] = jnp.broadcast_to(_lane_column(beta_ref[...], SM_GB + hd), (t, LANES))

        def act(x_ref, w_ref):
            y = _conv(x_ref[...], w_ref[...], row)
            return y * _sigmoid(y)

        cq = act(xq_ref, wq_ref)
        ck = act(xk_ref, wk_ref)
        cv_ref[...] = act(xv_ref, wv_ref)
        qn_ref[...] = cq * lax.rsqrt(jnp.sum(cq * cq, axis=-1, keepdims=True) + EPS) * GDN_QSCALE
        kn_ref[...] = ck * lax.rsqrt(jnp.sum(ck * ck, axis=-1, keepdims=True) + EPS)
        gc = jnp.broadcast_to(_lane_column(g_ref[...], SM_GA + hd), (t, LANES))
        pos = row % CHUNK
        step = 1
        while step < CHUNK:
            gc = gc + jnp.where(pos >= step, pltpu.roll(gc, step, 0), 0.0)
            step *= 2
        gc_ref[...] = gc

        group = 4 if nch % 4 == 0 else 1

        def chunks(gi, carry):
            ns = [gi * group + c for c in range(group)]
            sls = [pl.ds(pl.multiple_of(n * CHUNK, CHUNK), CHUNK) for n in ns]
            ks = [kn_ref[sl, :] for sl in sls]
            kk = [_mm_nt(k_c * be_ref[sl, :], k_c) for k_c, sl in zip(ks, sls)]
            qk = [_mm_nt(qn_ref[sl, :], k_c) for k_c, sl in zip(ks, sls)]
            for c, n in enumerate(ns):
                decay, strict = _chunk_decay(gc_ref[sls[c], :])
                m_ref[0, n] = jnp.where(strict, kk[c] * decay, 0.0)
                a_ref[0, n] = qk[c] * decay
            return carry

        lax.fori_loop(0, nch // group, chunks, 0)

    col, cw, mat, qkv = _gdn_specs(t)
    return pl.pallas_call(
        body, name="gdn_prep", grid=(N_GDN_HEADS,),
        in_specs=[qkv(0), qkv(1), qkv(2), cw(0), cw(4), cw(8)] + [pl.BlockSpec((t, LANES), lambda h: (0, 0))] * 2,
        out_specs=[col(0), col(0), col(0), col(0), col(0), mat, mat],
        out_shape=[_sds((t, D_GDN))] * 5 + [_sds((N_GDN_HEADS, nch, CHUNK, CHUNK))] * 2,
        compiler_params=_params("parallel"),
    )(proj, proj, proj, convw, convw, convw, beta, g)


def _tri_inverse(m3):
    assert m3.shape == (LANES, CHUNK, CHUNK)

    def body(m_ref, t_ref, ms, ts):
        for i in range(CHUNK):
            ms[i * CHUNK:(i + 1) * CHUNK, :] = m_ref[:, i, :].T
        cidx = _iota((CHUNK, LANES), 0)

        def outer(i, carry):
            def inner(jj, acc):
                mrow = ms[pl.ds(i * CHUNK + jj, 1), :]
                return acc - mrow * ts[pl.ds(pl.multiple_of(jj * CHUNK, CHUNK), CHUNK), :]

            acc = lax.fori_loop(0, i, inner, jnp.where(cidx == i, 1.0, 0.0).astype(F32))
            ts[pl.ds(pl.multiple_of(i * CHUNK, CHUNK), CHUNK), :] = acc
            return carry

        lax.fori_loop(0, CHUNK, outer, 0)
        for i in range(CHUNK):
            t_ref[:, i, :] = ts[i * CHUNK:(i + 1) * CHUNK, :].T

    return pl.pallas_call(
        body, name="tri_inverse", in_specs=[VMEM_SPEC], out_specs=VMEM_SPEC,
        out_shape=_sds((LANES, CHUNK, CHUNK)),
        scratch_shapes=[pltpu.VMEM((CHUNK * CHUNK, LANES), F32), pltpu.VMEM((CHUNK * CHUNK, LANES), F32)],
        compiler_params=_params(),
    )(m3)


def _gdn_chunk_terms(q, k, v, b, gcc):
    eg = jnp.exp(gcc)
    last = gcc[CHUNK - 1:CHUNK, :]
    egl = jnp.exp(last - gcc)
    gl = jnp.exp(last)
    kb = k * b
    return eg, egl, gl, kb, v * b, kb * eg, q * eg, k * egl


GDN_BLOCK_CHUNKS = 4


def _gdn_block_specs(t, reverse):
    cb = GDN_BLOCK_CHUNKS
    nb = t // (cb * CHUNK)
    idx = (lambda i: nb - 1 - i) if reverse else (lambda i: i)
    tok = pl.BlockSpec((cb * CHUNK, D_GDN), lambda i: (idx(i), 0))
    mat = pl.BlockSpec((N_GDN_HEADS, cb, CHUNK, CHUNK), lambda i: (0, idx(i), 0, 0))
    state = pl.BlockSpec((N_GDN_HEADS, cb, GDN_HEAD_DIM, GDN_HEAD_DIM), lambda i: (0, idx(i), 0, 0))
    return nb, tok, mat, state


def _gdn_scan(qn, kn, cv, be, gc, tinv, amat):
    t = qn.shape[0]
    nch = t // CHUNK

    def body(q_ref, k_ref, v_ref, b_ref, gc_ref, t_ref, a_ref, o_ref, sall_ref, vn_ref, s_scr):
        @pl.when(pl.program_id(0) == 0)
        def _():
            s_scr[...] = jnp.zeros_like(s_scr)

        heads = range(N_GDN_HEADS)
        cols = [slice(hd * LANES, (hd + 1) * LANES) for hd in heads]
        s = [s_scr[hd] for hd in heads]
        for cc in range(GDN_BLOCK_CHUNKS):
            rs = slice(cc * CHUNK, (cc + 1) * CHUNK)
            terms = [_gdn_chunk_terms(q_ref[rs, cs], k_ref[rs, cs], v_ref[rs, cs], b_ref[rs, cs], gc_ref[rs, cs])
                     for cs in cols]
            for hd in heads:
                sall_ref[hd, cc] = s[hd]
            uw = [_mm(t_ref[hd, cc], jnp.concatenate([terms[hd][4], terms[hd][5]], axis=1)) for hd in heads]
            ws_qs = [_mm(jnp.concatenate([uw[hd][:, LANES:], terms[hd][6]], axis=0), s[hd]) for hd in heads]
            vn = [uw[hd][:, :LANES] - ws_qs[hd][:CHUNK] for hd in heads]
            a_vn = [_mm(a_ref[hd, cc], vn[hd]) for hd in heads]
            kd_vn = [_mm_tn(terms[hd][7], vn[hd]) for hd in heads]
            for hd in heads:
                vn_ref[rs, cols[hd]] = vn[hd]
                o_ref[rs, cols[hd]] = ws_qs[hd][CHUNK:] + a_vn[hd]
                s[hd] = s[hd] * terms[hd][2] + kd_vn[hd]
        for hd in heads:
            s_scr[hd] = s[hd]

    nb, tok, mat, state = _gdn_block_specs(t, False)
    return pl.pallas_call(
        body, name="gdn_scan", grid=(nb,),
        in_specs=[tok] * 5 + [mat, mat], out_specs=[tok, state, tok],
        out_shape=[_sds((t, D_GDN)), _sds((N_GDN_HEADS, nch, GDN_HEAD_DIM, GDN_HEAD_DIM)), _sds((t, D_GDN))],
        scratch_shapes=[pltpu.VMEM((N_GDN_HEADS, GDN_HEAD_DIM, GDN_HEAD_DIM), F32)],
        compiler_params=_params("arbitrary"),
    )(qn, kn, cv, be, gc, tinv, amat)


def _gdn_bwd(qn, kn, cv, be, gc, tinv, amat, s_all, vn_all, do):
    t = qn.shape[0]

    def body(q_ref, k_ref, v_ref, b_ref, gc_ref, t_ref, a_ref, sall_ref, vn_ref, do_ref,
             dq_ref, dk_ref, dv_ref, db_ref, dg_ref, ds_scr):
        @pl.when(pl.program_id(0) == 0)
        def _():
            ds_scr[...] = jnp.zeros_like(ds_scr)

        lastrow = _iota((CHUNK, LANES), 0) == CHUNK - 1
        heads = range(N_GDN_HEADS)
        cols = [slice(hd * LANES, (hd + 1) * LANES) for hd in heads]
        each = lambda fn: [fn(hd) for hd in heads]
        rows_cat = lambda x, y: jnp.concatenate([x, y], axis=0)
        lane_cat = lambda x, y: jnp.concatenate([x, y], axis=1)
        dsp = each(lambda hd: ds_scr[hd])
        for cc in reversed(range(GDN_BLOCK_CHUNKS)):
            rs = slice(cc * CHUNK, (cc + 1) * CHUNK)
            q = each(lambda hd: q_ref[rs, cols[hd]])
            k = each(lambda hd: k_ref[rs, cols[hd]])
            v = each(lambda hd: v_ref[rs, cols[hd]])
            b = each(lambda hd: b_ref[rs, cols[hd]])
            gcc = each(lambda hd: gc_ref[rs, cols[hd]])
            do_c = each(lambda hd: do_ref[rs, cols[hd]])
            vn = each(lambda hd: vn_ref[rs, cols[hd]])
            tn = each(lambda hd: t_ref[hd, cc])
            st = each(lambda hd: sall_ref[hd, cc])
            terms = each(lambda hd: _gdn_chunk_terms(q[hd], k[hd], v[hd], b[hd], gcc[hd]))
            eg, egl, gl, kb, vb, kbg, qd, kd = [[terms[hd][i] for hd in heads] for i in range(8)]
            w = each(lambda hd: _mm(tn[hd], kbg[hd]))
            a_do = each(lambda hd: _mm_tn(a_ref[hd, cc], do_c[hd]))
            kd_ds = each(lambda hd: _mm(kd[hd], dsp[hd]))
            da = each(lambda hd: _mm_nt(do_c[hd], vn[hd]))
            dkd = each(lambda hd: _mm_nt(vn[hd], dsp[hd]))
            by_k = each(lambda hd: _mm_nt(rows_cat(kb[hd], q[hd]), k[hd]))
            dgl = each(lambda hd: jnp.sum(jnp.sum(dsp[hd] * st[hd], axis=-1, keepdims=True), axis=0, keepdims=True))
            dvn = each(lambda hd: a_do[hd] + kd_ds[hd])
            do_dvn = each(lambda hd: rows_cat(do_c[hd], dvn[hd]))
            by_s = each(lambda hd: _mm_nt(do_dvn[hd], st[hd]))
            dqd = each(lambda hd: by_s[hd][:CHUNK])
            dvn_dw = each(lambda hd: lane_cat(dvn[hd], -by_s[hd][CHUNK:]))
            dsp = each(lambda hd: _mm_tn(rows_cat(qd[hd], -w[hd]), do_dvn[hd]) + gl[hd] * dsp[hd])
            dt = each(lambda hd: _mm_nt(dvn_dw[hd], lane_cat(vb[hd], kbg[hd])))
            by_t = each(lambda hd: _mm_tn(tn[hd], dvn_dw[hd]))
            tt_dt = each(lambda hd: _mm_tn(tn[hd], dt[hd]))
            dm_raw = each(lambda hd: _mm_nt(tt_dt[hd], tn[hd]))
            masks = each(lambda hd: _chunk_decay(gcc[hd]))
            dkk = each(lambda hd: jnp.where(masks[hd][1], -dm_raw[hd], 0.0) * masks[hd][0])
            dqk = each(lambda hd: da[hd] * masks[hd][0])
            dqk_dkk = each(lambda hd: rows_cat(dqk[hd], dkk[hd]))
            on_k = each(lambda hd: _mm(dqk_dkk[hd], k[hd]))
            dk_mm = each(lambda hd: _mm_tn(dqk_dkk[hd], rows_cat(q[hd], kb[hd])))
            for hd in heads:
                cs = cols[hd]
                dvb, dkbg = by_t[hd][:, :LANES], by_t[hd][:, LANES:]
                gmat = dkk[hd] * by_k[hd][:CHUNK] + dqk[hd] * by_k[hd][CHUNK:]
                dq_ref[rs, cs] = dqd[hd] * eg[hd] + on_k[hd][:CHUNK]
                dkb = on_k[hd][CHUNK:] + dkbg * eg[hd]
                dk_ref[rs, cs] = dkd[hd] * egl[hd] + dk_mm[hd] + dkb * b[hd]
                db = jnp.sum(dkb * k[hd], axis=-1, keepdims=True) + jnp.sum(dvb * v[hd], axis=-1, keepdims=True)
                db_ref[rs, cs] = jnp.broadcast_to(db, (CHUNK, LANES))
                dv_ref[rs, cs] = dvb * b[hd]
                dkd_kd = jnp.sum(dkd[hd] * kd[hd], axis=-1, keepdims=True)
                col_sums = jnp.sum(lane_cat(gmat, jnp.zeros_like(gmat)).T, axis=-1, keepdims=True)
                dgc = (jnp.sum(gmat, axis=-1, keepdims=True) - col_sums[:CHUNK]
                       + jnp.sum(dqd[hd] * qd[hd], axis=-1, keepdims=True)
                       + jnp.sum(dkbg * kbg[hd], axis=-1, keepdims=True) - dkd_kd)
                extra = jnp.sum(dkd_kd, axis=0, keepdims=True) + dgl[hd] * gl[hd]
                dg_ref[rs, cs] = dgc + jnp.where(lastrow, extra, 0.0)
        for hd in heads:
            ds_scr[hd] = dsp[hd]
        dg = dg_ref[...]
        row = _iota(dg.shape, 0)
        pos = row % CHUNK
        step = 1
        while step < CHUNK:
            dg = dg + jnp.where(pos < CHUNK - step, pltpu.roll(dg, dg.shape[0] - step, 0), 0.0)
            step *= 2
        dg_ref[...] = dg

    nb, tok, mat, state = _gdn_block_specs(t, True)
    return pl.pallas_call(
        body, name="gdn_bwd", grid=(nb,),
        in_specs=[tok] * 5 + [mat, mat, state, tok, tok], out_specs=[tok] * 5, out_shape=[_sds((t, D_GDN))] * 5,
        scratch_shapes=[pltpu.VMEM((N_GDN_HEADS, GDN_HEAD_DIM, GDN_HEAD_DIM), F32)],
        compiler_params=_params("arbitrary"),
    )(qn, kn, cv, be, gc, tinv, amat, s_all, vn_all, do)


def _gdn_bwd_conv(proj, convw, dqn, dkn, dcv, dproj):
    t = proj.shape[0]

    def body(xq_ref, xk_ref, xv_ref, wq_ref, wk_ref, wv_ref, dq_ref, dk_ref, dv_ref, _,
             dqkv_ref, dwq_ref, dwk_ref, dwv_ref):
        row = _iota((t, LANES), 0)

        def one(x_ref, w_ref, d_ref, k, dw_ref, scale):
            x = x_ref[...]
            w = w_ref[...]
            y = _conv(x, w, row)
            sg = _sigmoid(y)
            dc = d_ref[...]
            if scale is not None:
                c = y * sg
                r = lax.rsqrt(jnp.sum(c * c, axis=-1, keepdims=True) + EPS)
                ch = c * r
                dc = scale * r * (dc - ch * jnp.sum(dc * ch, axis=-1, keepdims=True))
            dy = dc * sg * (1.0 + y * (1.0 - sg))
            dqkv_ref[:, k * LANES:(k + 1) * LANES] = (
                w[3:4, :] * dy + w[2:3, :] * _shift_up(dy, 1, row) + w[1:2, :] * _shift_up(dy, 2, row)
                + w[0:1, :] * _shift_up(dy, 3, row)).astype(BF)
            for jj in range(CONV_K):
                xs = x if jj == CONV_K - 1 else _shift_down(x, CONV_K - 1 - jj, row)
                dw_ref[jj:jj + 1, :] = jnp.sum(dy * xs, axis=0, keepdims=True)

        one(xq_ref, wq_ref, dq_ref, 0, dwq_ref, GDN_QSCALE)
        one(xk_ref, wk_ref, dk_ref, 1, dwk_ref, 1.0)
        one(xv_ref, wv_ref, dv_ref, 2, dwv_ref, None)

    col, cw, _, qkv = _gdn_specs(t)
    return pl.pallas_call(
        body, name="gdn_bwd_conv", grid=(N_GDN_HEADS,),
        in_specs=[qkv(0), qkv(1), qkv(2), cw(0), cw(4), cw(8), col(0), col(0), col(0), ANY_SPEC],
        out_specs=[pl.BlockSpec((t, QKV), lambda h: (0, COL_GDN // QKV + h)), cw(0), cw(0), cw(0)],
        out_shape=[_sds(dproj.shape, BF)] + [_sds((CONV_K, D_GDN))] * 3,
        input_output_aliases={9: 0}, compiler_params=_params("parallel"),
    )(proj, proj, proj, convw, convw, convw, dqn, dkn, dcv, dproj)


def _mix_out(fox_n, gdn_o, proj, gnw, w_out, x, pmw, plw, after):
    t = x.shape[0]
    tm = min(MATMUL_BLOCK, t)

    def body(fn_ref, go_ref, gz_ref, gnw_ref, w_ref, x_ref, pmw_ref, plw_ref, x1_ref, h2_ref, mixed_ref, omix_ref,
             h2t_ref):
        omix_ref[:, 0:D_FOX] = fn_ref[...]
        for hd in range(N_GDN_HEADS):
            cs = slice(hd * LANES, (hd + 1) * LANES)
            go = go_ref[:, cs]
            r = lax.rsqrt(jnp.mean(go * go, axis=-1, keepdims=True) + EPS)
            gz = gz_ref[:, cs]
            omix_ref[:, D_FOX + hd * LANES:D_FOX + (hd + 1) * LANES] = (
                go * r * gnw_ref[...] * (gz * _sigmoid(gz))).astype(BF)
        mixed = jnp.dot(omix_ref[...], w_ref[...], preferred_element_type=F32)
        mixed_ref[...] = mixed
        r2 = lax.rsqrt(jnp.mean(mixed * mixed, axis=-1, keepdims=True) + EPS)
        x1 = x_ref[...] + mixed * r2 * pmw_ref[...]
        x1_ref[...] = x1
        r3 = lax.rsqrt(jnp.mean(x1 * x1, axis=-1, keepdims=True) + EPS)
        h2 = x1 * r3 * plw_ref[...]
        h2_ref[...] = h2.astype(BF)
        h2t_ref[...] = h2.T.astype(BF)

    tok = lambda w: pl.BlockSpec((tm, w), lambda i: (i, 0))
    vec = lambda w: pl.BlockSpec((1, w), lambda i: (0, 0))
    return pl.pallas_call(
        _ordered(body), name="mix_out", grid=(t // tm,),
        in_specs=[ANY_SPEC, tok(D_FOX), tok(D_GDN), pl.BlockSpec((tm, D_GDN), lambda i: (i, COL_GZ // D_GDN)), vec(LANES),
                  pl.BlockSpec((D_MODEL, D_MODEL), lambda i: (0, 0)), tok(D_MODEL), vec(D_MODEL), vec(D_MODEL)],
        out_specs=[tok(D_MODEL)] * 4 + [pl.BlockSpec((D_MODEL, tm), lambda i: (0, i))],
        out_shape=[_sds((t, D_MODEL)), _sds((t, D_MODEL), BF), _sds((t, D_MODEL)), _sds((t, D_MODEL), BF),
                   _sds((D_MODEL, t), BF)],
        compiler_params=_params("parallel"),
    )(after, fox_n, gdn_o, proj, gnw, w_out, x, pmw, plw)


def _out_bwd(dmixed, w_out, o_fox, gdn_o, proj, fnw, gnw, after):
    t = dmixed.shape[0]
    tm = min(MATMUL_BLOCK, t)

    def body(dm_ref, w_ref, of_ref, go_ref, gz_ref, fnw_ref, gnw_ref, dof_ref, dgo_ref, dgz_ref, dfw_ref, dgw_ref):
        i = pl.program_id(0)

        @pl.when(i == 0)
        def _():
            dfw_ref[...] = jnp.zeros_like(dfw_ref)
            dgw_ref[...] = jnp.zeros_like(dgw_ref)

        domix = _mm_nt(dm_ref[...], w_ref[...])
        first = _iota((1, LANES), 1) < FOX_HEAD_DIM
        dfw = jnp.zeros((1, LANES), F32)
        dgw = jnp.zeros((1, LANES), F32)
        for pr in range(N_FOX_HEADS // 2):
            cs = slice(pr * LANES, (pr + 1) * LANES)
            o = of_ref[:, cs]
            dfn = domix[:, cs]
            o2 = o * o
            s0 = jnp.sum(jnp.where(first, o2, 0.0), axis=-1, keepdims=True)
            s1 = jnp.sum(jnp.where(first, 0.0, o2), axis=-1, keepdims=True)
            r = lax.rsqrt(jnp.where(first, s0, s1) * (1.0 / FOX_HEAD_DIM) + EPS)
            oh = o * r
            dfw = dfw + jnp.sum(dfn * oh, axis=0, keepdims=True)
            doh = dfn * fnw_ref[...]
            pr_ = doh * oh
            m0 = jnp.sum(jnp.where(first, pr_, 0.0), axis=-1, keepdims=True)
            m1 = jnp.sum(jnp.where(first, 0.0, pr_), axis=-1, keepdims=True)
            dof_ref[:, cs] = r * (doh - oh * jnp.where(first, m0, m1) * (1.0 / FOX_HEAD_DIM))
        for hd in range(N_GDN_HEADS):
            cs = slice(hd * LANES, (hd + 1) * LANES)
            go = go_ref[:, cs]
            gz = gz_ref[:, cs]
            dgated = domix[:, D_FOX + hd * LANES:D_FOX + (hd + 1) * LANES]
            r = lax.rsqrt(jnp.mean(go * go, axis=-1, keepdims=True) + EPS)
            goh = go * r
            sg = _sigmoid(gz)
            sz = gz * sg
            gn = goh * gnw_ref[...]
            dgn = dgated * sz
            dgz_ref[:, cs] = (dgated * gn * sg * (1.0 + gz * (1.0 - sg))).astype(BF)
            dgw = dgw + jnp.sum(dgn * goh, axis=0, keepdims=True)
            dgh = dgn * gnw_ref[...]
            dgo_ref[:, cs] = r * (dgh - goh * jnp.mean(dgh * goh, axis=-1, keepdims=True))
        dfw_ref[...] += dfw + pltpu.roll(dfw, FOX_HEAD_DIM, 1)
        dgw_ref[...] += dgw

    tok = lambda w: pl.BlockSpec((tm, w), lambda i: (i, 0))
    vec = lambda w: pl.BlockSpec((1, w), lambda i: (0, 0))
    return pl.pallas_call(
        _ordered(body), name="out_bwd", grid=(t // tm,),
        in_specs=[ANY_SPEC, tok(D_MODEL), pl.BlockSpec((D_MODEL, D_MODEL), lambda i: (0, 0)), tok(D_FOX), tok(D_GDN),
                  pl.BlockSpec((tm, D_GDN), lambda i: (i, COL_GZ // D_GDN)), vec(LANES), vec(LANES)],
        out_specs=[tok(D_FOX), tok(D_GDN), pl.BlockSpec((tm, D_GDN), lambda i: (i, COL_GZ // D_GDN)), vec(LANES),
                   vec(LANES)],
        out_shape=[_sds((t, D_FOX)), _sds((t, D_GDN)), _sds((t, PROJ_W), BF), _sds((1, LANES)), _sds((1, LANES))],
        compiler_params=_params("arbitrary"),
    )(after, dmixed, w_out, o_fox, gdn_o, proj, fnw, gnw)


def _mlp_up(h2, w_upt):
    t = h2.shape[0]
    tm = min(MATMUL_BLOCK, t)

    def body(h_ref, w_ref, up_ref):
        up_ref[...] = lax.dot_general(h_ref[...], w_ref[...], (((1,), (1,)), ((), ())),
                                      preferred_element_type=F32).astype(BF)

    return pl.pallas_call(
        body, name="mlp_up", grid=(t // tm,),
        in_specs=[pl.BlockSpec((tm, D_MODEL), lambda i: (i, 0)), pl.BlockSpec((D_FF, D_MODEL), lambda i: (0, 0))],
        out_specs=pl.BlockSpec((tm, D_FF), lambda i: (i, 0)), out_shape=_sds((t, D_FF), BF),
        compiler_params=_params("parallel"),
    )(h2, w_upt)


def _mlp_down_loss(up, w_down, x1, pw, target):
    t = up.shape[0]
    tm = min(MATMUL_BLOCK, t)

    def body(up_ref, w_ref, x1_ref, pw_ref, tg_ref, dy_ref, dx2_ref, loss_ref, dpw_ref):
        i = pl.program_id(0)

        @pl.when(i == 0)
        def _():
            loss_ref[...] = jnp.zeros_like(loss_ref)
            dpw_ref[...] = jnp.zeros_like(dpw_ref)

        u = jnp.maximum(up_ref[...].astype(F32), 0.0)
        y = jnp.dot((u * u).astype(BF), w_ref[...], preferred_element_type=F32)
        r = lax.rsqrt(jnp.mean(y * y, axis=-1, keepdims=True) + EPS)
        yh = y * r
        pw = pw_ref[...]
        err = x1_ref[...] + yh * pw - tg_ref[...]
        part = jnp.sum(jnp.sum(err * err, axis=-1, keepdims=True), axis=0, keepdims=True) * (0.5 / D_MODEL)
        loss_ref[...] += jnp.broadcast_to(part, loss_ref.shape)
        dx2 = err * (1.0 / D_MODEL)
        dx2_ref[...] = dx2
        dpw_ref[...] += jnp.sum(dx2 * yh, axis=0, keepdims=True)
        dyh = dx2 * pw
        dy_ref[...] = (r * (dyh - yh * jnp.mean(dyh * yh, axis=-1, keepdims=True))).astype(BF)

    tok = lambda w: pl.BlockSpec((tm, w), lambda i: (i, 0))
    vec = lambda w: pl.BlockSpec((1, w), lambda i: (0, 0))
    return pl.pallas_call(
        body, name="mlp_down_loss", grid=(t // tm,),
        in_specs=[tok(D_FF), pl.BlockSpec((D_FF, D_MODEL), lambda i: (0, 0)), tok(D_MODEL), vec(D_MODEL), tok(D_MODEL)],
        out_specs=[tok(D_MODEL), tok(D_MODEL), vec(LANES), vec(D_MODEL)],
        out_shape=[_sds((t, D_MODEL), BF), _sds((t, D_MODEL)), _sds((1, LANES)), _sds((1, D_MODEL))],
        compiler_params=_params("arbitrary"),
    )(up, w_down, x1, pw, target)


def _mlp_bwd_act(dy, w_down, up):
    t = dy.shape[0]
    tm = min(MATMUL_BLOCK, t)

    def body(dy_ref, w_ref, up_ref, dup_ref):
        da = lax.dot_general(dy_ref[...], w_ref[...], (((1,), (1,)), ((), ())), preferred_element_type=F32)
        dup_ref[...] = (da * (2.0 * jnp.maximum(up_ref[...].astype(F32), 0.0))).astype(BF)

    return pl.pallas_call(
        body, name="mlp_bwd_act", grid=(t // tm,),
        in_specs=[pl.BlockSpec((tm, D_MODEL), lambda i: (i, 0)), pl.BlockSpec((D_FF, D_MODEL), lambda i: (0, 0)),
                  pl.BlockSpec((tm, D_FF), lambda i: (i, 0))],
        out_specs=pl.BlockSpec((tm, D_FF), lambda i: (i, 0)), out_shape=_sds((t, D_FF), BF),
        compiler_params=_params("parallel"),
    )(dy, w_down, up)


def _mlp_bwd_in(dup, w_up, x1, plw, dx2, mixed, pmw, after):
    t = dup.shape[0]
    tm = min(MATMUL_BLOCK, t)

    def body(dup_ref, w_ref, x1_ref, plw_ref, dx2_ref, mx_ref, pmw_ref, dx1_ref, dmixed_ref, dplw_ref, dpmw_ref):
        i = pl.program_id(0)

        @pl.when(i == 0)
        def _():
            dplw_ref[...] = jnp.zeros_like(dplw_ref)
            dpmw_ref[...] = jnp.zeros_like(dpmw_ref)

        dh = jnp.dot(dup_ref[...], w_ref[...], preferred_element_type=F32)
        x1 = x1_ref[...]
        r = lax.rsqrt(jnp.mean(x1 * x1, axis=-1, keepdims=True) + EPS)
        xh = x1 * r
        dplw_ref[...] += jnp.sum(dh * xh, axis=0, keepdims=True)
        dxh = dh * plw_ref[...]
        dx1 = dx2_ref[...] + r * (dxh - xh * jnp.mean(dxh * xh, axis=-1, keepdims=True))
        dx1_ref[...] = dx1
        mx = mx_ref[...]
        r2 = lax.rsqrt(jnp.mean(mx * mx, axis=-1, keepdims=True) + EPS)
        mh = mx * r2
        dpmw_ref[...] += jnp.sum(dx1 * mh, axis=0, keepdims=True)
        dmh = dx1 * pmw_ref[...]
        dmixed_ref[...] = (r2 * (dmh - mh * jnp.mean(dmh * mh, axis=-1, keepdims=True))).astype(BF)

    tok = lambda w: pl.BlockSpec((tm, w), lambda i: (i, 0))
    vec = lambda w: pl.BlockSpec((1, w), lambda i: (0, 0))
    return pl.pallas_call(
        _ordered(body), name="mlp_bwd_in", grid=(t // tm,),
        in_specs=[ANY_SPEC, tok(D_FF), pl.BlockSpec((D_FF, D_MODEL), lambda i: (0, 0)), tok(D_MODEL),
                  vec(D_MODEL), tok(D_MODEL), tok(D_MODEL), vec(D_MODEL)],
        out_specs=[tok(D_MODEL), tok(D_MODEL), vec(D_MODEL), vec(D_MODEL)],
        out_shape=[_sds((t, D_MODEL)), _sds((t, D_MODEL), BF), _sds((1, D_MODEL)), _sds((1, D_MODEL))],
        compiler_params=_params("arbitrary"),
    )(after, dup, w_up, x1, plw, dx2, mixed, pmw)


def _wgrad(a, b, a_cols, split=1, a_fn=None, a_block0=0, name="wgrad"):
    t, b_cols = b.shape
    n_a = (a.shape[1] - a_block0 * a_cols) // a_cols if a_block0 else a.shape[1] // a_cols

    def body(a_ref, b_ref, o_ref):
        av = a_ref[...]
        if a_fn is not None:
            av = a_fn(av)
        o_ref[...] = _mm_tn(av, b_ref[...]).astype(BF).reshape(o_ref.shape)

    return pl.pallas_call(
        body, name=name, grid=(n_a,),
        in_specs=[pl.BlockSpec((t, a_cols), lambda i: (0, i + a_block0)), pl.BlockSpec((t, b_cols), lambda i: (0, 0))],
        out_specs=pl.BlockSpec((split, a_cols // split, b_cols), lambda i: (i, 0, 0)),
        out_shape=_sds((n_a * split, a_cols // split, b_cols), BF),
        compiler_params=_params("parallel"),
    )(a, b)


def _wgrad_pre_t(at, b, b_cols, name):
    rows, t = at.shape
    n_b = b.shape[1] // b_cols

    def body(a_ref, b_ref, o_ref):
        o_ref[0] = jnp.dot(a_ref[...], b_ref[...], preferred_element_type=F32).astype(BF)

    return pl.pallas_call(
        body, name=name, grid=(n_b,),
        in_specs=[pl.BlockSpec((rows, t), lambda j: (0, 0)), pl.BlockSpec((t, b_cols), lambda j: (0, j))],
        out_specs=pl.BlockSpec((1, rows, b_cols), lambda j: (j, 0, 0)), out_shape=_sds((n_b, rows, b_cols), BF),
        compiler_params=_params("parallel"),
    )(at, b)


def _small_bwd(proj, fb, al, dtb, dcq, dckt, dbe, dge, dproj):
    t = proj.shape[0]

    def body(sm_ref, fb_ref, al_ref, dtb_ref, dcq_ref, dckt_ref, dbe_ref, dge_ref, _, dsm_ref, dvec_ref):
        s = sm_ref[...]
        lane = _iota((1, LANES), 1)
        dcum = dcq_ref[...] - dckt_ref[...].T
        row = _iota((t, LANES), 0)
        step = 1
        while step < t:
            dcum = dcum + _shift_up(dcum, step, row)
            step *= 2
        dff = dcum * _sigmoid(-(s + fb_ref[...]))
        dbeta = jnp.zeros((t, LANES), F32)
        dg = jnp.zeros((t, LANES), F32)
        for hd in range(N_GDN_HEADS):
            dbeta = jnp.where(lane == SM_GB + hd, dbe_ref[:, hd * LANES:hd * LANES + 1], dbeta)
            dg = jnp.where(lane == SM_GA + hd, dge_ref[:, hd * LANES:hd * LANES + 1], dg)
        beta = _sigmoid(s)
        dgb = dbeta * beta * (1.0 - beta)
        za = s + dtb_ref[...]
        nea = -jnp.exp(al_ref[...])
        dga = dg * nea * _sigmoid(za)
        is_f = lane < SM_GB
        is_b = (lane >= SM_GB) & (lane < SM_GA)
        is_a = (lane >= SM_GA) & (lane < SM_GA + 4)
        dsm_ref[...] = jnp.where(is_f, dff, jnp.where(is_b, dgb, jnp.where(is_a, dga, 0.0))).astype(BF)
        dvec_ref[...] = jnp.zeros_like(dvec_ref)
        dvec_ref[0:1, :] = jnp.sum(jnp.where(is_f, dff, 0.0), axis=0, keepdims=True)
        dvec_ref[1:2, :] = jnp.sum(jnp.where(is_a, dg * nea * _softplus(za), 0.0), axis=0, keepdims=True)
        dvec_ref[2:3, :] = jnp.sum(jnp.where(is_a, dga, 0.0), axis=0, keepdims=True)

    vec = pl.BlockSpec((1, LANES), lambda i: (0, 0))
    full = lambda r, c: pl.BlockSpec((r, c), lambda i: (0, 0))
    small = pl.BlockSpec((t, LANES), lambda i: (0, COL_SMALL // LANES))
    return pl.pallas_call(
        body, name="small_bwd", grid=(1,),
        in_specs=[small, vec, vec, vec, full(t, LANES), full(LANES, t), full(t, 512), full(t, 512), ANY_SPEC],
        out_specs=[small, full(8, LANES)], out_shape=[_sds(dproj.shape, BF), _sds((8, LANES))],
        input_output_aliases={8: 0}, compiler_params=_params("arbitrary"),
    )(proj, fb, al, dtb, dcq, dckt, dbe, dge, dproj)


def _in_bwd(dproj, wt_al, x, nw, dx1, after):
    t = x.shape[0]
    tm = min(MATMUL_BLOCK, t)

    def body(dp_ref, w_ref, x_ref, nw_ref, dx1_ref, dx_ref, dnw_ref):
        i = pl.program_id(0)

        @pl.when(i == 0)
        def _():
            dnw_ref[...] = jnp.zeros_like(dnw_ref)

        dh = jnp.dot(dp_ref[...], w_ref[...], preferred_element_type=F32)
        xv = x_ref[...]
        r = lax.rsqrt(jnp.mean(xv * xv, axis=-1, keepdims=True) + EPS)
        xh = xv * r
        dnw_ref[...] += jnp.sum(dh * xh, axis=0, keepdims=True)
        dxh = dh * nw_ref[...]
        dx_ref[...] = dx1_ref[...] + r * (dxh - xh * jnp.mean(dxh * xh, axis=-1, keepdims=True))

    tok = lambda w: pl.BlockSpec((tm, w), lambda i: (i, 0))
    vec = lambda w: pl.BlockSpec((1, w), lambda i: (0, 0))
    return pl.pallas_call(
        _ordered(body), name="in_bwd", grid=(t // tm,),
        in_specs=[ANY_SPEC, tok(PROJ_W), pl.BlockSpec((PROJ_W, D_MODEL), lambda i: (0, 0)), tok(D_MODEL), vec(D_MODEL),
                  tok(D_MODEL)],
        out_specs=[tok(D_MODEL), vec(D_MODEL)], out_shape=[_sds((t, D_MODEL)), _sds((1, D_MODEL))],
        compiler_params=_params("arbitrary"),
    )(after, dproj, wt_al, x, nw, dx1)


def _row(v, width=None):
    v = v.reshape(1, -1).astype(F32)
    if width is not None and v.shape[1] < width:
        v = jnp.pad(v, ((0, 0), (0, width - v.shape[1])))
    return v


def _lane_vec(v, first):
    return jnp.pad(v.astype(F32), (first, LANES - first - v.shape[0])).reshape(1, LANES)


def _local_step(x, target, wt_al, started, late_weights, on_grads, convw, pre_mix_norm, fox_f_bias, fox_out_norm,
                gdn_a_log, gdn_dt_bias, gdn_out_norm, post_mix_norm, pre_mlp_norm, post_mlp_norm):
    t = x.shape[0]
    nch = t // CHUNK
    nw, pmw, plw, pw = _row(pre_mix_norm), _row(post_mix_norm), _row(pre_mlp_norm), _row(post_mlp_norm)
    fb, al, dtb = _lane_vec(fox_f_bias, SM_FF), _lane_vec(gdn_a_log, SM_GA), _lane_vec(gdn_dt_bias, SM_GA)
    fnw = _row(jnp.tile(fox_out_norm, 2))
    gnw = _row(gdn_out_norm)

    proj, h = _norm_proj(x, nw, wt_al, started)
    cumt, beta, g = _small_prep(proj, fb, al, dtb)
    o_fox, lse, fox_n = _fox_fwd(proj, cumt, fnw)
    qn, kn, cv, gc, be, mmat, amat = _gdn_prep(proj, convw, beta, g)
    n_prob = N_GDN_HEADS * nch
    m3 = mmat.reshape(n_prob, CHUNK, CHUNK)
    if n_prob < LANES:
        m3 = jnp.pad(m3, ((0, LANES - n_prob), (0, 0), (0, 0)))
    tinv = _tri_inverse(m3)[:n_prob].reshape(N_GDN_HEADS, nch, CHUNK, CHUNK)
    token = late_weights("mlp_relay", tinv)
    gdn_o, s_all, vn_all = _gdn_scan(qn, kn, cv, be, gc, tinv, amat)
    w_out = late_weights("w_out", gdn_o)
    x1, h2, mixed, omix, h2t = _mix_out(fox_n, gdn_o, proj, gnw, w_out, x, pmw, plw, token)
    w_up, w_down = late_weights("mlp", h2)
    up = _mlp_up(h2, w_up)
    dy, dx2, loss, d_pw = _mlp_down_loss(up, w_down, x1, pw, target)

    dup = _mlp_bwd_act(dy, w_down, up)
    relu2 = lambda u: jnp.square(jnp.maximum(u.astype(F32), 0.0))
    g_down = _wgrad(up, dy, D_FF // N_DEV, a_fn=relu2, name="wgrad_down")
    g_up = _wgrad_pre_t(h2t, dup, D_FF // N_DEV, name="wgrad_up")
    token = on_grads("mlp", (g_up, g_down))
    dx1, dmixed, d_plw, d_pmw = _mlp_bwd_in(dup, w_up, x1, plw, dx2, mixed, pmw, token)
    token = on_grads("w_out", _wgrad(omix, dmixed, 512, split=4, name="wgrad_out"))
    do_fox, dgo, dproj, d_fnw, d_gnw = _out_bwd(dmixed, w_out, o_fox, gdn_o, proj, fnw, gnw, token)
    dproj, dcq, dckt = _fox_bwd(proj, cumt, lse, o_fox, do_fox, dproj)
    dqn, dkn, dcv, dbe, dge = _gdn_bwd(qn, kn, cv, be, gc, tinv, amat, s_all, vn_all, dgo)
    dproj, dwq, dwk, dwv = _gdn_bwd_conv(proj, convw, dqn, dkn, dcv, dproj)
    dproj, dvec = _small_bwd(proj, fb, al, dtb, dcq, dckt, dbe, dge, dproj)
    g_main = _wgrad(dproj, h, WGRAD_IN_ROWS, name="wgrad_in")
    g_tail = _wgrad(dproj, h, LANES, a_block0=COL_SMALL // LANES, name="wgrad_in_small")
    token = on_grads("w_in", (g_main, g_tail))
    grad_x, d_nw = _in_bwd(dproj, wt_al, x, nw, dx1, token)
    small = dict(norms=(d_nw, d_pmw, d_plw, d_pw), fox_out_norm=d_fnw, gdn_out_norm=d_gnw, loss=loss, vectors=dvec,
                 conv=(dwq, dwk, dwv))
    return grad_x, small


MESH_IDS = pl.DeviceIdType.MESH
CHIP_FLIPS = ((0, 0), (1, 0), (0, 1), (1, 1))


def _place():
    return lax.axis_index("x"), lax.axis_index("y"), lax.axis_index("c")


def _all_gather(blocks):
    n = len(blocks)

    def body(*refs):
        ins, outs, (send_sems, recv_sems, local_sems) = refs[:n], refs[n:2 * n], refs[2 * n:]
        x, y, c = _place()
        sibling = (x, y, 1 - c)
        chips = [(x ^ fx, y ^ fy) for fx, fy in CHIP_FLIPS[1:]]

        def slot(out, px, py, pc):
            return out.at[4 * px + 2 * py + pc]

        def copy(a, k, block, to, src=None):
            return pltpu.make_async_remote_copy(
                src_ref=slot(outs[a], *block) if src is None else src, dst_ref=slot(outs[a], *block),
                send_sem=send_sems.at[a, k], recv_sem=recv_sems.at[a, k], device_id=to, device_id_type=MESH_IDS)

        pending = []
        for a in range(n):
            mine = pltpu.make_async_copy(ins[a], slot(outs[a], x, y, c), local_sems.at[a])
            mine.start()
            pending.append(mine)
        sends = []
        for a in range(n):
            first = [copy(a, 0, (x, y, c), sibling, src=ins[a])]
            first += [copy(a, 1 + j, (x, y, c), (*chip, c), src=ins[a]) for j, chip in enumerate(chips)]
            for cp in first:
                cp.start()
            sends += first
        for a in range(n):
            for j, chip in enumerate(chips):
                copy(a, 1 + j, (*chip, c), (x, y, c)).wait_recv()
                fwd = copy(a, 4 + j, (*chip, c), sibling)
                fwd.start()
                sends.append(fwd)
        for a in range(n):
            copy(a, 0, sibling, (x, y, c)).wait_recv()
            for j, chip in enumerate(chips):
                copy(a, 4 + j, (*chip, 1 - c), (x, y, c)).wait_recv()
        for cp in sends:
            cp.wait_send()
        for cp in pending:
            cp.wait()

    return pl.pallas_call(
        body, name="all_gather_weights", in_specs=[ANY_SPEC] * n, out_specs=[ANY_SPEC] * n,
        out_shape=[_sds((N_DEV,) + b.shape, b.dtype) for b in blocks],
        scratch_shapes=[pltpu.SemaphoreType.DMA((n, 7)), pltpu.SemaphoreType.DMA((n, 7)), pltpu.SemaphoreType.DMA((n,))],
        compiler_params=pltpu.CompilerParams(has_side_effects=True),
    )(*blocks)


def _adamw(w, g, m, v):
    m = ADAM_B1 * m + (1.0 - ADAM_B1) * g
    v = ADAM_B2 * v + (1.0 - ADAM_B2) * (g * g)
    m_hat = m / (1.0 - ADAM_B1 ** ADAM_STEP)
    v_hat = v / (1.0 - ADAM_B2 ** ADAM_STEP)
    return -ADAM_LR * (m_hat / (jnp.sqrt(v_hat) + ADAM_EPS) + ADAM_WD * w), m, v


def _pair_reduce(g, name):
    _, r, c_ = g.shape
    n = len(CHIP_FLIPS)

    def body(g_ref, out_ref, sib_buf, send_sems, recv_sems):
        x, y, c = _place()
        chips = [(x ^ fx, y ^ fy) for fx, fy in CHIP_FLIPS]
        piece = lambda chip, core: g_ref.at[4 * chip[0] + 2 * chip[1] + core]
        copies = [pltpu.make_async_remote_copy(
            src_ref=piece(chip, 1 - c), dst_ref=sib_buf.at[j], send_sem=send_sems.at[j], recv_sem=recv_sems.at[j],
            device_id=(x, y, 1 - c), device_id_type=MESH_IDS) for j, chip in enumerate(chips)]
        for cp in copies:
            cp.start()
        for j, chip in enumerate(chips):
            copies[j].wait_recv()
            out_ref[j] = (piece(chip, c)[...].astype(F32) + sib_buf[j].astype(F32)).astype(BF)
        for cp in copies:
            cp.wait_send()

    return pl.pallas_call(
        body, name=name, in_specs=[VMEM_SPEC], out_specs=VMEM_SPEC, out_shape=_sds((n, r, c_), BF),
        scratch_shapes=[pltpu.VMEM((n, r, c_), BF), pltpu.SemaphoreType.DMA((n,)), pltpu.SemaphoreType.DMA((n,))],
        compiler_params=pltpu.CompilerParams(vmem_limit_bytes=VMEM_LIMIT, has_side_effects=True),
    )(g)


HBM_SPEC = pl.BlockSpec(memory_space=pltpu.HBM)
SEM_SPEC = pl.BlockSpec(memory_space=pltpu.SEMAPHORE)
DATAFLOW = pltpu.SideEffectType.DATAFLOW_SIDE_EFFECTING


def _peers():
    x, y, c = _place()
    return 4 * x + 2 * y + c, [(x ^ (k >> 2), y ^ ((k >> 1) & 1), c ^ (k & 1)) for k in range(1, N_DEV)]


def _peer_index(peer):
    return 4 * peer[0] + 2 * peer[1] + peer[2]


def _zones_with_own(srcs, name, after, dtype):
    n = len(srcs)

    def body(me_ref, *refs):
        outs = refs[n + 1:]
        for a in range(n):
            val = refs[a][...].astype(dtype)
            outs[a][0] = val
            outs[n + a][...] = val

    shapes = [s_.shape for s_ in srcs]
    mine = lambda sh: pl.BlockSpec((1,) + sh, lambda i, me_ref: (me_ref[0], 0, 0))
    whole = lambda sh: pl.BlockSpec(sh, lambda i, me_ref: (0, 0))
    x, y, c = _place()
    out = pl.pallas_call(
        body, name=name,
        grid_spec=pltpu.PrefetchScalarGridSpec(
            num_scalar_prefetch=1, grid=(1,), in_specs=[whole(sh) for sh in shapes] + [ANY_SPEC],
            out_specs=[mine(sh) for sh in shapes] + [whole(sh) for sh in shapes]),
        out_shape=[_sds((N_DEV,) + sh, dtype) for sh in shapes] + [_sds(sh, dtype) for sh in shapes],
        compiler_params=_params("arbitrary"),
    )((4 * x + 2 * y + c).astype(jnp.int32).reshape(1), *srcs, after)
    return out[:n], out[n:]


def _exchange_start(srcs, zones, pieces, name, chips=False):
    n = len(srcs)
    fresh = zones is None
    if fresh:
        slots = len(CHIP_FLIPS) if chips else N_DEV
        zones = [_sds((slots,) + (v.shape[1:] if pieces else v.shape), v.dtype) for v in srcs]
    n_in = n if fresh else 2 * n

    def body(*refs):
        ins, sems, token = refs[:n], refs[n_in:n_in + 2 * n], refs[-1]
        zs = refs[n_in + 3 * n:n_in + 4 * n] if fresh else refs[n:2 * n]
        me, peers = _peers()
        x, y, c = _place()
        if chips and pieces:
            routes = [((x ^ fx, y ^ fy, c), j, j) for j, (fx, fy) in enumerate(CHIP_FLIPS) if j]
        elif chips:
            routes = [((x ^ fx, y ^ fy, c), None, me) for fx, fy in CHIP_FLIPS[1:]]
        else:
            routes = [(peer, _peer_index(peer) if pieces else None, me) for peer in peers]
        for peer, src_slot, dst_slot in routes:
            for a in range(n):
                pltpu.make_async_remote_copy(
                    src_ref=ins[a] if src_slot is None else ins[a].at[src_slot], dst_ref=zs[a].at[dst_slot],
                    send_sem=sems[2 * a], recv_sem=sems[2 * a + 1], device_id=peer, device_id_type=MESH_IDS).start()
        token[...] = jnp.zeros_like(token)

    hbm = lambda v: pltpu.with_memory_space_constraint(v, pltpu.HBM)
    out = pl.pallas_call(
        body, name=name,
        out_shape=tuple([pltpu.SemaphoreType.DMA(())] * (2 * n) + [pltpu.HBM(v.shape, v.dtype) for v in srcs]
                        + [pltpu.HBM(z.shape, z.dtype) for z in zones] + [_sds((8, LANES))]),
        in_specs=[HBM_SPEC] * n_in, out_specs=tuple([SEM_SPEC] * (2 * n) + [HBM_SPEC] * (2 * n) + [VMEM_SPEC]),
        input_output_aliases={i: 2 * n + i for i in range(n_in)},
        compiler_params=pltpu.CompilerParams(has_side_effects=DATAFLOW),
    )(*[hbm(v) for v in srcs], *([] if fresh else [hbm(z) for z in zones]))
    return out[:2 * n], out[2 * n:3 * n], out[3 * n:4 * n], out[-1]


def _relay_start(zones, name):
    n = len(zones)

    def body(*refs):
        zs, sems, token = refs[:n], refs[n:3 * n], refs[-1]
        x, y, c = _place()
        for fx, fy in CHIP_FLIPS:
            slot = 4 * (x ^ fx) + 2 * (y ^ fy) + c
            for a in range(n):
                pltpu.make_async_remote_copy(
                    src_ref=zs[a].at[slot], dst_ref=zs[a].at[slot], send_sem=sems[2 * a], recv_sem=sems[2 * a + 1],
                    device_id=(x, y, 1 - c), device_id_type=MESH_IDS).start()
        token[...] = jnp.zeros_like(token)

    out = pl.pallas_call(
        body, name=name,
        out_shape=tuple([pltpu.SemaphoreType.DMA(())] * (2 * n) + [pltpu.HBM(z.shape, z.dtype) for z in zones]
                        + [_sds((8, LANES))]),
        in_specs=[HBM_SPEC] * n, out_specs=tuple([SEM_SPEC] * (2 * n) + [HBM_SPEC] * n + [VMEM_SPEC]),
        input_output_aliases={i: 2 * n + i for i in range(n)},
        compiler_params=pltpu.CompilerParams(has_side_effects=DATAFLOW),
    )(*[pltpu.with_memory_space_constraint(z, pltpu.HBM) for z in zones])
    return out[:2 * n], [], out[2 * n:3 * n], out[-1]


def _exchange_wait(sems, srcs, zones, after, name, chips=False, n_copies=None):
    n, n_src = len(zones), len(srcs)
    after = list(after) if isinstance(after, (list, tuple)) else [after]
    n_copies = n_copies or (len(CHIP_FLIPS) - 1 if chips else N_DEV - 1)

    def body(*refs):
        zs, sm = refs[n_src:n_src + n], refs[n_src + n:n_src + 3 * n]
        me, peers = _peers()
        for a in range(n):
            seven = zs[a].at[pl.ds(0, n_copies)]
            cp = pltpu.make_async_remote_copy(src_ref=seven, dst_ref=seven, send_sem=sm[2 * a], recv_sem=sm[2 * a + 1],
                                              device_id=peers[0], device_id_type=MESH_IDS)
            cp.wait_send()
            cp.wait_recv()

    out = pl.pallas_call(
        body, name=name, out_shape=tuple([pltpu.HBM(v.shape, v.dtype) for v in srcs] + [pltpu.HBM(z.shape, z.dtype) for z in zones]),
        in_specs=[HBM_SPEC] * (n_src + n) + [SEM_SPEC] * (2 * n) + [ANY_SPEC] * len(after),
        out_specs=tuple([HBM_SPEC] * (n_src + n)), input_output_aliases={i: i for i in range(n_src + n)},
        compiler_params=pltpu.CompilerParams(has_side_effects=DATAFLOW),
    )(*srcs, *zones, *sems, *after)
    return out[:n_src], out[n_src:]


def _sum_adamw(zone, own, w, m, v, name, chips=False):
    n_slots, r, c_ = zone.shape
    rb = next((b for b in (256, 128) if r % b == 0), r)

    def body(me_ref, z_ref, own_ref, w_ref, m_ref, v_ref, grad_ref, delta_ref, nm_ref, nv_ref):
        total = None
        for d in range(n_slots):
            part = jnp.where(me_ref[0] == d, own_ref[0], z_ref[d]).astype(F32)
            total = part if total is None else total + part
        grad_ref[...] = total
        delta_ref[...], nm_ref[...], nv_ref[...] = _adamw(w_ref[...], total, m_ref[...], v_ref[...])

    x, y, c = _place()
    mine = 0 * x if chips else 4 * x + 2 * y + c
    blk = pl.BlockSpec((rb, c_), lambda i, me_ref: (i, 0))
    return pl.pallas_call(
        body, name=name,
        grid_spec=pltpu.PrefetchScalarGridSpec(
            num_scalar_prefetch=1, grid=(r // rb,),
            in_specs=[pl.BlockSpec((n_slots, rb, c_), lambda i, me_ref: (0, i, 0)),
                      pl.BlockSpec((1, rb, c_), lambda i, me_ref: (me_ref[0], i, 0)), blk, blk, blk],
            out_specs=[blk] * 4),
        out_shape=[_sds((r, c_))] * 4, compiler_params=_params("parallel"),
    )(mine.astype(jnp.int32).reshape(1), zone, own, w, m, v)


SMALL_NORMS = ("pre_mix_norm", "post_mix_norm", "pre_mlp_norm", "post_mlp_norm")
SMALL_ORDER = SMALL_NORMS + ("fox_out_norm", "gdn_out_norm", "fox_f_bias", "gdn_a_log", "gdn_dt_bias", "gdn_conv_w")
CONV_SLAB_ROWS, CONV_SLAB_LANES = 8, 256


def _small_pack(small):
    def body(n0, n1, n2, n3, fnw_ref, gnw_ref, loss_ref, vec_ref, out_ref):
        out_ref[...] = jnp.zeros_like(out_ref)
        for i, ref in enumerate((n0, n1, n2, n3)):
            out_ref[i:i + 1, :] = ref[...]
        out_ref[4:5, 0:LANES] = fnw_ref[...]
        out_ref[4:5, LANES:2 * LANES] = gnw_ref[...]
        out_ref[4:5, 2 * LANES:3 * LANES] = loss_ref[...]
        out_ref[5:8, 0:LANES] = vec_ref[0:3, :]

    return pl.pallas_call(body, name="small_pack", in_specs=[VMEM_SPEC] * 8, out_specs=VMEM_SPEC,
                          out_shape=_sds((8, D_MODEL)))(*small["norms"], small["fox_out_norm"], small["gdn_out_norm"],
                                                        small["loss"], small["vectors"])


def _conv_slabs(dconv):
    blocks = dconv.reshape(CONV_K, N_DEV, -1).transpose(1, 0, 2)
    blocks = jnp.pad(blocks, ((0, 0), (0, CONV_SLAB_ROWS - CONV_K), (0, CONV_SLAB_LANES - blocks.shape[2])))
    return blocks.reshape(N_DEV * CONV_SLAB_ROWS, CONV_SLAB_LANES)


def _small_update(zone, conv_zone, own, own_conv, w, m, v):
    n = len(SMALL_ORDER)
    n_conv = w["gdn_conv_w"].shape[1]

    def body(me_ref, z_ref, zc_ref, own_ref, ownc_ref, *refs):
        params, loss_ref, outs, (tot, totc) = refs[:3 * n], refs[3 * n], refs[3 * n + 1:7 * n + 1], refs[-2:]
        total, total_c = None, None
        for d in range(N_DEV):
            part = jnp.where(me_ref[0] == d, own_ref[...], z_ref[d])
            part_c = jnp.where(me_ref[0] == d, ownc_ref[...], zc_ref[d])
            total, total_c = (part, part_c) if d == 0 else (total + part, total_c + part_c)
        tot[...] = total
        totc[...] = total_c
        loss_ref[...] = tot[4, 2 * LANES:2 * LANES + 1]
        mine = totc[pl.ds(pl.multiple_of(me_ref[0] * CONV_SLAB_ROWS, CONV_SLAB_ROWS), CONV_SLAB_ROWS), :]
        g = dict(zip(SMALL_NORMS, (tot[0], tot[1], tot[2], tot[3])))
        g.update(fox_out_norm=tot[4, 0:FOX_HEAD_DIM], gdn_out_norm=tot[4, LANES:LANES + GDN_HEAD_DIM],
                 fox_f_bias=tot[5, SM_FF:SM_FF + N_FOX_HEADS], gdn_a_log=tot[6, SM_GA:SM_GA + N_GDN_HEADS],
                 gdn_dt_bias=tot[7, SM_GA:SM_GA + N_GDN_HEADS], gdn_conv_w=mine[0:CONV_K, 0:n_conv])
        for i, name in enumerate(SMALL_ORDER):
            w_ref, m_ref, v_ref = params[3 * i:3 * i + 3]
            outs[4 * i][...] = g[name]
            outs[4 * i + 1][...], outs[4 * i + 2][...], outs[4 * i + 3][...] = _adamw(w_ref[...], g[name], m_ref[...],
                                                                                     v_ref[...])

    x, y, c = _place()
    operands = [a[name] for name in SMALL_ORDER for a in (w, m, v)]
    out = pl.pallas_call(
        body, name="small_update",
        in_specs=[pl.BlockSpec(memory_space=pltpu.SMEM)] + [VMEM_SPEC] * (4 + 3 * n), out_specs=[VMEM_SPEC] * (1 + 4 * n),
        out_shape=[_sds((1,))] + [_sds(w[name].shape) for name in SMALL_ORDER for _ in range(4)],
        scratch_shapes=[pltpu.VMEM(zone.shape[1:], F32), pltpu.VMEM(conv_zone.shape[1:], F32)],
    )((4 * x + 2 * y + c).astype(jnp.int32).reshape(1), zone, conv_zone, own, own_conv, *operands)
    return out[0][0], {name: out[1 + 4 * i:5 + 4 * i] for i, name in enumerate(SMALL_ORDER)}


def _native_rows():
    groups = []
    for first, n_groups in ((0, N_FOX_HEADS // 2), (D_FOX * 3 + N_FOX_HEADS, N_GDN_HEADS)):
        for g in range(n_groups):
            groups += [(first + part * n_groups * LANES + g * LANES, first + part * n_groups * LANES + (g + 1) * LANES)
                       for part in range(3)]
    return tuple(groups) + ((3088, 3600), (1536, 1544), (3080, 3088))


NATIVE_ROWS = _native_rows()


W_IN_PIECE = D_PROJ // N_DEV
WGRAD_IN_ROWS = 512
SHUFFLE_LANES = 256


def _to_aligned_moves():
    moves, o = [], 0
    for lo, hi in NATIVE_ROWS:
        r = lo
        while r < hi:
            d = r // W_IN_PIECE
            k = min(hi, (d + 1) * W_IN_PIECE) - r
            moves.append((0, d, r - d * W_IN_PIECE, 0, o, k))
            r, o = r + k, o + k
    return moves


def _from_aligned_moves():
    moves = []
    for _, d, a, _, o, k in _to_aligned_moves():
        while k:
            n = min(k, WGRAD_IN_ROWS - o % WGRAD_IN_ROWS) if o < COL_SMALL else k
            moves.append((0, o // WGRAD_IN_ROWS, o % WGRAD_IN_ROWS, d, a, n) if o < COL_SMALL else
                         (1, 0, o - COL_SMALL, d, a, n))
            o, a, k = o + n, a + n, k - n
    return moves


def _shuffle_rows(srcs, moves, out_shape, name):
    c = srcs[0].shape[-1]

    def body(*refs):
        s_refs, o_ref, s_f, o_f = refs[:len(srcs)], refs[len(srcs)], refs[len(srcs) + 1:-1], refs[-1]
        for s_ref, f in zip(s_refs, s_f):
            f[...] = s_ref[...].astype(F32)
        o_f[...] = jnp.zeros_like(o_f)
        for i, ss, so, ds, do, k in moves:
            o_f[ds, pl.ds(do, k), :] = s_f[i][ss, pl.ds(so, k), :]
        o_ref[...] = o_f[...].astype(BF)

    blk = lambda shape: pl.BlockSpec(tuple(shape[:-1]) + (SHUFFLE_LANES,), lambda j: (0, 0, j))
    scratch = lambda shape: pltpu.VMEM(tuple(shape[:-1]) + (SHUFFLE_LANES,), F32)
    return pl.pallas_call(
        body, name=name, grid=(c // SHUFFLE_LANES,), in_specs=[blk(s.shape) for s in srcs], out_specs=blk(out_shape),
        out_shape=_sds(out_shape, BF), scratch_shapes=[scratch(s.shape) for s in srcs] + [scratch(out_shape)],
        compiler_params=_params("parallel"),
    )(*srcs)


def _cols_from_pieces(p):
    return p.transpose(1, 0, 2).reshape(p.shape[1], -1)


WEIGHT_ORDER = ("pre_mix_norm", "w_in", "fox_f_bias", "fox_out_norm", "gdn_conv_w", "gdn_a_log", "gdn_dt_bias",
                "gdn_out_norm", "w_out", "post_mix_norm", "pre_mlp_norm", "w_up", "w_down", "post_mlp_norm")


def kernel(x, pre_mix_norm, w_in, fox_f_bias, fox_out_norm, gdn_conv_w, gdn_a_log, gdn_dt_bias, gdn_out_norm, w_out, post_mix_norm, pre_mlp_norm, w_up, w_down, post_mlp_norm, loss_target, m_pre_mix_norm, m_w_in, m_fox_f_bias, m_fox_out_norm, m_gdn_conv_w, m_gdn_a_log, m_gdn_dt_bias, m_gdn_out_norm, m_w_out, m_post_mix_norm, m_pre_mlp_norm, m_w_up, m_w_down, m_post_mlp_norm, v_pre_mix_norm, v_w_in, v_fox_f_bias, v_fox_out_norm, v_gdn_conv_w, v_gdn_a_log, v_gdn_dt_bias, v_gdn_out_norm, v_w_out, v_post_mix_norm, v_pre_mlp_norm, v_w_up, v_w_down, v_post_mlp_norm):
    w = dict(pre_mix_norm=pre_mix_norm, w_in=w_in, fox_f_bias=fox_f_bias, fox_out_norm=fox_out_norm,
             gdn_conv_w=gdn_conv_w, gdn_a_log=gdn_a_log, gdn_dt_bias=gdn_dt_bias, gdn_out_norm=gdn_out_norm, w_out=w_out,
             post_mix_norm=post_mix_norm, pre_mlp_norm=pre_mlp_norm, w_up=w_up, w_down=w_down, post_mlp_norm=post_mlp_norm)
    mom = dict(pre_mix_norm=m_pre_mix_norm, w_in=m_w_in, fox_f_bias=m_fox_f_bias, fox_out_norm=m_fox_out_norm,
               gdn_conv_w=m_gdn_conv_w, gdn_a_log=m_gdn_a_log, gdn_dt_bias=m_gdn_dt_bias, gdn_out_norm=m_gdn_out_norm,
               w_out=m_w_out, post_mix_norm=m_post_mix_norm, pre_mlp_norm=m_pre_mlp_norm, w_up=m_w_up, w_down=m_w_down,
               post_mlp_norm=m_post_mlp_norm)
    var = dict(pre_mix_norm=v_pre_mix_norm, w_in=v_w_in, fox_f_bias=v_fox_f_bias, fox_out_norm=v_fox_out_norm,
               gdn_conv_w=v_gdn_conv_w, gdn_a_log=v_gdn_a_log, gdn_dt_bias=v_gdn_dt_bias, gdn_out_norm=v_gdn_out_norm,
               w_out=v_w_out, post_mix_norm=v_post_mix_norm, pre_mlp_norm=v_pre_mlp_norm, w_up=v_w_up, w_down=v_w_down,
               post_mlp_norm=v_post_mlp_norm)

    win_g, conv_g = _all_gather([w_in.T.astype(BF), gdn_conv_w])
    wt_al = _shuffle_rows([win_g], _to_aligned_moves(), (1, PROJ_W, D_MODEL), "w_in_to_aligned")[0]
    convw = _cols_from_pieces(conv_g)
    gathers, after = {}, win_g
    for name, shards in (("w_out", [w_out]), ("mlp", [w_up.T, w_down])):
        zones, shards = _zones_with_own(shards, "gather_" + name + "_own", after, BF)
        gathers[name] = _exchange_start(shards, zones, False, "gather_" + name + "_start", chips=name == "mlp")
        after = gathers[name][3]

    def late_weights(name, after):
        if name == "mlp_relay":
            sems, shards, zones, _ = gathers["mlp"]
            _, zones = _exchange_wait(sems, shards, zones, after, "gather_mlp_wait", chips=True)
            gathers["mlp"] = _relay_start(zones, "gather_mlp_relay")
            return gathers["mlp"][3]
        sems, shards, zones, _ = gathers[name]
        _, got = _exchange_wait(sems, shards, zones, after, "gather_" + name + "_done",
                                n_copies=len(CHIP_FLIPS) if name == "mlp" else None)
        if name == "w_out":
            return got[0].reshape(D_MODEL, D_MODEL)
        return got[0].reshape(D_FF, D_MODEL), got[1].reshape(D_FF, D_MODEL)

    scatters = {}

    def on_grads(name, g):
        chips = name == "w_in"
        if name == "w_in":
            g = _shuffle_rows(list(g), _from_aligned_moves(), (N_DEV, W_IN_PIECE, D_MODEL), "w_in_grad_from_aligned")
            g = _pair_reduce(g, "pair_reduce_w_in")
        srcs = list(g) if name == "mlp" else [g]
        scatters[name] = _exchange_start(srcs, None, True, "scatter_" + name + "_start", chips=chips)
        return scatters[name][3]

    grad_x, small = _local_step(
        x[0], loss_target[0], wt_al, after, late_weights, on_grads, convw, pre_mix_norm,
        fox_f_bias, fox_out_norm, gdn_a_log, gdn_dt_bias, gdn_out_norm, post_mix_norm, pre_mlp_norm, post_mlp_norm)
    slabs = [_small_pack(small), _conv_slabs(jnp.concatenate(small["conv"], axis=1))]
    scatters["small"] = _exchange_start(slabs, None, False, "small_start")

    grads, delta, new_m, new_v = {}, {}, {}, {}
    after = scatters["small"][3]
    for name, members in (("mlp", ("w_up", "w_down")), ("w_out", ("w_out",)), ("small", ()), ("w_in", ("w_in",))):
        sems, srcs, zones, _ = scatters[name]
        srcs, zones = _exchange_wait(sems, srcs, zones, after, "scatter_" + name + "_wait", chips=name == "w_in")
        if name == "small":
            loss, updated = _small_update(*zones, *srcs, w, mom, var)
            for n, res in updated.items():
                grads[n], delta[n], new_m[n], new_v[n] = res
            after = grads["pre_mix_norm"]
        for n, zone, own in zip(members, zones, srcs):
            if n == "w_in":
                res = _sum_adamw(zone, own, w[n].T, mom[n].T, var[n].T, "adamw_" + n, chips=True)
                grads[n], delta[n], new_m[n], new_v[n] = [r.T for r in res]
            else:
                grads[n], delta[n], new_m[n], new_v[n] = _sum_adamw(zone, own, w[n], mom[n], var[n], "adamw_" + n)
        if members:
            after = [grads[n] for n in members]

    return (loss, grad_x[None], *[grads[n] for n in WEIGHT_ORDER], *[delta[n] for n in WEIGHT_ORDER],
            *[new_m[n] for n in WEIGHT_ORDER], *[new_v[n] for n in WEIGHT_ORDER])
```

```python
import jax
import jax.numpy as jnp
from jax import lax
from jax.experimental import pallas as pl
from jax.experimental.pallas import tpu as pltpu

F32 = jnp.float32
BF = jnp.bfloat16

D_MODEL = 1024
N_FOX_HEADS, FOX_HEAD_DIM = 8, 64
N_GDN_HEADS, GDN_HEAD_DIM = 4, 128
D_FOX = N_FOX_HEADS * FOX_HEAD_DIM
D_GDN = N_GDN_HEADS * GDN_HEAD_DIM
CHUNK = 64
CONV_K = 4
D_FF = 4 * D_MODEL
EPS = 1e-6
D_PROJ = 3600
N_DEV = 8

PROJ_W = 3712
COL_FOX, COL_GDN, COL_GZ, COL_SMALL = 0, 1536, 3072, 3584
LANES = 128
QKV = 3 * LANES
SM_FF, SM_GB, SM_GA = 0, 8, 12

ADAM_LR, ADAM_B1, ADAM_B2, ADAM_EPS, ADAM_WD, ADAM_STEP = 0.001, 0.9, 0.999, 1e-08, 0.01, 10

TOKEN_BLOCK = 256
MATMUL_BLOCK = 512
FF_BLOCK = 1024
ADAM_ROWS = 128
FOX_SCALE = FOX_HEAD_DIM ** -0.5
GDN_QSCALE = GDN_HEAD_DIM ** -0.5
NEG_BIG = -1e30
VMEM_LIMIT = 56 * 1024 * 1024

VMEM_SPEC = pl.BlockSpec(memory_space=pltpu.VMEM)
ANY_SPEC = pl.BlockSpec(memory_space=pl.ANY)


def _sds(shape, dtype=F32):
    return jax.ShapeDtypeStruct(shape, dtype)


def _params(*sem):
    return pltpu.CompilerParams(dimension_semantics=sem if sem else None, vmem_limit_bytes=VMEM_LIMIT)


def _ordered(body):
    def ordered(_, *refs):
        body(*refs)

    return ordered


def _mm(a, b):
    return jnp.dot(a.astype(BF), b.astype(BF), preferred_element_type=F32)


def _mm_nt(a, b):
    return lax.dot_general(a.astype(BF), b.astype(BF), (((1,), (1,)), ((), ())), preferred_element_type=F32)


def _mm_tn(a, b):
    return lax.dot_general(a.astype(BF), b.astype(BF), (((0,), (0,)), ((), ())), preferred_element_type=F32)


def _sigmoid(x):
    return 1.0 / (1.0 + jnp.exp(-x))


def _softplus(x):
    return jnp.maximum(x, 0.0) + jnp.log1p(jnp.exp(-jnp.abs(x)))


def _iota(shape, dim):
    return lax.broadcasted_iota(jnp.int32, shape, dim)


def _shift_down(x, s, row):
    return jnp.where(row >= s, pltpu.roll(x, s, 0), 0.0)


def _shift_up(x, s, row):
    n = x.shape[0]
    return jnp.where(row < n - s, pltpu.roll(x, n - s, 0), 0.0)


def _norm_proj(x, nw, wt_al, after):
    t = x.shape[0]

    def body(x_ref, nw_ref, w_ref, proj_ref, h_ref):
        xv = x_ref[...]
        r = lax.rsqrt(jnp.mean(xv * xv, axis=-1, keepdims=True) + EPS)
        h = (xv * r * nw_ref[...]).astype(BF)
        h_ref[...] = h
        proj_ref[...] = lax.dot_general(h, w_ref[...], (((1,), (1,)), ((), ())), preferred_element_type=F32)

    tm = min(MATMUL_BLOCK, t)
    return pl.pallas_call(
        _ordered(body), name="norm_proj", grid=(t // tm,),
        in_specs=[ANY_SPEC, pl.BlockSpec((tm, D_MODEL), lambda i: (i, 0)), pl.BlockSpec((1, D_MODEL), lambda i: (0, 0)),
                  pl.BlockSpec((PROJ_W, D_MODEL), lambda i: (0, 0))],
        out_specs=[pl.BlockSpec((tm, PROJ_W), lambda i: (i, 0)), pl.BlockSpec((tm, D_MODEL), lambda i: (i, 0))],
        out_shape=[_sds((t, PROJ_W)), _sds((t, D_MODEL), BF)],
        compiler_params=_params("parallel"),
    )(after, x, nw, wt_al)


def _lane_column(x, lane):
    return jnp.sum(jnp.where(_iota((1, LANES), 1) == lane, x, 0.0), axis=-1, keepdims=True)


def _small_prep(proj, fb, al, dtb):
    t = proj.shape[0]

    def body(sm_ref, fb_ref, al_ref, dtb_ref, cumt_ref, beta_ref, g_ref):
        s = sm_ref[...]
        z = s + fb_ref[...]
        cum = jnp.minimum(z, 0.0) - jnp.log1p(jnp.exp(-jnp.abs(z)))
        row = _iota((t, LANES), 0)
        step = 1
        while step < t:
            cum = cum + _shift_down(cum, step, row)
            step *= 2
        cumt_ref[...] = cum.T
        beta_ref[...] = _sigmoid(s)
        g_ref[...] = -jnp.exp(al_ref[...]) * _softplus(s + dtb_ref[...])

    vec = pl.BlockSpec((1, LANES), lambda i: (0, 0))
    tok = pl.BlockSpec((t, LANES), lambda i: (0, 0))
    return pl.pallas_call(
        body, name="small_prep", grid=(1,),
        in_specs=[pl.BlockSpec((t, LANES), lambda i: (0, COL_SMALL // LANES)), vec, vec, vec],
        out_specs=[pl.BlockSpec((LANES, t), lambda i: (0, 0)), tok, tok],
        out_shape=[_sds((LANES, t)), _sds((t, LANES)), _sds((t, LANES))],
        compiler_params=_params("arbitrary"),
    )(proj, fb, al, dtb)


def _fox_stack(x, first):
    return jnp.concatenate([jnp.where(first, x, 0.0), jnp.where(first, 0.0, x)], axis=0).astype(BF)


def _fox_unstack(y, first):
    n = y.shape[0] // 2
    return jnp.where(first, y[:n], y[n:])


def _fox_logits(q2_i, kb, cumt_ref, pair, i, tq):
    klen = (i + 1) * tq
    s = lax.dot_general(q2_i, kb[:klen], (((1,), (1,)), ((), ())), preferred_element_type=F32)
    upper = _iota((2 * tq, 1), 0) < tq
    s = s - jnp.where(upper, cumt_ref[pl.ds(2 * pair, 1), 0:klen], cumt_ref[pl.ds(2 * pair + 1, 1), 0:klen])
    causal = _iota((2 * tq, tq), 1) <= _iota((2 * tq, tq), 0) % tq
    parts = [(s[:, :klen - tq], 0, klen - tq)] if i else []
    return parts + [(jnp.where(causal, s[:, klen - tq:], NEG_BIG), klen - tq, klen)]


def _fox_fwd(proj, cumt, fnw):
    t = proj.shape[0]
    tq = min(TOKEN_BLOCK, t // 2)
    nq = t // tq

    def body(q_ref, k_ref, v_ref, cumt_ref, fnw_ref, o_ref, lse_ref, fn_ref):
        j = pl.program_id(0)
        first = _iota((1, LANES), 1) < FOX_HEAD_DIM
        kb = k_ref[...].astype(BF)
        vb = v_ref[...].astype(BF)
        for i in range(nq):
            rows = slice(i * tq, (i + 1) * tq)
            q2 = _fox_stack(q_ref[rows, :] * FOX_SCALE, first)
            parts = _fox_logits(q2, kb, cumt_ref, j, i, tq)
            m = jnp.max(parts[-1][0], axis=-1, keepdims=True)
            if i:
                m = jnp.maximum(m, jnp.max(parts[0][0], axis=-1, keepdims=True))
            l = jnp.zeros((2 * tq, 1), F32)
            o = jnp.zeros((2 * tq, LANES), F32)
            for s, lo, hi in parts:
                p = jnp.exp(s - m)
                l = l + jnp.sum(p, axis=-1, keepdims=True)
                o = o + jnp.dot(p.astype(BF), vb[lo:hi], preferred_element_type=F32)
            o_acc = _fox_unstack(o / l, first)
            lse_acc = _fox_unstack(jnp.broadcast_to(m + jnp.log(l), (2 * tq, LANES)), first)
            o_ref[rows, :] = o_acc
            lse_ref[rows, :] = lse_acc
            o2 = o_acc * o_acc
            s0 = jnp.sum(jnp.where(first, o2, 0.0), axis=-1, keepdims=True)
            s1 = jnp.sum(jnp.where(first, 0.0, o2), axis=-1, keepdims=True)
            r = lax.rsqrt(jnp.where(first, s0, s1) * (1.0 / FOX_HEAD_DIM) + EPS)
            fn_ref[rows, :] = (o_acc * r * fnw_ref[...]).astype(BF)

    qkv = lambda k: pl.BlockSpec((t, LANES), lambda j: (0, COL_FOX // LANES + 3 * j + k))
    pair = pl.BlockSpec((t, LANES), lambda j: (0, j))
    return pl.pallas_call(
        body, name="fox_fwd", grid=(N_FOX_HEADS // 2,),
        in_specs=[qkv(0), qkv(1), qkv(2), pl.BlockSpec((LANES, t), lambda j: (0, 0)),
                  pl.BlockSpec((1, LANES), lambda j: (0, 0))],
        out_specs=[pair, pair, pair],
        out_shape=[_sds((t, D_FOX)), _sds((t, D_FOX)), _sds((t, D_FOX), BF)],
        compiler_params=_params("parallel"),
    )(proj, proj, proj, cumt, fnw)


def _fox_bwd(proj, cumt, lse, o, do, dproj):
    t = proj.shape[0]
    tq = min(TOKEN_BLOCK, t // 2)
    nq = t // tq

    def body(q_ref, k_ref, v_ref, cumt_ref, lse_ref, o_ref, do_ref, _, dqkv_ref, dcq_ref, dckt_ref, dk_s, dv_s):
        j = pl.program_id(0)

        @pl.when(j == 0)
        def _():
            dcq_ref[...] = jnp.zeros_like(dcq_ref)
            dckt_ref[...] = jnp.zeros_like(dckt_ref)

        lane = _iota((1, LANES), 1)

        first = _iota((1, LANES), 1) < FOX_HEAD_DIM
        kb = k_ref[...].astype(BF)
        vb = v_ref[...].astype(BF)
        dk_s[...] = jnp.zeros_like(dk_s)
        dv_s[...] = jnp.zeros_like(dv_s)
        for i in range(nq):
            rows = slice(i * tq, (i + 1) * tq)
            do_i = do_ref[rows, :]
            prod = do_i * o_ref[rows, :]
            lse_i = lse_ref[rows, :]
            q2 = _fox_stack(q_ref[rows, :] * FOX_SCALE, first)
            do2 = _fox_stack(do_i, first)
            delta = jnp.concatenate([jnp.sum(jnp.where(first, prod, 0.0), axis=-1, keepdims=True),
                                     jnp.sum(jnp.where(first, 0.0, prod), axis=-1, keepdims=True)], axis=0)
            lse2 = jnp.concatenate([lse_i[:, 0:1], lse_i[:, FOX_HEAD_DIM:FOX_HEAD_DIM + 1]], axis=0)
            dq2 = jnp.zeros((2 * tq, LANES), F32)
            dcq2 = jnp.zeros((2 * tq, 1), F32)
            for s, lo, hi in _fox_logits(q2, kb, cumt_ref, j, i, tq):
                p = jnp.exp(s - lse2)
                ds = p * (_mm_nt(do2, vb[lo:hi]) - delta)
                dsb = ds.astype(BF)
                dq2 = dq2 + jnp.dot(dsb, kb[lo:hi], preferred_element_type=F32)
                dk_s[lo:hi, :] += _mm_tn(dsb, q2)
                dv_s[lo:hi, :] += _mm_tn(p, do2)
                dcq2 = dcq2 + jnp.sum(ds, axis=-1, keepdims=True)
                dckt_ref[pl.ds(2 * j, 1), lo:hi] += jnp.sum(ds[:tq], axis=0, keepdims=True)
                dckt_ref[pl.ds(2 * j + 1, 1), lo:hi] += jnp.sum(ds[tq:], axis=0, keepdims=True)
            dqkv_ref[rows, 0:LANES] = (_fox_unstack(dq2, first) * FOX_SCALE).astype(BF)
            dcq_ref[rows, :] += jnp.where(lane == 2 * j, dcq2[:tq], jnp.where(lane == 2 * j + 1, dcq2[tq:], 0.0))
        dqkv_ref[:, LANES:2 * LANES] = dk_s[...].astype(BF)
        dqkv_ref[:, 2 * LANES:QKV] = dv_s[...].astype(BF)

    qkv = lambda k: pl.BlockSpec((t, LANES), lambda j: (0, COL_FOX // LANES + 3 * j + k))
    pair = pl.BlockSpec((t, LANES), lambda j: (0, j))
    rows128 = pl.BlockSpec((LANES, t), lambda j: (0, 0))
    return pl.pallas_call(
        body, name="fox_bwd", grid=(N_FOX_HEADS // 2,),
        in_specs=[qkv(0), qkv(1), qkv(2), rows128, pair, pair, pair, ANY_SPEC],
        out_specs=[pl.BlockSpec((t, QKV), lambda j: (0, COL_FOX // QKV + j)),
                   pl.BlockSpec((t, LANES), lambda j: (0, 0)), rows128],
        out_shape=[_sds(dproj.shape, BF), _sds((t, LANES)), _sds((LANES, t))],
        scratch_shapes=[pltpu.VMEM((t, LANES), F32), pltpu.VMEM((t, LANES), F32)],
        input_output_aliases={7: 0}, compiler_params=_params("arbitrary"),
    )(proj, proj, proj, cumt, lse, o, do, dproj)


def _conv(x, w, row):
    return (w[3:4, :] * x + w[2:3, :] * _shift_down(x, 1, row) + w[1:2, :] * _shift_down(x, 2, row)
            + w[0:1, :] * _shift_down(x, 3, row))


def _chunk_decay(gc_c):
    gi = gc_c[:, 0:CHUNK]
    gj = gc_c.T[0:CHUNK, :]
    ri = _iota((CHUNK, CHUNK), 0)
    cj = _iota((CHUNK, CHUNK), 1)
    return jnp.where(ri >= cj, jnp.exp(jnp.minimum(gi - gj, 0.0)), 0.0), ri > cj


def _gdn_specs(t):
    col = lambda off: pl.BlockSpec((t, LANES), lambda h: (0, off + h))
    cw = lambda off: pl.BlockSpec((CONV_K, LANES), lambda h: (0, off + h))
    mat = pl.BlockSpec((1, t // CHUNK, CHUNK, CHUNK), lambda h: (h, 0, 0, 0))
    qkv = lambda k: pl.BlockSpec((t, LANES), lambda h: (0, COL_GDN // LANES + 3 * h + k))
    return col, cw, mat, qkv


def _gdn_prep(proj, convw, beta, g):
    t = proj.shape[0]
    nch = t // CHUNK

    def body(xq_ref, xk_ref, xv_ref, wq_ref, wk_ref, wv_ref, beta_ref, g_ref,
             qn_ref, kn_ref, cv_ref, gc_ref, be_ref, m_ref, a_ref):
        row = _iota((t, LANES), 0)
        hd = pl.program_id(0)
        be_ref[...] = jnp.broadcast_to(_lane_column(beta_ref[...], SM_GB + hd), (t, LANES))

        def act(x_ref, w_ref):
            y = _conv(x_ref[...], w_ref[...], row)
            return y * _sigmoid(y)

        cq = act(xq_ref, wq_ref)
        ck = act(xk_ref, wk_ref)
        cv_ref[...] = act(xv_ref, wv_ref)
        qn_ref[...] = cq * lax.rsqrt(jnp.sum(cq * cq, axis=-1, keepdims=True) + EPS) * GDN_QSCALE
        kn_ref[...] = ck * lax.rsqrt(jnp.sum(ck * ck, axis=-1, keepdims=True) + EPS)
        gc = jnp.broadcast_to(_lane_column(g_ref[...], SM_GA + hd), (t, LANES))
        pos = row % CHUNK
        step = 1
        while step < CHUNK:
            gc = gc + jnp.where(pos >= step, pltpu.roll(gc, step, 0), 0.0)
            step *= 2
        gc_ref[...] = gc

        group = 4 if nch % 4 == 0 else 1

        def chunks(gi, carry):
            ns = [gi * group + c for c in range(group)]
            sls = [pl.ds(pl.multiple_of(n * CHUNK, CHUNK), CHUNK) for n in ns]
            ks = [kn_ref[sl, :] for sl in sls]
            kk = [_mm_nt(k_c * be_ref[sl, :], k_c) for k_c, sl in zip(ks, sls)]
            qk = [_mm_nt(qn_ref[sl, :], k_c) for k_c, sl in zip(ks, sls)]
            for c, n in enumerate(ns):
                decay, strict = _chunk_decay(gc_ref[sls[c], :])
                m_ref[0, n] = jnp.where(strict, kk[c] * decay, 0.0)
                a_ref[0, n] = qk[c] * decay
            return carry

        lax.fori_loop(0, nch // group, chunks, 0)

    col, cw, mat, qkv = _gdn_specs(t)
    return pl.pallas_call(
        body, name="gdn_prep", grid=(N_GDN_HEADS,),
        in_specs=[qkv(0), qkv(1), qkv(2), cw(0), cw(4), cw(8)] + [pl.BlockSpec((t, LANES), lambda h: (0, 0))] * 2,
        out_specs=[col(0), col(0), col(0), col(0), col(0), mat, mat],
        out_shape=[_sds((t, D_GDN))] * 5 + [_sds((N_GDN_HEADS, nch, CHUNK, CHUNK))] * 2,
        compiler_params=_params("parallel"),
    )(proj, proj, proj, convw, convw, convw, beta, g)


def _tri_inverse(m3):
    assert m3.shape == (LANES, CHUNK, CHUNK)

    def body(m_ref, t_ref, ms, ts):
        for i in range(CHUNK):
            ms[i * CHUNK:(i + 1) * CHUNK, :] = m_ref[:, i, :].T
        cidx = _iota((CHUNK, LANES), 0)

        def outer(i, carry):
            def inner(jj, acc):
                mrow = ms[pl.ds(i * CHUNK + jj, 1), :]
                return acc - mrow * ts[pl.ds(pl.multiple_of(jj * CHUNK, CHUNK), CHUNK), :]

            acc = lax.fori_loop(0, i, inner, jnp.where(cidx == i, 1.0, 0.0).astype(F32))
            ts[pl.ds(pl.multiple_of(i * CHUNK, CHUNK), CHUNK), :] = acc
            return carry

        lax.fori_loop(0, CHUNK, outer, 0)
        for i in range(CHUNK):
            t_ref[:, i, :] = ts[i * CHUNK:(i + 1) * CHUNK, :].T

    return pl.pallas_call(
        body, name="tri_inverse", in_specs=[VMEM_SPEC], out_specs=VMEM_SPEC,
        out_shape=_sds((LANES, CHUNK, CHUNK)),
        scratch_shapes=[pltpu.VMEM((CHUNK * CHUNK, LANES), F32), pltpu.VMEM((CHUNK * CHUNK, LANES), F32)],
        compiler_params=_params(),
    )(m3)


def _gdn_chunk_terms(q, k, v, b, gcc):
    eg = jnp.exp(gcc)
    last = gcc[CHUNK - 1:CHUNK, :]
    egl = jnp.exp(last - gcc)
    gl = jnp.exp(last)
    kb = k * b
    return eg, egl, gl, kb, v * b, kb * eg, q * eg, k * egl


GDN_BLOCK_CHUNKS = 4


def _gdn_block_specs(t, reverse):
    cb = GDN_BLOCK_CHUNKS
    nb = t // (cb * CHUNK)
    idx = (lambda i: nb - 1 - i) if reverse else (lambda i: i)
    tok = pl.BlockSpec((cb * CHUNK, D_GDN), lambda i: (idx(i), 0))
    mat = pl.BlockSpec((N_GDN_HEADS, cb, CHUNK, CHUNK), lambda i: (0, idx(i), 0, 0))
    state = pl.BlockSpec((N_GDN_HEADS, cb, GDN_HEAD_DIM, GDN_HEAD_DIM), lambda i: (0, idx(i), 0, 0))
    return nb, tok, mat, state


def _gdn_scan(qn, kn, cv, be, gc, tinv, amat):
    t = qn.shape[0]
    nch = t // CHUNK

    def body(q_ref, k_ref, v_ref, b_ref, gc_ref, t_ref, a_ref, o_ref, sall_ref, vn_ref, s_scr):
        @pl.when(pl.program_id(0) == 0)
        def _():
            s_scr[...] = jnp.zeros_like(s_scr)

        heads = range(N_GDN_HEADS)
        cols = [slice(hd * LANES, (hd + 1) * LANES) for hd in heads]
        s = [s_scr[hd] for hd in heads]
        for cc in range(GDN_BLOCK_CHUNKS):
            rs = slice(cc * CHUNK, (cc + 1) * CHUNK)
            terms = [_gdn_chunk_terms(q_ref[rs, cs], k_ref[rs, cs], v_ref[rs, cs], b_ref[rs, cs], gc_ref[rs, cs])
                     for cs in cols]
            for hd in heads:
                sall_ref[hd, cc] = s[hd]
            uw = [_mm(t_ref[hd, cc], jnp.concatenate([terms[hd][4], terms[hd][5]], axis=1)) for hd in heads]
            ws_qs = [_mm(jnp.concatenate([uw[hd][:, LANES:], terms[hd][6]], axis=0), s[hd]) for hd in heads]
            vn = [uw[hd][:, :LANES] - ws_qs[hd][:CHUNK] for hd in heads]
            a_vn = [_mm(a_ref[hd, cc], vn[hd]) for hd in heads]
            kd_vn = [_mm_tn(terms[hd][7], vn[hd]) for hd in heads]
            for hd in heads:
                vn_ref[rs, cols[hd]] = vn[hd]
                o_ref[rs, cols[hd]] = ws_qs[hd][CHUNK:] + a_vn[hd]
                s[hd] = s[hd] * terms[hd][2] + kd_vn[hd]
        for hd in heads:
            s_scr[hd] = s[hd]

    nb, tok, mat, state = _gdn_block_specs(t, False)
    return pl.pallas_call(
        body, name="gdn_scan", grid=(nb,),
        in_specs=[tok] * 5 + [mat, mat], out_specs=[tok, state, tok],
        out_shape=[_sds((t, D_GDN)), _sds((N_GDN_HEADS, nch, GDN_HEAD_DIM, GDN_HEAD_DIM)), _sds((t, D_GDN))],
        scratch_shapes=[pltpu.VMEM((N_GDN_HEADS, GDN_HEAD_DIM, GDN_HEAD_DIM), F32)],
        compiler_params=_params("arbitrary"),
    )(qn, kn, cv, be, gc, tinv, amat)


def _gdn_bwd(qn, kn, cv, be, gc, tinv, amat, s_all, vn_all, do):
    t = qn.shape[0]

    def body(q_ref, k_ref, v_ref, b_ref, gc_ref, t_ref, a_ref, sall_ref, vn_ref, do_ref,
             dq_ref, dk_ref, dv_ref, db_ref, dg_ref, ds_scr):
        @pl.when(pl.program_id(0) == 0)
        def _():
            ds_scr[...] = jnp.zeros_like(ds_scr)

        lastrow = _iota((CHUNK, LANES), 0) == CHUNK - 1
        heads = range(N_GDN_HEADS)
        cols = [slice(hd * LANES, (hd + 1) * LANES) for hd in heads]
        each = lambda fn: [fn(hd) for hd in heads]
        rows_cat = lambda x, y: jnp.concatenate([x, y], axis=0)
        lane_cat = lambda x, y: jnp.concatenate([x, y], axis=1)
        dsp = each(lambda hd: ds_scr[hd])
        for cc in reversed(range(GDN_BLOCK_CHUNKS)):
            rs = slice(cc * CHUNK, (cc + 1) * CHUNK)
            q = each(lambda hd: q_ref[rs, cols[hd]])
            k = each(lambda hd: k_ref[rs, cols[hd]])
            v = each(lambda hd: v_ref[rs, cols[hd]])
            b = each(lambda hd: b_ref[rs, cols[hd]])
            gcc = each(lambda hd: gc_ref[rs, cols[hd]])
            do_c = each(lambda hd: do_ref[rs, cols[hd]])
            vn = each(lambda hd: vn_ref[rs, cols[hd]])
            tn = each(lambda hd: t_ref[hd, cc])
            st = each(lambda hd: sall_ref[hd, cc])
            terms = each(lambda hd: _gdn_chunk_terms(q[hd], k[hd], v[hd], b[hd], gcc[hd]))
            eg, egl, gl, kb, vb, kbg, qd, kd = [[terms[hd][i] for hd in heads] for i in range(8)]
            w = each(lambda hd: _mm(tn[hd], kbg[hd]))
            a_do = each(lambda hd: _mm_tn(a_ref[hd, cc], do_c[hd]))
            kd_ds = each(lambda hd: _mm(kd[hd], dsp[hd]))
            da = each(lambda hd: _mm_nt(do_c[hd], vn[hd]))
            dkd = each(lambda hd: _mm_nt(vn[hd], dsp[hd]))
            by_k = each(lambda hd: _mm_nt(rows_cat(kb[hd], q[hd]), k[hd]))
            dgl = each(lambda hd: jnp.sum(jnp.sum(dsp[hd] * st[hd], axis=-1, keepdims=True), axis=0, keepdims=True))
            dvn = each(lambda hd: a_do[hd] + kd_ds[hd])
            do_dvn = each(lambda hd: rows_cat(do_c[hd], dvn[hd]))
            by_s = each(lambda hd: _mm_nt(do_dvn[hd], st[hd]))
            dqd = each(lambda hd: by_s[hd][:CHUNK])
            dvn_dw = each(lambda hd: lane_cat(dvn[hd], -by_s[hd][CHUNK:]))
            dsp = each(lambda hd: _mm_tn(rows_cat(qd[hd], -w[hd]), do_dvn[hd]) + gl[hd] * dsp[hd])
            dt = each(lambda hd: _mm_nt(dvn_dw[hd], lane_cat(vb[hd], kbg[hd])))
            by_t = each(lambda hd: _mm_tn(tn[hd], dvn_dw[hd]))
            tt_dt = each(lambda hd: _mm_tn(tn[hd], dt[hd]))
            dm_raw = each(lambda hd: _mm_nt(tt_dt[hd], tn[hd]))
            masks = each(lambda hd: _chunk_decay(gcc[hd]))
            dkk = each(lambda hd: jnp.where(masks[hd][1], -dm_raw[hd], 0.0) * masks[hd][0])
            dqk = each(lambda hd: da[hd] * masks[hd][0])
            dqk_dkk = each(lambda hd: rows_cat(dqk[hd], dkk[hd]))
            on_k = each(lambda hd: _mm(dqk_dkk[hd], k[hd]))
            dk_mm = each(lambda hd: _mm_tn(dqk_dkk[hd], rows_cat(q[hd], kb[hd])))
            for hd in heads:
                cs = cols[hd]
                dvb, dkbg = by_t[hd][:, :LANES], by_t[hd][:, LANES:]
                gmat = dkk[hd] * by_k[hd][:CHUNK] + dqk[hd] * by_k[hd][CHUNK:]
                dq_ref[rs, cs] = dqd[hd] * eg[hd] + on_k[hd][:CHUNK]
                dkb = on_k[hd][CHUNK:] + dkbg * eg[hd]
                dk_ref[rs, cs] = dkd[hd] * egl[hd] + dk_mm[hd] + dkb * b[hd]
                db = jnp.sum(dkb * k[hd], axis=-1, keepdims=True) + jnp.sum(dvb * v[hd], axis=-1, keepdims=True)
                db_ref[rs, cs] = jnp.broadcast_to(db, (CHUNK, LANES))
                dv_ref[rs, cs] = dvb * b[hd]
                dkd_kd = jnp.sum(dkd[hd] * kd[hd], axis=-1, keepdims=True)
                col_sums = jnp.sum(lane_cat(gmat, jnp.zeros_like(gmat)).T, axis=-1, keepdims=True)
                dgc = (jnp.sum(gmat, axis=-1, keepdims=True) - col_sums[:CHUNK]
                       + jnp.sum(dqd[hd] * qd[hd], axis=-1, keepdims=True)
                       + jnp.sum(dkbg * kbg[hd], axis=-1, keepdims=True) - dkd_kd)
                extra = jnp.sum(dkd_kd, axis=0, keepdims=True) + dgl[hd] * gl[hd]
                dg_ref[rs, cs] = dgc + jnp.where(lastrow, extra, 0.0)
        for hd in heads:
            ds_scr[hd] = dsp[hd]
        dg = dg_ref[...]
        row = _iota(dg.shape, 0)
        pos = row % CHUNK
        step = 1
        while step < CHUNK:
            dg = dg + jnp.where(pos < CHUNK - step, pltpu.roll(dg, dg.shape[0] - step, 0), 0.0)
            step *= 2
        dg_ref[...] = dg

    nb, tok, mat, state = _gdn_block_specs(t, True)
    return pl.pallas_call(
        body, name="gdn_bwd", grid=(nb,),
        in_specs=[tok] * 5 + [mat, mat, state, tok, tok], out_specs=[tok] * 5, out_shape=[_sds((t, D_GDN))] * 5,
        scratch_shapes=[pltpu.VMEM((N_GDN_HEADS, GDN_HEAD_DIM, GDN_HEAD_DIM), F32)],
        compiler_params=_params("arbitrary"),
    )(qn, kn, cv, be, gc, tinv, amat, s_all, vn_all, do)


def _gdn_bwd_conv(proj, convw, dqn, dkn, dcv, dproj):
    t = proj.shape[0]

    def body(xq_ref, xk_ref, xv_ref, wq_ref, wk_ref, wv_ref, dq_ref, dk_ref, dv_ref, _,
             dqkv_ref, dwq_ref, dwk_ref, dwv_ref):
        row = _iota((t, LANES), 0)

        def one(x_ref, w_ref, d_ref, k, dw_ref, scale):
            x = x_ref[...]
            w = w_ref[...]
            y = _conv(x, w, row)
            sg = _sigmoid(y)
            dc = d_ref[...]
            if scale is not None:
                c = y * sg
                r = lax.rsqrt(jnp.sum(c * c, axis=-1, keepdims=True) + EPS)
                ch = c * r
                dc = scale * r * (dc - ch * jnp.sum(dc * ch, axis=-1, keepdims=True))
            dy = dc * sg * (1.0 + y * (1.0 - sg))
            dqkv_ref[:, k * LANES:(k + 1) * LANES] = (
                w[3:4, :] * dy + w[2:3, :] * _shift_up(dy, 1, row) + w[1:2, :] * _shift_up(dy, 2, row)
                + w[0:1, :] * _shift_up(dy, 3, row)).astype(BF)
            for jj in range(CONV_K):
                xs = x if jj == CONV_K - 1 else _shift_down(x, CONV_K - 1 - jj, row)
                dw_ref[jj:jj + 1, :] = jnp.sum(dy * xs, axis=0, keepdims=True)

        one(xq_ref, wq_ref, dq_ref, 0, dwq_ref, GDN_QSCALE)
        one(xk_ref, wk_ref, dk_ref, 1, dwk_ref, 1.0)
        one(xv_ref, wv_ref, dv_ref, 2, dwv_ref, None)

    col, cw, _, qkv = _gdn_specs(t)
    return pl.pallas_call(
        body, name="gdn_bwd_conv", grid=(N_GDN_HEADS,),
        in_specs=[qkv(0), qkv(1), qkv(2), cw(0), cw(4), cw(8), col(0), col(0), col(0), ANY_SPEC],
        out_specs=[pl.BlockSpec((t, QKV), lambda h: (0, COL_GDN // QKV + h)), cw(0), cw(0), cw(0)],
        out_shape=[_sds(dproj.shape, BF)] + [_sds((CONV_K, D_GDN))] * 3,
        input_output_aliases={9: 0}, compiler_params=_params("parallel"),
    )(proj, proj, proj, convw, convw, convw, dqn, dkn, dcv, dproj)


def _mix_out(fox_n, gdn_o, proj, gnw, w_out, x, pmw, plw, after):
    t = x.shape[0]
    tm = min(MATMUL_BLOCK, t)

    def body(fn_ref, go_ref, gz_ref, gnw_ref, w_ref, x_ref, pmw_ref, plw_ref, x1_ref, h2_ref, mixed_ref, omix_ref,
             h2t_ref):
        omix_ref[:, 0:D_FOX] = fn_ref[...]
        for hd in range(N_GDN_HEADS):
            cs = slice(hd * LANES, (hd + 1) * LANES)
            go = go_ref[:, cs]
            r = lax.rsqrt(jnp.mean(go * go, axis=-1, keepdims=True) + EPS)
            gz = gz_ref[:, cs]
            omix_ref[:, D_FOX + hd * LANES:D_FOX + (hd + 1) * LANES] = (
                go * r * gnw_ref[...] * (gz * _sigmoid(gz))).astype(BF)
        mixed = jnp.dot(omix_ref[...], w_ref[...], preferred_element_type=F32)
        mixed_ref[...] = mixed
        r2 = lax.rsqrt(jnp.mean(mixed * mixed, axis=-1, keepdims=True) + EPS)
        x1 = x_ref[...] + mixed * r2 * pmw_ref[...]
        x1_ref[...] = x1
        r3 = lax.rsqrt(jnp.mean(x1 * x1, axis=-1, keepdims=True) + EPS)
        h2 = x1 * r3 * plw_ref[...]
        h2_ref[...] = h2.astype(BF)
        h2t_ref[...] = h2.T.astype(BF)

    tok = lambda w: pl.BlockSpec((tm, w), lambda i: (i, 0))
    vec = lambda w: pl.BlockSpec((1, w), lambda i: (0, 0))
    return pl.pallas_call(
        _ordered(body), name="mix_out", grid=(t // tm,),
        in_specs=[ANY_SPEC, tok(D_FOX), tok(D_GDN), pl.BlockSpec((tm, D_GDN), lambda i: (i, COL_GZ // D_GDN)), vec(LANES),
                  pl.BlockSpec((D_MODEL, D_MODEL), lambda i: (0, 0)), tok(D_MODEL), vec(D_MODEL), vec(D_MODEL)],
        out_specs=[tok(D_MODEL)] * 4 + [pl.BlockSpec((D_MODEL, tm), lambda i: (0, i))],
        out_shape=[_sds((t, D_MODEL)), _sds((t, D_MODEL), BF), _sds((t, D_MODEL)), _sds((t, D_MODEL), BF),
                   _sds((D_MODEL, t), BF)],
        compiler_params=_params("parallel"),
    )(after, fox_n, gdn_o, proj, gnw, w_out, x, pmw, plw)


def _out_bwd(dmixed, w_out, o_fox, gdn_o, proj, fnw, gnw, after):
    t = dmixed.shape[0]
    tm = min(MATMUL_BLOCK, t)

    def body(dm_ref, w_ref, of_ref, go_ref, gz_ref, fnw_ref, gnw_ref, dof_ref, dgo_ref, dgz_ref, dfw_ref, dgw_ref):
        i = pl.program_id(0)

        @pl.when(i == 0)
        def _():
            dfw_ref[...] = jnp.zeros_like(dfw_ref)
            dgw_ref[...] = jnp.zeros_like(dgw_ref)

        domix = _mm_nt(dm_ref[...], w_ref[...])
        first = _iota((1, LANES), 1) < FOX_HEAD_DIM
        dfw = jnp.zeros((1, LANES), F32)
        dgw = jnp.zeros((1, LANES), F32)
        for pr in range(N_FOX_HEADS // 2):
            cs = slice(pr * LANES, (pr + 1) * LANES)
            o = of_ref[:, cs]
            dfn = domix[:, cs]
            o2 = o * o
            s0 = jnp.sum(jnp.where(first, o2, 0.0), axis=-1, keepdims=True)
            s1 = jnp.sum(jnp.where(first, 0.0, o2), axis=-1, keepdims=True)
            r = lax.rsqrt(jnp.where(first, s0, s1) * (1.0 / FOX_HEAD_DIM) + EPS)
            oh = o * r
            dfw = dfw + jnp.sum(dfn * oh, axis=0, keepdims=True)
            doh = dfn * fnw_ref[...]
            pr_ = doh * oh
            m0 = jnp.sum(jnp.where(first, pr_, 0.0), axis=-1, keepdims=True)
            m1 = jnp.sum(jnp.where(first, 0.0, pr_), axis=-1, keepdims=True)
            dof_ref[:, cs] = r * (doh - oh * jnp.where(first, m0, m1) * (1.0 / FOX_HEAD_DIM))
        for hd in range(N_GDN_HEADS):
            cs = slice(hd * LANES, (hd + 1) * LANES)
            go = go_ref[:, cs]
            gz = gz_ref[:, cs]
            dgated = domix[:, D_FOX + hd * LANES:D_FOX + (hd + 1) * LANES]
            r = lax.rsqrt(jnp.mean(go * go, axis=-1, keepdims=True) + EPS)
            goh = go * r
            sg = _sigmoid(gz)
            sz = gz * sg
            gn = goh * gnw_ref[...]
            dgn = dgated * sz
            dgz_ref[:, cs] = (dgated * gn * sg * (1.0 + gz * (1.0 - sg))).astype(BF)
            dgw = dgw + jnp.sum(dgn * goh, axis=0, keepdims=True)
            dgh = dgn * gnw_ref[...]
            dgo_ref[:, cs] = r * (dgh - goh * jnp.mean(dgh * goh, axis=-1, keepdims=True))
        dfw_ref[...] += dfw + pltpu.roll(dfw, FOX_HEAD_DIM, 1)
        dgw_ref[...] += dgw

    tok = lambda w: pl.BlockSpec((tm, w), lambda i: (i, 0))
    vec = lambda w: pl.BlockSpec((1, w), lambda i: (0, 0))
    return pl.pallas_call(
        _ordered(body), name="out_bwd", grid=(t // tm,),
        in_specs=[ANY_SPEC, tok(D_MODEL), pl.BlockSpec((D_MODEL, D_MODEL), lambda i: (0, 0)), tok(D_FOX), tok(D_GDN),
                  pl.BlockSpec((tm, D_GDN), lambda i: (i, COL_GZ // D_GDN)), vec(LANES), vec(LANES)],
        out_specs=[tok(D_FOX), tok(D_GDN), pl.BlockSpec((tm, D_GDN), lambda i: (i, COL_GZ // D_GDN)), vec(LANES),
                   vec(LANES)],
        out_shape=[_sds((t, D_FOX)), _sds((t, D_GDN)), _sds((t, PROJ_W), BF), _sds((1, LANES)), _sds((1, LANES))],
        compiler_params=_params("arbitrary"),
    )(after, dmixed, w_out, o_fox, gdn_o, proj, fnw, gnw)


def _mlp_up(h2, w_upt):
    t = h2.shape[0]
    tm = min(MATMUL_BLOCK, t)

    def body(h_ref, w_ref, up_ref):
        up_ref[...] = lax.dot_general(h_ref[...], w_ref[...], (((1,), (1,)), ((), ())),
                                      preferred_element_type=F32).astype(BF)

    return pl.pallas_call(
        body, name="mlp_up", grid=(D_FF // FF_BLOCK, t // tm),
        in_specs=[pl.BlockSpec((tm, D_MODEL), lambda j, i: (i, 0)), pl.BlockSpec((FF_BLOCK, D_MODEL), lambda j, i: (j, 0))],
        out_specs=pl.BlockSpec((tm, FF_BLOCK), lambda j, i: (i, j)), out_shape=_sds((t, D_FF), BF),
        compiler_params=_params("parallel", "parallel"),
    )(h2, w_upt)


def _mlp_down_loss(up, w_down, x1, pw, target):
    t = up.shape[0]
    tm = min(MATMUL_BLOCK, t)

    def body(up_ref, w_ref, x1_ref, pw_ref, tg_ref, dy_ref, dx2_ref, loss_ref, dpw_ref):
        i = pl.program_id(0)

        @pl.when(i == 0)
        def _():
            loss_ref[...] = jnp.zeros_like(loss_ref)
            dpw_ref[...] = jnp.zeros_like(dpw_ref)

        u = jnp.maximum(up_ref[...].astype(F32), 0.0)
        y = jnp.dot((u * u).astype(BF), w_ref[...], preferred_element_type=F32)
        r = lax.rsqrt(jnp.mean(y * y, axis=-1, keepdims=True) + EPS)
        yh = y * r
        pw = pw_ref[...]
        err = x1_ref[...] + yh * pw - tg_ref[...]
        part = jnp.sum(jnp.sum(err * err, axis=-1, keepdims=True), axis=0, keepdims=True) * (0.5 / D_MODEL)
        loss_ref[...] += jnp.broadcast_to(part, loss_ref.shape)
        dx2 = err * (1.0 / D_MODEL)
        dx2_ref[...] = dx2
        dpw_ref[...] += jnp.sum(dx2 * yh, axis=0, keepdims=True)
        dyh = dx2 * pw
        dy_ref[...] = (r * (dyh - yh * jnp.mean(dyh * yh, axis=-1, keepdims=True))).astype(BF)

    tok = lambda w: pl.BlockSpec((tm, w), lambda i: (i, 0))
    vec = lambda w: pl.BlockSpec((1, w), lambda i: (0, 0))
    return pl.pallas_call(
        body, name="mlp_down_loss", grid=(t // tm,),
        in_specs=[tok(D_FF), pl.BlockSpec((D_FF, D_MODEL), lambda i: (0, 0)), tok(D_MODEL), vec(D_MODEL), tok(D_MODEL)],
        out_specs=[tok(D_MODEL), tok(D_MODEL), vec(LANES), vec(D_MODEL)],
        out_shape=[_sds((t, D_MODEL), BF), _sds((t, D_MODEL)), _sds((1, LANES)), _sds((1, D_MODEL))],
        compiler_params=_params("arbitrary"),
    )(up, w_down, x1, pw, target)


def _mlp_bwd_act(dy, w_down, up):
    t = dy.shape[0]
    tm = min(MATMUL_BLOCK, t)

    def body(dy_ref, w_ref, up_ref, dup_ref):
        da = lax.dot_general(dy_ref[...], w_ref[...], (((1,), (1,)), ((), ())), preferred_element_type=F32)
        dup_ref[...] = (da * (2.0 * jnp.maximum(up_ref[...].astype(F32), 0.0))).astype(BF)

    return pl.pallas_call(
        body, name="mlp_bwd_act", grid=(D_FF // FF_BLOCK, t // tm),
        in_specs=[pl.BlockSpec((tm, D_MODEL), lambda j, i: (i, 0)), pl.BlockSpec((FF_BLOCK, D_MODEL), lambda j, i: (j, 0)),
                  pl.BlockSpec((tm, FF_BLOCK), lambda j, i: (i, j))],
        out_specs=pl.BlockSpec((tm, FF_BLOCK), lambda j, i: (i, j)), out_shape=_sds((t, D_FF), BF),
        compiler_params=_params("parallel", "parallel"),
    )(dy, w_down, up)


def _mlp_bwd_in(dup, w_up, x1, plw, dx2, mixed, pmw, after):
    t = dup.shape[0]
    tm = min(MATMUL_BLOCK, t)

    def body(dup_ref, w_ref, x1_ref, plw_ref, dx2_ref, mx_ref, pmw_ref, dx1_ref, dmixed_ref, dplw_ref, dpmw_ref):
        i = pl.program_id(0)

        @pl.when(i == 0)
        def _():
            dplw_ref[...] = jnp.zeros_like(dplw_ref)
            dpmw_ref[...] = jnp.zeros_like(dpmw_ref)

        dh = jnp.dot(dup_ref[...], w_ref[...], preferred_element_type=F32)
        x1 = x1_ref[...]
        r = lax.rsqrt(jnp.mean(x1 * x1, axis=-1, keepdims=True) + EPS)
        xh = x1 * r
        dplw_ref[...] += jnp.sum(dh * xh, axis=0, keepdims=True)
        dxh = dh * plw_ref[...]
        dx1 = dx2_ref[...] + r * (dxh - xh * jnp.mean(dxh * xh, axis=-1, keepdims=True))
        dx1_ref[...] = dx1
        mx = mx_ref[...]
        r2 = lax.rsqrt(jnp.mean(mx * mx, axis=-1, keepdims=True) + EPS)
        mh = mx * r2
        dpmw_ref[...] += jnp.sum(dx1 * mh, axis=0, keepdims=True)
        dmh = dx1 * pmw_ref[...]
        dmixed_ref[...] = (r2 * (dmh - mh * jnp.mean(dmh * mh, axis=-1, keepdims=True))).astype(BF)

    tok = lambda w: pl.BlockSpec((tm, w), lambda i: (i, 0))
    vec = lambda w: pl.BlockSpec((1, w), lambda i: (0, 0))
    return pl.pallas_call(
        _ordered(body), name="mlp_bwd_in", grid=(t // tm,),
        in_specs=[ANY_SPEC, tok(D_FF), pl.BlockSpec((D_FF, D_MODEL), lambda i: (0, 0)), tok(D_MODEL),
                  vec(D_MODEL), tok(D_MODEL), tok(D_MODEL), vec(D_MODEL)],
        out_specs=[tok(D_MODEL), tok(D_MODEL), vec(D_MODEL), vec(D_MODEL)],
        out_shape=[_sds((t, D_MODEL)), _sds((t, D_MODEL), BF), _sds((1, D_MODEL)), _sds((1, D_MODEL))],
        compiler_params=_params("arbitrary"),
    )(after, dup, w_up, x1, plw, dx2, mixed, pmw)


def _wgrad(a, b, a_cols, split=1, a_fn=None, a_block0=0, name="wgrad"):
    t, b_cols = b.shape
    n_a = (a.shape[1] - a_block0 * a_cols) // a_cols if a_block0 else a.shape[1] // a_cols

    def body(a_ref, b_ref, o_ref):
        av = a_ref[...]
        if a_fn is not None:
            av = a_fn(av)
        o_ref[...] = _mm_tn(av, b_ref[...]).astype(BF).reshape(o_ref.shape)

    return pl.pallas_call(
        body, name=name, grid=(n_a,),
        in_specs=[pl.BlockSpec((t, a_cols), lambda i: (0, i + a_block0)), pl.BlockSpec((t, b_cols), lambda i: (0, 0))],
        out_specs=pl.BlockSpec((split, a_cols // split, b_cols), lambda i: (i, 0, 0)),
        out_shape=_sds((n_a * split, a_cols // split, b_cols), BF),
        compiler_params=_params("parallel"),
    )(a, b)


def _wgrad_pre_t(at, b, b_cols, name):
    rows, t = at.shape
    n_b = b.shape[1] // b_cols

    def body(a_ref, b_ref, o_ref):
        o_ref[0] = jnp.dot(a_ref[...], b_ref[...], preferred_element_type=F32).astype(BF)

    return pl.pallas_call(
        body, name=name, grid=(n_b,),
        in_specs=[pl.BlockSpec((rows, t), lambda j: (0, 0)), pl.BlockSpec((t, b_cols), lambda j: (0, j))],
        out_specs=pl.BlockSpec((1, rows, b_cols), lambda j: (j, 0, 0)), out_shape=_sds((n_b, rows, b_cols), BF),
        compiler_params=_params("parallel"),
    )(at, b)


def _small_bwd(proj, fb, al, dtb, dcq, dckt, dbe, dge, dproj):
    t = proj.shape[0]

    def body(sm_ref, fb_ref, al_ref, dtb_ref, dcq_ref, dckt_ref, dbe_ref, dge_ref, _, dsm_ref, dvec_ref):
        s = sm_ref[...]
        lane = _iota((1, LANES), 1)
        dcum = dcq_ref[...] - dckt_ref[...].T
        row = _iota((t, LANES), 0)
        step = 1
        while step < t:
            dcum = dcum + _shift_up(dcum, step, row)
            step *= 2
        dff = dcum * _sigmoid(-(s + fb_ref[...]))
        dbeta = jnp.zeros((t, LANES), F32)
        dg = jnp.zeros((t, LANES), F32)
        for hd in range(N_GDN_HEADS):
            dbeta = jnp.where(lane == SM_GB + hd, dbe_ref[:, hd * LANES:hd * LANES + 1], dbeta)
            dg = jnp.where(lane == SM_GA + hd, dge_ref[:, hd * LANES:hd * LANES + 1], dg)
        beta = _sigmoid(s)
        dgb = dbeta * beta * (1.0 - beta)
        za = s + dtb_ref[...]
        nea = -jnp.exp(al_ref[...])
        dga = dg * nea * _sigmoid(za)
        is_f = lane < SM_GB
        is_b = (lane >= SM_GB) & (lane < SM_GA)
        is_a = (lane >= SM_GA) & (lane < SM_GA + 4)
        dsm_ref[...] = jnp.where(is_f, dff, jnp.where(is_b, dgb, jnp.where(is_a, dga, 0.0))).astype(BF)
        dvec_ref[...] = jnp.zeros_like(dvec_ref)
        dvec_ref[0:1, :] = jnp.sum(jnp.where(is_f, dff, 0.0), axis=0, keepdims=True)
        dvec_ref[1:2, :] = jnp.sum(jnp.where(is_a, dg * nea * _softplus(za), 0.0), axis=0, keepdims=True)
        dvec_ref[2:3, :] = jnp.sum(jnp.where(is_a, dga, 0.0), axis=0, keepdims=True)

    vec = pl.BlockSpec((1, LANES), lambda i: (0, 0))
    full = lambda r, c: pl.BlockSpec((r, c), lambda i: (0, 0))
    small = pl.BlockSpec((t, LANES), lambda i: (0, COL_SMALL // LANES))
    return pl.pallas_call(
        body, name="small_bwd", grid=(1,),
        in_specs=[small, vec, vec, vec, full(t, LANES), full(LANES, t), full(t, 512), full(t, 512), ANY_SPEC],
        out_specs=[small, full(8, LANES)], out_shape=[_sds(dproj.shape, BF), _sds((8, LANES))],
        input_output_aliases={8: 0}, compiler_params=_params("arbitrary"),
    )(proj, fb, al, dtb, dcq, dckt, dbe, dge, dproj)


def _in_bwd(dproj, wt_al, x, nw, dx1, after):
    t = x.shape[0]
    tm = min(MATMUL_BLOCK, t)

    def body(dp_ref, w_ref, x_ref, nw_ref, dx1_ref, dx_ref, dnw_ref):
        i = pl.program_id(0)

        @pl.when(i == 0)
        def _():
            dnw_ref[...] = jnp.zeros_like(dnw_ref)

        dh = jnp.dot(dp_ref[...], w_ref[...], preferred_element_type=F32)
        xv = x_ref[...]
        r = lax.rsqrt(jnp.mean(xv * xv, axis=-1, keepdims=True) + EPS)
        xh = xv * r
        dnw_ref[...] += jnp.sum(dh * xh, axis=0, keepdims=True)
        dxh = dh * nw_ref[...]
        dx_ref[...] = dx1_ref[...] + r * (dxh - xh * jnp.mean(dxh * xh, axis=-1, keepdims=True))

    tok = lambda w: pl.BlockSpec((tm, w), lambda i: (i, 0))
    vec = lambda w: pl.BlockSpec((1, w), lambda i: (0, 0))
    return pl.pallas_call(
        _ordered(body), name="in_bwd", grid=(t // tm,),
        in_specs=[ANY_SPEC, tok(PROJ_W), pl.BlockSpec((PROJ_W, D_MODEL), lambda i: (0, 0)), tok(D_MODEL), vec(D_MODEL),
                  tok(D_MODEL)],
        out_specs=[tok(D_MODEL), vec(D_MODEL)], out_shape=[_sds((t, D_MODEL)), _sds((1, D_MODEL))],
        compiler_params=_params("arbitrary"),
    )(after, dproj, wt_al, x, nw, dx1)


def _row(v, width=None):
    v = v.reshape(1, -1).astype(F32)
    if width is not None and v.shape[1] < width:
        v = jnp.pad(v, ((0, 0), (0, width - v.shape[1])))
    return v


def _lane_vec(v, first):
    return jnp.pad(v.astype(F32), (first, LANES - first - v.shape[0])).reshape(1, LANES)


def _local_step(x, target, wt_al, started, late_weights, on_grads, convw, pre_mix_norm, fox_f_bias, fox_out_norm,
                gdn_a_log, gdn_dt_bias, gdn_out_norm, post_mix_norm, pre_mlp_norm, post_mlp_norm):
    t = x.shape[0]
    nch = t // CHUNK
    nw, pmw, plw, pw = _row(pre_mix_norm), _row(post_mix_norm), _row(pre_mlp_norm), _row(post_mlp_norm)
    fb, al, dtb = _lane_vec(fox_f_bias, SM_FF), _lane_vec(gdn_a_log, SM_GA), _lane_vec(gdn_dt_bias, SM_GA)
    fnw = _row(jnp.tile(fox_out_norm, 2))
    gnw = _row(gdn_out_norm)

    proj, h = _norm_proj(x, nw, wt_al, started)
    cumt, beta, g = _small_prep(proj, fb, al, dtb)
    o_fox, lse, fox_n = _fox_fwd(proj, cumt, fnw)
    qn, kn, cv, gc, be, mmat, amat = _gdn_prep(proj, convw, beta, g)
    n_prob = N_GDN_HEADS * nch
    m3 = mmat.reshape(n_prob, CHUNK, CHUNK)
    if n_prob < LANES:
        m3 = jnp.pad(m3, ((0, LANES - n_prob), (0, 0), (0, 0)))
    tinv = _tri_inverse(m3)[:n_prob].reshape(N_GDN_HEADS, nch, CHUNK, CHUNK)
    token = late_weights("mlp_relay", tinv)
    gdn_o, s_all, vn_all = _gdn_scan(qn, kn, cv, be, gc, tinv, amat)
    w_out = late_weights("w_out", gdn_o)
    x1, h2, mixed, omix, h2t = _mix_out(fox_n, gdn_o, proj, gnw, w_out, x, pmw, plw, token)
    w_up, w_down = late_weights("mlp", h2)
    up = _mlp_up(h2, w_up)
    dy, dx2, loss, d_pw = _mlp_down_loss(up, w_down, x1, pw, target)

    dup = _mlp_bwd_act(dy, w_down, up)
    relu2 = lambda u: jnp.square(jnp.maximum(u.astype(F32), 0.0))
    g_down = _wgrad(up, dy, D_FF // N_DEV, a_fn=relu2, name="wgrad_down")
    g_up = _wgrad_pre_t(h2t, dup, D_FF // N_DEV, name="wgrad_up")
    token = on_grads("mlp", (g_up, g_down))
    dx1, dmixed, d_plw, d_pmw = _mlp_bwd_in(dup, w_up, x1, plw, dx2, mixed, pmw, token)
    token = on_grads("w_out", _wgrad(omix, dmixed, 512, split=4, name="wgrad_out"))
    do_fox, dgo, dproj, d_fnw, d_gnw = _out_bwd(dmixed, w_out, o_fox, gdn_o, proj, fnw, gnw, token)
    dproj, dcq, dckt = _fox_bwd(proj, cumt, lse, o_fox, do_fox, dproj)
    dqn, dkn, dcv, dbe, dge = _gdn_bwd(qn, kn, cv, be, gc, tinv, amat, s_all, vn_all, dgo)
    dproj, dwq, dwk, dwv = _gdn_bwd_conv(proj, convw, dqn, dkn, dcv, dproj)
    dproj, dvec = _small_bwd(proj, fb, al, dtb, dcq, dckt, dbe, dge, dproj)
    g_main = _wgrad(dproj, h, WGRAD_IN_ROWS, name="wgrad_in")
    g_tail = _wgrad(dproj, h, LANES, a_block0=COL_SMALL // LANES, name="wgrad_in_small")
    token = on_grads("w_in", (g_main, g_tail))
    grad_x, d_nw = _in_bwd(dproj, wt_al, x, nw, dx1, token)
    small = dict(norms=(d_nw, d_pmw, d_plw, d_pw), fox_out_norm=d_fnw, gdn_out_norm=d_gnw, loss=loss, vectors=dvec,
                 conv=(dwq, dwk, dwv))
    return grad_x, small


MESH_IDS = pl.DeviceIdType.MESH
CHIP_FLIPS = ((0, 0), (1, 0), (0, 1), (1, 1))


def _place():
    return lax.axis_index("x"), lax.axis_index("y"), lax.axis_index("c")


def _all_gather(blocks):
    n = len(blocks)

    def body(*refs):
        ins, outs, (send_sems, recv_sems, local_sems) = refs[:n], refs[n:2 * n], refs[2 * n:]
        x, y, c = _place()
        sibling = (x, y, 1 - c)
        chips = [(x ^ fx, y ^ fy) for fx, fy in CHIP_FLIPS[1:]]

        def slot(out, px, py, pc):
            return out.at[4 * px + 2 * py + pc]

        def copy(a, k, block, to, src=None):
            return pltpu.make_async_remote_copy(
                src_ref=slot(outs[a], *block) if src is None else src, dst_ref=slot(outs[a], *block),
                send_sem=send_sems.at[a, k], recv_sem=recv_sems.at[a, k], device_id=to, device_id_type=MESH_IDS)

        pending = []
        for a in range(n):
            mine = pltpu.make_async_copy(ins[a], slot(outs[a], x, y, c), local_sems.at[a])
            mine.start()
            pending.append(mine)
        sends = []
        for a in range(n):
            first = [copy(a, 0, (x, y, c), sibling, src=ins[a])]
            first += [copy(a, 1 + j, (x, y, c), (*chip, c), src=ins[a]) for j, chip in enumerate(chips)]
            for cp in first:
                cp.start()
            sends += first
        for a in range(n):
            for j, chip in enumerate(chips):
                copy(a, 1 + j, (*chip, c), (x, y, c)).wait_recv()
                fwd = copy(a, 4 + j, (*chip, c), sibling)
                fwd.start()
                sends.append(fwd)
        for a in range(n):
            copy(a, 0, sibling, (x, y, c)).wait_recv()
            for j, chip in enumerate(chips):
                copy(a, 4 + j, (*chip, 1 - c), (x, y, c)).wait_recv()
        for cp in sends:
            cp.wait_send()
        for cp in pending:
            cp.wait()

    return pl.pallas_call(
        body, name="all_gather_weights", in_specs=[ANY_SPEC] * n, out_specs=[ANY_SPEC] * n,
        out_shape=[_sds((N_DEV,) + b.shape, b.dtype) for b in blocks],
        scratch_shapes=[pltpu.SemaphoreType.DMA((n, 7)), pltpu.SemaphoreType.DMA((n, 7)), pltpu.SemaphoreType.DMA((n,))],
        compiler_params=pltpu.CompilerParams(has_side_effects=True),
    )(*blocks)


def _adamw(w, g, m, v):
    m = ADAM_B1 * m + (1.0 - ADAM_B1) * g
    v = ADAM_B2 * v + (1.0 - ADAM_B2) * (g * g)
    m_hat = m / (1.0 - ADAM_B1 ** ADAM_STEP)
    v_hat = v / (1.0 - ADAM_B2 ** ADAM_STEP)
    return -ADAM_LR * (m_hat / (jnp.sqrt(v_hat) + ADAM_EPS) + ADAM_WD * w), m, v


def _pair_reduce(g, name):
    _, r, c_ = g.shape
    n = len(CHIP_FLIPS)

    def body(g_ref, out_ref, sib_buf, send_sems, recv_sems):
        x, y, c = _place()
        chips = [(x ^ fx, y ^ fy) for fx, fy in CHIP_FLIPS]
        piece = lambda chip, core: g_ref.at[4 * chip[0] + 2 * chip[1] + core]
        copies = [pltpu.make_async_remote_copy(
            src_ref=piece(chip, 1 - c), dst_ref=sib_buf.at[j], send_sem=send_sems.at[j], recv_sem=recv_sems.at[j],
            device_id=(x, y, 1 - c), device_id_type=MESH_IDS) for j, chip in enumerate(chips)]
        for cp in copies:
            cp.start()
        for j, chip in enumerate(chips):
            copies[j].wait_recv()
            out_ref[j] = (piece(chip, c)[...].astype(F32) + sib_buf[j].astype(F32)).astype(BF)
        for cp in copies:
            cp.wait_send()

    return pl.pallas_call(
        body, name=name, in_specs=[VMEM_SPEC], out_specs=VMEM_SPEC, out_shape=_sds((n, r, c_), BF),
        scratch_shapes=[pltpu.VMEM((n, r, c_), BF), pltpu.SemaphoreType.DMA((n,)), pltpu.SemaphoreType.DMA((n,))],
        compiler_params=pltpu.CompilerParams(vmem_limit_bytes=VMEM_LIMIT, has_side_effects=True),
    )(g)


HBM_SPEC = pl.BlockSpec(memory_space=pltpu.HBM)
SEM_SPEC = pl.BlockSpec(memory_space=pltpu.SEMAPHORE)
DATAFLOW = pltpu.SideEffectType.DATAFLOW_SIDE_EFFECTING


def _peers():
    x, y, c = _place()
    return 4 * x + 2 * y + c, [(x ^ (k >> 2), y ^ ((k >> 1) & 1), c ^ (k & 1)) for k in range(1, N_DEV)]


def _peer_index(peer):
    return 4 * peer[0] + 2 * peer[1] + peer[2]


def _zones_with_own(srcs, name, after, dtype):
    n = len(srcs)

    def body(me_ref, *refs):
        outs = refs[n + 1:]
        for a in range(n):
            val = refs[a][...].astype(dtype)
            outs[a][0] = val
            outs[n + a][...] = val

    shapes = [s_.shape for s_ in srcs]
    mine = lambda sh: pl.BlockSpec((1,) + sh, lambda i, me_ref: (me_ref[0], 0, 0))
    whole = lambda sh: pl.BlockSpec(sh, lambda i, me_ref: (0, 0))
    x, y, c = _place()
    out = pl.pallas_call(
        body, name=name,
        grid_spec=pltpu.PrefetchScalarGridSpec(
            num_scalar_prefetch=1, grid=(1,), in_specs=[whole(sh) for sh in shapes] + [ANY_SPEC],
            out_specs=[mine(sh) for sh in shapes] + [whole(sh) for sh in shapes]),
        out_shape=[_sds((N_DEV,) + sh, dtype) for sh in shapes] + [_sds(sh, dtype) for sh in shapes],
        compiler_params=_params("arbitrary"),
    )((4 * x + 2 * y + c).astype(jnp.int32).reshape(1), *srcs, after)
    return out[:n], out[n:]


def _exchange_start(srcs, zones, pieces, name, chips=False):
    n = len(srcs)
    fresh = zones is None
    if fresh:
        slots = len(CHIP_FLIPS) if chips else N_DEV
        zones = [_sds((slots,) + (v.shape[1:] if pieces else v.shape), v.dtype) for v in srcs]
    n_in = n if fresh else 2 * n

    def body(*refs):
        ins, sems, token = refs[:n], refs[n_in:n_in + 2 * n], refs[-1]
        zs = refs[n_in + 3 * n:n_in + 4 * n] if fresh else refs[n:2 * n]
        me, peers = _peers()
        x, y, c = _place()
        if chips and pieces:
            routes = [((x ^ fx, y ^ fy, c), j, j) for j, (fx, fy) in enumerate(CHIP_FLIPS) if j]
        elif chips:
            routes = [((x ^ fx, y ^ fy, c), None, me) for fx, fy in CHIP_FLIPS[1:]]
        else:
            routes = [(peer, _peer_index(peer) if pieces else None, me) for peer in peers]
        for peer, src_slot, dst_slot in routes:
            for a in range(n):
                pltpu.make_async_remote_copy(
                    src_ref=ins[a] if src_slot is None else ins[a].at[src_slot], dst_ref=zs[a].at[dst_slot],
                    send_sem=sems[2 * a], recv_sem=sems[2 * a + 1], device_id=peer, device_id_type=MESH_IDS).start()
        token[...] = jnp.zeros_like(token)

    hbm = lambda v: pltpu.with_memory_space_constraint(v, pltpu.HBM)
    out = pl.pallas_call(
        body, name=name,
        out_shape=tuple([pltpu.SemaphoreType.DMA(())] * (2 * n) + [pltpu.HBM(v.shape, v.dtype) for v in srcs]
                        + [pltpu.HBM(z.shape, z.dtype) for z in zones] + [_sds((8, LANES))]),
        in_specs=[HBM_SPEC] * n_in, out_specs=tuple([SEM_SPEC] * (2 * n) + [HBM_SPEC] * (2 * n) + [VMEM_SPEC]),
        input_output_aliases={i: 2 * n + i for i in range(n_in)},
        compiler_params=pltpu.CompilerParams(has_side_effects=DATAFLOW),
    )(*[hbm(v) for v in srcs], *([] if fresh else [hbm(z) for z in zones]))
    return out[:2 * n], out[2 * n:3 * n], out[3 * n:4 * n], out[-1]


def _relay_start(zones, name):
    n = len(zones)

    def body(*refs):
        zs, sems, token = refs[:n], refs[n:3 * n], refs[-1]
        x, y, c = _place()
        for fx, fy in CHIP_FLIPS:
            slot = 4 * (x ^ fx) + 2 * (y ^ fy) + c
            for a in range(n):
                pltpu.make_async_remote_copy(
                    src_ref=zs[a].at[slot], dst_ref=zs[a].at[slot], send_sem=sems[2 * a], recv_sem=sems[2 * a + 1],
                    device_id=(x, y, 1 - c), device_id_type=MESH_IDS).start()
        token[...] = jnp.zeros_like(token)

    out = pl.pallas_call(
        body, name=name,
        out_shape=tuple([pltpu.SemaphoreType.DMA(())] * (2 * n) + [pltpu.HBM(z.shape, z.dtype) for z in zones]
                        + [_sds((8, LANES))]),
        in_specs=[HBM_SPEC] * n, out_specs=tuple([SEM_SPEC] * (2 * n) + [HBM_SPEC] * n + [VMEM_SPEC]),
        input_output_aliases={i: 2 * n + i for i in range(n)},
        compiler_params=pltpu.CompilerParams(has_side_effects=DATAFLOW),
    )(*[pltpu.with_memory_space_constraint(z, pltpu.HBM) for z in zones])
    return out[:2 * n], [], out[2 * n:3 * n], out[-1]


def _exchange_wait(sems, srcs, zones, after, name, chips=False, n_copies=None):
    n, n_src = len(zones), len(srcs)
    after = list(after) if isinstance(after, (list, tuple)) else [after]
    n_copies = n_copies or (len(CHIP_FLIPS) - 1 if chips else N_DEV - 1)

    def body(*refs):
        zs, sm = refs[n_src:n_src + n], refs[n_src + n:n_src + 3 * n]
        me, peers = _peers()
        for a in range(n):
            seven = zs[a].at[pl.ds(0, n_copies)]
            cp = pltpu.make_async_remote_copy(src_ref=seven, dst_ref=seven, send_sem=sm[2 * a], recv_sem=sm[2 * a + 1],
                                              device_id=peers[0], device_id_type=MESH_IDS)
            cp.wait_send()
            cp.wait_recv()

    out = pl.pallas_call(
        body, name=name, out_shape=tuple([pltpu.HBM(v.shape, v.dtype) for v in srcs] + [pltpu.HBM(z.shape, z.dtype) for z in zones]),
        in_specs=[HBM_SPEC] * (n_src + n) + [SEM_SPEC] * (2 * n) + [ANY_SPEC] * len(after),
        out_specs=tuple([HBM_SPEC] * (n_src + n)), input_output_aliases={i: i for i in range(n_src + n)},
        compiler_params=pltpu.CompilerParams(has_side_effects=DATAFLOW),
    )(*srcs, *zones, *sems, *after)
    return out[:n_src], out[n_src:]


def _sum_adamw(zone, own, w, m, v, name, chips=False):
    n_slots, r, c_ = zone.shape
    rb, cb = (ADAM_ROWS, 512) if r % ADAM_ROWS == 0 else (r, 256)

    def body(me_ref, z_ref, own_ref, w_ref, m_ref, v_ref, grad_ref, delta_ref, nm_ref, nv_ref):
        total = None
        for d in range(n_slots):
            part = jnp.where(me_ref[0] == d, own_ref[0], z_ref[d]).astype(F32)
            total = part if total is None else total + part
        grad_ref[...] = total
        delta_ref[...], nm_ref[...], nv_ref[...] = _adamw(w_ref[...], total, m_ref[...], v_ref[...])

    x, y, c = _place()
    mine = 0 * x if chips else 4 * x + 2 * y + c
    blk = pl.BlockSpec((rb, cb), lambda i, j, me_ref: (i, j))
    return pl.pallas_call(
        body, name=name,
        grid_spec=pltpu.PrefetchScalarGridSpec(
            num_scalar_prefetch=1, grid=(r // rb, c_ // cb),
            in_specs=[pl.BlockSpec((n_slots, rb, cb), lambda i, j, me_ref: (0, i, j)),
                      pl.BlockSpec((1, rb, cb), lambda i, j, me_ref: (me_ref[0], i, j)), blk, blk, blk],
            out_specs=[blk] * 4),
        out_shape=[_sds((r, c_))] * 4, compiler_params=_params("parallel", "parallel"),
    )(mine.astype(jnp.int32).reshape(1), zone, own, w, m, v)


SMALL_NORMS = ("pre_mix_norm", "post_mix_norm", "pre_mlp_norm", "post_mlp_norm")
SMALL_ORDER = SMALL_NORMS + ("fox_out_norm", "gdn_out_norm", "fox_f_bias", "gdn_a_log", "gdn_dt_bias", "gdn_conv_w")
CONV_SLAB_ROWS, CONV_SLAB_LANES = 8, 256


def _small_pack(small):
    def body(n0, n1, n2, n3, fnw_ref, gnw_ref, loss_ref, vec_ref, out_ref):
        out_ref[...] = jnp.zeros_like(out_ref)
        for i, ref in enumerate((n0, n1, n2, n3)):
            out_ref[i:i + 1, :] = ref[...]
        out_ref[4:5, 0:LANES] = fnw_ref[...]
        out_ref[4:5, LANES:2 * LANES] = gnw_ref[...]
        out_ref[4:5, 2 * LANES:3 * LANES] = loss_ref[...]
        out_ref[5:8, 0:LANES] = vec_ref[0:3, :]

    return pl.pallas_call(body, name="small_pack", in_specs=[VMEM_SPEC] * 8, out_specs=VMEM_SPEC,
                          out_shape=_sds((8, D_MODEL)))(*small["norms"], small["fox_out_norm"], small["gdn_out_norm"],
                                                        small["loss"], small["vectors"])


def _conv_slabs(dconv):
    blocks = dconv.reshape(CONV_K, N_DEV, -1).transpose(1, 0, 2)
    blocks = jnp.pad(blocks, ((0, 0), (0, CONV_SLAB_ROWS - CONV_K), (0, CONV_SLAB_LANES - blocks.shape[2])))
    return blocks.reshape(N_DEV * CONV_SLAB_ROWS, CONV_SLAB_LANES)


def _small_update(zone, conv_zone, own, own_conv, w, m, v):
    n = len(SMALL_ORDER)
    n_conv = w["gdn_conv_w"].shape[1]

    def body(me_ref, z_ref, zc_ref, own_ref, ownc_ref, *refs):
        params, loss_ref, outs, (tot, totc) = refs[:3 * n], refs[3 * n], refs[3 * n + 1:7 * n + 1], refs[-2:]
        total, total_c = None, None
        for d in range(N_DEV):
            part = jnp.where(me_ref[0] == d, own_ref[...], z_ref[d])
            part_c = jnp.where(me_ref[0] == d, ownc_ref[...], zc_ref[d])
            total, total_c = (part, part_c) if d == 0 else (total + part, total_c + part_c)
        tot[...] = total
        totc[...] = total_c
        loss_ref[...] = tot[4, 2 * LANES:2 * LANES + 1]
        mine = totc[pl.ds(pl.multiple_of(me_ref[0] * CONV_SLAB_ROWS, CONV_SLAB_ROWS), CONV_SLAB_ROWS), :]
        g = dict(zip(SMALL_NORMS, (tot[0], tot[1], tot[2], tot[3])))
        g.update(fox_out_norm=tot[4, 0:FOX_HEAD_DIM], gdn_out_norm=tot[4, LANES:LANES + GDN_HEAD_DIM],
                 fox_f_bias=tot[5, SM_FF:SM_FF + N_FOX_HEADS], gdn_a_log=tot[6, SM_GA:SM_GA + N_GDN_HEADS],
                 gdn_dt_bias=tot[7, SM_GA:SM_GA + N_GDN_HEADS], gdn_conv_w=mine[0:CONV_K, 0:n_conv])
        for i, name in enumerate(SMALL_ORDER):
            w_ref, m_ref, v_ref = params[3 * i:3 * i + 3]
            outs[4 * i][...] = g[name]
            outs[4 * i + 1][...], outs[4 * i + 2][...], outs[4 * i + 3][...] = _adamw(w_ref[...], g[name], m_ref[...],
                                                                                     v_ref[...])

    x, y, c = _place()
    operands = [a[name] for name in SMALL_ORDER for a in (w, m, v)]
    out = pl.pallas_call(
        body, name="small_update",
        in_specs=[pl.BlockSpec(memory_space=pltpu.SMEM)] + [VMEM_SPEC] * (4 + 3 * n), out_specs=[VMEM_SPEC] * (1 + 4 * n),
        out_shape=[_sds((1,))] + [_sds(w[name].shape) for name in SMALL_ORDER for _ in range(4)],
        scratch_shapes=[pltpu.VMEM(zone.shape[1:], F32), pltpu.VMEM(conv_zone.shape[1:], F32)],
    )((4 * x + 2 * y + c).astype(jnp.int32).reshape(1), zone, conv_zone, own, own_conv, *operands)
    return out[0][0], {name: out[1 + 4 * i:5 + 4 * i] for i, name in enumerate(SMALL_ORDER)}


def _native_rows():
    groups = []
    for first, n_groups in ((0, N_FOX_HEADS // 2), (D_FOX * 3 + N_FOX_HEADS, N_GDN_HEADS)):
        for g in range(n_groups):
            groups += [(first + part * n_groups * LANES + g * LANES, first + part * n_groups * LANES + (g + 1) * LANES)
                       for part in range(3)]
    return tuple(groups) + ((3088, 3600), (1536, 1544), (3080, 3088))


NATIVE_ROWS = _native_rows()


W_IN_PIECE = D_PROJ // N_DEV
WGRAD_IN_ROWS = 512
SHUFFLE_LANES = 256


def _to_aligned_moves():
    moves, o = [], 0
    for lo, hi in NATIVE_ROWS:
        r = lo
        while r < hi:
            d = r // W_IN_PIECE
            k = min(hi, (d + 1) * W_IN_PIECE) - r
            moves.append((0, d, r - d * W_IN_PIECE, 0, o, k))
            r, o = r + k, o + k
    return moves


def _from_aligned_moves():
    moves = []
    for _, d, a, _, o, k in _to_aligned_moves():
        while k:
            n = min(k, WGRAD_IN_ROWS - o % WGRAD_IN_ROWS) if o < COL_SMALL else k
            moves.append((0, o // WGRAD_IN_ROWS, o % WGRAD_IN_ROWS, d, a, n) if o < COL_SMALL else
                         (1, 0, o - COL_SMALL, d, a, n))
            o, a, k = o + n, a + n, k - n
    return moves


def _shuffle_rows(srcs, moves, out_shape, name):
    c = srcs[0].shape[-1]

    def body(*refs):
        s_refs, o_ref, s_f, o_f = refs[:len(srcs)], refs[len(srcs)], refs[len(srcs) + 1:-1], refs[-1]
        for s_ref, f in zip(s_refs, s_f):
            f[...] = s_ref[...].astype(F32)
        o_f[...] = jnp.zeros_like(o_f)
        for i, ss, so, ds, do, k in moves:
            o_f[ds, pl.ds(do, k), :] = s_f[i][ss, pl.ds(so, k), :]
        o_ref[...] = o_f[...].astype(BF)

    blk = lambda shape: pl.BlockSpec(tuple(shape[:-1]) + (SHUFFLE_LANES,), lambda j: (0, 0, j))
    scratch = lambda shape: pltpu.VMEM(tuple(shape[:-1]) + (SHUFFLE_LANES,), F32)
    return pl.pallas_call(
        body, name=name, grid=(c // SHUFFLE_LANES,), in_specs=[blk(s.shape) for s in srcs], out_specs=blk(out_shape),
        out_shape=_sds(out_shape, BF), scratch_shapes=[scratch(s.shape) for s in srcs] + [scratch(out_shape)],
        compiler_params=_params("parallel"),
    )(*srcs)


def _cols_from_pieces(p):
    return p.transpose(1, 0, 2).reshape(p.shape[1], -1)


WEIGHT_ORDER = ("pre_mix_norm", "w_in", "fox_f_bias", "fox_out_norm", "gdn_conv_w", "gdn_a_log", "gdn_dt_bias",
                "gdn_out_norm", "w_out", "post_mix_norm", "pre_mlp_norm", "w_up", "w_down", "post_mlp_norm")


def kernel(x, pre_mix_norm, w_in, fox_f_bias, fox_out_norm, gdn_conv_w, gdn_a_log, gdn_dt_bias, gdn_out_norm, w_out, post_mix_norm, pre_mlp_norm, w_up, w_down, post_mlp_norm, loss_target, m_pre_mix_norm, m_w_in, m_fox_f_bias, m_fox_out_norm, m_gdn_conv_w, m_gdn_a_log, m_gdn_dt_bias, m_gdn_out_norm, m_w_out, m_post_mix_norm, m_pre_mlp_norm, m_w_up, m_w_down, m_post_mlp_norm, v_pre_mix_norm, v_w_in, v_fox_f_bias, v_fox_out_norm, v_gdn_conv_w, v_gdn_a_log, v_gdn_dt_bias, v_gdn_out_norm, v_w_out, v_post_mix_norm, v_pre_mlp_norm, v_w_up, v_w_down, v_post_mlp_norm):
    w = dict(pre_mix_norm=pre_mix_norm, w_in=w_in, fox_f_bias=fox_f_bias, fox_out_norm=fox_out_norm,
             gdn_conv_w=gdn_conv_w, gdn_a_log=gdn_a_log, gdn_dt_bias=gdn_dt_bias, gdn_out_norm=gdn_out_norm, w_out=w_out,
             post_mix_norm=post_mix_norm, pre_mlp_norm=pre_mlp_norm, w_up=w_up, w_down=w_down, post_mlp_norm=post_mlp_norm)
    mom = dict(pre_mix_norm=m_pre_mix_norm, w_in=m_w_in, fox_f_bias=m_fox_f_bias, fox_out_norm=m_fox_out_norm,
               gdn_conv_w=m_gdn_conv_w, gdn_a_log=m_gdn_a_log, gdn_dt_bias=m_gdn_dt_bias, gdn_out_norm=m_gdn_out_norm,
               w_out=m_w_out, post_mix_norm=m_post_mix_norm, pre_mlp_norm=m_pre_mlp_norm, w_up=m_w_up, w_down=m_w_down,
               post_mlp_norm=m_post_mlp_norm)
    var = dict(pre_mix_norm=v_pre_mix_norm, w_in=v_w_in, fox_f_bias=v_fox_f_bias, fox_out_norm=v_fox_out_norm,
               gdn_conv_w=v_gdn_conv_w, gdn_a_log=v_gdn_a_log, gdn_dt_bias=v_gdn_dt_bias, gdn_out_norm=v_gdn_out_norm,
               w_out=v_w_out, post_mix_norm=v_post_mix_norm, pre_mlp_norm=v_pre_mlp_norm, w_up=v_w_up, w_down=v_w_down,
               post_mlp_norm=v_post_mlp_norm)

    win_g, conv_g = _all_gather([w_in.T.astype(BF), gdn_conv_w])
    wt_al = _shuffle_rows([win_g], _to_aligned_moves(), (1, PROJ_W, D_MODEL), "w_in_to_aligned")[0]
    convw = _cols_from_pieces(conv_g)
    gathers, after = {}, win_g
    for name, shards in (("w_out", [w_out]), ("mlp", [w_up.T, w_down])):
        zones, shards = _zones_with_own(shards, "gather_" + name + "_own", after, BF)
        gathers[name] = _exchange_start(shards, zones, False, "gather_" + name + "_start", chips=name == "mlp")
        after = gathers[name][3]

    def late_weights(name, after):
        if name == "mlp_relay":
            sems, shards, zones, _ = gathers["mlp"]
            _, zones = _exchange_wait(sems, shards, zones, after, "gather_mlp_wait", chips=True)
            gathers["mlp"] = _relay_start(zones, "gather_mlp_relay")
            return gathers["mlp"][3]
        sems, shards, zones, _ = gathers[name]
        _, got = _exchange_wait(sems, shards, zones, after, "gather_" + name + "_done",
                                n_copies=len(CHIP_FLIPS) if name == "mlp" else None)
        if name == "w_out":
            return got[0].reshape(D_MODEL, D_MODEL)
        return got[0].reshape(D_FF, D_MODEL), got[1].reshape(D_FF, D_MODEL)

    scatters = {}

    def on_grads(name, g):
        chips = name == "w_in"
        if name == "w_in":
            g = _shuffle_rows(list(g), _from_aligned_moves(), (N_DEV, W_IN_PIECE, D_MODEL), "w_in_grad_from_aligned")
            g = _pair_reduce(g, "pair_reduce_w_in")
        srcs = list(g) if name == "mlp" else [g]
        scatters[name] = _exchange_start(srcs, None, True, "scatter_" + name + "_start", chips=chips)
        return scatters[name][3]

    grad_x, small = _local_step(
        x[0], loss_target[0], wt_al, after, late_weights, on_grads, convw, pre_mix_norm,
        fox_f_bias, fox_out_norm, gdn_a_log, gdn_dt_bias, gdn_out_norm, post_mix_norm, pre_mlp_norm, post_mlp_norm)
    slabs = [_small_pack(small), _conv_slabs(jnp.concatenate(small["conv"], axis=1))]
    scatters["small"] = _exchange_start(slabs, None, False, "small_start")

    grads, delta, new_m, new_v = {}, {}, {}, {}
    after = scatters["small"][3]
    for name, members in (("mlp", ("w_up", "w_down")), ("w_out", ("w_out",)), ("small", ()), ("w_in", ("w_in",))):
        sems, srcs, zones, _ = scatters[name]
        srcs, zones = _exchange_wait(sems, srcs, zones, after, "scatter_" + name + "_wait", chips=name == "w_in")
        if name == "small":
            loss, updated = _small_update(*zones, *srcs, w, mom, var)
            for n, res in updated.items():
                grads[n], delta[n], new_m[n], new_v[n] = res
            after = grads["pre_mix_norm"]
        for n, zone, own in zip(members, zones, srcs):
            if n == "w_in":
                res = _sum_adamw(zone, own, w[n].T, mom[n].T, var[n].T, "adamw_" + n, chips=True)
                grads[n], delta[n], new_m[n], new_v[n] = [r.T for r in res]
            else:
                grads[n], delta[n], new_m[n], new_v[n] = _sum_adamw(zone, own, w[n], mom[n], var[n], "adamw_" + n)
        if members:
            after = [grads[n] for n in members]

    return (loss, grad_x[None], *[grads[n] for n in WEIGHT_ORDER], *[delta[n] for n in WEIGHT_ORDER],
            *[new_m[n] for n in WEIGHT_ORDER], *[new_v[n] for n in WEIGHT_ORDER])
```

```python
import jax
import jax.numpy as jnp
from jax import lax
from jax.experimental import pallas as pl
from jax.experimental.pallas import tpu as pltpu

F32 = jnp.float32
BF = jnp.bfloat16

D_MODEL = 1024
N_FOX_HEADS, FOX_HEAD_DIM = 8, 64
N_GDN_HEADS, GDN_HEAD_DIM = 4, 128
D_FOX = N_FOX_HEADS * FOX_HEAD_DIM
D_GDN = N_GDN_HEADS * GDN_HEAD_DIM
CHUNK = 64
CONV_K = 4
D_FF = 4 * D_MODEL
EPS = 1e-6
D_PROJ = 3600
N_DEV = 8

PROJ_W = 3712
COL_FOX, COL_GDN, COL_GZ, COL_SMALL = 0, 1536, 3072, 3584
LANES = 128
QKV = 3 * LANES
SM_FF, SM_GB, SM_GA = 0, 8, 12

ADAM_LR, ADAM_B1, ADAM_B2, ADAM_EPS, ADAM_WD, ADAM_STEP = 0.001, 0.9, 0.999, 1e-08, 0.01, 10

TOKEN_BLOCK = 256
MATMUL_BLOCK = 512
FOX_SCALE = FOX_HEAD_DIM ** -0.5
GDN_QSCALE = GDN_HEAD_DIM ** -0.5
NEG_BIG = -1e30
VMEM_LIMIT = 56 * 1024 * 1024

VMEM_SPEC = pl.BlockSpec(memory_space=pltpu.VMEM)
ANY_SPEC = pl.BlockSpec(memory_space=pl.ANY)


def _sds(shape, dtype=F32):
    return jax.ShapeDtypeStruct(shape, dtype)


def _params(*sem):
    return pltpu.CompilerParams(dimension_semantics=sem if sem else None, vmem_limit_bytes=VMEM_LIMIT)


def _ordered(body):
    def ordered(_, *refs):
        body(*refs)

    return ordered


def _mm(a, b):
    return jnp.dot(a.astype(BF), b.astype(BF), preferred_element_type=F32)


def _mm_nt(a, b):
    return lax.dot_general(a.astype(BF), b.astype(BF), (((1,), (1,)), ((), ())), preferred_element_type=F32)


def _mm_tn(a, b):
    return lax.dot_general(a.astype(BF), b.astype(BF), (((0,), (0,)), ((), ())), preferred_element_type=F32)


def _sigmoid(x):
    return 1.0 / (1.0 + jnp.exp(-x))


def _softplus(x):
    return jnp.maximum(x, 0.0) + jnp.log1p(jnp.exp(-jnp.abs(x)))


def _iota(shape, dim):
    return lax.broadcasted_iota(jnp.int32, shape, dim)


def _shift_down(x, s, row):
    return jnp.where(row >= s, pltpu.roll(x, s, 0), 0.0)


def _shift_up(x, s, row):
    n = x.shape[0]
    return jnp.where(row < n - s, pltpu.roll(x, n - s, 0), 0.0)


def _norm_proj(x, nw, wt_al, after):
    t = x.shape[0]

    def body(x_ref, nw_ref, w_ref, proj_ref, h_ref):
        xv = x_ref[...]
        r = lax.rsqrt(jnp.mean(xv * xv, axis=-1, keepdims=True) + EPS)
        h = (xv * r * nw_ref[...]).astype(BF)
        h_ref[...] = h
        proj_ref[...] = lax.dot_general(h, w_ref[...], (((1,), (1,)), ((), ())), preferred_element_type=F32)

    tm = min(MATMUL_BLOCK, t)
    return pl.pallas_call(
        _ordered(body), name="norm_proj", grid=(t // tm,),
        in_specs=[ANY_SPEC, pl.BlockSpec((tm, D_MODEL), lambda i: (i, 0)), pl.BlockSpec((1, D_MODEL), lambda i: (0, 0)),
                  pl.BlockSpec((PROJ_W, D_MODEL), lambda i: (0, 0))],
        out_specs=[pl.BlockSpec((tm, PROJ_W), lambda i: (i, 0)), pl.BlockSpec((tm, D_MODEL), lambda i: (i, 0))],
        out_shape=[_sds((t, PROJ_W)), _sds((t, D_MODEL), BF)],
        compiler_params=_params("parallel"),
    )(after, x, nw, wt_al)


def _lane_column(x, lane):
    return jnp.sum(jnp.where(_iota((1, LANES), 1) == lane, x, 0.0), axis=-1, keepdims=True)


def _small_prep(proj, fb, al, dtb):
    t = proj.shape[0]

    def body(sm_ref, fb_ref, al_ref, dtb_ref, cumt_ref, beta_ref, g_ref):
        s = sm_ref[...]
        z = s + fb_ref[...]
        cum = jnp.minimum(z, 0.0) - jnp.log1p(jnp.exp(-jnp.abs(z)))
        row = _iota((t, LANES), 0)
        step = 1
        while step < t:
            cum = cum + _shift_down(cum, step, row)
            step *= 2
        cumt_ref[...] = cum.T
        beta_ref[...] = _sigmoid(s)
        g_ref[...] = -jnp.exp(al_ref[...]) * _softplus(s + dtb_ref[...])

    vec = pl.BlockSpec((1, LANES), lambda i: (0, 0))
    tok = pl.BlockSpec((t, LANES), lambda i: (0, 0))
    return pl.pallas_call(
        body, name="small_prep", grid=(1,),
        in_specs=[pl.BlockSpec((t, LANES), lambda i: (0, COL_SMALL // LANES)), vec, vec, vec],
        out_specs=[pl.BlockSpec((LANES, t), lambda i: (0, 0)), tok, tok],
        out_shape=[_sds((LANES, t)), _sds((t, LANES)), _sds((t, LANES))],
        compiler_params=_params("arbitrary"),
    )(proj, fb, al, dtb)


def _fox_stack(x, first):
    return jnp.concatenate([jnp.where(first, x, 0.0), jnp.where(first, 0.0, x)], axis=0).astype(BF)


def _fox_unstack(y, first):
    n = y.shape[0] // 2
    return jnp.where(first, y[:n], y[n:])


def _fox_logits(q2_i, kb, cumt_ref, pair, i, tq):
    klen = (i + 1) * tq
    s = lax.dot_general(q2_i, kb[:klen], (((1,), (1,)), ((), ())), preferred_element_type=F32)
    upper = _iota((2 * tq, 1), 0) < tq
    s = s - jnp.where(upper, cumt_ref[pl.ds(2 * pair, 1), 0:klen], cumt_ref[pl.ds(2 * pair + 1, 1), 0:klen])
    causal = _iota((2 * tq, tq), 1) <= _iota((2 * tq, tq), 0) % tq
    parts = [(s[:, :klen - tq], 0, klen - tq)] if i else []
    return parts + [(jnp.where(causal, s[:, klen - tq:], NEG_BIG), klen - tq, klen)]


def _fox_fwd(proj, cumt, fnw):
    t = proj.shape[0]
    tq = min(TOKEN_BLOCK, t // 2)
    nq = t // tq

    def body(q_ref, k_ref, v_ref, cumt_ref, fnw_ref, o_ref, lse_ref, fn_ref):
        j = pl.program_id(0)
        first = _iota((1, LANES), 1) < FOX_HEAD_DIM
        kb = k_ref[...].astype(BF)
        vb = v_ref[...].astype(BF)
        for i in range(nq):
            rows = slice(i * tq, (i + 1) * tq)
            q2 = _fox_stack(q_ref[rows, :] * FOX_SCALE, first)
            parts = _fox_logits(q2, kb, cumt_ref, j, i, tq)
            m = jnp.max(parts[-1][0], axis=-1, keepdims=True)
            if i:
                m = jnp.maximum(m, jnp.max(parts[0][0], axis=-1, keepdims=True))
            l = jnp.zeros((2 * tq, 1), F32)
            o = jnp.zeros((2 * tq, LANES), F32)
            for s, lo, hi in parts:
                p = jnp.exp(s - m)
                l = l + jnp.sum(p, axis=-1, keepdims=True)
                o = o + jnp.dot(p.astype(BF), vb[lo:hi], preferred_element_type=F32)
            o_acc = _fox_unstack(o / l, first)
            lse_acc = _fox_unstack(jnp.broadcast_to(m + jnp.log(l), (2 * tq, LANES)), first)
            o_ref[rows, :] = o_acc
            lse_ref[rows, :] = lse_acc
            o2 = o_acc * o_acc
            s0 = jnp.sum(jnp.where(first, o2, 0.0), axis=-1, keepdims=True)
            s1 = jnp.sum(jnp.where(first, 0.0, o2), axis=-1, keepdims=True)
            r = lax.rsqrt(jnp.where(first, s0, s1) * (1.0 / FOX_HEAD_DIM) + EPS)
            fn_ref[rows, :] = (o_acc * r * fnw_ref[...]).astype(BF)

    qkv = lambda k: pl.BlockSpec((t, LANES), lambda j: (0, COL_FOX // LANES + 3 * j + k))
    pair = pl.BlockSpec((t, LANES), lambda j: (0, j))
    return pl.pallas_call(
        body, name="fox_fwd", grid=(N_FOX_HEADS // 2,),
        in_specs=[qkv(0), qkv(1), qkv(2), pl.BlockSpec((LANES, t), lambda j: (0, 0)),
                  pl.BlockSpec((1, LANES), lambda j: (0, 0))],
        out_specs=[pair, pair, pair],
        out_shape=[_sds((t, D_FOX)), _sds((t, D_FOX)), _sds((t, D_FOX), BF)],
        compiler_params=_params("parallel"),
    )(proj, proj, proj, cumt, fnw)


def _fox_bwd(proj, cumt, lse, o, do, dproj):
    t = proj.shape[0]
    tq = min(TOKEN_BLOCK, t // 2)
    nq = t // tq

    def body(q_ref, k_ref, v_ref, cumt_ref, lse_ref, o_ref, do_ref, _, dqkv_ref, dcq_ref, dckt_ref, dk_s, dv_s):
        j = pl.program_id(0)

        @pl.when(j == 0)
        def _():
            dcq_ref[...] = jnp.zeros_like(dcq_ref)
            dckt_ref[...] = jnp.zeros_like(dckt_ref)

        lane = _iota((1, LANES), 1)

        first = _iota((1, LANES), 1) < FOX_HEAD_DIM
        kb = k_ref[...].astype(BF)
        vb = v_ref[...].astype(BF)
        dk_s[...] = jnp.zeros_like(dk_s)
        dv_s[...] = jnp.zeros_like(dv_s)
        for i in range(nq):
            rows = slice(i * tq, (i + 1) * tq)
            do_i = do_ref[rows, :]
            prod = do_i * o_ref[rows, :]
            lse_i = lse_ref[rows, :]
            q2 = _fox_stack(q_ref[rows, :] * FOX_SCALE, first)
            do2 = _fox_stack(do_i, first)
            delta = jnp.concatenate([jnp.sum(jnp.where(first, prod, 0.0), axis=-1, keepdims=True),
                                     jnp.sum(jnp.where(first, 0.0, prod), axis=-1, keepdims=True)], axis=0)
            lse2 = jnp.concatenate([lse_i[:, 0:1], lse_i[:, FOX_HEAD_DIM:FOX_HEAD_DIM + 1]], axis=0)
            dq2 = jnp.zeros((2 * tq, LANES), F32)
            dcq2 = jnp.zeros((2 * tq, 1), F32)
            for s, lo, hi in _fox_logits(q2, kb, cumt_ref, j, i, tq):
                p = jnp.exp(s - lse2)
                ds = p * (_mm_nt(do2, vb[lo:hi]) - delta)
                dsb = ds.astype(BF)
                dq2 = dq2 + jnp.dot(dsb, kb[lo:hi], preferred_element_type=F32)
                dk_s[lo:hi, :] += _mm_tn(dsb, q2)
                dv_s[lo:hi, :] += _mm_tn(p, do2)
                dcq2 = dcq2 + jnp.sum(ds, axis=-1, keepdims=True)
                dckt_ref[pl.ds(2 * j, 1), lo:hi] += jnp.sum(ds[:tq], axis=0, keepdims=True)
                dckt_ref[pl.ds(2 * j + 1, 1), lo:hi] += jnp.sum(ds[tq:], axis=0, keepdims=True)
            dqkv_ref[rows, 0:LANES] = (_fox_unstack(dq2, first) * FOX_SCALE).astype(BF)
            dcq_ref[rows, :] += jnp.where(lane == 2 * j, dcq2[:tq], jnp.where(lane == 2 * j + 1, dcq2[tq:], 0.0))
        dqkv_ref[:, LANES:2 * LANES] = dk_s[...].astype(BF)
        dqkv_ref[:, 2 * LANES:QKV] = dv_s[...].astype(BF)

    qkv = lambda k: pl.BlockSpec((t, LANES), lambda j: (0, COL_FOX // LANES + 3 * j + k))
    pair = pl.BlockSpec((t, LANES), lambda j: (0, j))
    rows128 = pl.BlockSpec((LANES, t), lambda j: (0, 0))
    return pl.pallas_call(
        body, name="fox_bwd", grid=(N_FOX_HEADS // 2,),
        in_specs=[qkv(0), qkv(1), qkv(2), rows128, pair, pair, pair, ANY_SPEC],
        out_specs=[pl.BlockSpec((t, QKV), lambda j: (0, COL_FOX // QKV + j)),
                   pl.BlockSpec((t, LANES), lambda j: (0, 0)), rows128],
        out_shape=[_sds(dproj.shape, BF), _sds((t, LANES)), _sds((LANES, t))],
        scratch_shapes=[pltpu.VMEM((t, LANES), F32), pltpu.VMEM((t, LANES), F32)],
        input_output_aliases={7: 0}, compiler_params=_params("arbitrary"),
    )(proj, proj, proj, cumt, lse, o, do, dproj)


def _conv(x, w, row):
    return (w[3:4, :] * x + w[2:3, :] * _shift_down(x, 1, row) + w[1:2, :] * _shift_down(x, 2, row)
            + w[0:1, :] * _shift_down(x, 3, row))


def _chunk_decay(gc_c):
    gi = gc_c[:, 0:CHUNK]
    gj = gc_c.T[0:CHUNK, :]
    ri = _iota((CHUNK, CHUNK), 0)
    cj = _iota((CHUNK, CHUNK), 1)
    return jnp.where(ri >= cj, jnp.exp(jnp.minimum(gi - gj, 0.0)), 0.0), ri > cj


def _gdn_specs(t):
    col = lambda off: pl.BlockSpec((t, LANES), lambda h: (0, off + h))
    cw = lambda off: pl.BlockSpec((CONV_K, LANES), lambda h: (0, off + h))
    mat = pl.BlockSpec((1, t // CHUNK, CHUNK, CHUNK), lambda h: (h, 0, 0, 0))
    qkv = lambda k: pl.BlockSpec((t, LANES), lambda h: (0, COL_GDN // LANES + 3 * h + k))
    return col, cw, mat, qkv


def _gdn_prep(proj, convw, beta, g):
    t = proj.shape[0]
    nch = t // CHUNK

    def body(xq_ref, xk_ref, xv_ref, wq_ref, wk_ref, wv_ref, beta_ref, g_ref,
             qn_ref, kn_ref, cv_ref, gc_ref, be_ref, m_ref, a_ref):
        row = _iota((t, LANES), 0)
        hd = pl.program_id(0)
        be_ref[...] = jnp.broadcast_to(_lane_column(beta_ref[...], SM_GB + hd), (t, LANES))

        def act(x_ref, w_ref):
            y = _conv(x_ref[...], w_ref[...], row)
            return y * _sigmoid(y)

        cq = act(xq_ref, wq_ref)
        ck = act(xk_ref, wk_ref)
        cv_ref[...] = act(xv_ref, wv_ref)
        qn_ref[...] = cq * lax.rsqrt(jnp.sum(cq * cq, axis=-1, keepdims=True) + EPS) * GDN_QSCALE
        kn_ref[...] = ck * lax.rsqrt(jnp.sum(ck * ck, axis=-1, keepdims=True) + EPS)
        gc = jnp.broadcast_to(_lane_column(g_ref[...], SM_GA + hd), (t, LANES))
        pos = row % CHUNK
        step = 1
        while step < CHUNK:
            gc = gc + jnp.where(pos >= step, pltpu.roll(gc, step, 0), 0.0)
            step *= 2
        gc_ref[...] = gc

        group = 4 if nch % 4 == 0 else 1

        def chunks(gi, carry):
            ns = [gi * group + c for c in range(group)]
            sls = [pl.ds(pl.multiple_of(n * CHUNK, CHUNK), CHUNK) for n in ns]
            ks = [kn_ref[sl, :] for sl in sls]
            kk = [_mm_nt(k_c * be_ref[sl, :], k_c) for k_c, sl in zip(ks, sls)]
            qk = [_mm_nt(qn_ref[sl, :], k_c) for k_c, sl in zip(ks, sls)]
            for c, n in enumerate(ns):
                decay, strict = _chunk_decay(gc_ref[sls[c], :])
                m_ref[0, n] = jnp.where(strict, kk[c] * decay, 0.0)
                a_ref[0, n] = qk[c] * decay
            return carry

        lax.fori_loop(0, nch // group, chunks, 0)

    col, cw, mat, qkv = _gdn_specs(t)
    return pl.pallas_call(
        body, name="gdn_prep", grid=(N_GDN_HEADS,),
        in_specs=[qkv(0), qkv(1), qkv(2), cw(0), cw(4), cw(8)] + [pl.BlockSpec((t, LANES), lambda h: (0, 0))] * 2,
        out_specs=[col(0), col(0), col(0), col(0), col(0), mat, mat],
        out_shape=[_sds((t, D_GDN))] * 5 + [_sds((N_GDN_HEADS, nch, CHUNK, CHUNK))] * 2,
        compiler_params=_params("parallel"),
    )(proj, proj, proj, convw, convw, convw, beta, g)


def _tri_inverse(m3):
    assert m3.shape == (LANES, CHUNK, CHUNK)

    def body(m_ref, t_ref, ms, ts):
        for i in range(CHUNK):
            ms[i * CHUNK:(i + 1) * CHUNK, :] = m_ref[:, i, :].T
        cidx = _iota((CHUNK, LANES), 0)

        def outer(i, carry):
            def inner(jj, acc):
                mrow = ms[pl.ds(i * CHUNK + jj, 1), :]
                return acc - mrow * ts[pl.ds(pl.multiple_of(jj * CHUNK, CHUNK), CHUNK), :]

            acc = lax.fori_loop(0, i, inner, jnp.where(cidx == i, 1.0, 0.0).astype(F32))
            ts[pl.ds(pl.multiple_of(i * CHUNK, CHUNK), CHUNK), :] = acc
            return carry

        lax.fori_loop(0, CHUNK, outer, 0)
        for i in range(CHUNK):
            t_ref[:, i, :] = ts[i * CHUNK:(i + 1) * CHUNK, :].T

    return pl.pallas_call(
        body, name="tri_inverse", in_specs=[VMEM_SPEC], out_specs=VMEM_SPEC,
        out_shape=_sds((LANES, CHUNK, CHUNK)),
        scratch_shapes=[pltpu.VMEM((CHUNK * CHUNK, LANES), F32), pltpu.VMEM((CHUNK * CHUNK, LANES), F32)],
        compiler_params=_params(),
    )(m3)


def _gdn_chunk_terms(q, k, v, b, gcc):
    eg = jnp.exp(gcc)
    last = gcc[CHUNK - 1:CHUNK, :]
    egl = jnp.exp(last - gcc)
    gl = jnp.exp(last)
    kb = k * b
    return eg, egl, gl, kb, v * b, kb * eg, q * eg, k * egl


GDN_BLOCK_CHUNKS = 4


def _gdn_block_specs(t, reverse):
    cb = GDN_BLOCK_CHUNKS
    nb = t // (cb * CHUNK)
    idx = (lambda i: nb - 1 - i) if reverse else (lambda i: i)
    tok = pl.BlockSpec((cb * CHUNK, D_GDN), lambda i: (idx(i), 0))
    mat = pl.BlockSpec((N_GDN_HEADS, cb, CHUNK, CHUNK), lambda i: (0, idx(i), 0, 0))
    state = pl.BlockSpec((N_GDN_HEADS, cb, GDN_HEAD_DIM, GDN_HEAD_DIM), lambda i: (0, idx(i), 0, 0))
    return nb, tok, mat, state


def _gdn_scan(qn, kn, cv, be, gc, tinv, amat):
    t = qn.shape[0]
    nch = t // CHUNK

    def body(q_ref, k_ref, v_ref, b_ref, gc_ref, t_ref, a_ref, o_ref, sall_ref, vn_ref, s_scr):
        @pl.when(pl.program_id(0) == 0)
        def _():
            s_scr[...] = jnp.zeros_like(s_scr)

        heads = range(N_GDN_HEADS)
        cols = [slice(hd * LANES, (hd + 1) * LANES) for hd in heads]
        s = [s_scr[hd] for hd in heads]
        for cc in range(GDN_BLOCK_CHUNKS):
            rs = slice(cc * CHUNK, (cc + 1) * CHUNK)
            terms = [_gdn_chunk_terms(q_ref[rs, cs], k_ref[rs, cs], v_ref[rs, cs], b_ref[rs, cs], gc_ref[rs, cs])
                     for cs in cols]
            for hd in heads:
                sall_ref[hd, cc] = s[hd]
            uw = [_mm(t_ref[hd, cc], jnp.concatenate([terms[hd][4], terms[hd][5]], axis=1)) for hd in heads]
            ws_qs = [_mm(jnp.concatenate([uw[hd][:, LANES:], terms[hd][6]], axis=0), s[hd]) for hd in heads]
            vn = [uw[hd][:, :LANES] - ws_qs[hd][:CHUNK] for hd in heads]
            a_vn = [_mm(a_ref[hd, cc], vn[hd]) for hd in heads]
            kd_vn = [_mm_tn(terms[hd][7], vn[hd]) for hd in heads]
            for hd in heads:
                vn_ref[rs, cols[hd]] = vn[hd]
                o_ref[rs, cols[hd]] = ws_qs[hd][CHUNK:] + a_vn[hd]
                s[hd] = s[hd] * terms[hd][2] + kd_vn[hd]
        for hd in heads:
            s_scr[hd] = s[hd]

    nb, tok, mat, state = _gdn_block_specs(t, False)
    return pl.pallas_call(
        body, name="gdn_scan", grid=(nb,),
        in_specs=[tok] * 5 + [mat, mat], out_specs=[tok, state, tok],
        out_shape=[_sds((t, D_GDN)), _sds((N_GDN_HEADS, nch, GDN_HEAD_DIM, GDN_HEAD_DIM)), _sds((t, D_GDN))],
        scratch_shapes=[pltpu.VMEM((N_GDN_HEADS, GDN_HEAD_DIM, GDN_HEAD_DIM), F32)],
        compiler_params=_params("arbitrary"),
    )(qn, kn, cv, be, gc, tinv, amat)


def _gdn_bwd(qn, kn, cv, be, gc, tinv, amat, s_all, vn_all, do):
    t = qn.shape[0]

    def body(q_ref, k_ref, v_ref, b_ref, gc_ref, t_ref, a_ref, sall_ref, vn_ref, do_ref,
             dq_ref, dk_ref, dv_ref, db_ref, dg_ref, ds_scr):
        @pl.when(pl.program_id(0) == 0)
        def _():
            ds_scr[...] = jnp.zeros_like(ds_scr)

        lastrow = _iota((CHUNK, LANES), 0) == CHUNK - 1
        heads = range(N_GDN_HEADS)
        cols = [slice(hd * LANES, (hd + 1) * LANES) for hd in heads]
        each = lambda fn: [fn(hd) for hd in heads]
        rows_cat = lambda x, y: jnp.concatenate([x, y], axis=0)
        lane_cat = lambda x, y: jnp.concatenate([x, y], axis=1)
        dsp = each(lambda hd: ds_scr[hd])
        for cc in reversed(range(GDN_BLOCK_CHUNKS)):
            rs = slice(cc * CHUNK, (cc + 1) * CHUNK)
            q = each(lambda hd: q_ref[rs, cols[hd]])
            k = each(lambda hd: k_ref[rs, cols[hd]])
            v = each(lambda hd: v_ref[rs, cols[hd]])
            b = each(lambda hd: b_ref[rs, cols[hd]])
            gcc = each(lambda hd: gc_ref[rs, cols[hd]])
            do_c = each(lambda hd: do_ref[rs, cols[hd]])
            vn = each(lambda hd: vn_ref[rs, cols[hd]])
            tn = each(lambda hd: t_ref[hd, cc])
            st = each(lambda hd: sall_ref[hd, cc])
            terms = each(lambda hd: _gdn_chunk_terms(q[hd], k[hd], v[hd], b[hd], gcc[hd]))
            eg, egl, gl, kb, vb, kbg, qd, kd = [[terms[hd][i] for hd in heads] for i in range(8)]
            w = each(lambda hd: _mm(tn[hd], kbg[hd]))
            a_do = each(lambda hd: _mm_tn(a_ref[hd, cc], do_c[hd]))
            kd_ds = each(lambda hd: _mm(kd[hd], dsp[hd]))
            da = each(lambda hd: _mm_nt(do_c[hd], vn[hd]))
            dkd = each(lambda hd: _mm_nt(vn[hd], dsp[hd]))
            by_k = each(lambda hd: _mm_nt(rows_cat(kb[hd], q[hd]), k[hd]))
            dgl = each(lambda hd: jnp.sum(jnp.sum(dsp[hd] * st[hd], axis=-1, keepdims=True), axis=0, keepdims=True))
            dvn = each(lambda hd: a_do[hd] + kd_ds[hd])
            do_dvn = each(lambda hd: rows_cat(do_c[hd], dvn[hd]))
            by_s = each(lambda hd: _mm_nt(do_dvn[hd], st[hd]))
            dqd = each(lambda hd: by_s[hd][:CHUNK])
            dvn_dw = each(lambda hd: lane_cat(dvn[hd], -by_s[hd][CHUNK:]))
            dsp = each(lambda hd: _mm_tn(rows_cat(qd[hd], -w[hd]), do_dvn[hd]) + gl[hd] * dsp[hd])
            dt = each(lambda hd: _mm_nt(dvn_dw[hd], lane_cat(vb[hd], kbg[hd])))
            by_t = each(lambda hd: _mm_tn(tn[hd], dvn_dw[hd]))
            tt_dt = each(lambda hd: _mm_tn(tn[hd], dt[hd]))
            dm_raw = each(lambda hd: _mm_nt(tt_dt[hd], tn[hd]))
            masks = each(lambda hd: _chunk_decay(gcc[hd]))
            dkk = each(lambda hd: jnp.where(masks[hd][1], -dm_raw[hd], 0.0) * masks[hd][0])
            dqk = each(lambda hd: da[hd] * masks[hd][0])
            dqk_dkk = each(lambda hd: rows_cat(dqk[hd], dkk[hd]))
            on_k = each(lambda hd: _mm(dqk_dkk[hd], k[hd]))
            dk_mm = each(lambda hd: _mm_tn(dqk_dkk[hd], rows_cat(q[hd], kb[hd])))
            for hd in heads:
                cs = cols[hd]
                dvb, dkbg = by_t[hd][:, :LANES], by_t[hd][:, LANES:]
                gmat = dkk[hd] * by_k[hd][:CHUNK] + dqk[hd] * by_k[hd][CHUNK:]
                dq_ref[rs, cs] = dqd[hd] * eg[hd] + on_k[hd][:CHUNK]
                dkb = on_k[hd][CHUNK:] + dkbg * eg[hd]
                dk_ref[rs, cs] = dkd[hd] * egl[hd] + dk_mm[hd] + dkb * b[hd]
                db = jnp.sum(dkb * k[hd], axis=-1, keepdims=True) + jnp.sum(dvb * v[hd], axis=-1, keepdims=True)
                db_ref[rs, cs] = jnp.broadcast_to(db, (CHUNK, LANES))
                dv_ref[rs, cs] = dvb * b[hd]
                dkd_kd = jnp.sum(dkd[hd] * kd[hd], axis=-1, keepdims=True)
                col_sums = jnp.sum(lane_cat(gmat, jnp.zeros_like(gmat)).T, axis=-1, keepdims=True)
                dgc = (jnp.sum(gmat, axis=-1, keepdims=True) - col_sums[:CHUNK]
                       + jnp.sum(dqd[hd] * qd[hd], axis=-1, keepdims=True)
                       + jnp.sum(dkbg * kbg[hd], axis=-1, keepdims=True) - dkd_kd)
                extra = jnp.sum(dkd_kd, axis=0, keepdims=True) + dgl[hd] * gl[hd]
                dg_ref[rs, cs] = dgc + jnp.where(lastrow, extra, 0.0)
        for hd in heads:
            ds_scr[hd] = dsp[hd]
        dg = dg_ref[...]
        row = _iota(dg.shape, 0)
        pos = row % CHUNK
        step = 1
        while step < CHUNK:
            dg = dg + jnp.where(pos < CHUNK - step, pltpu.roll(dg, dg.shape[0] - step, 0), 0.0)
            step *= 2
        dg_ref[...] = dg

    nb, tok, mat, state = _gdn_block_specs(t, True)
    return pl.pallas_call(
        body, name="gdn_bwd", grid=(nb,),
        in_specs=[tok] * 5 + [mat, mat, state, tok, tok], out_specs=[tok] * 5, out_shape=[_sds((t, D_GDN))] * 5,
        scratch_shapes=[pltpu.VMEM((N_GDN_HEADS, GDN_HEAD_DIM, GDN_HEAD_DIM), F32)],
        compiler_params=_params("arbitrary"),
    )(qn, kn, cv, be, gc, tinv, amat, s_all, vn_all, do)


def _gdn_bwd_conv(proj, convw, dqn, dkn, dcv, dproj):
    t = proj.shape[0]

    def body(xq_ref, xk_ref, xv_ref, wq_ref, wk_ref, wv_ref, dq_ref, dk_ref, dv_ref, _,
             dqkv_ref, dwq_ref, dwk_ref, dwv_ref):
        row = _iota((t, LANES), 0)

        def one(x_ref, w_ref, d_ref, k, dw_ref, scale):
            x = x_ref[...]
            w = w_ref[...]
            y = _conv(x, w, row)
            sg = _sigmoid(y)
            dc = d_ref[...]
            if scale is not None:
                c = y * sg
                r = lax.rsqrt(jnp.sum(c * c, axis=-1, keepdims=True) + EPS)
                ch = c * r
                dc = scale * r * (dc - ch * jnp.sum(dc * ch, axis=-1, keepdims=True))
            dy = dc * sg * (1.0 + y * (1.0 - sg))
            dqkv_ref[:, k * LANES:(k + 1) * LANES] = (
                w[3:4, :] * dy + w[2:3, :] * _shift_up(dy, 1, row) + w[1:2, :] * _shift_up(dy, 2, row)
                + w[0:1, :] * _shift_up(dy, 3, row)).astype(BF)
            for jj in range(CONV_K):
                xs = x if jj == CONV_K - 1 else _shift_down(x, CONV_K - 1 - jj, row)
                dw_ref[jj:jj + 1, :] = jnp.sum(dy * xs, axis=0, keepdims=True)

        one(xq_ref, wq_ref, dq_ref, 0, dwq_ref, GDN_QSCALE)
        one(xk_ref, wk_ref, dk_ref, 1, dwk_ref, 1.0)
        one(xv_ref, wv_ref, dv_ref, 2, dwv_ref, None)

    col, cw, _, qkv = _gdn_specs(t)
    return pl.pallas_call(
        body, name="gdn_bwd_conv", grid=(N_GDN_HEADS,),
        in_specs=[qkv(0), qkv(1), qkv(2), cw(0), cw(4), cw(8), col(0), col(0), col(0), ANY_SPEC],
        out_specs=[pl.BlockSpec((t, QKV), lambda h: (0, COL_GDN // QKV + h)), cw(0), cw(0), cw(0)],
        out_shape=[_sds(dproj.shape, BF)] + [_sds((CONV_K, D_GDN))] * 3,
        input_output_aliases={9: 0}, compiler_params=_params("parallel"),
    )(proj, proj, proj, convw, convw, convw, dqn, dkn, dcv, dproj)


def _mix_out(fox_n, gdn_o, proj, gnw, w_out, x, pmw, plw, after):
    t = x.shape[0]
    tm = min(MATMUL_BLOCK, t)

    def body(fn_ref, go_ref, gz_ref, gnw_ref, w_ref, x_ref, pmw_ref, plw_ref, x1_ref, h2_ref, mixed_ref, omix_ref,
             h2t_ref):
        omix_ref[:, 0:D_FOX] = fn_ref[...]
        for hd in range(N_GDN_HEADS):
            cs = slice(hd * LANES, (hd + 1) * LANES)
            go = go_ref[:, cs]
            r = lax.rsqrt(jnp.mean(go * go, axis=-1, keepdims=True) + EPS)
            gz = gz_ref[:, cs]
            omix_ref[:, D_FOX + hd * LANES:D_FOX + (hd + 1) * LANES] = (
                go * r * gnw_ref[...] * (gz * _sigmoid(gz))).astype(BF)
        mixed = jnp.dot(omix_ref[...], w_ref[...], preferred_element_type=F32)
        mixed_ref[...] = mixed
        r2 = lax.rsqrt(jnp.mean(mixed * mixed, axis=-1, keepdims=True) + EPS)
        x1 = x_ref[...] + mixed * r2 * pmw_ref[...]
        x1_ref[...] = x1
        r3 = lax.rsqrt(jnp.mean(x1 * x1, axis=-1, keepdims=True) + EPS)
        h2 = x1 * r3 * plw_ref[...]
        h2_ref[...] = h2.astype(BF)
        h2t_ref[...] = h2.T.astype(BF)

    tok = lambda w: pl.BlockSpec((tm, w), lambda i: (i, 0))
    vec = lambda w: pl.BlockSpec((1, w), lambda i: (0, 0))
    return pl.pallas_call(
        _ordered(body), name="mix_out", grid=(t // tm,),
        in_specs=[ANY_SPEC, tok(D_FOX), tok(D_GDN), pl.BlockSpec((tm, D_GDN), lambda i: (i, COL_GZ // D_GDN)), vec(LANES),
                  pl.BlockSpec((D_MODEL, D_MODEL), lambda i: (0, 0)), tok(D_MODEL), vec(D_MODEL), vec(D_MODEL)],
        out_specs=[tok(D_MODEL)] * 4 + [pl.BlockSpec((D_MODEL, tm), lambda i: (0, i))],
        out_shape=[_sds((t, D_MODEL)), _sds((t, D_MODEL), BF), _sds((t, D_MODEL)), _sds((t, D_MODEL), BF),
                   _sds((D_MODEL, t), BF)],
        compiler_params=_params("parallel"),
    )(after, fox_n, gdn_o, proj, gnw, w_out, x, pmw, plw)


def _out_bwd(dmixed, w_out, o_fox, gdn_o, proj, fnw, gnw, after):
    t = dmixed.shape[0]
    tm = min(MATMUL_BLOCK, t)

    def body(dm_ref, w_ref, of_ref, go_ref, gz_ref, fnw_ref, gnw_ref, dof_ref, dgo_ref, dgz_ref, dfw_ref, dgw_ref):
        i = pl.program_id(0)

        @pl.when(i == 0)
        def _():
            dfw_ref[...] = jnp.zeros_like(dfw_ref)
            dgw_ref[...] = jnp.zeros_like(dgw_ref)

        domix = _mm_nt(dm_ref[...], w_ref[...])
        first = _iota((1, LANES), 1) < FOX_HEAD_DIM
        dfw = jnp.zeros((1, LANES), F32)
        dgw = jnp.zeros((1, LANES), F32)
        for pr in range(N_FOX_HEADS // 2):
            cs = slice(pr * LANES, (pr + 1) * LANES)
            o = of_ref[:, cs]
            dfn = domix[:, cs]
            o2 = o * o
            s0 = jnp.sum(jnp.where(first, o2, 0.0), axis=-1, keepdims=True)
            s1 = jnp.sum(jnp.where(first, 0.0, o2), axis=-1, keepdims=True)
            r = lax.rsqrt(jnp.where(first, s0, s1) * (1.0 / FOX_HEAD_DIM) + EPS)
            oh = o * r
            dfw = dfw + jnp.sum(dfn * oh, axis=0, keepdims=True)
            doh = dfn * fnw_ref[...]
            pr_ = doh * oh
            m0 = jnp.sum(jnp.where(first, pr_, 0.0), axis=-1, keepdims=True)
            m1 = jnp.sum(jnp.where(first, 0.0, pr_), axis=-1, keepdims=True)
            dof_ref[:, cs] = r * (doh - oh * jnp.where(first, m0, m1) * (1.0 / FOX_HEAD_DIM))
        for hd in range(N_GDN_HEADS):
            cs = slice(hd * LANES, (hd + 1) * LANES)
            go = go_ref[:, cs]
            gz = gz_ref[:, cs]
            dgated = domix[:, D_FOX + hd * LANES:D_FOX + (hd + 1) * LANES]
            r = lax.rsqrt(jnp.mean(go * go, axis=-1, keepdims=True) + EPS)
            goh = go * r
            sg = _sigmoid(gz)
            sz = gz * sg
            gn = goh * gnw_ref[...]
            dgn = dgated * sz
            dgz_ref[:, cs] = (dgated * gn * sg * (1.0 + gz * (1.0 - sg))).astype(BF)
            dgw = dgw + jnp.sum(dgn * goh, axis=0, keepdims=True)
            dgh = dgn * gnw_ref[...]
            dgo_ref[:, cs] = r * (dgh - goh * jnp.mean(dgh * goh, axis=-1, keepdims=True))
        dfw_ref[...] += dfw + pltpu.roll(dfw, FOX_HEAD_DIM, 1)
        dgw_ref[...] += dgw

    tok = lambda w: pl.BlockSpec((tm, w), lambda i: (i, 0))
    vec = lambda w: pl.BlockSpec((1, w), lambda i: (0, 0))
    return pl.pallas_call(
        _ordered(body), name="out_bwd", grid=(t // tm,),
        in_specs=[ANY_SPEC, tok(D_MODEL), pl.BlockSpec((D_MODEL, D_MODEL), lambda i: (0, 0)), tok(D_FOX), tok(D_GDN),
                  pl.BlockSpec((tm, D_GDN), lambda i: (i, COL_GZ // D_GDN)), vec(LANES), vec(LANES)],
        out_specs=[tok(D_FOX), tok(D_GDN), pl.BlockSpec((tm, D_GDN), lambda i: (i, COL_GZ // D_GDN)), vec(LANES),
                   vec(LANES)],
        out_shape=[_sds((t, D_FOX)), _sds((t, D_GDN)), _sds((t, PROJ_W), BF), _sds((1, LANES)), _sds((1, LANES))],
        compiler_params=_params("arbitrary"),
    )(after, dmixed, w_out, o_fox, gdn_o, proj, fnw, gnw)


def _mlp_up(h2, w_upt):
    t = h2.shape[0]
    tm = min(MATMUL_BLOCK, t)

    def body(h_ref, w_ref, up_ref):
        up_ref[...] = lax.dot_general(h_ref[...], w_ref[...], (((1,), (1,)), ((), ())),
                                      preferred_element_type=F32).astype(BF)

    return pl.pallas_call(
        body, name="mlp_up", grid=(t // tm,),
        in_specs=[pl.BlockSpec((tm, D_MODEL), lambda i: (i, 0)), pl.BlockSpec((D_FF, D_MODEL), lambda i: (0, 0))],
        out_specs=pl.BlockSpec((tm, D_FF), lambda i: (i, 0)), out_shape=_sds((t, D_FF), BF),
        compiler_params=_params("parallel"),
    )(h2, w_upt)


def _mlp_down_loss(up, w_down, x1, pw, target):
    t = up.shape[0]
    tm = min(MATMUL_BLOCK, t)

    def body(up_ref, w_ref, x1_ref, pw_ref, tg_ref, dy_ref, dx2_ref, loss_ref, dpw_ref):
        i = pl.program_id(0)

        @pl.when(i == 0)
        def _():
            loss_ref[...] = jnp.zeros_like(loss_ref)
            dpw_ref[...] = jnp.zeros_like(dpw_ref)

        u = jnp.maximum(up_ref[...].astype(F32), 0.0)
        y = jnp.dot((u * u).astype(BF), w_ref[...], preferred_element_type=F32)
        r = lax.rsqrt(jnp.mean(y * y, axis=-1, keepdims=True) + EPS)
        yh = y * r
        pw = pw_ref[...]
        err = x1_ref[...] + yh * pw - tg_ref[...]
        part = jnp.sum(jnp.sum(err * err, axis=-1, keepdims=True), axis=0, keepdims=True) * (0.5 / D_MODEL)
        loss_ref[...] += jnp.broadcast_to(part, loss_ref.shape)
        dx2 = err * (1.0 / D_MODEL)
        dx2_ref[...] = dx2
        dpw_ref[...] += jnp.sum(dx2 * yh, axis=0, keepdims=True)
        dyh = dx2 * pw
        dy_ref[...] = (r * (dyh - yh * jnp.mean(dyh * yh, axis=-1, keepdims=True))).astype(BF)

    tok = lambda w: pl.BlockSpec((tm, w), lambda i: (i, 0))
    vec = lambda w: pl.BlockSpec((1, w), lambda i: (0, 0))
    return pl.pallas_call(
        body, name="mlp_down_loss", grid=(t // tm,),
        in_specs=[tok(D_FF), pl.BlockSpec((D_FF, D_MODEL), lambda i: (0, 0)), tok(D_MODEL), vec(D_MODEL), tok(D_MODEL)],
        out_specs=[tok(D_MODEL), tok(D_MODEL), vec(LANES), vec(D_MODEL)],
        out_shape=[_sds((t, D_MODEL), BF), _sds((t, D_MODEL)), _sds((1, LANES)), _sds((1, D_MODEL))],
        compiler_params=_params("arbitrary"),
    )(up, w_down, x1, pw, target)


def _mlp_bwd_act(dy, w_down, up):
    t = dy.shape[0]
    tm = min(MATMUL_BLOCK, t)

    def body(dy_ref, w_ref, up_ref, dup_ref):
        da = lax.dot_general(dy_ref[...], w_ref[...], (((1,), (1,)), ((), ())), preferred_element_type=F32)
        dup_ref[...] = (da * (2.0 * jnp.maximum(up_ref[...].astype(F32), 0.0))).astype(BF)

    return pl.pallas_call(
        body, name="mlp_bwd_act", grid=(t // tm,),
        in_specs=[pl.BlockSpec((tm, D_MODEL), lambda i: (i, 0)), pl.BlockSpec((D_FF, D_MODEL), lambda i: (0, 0)),
                  pl.BlockSpec((tm, D_FF), lambda i: (i, 0))],
        out_specs=pl.BlockSpec((tm, D_FF), lambda i: (i, 0)), out_shape=_sds((t, D_FF), BF),
        compiler_params=_params("parallel"),
    )(dy, w_down, up)


def _mlp_bwd_in(dup, w_up, x1, plw, dx2, mixed, pmw, after):
    t = dup.shape[0]
    tm = min(MATMUL_BLOCK, t)

    def body(dup_ref, w_ref, x1_ref, plw_ref, dx2_ref, mx_ref, pmw_ref, dx1_ref, dmixed_ref, dplw_ref, dpmw_ref):
        i = pl.program_id(0)

        @pl.when(i == 0)
        def _():
            dplw_ref[...] = jnp.zeros_like(dplw_ref)
            dpmw_ref[...] = jnp.zeros_like(dpmw_ref)

        dh = jnp.dot(dup_ref[...], w_ref[...], preferred_element_type=F32)
        x1 = x1_ref[...]
        r = lax.rsqrt(jnp.mean(x1 * x1, axis=-1, keepdims=True) + EPS)
        xh = x1 * r
        dplw_ref[...] += jnp.sum(dh * xh, axis=0, keepdims=True)
        dxh = dh * plw_ref[...]
        dx1 = dx2_ref[...] + r * (dxh - xh * jnp.mean(dxh * xh, axis=-1, keepdims=True))
        dx1_ref[...] = dx1
        mx = mx_ref[...]
        r2 = lax.rsqrt(jnp.mean(mx * mx, axis=-1, keepdims=True) + EPS)
        mh = mx * r2
        dpmw_ref[...] += jnp.sum(dx1 * mh, axis=0, keepdims=True)
        dmh = dx1 * pmw_ref[...]
        dmixed_ref[...] = (r2 * (dmh - mh * jnp.mean(dmh * mh, axis=-1, keepdims=True))).astype(BF)

    tok = lambda w: pl.BlockSpec((tm, w), lambda i: (i, 0))
    vec = lambda w: pl.BlockSpec((1, w), lambda i: (0, 0))
    return pl.pallas_call(
        _ordered(body), name="mlp_bwd_in", grid=(t // tm,),
        in_specs=[ANY_SPEC, tok(D_FF), pl.BlockSpec((D_FF, D_MODEL), lambda i: (0, 0)), tok(D_MODEL),
                  vec(D_MODEL), tok(D_MODEL), tok(D_MODEL), vec(D_MODEL)],
        out_specs=[tok(D_MODEL), tok(D_MODEL), vec(D_MODEL), vec(D_MODEL)],
        out_shape=[_sds((t, D_MODEL)), _sds((t, D_MODEL), BF), _sds((1, D_MODEL)), _sds((1, D_MODEL))],
        compiler_params=_params("arbitrary"),
    )(after, dup, w_up, x1, plw, dx2, mixed, pmw)


def _wgrad(a, b, a_cols, split=1, a_fn=None, a_block0=0, name="wgrad"):
    t, b_cols = b.shape
    n_a = (a.shape[1] - a_block0 * a_cols) // a_cols if a_block0 else a.shape[1] // a_cols

    def body(a_ref, b_ref, o_ref):
        av = a_ref[...]
        if a_fn is not None:
            av = a_fn(av)
        o_ref[...] = _mm_tn(av, b_ref[...]).astype(BF).reshape(o_ref.shape)

    return pl.pallas_call(
        body, name=name, grid=(n_a,),
        in_specs=[pl.BlockSpec((t, a_cols), lambda i: (0, i + a_block0)), pl.BlockSpec((t, b_cols), lambda i: (0, 0))],
        out_specs=pl.BlockSpec((split, a_cols // split, b_cols), lambda i: (i, 0, 0)),
        out_shape=_sds((n_a * split, a_cols // split, b_cols), BF),
        compiler_params=_params("parallel"),
    )(a, b)


def _wgrad_pre_t(at, b, b_cols, name):
    rows, t = at.shape
    n_b = b.shape[1] // b_cols

    def body(a_ref, b_ref, o_ref):
        o_ref[0] = jnp.dot(a_ref[...], b_ref[...], preferred_element_type=F32).astype(BF)

    return pl.pallas_call(
        body, name=name, grid=(n_b,),
        in_specs=[pl.BlockSpec((rows, t), lambda j: (0, 0)), pl.BlockSpec((t, b_cols), lambda j: (0, j))],
        out_specs=pl.BlockSpec((1, rows, b_cols), lambda j: (j, 0, 0)), out_shape=_sds((n_b, rows, b_cols), BF),
        compiler_params=_params("parallel"),
    )(at, b)


def _small_bwd(proj, fb, al, dtb, dcq, dckt, dbe, dge, dproj):
    t = proj.shape[0]

    def body(sm_ref, fb_ref, al_ref, dtb_ref, dcq_ref, dckt_ref, dbe_ref, dge_ref, _, dsm_ref, dvec_ref):
        s = sm_ref[...]
        lane = _iota((1, LANES), 1)
        dcum = dcq_ref[...] - dckt_ref[...].T
        row = _iota((t, LANES), 0)
        step = 1
        while step < t:
            dcum = dcum + _shift_up(dcum, step, row)
            step *= 2
        dff = dcum * _sigmoid(-(s + fb_ref[...]))
        dbeta = jnp.zeros((t, LANES), F32)
        dg = jnp.zeros((t, LANES), F32)
        for hd in range(N_GDN_HEADS):
            dbeta = jnp.where(lane == SM_GB + hd, dbe_ref[:, hd * LANES:hd * LANES + 1], dbeta)
            dg = jnp.where(lane == SM_GA + hd, dge_ref[:, hd * LANES:hd * LANES + 1], dg)
        beta = _sigmoid(s)
        dgb = dbeta * beta * (1.0 - beta)
        za = s + dtb_ref[...]
        nea = -jnp.exp(al_ref[...])
        dga = dg * nea * _sigmoid(za)
        is_f = lane < SM_GB
        is_b = (lane >= SM_GB) & (lane < SM_GA)
        is_a = (lane >= SM_GA) & (lane < SM_GA + 4)
        dsm_ref[...] = jnp.where(is_f, dff, jnp.where(is_b, dgb, jnp.where(is_a, dga, 0.0))).astype(BF)
        dvec_ref[...] = jnp.zeros_like(dvec_ref)
        dvec_ref[0:1, :] = jnp.sum(jnp.where(is_f, dff, 0.0), axis=0, keepdims=True)
        dvec_ref[1:2, :] = jnp.sum(jnp.where(is_a, dg * nea * _softplus(za), 0.0), axis=0, keepdims=True)
        dvec_ref[2:3, :] = jnp.sum(jnp.where(is_a, dga, 0.0), axis=0, keepdims=True)

    vec = pl.BlockSpec((1, LANES), lambda i: (0, 0))
    full = lambda r, c: pl.BlockSpec((r, c), lambda i: (0, 0))
    small = pl.BlockSpec((t, LANES), lambda i: (0, COL_SMALL // LANES))
    return pl.pallas_call(
        body, name="small_bwd", grid=(1,),
        in_specs=[small, vec, vec, vec, full(t, LANES), full(LANES, t), full(t, 512), full(t, 512), ANY_SPEC],
        out_specs=[small, full(8, LANES)], out_shape=[_sds(dproj.shape, BF), _sds((8, LANES))],
        input_output_aliases={8: 0}, compiler_params=_params("arbitrary"),
    )(proj, fb, al, dtb, dcq, dckt, dbe, dge, dproj)


def _in_bwd(dproj, wt_al, x, nw, dx1, after):
    t = x.shape[0]
    tm = min(MATMUL_BLOCK, t)

    def body(dp_ref, w_ref, x_ref, nw_ref, dx1_ref, dx_ref, dnw_ref):
        i = pl.program_id(0)

        @pl.when(i == 0)
        def _():
            dnw_ref[...] = jnp.zeros_like(dnw_ref)

        dh = jnp.dot(dp_ref[...], w_ref[...], preferred_element_type=F32)
        xv = x_ref[...]
        r = lax.rsqrt(jnp.mean(xv * xv, axis=-1, keepdims=True) + EPS)
        xh = xv * r
        dnw_ref[...] += jnp.sum(dh * xh, axis=0, keepdims=True)
        dxh = dh * nw_ref[...]
        dx_ref[...] = dx1_ref[...] + r * (dxh - xh * jnp.mean(dxh * xh, axis=-1, keepdims=True))

    tok = lambda w: pl.BlockSpec((tm, w), lambda i: (i, 0))
    vec = lambda w: pl.BlockSpec((1, w), lambda i: (0, 0))
    return pl.pallas_call(
        _ordered(body), name="in_bwd", grid=(t // tm,),
        in_specs=[ANY_SPEC, tok(PROJ_W), pl.BlockSpec((PROJ_W, D_MODEL), lambda i: (0, 0)), tok(D_MODEL), vec(D_MODEL),
                  tok(D_MODEL)],
        out_specs=[tok(D_MODEL), vec(D_MODEL)], out_shape=[_sds((t, D_MODEL)), _sds((1, D_MODEL))],
        compiler_params=_params("arbitrary"),
    )(after, dproj, wt_al, x, nw, dx1)


def _row(v, width=None):
    v = v.reshape(1, -1).astype(F32)
    if width is not None and v.shape[1] < width:
        v = jnp.pad(v, ((0, 0), (0, width - v.shape[1])))
    return v


def _lane_vec(v, first):
    return jnp.pad(v.astype(F32), (first, LANES - first - v.shape[0])).reshape(1, LANES)


def _local_step(x, target, wt_al, started, late_weights, on_grads, convw, pre_mix_norm, fox_f_bias, fox_out_norm,
                gdn_a_log, gdn_dt_bias, gdn_out_norm, post_mix_norm, pre_mlp_norm, post_mlp_norm):
    t = x.shape[0]
    nch = t // CHUNK
    nw, pmw, plw, pw = _row(pre_mix_norm), _row(post_mix_norm), _row(pre_mlp_norm), _row(post_mlp_norm)
    fb, al, dtb = _lane_vec(fox_f_bias, SM_FF), _lane_vec(gdn_a_log, SM_GA), _lane_vec(gdn_dt_bias, SM_GA)
    fnw = _row(jnp.tile(fox_out_norm, 2))
    gnw = _row(gdn_out_norm)

    proj, h = _norm_proj(x, nw, wt_al, started)
    cumt, beta, g = _small_prep(proj, fb, al, dtb)
    o_fox, lse, fox_n = _fox_fwd(proj, cumt, fnw)
    qn, kn, cv, gc, be, mmat, amat = _gdn_prep(proj, convw, beta, g)
    n_prob = N_GDN_HEADS * nch
    m3 = mmat.reshape(n_prob, CHUNK, CHUNK)
    if n_prob < LANES:
        m3 = jnp.pad(m3, ((0, LANES - n_prob), (0, 0), (0, 0)))
    tinv = _tri_inverse(m3)[:n_prob].reshape(N_GDN_HEADS, nch, CHUNK, CHUNK)
    token = late_weights("mlp_relay", tinv)
    gdn_o, s_all, vn_all = _gdn_scan(qn, kn, cv, be, gc, tinv, amat)
    w_out = late_weights("w_out", gdn_o)
    x1, h2, mixed, omix, h2t = _mix_out(fox_n, gdn_o, proj, gnw, w_out, x, pmw, plw, token)
    w_up, w_down = late_weights("mlp", h2)
    up = _mlp_up(h2, w_up)
    dy, dx2, loss, d_pw = _mlp_down_loss(up, w_down, x1, pw, target)

    dup = _mlp_bwd_act(dy, w_down, up)
    relu2 = lambda u: jnp.square(jnp.maximum(u.astype(F32), 0.0))
    g_down = _wgrad(up, dy, D_FF // N_DEV, a_fn=relu2, name="wgrad_down")
    g_up = _wgrad_pre_t(h2t, dup, D_FF // N_DEV, name="wgrad_up")
    token = on_grads("mlp", (g_up, g_down))
    dx1, dmixed, d_plw, d_pmw = _mlp_bwd_in(dup, w_up, x1, plw, dx2, mixed, pmw, token)
    token = on_grads("w_out", _wgrad(omix, dmixed, 512, split=4, name="wgrad_out"))
    do_fox, dgo, dproj, d_fnw, d_gnw = _out_bwd(dmixed, w_out, o_fox, gdn_o, proj, fnw, gnw, token)
    dproj, dcq, dckt = _fox_bwd(proj, cumt, lse, o_fox, do_fox, dproj)
    dqn, dkn, dcv, dbe, dge = _gdn_bwd(qn, kn, cv, be, gc, tinv, amat, s_all, vn_all, dgo)
    dproj, dwq, dwk, dwv = _gdn_bwd_conv(proj, convw, dqn, dkn, dcv, dproj)
    dproj, dvec = _small_bwd(proj, fb, al, dtb, dcq, dckt, dbe, dge, dproj)
    g_main = _wgrad(dproj, h, WGRAD_IN_ROWS, name="wgrad_in")
    g_tail = _wgrad(dproj, h, LANES, a_block0=COL_SMALL // LANES, name="wgrad_in_small")
    token = on_grads("w_in", (g_main, g_tail))
    grad_x, d_nw = _in_bwd(dproj, wt_al, x, nw, dx1, token)
    small = dict(norms=(d_nw, d_pmw, d_plw, d_pw), fox_out_norm=d_fnw, gdn_out_norm=d_gnw, loss=loss, vectors=dvec,
                 conv=(dwq, dwk, dwv))
    return grad_x, small


MESH_IDS = pl.DeviceIdType.MESH
CHIP_FLIPS = ((0, 0), (1, 0), (0, 1), (1, 1))


def _place():
    return lax.axis_index("x"), lax.axis_index("y"), lax.axis_index("c")


def _all_gather(blocks, later, dtype):
    n, k = len(blocks), len(later)

    def body(me_ref, *refs):
        ins, shards, outs = refs[:n], refs[n:n + k], refs[n + k:2 * n + k]
        zone_own, to_send = refs[2 * n + k:2 * n + 2 * k], refs[2 * n + 2 * k:2 * n + 3 * k]
        send_sems, recv_sems, local_sems = refs[2 * n + 3 * k:]
        x, y, c = _place()
        sibling = (x, y, 1 - c)
        chips = [(x ^ fx, y ^ fy) for fx, fy in CHIP_FLIPS[1:]]

        def slot(out, px, py, pc):
            return out.at[4 * px + 2 * py + pc]

        def copy(a, k, block, to, src=None):
            return pltpu.make_async_remote_copy(
                src_ref=slot(outs[a], *block) if src is None else src, dst_ref=slot(outs[a], *block),
                send_sem=send_sems.at[a, k], recv_sem=recv_sems.at[a, k], device_id=to, device_id_type=MESH_IDS)

        pending = []
        for a in range(n):
            mine = pltpu.make_async_copy(ins[a], slot(outs[a], x, y, c), local_sems.at[a])
            mine.start()
            pending.append(mine)
        sends = []
        for a in range(n):
            first = [copy(a, 0, (x, y, c), sibling, src=ins[a])]
            first += [copy(a, 1 + j, (x, y, c), (*chip, c), src=ins[a]) for j, chip in enumerate(chips)]
            for cp in first:
                cp.start()
            sends += first
        for a, (_, transposed) in enumerate(later):
            val = shards[a][...]
            val = (val.T if transposed else val).astype(dtype)
            zone_own[a][0] = val
            to_send[a][...] = val
        for a in range(n):
            for j, chip in enumerate(chips):
                copy(a, 1 + j, (*chip, c), (x, y, c)).wait_recv()
                fwd = copy(a, 4 + j, (*chip, c), sibling)
                fwd.start()
                sends.append(fwd)
        for a in range(n):
            copy(a, 0, sibling, (x, y, c)).wait_recv()
            for j, chip in enumerate(chips):
                copy(a, 4 + j, (*chip, 1 - c), (x, y, c)).wait_recv()
        for cp in sends:
            cp.wait_send()
        for cp in pending:
            cp.wait()

    shapes = [s_.shape[::-1] if transposed else s_.shape for s_, transposed in later]
    mine = lambda sh: pl.BlockSpec((1,) + sh, lambda i, me_ref: (me_ref[0], 0, 0))
    whole = lambda sh: pl.BlockSpec(sh, lambda i, me_ref: (0, 0))
    x, y, c = _place()
    out = pl.pallas_call(
        body, name="all_gather_weights",
        grid_spec=pltpu.PrefetchScalarGridSpec(
            num_scalar_prefetch=1, grid=(1,), in_specs=[ANY_SPEC] * n + [whole(s_.shape) for s_, _ in later],
            out_specs=[ANY_SPEC] * n + [mine(sh) for sh in shapes] + [whole(sh) for sh in shapes],
            scratch_shapes=[pltpu.SemaphoreType.DMA((n, 7)), pltpu.SemaphoreType.DMA((n, 7)),
                            pltpu.SemaphoreType.DMA((n,))]),
        out_shape=[_sds((N_DEV,) + b.shape, b.dtype) for b in blocks] + [_sds((N_DEV,) + sh, dtype) for sh in shapes]
        + [_sds(sh, dtype) for sh in shapes],
        compiler_params=pltpu.CompilerParams(dimension_semantics=("arbitrary",), vmem_limit_bytes=VMEM_LIMIT,
                                             has_side_effects=True),
    )((4 * x + 2 * y + c).astype(jnp.int32).reshape(1), *blocks, *[s_ for s_, _ in later])
    return out[:n], out[n:n + k], out[n + k:]


def _adamw(w, g, m, v):
    m = ADAM_B1 * m + (1.0 - ADAM_B1) * g
    v = ADAM_B2 * v + (1.0 - ADAM_B2) * (g * g)
    m_hat = m / (1.0 - ADAM_B1 ** ADAM_STEP)
    v_hat = v / (1.0 - ADAM_B2 ** ADAM_STEP)
    return -ADAM_LR * (m_hat / (jnp.sqrt(v_hat) + ADAM_EPS) + ADAM_WD * w), m, v


def _pair_reduce(g, name):
    _, r, c_ = g.shape
    n = len(CHIP_FLIPS)

    def body(g_ref, out_ref, sib_buf, send_sems, recv_sems):
        x, y, c = _place()
        chips = [(x ^ fx, y ^ fy) for fx, fy in CHIP_FLIPS]
        piece = lambda chip, core: g_ref.at[4 * chip[0] + 2 * chip[1] + core]
        copies = [pltpu.make_async_remote_copy(
            src_ref=piece(chip, 1 - c), dst_ref=sib_buf.at[j], send_sem=send_sems.at[j], recv_sem=recv_sems.at[j],
            device_id=(x, y, 1 - c), device_id_type=MESH_IDS) for j, chip in enumerate(chips)]
        for cp in copies:
            cp.start()
        for j, chip in enumerate(chips):
            copies[j].wait_recv()
            out_ref[j] = (piece(chip, c)[...].astype(F32) + sib_buf[j].astype(F32)).astype(BF)
        for cp in copies:
            cp.wait_send()

    return pl.pallas_call(
        body, name=name, in_specs=[VMEM_SPEC], out_specs=VMEM_SPEC, out_shape=_sds((n, r, c_), BF),
        scratch_shapes=[pltpu.VMEM((n, r, c_), BF), pltpu.SemaphoreType.DMA((n,)), pltpu.SemaphoreType.DMA((n,))],
        compiler_params=pltpu.CompilerParams(vmem_limit_bytes=VMEM_LIMIT, has_side_effects=True),
    )(g)


HBM_SPEC = pl.BlockSpec(memory_space=pltpu.HBM)
SEM_SPEC = pl.BlockSpec(memory_space=pltpu.SEMAPHORE)
DATAFLOW = pltpu.SideEffectType.DATAFLOW_SIDE_EFFECTING


def _peers():
    x, y, c = _place()
    return 4 * x + 2 * y + c, [(x ^ (k >> 2), y ^ ((k >> 1) & 1), c ^ (k & 1)) for k in range(1, N_DEV)]


def _peer_index(peer):
    return 4 * peer[0] + 2 * peer[1] + peer[2]


def _exchange_start(srcs, zones, pieces, name, chips=False):
    n = len(srcs)
    fresh = zones is None
    if fresh:
        slots = len(CHIP_FLIPS) if chips else N_DEV
        zones = [_sds((slots,) + (v.shape[1:] if pieces else v.shape), v.dtype) for v in srcs]
    n_in = n if fresh else 2 * n
    among_chips = list(chips) if isinstance(chips, (list, tuple)) else [chips] * n

    def body(*refs):
        ins, sems, token = refs[:n], refs[n_in:n_in + 2 * n], refs[-1]
        zs = refs[n_in + 3 * n:n_in + 4 * n] if fresh else refs[n:2 * n]
        me, peers = _peers()
        x, y, c = _place()
        for a in range(n):
            if among_chips[a] and pieces:
                routes = [((x ^ fx, y ^ fy, c), j, j) for j, (fx, fy) in enumerate(CHIP_FLIPS) if j]
            elif among_chips[a]:
                routes = [((x ^ fx, y ^ fy, c), None, me) for fx, fy in CHIP_FLIPS[1:]]
            else:
                routes = [(peer, _peer_index(peer) if pieces else None, me) for peer in peers]
            for peer, src_slot, dst_slot in routes:
                pltpu.make_async_remote_copy(
                    src_ref=ins[a] if src_slot is None else ins[a].at[src_slot], dst_ref=zs[a].at[dst_slot],
                    send_sem=sems[2 * a], recv_sem=sems[2 * a + 1], device_id=peer, device_id_type=MESH_IDS).start()
        token[...] = jnp.zeros_like(token)

    hbm = lambda v: pltpu.with_memory_space_constraint(v, pltpu.HBM)
    out = pl.pallas_call(
        body, name=name,
        out_shape=tuple([pltpu.SemaphoreType.DMA(())] * (2 * n) + [pltpu.HBM(v.shape, v.dtype) for v in srcs]
                        + [pltpu.HBM(z.shape, z.dtype) for z in zones] + [_sds((8, LANES))]),
        in_specs=[HBM_SPEC] * n_in, out_specs=tuple([SEM_SPEC] * (2 * n) + [HBM_SPEC] * (2 * n) + [VMEM_SPEC]),
        input_output_aliases={i: 2 * n + i for i in range(n_in)},
        compiler_params=pltpu.CompilerParams(has_side_effects=DATAFLOW),
    )(*[hbm(v) for v in srcs], *([] if fresh else [hbm(z) for z in zones]))
    return out[:2 * n], out[2 * n:3 * n], out[3 * n:4 * n], out[-1]


def _relay_start(zones, name):
    n = len(zones)

    def body(*refs):
        zs, sems, token = refs[:n], refs[n:3 * n], refs[-1]
        x, y, c = _place()
        for fx, fy in CHIP_FLIPS:
            slot = 4 * (x ^ fx) + 2 * (y ^ fy) + c
            for a in range(n):
                pltpu.make_async_remote_copy(
                    src_ref=zs[a].at[slot], dst_ref=zs[a].at[slot], send_sem=sems[2 * a], recv_sem=sems[2 * a + 1],
                    device_id=(x, y, 1 - c), device_id_type=MESH_IDS).start()
        token[...] = jnp.zeros_like(token)

    out = pl.pallas_call(
        body, name=name,
        out_shape=tuple([pltpu.SemaphoreType.DMA(())] * (2 * n) + [pltpu.HBM(z.shape, z.dtype) for z in zones]
                        + [_sds((8, LANES))]),
        in_specs=[HBM_SPEC] * n, out_specs=tuple([SEM_SPEC] * (2 * n) + [HBM_SPEC] * n + [VMEM_SPEC]),
        input_output_aliases={i: 2 * n + i for i in range(n)},
        compiler_params=pltpu.CompilerParams(has_side_effects=DATAFLOW),
    )(*[pltpu.with_memory_space_constraint(z, pltpu.HBM) for z in zones])
    return out[:2 * n], [], out[2 * n:3 * n], out[-1]


def _exchange_wait(sems, srcs, zones, after, name, chips=False, n_copies=None):
    n, n_src = len(zones), len(srcs)
    after = list(after) if isinstance(after, (list, tuple)) else [after]
    n_copies = n_copies or (len(CHIP_FLIPS) - 1 if chips else N_DEV - 1)

    def body(*refs):
        zs, sm = refs[n_src:n_src + n], refs[n_src + n:n_src + 3 * n]
        me, peers = _peers()
        for a in range(n):
            seven = zs[a].at[pl.ds(0, n_copies)]
            cp = pltpu.make_async_remote_copy(src_ref=seven, dst_ref=seven, send_sem=sm[2 * a], recv_sem=sm[2 * a + 1],
                                              device_id=peers[0], device_id_type=MESH_IDS)
            cp.wait_send()
            cp.wait_recv()

    out = pl.pallas_call(
        body, name=name, out_shape=tuple([pltpu.HBM(v.shape, v.dtype) for v in srcs] + [pltpu.HBM(z.shape, z.dtype) for z in zones]),
        in_specs=[HBM_SPEC] * (n_src + n) + [SEM_SPEC] * (2 * n) + [ANY_SPEC] * len(after),
        out_specs=tuple([HBM_SPEC] * (n_src + n)), input_output_aliases={i: i for i in range(n_src + n)},
        compiler_params=pltpu.CompilerParams(has_side_effects=DATAFLOW),
    )(*srcs, *zones, *sems, *after)
    return out[:n_src], out[n_src:]


def _sum_adamw(zone, own, w, m, v, name, chips=False):
    n_slots, r, c_ = zone.shape
    rb = next((b for b in (256, 128) if r % b == 0), r)

    def body(me_ref, z_ref, own_ref, w_ref, m_ref, v_ref, grad_ref, delta_ref, nm_ref, nv_ref):
        total = None
        for d in range(n_slots):
            part = jnp.where(me_ref[0] == d, own_ref[0], z_ref[d]).astype(F32)
            total = part if total is None else total + part
        grad_ref[...] = total
        delta_ref[...], nm_ref[...], nv_ref[...] = _adamw(w_ref[...], total, m_ref[...], v_ref[...])

    x, y, c = _place()
    mine = 0 * x if chips else 4 * x + 2 * y + c
    blk = pl.BlockSpec((rb, c_), lambda i, me_ref: (i, 0))
    return pl.pallas_call(
        body, name=name,
        grid_spec=pltpu.PrefetchScalarGridSpec(
            num_scalar_prefetch=1, grid=(r // rb,),
            in_specs=[pl.BlockSpec((n_slots, rb, c_), lambda i, me_ref: (0, i, 0)),
                      pl.BlockSpec((1, rb, c_), lambda i, me_ref: (me_ref[0], i, 0)), blk, blk, blk],
            out_specs=[blk] * 4),
        out_shape=[_sds((r, c_))] * 4, compiler_params=_params("parallel"),
    )(mine.astype(jnp.int32).reshape(1), zone, own, w, m, v)


SMALL_NORMS = ("pre_mix_norm", "post_mix_norm", "pre_mlp_norm", "post_mlp_norm")
SMALL_ORDER = SMALL_NORMS + ("fox_out_norm", "gdn_out_norm", "fox_f_bias", "gdn_a_log", "gdn_dt_bias", "gdn_conv_w")
CONV_SLAB_ROWS, CONV_SLAB_LANES = 8, 256


def _small_pack(small):
    def body(n0, n1, n2, n3, fnw_ref, gnw_ref, loss_ref, vec_ref, out_ref):
        out_ref[...] = jnp.zeros_like(out_ref)
        for i, ref in enumerate((n0, n1, n2, n3)):
            out_ref[i:i + 1, :] = ref[...]
        out_ref[4:5, 0:LANES] = fnw_ref[...]
        out_ref[4:5, LANES:2 * LANES] = gnw_ref[...]
        out_ref[4:5, 2 * LANES:3 * LANES] = loss_ref[...]
        out_ref[5:8, 0:LANES] = vec_ref[0:3, :]

    return pl.pallas_call(body, name="small_pack", in_specs=[VMEM_SPEC] * 8, out_specs=VMEM_SPEC,
                          out_shape=_sds((8, D_MODEL)))(*small["norms"], small["fox_out_norm"], small["gdn_out_norm"],
                                                        small["loss"], small["vectors"])


def _conv_slabs(dconv):
    blocks = dconv.reshape(CONV_K, N_DEV, -1).transpose(1, 0, 2)
    blocks = jnp.pad(blocks, ((0, 0), (0, CONV_SLAB_ROWS - CONV_K), (0, CONV_SLAB_LANES - blocks.shape[2])))
    return blocks.reshape(N_DEV * CONV_SLAB_ROWS, CONV_SLAB_LANES)


def _small_update(zone, conv_zone, own, own_conv, w, m, v):
    n = len(SMALL_ORDER)
    n_conv = w["gdn_conv_w"].shape[1]

    def body(me_ref, z_ref, zc_ref, own_ref, ownc_ref, *refs):
        params, loss_ref, outs, (tot, totc) = refs[:3 * n], refs[3 * n], refs[3 * n + 1:7 * n + 1], refs[-2:]
        total, total_c = None, None
        for d in range(N_DEV):
            part = jnp.where(me_ref[0] == d, own_ref[...], z_ref[d])
            part_c = jnp.where(me_ref[0] == d, ownc_ref[...], zc_ref[d])
            total, total_c = (part, part_c) if d == 0 else (total + part, total_c + part_c)
        tot[...] = total
        totc[...] = total_c
        loss_ref[...] = tot[4, 2 * LANES:2 * LANES + 1]
        mine = totc[pl.ds(pl.multiple_of(me_ref[0] * CONV_SLAB_ROWS, CONV_SLAB_ROWS), CONV_SLAB_ROWS), :]
        g = dict(zip(SMALL_NORMS, (tot[0], tot[1], tot[2], tot[3])))
        g.update(fox_out_norm=tot[4, 0:FOX_HEAD_DIM], gdn_out_norm=tot[4, LANES:LANES + GDN_HEAD_DIM],
                 fox_f_bias=tot[5, SM_FF:SM_FF + N_FOX_HEADS], gdn_a_log=tot[6, SM_GA:SM_GA + N_GDN_HEADS],
                 gdn_dt_bias=tot[7, SM_GA:SM_GA + N_GDN_HEADS], gdn_conv_w=mine[0:CONV_K, 0:n_conv])
        for i, name in enumerate(SMALL_ORDER):
            w_ref, m_ref, v_ref = params[3 * i:3 * i + 3]
            outs[4 * i][...] = g[name]
            outs[4 * i + 1][...], outs[4 * i + 2][...], outs[4 * i + 3][...] = _adamw(w_ref[...], g[name], m_ref[...],
                                                                                     v_ref[...])

    x, y, c = _place()
    operands = [a[name] for name in SMALL_ORDER for a in (w, m, v)]
    out = pl.pallas_call(
        body, name="small_update",
        in_specs=[pl.BlockSpec(memory_space=pltpu.SMEM)] + [VMEM_SPEC] * (4 + 3 * n), out_specs=[VMEM_SPEC] * (1 + 4 * n),
        out_shape=[_sds((1,))] + [_sds(w[name].shape) for name in SMALL_ORDER for _ in range(4)],
        scratch_shapes=[pltpu.VMEM(zone.shape[1:], F32), pltpu.VMEM(conv_zone.shape[1:], F32)],
    )((4 * x + 2 * y + c).astype(jnp.int32).reshape(1), zone, conv_zone, own, own_conv, *operands)
    return out[0][0], {name: out[1 + 4 * i:5 + 4 * i] for i, name in enumerate(SMALL_ORDER)}


def _native_rows():
    groups = []
    for first, n_groups in ((0, N_FOX_HEADS // 2), (D_FOX * 3 + N_FOX_HEADS, N_GDN_HEADS)):
        for g in range(n_groups):
            groups += [(first + part * n_groups * LANES + g * LANES, first + part * n_groups * LANES + (g + 1) * LANES)
                       for part in range(3)]
    return tuple(groups) + ((3088, 3600), (1536, 1544), (3080, 3088))


NATIVE_ROWS = _native_rows()


W_IN_PIECE = D_PROJ // N_DEV
WGRAD_IN_ROWS = 512
SHUFFLE_LANES = 256


def _to_aligned_moves():
    moves, o = [], 0
    for lo, hi in NATIVE_ROWS:
        r = lo
        while r < hi:
            d = r // W_IN_PIECE
            k = min(hi, (d + 1) * W_IN_PIECE) - r
            moves.append((0, d, r - d * W_IN_PIECE, 0, o, k))
            r, o = r + k, o + k
    return moves


def _from_aligned_moves():
    moves = []
    for _, d, a, _, o, k in _to_aligned_moves():
        while k:
            n = min(k, WGRAD_IN_ROWS - o % WGRAD_IN_ROWS) if o < COL_SMALL else k
            moves.append((0, o // WGRAD_IN_ROWS, o % WGRAD_IN_ROWS, d, a, n) if o < COL_SMALL else
                         (1, 0, o - COL_SMALL, d, a, n))
            o, a, k = o + n, a + n, k - n
    return moves


def _shuffle_rows(srcs, moves, out_shape, name):
    c = srcs[0].shape[-1]

    def body(*refs):
        s_refs, o_ref, s_f, o_f = refs[:len(srcs)], refs[len(srcs)], refs[len(srcs) + 1:-1], refs[-1]
        for s_ref, f in zip(s_refs, s_f):
            f[...] = s_ref[...].astype(F32)
        o_f[...] = jnp.zeros_like(o_f)
        for i, ss, so, ds, do, k in moves:
            o_f[ds, pl.ds(do, k), :] = s_f[i][ss, pl.ds(so, k), :]
        o_ref[...] = o_f[...].astype(BF)

    blk = lambda shape: pl.BlockSpec(tuple(shape[:-1]) + (SHUFFLE_LANES,), lambda j: (0, 0, j))
    scratch = lambda shape: pltpu.VMEM(tuple(shape[:-1]) + (SHUFFLE_LANES,), F32)
    return pl.pallas_call(
        body, name=name, grid=(c // SHUFFLE_LANES,), in_specs=[blk(s.shape) for s in srcs], out_specs=blk(out_shape),
        out_shape=_sds(out_shape, BF), scratch_shapes=[scratch(s.shape) for s in srcs] + [scratch(out_shape)],
        compiler_params=_params("parallel"),
    )(*srcs)


def _cols_from_pieces(p):
    return p.transpose(1, 0, 2).reshape(p.shape[1], -1)


WEIGHT_ORDER = ("pre_mix_norm", "w_in", "fox_f_bias", "fox_out_norm", "gdn_conv_w", "gdn_a_log", "gdn_dt_bias",
                "gdn_out_norm", "w_out", "post_mix_norm", "pre_mlp_norm", "w_up", "w_down", "post_mlp_norm")


def kernel(x, pre_mix_norm, w_in, fox_f_bias, fox_out_norm, gdn_conv_w, gdn_a_log, gdn_dt_bias, gdn_out_norm, w_out, post_mix_norm, pre_mlp_norm, w_up, w_down, post_mlp_norm, loss_target, m_pre_mix_norm, m_w_in, m_fox_f_bias, m_fox_out_norm, m_gdn_conv_w, m_gdn_a_log, m_gdn_dt_bias, m_gdn_out_norm, m_w_out, m_post_mix_norm, m_pre_mlp_norm, m_w_up, m_w_down, m_post_mlp_norm, v_pre_mix_norm, v_w_in, v_fox_f_bias, v_fox_out_norm, v_gdn_conv_w, v_gdn_a_log, v_gdn_dt_bias, v_gdn_out_norm, v_w_out, v_post_mix_norm, v_pre_mlp_norm, v_w_up, v_w_down, v_post_mlp_norm):
    w = dict(pre_mix_norm=pre_mix_norm, w_in=w_in, fox_f_bias=fox_f_bias, fox_out_norm=fox_out_norm,
             gdn_conv_w=gdn_conv_w, gdn_a_log=gdn_a_log, gdn_dt_bias=gdn_dt_bias, gdn_out_norm=gdn_out_norm, w_out=w_out,
             post_mix_norm=post_mix_norm, pre_mlp_norm=pre_mlp_norm, w_up=w_up, w_down=w_down, post_mlp_norm=post_mlp_norm)
    mom = dict(pre_mix_norm=m_pre_mix_norm, w_in=m_w_in, fox_f_bias=m_fox_f_bias, fox_out_norm=m_fox_out_norm,
               gdn_conv_w=m_gdn_conv_w, gdn_a_log=m_gdn_a_log, gdn_dt_bias=m_gdn_dt_bias, gdn_out_norm=m_gdn_out_norm,
               w_out=m_w_out, post_mix_norm=m_post_mix_norm, pre_mlp_norm=m_pre_mlp_norm, w_up=m_w_up, w_down=m_w_down,
               post_mlp_norm=m_post_mlp_norm)
    var = dict(pre_mix_norm=v_pre_mix_norm, w_in=v_w_in, fox_f_bias=v_fox_f_bias, fox_out_norm=v_fox_out_norm,
               gdn_conv_w=v_gdn_conv_w, gdn_a_log=v_gdn_a_log, gdn_dt_bias=v_gdn_dt_bias, gdn_out_norm=v_gdn_out_norm,
               w_out=v_w_out, post_mix_norm=v_post_mix_norm, pre_mlp_norm=v_pre_mlp_norm, w_up=v_w_up, w_down=v_w_down,
               post_mlp_norm=v_post_mlp_norm)

    (win_g, conv_g), zones, shards = _all_gather([w_in.T.astype(BF), gdn_conv_w],
                                                 [(w_out, False), (w_up, True), (w_down, False)], BF)
    wt_al = _shuffle_rows([win_g], _to_aligned_moves(), (1, PROJ_W, D_MODEL), "w_in_to_aligned")[0]
    convw = _cols_from_pieces(conv_g)
    sems, shards, zones, after = _exchange_start(shards, zones, False, "gather_start", chips=[False, True, True])
    gathers = dict(w_out=(sems[:2], shards[:1], zones[:1], after), mlp=(sems[2:], shards[1:], zones[1:], after))

    def late_weights(name, after):
        if name == "mlp_relay":
            sems, shards, zones, _ = gathers["mlp"]
            _, zones = _exchange_wait(sems, shards, zones, after, "gather_mlp_wait", chips=True)
            gathers["mlp"] = _relay_start(zones, "gather_mlp_relay")
            return gathers["mlp"][3]
        sems, shards, zones, _ = gathers[name]
        _, got = _exchange_wait(sems, shards, zones, after, "gather_" + name + "_done",
                                n_copies=len(CHIP_FLIPS) if name == "mlp" else None)
        if name == "w_out":
            return got[0].reshape(D_MODEL, D_MODEL)
        return got[0].reshape(D_FF, D_MODEL), got[1].reshape(D_FF, D_MODEL)

    scatters = {}

    def on_grads(name, g):
        chips = name == "w_in"
        if name == "w_in":
            g = _shuffle_rows(list(g), _from_aligned_moves(), (N_DEV, W_IN_PIECE, D_MODEL), "w_in_grad_from_aligned")
            g = _pair_reduce(g, "pair_reduce_w_in")
        srcs = list(g) if name == "mlp" else [g]
        scatters[name] = _exchange_start(srcs, None, True, "scatter_" + name + "_start", chips=chips)
        return scatters[name][3]

    grad_x, small = _local_step(
        x[0], loss_target[0], wt_al, after, late_weights, on_grads, convw, pre_mix_norm,
        fox_f_bias, fox_out_norm, gdn_a_log, gdn_dt_bias, gdn_out_norm, post_mix_norm, pre_mlp_norm, post_mlp_norm)
    slabs = [_small_pack(small), _conv_slabs(jnp.concatenate(small["conv"], axis=1))]
    scatters["small"] = _exchange_start(slabs, None, False, "small_start")

    grads, delta, new_m, new_v = {}, {}, {}, {}
    after = scatters["small"][3]
    for name, members in (("mlp", ("w_up", "w_down")), ("w_out", ("w_out",)), ("small", ()), ("w_in", ("w_in",))):
        sems, srcs, zones, _ = scatters[name]
        srcs, zones = _exchange_wait(sems, srcs, zones, after, "scatter_" + name + "_wait", chips=name == "w_in")
        if name == "small":
            loss, updated = _small_update(*zones, *srcs, w, mom, var)
            for n, res in updated.items():
                grads[n], delta[n], new_m[n], new_v[n] = res
            after = grads["pre_mix_norm"]
        for n, zone, own in zip(members, zones, srcs):
            if n == "w_in":
                res = _sum_adamw(zone, own, w[n].T, mom[n].T, var[n].T, "adamw_" + n, chips=True)
                grads[n], delta[n], new_m[n], new_v[n] = [r.T for r in res]
            else:
                grads[n], delta[n], new_m[n], new_v[n] = _sum_adamw(zone, own, w[n], mom[n], var[n], "adamw_" + n)
        if members:
            after = [grads[n] for n in members]

    return (loss, grad_x[None], *[grads[n] for n in WEIGHT_ORDER], *[delta[n] for n in WEIGHT_ORDER],
            *[new_m[n] for n in WEIGHT_ORDER], *[new_v[n] for n in WEIGHT_ORDER])
```

```python
import jax
import jax.numpy as jnp
from jax import lax
from jax.experimental import pallas as pl
from jax.experimental.pallas import tpu as pltpu

F32 = jnp.float32
BF = jnp.bfloat16

D_MODEL = 1024
N_FOX_HEADS, FOX_HEAD_DIM = 8, 64
N_GDN_HEADS, GDN_HEAD_DIM = 4, 128
D_FOX = N_FOX_HEADS * FOX_HEAD_DIM
D_GDN = N_GDN_HEADS * GDN_HEAD_DIM
CHUNK = 64
CONV_K = 4
D_FF = 4 * D_MODEL
EPS = 1e-6
D_PROJ = 3600
N_DEV = 8

PROJ_W = 3712
COL_FOX, COL_GDN, COL_GZ, COL_SMALL = 0, 1536, 3072, 3584
LANES = 128
QKV = 3 * LANES
SM_FF, SM_GB, SM_GA = 0, 8, 12

ADAM_LR, ADAM_B1, ADAM_B2, ADAM_EPS, ADAM_WD, ADAM_STEP = 0.001, 0.9, 0.999, 1e-08, 0.01, 10

TOKEN_BLOCK = 256
MATMUL_BLOCK = 512
FOX_SCALE = FOX_HEAD_DIM ** -0.5
GDN_QSCALE = GDN_HEAD_DIM ** -0.5
NEG_BIG = -1e30
VMEM_LIMIT = 56 * 1024 * 1024

VMEM_SPEC = pl.BlockSpec(memory_space=pltpu.VMEM)
ANY_SPEC = pl.BlockSpec(memory_space=pl.ANY)


def _sds(shape, dtype=F32):
    return jax.ShapeDtypeStruct(shape, dtype)


def _params(*sem):
    return pltpu.CompilerParams(dimension_semantics=sem if sem else None, vmem_limit_bytes=VMEM_LIMIT)


def _ordered(body):
    def ordered(_, *refs):
        body(*refs)

    return ordered


def _mm(a, b):
    return jnp.dot(a.astype(BF), b.astype(BF), preferred_element_type=F32)


def _mm_nt(a, b):
    return lax.dot_general(a.astype(BF), b.astype(BF), (((1,), (1,)), ((), ())), preferred_element_type=F32)


def _mm_tn(a, b):
    return lax.dot_general(a.astype(BF), b.astype(BF), (((0,), (0,)), ((), ())), preferred_element_type=F32)


def _sigmoid(x):
    return 1.0 / (1.0 + jnp.exp(-x))


def _softplus(x):
    return jnp.maximum(x, 0.0) + jnp.log1p(jnp.exp(-jnp.abs(x)))


def _iota(shape, dim):
    return lax.broadcasted_iota(jnp.int32, shape, dim)


def _shift_down(x, s, row):
    return jnp.where(row >= s, pltpu.roll(x, s, 0), 0.0)


def _shift_up(x, s, row):
    n = x.shape[0]
    return jnp.where(row < n - s, pltpu.roll(x, n - s, 0), 0.0)


def _norm_proj(x, nw, wt_al, after):
    t = x.shape[0]

    def body(x_ref, nw_ref, w_ref, proj_ref, h_ref):
        xv = x_ref[...]
        r = lax.rsqrt(jnp.mean(xv * xv, axis=-1, keepdims=True) + EPS)
        h = (xv * r * nw_ref[...]).astype(BF)
        h_ref[...] = h
        proj_ref[...] = lax.dot_general(h, w_ref[...], (((1,), (1,)), ((), ())), preferred_element_type=F32)

    tm = min(MATMUL_BLOCK, t)
    return pl.pallas_call(
        _ordered(body), name="norm_proj", grid=(t // tm,),
        in_specs=[ANY_SPEC, pl.BlockSpec((tm, D_MODEL), lambda i: (i, 0)), pl.BlockSpec((1, D_MODEL), lambda i: (0, 0)),
                  pl.BlockSpec((PROJ_W, D_MODEL), lambda i: (0, 0))],
        out_specs=[pl.BlockSpec((tm, PROJ_W), lambda i: (i, 0)), pl.BlockSpec((tm, D_MODEL), lambda i: (i, 0))],
        out_shape=[_sds((t, PROJ_W)), _sds((t, D_MODEL), BF)],
        compiler_params=_params("parallel"),
    )(after, x, nw, wt_al)


def _lane_column(x, lane):
    return jnp.sum(jnp.where(_iota((1, LANES), 1) == lane, x, 0.0), axis=-1, keepdims=True)


def _small_prep(proj, fb, al, dtb):
    t = proj.shape[0]

    def body(sm_ref, fb_ref, al_ref, dtb_ref, cumt_ref, beta_ref, g_ref):
        s = sm_ref[...]
        z = s + fb_ref[...]
        cum = jnp.minimum(z, 0.0) - jnp.log1p(jnp.exp(-jnp.abs(z)))
        row = _iota((t, LANES), 0)
        step = 1
        while step < t:
            cum = cum + _shift_down(cum, step, row)
            step *= 2
        cumt_ref[...] = cum.T
        beta_ref[...] = _sigmoid(s)
        g_ref[...] = -jnp.exp(al_ref[...]) * _softplus(s + dtb_ref[...])

    vec = pl.BlockSpec((1, LANES), lambda i: (0, 0))
    tok = pl.BlockSpec((t, LANES), lambda i: (0, 0))
    return pl.pallas_call(
        body, name="small_prep", grid=(1,),
        in_specs=[pl.BlockSpec((t, LANES), lambda i: (0, COL_SMALL // LANES)), vec, vec, vec],
        out_specs=[pl.BlockSpec((LANES, t), lambda i: (0, 0)), tok, tok],
        out_shape=[_sds((LANES, t)), _sds((t, LANES)), _sds((t, LANES))],
        compiler_params=_params("arbitrary"),
    )(proj, fb, al, dtb)


def _fox_stack(x, first):
    return jnp.concatenate([jnp.where(first, x, 0.0), jnp.where(first, 0.0, x)], axis=0).astype(BF)


def _fox_unstack(y, first):
    n = y.shape[0] // 2
    return jnp.where(first, y[:n], y[n:])


def _fox_logits(q2_i, kb, cumt_ref, pair, i, tq):
    klen = (i + 1) * tq
    s = lax.dot_general(q2_i, kb[:klen], (((1,), (1,)), ((), ())), preferred_element_type=F32)
    upper = _iota((2 * tq, 1), 0) < tq
    s = s - jnp.where(upper, cumt_ref[pl.ds(2 * pair, 1), 0:klen], cumt_ref[pl.ds(2 * pair + 1, 1), 0:klen])
    causal = _iota((2 * tq, tq), 1) <= _iota((2 * tq, tq), 0) % tq
    parts = [(s[:, :klen - tq], 0, klen - tq)] if i else []
    return parts + [(jnp.where(causal, s[:, klen - tq:], NEG_BIG), klen - tq, klen)]


def _fox_fwd(proj, cumt, fnw):
    t = proj.shape[0]
    tq = min(TOKEN_BLOCK, t // 2)
    nq = t // tq

    def body(q_ref, k_ref, v_ref, cumt_ref, fnw_ref, o_ref, lse_ref, fn_ref):
        j = pl.program_id(0)
        first = _iota((1, LANES), 1) < FOX_HEAD_DIM
        kb = k_ref[...].astype(BF)
        vb = v_ref[...].astype(BF)
        for i in range(nq):
            rows = slice(i * tq, (i + 1) * tq)
            q2 = _fox_stack(q_ref[rows, :] * FOX_SCALE, first)
            parts = _fox_logits(q2, kb, cumt_ref, j, i, tq)
            m = jnp.max(parts[-1][0], axis=-1, keepdims=True)
            if i:
                m = jnp.maximum(m, jnp.max(parts[0][0], axis=-1, keepdims=True))
            l = jnp.zeros((2 * tq, 1), F32)
            o = jnp.zeros((2 * tq, LANES), F32)
            for s, lo, hi in parts:
                p = jnp.exp(s - m)
                l = l + jnp.sum(p, axis=-1, keepdims=True)
                o = o + jnp.dot(p.astype(BF), vb[lo:hi], preferred_element_type=F32)
            o_acc = _fox_unstack(o / l, first)
            lse_acc = _fox_unstack(jnp.broadcast_to(m + jnp.log(l), (2 * tq, LANES)), first)
            o_ref[rows, :] = o_acc
            lse_ref[rows, :] = lse_acc
            o2 = o_acc * o_acc
            s0 = jnp.sum(jnp.where(first, o2, 0.0), axis=-1, keepdims=True)
            s1 = jnp.sum(jnp.where(first, 0.0, o2), axis=-1, keepdims=True)
            r = lax.rsqrt(jnp.where(first, s0, s1) * (1.0 / FOX_HEAD_DIM) + EPS)
            fn_ref[rows, :] = (o_acc * r * fnw_ref[...]).astype(BF)

    qkv = lambda k: pl.BlockSpec((t, LANES), lambda j: (0, COL_FOX // LANES + 3 * j + k))
    pair = pl.BlockSpec((t, LANES), lambda j: (0, j))
    return pl.pallas_call(
        body, name="fox_fwd", grid=(N_FOX_HEADS // 2,),
        in_specs=[qkv(0), qkv(1), qkv(2), pl.BlockSpec((LANES, t), lambda j: (0, 0)),
                  pl.BlockSpec((1, LANES), lambda j: (0, 0))],
        out_specs=[pair, pair, pair],
        out_shape=[_sds((t, D_FOX)), _sds((t, D_FOX)), _sds((t, D_FOX), BF)],
        compiler_params=_params("parallel"),
    )(proj, proj, proj, cumt, fnw)


def _fox_bwd(proj, cumt, lse, o, do, dproj):
    t = proj.shape[0]
    tq = min(TOKEN_BLOCK, t // 2)
    nq = t // tq

    def body(q_ref, k_ref, v_ref, cumt_ref, lse_ref, o_ref, do_ref, _, dqkv_ref, dcq_ref, dckt_ref, dk_s, dv_s):
        j = pl.program_id(0)

        @pl.when(j == 0)
        def _():
            dcq_ref[...] = jnp.zeros_like(dcq_ref)
            dckt_ref[...] = jnp.zeros_like(dckt_ref)

        lane = _iota((1, LANES), 1)

        first = _iota((1, LANES), 1) < FOX_HEAD_DIM
        kb = k_ref[...].astype(BF)
        vb = v_ref[...].astype(BF)
        dk_s[...] = jnp.zeros_like(dk_s)
        dv_s[...] = jnp.zeros_like(dv_s)
        for i in range(nq):
            rows = slice(i * tq, (i + 1) * tq)
            do_i = do_ref[rows, :]
            prod = do_i * o_ref[rows, :]
            lse_i = lse_ref[rows, :]
            q2 = _fox_stack(q_ref[rows, :] * FOX_SCALE, first)
            do2 = _fox_stack(do_i, first)
            delta = jnp.concatenate([jnp.sum(jnp.where(first, prod, 0.0), axis=-1, keepdims=True),
                                     jnp.sum(jnp.where(first, 0.0, prod), axis=-1, keepdims=True)], axis=0)
            lse2 = jnp.concatenate([lse_i[:, 0:1], lse_i[:, FOX_HEAD_DIM:FOX_HEAD_DIM + 1]], axis=0)
            dq2 = jnp.zeros((2 * tq, LANES), F32)
            dcq2 = jnp.zeros((2 * tq, 1), F32)
            for s, lo, hi in _fox_logits(q2, kb, cumt_ref, j, i, tq):
                p = jnp.exp(s - lse2)
                ds = p * (_mm_nt(do2, vb[lo:hi]) - delta)
                dsb = ds.astype(BF)
                dq2 = dq2 + jnp.dot(dsb, kb[lo:hi], preferred_element_type=F32)
                dk_s[lo:hi, :] += _mm_tn(dsb, q2)
                dv_s[lo:hi, :] += _mm_tn(p, do2)
                dcq2 = dcq2 + jnp.sum(ds, axis=-1, keepdims=True)
                dckt_ref[pl.ds(2 * j, 1), lo:hi] += jnp.sum(ds[:tq], axis=0, keepdims=True)
                dckt_ref[pl.ds(2 * j + 1, 1), lo:hi] += jnp.sum(ds[tq:], axis=0, keepdims=True)
            dqkv_ref[rows, 0:LANES] = (_fox_unstack(dq2, first) * FOX_SCALE).astype(BF)
            dcq_ref[rows, :] += jnp.where(lane == 2 * j, dcq2[:tq], jnp.where(lane == 2 * j + 1, dcq2[tq:], 0.0))
        dqkv_ref[:, LANES:2 * LANES] = dk_s[...].astype(BF)
        dqkv_ref[:, 2 * LANES:QKV] = dv_s[...].astype(BF)

    qkv = lambda k: pl.BlockSpec((t, LANES), lambda j: (0, COL_FOX // LANES + 3 * j + k))
    pair = pl.BlockSpec((t, LANES), lambda j: (0, j))
    rows128 = pl.BlockSpec((LANES, t), lambda j: (0, 0))
    return pl.pallas_call(
        body, name="fox_bwd", grid=(N_FOX_HEADS // 2,),
        in_specs=[qkv(0), qkv(1), qkv(2), rows128, pair, pair, pair, ANY_SPEC],
        out_specs=[pl.BlockSpec((t, QKV), lambda j: (0, COL_FOX // QKV + j)),
                   pl.BlockSpec((t, LANES), lambda j: (0, 0)), rows128],
        out_shape=[_sds(dproj.shape, BF), _sds((t, LANES)), _sds((LANES, t))],
        scratch_shapes=[pltpu.VMEM((t, LANES), F32), pltpu.VMEM((t, LANES), F32)],
        input_output_aliases={7: 0}, compiler_params=_params("arbitrary"),
    )(proj, proj, proj, cumt, lse, o, do, dproj)


def _conv(x, w, row):
    return (w[3:4, :] * x + w[2:3, :] * _shift_down(x, 1, row) + w[1:2, :] * _shift_down(x, 2, row)
            + w[0:1, :] * _shift_down(x, 3, row))


def _chunk_decay(gc_c):
    gi = gc_c[:, 0:CHUNK]
    gj = gc_c.T[0:CHUNK, :]
    ri = _iota((CHUNK, CHUNK), 0)
    cj = _iota((CHUNK, CHUNK), 1)
    return jnp.where(ri >= cj, jnp.exp(jnp.minimum(gi - gj, 0.0)), 0.0), ri > cj


def _gdn_specs(t):
    col = lambda off: pl.BlockSpec((t, LANES), lambda h: (0, off + h))
    cw = lambda off: pl.BlockSpec((CONV_K, LANES), lambda h: (0, off + h))
    mat = pl.BlockSpec((1, t // CHUNK, CHUNK, CHUNK), lambda h: (h, 0, 0, 0))
    qkv = lambda k: pl.BlockSpec((t, LANES), lambda h: (0, COL_GDN // LANES + 3 * h + k))
    return col, cw, mat, qkv


def _gdn_prep(proj, convw, beta, g):
    t = proj.shape[0]
    nch = t // CHUNK

    def body(xq_ref, xk_ref, xv_ref, wq_ref, wk_ref, wv_ref, beta_ref, g_ref,
             qn_ref, kn_ref, cv_ref, gc_ref, be_ref, m_ref, a_ref):
        row = _iota((t, LANES), 0)
        hd = pl.program_id(0)
        be_ref[...] = jnp.broadcast_to(_lane_column(beta_ref[...], SM_GB + hd), (t, LANES))

        def act(x_ref, w_ref):
            y = _conv(x_ref[...], w_ref[...], row)
            return y * _sigmoid(y)

        cq = act(xq_ref, wq_ref)
        ck = act(xk_ref, wk_ref)
        cv_ref[...] = act(xv_ref, wv_ref)
        qn_ref[...] = cq * lax.rsqrt(jnp.sum(cq * cq, axis=-1, keepdims=True) + EPS) * GDN_QSCALE
        kn_ref[...] = ck * lax.rsqrt(jnp.sum(ck * ck, axis=-1, keepdims=True) + EPS)
        gc = jnp.broadcast_to(_lane_column(g_ref[...], SM_GA + hd), (t, LANES))
        pos = row % CHUNK
        step = 1
        while step < CHUNK:
            gc = gc + jnp.where(pos >= step, pltpu.roll(gc, step, 0), 0.0)
            step *= 2
        gc_ref[...] = gc

        group = 4 if nch % 4 == 0 else 1

        def chunks(gi, carry):
            ns = [gi * group + c for c in range(group)]
            sls = [pl.ds(pl.multiple_of(n * CHUNK, CHUNK), CHUNK) for n in ns]
            ks = [kn_ref[sl, :] for sl in sls]
            kk = [_mm_nt(k_c * be_ref[sl, :], k_c) for k_c, sl in zip(ks, sls)]
            qk = [_mm_nt(qn_ref[sl, :], k_c) for k_c, sl in zip(ks, sls)]
            for c, n in enumerate(ns):
                decay, strict = _chunk_decay(gc_ref[sls[c], :])
                m_ref[0, n] = jnp.where(strict, kk[c] * decay, 0.0)
                a_ref[0, n] = qk[c] * decay
            return carry

        lax.fori_loop(0, nch // group, chunks, 0)

    col, cw, mat, qkv = _gdn_specs(t)
    return pl.pallas_call(
        body, name="gdn_prep", grid=(N_GDN_HEADS,),
        in_specs=[qkv(0), qkv(1), qkv(2), cw(0), cw(4), cw(8)] + [pl.BlockSpec((t, LANES), lambda h: (0, 0))] * 2,
        out_specs=[col(0), col(0), col(0), col(0), col(0), mat, mat],
        out_shape=[_sds((t, D_GDN))] * 5 + [_sds((N_GDN_HEADS, nch, CHUNK, CHUNK))] * 2,
        compiler_params=_params("parallel"),
    )(proj, proj, proj, convw, convw, convw, beta, g)


def _tri_inverse(m3):
    assert m3.shape == (LANES, CHUNK, CHUNK)

    def body(m_ref, t_ref, ms, ts):
        for i in range(CHUNK):
            ms[i * CHUNK:(i + 1) * CHUNK, :] = m_ref[:, i, :].T
        cidx = _iota((CHUNK, LANES), 0)

        def outer(i, carry):
            def inner(jj, acc):
                mrow = ms[pl.ds(i * CHUNK + jj, 1), :]
                return acc - mrow * ts[pl.ds(pl.multiple_of(jj * CHUNK, CHUNK), CHUNK), :]

            acc = lax.fori_loop(0, i, inner, jnp.where(cidx == i, 1.0, 0.0).astype(F32))
            ts[pl.ds(pl.multiple_of(i * CHUNK, CHUNK), CHUNK), :] = acc
            return carry

        lax.fori_loop(0, CHUNK, outer, 0)
        for i in range(CHUNK):
            t_ref[:, i, :] = ts[i * CHUNK:(i + 1) * CHUNK, :].T

    return pl.pallas_call(
        body, name="tri_inverse", in_specs=[VMEM_SPEC], out_specs=VMEM_SPEC,
        out_shape=_sds((LANES, CHUNK, CHUNK)),
        scratch_shapes=[pltpu.VMEM((CHUNK * CHUNK, LANES), F32), pltpu.VMEM((CHUNK * CHUNK, LANES), F32)],
        compiler_params=_params(),
    )(m3)


def _gdn_chunk_terms(q, k, v, b, gcc):
    eg = jnp.exp(gcc)
    last = gcc[CHUNK - 1:CHUNK, :]
    egl = jnp.exp(last - gcc)
    gl = jnp.exp(last)
    kb = k * b
    return eg, egl, gl, kb, v * b, kb * eg, q * eg, k * egl


GDN_BLOCK_CHUNKS = 4


def _gdn_block_specs(t, reverse):
    cb = GDN_BLOCK_CHUNKS
    nb = t // (cb * CHUNK)
    idx = (lambda i: nb - 1 - i) if reverse else (lambda i: i)
    tok = pl.BlockSpec((cb * CHUNK, D_GDN), lambda i: (idx(i), 0))
    mat = pl.BlockSpec((N_GDN_HEADS, cb, CHUNK, CHUNK), lambda i: (0, idx(i), 0, 0))
    state = pl.BlockSpec((N_GDN_HEADS, cb, GDN_HEAD_DIM, GDN_HEAD_DIM), lambda i: (0, idx(i), 0, 0))
    return nb, tok, mat, state


def _gdn_scan(qn, kn, cv, be, gc, tinv, amat):
    t = qn.shape[0]
    nch = t // CHUNK

    def body(q_ref, k_ref, v_ref, b_ref, gc_ref, t_ref, a_ref, o_ref, sall_ref, vn_ref, s_scr):
        @pl.when(pl.program_id(0) == 0)
        def _():
            s_scr[...] = jnp.zeros_like(s_scr)

        heads = range(N_GDN_HEADS)
        cols = [slice(hd * LANES, (hd + 1) * LANES) for hd in heads]
        s = [s_scr[hd] for hd in heads]
        for cc in range(GDN_BLOCK_CHUNKS):
            rs = slice(cc * CHUNK, (cc + 1) * CHUNK)
            terms = [_gdn_chunk_terms(q_ref[rs, cs], k_ref[rs, cs], v_ref[rs, cs], b_ref[rs, cs], gc_ref[rs, cs])
                     for cs in cols]
            for hd in heads:
                sall_ref[hd, cc] = s[hd]
            uw = [_mm(t_ref[hd, cc], jnp.concatenate([terms[hd][4], terms[hd][5]], axis=1)) for hd in heads]
            ws_qs = [_mm(jnp.concatenate([uw[hd][:, LANES:], terms[hd][6]], axis=0), s[hd]) for hd in heads]
            vn = [uw[hd][:, :LANES] - ws_qs[hd][:CHUNK] for hd in heads]
            a_vn = [_mm(a_ref[hd, cc], vn[hd]) for hd in heads]
            kd_vn = [_mm_tn(terms[hd][7], vn[hd]) for hd in heads]
            for hd in heads:
                vn_ref[rs, cols[hd]] = vn[hd]
                o_ref[rs, cols[hd]] = ws_qs[hd][CHUNK:] + a_vn[hd]
                s[hd] = s[hd] * terms[hd][2] + kd_vn[hd]
        for hd in heads:
            s_scr[hd] = s[hd]

    nb, tok, mat, state = _gdn_block_specs(t, False)
    return pl.pallas_call(
        body, name="gdn_scan", grid=(nb,),
        in_specs=[tok] * 5 + [mat, mat], out_specs=[tok, state, tok],
        out_shape=[_sds((t, D_GDN)), _sds((N_GDN_HEADS, nch, GDN_HEAD_DIM, GDN_HEAD_DIM)), _sds((t, D_GDN))],
        scratch_shapes=[pltpu.VMEM((N_GDN_HEADS, GDN_HEAD_DIM, GDN_HEAD_DIM), F32)],
        compiler_params=_params("arbitrary"),
    )(qn, kn, cv, be, gc, tinv, amat)


def _gdn_bwd(qn, kn, cv, be, gc, tinv, amat, s_all, vn_all, do):
    t = qn.shape[0]

    def body(q_ref, k_ref, v_ref, b_ref, gc_ref, t_ref, a_ref, sall_ref, vn_ref, do_ref,
             dq_ref, dk_ref, dv_ref, db_ref, dg_ref, ds_scr):
        @pl.when(pl.program_id(0) == 0)
        def _():
            ds_scr[...] = jnp.zeros_like(ds_scr)

        lastrow = _iota((CHUNK, LANES), 0) == CHUNK - 1
        heads = range(N_GDN_HEADS)
        cols = [slice(hd * LANES, (hd + 1) * LANES) for hd in heads]
        each = lambda fn: [fn(hd) for hd in heads]
        rows_cat = lambda x, y: jnp.concatenate([x, y], axis=0)
        lane_cat = lambda x, y: jnp.concatenate([x, y], axis=1)
        dsp = each(lambda hd: ds_scr[hd])
        for cc in reversed(range(GDN_BLOCK_CHUNKS)):
            rs = slice(cc * CHUNK, (cc + 1) * CHUNK)
            q = each(lambda hd: q_ref[rs, cols[hd]])
            k = each(lambda hd: k_ref[rs, cols[hd]])
            v = each(lambda hd: v_ref[rs, cols[hd]])
            b = each(lambda hd: b_ref[rs, cols[hd]])
            gcc = each(lambda hd: gc_ref[rs, cols[hd]])
            do_c = each(lambda hd: do_ref[rs, cols[hd]])
            vn = each(lambda hd: vn_ref[rs, cols[hd]])
            tn = each(lambda hd: t_ref[hd, cc])
            st = each(lambda hd: sall_ref[hd, cc])
            terms = each(lambda hd: _gdn_chunk_terms(q[hd], k[hd], v[hd], b[hd], gcc[hd]))
            eg, egl, gl, kb, vb, kbg, qd, kd = [[terms[hd][i] for hd in heads] for i in range(8)]
            w = each(lambda hd: _mm(tn[hd], kbg[hd]))
            a_do = each(lambda hd: _mm_tn(a_ref[hd, cc], do_c[hd]))
            kd_ds = each(lambda hd: _mm(kd[hd], dsp[hd]))
            da = each(lambda hd: _mm_nt(do_c[hd], vn[hd]))
            dkd = each(lambda hd: _mm_nt(vn[hd], dsp[hd]))
            by_k = each(lambda hd: _mm_nt(rows_cat(kb[hd], q[hd]), k[hd]))
            dgl = each(lambda hd: jnp.sum(jnp.sum(dsp[hd] * st[hd], axis=-1, keepdims=True), axis=0, keepdims=True))
            dvn = each(lambda hd: a_do[hd] + kd_ds[hd])
            do_dvn = each(lambda hd: rows_cat(do_c[hd], dvn[hd]))
            by_s = each(lambda hd: _mm_nt(do_dvn[hd], st[hd]))
            dqd = each(lambda hd: by_s[hd][:CHUNK])
            dvn_dw = each(lambda hd: lane_cat(dvn[hd], -by_s[hd][CHUNK:]))
            dsp = each(lambda hd: _mm_tn(rows_cat(qd[hd], -w[hd]), do_dvn[hd]) + gl[hd] * dsp[hd])
            dt = each(lambda hd: _mm_nt(dvn_dw[hd], lane_cat(vb[hd], kbg[hd])))
            by_t = each(lambda hd: _mm_tn(tn[hd], dvn_dw[hd]))
            tt_dt = each(lambda hd: _mm_tn(tn[hd], dt[hd]))
            dm_raw = each(lambda hd: _mm_nt(tt_dt[hd], tn[hd]))
            masks = each(lambda hd: _chunk_decay(gcc[hd]))
            dkk = each(lambda hd: jnp.where(masks[hd][1], -dm_raw[hd], 0.0) * masks[hd][0])
            dqk = each(lambda hd: da[hd] * masks[hd][0])
            dqk_dkk = each(lambda hd: rows_cat(dqk[hd], dkk[hd]))
            on_k = each(lambda hd: _mm(dqk_dkk[hd], k[hd]))
            dk_mm = each(lambda hd: _mm_tn(dqk_dkk[hd], rows_cat(q[hd], kb[hd])))
            for hd in heads:
                cs = cols[hd]
                dvb, dkbg = by_t[hd][:, :LANES], by_t[hd][:, LANES:]
                gmat = dkk[hd] * by_k[hd][:CHUNK] + dqk[hd] * by_k[hd][CHUNK:]
                dq_ref[rs, cs] = dqd[hd] * eg[hd] + on_k[hd][:CHUNK]
                dkb = on_k[hd][CHUNK:] + dkbg * eg[hd]
                dk_ref[rs, cs] = dkd[hd] * egl[hd] + dk_mm[hd] + dkb * b[hd]
                db = jnp.sum(dkb * k[hd], axis=-1, keepdims=True) + jnp.sum(dvb * v[hd], axis=-1, keepdims=True)
                db_ref[rs, cs] = jnp.broadcast_to(db, (CHUNK, LANES))
                dv_ref[rs, cs] = dvb * b[hd]
                dkd_kd = jnp.sum(dkd[hd] * kd[hd], axis=-1, keepdims=True)
                col_sums = jnp.sum(lane_cat(gmat, jnp.zeros_like(gmat)).T, axis=-1, keepdims=True)
                dgc = (jnp.sum(gmat, axis=-1, keepdims=True) - col_sums[:CHUNK]
                       + jnp.sum(dqd[hd] * qd[hd], axis=-1, keepdims=True)
                       + jnp.sum(dkbg * kbg[hd], axis=-1, keepdims=True) - dkd_kd)
                extra = jnp.sum(dkd_kd, axis=0, keepdims=True) + dgl[hd] * gl[hd]
                dg_ref[rs, cs] = dgc + jnp.where(lastrow, extra, 0.0)
        for hd in heads:
            ds_scr[hd] = dsp[hd]
        dg = dg_ref[...]
        row = _iota(dg.shape, 0)
        pos = row % CHUNK
        step = 1
        while step < CHUNK:
            dg = dg + jnp.where(pos < CHUNK - step, pltpu.roll(dg, dg.shape[0] - step, 0), 0.0)
            step *= 2
        dg_ref[...] = dg

    nb, tok, mat, state = _gdn_block_specs(t, True)
    return pl.pallas_call(
        body, name="gdn_bwd", grid=(nb,),
        in_specs=[tok] * 5 + [mat, mat, state, tok, tok], out_specs=[tok] * 5, out_shape=[_sds((t, D_GDN))] * 5,
        scratch_shapes=[pltpu.VMEM((N_GDN_HEADS, GDN_HEAD_DIM, GDN_HEAD_DIM), F32)],
        compiler_params=_params("arbitrary"),
    )(qn, kn, cv, be, gc, tinv, amat, s_all, vn_all, do)


def _gdn_bwd_conv(proj, convw, dqn, dkn, dcv, dproj):
    t = proj.shape[0]

    def body(xq_ref, xk_ref, xv_ref, wq_ref, wk_ref, wv_ref, dq_ref, dk_ref, dv_ref, _,
             dqkv_ref, dwq_ref, dwk_ref, dwv_ref):
        row = _iota((t, LANES), 0)

        def one(x_ref, w_ref, d_ref, k, dw_ref, scale):
            x = x_ref[...]
            w = w_ref[...]
            y = _conv(x, w, row)
            sg = _sigmoid(y)
            dc = d_ref[...]
            if scale is not None:
                c = y * sg
                r = lax.rsqrt(jnp.sum(c * c, axis=-1, keepdims=True) + EPS)
                ch = c * r
                dc = scale * r * (dc - ch * jnp.sum(dc * ch, axis=-1, keepdims=True))
            dy = dc * sg * (1.0 + y * (1.0 - sg))
            dqkv_ref[:, k * LANES:(k + 1) * LANES] = (
                w[3:4, :] * dy + w[2:3, :] * _shift_up(dy, 1, row) + w[1:2, :] * _shift_up(dy, 2, row)
                + w[0:1, :] * _shift_up(dy, 3, row)).astype(BF)
            for jj in range(CONV_K):
                xs = x if jj == CONV_K - 1 else _shift_down(x, CONV_K - 1 - jj, row)
                dw_ref[jj:jj + 1, :] = jnp.sum(dy * xs, axis=0, keepdims=True)

        one(xq_ref, wq_ref, dq_ref, 0, dwq_ref, GDN_QSCALE)
        one(xk_ref, wk_ref, dk_ref, 1, dwk_ref, 1.0)
        one(xv_ref, wv_ref, dv_ref, 2, dwv_ref, None)

    col, cw, _, qkv = _gdn_specs(t)
    return pl.pallas_call(
        body, name="gdn_bwd_conv", grid=(N_GDN_HEADS,),
        in_specs=[qkv(0), qkv(1), qkv(2), cw(0), cw(4), cw(8), col(0), col(0), col(0), ANY_SPEC],
        out_specs=[pl.BlockSpec((t, QKV), lambda h: (0, COL_GDN // QKV + h)), cw(0), cw(0), cw(0)],
        out_shape=[_sds(dproj.shape, BF)] + [_sds((CONV_K, D_GDN))] * 3,
        input_output_aliases={9: 0}, compiler_params=_params("parallel"),
    )(proj, proj, proj, convw, convw, convw, dqn, dkn, dcv, dproj)


def _mix_out(fox_n, gdn_o, proj, gnw, w_out, x, pmw, plw, after):
    t = x.shape[0]
    tm = min(MATMUL_BLOCK, t)

    def body(fn_ref, go_ref, gz_ref, gnw_ref, w_ref, x_ref, pmw_ref, plw_ref, x1_ref, h2_ref, mixed_ref, omix_ref,
             h2t_ref):
        omix_ref[:, 0:D_FOX] = fn_ref[...]
        for hd in range(N_GDN_HEADS):
            cs = slice(hd * LANES, (hd + 1) * LANES)
            go = go_ref[:, cs]
            r = lax.rsqrt(jnp.mean(go * go, axis=-1, keepdims=True) + EPS)
            gz = gz_ref[:, cs]
            omix_ref[:, D_FOX + hd * LANES:D_FOX + (hd + 1) * LANES] = (
                go * r * gnw_ref[...] * (gz * _sigmoid(gz))).astype(BF)
        mixed = jnp.dot(omix_ref[...], w_ref[...], preferred_element_type=F32)
        mixed_ref[...] = mixed
        r2 = lax.rsqrt(jnp.mean(mixed * mixed, axis=-1, keepdims=True) + EPS)
        x1 = x_ref[...] + mixed * r2 * pmw_ref[...]
        x1_ref[...] = x1
        r3 = lax.rsqrt(jnp.mean(x1 * x1, axis=-1, keepdims=True) + EPS)
        h2 = x1 * r3 * plw_ref[...]
        h2_ref[...] = h2.astype(BF)
        h2t_ref[...] = h2.T.astype(BF)

    tok = lambda w: pl.BlockSpec((tm, w), lambda i: (i, 0))
    vec = lambda w: pl.BlockSpec((1, w), lambda i: (0, 0))
    return pl.pallas_call(
        _ordered(body), name="mix_out", grid=(t // tm,),
        in_specs=[ANY_SPEC, tok(D_FOX), tok(D_GDN), pl.BlockSpec((tm, D_GDN), lambda i: (i, COL_GZ // D_GDN)), vec(LANES),
                  pl.BlockSpec((D_MODEL, D_MODEL), lambda i: (0, 0)), tok(D_MODEL), vec(D_MODEL), vec(D_MODEL)],
        out_specs=[tok(D_MODEL)] * 4 + [pl.BlockSpec((D_MODEL, tm), lambda i: (0, i))],
        out_shape=[_sds((t, D_MODEL)), _sds((t, D_MODEL), BF), _sds((t, D_MODEL)), _sds((t, D_MODEL), BF),
                   _sds((D_MODEL, t), BF)],
        compiler_params=_params("parallel"),
    )(after, fox_n, gdn_o, proj, gnw, w_out, x, pmw, plw)


def _out_bwd(dmixed, w_out, o_fox, gdn_o, proj, fnw, gnw, after):
    t = dmixed.shape[0]
    tm = min(MATMUL_BLOCK, t)

    def body(dm_ref, w_ref, of_ref, go_ref, gz_ref, fnw_ref, gnw_ref, dof_ref, dgo_ref, dgz_ref, dfw_ref, dgw_ref):
        i = pl.program_id(0)

        @pl.when(i == 0)
        def _():
            dfw_ref[...] = jnp.zeros_like(dfw_ref)
            dgw_ref[...] = jnp.zeros_like(dgw_ref)

        domix = _mm_nt(dm_ref[...], w_ref[...])
        first = _iota((1, LANES), 1) < FOX_HEAD_DIM
        dfw = jnp.zeros((1, LANES), F32)
        dgw = jnp.zeros((1, LANES), F32)
        for pr in range(N_FOX_HEADS // 2):
            cs = slice(pr * LANES, (pr + 1) * LANES)
            o = of_ref[:, cs]
            dfn = domix[:, cs]
            o2 = o * o
            s0 = jnp.sum(jnp.where(first, o2, 0.0), axis=-1, keepdims=True)
            s1 = jnp.sum(jnp.where(first, 0.0, o2), axis=-1, keepdims=True)
            r = lax.rsqrt(jnp.where(first, s0, s1) * (1.0 / FOX_HEAD_DIM) + EPS)
            oh = o * r
            dfw = dfw + jnp.sum(dfn * oh, axis=0, keepdims=True)
            doh = dfn * fnw_ref[...]
            pr_ = doh * oh
            m0 = jnp.sum(jnp.where(first, pr_, 0.0), axis=-1, keepdims=True)
            m1 = jnp.sum(jnp.where(first, 0.0, pr_), axis=-1, keepdims=True)
            dof_ref[:, cs] = r * (doh - oh * jnp.where(first, m0, m1) * (1.0 / FOX_HEAD_DIM))
        for hd in range(N_GDN_HEADS):
            cs = slice(hd * LANES, (hd + 1) * LANES)
            go = go_ref[:, cs]
            gz = gz_ref[:, cs]
            dgated = domix[:, D_FOX + hd * LANES:D_FOX + (hd + 1) * LANES]
            r = lax.rsqrt(jnp.mean(go * go, axis=-1, keepdims=True) + EPS)
            goh = go * r
            sg = _sigmoid(gz)
            sz = gz * sg
            gn = goh * gnw_ref[...]
            dgn = dgated * sz
            dgz_ref[:, cs] = (dgated * gn * sg * (1.0 + gz * (1.0 - sg))).astype(BF)
            dgw = dgw + jnp.sum(dgn * goh, axis=0, keepdims=True)
            dgh = dgn * gnw_ref[...]
            dgo_ref[:, cs] = r * (dgh - goh * jnp.mean(dgh * goh, axis=-1, keepdims=True))
        dfw_ref[...] += dfw + pltpu.roll(dfw, FOX_HEAD_DIM, 1)
        dgw_ref[...] += dgw

    tok = lambda w: pl.BlockSpec((tm, w), lambda i: (i, 0))
    vec = lambda w: pl.BlockSpec((1, w), lambda i: (0, 0))
    return pl.pallas_call(
        _ordered(body), name="out_bwd", grid=(t // tm,),
        in_specs=[ANY_SPEC, tok(D_MODEL), pl.BlockSpec((D_MODEL, D_MODEL), lambda i: (0, 0)), tok(D_FOX), tok(D_GDN),
                  pl.BlockSpec((tm, D_GDN), lambda i: (i, COL_GZ // D_GDN)), vec(LANES), vec(LANES)],
        out_specs=[tok(D_FOX), tok(D_GDN), pl.BlockSpec((tm, D_GDN), lambda i: (i, COL_GZ // D_GDN)), vec(LANES),
                   vec(LANES)],
        out_shape=[_sds((t, D_FOX)), _sds((t, D_GDN)), _sds((t, PROJ_W), BF), _sds((1, LANES)), _sds((1, LANES))],
        compiler_params=_params("arbitrary"),
    )(after, dmixed, w_out, o_fox, gdn_o, proj, fnw, gnw)


def _mlp_up(h2, w_upt):
    t = h2.shape[0]
    tm = min(MATMUL_BLOCK, t)

    def body(h_ref, w_ref, up_ref):
        up_ref[...] = lax.dot_general(h_ref[...], w_ref[...], (((1,), (1,)), ((), ())),
                                      preferred_element_type=F32).astype(BF)

    return pl.pallas_call(
        body, name="mlp_up", grid=(t // tm,),
        in_specs=[pl.BlockSpec((tm, D_MODEL), lambda i: (i, 0)), pl.BlockSpec((D_FF, D_MODEL), lambda i: (0, 0))],
        out_specs=pl.BlockSpec((tm, D_FF), lambda i: (i, 0)), out_shape=_sds((t, D_FF), BF),
        compiler_params=_params("parallel"),
    )(h2, w_upt)


def _mlp_down_loss(up, w_down, x1, pw, target):
    t = up.shape[0]
    tm = min(MATMUL_BLOCK, t)

    def body(up_ref, w_ref, x1_ref, pw_ref, tg_ref, dy_ref, dx2_ref, loss_ref, dpw_ref):
        i = pl.program_id(0)

        @pl.when(i == 0)
        def _():
            loss_ref[...] = jnp.zeros_like(loss_ref)
            dpw_ref[...] = jnp.zeros_like(dpw_ref)

        u = jnp.maximum(up_ref[...].astype(F32), 0.0)
        y = jnp.dot((u * u).astype(BF), w_ref[...], preferred_element_type=F32)
        r = lax.rsqrt(jnp.mean(y * y, axis=-1, keepdims=True) + EPS)
        yh = y * r
        pw = pw_ref[...]
        err = x1_ref[...] + yh * pw - tg_ref[...]
        part = jnp.sum(jnp.sum(err * err, axis=-1, keepdims=True), axis=0, keepdims=True) * (0.5 / D_MODEL)
        loss_ref[...] += jnp.broadcast_to(part, loss_ref.shape)
        dx2 = err * (1.0 / D_MODEL)
        dx2_ref[...] = dx2
        dpw_ref[...] += jnp.sum(dx2 * yh, axis=0, keepdims=True)
        dyh = dx2 * pw
        dy_ref[...] = (r * (dyh - yh * jnp.mean(dyh * yh, axis=-1, keepdims=True))).astype(BF)

    tok = lambda w: pl.BlockSpec((tm, w), lambda i: (i, 0))
    vec = lambda w: pl.BlockSpec((1, w), lambda i: (0, 0))
    return pl.pallas_call(
        body, name="mlp_down_loss", grid=(t // tm,),
        in_specs=[tok(D_FF), pl.BlockSpec((D_FF, D_MODEL), lambda i: (0, 0)), tok(D_MODEL), vec(D_MODEL), tok(D_MODEL)],
        out_specs=[tok(D_MODEL), tok(D_MODEL), vec(LANES), vec(D_MODEL)],
        out_shape=[_sds((t, D_MODEL), BF), _sds((t, D_MODEL)), _sds((1, LANES)), _sds((1, D_MODEL))],
        compiler_params=_params("arbitrary"),
    )(up, w_down, x1, pw, target)


def _mlp_bwd_act(dy, w_down, up):
    t = dy.shape[0]
    tm = min(MATMUL_BLOCK, t)

    def body(dy_ref, w_ref, up_ref, dup_ref):
        da = lax.dot_general(dy_ref[...], w_ref[...], (((1,), (1,)), ((), ())), preferred_element_type=F32)
        dup_ref[...] = (da * (2.0 * jnp.maximum(up_ref[...].astype(F32), 0.0))).astype(BF)

    return pl.pallas_call(
        body, name="mlp_bwd_act", grid=(t // tm,),
        in_specs=[pl.BlockSpec((tm, D_MODEL), lambda i: (i, 0)), pl.BlockSpec((D_FF, D_MODEL), lambda i: (0, 0)),
                  pl.BlockSpec((tm, D_FF), lambda i: (i, 0))],
        out_specs=pl.BlockSpec((tm, D_FF), lambda i: (i, 0)), out_shape=_sds((t, D_FF), BF),
        compiler_params=_params("parallel"),
    )(dy, w_down, up)


def _mlp_bwd_in(dup, w_up, x1, plw, dx2, mixed, pmw, after):
    t = dup.shape[0]
    tm = min(MATMUL_BLOCK, t)

    def body(dup_ref, w_ref, x1_ref, plw_ref, dx2_ref, mx_ref, pmw_ref, dx1_ref, dmixed_ref, dplw_ref, dpmw_ref):
        i = pl.program_id(0)

        @pl.when(i == 0)
        def _():
            dplw_ref[...] = jnp.zeros_like(dplw_ref)
            dpmw_ref[...] = jnp.zeros_like(dpmw_ref)

        dh = jnp.dot(dup_ref[...], w_ref[...], preferred_element_type=F32)
        x1 = x1_ref[...]
        r = lax.rsqrt(jnp.mean(x1 * x1, axis=-1, keepdims=True) + EPS)
        xh = x1 * r
        dplw_ref[...] += jnp.sum(dh * xh, axis=0, keepdims=True)
        dxh = dh * plw_ref[...]
        dx1 = dx2_ref[...] + r * (dxh - xh * jnp.mean(dxh * xh, axis=-1, keepdims=True))
        dx1_ref[...] = dx1
        mx = mx_ref[...]
        r2 = lax.rsqrt(jnp.mean(mx * mx, axis=-1, keepdims=True) + EPS)
        mh = mx * r2
        dpmw_ref[...] += jnp.sum(dx1 * mh, axis=0, keepdims=True)
        dmh = dx1 * pmw_ref[...]
        dmixed_ref[...] = (r2 * (dmh - mh * jnp.mean(dmh * mh, axis=-1, keepdims=True))).astype(BF)

    tok = lambda w: pl.BlockSpec((tm, w), lambda i: (i, 0))
    vec = lambda w: pl.BlockSpec((1, w), lambda i: (0, 0))
    return pl.pallas_call(
        _ordered(body), name="mlp_bwd_in", grid=(t // tm,),
        in_specs=[ANY_SPEC, tok(D_FF), pl.BlockSpec((D_FF, D_MODEL), lambda i: (0, 0)), tok(D_MODEL),
                  vec(D_MODEL), tok(D_MODEL), tok(D_MODEL), vec(D_MODEL)],
        out_specs=[tok(D_MODEL), tok(D_MODEL), vec(D_MODEL), vec(D_MODEL)],
        out_shape=[_sds((t, D_MODEL)), _sds((t, D_MODEL), BF), _sds((1, D_MODEL)), _sds((1, D_MODEL))],
        compiler_params=_params("arbitrary"),
    )(after, dup, w_up, x1, plw, dx2, mixed, pmw)


def _wgrad(a, b, a_cols, split=1, a_fn=None, a_block0=0, name="wgrad"):
    t, b_cols = b.shape
    n_a = (a.shape[1] - a_block0 * a_cols) // a_cols if a_block0 else a.shape[1] // a_cols

    def body(a_ref, b_ref, o_ref):
        av = a_ref[...]
        if a_fn is not None:
            av = a_fn(av)
        o_ref[...] = _mm_tn(av, b_ref[...]).astype(BF).reshape(o_ref.shape)

    return pl.pallas_call(
        body, name=name, grid=(n_a,),
        in_specs=[pl.BlockSpec((t, a_cols), lambda i: (0, i + a_block0)), pl.BlockSpec((t, b_cols), lambda i: (0, 0))],
        out_specs=pl.BlockSpec((split, a_cols // split, b_cols), lambda i: (i, 0, 0)),
        out_shape=_sds((n_a * split, a_cols // split, b_cols), BF),
        compiler_params=_params("parallel"),
    )(a, b)


def _wgrad_pre_t(at, b, b_cols, name):
    rows, t = at.shape
    n_b = b.shape[1] // b_cols

    def body(a_ref, b_ref, o_ref):
        o_ref[0] = jnp.dot(a_ref[...], b_ref[...], preferred_element_type=F32).astype(BF)

    return pl.pallas_call(
        body, name=name, grid=(n_b,),
        in_specs=[pl.BlockSpec((rows, t), lambda j: (0, 0)), pl.BlockSpec((t, b_cols), lambda j: (0, j))],
        out_specs=pl.BlockSpec((1, rows, b_cols), lambda j: (j, 0, 0)), out_shape=_sds((n_b, rows, b_cols), BF),
        compiler_params=_params("parallel"),
    )(at, b)


def _small_bwd(proj, fb, al, dtb, dcq, dckt, dbe, dge, dproj):
    t = proj.shape[0]

    def body(sm_ref, fb_ref, al_ref, dtb_ref, dcq_ref, dckt_ref, dbe_ref, dge_ref, _, dsm_ref, dvec_ref):
        s = sm_ref[...]
        lane = _iota((1, LANES), 1)
        dcum = dcq_ref[...] - dckt_ref[...].T
        row = _iota((t, LANES), 0)
        step = 1
        while step < t:
            dcum = dcum + _shift_up(dcum, step, row)
            step *= 2
        dff = dcum * _sigmoid(-(s + fb_ref[...]))
        dbeta = jnp.zeros((t, LANES), F32)
        dg = jnp.zeros((t, LANES), F32)
        for hd in range(N_GDN_HEADS):
            dbeta = jnp.where(lane == SM_GB + hd, dbe_ref[:, hd * LANES:hd * LANES + 1], dbeta)
            dg = jnp.where(lane == SM_GA + hd, dge_ref[:, hd * LANES:hd * LANES + 1], dg)
        beta = _sigmoid(s)
        dgb = dbeta * beta * (1.0 - beta)
        za = s + dtb_ref[...]
        nea = -jnp.exp(al_ref[...])
        dga = dg * nea * _sigmoid(za)
        is_f = lane < SM_GB
        is_b = (lane >= SM_GB) & (lane < SM_GA)
        is_a = (lane >= SM_GA) & (lane < SM_GA + 4)
        dsm_ref[...] = jnp.where(is_f, dff, jnp.where(is_b, dgb, jnp.where(is_a, dga, 0.0))).astype(BF)
        dvec_ref[...] = jnp.zeros_like(dvec_ref)
        dvec_ref[0:1, :] = jnp.sum(jnp.where(is_f, dff, 0.0), axis=0, keepdims=True)
        dvec_ref[1:2, :] = jnp.sum(jnp.where(is_a, dg * nea * _softplus(za), 0.0), axis=0, keepdims=True)
        dvec_ref[2:3, :] = jnp.sum(jnp.where(is_a, dga, 0.0), axis=0, keepdims=True)

    vec = pl.BlockSpec((1, LANES), lambda i: (0, 0))
    full = lambda r, c: pl.BlockSpec((r, c), lambda i: (0, 0))
    small = pl.BlockSpec((t, LANES), lambda i: (0, COL_SMALL // LANES))
    return pl.pallas_call(
        body, name="small_bwd", grid=(1,),
        in_specs=[small, vec, vec, vec, full(t, LANES), full(LANES, t), full(t, 512), full(t, 512), ANY_SPEC],
        out_specs=[small, full(8, LANES)], out_shape=[_sds(dproj.shape, BF), _sds((8, LANES))],
        input_output_aliases={8: 0}, compiler_params=_params("arbitrary"),
    )(proj, fb, al, dtb, dcq, dckt, dbe, dge, dproj)


def _in_bwd(dproj, wt_al, x, nw, dx1, after):
    t = x.shape[0]
    tm = min(MATMUL_BLOCK, t)

    def body(dp_ref, w_ref, x_ref, nw_ref, dx1_ref, dx_ref, dnw_ref):
        i = pl.program_id(0)

        @pl.when(i == 0)
        def _():
            dnw_ref[...] = jnp.zeros_like(dnw_ref)

        dh = jnp.dot(dp_ref[...], w_ref[...], preferred_element_type=F32)
        xv = x_ref[...]
        r = lax.rsqrt(jnp.mean(xv * xv, axis=-1, keepdims=True) + EPS)
        xh = xv * r
        dnw_ref[...] += jnp.sum(dh * xh, axis=0, keepdims=True)
        dxh = dh * nw_ref[...]
        dx_ref[...] = dx1_ref[...] + r * (dxh - xh * jnp.mean(dxh * xh, axis=-1, keepdims=True))

    tok = lambda w: pl.BlockSpec((tm, w), lambda i: (i, 0))
    vec = lambda w: pl.BlockSpec((1, w), lambda i: (0, 0))
    return pl.pallas_call(
        _ordered(body), name="in_bwd", grid=(t // tm,),
        in_specs=[ANY_SPEC, tok(PROJ_W), pl.BlockSpec((PROJ_W, D_MODEL), lambda i: (0, 0)), tok(D_MODEL), vec(D_MODEL),
                  tok(D_MODEL)],
        out_specs=[tok(D_MODEL), vec(D_MODEL)], out_shape=[_sds((t, D_MODEL)), _sds((1, D_MODEL))],
        compiler_params=_params("arbitrary"),
    )(after, dproj, wt_al, x, nw, dx1)


def _row(v, width=None):
    v = v.reshape(1, -1).astype(F32)
    if width is not None and v.shape[1] < width:
        v = jnp.pad(v, ((0, 0), (0, width - v.shape[1])))
    return v


def _lane_vec(v, first):
    return jnp.pad(v.astype(F32), (first, LANES - first - v.shape[0])).reshape(1, LANES)


def _local_step(x, target, wt_al, started, late_weights, on_grads, convw, pre_mix_norm, fox_f_bias, fox_out_norm,
                gdn_a_log, gdn_dt_bias, gdn_out_norm, post_mix_norm, pre_mlp_norm, post_mlp_norm):
    t = x.shape[0]
    nch = t // CHUNK
    nw, pmw, plw, pw = _row(pre_mix_norm), _row(post_mix_norm), _row(pre_mlp_norm), _row(post_mlp_norm)
    fb, al, dtb = _lane_vec(fox_f_bias, SM_FF), _lane_vec(gdn_a_log, SM_GA), _lane_vec(gdn_dt_bias, SM_GA)
    fnw = _row(jnp.tile(fox_out_norm, 2))
    gnw = _row(gdn_out_norm)

    proj, h = _norm_proj(x, nw, wt_al, started)
    cumt, beta, g = _small_prep(proj, fb, al, dtb)
    o_fox, lse, fox_n = _fox_fwd(proj, cumt, fnw)
    qn, kn, cv, gc, be, mmat, amat = _gdn_prep(proj, convw, beta, g)
    n_prob = N_GDN_HEADS * nch
    m3 = mmat.reshape(n_prob, CHUNK, CHUNK)
    if n_prob < LANES:
        m3 = jnp.pad(m3, ((0, LANES - n_prob), (0, 0), (0, 0)))
    tinv = _tri_inverse(m3)[:n_prob].reshape(N_GDN_HEADS, nch, CHUNK, CHUNK)
    token = late_weights("mlp_relay", tinv)
    gdn_o, s_all, vn_all = _gdn_scan(qn, kn, cv, be, gc, tinv, amat)
    w_out = late_weights("w_out", gdn_o)
    x1, h2, mixed, omix, h2t = _mix_out(fox_n, gdn_o, proj, gnw, w_out, x, pmw, plw, token)
    w_up, w_down = late_weights("mlp", h2)
    up = _mlp_up(h2, w_up)
    dy, dx2, loss, d_pw = _mlp_down_loss(up, w_down, x1, pw, target)

    dup = _mlp_bwd_act(dy, w_down, up)
    relu2 = lambda u: jnp.square(jnp.maximum(u.astype(F32), 0.0))
    g_down = _wgrad(up, dy, D_FF // N_DEV, a_fn=relu2, name="wgrad_down")
    g_up = _wgrad_pre_t(h2t, dup, D_FF // N_DEV, name="wgrad_up")
    token = on_grads("mlp", (g_up, g_down))
    dx1, dmixed, d_plw, d_pmw = _mlp_bwd_in(dup, w_up, x1, plw, dx2, mixed, pmw, token)
    token = on_grads("w_out", _wgrad(omix, dmixed, 512, split=4, name="wgrad_out"))
    do_fox, dgo, dproj, d_fnw, d_gnw = _out_bwd(dmixed, w_out, o_fox, gdn_o, proj, fnw, gnw, token)
    dproj, dcq, dckt = _fox_bwd(proj, cumt, lse, o_fox, do_fox, dproj)
    dqn, dkn, dcv, dbe, dge = _gdn_bwd(qn, kn, cv, be, gc, tinv, amat, s_all, vn_all, dgo)
    dproj, dwq, dwk, dwv = _gdn_bwd_conv(proj, convw, dqn, dkn, dcv, dproj)
    dproj, dvec = _small_bwd(proj, fb, al, dtb, dcq, dckt, dbe, dge, dproj)
    g_main = _wgrad(dproj, h, WGRAD_IN_ROWS, name="wgrad_in")
    g_tail = _wgrad(dproj, h, LANES, a_block0=COL_SMALL // LANES, name="wgrad_in_small")
    token = on_grads("w_in", (g_main, g_tail))
    grad_x, d_nw = _in_bwd(dproj, wt_al, x, nw, dx1, token)
    small = dict(norms=(d_nw, d_pmw, d_plw, d_pw), fox_out_norm=d_fnw, gdn_out_norm=d_gnw, loss=loss, vectors=dvec,
                 conv=(dwq, dwk, dwv))
    return grad_x, small


MESH_IDS = pl.DeviceIdType.MESH
CHIP_FLIPS = ((0, 0), (1, 0), (0, 1), (1, 1))


def _place():
    return lax.axis_index("x"), lax.axis_index("y"), lax.axis_index("c")


def _all_gather(blocks, later, dtype):
    n, k = len(blocks), len(later)

    def body(*refs):
        ins, shards, outs = refs[:n], refs[n:n + k], refs[n + k:2 * n + k]
        zones, to_send = refs[2 * n + k:2 * n + 2 * k], refs[2 * n + 2 * k:2 * n + 3 * k]
        stage_in, stage_out = refs[2 * n + 3 * k:2 * n + 4 * k], refs[2 * n + 4 * k:2 * n + 5 * k]
        send_sems, recv_sems, local_sems, late_sems = refs[2 * n + 5 * k:]
        x, y, c = _place()
        sibling = (x, y, 1 - c)
        chips = [(x ^ fx, y ^ fy) for fx, fy in CHIP_FLIPS[1:]]

        def slot(out, px, py, pc):
            return out.at[4 * px + 2 * py + pc]

        def copy(a, k, block, to, src=None):
            return pltpu.make_async_remote_copy(
                src_ref=slot(outs[a], *block) if src is None else src, dst_ref=slot(outs[a], *block),
                send_sem=send_sems.at[a, k], recv_sem=recv_sems.at[a, k], device_id=to, device_id_type=MESH_IDS)

        pending = []
        for a in range(n):
            mine = pltpu.make_async_copy(ins[a], slot(outs[a], x, y, c), local_sems.at[a])
            mine.start()
            pending.append(mine)
        sends = []
        for a in range(n):
            first = [copy(a, 0, (x, y, c), sibling, src=ins[a])]
            first += [copy(a, 1 + j, (x, y, c), (*chip, c), src=ins[a]) for j, chip in enumerate(chips)]
            for cp in first:
                cp.start()
            sends += first
        loads = [pltpu.make_async_copy(shards[a], stage_in[a], late_sems.at[a, 0]) for a in range(k)]
        for cp in loads:
            cp.start()
        for a, (_, transposed) in enumerate(later):
            loads[a].wait()
            val = stage_in[a][...]
            stage_out[a][...] = (val.T if transposed else val).astype(dtype)
            for j, dst in enumerate((slot(zones[a], x, y, c), to_send[a])):
                cp = pltpu.make_async_copy(stage_out[a], dst, late_sems.at[a, 1 + j])
                cp.start()
                pending.append(cp)
        for a in range(n):
            for j, chip in enumerate(chips):
                copy(a, 1 + j, (*chip, c), (x, y, c)).wait_recv()
                fwd = copy(a, 4 + j, (*chip, c), sibling)
                fwd.start()
                sends.append(fwd)
        for a in range(n):
            copy(a, 0, sibling, (x, y, c)).wait_recv()
            for j, chip in enumerate(chips):
                copy(a, 4 + j, (*chip, 1 - c), (x, y, c)).wait_recv()
        for cp in sends:
            cp.wait_send()
        for cp in pending:
            cp.wait()

    shapes = [s_.shape[::-1] if transposed else s_.shape for s_, transposed in later]
    out = pl.pallas_call(
        body, name="all_gather_weights", in_specs=[ANY_SPEC] * (n + k), out_specs=[ANY_SPEC] * (n + 2 * k),
        out_shape=[_sds((N_DEV,) + b.shape, b.dtype) for b in blocks] + [_sds((N_DEV,) + sh, dtype) for sh in shapes]
        + [_sds(sh, dtype) for sh in shapes],
        scratch_shapes=[pltpu.VMEM(s_.shape, s_.dtype) for s_, _ in later] + [pltpu.VMEM(sh, dtype) for sh in shapes]
        + [pltpu.SemaphoreType.DMA((n, 7)), pltpu.SemaphoreType.DMA((n, 7)), pltpu.SemaphoreType.DMA((n,)),
           pltpu.SemaphoreType.DMA((k, 3))],
        compiler_params=pltpu.CompilerParams(vmem_limit_bytes=VMEM_LIMIT, has_side_effects=True),
    )(*blocks, *[s_ for s_, _ in later])
    return out[:n], out[n:n + k], out[n + k:]


def _adamw(w, g, m, v):
    m = ADAM_B1 * m + (1.0 - ADAM_B1) * g
    v = ADAM_B2 * v + (1.0 - ADAM_B2) * (g * g)
    m_hat = m / (1.0 - ADAM_B1 ** ADAM_STEP)
    v_hat = v / (1.0 - ADAM_B2 ** ADAM_STEP)
    return -ADAM_LR * (m_hat / (jnp.sqrt(v_hat) + ADAM_EPS) + ADAM_WD * w), m, v


def _pair_reduce(g, name):
    _, r, c_ = g.shape
    n = len(CHIP_FLIPS)

    def body(g_ref, out_ref, sib_buf, send_sems, recv_sems):
        x, y, c = _place()
        chips = [(x ^ fx, y ^ fy) for fx, fy in CHIP_FLIPS]
        piece = lambda chip, core: g_ref.at[4 * chip[0] + 2 * chip[1] + core]
        copies = [pltpu.make_async_remote_copy(
            src_ref=piece(chip, 1 - c), dst_ref=sib_buf.at[j], send_sem=send_sems.at[j], recv_sem=recv_sems.at[j],
            device_id=(x, y, 1 - c), device_id_type=MESH_IDS) for j, chip in enumerate(chips)]
        for cp in copies:
            cp.start()
        for j, chip in enumerate(chips):
            copies[j].wait_recv()
            out_ref[j] = (piece(chip, c)[...].astype(F32) + sib_buf[j].astype(F32)).astype(BF)
        for cp in copies:
            cp.wait_send()

    return pl.pallas_call(
        body, name=name, in_specs=[VMEM_SPEC], out_specs=VMEM_SPEC, out_shape=_sds((n, r, c_), BF),
        scratch_shapes=[pltpu.VMEM((n, r, c_), BF), pltpu.SemaphoreType.DMA((n,)), pltpu.SemaphoreType.DMA((n,))],
        compiler_params=pltpu.CompilerParams(vmem_limit_bytes=VMEM_LIMIT, has_side_effects=True),
    )(g)


HBM_SPEC = pl.BlockSpec(memory_space=pltpu.HBM)
SEM_SPEC = pl.BlockSpec(memory_space=pltpu.SEMAPHORE)
DATAFLOW = pltpu.SideEffectType.DATAFLOW_SIDE_EFFECTING


def _peers():
    x, y, c = _place()
    return 4 * x + 2 * y + c, [(x ^ (k >> 2), y ^ ((k >> 1) & 1), c ^ (k & 1)) for k in range(1, N_DEV)]


def _peer_index(peer):
    return 4 * peer[0] + 2 * peer[1] + peer[2]


def _exchange_start(srcs, zones, pieces, name, chips=False):
    n = len(srcs)
    fresh = zones is None
    if fresh:
        slots = len(CHIP_FLIPS) if chips else N_DEV
        zones = [_sds((slots,) + (v.shape[1:] if pieces else v.shape), v.dtype) for v in srcs]
    n_in = n if fresh else 2 * n
    among_chips = list(chips) if isinstance(chips, (list, tuple)) else [chips] * n

    def body(*refs):
        ins, sems, token = refs[:n], refs[n_in:n_in + 2 * n], refs[-1]
        zs = refs[n_in + 3 * n:n_in + 4 * n] if fresh else refs[n:2 * n]
        me, peers = _peers()
        x, y, c = _place()
        for a in range(n):
            if among_chips[a] and pieces:
                routes = [((x ^ fx, y ^ fy, c), j, j) for j, (fx, fy) in enumerate(CHIP_FLIPS) if j]
            elif among_chips[a]:
                routes = [((x ^ fx, y ^ fy, c), None, me) for fx, fy in CHIP_FLIPS[1:]]
            else:
                routes = [(peer, _peer_index(peer) if pieces else None, me) for peer in peers]
            for peer, src_slot, dst_slot in routes:
                pltpu.make_async_remote_copy(
                    src_ref=ins[a] if src_slot is None else ins[a].at[src_slot], dst_ref=zs[a].at[dst_slot],
                    send_sem=sems[2 * a], recv_sem=sems[2 * a + 1], device_id=peer, device_id_type=MESH_IDS).start()
        token[...] = jnp.zeros_like(token)

    hbm = lambda v: pltpu.with_memory_space_constraint(v, pltpu.HBM)
    out = pl.pallas_call(
        body, name=name,
        out_shape=tuple([pltpu.SemaphoreType.DMA(())] * (2 * n) + [pltpu.HBM(v.shape, v.dtype) for v in srcs]
                        + [pltpu.HBM(z.shape, z.dtype) for z in zones] + [_sds((8, LANES))]),
        in_specs=[HBM_SPEC] * n_in, out_specs=tuple([SEM_SPEC] * (2 * n) + [HBM_SPEC] * (2 * n) + [VMEM_SPEC]),
        input_output_aliases={i: 2 * n + i for i in range(n_in)},
        compiler_params=pltpu.CompilerParams(has_side_effects=DATAFLOW),
    )(*[hbm(v) for v in srcs], *([] if fresh else [hbm(z) for z in zones]))
    return out[:2 * n], out[2 * n:3 * n], out[3 * n:4 * n], out[-1]


def _relay_start(zones, name):
    n = len(zones)

    def body(*refs):
        zs, sems, token = refs[:n], refs[n:3 * n], refs[-1]
        x, y, c = _place()
        for fx, fy in CHIP_FLIPS:
            slot = 4 * (x ^ fx) + 2 * (y ^ fy) + c
            for a in range(n):
                pltpu.make_async_remote_copy(
                    src_ref=zs[a].at[slot], dst_ref=zs[a].at[slot], send_sem=sems[2 * a], recv_sem=sems[2 * a + 1],
                    device_id=(x, y, 1 - c), device_id_type=MESH_IDS).start()
        token[...] = jnp.zeros_like(token)

    out = pl.pallas_call(
        body, name=name,
        out_shape=tuple([pltpu.SemaphoreType.DMA(())] * (2 * n) + [pltpu.HBM(z.shape, z.dtype) for z in zones]
                        + [_sds((8, LANES))]),
        in_specs=[HBM_SPEC] * n, out_specs=tuple([SEM_SPEC] * (2 * n) + [HBM_SPEC] * n + [VMEM_SPEC]),
        input_output_aliases={i: 2 * n + i for i in range(n)},
        compiler_params=pltpu.CompilerParams(has_side_effects=DATAFLOW),
    )(*[pltpu.with_memory_space_constraint(z, pltpu.HBM) for z in zones])
    return out[:2 * n], [], out[2 * n:3 * n], out[-1]


def _exchange_wait(sems, srcs, zones, after, name, chips=False, n_copies=None):
    n, n_src = len(zones), len(srcs)
    after = list(after) if isinstance(after, (list, tuple)) else [after]
    n_copies = n_copies or (len(CHIP_FLIPS) - 1 if chips else N_DEV - 1)

    def body(*refs):
        zs, sm = refs[n_src:n_src + n], refs[n_src + n:n_src + 3 * n]
        me, peers = _peers()
        for a in range(n):
            seven = zs[a].at[pl.ds(0, n_copies)]
            cp = pltpu.make_async_remote_copy(src_ref=seven, dst_ref=seven, send_sem=sm[2 * a], recv_sem=sm[2 * a + 1],
                                              device_id=peers[0], device_id_type=MESH_IDS)
            cp.wait_send()
            cp.wait_recv()

    out = pl.pallas_call(
        body, name=name, out_shape=tuple([pltpu.HBM(v.shape, v.dtype) for v in srcs] + [pltpu.HBM(z.shape, z.dtype) for z in zones]),
        in_specs=[HBM_SPEC] * (n_src + n) + [SEM_SPEC] * (2 * n) + [ANY_SPEC] * len(after),
        out_specs=tuple([HBM_SPEC] * (n_src + n)), input_output_aliases={i: i for i in range(n_src + n)},
        compiler_params=pltpu.CompilerParams(has_side_effects=DATAFLOW),
    )(*srcs, *zones, *sems, *after)
    return out[:n_src], out[n_src:]


def _sum_adamw(zone, own, w, m, v, name, chips=False):
    n_slots, r, c_ = zone.shape
    rb = next((b for b in (256, 128) if r % b == 0), r)

    def body(me_ref, z_ref, own_ref, w_ref, m_ref, v_ref, grad_ref, delta_ref, nm_ref, nv_ref):
        total = None
        for d in range(n_slots):
            part = jnp.where(me_ref[0] == d, own_ref[0], z_ref[d]).astype(F32)
            total = part if total is None else total + part
        grad_ref[...] = total
        delta_ref[...], nm_ref[...], nv_ref[...] = _adamw(w_ref[...], total, m_ref[...], v_ref[...])

    x, y, c = _place()
    mine = 0 * x if chips else 4 * x + 2 * y + c
    blk = pl.BlockSpec((rb, c_), lambda i, me_ref: (i, 0))
    return pl.pallas_call(
        body, name=name,
        grid_spec=pltpu.PrefetchScalarGridSpec(
            num_scalar_prefetch=1, grid=(r // rb,),
            in_specs=[pl.BlockSpec((n_slots, rb, c_), lambda i, me_ref: (0, i, 0)),
                      pl.BlockSpec((1, rb, c_), lambda i, me_ref: (me_ref[0], i, 0)), blk, blk, blk],
            out_specs=[blk] * 4),
        out_shape=[_sds((r, c_))] * 4, compiler_params=_params("parallel"),
    )(mine.astype(jnp.int32).reshape(1), zone, own, w, m, v)


SMALL_NORMS = ("pre_mix_norm", "post_mix_norm", "pre_mlp_norm", "post_mlp_norm")
SMALL_ORDER = SMALL_NORMS + ("fox_out_norm", "gdn_out_norm", "fox_f_bias", "gdn_a_log", "gdn_dt_bias", "gdn_conv_w")
CONV_SLAB_ROWS, CONV_SLAB_LANES = 8, 256


def _small_pack(small):
    def body(n0, n1, n2, n3, fnw_ref, gnw_ref, loss_ref, vec_ref, out_ref):
        out_ref[...] = jnp.zeros_like(out_ref)
        for i, ref in enumerate((n0, n1, n2, n3)):
            out_ref[i:i + 1, :] = ref[...]
        out_ref[4:5, 0:LANES] = fnw_ref[...]
        out_ref[4:5, LANES:2 * LANES] = gnw_ref[...]
        out_ref[4:5, 2 * LANES:3 * LANES] = loss_ref[...]
        out_ref[5:8, 0:LANES] = vec_ref[0:3, :]

    return pl.pallas_call(body, name="small_pack", in_specs=[VMEM_SPEC] * 8, out_specs=VMEM_SPEC,
                          out_shape=_sds((8, D_MODEL)))(*small["norms"], small["fox_out_norm"], small["gdn_out_norm"],
                                                        small["loss"], small["vectors"])


def _conv_slabs(dconv):
    blocks = dconv.reshape(CONV_K, N_DEV, -1).transpose(1, 0, 2)
    blocks = jnp.pad(blocks, ((0, 0), (0, CONV_SLAB_ROWS - CONV_K), (0, CONV_SLAB_LANES - blocks.shape[2])))
    return blocks.reshape(N_DEV * CONV_SLAB_ROWS, CONV_SLAB_LANES)


def _small_update(zone, conv_zone, own, own_conv, w, m, v):
    n = len(SMALL_ORDER)
    n_conv = w["gdn_conv_w"].shape[1]

    def body(me_ref, z_ref, zc_ref, own_ref, ownc_ref, *refs):
        params, loss_ref, outs, (tot, totc) = refs[:3 * n], refs[3 * n], refs[3 * n + 1:7 * n + 1], refs[-2:]
        total, total_c = None, None
        for d in range(N_DEV):
            part = jnp.where(me_ref[0] == d, own_ref[...], z_ref[d])
            part_c = jnp.where(me_ref[0] == d, ownc_ref[...], zc_ref[d])
            total, total_c = (part, part_c) if d == 0 else (total + part, total_c + part_c)
        tot[...] = total
        totc[...] = total_c
        loss_ref[...] = tot[4, 2 * LANES:2 * LANES + 1]
        mine = totc[pl.ds(pl.multiple_of(me_ref[0] * CONV_SLAB_ROWS, CONV_SLAB_ROWS), CONV_SLAB_ROWS), :]
        g = dict(zip(SMALL_NORMS, (tot[0], tot[1], tot[2], tot[3])))
        g.update(fox_out_norm=tot[4, 0:FOX_HEAD_DIM], gdn_out_norm=tot[4, LANES:LANES + GDN_HEAD_DIM],
                 fox_f_bias=tot[5, SM_FF:SM_FF + N_FOX_HEADS], gdn_a_log=tot[6, SM_GA:SM_GA + N_GDN_HEADS],
                 gdn_dt_bias=tot[7, SM_GA:SM_GA + N_GDN_HEADS], gdn_conv_w=mine[0:CONV_K, 0:n_conv])
        for i, name in enumerate(SMALL_ORDER):
            w_ref, m_ref, v_ref = params[3 * i:3 * i + 3]
            outs[4 * i][...] = g[name]
            outs[4 * i + 1][...], outs[4 * i + 2][...], outs[4 * i + 3][...] = _adamw(w_ref[...], g[name], m_ref[...],
                                                                                     v_ref[...])

    x, y, c = _place()
    operands = [a[name] for name in SMALL_ORDER for a in (w, m, v)]
    out = pl.pallas_call(
        body, name="small_update",
        in_specs=[pl.BlockSpec(memory_space=pltpu.SMEM)] + [VMEM_SPEC] * (4 + 3 * n), out_specs=[VMEM_SPEC] * (1 + 4 * n),
        out_shape=[_sds((1,))] + [_sds(w[name].shape) for name in SMALL_ORDER for _ in range(4)],
        scratch_shapes=[pltpu.VMEM(zone.shape[1:], F32), pltpu.VMEM(conv_zone.shape[1:], F32)],
    )((4 * x + 2 * y + c).astype(jnp.int32).reshape(1), zone, conv_zone, own, own_conv, *operands)
    return out[0][0], {name: out[1 + 4 * i:5 + 4 * i] for i, name in enumerate(SMALL_ORDER)}


def _native_rows():
    groups = []
    for first, n_groups in ((0, N_FOX_HEADS // 2), (D_FOX * 3 + N_FOX_HEADS, N_GDN_HEADS)):
        for g in range(n_groups):
            groups += [(first + part * n_groups * LANES + g * LANES, first + part * n_groups * LANES + (g + 1) * LANES)
                       for part in range(3)]
    return tuple(groups) + ((3088, 3600), (1536, 1544), (3080, 3088))


NATIVE_ROWS = _native_rows()


W_IN_PIECE = D_PROJ // N_DEV
WGRAD_IN_ROWS = 512
SHUFFLE_LANES = 256


def _to_aligned_moves():
    moves, o = [], 0
    for lo, hi in NATIVE_ROWS:
        r = lo
        while r < hi:
            d = r // W_IN_PIECE
            k = min(hi, (d + 1) * W_IN_PIECE) - r
            moves.append((0, d, r - d * W_IN_PIECE, 0, o, k))
            r, o = r + k, o + k
    return moves


def _from_aligned_moves():
    moves = []
    for _, d, a, _, o, k in _to_aligned_moves():
        while k:
            n = min(k, WGRAD_IN_ROWS - o % WGRAD_IN_ROWS) if o < COL_SMALL else k
            moves.append((0, o // WGRAD_IN_ROWS, o % WGRAD_IN_ROWS, d, a, n) if o < COL_SMALL else
                         (1, 0, o - COL_SMALL, d, a, n))
            o, a, k = o + n, a + n, k - n
    return moves


def _shuffle_rows(srcs, moves, out_shape, name):
    c = srcs[0].shape[-1]

    def body(*refs):
        s_refs, o_ref, s_f, o_f = refs[:len(srcs)], refs[len(srcs)], refs[len(srcs) + 1:-1], refs[-1]
        for s_ref, f in zip(s_refs, s_f):
            f[...] = s_ref[...].astype(F32)
        o_f[...] = jnp.zeros_like(o_f)
        for i, ss, so, ds, do, k in moves:
            o_f[ds, pl.ds(do, k), :] = s_f[i][ss, pl.ds(so, k), :]
        o_ref[...] = o_f[...].astype(BF)

    blk = lambda shape: pl.BlockSpec(tuple(shape[:-1]) + (SHUFFLE_LANES,), lambda j: (0, 0, j))
    scratch = lambda shape: pltpu.VMEM(tuple(shape[:-1]) + (SHUFFLE_LANES,), F32)
    return pl.pallas_call(
        body, name=name, grid=(c // SHUFFLE_LANES,), in_specs=[blk(s.shape) for s in srcs], out_specs=blk(out_shape),
        out_shape=_sds(out_shape, BF), scratch_shapes=[scratch(s.shape) for s in srcs] + [scratch(out_shape)],
        compiler_params=_params("parallel"),
    )(*srcs)


def _cols_from_pieces(p):
    return p.transpose(1, 0, 2).reshape(p.shape[1], -1)


WEIGHT_ORDER = ("pre_mix_norm", "w_in", "fox_f_bias", "fox_out_norm", "gdn_conv_w", "gdn_a_log", "gdn_dt_bias",
                "gdn_out_norm", "w_out", "post_mix_norm", "pre_mlp_norm", "w_up", "w_down", "post_mlp_norm")


def kernel(x, pre_mix_norm, w_in, fox_f_bias, fox_out_norm, gdn_conv_w, gdn_a_log, gdn_dt_bias, gdn_out_norm, w_out, post_mix_norm, pre_mlp_norm, w_up, w_down, post_mlp_norm, loss_target, m_pre_mix_norm, m_w_in, m_fox_f_bias, m_fox_out_norm, m_gdn_conv_w, m_gdn_a_log, m_gdn_dt_bias, m_gdn_out_norm, m_w_out, m_post_mix_norm, m_pre_mlp_norm, m_w_up, m_w_down, m_post_mlp_norm, v_pre_mix_norm, v_w_in, v_fox_f_bias, v_fox_out_norm, v_gdn_conv_w, v_gdn_a_log, v_gdn_dt_bias, v_gdn_out_norm, v_w_out, v_post_mix_norm, v_pre_mlp_norm, v_w_up, v_w_down, v_post_mlp_norm):
    w = dict(pre_mix_norm=pre_mix_norm, w_in=w_in, fox_f_bias=fox_f_bias, fox_out_norm=fox_out_norm,
             gdn_conv_w=gdn_conv_w, gdn_a_log=gdn_a_log, gdn_dt_bias=gdn_dt_bias, gdn_out_norm=gdn_out_norm, w_out=w_out,
             post_mix_norm=post_mix_norm, pre_mlp_norm=pre_mlp_norm, w_up=w_up, w_down=w_down, post_mlp_norm=post_mlp_norm)
    mom = dict(pre_mix_norm=m_pre_mix_norm, w_in=m_w_in, fox_f_bias=m_fox_f_bias, fox_out_norm=m_fox_out_norm,
               gdn_conv_w=m_gdn_conv_w, gdn_a_log=m_gdn_a_log, gdn_dt_bias=m_gdn_dt_bias, gdn_out_norm=m_gdn_out_norm,
               w_out=m_w_out, post_mix_norm=m_post_mix_norm, pre_mlp_norm=m_pre_mlp_norm, w_up=m_w_up, w_down=m_w_down,
               post_mlp_norm=m_post_mlp_norm)
    var = dict(pre_mix_norm=v_pre_mix_norm, w_in=v_w_in, fox_f_bias=v_fox_f_bias, fox_out_norm=v_fox_out_norm,
               gdn_conv_w=v_gdn_conv_w, gdn_a_log=v_gdn_a_log, gdn_dt_bias=v_gdn_dt_bias, gdn_out_norm=v_gdn_out_norm,
               w_out=v_w_out, post_mix_norm=v_post_mix_norm, pre_mlp_norm=v_pre_mlp_norm, w_up=v_w_up, w_down=v_w_down,
               post_mlp_norm=v_post_mlp_norm)

    (win_g, conv_g), zones, shards = _all_gather([w_in.T.astype(BF), gdn_conv_w],
                                                 [(w_out, False), (w_up, True), (w_down, False)], BF)
    wt_al = _shuffle_rows([win_g], _to_aligned_moves(), (1, PROJ_W, D_MODEL), "w_in_to_aligned")[0]
    convw = _cols_from_pieces(conv_g)
    sems, shards, zones, after = _exchange_start(shards, zones, False, "gather_start", chips=[False, True, True])
    gathers = dict(w_out=(sems[:2], shards[:1], zones[:1], after), mlp=(sems[2:], shards[1:], zones[1:], after))

    def late_weights(name, after):
        if name == "mlp_relay":
            sems, shards, zones, _ = gathers["mlp"]
            _, zones = _exchange_wait(sems, shards, zones, after, "gather_mlp_wait", chips=True)
            gathers["mlp"] = _relay_start(zones, "gather_mlp_relay")
            return gathers["mlp"][3]
        sems, shards, zones, _ = gathers[name]
        _, got = _exchange_wait(sems, shards, zones, after, "gather_" + name + "_done",
                                n_copies=len(CHIP_FLIPS) if name == "mlp" else None)
        if name == "w_out":
            return got[0].reshape(D_MODEL, D_MODEL)
        return got[0].reshape(D_FF, D_MODEL), got[1].reshape(D_FF, D_MODEL)

    scatters = {}

    def on_grads(name, g):
        chips = name == "w_in"
        if name == "w_in":
            g = _shuffle_rows(list(g), _from_aligned_moves(), (N_DEV, W_IN_PIECE, D_MODEL), "w_in_grad_from_aligned")
            g = _pair_reduce(g, "pair_reduce_w_in")
        srcs = list(g) if name == "mlp" else [g]
        scatters[name] = _exchange_start(srcs, None, True, "scatter_" + name + "_start", chips=chips)
        return scatters[name][3]

    grad_x, small = _local_step(
        x[0], loss_target[0], wt_al, after, late_weights, on_grads, convw, pre_mix_norm,
        fox_f_bias, fox_out_norm, gdn_a_log, gdn_dt_bias, gdn_out_norm, post_mix_norm, pre_mlp_norm, post_mlp_norm)
    slabs = [_small_pack(small), _conv_slabs(jnp.concatenate(small["conv"], axis=1))]
    scatters["small"] = _exchange_start(slabs, None, False, "small_start")

    grads, delta, new_m, new_v = {}, {}, {}, {}
    after = scatters["small"][3]
    for name, members in (("mlp", ("w_up", "w_down")), ("w_out", ("w_out",)), ("small", ()), ("w_in", ("w_in",))):
        sems, srcs, zones, _ = scatters[name]
        srcs, zones = _exchange_wait(sems, srcs, zones, after, "scatter_" + name + "_wait", chips=name == "w_in")
        if name == "small":
            loss, updated = _small_update(*zones, *srcs, w, mom, var)
            for n, res in updated.items():
                grads[n], delta[n], new_m[n], new_v[n] = res
            after = grads["pre_mix_norm"]
        for n, zone, own in zip(members, zones, srcs):
            if n == "w_in":
                res = _sum_adamw(zone, own, w[n].T, mom[n].T, var[n].T, "adamw_" + n, chips=True)
                grads[n], delta[n], new_m[n], new_v[n] = [r.T for r in res]
            else:
                grads[n], delta[n], new_m[n], new_v[n] = _sum_adamw(zone, own, w[n], mom[n], var[n], "adamw_" + n)
        if members:
            after = [grads[n] for n in members]

    return (loss, grad_x[None], *[grads[n] for n in WEIGHT_ORDER], *[delta[n] for n in WEIGHT_ORDER],
            *[new_m[n] for n in WEIGHT_ORDER], *[new_v[n] for n in WEIGHT_ORDER])
```

```python
import jax
import jax.numpy as jnp
from jax import lax
from jax.experimental import pallas as pl
from jax.experimental.pallas import tpu as pltpu

F32 = jnp.float32
BF = jnp.bfloat16

D_MODEL = 1024
N_FOX_HEADS, FOX_HEAD_DIM = 8, 64
N_GDN_HEADS, GDN_HEAD_DIM = 4, 128
D_FOX = N_FOX_HEADS * FOX_HEAD_DIM
D_GDN = N_GDN_HEADS * GDN_HEAD_DIM
CHUNK = 64
CONV_K = 4
D_FF = 4 * D_MODEL
EPS = 1e-6
D_PROJ = 3600
N_DEV = 8

PROJ_W = 3712
COL_FOX, COL_GDN, COL_GZ, COL_SMALL = 0, 1536, 3072, 3584
LANES = 128
QKV = 3 * LANES
SM_FF, SM_GB, SM_GA = 0, 8, 12

ADAM_LR, ADAM_B1, ADAM_B2, ADAM_EPS, ADAM_WD, ADAM_STEP = 0.001, 0.9, 0.999, 1e-08, 0.01, 10

TOKEN_BLOCK = 256
MATMUL_BLOCK = 512
FOX_SCALE = FOX_HEAD_DIM ** -0.5
GDN_QSCALE = GDN_HEAD_DIM ** -0.5
NEG_BIG = -1e30
VMEM_LIMIT = 56 * 1024 * 1024

VMEM_SPEC = pl.BlockSpec(memory_space=pltpu.VMEM)
ANY_SPEC = pl.BlockSpec(memory_space=pl.ANY)


def _sds(shape, dtype=F32):
    return jax.ShapeDtypeStruct(shape, dtype)


def _params(*sem):
    return pltpu.CompilerParams(dimension_semantics=sem if sem else None, vmem_limit_bytes=VMEM_LIMIT)


def _ordered(body):
    def ordered(_, *refs):
        body(*refs)

    return ordered


def _mm(a, b):
    return jnp.dot(a.astype(BF), b.astype(BF), preferred_element_type=F32)


def _mm_nt(a, b):
    return lax.dot_general(a.astype(BF), b.astype(BF), (((1,), (1,)), ((), ())), preferred_element_type=F32)


def _mm_tn(a, b):
    return lax.dot_general(a.astype(BF), b.astype(BF), (((0,), (0,)), ((), ())), preferred_element_type=F32)


def _sigmoid(x):
    return 1.0 / (1.0 + jnp.exp(-x))


def _softplus(x):
    return jnp.maximum(x, 0.0) + jnp.log1p(jnp.exp(-jnp.abs(x)))


def _iota(shape, dim):
    return lax.broadcasted_iota(jnp.int32, shape, dim)


def _shift_down(x, s, row):
    return jnp.where(row >= s, pltpu.roll(x, s, 0), 0.0)


def _shift_up(x, s, row):
    n = x.shape[0]
    return jnp.where(row < n - s, pltpu.roll(x, n - s, 0), 0.0)


def _norm_proj(x, nw, wt_al, after):
    t = x.shape[0]

    def body(x_ref, nw_ref, w_ref, proj_ref, h_ref):
        xv = x_ref[...]
        r = lax.rsqrt(jnp.mean(xv * xv, axis=-1, keepdims=True) + EPS)
        h = (xv * r * nw_ref[...]).astype(BF)
        h_ref[...] = h
        proj_ref[...] = lax.dot_general(h, w_ref[...], (((1,), (1,)), ((), ())), preferred_element_type=F32)

    tm = min(MATMUL_BLOCK, t)
    return pl.pallas_call(
        _ordered(body), name="norm_proj", grid=(t // tm,),
        in_specs=[ANY_SPEC, pl.BlockSpec((tm, D_MODEL), lambda i: (i, 0)), pl.BlockSpec((1, D_MODEL), lambda i: (0, 0)),
                  pl.BlockSpec((PROJ_W, D_MODEL), lambda i: (0, 0))],
        out_specs=[pl.BlockSpec((tm, PROJ_W), lambda i: (i, 0)), pl.BlockSpec((tm, D_MODEL), lambda i: (i, 0))],
        out_shape=[_sds((t, PROJ_W)), _sds((t, D_MODEL), BF)],
        compiler_params=_params("parallel"),
    )(after, x, nw, wt_al)


def _lane_column(x, lane):
    return jnp.sum(jnp.where(_iota((1, LANES), 1) == lane, x, 0.0), axis=-1, keepdims=True)


def _small_prep(proj, fb, al, dtb):
    t = proj.shape[0]

    def body(sm_ref, fb_ref, al_ref, dtb_ref, cumt_ref, beta_ref, g_ref):
        s = sm_ref[...]
        z = s + fb_ref[...]
        cum = jnp.minimum(z, 0.0) - jnp.log1p(jnp.exp(-jnp.abs(z)))
        row = _iota((t, LANES), 0)
        step = 1
        while step < t:
            cum = cum + _shift_down(cum, step, row)
            step *= 2
        cumt_ref[...] = cum.T
        beta_ref[...] = _sigmoid(s)
        g_ref[...] = -jnp.exp(al_ref[...]) * _softplus(s + dtb_ref[...])

    vec = pl.BlockSpec((1, LANES), lambda i: (0, 0))
    tok = pl.BlockSpec((t, LANES), lambda i: (0, 0))
    return pl.pallas_call(
        body, name="small_prep", grid=(1,),
        in_specs=[pl.BlockSpec((t, LANES), lambda i: (0, COL_SMALL // LANES)), vec, vec, vec],
        out_specs=[pl.BlockSpec((LANES, t), lambda i: (0, 0)), tok, tok],
        out_shape=[_sds((LANES, t)), _sds((t, LANES)), _sds((t, LANES))],
        compiler_params=_params("arbitrary"),
    )(proj, fb, al, dtb)


def _fox_stack(x, first):
    return jnp.concatenate([jnp.where(first, x, 0.0), jnp.where(first, 0.0, x)], axis=0).astype(BF)


def _fox_unstack(y, first):
    n = y.shape[0] // 2
    return jnp.where(first, y[:n], y[n:])


def _fox_logits(q2_i, kb, cumt_ref, pair, i, tq):
    klen = (i + 1) * tq
    s = lax.dot_general(q2_i, kb[:klen], (((1,), (1,)), ((), ())), preferred_element_type=F32)
    upper = _iota((2 * tq, 1), 0) < tq
    s = s - jnp.where(upper, cumt_ref[pl.ds(2 * pair, 1), 0:klen], cumt_ref[pl.ds(2 * pair + 1, 1), 0:klen])
    causal = _iota((2 * tq, tq), 1) <= _iota((2 * tq, tq), 0) % tq
    parts = [(s[:, :klen - tq], 0, klen - tq)] if i else []
    return parts + [(jnp.where(causal, s[:, klen - tq:], NEG_BIG), klen - tq, klen)]


def _fox_fwd(proj, cumt, fnw):
    t = proj.shape[0]
    tq = min(TOKEN_BLOCK, t // 2)
    nq = t // tq

    def body(q_ref, k_ref, v_ref, cumt_ref, fnw_ref, o_ref, lse_ref, fn_ref):
        j = pl.program_id(0)
        first = _iota((1, LANES), 1) < FOX_HEAD_DIM
        kb = k_ref[...].astype(BF)
        vb = v_ref[...].astype(BF)
        for i in range(nq):
            rows = slice(i * tq, (i + 1) * tq)
            q2 = _fox_stack(q_ref[rows, :] * FOX_SCALE, first)
            parts = _fox_logits(q2, kb, cumt_ref, j, i, tq)
            m = jnp.max(parts[-1][0], axis=-1, keepdims=True)
            if i:
                m = jnp.maximum(m, jnp.max(parts[0][0], axis=-1, keepdims=True))
            l = jnp.zeros((2 * tq, 1), F32)
            o = jnp.zeros((2 * tq, LANES), F32)
            for s, lo, hi in parts:
                p = jnp.exp(s - m)
                l = l + jnp.sum(p, axis=-1, keepdims=True)
                o = o + jnp.dot(p.astype(BF), vb[lo:hi], preferred_element_type=F32)
            o_acc = _fox_unstack(o / l, first)
            lse_acc = _fox_unstack(jnp.broadcast_to(m + jnp.log(l), (2 * tq, LANES)), first)
            o_ref[rows, :] = o_acc
            lse_ref[rows, :] = lse_acc
            o2 = o_acc * o_acc
            s0 = jnp.sum(jnp.where(first, o2, 0.0), axis=-1, keepdims=True)
            s1 = jnp.sum(jnp.where(first, 0.0, o2), axis=-1, keepdims=True)
            r = lax.rsqrt(jnp.where(first, s0, s1) * (1.0 / FOX_HEAD_DIM) + EPS)
            fn_ref[rows, :] = (o_acc * r * fnw_ref[...]).astype(BF)

    qkv = lambda k: pl.BlockSpec((t, LANES), lambda j: (0, COL_FOX // LANES + 3 * j + k))
    pair = pl.BlockSpec((t, LANES), lambda j: (0, j))
    return pl.pallas_call(
        body, name="fox_fwd", grid=(N_FOX_HEADS // 2,),
        in_specs=[qkv(0), qkv(1), qkv(2), pl.BlockSpec((LANES, t), lambda j: (0, 0)),
                  pl.BlockSpec((1, LANES), lambda j: (0, 0))],
        out_specs=[pair, pair, pair],
        out_shape=[_sds((t, D_FOX)), _sds((t, D_FOX)), _sds((t, D_FOX), BF)],
        compiler_params=_params("parallel"),
    )(proj, proj, proj, cumt, fnw)


def _fox_bwd(proj, cumt, lse, o, do, dproj):
    t = proj.shape[0]
    tq = min(TOKEN_BLOCK, t // 2)
    nq = t // tq

    def body(q_ref, k_ref, v_ref, cumt_ref, lse_ref, o_ref, do_ref, _, dqkv_ref, dcq_ref, dckt_ref, dk_s, dv_s):
        j = pl.program_id(0)

        @pl.when(j == 0)
        def _():
            dcq_ref[...] = jnp.zeros_like(dcq_ref)
            dckt_ref[...] = jnp.zeros_like(dckt_ref)

        lane = _iota((1, LANES), 1)

        first = _iota((1, LANES), 1) < FOX_HEAD_DIM
        kb = k_ref[...].astype(BF)
        vb = v_ref[...].astype(BF)
        dk_s[...] = jnp.zeros_like(dk_s)
        dv_s[...] = jnp.zeros_like(dv_s)
        for i in range(nq):
            rows = slice(i * tq, (i + 1) * tq)
            do_i = do_ref[rows, :]
            prod = do_i * o_ref[rows, :]
            lse_i = lse_ref[rows, :]
            q2 = _fox_stack(q_ref[rows, :] * FOX_SCALE, first)
            do2 = _fox_stack(do_i, first)
            delta = jnp.concatenate([jnp.sum(jnp.where(first, prod, 0.0), axis=-1, keepdims=True),
                                     jnp.sum(jnp.where(first, 0.0, prod), axis=-1, keepdims=True)], axis=0)
            lse2 = jnp.concatenate([lse_i[:, 0:1], lse_i[:, FOX_HEAD_DIM:FOX_HEAD_DIM + 1]], axis=0)
            dq2 = jnp.zeros((2 * tq, LANES), F32)
            dcq2 = jnp.zeros((2 * tq, 1), F32)
            for s, lo, hi in _fox_logits(q2, kb, cumt_ref, j, i, tq):
                p = jnp.exp(s - lse2)
                ds = p * (_mm_nt(do2, vb[lo:hi]) - delta)
                dsb = ds.astype(BF)
                dq2 = dq2 + jnp.dot(dsb, kb[lo:hi], preferred_element_type=F32)
                dk_s[lo:hi, :] += _mm_tn(dsb, q2)
                dv_s[lo:hi, :] += _mm_tn(p, do2)
                dcq2 = dcq2 + jnp.sum(ds, axis=-1, keepdims=True)
                dckt_ref[pl.ds(2 * j, 1), lo:hi] += jnp.sum(ds[:tq], axis=0, keepdims=True)
                dckt_ref[pl.ds(2 * j + 1, 1), lo:hi] += jnp.sum(ds[tq:], axis=0, keepdims=True)
            dqkv_ref[rows, 0:LANES] = (_fox_unstack(dq2, first) * FOX_SCALE).astype(BF)
            dcq_ref[rows, :] += jnp.where(lane == 2 * j, dcq2[:tq], jnp.where(lane == 2 * j + 1, dcq2[tq:], 0.0))
        dqkv_ref[:, LANES:2 * LANES] = dk_s[...].astype(BF)
        dqkv_ref[:, 2 * LANES:QKV] = dv_s[...].astype(BF)

    qkv = lambda k: pl.BlockSpec((t, LANES), lambda j: (0, COL_FOX // LANES + 3 * j + k))
    pair = pl.BlockSpec((t, LANES), lambda j: (0, j))
    rows128 = pl.BlockSpec((LANES, t), lambda j: (0, 0))
    return pl.pallas_call(
        body, name="fox_bwd", grid=(N_FOX_HEADS // 2,),
        in_specs=[qkv(0), qkv(1), qkv(2), rows128, pair, pair, pair, ANY_SPEC],
        out_specs=[pl.BlockSpec((t, QKV), lambda j: (0, COL_FOX // QKV + j)),
                   pl.BlockSpec((t, LANES), lambda j: (0, 0)), rows128],
        out_shape=[_sds(dproj.shape, BF), _sds((t, LANES)), _sds((LANES, t))],
        scratch_shapes=[pltpu.VMEM((t, LANES), F32), pltpu.VMEM((t, LANES), F32)],
        input_output_aliases={7: 0}, compiler_params=_params("arbitrary"),
    )(proj, proj, proj, cumt, lse, o, do, dproj)


def _conv(x, w, row):
    return (w[3:4, :] * x + w[2:3, :] * _shift_down(x, 1, row) + w[1:2, :] * _shift_down(x, 2, row)
            + w[0:1, :] * _shift_down(x, 3, row))


def _chunk_decay(gc_c):
    gi = gc_c[:, 0:CHUNK]
    gj = gc_c.T[0:CHUNK, :]
    ri = _iota((CHUNK, CHUNK), 0)
    cj = _iota((CHUNK, CHUNK), 1)
    return jnp.where(ri >= cj, jnp.exp(jnp.minimum(gi - gj, 0.0)), 0.0), ri > cj


def _gdn_specs(t):
    col = lambda off: pl.BlockSpec((t, LANES), lambda h: (0, off + h))
    cw = lambda off: pl.BlockSpec((CONV_K, LANES), lambda h: (0, off + h))
    mat = pl.BlockSpec((1, t // CHUNK, CHUNK, CHUNK), lambda h: (h, 0, 0, 0))
    qkv = lambda k: pl.BlockSpec((t, LANES), lambda h: (0, COL_GDN // LANES + 3 * h + k))
    return col, cw, mat, qkv


def _gdn_prep(proj, convw, beta, g):
    t = proj.shape[0]
    nch = t // CHUNK

    def body(xq_ref, xk_ref, xv_ref, wq_ref, wk_ref, wv_ref, beta_ref, g_ref,
             qn_ref, kn_ref, cv_ref, gc_ref, be_ref, m_ref, a_ref):
        row = _iota((t, LANES), 0)
        hd = pl.program_id(0)
        be_ref[...] = jnp.broadcast_to(_lane_column(beta_ref[...], SM_GB + hd), (t, LANES))

        def act(x_ref, w_ref):
            y = _conv(x_ref[...], w_ref[...], row)
            return y * _sigmoid(y)

        cq = act(xq_ref, wq_ref)
        ck = act(xk_ref, wk_ref)
        cv_ref[...] = act(xv_ref, wv_ref)
        qn_ref[...] = cq * lax.rsqrt(jnp.sum(cq * cq, axis=-1, keepdims=True) + EPS) * GDN_QSCALE
        kn_ref[...] = ck * lax.rsqrt(jnp.sum(ck * ck, axis=-1, keepdims=True) + EPS)
        gc = jnp.broadcast_to(_lane_column(g_ref[...], SM_GA + hd), (t, LANES))
        pos = row % CHUNK
        step = 1
        while step < CHUNK:
            gc = gc + jnp.where(pos >= step, pltpu.roll(gc, step, 0), 0.0)
            step *= 2
        gc_ref[...] = gc

        group = 4 if nch % 4 == 0 else 1

        def chunks(gi, carry):
            ns = [gi * group + c for c in range(group)]
            sls = [pl.ds(pl.multiple_of(n * CHUNK, CHUNK), CHUNK) for n in ns]
            ks = [kn_ref[sl, :] for sl in sls]
            kk = [_mm_nt(k_c * be_ref[sl, :], k_c) for k_c, sl in zip(ks, sls)]
            qk = [_mm_nt(qn_ref[sl, :], k_c) for k_c, sl in zip(ks, sls)]
            for c, n in enumerate(ns):
                decay, strict = _chunk_decay(gc_ref[sls[c], :])
                m_ref[0, n] = jnp.where(strict, kk[c] * decay, 0.0)
                a_ref[0, n] = qk[c] * decay
            return carry

        lax.fori_loop(0, nch // group, chunks, 0)

    col, cw, mat, qkv = _gdn_specs(t)
    return pl.pallas_call(
        body, name="gdn_prep", grid=(N_GDN_HEADS,),
        in_specs=[qkv(0), qkv(1), qkv(2), cw(0), cw(4), cw(8)] + [pl.BlockSpec((t, LANES), lambda h: (0, 0))] * 2,
        out_specs=[col(0), col(0), col(0), col(0), col(0), mat, mat],
        out_shape=[_sds((t, D_GDN))] * 5 + [_sds((N_GDN_HEADS, nch, CHUNK, CHUNK))] * 2,
        compiler_params=_params("parallel"),
    )(proj, proj, proj, convw, convw, convw, beta, g)


def _tri_inverse(m3):
    assert m3.shape == (LANES, CHUNK, CHUNK)

    def body(m_ref, t_ref, ms, ts):
        for i in range(CHUNK):
            ms[i * CHUNK:(i + 1) * CHUNK, :] = m_ref[:, i, :].T
        cidx = _iota((CHUNK, LANES), 0)

        def outer(i, carry):
            def inner(jj, acc):
                mrow = ms[pl.ds(i * CHUNK + jj, 1), :]
                return acc - mrow * ts[pl.ds(pl.multiple_of(jj * CHUNK, CHUNK), CHUNK), :]

            acc = lax.fori_loop(0, i, inner, jnp.where(cidx == i, 1.0, 0.0).astype(F32))
            ts[pl.ds(pl.multiple_of(i * CHUNK, CHUNK), CHUNK), :] = acc
            return carry

        lax.fori_loop(0, CHUNK, outer, 0)
        for i in range(CHUNK):
            t_ref[:, i, :] = ts[i * CHUNK:(i + 1) * CHUNK, :].T

    return pl.pallas_call(
        body, name="tri_inverse", in_specs=[VMEM_SPEC], out_specs=VMEM_SPEC,
        out_shape=_sds((LANES, CHUNK, CHUNK)),
        scratch_shapes=[pltpu.VMEM((CHUNK * CHUNK, LANES), F32), pltpu.VMEM((CHUNK * CHUNK, LANES), F32)],
        compiler_params=_params(),
    )(m3)


def _gdn_chunk_terms(q, k, v, b, gcc):
    eg = jnp.exp(gcc)
    last = gcc[CHUNK - 1:CHUNK, :]
    egl = jnp.exp(last - gcc)
    gl = jnp.exp(last)
    kb = k * b
    return eg, egl, gl, kb, v * b, kb * eg, q * eg, k * egl


GDN_BLOCK_CHUNKS = 4


def _gdn_block_specs(t, reverse):
    cb = GDN_BLOCK_CHUNKS
    nb = t // (cb * CHUNK)
    idx = (lambda i: nb - 1 - i) if reverse else (lambda i: i)
    tok = pl.BlockSpec((cb * CHUNK, D_GDN), lambda i: (idx(i), 0))
    mat = pl.BlockSpec((N_GDN_HEADS, cb, CHUNK, CHUNK), lambda i: (0, idx(i), 0, 0))
    state = pl.BlockSpec((N_GDN_HEADS, cb, GDN_HEAD_DIM, GDN_HEAD_DIM), lambda i: (0, idx(i), 0, 0))
    return nb, tok, mat, state


def _gdn_scan(qn, kn, cv, be, gc, tinv, amat):
    t = qn.shape[0]
    nch = t // CHUNK

    def body(q_ref, k_ref, v_ref, b_ref, gc_ref, t_ref, a_ref, o_ref, sall_ref, vn_ref, s_scr):
        @pl.when(pl.program_id(0) == 0)
        def _():
            s_scr[...] = jnp.zeros_like(s_scr)

        heads = range(N_GDN_HEADS)
        cols = [slice(hd * LANES, (hd + 1) * LANES) for hd in heads]
        s = [s_scr[hd] for hd in heads]
        for cc in range(GDN_BLOCK_CHUNKS):
            rs = slice(cc * CHUNK, (cc + 1) * CHUNK)
            terms = [_gdn_chunk_terms(q_ref[rs, cs], k_ref[rs, cs], v_ref[rs, cs], b_ref[rs, cs], gc_ref[rs, cs])
                     for cs in cols]
            for hd in heads:
                sall_ref[hd, cc] = s[hd]
            uw = [_mm(t_ref[hd, cc], jnp.concatenate([terms[hd][4], terms[hd][5]], axis=1)) for hd in heads]
            ws_qs = [_mm(jnp.concatenate([uw[hd][:, LANES:], terms[hd][6]], axis=0), s[hd]) for hd in heads]
            vn = [uw[hd][:, :LANES] - ws_qs[hd][:CHUNK] for hd in heads]
            a_vn = [_mm(a_ref[hd, cc], vn[hd]) for hd in heads]
            kd_vn = [_mm_tn(terms[hd][7], vn[hd]) for hd in heads]
            for hd in heads:
                vn_ref[rs, cols[hd]] = vn[hd]
                o_ref[rs, cols[hd]] = ws_qs[hd][CHUNK:] + a_vn[hd]
                s[hd] = s[hd] * terms[hd][2] + kd_vn[hd]
        for hd in heads:
            s_scr[hd] = s[hd]

    nb, tok, mat, state = _gdn_block_specs(t, False)
    return pl.pallas_call(
        body, name="gdn_scan", grid=(nb,),
        in_specs=[tok] * 5 + [mat, mat], out_specs=[tok, state, tok],
        out_shape=[_sds((t, D_GDN)), _sds((N_GDN_HEADS, nch, GDN_HEAD_DIM, GDN_HEAD_DIM)), _sds((t, D_GDN))],
        scratch_shapes=[pltpu.VMEM((N_GDN_HEADS, GDN_HEAD_DIM, GDN_HEAD_DIM), F32)],
        compiler_params=_params("arbitrary"),
    )(qn, kn, cv, be, gc, tinv, amat)


def _gdn_bwd(qn, kn, cv, be, gc, tinv, amat, s_all, vn_all, do):
    t = qn.shape[0]

    def body(q_ref, k_ref, v_ref, b_ref, gc_ref, t_ref, a_ref, sall_ref, vn_ref, do_ref,
             dq_ref, dk_ref, dv_ref, db_ref, dg_ref, ds_scr):
        @pl.when(pl.program_id(0) == 0)
        def _():
            ds_scr[...] = jnp.zeros_like(ds_scr)

        lastrow = _iota((CHUNK, LANES), 0) == CHUNK - 1
        heads = range(N_GDN_HEADS)
        cols = [slice(hd * LANES, (hd + 1) * LANES) for hd in heads]
        each = lambda fn: [fn(hd) for hd in heads]
        rows_cat = lambda x, y: jnp.concatenate([x, y], axis=0)
        lane_cat = lambda x, y: jnp.concatenate([x, y], axis=1)
        dsp = each(lambda hd: ds_scr[hd])
        for cc in reversed(range(GDN_BLOCK_CHUNKS)):
            rs = slice(cc * CHUNK, (cc + 1) * CHUNK)
            q = each(lambda hd: q_ref[rs, cols[hd]])
            k = each(lambda hd: k_ref[rs, cols[hd]])
            v = each(lambda hd: v_ref[rs, cols[hd]])
            b = each(lambda hd: b_ref[rs, cols[hd]])
            gcc = each(lambda hd: gc_ref[rs, cols[hd]])
            do_c = each(lambda hd: do_ref[rs, cols[hd]])
            vn = each(lambda hd: vn_ref[rs, cols[hd]])
            tn = each(lambda hd: t_ref[hd, cc])
            st = each(lambda hd: sall_ref[hd, cc])
            terms = each(lambda hd: _gdn_chunk_terms(q[hd], k[hd], v[hd], b[hd], gcc[hd]))
            eg, egl, gl, kb, vb, kbg, qd, kd = [[terms[hd][i] for hd in heads] for i in range(8)]
            w = each(lambda hd: _mm(tn[hd], kbg[hd]))
            a_do = each(lambda hd: _mm_tn(a_ref[hd, cc], do_c[hd]))
            kd_ds = each(lambda hd: _mm(kd[hd], dsp[hd]))
            da = each(lambda hd: _mm_nt(do_c[hd], vn[hd]))
            dkd = each(lambda hd: _mm_nt(vn[hd], dsp[hd]))
            by_k = each(lambda hd: _mm_nt(rows_cat(kb[hd], q[hd]), k[hd]))
            dgl = each(lambda hd: jnp.sum(jnp.sum(dsp[hd] * st[hd], axis=-1, keepdims=True), axis=0, keepdims=True))
            dvn = each(lambda hd: a_do[hd] + kd_ds[hd])
            do_dvn = each(lambda hd: rows_cat(do_c[hd], dvn[hd]))
            by_s = each(lambda hd: _mm_nt(do_dvn[hd], st[hd]))
            dqd = each(lambda hd: by_s[hd][:CHUNK])
            dvn_dw = each(lambda hd: lane_cat(dvn[hd], -by_s[hd][CHUNK:]))
            dsp = each(lambda hd: _mm_tn(rows_cat(qd[hd], -w[hd]), do_dvn[hd]) + gl[hd] * dsp[hd])
            dt = each(lambda hd: _mm_nt(dvn_dw[hd], lane_cat(vb[hd], kbg[hd])))
            by_t = each(lambda hd: _mm_tn(tn[hd], dvn_dw[hd]))
            tt_dt = each(lambda hd: _mm_tn(tn[hd], dt[hd]))
            dm_raw = each(lambda hd: _mm_nt(tt_dt[hd], tn[hd]))
            masks = each(lambda hd: _chunk_decay(gcc[hd]))
            dkk = each(lambda hd: jnp.where(masks[hd][1], -dm_raw[hd], 0.0) * masks[hd][0])
            dqk = each(lambda hd: da[hd] * masks[hd][0])
            dqk_dkk = each(lambda hd: rows_cat(dqk[hd], dkk[hd]))
            on_k = each(lambda hd: _mm(dqk_dkk[hd], k[hd]))
            dk_mm = each(lambda hd: _mm_tn(dqk_dkk[hd], rows_cat(q[hd], kb[hd])))
            for hd in heads:
                cs = cols[hd]
                dvb, dkbg = by_t[hd][:, :LANES], by_t[hd][:, LANES:]
                gmat = dkk[hd] * by_k[hd][:CHUNK] + dqk[hd] * by_k[hd][CHUNK:]
                dq_ref[rs, cs] = dqd[hd] * eg[hd] + on_k[hd][:CHUNK]
                dkb = on_k[hd][CHUNK:] + dkbg * eg[hd]
                dk_ref[rs, cs] = dkd[hd] * egl[hd] + dk_mm[hd] + dkb * b[hd]
                db = jnp.sum(dkb * k[hd], axis=-1, keepdims=True) + jnp.sum(dvb * v[hd], axis=-1, keepdims=True)
                db_ref[rs, cs] = jnp.broadcast_to(db, (CHUNK, LANES))
                dv_ref[rs, cs] = dvb * b[hd]
                dkd_kd = jnp.sum(dkd[hd] * kd[hd], axis=-1, keepdims=True)
                col_sums = jnp.sum(lane_cat(gmat, jnp.zeros_like(gmat)).T, axis=-1, keepdims=True)
                dgc = (jnp.sum(gmat, axis=-1, keepdims=True) - col_sums[:CHUNK]
                       + jnp.sum(dqd[hd] * qd[hd], axis=-1, keepdims=True)
                       + jnp.sum(dkbg * kbg[hd], axis=-1, keepdims=True) - dkd_kd)
                extra = jnp.sum(dkd_kd, axis=0, keepdims=True) + dgl[hd] * gl[hd]
                dg_ref[rs, cs] = dgc + jnp.where(lastrow, extra, 0.0)
        for hd in heads:
            ds_scr[hd] = dsp[hd]
        dg = dg_ref[...]
        row = _iota(dg.shape, 0)
        pos = row % CHUNK
        step = 1
        while step < CHUNK:
            dg = dg + jnp.where(pos < CHUNK - step, pltpu.roll(dg, dg.shape[0] - step, 0), 0.0)
            step *= 2
        dg_ref[...] = dg

    nb, tok, mat, state = _gdn_block_specs(t, True)
    return pl.pallas_call(
        body, name="gdn_bwd", grid=(nb,),
        in_specs=[tok] * 5 + [mat, mat, state, tok, tok], out_specs=[tok] * 5, out_shape=[_sds((t, D_GDN))] * 5,
        scratch_shapes=[pltpu.VMEM((N_GDN_HEADS, GDN_HEAD_DIM, GDN_HEAD_DIM), F32)],
        compiler_params=_params("arbitrary"),
    )(qn, kn, cv, be, gc, tinv, amat, s_all, vn_all, do)


def _gdn_bwd_conv(proj, convw, dqn, dkn, dcv, dproj):
    t = proj.shape[0]

    def body(xq_ref, xk_ref, xv_ref, wq_ref, wk_ref, wv_ref, dq_ref, dk_ref, dv_ref, _,
             dqkv_ref, dwq_ref, dwk_ref, dwv_ref):
        row = _iota((t, LANES), 0)

        def one(x_ref, w_ref, d_ref, k, dw_ref, scale):
            x = x_ref[...]
            w = w_ref[...]
            y = _conv(x, w, row)
            sg = _sigmoid(y)
            dc = d_ref[...]
            if scale is not None:
                c = y * sg
                r = lax.rsqrt(jnp.sum(c * c, axis=-1, keepdims=True) + EPS)
                ch = c * r
                dc = scale * r * (dc - ch * jnp.sum(dc * ch, axis=-1, keepdims=True))
            dy = dc * sg * (1.0 + y * (1.0 - sg))
            dqkv_ref[:, k * LANES:(k + 1) * LANES] = (
                w[3:4, :] * dy + w[2:3, :] * _shift_up(dy, 1, row) + w[1:2, :] * _shift_up(dy, 2, row)
                + w[0:1, :] * _shift_up(dy, 3, row)).astype(BF)
            for jj in range(CONV_K):
                xs = x if jj == CONV_K - 1 else _shift_down(x, CONV_K - 1 - jj, row)
                dw_ref[jj:jj + 1, :] = jnp.sum(dy * xs, axis=0, keepdims=True)

        one(xq_ref, wq_ref, dq_ref, 0, dwq_ref, GDN_QSCALE)
        one(xk_ref, wk_ref, dk_ref, 1, dwk_ref, 1.0)
        one(xv_ref, wv_ref, dv_ref, 2, dwv_ref, None)

    col, cw, _, qkv = _gdn_specs(t)
    return pl.pallas_call(
        body, name="gdn_bwd_conv", grid=(N_GDN_HEADS,),
        in_specs=[qkv(0), qkv(1), qkv(2), cw(0), cw(4), cw(8), col(0), col(0), col(0), ANY_SPEC],
        out_specs=[pl.BlockSpec((t, QKV), lambda h: (0, COL_GDN // QKV + h)), cw(0), cw(0), cw(0)],
        out_shape=[_sds(dproj.shape, BF)] + [_sds((CONV_K, D_GDN))] * 3,
        input_output_aliases={9: 0}, compiler_params=_params("parallel"),
    )(proj, proj, proj, convw, convw, convw, dqn, dkn, dcv, dproj)


def _mix_out(fox_n, gdn_o, proj, gnw, w_out, x, pmw, plw, after):
    t = x.shape[0]
    tm = min(MATMUL_BLOCK, t)

    def body(fn_ref, go_ref, gz_ref, gnw_ref, w_ref, x_ref, pmw_ref, plw_ref, x1_ref, h2_ref, mixed_ref, omix_ref,
             h2t_ref):
        omix_ref[:, 0:D_FOX] = fn_ref[...]
        for hd in range(N_GDN_HEADS):
            cs = slice(hd * LANES, (hd + 1) * LANES)
            go = go_ref[:, cs]
            r = lax.rsqrt(jnp.mean(go * go, axis=-1, keepdims=True) + EPS)
            gz = gz_ref[:, cs]
            omix_ref[:, D_FOX + hd * LANES:D_FOX + (hd + 1) * LANES] = (
                go * r * gnw_ref[...] * (gz * _sigmoid(gz))).astype(BF)
        mixed = jnp.dot(omix_ref[...], w_ref[...], preferred_element_type=F32)
        mixed_ref[...] = mixed
        r2 = lax.rsqrt(jnp.mean(mixed * mixed, axis=-1, keepdims=True) + EPS)
        x1 = x_ref[...] + mixed * r2 * pmw_ref[...]
        x1_ref[...] = x1
        r3 = lax.rsqrt(jnp.mean(x1 * x1, axis=-1, keepdims=True) + EPS)
        h2 = x1 * r3 * plw_ref[...]
        h2_ref[...] = h2.astype(BF)
        h2t_ref[...] = h2.T.astype(BF)

    tok = lambda w: pl.BlockSpec((tm, w), lambda i: (i, 0))
    vec = lambda w: pl.BlockSpec((1, w), lambda i: (0, 0))
    return pl.pallas_call(
        _ordered(body), name="mix_out", grid=(t // tm,),
        in_specs=[ANY_SPEC, tok(D_FOX), tok(D_GDN), pl.BlockSpec((tm, D_GDN), lambda i: (i, COL_GZ // D_GDN)), vec(LANES),
                  pl.BlockSpec((D_MODEL, D_MODEL), lambda i: (0, 0)), tok(D_MODEL), vec(D_MODEL), vec(D_MODEL)],
        out_specs=[tok(D_MODEL)] * 4 + [pl.BlockSpec((D_MODEL, tm), lambda i: (0, i))],
        out_shape=[_sds((t, D_MODEL)), _sds((t, D_MODEL), BF), _sds((t, D_MODEL)), _sds((t, D_MODEL), BF),
                   _sds((D_MODEL, t), BF)],
        compiler_params=_params("parallel"),
    )(after, fox_n, gdn_o, proj, gnw, w_out, x, pmw, plw)


def _out_bwd(dmixed, w_out, o_fox, gdn_o, proj, fnw, gnw, after):
    t = dmixed.shape[0]
    tm = min(MATMUL_BLOCK, t)

    def body(dm_ref, w_ref, of_ref, go_ref, gz_ref, fnw_ref, gnw_ref, dof_ref, dgo_ref, dgz_ref, dfw_ref, dgw_ref):
        i = pl.program_id(0)

        @pl.when(i == 0)
        def _():
            dfw_ref[...] = jnp.zeros_like(dfw_ref)
            dgw_ref[...] = jnp.zeros_like(dgw_ref)

        domix = _mm_nt(dm_ref[...], w_ref[...])
        first = _iota((1, LANES), 1) < FOX_HEAD_DIM
        dfw = jnp.zeros((1, LANES), F32)
        dgw = jnp.zeros((1, LANES), F32)
        for pr in range(N_FOX_HEADS // 2):
            cs = slice(pr * LANES, (pr + 1) * LANES)
            o = of_ref[:, cs]
            dfn = domix[:, cs]
            o2 = o * o
            s0 = jnp.sum(jnp.where(first, o2, 0.0), axis=-1, keepdims=True)
            s1 = jnp.sum(jnp.where(first, 0.0, o2), axis=-1, keepdims=True)
            r = lax.rsqrt(jnp.where(first, s0, s1) * (1.0 / FOX_HEAD_DIM) + EPS)
            oh = o * r
            dfw = dfw + jnp.sum(dfn * oh, axis=0, keepdims=True)
            doh = dfn * fnw_ref[...]
            pr_ = doh * oh
            m0 = jnp.sum(jnp.where(first, pr_, 0.0), axis=-1, keepdims=True)
            m1 = jnp.sum(jnp.where(first, 0.0, pr_), axis=-1, keepdims=True)
            dof_ref[:, cs] = r * (doh - oh * jnp.where(first, m0, m1) * (1.0 / FOX_HEAD_DIM))
        for hd in range(N_GDN_HEADS):
            cs = slice(hd * LANES, (hd + 1) * LANES)
            go = go_ref[:, cs]
            gz = gz_ref[:, cs]
            dgated = domix[:, D_FOX + hd * LANES:D_FOX + (hd + 1) * LANES]
            r = lax.rsqrt(jnp.mean(go * go, axis=-1, keepdims=True) + EPS)
            goh = go * r
            sg = _sigmoid(gz)
            sz = gz * sg
            gn = goh * gnw_ref[...]
            dgn = dgated * sz
            dgz_ref[:, cs] = (dgated * gn * sg * (1.0 + gz * (1.0 - sg))).astype(BF)
            dgw = dgw + jnp.sum(dgn * goh, axis=0, keepdims=True)
            dgh = dgn * gnw_ref[...]
            dgo_ref[:, cs] = r * (dgh - goh * jnp.mean(dgh * goh, axis=-1, keepdims=True))
        dfw_ref[...] += dfw + pltpu.roll(dfw, FOX_HEAD_DIM, 1)
        dgw_ref[...] += dgw

    tok = lambda w: pl.BlockSpec((tm, w), lambda i: (i, 0))
    vec = lambda w: pl.BlockSpec((1, w), lambda i: (0, 0))
    return pl.pallas_call(
        _ordered(body), name="out_bwd", grid=(t // tm,),
        in_specs=[ANY_SPEC, tok(D_MODEL), pl.BlockSpec((D_MODEL, D_MODEL), lambda i: (0, 0)), tok(D_FOX), tok(D_GDN),
                  pl.BlockSpec((tm, D_GDN), lambda i: (i, COL_GZ // D_GDN)), vec(LANES), vec(LANES)],
        out_specs=[tok(D_FOX), tok(D_GDN), pl.BlockSpec((tm, D_GDN), lambda i: (i, COL_GZ // D_GDN)), vec(LANES),
                   vec(LANES)],
        out_shape=[_sds((t, D_FOX)), _sds((t, D_GDN)), _sds((t, PROJ_W), BF), _sds((1, LANES)), _sds((1, LANES))],
        compiler_params=_params("arbitrary"),
    )(after, dmixed, w_out, o_fox, gdn_o, proj, fnw, gnw)


def _mlp_up(h2, w_upt):
    t = h2.shape[0]
    tm = min(MATMUL_BLOCK, t)

    def body(h_ref, w_ref, up_ref):
        up_ref[...] = lax.dot_general(h_ref[...], w_ref[...], (((1,), (1,)), ((), ())),
                                      preferred_element_type=F32).astype(BF)

    return pl.pallas_call(
        body, name="mlp_up", grid=(t // tm,),
        in_specs=[pl.BlockSpec((tm, D_MODEL), lambda i: (i, 0)), pl.BlockSpec((D_FF, D_MODEL), lambda i: (0, 0))],
        out_specs=pl.BlockSpec((tm, D_FF), lambda i: (i, 0)), out_shape=_sds((t, D_FF), BF),
        compiler_params=_params("parallel"),
    )(h2, w_upt)


def _mlp_down_loss(up, w_down, x1, pw, target):
    t = up.shape[0]
    tm = min(MATMUL_BLOCK, t)

    def body(up_ref, w_ref, x1_ref, pw_ref, tg_ref, dy_ref, dx2_ref, loss_ref, dpw_ref):
        i = pl.program_id(0)

        @pl.when(i == 0)
        def _():
            loss_ref[...] = jnp.zeros_like(loss_ref)
            dpw_ref[...] = jnp.zeros_like(dpw_ref)

        u = jnp.maximum(up_ref[...].astype(F32), 0.0)
        y = jnp.dot((u * u).astype(BF), w_ref[...], preferred_element_type=F32)
        r = lax.rsqrt(jnp.mean(y * y, axis=-1, keepdims=True) + EPS)
        yh = y * r
        pw = pw_ref[...]
        err = x1_ref[...] + yh * pw - tg_ref[...]
        part = jnp.sum(jnp.sum(err * err, axis=-1, keepdims=True), axis=0, keepdims=True) * (0.5 / D_MODEL)
        loss_ref[...] += jnp.broadcast_to(part, loss_ref.shape)
        dx2 = err * (1.0 / D_MODEL)
        dx2_ref[...] = dx2
        dpw_ref[...] += jnp.sum(dx2 * yh, axis=0, keepdims=True)
        dyh = dx2 * pw
        dy_ref[...] = (r * (dyh - yh * jnp.mean(dyh * yh, axis=-1, keepdims=True))).astype(BF)

    tok = lambda w: pl.BlockSpec((tm, w), lambda i: (i, 0))
    vec = lambda w: pl.BlockSpec((1, w), lambda i: (0, 0))
    return pl.pallas_call(
        body, name="mlp_down_loss", grid=(t // tm,),
        in_specs=[tok(D_FF), pl.BlockSpec((D_FF, D_MODEL), lambda i: (0, 0)), tok(D_MODEL), vec(D_MODEL), tok(D_MODEL)],
        out_specs=[tok(D_MODEL), tok(D_MODEL), vec(LANES), vec(D_MODEL)],
        out_shape=[_sds((t, D_MODEL), BF), _sds((t, D_MODEL)), _sds((1, LANES)), _sds((1, D_MODEL))],
        compiler_params=_params("arbitrary"),
    )(up, w_down, x1, pw, target)


def _mlp_bwd_act(dy, w_down, up):
    t = dy.shape[0]
    tm = min(MATMUL_BLOCK, t)

    def body(dy_ref, w_ref, up_ref, dup_ref):
        da = lax.dot_general(dy_ref[...], w_ref[...], (((1,), (1,)), ((), ())), preferred_element_type=F32)
        dup_ref[...] = (da * (2.0 * jnp.maximum(up_ref[...].astype(F32), 0.0))).astype(BF)

    return pl.pallas_call(
        body, name="mlp_bwd_act", grid=(t // tm,),
        in_specs=[pl.BlockSpec((tm, D_MODEL), lambda i: (i, 0)), pl.BlockSpec((D_FF, D_MODEL), lambda i: (0, 0)),
                  pl.BlockSpec((tm, D_FF), lambda i: (i, 0))],
        out_specs=pl.BlockSpec((tm, D_FF), lambda i: (i, 0)), out_shape=_sds((t, D_FF), BF),
        compiler_params=_params("parallel"),
    )(dy, w_down, up)


def _mlp_bwd_in(dup, w_up, x1, plw, dx2, mixed, pmw, after):
    t = dup.shape[0]
    tm = min(MATMUL_BLOCK, t)

    def body(dup_ref, w_ref, x1_ref, plw_ref, dx2_ref, mx_ref, pmw_ref, dx1_ref, dmixed_ref, dplw_ref, dpmw_ref):
        i = pl.program_id(0)

        @pl.when(i == 0)
        def _():
            dplw_ref[...] = jnp.zeros_like(dplw_ref)
            dpmw_ref[...] = jnp.zeros_like(dpmw_ref)

        dh = jnp.dot(dup_ref[...], w_ref[...], preferred_element_type=F32)
        x1 = x1_ref[...]
        r = lax.rsqrt(jnp.mean(x1 * x1, axis=-1, keepdims=True) + EPS)
        xh = x1 * r
        dplw_ref[...] += jnp.sum(dh * xh, axis=0, keepdims=True)
        dxh = dh * plw_ref[...]
        dx1 = dx2_ref[...] + r * (dxh - xh * jnp.mean(dxh * xh, axis=-1, keepdims=True))
        dx1_ref[...] = dx1
        mx = mx_ref[...]
        r2 = lax.rsqrt(jnp.mean(mx * mx, axis=-1, keepdims=True) + EPS)
        mh = mx * r2
        dpmw_ref[...] += jnp.sum(dx1 * mh, axis=0, keepdims=True)
        dmh = dx1 * pmw_ref[...]
        dmixed_ref[...] = (r2 * (dmh - mh * jnp.mean(dmh * mh, axis=-1, keepdims=True))).astype(BF)

    tok = lambda w: pl.BlockSpec((tm, w), lambda i: (i, 0))
    vec = lambda w: pl.BlockSpec((1, w), lambda i: (0, 0))
    return pl.pallas_call(
        _ordered(body), name="mlp_bwd_in", grid=(t // tm,),
        in_specs=[ANY_SPEC, tok(D_FF), pl.BlockSpec((D_FF, D_MODEL), lambda i: (0, 0)), tok(D_MODEL),
                  vec(D_MODEL), tok(D_MODEL), tok(D_MODEL), vec(D_MODEL)],
        out_specs=[tok(D_MODEL), tok(D_MODEL), vec(D_MODEL), vec(D_MODEL)],
        out_shape=[_sds((t, D_MODEL)), _sds((t, D_MODEL), BF), _sds((1, D_MODEL)), _sds((1, D_MODEL))],
        compiler_params=_params("arbitrary"),
    )(after, dup, w_up, x1, plw, dx2, mixed, pmw)


def _wgrad(a, b, a_cols, split=1, a_fn=None, a_block0=0, name="wgrad"):
    t, b_cols = b.shape
    n_a = (a.shape[1] - a_block0 * a_cols) // a_cols if a_block0 else a.shape[1] // a_cols

    def body(a_ref, b_ref, o_ref):
        av = a_ref[...]
        if a_fn is not None:
            av = a_fn(av)
        o_ref[...] = _mm_tn(av, b_ref[...]).astype(BF).reshape(o_ref.shape)

    return pl.pallas_call(
        body, name=name, grid=(n_a,),
        in_specs=[pl.BlockSpec((t, a_cols), lambda i: (0, i + a_block0)), pl.BlockSpec((t, b_cols), lambda i: (0, 0))],
        out_specs=pl.BlockSpec((split, a_cols // split, b_cols), lambda i: (i, 0, 0)),
        out_shape=_sds((n_a * split, a_cols // split, b_cols), BF),
        compiler_params=_params("parallel"),
    )(a, b)


def _wgrad_pre_t(at, b, b_cols, name):
    rows, t = at.shape
    n_b = b.shape[1] // b_cols

    def body(a_ref, b_ref, o_ref):
        o_ref[0] = jnp.dot(a_ref[...], b_ref[...], preferred_element_type=F32).astype(BF)

    return pl.pallas_call(
        body, name=name, grid=(n_b,),
        in_specs=[pl.BlockSpec((rows, t), lambda j: (0, 0)), pl.BlockSpec((t, b_cols), lambda j: (0, j))],
        out_specs=pl.BlockSpec((1, rows, b_cols), lambda j: (j, 0, 0)), out_shape=_sds((n_b, rows, b_cols), BF),
        compiler_params=_params("parallel"),
    )(at, b)


def _small_bwd(proj, fb, al, dtb, dcq, dckt, dbe, dge, dproj):
    t = proj.shape[0]

    def body(sm_ref, fb_ref, al_ref, dtb_ref, dcq_ref, dckt_ref, dbe_ref, dge_ref, _, dsm_ref, dvec_ref):
        s = sm_ref[...]
        lane = _iota((1, LANES), 1)
        dcum = dcq_ref[...] - dckt_ref[...].T
        row = _iota((t, LANES), 0)
        step = 1
        while step < t:
            dcum = dcum + _shift_up(dcum, step, row)
            step *= 2
        dff = dcum * _sigmoid(-(s + fb_ref[...]))
        dbeta = jnp.zeros((t, LANES), F32)
        dg = jnp.zeros((t, LANES), F32)
        for hd in range(N_GDN_HEADS):
            dbeta = jnp.where(lane == SM_GB + hd, dbe_ref[:, hd * LANES:hd * LANES + 1], dbeta)
            dg = jnp.where(lane == SM_GA + hd, dge_ref[:, hd * LANES:hd * LANES + 1], dg)
        beta = _sigmoid(s)
        dgb = dbeta * beta * (1.0 - beta)
        za = s + dtb_ref[...]
        nea = -jnp.exp(al_ref[...])
        dga = dg * nea * _sigmoid(za)
        is_f = lane < SM_GB
        is_b = (lane >= SM_GB) & (lane < SM_GA)
        is_a = (lane >= SM_GA) & (lane < SM_GA + 4)
        dsm_ref[...] = jnp.where(is_f, dff, jnp.where(is_b, dgb, jnp.where(is_a, dga, 0.0))).astype(BF)
        dvec_ref[...] = jnp.zeros_like(dvec_ref)
        dvec_ref[0:1, :] = jnp.sum(jnp.where(is_f, dff, 0.0), axis=0, keepdims=True)
        dvec_ref[1:2, :] = jnp.sum(jnp.where(is_a, dg * nea * _softplus(za), 0.0), axis=0, keepdims=True)
        dvec_ref[2:3, :] = jnp.sum(jnp.where(is_a, dga, 0.0), axis=0, keepdims=True)

    vec = pl.BlockSpec((1, LANES), lambda i: (0, 0))
    full = lambda r, c: pl.BlockSpec((r, c), lambda i: (0, 0))
    small = pl.BlockSpec((t, LANES), lambda i: (0, COL_SMALL // LANES))
    return pl.pallas_call(
        body, name="small_bwd", grid=(1,),
        in_specs=[small, vec, vec, vec, full(t, LANES), full(LANES, t), full(t, 512), full(t, 512), ANY_SPEC],
        out_specs=[small, full(8, LANES)], out_shape=[_sds(dproj.shape, BF), _sds((8, LANES))],
        input_output_aliases={8: 0}, compiler_params=_params("arbitrary"),
    )(proj, fb, al, dtb, dcq, dckt, dbe, dge, dproj)


def _in_bwd(dproj, wt_al, x, nw, dx1, after):
    t = x.shape[0]
    tm = min(MATMUL_BLOCK, t)

    def body(dp_ref, w_ref, x_ref, nw_ref, dx1_ref, dx_ref, dnw_ref):
        i = pl.program_id(0)

        @pl.when(i == 0)
        def _():
            dnw_ref[...] = jnp.zeros_like(dnw_ref)

        dh = jnp.dot(dp_ref[...], w_ref[...], preferred_element_type=F32)
        xv = x_ref[...]
        r = lax.rsqrt(jnp.mean(xv * xv, axis=-1, keepdims=True) + EPS)
        xh = xv * r
        dnw_ref[...] += jnp.sum(dh * xh, axis=0, keepdims=True)
        dxh = dh * nw_ref[...]
        dx_ref[...] = dx1_ref[...] + r * (dxh - xh * jnp.mean(dxh * xh, axis=-1, keepdims=True))

    tok = lambda w: pl.BlockSpec((tm, w), lambda i: (i, 0))
    vec = lambda w: pl.BlockSpec((1, w), lambda i: (0, 0))
    return pl.pallas_call(
        _ordered(body), name="in_bwd", grid=(t // tm,),
        in_specs=[ANY_SPEC, tok(PROJ_W), pl.BlockSpec((PROJ_W, D_MODEL), lambda i: (0, 0)), tok(D_MODEL), vec(D_MODEL),
                  tok(D_MODEL)],
        out_specs=[tok(D_MODEL), vec(D_MODEL)], out_shape=[_sds((t, D_MODEL)), _sds((1, D_MODEL))],
        compiler_params=_params("arbitrary"),
    )(after, dproj, wt_al, x, nw, dx1)


def _row(v, width=None):
    v = v.reshape(1, -1).astype(F32)
    if width is not None and v.shape[1] < width:
        v = jnp.pad(v, ((0, 0), (0, width - v.shape[1])))
    return v


def _lane_vec(v, first):
    return jnp.pad(v.astype(F32), (first, LANES - first - v.shape[0])).reshape(1, LANES)


def _local_step(x, target, wt_al, started, late_weights, on_grads, convw, pre_mix_norm, fox_f_bias, fox_out_norm,
                gdn_a_log, gdn_dt_bias, gdn_out_norm, post_mix_norm, pre_mlp_norm, post_mlp_norm):
    t = x.shape[0]
    nch = t // CHUNK
    nw, pmw, plw, pw = _row(pre_mix_norm), _row(post_mix_norm), _row(pre_mlp_norm), _row(post_mlp_norm)
    fb, al, dtb = _lane_vec(fox_f_bias, SM_FF), _lane_vec(gdn_a_log, SM_GA), _lane_vec(gdn_dt_bias, SM_GA)
    fnw = _row(jnp.tile(fox_out_norm, 2))
    gnw = _row(gdn_out_norm)

    proj, h = _norm_proj(x, nw, wt_al, started)
    cumt, beta, g = _small_prep(proj, fb, al, dtb)
    o_fox, lse, fox_n = _fox_fwd(proj, cumt, fnw)
    qn, kn, cv, gc, be, mmat, amat = _gdn_prep(proj, convw, beta, g)
    n_prob = N_GDN_HEADS * nch
    m3 = mmat.reshape(n_prob, CHUNK, CHUNK)
    if n_prob < LANES:
        m3 = jnp.pad(m3, ((0, LANES - n_prob), (0, 0), (0, 0)))
    tinv = _tri_inverse(m3)[:n_prob].reshape(N_GDN_HEADS, nch, CHUNK, CHUNK)
    token = late_weights("mlp_relay", tinv)
    gdn_o, s_all, vn_all = _gdn_scan(qn, kn, cv, be, gc, tinv, amat)
    w_out = late_weights("w_out", gdn_o)
    x1, h2, mixed, omix, h2t = _mix_out(fox_n, gdn_o, proj, gnw, w_out, x, pmw, plw, token)
    w_up, w_down = late_weights("mlp", h2)
    up = _mlp_up(h2, w_up)
    dy, dx2, loss, d_pw = _mlp_down_loss(up, w_down, x1, pw, target)

    dup = _mlp_bwd_act(dy, w_down, up)
    relu2 = lambda u: jnp.square(jnp.maximum(u.astype(F32), 0.0))
    g_down = _wgrad(up, dy, D_FF // N_DEV, a_fn=relu2, name="wgrad_down")
    g_up = _wgrad_pre_t(h2t, dup, D_FF // N_DEV, name="wgrad_up")
    token = on_grads("mlp", (g_up, g_down))
    dx1, dmixed, d_plw, d_pmw = _mlp_bwd_in(dup, w_up, x1, plw, dx2, mixed, pmw, token)
    token = on_grads("w_out", _wgrad(omix, dmixed, 512, split=4, name="wgrad_out"))
    do_fox, dgo, dproj, d_fnw, d_gnw = _out_bwd(dmixed, w_out, o_fox, gdn_o, proj, fnw, gnw, token)
    dproj, dcq, dckt = _fox_bwd(proj, cumt, lse, o_fox, do_fox, dproj)
    dqn, dkn, dcv, dbe, dge = _gdn_bwd(qn, kn, cv, be, gc, tinv, amat, s_all, vn_all, dgo)
    dproj, dwq, dwk, dwv = _gdn_bwd_conv(proj, convw, dqn, dkn, dcv, dproj)
    dproj, dvec = _small_bwd(proj, fb, al, dtb, dcq, dckt, dbe, dge, dproj)
    g_main = _wgrad(dproj, h, WGRAD_IN_ROWS, name="wgrad_in")
    g_tail = _wgrad(dproj, h, LANES, a_block0=COL_SMALL // LANES, name="wgrad_in_small")
    token = on_grads("w_in", (g_main, g_tail))
    grad_x, d_nw = _in_bwd(dproj, wt_al, x, nw, dx1, token)
    small = dict(norms=(d_nw, d_pmw, d_plw, d_pw), fox_out_norm=d_fnw, gdn_out_norm=d_gnw, loss=loss, vectors=dvec,
                 conv=(dwq, dwk, dwv))
    return grad_x, small


MESH_IDS = pl.DeviceIdType.MESH
CHIP_FLIPS = ((0, 0), (1, 0), (0, 1), (1, 1))


def _place():
    return lax.axis_index("x"), lax.axis_index("y"), lax.axis_index("c")


def _all_gather(blocks, later, dtype):
    n, k = len(blocks), len(later)

    def body(*refs):
        ins, shards, outs = refs[:n], refs[n:n + k], refs[n + k:2 * n + k]
        zones, to_send = refs[2 * n + k:2 * n + 2 * k], refs[2 * n + 2 * k:2 * n + 3 * k]
        stage_in, stage_out = refs[2 * n + 3 * k:2 * n + 4 * k], refs[2 * n + 4 * k:2 * n + 5 * k]
        send_sems, recv_sems, local_sems, late_sems = refs[2 * n + 5 * k:]
        x, y, c = _place()
        sibling = (x, y, 1 - c)
        chips = [(x ^ fx, y ^ fy) for fx, fy in CHIP_FLIPS[1:]]

        def slot(out, px, py, pc):
            return out.at[4 * px + 2 * py + pc]

        def copy(a, k, block, to, src=None):
            return pltpu.make_async_remote_copy(
                src_ref=slot(outs[a], *block) if src is None else src, dst_ref=slot(outs[a], *block),
                send_sem=send_sems.at[a, k], recv_sem=recv_sems.at[a, k], device_id=to, device_id_type=MESH_IDS)

        pending = []
        for a in range(n):
            mine = pltpu.make_async_copy(ins[a], slot(outs[a], x, y, c), local_sems.at[a])
            mine.start()
            pending.append(mine)
        sends = []
        for a in range(n):
            first = [copy(a, 0, (x, y, c), sibling, src=ins[a])]
            first += [copy(a, 1 + j, (x, y, c), (*chip, c), src=ins[a]) for j, chip in enumerate(chips)]
            for cp in first:
                cp.start()
            sends += first
        loads = [pltpu.make_async_copy(shards[a], stage_in[a], late_sems.at[a, 0]) for a in range(k)]
        for cp in loads:
            cp.start()
        for a, (_, transposed) in enumerate(later):
            loads[a].wait()
            val = stage_in[a][...]
            stage_out[a][...] = (val.T if transposed else val).astype(dtype)
            for j, dst in enumerate((slot(zones[a], x, y, c), to_send[a])):
                cp = pltpu.make_async_copy(stage_out[a], dst, late_sems.at[a, 1 + j])
                cp.start()
                pending.append(cp)
        for a in range(n):
            for j, chip in enumerate(chips):
                copy(a, 1 + j, (*chip, c), (x, y, c)).wait_recv()
                fwd = copy(a, 4 + j, (*chip, c), sibling)
                fwd.start()
                sends.append(fwd)
        for a in range(n):
            copy(a, 0, sibling, (x, y, c)).wait_recv()
            for j, chip in enumerate(chips):
                copy(a, 4 + j, (*chip, 1 - c), (x, y, c)).wait_recv()
        for cp in sends:
            cp.wait_send()
        for cp in pending:
            cp.wait()

    shapes = [s_.shape[::-1] if transposed else s_.shape for s_, transposed in later]
    out = pl.pallas_call(
        body, name="all_gather_weights", in_specs=[ANY_SPEC] * (n + k), out_specs=[ANY_SPEC] * (n + 2 * k),
        out_shape=[_sds((N_DEV,) + b.shape, b.dtype) for b in blocks] + [_sds((N_DEV,) + sh, dtype) for sh in shapes]
        + [_sds(sh, dtype) for sh in shapes],
        scratch_shapes=[pltpu.VMEM(s_.shape, s_.dtype) for s_, _ in later] + [pltpu.VMEM(sh, dtype) for sh in shapes]
        + [pltpu.SemaphoreType.DMA((n, 7)), pltpu.SemaphoreType.DMA((n, 7)), pltpu.SemaphoreType.DMA((n,)),
           pltpu.SemaphoreType.DMA((k, 3))],
        compiler_params=pltpu.CompilerParams(vmem_limit_bytes=VMEM_LIMIT, has_side_effects=True),
    )(*blocks, *[s_ for s_, _ in later])
    return out[:n], out[n:n + k], out[n + k:]


def _adamw(w, g, m, v):
    m = ADAM_B1 * m + (1.0 - ADAM_B1) * g
    v = ADAM_B2 * v + (1.0 - ADAM_B2) * (g * g)
    m_hat = m / (1.0 - ADAM_B1 ** ADAM_STEP)
    v_hat = v / (1.0 - ADAM_B2 ** ADAM_STEP)
    return -ADAM_LR * (m_hat / (jnp.sqrt(v_hat) + ADAM_EPS) + ADAM_WD * w), m, v


def _pair_reduce(g, name):
    _, r, c_ = g.shape
    n = len(CHIP_FLIPS)

    def body(g_ref, out_ref, sib_buf, send_sems, recv_sems):
        x, y, c = _place()
        chips = [(x ^ fx, y ^ fy) for fx, fy in CHIP_FLIPS]
        piece = lambda chip, core: g_ref.at[4 * chip[0] + 2 * chip[1] + core]
        copies = [pltpu.make_async_remote_copy(
            src_ref=piece(chip, 1 - c), dst_ref=sib_buf.at[j], send_sem=send_sems.at[j], recv_sem=recv_sems.at[j],
            device_id=(x, y, 1 - c), device_id_type=MESH_IDS) for j, chip in enumerate(chips)]
        for cp in copies:
            cp.start()
        for j, chip in enumerate(chips):
            copies[j].wait_recv()
            out_ref[j] = (piece(chip, c)[...].astype(F32) + sib_buf[j].astype(F32)).astype(BF)
        for cp in copies:
            cp.wait_send()

    return pl.pallas_call(
        body, name=name, in_specs=[VMEM_SPEC], out_specs=VMEM_SPEC, out_shape=_sds((n, r, c_), BF),
        scratch_shapes=[pltpu.VMEM((n, r, c_), BF), pltpu.SemaphoreType.DMA((n,)), pltpu.SemaphoreType.DMA((n,))],
        compiler_params=pltpu.CompilerParams(vmem_limit_bytes=VMEM_LIMIT, has_side_effects=True),
    )(g)


HBM_SPEC = pl.BlockSpec(memory_space=pltpu.HBM)
SEM_SPEC = pl.BlockSpec(memory_space=pltpu.SEMAPHORE)
DATAFLOW = pltpu.SideEffectType.DATAFLOW_SIDE_EFFECTING


def _peers():
    x, y, c = _place()
    return 4 * x + 2 * y + c, [(x ^ (k >> 2), y ^ ((k >> 1) & 1), c ^ (k & 1)) for k in range(1, N_DEV)]


def _peer_index(peer):
    return 4 * peer[0] + 2 * peer[1] + peer[2]


def _exchange_start(srcs, zones, pieces, name, chips=False):
    n = len(srcs)
    fresh = zones is None
    if fresh:
        slots = len(CHIP_FLIPS) if chips else N_DEV
        zones = [_sds((slots,) + (v.shape[1:] if pieces else v.shape), v.dtype) for v in srcs]
    n_in = n if fresh else 2 * n
    among_chips = list(chips) if isinstance(chips, (list, tuple)) else [chips] * n

    def body(*refs):
        ins, sems, token = refs[:n], refs[n_in:n_in + 2 * n], refs[-1]
        zs = refs[n_in + 3 * n:n_in + 4 * n] if fresh else refs[n:2 * n]
        me, peers = _peers()
        x, y, c = _place()
        for a in range(n):
            if among_chips[a] and pieces:
                routes = [((x ^ fx, y ^ fy, c), j, j) for j, (fx, fy) in enumerate(CHIP_FLIPS) if j]
            elif among_chips[a]:
                routes = [((x ^ fx, y ^ fy, c), None, me) for fx, fy in CHIP_FLIPS[1:]]
            else:
                routes = [(peer, _peer_index(peer) if pieces else None, me) for peer in peers]
            for peer, src_slot, dst_slot in routes:
                pltpu.make_async_remote_copy(
                    src_ref=ins[a] if src_slot is None else ins[a].at[src_slot], dst_ref=zs[a].at[dst_slot],
                    send_sem=sems[2 * a], recv_sem=sems[2 * a + 1], device_id=peer, device_id_type=MESH_IDS).start()
        token[...] = jnp.zeros_like(token)

    hbm = lambda v: pltpu.with_memory_space_constraint(v, pltpu.HBM)
    out = pl.pallas_call(
        body, name=name,
        out_shape=tuple([pltpu.SemaphoreType.DMA(())] * (2 * n) + [pltpu.HBM(v.shape, v.dtype) for v in srcs]
                        + [pltpu.HBM(z.shape, z.dtype) for z in zones] + [_sds((8, LANES))]),
        in_specs=[HBM_SPEC] * n_in, out_specs=tuple([SEM_SPEC] * (2 * n) + [HBM_SPEC] * (2 * n) + [VMEM_SPEC]),
        input_output_aliases={i: 2 * n + i for i in range(n_in)},
        compiler_params=pltpu.CompilerParams(has_side_effects=DATAFLOW),
    )(*[hbm(v) for v in srcs], *([] if fresh else [hbm(z) for z in zones]))
    return out[:2 * n], out[2 * n:3 * n], out[3 * n:4 * n], out[-1]


def _relay_start(zones, name):
    n = len(zones)

    def body(*refs):
        zs, sems, token = refs[:n], refs[n:3 * n], refs[-1]
        x, y, c = _place()
        for fx, fy in CHIP_FLIPS:
            slot = 4 * (x ^ fx) + 2 * (y ^ fy) + c
            for a in range(n):
                pltpu.make_async_remote_copy(
                    src_ref=zs[a].at[slot], dst_ref=zs[a].at[slot], send_sem=sems[2 * a], recv_sem=sems[2 * a + 1],
                    device_id=(x, y, 1 - c), device_id_type=MESH_IDS).start()
        token[...] = jnp.zeros_like(token)

    out = pl.pallas_call(
        body, name=name,
        out_shape=tuple([pltpu.SemaphoreType.DMA(())] * (2 * n) + [pltpu.HBM(z.shape, z.dtype) for z in zones]
                        + [_sds((8, LANES))]),
        in_specs=[HBM_SPEC] * n, out_specs=tuple([SEM_SPEC] * (2 * n) + [HBM_SPEC] * n + [VMEM_SPEC]),
        input_output_aliases={i: 2 * n + i for i in range(n)},
        compiler_params=pltpu.CompilerParams(has_side_effects=DATAFLOW),
    )(*[pltpu.with_memory_space_constraint(z, pltpu.HBM) for z in zones])
    return out[:2 * n], [], out[2 * n:3 * n], out[-1]


def _exchange_wait(sems, srcs, zones, after, name, chips=False, n_copies=None):
    n, n_src = len(zones), len(srcs)
    after = list(after) if isinstance(after, (list, tuple)) else [after]
    n_copies = n_copies or (len(CHIP_FLIPS) - 1 if chips else N_DEV - 1)

    def body(*refs):
        zs, sm = refs[n_src:n_src + n], refs[n_src + n:n_src + 3 * n]
        me, peers = _peers()
        for a in range(n):
            seven = zs[a].at[pl.ds(0, n_copies)]
            cp = pltpu.make_async_remote_copy(src_ref=seven, dst_ref=seven, send_sem=sm[2 * a], recv_sem=sm[2 * a + 1],
                                              device_id=peers[0], device_id_type=MESH_IDS)
            cp.wait_send()
            cp.wait_recv()

    out = pl.pallas_call(
        body, name=name, out_shape=tuple([pltpu.HBM(v.shape, v.dtype) for v in srcs] + [pltpu.HBM(z.shape, z.dtype) for z in zones]),
        in_specs=[HBM_SPEC] * (n_src + n) + [SEM_SPEC] * (2 * n) + [ANY_SPEC] * len(after),
        out_specs=tuple([HBM_SPEC] * (n_src + n)), input_output_aliases={i: i for i in range(n_src + n)},
        compiler_params=pltpu.CompilerParams(has_side_effects=DATAFLOW),
    )(*srcs, *zones, *sems, *after)
    return out[:n_src], out[n_src:]


def _sum_adamw(zone, own, w, m, v, name, chips=False):
    n_slots, r, c_ = zone.shape
    rb = next((b for b in (256, 128) if r % b == 0), r)

    def body(me_ref, z_ref, own_ref, w_ref, m_ref, v_ref, grad_ref, delta_ref, nm_ref, nv_ref):
        total = None
        for d in range(n_slots):
            part = jnp.where(me_ref[0] == d, own_ref[0], z_ref[d]).astype(F32)
            total = part if total is None else total + part
        grad_ref[...] = total
        delta_ref[...], nm_ref[...], nv_ref[...] = _adamw(w_ref[...], total, m_ref[...], v_ref[...])

    x, y, c = _place()
    mine = 0 * x if chips else 4 * x + 2 * y + c
    blk = pl.BlockSpec((rb, c_), lambda i, me_ref: (i, 0))
    return pl.pallas_call(
        body, name=name,
        grid_spec=pltpu.PrefetchScalarGridSpec(
            num_scalar_prefetch=1, grid=(r // rb,),
            in_specs=[pl.BlockSpec((n_slots, rb, c_), lambda i, me_ref: (0, i, 0)),
                      pl.BlockSpec((1, rb, c_), lambda i, me_ref: (me_ref[0], i, 0)), blk, blk, blk],
            out_specs=[blk] * 4),
        out_shape=[_sds((r, c_))] * 4, compiler_params=_params("parallel"),
    )(mine.astype(jnp.int32).reshape(1), zone, own, w, m, v)


SMALL_NORMS = ("pre_mix_norm", "post_mix_norm", "pre_mlp_norm", "post_mlp_norm")
SMALL_ORDER = SMALL_NORMS + ("fox_out_norm", "gdn_out_norm", "fox_f_bias", "gdn_a_log", "gdn_dt_bias", "gdn_conv_w")
CONV_SLAB_ROWS, CONV_SLAB_LANES = 8, 256


def _small_pack(small):
    def body(n0, n1, n2, n3, fnw_ref, gnw_ref, loss_ref, vec_ref, out_ref):
        out_ref[...] = jnp.zeros_like(out_ref)
        for i, ref in enumerate((n0, n1, n2, n3)):
            out_ref[i:i + 1, :] = ref[...]
        out_ref[4:5, 0:LANES] = fnw_ref[...]
        out_ref[4:5, LANES:2 * LANES] = gnw_ref[...]
        out_ref[4:5, 2 * LANES:3 * LANES] = loss_ref[...]
        out_ref[5:8, 0:LANES] = vec_ref[0:3, :]

    return pl.pallas_call(body, name="small_pack", in_specs=[VMEM_SPEC] * 8, out_specs=VMEM_SPEC,
                          out_shape=_sds((8, D_MODEL)))(*small["norms"], small["fox_out_norm"], small["gdn_out_norm"],
                                                        small["loss"], small["vectors"])


def _conv_slabs(dconv):
    blocks = dconv.reshape(CONV_K, N_DEV, -1).transpose(1, 0, 2)
    blocks = jnp.pad(blocks, ((0, 0), (0, CONV_SLAB_ROWS - CONV_K), (0, CONV_SLAB_LANES - blocks.shape[2])))
    return blocks.reshape(N_DEV * CONV_SLAB_ROWS, CONV_SLAB_LANES)


def _small_update(zone, conv_zone, own, own_conv, w, m, v):
    n = len(SMALL_ORDER)
    n_conv = w["gdn_conv_w"].shape[1]

    def body(me_ref, z_ref, zc_ref, own_ref, ownc_ref, *refs):
        params, loss_ref, outs, (tot, totc) = refs[:3 * n], refs[3 * n], refs[3 * n + 1:7 * n + 1], refs[-2:]
        total, total_c = None, None
        for d in range(N_DEV):
            part = jnp.where(me_ref[0] == d, own_ref[...], z_ref[d])
            part_c = jnp.where(me_ref[0] == d, ownc_ref[...], zc_ref[d])
            total, total_c = (part, part_c) if d == 0 else (total + part, total_c + part_c)
        tot[...] = total
        totc[...] = total_c
        loss_ref[...] = tot[4, 2 * LANES:2 * LANES + 1]
        mine = totc[pl.ds(pl.multiple_of(me_ref[0] * CONV_SLAB_ROWS, CONV_SLAB_ROWS), CONV_SLAB_ROWS), :]
        g = dict(zip(SMALL_NORMS, (tot[0], tot[1], tot[2], tot[3])))
        g.update(fox_out_norm=tot[4, 0:FOX_HEAD_DIM], gdn_out_norm=tot[4, LANES:LANES + GDN_HEAD_DIM],
                 fox_f_bias=tot[5, SM_FF:SM_FF + N_FOX_HEADS], gdn_a_log=tot[6, SM_GA:SM_GA + N_GDN_HEADS],
                 gdn_dt_bias=tot[7, SM_GA:SM_GA + N_GDN_HEADS], gdn_conv_w=mine[0:CONV_K, 0:n_conv])
        for i, name in enumerate(SMALL_ORDER):
            w_ref, m_ref, v_ref = params[3 * i:3 * i + 3]
            outs[4 * i][...] = g[name]
            outs[4 * i + 1][...], outs[4 * i + 2][...], outs[4 * i + 3][...] = _adamw(w_ref[...], g[name], m_ref[...],
                                                                                     v_ref[...])

    x, y, c = _place()
    operands = [a[name] for name in SMALL_ORDER for a in (w, m, v)]
    out = pl.pallas_call(
        body, name="small_update",
        in_specs=[pl.BlockSpec(memory_space=pltpu.SMEM)] + [VMEM_SPEC] * (4 + 3 * n), out_specs=[VMEM_SPEC] * (1 + 4 * n),
        out_shape=[_sds((1,))] + [_sds(w[name].shape) for name in SMALL_ORDER for _ in range(4)],
        scratch_shapes=[pltpu.VMEM(zone.shape[1:], F32), pltpu.VMEM(conv_zone.shape[1:], F32)],
    )((4 * x + 2 * y + c).astype(jnp.int32).reshape(1), zone, conv_zone, own, own_conv, *operands)
    return out[0][0], {name: out[1 + 4 * i:5 + 4 * i] for i, name in enumerate(SMALL_ORDER)}


def _native_rows():
    groups = []
    for first, n_groups in ((0, N_FOX_HEADS // 2), (D_FOX * 3 + N_FOX_HEADS, N_GDN_HEADS)):
        for g in range(n_groups):
            groups += [(first + part * n_groups * LANES + g * LANES, first + part * n_groups * LANES + (g + 1) * LANES)
                       for part in range(3)]
    return tuple(groups) + ((3088, 3600), (1536, 1544), (3080, 3088))


NATIVE_ROWS = _native_rows()


W_IN_PIECE = D_PROJ // N_DEV
WGRAD_IN_ROWS = 512
SHUFFLE_LANES = 256


def _to_aligned_moves():
    moves, o = [], 0
    for lo, hi in NATIVE_ROWS:
        r = lo
        while r < hi:
            d = r // W_IN_PIECE
            k = min(hi, (d + 1) * W_IN_PIECE) - r
            moves.append((0, d, r - d * W_IN_PIECE, 0, o, k))
            r, o = r + k, o + k
    return moves


def _from_aligned_moves():
    moves = []
    for _, d, a, _, o, k in _to_aligned_moves():
        while k:
            n = min(k, WGRAD_IN_ROWS - o % WGRAD_IN_ROWS) if o < COL_SMALL else k
            moves.append((0, o // WGRAD_IN_ROWS, o % WGRAD_IN_ROWS, d, a, n) if o < COL_SMALL else
                         (1, 0, o - COL_SMALL, d, a, n))
            o, a, k = o + n, a + n, k - n
    return moves


def _shuffle_rows(srcs, moves, out_shape, name):
    c = srcs[0].shape[-1]

    def body(*refs):
        s_refs, o_ref, s_f, o_f = refs[:len(srcs)], refs[len(srcs)], refs[len(srcs) + 1:-1], refs[-1]
        for s_ref, f in zip(s_refs, s_f):
            f[...] = s_ref[...].astype(F32)
        o_f[...] = jnp.zeros_like(o_f)
        for i, ss, so, ds, do, k in moves:
            o_f[ds, pl.ds(do, k), :] = s_f[i][ss, pl.ds(so, k), :]
        o_ref[...] = o_f[...].astype(BF)

    blk = lambda shape: pl.BlockSpec(tuple(shape[:-1]) + (SHUFFLE_LANES,), lambda j: (0, 0, j))
    scratch = lambda shape: pltpu.VMEM(tuple(shape[:-1]) + (SHUFFLE_LANES,), F32)
    return pl.pallas_call(
        body, name=name, grid=(c // SHUFFLE_LANES,), in_specs=[blk(s.shape) for s in srcs], out_specs=blk(out_shape),
        out_shape=_sds(out_shape, BF), scratch_shapes=[scratch(s.shape) for s in srcs] + [scratch(out_shape)],
        compiler_params=_params("parallel"),
    )(*srcs)


def _cols_from_pieces(p):
    return p.transpose(1, 0, 2).reshape(p.shape[1], -1)


WEIGHT_ORDER = ("pre_mix_norm", "w_in", "fox_f_bias", "fox_out_norm", "gdn_conv_w", "gdn_a_log", "gdn_dt_bias",
                "gdn_out_norm", "w_out", "post_mix_norm", "pre_mlp_norm", "w_up", "w_down", "post_mlp_norm")


def kernel(x, pre_mix_norm, w_in, fox_f_bias, fox_out_norm, gdn_conv_w, gdn_a_log, gdn_dt_bias, gdn_out_norm, w_out, post_mix_norm, pre_mlp_norm, w_up, w_down, post_mlp_norm, loss_target, m_pre_mix_norm, m_w_in, m_fox_f_bias, m_fox_out_norm, m_gdn_conv_w, m_gdn_a_log, m_gdn_dt_bias, m_gdn_out_norm, m_w_out, m_post_mix_norm, m_pre_mlp_norm, m_w_up, m_w_down, m_post_mlp_norm, v_pre_mix_norm, v_w_in, v_fox_f_bias, v_fox_out_norm, v_gdn_conv_w, v_gdn_a_log, v_gdn_dt_bias, v_gdn_out_norm, v_w_out, v_post_mix_norm, v_pre_mlp_norm, v_w_up, v_w_down, v_post_mlp_norm):
    w = dict(pre_mix_norm=pre_mix_norm, w_in=w_in, fox_f_bias=fox_f_bias, fox_out_norm=fox_out_norm,
             gdn_conv_w=gdn_conv_w, gdn_a_log=gdn_a_log, gdn_dt_bias=gdn_dt_bias, gdn_out_norm=gdn_out_norm, w_out=w_out,
             post_mix_norm=post_mix_norm, pre_mlp_norm=pre_mlp_norm, w_up=w_up, w_down=w_down, post_mlp_norm=post_mlp_norm)
    mom = dict(pre_mix_norm=m_pre_mix_norm, w_in=m_w_in, fox_f_bias=m_fox_f_bias, fox_out_norm=m_fox_out_norm,
               gdn_conv_w=m_gdn_conv_w, gdn_a_log=m_gdn_a_log, gdn_dt_bias=m_gdn_dt_bias, gdn_out_norm=m_gdn_out_norm,
               w_out=m_w_out, post_mix_norm=m_post_mix_norm, pre_mlp_norm=m_pre_mlp_norm, w_up=m_w_up, w_down=m_w_down,
               post_mlp_norm=m_post_mlp_norm)
    var = dict(pre_mix_norm=v_pre_mix_norm, w_in=v_w_in, fox_f_bias=v_fox_f_bias, fox_out_norm=v_fox_out_norm,
               gdn_conv_w=v_gdn_conv_w, gdn_a_log=v_gdn_a_log, gdn_dt_bias=v_gdn_dt_bias, gdn_out_norm=v_gdn_out_norm,
               w_out=v_w_out, post_mix_norm=v_post_mix_norm, pre_mlp_norm=v_pre_mlp_norm, w_up=v_w_up, w_down=v_w_down,
               post_mlp_norm=v_post_mlp_norm)

    (win_g, conv_g), zones, shards = _all_gather([w_in.T.astype(BF), gdn_conv_w],
                                                 [(w_out, False), (w_up, True), (w_down, False)], BF)
    wt_al = _shuffle_rows([win_g], _to_aligned_moves(), (1, PROJ_W, D_MODEL), "w_in_to_aligned")[0]
    convw = _cols_from_pieces(conv_g)
    sems, shards, zones, after = _exchange_start(shards, zones, False, "gather_start", chips=[False, True, True])
    gathers = dict(w_out=(sems[:2], shards[:1], zones[:1], after), mlp=(sems[2:], shards[1:], zones[1:], after))

    def late_weights(name, after):
        if name == "mlp_relay":
            sems, shards, zones, _ = gathers["mlp"]
            _, zones = _exchange_wait(sems, shards, zones, after, "gather_mlp_wait", chips=True)
            gathers["mlp"] = _relay_start(zones, "gather_mlp_relay")
            return gathers["mlp"][3]
        sems, shards, zones, _ = gathers[name]
        _, got = _exchange_wait(sems, shards, zones, after, "gather_" + name + "_done",
                                n_copies=len(CHIP_FLIPS) if name == "mlp" else None)
        if name == "w_out":
            return got[0].reshape(D_MODEL, D_MODEL)
        return got[0].reshape(D_FF, D_MODEL), got[1].reshape(D_FF, D_MODEL)

    scatters = {}

    def on_grads(name, g):
        if name == "mlp":
            scatters["mlp"] = list(g)
            return g[0]
        if name == "w_out":
            sems, srcs, zones, token = _exchange_start(scatters["mlp"] + [g], None, True, "scatter_mlp_w_out_start")
            scatters["mlp"] = (sems[:4], srcs[:2], zones[:2], token)
            scatters["w_out"] = (sems[4:], srcs[2:], zones[2:], token)
            return token
        g = _shuffle_rows(list(g), _from_aligned_moves(), (N_DEV, W_IN_PIECE, D_MODEL), "w_in_grad_from_aligned")
        scatters[name] = _exchange_start([_pair_reduce(g, "pair_reduce_w_in")], None, True, "scatter_w_in_start", chips=True)
        return scatters[name][3]

    grad_x, small = _local_step(
        x[0], loss_target[0], wt_al, after, late_weights, on_grads, convw, pre_mix_norm,
        fox_f_bias, fox_out_norm, gdn_a_log, gdn_dt_bias, gdn_out_norm, post_mix_norm, pre_mlp_norm, post_mlp_norm)
    slabs = [_small_pack(small), _conv_slabs(jnp.concatenate(small["conv"], axis=1))]
    scatters["small"] = _exchange_start(slabs, None, False, "small_start")

    grads, delta, new_m, new_v = {}, {}, {}, {}
    after = scatters["small"][3]
    for name, members in (("mlp", ("w_up", "w_down")), ("w_out", ("w_out",)), ("small", ()), ("w_in", ("w_in",))):
        sems, srcs, zones, _ = scatters[name]
        srcs, zones = _exchange_wait(sems, srcs, zones, after, "scatter_" + name + "_wait", chips=name == "w_in")
        if name == "small":
            loss, updated = _small_update(*zones, *srcs, w, mom, var)
            for n, res in updated.items():
                grads[n], delta[n], new_m[n], new_v[n] = res
            after = grads["pre_mix_norm"]
        for n, zone, own in zip(members, zones, srcs):
            if n == "w_in":
                res = _sum_adamw(zone, own, w[n].T, mom[n].T, var[n].T, "adamw_" + n, chips=True)
                grads[n], delta[n], new_m[n], new_v[n] = [r.T for r in res]
            else:
                grads[n], delta[n], new_m[n], new_v[n] = _sum_adamw(zone, own, w[n], mom[n], var[n], "adamw_" + n)
        if members:
            after = [grads[n] for n in members]

    return (loss, grad_x[None], *[grads[n] for n in WEIGHT_ORDER], *[delta[n] for n in WEIGHT_ORDER],
            *[new_m[n] for n in WEIGHT_ORDER], *[new_v[n] for n in WEIGHT_ORDER])
```

```python
import jax
import jax.numpy as jnp
from jax import lax
from jax.experimental import pallas as pl
from jax.experimental.pallas import tpu as pltpu

F32 = jnp.float32
BF = jnp.bfloat16

D_MODEL = 1024
N_FOX_HEADS, FOX_HEAD_DIM = 8, 64
N_GDN_HEADS, GDN_HEAD_DIM = 4, 128
D_FOX = N_FOX_HEADS * FOX_HEAD_DIM
D_GDN = N_GDN_HEADS * GDN_HEAD_DIM
CHUNK = 64
CONV_K = 4
D_FF = 4 * D_MODEL
EPS = 1e-6
D_PROJ = 3600
N_DEV = 8

PROJ_W = 3712
COL_FOX, COL_GDN, COL_GZ, COL_SMALL = 0, 1536, 3072, 3584
LANES = 128
QKV = 3 * LANES
SM_FF, SM_GB, SM_GA = 0, 8, 12

ADAM_LR, ADAM_B1, ADAM_B2, ADAM_EPS, ADAM_WD, ADAM_STEP = 0.001, 0.9, 0.999, 1e-08, 0.01, 10

TOKEN_BLOCK = 256
MATMUL_BLOCK = 512
FOX_SCALE = FOX_HEAD_DIM ** -0.5
GDN_QSCALE = GDN_HEAD_DIM ** -0.5
NEG_BIG = -1e30
VMEM_LIMIT = 56 * 1024 * 1024

VMEM_SPEC = pl.BlockSpec(memory_space=pltpu.VMEM)
ANY_SPEC = pl.BlockSpec(memory_space=pl.ANY)


def _sds(shape, dtype=F32):
    return jax.ShapeDtypeStruct(shape, dtype)


def _params(*sem):
    return pltpu.CompilerParams(dimension_semantics=sem if sem else None, vmem_limit_bytes=VMEM_LIMIT)


def _ordered(body):
    def ordered(_, *refs):
        body(*refs)

    return ordered


def _mm(a, b):
    return jnp.dot(a.astype(BF), b.astype(BF), preferred_element_type=F32)


def _mm_nt(a, b):
    return lax.dot_general(a.astype(BF), b.astype(BF), (((1,), (1,)), ((), ())), preferred_element_type=F32)


def _mm_tn(a, b):
    return lax.dot_general(a.astype(BF), b.astype(BF), (((0,), (0,)), ((), ())), preferred_element_type=F32)


def _sigmoid(x):
    return 1.0 / (1.0 + jnp.exp(-x))


def _softplus(x):
    return jnp.maximum(x, 0.0) + jnp.log1p(jnp.exp(-jnp.abs(x)))


def _iota(shape, dim):
    return lax.broadcasted_iota(jnp.int32, shape, dim)


def _shift_down(x, s, row):
    return jnp.where(row >= s, pltpu.roll(x, s, 0), 0.0)


def _shift_up(x, s, row):
    n = x.shape[0]
    return jnp.where(row < n - s, pltpu.roll(x, n - s, 0), 0.0)


def _norm_proj(x, nw, wt_al, after):
    t = x.shape[0]

    def body(x_ref, nw_ref, w_ref, proj_ref, h_ref):
        xv = x_ref[...]
        r = lax.rsqrt(jnp.mean(xv * xv, axis=-1, keepdims=True) + EPS)
        h = (xv * r * nw_ref[...]).astype(BF)
        h_ref[...] = h
        proj_ref[...] = lax.dot_general(h, w_ref[...], (((1,), (1,)), ((), ())), preferred_element_type=F32)

    tm = min(MATMUL_BLOCK, t)
    return pl.pallas_call(
        _ordered(body), name="norm_proj", grid=(t // tm,),
        in_specs=[ANY_SPEC, pl.BlockSpec((tm, D_MODEL), lambda i: (i, 0)), pl.BlockSpec((1, D_MODEL), lambda i: (0, 0)),
                  pl.BlockSpec((PROJ_W, D_MODEL), lambda i: (0, 0))],
        out_specs=[pl.BlockSpec((tm, PROJ_W), lambda i: (i, 0)), pl.BlockSpec((tm, D_MODEL), lambda i: (i, 0))],
        out_shape=[_sds((t, PROJ_W)), _sds((t, D_MODEL), BF)],
        compiler_params=_params("parallel"),
    )(after, x, nw, wt_al)


def _lane_column(x, lane):
    return jnp.sum(jnp.where(_iota((1, LANES), 1) == lane, x, 0.0), axis=-1, keepdims=True)


def _small_prep(proj, fb, al, dtb):
    t = proj.shape[0]

    def body(sm_ref, fb_ref, al_ref, dtb_ref, cumt_ref, beta_ref, g_ref):
        s = sm_ref[...]
        z = s + fb_ref[...]
        cum = jnp.minimum(z, 0.0) - jnp.log1p(jnp.exp(-jnp.abs(z)))
        row = _iota((t, LANES), 0)
        step = 1
        while step < t:
            cum = cum + _shift_down(cum, step, row)
            step *= 2
        cumt_ref[...] = cum.T
        beta_ref[...] = _sigmoid(s)
        g_ref[...] = -jnp.exp(al_ref[...]) * _softplus(s + dtb_ref[...])

    vec = pl.BlockSpec((1, LANES), lambda i: (0, 0))
    tok = pl.BlockSpec((t, LANES), lambda i: (0, 0))
    return pl.pallas_call(
        body, name="small_prep", grid=(1,),
        in_specs=[pl.BlockSpec((t, LANES), lambda i: (0, COL_SMALL // LANES)), vec, vec, vec],
        out_specs=[pl.BlockSpec((LANES, t), lambda i: (0, 0)), tok, tok],
        out_shape=[_sds((LANES, t)), _sds((t, LANES)), _sds((t, LANES))],
        compiler_params=_params("arbitrary"),
    )(proj, fb, al, dtb)


def _fox_stack(x, first):
    return jnp.concatenate([jnp.where(first, x, 0.0), jnp.where(first, 0.0, x)], axis=0).astype(BF)


def _fox_unstack(y, first):
    n = y.shape[0] // 2
    return jnp.where(first, y[:n], y[n:])


def _fox_logits(q2_i, kb, cumt_ref, pair, i, tq):
    klen = (i + 1) * tq
    s = lax.dot_general(q2_i, kb[:klen], (((1,), (1,)), ((), ())), preferred_element_type=F32)
    upper = _iota((2 * tq, 1), 0) < tq
    s = s - jnp.where(upper, cumt_ref[pl.ds(2 * pair, 1), 0:klen], cumt_ref[pl.ds(2 * pair + 1, 1), 0:klen])
    causal = _iota((2 * tq, tq), 1) <= _iota((2 * tq, tq), 0) % tq
    parts = [(s[:, :klen - tq], 0, klen - tq)] if i else []
    return parts + [(jnp.where(causal, s[:, klen - tq:], NEG_BIG), klen - tq, klen)]


def _fox_fwd(proj, cumt, fnw, after):
    t = proj.shape[0]
    tq = min(TOKEN_BLOCK, t // 2)
    nq = t // tq

    def body(q_ref, k_ref, v_ref, cumt_ref, fnw_ref, o_ref, lse_ref, fn_ref):
        j = pl.program_id(0)
        first = _iota((1, LANES), 1) < FOX_HEAD_DIM
        kb = k_ref[...].astype(BF)
        vb = v_ref[...].astype(BF)
        for i in range(nq):
            rows = slice(i * tq, (i + 1) * tq)
            q2 = _fox_stack(q_ref[rows, :] * FOX_SCALE, first)
            parts = _fox_logits(q2, kb, cumt_ref, j, i, tq)
            m = jnp.max(parts[-1][0], axis=-1, keepdims=True)
            if i:
                m = jnp.maximum(m, jnp.max(parts[0][0], axis=-1, keepdims=True))
            l = jnp.zeros((2 * tq, 1), F32)
            o = jnp.zeros((2 * tq, LANES), F32)
            for s, lo, hi in parts:
                p = jnp.exp(s - m)
                l = l + jnp.sum(p, axis=-1, keepdims=True)
                o = o + jnp.dot(p.astype(BF), vb[lo:hi], preferred_element_type=F32)
            o_acc = _fox_unstack(o / l, first)
            lse_acc = _fox_unstack(jnp.broadcast_to(m + jnp.log(l), (2 * tq, LANES)), first)
            o_ref[rows, :] = o_acc
            lse_ref[rows, :] = lse_acc
            o2 = o_acc * o_acc
            s0 = jnp.sum(jnp.where(first, o2, 0.0), axis=-1, keepdims=True)
            s1 = jnp.sum(jnp.where(first, 0.0, o2), axis=-1, keepdims=True)
            r = lax.rsqrt(jnp.where(first, s0, s1) * (1.0 / FOX_HEAD_DIM) + EPS)
            fn_ref[rows, :] = (o_acc * r * fnw_ref[...]).astype(BF)

    qkv = lambda k: pl.BlockSpec((t, LANES), lambda j: (0, COL_FOX // LANES + 3 * j + k))
    pair = pl.BlockSpec((t, LANES), lambda j: (0, j))
    return pl.pallas_call(
        _ordered(body), name="fox_fwd", grid=(N_FOX_HEADS // 2,),
        in_specs=[ANY_SPEC, qkv(0), qkv(1), qkv(2), pl.BlockSpec((LANES, t), lambda j: (0, 0)),
                  pl.BlockSpec((1, LANES), lambda j: (0, 0))],
        out_specs=[pair, pair, pair],
        out_shape=[_sds((t, D_FOX)), _sds((t, D_FOX)), _sds((t, D_FOX), BF)],
        compiler_params=_params("parallel"),
    )(after, proj, proj, proj, cumt, fnw)


def _fox_bwd(proj, cumt, lse, o, do, dproj):
    t = proj.shape[0]
    tq = min(TOKEN_BLOCK, t // 2)
    nq = t // tq

    def body(q_ref, k_ref, v_ref, cumt_ref, lse_ref, o_ref, do_ref, _, dqkv_ref, dcq_ref, dckt_ref, dk_s, dv_s):
        j = pl.program_id(0)

        @pl.when(j == 0)
        def _():
            dcq_ref[...] = jnp.zeros_like(dcq_ref)
            dckt_ref[...] = jnp.zeros_like(dckt_ref)

        lane = _iota((1, LANES), 1)

        first = _iota((1, LANES), 1) < FOX_HEAD_DIM
        kb = k_ref[...].astype(BF)
        vb = v_ref[...].astype(BF)
        dk_s[...] = jnp.zeros_like(dk_s)
        dv_s[...] = jnp.zeros_like(dv_s)
        for i in range(nq):
            rows = slice(i * tq, (i + 1) * tq)
            do_i = do_ref[rows, :]
            prod = do_i * o_ref[rows, :]
            lse_i = lse_ref[rows, :]
            q2 = _fox_stack(q_ref[rows, :] * FOX_SCALE, first)
            do2 = _fox_stack(do_i, first)
            delta = jnp.concatenate([jnp.sum(jnp.where(first, prod, 0.0), axis=-1, keepdims=True),
                                     jnp.sum(jnp.where(first, 0.0, prod), axis=-1, keepdims=True)], axis=0)
            lse2 = jnp.concatenate([lse_i[:, 0:1], lse_i[:, FOX_HEAD_DIM:FOX_HEAD_DIM + 1]], axis=0)
            dq2 = jnp.zeros((2 * tq, LANES), F32)
            dcq2 = jnp.zeros((2 * tq, 1), F32)
            for s, lo, hi in _fox_logits(q2, kb, cumt_ref, j, i, tq):
                p = jnp.exp(s - lse2)
                ds = p * (_mm_nt(do2, vb[lo:hi]) - delta)
                dsb = ds.astype(BF)
                dq2 = dq2 + jnp.dot(dsb, kb[lo:hi], preferred_element_type=F32)
                dk_s[lo:hi, :] += _mm_tn(dsb, q2)
                dv_s[lo:hi, :] += _mm_tn(p, do2)
                dcq2 = dcq2 + jnp.sum(ds, axis=-1, keepdims=True)
                dckt_ref[pl.ds(2 * j, 1), lo:hi] += jnp.sum(ds[:tq], axis=0, keepdims=True)
                dckt_ref[pl.ds(2 * j + 1, 1), lo:hi] += jnp.sum(ds[tq:], axis=0, keepdims=True)
            dqkv_ref[rows, 0:LANES] = (_fox_unstack(dq2, first) * FOX_SCALE).astype(BF)
            dcq_ref[rows, :] += jnp.where(lane == 2 * j, dcq2[:tq], jnp.where(lane == 2 * j + 1, dcq2[tq:], 0.0))
        dqkv_ref[:, LANES:2 * LANES] = dk_s[...].astype(BF)
        dqkv_ref[:, 2 * LANES:QKV] = dv_s[...].astype(BF)

    qkv = lambda k: pl.BlockSpec((t, LANES), lambda j: (0, COL_FOX // LANES + 3 * j + k))
    pair = pl.BlockSpec((t, LANES), lambda j: (0, j))
    rows128 = pl.BlockSpec((LANES, t), lambda j: (0, 0))
    return pl.pallas_call(
        body, name="fox_bwd", grid=(N_FOX_HEADS // 2,),
        in_specs=[qkv(0), qkv(1), qkv(2), rows128, pair, pair, pair, ANY_SPEC],
        out_specs=[pl.BlockSpec((t, QKV), lambda j: (0, COL_FOX // QKV + j)),
                   pl.BlockSpec((t, LANES), lambda j: (0, 0)), rows128],
        out_shape=[_sds(dproj.shape, BF), _sds((t, LANES)), _sds((LANES, t))],
        scratch_shapes=[pltpu.VMEM((t, LANES), F32), pltpu.VMEM((t, LANES), F32)],
        input_output_aliases={7: 0}, compiler_params=_params("arbitrary"),
    )(proj, proj, proj, cumt, lse, o, do, dproj)


def _conv(x, w, row):
    return (w[3:4, :] * x + w[2:3, :] * _shift_down(x, 1, row) + w[1:2, :] * _shift_down(x, 2, row)
            + w[0:1, :] * _shift_down(x, 3, row))


def _chunk_decay(gc_c):
    gi = gc_c[:, 0:CHUNK]
    gj = gc_c.T[0:CHUNK, :]
    ri = _iota((CHUNK, CHUNK), 0)
    cj = _iota((CHUNK, CHUNK), 1)
    return jnp.where(ri >= cj, jnp.exp(jnp.minimum(gi - gj, 0.0)), 0.0), ri > cj


def _gdn_specs(t):
    col = lambda off: pl.BlockSpec((t, LANES), lambda h: (0, off + h))
    cw = lambda off: pl.BlockSpec((CONV_K, LANES), lambda h: (0, off + h))
    mat = pl.BlockSpec((1, t // CHUNK, CHUNK, CHUNK), lambda h: (h, 0, 0, 0))
    qkv = lambda k: pl.BlockSpec((t, LANES), lambda h: (0, COL_GDN // LANES + 3 * h + k))
    return col, cw, mat, qkv


def _gdn_prep(proj, convw, beta, g):
    t = proj.shape[0]
    nch = t // CHUNK

    def body(xq_ref, xk_ref, xv_ref, wq_ref, wk_ref, wv_ref, beta_ref, g_ref,
             qn_ref, kn_ref, cv_ref, gc_ref, be_ref, m_ref, a_ref):
        row = _iota((t, LANES), 0)
        hd = pl.program_id(0)
        be_ref[...] = jnp.broadcast_to(_lane_column(beta_ref[...], SM_GB + hd), (t, LANES))

        def act(x_ref, w_ref):
            y = _conv(x_ref[...], w_ref[...], row)
            return y * _sigmoid(y)

        cq = act(xq_ref, wq_ref)
        ck = act(xk_ref, wk_ref)
        cv_ref[...] = act(xv_ref, wv_ref)
        qn_ref[...] = cq * lax.rsqrt(jnp.sum(cq * cq, axis=-1, keepdims=True) + EPS) * GDN_QSCALE
        kn_ref[...] = ck * lax.rsqrt(jnp.sum(ck * ck, axis=-1, keepdims=True) + EPS)
        gc = jnp.broadcast_to(_lane_column(g_ref[...], SM_GA + hd), (t, LANES))
        pos = row % CHUNK
        step = 1
        while step < CHUNK:
            gc = gc + jnp.where(pos >= step, pltpu.roll(gc, step, 0), 0.0)
            step *= 2
        gc_ref[...] = gc

        group = 4 if nch % 4 == 0 else 1

        def chunks(gi, carry):
            ns = [gi * group + c for c in range(group)]
            sls = [pl.ds(pl.multiple_of(n * CHUNK, CHUNK), CHUNK) for n in ns]
            ks = [kn_ref[sl, :] for sl in sls]
            kk = [_mm_nt(k_c * be_ref[sl, :], k_c) for k_c, sl in zip(ks, sls)]
            qk = [_mm_nt(qn_ref[sl, :], k_c) for k_c, sl in zip(ks, sls)]
            for c, n in enumerate(ns):
                decay, strict = _chunk_decay(gc_ref[sls[c], :])
                m_ref[0, n] = jnp.where(strict, kk[c] * decay, 0.0)
                a_ref[0, n] = qk[c] * decay
            return carry

        lax.fori_loop(0, nch // group, chunks, 0)

    col, cw, mat, qkv = _gdn_specs(t)
    return pl.pallas_call(
        body, name="gdn_prep", grid=(N_GDN_HEADS,),
        in_specs=[qkv(0), qkv(1), qkv(2), cw(0), cw(4), cw(8)] + [pl.BlockSpec((t, LANES), lambda h: (0, 0))] * 2,
        out_specs=[col(0), col(0), col(0), col(0), col(0), mat, mat],
        out_shape=[_sds((t, D_GDN))] * 5 + [_sds((N_GDN_HEADS, nch, CHUNK, CHUNK))] * 2,
        compiler_params=_params("parallel"),
    )(proj, proj, proj, convw, convw, convw, beta, g)


def _tri_inverse(m3):
    assert m3.shape == (LANES, CHUNK, CHUNK)

    def body(m_ref, t_ref, ms, ts):
        for i in range(CHUNK):
            ms[i * CHUNK:(i + 1) * CHUNK, :] = m_ref[:, i, :].T
        cidx = _iota((CHUNK, LANES), 0)

        def outer(i, carry):
            def inner(jj, acc):
                mrow = ms[pl.ds(i * CHUNK + jj, 1), :]
                return acc - mrow * ts[pl.ds(pl.multiple_of(jj * CHUNK, CHUNK), CHUNK), :]

            acc = lax.fori_loop(0, i, inner, jnp.where(cidx == i, 1.0, 0.0).astype(F32))
            ts[pl.ds(pl.multiple_of(i * CHUNK, CHUNK), CHUNK), :] = acc
            return carry

        lax.fori_loop(0, CHUNK, outer, 0)
        for i in range(CHUNK):
            t_ref[:, i, :] = ts[i * CHUNK:(i + 1) * CHUNK, :].T

    return pl.pallas_call(
        body, name="tri_inverse", in_specs=[VMEM_SPEC], out_specs=VMEM_SPEC,
        out_shape=_sds((LANES, CHUNK, CHUNK)),
        scratch_shapes=[pltpu.VMEM((CHUNK * CHUNK, LANES), F32), pltpu.VMEM((CHUNK * CHUNK, LANES), F32)],
        compiler_params=_params(),
    )(m3)


def _gdn_chunk_terms(q, k, v, b, gcc):
    eg = jnp.exp(gcc)
    last = gcc[CHUNK - 1:CHUNK, :]
    egl = jnp.exp(last - gcc)
    gl = jnp.exp(last)
    kb = k * b
    return eg, egl, gl, kb, v * b, kb * eg, q * eg, k * egl


GDN_BLOCK_CHUNKS = 4


def _gdn_block_specs(t, reverse):
    cb = GDN_BLOCK_CHUNKS
    nb = t // (cb * CHUNK)
    idx = (lambda i: nb - 1 - i) if reverse else (lambda i: i)
    tok = pl.BlockSpec((cb * CHUNK, D_GDN), lambda i: (idx(i), 0))
    mat = pl.BlockSpec((N_GDN_HEADS, cb, CHUNK, CHUNK), lambda i: (0, idx(i), 0, 0))
    state = pl.BlockSpec((N_GDN_HEADS, cb, GDN_HEAD_DIM, GDN_HEAD_DIM), lambda i: (0, idx(i), 0, 0))
    return nb, tok, mat, state


def _gdn_scan(qn, kn, cv, be, gc, tinv, amat):
    t = qn.shape[0]
    nch = t // CHUNK

    def body(q_ref, k_ref, v_ref, b_ref, gc_ref, t_ref, a_ref, o_ref, sall_ref, vn_ref, s_scr):
        @pl.when(pl.program_id(0) == 0)
        def _():
            s_scr[...] = jnp.zeros_like(s_scr)

        heads = range(N_GDN_HEADS)
        cols = [slice(hd * LANES, (hd + 1) * LANES) for hd in heads]
        s = [s_scr[hd] for hd in heads]
        for cc in range(GDN_BLOCK_CHUNKS):
            rs = slice(cc * CHUNK, (cc + 1) * CHUNK)
            terms = [_gdn_chunk_terms(q_ref[rs, cs], k_ref[rs, cs], v_ref[rs, cs], b_ref[rs, cs], gc_ref[rs, cs])
                     for cs in cols]
            for hd in heads:
                sall_ref[hd, cc] = s[hd]
            uw = [_mm(t_ref[hd, cc], jnp.concatenate([terms[hd][4], terms[hd][5]], axis=1)) for hd in heads]
            ws_qs = [_mm(jnp.concatenate([uw[hd][:, LANES:], terms[hd][6]], axis=0), s[hd]) for hd in heads]
            vn = [uw[hd][:, :LANES] - ws_qs[hd][:CHUNK] for hd in heads]
            a_vn = [_mm(a_ref[hd, cc], vn[hd]) for hd in heads]
            kd_vn = [_mm_tn(terms[hd][7], vn[hd]) for hd in heads]
            for hd in heads:
                vn_ref[rs, cols[hd]] = vn[hd]
                o_ref[rs, cols[hd]] = ws_qs[hd][CHUNK:] + a_vn[hd]
                s[hd] = s[hd] * terms[hd][2] + kd_vn[hd]
        for hd in heads:
            s_scr[hd] = s[hd]

    nb, tok, mat, state = _gdn_block_specs(t, False)
    return pl.pallas_call(
        body, name="gdn_scan", grid=(nb,),
        in_specs=[tok] * 5 + [mat, mat], out_specs=[tok, state, tok],
        out_shape=[_sds((t, D_GDN)), _sds((N_GDN_HEADS, nch, GDN_HEAD_DIM, GDN_HEAD_DIM)), _sds((t, D_GDN))],
        scratch_shapes=[pltpu.VMEM((N_GDN_HEADS, GDN_HEAD_DIM, GDN_HEAD_DIM), F32)],
        compiler_params=_params("arbitrary"),
    )(qn, kn, cv, be, gc, tinv, amat)


def _gdn_bwd(qn, kn, cv, be, gc, tinv, amat, s_all, vn_all, do):
    t = qn.shape[0]

    def body(q_ref, k_ref, v_ref, b_ref, gc_ref, t_ref, a_ref, sall_ref, vn_ref, do_ref,
             dq_ref, dk_ref, dv_ref, db_ref, dg_ref, ds_scr):
        @pl.when(pl.program_id(0) == 0)
        def _():
            ds_scr[...] = jnp.zeros_like(ds_scr)

        lastrow = _iota((CHUNK, LANES), 0) == CHUNK - 1
        heads = range(N_GDN_HEADS)
        cols = [slice(hd * LANES, (hd + 1) * LANES) for hd in heads]
        each = lambda fn: [fn(hd) for hd in heads]
        rows_cat = lambda x, y: jnp.concatenate([x, y], axis=0)
        lane_cat = lambda x, y: jnp.concatenate([x, y], axis=1)
        dsp = each(lambda hd: ds_scr[hd])
        for cc in reversed(range(GDN_BLOCK_CHUNKS)):
            rs = slice(cc * CHUNK, (cc + 1) * CHUNK)
            q = each(lambda hd: q_ref[rs, cols[hd]])
            k = each(lambda hd: k_ref[rs, cols[hd]])
            v = each(lambda hd: v_ref[rs, cols[hd]])
            b = each(lambda hd: b_ref[rs, cols[hd]])
            gcc = each(lambda hd: gc_ref[rs, cols[hd]])
            do_c = each(lambda hd: do_ref[rs, cols[hd]])
            vn = each(lambda hd: vn_ref[rs, cols[hd]])
            tn = each(lambda hd: t_ref[hd, cc])
            st = each(lambda hd: sall_ref[hd, cc])
            terms = each(lambda hd: _gdn_chunk_terms(q[hd], k[hd], v[hd], b[hd], gcc[hd]))
            eg, egl, gl, kb, vb, kbg, qd, kd = [[terms[hd][i] for hd in heads] for i in range(8)]
            w = each(lambda hd: _mm(tn[hd], kbg[hd]))
            a_do = each(lambda hd: _mm_tn(a_ref[hd, cc], do_c[hd]))
            kd_ds = each(lambda hd: _mm(kd[hd], dsp[hd]))
            da = each(lambda hd: _mm_nt(do_c[hd], vn[hd]))
            dkd = each(lambda hd: _mm_nt(vn[hd], dsp[hd]))
            by_k = each(lambda hd: _mm_nt(rows_cat(kb[hd], q[hd]), k[hd]))
            dgl = each(lambda hd: jnp.sum(jnp.sum(dsp[hd] * st[hd], axis=-1, keepdims=True), axis=0, keepdims=True))
            dvn = each(lambda hd: a_do[hd] + kd_ds[hd])
            do_dvn = each(lambda hd: rows_cat(do_c[hd], dvn[hd]))
            by_s = each(lambda hd: _mm_nt(do_dvn[hd], st[hd]))
            dqd = each(lambda hd: by_s[hd][:CHUNK])
            dvn_dw = each(lambda hd: lane_cat(dvn[hd], -by_s[hd][CHUNK:]))
            dsp = each(lambda hd: _mm_tn(rows_cat(qd[hd], -w[hd]), do_dvn[hd]) + gl[hd] * dsp[hd])
            dt = each(lambda hd: _mm_nt(dvn_dw[hd], lane_cat(vb[hd], kbg[hd])))
            by_t = each(lambda hd: _mm_tn(tn[hd], dvn_dw[hd]))
            tt_dt = each(lambda hd: _mm_tn(tn[hd], dt[hd]))
            dm_raw = each(lambda hd: _mm_nt(tt_dt[hd], tn[hd]))
            masks = each(lambda hd: _chunk_decay(gcc[hd]))
            dkk = each(lambda hd: jnp.where(masks[hd][1], -dm_raw[hd], 0.0) * masks[hd][0])
            dqk = each(lambda hd: da[hd] * masks[hd][0])
            dqk_dkk = each(lambda hd: rows_cat(dqk[hd], dkk[hd]))
            on_k = each(lambda hd: _mm(dqk_dkk[hd], k[hd]))
            dk_mm = each(lambda hd: _mm_tn(dqk_dkk[hd], rows_cat(q[hd], kb[hd])))
            for hd in heads:
                cs = cols[hd]
                dvb, dkbg = by_t[hd][:, :LANES], by_t[hd][:, LANES:]
                gmat = dkk[hd] * by_k[hd][:CHUNK] + dqk[hd] * by_k[hd][CHUNK:]
                dq_ref[rs, cs] = dqd[hd] * eg[hd] + on_k[hd][:CHUNK]
                dkb = on_k[hd][CHUNK:] + dkbg * eg[hd]
                dk_ref[rs, cs] = dkd[hd] * egl[hd] + dk_mm[hd] + dkb * b[hd]
                db = jnp.sum(dkb * k[hd], axis=-1, keepdims=True) + jnp.sum(dvb * v[hd], axis=-1, keepdims=True)
                db_ref[rs, cs] = jnp.broadcast_to(db, (CHUNK, LANES))
                dv_ref[rs, cs] = dvb * b[hd]
                dkd_kd = jnp.sum(dkd[hd] * kd[hd], axis=-1, keepdims=True)
                col_sums = jnp.sum(lane_cat(gmat, jnp.zeros_like(gmat)).T, axis=-1, keepdims=True)
                dgc = (jnp.sum(gmat, axis=-1, keepdims=True) - col_sums[:CHUNK]
                       + jnp.sum(dqd[hd] * qd[hd], axis=-1, keepdims=True)
                       + jnp.sum(dkbg * kbg[hd], axis=-1, keepdims=True) - dkd_kd)
                extra = jnp.sum(dkd_kd, axis=0, keepdims=True) + dgl[hd] * gl[hd]
                dg_ref[rs, cs] = dgc + jnp.where(lastrow, extra, 0.0)
        for hd in heads:
            ds_scr[hd] = dsp[hd]
        dg = dg_ref[...]
        row = _iota(dg.shape, 0)
        pos = row % CHUNK
        step = 1
        while step < CHUNK:
            dg = dg + jnp.where(pos < CHUNK - step, pltpu.roll(dg, dg.shape[0] - step, 0), 0.0)
            step *= 2
        dg_ref[...] = dg

    nb, tok, mat, state = _gdn_block_specs(t, True)
    return pl.pallas_call(
        body, name="gdn_bwd", grid=(nb,),
        in_specs=[tok] * 5 + [mat, mat, state, tok, tok], out_specs=[tok] * 5, out_shape=[_sds((t, D_GDN))] * 5,
        scratch_shapes=[pltpu.VMEM((N_GDN_HEADS, GDN_HEAD_DIM, GDN_HEAD_DIM), F32)],
        compiler_params=_params("arbitrary"),
    )(qn, kn, cv, be, gc, tinv, amat, s_all, vn_all, do)


def _gdn_bwd_conv(proj, convw, dqn, dkn, dcv, dproj):
    t = proj.shape[0]

    def body(xq_ref, xk_ref, xv_ref, wq_ref, wk_ref, wv_ref, dq_ref, dk_ref, dv_ref, _,
             dqkv_ref, dwq_ref, dwk_ref, dwv_ref):
        row = _iota((t, LANES), 0)

        def one(x_ref, w_ref, d_ref, k, dw_ref, scale):
            x = x_ref[...]
            w = w_ref[...]
            y = _conv(x, w, row)
            sg = _sigmoid(y)
            dc = d_ref[...]
            if scale is not None:
                c = y * sg
                r = lax.rsqrt(jnp.sum(c * c, axis=-1, keepdims=True) + EPS)
                ch = c * r
                dc = scale * r * (dc - ch * jnp.sum(dc * ch, axis=-1, keepdims=True))
            dy = dc * sg * (1.0 + y * (1.0 - sg))
            dqkv_ref[:, k * LANES:(k + 1) * LANES] = (
                w[3:4, :] * dy + w[2:3, :] * _shift_up(dy, 1, row) + w[1:2, :] * _shift_up(dy, 2, row)
                + w[0:1, :] * _shift_up(dy, 3, row)).astype(BF)
            for jj in range(CONV_K):
                xs = x if jj == CONV_K - 1 else _shift_down(x, CONV_K - 1 - jj, row)
                dw_ref[jj:jj + 1, :] = jnp.sum(dy * xs, axis=0, keepdims=True)

        one(xq_ref, wq_ref, dq_ref, 0, dwq_ref, GDN_QSCALE)
        one(xk_ref, wk_ref, dk_ref, 1, dwk_ref, 1.0)
        one(xv_ref, wv_ref, dv_ref, 2, dwv_ref, None)

    col, cw, _, qkv = _gdn_specs(t)
    return pl.pallas_call(
        body, name="gdn_bwd_conv", grid=(N_GDN_HEADS,),
        in_specs=[qkv(0), qkv(1), qkv(2), cw(0), cw(4), cw(8), col(0), col(0), col(0), ANY_SPEC],
        out_specs=[pl.BlockSpec((t, QKV), lambda h: (0, COL_GDN // QKV + h)), cw(0), cw(0), cw(0)],
        out_shape=[_sds(dproj.shape, BF)] + [_sds((CONV_K, D_GDN))] * 3,
        input_output_aliases={9: 0}, compiler_params=_params("parallel"),
    )(proj, proj, proj, convw, convw, convw, dqn, dkn, dcv, dproj)


def _mix_out(fox_n, gdn_o, proj, gnw, w_out, x, pmw, plw, after):
    t = x.shape[0]
    tm = min(MATMUL_BLOCK, t)

    def body(fn_ref, go_ref, gz_ref, gnw_ref, w_ref, x_ref, pmw_ref, plw_ref, x1_ref, h2_ref, mixed_ref, omix_ref,
             h2t_ref):
        omix_ref[:, 0:D_FOX] = fn_ref[...]
        for hd in range(N_GDN_HEADS):
            cs = slice(hd * LANES, (hd + 1) * LANES)
            go = go_ref[:, cs]
            r = lax.rsqrt(jnp.mean(go * go, axis=-1, keepdims=True) + EPS)
            gz = gz_ref[:, cs]
            omix_ref[:, D_FOX + hd * LANES:D_FOX + (hd + 1) * LANES] = (
                go * r * gnw_ref[...] * (gz * _sigmoid(gz))).astype(BF)
        mixed = jnp.dot(omix_ref[...], w_ref[...], preferred_element_type=F32)
        mixed_ref[...] = mixed
        r2 = lax.rsqrt(jnp.mean(mixed * mixed, axis=-1, keepdims=True) + EPS)
        x1 = x_ref[...] + mixed * r2 * pmw_ref[...]
        x1_ref[...] = x1
        r3 = lax.rsqrt(jnp.mean(x1 * x1, axis=-1, keepdims=True) + EPS)
        h2 = x1 * r3 * plw_ref[...]
        h2_ref[...] = h2.astype(BF)
        h2t_ref[...] = h2.T.astype(BF)

    tok = lambda w: pl.BlockSpec((tm, w), lambda i: (i, 0))
    vec = lambda w: pl.BlockSpec((1, w), lambda i: (0, 0))
    return pl.pallas_call(
        _ordered(body), name="mix_out", grid=(t // tm,),
        in_specs=[ANY_SPEC, tok(D_FOX), tok(D_GDN), pl.BlockSpec((tm, D_GDN), lambda i: (i, COL_GZ // D_GDN)), vec(LANES),
                  pl.BlockSpec((D_MODEL, D_MODEL), lambda i: (0, 0)), tok(D_MODEL), vec(D_MODEL), vec(D_MODEL)],
        out_specs=[tok(D_MODEL)] * 4 + [pl.BlockSpec((D_MODEL, tm), lambda i: (0, i))],
        out_shape=[_sds((t, D_MODEL)), _sds((t, D_MODEL), BF), _sds((t, D_MODEL)), _sds((t, D_MODEL), BF),
                   _sds((D_MODEL, t), BF)],
        compiler_params=_params("parallel"),
    )(after, fox_n, gdn_o, proj, gnw, w_out, x, pmw, plw)


def _out_bwd(dmixed, w_out, o_fox, gdn_o, proj, fnw, gnw, after):
    t = dmixed.shape[0]
    tm = min(MATMUL_BLOCK, t)

    def body(dm_ref, w_ref, of_ref, go_ref, gz_ref, fnw_ref, gnw_ref, dof_ref, dgo_ref, dgz_ref, dfw_ref, dgw_ref):
        i = pl.program_id(0)

        @pl.when(i == 0)
        def _():
            dfw_ref[...] = jnp.zeros_like(dfw_ref)
            dgw_ref[...] = jnp.zeros_like(dgw_ref)

        domix = _mm_nt(dm_ref[...], w_ref[...])
        first = _iota((1, LANES), 1) < FOX_HEAD_DIM
        dfw = jnp.zeros((1, LANES), F32)
        dgw = jnp.zeros((1, LANES), F32)
        for pr in range(N_FOX_HEADS // 2):
            cs = slice(pr * LANES, (pr + 1) * LANES)
            o = of_ref[:, cs]
            dfn = domix[:, cs]
            o2 = o * o
            s0 = jnp.sum(jnp.where(first, o2, 0.0), axis=-1, keepdims=True)
            s1 = jnp.sum(jnp.where(first, 0.0, o2), axis=-1, keepdims=True)
            r = lax.rsqrt(jnp.where(first, s0, s1) * (1.0 / FOX_HEAD_DIM) + EPS)
            oh = o * r
            dfw = dfw + jnp.sum(dfn * oh, axis=0, keepdims=True)
            doh = dfn * fnw_ref[...]
            pr_ = doh * oh
            m0 = jnp.sum(jnp.where(first, pr_, 0.0), axis=-1, keepdims=True)
            m1 = jnp.sum(jnp.where(first, 0.0, pr_), axis=-1, keepdims=True)
            dof_ref[:, cs] = r * (doh - oh * jnp.where(first, m0, m1) * (1.0 / FOX_HEAD_DIM))
        for hd in range(N_GDN_HEADS):
            cs = slice(hd * LANES, (hd + 1) * LANES)
            go = go_ref[:, cs]
            gz = gz_ref[:, cs]
            dgated = domix[:, D_FOX + hd * LANES:D_FOX + (hd + 1) * LANES]
            r = lax.rsqrt(jnp.mean(go * go, axis=-1, keepdims=True) + EPS)
            goh = go * r
            sg = _sigmoid(gz)
            sz = gz * sg
            gn = goh * gnw_ref[...]
            dgn = dgated * sz
            dgz_ref[:, cs] = (dgated * gn * sg * (1.0 + gz * (1.0 - sg))).astype(BF)
            dgw = dgw + jnp.sum(dgn * goh, axis=0, keepdims=True)
            dgh = dgn * gnw_ref[...]
            dgo_ref[:, cs] = r * (dgh - goh * jnp.mean(dgh * goh, axis=-1, keepdims=True))
        dfw_ref[...] += dfw + pltpu.roll(dfw, FOX_HEAD_DIM, 1)
        dgw_ref[...] += dgw

    tok = lambda w: pl.BlockSpec((tm, w), lambda i: (i, 0))
    vec = lambda w: pl.BlockSpec((1, w), lambda i: (0, 0))
    return pl.pallas_call(
        _ordered(body), name="out_bwd", grid=(t // tm,),
        in_specs=[ANY_SPEC, tok(D_MODEL), pl.BlockSpec((D_MODEL, D_MODEL), lambda i: (0, 0)), tok(D_FOX), tok(D_GDN),
                  pl.BlockSpec((tm, D_GDN), lambda i: (i, COL_GZ // D_GDN)), vec(LANES), vec(LANES)],
        out_specs=[tok(D_FOX), tok(D_GDN), pl.BlockSpec((tm, D_GDN), lambda i: (i, COL_GZ // D_GDN)), vec(LANES),
                   vec(LANES)],
        out_shape=[_sds((t, D_FOX)), _sds((t, D_GDN)), _sds((t, PROJ_W), BF), _sds((1, LANES)), _sds((1, LANES))],
        compiler_params=_params("arbitrary"),
    )(after, dmixed, w_out, o_fox, gdn_o, proj, fnw, gnw)


def _mlp_up(h2, w_upt):
    t = h2.shape[0]
    tm = min(MATMUL_BLOCK, t)

    def body(h_ref, w_ref, up_ref):
        up_ref[...] = lax.dot_general(h_ref[...], w_ref[...], (((1,), (1,)), ((), ())),
                                      preferred_element_type=F32).astype(BF)

    return pl.pallas_call(
        body, name="mlp_up", grid=(t // tm,),
        in_specs=[pl.BlockSpec((tm, D_MODEL), lambda i: (i, 0)), pl.BlockSpec((D_FF, D_MODEL), lambda i: (0, 0))],
        out_specs=pl.BlockSpec((tm, D_FF), lambda i: (i, 0)), out_shape=_sds((t, D_FF), BF),
        compiler_params=_params("parallel"),
    )(h2, w_upt)


def _mlp_down_loss(up, w_down, x1, pw, target):
    t = up.shape[0]
    tm = min(MATMUL_BLOCK, t)

    def body(up_ref, w_ref, x1_ref, pw_ref, tg_ref, dy_ref, dx2_ref, loss_ref, dpw_ref):
        i = pl.program_id(0)

        @pl.when(i == 0)
        def _():
            loss_ref[...] = jnp.zeros_like(loss_ref)
            dpw_ref[...] = jnp.zeros_like(dpw_ref)

        u = jnp.maximum(up_ref[...].astype(F32), 0.0)
        y = jnp.dot((u * u).astype(BF), w_ref[...], preferred_element_type=F32)
        r = lax.rsqrt(jnp.mean(y * y, axis=-1, keepdims=True) + EPS)
        yh = y * r
        pw = pw_ref[...]
        err = x1_ref[...] + yh * pw - tg_ref[...]
        part = jnp.sum(jnp.sum(err * err, axis=-1, keepdims=True), axis=0, keepdims=True) * (0.5 / D_MODEL)
        loss_ref[...] += jnp.broadcast_to(part, loss_ref.shape)
        dx2 = err * (1.0 / D_MODEL)
        dx2_ref[...] = dx2
        dpw_ref[...] += jnp.sum(dx2 * yh, axis=0, keepdims=True)
        dyh = dx2 * pw
        dy_ref[...] = (r * (dyh - yh * jnp.mean(dyh * yh, axis=-1, keepdims=True))).astype(BF)

    tok = lambda w: pl.BlockSpec((tm, w), lambda i: (i, 0))
    vec = lambda w: pl.BlockSpec((1, w), lambda i: (0, 0))
    return pl.pallas_call(
        body, name="mlp_down_loss", grid=(t // tm,),
        in_specs=[tok(D_FF), pl.BlockSpec((D_FF, D_MODEL), lambda i: (0, 0)), tok(D_MODEL), vec(D_MODEL), tok(D_MODEL)],
        out_specs=[tok(D_MODEL), tok(D_MODEL), vec(LANES), vec(D_MODEL)],
        out_shape=[_sds((t, D_MODEL), BF), _sds((t, D_MODEL)), _sds((1, LANES)), _sds((1, D_MODEL))],
        compiler_params=_params("arbitrary"),
    )(up, w_down, x1, pw, target)


def _mlp_bwd_act(dy, w_down, up):
    t = dy.shape[0]
    tm = min(MATMUL_BLOCK, t)

    def body(dy_ref, w_ref, up_ref, dup_ref):
        da = lax.dot_general(dy_ref[...], w_ref[...], (((1,), (1,)), ((), ())), preferred_element_type=F32)
        dup_ref[...] = (da * (2.0 * jnp.maximum(up_ref[...].astype(F32), 0.0))).astype(BF)

    return pl.pallas_call(
        body, name="mlp_bwd_act", grid=(t // tm,),
        in_specs=[pl.BlockSpec((tm, D_MODEL), lambda i: (i, 0)), pl.BlockSpec((D_FF, D_MODEL), lambda i: (0, 0)),
                  pl.BlockSpec((tm, D_FF), lambda i: (i, 0))],
        out_specs=pl.BlockSpec((tm, D_FF), lambda i: (i, 0)), out_shape=_sds((t, D_FF), BF),
        compiler_params=_params("parallel"),
    )(dy, w_down, up)


def _mlp_bwd_in(dup, w_up, x1, plw, dx2, mixed, pmw, after):
    t = dup.shape[0]
    tm = min(MATMUL_BLOCK, t)

    def body(dup_ref, w_ref, x1_ref, plw_ref, dx2_ref, mx_ref, pmw_ref, dx1_ref, dmixed_ref, dplw_ref, dpmw_ref):
        i = pl.program_id(0)

        @pl.when(i == 0)
        def _():
            dplw_ref[...] = jnp.zeros_like(dplw_ref)
            dpmw_ref[...] = jnp.zeros_like(dpmw_ref)

        dh = jnp.dot(dup_ref[...], w_ref[...], preferred_element_type=F32)
        x1 = x1_ref[...]
        r = lax.rsqrt(jnp.mean(x1 * x1, axis=-1, keepdims=True) + EPS)
        xh = x1 * r
        dplw_ref[...] += jnp.sum(dh * xh, axis=0, keepdims=True)
        dxh = dh * plw_ref[...]
        dx1 = dx2_ref[...] + r * (dxh - xh * jnp.mean(dxh * xh, axis=-1, keepdims=True))
        dx1_ref[...] = dx1
        mx = mx_ref[...]
        r2 = lax.rsqrt(jnp.mean(mx * mx, axis=-1, keepdims=True) + EPS)
        mh = mx * r2
        dpmw_ref[...] += jnp.sum(dx1 * mh, axis=0, keepdims=True)
        dmh = dx1 * pmw_ref[...]
        dmixed_ref[...] = (r2 * (dmh - mh * jnp.mean(dmh * mh, axis=-1, keepdims=True))).astype(BF)

    tok = lambda w: pl.BlockSpec((tm, w), lambda i: (i, 0))
    vec = lambda w: pl.BlockSpec((1, w), lambda i: (0, 0))
    return pl.pallas_call(
        _ordered(body), name="mlp_bwd_in", grid=(t // tm,),
        in_specs=[ANY_SPEC, tok(D_FF), pl.BlockSpec((D_FF, D_MODEL), lambda i: (0, 0)), tok(D_MODEL),
                  vec(D_MODEL), tok(D_MODEL), tok(D_MODEL), vec(D_MODEL)],
        out_specs=[tok(D_MODEL), tok(D_MODEL), vec(D_MODEL), vec(D_MODEL)],
        out_shape=[_sds((t, D_MODEL)), _sds((t, D_MODEL), BF), _sds((1, D_MODEL)), _sds((1, D_MODEL))],
        compiler_params=_params("arbitrary"),
    )(after, dup, w_up, x1, plw, dx2, mixed, pmw)


def _wgrad(a, b, a_cols, split=1, a_fn=None, a_block0=0, name="wgrad"):
    t, b_cols = b.shape
    n_a = (a.shape[1] - a_block0 * a_cols) // a_cols if a_block0 else a.shape[1] // a_cols

    def body(a_ref, b_ref, o_ref):
        av = a_ref[...]
        if a_fn is not None:
            av = a_fn(av)
        o_ref[...] = _mm_tn(av, b_ref[...]).astype(BF).reshape(o_ref.shape)

    return pl.pallas_call(
        body, name=name, grid=(n_a,),
        in_specs=[pl.BlockSpec((t, a_cols), lambda i: (0, i + a_block0)), pl.BlockSpec((t, b_cols), lambda i: (0, 0))],
        out_specs=pl.BlockSpec((split, a_cols // split, b_cols), lambda i: (i, 0, 0)),
        out_shape=_sds((n_a * split, a_cols // split, b_cols), BF),
        compiler_params=_params("parallel"),
    )(a, b)


def _wgrad_pre_t(at, b, b_cols, name):
    rows, t = at.shape
    n_b = b.shape[1] // b_cols

    def body(a_ref, b_ref, o_ref):
        o_ref[0] = jnp.dot(a_ref[...], b_ref[...], preferred_element_type=F32).astype(BF)

    return pl.pallas_call(
        body, name=name, grid=(n_b,),
        in_specs=[pl.BlockSpec((rows, t), lambda j: (0, 0)), pl.BlockSpec((t, b_cols), lambda j: (0, j))],
        out_specs=pl.BlockSpec((1, rows, b_cols), lambda j: (j, 0, 0)), out_shape=_sds((n_b, rows, b_cols), BF),
        compiler_params=_params("parallel"),
    )(at, b)


def _small_bwd(proj, fb, al, dtb, dcq, dckt, dbe, dge, dproj):
    t = proj.shape[0]

    def body(sm_ref, fb_ref, al_ref, dtb_ref, dcq_ref, dckt_ref, dbe_ref, dge_ref, _, dsm_ref, dvec_ref):
        s = sm_ref[...]
        lane = _iota((1, LANES), 1)
        dcum = dcq_ref[...] - dckt_ref[...].T
        row = _iota((t, LANES), 0)
        step = 1
        while step < t:
            dcum = dcum + _shift_up(dcum, step, row)
            step *= 2
        dff = dcum * _sigmoid(-(s + fb_ref[...]))
        dbeta = jnp.zeros((t, LANES), F32)
        dg = jnp.zeros((t, LANES), F32)
        for hd in range(N_GDN_HEADS):
            dbeta = jnp.where(lane == SM_GB + hd, dbe_ref[:, hd * LANES:hd * LANES + 1], dbeta)
            dg = jnp.where(lane == SM_GA + hd, dge_ref[:, hd * LANES:hd * LANES + 1], dg)
        beta = _sigmoid(s)
        dgb = dbeta * beta * (1.0 - beta)
        za = s + dtb_ref[...]
        nea = -jnp.exp(al_ref[...])
        dga = dg * nea * _sigmoid(za)
        is_f = lane < SM_GB
        is_b = (lane >= SM_GB) & (lane < SM_GA)
        is_a = (lane >= SM_GA) & (lane < SM_GA + 4)
        dsm_ref[...] = jnp.where(is_f, dff, jnp.where(is_b, dgb, jnp.where(is_a, dga, 0.0))).astype(BF)
        dvec_ref[...] = jnp.zeros_like(dvec_ref)
        dvec_ref[0:1, :] = jnp.sum(jnp.where(is_f, dff, 0.0), axis=0, keepdims=True)
        dvec_ref[1:2, :] = jnp.sum(jnp.where(is_a, dg * nea * _softplus(za), 0.0), axis=0, keepdims=True)
        dvec_ref[2:3, :] = jnp.sum(jnp.where(is_a, dga, 0.0), axis=0, keepdims=True)

    vec = pl.BlockSpec((1, LANES), lambda i: (0, 0))
    full = lambda r, c: pl.BlockSpec((r, c), lambda i: (0, 0))
    small = pl.BlockSpec((t, LANES), lambda i: (0, COL_SMALL // LANES))
    return pl.pallas_call(
        body, name="small_bwd", grid=(1,),
        in_specs=[small, vec, vec, vec, full(t, LANES), full(LANES, t), full(t, 512), full(t, 512), ANY_SPEC],
        out_specs=[small, full(8, LANES)], out_shape=[_sds(dproj.shape, BF), _sds((8, LANES))],
        input_output_aliases={8: 0}, compiler_params=_params("arbitrary"),
    )(proj, fb, al, dtb, dcq, dckt, dbe, dge, dproj)


def _in_bwd(dproj, wt_al, x, nw, dx1, after):
    t = x.shape[0]
    tm = min(MATMUL_BLOCK, t)

    def body(dp_ref, w_ref, x_ref, nw_ref, dx1_ref, dx_ref, dnw_ref):
        i = pl.program_id(0)

        @pl.when(i == 0)
        def _():
            dnw_ref[...] = jnp.zeros_like(dnw_ref)

        dh = jnp.dot(dp_ref[...], w_ref[...], preferred_element_type=F32)
        xv = x_ref[...]
        r = lax.rsqrt(jnp.mean(xv * xv, axis=-1, keepdims=True) + EPS)
        xh = xv * r
        dnw_ref[...] += jnp.sum(dh * xh, axis=0, keepdims=True)
        dxh = dh * nw_ref[...]
        dx_ref[...] = dx1_ref[...] + r * (dxh - xh * jnp.mean(dxh * xh, axis=-1, keepdims=True))

    tok = lambda w: pl.BlockSpec((tm, w), lambda i: (i, 0))
    vec = lambda w: pl.BlockSpec((1, w), lambda i: (0, 0))
    return pl.pallas_call(
        _ordered(body), name="in_bwd", grid=(t // tm,),
        in_specs=[ANY_SPEC, tok(PROJ_W), pl.BlockSpec((PROJ_W, D_MODEL), lambda i: (0, 0)), tok(D_MODEL), vec(D_MODEL),
                  tok(D_MODEL)],
        out_specs=[tok(D_MODEL), vec(D_MODEL)], out_shape=[_sds((t, D_MODEL)), _sds((1, D_MODEL))],
        compiler_params=_params("arbitrary"),
    )(after, dproj, wt_al, x, nw, dx1)


def _row(v, width=None):
    v = v.reshape(1, -1).astype(F32)
    if width is not None and v.shape[1] < width:
        v = jnp.pad(v, ((0, 0), (0, width - v.shape[1])))
    return v


def _lane_vec(v, first):
    return jnp.pad(v.astype(F32), (first, LANES - first - v.shape[0])).reshape(1, LANES)


def _local_step(x, target, wt_al, started, late_weights, on_grads, convw, pre_mix_norm, fox_f_bias, fox_out_norm,
                gdn_a_log, gdn_dt_bias, gdn_out_norm, post_mix_norm, pre_mlp_norm, post_mlp_norm):
    t = x.shape[0]
    nch = t // CHUNK
    nw, pmw, plw, pw = _row(pre_mix_norm), _row(post_mix_norm), _row(pre_mlp_norm), _row(post_mlp_norm)
    fb, al, dtb = _lane_vec(fox_f_bias, SM_FF), _lane_vec(gdn_a_log, SM_GA), _lane_vec(gdn_dt_bias, SM_GA)
    fnw = _row(jnp.tile(fox_out_norm, 2))
    gnw = _row(gdn_out_norm)

    proj, h = _norm_proj(x, nw, wt_al, started)
    cumt, beta, g = _small_prep(proj, fb, al, dtb)
    qn, kn, cv, gc, be, mmat, amat = _gdn_prep(proj, convw, beta, g)
    n_prob = N_GDN_HEADS * nch
    m3 = mmat.reshape(n_prob, CHUNK, CHUNK)
    if n_prob < LANES:
        m3 = jnp.pad(m3, ((0, LANES - n_prob), (0, 0), (0, 0)))
    tinv = _tri_inverse(m3)[:n_prob].reshape(N_GDN_HEADS, nch, CHUNK, CHUNK)
    gdn_o, s_all, vn_all = _gdn_scan(qn, kn, cv, be, gc, tinv, amat)
    token = late_weights("mlp_relay", gdn_o)
    o_fox, lse, fox_n = _fox_fwd(proj, cumt, fnw, token)
    w_out = late_weights("w_out", fox_n)
    x1, h2, mixed, omix, h2t = _mix_out(fox_n, gdn_o, proj, gnw, w_out, x, pmw, plw, token)
    w_up, w_down = late_weights("mlp", h2)
    up = _mlp_up(h2, w_up)
    dy, dx2, loss, d_pw = _mlp_down_loss(up, w_down, x1, pw, target)

    dup = _mlp_bwd_act(dy, w_down, up)
    relu2 = lambda u: jnp.square(jnp.maximum(u.astype(F32), 0.0))
    g_down = _wgrad(up, dy, D_FF // N_DEV, a_fn=relu2, name="wgrad_down")
    g_up = _wgrad_pre_t(h2t, dup, D_FF // N_DEV, name="wgrad_up")
    token = on_grads("mlp", (g_up, g_down))
    dx1, dmixed, d_plw, d_pmw = _mlp_bwd_in(dup, w_up, x1, plw, dx2, mixed, pmw, token)
    token = on_grads("w_out", _wgrad(omix, dmixed, 512, split=4, name="wgrad_out"))
    do_fox, dgo, dproj, d_fnw, d_gnw = _out_bwd(dmixed, w_out, o_fox, gdn_o, proj, fnw, gnw, token)
    dproj, dcq, dckt = _fox_bwd(proj, cumt, lse, o_fox, do_fox, dproj)
    dqn, dkn, dcv, dbe, dge = _gdn_bwd(qn, kn, cv, be, gc, tinv, amat, s_all, vn_all, dgo)
    dproj, dwq, dwk, dwv = _gdn_bwd_conv(proj, convw, dqn, dkn, dcv, dproj)
    dproj, dvec = _small_bwd(proj, fb, al, dtb, dcq, dckt, dbe, dge, dproj)
    g_main = _wgrad(dproj, h, WGRAD_IN_ROWS, name="wgrad_in")
    g_tail = _wgrad(dproj, h, LANES, a_block0=COL_SMALL // LANES, name="wgrad_in_small")
    token = on_grads("w_in", (g_main, g_tail))
    grad_x, d_nw = _in_bwd(dproj, wt_al, x, nw, dx1, token)
    small = dict(norms=(d_nw, d_pmw, d_plw, d_pw), fox_out_norm=d_fnw, gdn_out_norm=d_gnw, loss=loss, vectors=dvec,
                 conv=(dwq, dwk, dwv))
    return grad_x, small


MESH_IDS = pl.DeviceIdType.MESH
CHIP_FLIPS = ((0, 0), (1, 0), (0, 1), (1, 1))


def _place():
    return lax.axis_index("x"), lax.axis_index("y"), lax.axis_index("c")


def _all_gather(blocks, later, dtype):
    n, k = len(blocks), len(later)

    def body(*refs):
        ins, shards, outs = refs[:n], refs[n:n + k], refs[n + k:2 * n + k]
        zones, to_send = refs[2 * n + k:2 * n + 2 * k], refs[2 * n + 2 * k:2 * n + 3 * k]
        stage_in, stage_out = refs[2 * n + 3 * k:2 * n + 4 * k], refs[2 * n + 4 * k:2 * n + 5 * k]
        send_sems, recv_sems, local_sems, late_sems = refs[2 * n + 5 * k:]
        x, y, c = _place()
        sibling = (x, y, 1 - c)
        chips = [(x ^ fx, y ^ fy) for fx, fy in CHIP_FLIPS[1:]]

        def slot(out, px, py, pc):
            return out.at[4 * px + 2 * py + pc]

        def copy(a, k, block, to, src=None):
            return pltpu.make_async_remote_copy(
                src_ref=slot(outs[a], *block) if src is None else src, dst_ref=slot(outs[a], *block),
                send_sem=send_sems.at[a, k], recv_sem=recv_sems.at[a, k], device_id=to, device_id_type=MESH_IDS)

        pending = []
        for a in range(n):
            mine = pltpu.make_async_copy(ins[a], slot(outs[a], x, y, c), local_sems.at[a])
            mine.start()
            pending.append(mine)
        sends = []
        for a in range(n):
            first = [copy(a, 0, (x, y, c), sibling, src=ins[a])]
            first += [copy(a, 1 + j, (x, y, c), (*chip, c), src=ins[a]) for j, chip in enumerate(chips)]
            for cp in first:
                cp.start()
            sends += first
        loads = [pltpu.make_async_copy(shards[a], stage_in[a], late_sems.at[a, 0]) for a in range(k)]
        for cp in loads:
            cp.start()
        for a, (_, transposed) in enumerate(later):
            loads[a].wait()
            val = stage_in[a][...]
            stage_out[a][...] = (val.T if transposed else val).astype(dtype)
            for j, dst in enumerate((slot(zones[a], x, y, c), to_send[a])):
                cp = pltpu.make_async_copy(stage_out[a], dst, late_sems.at[a, 1 + j])
                cp.start()
                pending.append(cp)
        for a in range(n):
            for j, chip in enumerate(chips):
                copy(a, 1 + j, (*chip, c), (x, y, c)).wait_recv()
                fwd = copy(a, 4 + j, (*chip, c), sibling)
                fwd.start()
                sends.append(fwd)
        for a in range(n):
            copy(a, 0, sibling, (x, y, c)).wait_recv()
            for j, chip in enumerate(chips):
                copy(a, 4 + j, (*chip, 1 - c), (x, y, c)).wait_recv()
        for cp in sends:
            cp.wait_send()
        for cp in pending:
            cp.wait()

    shapes = [s_.shape[::-1] if transposed else s_.shape for s_, transposed in later]
    out = pl.pallas_call(
        body, name="all_gather_weights", in_specs=[ANY_SPEC] * (n + k), out_specs=[ANY_SPEC] * (n + 2 * k),
        out_shape=[_sds((N_DEV,) + b.shape, b.dtype) for b in blocks] + [_sds((N_DEV,) + sh, dtype) for sh in shapes]
        + [_sds(sh, dtype) for sh in shapes],
        scratch_shapes=[pltpu.VMEM(s_.shape, s_.dtype) for s_, _ in later] + [pltpu.VMEM(sh, dtype) for sh in shapes]
        + [pltpu.SemaphoreType.DMA((n, 7)), pltpu.SemaphoreType.DMA((n, 7)), pltpu.SemaphoreType.DMA((n,)),
           pltpu.SemaphoreType.DMA((k, 3))],
        compiler_params=pltpu.CompilerParams(vmem_limit_bytes=VMEM_LIMIT, has_side_effects=True),
    )(*blocks, *[s_ for s_, _ in later])
    return out[:n], out[n:n + k], out[n + k:]


def _adamw(w, g, m, v):
    m = ADAM_B1 * m + (1.0 - ADAM_B1) * g
    v = ADAM_B2 * v + (1.0 - ADAM_B2) * (g * g)
    m_hat = m / (1.0 - ADAM_B1 ** ADAM_STEP)
    v_hat = v / (1.0 - ADAM_B2 ** ADAM_STEP)
    return -ADAM_LR * (m_hat / (jnp.sqrt(v_hat) + ADAM_EPS) + ADAM_WD * w), m, v


def _pair_reduce(g, name):
    _, r, c_ = g.shape
    n = len(CHIP_FLIPS)

    def body(g_ref, out_ref, sib_buf, send_sems, recv_sems):
        x, y, c = _place()
        chips = [(x ^ fx, y ^ fy) for fx, fy in CHIP_FLIPS]
        piece = lambda chip, core: g_ref.at[4 * chip[0] + 2 * chip[1] + core]
        copies = [pltpu.make_async_remote_copy(
            src_ref=piece(chip, 1 - c), dst_ref=sib_buf.at[j], send_sem=send_sems.at[j], recv_sem=recv_sems.at[j],
            device_id=(x, y, 1 - c), device_id_type=MESH_IDS) for j, chip in enumerate(chips)]
        for cp in copies:
            cp.start()
        for j, chip in enumerate(chips):
            copies[j].wait_recv()
            out_ref[j] = (piece(chip, c)[...].astype(F32) + sib_buf[j].astype(F32)).astype(BF)
        for cp in copies:
            cp.wait_send()

    return pl.pallas_call(
        body, name=name, in_specs=[VMEM_SPEC], out_specs=VMEM_SPEC, out_shape=_sds((n, r, c_), BF),
        scratch_shapes=[pltpu.VMEM((n, r, c_), BF), pltpu.SemaphoreType.DMA((n,)), pltpu.SemaphoreType.DMA((n,))],
        compiler_params=pltpu.CompilerParams(vmem_limit_bytes=VMEM_LIMIT, has_side_effects=True),
    )(g)


HBM_SPEC = pl.BlockSpec(memory_space=pltpu.HBM)
SEM_SPEC = pl.BlockSpec(memory_space=pltpu.SEMAPHORE)
DATAFLOW = pltpu.SideEffectType.DATAFLOW_SIDE_EFFECTING


def _peers():
    x, y, c = _place()
    return 4 * x + 2 * y + c, [(x ^ (k >> 2), y ^ ((k >> 1) & 1), c ^ (k & 1)) for k in range(1, N_DEV)]


def _peer_index(peer):
    return 4 * peer[0] + 2 * peer[1] + peer[2]


def _exchange_start(srcs, zones, pieces, name, chips=False):
    n = len(srcs)
    fresh = zones is None
    if fresh:
        slots = len(CHIP_FLIPS) if chips else N_DEV
        zones = [_sds((slots,) + (v.shape[1:] if pieces else v.shape), v.dtype) for v in srcs]
    n_in = n if fresh else 2 * n
    among_chips = list(chips) if isinstance(chips, (list, tuple)) else [chips] * n

    def body(*refs):
        ins, sems, token = refs[:n], refs[n_in:n_in + 2 * n], refs[-1]
        zs = refs[n_in + 3 * n:n_in + 4 * n] if fresh else refs[n:2 * n]
        me, peers = _peers()
        x, y, c = _place()
        for a in range(n):
            if among_chips[a] and pieces:
                routes = [((x ^ fx, y ^ fy, c), j, j) for j, (fx, fy) in enumerate(CHIP_FLIPS) if j]
            elif among_chips[a]:
                routes = [((x ^ fx, y ^ fy, c), None, me) for fx, fy in CHIP_FLIPS[1:]]
            else:
                routes = [(peer, _peer_index(peer) if pieces else None, me) for peer in peers]
            for peer, src_slot, dst_slot in routes:
                pltpu.make_async_remote_copy(
                    src_ref=ins[a] if src_slot is None else ins[a].at[src_slot], dst_ref=zs[a].at[dst_slot],
                    send_sem=sems[2 * a], recv_sem=sems[2 * a + 1], device_id=peer, device_id_type=MESH_IDS).start()
        token[...] = jnp.zeros_like(token)

    hbm = lambda v: pltpu.with_memory_space_constraint(v, pltpu.HBM)
    out = pl.pallas_call(
        body, name=name,
        out_shape=tuple([pltpu.SemaphoreType.DMA(())] * (2 * n) + [pltpu.HBM(v.shape, v.dtype) for v in srcs]
                        + [pltpu.HBM(z.shape, z.dtype) for z in zones] + [_sds((8, LANES))]),
        in_specs=[HBM_SPEC] * n_in, out_specs=tuple([SEM_SPEC] * (2 * n) + [HBM_SPEC] * (2 * n) + [VMEM_SPEC]),
        input_output_aliases={i: 2 * n + i for i in range(n_in)},
        compiler_params=pltpu.CompilerParams(has_side_effects=DATAFLOW),
    )(*[hbm(v) for v in srcs], *([] if fresh else [hbm(z) for z in zones]))
    return out[:2 * n], out[2 * n:3 * n], out[3 * n:4 * n], out[-1]


def _relay_start(zones, name):
    n = len(zones)

    def body(*refs):
        zs, sems, token = refs[:n], refs[n:3 * n], refs[-1]
        x, y, c = _place()
        for fx, fy in CHIP_FLIPS:
            slot = 4 * (x ^ fx) + 2 * (y ^ fy) + c
            for a in range(n):
                pltpu.make_async_remote_copy(
                    src_ref=zs[a].at[slot], dst_ref=zs[a].at[slot], send_sem=sems[2 * a], recv_sem=sems[2 * a + 1],
                    device_id=(x, y, 1 - c), device_id_type=MESH_IDS).start()
        token[...] = jnp.zeros_like(token)

    out = pl.pallas_call(
        body, name=name,
        out_shape=tuple([pltpu.SemaphoreType.DMA(())] * (2 * n) + [pltpu.HBM(z.shape, z.dtype) for z in zones]
                        + [_sds((8, LANES))]),
        in_specs=[HBM_SPEC] * n, out_specs=tuple([SEM_SPEC] * (2 * n) + [HBM_SPEC] * n + [VMEM_SPEC]),
        input_output_aliases={i: 2 * n + i for i in range(n)},
        compiler_params=pltpu.CompilerParams(has_side_effects=DATAFLOW),
    )(*[pltpu.with_memory_space_constraint(z, pltpu.HBM) for z in zones])
    return out[:2 * n], [], out[2 * n:3 * n], out[-1]


def _exchange_wait(sems, srcs, zones, after, name, chips=False, n_copies=None):
    n, n_src = len(zones), len(srcs)
    after = list(after) if isinstance(after, (list, tuple)) else [after]
    n_copies = n_copies or (len(CHIP_FLIPS) - 1 if chips else N_DEV - 1)

    def body(*refs):
        zs, sm = refs[n_src:n_src + n], refs[n_src + n:n_src + 3 * n]
        me, peers = _peers()
        for a in range(n):
            seven = zs[a].at[pl.ds(0, n_copies)]
            cp = pltpu.make_async_remote_copy(src_ref=seven, dst_ref=seven, send_sem=sm[2 * a], recv_sem=sm[2 * a + 1],
                                              device_id=peers[0], device_id_type=MESH_IDS)
            cp.wait_send()
            cp.wait_recv()

    out = pl.pallas_call(
        body, name=name, out_shape=tuple([pltpu.HBM(v.shape, v.dtype) for v in srcs] + [pltpu.HBM(z.shape, z.dtype) for z in zones]),
        in_specs=[HBM_SPEC] * (n_src + n) + [SEM_SPEC] * (2 * n) + [ANY_SPEC] * len(after),
        out_specs=tuple([HBM_SPEC] * (n_src + n)), input_output_aliases={i: i for i in range(n_src + n)},
        compiler_params=pltpu.CompilerParams(has_side_effects=DATAFLOW),
    )(*srcs, *zones, *sems, *after)
    return out[:n_src], out[n_src:]


def _sum_adamw(zone, own, w, m, v, name, chips=False):
    n_slots, r, c_ = zone.shape
    rb = next((b for b in (256, 128) if r % b == 0), r)

    def body(me_ref, z_ref, own_ref, w_ref, m_ref, v_ref, grad_ref, delta_ref, nm_ref, nv_ref):
        total = None
        for d in range(n_slots):
            part = jnp.where(me_ref[0] == d, own_ref[0], z_ref[d]).astype(F32)
            total = part if total is None else total + part
        grad_ref[...] = total
        delta_ref[...], nm_ref[...], nv_ref[...] = _adamw(w_ref[...], total, m_ref[...], v_ref[...])

    x, y, c = _place()
    mine = 0 * x if chips else 4 * x + 2 * y + c
    blk = pl.BlockSpec((rb, c_), lambda i, me_ref: (i, 0))
    return pl.pallas_call(
        body, name=name,
        grid_spec=pltpu.PrefetchScalarGridSpec(
            num_scalar_prefetch=1, grid=(r // rb,),
            in_specs=[pl.BlockSpec((n_slots, rb, c_), lambda i, me_ref: (0, i, 0)),
                      pl.BlockSpec((1, rb, c_), lambda i, me_ref: (me_ref[0], i, 0)), blk, blk, blk],
            out_specs=[blk] * 4),
        out_shape=[_sds((r, c_))] * 4, compiler_params=_params("parallel"),
    )(mine.astype(jnp.int32).reshape(1), zone, own, w, m, v)


SMALL_NORMS = ("pre_mix_norm", "post_mix_norm", "pre_mlp_norm", "post_mlp_norm")
SMALL_ORDER = SMALL_NORMS + ("fox_out_norm", "gdn_out_norm", "fox_f_bias", "gdn_a_log", "gdn_dt_bias", "gdn_conv_w")
CONV_SLAB_ROWS, CONV_SLAB_LANES = 8, 256


def _small_pack(small):
    def body(n0, n1, n2, n3, fnw_ref, gnw_ref, loss_ref, vec_ref, out_ref):
        out_ref[...] = jnp.zeros_like(out_ref)
        for i, ref in enumerate((n0, n1, n2, n3)):
            out_ref[i:i + 1, :] = ref[...]
        out_ref[4:5, 0:LANES] = fnw_ref[...]
        out_ref[4:5, LANES:2 * LANES] = gnw_ref[...]
        out_ref[4:5, 2 * LANES:3 * LANES] = loss_ref[...]
        out_ref[5:8, 0:LANES] = vec_ref[0:3, :]

    return pl.pallas_call(body, name="small_pack", in_specs=[VMEM_SPEC] * 8, out_specs=VMEM_SPEC,
                          out_shape=_sds((8, D_MODEL)))(*small["norms"], small["fox_out_norm"], small["gdn_out_norm"],
                                                        small["loss"], small["vectors"])


def _conv_slabs(dconv):
    blocks = dconv.reshape(CONV_K, N_DEV, -1).transpose(1, 0, 2)
    blocks = jnp.pad(blocks, ((0, 0), (0, CONV_SLAB_ROWS - CONV_K), (0, CONV_SLAB_LANES - blocks.shape[2])))
    return blocks.reshape(N_DEV * CONV_SLAB_ROWS, CONV_SLAB_LANES)


def _small_update(zone, conv_zone, own, own_conv, w, m, v):
    n = len(SMALL_ORDER)
    n_conv = w["gdn_conv_w"].shape[1]

    def body(me_ref, z_ref, zc_ref, own_ref, ownc_ref, *refs):
        params, loss_ref, outs, (tot, totc) = refs[:3 * n], refs[3 * n], refs[3 * n + 1:7 * n + 1], refs[-2:]
        total, total_c = None, None
        for d in range(N_DEV):
            part = jnp.where(me_ref[0] == d, own_ref[...], z_ref[d])
            part_c = jnp.where(me_ref[0] == d, ownc_ref[...], zc_ref[d])
            total, total_c = (part, part_c) if d == 0 else (total + part, total_c + part_c)
        tot[...] = total
        totc[...] = total_c
        loss_ref[...] = tot[4, 2 * LANES:2 * LANES + 1]
        mine = totc[pl.ds(pl.multiple_of(me_ref[0] * CONV_SLAB_ROWS, CONV_SLAB_ROWS), CONV_SLAB_ROWS), :]
        g = dict(zip(SMALL_NORMS, (tot[0], tot[1], tot[2], tot[3])))
        g.update(fox_out_norm=tot[4, 0:FOX_HEAD_DIM], gdn_out_norm=tot[4, LANES:LANES + GDN_HEAD_DIM],
                 fox_f_bias=tot[5, SM_FF:SM_FF + N_FOX_HEADS], gdn_a_log=tot[6, SM_GA:SM_GA + N_GDN_HEADS],
                 gdn_dt_bias=tot[7, SM_GA:SM_GA + N_GDN_HEADS], gdn_conv_w=mine[0:CONV_K, 0:n_conv])
        for i, name in enumerate(SMALL_ORDER):
            w_ref, m_ref, v_ref = params[3 * i:3 * i + 3]
            outs[4 * i][...] = g[name]
            outs[4 * i + 1][...], outs[4 * i + 2][...], outs[4 * i + 3][...] = _adamw(w_ref[...], g[name], m_ref[...],
                                                                                     v_ref[...])

    x, y, c = _place()
    operands = [a[name] for name in SMALL_ORDER for a in (w, m, v)]
    out = pl.pallas_call(
        body, name="small_update",
        in_specs=[pl.BlockSpec(memory_space=pltpu.SMEM)] + [VMEM_SPEC] * (4 + 3 * n), out_specs=[VMEM_SPEC] * (1 + 4 * n),
        out_shape=[_sds((1,))] + [_sds(w[name].shape) for name in SMALL_ORDER for _ in range(4)],
        scratch_shapes=[pltpu.VMEM(zone.shape[1:], F32), pltpu.VMEM(conv_zone.shape[1:], F32)],
    )((4 * x + 2 * y + c).astype(jnp.int32).reshape(1), zone, conv_zone, own, own_conv, *operands)
    return out[0][0], {name: out[1 + 4 * i:5 + 4 * i] for i, name in enumerate(SMALL_ORDER)}


def _native_rows():
    groups = []
    for first, n_groups in ((0, N_FOX_HEADS // 2), (D_FOX * 3 + N_FOX_HEADS, N_GDN_HEADS)):
        for g in range(n_groups):
            groups += [(first + part * n_groups * LANES + g * LANES, first + part * n_groups * LANES + (g + 1) * LANES)
                       for part in range(3)]
    return tuple(groups) + ((3088, 3600), (1536, 1544), (3080, 3088))


NATIVE_ROWS = _native_rows()


W_IN_PIECE = D_PROJ // N_DEV
WGRAD_IN_ROWS = 512
SHUFFLE_LANES = 256


def _to_aligned_moves():
    moves, o = [], 0
    for lo, hi in NATIVE_ROWS:
        r = lo
        while r < hi:
            d = r // W_IN_PIECE
            k = min(hi, (d + 1) * W_IN_PIECE) - r
            moves.append((0, d, r - d * W_IN_PIECE, 0, o, k))
            r, o = r + k, o + k
    return moves


def _from_aligned_moves():
    moves = []
    for _, d, a, _, o, k in _to_aligned_moves():
        while k:
            n = min(k, WGRAD_IN_ROWS - o % WGRAD_IN_ROWS) if o < COL_SMALL else k
            moves.append((0, o // WGRAD_IN_ROWS, o % WGRAD_IN_ROWS, d, a, n) if o < COL_SMALL else
                         (1, 0, o - COL_SMALL, d, a, n))
            o, a, k = o + n, a + n, k - n
    return moves


def _shuffle_rows(srcs, moves, out_shape, name):
    c = srcs[0].shape[-1]

    def body(*refs):
        s_refs, o_ref, s_f, o_f = refs[:len(srcs)], refs[len(srcs)], refs[len(srcs) + 1:-1], refs[-1]
        for s_ref, f in zip(s_refs, s_f):
            f[...] = s_ref[...].astype(F32)
        o_f[...] = jnp.zeros_like(o_f)
        for i, ss, so, ds, do, k in moves:
            o_f[ds, pl.ds(do, k), :] = s_f[i][ss, pl.ds(so, k), :]
        o_ref[...] = o_f[...].astype(BF)

    blk = lambda shape: pl.BlockSpec(tuple(shape[:-1]) + (SHUFFLE_LANES,), lambda j: (0, 0, j))
    scratch = lambda shape: pltpu.VMEM(tuple(shape[:-1]) + (SHUFFLE_LANES,), F32)
    return pl.pallas_call(
        body, name=name, grid=(c // SHUFFLE_LANES,), in_specs=[blk(s.shape) for s in srcs], out_specs=blk(out_shape),
        out_shape=_sds(out_shape, BF), scratch_shapes=[scratch(s.shape) for s in srcs] + [scratch(out_shape)],
        compiler_params=_params("parallel"),
    )(*srcs)


def _cols_from_pieces(p):
    return p.transpose(1, 0, 2).reshape(p.shape[1], -1)


WEIGHT_ORDER = ("pre_mix_norm", "w_in", "fox_f_bias", "fox_out_norm", "gdn_conv_w", "gdn_a_log", "gdn_dt_bias",
                "gdn_out_norm", "w_out", "post_mix_norm", "pre_mlp_norm", "w_up", "w_down", "post_mlp_norm")


def kernel(x, pre_mix_norm, w_in, fox_f_bias, fox_out_norm, gdn_conv_w, gdn_a_log, gdn_dt_bias, gdn_out_norm, w_out, post_mix_norm, pre_mlp_norm, w_up, w_down, post_mlp_norm, loss_target, m_pre_mix_norm, m_w_in, m_fox_f_bias, m_fox_out_norm, m_gdn_conv_w, m_gdn_a_log, m_gdn_dt_bias, m_gdn_out_norm, m_w_out, m_post_mix_norm, m_pre_mlp_norm, m_w_up, m_w_down, m_post_mlp_norm, v_pre_mix_norm, v_w_in, v_fox_f_bias, v_fox_out_norm, v_gdn_conv_w, v_gdn_a_log, v_gdn_dt_bias, v_gdn_out_norm, v_w_out, v_post_mix_norm, v_pre_mlp_norm, v_w_up, v_w_down, v_post_mlp_norm):
    w = dict(pre_mix_norm=pre_mix_norm, w_in=w_in, fox_f_bias=fox_f_bias, fox_out_norm=fox_out_norm,
             gdn_conv_w=gdn_conv_w, gdn_a_log=gdn_a_log, gdn_dt_bias=gdn_dt_bias, gdn_out_norm=gdn_out_norm, w_out=w_out,
             post_mix_norm=post_mix_norm, pre_mlp_norm=pre_mlp_norm, w_up=w_up, w_down=w_down, post_mlp_norm=post_mlp_norm)
    mom = dict(pre_mix_norm=m_pre_mix_norm, w_in=m_w_in, fox_f_bias=m_fox_f_bias, fox_out_norm=m_fox_out_norm,
               gdn_conv_w=m_gdn_conv_w, gdn_a_log=m_gdn_a_log, gdn_dt_bias=m_gdn_dt_bias, gdn_out_norm=m_gdn_out_norm,
               w_out=m_w_out, post_mix_norm=m_post_mix_norm, pre_mlp_norm=m_pre_mlp_norm, w_up=m_w_up, w_down=m_w_down,
               post_mlp_norm=m_post_mlp_norm)
    var = dict(pre_mix_norm=v_pre_mix_norm, w_in=v_w_in, fox_f_bias=v_fox_f_bias, fox_out_norm=v_fox_out_norm,
               gdn_conv_w=v_gdn_conv_w, gdn_a_log=v_gdn_a_log, gdn_dt_bias=v_gdn_dt_bias, gdn_out_norm=v_gdn_out_norm,
               w_out=v_w_out, post_mix_norm=v_post_mix_norm, pre_mlp_norm=v_pre_mlp_norm, w_up=v_w_up, w_down=v_w_down,
               post_mlp_norm=v_post_mlp_norm)

    (win_g, conv_g), zones, shards = _all_gather([w_in.T.astype(BF), gdn_conv_w],
                                                 [(w_out, False), (w_up, True), (w_down, False)], BF)
    wt_al = _shuffle_rows([win_g], _to_aligned_moves(), (1, PROJ_W, D_MODEL), "w_in_to_aligned")[0]
    convw = _cols_from_pieces(conv_g)
    sems, shards, zones, after = _exchange_start(shards, zones, False, "gather_start", chips=[False, True, True])
    gathers = dict(w_out=(sems[:2], shards[:1], zones[:1], after), mlp=(sems[2:], shards[1:], zones[1:], after))

    def late_weights(name, after):
        if name == "mlp_relay":
            sems, shards, zones, _ = gathers["mlp"]
            _, zones = _exchange_wait(sems, shards, zones, after, "gather_mlp_wait", chips=True)
            gathers["mlp"] = _relay_start(zones, "gather_mlp_relay")
            return gathers["mlp"][3]
        sems, shards, zones, _ = gathers[name]
        _, got = _exchange_wait(sems, shards, zones, after, "gather_" + name + "_done",
                                n_copies=len(CHIP_FLIPS) if name == "mlp" else None)
        if name == "w_out":
            return got[0].reshape(D_MODEL, D_MODEL)
        return got[0].reshape(D_FF, D_MODEL), got[1].reshape(D_FF, D_MODEL)

    scatters = {}

    def on_grads(name, g):
        if name == "mlp":
            scatters["mlp"] = list(g)
            return g[0]
        if name == "w_out":
            sems, srcs, zones, token = _exchange_start(scatters["mlp"] + [g], None, True, "scatter_mlp_w_out_start")
            scatters["mlp"] = (sems[:4], srcs[:2], zones[:2], token)
            scatters["w_out"] = (sems[4:], srcs[2:], zones[2:], token)
            return token
        g = _shuffle_rows(list(g), _from_aligned_moves(), (N_DEV, W_IN_PIECE, D_MODEL), "w_in_grad_from_aligned")
        scatters[name] = _exchange_start([_pair_reduce(g, "pair_reduce_w_in")], None, True, "scatter_w_in_start", chips=True)
        return scatters[name][3]

    grad_x, small = _local_step(
        x[0], loss_target[0], wt_al, after, late_weights, on_grads, convw, pre_mix_norm,
        fox_f_bias, fox_out_norm, gdn_a_log, gdn_dt_bias, gdn_out_norm, post_mix_norm, pre_mlp_norm, post_mlp_norm)
    slabs = [_small_pack(small), _conv_slabs(jnp.concatenate(small["conv"], axis=1))]
    scatters["small"] = _exchange_start(slabs, None, False, "small_start")

    grads, delta, new_m, new_v = {}, {}, {}, {}
    after = scatters["small"][3]
    for name, members in (("mlp", ("w_up", "w_down")), ("w_out", ("w_out",)), ("small", ()), ("w_in", ("w_in",))):
        sems, srcs, zones, _ = scatters[name]
        srcs, zones = _exchange_wait(sems, srcs, zones, after, "scatter_" + name + "_wait", chips=name == "w_in")
        if name == "small":
            loss, updated = _small_update(*zones, *srcs, w, mom, var)
            for n, res in updated.items():
                grads[n], delta[n], new_m[n], new_v[n] = res
            after = grads["pre_mix_norm"]
        for n, zone, own in zip(members, zones, srcs):
            if n == "w_in":
                res = _sum_adamw(zone, own, w[n].T, mom[n].T, var[n].T, "adamw_" + n, chips=True)
                grads[n], delta[n], new_m[n], new_v[n] = [r.T for r in res]
            else:
                grads[n], delta[n], new_m[n], new_v[n] = _sum_adamw(zone, own, w[n], mom[n], var[n], "adamw_" + n)
        if members:
            after = [grads[n] for n in members]

    return (loss, grad_x[None], *[grads[n] for n in WEIGHT_ORDER], *[delta[n] for n in WEIGHT_ORDER],
            *[new_m[n] for n in WEIGHT_ORDER], *[new_v[n] for n in WEIGHT_ORDER])
```

```python
import jax
import jax.numpy as jnp
from jax import lax
from jax.experimental import pallas as pl
from jax.experimental.pallas import tpu as pltpu

F32 = jnp.float32
BF = jnp.bfloat16

D_MODEL = 1024
N_FOX_HEADS, FOX_HEAD_DIM = 8, 64
N_GDN_HEADS, GDN_HEAD_DIM = 4, 128
D_FOX = N_FOX_HEADS * FOX_HEAD_DIM
D_GDN = N_GDN_HEADS * GDN_HEAD_DIM
CHUNK = 64
CONV_K = 4
D_FF = 4 * D_MODEL
EPS = 1e-6
D_PROJ = 3600
N_DEV = 8

PROJ_W = 3712
COL_FOX, COL_GDN, COL_GZ, COL_SMALL = 0, 1536, 3072, 3584
LANES = 128
QKV = 3 * LANES
SM_FF, SM_GB, SM_GA = 0, 8, 12

ADAM_LR, ADAM_B1, ADAM_B2, ADAM_EPS, ADAM_WD, ADAM_STEP = 0.001, 0.9, 0.999, 1e-08, 0.01, 10

TOKEN_BLOCK = 256
MATMUL_BLOCK = 512
TRI_ROWS = 4
FOX_SCALE = FOX_HEAD_DIM ** -0.5
GDN_QSCALE = GDN_HEAD_DIM ** -0.5
NEG_BIG = -1e30
VMEM_LIMIT = 56 * 1024 * 1024

VMEM_SPEC = pl.BlockSpec(memory_space=pltpu.VMEM)
ANY_SPEC = pl.BlockSpec(memory_space=pl.ANY)


def _sds(shape, dtype=F32):
    return jax.ShapeDtypeStruct(shape, dtype)


def _params(*sem):
    return pltpu.CompilerParams(dimension_semantics=sem if sem else None, vmem_limit_bytes=VMEM_LIMIT)


def _ordered(body):
    def ordered(_, *refs):
        body(*refs)

    return ordered


def _mm(a, b):
    return jnp.dot(a.astype(BF), b.astype(BF), preferred_element_type=F32)


def _mm_nt(a, b):
    return lax.dot_general(a.astype(BF), b.astype(BF), (((1,), (1,)), ((), ())), preferred_element_type=F32)


def _mm_tn(a, b):
    return lax.dot_general(a.astype(BF), b.astype(BF), (((0,), (0,)), ((), ())), preferred_element_type=F32)


def _sigmoid(x):
    return 1.0 / (1.0 + jnp.exp(-x))


def _softplus(x):
    return jnp.maximum(x, 0.0) + jnp.log1p(jnp.exp(-jnp.abs(x)))


def _iota(shape, dim):
    return lax.broadcasted_iota(jnp.int32, shape, dim)


def _shift_down(x, s, row):
    return jnp.where(row >= s, pltpu.roll(x, s, 0), 0.0)


def _shift_up(x, s, row):
    n = x.shape[0]
    return jnp.where(row < n - s, pltpu.roll(x, n - s, 0), 0.0)


def _norm_proj(x, nw, wt_al, after):
    t = x.shape[0]

    def body(x_ref, nw_ref, w_ref, proj_ref, h_ref):
        xv = x_ref[...]
        r = lax.rsqrt(jnp.mean(xv * xv, axis=-1, keepdims=True) + EPS)
        h = (xv * r * nw_ref[...]).astype(BF)
        h_ref[...] = h
        proj_ref[...] = lax.dot_general(h, w_ref[...], (((1,), (1,)), ((), ())), preferred_element_type=F32)

    tm = min(MATMUL_BLOCK, t)
    return pl.pallas_call(
        _ordered(body), name="norm_proj", grid=(t // tm,),
        in_specs=[ANY_SPEC, pl.BlockSpec((tm, D_MODEL), lambda i: (i, 0)), pl.BlockSpec((1, D_MODEL), lambda i: (0, 0)),
                  pl.BlockSpec((PROJ_W, D_MODEL), lambda i: (0, 0))],
        out_specs=[pl.BlockSpec((tm, PROJ_W), lambda i: (i, 0)), pl.BlockSpec((tm, D_MODEL), lambda i: (i, 0))],
        out_shape=[_sds((t, PROJ_W)), _sds((t, D_MODEL), BF)],
        compiler_params=_params("parallel"),
    )(after, x, nw, wt_al)


def _lane_column(x, lane):
    return jnp.sum(jnp.where(_iota((1, LANES), 1) == lane, x, 0.0), axis=-1, keepdims=True)


def _small_prep(proj, fb, al, dtb):
    t = proj.shape[0]

    def body(sm_ref, fb_ref, al_ref, dtb_ref, cumt_ref, beta_ref, g_ref):
        s = sm_ref[...]
        z = s + fb_ref[...]
        cum = jnp.minimum(z, 0.0) - jnp.log1p(jnp.exp(-jnp.abs(z)))
        row = _iota((t, LANES), 0)
        step = 1
        while step < t:
            cum = cum + _shift_down(cum, step, row)
            step *= 2
        cumt_ref[...] = cum.T
        beta_ref[...] = _sigmoid(s)
        g_ref[...] = -jnp.exp(al_ref[...]) * _softplus(s + dtb_ref[...])

    vec = pl.BlockSpec((1, LANES), lambda i: (0, 0))
    tok = pl.BlockSpec((t, LANES), lambda i: (0, 0))
    return pl.pallas_call(
        body, name="small_prep", grid=(1,),
        in_specs=[pl.BlockSpec((t, LANES), lambda i: (0, COL_SMALL // LANES)), vec, vec, vec],
        out_specs=[pl.BlockSpec((LANES, t), lambda i: (0, 0)), tok, tok],
        out_shape=[_sds((LANES, t)), _sds((t, LANES)), _sds((t, LANES))],
        compiler_params=_params("arbitrary"),
    )(proj, fb, al, dtb)


def _fox_stack(x, first):
    return jnp.concatenate([jnp.where(first, x, 0.0), jnp.where(first, 0.0, x)], axis=0).astype(BF)


def _fox_unstack(y, first):
    n = y.shape[0] // 2
    return jnp.where(first, y[:n], y[n:])


def _fox_logits(q2_i, kb, cumt_ref, pair, i, tq):
    klen = (i + 1) * tq
    s = lax.dot_general(q2_i, kb[:klen], (((1,), (1,)), ((), ())), preferred_element_type=F32)
    upper = _iota((2 * tq, 1), 0) < tq
    s = s - jnp.where(upper, cumt_ref[pl.ds(2 * pair, 1), 0:klen], cumt_ref[pl.ds(2 * pair + 1, 1), 0:klen])
    causal = _iota((2 * tq, tq), 1) <= _iota((2 * tq, tq), 0) % tq
    parts = [(s[:, :klen - tq], 0, klen - tq)] if i else []
    return parts + [(jnp.where(causal, s[:, klen - tq:], NEG_BIG), klen - tq, klen)]


def _fox_fwd(proj, cumt, fnw, after):
    t = proj.shape[0]
    tq = min(TOKEN_BLOCK, t // 2)
    nq = t // tq

    def body(q_ref, k_ref, v_ref, cumt_ref, fnw_ref, o_ref, lse_ref, fn_ref):
        j = pl.program_id(0)
        first = _iota((1, LANES), 1) < FOX_HEAD_DIM
        kb = k_ref[...].astype(BF)
        vb = v_ref[...].astype(BF)
        for i in range(nq):
            rows = slice(i * tq, (i + 1) * tq)
            q2 = _fox_stack(q_ref[rows, :] * FOX_SCALE, first)
            parts = _fox_logits(q2, kb, cumt_ref, j, i, tq)
            m = jnp.max(parts[-1][0], axis=-1, keepdims=True)
            if i:
                m = jnp.maximum(m, jnp.max(parts[0][0], axis=-1, keepdims=True))
            l = jnp.zeros((2 * tq, 1), F32)
            o = jnp.zeros((2 * tq, LANES), F32)
            for s, lo, hi in parts:
                p = jnp.exp(s - m)
                l = l + jnp.sum(p, axis=-1, keepdims=True)
                o = o + jnp.dot(p.astype(BF), vb[lo:hi], preferred_element_type=F32)
            o_acc = _fox_unstack(o / l, first)
            lse_acc = _fox_unstack(jnp.broadcast_to(m + jnp.log(l), (2 * tq, LANES)), first)
            o_ref[rows, :] = o_acc
            lse_ref[rows, :] = lse_acc
            o2 = o_acc * o_acc
            s0 = jnp.sum(jnp.where(first, o2, 0.0), axis=-1, keepdims=True)
            s1 = jnp.sum(jnp.where(first, 0.0, o2), axis=-1, keepdims=True)
            r = lax.rsqrt(jnp.where(first, s0, s1) * (1.0 / FOX_HEAD_DIM) + EPS)
            fn_ref[rows, :] = (o_acc * r * fnw_ref[...]).astype(BF)

    qkv = lambda k: pl.BlockSpec((t, LANES), lambda j: (0, COL_FOX // LANES + 3 * j + k))
    pair = pl.BlockSpec((t, LANES), lambda j: (0, j))
    return pl.pallas_call(
        _ordered(body), name="fox_fwd", grid=(N_FOX_HEADS // 2,),
        in_specs=[ANY_SPEC, qkv(0), qkv(1), qkv(2), pl.BlockSpec((LANES, t), lambda j: (0, 0)),
                  pl.BlockSpec((1, LANES), lambda j: (0, 0))],
        out_specs=[pair, pair, pair],
        out_shape=[_sds((t, D_FOX)), _sds((t, D_FOX)), _sds((t, D_FOX), BF)],
        compiler_params=_params("parallel"),
    )(after, proj, proj, proj, cumt, fnw)


def _fox_bwd(proj, cumt, lse, o, do, dproj):
    t = proj.shape[0]
    tq = min(TOKEN_BLOCK, t // 2)
    nq = t // tq

    def body(q_ref, k_ref, v_ref, cumt_ref, lse_ref, o_ref, do_ref, _, dqkv_ref, dcq_ref, dckt_ref, dk_s, dv_s):
        j = pl.program_id(0)

        @pl.when(j == 0)
        def _():
            dcq_ref[...] = jnp.zeros_like(dcq_ref)
            dckt_ref[...] = jnp.zeros_like(dckt_ref)

        lane = _iota((1, LANES), 1)

        first = _iota((1, LANES), 1) < FOX_HEAD_DIM
        kb = k_ref[...].astype(BF)
        vb = v_ref[...].astype(BF)
        dk_s[...] = jnp.zeros_like(dk_s)
        dv_s[...] = jnp.zeros_like(dv_s)
        for i in range(nq):
            rows = slice(i * tq, (i + 1) * tq)
            do_i = do_ref[rows, :]
            prod = do_i * o_ref[rows, :]
            lse_i = lse_ref[rows, :]
            q2 = _fox_stack(q_ref[rows, :] * FOX_SCALE, first)
            do2 = _fox_stack(do_i, first)
            delta = jnp.concatenate([jnp.sum(jnp.where(first, prod, 0.0), axis=-1, keepdims=True),
                                     jnp.sum(jnp.where(first, 0.0, prod), axis=-1, keepdims=True)], axis=0)
            lse2 = jnp.concatenate([lse_i[:, 0:1], lse_i[:, FOX_HEAD_DIM:FOX_HEAD_DIM + 1]], axis=0)
            dq2 = jnp.zeros((2 * tq, LANES), F32)
            dcq2 = jnp.zeros((2 * tq, 1), F32)
            for s, lo, hi in _fox_logits(q2, kb, cumt_ref, j, i, tq):
                p = jnp.exp(s - lse2)
                ds = p * (_mm_nt(do2, vb[lo:hi]) - delta)
                dsb = ds.astype(BF)
                dq2 = dq2 + jnp.dot(dsb, kb[lo:hi], preferred_element_type=F32)
                dk_s[lo:hi, :] += _mm_tn(dsb, q2)
                dv_s[lo:hi, :] += _mm_tn(p, do2)
                dcq2 = dcq2 + jnp.sum(ds, axis=-1, keepdims=True)
                dckt_ref[pl.ds(2 * j, 1), lo:hi] += jnp.sum(ds[:tq], axis=0, keepdims=True)
                dckt_ref[pl.ds(2 * j + 1, 1), lo:hi] += jnp.sum(ds[tq:], axis=0, keepdims=True)
            dqkv_ref[rows, 0:LANES] = (_fox_unstack(dq2, first) * FOX_SCALE).astype(BF)
            dcq_ref[rows, :] += jnp.where(lane == 2 * j, dcq2[:tq], jnp.where(lane == 2 * j + 1, dcq2[tq:], 0.0))
        dqkv_ref[:, LANES:2 * LANES] = dk_s[...].astype(BF)
        dqkv_ref[:, 2 * LANES:QKV] = dv_s[...].astype(BF)

    qkv = lambda k: pl.BlockSpec((t, LANES), lambda j: (0, COL_FOX // LANES + 3 * j + k))
    pair = pl.BlockSpec((t, LANES), lambda j: (0, j))
    rows128 = pl.BlockSpec((LANES, t), lambda j: (0, 0))
    return pl.pallas_call(
        body, name="fox_bwd", grid=(N_FOX_HEADS // 2,),
        in_specs=[qkv(0), qkv(1), qkv(2), rows128, pair, pair, pair, ANY_SPEC],
        out_specs=[pl.BlockSpec((t, QKV), lambda j: (0, COL_FOX // QKV + j)),
                   pl.BlockSpec((t, LANES), lambda j: (0, 0)), rows128],
        out_shape=[_sds(dproj.shape, BF), _sds((t, LANES)), _sds((LANES, t))],
        scratch_shapes=[pltpu.VMEM((t, LANES), F32), pltpu.VMEM((t, LANES), F32)],
        input_output_aliases={7: 0}, compiler_params=_params("arbitrary"),
    )(proj, proj, proj, cumt, lse, o, do, dproj)


def _conv(x, w, row):
    return (w[3:4, :] * x + w[2:3, :] * _shift_down(x, 1, row) + w[1:2, :] * _shift_down(x, 2, row)
            + w[0:1, :] * _shift_down(x, 3, row))


def _chunk_decay(gc_c):
    gi = gc_c[:, 0:CHUNK]
    gj = gc_c.T[0:CHUNK, :]
    ri = _iota((CHUNK, CHUNK), 0)
    cj = _iota((CHUNK, CHUNK), 1)
    return jnp.where(ri >= cj, jnp.exp(jnp.minimum(gi - gj, 0.0)), 0.0), ri > cj


def _gdn_specs(t):
    col = lambda off: pl.BlockSpec((t, LANES), lambda h: (0, off + h))
    cw = lambda off: pl.BlockSpec((CONV_K, LANES), lambda h: (0, off + h))
    mat = pl.BlockSpec((1, t // CHUNK, CHUNK, CHUNK), lambda h: (h, 0, 0, 0))
    qkv = lambda k: pl.BlockSpec((t, LANES), lambda h: (0, COL_GDN // LANES + 3 * h + k))
    return col, cw, mat, qkv


def _gdn_prep(proj, convw, beta, g):
    t = proj.shape[0]
    nch = t // CHUNK

    def body(xq_ref, xk_ref, xv_ref, wq_ref, wk_ref, wv_ref, beta_ref, g_ref,
             qn_ref, kn_ref, cv_ref, gc_ref, be_ref, m_ref, a_ref):
        row = _iota((t, LANES), 0)
        hd = pl.program_id(0)
        be_ref[...] = jnp.broadcast_to(_lane_column(beta_ref[...], SM_GB + hd), (t, LANES))

        def act(x_ref, w_ref):
            y = _conv(x_ref[...], w_ref[...], row)
            return y * _sigmoid(y)

        cq = act(xq_ref, wq_ref)
        ck = act(xk_ref, wk_ref)
        cv_ref[...] = act(xv_ref, wv_ref)
        qn_ref[...] = cq * lax.rsqrt(jnp.sum(cq * cq, axis=-1, keepdims=True) + EPS) * GDN_QSCALE
        kn_ref[...] = ck * lax.rsqrt(jnp.sum(ck * ck, axis=-1, keepdims=True) + EPS)
        gc = jnp.broadcast_to(_lane_column(g_ref[...], SM_GA + hd), (t, LANES))
        pos = row % CHUNK
        step = 1
        while step < CHUNK:
            gc = gc + jnp.where(pos >= step, pltpu.roll(gc, step, 0), 0.0)
            step *= 2
        gc_ref[...] = gc

        group = 4 if nch % 4 == 0 else 1

        def chunks(gi, carry):
            ns = [gi * group + c for c in range(group)]
            sls = [pl.ds(pl.multiple_of(n * CHUNK, CHUNK), CHUNK) for n in ns]
            ks = [kn_ref[sl, :] for sl in sls]
            kk = [_mm_nt(k_c * be_ref[sl, :], k_c) for k_c, sl in zip(ks, sls)]
            qk = [_mm_nt(qn_ref[sl, :], k_c) for k_c, sl in zip(ks, sls)]
            for c, n in enumerate(ns):
                decay, strict = _chunk_decay(gc_ref[sls[c], :])
                m_ref[0, n] = jnp.where(strict, kk[c] * decay, 0.0)
                a_ref[0, n] = qk[c] * decay
            return carry

        lax.fori_loop(0, nch // group, chunks, 0)

    col, cw, mat, qkv = _gdn_specs(t)
    return pl.pallas_call(
        body, name="gdn_prep", grid=(N_GDN_HEADS,),
        in_specs=[qkv(0), qkv(1), qkv(2), cw(0), cw(4), cw(8)] + [pl.BlockSpec((t, LANES), lambda h: (0, 0))] * 2,
        out_specs=[col(0), col(0), col(0), col(0), col(0), mat, mat],
        out_shape=[_sds((t, D_GDN))] * 5 + [_sds((N_GDN_HEADS, nch, CHUNK, CHUNK))] * 2,
        compiler_params=_params("parallel"),
    )(proj, proj, proj, convw, convw, convw, beta, g)


def _tri_inverse(m3):
    assert m3.shape == (LANES, CHUNK, CHUNK)

    def body(m_ref, t_ref, ms, ts):
        for i in range(CHUNK):
            ms[i * CHUNK:(i + 1) * CHUNK, :] = m_ref[:, i, :].T
        cidx = _iota((CHUNK, LANES), 0)

        def t_row(j):
            return ts[pl.ds(pl.multiple_of(j * CHUNK, CHUNK), CHUNK), :]

        def outer(ib, carry):
            i0 = ib * TRI_ROWS

            def inner(jj, accs):
                earlier = t_row(jj)
                return tuple(acc - ms[pl.ds((i0 + r) * CHUNK + jj, 1), :] * earlier for r, acc in enumerate(accs))

            accs = list(lax.fori_loop(
                0, i0, inner, tuple(jnp.where(cidx == i0 + r, 1.0, 0.0).astype(F32) for r in range(TRI_ROWS))))
            for r in range(TRI_ROWS):
                for q in range(r):
                    accs[r] = accs[r] - ms[pl.ds((i0 + r) * CHUNK + i0 + q, 1), :] * accs[q]
                ts[pl.ds(pl.multiple_of((i0 + r) * CHUNK, CHUNK), CHUNK), :] = accs[r]
            return carry

        lax.fori_loop(0, CHUNK // TRI_ROWS, outer, 0)
        for i in range(CHUNK):
            t_ref[:, i, :] = ts[i * CHUNK:(i + 1) * CHUNK, :].T

    return pl.pallas_call(
        body, name="tri_inverse", in_specs=[VMEM_SPEC], out_specs=VMEM_SPEC,
        out_shape=_sds((LANES, CHUNK, CHUNK)),
        scratch_shapes=[pltpu.VMEM((CHUNK * CHUNK, LANES), F32), pltpu.VMEM((CHUNK * CHUNK, LANES), F32)],
        compiler_params=_params(),
    )(m3)


def _gdn_chunk_terms(q, k, v, b, gcc):
    eg = jnp.exp(gcc)
    last = gcc[CHUNK - 1:CHUNK, :]
    egl = jnp.exp(last - gcc)
    gl = jnp.exp(last)
    kb = k * b
    return eg, egl, gl, kb, v * b, kb * eg, q * eg, k * egl


GDN_BLOCK_CHUNKS = 4


def _gdn_block_specs(t, reverse):
    cb = GDN_BLOCK_CHUNKS
    nb = t // (cb * CHUNK)
    idx = (lambda i: nb - 1 - i) if reverse else (lambda i: i)
    tok = pl.BlockSpec((cb * CHUNK, D_GDN), lambda i: (idx(i), 0))
    mat = pl.BlockSpec((N_GDN_HEADS, cb, CHUNK, CHUNK), lambda i: (0, idx(i), 0, 0))
    state = pl.BlockSpec((N_GDN_HEADS, cb, GDN_HEAD_DIM, GDN_HEAD_DIM), lambda i: (0, idx(i), 0, 0))
    return nb, tok, mat, state


def _gdn_scan(qn, kn, cv, be, gc, tinv, amat):
    t = qn.shape[0]
    nch = t // CHUNK

    def body(q_ref, k_ref, v_ref, b_ref, gc_ref, t_ref, a_ref, o_ref, sall_ref, vn_ref, s_scr):
        @pl.when(pl.program_id(0) == 0)
        def _():
            s_scr[...] = jnp.zeros_like(s_scr)

        heads = range(N_GDN_HEADS)
        cols = [slice(hd * LANES, (hd + 1) * LANES) for hd in heads]
        s = [s_scr[hd] for hd in heads]
        for cc in range(GDN_BLOCK_CHUNKS):
            rs = slice(cc * CHUNK, (cc + 1) * CHUNK)
            terms = [_gdn_chunk_terms(q_ref[rs, cs], k_ref[rs, cs], v_ref[rs, cs], b_ref[rs, cs], gc_ref[rs, cs])
                     for cs in cols]
            for hd in heads:
                sall_ref[hd, cc] = s[hd]
            uw = [_mm(t_ref[hd, cc], jnp.concatenate([terms[hd][4], terms[hd][5]], axis=1)) for hd in heads]
            ws_qs = [_mm(jnp.concatenate([uw[hd][:, LANES:], terms[hd][6]], axis=0), s[hd]) for hd in heads]
            vn = [uw[hd][:, :LANES] - ws_qs[hd][:CHUNK] for hd in heads]
            a_vn = [_mm(a_ref[hd, cc], vn[hd]) for hd in heads]
            kd_vn = [_mm_tn(terms[hd][7], vn[hd]) for hd in heads]
            for hd in heads:
                vn_ref[rs, cols[hd]] = vn[hd]
                o_ref[rs, cols[hd]] = ws_qs[hd][CHUNK:] + a_vn[hd]
                s[hd] = s[hd] * terms[hd][2] + kd_vn[hd]
        for hd in heads:
            s_scr[hd] = s[hd]

    nb, tok, mat, state = _gdn_block_specs(t, False)
    return pl.pallas_call(
        body, name="gdn_scan", grid=(nb,),
        in_specs=[tok] * 5 + [mat, mat], out_specs=[tok, state, tok],
        out_shape=[_sds((t, D_GDN)), _sds((N_GDN_HEADS, nch, GDN_HEAD_DIM, GDN_HEAD_DIM)), _sds((t, D_GDN))],
        scratch_shapes=[pltpu.VMEM((N_GDN_HEADS, GDN_HEAD_DIM, GDN_HEAD_DIM), F32)],
        compiler_params=_params("arbitrary"),
    )(qn, kn, cv, be, gc, tinv, amat)


def _gdn_bwd(qn, kn, cv, be, gc, tinv, amat, s_all, vn_all, do):
    t = qn.shape[0]

    def body(q_ref, k_ref, v_ref, b_ref, gc_ref, t_ref, a_ref, sall_ref, vn_ref, do_ref,
             dq_ref, dk_ref, dv_ref, db_ref, dg_ref, ds_scr):
        @pl.when(pl.program_id(0) == 0)
        def _():
            ds_scr[...] = jnp.zeros_like(ds_scr)

        lastrow = _iota((CHUNK, LANES), 0) == CHUNK - 1
        heads = range(N_GDN_HEADS)
        cols = [slice(hd * LANES, (hd + 1) * LANES) for hd in heads]
        each = lambda fn: [fn(hd) for hd in heads]
        rows_cat = lambda x, y: jnp.concatenate([x, y], axis=0)
        lane_cat = lambda x, y: jnp.concatenate([x, y], axis=1)
        dsp = each(lambda hd: ds_scr[hd])
        for cc in reversed(range(GDN_BLOCK_CHUNKS)):
            rs = slice(cc * CHUNK, (cc + 1) * CHUNK)
            q = each(lambda hd: q_ref[rs, cols[hd]])
            k = each(lambda hd: k_ref[rs, cols[hd]])
            v = each(lambda hd: v_ref[rs, cols[hd]])
            b = each(lambda hd: b_ref[rs, cols[hd]])
            gcc = each(lambda hd: gc_ref[rs, cols[hd]])
            do_c = each(lambda hd: do_ref[rs, cols[hd]])
            vn = each(lambda hd: vn_ref[rs, cols[hd]])
            tn = each(lambda hd: t_ref[hd, cc])
            st = each(lambda hd: sall_ref[hd, cc])
            terms = each(lambda hd: _gdn_chunk_terms(q[hd], k[hd], v[hd], b[hd], gcc[hd]))
            eg, egl, gl, kb, vb, kbg, qd, kd = [[terms[hd][i] for hd in heads] for i in range(8)]
            w = each(lambda hd: _mm(tn[hd], kbg[hd]))
            a_do = each(lambda hd: _mm_tn(a_ref[hd, cc], do_c[hd]))
            kd_ds = each(lambda hd: _mm(kd[hd], dsp[hd]))
            da = each(lambda hd: _mm_nt(do_c[hd], vn[hd]))
            dkd = each(lambda hd: _mm_nt(vn[hd], dsp[hd]))
            by_k = each(lambda hd: _mm_nt(rows_cat(kb[hd], q[hd]), k[hd]))
            dgl = each(lambda hd: jnp.sum(jnp.sum(dsp[hd] * st[hd], axis=-1, keepdims=True), axis=0, keepdims=True))
            dvn = each(lambda hd: a_do[hd] + kd_ds[hd])
            do_dvn = each(lambda hd: rows_cat(do_c[hd], dvn[hd]))
            by_s = each(lambda hd: _mm_nt(do_dvn[hd], st[hd]))
            dqd = each(lambda hd: by_s[hd][:CHUNK])
            dvn_dw = each(lambda hd: lane_cat(dvn[hd], -by_s[hd][CHUNK:]))
            dsp = each(lambda hd: _mm_tn(rows_cat(qd[hd], -w[hd]), do_dvn[hd]) + gl[hd] * dsp[hd])
            dt = each(lambda hd: _mm_nt(dvn_dw[hd], lane_cat(vb[hd], kbg[hd])))
            by_t = each(lambda hd: _mm_tn(tn[hd], dvn_dw[hd]))
            tt_dt = each(lambda hd: _mm_tn(tn[hd], dt[hd]))
            dm_raw = each(lambda hd: _mm_nt(tt_dt[hd], tn[hd]))
            masks = each(lambda hd: _chunk_decay(gcc[hd]))
            dkk = each(lambda hd: jnp.where(masks[hd][1], -dm_raw[hd], 0.0) * masks[hd][0])
            dqk = each(lambda hd: da[hd] * masks[hd][0])
            dqk_dkk = each(lambda hd: rows_cat(dqk[hd], dkk[hd]))
            on_k = each(lambda hd: _mm(dqk_dkk[hd], k[hd]))
            dk_mm = each(lambda hd: _mm_tn(dqk_dkk[hd], rows_cat(q[hd], kb[hd])))
            for hd in heads:
                cs = cols[hd]
                dvb, dkbg = by_t[hd][:, :LANES], by_t[hd][:, LANES:]
                gmat = dkk[hd] * by_k[hd][:CHUNK] + dqk[hd] * by_k[hd][CHUNK:]
                dq_ref[rs, cs] = dqd[hd] * eg[hd] + on_k[hd][:CHUNK]
                dkb = on_k[hd][CHUNK:] + dkbg * eg[hd]
                dk_ref[rs, cs] = dkd[hd] * egl[hd] + dk_mm[hd] + dkb * b[hd]
                db = jnp.sum(dkb * k[hd], axis=-1, keepdims=True) + jnp.sum(dvb * v[hd], axis=-1, keepdims=True)
                db_ref[rs, cs] = jnp.broadcast_to(db, (CHUNK, LANES))
                dv_ref[rs, cs] = dvb * b[hd]
                dkd_kd = jnp.sum(dkd[hd] * kd[hd], axis=-1, keepdims=True)
                col_sums = jnp.sum(lane_cat(gmat, jnp.zeros_like(gmat)).T, axis=-1, keepdims=True)
                dgc = (jnp.sum(gmat, axis=-1, keepdims=True) - col_sums[:CHUNK]
                       + jnp.sum(dqd[hd] * qd[hd], axis=-1, keepdims=True)
                       + jnp.sum(dkbg * kbg[hd], axis=-1, keepdims=True) - dkd_kd)
                extra = jnp.sum(dkd_kd, axis=0, keepdims=True) + dgl[hd] * gl[hd]
                dg_ref[rs, cs] = dgc + jnp.where(lastrow, extra, 0.0)
        for hd in heads:
            ds_scr[hd] = dsp[hd]
        dg = dg_ref[...]
        row = _iota(dg.shape, 0)
        pos = row % CHUNK
        step = 1
        while step < CHUNK:
            dg = dg + jnp.where(pos < CHUNK - step, pltpu.roll(dg, dg.shape[0] - step, 0), 0.0)
            step *= 2
        dg_ref[...] = dg

    nb, tok, mat, state = _gdn_block_specs(t, True)
    return pl.pallas_call(
        body, name="gdn_bwd", grid=(nb,),
        in_specs=[tok] * 5 + [mat, mat, state, tok, tok], out_specs=[tok] * 5, out_shape=[_sds((t, D_GDN))] * 5,
        scratch_shapes=[pltpu.VMEM((N_GDN_HEADS, GDN_HEAD_DIM, GDN_HEAD_DIM), F32)],
        compiler_params=_params("arbitrary"),
    )(qn, kn, cv, be, gc, tinv, amat, s_all, vn_all, do)


def _gdn_bwd_conv(proj, convw, dqn, dkn, dcv, dproj):
    t = proj.shape[0]

    def body(xq_ref, xk_ref, xv_ref, wq_ref, wk_ref, wv_ref, dq_ref, dk_ref, dv_ref, _,
             dqkv_ref, dwq_ref, dwk_ref, dwv_ref):
        row = _iota((t, LANES), 0)

        def one(x_ref, w_ref, d_ref, k, dw_ref, scale):
            x = x_ref[...]
            w = w_ref[...]
            y = _conv(x, w, row)
            sg = _sigmoid(y)
            dc = d_ref[...]
            if scale is not None:
                c = y * sg
                r = lax.rsqrt(jnp.sum(c * c, axis=-1, keepdims=True) + EPS)
                ch = c * r
                dc = scale * r * (dc - ch * jnp.sum(dc * ch, axis=-1, keepdims=True))
            dy = dc * sg * (1.0 + y * (1.0 - sg))
            dqkv_ref[:, k * LANES:(k + 1) * LANES] = (
                w[3:4, :] * dy + w[2:3, :] * _shift_up(dy, 1, row) + w[1:2, :] * _shift_up(dy, 2, row)
                + w[0:1, :] * _shift_up(dy, 3, row)).astype(BF)
            for jj in range(CONV_K):
                xs = x if jj == CONV_K - 1 else _shift_down(x, CONV_K - 1 - jj, row)
                dw_ref[jj:jj + 1, :] = jnp.sum(dy * xs, axis=0, keepdims=True)

        one(xq_ref, wq_ref, dq_ref, 0, dwq_ref, GDN_QSCALE)
        one(xk_ref, wk_ref, dk_ref, 1, dwk_ref, 1.0)
        one(xv_ref, wv_ref, dv_ref, 2, dwv_ref, None)

    col, cw, _, qkv = _gdn_specs(t)
    return pl.pallas_call(
        body, name="gdn_bwd_conv", grid=(N_GDN_HEADS,),
        in_specs=[qkv(0), qkv(1), qkv(2), cw(0), cw(4), cw(8), col(0), col(0), col(0), ANY_SPEC],
        out_specs=[pl.BlockSpec((t, QKV), lambda h: (0, COL_GDN // QKV + h)), cw(0), cw(0), cw(0)],
        out_shape=[_sds(dproj.shape, BF)] + [_sds((CONV_K, D_GDN))] * 3,
        input_output_aliases={9: 0}, compiler_params=_params("parallel"),
    )(proj, proj, proj, convw, convw, convw, dqn, dkn, dcv, dproj)


def _mix_out(fox_n, gdn_o, proj, gnw, w_out, x, pmw, plw, after):
    t = x.shape[0]
    tm = min(MATMUL_BLOCK, t)

    def body(fn_ref, go_ref, gz_ref, gnw_ref, w_ref, x_ref, pmw_ref, plw_ref, x1_ref, h2_ref, mixed_ref, omix_ref,
             h2t_ref):
        omix_ref[:, 0:D_FOX] = fn_ref[...]
        for hd in range(N_GDN_HEADS):
            cs = slice(hd * LANES, (hd + 1) * LANES)
            go = go_ref[:, cs]
            r = lax.rsqrt(jnp.mean(go * go, axis=-1, keepdims=True) + EPS)
            gz = gz_ref[:, cs]
            omix_ref[:, D_FOX + hd * LANES:D_FOX + (hd + 1) * LANES] = (
                go * r * gnw_ref[...] * (gz * _sigmoid(gz))).astype(BF)
        mixed = jnp.dot(omix_ref[...], w_ref[...], preferred_element_type=F32)
        mixed_ref[...] = mixed
        r2 = lax.rsqrt(jnp.mean(mixed * mixed, axis=-1, keepdims=True) + EPS)
        x1 = x_ref[...] + mixed * r2 * pmw_ref[...]
        x1_ref[...] = x1
        r3 = lax.rsqrt(jnp.mean(x1 * x1, axis=-1, keepdims=True) + EPS)
        h2 = x1 * r3 * plw_ref[...]
        h2_ref[...] = h2.astype(BF)
        h2t_ref[...] = h2.T.astype(BF)

    tok = lambda w: pl.BlockSpec((tm, w), lambda i: (i, 0))
    vec = lambda w: pl.BlockSpec((1, w), lambda i: (0, 0))
    return pl.pallas_call(
        _ordered(body), name="mix_out", grid=(t // tm,),
        in_specs=[ANY_SPEC, tok(D_FOX), tok(D_GDN), pl.BlockSpec((tm, D_GDN), lambda i: (i, COL_GZ // D_GDN)), vec(LANES),
                  pl.BlockSpec((D_MODEL, D_MODEL), lambda i: (0, 0)), tok(D_MODEL), vec(D_MODEL), vec(D_MODEL)],
        out_specs=[tok(D_MODEL)] * 4 + [pl.BlockSpec((D_MODEL, tm), lambda i: (0, i))],
        out_shape=[_sds((t, D_MODEL)), _sds((t, D_MODEL), BF), _sds((t, D_MODEL)), _sds((t, D_MODEL), BF),
                   _sds((D_MODEL, t), BF)],
        compiler_params=_params("parallel"),
    )(after, fox_n, gdn_o, proj, gnw, w_out, x, pmw, plw)


def _out_bwd(dmixed, w_out, o_fox, gdn_o, proj, fnw, gnw, after):
    t = dmixed.shape[0]
    tm = min(MATMUL_BLOCK, t)

    def body(dm_ref, w_ref, of_ref, go_ref, gz_ref, fnw_ref, gnw_ref, dof_ref, dgo_ref, dgz_ref, dfw_ref, dgw_ref):
        i = pl.program_id(0)

        @pl.when(i == 0)
        def _():
            dfw_ref[...] = jnp.zeros_like(dfw_ref)
            dgw_ref[...] = jnp.zeros_like(dgw_ref)

        domix = _mm_nt(dm_ref[...], w_ref[...])
        first = _iota((1, LANES), 1) < FOX_HEAD_DIM
        dfw = jnp.zeros((1, LANES), F32)
        dgw = jnp.zeros((1, LANES), F32)
        for pr in range(N_FOX_HEADS // 2):
            cs = slice(pr * LANES, (pr + 1) * LANES)
            o = of_ref[:, cs]
            dfn = domix[:, cs]
            o2 = o * o
            s0 = jnp.sum(jnp.where(first, o2, 0.0), axis=-1, keepdims=True)
            s1 = jnp.sum(jnp.where(first, 0.0, o2), axis=-1, keepdims=True)
            r = lax.rsqrt(jnp.where(first, s0, s1) * (1.0 / FOX_HEAD_DIM) + EPS)
            oh = o * r
            dfw = dfw + jnp.sum(dfn * oh, axis=0, keepdims=True)
            doh = dfn * fnw_ref[...]
            pr_ = doh * oh
            m0 = jnp.sum(jnp.where(first, pr_, 0.0), axis=-1, keepdims=True)
            m1 = jnp.sum(jnp.where(first, 0.0, pr_), axis=-1, keepdims=True)
            dof_ref[:, cs] = r * (doh - oh * jnp.where(first, m0, m1) * (1.0 / FOX_HEAD_DIM))
        for hd in range(N_GDN_HEADS):
            cs = slice(hd * LANES, (hd + 1) * LANES)
            go = go_ref[:, cs]
            gz = gz_ref[:, cs]
            dgated = domix[:, D_FOX + hd * LANES:D_FOX + (hd + 1) * LANES]
            r = lax.rsqrt(jnp.mean(go * go, axis=-1, keepdims=True) + EPS)
            goh = go * r
            sg = _sigmoid(gz)
            sz = gz * sg
            gn = goh * gnw_ref[...]
            dgn = dgated * sz
            dgz_ref[:, cs] = (dgated * gn * sg * (1.0 + gz * (1.0 - sg))).astype(BF)
            dgw = dgw + jnp.sum(dgn * goh, axis=0, keepdims=True)
            dgh = dgn * gnw_ref[...]
            dgo_ref[:, cs] = r * (dgh - goh * jnp.mean(dgh * goh, axis=-1, keepdims=True))
        dfw_ref[...] += dfw + pltpu.roll(dfw, FOX_HEAD_DIM, 1)
        dgw_ref[...] += dgw

    tok = lambda w: pl.BlockSpec((tm, w), lambda i: (i, 0))
    vec = lambda w: pl.BlockSpec((1, w), lambda i: (0, 0))
    return pl.pallas_call(
        _ordered(body), name="out_bwd", grid=(t // tm,),
        in_specs=[ANY_SPEC, tok(D_MODEL), pl.BlockSpec((D_MODEL, D_MODEL), lambda i: (0, 0)), tok(D_FOX), tok(D_GDN),
                  pl.BlockSpec((tm, D_GDN), lambda i: (i, COL_GZ // D_GDN)), vec(LANES), vec(LANES)],
        out_specs=[tok(D_FOX), tok(D_GDN), pl.BlockSpec((tm, D_GDN), lambda i: (i, COL_GZ // D_GDN)), vec(LANES),
                   vec(LANES)],
        out_shape=[_sds((t, D_FOX)), _sds((t, D_GDN)), _sds((t, PROJ_W), BF), _sds((1, LANES)), _sds((1, LANES))],
        compiler_params=_params("arbitrary"),
    )(after, dmixed, w_out, o_fox, gdn_o, proj, fnw, gnw)


def _mlp_up(h2, w_upt):
    t = h2.shape[0]
    tm = min(MATMUL_BLOCK, t)

    def body(h_ref, w_ref, up_ref):
        up_ref[...] = lax.dot_general(h_ref[...], w_ref[...], (((1,), (1,)), ((), ())),
                                      preferred_element_type=F32).astype(BF)

    return pl.pallas_call(
        body, name="mlp_up", grid=(t // tm,),
        in_specs=[pl.BlockSpec((tm, D_MODEL), lambda i: (i, 0)), pl.BlockSpec((D_FF, D_MODEL), lambda i: (0, 0))],
        out_specs=pl.BlockSpec((tm, D_FF), lambda i: (i, 0)), out_shape=_sds((t, D_FF), BF),
        compiler_params=_params("parallel"),
    )(h2, w_upt)


def _mlp_down_loss(up, w_down, x1, pw, target):
    t = up.shape[0]
    tm = min(MATMUL_BLOCK, t)

    def body(up_ref, w_ref, x1_ref, pw_ref, tg_ref, dy_ref, dx2_ref, loss_ref, dpw_ref):
        i = pl.program_id(0)

        @pl.when(i == 0)
        def _():
            loss_ref[...] = jnp.zeros_like(loss_ref)
            dpw_ref[...] = jnp.zeros_like(dpw_ref)

        u = jnp.maximum(up_ref[...].astype(F32), 0.0)
        y = jnp.dot((u * u).astype(BF), w_ref[...], preferred_element_type=F32)
        r = lax.rsqrt(jnp.mean(y * y, axis=-1, keepdims=True) + EPS)
        yh = y * r
        pw = pw_ref[...]
        err = x1_ref[...] + yh * pw - tg_ref[...]
        part = jnp.sum(jnp.sum(err * err, axis=-1, keepdims=True), axis=0, keepdims=True) * (0.5 / D_MODEL)
        loss_ref[...] += jnp.broadcast_to(part, loss_ref.shape)
        dx2 = err * (1.0 / D_MODEL)
        dx2_ref[...] = dx2
        dpw_ref[...] += jnp.sum(dx2 * yh, axis=0, keepdims=True)
        dyh = dx2 * pw
        dy_ref[...] = (r * (dyh - yh * jnp.mean(dyh * yh, axis=-1, keepdims=True))).astype(BF)

    tok = lambda w: pl.BlockSpec((tm, w), lambda i: (i, 0))
    vec = lambda w: pl.BlockSpec((1, w), lambda i: (0, 0))
    return pl.pallas_call(
        body, name="mlp_down_loss", grid=(t // tm,),
        in_specs=[tok(D_FF), pl.BlockSpec((D_FF, D_MODEL), lambda i: (0, 0)), tok(D_MODEL), vec(D_MODEL), tok(D_MODEL)],
        out_specs=[tok(D_MODEL), tok(D_MODEL), vec(LANES), vec(D_MODEL)],
        out_shape=[_sds((t, D_MODEL), BF), _sds((t, D_MODEL)), _sds((1, LANES)), _sds((1, D_MODEL))],
        compiler_params=_params("arbitrary"),
    )(up, w_down, x1, pw, target)


def _mlp_bwd_act(dy, w_down, up):
    t = dy.shape[0]
    tm = min(MATMUL_BLOCK, t)

    def body(dy_ref, w_ref, up_ref, dup_ref):
        da = lax.dot_general(dy_ref[...], w_ref[...], (((1,), (1,)), ((), ())), preferred_element_type=F32)
        dup_ref[...] = (da * (2.0 * jnp.maximum(up_ref[...].astype(F32), 0.0))).astype(BF)

    return pl.pallas_call(
        body, name="mlp_bwd_act", grid=(t // tm,),
        in_specs=[pl.BlockSpec((tm, D_MODEL), lambda i: (i, 0)), pl.BlockSpec((D_FF, D_MODEL), lambda i: (0, 0)),
                  pl.BlockSpec((tm, D_FF), lambda i: (i, 0))],
        out_specs=pl.BlockSpec((tm, D_FF), lambda i: (i, 0)), out_shape=_sds((t, D_FF), BF),
        compiler_params=_params("parallel"),
    )(dy, w_down, up)


def _mlp_bwd_in(dup, w_up, x1, plw, dx2, mixed, pmw, after):
    t = dup.shape[0]
    tm = min(MATMUL_BLOCK, t)

    def body(dup_ref, w_ref, x1_ref, plw_ref, dx2_ref, mx_ref, pmw_ref, dx1_ref, dmixed_ref, dplw_ref, dpmw_ref):
        i = pl.program_id(0)

        @pl.when(i == 0)
        def _():
            dplw_ref[...] = jnp.zeros_like(dplw_ref)
            dpmw_ref[...] = jnp.zeros_like(dpmw_ref)

        dh = jnp.dot(dup_ref[...], w_ref[...], preferred_element_type=F32)
        x1 = x1_ref[...]
        r = lax.rsqrt(jnp.mean(x1 * x1, axis=-1, keepdims=True) + EPS)
        xh = x1 * r
        dplw_ref[...] += jnp.sum(dh * xh, axis=0, keepdims=True)
        dxh = dh * plw_ref[...]
        dx1 = dx2_ref[...] + r * (dxh - xh * jnp.mean(dxh * xh, axis=-1, keepdims=True))
        dx1_ref[...] = dx1
        mx = mx_ref[...]
        r2 = lax.rsqrt(jnp.mean(mx * mx, axis=-1, keepdims=True) + EPS)
        mh = mx * r2
        dpmw_ref[...] += jnp.sum(dx1 * mh, axis=0, keepdims=True)
        dmh = dx1 * pmw_ref[...]
        dmixed_ref[...] = (r2 * (dmh - mh * jnp.mean(dmh * mh, axis=-1, keepdims=True))).astype(BF)

    tok = lambda w: pl.BlockSpec((tm, w), lambda i: (i, 0))
    vec = lambda w: pl.BlockSpec((1, w), lambda i: (0, 0))
    return pl.pallas_call(
        _ordered(body), name="mlp_bwd_in", grid=(t // tm,),
        in_specs=[ANY_SPEC, tok(D_FF), pl.BlockSpec((D_FF, D_MODEL), lambda i: (0, 0)), tok(D_MODEL),
                  vec(D_MODEL), tok(D_MODEL), tok(D_MODEL), vec(D_MODEL)],
        out_specs=[tok(D_MODEL), tok(D_MODEL), vec(D_MODEL), vec(D_MODEL)],
        out_shape=[_sds((t, D_MODEL)), _sds((t, D_MODEL), BF), _sds((1, D_MODEL)), _sds((1, D_MODEL))],
        compiler_params=_params("arbitrary"),
    )(after, dup, w_up, x1, plw, dx2, mixed, pmw)


def _wgrad(a, b, a_cols, split=1, a_fn=None, a_block0=0, name="wgrad"):
    t, b_cols = b.shape
    n_a = (a.shape[1] - a_block0 * a_cols) // a_cols if a_block0 else a.shape[1] // a_cols

    def body(a_ref, b_ref, o_ref):
        av = a_ref[...]
        if a_fn is not None:
            av = a_fn(av)
        o_ref[...] = _mm_tn(av, b_ref[...]).astype(BF).reshape(o_ref.shape)

    return pl.pallas_call(
        body, name=name, grid=(n_a,),
        in_specs=[pl.BlockSpec((t, a_cols), lambda i: (0, i + a_block0)), pl.BlockSpec((t, b_cols), lambda i: (0, 0))],
        out_specs=pl.BlockSpec((split, a_cols // split, b_cols), lambda i: (i, 0, 0)),
        out_shape=_sds((n_a * split, a_cols // split, b_cols), BF),
        compiler_params=_params("parallel"),
    )(a, b)


def _wgrad_pre_t(at, b, b_cols, name):
    rows, t = at.shape
    n_b = b.shape[1] // b_cols

    def body(a_ref, b_ref, o_ref):
        o_ref[0] = jnp.dot(a_ref[...], b_ref[...], preferred_element_type=F32).astype(BF)

    return pl.pallas_call(
        body, name=name, grid=(n_b,),
        in_specs=[pl.BlockSpec((rows, t), lambda j: (0, 0)), pl.BlockSpec((t, b_cols), lambda j: (0, j))],
        out_specs=pl.BlockSpec((1, rows, b_cols), lambda j: (j, 0, 0)), out_shape=_sds((n_b, rows, b_cols), BF),
        compiler_params=_params("parallel"),
    )(at, b)


def _small_bwd(proj, fb, al, dtb, dcq, dckt, dbe, dge, dproj):
    t = proj.shape[0]

    def body(sm_ref, fb_ref, al_ref, dtb_ref, dcq_ref, dckt_ref, dbe_ref, dge_ref, _, dsm_ref, dvec_ref):
        s = sm_ref[...]
        lane = _iota((1, LANES), 1)
        dcum = dcq_ref[...] - dckt_ref[...].T
        row = _iota((t, LANES), 0)
        step = 1
        while step < t:
            dcum = dcum + _shift_up(dcum, step, row)
            step *= 2
        dff = dcum * _sigmoid(-(s + fb_ref[...]))
        dbeta = jnp.zeros((t, LANES), F32)
        dg = jnp.zeros((t, LANES), F32)
        for hd in range(N_GDN_HEADS):
            dbeta = jnp.where(lane == SM_GB + hd, dbe_ref[:, hd * LANES:hd * LANES + 1], dbeta)
            dg = jnp.where(lane == SM_GA + hd, dge_ref[:, hd * LANES:hd * LANES + 1], dg)
        beta = _sigmoid(s)
        dgb = dbeta * beta * (1.0 - beta)
        za = s + dtb_ref[...]
        nea = -jnp.exp(al_ref[...])
        dga = dg * nea * _sigmoid(za)
        is_f = lane < SM_GB
        is_b = (lane >= SM_GB) & (lane < SM_GA)
        is_a = (lane >= SM_GA) & (lane < SM_GA + 4)
        dsm_ref[...] = jnp.where(is_f, dff, jnp.where(is_b, dgb, jnp.where(is_a, dga, 0.0))).astype(BF)
        dvec_ref[...] = jnp.zeros_like(dvec_ref)
        dvec_ref[0:1, :] = jnp.sum(jnp.where(is_f, dff, 0.0), axis=0, keepdims=True)
        dvec_ref[1:2, :] = jnp.sum(jnp.where(is_a, dg * nea * _softplus(za), 0.0), axis=0, keepdims=True)
        dvec_ref[2:3, :] = jnp.sum(jnp.where(is_a, dga, 0.0), axis=0, keepdims=True)

    vec = pl.BlockSpec((1, LANES), lambda i: (0, 0))
    full = lambda r, c: pl.BlockSpec((r, c), lambda i: (0, 0))
    small = pl.BlockSpec((t, LANES), lambda i: (0, COL_SMALL // LANES))
    return pl.pallas_call(
        body, name="small_bwd", grid=(1,),
        in_specs=[small, vec, vec, vec, full(t, LANES), full(LANES, t), full(t, 512), full(t, 512), ANY_SPEC],
        out_specs=[small, full(8, LANES)], out_shape=[_sds(dproj.shape, BF), _sds((8, LANES))],
        input_output_aliases={8: 0}, compiler_params=_params("arbitrary"),
    )(proj, fb, al, dtb, dcq, dckt, dbe, dge, dproj)


def _in_bwd(dproj, wt_al, x, nw, dx1, after):
    t = x.shape[0]
    tm = min(MATMUL_BLOCK, t)

    def body(dp_ref, w_ref, x_ref, nw_ref, dx1_ref, dx_ref, dnw_ref):
        i = pl.program_id(0)

        @pl.when(i == 0)
        def _():
            dnw_ref[...] = jnp.zeros_like(dnw_ref)

        dh = jnp.dot(dp_ref[...], w_ref[...], preferred_element_type=F32)
        xv = x_ref[...]
        r = lax.rsqrt(jnp.mean(xv * xv, axis=-1, keepdims=True) + EPS)
        xh = xv * r
        dnw_ref[...] += jnp.sum(dh * xh, axis=0, keepdims=True)
        dxh = dh * nw_ref[...]
        dx_ref[...] = dx1_ref[...] + r * (dxh - xh * jnp.mean(dxh * xh, axis=-1, keepdims=True))

    tok = lambda w: pl.BlockSpec((tm, w), lambda i: (i, 0))
    vec = lambda w: pl.BlockSpec((1, w), lambda i: (0, 0))
    return pl.pallas_call(
        _ordered(body), name="in_bwd", grid=(t // tm,),
        in_specs=[ANY_SPEC, tok(PROJ_W), pl.BlockSpec((PROJ_W, D_MODEL), lambda i: (0, 0)), tok(D_MODEL), vec(D_MODEL),
                  tok(D_MODEL)],
        out_specs=[tok(D_MODEL), vec(D_MODEL)], out_shape=[_sds((t, D_MODEL)), _sds((1, D_MODEL))],
        compiler_params=_params("arbitrary"),
    )(after, dproj, wt_al, x, nw, dx1)


def _row(v, width=None):
    v = v.reshape(1, -1).astype(F32)
    if width is not None and v.shape[1] < width:
        v = jnp.pad(v, ((0, 0), (0, width - v.shape[1])))
    return v


def _lane_vec(v, first):
    return jnp.pad(v.astype(F32), (first, LANES - first - v.shape[0])).reshape(1, LANES)


def _local_step(x, target, wt_al, started, late_weights, on_grads, convw, pre_mix_norm, fox_f_bias, fox_out_norm,
                gdn_a_log, gdn_dt_bias, gdn_out_norm, post_mix_norm, pre_mlp_norm, post_mlp_norm):
    t = x.shape[0]
    nch = t // CHUNK
    nw, pmw, plw, pw = _row(pre_mix_norm), _row(post_mix_norm), _row(pre_mlp_norm), _row(post_mlp_norm)
    fb, al, dtb = _lane_vec(fox_f_bias, SM_FF), _lane_vec(gdn_a_log, SM_GA), _lane_vec(gdn_dt_bias, SM_GA)
    fnw = _row(jnp.tile(fox_out_norm, 2))
    gnw = _row(gdn_out_norm)

    proj, h = _norm_proj(x, nw, wt_al, started)
    cumt, beta, g = _small_prep(proj, fb, al, dtb)
    qn, kn, cv, gc, be, mmat, amat = _gdn_prep(proj, convw, beta, g)
    n_prob = N_GDN_HEADS * nch
    m3 = mmat.reshape(n_prob, CHUNK, CHUNK)
    if n_prob < LANES:
        m3 = jnp.pad(m3, ((0, LANES - n_prob), (0, 0), (0, 0)))
    tinv = _tri_inverse(m3)[:n_prob].reshape(N_GDN_HEADS, nch, CHUNK, CHUNK)
    gdn_o, s_all, vn_all = _gdn_scan(qn, kn, cv, be, gc, tinv, amat)
    token = late_weights("mlp_relay", gdn_o)
    o_fox, lse, fox_n = _fox_fwd(proj, cumt, fnw, token)
    w_out = late_weights("w_out", fox_n)
    x1, h2, mixed, omix, h2t = _mix_out(fox_n, gdn_o, proj, gnw, w_out, x, pmw, plw, token)
    w_up, w_down = late_weights("mlp", h2)
    up = _mlp_up(h2, w_up)
    dy, dx2, loss, d_pw = _mlp_down_loss(up, w_down, x1, pw, target)

    dup = _mlp_bwd_act(dy, w_down, up)
    relu2 = lambda u: jnp.square(jnp.maximum(u.astype(F32), 0.0))
    g_down = _wgrad(up, dy, D_FF // N_DEV, a_fn=relu2, name="wgrad_down")
    g_up = _wgrad_pre_t(h2t, dup, D_FF // N_DEV, name="wgrad_up")
    token = on_grads("mlp", (g_up, g_down))
    dx1, dmixed, d_plw, d_pmw = _mlp_bwd_in(dup, w_up, x1, plw, dx2, mixed, pmw, token)
    token = on_grads("w_out", _wgrad(omix, dmixed, 512, split=4, name="wgrad_out"))
    do_fox, dgo, dproj, d_fnw, d_gnw = _out_bwd(dmixed, w_out, o_fox, gdn_o, proj, fnw, gnw, token)
    dproj, dcq, dckt = _fox_bwd(proj, cumt, lse, o_fox, do_fox, dproj)
    dqn, dkn, dcv, dbe, dge = _gdn_bwd(qn, kn, cv, be, gc, tinv, amat, s_all, vn_all, dgo)
    dproj, dwq, dwk, dwv = _gdn_bwd_conv(proj, convw, dqn, dkn, dcv, dproj)
    dproj, dvec = _small_bwd(proj, fb, al, dtb, dcq, dckt, dbe, dge, dproj)
    g_main = _wgrad(dproj, h, WGRAD_IN_ROWS, name="wgrad_in")
    g_tail = _wgrad(dproj, h, LANES, a_block0=COL_SMALL // LANES, name="wgrad_in_small")
    token = on_grads("w_in", (g_main, g_tail))
    grad_x, d_nw = _in_bwd(dproj, wt_al, x, nw, dx1, token)
    small = dict(norms=(d_nw, d_pmw, d_plw, d_pw), fox_out_norm=d_fnw, gdn_out_norm=d_gnw, loss=loss, vectors=dvec,
                 conv=(dwq, dwk, dwv))
    return grad_x, small


MESH_IDS = pl.DeviceIdType.MESH
CHIP_FLIPS = ((0, 0), (1, 0), (0, 1), (1, 1))


def _place():
    return lax.axis_index("x"), lax.axis_index("y"), lax.axis_index("c")


def _all_gather(blocks, later, dtype):
    n, k = len(blocks), len(later)

    def body(*refs):
        ins, shards, outs = refs[:n], refs[n:n + k], refs[n + k:2 * n + k]
        zones, to_send = refs[2 * n + k:2 * n + 2 * k], refs[2 * n + 2 * k:2 * n + 3 * k]
        stage_in, stage_out = refs[2 * n + 3 * k:2 * n + 4 * k], refs[2 * n + 4 * k:2 * n + 5 * k]
        send_sems, recv_sems, local_sems, late_sems = refs[2 * n + 5 * k:]
        x, y, c = _place()
        sibling = (x, y, 1 - c)
        chips = [(x ^ fx, y ^ fy) for fx, fy in CHIP_FLIPS[1:]]

        def slot(out, px, py, pc):
            return out.at[4 * px + 2 * py + pc]

        def copy(a, k, block, to, src=None):
            return pltpu.make_async_remote_copy(
                src_ref=slot(outs[a], *block) if src is None else src, dst_ref=slot(outs[a], *block),
                send_sem=send_sems.at[a, k], recv_sem=recv_sems.at[a, k], device_id=to, device_id_type=MESH_IDS)

        pending = []
        for a in range(n):
            mine = pltpu.make_async_copy(ins[a], slot(outs[a], x, y, c), local_sems.at[a])
            mine.start()
            pending.append(mine)
        sends = []
        for a in range(n):
            first = [copy(a, 0, (x, y, c), sibling, src=ins[a])]
            first += [copy(a, 1 + j, (x, y, c), (*chip, c), src=ins[a]) for j, chip in enumerate(chips)]
            for cp in first:
                cp.start()
            sends += first
        loads = [pltpu.make_async_copy(shards[a], stage_in[a], late_sems.at[a, 0]) for a in range(k)]
        for cp in loads:
            cp.start()
        for a, (_, transposed) in enumerate(later):
            loads[a].wait()
            val = stage_in[a][...]
            stage_out[a][...] = (val.T if transposed else val).astype(dtype)
            for j, dst in enumerate((slot(zones[a], x, y, c), to_send[a])):
                cp = pltpu.make_async_copy(stage_out[a], dst, late_sems.at[a, 1 + j])
                cp.start()
                pending.append(cp)
        for a in range(n):
            for j, chip in enumerate(chips):
                copy(a, 1 + j, (*chip, c), (x, y, c)).wait_recv()
                fwd = copy(a, 4 + j, (*chip, c), sibling)
                fwd.start()
                sends.append(fwd)
        for a in range(n):
            copy(a, 0, sibling, (x, y, c)).wait_recv()
            for j, chip in enumerate(chips):
                copy(a, 4 + j, (*chip, 1 - c), (x, y, c)).wait_recv()
        for cp in sends:
            cp.wait_send()
        for cp in pending:
            cp.wait()

    shapes = [s_.shape[::-1] if transposed else s_.shape for s_, transposed in later]
    out = pl.pallas_call(
        body, name="all_gather_weights", in_specs=[ANY_SPEC] * (n + k), out_specs=[ANY_SPEC] * (n + 2 * k),
        out_shape=[_sds((N_DEV,) + b.shape, b.dtype) for b in blocks] + [_sds((N_DEV,) + sh, dtype) for sh in shapes]
        + [_sds(sh, dtype) for sh in shapes],
        scratch_shapes=[pltpu.VMEM(s_.shape, s_.dtype) for s_, _ in later] + [pltpu.VMEM(sh, dtype) for sh in shapes]
        + [pltpu.SemaphoreType.DMA((n, 7)), pltpu.SemaphoreType.DMA((n, 7)), pltpu.SemaphoreType.DMA((n,)),
           pltpu.SemaphoreType.DMA((k, 3))],
        compiler_params=pltpu.CompilerParams(vmem_limit_bytes=VMEM_LIMIT, has_side_effects=True),
    )(*blocks, *[s_ for s_, _ in later])
    return out[:n], out[n:n + k], out[n + k:]


def _adamw(w, g, m, v):
    m = ADAM_B1 * m + (1.0 - ADAM_B1) * g
    v = ADAM_B2 * v + (1.0 - ADAM_B2) * (g * g)
    m_hat = m / (1.0 - ADAM_B1 ** ADAM_STEP)
    v_hat = v / (1.0 - ADAM_B2 ** ADAM_STEP)
    return -ADAM_LR * (m_hat / (jnp.sqrt(v_hat) + ADAM_EPS) + ADAM_WD * w), m, v


def _pair_reduce(g, name):
    _, r, c_ = g.shape
    n = len(CHIP_FLIPS)

    def body(g_ref, out_ref, sib_buf, send_sems, recv_sems):
        x, y, c = _place()
        chips = [(x ^ fx, y ^ fy) for fx, fy in CHIP_FLIPS]
        piece = lambda chip, core: g_ref.at[4 * chip[0] + 2 * chip[1] + core]
        copies = [pltpu.make_async_remote_copy(
            src_ref=piece(chip, 1 - c), dst_ref=sib_buf.at[j], send_sem=send_sems.at[j], recv_sem=recv_sems.at[j],
            device_id=(x, y, 1 - c), device_id_type=MESH_IDS) for j, chip in enumerate(chips)]
        for cp in copies:
            cp.start()
        for j, chip in enumerate(chips):
            copies[j].wait_recv()
            out_ref[j] = (piece(chip, c)[...].astype(F32) + sib_buf[j].astype(F32)).astype(BF)
        for cp in copies:
            cp.wait_send()

    return pl.pallas_call(
        body, name=name, in_specs=[VMEM_SPEC], out_specs=VMEM_SPEC, out_shape=_sds((n, r, c_), BF),
        scratch_shapes=[pltpu.VMEM((n, r, c_), BF), pltpu.SemaphoreType.DMA((n,)), pltpu.SemaphoreType.DMA((n,))],
        compiler_params=pltpu.CompilerParams(vmem_limit_bytes=VMEM_LIMIT, has_side_effects=True),
    )(g)


HBM_SPEC = pl.BlockSpec(memory_space=pltpu.HBM)
SEM_SPEC = pl.BlockSpec(memory_space=pltpu.SEMAPHORE)
DATAFLOW = pltpu.SideEffectType.DATAFLOW_SIDE_EFFECTING


def _peers():
    x, y, c = _place()
    return 4 * x + 2 * y + c, [(x ^ (k >> 2), y ^ ((k >> 1) & 1), c ^ (k & 1)) for k in range(1, N_DEV)]


def _peer_index(peer):
    return 4 * peer[0] + 2 * peer[1] + peer[2]


def _exchange_start(srcs, zones, pieces, name, chips=False):
    n = len(srcs)
    fresh = zones is None
    if fresh:
        slots = len(CHIP_FLIPS) if chips else N_DEV
        zones = [_sds((slots,) + (v.shape[1:] if pieces else v.shape), v.dtype) for v in srcs]
    n_in = n if fresh else 2 * n
    among_chips = list(chips) if isinstance(chips, (list, tuple)) else [chips] * n

    def body(*refs):
        ins, sems, token = refs[:n], refs[n_in:n_in + 2 * n], refs[-1]
        zs = refs[n_in + 3 * n:n_in + 4 * n] if fresh else refs[n:2 * n]
        me, peers = _peers()
        x, y, c = _place()
        for a in range(n):
            if among_chips[a] and pieces:
                routes = [((x ^ fx, y ^ fy, c), j, j) for j, (fx, fy) in enumerate(CHIP_FLIPS) if j]
            elif among_chips[a]:
                routes = [((x ^ fx, y ^ fy, c), None, me) for fx, fy in CHIP_FLIPS[1:]]
            else:
                routes = [(peer, _peer_index(peer) if pieces else None, me) for peer in peers]
            for peer, src_slot, dst_slot in routes:
                pltpu.make_async_remote_copy(
                    src_ref=ins[a] if src_slot is None else ins[a].at[src_slot], dst_ref=zs[a].at[dst_slot],
                    send_sem=sems[2 * a], recv_sem=sems[2 * a + 1], device_id=peer, device_id_type=MESH_IDS).start()
        token[...] = jnp.zeros_like(token)

    hbm = lambda v: pltpu.with_memory_space_constraint(v, pltpu.HBM)
    out = pl.pallas_call(
        body, name=name,
        out_shape=tuple([pltpu.SemaphoreType.DMA(())] * (2 * n) + [pltpu.HBM(v.shape, v.dtype) for v in srcs]
                        + [pltpu.HBM(z.shape, z.dtype) for z in zones] + [_sds((8, LANES))]),
        in_specs=[HBM_SPEC] * n_in, out_specs=tuple([SEM_SPEC] * (2 * n) + [HBM_SPEC] * (2 * n) + [VMEM_SPEC]),
        input_output_aliases={i: 2 * n + i for i in range(n_in)},
        compiler_params=pltpu.CompilerParams(has_side_effects=DATAFLOW),
    )(*[hbm(v) for v in srcs], *([] if fresh else [hbm(z) for z in zones]))
    return out[:2 * n], out[2 * n:3 * n], out[3 * n:4 * n], out[-1]


def _relay_start(zones, name):
    n = len(zones)

    def body(*refs):
        zs, sems, token = refs[:n], refs[n:3 * n], refs[-1]
        x, y, c = _place()
        for fx, fy in CHIP_FLIPS:
            slot = 4 * (x ^ fx) + 2 * (y ^ fy) + c
            for a in range(n):
                pltpu.make_async_remote_copy(
                    src_ref=zs[a].at[slot], dst_ref=zs[a].at[slot], send_sem=sems[2 * a], recv_sem=sems[2 * a + 1],
                    device_id=(x, y, 1 - c), device_id_type=MESH_IDS).start()
        token[...] = jnp.zeros_like(token)

    out = pl.pallas_call(
        body, name=name,
        out_shape=tuple([pltpu.SemaphoreType.DMA(())] * (2 * n) + [pltpu.HBM(z.shape, z.dtype) for z in zones]
                        + [_sds((8, LANES))]),
        in_specs=[HBM_SPEC] * n, out_specs=tuple([SEM_SPEC] * (2 * n) + [HBM_SPEC] * n + [VMEM_SPEC]),
        input_output_aliases={i: 2 * n + i for i in range(n)},
        compiler_params=pltpu.CompilerParams(has_side_effects=DATAFLOW),
    )(*[pltpu.with_memory_space_constraint(z, pltpu.HBM) for z in zones])
    return out[:2 * n], [], out[2 * n:3 * n], out[-1]


def _exchange_wait(sems, srcs, zones, after, name, chips=False, n_copies=None):
    n, n_src = len(zones), len(srcs)
    after = list(after) if isinstance(after, (list, tuple)) else [after]
    n_copies = n_copies or (len(CHIP_FLIPS) - 1 if chips else N_DEV - 1)

    def body(*refs):
        zs, sm = refs[n_src:n_src + n], refs[n_src + n:n_src + 3 * n]
        me, peers = _peers()
        for a in range(n):
            seven = zs[a].at[pl.ds(0, n_copies)]
            cp = pltpu.make_async_remote_copy(src_ref=seven, dst_ref=seven, send_sem=sm[2 * a], recv_sem=sm[2 * a + 1],
                                              device_id=peers[0], device_id_type=MESH_IDS)
            cp.wait_send()
            cp.wait_recv()

    out = pl.pallas_call(
        body, name=name, out_shape=tuple([pltpu.HBM(v.shape, v.dtype) for v in srcs] + [pltpu.HBM(z.shape, z.dtype) for z in zones]),
        in_specs=[HBM_SPEC] * (n_src + n) + [SEM_SPEC] * (2 * n) + [ANY_SPEC] * len(after),
        out_specs=tuple([HBM_SPEC] * (n_src + n)), input_output_aliases={i: i for i in range(n_src + n)},
        compiler_params=pltpu.CompilerParams(has_side_effects=DATAFLOW),
    )(*srcs, *zones, *sems, *after)
    return out[:n_src], out[n_src:]


def _sum_adamw(zone, own, w, m, v, name, chips=False):
    n_slots, r, c_ = zone.shape
    rb = next((b for b in (256, 128) if r % b == 0), r)

    def body(me_ref, z_ref, own_ref, w_ref, m_ref, v_ref, grad_ref, delta_ref, nm_ref, nv_ref):
        total = None
        for d in range(n_slots):
            part = jnp.where(me_ref[0] == d, own_ref[0], z_ref[d]).astype(F32)
            total = part if total is None else total + part
        grad_ref[...] = total
        delta_ref[...], nm_ref[...], nv_ref[...] = _adamw(w_ref[...], total, m_ref[...], v_ref[...])

    x, y, c = _place()
    mine = 0 * x if chips else 4 * x + 2 * y + c
    blk = pl.BlockSpec((rb, c_), lambda i, me_ref: (i, 0))
    return pl.pallas_call(
        body, name=name,
        grid_spec=pltpu.PrefetchScalarGridSpec(
            num_scalar_prefetch=1, grid=(r // rb,),
            in_specs=[pl.BlockSpec((n_slots, rb, c_), lambda i, me_ref: (0, i, 0)),
                      pl.BlockSpec((1, rb, c_), lambda i, me_ref: (me_ref[0], i, 0)), blk, blk, blk],
            out_specs=[blk] * 4),
        out_shape=[_sds((r, c_))] * 4, compiler_params=_params("parallel"),
    )(mine.astype(jnp.int32).reshape(1), zone, own, w, m, v)


SMALL_NORMS = ("pre_mix_norm", "post_mix_norm", "pre_mlp_norm", "post_mlp_norm")
SMALL_ORDER = SMALL_NORMS + ("fox_out_norm", "gdn_out_norm", "fox_f_bias", "gdn_a_log", "gdn_dt_bias", "gdn_conv_w")
CONV_SLAB_ROWS, CONV_SLAB_LANES = 8, 256


def _small_pack(small):
    def body(n0, n1, n2, n3, fnw_ref, gnw_ref, loss_ref, vec_ref, out_ref):
        out_ref[...] = jnp.zeros_like(out_ref)
        for i, ref in enumerate((n0, n1, n2, n3)):
            out_ref[i:i + 1, :] = ref[...]
        out_ref[4:5, 0:LANES] = fnw_ref[...]
        out_ref[4:5, LANES:2 * LANES] = gnw_ref[...]
        out_ref[4:5, 2 * LANES:3 * LANES] = loss_ref[...]
        out_ref[5:8, 0:LANES] = vec_ref[0:3, :]

    return pl.pallas_call(body, name="small_pack", in_specs=[VMEM_SPEC] * 8, out_specs=VMEM_SPEC,
                          out_shape=_sds((8, D_MODEL)))(*small["norms"], small["fox_out_norm"], small["gdn_out_norm"],
                                                        small["loss"], small["vectors"])


def _conv_slabs(dconv):
    blocks = dconv.reshape(CONV_K, N_DEV, -1).transpose(1, 0, 2)
    blocks = jnp.pad(blocks, ((0, 0), (0, CONV_SLAB_ROWS - CONV_K), (0, CONV_SLAB_LANES - blocks.shape[2])))
    return blocks.reshape(N_DEV * CONV_SLAB_ROWS, CONV_SLAB_LANES)


def _small_update(zone, conv_zone, own, own_conv, w, m, v):
    n = len(SMALL_ORDER)
    n_conv = w["gdn_conv_w"].shape[1]

    def body(me_ref, z_ref, zc_ref, own_ref, ownc_ref, *refs):
        params, loss_ref, outs, (tot, totc) = refs[:3 * n], refs[3 * n], refs[3 * n + 1:7 * n + 1], refs[-2:]
        total, total_c = None, None
        for d in range(N_DEV):
            part = jnp.where(me_ref[0] == d, own_ref[...], z_ref[d])
            part_c = jnp.where(me_ref[0] == d, ownc_ref[...], zc_ref[d])
            total, total_c = (part, part_c) if d == 0 else (total + part, total_c + part_c)
        tot[...] = total
        totc[...] = total_c
        loss_ref[...] = tot[4, 2 * LANES:2 * LANES + 1]
        mine = totc[pl.ds(pl.multiple_of(me_ref[0] * CONV_SLAB_ROWS, CONV_SLAB_ROWS), CONV_SLAB_ROWS), :]
        g = dict(zip(SMALL_NORMS, (tot[0], tot[1], tot[2], tot[3])))
        g.update(fox_out_norm=tot[4, 0:FOX_HEAD_DIM], gdn_out_norm=tot[4, LANES:LANES + GDN_HEAD_DIM],
                 fox_f_bias=tot[5, SM_FF:SM_FF + N_FOX_HEADS], gdn_a_log=tot[6, SM_GA:SM_GA + N_GDN_HEADS],
                 gdn_dt_bias=tot[7, SM_GA:SM_GA + N_GDN_HEADS], gdn_conv_w=mine[0:CONV_K, 0:n_conv])
        for i, name in enumerate(SMALL_ORDER):
            w_ref, m_ref, v_ref = params[3 * i:3 * i + 3]
            outs[4 * i][...] = g[name]
            outs[4 * i + 1][...], outs[4 * i + 2][...], outs[4 * i + 3][...] = _adamw(w_ref[...], g[name], m_ref[...],
                                                                                     v_ref[...])

    x, y, c = _place()
    operands = [a[name] for name in SMALL_ORDER for a in (w, m, v)]
    out = pl.pallas_call(
        body, name="small_update",
        in_specs=[pl.BlockSpec(memory_space=pltpu.SMEM)] + [VMEM_SPEC] * (4 + 3 * n), out_specs=[VMEM_SPEC] * (1 + 4 * n),
        out_shape=[_sds((1,))] + [_sds(w[name].shape) for name in SMALL_ORDER for _ in range(4)],
        scratch_shapes=[pltpu.VMEM(zone.shape[1:], F32), pltpu.VMEM(conv_zone.shape[1:], F32)],
    )((4 * x + 2 * y + c).astype(jnp.int32).reshape(1), zone, conv_zone, own, own_conv, *operands)
    return out[0][0], {name: out[1 + 4 * i:5 + 4 * i] for i, name in enumerate(SMALL_ORDER)}


def _native_rows():
    groups = []
    for first, n_groups in ((0, N_FOX_HEADS // 2), (D_FOX * 3 + N_FOX_HEADS, N_GDN_HEADS)):
        for g in range(n_groups):
            groups += [(first + part * n_groups * LANES + g * LANES, first + part * n_groups * LANES + (g + 1) * LANES)
                       for part in range(3)]
    return tuple(groups) + ((3088, 3600), (1536, 1544), (3080, 3088))


NATIVE_ROWS = _native_rows()


W_IN_PIECE = D_PROJ // N_DEV
WGRAD_IN_ROWS = 512
SHUFFLE_LANES = 256


def _to_aligned_moves():
    moves, o = [], 0
    for lo, hi in NATIVE_ROWS:
        r = lo
        while r < hi:
            d = r // W_IN_PIECE
            k = min(hi, (d + 1) * W_IN_PIECE) - r
            moves.append((0, d, r - d * W_IN_PIECE, 0, o, k))
            r, o = r + k, o + k
    return moves


def _from_aligned_moves():
    moves = []
    for _, d, a, _, o, k in _to_aligned_moves():
        while k:
            n = min(k, WGRAD_IN_ROWS - o % WGRAD_IN_ROWS) if o < COL_SMALL else k
            moves.append((0, o // WGRAD_IN_ROWS, o % WGRAD_IN_ROWS, d, a, n) if o < COL_SMALL else
                         (1, 0, o - COL_SMALL, d, a, n))
            o, a, k = o + n, a + n, k - n
    return moves


def _shuffle_rows(srcs, moves, out_shape, name):
    c = srcs[0].shape[-1]

    def body(*refs):
        s_refs, o_ref, s_f, o_f = refs[:len(srcs)], refs[len(srcs)], refs[len(srcs) + 1:-1], refs[-1]
        for s_ref, f in zip(s_refs, s_f):
            f[...] = s_ref[...].astype(F32)
        o_f[...] = jnp.zeros_like(o_f)
        for i, ss, so, ds, do, k in moves:
            o_f[ds, pl.ds(do, k), :] = s_f[i][ss, pl.ds(so, k), :]
        o_ref[...] = o_f[...].astype(BF)

    blk = lambda shape: pl.BlockSpec(tuple(shape[:-1]) + (SHUFFLE_LANES,), lambda j: (0, 0, j))
    scratch = lambda shape: pltpu.VMEM(tuple(shape[:-1]) + (SHUFFLE_LANES,), F32)
    return pl.pallas_call(
        body, name=name, grid=(c // SHUFFLE_LANES,), in_specs=[blk(s.shape) for s in srcs], out_specs=blk(out_shape),
        out_shape=_sds(out_shape, BF), scratch_shapes=[scratch(s.shape) for s in srcs] + [scratch(out_shape)],
        compiler_params=_params("parallel"),
    )(*srcs)


def _cols_from_pieces(p):
    return p.transpose(1, 0, 2).reshape(p.shape[1], -1)


WEIGHT_ORDER = ("pre_mix_norm", "w_in", "fox_f_bias", "fox_out_norm", "gdn_conv_w", "gdn_a_log", "gdn_dt_bias",
                "gdn_out_norm", "w_out", "post_mix_norm", "pre_mlp_norm", "w_up", "w_down", "post_mlp_norm")


def kernel(x, pre_mix_norm, w_in, fox_f_bias, fox_out_norm, gdn_conv_w, gdn_a_log, gdn_dt_bias, gdn_out_norm, w_out, post_mix_norm, pre_mlp_norm, w_up, w_down, post_mlp_norm, loss_target, m_pre_mix_norm, m_w_in, m_fox_f_bias, m_fox_out_norm, m_gdn_conv_w, m_gdn_a_log, m_gdn_dt_bias, m_gdn_out_norm, m_w_out, m_post_mix_norm, m_pre_mlp_norm, m_w_up, m_w_down, m_post_mlp_norm, v_pre_mix_norm, v_w_in, v_fox_f_bias, v_fox_out_norm, v_gdn_conv_w, v_gdn_a_log, v_gdn_dt_bias, v_gdn_out_norm, v_w_out, v_post_mix_norm, v_pre_mlp_norm, v_w_up, v_w_down, v_post_mlp_norm):
    w = dict(pre_mix_norm=pre_mix_norm, w_in=w_in, fox_f_bias=fox_f_bias, fox_out_norm=fox_out_norm,
             gdn_conv_w=gdn_conv_w, gdn_a_log=gdn_a_log, gdn_dt_bias=gdn_dt_bias, gdn_out_norm=gdn_out_norm, w_out=w_out,
             post_mix_norm=post_mix_norm, pre_mlp_norm=pre_mlp_norm, w_up=w_up, w_down=w_down, post_mlp_norm=post_mlp_norm)
    mom = dict(pre_mix_norm=m_pre_mix_norm, w_in=m_w_in, fox_f_bias=m_fox_f_bias, fox_out_norm=m_fox_out_norm,
               gdn_conv_w=m_gdn_conv_w, gdn_a_log=m_gdn_a_log, gdn_dt_bias=m_gdn_dt_bias, gdn_out_norm=m_gdn_out_norm,
               w_out=m_w_out, post_mix_norm=m_post_mix_norm, pre_mlp_norm=m_pre_mlp_norm, w_up=m_w_up, w_down=m_w_down,
               post_mlp_norm=m_post_mlp_norm)
    var = dict(pre_mix_norm=v_pre_mix_norm, w_in=v_w_in, fox_f_bias=v_fox_f_bias, fox_out_norm=v_fox_out_norm,
               gdn_conv_w=v_gdn_conv_w, gdn_a_log=v_gdn_a_log, gdn_dt_bias=v_gdn_dt_bias, gdn_out_norm=v_gdn_out_norm,
               w_out=v_w_out, post_mix_norm=v_post_mix_norm, pre_mlp_norm=v_pre_mlp_norm, w_up=v_w_up, w_down=v_w_down,
               post_mlp_norm=v_post_mlp_norm)

    (win_g, conv_g), zones, shards = _all_gather([w_in.T.astype(BF), gdn_conv_w],
                                                 [(w_out, False), (w_up, True), (w_down, False)], BF)
    wt_al = _shuffle_rows([win_g], _to_aligned_moves(), (1, PROJ_W, D_MODEL), "w_in_to_aligned")[0]
    convw = _cols_from_pieces(conv_g)
    sems, shards, zones, after = _exchange_start(shards, zones, False, "gather_start", chips=[False, True, True])
    gathers = dict(w_out=(sems[:2], shards[:1], zones[:1], after), mlp=(sems[2:], shards[1:], zones[1:], after))

    def late_weights(name, after):
        if name == "mlp_relay":
            sems, shards, zones, _ = gathers["mlp"]
            _, zones = _exchange_wait(sems, shards, zones, after, "gather_mlp_wait", chips=True)
            gathers["mlp"] = _relay_start(zones, "gather_mlp_relay")
            return gathers["mlp"][3]
        sems, shards, zones, _ = gathers[name]
        _, got = _exchange_wait(sems, shards, zones, after, "gather_" + name + "_done",
                                n_copies=len(CHIP_FLIPS) if name == "mlp" else None)
        if name == "w_out":
            return got[0].reshape(D_MODEL, D_MODEL)
        return got[0].reshape(D_FF, D_MODEL), got[1].reshape(D_FF, D_MODEL)

    scatters = {}

    def on_grads(name, g):
        if name == "mlp":
            scatters["mlp"] = list(g)
            return g[0]
        if name == "w_out":
            sems, srcs, zones, token = _exchange_start(scatters["mlp"] + [g], None, True, "scatter_mlp_w_out_start")
            scatters["mlp"] = (sems[:4], srcs[:2], zones[:2], token)
            scatters["w_out"] = (sems[4:], srcs[2:], zones[2:], token)
            return token
        g = _shuffle_rows(list(g), _from_aligned_moves(), (N_DEV, W_IN_PIECE, D_MODEL), "w_in_grad_from_aligned")
        scatters[name] = _exchange_start([_pair_reduce(g, "pair_reduce_w_in")], None, True, "scatter_w_in_start", chips=True)
        return scatters[name][3]

    grad_x, small = _local_step(
        x[0], loss_target[0], wt_al, after, late_weights, on_grads, convw, pre_mix_norm,
        fox_f_bias, fox_out_norm, gdn_a_log, gdn_dt_bias, gdn_out_norm, post_mix_norm, pre_mlp_norm, post_mlp_norm)
    slabs = [_small_pack(small), _conv_slabs(jnp.concatenate(small["conv"], axis=1))]
    scatters["small"] = _exchange_start(slabs, None, False, "small_start")

    grads, delta, new_m, new_v = {}, {}, {}, {}
    after = scatters["small"][3]
    for name, members in (("mlp", ("w_up", "w_down")), ("w_out", ("w_out",)), ("small", ()), ("w_in", ("w_in",))):
        sems, srcs, zones, _ = scatters[name]
        srcs, zones = _exchange_wait(sems, srcs, zones, after, "scatter_" + name + "_wait", chips=name == "w_in")
        if name == "small":
            loss, updated = _small_update(*zones, *srcs, w, mom, var)
            for n, res in updated.items():
                grads[n], delta[n], new_m[n], new_v[n] = res
            after = grads["pre_mix_norm"]
        for n, zone, own in zip(members, zones, srcs):
            if n == "w_in":
                res = _sum_adamw(zone, own, w[n].T, mom[n].T, var[n].T, "adamw_" + n, chips=True)
                grads[n], delta[n], new_m[n], new_v[n] = [r.T for r in res]
            else:
                grads[n], delta[n], new_m[n], new_v[n] = _sum_adamw(zone, own, w[n], mom[n], var[n], "adamw_" + n)
        if members:
            after = [grads[n] for n in members]

    return (loss, grad_x[None], *[grads[n] for n in WEIGHT_ORDER], *[delta[n] for n in WEIGHT_ORDER],
            *[new_m[n] for n in WEIGHT_ORDER], *[new_v[n] for n in WEIGHT_ORDER])
```

```python
import jax
import jax.numpy as jnp
from jax import lax
from jax.experimental import pallas as pl
from jax.experimental.pallas import tpu as pltpu

F32 = jnp.float32
BF = jnp.bfloat16

D_MODEL = 1024
N_FOX_HEADS, FOX_HEAD_DIM = 8, 64
N_GDN_HEADS, GDN_HEAD_DIM = 4, 128
D_FOX = N_FOX_HEADS * FOX_HEAD_DIM
D_GDN = N_GDN_HEADS * GDN_HEAD_DIM
CHUNK = 64
CONV_K = 4
D_FF = 4 * D_MODEL
EPS = 1e-6
D_PROJ = 3600
N_DEV = 8

PROJ_W = 3712
COL_FOX, COL_GDN, COL_GZ, COL_SMALL = 0, 1536, 3072, 3584
LANES = 128
QKV = 3 * LANES
SM_FF, SM_GB, SM_GA = 0, 8, 12

ADAM_LR, ADAM_B1, ADAM_B2, ADAM_EPS, ADAM_WD, ADAM_STEP = 0.001, 0.9, 0.999, 1e-08, 0.01, 10

TOKEN_BLOCK = 256
MATMUL_BLOCK = 512
TRI_ROWS = 4
FOX_SCALE = FOX_HEAD_DIM ** -0.5
GDN_QSCALE = GDN_HEAD_DIM ** -0.5
NEG_BIG = -1e30
VMEM_LIMIT = 56 * 1024 * 1024

VMEM_SPEC = pl.BlockSpec(memory_space=pltpu.VMEM)
ANY_SPEC = pl.BlockSpec(memory_space=pl.ANY)


def _sds(shape, dtype=F32):
    return jax.ShapeDtypeStruct(shape, dtype)


def _params(*sem):
    return pltpu.CompilerParams(dimension_semantics=sem if sem else None, vmem_limit_bytes=VMEM_LIMIT)


def _ordered(body):
    def ordered(_, *refs):
        body(*refs)

    return ordered


def _mm(a, b):
    return jnp.dot(a.astype(BF), b.astype(BF), preferred_element_type=F32)


def _mm_nt(a, b):
    return lax.dot_general(a.astype(BF), b.astype(BF), (((1,), (1,)), ((), ())), preferred_element_type=F32)


def _mm_tn(a, b):
    return lax.dot_general(a.astype(BF), b.astype(BF), (((0,), (0,)), ((), ())), preferred_element_type=F32)


def _sigmoid(x):
    return 1.0 / (1.0 + jnp.exp(-x))


def _softplus(x):
    return jnp.maximum(x, 0.0) + jnp.log1p(jnp.exp(-jnp.abs(x)))


def _iota(shape, dim):
    return lax.broadcasted_iota(jnp.int32, shape, dim)


def _shift_down(x, s, row):
    return jnp.where(row >= s, pltpu.roll(x, s, 0), 0.0)


def _shift_up(x, s, row):
    n = x.shape[0]
    return jnp.where(row < n - s, pltpu.roll(x, n - s, 0), 0.0)


def _norm_proj(x, nw, wt_al, after):
    t = x.shape[0]

    def body(x_ref, nw_ref, w_ref, proj_ref, h_ref):
        xv = x_ref[...]
        r = lax.rsqrt(jnp.mean(xv * xv, axis=-1, keepdims=True) + EPS)
        h = (xv * r * nw_ref[...]).astype(BF)
        h_ref[...] = h
        proj_ref[...] = lax.dot_general(h, w_ref[...], (((1,), (1,)), ((), ())), preferred_element_type=F32)

    tm = min(MATMUL_BLOCK, t)
    return pl.pallas_call(
        _ordered(body), name="norm_proj", grid=(t // tm,),
        in_specs=[ANY_SPEC, pl.BlockSpec((tm, D_MODEL), lambda i: (i, 0)), pl.BlockSpec((1, D_MODEL), lambda i: (0, 0)),
                  pl.BlockSpec((PROJ_W, D_MODEL), lambda i: (0, 0))],
        out_specs=[pl.BlockSpec((tm, PROJ_W), lambda i: (i, 0)), pl.BlockSpec((tm, D_MODEL), lambda i: (i, 0))],
        out_shape=[_sds((t, PROJ_W)), _sds((t, D_MODEL), BF)],
        compiler_params=_params("parallel"),
    )(after, x, nw, wt_al)


def _lane_column(x, lane):
    return jnp.sum(jnp.where(_iota((1, LANES), 1) == lane, x, 0.0), axis=-1, keepdims=True)


def _small_prep(proj, fb, al, dtb):
    t = proj.shape[0]

    def body(sm_ref, fb_ref, al_ref, dtb_ref, cumt_ref, beta_ref, g_ref):
        s = sm_ref[...]
        z = s + fb_ref[...]
        cum = jnp.minimum(z, 0.0) - jnp.log1p(jnp.exp(-jnp.abs(z)))
        row = _iota((t, LANES), 0)
        step = 1
        while step < t:
            cum = cum + _shift_down(cum, step, row)
            step *= 2
        cumt_ref[...] = cum.T
        beta_ref[...] = _sigmoid(s)
        g_ref[...] = -jnp.exp(al_ref[...]) * _softplus(s + dtb_ref[...])

    vec = pl.BlockSpec((1, LANES), lambda i: (0, 0))
    tok = pl.BlockSpec((t, LANES), lambda i: (0, 0))
    return pl.pallas_call(
        body, name="small_prep", grid=(1,),
        in_specs=[pl.BlockSpec((t, LANES), lambda i: (0, COL_SMALL // LANES)), vec, vec, vec],
        out_specs=[pl.BlockSpec((LANES, t), lambda i: (0, 0)), tok, tok],
        out_shape=[_sds((LANES, t)), _sds((t, LANES)), _sds((t, LANES))],
        compiler_params=_params("arbitrary"),
    )(proj, fb, al, dtb)


def _fox_stack(x, first):
    return jnp.concatenate([jnp.where(first, x, 0.0), jnp.where(first, 0.0, x)], axis=0).astype(BF)


def _fox_unstack(y, first):
    n = y.shape[0] // 2
    return jnp.where(first, y[:n], y[n:])


def _fox_logits(q2_i, kb, cumt_ref, pair, i, tq):
    klen = (i + 1) * tq
    s = lax.dot_general(q2_i, kb[:klen], (((1,), (1,)), ((), ())), preferred_element_type=F32)
    upper = _iota((2 * tq, 1), 0) < tq
    s = s - jnp.where(upper, cumt_ref[pl.ds(2 * pair, 1), 0:klen], cumt_ref[pl.ds(2 * pair + 1, 1), 0:klen])
    causal = _iota((2 * tq, tq), 1) <= _iota((2 * tq, tq), 0) % tq
    parts = [(s[:, :klen - tq], 0, klen - tq)] if i else []
    return parts + [(jnp.where(causal, s[:, klen - tq:], NEG_BIG), klen - tq, klen)]


def _fox_fwd(proj, cumt, fnw, after):
    t = proj.shape[0]
    tq = min(TOKEN_BLOCK, t // 2)
    nq = t // tq

    def body(q_ref, k_ref, v_ref, cumt_ref, fnw_ref, o_ref, lse_ref, fn_ref):
        j = pl.program_id(0)
        first = _iota((1, LANES), 1) < FOX_HEAD_DIM
        kb = k_ref[...].astype(BF)
        vb = v_ref[...].astype(BF)
        for i in range(nq):
            rows = slice(i * tq, (i + 1) * tq)
            q2 = _fox_stack(q_ref[rows, :] * FOX_SCALE, first)
            parts = _fox_logits(q2, kb, cumt_ref, j, i, tq)
            m = jnp.max(parts[-1][0], axis=-1, keepdims=True)
            if i:
                m = jnp.maximum(m, jnp.max(parts[0][0], axis=-1, keepdims=True))
            l = jnp.zeros((2 * tq, 1), F32)
            o = jnp.zeros((2 * tq, LANES), F32)
            for s, lo, hi in parts:
                p = jnp.exp(s - m)
                l = l + jnp.sum(p, axis=-1, keepdims=True)
                o = o + jnp.dot(p.astype(BF), vb[lo:hi], preferred_element_type=F32)
            o_acc = _fox_unstack(o / l, first)
            lse_acc = _fox_unstack(jnp.broadcast_to(m + jnp.log(l), (2 * tq, LANES)), first)
            o_ref[rows, :] = o_acc
            lse_ref[rows, :] = lse_acc
            o2 = o_acc * o_acc
            s0 = jnp.sum(jnp.where(first, o2, 0.0), axis=-1, keepdims=True)
            s1 = jnp.sum(jnp.where(first, 0.0, o2), axis=-1, keepdims=True)
            r = lax.rsqrt(jnp.where(first, s0, s1) * (1.0 / FOX_HEAD_DIM) + EPS)
            fn_ref[rows, :] = (o_acc * r * fnw_ref[...]).astype(BF)

    qkv = lambda k: pl.BlockSpec((t, LANES), lambda j: (0, COL_FOX // LANES + 3 * j + k))
    pair = pl.BlockSpec((t, LANES), lambda j: (0, j))
    return pl.pallas_call(
        _ordered(body), name="fox_fwd", grid=(N_FOX_HEADS // 2,),
        in_specs=[ANY_SPEC, qkv(0), qkv(1), qkv(2), pl.BlockSpec((LANES, t), lambda j: (0, 0)),
                  pl.BlockSpec((1, LANES), lambda j: (0, 0))],
        out_specs=[pair, pair, pair],
        out_shape=[_sds((t, D_FOX)), _sds((t, D_FOX)), _sds((t, D_FOX), BF)],
        compiler_params=_params("parallel"),
    )(after, proj, proj, proj, cumt, fnw)


def _fox_bwd(proj, cumt, lse, o, do, dproj):
    t = proj.shape[0]
    tq = min(TOKEN_BLOCK, t // 2)
    nq = t // tq

    def body(q_ref, k_ref, v_ref, cumt_ref, lse_ref, o_ref, do_ref, _, dqkv_ref, dcq_ref, dckt_ref, dk_s, dv_s):
        j = pl.program_id(0)

        @pl.when(j == 0)
        def _():
            dcq_ref[...] = jnp.zeros_like(dcq_ref)
            dckt_ref[...] = jnp.zeros_like(dckt_ref)

        lane = _iota((1, LANES), 1)

        first = _iota((1, LANES), 1) < FOX_HEAD_DIM
        kb = k_ref[...].astype(BF)
        vb = v_ref[...].astype(BF)
        dk_s[...] = jnp.zeros_like(dk_s)
        dv_s[...] = jnp.zeros_like(dv_s)
        for i in range(nq):
            rows = slice(i * tq, (i + 1) * tq)
            do_i = do_ref[rows, :]
            prod = do_i * o_ref[rows, :]
            lse_i = lse_ref[rows, :]
            q2 = _fox_stack(q_ref[rows, :] * FOX_SCALE, first)
            do2 = _fox_stack(do_i, first)
            delta = jnp.concatenate([jnp.sum(jnp.where(first, prod, 0.0), axis=-1, keepdims=True),
                                     jnp.sum(jnp.where(first, 0.0, prod), axis=-1, keepdims=True)], axis=0)
            lse2 = jnp.concatenate([lse_i[:, 0:1], lse_i[:, FOX_HEAD_DIM:FOX_HEAD_DIM + 1]], axis=0)
            dq2 = jnp.zeros((2 * tq, LANES), F32)
            dcq2 = jnp.zeros((2 * tq, 1), F32)
            for s, lo, hi in _fox_logits(q2, kb, cumt_ref, j, i, tq):
                p = jnp.exp(s - lse2)
                ds = p * (_mm_nt(do2, vb[lo:hi]) - delta)
                dsb = ds.astype(BF)
                dq2 = dq2 + jnp.dot(dsb, kb[lo:hi], preferred_element_type=F32)
                dk_s[lo:hi, :] += _mm_tn(dsb, q2)
                dv_s[lo:hi, :] += _mm_tn(p, do2)
                dcq2 = dcq2 + jnp.sum(ds, axis=-1, keepdims=True)
                dckt_ref[pl.ds(2 * j, 1), lo:hi] += jnp.sum(ds[:tq], axis=0, keepdims=True)
                dckt_ref[pl.ds(2 * j + 1, 1), lo:hi] += jnp.sum(ds[tq:], axis=0, keepdims=True)
            dqkv_ref[rows, 0:LANES] = (_fox_unstack(dq2, first) * FOX_SCALE).astype(BF)
            dcq_ref[rows, :] += jnp.where(lane == 2 * j, dcq2[:tq], jnp.where(lane == 2 * j + 1, dcq2[tq:], 0.0))
        dqkv_ref[:, LANES:2 * LANES] = dk_s[...].astype(BF)
        dqkv_ref[:, 2 * LANES:QKV] = dv_s[...].astype(BF)

    qkv = lambda k: pl.BlockSpec((t, LANES), lambda j: (0, COL_FOX // LANES + 3 * j + k))
    pair = pl.BlockSpec((t, LANES), lambda j: (0, j))
    rows128 = pl.BlockSpec((LANES, t), lambda j: (0, 0))
    return pl.pallas_call(
        body, name="fox_bwd", grid=(N_FOX_HEADS // 2,),
        in_specs=[qkv(0), qkv(1), qkv(2), rows128, pair, pair, pair, ANY_SPEC],
        out_specs=[pl.BlockSpec((t, QKV), lambda j: (0, COL_FOX // QKV + j)),
                   pl.BlockSpec((t, LANES), lambda j: (0, 0)), rows128],
        out_shape=[_sds(dproj.shape, BF), _sds((t, LANES)), _sds((LANES, t))],
        scratch_shapes=[pltpu.VMEM((t, LANES), F32), pltpu.VMEM((t, LANES), F32)],
        input_output_aliases={7: 0}, compiler_params=_params("arbitrary"),
    )(proj, proj, proj, cumt, lse, o, do, dproj)


def _conv(x, w, row):
    return (w[3:4, :] * x + w[2:3, :] * _shift_down(x, 1, row) + w[1:2, :] * _shift_down(x, 2, row)
            + w[0:1, :] * _shift_down(x, 3, row))


def _chunk_decay(gc_c):
    gi = gc_c[:, 0:CHUNK]
    gj = gc_c.T[0:CHUNK, :]
    ri = _iota((CHUNK, CHUNK), 0)
    cj = _iota((CHUNK, CHUNK), 1)
    return jnp.where(ri >= cj, jnp.exp(jnp.minimum(gi - gj, 0.0)), 0.0), ri > cj


def _gdn_specs(t):
    col = lambda off: pl.BlockSpec((t, LANES), lambda h: (0, off + h))
    cw = lambda off: pl.BlockSpec((CONV_K, LANES), lambda h: (0, off + h))
    mat = pl.BlockSpec((1, t // CHUNK, CHUNK, CHUNK), lambda h: (h, 0, 0, 0))
    qkv = lambda k: pl.BlockSpec((t, LANES), lambda h: (0, COL_GDN // LANES + 3 * h + k))
    return col, cw, mat, qkv


def _gdn_prep(proj, convw, beta, g):
    t = proj.shape[0]
    nch = t // CHUNK

    def body(xq_ref, xk_ref, xv_ref, wq_ref, wk_ref, wv_ref, beta_ref, g_ref,
             qn_ref, kn_ref, cv_ref, gc_ref, be_ref, m_ref, a_ref):
        row = _iota((t, LANES), 0)
        hd = pl.program_id(0)
        be_ref[...] = jnp.broadcast_to(_lane_column(beta_ref[...], SM_GB + hd), (t, LANES))

        def act(x_ref, w_ref):
            y = _conv(x_ref[...], w_ref[...], row)
            return y * _sigmoid(y)

        cq = act(xq_ref, wq_ref)
        ck = act(xk_ref, wk_ref)
        cv_ref[...] = act(xv_ref, wv_ref)
        qn_ref[...] = cq * lax.rsqrt(jnp.sum(cq * cq, axis=-1, keepdims=True) + EPS) * GDN_QSCALE
        kn_ref[...] = ck * lax.rsqrt(jnp.sum(ck * ck, axis=-1, keepdims=True) + EPS)
        gc = jnp.broadcast_to(_lane_column(g_ref[...], SM_GA + hd), (t, LANES))
        pos = row % CHUNK
        step = 1
        while step < CHUNK:
            gc = gc + jnp.where(pos >= step, pltpu.roll(gc, step, 0), 0.0)
            step *= 2
        gc_ref[...] = gc

        group = 4 if nch % 4 == 0 else 1

        def chunks(gi, carry):
            ns = [gi * group + c for c in range(group)]
            sls = [pl.ds(pl.multiple_of(n * CHUNK, CHUNK), CHUNK) for n in ns]
            ks = [kn_ref[sl, :] for sl in sls]
            kk = [_mm_nt(k_c * be_ref[sl, :], k_c) for k_c, sl in zip(ks, sls)]
            qk = [_mm_nt(qn_ref[sl, :], k_c) for k_c, sl in zip(ks, sls)]
            for c, n in enumerate(ns):
                decay, strict = _chunk_decay(gc_ref[sls[c], :])
                m_ref[0, n] = jnp.where(strict, kk[c] * decay, 0.0)
                a_ref[0, n] = qk[c] * decay
            return carry

        lax.fori_loop(0, nch // group, chunks, 0)

    col, cw, mat, qkv = _gdn_specs(t)
    return pl.pallas_call(
        body, name="gdn_prep", grid=(N_GDN_HEADS,),
        in_specs=[qkv(0), qkv(1), qkv(2), cw(0), cw(4), cw(8)] + [pl.BlockSpec((t, LANES), lambda h: (0, 0))] * 2,
        out_specs=[col(0), col(0), col(0), col(0), col(0), mat, mat],
        out_shape=[_sds((t, D_GDN))] * 5 + [_sds((N_GDN_HEADS, nch, CHUNK, CHUNK))] * 2,
        compiler_params=_params("parallel"),
    )(proj, proj, proj, convw, convw, convw, beta, g)


def _tri_inverse(m3):
    assert m3.shape == (LANES, CHUNK, CHUNK)

    def body(m_ref, t_ref, ms, ts):
        for i in range(CHUNK):
            ms[i * CHUNK:(i + 1) * CHUNK, :] = m_ref[:, i, :].T
        cidx = _iota((CHUNK, LANES), 0)

        def t_row(j):
            return ts[pl.ds(pl.multiple_of(j * CHUNK, CHUNK), CHUNK), :]

        def outer(ib, carry):
            i0 = ib * TRI_ROWS

            def inner(jj, accs):
                earlier = t_row(jj)
                return tuple(acc - ms[pl.ds((i0 + r) * CHUNK + jj, 1), :] * earlier for r, acc in enumerate(accs))

            accs = list(lax.fori_loop(
                0, i0, inner, tuple(jnp.where(cidx == i0 + r, 1.0, 0.0).astype(F32) for r in range(TRI_ROWS))))
            for r in range(TRI_ROWS):
                for q in range(r):
                    accs[r] = accs[r] - ms[pl.ds((i0 + r) * CHUNK + i0 + q, 1), :] * accs[q]
                ts[pl.ds(pl.multiple_of((i0 + r) * CHUNK, CHUNK), CHUNK), :] = accs[r]
            return carry

        lax.fori_loop(0, CHUNK // TRI_ROWS, outer, 0)
        for i in range(CHUNK):
            t_ref[:, i, :] = ts[i * CHUNK:(i + 1) * CHUNK, :].T

    return pl.pallas_call(
        body, name="tri_inverse", in_specs=[VMEM_SPEC], out_specs=VMEM_SPEC,
        out_shape=_sds((LANES, CHUNK, CHUNK)),
        scratch_shapes=[pltpu.VMEM((CHUNK * CHUNK, LANES), F32), pltpu.VMEM((CHUNK * CHUNK, LANES), F32)],
        compiler_params=_params(),
    )(m3)


def _gdn_chunk_terms(q, k, v, b, gcc):
    eg = jnp.exp(gcc)
    last = gcc[CHUNK - 1:CHUNK, :]
    egl = jnp.exp(last - gcc)
    gl = jnp.exp(last)
    kb = k * b
    return eg, egl, gl, kb, v * b, kb * eg, q * eg, k * egl


GDN_BLOCK_CHUNKS = 4


def _gdn_block_specs(t, reverse):
    cb = GDN_BLOCK_CHUNKS
    nb = t // (cb * CHUNK)
    idx = (lambda i: nb - 1 - i) if reverse else (lambda i: i)
    tok = pl.BlockSpec((cb * CHUNK, D_GDN), lambda i: (idx(i), 0))
    mat = pl.BlockSpec((N_GDN_HEADS, cb, CHUNK, CHUNK), lambda i: (0, idx(i), 0, 0))
    state = pl.BlockSpec((N_GDN_HEADS, cb, GDN_HEAD_DIM, GDN_HEAD_DIM), lambda i: (0, idx(i), 0, 0))
    return nb, tok, mat, state


def _gdn_scan(qn, kn, cv, be, gc, tinv, amat):
    t = qn.shape[0]
    nch = t // CHUNK

    def body(q_ref, k_ref, v_ref, b_ref, gc_ref, t_ref, a_ref, o_ref, sall_ref, vn_ref, s_scr):
        @pl.when(pl.program_id(0) == 0)
        def _():
            s_scr[...] = jnp.zeros_like(s_scr)

        heads = range(N_GDN_HEADS)
        cols = [slice(hd * LANES, (hd + 1) * LANES) for hd in heads]
        s = [s_scr[hd] for hd in heads]
        for cc in range(GDN_BLOCK_CHUNKS):
            rs = slice(cc * CHUNK, (cc + 1) * CHUNK)
            terms = [_gdn_chunk_terms(q_ref[rs, cs], k_ref[rs, cs], v_ref[rs, cs], b_ref[rs, cs], gc_ref[rs, cs])
                     for cs in cols]
            for hd in heads:
                sall_ref[hd, cc] = s[hd]
            uw = [_mm(t_ref[hd, cc], jnp.concatenate([terms[hd][4], terms[hd][5]], axis=1)) for hd in heads]
            ws_qs = [_mm(jnp.concatenate([uw[hd][:, LANES:], terms[hd][6]], axis=0), s[hd]) for hd in heads]
            vn = [uw[hd][:, :LANES] - ws_qs[hd][:CHUNK] for hd in heads]
            a_vn = [_mm(a_ref[hd, cc], vn[hd]) for hd in heads]
            kd_vn = [_mm_tn(terms[hd][7], vn[hd]) for hd in heads]
            for hd in heads:
                vn_ref[rs, cols[hd]] = vn[hd]
                o_ref[rs, cols[hd]] = ws_qs[hd][CHUNK:] + a_vn[hd]
                s[hd] = s[hd] * terms[hd][2] + kd_vn[hd]
        for hd in heads:
            s_scr[hd] = s[hd]

    nb, tok, mat, state = _gdn_block_specs(t, False)
    return pl.pallas_call(
        body, name="gdn_scan", grid=(nb,),
        in_specs=[tok] * 5 + [mat, mat], out_specs=[tok, state, tok],
        out_shape=[_sds((t, D_GDN)), _sds((N_GDN_HEADS, nch, GDN_HEAD_DIM, GDN_HEAD_DIM)), _sds((t, D_GDN))],
        scratch_shapes=[pltpu.VMEM((N_GDN_HEADS, GDN_HEAD_DIM, GDN_HEAD_DIM), F32)],
        compiler_params=_params("arbitrary"),
    )(qn, kn, cv, be, gc, tinv, amat)


def _gdn_bwd(qn, kn, cv, be, gc, tinv, amat, s_all, vn_all, do):
    t = qn.shape[0]

    def body(q_ref, k_ref, v_ref, b_ref, gc_ref, t_ref, a_ref, sall_ref, vn_ref, do_ref,
             dq_ref, dk_ref, dv_ref, db_ref, dg_ref, ds_scr):
        @pl.when(pl.program_id(0) == 0)
        def _():
            ds_scr[...] = jnp.zeros_like(ds_scr)

        lastrow = _iota((CHUNK, LANES), 0) == CHUNK - 1
        heads = range(N_GDN_HEADS)
        cols = [slice(hd * LANES, (hd + 1) * LANES) for hd in heads]
        each = lambda fn: [fn(hd) for hd in heads]
        rows_cat = lambda x, y: jnp.concatenate([x, y], axis=0)
        lane_cat = lambda x, y: jnp.concatenate([x, y], axis=1)
        dsp = each(lambda hd: ds_scr[hd])
        for cc in reversed(range(GDN_BLOCK_CHUNKS)):
            rs = slice(cc * CHUNK, (cc + 1) * CHUNK)
            q = each(lambda hd: q_ref[rs, cols[hd]])
            k = each(lambda hd: k_ref[rs, cols[hd]])
            v = each(lambda hd: v_ref[rs, cols[hd]])
            b = each(lambda hd: b_ref[rs, cols[hd]])
            gcc = each(lambda hd: gc_ref[rs, cols[hd]])
            do_c = each(lambda hd: do_ref[rs, cols[hd]])
            vn = each(lambda hd: vn_ref[rs, cols[hd]])
            tn = each(lambda hd: t_ref[hd, cc])
            st = each(lambda hd: sall_ref[hd, cc])
            terms = each(lambda hd: _gdn_chunk_terms(q[hd], k[hd], v[hd], b[hd], gcc[hd]))
            eg, egl, gl, kb, vb, kbg, qd, kd = [[terms[hd][i] for hd in heads] for i in range(8)]
            w = each(lambda hd: _mm(tn[hd], kbg[hd]))
            a_do = each(lambda hd: _mm_tn(a_ref[hd, cc], do_c[hd]))
            kd_ds = each(lambda hd: _mm(kd[hd], dsp[hd]))
            da = each(lambda hd: _mm_nt(do_c[hd], vn[hd]))
            dkd = each(lambda hd: _mm_nt(vn[hd], dsp[hd]))
            by_k = each(lambda hd: _mm_nt(rows_cat(kb[hd], q[hd]), k[hd]))
            dgl = each(lambda hd: jnp.sum(jnp.sum(dsp[hd] * st[hd], axis=-1, keepdims=True), axis=0, keepdims=True))
            dvn = each(lambda hd: a_do[hd] + kd_ds[hd])
            do_dvn = each(lambda hd: rows_cat(do_c[hd], dvn[hd]))
            by_s = each(lambda hd: _mm_nt(do_dvn[hd], st[hd]))
            dqd = each(lambda hd: by_s[hd][:CHUNK])
            dvn_dw = each(lambda hd: lane_cat(dvn[hd], -by_s[hd][CHUNK:]))
            dsp = each(lambda hd: _mm_tn(rows_cat(qd[hd], -w[hd]), do_dvn[hd]) + gl[hd] * dsp[hd])
            dt = each(lambda hd: _mm_nt(dvn_dw[hd], lane_cat(vb[hd], kbg[hd])))
            by_t = each(lambda hd: _mm_tn(tn[hd], dvn_dw[hd]))
            tt_dt = each(lambda hd: _mm_tn(tn[hd], dt[hd]))
            dm_raw = each(lambda hd: _mm_nt(tt_dt[hd], tn[hd]))
            masks = each(lambda hd: _chunk_decay(gcc[hd]))
            dkk = each(lambda hd: jnp.where(masks[hd][1], -dm_raw[hd], 0.0) * masks[hd][0])
            dqk = each(lambda hd: da[hd] * masks[hd][0])
            dqk_dkk = each(lambda hd: rows_cat(dqk[hd], dkk[hd]))
            on_k = each(lambda hd: _mm(dqk_dkk[hd], k[hd]))
            dk_mm = each(lambda hd: _mm_tn(dqk_dkk[hd], rows_cat(q[hd], kb[hd])))
            for hd in heads:
                cs = cols[hd]
                dvb, dkbg = by_t[hd][:, :LANES], by_t[hd][:, LANES:]
                gmat = dkk[hd] * by_k[hd][:CHUNK] + dqk[hd] * by_k[hd][CHUNK:]
                dq_ref[rs, cs] = dqd[hd] * eg[hd] + on_k[hd][:CHUNK]
                dkb = on_k[hd][CHUNK:] + dkbg * eg[hd]
                dk_ref[rs, cs] = dkd[hd] * egl[hd] + dk_mm[hd] + dkb * b[hd]
                db = jnp.sum(dkb * k[hd], axis=-1, keepdims=True) + jnp.sum(dvb * v[hd], axis=-1, keepdims=True)
                db_ref[rs, cs] = jnp.broadcast_to(db, (CHUNK, LANES))
                dv_ref[rs, cs] = dvb * b[hd]
                dkd_kd = jnp.sum(dkd[hd] * kd[hd], axis=-1, keepdims=True)
                col_sums = jnp.sum(lane_cat(gmat, jnp.zeros_like(gmat)).T, axis=-1, keepdims=True)
                dgc = (jnp.sum(gmat, axis=-1, keepdims=True) - col_sums[:CHUNK]
                       + jnp.sum(dqd[hd] * qd[hd], axis=-1, keepdims=True)
                       + jnp.sum(dkbg * kbg[hd], axis=-1, keepdims=True) - dkd_kd)
                extra = jnp.sum(dkd_kd, axis=0, keepdims=True) + dgl[hd] * gl[hd]
                dg_ref[rs, cs] = dgc + jnp.where(lastrow, extra, 0.0)
        for hd in heads:
            ds_scr[hd] = dsp[hd]
        dg = dg_ref[...]
        row = _iota(dg.shape, 0)
        pos = row % CHUNK
        step = 1
        while step < CHUNK:
            dg = dg + jnp.where(pos < CHUNK - step, pltpu.roll(dg, dg.shape[0] - step, 0), 0.0)
            step *= 2
        dg_ref[...] = dg

    nb, tok, mat, state = _gdn_block_specs(t, True)
    return pl.pallas_call(
        body, name="gdn_bwd", grid=(nb,),
        in_specs=[tok] * 5 + [mat, mat, state, tok, tok], out_specs=[tok] * 5, out_shape=[_sds((t, D_GDN))] * 5,
        scratch_shapes=[pltpu.VMEM((N_GDN_HEADS, GDN_HEAD_DIM, GDN_HEAD_DIM), F32)],
        compiler_params=_params("arbitrary"),
    )(qn, kn, cv, be, gc, tinv, amat, s_all, vn_all, do)


def _gdn_bwd_conv(proj, convw, dqn, dkn, dcv, dproj):
    t = proj.shape[0]

    def body(xq_ref, xk_ref, xv_ref, wq_ref, wk_ref, wv_ref, dq_ref, dk_ref, dv_ref, _,
             dqkv_ref, dwq_ref, dwk_ref, dwv_ref):
        row = _iota((t, LANES), 0)

        def one(x_ref, w_ref, d_ref, k, dw_ref, scale):
            x = x_ref[...]
            w = w_ref[...]
            y = _conv(x, w, row)
            sg = _sigmoid(y)
            dc = d_ref[...]
            if scale is not None:
                c = y * sg
                r = lax.rsqrt(jnp.sum(c * c, axis=-1, keepdims=True) + EPS)
                ch = c * r
                dc = scale * r * (dc - ch * jnp.sum(dc * ch, axis=-1, keepdims=True))
            dy = dc * sg * (1.0 + y * (1.0 - sg))
            dqkv_ref[:, k * LANES:(k + 1) * LANES] = (
                w[3:4, :] * dy + w[2:3, :] * _shift_up(dy, 1, row) + w[1:2, :] * _shift_up(dy, 2, row)
                + w[0:1, :] * _shift_up(dy, 3, row)).astype(BF)
            for jj in range(CONV_K):
                xs = x if jj == CONV_K - 1 else _shift_down(x, CONV_K - 1 - jj, row)
                dw_ref[jj:jj + 1, :] = jnp.sum(dy * xs, axis=0, keepdims=True)

        one(xq_ref, wq_ref, dq_ref, 0, dwq_ref, GDN_QSCALE)
        one(xk_ref, wk_ref, dk_ref, 1, dwk_ref, 1.0)
        one(xv_ref, wv_ref, dv_ref, 2, dwv_ref, None)

    col, cw, _, qkv = _gdn_specs(t)
    return pl.pallas_call(
        body, name="gdn_bwd_conv", grid=(N_GDN_HEADS,),
        in_specs=[qkv(0), qkv(1), qkv(2), cw(0), cw(4), cw(8), col(0), col(0), col(0), ANY_SPEC],
        out_specs=[pl.BlockSpec((t, QKV), lambda h: (0, COL_GDN // QKV + h)), cw(0), cw(0), cw(0)],
        out_shape=[_sds(dproj.shape, BF)] + [_sds((CONV_K, D_GDN))] * 3,
        input_output_aliases={9: 0}, compiler_params=_params("parallel"),
    )(proj, proj, proj, convw, convw, convw, dqn, dkn, dcv, dproj)


def _mix_out(fox_n, gdn_o, proj, gnw, w_out, x, pmw, plw, after):
    t = x.shape[0]
    tm = min(MATMUL_BLOCK, t)

    def body(fn_ref, go_ref, gz_ref, gnw_ref, w_ref, x_ref, pmw_ref, plw_ref, x1_ref, h2_ref, mixed_ref, omix_ref,
             h2t_ref):
        omix_ref[:, 0:D_FOX] = fn_ref[...]
        for hd in range(N_GDN_HEADS):
            cs = slice(hd * LANES, (hd + 1) * LANES)
            go = go_ref[:, cs]
            r = lax.rsqrt(jnp.mean(go * go, axis=-1, keepdims=True) + EPS)
            gz = gz_ref[:, cs]
            omix_ref[:, D_FOX + hd * LANES:D_FOX + (hd + 1) * LANES] = (
                go * r * gnw_ref[...] * (gz * _sigmoid(gz))).astype(BF)
        mixed = jnp.dot(omix_ref[...], w_ref[...], preferred_element_type=F32)
        mixed_ref[...] = mixed
        r2 = lax.rsqrt(jnp.mean(mixed * mixed, axis=-1, keepdims=True) + EPS)
        x1 = x_ref[...] + mixed * r2 * pmw_ref[...]
        x1_ref[...] = x1
        r3 = lax.rsqrt(jnp.mean(x1 * x1, axis=-1, keepdims=True) + EPS)
        h2 = x1 * r3 * plw_ref[...]
        h2_ref[...] = h2.astype(BF)
        h2t_ref[...] = h2.T.astype(BF)

    tok = lambda w: pl.BlockSpec((tm, w), lambda i: (i, 0))
    vec = lambda w: pl.BlockSpec((1, w), lambda i: (0, 0))
    return pl.pallas_call(
        _ordered(body), name="mix_out", grid=(t // tm,),
        in_specs=[ANY_SPEC, tok(D_FOX), tok(D_GDN), pl.BlockSpec((tm, D_GDN), lambda i: (i, COL_GZ // D_GDN)), vec(LANES),
                  pl.BlockSpec((D_MODEL, D_MODEL), lambda i: (0, 0)), tok(D_MODEL), vec(D_MODEL), vec(D_MODEL)],
        out_specs=[tok(D_MODEL)] * 4 + [pl.BlockSpec((D_MODEL, tm), lambda i: (0, i))],
        out_shape=[_sds((t, D_MODEL)), _sds((t, D_MODEL), BF), _sds((t, D_MODEL)), _sds((t, D_MODEL), BF),
                   _sds((D_MODEL, t), BF)],
        compiler_params=_params("parallel"),
    )(after, fox_n, gdn_o, proj, gnw, w_out, x, pmw, plw)


def _out_bwd(dmixed, w_out, o_fox, gdn_o, proj, fnw, gnw, after):
    t = dmixed.shape[0]
    tm = min(MATMUL_BLOCK, t)

    def body(dm_ref, w_ref, of_ref, go_ref, gz_ref, fnw_ref, gnw_ref, dof_ref, dgo_ref, dgz_ref, dfw_ref, dgw_ref):
        i = pl.program_id(0)

        @pl.when(i == 0)
        def _():
            dfw_ref[...] = jnp.zeros_like(dfw_ref)
            dgw_ref[...] = jnp.zeros_like(dgw_ref)

        domix = _mm_nt(dm_ref[...], w_ref[...])
        first = _iota((1, LANES), 1) < FOX_HEAD_DIM
        dfw = jnp.zeros((1, LANES), F32)
        dgw = jnp.zeros((1, LANES), F32)
        for pr in range(N_FOX_HEADS // 2):
            cs = slice(pr * LANES, (pr + 1) * LANES)
            o = of_ref[:, cs]
            dfn = domix[:, cs]
            o2 = o * o
            s0 = jnp.sum(jnp.where(first, o2, 0.0), axis=-1, keepdims=True)
            s1 = jnp.sum(jnp.where(first, 0.0, o2), axis=-1, keepdims=True)
            r = lax.rsqrt(jnp.where(first, s0, s1) * (1.0 / FOX_HEAD_DIM) + EPS)
            oh = o * r
            dfw = dfw + jnp.sum(dfn * oh, axis=0, keepdims=True)
            doh = dfn * fnw_ref[...]
            pr_ = doh * oh
            m0 = jnp.sum(jnp.where(first, pr_, 0.0), axis=-1, keepdims=True)
            m1 = jnp.sum(jnp.where(first, 0.0, pr_), axis=-1, keepdims=True)
            dof_ref[:, cs] = r * (doh - oh * jnp.where(first, m0, m1) * (1.0 / FOX_HEAD_DIM))
        for hd in range(N_GDN_HEADS):
            cs = slice(hd * LANES, (hd + 1) * LANES)
            go = go_ref[:, cs]
            gz = gz_ref[:, cs]
            dgated = domix[:, D_FOX + hd * LANES:D_FOX + (hd + 1) * LANES]
            r = lax.rsqrt(jnp.mean(go * go, axis=-1, keepdims=True) + EPS)
            goh = go * r
            sg = _sigmoid(gz)
            sz = gz * sg
            gn = goh * gnw_ref[...]
            dgn = dgated * sz
            dgz_ref[:, cs] = (dgated * gn * sg * (1.0 + gz * (1.0 - sg))).astype(BF)
            dgw = dgw + jnp.sum(dgn * goh, axis=0, keepdims=True)
            dgh = dgn * gnw_ref[...]
            dgo_ref[:, cs] = r * (dgh - goh * jnp.mean(dgh * goh, axis=-1, keepdims=True))
        dfw_ref[...] += dfw + pltpu.roll(dfw, FOX_HEAD_DIM, 1)
        dgw_ref[...] += dgw

    tok = lambda w: pl.BlockSpec((tm, w), lambda i: (i, 0))
    vec = lambda w: pl.BlockSpec((1, w), lambda i: (0, 0))
    return pl.pallas_call(
        _ordered(body), name="out_bwd", grid=(t // tm,),
        in_specs=[ANY_SPEC, tok(D_MODEL), pl.BlockSpec((D_MODEL, D_MODEL), lambda i: (0, 0)), tok(D_FOX), tok(D_GDN),
                  pl.BlockSpec((tm, D_GDN), lambda i: (i, COL_GZ // D_GDN)), vec(LANES), vec(LANES)],
        out_specs=[tok(D_FOX), tok(D_GDN), pl.BlockSpec((tm, D_GDN), lambda i: (i, COL_GZ // D_GDN)), vec(LANES),
                   vec(LANES)],
        out_shape=[_sds((t, D_FOX)), _sds((t, D_GDN)), _sds((t, PROJ_W), BF), _sds((1, LANES)), _sds((1, LANES))],
        compiler_params=_params("arbitrary"),
    )(after, dmixed, w_out, o_fox, gdn_o, proj, fnw, gnw)


def _mlp_up(h2, w_upt):
    t = h2.shape[0]
    tm = min(MATMUL_BLOCK, t)

    def body(h_ref, w_ref, up_ref):
        up_ref[...] = lax.dot_general(h_ref[...], w_ref[...], (((1,), (1,)), ((), ())),
                                      preferred_element_type=F32).astype(BF)

    return pl.pallas_call(
        body, name="mlp_up", grid=(t // tm,),
        in_specs=[pl.BlockSpec((tm, D_MODEL), lambda i: (i, 0)), pl.BlockSpec((D_FF, D_MODEL), lambda i: (0, 0))],
        out_specs=pl.BlockSpec((tm, D_FF), lambda i: (i, 0)), out_shape=_sds((t, D_FF), BF),
        compiler_params=_params("parallel"),
    )(h2, w_upt)


def _mlp_down_loss(up, w_down, x1, pw, target):
    t = up.shape[0]
    tm = min(MATMUL_BLOCK, t)

    def body(up_ref, w_ref, x1_ref, pw_ref, tg_ref, dy_ref, dx2_ref, loss_ref, dpw_ref):
        i = pl.program_id(0)

        @pl.when(i == 0)
        def _():
            loss_ref[...] = jnp.zeros_like(loss_ref)
            dpw_ref[...] = jnp.zeros_like(dpw_ref)

        u = jnp.maximum(up_ref[...].astype(F32), 0.0)
        y = jnp.dot((u * u).astype(BF), w_ref[...], preferred_element_type=F32)
        r = lax.rsqrt(jnp.mean(y * y, axis=-1, keepdims=True) + EPS)
        yh = y * r
        pw = pw_ref[...]
        err = x1_ref[...] + yh * pw - tg_ref[...]
        part = jnp.sum(jnp.sum(err * err, axis=-1, keepdims=True), axis=0, keepdims=True) * (0.5 / D_MODEL)
        loss_ref[...] += jnp.broadcast_to(part, loss_ref.shape)
        dx2 = err * (1.0 / D_MODEL)
        dx2_ref[...] = dx2
        dpw_ref[...] += jnp.sum(dx2 * yh, axis=0, keepdims=True)
        dyh = dx2 * pw
        dy_ref[...] = (r * (dyh - yh * jnp.mean(dyh * yh, axis=-1, keepdims=True))).astype(BF)

    tok = lambda w: pl.BlockSpec((tm, w), lambda i: (i, 0))
    vec = lambda w: pl.BlockSpec((1, w), lambda i: (0, 0))
    return pl.pallas_call(
        body, name="mlp_down_loss", grid=(t // tm,),
        in_specs=[tok(D_FF), pl.BlockSpec((D_FF, D_MODEL), lambda i: (0, 0)), tok(D_MODEL), vec(D_MODEL), tok(D_MODEL)],
        out_specs=[tok(D_MODEL), tok(D_MODEL), vec(LANES), vec(D_MODEL)],
        out_shape=[_sds((t, D_MODEL), BF), _sds((t, D_MODEL)), _sds((1, LANES)), _sds((1, D_MODEL))],
        compiler_params=_params("arbitrary"),
    )(up, w_down, x1, pw, target)


def _mlp_bwd_act(dy, w_down, up):
    t = dy.shape[0]
    tm = min(MATMUL_BLOCK, t)

    def body(dy_ref, w_ref, up_ref, dup_ref):
        da = lax.dot_general(dy_ref[...], w_ref[...], (((1,), (1,)), ((), ())), preferred_element_type=F32)
        dup_ref[...] = (da * (2.0 * jnp.maximum(up_ref[...].astype(F32), 0.0))).astype(BF)

    return pl.pallas_call(
        body, name="mlp_bwd_act", grid=(t // tm,),
        in_specs=[pl.BlockSpec((tm, D_MODEL), lambda i: (i, 0)), pl.BlockSpec((D_FF, D_MODEL), lambda i: (0, 0)),
                  pl.BlockSpec((tm, D_FF), lambda i: (i, 0))],
        out_specs=pl.BlockSpec((tm, D_FF), lambda i: (i, 0)), out_shape=_sds((t, D_FF), BF),
        compiler_params=_params("parallel"),
    )(dy, w_down, up)


def _mlp_bwd_in(dup, w_up, x1, plw, dx2, mixed, pmw, after):
    t = dup.shape[0]
    tm = min(MATMUL_BLOCK, t)

    def body(dup_ref, w_ref, x1_ref, plw_ref, dx2_ref, mx_ref, pmw_ref, dx1_ref, dmixed_ref, dplw_ref, dpmw_ref):
        i = pl.program_id(0)

        @pl.when(i == 0)
        def _():
            dplw_ref[...] = jnp.zeros_like(dplw_ref)
            dpmw_ref[...] = jnp.zeros_like(dpmw_ref)

        dh = jnp.dot(dup_ref[...], w_ref[...], preferred_element_type=F32)
        x1 = x1_ref[...]
        r = lax.rsqrt(jnp.mean(x1 * x1, axis=-1, keepdims=True) + EPS)
        xh = x1 * r
        dplw_ref[...] += jnp.sum(dh * xh, axis=0, keepdims=True)
        dxh = dh * plw_ref[...]
        dx1 = dx2_ref[...] + r * (dxh - xh * jnp.mean(dxh * xh, axis=-1, keepdims=True))
        dx1_ref[...] = dx1
        mx = mx_ref[...]
        r2 = lax.rsqrt(jnp.mean(mx * mx, axis=-1, keepdims=True) + EPS)
        mh = mx * r2
        dpmw_ref[...] += jnp.sum(dx1 * mh, axis=0, keepdims=True)
        dmh = dx1 * pmw_ref[...]
        dmixed_ref[...] = (r2 * (dmh - mh * jnp.mean(dmh * mh, axis=-1, keepdims=True))).astype(BF)

    tok = lambda w: pl.BlockSpec((tm, w), lambda i: (i, 0))
    vec = lambda w: pl.BlockSpec((1, w), lambda i: (0, 0))
    return pl.pallas_call(
        _ordered(body), name="mlp_bwd_in", grid=(t // tm,),
        in_specs=[ANY_SPEC, tok(D_FF), pl.BlockSpec((D_FF, D_MODEL), lambda i: (0, 0)), tok(D_MODEL),
                  vec(D_MODEL), tok(D_MODEL), tok(D_MODEL), vec(D_MODEL)],
        out_specs=[tok(D_MODEL), tok(D_MODEL), vec(D_MODEL), vec(D_MODEL)],
        out_shape=[_sds((t, D_MODEL)), _sds((t, D_MODEL), BF), _sds((1, D_MODEL)), _sds((1, D_MODEL))],
        compiler_params=_params("arbitrary"),
    )(after, dup, w_up, x1, plw, dx2, mixed, pmw)


def _wgrad(a, b, a_cols, split=1, a_fn=None, a_block0=0, name="wgrad"):
    t, b_cols = b.shape
    n_a = (a.shape[1] - a_block0 * a_cols) // a_cols if a_block0 else a.shape[1] // a_cols

    def body(a_ref, b_ref, o_ref):
        av = a_ref[...]
        if a_fn is not None:
            av = a_fn(av)
        o_ref[...] = _mm_tn(av, b_ref[...]).astype(BF).reshape(o_ref.shape)

    return pl.pallas_call(
        body, name=name, grid=(n_a,),
        in_specs=[pl.BlockSpec((t, a_cols), lambda i: (0, i + a_block0)), pl.BlockSpec((t, b_cols), lambda i: (0, 0))],
        out_specs=pl.BlockSpec((split, a_cols // split, b_cols), lambda i: (i, 0, 0)),
        out_shape=_sds((n_a * split, a_cols // split, b_cols), BF),
        compiler_params=_params("parallel"),
    )(a, b)


def _wgrad_pre_t(at, b, b_cols, name):
    rows, t = at.shape
    n_b = b.shape[1] // b_cols

    def body(a_ref, b_ref, o_ref):
        o_ref[0] = jnp.dot(a_ref[...], b_ref[...], preferred_element_type=F32).astype(BF)

    return pl.pallas_call(
        body, name=name, grid=(n_b,),
        in_specs=[pl.BlockSpec((rows, t), lambda j: (0, 0)), pl.BlockSpec((t, b_cols), lambda j: (0, j))],
        out_specs=pl.BlockSpec((1, rows, b_cols), lambda j: (j, 0, 0)), out_shape=_sds((n_b, rows, b_cols), BF),
        compiler_params=_params("parallel"),
    )(at, b)


def _small_bwd(proj, fb, al, dtb, dcq, dckt, dbe, dge, dproj):
    t = proj.shape[0]

    def body(sm_ref, fb_ref, al_ref, dtb_ref, dcq_ref, dckt_ref, dbe_ref, dge_ref, _, dsm_ref, dvec_ref):
        s = sm_ref[...]
        lane = _iota((1, LANES), 1)
        dcum = dcq_ref[...] - dckt_ref[...].T
        row = _iota((t, LANES), 0)
        step = 1
        while step < t:
            dcum = dcum + _shift_up(dcum, step, row)
            step *= 2
        dff = dcum * _sigmoid(-(s + fb_ref[...]))
        dbeta = jnp.zeros((t, LANES), F32)
        dg = jnp.zeros((t, LANES), F32)
        for hd in range(N_GDN_HEADS):
            dbeta = jnp.where(lane == SM_GB + hd, dbe_ref[:, hd * LANES:hd * LANES + 1], dbeta)
            dg = jnp.where(lane == SM_GA + hd, dge_ref[:, hd * LANES:hd * LANES + 1], dg)
        beta = _sigmoid(s)
        dgb = dbeta * beta * (1.0 - beta)
        za = s + dtb_ref[...]
        nea = -jnp.exp(al_ref[...])
        dga = dg * nea * _sigmoid(za)
        is_f = lane < SM_GB
        is_b = (lane >= SM_GB) & (lane < SM_GA)
        is_a = (lane >= SM_GA) & (lane < SM_GA + 4)
        dsm_ref[...] = jnp.where(is_f, dff, jnp.where(is_b, dgb, jnp.where(is_a, dga, 0.0))).astype(BF)
        dvec_ref[...] = jnp.zeros_like(dvec_ref)
        dvec_ref[0:1, :] = jnp.sum(jnp.where(is_f, dff, 0.0), axis=0, keepdims=True)
        dvec_ref[1:2, :] = jnp.sum(jnp.where(is_a, dg * nea * _softplus(za), 0.0), axis=0, keepdims=True)
        dvec_ref[2:3, :] = jnp.sum(jnp.where(is_a, dga, 0.0), axis=0, keepdims=True)

    vec = pl.BlockSpec((1, LANES), lambda i: (0, 0))
    full = lambda r, c: pl.BlockSpec((r, c), lambda i: (0, 0))
    small = pl.BlockSpec((t, LANES), lambda i: (0, COL_SMALL // LANES))
    return pl.pallas_call(
        body, name="small_bwd", grid=(1,),
        in_specs=[small, vec, vec, vec, full(t, LANES), full(LANES, t), full(t, 512), full(t, 512), ANY_SPEC],
        out_specs=[small, full(8, LANES)], out_shape=[_sds(dproj.shape, BF), _sds((8, LANES))],
        input_output_aliases={8: 0}, compiler_params=_params("arbitrary"),
    )(proj, fb, al, dtb, dcq, dckt, dbe, dge, dproj)


def _in_bwd(dproj, wt_al, x, nw, dx1, after):
    t = x.shape[0]
    tm = min(MATMUL_BLOCK, t)

    def body(dp_ref, w_ref, x_ref, nw_ref, dx1_ref, dx_ref, dnw_ref):
        i = pl.program_id(0)

        @pl.when(i == 0)
        def _():
            dnw_ref[...] = jnp.zeros_like(dnw_ref)

        dh = jnp.dot(dp_ref[...], w_ref[...], preferred_element_type=F32)
        xv = x_ref[...]
        r = lax.rsqrt(jnp.mean(xv * xv, axis=-1, keepdims=True) + EPS)
        xh = xv * r
        dnw_ref[...] += jnp.sum(dh * xh, axis=0, keepdims=True)
        dxh = dh * nw_ref[...]
        dx_ref[...] = dx1_ref[...] + r * (dxh - xh * jnp.mean(dxh * xh, axis=-1, keepdims=True))

    tok = lambda w: pl.BlockSpec((tm, w), lambda i: (i, 0))
    vec = lambda w: pl.BlockSpec((1, w), lambda i: (0, 0))
    return pl.pallas_call(
        _ordered(body), name="in_bwd", grid=(t // tm,),
        in_specs=[ANY_SPEC, tok(PROJ_W), pl.BlockSpec((PROJ_W, D_MODEL), lambda i: (0, 0)), tok(D_MODEL), vec(D_MODEL),
                  tok(D_MODEL)],
        out_specs=[tok(D_MODEL), vec(D_MODEL)], out_shape=[_sds((t, D_MODEL)), _sds((1, D_MODEL))],
        compiler_params=_params("arbitrary"),
    )(after, dproj, wt_al, x, nw, dx1)


def _row(v, width=None):
    v = v.reshape(1, -1).astype(F32)
    if width is not None and v.shape[1] < width:
        v = jnp.pad(v, ((0, 0), (0, width - v.shape[1])))
    return v


def _lane_vec(v, first):
    return jnp.pad(v.astype(F32), (first, LANES - first - v.shape[0])).reshape(1, LANES)


def _local_step(x, target, wt_al, started, late_weights, on_grads, convw, pre_mix_norm, fox_f_bias, fox_out_norm,
                gdn_a_log, gdn_dt_bias, gdn_out_norm, post_mix_norm, pre_mlp_norm, post_mlp_norm):
    t = x.shape[0]
    nch = t // CHUNK
    nw, pmw, plw, pw = _row(pre_mix_norm), _row(post_mix_norm), _row(pre_mlp_norm), _row(post_mlp_norm)
    fb, al, dtb = _lane_vec(fox_f_bias, SM_FF), _lane_vec(gdn_a_log, SM_GA), _lane_vec(gdn_dt_bias, SM_GA)
    fnw = _row(jnp.tile(fox_out_norm, 2))
    gnw = _row(gdn_out_norm)

    proj, h = _norm_proj(x, nw, wt_al, started)
    cumt, beta, g = _small_prep(proj, fb, al, dtb)
    qn, kn, cv, gc, be, mmat, amat = _gdn_prep(proj, convw, beta, g)
    n_prob = N_GDN_HEADS * nch
    m3 = mmat.reshape(n_prob, CHUNK, CHUNK)
    if n_prob < LANES:
        m3 = jnp.pad(m3, ((0, LANES - n_prob), (0, 0), (0, 0)))
    tinv = _tri_inverse(m3)[:n_prob].reshape(N_GDN_HEADS, nch, CHUNK, CHUNK)
    gdn_o, s_all, vn_all = _gdn_scan(qn, kn, cv, be, gc, tinv, amat)
    o_fox, lse, fox_n = _fox_fwd(proj, cumt, fnw, gdn_o)
    token = late_weights("mlp_relay", fox_n)
    w_out = late_weights("w_out", fox_n)
    x1, h2, mixed, omix, h2t = _mix_out(fox_n, gdn_o, proj, gnw, w_out, x, pmw, plw, token)
    w_up, w_down = late_weights("mlp", h2)
    up = _mlp_up(h2, w_up)
    dy, dx2, loss, d_pw = _mlp_down_loss(up, w_down, x1, pw, target)

    dup = _mlp_bwd_act(dy, w_down, up)
    relu2 = lambda u: jnp.square(jnp.maximum(u.astype(F32), 0.0))
    g_down = _wgrad(up, dy, D_FF // N_DEV, a_fn=relu2, name="wgrad_down")
    g_up = _wgrad_pre_t(h2t, dup, D_FF // N_DEV, name="wgrad_up")
    token = on_grads("mlp", (g_up, g_down))
    dx1, dmixed, d_plw, d_pmw = _mlp_bwd_in(dup, w_up, x1, plw, dx2, mixed, pmw, token)
    token = on_grads("w_out", _wgrad(omix, dmixed, 512, split=4, name="wgrad_out"))
    do_fox, dgo, dproj, d_fnw, d_gnw = _out_bwd(dmixed, w_out, o_fox, gdn_o, proj, fnw, gnw, token)
    dproj, dcq, dckt = _fox_bwd(proj, cumt, lse, o_fox, do_fox, dproj)
    dqn, dkn, dcv, dbe, dge = _gdn_bwd(qn, kn, cv, be, gc, tinv, amat, s_all, vn_all, dgo)
    dproj, dwq, dwk, dwv = _gdn_bwd_conv(proj, convw, dqn, dkn, dcv, dproj)
    dproj, dvec = _small_bwd(proj, fb, al, dtb, dcq, dckt, dbe, dge, dproj)
    g_main = _wgrad(dproj, h, WGRAD_IN_ROWS, name="wgrad_in")
    g_tail = _wgrad(dproj, h, LANES, a_block0=COL_SMALL // LANES, name="wgrad_in_small")
    token = on_grads("w_in", (g_main, g_tail))
    grad_x, d_nw = _in_bwd(dproj, wt_al, x, nw, dx1, token)
    small = dict(norms=(d_nw, d_pmw, d_plw, d_pw), fox_out_norm=d_fnw, gdn_out_norm=d_gnw, loss=loss, vectors=dvec,
                 conv=(dwq, dwk, dwv))
    return grad_x, small


MESH_IDS = pl.DeviceIdType.MESH
CHIP_FLIPS = ((0, 0), (1, 0), (0, 1), (1, 1))


def _place():
    return lax.axis_index("x"), lax.axis_index("y"), lax.axis_index("c")


def _all_gather(blocks, later, dtype):
    n, k = len(blocks), len(later)

    def body(*refs):
        ins, shards, outs = refs[:n], refs[n:n + k], refs[n + k:2 * n + k]
        zones, to_send = refs[2 * n + k:2 * n + 2 * k], refs[2 * n + 2 * k:2 * n + 3 * k]
        stage_in, stage_out = refs[2 * n + 3 * k:2 * n + 4 * k], refs[2 * n + 4 * k:2 * n + 5 * k]
        send_sems, recv_sems, local_sems, late_sems = refs[2 * n + 5 * k:]
        x, y, c = _place()
        sibling = (x, y, 1 - c)
        chips = [(x ^ fx, y ^ fy) for fx, fy in CHIP_FLIPS[1:]]

        def slot(out, px, py, pc):
            return out.at[4 * px + 2 * py + pc]

        def copy(a, k, block, to, src=None):
            return pltpu.make_async_remote_copy(
                src_ref=slot(outs[a], *block) if src is None else src, dst_ref=slot(outs[a], *block),
                send_sem=send_sems.at[a, k], recv_sem=recv_sems.at[a, k], device_id=to, device_id_type=MESH_IDS)

        pending = []
        for a in range(n):
            mine = pltpu.make_async_copy(ins[a], slot(outs[a], x, y, c), local_sems.at[a])
            mine.start()
            pending.append(mine)
        sends = []
        for a in range(n):
            first = [copy(a, 0, (x, y, c), sibling, src=ins[a])]
            first += [copy(a, 1 + j, (x, y, c), (*chip, c), src=ins[a]) for j, chip in enumerate(chips)]
            for cp in first:
                cp.start()
            sends += first
        loads = [pltpu.make_async_copy(shards[a], stage_in[a], late_sems.at[a, 0]) for a in range(k)]
        for cp in loads:
            cp.start()
        for a, (_, transposed) in enumerate(later):
            loads[a].wait()
            val = stage_in[a][...]
            stage_out[a][...] = (val.T if transposed else val).astype(dtype)
            for j, dst in enumerate((slot(zones[a], x, y, c), to_send[a])):
                cp = pltpu.make_async_copy(stage_out[a], dst, late_sems.at[a, 1 + j])
                cp.start()
                pending.append(cp)
        for a in range(n):
            for j, chip in enumerate(chips):
                copy(a, 1 + j, (*chip, c), (x, y, c)).wait_recv()
                fwd = copy(a, 4 + j, (*chip, c), sibling)
                fwd.start()
                sends.append(fwd)
        for a in range(n):
            copy(a, 0, sibling, (x, y, c)).wait_recv()
            for j, chip in enumerate(chips):
                copy(a, 4 + j, (*chip, 1 - c), (x, y, c)).wait_recv()
        for cp in sends:
            cp.wait_send()
        for cp in pending:
            cp.wait()

    shapes = [s_.shape[::-1] if transposed else s_.shape for s_, transposed in later]
    out = pl.pallas_call(
        body, name="all_gather_weights", in_specs=[ANY_SPEC] * (n + k), out_specs=[ANY_SPEC] * (n + 2 * k),
        out_shape=[_sds((N_DEV,) + b.shape, b.dtype) for b in blocks] + [_sds((N_DEV,) + sh, dtype) for sh in shapes]
        + [_sds(sh, dtype) for sh in shapes],
        scratch_shapes=[pltpu.VMEM(s_.shape, s_.dtype) for s_, _ in later] + [pltpu.VMEM(sh, dtype) for sh in shapes]
        + [pltpu.SemaphoreType.DMA((n, 7)), pltpu.SemaphoreType.DMA((n, 7)), pltpu.SemaphoreType.DMA((n,)),
           pltpu.SemaphoreType.DMA((k, 3))],
        compiler_params=pltpu.CompilerParams(vmem_limit_bytes=VMEM_LIMIT, has_side_effects=True),
    )(*blocks, *[s_ for s_, _ in later])
    return out[:n], out[n:n + k], out[n + k:]


def _adamw(w, g, m, v):
    m = ADAM_B1 * m + (1.0 - ADAM_B1) * g
    v = ADAM_B2 * v + (1.0 - ADAM_B2) * (g * g)
    m_hat = m / (1.0 - ADAM_B1 ** ADAM_STEP)
    v_hat = v / (1.0 - ADAM_B2 ** ADAM_STEP)
    return -ADAM_LR * (m_hat / (jnp.sqrt(v_hat) + ADAM_EPS) + ADAM_WD * w), m, v


def _pair_reduce(g, name):
    _, r, c_ = g.shape
    n = len(CHIP_FLIPS)

    def body(g_ref, out_ref, sib_buf, send_sems, recv_sems):
        x, y, c = _place()
        chips = [(x ^ fx, y ^ fy) for fx, fy in CHIP_FLIPS]
        piece = lambda chip, core: g_ref.at[4 * chip[0] + 2 * chip[1] + core]
        copies = [pltpu.make_async_remote_copy(
            src_ref=piece(chip, 1 - c), dst_ref=sib_buf.at[j], send_sem=send_sems.at[j], recv_sem=recv_sems.at[j],
            device_id=(x, y, 1 - c), device_id_type=MESH_IDS) for j, chip in enumerate(chips)]
        for cp in copies:
            cp.start()
        for j, chip in enumerate(chips):
            copies[j].wait_recv()
            out_ref[j] = (piece(chip, c)[...].astype(F32) + sib_buf[j].astype(F32)).astype(BF)
        for cp in copies:
            cp.wait_send()

    return pl.pallas_call(
        body, name=name, in_specs=[VMEM_SPEC], out_specs=VMEM_SPEC, out_shape=_sds((n, r, c_), BF),
        scratch_shapes=[pltpu.VMEM((n, r, c_), BF), pltpu.SemaphoreType.DMA((n,)), pltpu.SemaphoreType.DMA((n,))],
        compiler_params=pltpu.CompilerParams(vmem_limit_bytes=VMEM_LIMIT, has_side_effects=True),
    )(g)


HBM_SPEC = pl.BlockSpec(memory_space=pltpu.HBM)
SEM_SPEC = pl.BlockSpec(memory_space=pltpu.SEMAPHORE)
DATAFLOW = pltpu.SideEffectType.DATAFLOW_SIDE_EFFECTING


def _peers():
    x, y, c = _place()
    return 4 * x + 2 * y + c, [(x ^ (k >> 2), y ^ ((k >> 1) & 1), c ^ (k & 1)) for k in range(1, N_DEV)]


def _peer_index(peer):
    return 4 * peer[0] + 2 * peer[1] + peer[2]


def _exchange_start(srcs, zones, pieces, name, chips=False):
    n = len(srcs)
    fresh = zones is None
    if fresh:
        slots = len(CHIP_FLIPS) if chips else N_DEV
        zones = [_sds((slots,) + (v.shape[1:] if pieces else v.shape), v.dtype) for v in srcs]
    n_in = n if fresh else 2 * n
    among_chips = list(chips) if isinstance(chips, (list, tuple)) else [chips] * n

    def body(*refs):
        ins, sems, token = refs[:n], refs[n_in:n_in + 2 * n], refs[-1]
        zs = refs[n_in + 3 * n:n_in + 4 * n] if fresh else refs[n:2 * n]
        me, peers = _peers()
        x, y, c = _place()
        for a in range(n):
            if among_chips[a] and pieces:
                routes = [((x ^ fx, y ^ fy, c), j, j) for j, (fx, fy) in enumerate(CHIP_FLIPS) if j]
            elif among_chips[a]:
                routes = [((x ^ fx, y ^ fy, c), None, me) for fx, fy in CHIP_FLIPS[1:]]
            else:
                routes = [(peer, _peer_index(peer) if pieces else None, me) for peer in peers]
            for peer, src_slot, dst_slot in routes:
                pltpu.make_async_remote_copy(
                    src_ref=ins[a] if src_slot is None else ins[a].at[src_slot], dst_ref=zs[a].at[dst_slot],
                    send_sem=sems[2 * a], recv_sem=sems[2 * a + 1], device_id=peer, device_id_type=MESH_IDS).start()
        token[...] = jnp.zeros_like(token)

    hbm = lambda v: pltpu.with_memory_space_constraint(v, pltpu.HBM)
    out = pl.pallas_call(
        body, name=name,
        out_shape=tuple([pltpu.SemaphoreType.DMA(())] * (2 * n) + [pltpu.HBM(v.shape, v.dtype) for v in srcs]
                        + [pltpu.HBM(z.shape, z.dtype) for z in zones] + [_sds((8, LANES))]),
        in_specs=[HBM_SPEC] * n_in, out_specs=tuple([SEM_SPEC] * (2 * n) + [HBM_SPEC] * (2 * n) + [VMEM_SPEC]),
        input_output_aliases={i: 2 * n + i for i in range(n_in)},
        compiler_params=pltpu.CompilerParams(has_side_effects=DATAFLOW),
    )(*[hbm(v) for v in srcs], *([] if fresh else [hbm(z) for z in zones]))
    return out[:2 * n], out[2 * n:3 * n], out[3 * n:4 * n], out[-1]


def _relay_start(zones, name):
    n = len(zones)

    def body(*refs):
        zs, sems, token = refs[:n], refs[n:3 * n], refs[-1]
        x, y, c = _place()
        for fx, fy in CHIP_FLIPS:
            slot = 4 * (x ^ fx) + 2 * (y ^ fy) + c
            for a in range(n):
                pltpu.make_async_remote_copy(
                    src_ref=zs[a].at[slot], dst_ref=zs[a].at[slot], send_sem=sems[2 * a], recv_sem=sems[2 * a + 1],
                    device_id=(x, y, 1 - c), device_id_type=MESH_IDS).start()
        token[...] = jnp.zeros_like(token)

    out = pl.pallas_call(
        body, name=name,
        out_shape=tuple([pltpu.SemaphoreType.DMA(())] * (2 * n) + [pltpu.HBM(z.shape, z.dtype) for z in zones]
                        + [_sds((8, LANES))]),
        in_specs=[HBM_SPEC] * n, out_specs=tuple([SEM_SPEC] * (2 * n) + [HBM_SPEC] * n + [VMEM_SPEC]),
        input_output_aliases={i: 2 * n + i for i in range(n)},
        compiler_params=pltpu.CompilerParams(has_side_effects=DATAFLOW),
    )(*[pltpu.with_memory_space_constraint(z, pltpu.HBM) for z in zones])
    return out[:2 * n], [], out[2 * n:3 * n], out[-1]


def _exchange_wait(sems, srcs, zones, after, name, chips=False, n_copies=None):
    n, n_src = len(zones), len(srcs)
    after = list(after) if isinstance(after, (list, tuple)) else [after]
    n_copies = n_copies or (len(CHIP_FLIPS) - 1 if chips else N_DEV - 1)

    def body(*refs):
        zs, sm = refs[n_src:n_src + n], refs[n_src + n:n_src + 3 * n]
        me, peers = _peers()
        for a in range(n):
            seven = zs[a].at[pl.ds(0, n_copies)]
            cp = pltpu.make_async_remote_copy(src_ref=seven, dst_ref=seven, send_sem=sm[2 * a], recv_sem=sm[2 * a + 1],
                                              device_id=peers[0], device_id_type=MESH_IDS)
            cp.wait_send()
            cp.wait_recv()

    out = pl.pallas_call(
        body, name=name, out_shape=tuple([pltpu.HBM(v.shape, v.dtype) for v in srcs] + [pltpu.HBM(z.shape, z.dtype) for z in zones]),
        in_specs=[HBM_SPEC] * (n_src + n) + [SEM_SPEC] * (2 * n) + [ANY_SPEC] * len(after),
        out_specs=tuple([HBM_SPEC] * (n_src + n)), input_output_aliases={i: i for i in range(n_src + n)},
        compiler_params=pltpu.CompilerParams(has_side_effects=DATAFLOW),
    )(*srcs, *zones, *sems, *after)
    return out[:n_src], out[n_src:]


def _sum_adamw(zone, own, w, m, v, name, chips=False):
    n_slots, r, c_ = zone.shape
    rb = next((b for b in (256, 128) if r % b == 0), r)

    def body(me_ref, z_ref, own_ref, w_ref, m_ref, v_ref, grad_ref, delta_ref, nm_ref, nv_ref):
        total = None
        for d in range(n_slots):
            part = jnp.where(me_ref[0] == d, own_ref[0], z_ref[d]).astype(F32)
            total = part if total is None else total + part
        grad_ref[...] = total
        delta_ref[...], nm_ref[...], nv_ref[...] = _adamw(w_ref[...], total, m_ref[...], v_ref[...])

    x, y, c = _place()
    mine = 0 * x if chips else 4 * x + 2 * y + c
    blk = pl.BlockSpec((rb, c_), lambda i, me_ref: (i, 0))
    return pl.pallas_call(
        body, name=name,
        grid_spec=pltpu.PrefetchScalarGridSpec(
            num_scalar_prefetch=1, grid=(r // rb,),
            in_specs=[pl.BlockSpec((n_slots, rb, c_), lambda i, me_ref: (0, i, 0)),
                      pl.BlockSpec((1, rb, c_), lambda i, me_ref: (me_ref[0], i, 0)), blk, blk, blk],
            out_specs=[blk] * 4),
        out_shape=[_sds((r, c_))] * 4, compiler_params=_params("parallel"),
    )(mine.astype(jnp.int32).reshape(1), zone, own, w, m, v)


SMALL_NORMS = ("pre_mix_norm", "post_mix_norm", "pre_mlp_norm", "post_mlp_norm")
SMALL_ORDER = SMALL_NORMS + ("fox_out_norm", "gdn_out_norm", "fox_f_bias", "gdn_a_log", "gdn_dt_bias", "gdn_conv_w")
CONV_SLAB_ROWS, CONV_SLAB_LANES = 8, 256


def _small_pack(small):
    def body(n0, n1, n2, n3, fnw_ref, gnw_ref, loss_ref, vec_ref, out_ref):
        out_ref[...] = jnp.zeros_like(out_ref)
        for i, ref in enumerate((n0, n1, n2, n3)):
            out_ref[i:i + 1, :] = ref[...]
        out_ref[4:5, 0:LANES] = fnw_ref[...]
        out_ref[4:5, LANES:2 * LANES] = gnw_ref[...]
        out_ref[4:5, 2 * LANES:3 * LANES] = loss_ref[...]
        out_ref[5:8, 0:LANES] = vec_ref[0:3, :]

    return pl.pallas_call(body, name="small_pack", in_specs=[VMEM_SPEC] * 8, out_specs=VMEM_SPEC,
                          out_shape=_sds((8, D_MODEL)))(*small["norms"], small["fox_out_norm"], small["gdn_out_norm"],
                                                        small["loss"], small["vectors"])


def _conv_slabs(dconv):
    blocks = dconv.reshape(CONV_K, N_DEV, -1).transpose(1, 0, 2)
    blocks = jnp.pad(blocks, ((0, 0), (0, CONV_SLAB_ROWS - CONV_K), (0, CONV_SLAB_LANES - blocks.shape[2])))
    return blocks.reshape(N_DEV * CONV_SLAB_ROWS, CONV_SLAB_LANES)


def _small_update(zone, conv_zone, own, own_conv, w, m, v):
    n = len(SMALL_ORDER)
    n_conv = w["gdn_conv_w"].shape[1]

    def body(me_ref, z_ref, zc_ref, own_ref, ownc_ref, *refs):
        params, loss_ref, outs, (tot, totc) = refs[:3 * n], refs[3 * n], refs[3 * n + 1:7 * n + 1], refs[-2:]
        total, total_c = None, None
        for d in range(N_DEV):
            part = jnp.where(me_ref[0] == d, own_ref[...], z_ref[d])
            part_c = jnp.where(me_ref[0] == d, ownc_ref[...], zc_ref[d])
            total, total_c = (part, part_c) if d == 0 else (total + part, total_c + part_c)
        tot[...] = total
        totc[...] = total_c
        loss_ref[...] = tot[4, 2 * LANES:2 * LANES + 1]
        mine = totc[pl.ds(pl.multiple_of(me_ref[0] * CONV_SLAB_ROWS, CONV_SLAB_ROWS), CONV_SLAB_ROWS), :]
        g = dict(zip(SMALL_NORMS, (tot[0], tot[1], tot[2], tot[3])))
        g.update(fox_out_norm=tot[4, 0:FOX_HEAD_DIM], gdn_out_norm=tot[4, LANES:LANES + GDN_HEAD_DIM],
                 fox_f_bias=tot[5, SM_FF:SM_FF + N_FOX_HEADS], gdn_a_log=tot[6, SM_GA:SM_GA + N_GDN_HEADS],
                 gdn_dt_bias=tot[7, SM_GA:SM_GA + N_GDN_HEADS], gdn_conv_w=mine[0:CONV_K, 0:n_conv])
        for i, name in enumerate(SMALL_ORDER):
            w_ref, m_ref, v_ref = params[3 * i:3 * i + 3]
            outs[4 * i][...] = g[name]
            outs[4 * i + 1][...], outs[4 * i + 2][...], outs[4 * i + 3][...] = _adamw(w_ref[...], g[name], m_ref[...],
                                                                                     v_ref[...])

    x, y, c = _place()
    operands = [a[name] for name in SMALL_ORDER for a in (w, m, v)]
    out = pl.pallas_call(
        body, name="small_update",
        in_specs=[pl.BlockSpec(memory_space=pltpu.SMEM)] + [VMEM_SPEC] * (4 + 3 * n), out_specs=[VMEM_SPEC] * (1 + 4 * n),
        out_shape=[_sds((1,))] + [_sds(w[name].shape) for name in SMALL_ORDER for _ in range(4)],
        scratch_shapes=[pltpu.VMEM(zone.shape[1:], F32), pltpu.VMEM(conv_zone.shape[1:], F32)],
    )((4 * x + 2 * y + c).astype(jnp.int32).reshape(1), zone, conv_zone, own, own_conv, *operands)
    return out[0][0], {name: out[1 + 4 * i:5 + 4 * i] for i, name in enumerate(SMALL_ORDER)}


def _native_rows():
    groups = []
    for first, n_groups in ((0, N_FOX_HEADS // 2), (D_FOX * 3 + N_FOX_HEADS, N_GDN_HEADS)):
        for g in range(n_groups):
            groups += [(first + part * n_groups * LANES + g * LANES, first + part * n_groups * LANES + (g + 1) * LANES)
                       for part in range(3)]
    return tuple(groups) + ((3088, 3600), (1536, 1544), (3080, 3088))


NATIVE_ROWS = _native_rows()


W_IN_PIECE = D_PROJ // N_DEV
WGRAD_IN_ROWS = 512
SHUFFLE_LANES = 256


def _to_aligned_moves():
    moves, o = [], 0
    for lo, hi in NATIVE_ROWS:
        r = lo
        while r < hi:
            d = r // W_IN_PIECE
            k = min(hi, (d + 1) * W_IN_PIECE) - r
            moves.append((0, d, r - d * W_IN_PIECE, 0, o, k))
            r, o = r + k, o + k
    return moves


def _from_aligned_moves():
    moves = []
    for _, d, a, _, o, k in _to_aligned_moves():
        while k:
            n = min(k, WGRAD_IN_ROWS - o % WGRAD_IN_ROWS) if o < COL_SMALL else k
            moves.append((0, o // WGRAD_IN_ROWS, o % WGRAD_IN_ROWS, d, a, n) if o < COL_SMALL else
                         (1, 0, o - COL_SMALL, d, a, n))
            o, a, k = o + n, a + n, k - n
    return moves


def _shuffle_rows(srcs, moves, out_shape, name):
    c = srcs[0].shape[-1]

    def body(*refs):
        s_refs, o_ref, s_f, o_f = refs[:len(srcs)], refs[len(srcs)], refs[len(srcs) + 1:-1], refs[-1]
        for s_ref, f in zip(s_refs, s_f):
            f[...] = s_ref[...].astype(F32)
        o_f[...] = jnp.zeros_like(o_f)
        for i, ss, so, ds, do, k in moves:
            o_f[ds, pl.ds(do, k), :] = s_f[i][ss, pl.ds(so, k), :]
        o_ref[...] = o_f[...].astype(BF)

    blk = lambda shape: pl.BlockSpec(tuple(shape[:-1]) + (SHUFFLE_LANES,), lambda j: (0, 0, j))
    scratch = lambda shape: pltpu.VMEM(tuple(shape[:-1]) + (SHUFFLE_LANES,), F32)
    return pl.pallas_call(
        body, name=name, grid=(c // SHUFFLE_LANES,), in_specs=[blk(s.shape) for s in srcs], out_specs=blk(out_shape),
        out_shape=_sds(out_shape, BF), scratch_shapes=[scratch(s.shape) for s in srcs] + [scratch(out_shape)],
        compiler_params=_params("parallel"),
    )(*srcs)


def _cols_from_pieces(p):
    return p.transpose(1, 0, 2).reshape(p.shape[1], -1)


WEIGHT_ORDER = ("pre_mix_norm", "w_in", "fox_f_bias", "fox_out_norm", "gdn_conv_w", "gdn_a_log", "gdn_dt_bias",
                "gdn_out_norm", "w_out", "post_mix_norm", "pre_mlp_norm", "w_up", "w_down", "post_mlp_norm")


def kernel(x, pre_mix_norm, w_in, fox_f_bias, fox_out_norm, gdn_conv_w, gdn_a_log, gdn_dt_bias, gdn_out_norm, w_out, post_mix_norm, pre_mlp_norm, w_up, w_down, post_mlp_norm, loss_target, m_pre_mix_norm, m_w_in, m_fox_f_bias, m_fox_out_norm, m_gdn_conv_w, m_gdn_a_log, m_gdn_dt_bias, m_gdn_out_norm, m_w_out, m_post_mix_norm, m_pre_mlp_norm, m_w_up, m_w_down, m_post_mlp_norm, v_pre_mix_norm, v_w_in, v_fox_f_bias, v_fox_out_norm, v_gdn_conv_w, v_gdn_a_log, v_gdn_dt_bias, v_gdn_out_norm, v_w_out, v_post_mix_norm, v_pre_mlp_norm, v_w_up, v_w_down, v_post_mlp_norm):
    w = dict(pre_mix_norm=pre_mix_norm, w_in=w_in, fox_f_bias=fox_f_bias, fox_out_norm=fox_out_norm,
             gdn_conv_w=gdn_conv_w, gdn_a_log=gdn_a_log, gdn_dt_bias=gdn_dt_bias, gdn_out_norm=gdn_out_norm, w_out=w_out,
             post_mix_norm=post_mix_norm, pre_mlp_norm=pre_mlp_norm, w_up=w_up, w_down=w_down, post_mlp_norm=post_mlp_norm)
    mom = dict(pre_mix_norm=m_pre_mix_norm, w_in=m_w_in, fox_f_bias=m_fox_f_bias, fox_out_norm=m_fox_out_norm,
               gdn_conv_w=m_gdn_conv_w, gdn_a_log=m_gdn_a_log, gdn_dt_bias=m_gdn_dt_bias, gdn_out_norm=m_gdn_out_norm,
               w_out=m_w_out, post_mix_norm=m_post_mix_norm, pre_mlp_norm=m_pre_mlp_norm, w_up=m_w_up, w_down=m_w_down,
               post_mlp_norm=m_post_mlp_norm)
    var = dict(pre_mix_norm=v_pre_mix_norm, w_in=v_w_in, fox_f_bias=v_fox_f_bias, fox_out_norm=v_fox_out_norm,
               gdn_conv_w=v_gdn_conv_w, gdn_a_log=v_gdn_a_log, gdn_dt_bias=v_gdn_dt_bias, gdn_out_norm=v_gdn_out_norm,
               w_out=v_w_out, post_mix_norm=v_post_mix_norm, pre_mlp_norm=v_pre_mlp_norm, w_up=v_w_up, w_down=v_w_down,
               post_mlp_norm=v_post_mlp_norm)

    (win_g, conv_g), zones, shards = _all_gather([w_in.T.astype(BF), gdn_conv_w],
                                                 [(w_out, False), (w_up, True), (w_down, False)], BF)
    wt_al = _shuffle_rows([win_g], _to_aligned_moves(), (1, PROJ_W, D_MODEL), "w_in_to_aligned")[0]
    convw = _cols_from_pieces(conv_g)
    sems, shards, zones, after = _exchange_start(shards, zones, False, "gather_start", chips=[False, True, True])
    gathers = dict(w_out=(sems[:2], shards[:1], zones[:1], after), mlp=(sems[2:], shards[1:], zones[1:], after))

    def late_weights(name, after):
        if name == "mlp_relay":
            sems, shards, zones, _ = gathers["mlp"]
            _, zones = _exchange_wait(sems, shards, zones, after, "gather_mlp_wait", chips=True)
            gathers["mlp"] = _relay_start(zones, "gather_mlp_relay")
            return gathers["mlp"][3]
        sems, shards, zones, _ = gathers[name]
        _, got = _exchange_wait(sems, shards, zones, after, "gather_" + name + "_done",
                                n_copies=len(CHIP_FLIPS) if name == "mlp" else None)
        if name == "w_out":
            return got[0].reshape(D_MODEL, D_MODEL)
        return got[0].reshape(D_FF, D_MODEL), got[1].reshape(D_FF, D_MODEL)

    scatters = {}

    def on_grads(name, g):
        if name == "mlp":
            scatters["mlp"] = list(g)
            return g[0]
        if name == "w_out":
            sems, srcs, zones, token = _exchange_start(scatters["mlp"] + [g], None, True, "scatter_mlp_w_out_start")
            scatters["mlp"] = (sems[:4], srcs[:2], zones[:2], token)
            scatters["w_out"] = (sems[4:], srcs[2:], zones[2:], token)
            return token
        g = _shuffle_rows(list(g), _from_aligned_moves(), (N_DEV, W_IN_PIECE, D_MODEL), "w_in_grad_from_aligned")
        scatters[name] = _exchange_start([_pair_reduce(g, "pair_reduce_w_in")], None, True, "scatter_w_in_start", chips=True)
        return scatters[name][3]

    grad_x, small = _local_step(
        x[0], loss_target[0], wt_al, after, late_weights, on_grads, convw, pre_mix_norm,
        fox_f_bias, fox_out_norm, gdn_a_log, gdn_dt_bias, gdn_out_norm, post_mix_norm, pre_mlp_norm, post_mlp_norm)
    slabs = [_small_pack(small), _conv_slabs(jnp.concatenate(small["conv"], axis=1))]
    scatters["small"] = _exchange_start(slabs, None, False, "small_start")

    grads, delta, new_m, new_v = {}, {}, {}, {}
    after = scatters["small"][3]
    for name, members in (("mlp", ("w_up", "w_down")), ("w_out", ("w_out",)), ("small", ()), ("w_in", ("w_in",))):
        sems, srcs, zones, _ = scatters[name]
        srcs, zones = _exchange_wait(sems, srcs, zones, after, "scatter_" + name + "_wait", chips=name == "w_in")
        if name == "small":
            loss, updated = _small_update(*zones, *srcs, w, mom, var)
            for n, res in updated.items():
                grads[n], delta[n], new_m[n], new_v[n] = res
            after = grads["pre_mix_norm"]
        for n, zone, own in zip(members, zones, srcs):
            if n == "w_in":
                res = _sum_adamw(zone, own, w[n].T, mom[n].T, var[n].T, "adamw_" + n, chips=True)
                grads[n], delta[n], new_m[n], new_v[n] = [r.T for r in res]
            else:
                grads[n], delta[n], new_m[n], new_v[n] = _sum_adamw(zone, own, w[n], mom[n], var[n], "adamw_" + n)
        if members:
            after = [grads[n] for n in members]

    return (loss, grad_x[None], *[grads[n] for n in WEIGHT_ORDER], *[delta[n] for n in WEIGHT_ORDER],
            *[new_m[n] for n in WEIGHT_ORDER], *[new_v[n] for n in WEIGHT_ORDER])
```

```python
import jax
import jax.numpy as jnp
from jax import lax
from jax.experimental import pallas as pl
from jax.experimental.pallas import tpu as pltpu

F32 = jnp.float32
BF = jnp.bfloat16

D_MODEL = 1024
N_FOX_HEADS, FOX_HEAD_DIM = 8, 64
N_GDN_HEADS, GDN_HEAD_DIM = 4, 128
D_FOX = N_FOX_HEADS * FOX_HEAD_DIM
D_GDN = N_GDN_HEADS * GDN_HEAD_DIM
CHUNK = 64
CONV_K = 4
D_FF = 4 * D_MODEL
EPS = 1e-6
D_PROJ = 3600
N_DEV = 8

PROJ_W = 3712
COL_FOX, COL_GDN, COL_GZ, COL_SMALL = 0, 1536, 3072, 3584
LANES = 128
QKV = 3 * LANES
SM_FF, SM_GB, SM_GA = 0, 8, 12

ADAM_LR, ADAM_B1, ADAM_B2, ADAM_EPS, ADAM_WD, ADAM_STEP = 0.001, 0.9, 0.999, 1e-08, 0.01, 10

TOKEN_BLOCK = 256
MATMUL_BLOCK = 512
TRI_ROWS = 4
FOX_SCALE = FOX_HEAD_DIM ** -0.5
GDN_QSCALE = GDN_HEAD_DIM ** -0.5
NEG_BIG = -1e30
VMEM_LIMIT = 56 * 1024 * 1024

VMEM_SPEC = pl.BlockSpec(memory_space=pltpu.VMEM)
ANY_SPEC = pl.BlockSpec(memory_space=pl.ANY)


def _sds(shape, dtype=F32):
    return jax.ShapeDtypeStruct(shape, dtype)


def _params(*sem):
    return pltpu.CompilerParams(dimension_semantics=sem if sem else None, vmem_limit_bytes=VMEM_LIMIT)


def _ordered(body):
    def ordered(_, *refs):
        body(*refs)

    return ordered


def _mm(a, b):
    return jnp.dot(a.astype(BF), b.astype(BF), preferred_element_type=F32)


def _mm_nt(a, b):
    return lax.dot_general(a.astype(BF), b.astype(BF), (((1,), (1,)), ((), ())), preferred_element_type=F32)


def _mm_tn(a, b):
    return lax.dot_general(a.astype(BF), b.astype(BF), (((0,), (0,)), ((), ())), preferred_element_type=F32)


def _sigmoid(x):
    return 1.0 / (1.0 + jnp.exp(-x))


def _softplus(x):
    return jnp.maximum(x, 0.0) + jnp.log1p(jnp.exp(-jnp.abs(x)))


def _iota(shape, dim):
    return lax.broadcasted_iota(jnp.int32, shape, dim)


def _shift_down(x, s, row):
    return jnp.where(row >= s, pltpu.roll(x, s, 0), 0.0)


def _shift_up(x, s, row):
    n = x.shape[0]
    return jnp.where(row < n - s, pltpu.roll(x, n - s, 0), 0.0)


def _norm_proj(x, nw, wt_al, after):
    t = x.shape[0]

    def body(x_ref, nw_ref, w_ref, proj_ref, h_ref):
        xv = x_ref[...]
        r = lax.rsqrt(jnp.mean(xv * xv, axis=-1, keepdims=True) + EPS)
        h = (xv * r * nw_ref[...]).astype(BF)
        h_ref[...] = h
        proj_ref[...] = lax.dot_general(h, w_ref[...], (((1,), (1,)), ((), ())), preferred_element_type=F32)

    tm = min(MATMUL_BLOCK, t)
    return pl.pallas_call(
        _ordered(body), name="norm_proj", grid=(t // tm,),
        in_specs=[ANY_SPEC, pl.BlockSpec((tm, D_MODEL), lambda i: (i, 0)), pl.BlockSpec((1, D_MODEL), lambda i: (0, 0)),
                  pl.BlockSpec((PROJ_W, D_MODEL), lambda i: (0, 0))],
        out_specs=[pl.BlockSpec((tm, PROJ_W), lambda i: (i, 0)), pl.BlockSpec((tm, D_MODEL), lambda i: (i, 0))],
        out_shape=[_sds((t, PROJ_W)), _sds((t, D_MODEL), BF)],
        compiler_params=_params("parallel"),
    )(after, x, nw, wt_al)


def _lane_column(x, lane):
    return jnp.sum(jnp.where(_iota((1, LANES), 1) == lane, x, 0.0), axis=-1, keepdims=True)


def _small_prep(proj, fb, al, dtb):
    t = proj.shape[0]

    def body(sm_ref, fb_ref, al_ref, dtb_ref, cumt_ref, beta_ref, g_ref):
        s = sm_ref[...]
        z = s + fb_ref[...]
        cum = jnp.minimum(z, 0.0) - jnp.log1p(jnp.exp(-jnp.abs(z)))
        row = _iota((t, LANES), 0)
        step = 1
        while step < t:
            cum = cum + _shift_down(cum, step, row)
            step *= 2
        cumt_ref[...] = cum.T
        beta_ref[...] = _sigmoid(s)
        g_ref[...] = -jnp.exp(al_ref[...]) * _softplus(s + dtb_ref[...])

    vec = pl.BlockSpec((1, LANES), lambda i: (0, 0))
    tok = pl.BlockSpec((t, LANES), lambda i: (0, 0))
    return pl.pallas_call(
        body, name="small_prep", grid=(1,),
        in_specs=[pl.BlockSpec((t, LANES), lambda i: (0, COL_SMALL // LANES)), vec, vec, vec],
        out_specs=[pl.BlockSpec((LANES, t), lambda i: (0, 0)), tok, tok],
        out_shape=[_sds((LANES, t)), _sds((t, LANES)), _sds((t, LANES))],
        compiler_params=_params("arbitrary"),
    )(proj, fb, al, dtb)


def _fox_stack(x, first):
    return jnp.concatenate([jnp.where(first, x, 0.0), jnp.where(first, 0.0, x)], axis=0).astype(BF)


def _fox_unstack(y, first):
    n = y.shape[0] // 2
    return jnp.where(first, y[:n], y[n:])


def _fox_logits(q2_i, kb, cumt_ref, pair, i, tq):
    klen = (i + 1) * tq
    s = lax.dot_general(q2_i, kb[:klen], (((1,), (1,)), ((), ())), preferred_element_type=F32)
    upper = _iota((2 * tq, 1), 0) < tq
    s = s - jnp.where(upper, cumt_ref[pl.ds(2 * pair, 1), 0:klen], cumt_ref[pl.ds(2 * pair + 1, 1), 0:klen])
    causal = _iota((2 * tq, tq), 1) <= _iota((2 * tq, tq), 0) % tq
    parts = [(s[:, :klen - tq], 0, klen - tq)] if i else []
    return parts + [(jnp.where(causal, s[:, klen - tq:], NEG_BIG), klen - tq, klen)]


def _fox_fwd(proj, cumt, fnw, after):
    t = proj.shape[0]
    tq = min(TOKEN_BLOCK, t // 2)
    nq = t // tq

    def body(q_ref, k_ref, v_ref, cumt_ref, fnw_ref, o_ref, lse_ref, fn_ref):
        j = pl.program_id(0)
        first = _iota((1, LANES), 1) < FOX_HEAD_DIM
        kb = k_ref[...].astype(BF)
        vb = v_ref[...].astype(BF)
        for i in range(nq):
            rows = slice(i * tq, (i + 1) * tq)
            q2 = _fox_stack(q_ref[rows, :] * FOX_SCALE, first)
            parts = _fox_logits(q2, kb, cumt_ref, j, i, tq)
            m = jnp.max(parts[-1][0], axis=-1, keepdims=True)
            if i:
                m = jnp.maximum(m, jnp.max(parts[0][0], axis=-1, keepdims=True))
            l = jnp.zeros((2 * tq, 1), F32)
            o = jnp.zeros((2 * tq, LANES), F32)
            for s, lo, hi in parts:
                p = jnp.exp(s - m)
                l = l + jnp.sum(p, axis=-1, keepdims=True)
                o = o + jnp.dot(p.astype(BF), vb[lo:hi], preferred_element_type=F32)
            o_acc = _fox_unstack(o / l, first)
            lse_acc = _fox_unstack(jnp.broadcast_to(m + jnp.log(l), (2 * tq, LANES)), first)
            o_ref[rows, :] = o_acc
            lse_ref[rows, :] = lse_acc
            o2 = o_acc * o_acc
            s0 = jnp.sum(jnp.where(first, o2, 0.0), axis=-1, keepdims=True)
            s1 = jnp.sum(jnp.where(first, 0.0, o2), axis=-1, keepdims=True)
            r = lax.rsqrt(jnp.where(first, s0, s1) * (1.0 / FOX_HEAD_DIM) + EPS)
            fn_ref[rows, :] = (o_acc * r * fnw_ref[...]).astype(BF)

    qkv = lambda k: pl.BlockSpec((t, LANES), lambda j: (0, COL_FOX // LANES + 3 * j + k))
    pair = pl.BlockSpec((t, LANES), lambda j: (0, j))
    return pl.pallas_call(
        _ordered(body), name="fox_fwd", grid=(N_FOX_HEADS // 2,),
        in_specs=[ANY_SPEC, qkv(0), qkv(1), qkv(2), pl.BlockSpec((LANES, t), lambda j: (0, 0)),
                  pl.BlockSpec((1, LANES), lambda j: (0, 0))],
        out_specs=[pair, pair, pair],
        out_shape=[_sds((t, D_FOX)), _sds((t, D_FOX)), _sds((t, D_FOX), BF)],
        compiler_params=_params("parallel"),
    )(after, proj, proj, proj, cumt, fnw)


def _fox_bwd(proj, cumt, lse, o, do, dproj):
    t = proj.shape[0]
    tq = min(TOKEN_BLOCK, t // 2)
    nq = t // tq

    def body(q_ref, k_ref, v_ref, cumt_ref, lse_ref, o_ref, do_ref, _, dqkv_ref, dcq_ref, dckt_ref, dk_s, dv_s):
        j = pl.program_id(0)

        @pl.when(j == 0)
        def _():
            dcq_ref[...] = jnp.zeros_like(dcq_ref)
            dckt_ref[...] = jnp.zeros_like(dckt_ref)

        lane = _iota((1, LANES), 1)

        first = _iota((1, LANES), 1) < FOX_HEAD_DIM
        kb = k_ref[...].astype(BF)
        vb = v_ref[...].astype(BF)
        dk_s[...] = jnp.zeros_like(dk_s)
        dv_s[...] = jnp.zeros_like(dv_s)
        for i in range(nq):
            rows = slice(i * tq, (i + 1) * tq)
            do_i = do_ref[rows, :]
            prod = do_i * o_ref[rows, :]
            lse_i = lse_ref[rows, :]
            q2 = _fox_stack(q_ref[rows, :] * FOX_SCALE, first)
            do2 = _fox_stack(do_i, first)
            delta = jnp.concatenate([jnp.sum(jnp.where(first, prod, 0.0), axis=-1, keepdims=True),
                                     jnp.sum(jnp.where(first, 0.0, prod), axis=-1, keepdims=True)], axis=0)
            lse2 = jnp.concatenate([lse_i[:, 0:1], lse_i[:, FOX_HEAD_DIM:FOX_HEAD_DIM + 1]], axis=0)
            dq2 = jnp.zeros((2 * tq, LANES), F32)
            dcq2 = jnp.zeros((2 * tq, 1), F32)
            for s, lo, hi in _fox_logits(q2, kb, cumt_ref, j, i, tq):
                p = jnp.exp(s - lse2)
                ds = p * (_mm_nt(do2, vb[lo:hi]) - delta)
                dsb = ds.astype(BF)
                dq2 = dq2 + jnp.dot(dsb, kb[lo:hi], preferred_element_type=F32)
                dk_s[lo:hi, :] += _mm_tn(dsb, q2)
                dv_s[lo:hi, :] += _mm_tn(p, do2)
                dcq2 = dcq2 + jnp.sum(ds, axis=-1, keepdims=True)
                dckt_ref[pl.ds(2 * j, 1), lo:hi] += jnp.sum(ds[:tq], axis=0, keepdims=True)
                dckt_ref[pl.ds(2 * j + 1, 1), lo:hi] += jnp.sum(ds[tq:], axis=0, keepdims=True)
            dqkv_ref[rows, 0:LANES] = (_fox_unstack(dq2, first) * FOX_SCALE).astype(BF)
            dcq_ref[rows, :] += jnp.where(lane == 2 * j, dcq2[:tq], jnp.where(lane == 2 * j + 1, dcq2[tq:], 0.0))
        dqkv_ref[:, LANES:2 * LANES] = dk_s[...].astype(BF)
        dqkv_ref[:, 2 * LANES:QKV] = dv_s[...].astype(BF)

    qkv = lambda k: pl.BlockSpec((t, LANES), lambda j: (0, COL_FOX // LANES + 3 * j + k))
    pair = pl.BlockSpec((t, LANES), lambda j: (0, j))
    rows128 = pl.BlockSpec((LANES, t), lambda j: (0, 0))
    return pl.pallas_call(
        body, name="fox_bwd", grid=(N_FOX_HEADS // 2,),
        in_specs=[qkv(0), qkv(1), qkv(2), rows128, pair, pair, pair, ANY_SPEC],
        out_specs=[pl.BlockSpec((t, QKV), lambda j: (0, COL_FOX // QKV + j)),
                   pl.BlockSpec((t, LANES), lambda j: (0, 0)), rows128],
        out_shape=[_sds(dproj.shape, BF), _sds((t, LANES)), _sds((LANES, t))],
        scratch_shapes=[pltpu.VMEM((t, LANES), F32), pltpu.VMEM((t, LANES), F32)],
        input_output_aliases={7: 0}, compiler_params=_params("arbitrary"),
    )(proj, proj, proj, cumt, lse, o, do, dproj)


def _conv(x, w, row):
    return (w[3:4, :] * x + w[2:3, :] * _shift_down(x, 1, row) + w[1:2, :] * _shift_down(x, 2, row)
            + w[0:1, :] * _shift_down(x, 3, row))


def _chunk_decay(gc_c):
    gi = gc_c[:, 0:CHUNK]
    gj = gc_c.T[0:CHUNK, :]
    ri = _iota((CHUNK, CHUNK), 0)
    cj = _iota((CHUNK, CHUNK), 1)
    return jnp.where(ri >= cj, jnp.exp(jnp.minimum(gi - gj, 0.0)), 0.0), ri > cj


def _gdn_specs(t):
    col = lambda off: pl.BlockSpec((t, LANES), lambda h: (0, off + h))
    cw = lambda off: pl.BlockSpec((CONV_K, LANES), lambda h: (0, off + h))
    mat = pl.BlockSpec((1, t // CHUNK, CHUNK, CHUNK), lambda h: (h, 0, 0, 0))
    qkv = lambda k: pl.BlockSpec((t, LANES), lambda h: (0, COL_GDN // LANES + 3 * h + k))
    return col, cw, mat, qkv


def _gdn_prep(proj, convw, beta, g):
    t = proj.shape[0]
    nch = t // CHUNK

    def body(xq_ref, xk_ref, xv_ref, wq_ref, wk_ref, wv_ref, beta_ref, g_ref,
             qn_ref, kn_ref, cv_ref, gc_ref, be_ref, m_ref, a_ref):
        row = _iota((t, LANES), 0)
        hd = pl.program_id(0)
        be_ref[...] = jnp.broadcast_to(_lane_column(beta_ref[...], SM_GB + hd), (t, LANES))

        def act(x_ref, w_ref):
            y = _conv(x_ref[...], w_ref[...], row)
            return y * _sigmoid(y)

        cq = act(xq_ref, wq_ref)
        ck = act(xk_ref, wk_ref)
        cv_ref[...] = act(xv_ref, wv_ref)
        qn_ref[...] = cq * lax.rsqrt(jnp.sum(cq * cq, axis=-1, keepdims=True) + EPS) * GDN_QSCALE
        kn_ref[...] = ck * lax.rsqrt(jnp.sum(ck * ck, axis=-1, keepdims=True) + EPS)
        gc = jnp.broadcast_to(_lane_column(g_ref[...], SM_GA + hd), (t, LANES))
        pos = row % CHUNK
        step = 1
        while step < CHUNK:
            gc = gc + jnp.where(pos >= step, pltpu.roll(gc, step, 0), 0.0)
            step *= 2
        gc_ref[...] = gc

        group = 4 if nch % 4 == 0 else 1

        def chunks(gi, carry):
            ns = [gi * group + c for c in range(group)]
            sls = [pl.ds(pl.multiple_of(n * CHUNK, CHUNK), CHUNK) for n in ns]
            ks = [kn_ref[sl, :] for sl in sls]
            kk = [_mm_nt(k_c * be_ref[sl, :], k_c) for k_c, sl in zip(ks, sls)]
            qk = [_mm_nt(qn_ref[sl, :], k_c) for k_c, sl in zip(ks, sls)]
            for c, n in enumerate(ns):
                decay, strict = _chunk_decay(gc_ref[sls[c], :])
                m_ref[0, n] = jnp.where(strict, kk[c] * decay, 0.0)
                a_ref[0, n] = qk[c] * decay
            return carry

        lax.fori_loop(0, nch // group, chunks, 0)

    col, cw, mat, qkv = _gdn_specs(t)
    return pl.pallas_call(
        body, name="gdn_prep", grid=(N_GDN_HEADS,),
        in_specs=[qkv(0), qkv(1), qkv(2), cw(0), cw(4), cw(8)] + [pl.BlockSpec((t, LANES), lambda h: (0, 0))] * 2,
        out_specs=[col(0), col(0), col(0), col(0), col(0), mat, mat],
        out_shape=[_sds((t, D_GDN))] * 5 + [_sds((N_GDN_HEADS, nch, CHUNK, CHUNK))] * 2,
        compiler_params=_params("parallel"),
    )(proj, proj, proj, convw, convw, convw, beta, g)


def _tri_inverse(m3):
    assert m3.shape == (LANES, CHUNK, CHUNK)

    def body(m_ref, t_ref, ms, ts):
        for i in range(CHUNK):
            ms[i * CHUNK:(i + 1) * CHUNK, :] = m_ref[:, i, :].T
        cidx = _iota((CHUNK, LANES), 0)

        def t_row(j):
            return ts[pl.ds(pl.multiple_of(j * CHUNK, CHUNK), CHUNK), :]

        def outer(ib, carry):
            i0 = ib * TRI_ROWS

            def inner(jj, accs):
                earlier = t_row(jj)
                return tuple(acc - ms[pl.ds((i0 + r) * CHUNK + jj, 1), :] * earlier for r, acc in enumerate(accs))

            accs = list(lax.fori_loop(
                0, i0, inner, tuple(jnp.where(cidx == i0 + r, 1.0, 0.0).astype(F32) for r in range(TRI_ROWS))))
            for r in range(TRI_ROWS):
                for q in range(r):
                    accs[r] = accs[r] - ms[pl.ds((i0 + r) * CHUNK + i0 + q, 1), :] * accs[q]
                ts[pl.ds(pl.multiple_of((i0 + r) * CHUNK, CHUNK), CHUNK), :] = accs[r]
            return carry

        lax.fori_loop(0, CHUNK // TRI_ROWS, outer, 0)
        for i in range(CHUNK):
            t_ref[:, i, :] = ts[i * CHUNK:(i + 1) * CHUNK, :].T

    return pl.pallas_call(
        body, name="tri_inverse", in_specs=[VMEM_SPEC], out_specs=VMEM_SPEC,
        out_shape=_sds((LANES, CHUNK, CHUNK)),
        scratch_shapes=[pltpu.VMEM((CHUNK * CHUNK, LANES), F32), pltpu.VMEM((CHUNK * CHUNK, LANES), F32)],
        compiler_params=_params(),
    )(m3)


def _gdn_chunk_terms(q, k, v, b, gcc):
    eg = jnp.exp(gcc)
    last = gcc[CHUNK - 1:CHUNK, :]
    egl = jnp.exp(last - gcc)
    gl = jnp.exp(last)
    kb = k * b
    return eg, egl, gl, kb, v * b, kb * eg, q * eg, k * egl


GDN_BLOCK_CHUNKS = 4


def _gdn_block_specs(t, reverse):
    cb = GDN_BLOCK_CHUNKS
    nb = t // (cb * CHUNK)
    idx = (lambda i: nb - 1 - i) if reverse else (lambda i: i)
    tok = pl.BlockSpec((cb * CHUNK, D_GDN), lambda i: (idx(i), 0))
    mat = pl.BlockSpec((N_GDN_HEADS, cb, CHUNK, CHUNK), lambda i: (0, idx(i), 0, 0))
    state = pl.BlockSpec((N_GDN_HEADS, cb, GDN_HEAD_DIM, GDN_HEAD_DIM), lambda i: (0, idx(i), 0, 0))
    return nb, tok, mat, state


def _gdn_scan(qn, kn, cv, be, gc, tinv, amat):
    t = qn.shape[0]
    nch = t // CHUNK

    def body(q_ref, k_ref, v_ref, b_ref, gc_ref, t_ref, a_ref, o_ref, sall_ref, vn_ref, s_scr):
        @pl.when(pl.program_id(0) == 0)
        def _():
            s_scr[...] = jnp.zeros_like(s_scr)

        heads = range(N_GDN_HEADS)
        cols = [slice(hd * LANES, (hd + 1) * LANES) for hd in heads]
        s = [s_scr[hd] for hd in heads]
        for cc in range(GDN_BLOCK_CHUNKS):
            rs = slice(cc * CHUNK, (cc + 1) * CHUNK)
            terms = [_gdn_chunk_terms(q_ref[rs, cs], k_ref[rs, cs], v_ref[rs, cs], b_ref[rs, cs], gc_ref[rs, cs])
                     for cs in cols]
            for hd in heads:
                sall_ref[hd, cc] = s[hd]
            uw = [_mm(t_ref[hd, cc], jnp.concatenate([terms[hd][4], terms[hd][5]], axis=1)) for hd in heads]
            ws_qs = [_mm(jnp.concatenate([uw[hd][:, LANES:], terms[hd][6]], axis=0), s[hd]) for hd in heads]
            vn = [uw[hd][:, :LANES] - ws_qs[hd][:CHUNK] for hd in heads]
            a_vn = [_mm(a_ref[hd, cc], vn[hd]) for hd in heads]
            kd_vn = [_mm_tn(terms[hd][7], vn[hd]) for hd in heads]
            for hd in heads:
                vn_ref[rs, cols[hd]] = vn[hd]
                o_ref[rs, cols[hd]] = ws_qs[hd][CHUNK:] + a_vn[hd]
                s[hd] = s[hd] * terms[hd][2] + kd_vn[hd]
        for hd in heads:
            s_scr[hd] = s[hd]

    nb, tok, mat, state = _gdn_block_specs(t, False)
    return pl.pallas_call(
        body, name="gdn_scan", grid=(nb,),
        in_specs=[tok] * 5 + [mat, mat], out_specs=[tok, state, tok],
        out_shape=[_sds((t, D_GDN)), _sds((N_GDN_HEADS, nch, GDN_HEAD_DIM, GDN_HEAD_DIM)), _sds((t, D_GDN))],
        scratch_shapes=[pltpu.VMEM((N_GDN_HEADS, GDN_HEAD_DIM, GDN_HEAD_DIM), F32)],
        compiler_params=_params("arbitrary"),
    )(qn, kn, cv, be, gc, tinv, amat)


def _gdn_bwd(qn, kn, cv, be, gc, tinv, amat, s_all, vn_all, do):
    t = qn.shape[0]

    def body(q_ref, k_ref, v_ref, b_ref, gc_ref, t_ref, a_ref, sall_ref, vn_ref, do_ref,
             dq_ref, dk_ref, dv_ref, db_ref, dg_ref, ds_scr):
        @pl.when(pl.program_id(0) == 0)
        def _():
            ds_scr[...] = jnp.zeros_like(ds_scr)

        lastrow = _iota((CHUNK, LANES), 0) == CHUNK - 1
        heads = range(N_GDN_HEADS)
        cols = [slice(hd * LANES, (hd + 1) * LANES) for hd in heads]
        each = lambda fn: [fn(hd) for hd in heads]
        rows_cat = lambda x, y: jnp.concatenate([x, y], axis=0)
        lane_cat = lambda x, y: jnp.concatenate([x, y], axis=1)
        dsp = each(lambda hd: ds_scr[hd])
        for cc in reversed(range(GDN_BLOCK_CHUNKS)):
            rs = slice(cc * CHUNK, (cc + 1) * CHUNK)
            q = each(lambda hd: q_ref[rs, cols[hd]])
            k = each(lambda hd: k_ref[rs, cols[hd]])
            v = each(lambda hd: v_ref[rs, cols[hd]])
            b = each(lambda hd: b_ref[rs, cols[hd]])
            gcc = each(lambda hd: gc_ref[rs, cols[hd]])
            do_c = each(lambda hd: do_ref[rs, cols[hd]])
            vn = each(lambda hd: vn_ref[rs, cols[hd]])
            tn = each(lambda hd: t_ref[hd, cc])
            st = each(lambda hd: sall_ref[hd, cc])
            terms = each(lambda hd: _gdn_chunk_terms(q[hd], k[hd], v[hd], b[hd], gcc[hd]))
            eg, egl, gl, kb, vb, kbg, qd, kd = [[terms[hd][i] for hd in heads] for i in range(8)]
            w = each(lambda hd: _mm(tn[hd], kbg[hd]))
            a_do = each(lambda hd: _mm_tn(a_ref[hd, cc], do_c[hd]))
            kd_ds = each(lambda hd: _mm(kd[hd], dsp[hd]))
            da = each(lambda hd: _mm_nt(do_c[hd], vn[hd]))
            dkd = each(lambda hd: _mm_nt(vn[hd], dsp[hd]))
            by_k = each(lambda hd: _mm_nt(rows_cat(kb[hd], q[hd]), k[hd]))
            dgl = each(lambda hd: jnp.sum(jnp.sum(dsp[hd] * st[hd], axis=-1, keepdims=True), axis=0, keepdims=True))
            dvn = each(lambda hd: a_do[hd] + kd_ds[hd])
            do_dvn = each(lambda hd: rows_cat(do_c[hd], dvn[hd]))
            by_s = each(lambda hd: _mm_nt(do_dvn[hd], st[hd]))
            dqd = each(lambda hd: by_s[hd][:CHUNK])
            dvn_dw = each(lambda hd: lane_cat(dvn[hd], -by_s[hd][CHUNK:]))
            dsp = each(lambda hd: _mm_tn(rows_cat(qd[hd], -w[hd]), do_dvn[hd]) + gl[hd] * dsp[hd])
            dt = each(lambda hd: _mm_nt(dvn_dw[hd], lane_cat(vb[hd], kbg[hd])))
            by_t = each(lambda hd: _mm_tn(tn[hd], dvn_dw[hd]))
            tt_dt = each(lambda hd: _mm_tn(tn[hd], dt[hd]))
            dm_raw = each(lambda hd: _mm_nt(tt_dt[hd], tn[hd]))
            masks = each(lambda hd: _chunk_decay(gcc[hd]))
            dkk = each(lambda hd: jnp.where(masks[hd][1], -dm_raw[hd], 0.0) * masks[hd][0])
            dqk = each(lambda hd: da[hd] * masks[hd][0])
            dqk_dkk = each(lambda hd: rows_cat(dqk[hd], dkk[hd]))
            on_k = each(lambda hd: _mm(dqk_dkk[hd], k[hd]))
            dk_mm = each(lambda hd: _mm_tn(dqk_dkk[hd], rows_cat(q[hd], kb[hd])))
            for hd in heads:
                cs = cols[hd]
                dvb, dkbg = by_t[hd][:, :LANES], by_t[hd][:, LANES:]
                gmat = dkk[hd] * by_k[hd][:CHUNK] + dqk[hd] * by_k[hd][CHUNK:]
                dq_ref[rs, cs] = dqd[hd] * eg[hd] + on_k[hd][:CHUNK]
                dkb = on_k[hd][CHUNK:] + dkbg * eg[hd]
                dk_ref[rs, cs] = dkd[hd] * egl[hd] + dk_mm[hd] + dkb * b[hd]
                db = jnp.sum(dkb * k[hd], axis=-1, keepdims=True) + jnp.sum(dvb * v[hd], axis=-1, keepdims=True)
                db_ref[rs, cs] = jnp.broadcast_to(db, (CHUNK, LANES))
                dv_ref[rs, cs] = dvb * b[hd]
                dkd_kd = jnp.sum(dkd[hd] * kd[hd], axis=-1, keepdims=True)
                col_sums = jnp.sum(lane_cat(gmat, jnp.zeros_like(gmat)).T, axis=-1, keepdims=True)
                dgc = (jnp.sum(gmat, axis=-1, keepdims=True) - col_sums[:CHUNK]
                       + jnp.sum(dqd[hd] * qd[hd], axis=-1, keepdims=True)
                       + jnp.sum(dkbg * kbg[hd], axis=-1, keepdims=True) - dkd_kd)
                extra = jnp.sum(dkd_kd, axis=0, keepdims=True) + dgl[hd] * gl[hd]
                dg_ref[rs, cs] = dgc + jnp.where(lastrow, extra, 0.0)
        for hd in heads:
            ds_scr[hd] = dsp[hd]
        dg = dg_ref[...]
        row = _iota(dg.shape, 0)
        pos = row % CHUNK
        step = 1
        while step < CHUNK:
            dg = dg + jnp.where(pos < CHUNK - step, pltpu.roll(dg, dg.shape[0] - step, 0), 0.0)
            step *= 2
        dg_ref[...] = dg

    nb, tok, mat, state = _gdn_block_specs(t, True)
    return pl.pallas_call(
        body, name="gdn_bwd", grid=(nb,),
        in_specs=[tok] * 5 + [mat, mat, state, tok, tok], out_specs=[tok] * 5, out_shape=[_sds((t, D_GDN))] * 5,
        scratch_shapes=[pltpu.VMEM((N_GDN_HEADS, GDN_HEAD_DIM, GDN_HEAD_DIM), F32)],
        compiler_params=_params("arbitrary"),
    )(qn, kn, cv, be, gc, tinv, amat, s_all, vn_all, do)


def _gdn_bwd_conv(proj, convw, dqn, dkn, dcv, dproj):
    t = proj.shape[0]

    def body(xq_ref, xk_ref, xv_ref, wq_ref, wk_ref, wv_ref, dq_ref, dk_ref, dv_ref, _,
             dqkv_ref, dwq_ref, dwk_ref, dwv_ref):
        row = _iota((t, LANES), 0)

        def one(x_ref, w_ref, d_ref, k, dw_ref, scale):
            x = x_ref[...]
            w = w_ref[...]
            y = _conv(x, w, row)
            sg = _sigmoid(y)
            dc = d_ref[...]
            if scale is not None:
                c = y * sg
                r = lax.rsqrt(jnp.sum(c * c, axis=-1, keepdims=True) + EPS)
                ch = c * r
                dc = scale * r * (dc - ch * jnp.sum(dc * ch, axis=-1, keepdims=True))
            dy = dc * sg * (1.0 + y * (1.0 - sg))
            dqkv_ref[:, k * LANES:(k + 1) * LANES] = (
                w[3:4, :] * dy + w[2:3, :] * _shift_up(dy, 1, row) + w[1:2, :] * _shift_up(dy, 2, row)
                + w[0:1, :] * _shift_up(dy, 3, row)).astype(BF)
            for jj in range(CONV_K):
                xs = x if jj == CONV_K - 1 else _shift_down(x, CONV_K - 1 - jj, row)
                dw_ref[jj:jj + 1, :] = jnp.sum(dy * xs, axis=0, keepdims=True)

        one(xq_ref, wq_ref, dq_ref, 0, dwq_ref, GDN_QSCALE)
        one(xk_ref, wk_ref, dk_ref, 1, dwk_ref, 1.0)
        one(xv_ref, wv_ref, dv_ref, 2, dwv_ref, None)

    col, cw, _, qkv = _gdn_specs(t)
    return pl.pallas_call(
        body, name="gdn_bwd_conv", grid=(N_GDN_HEADS,),
        in_specs=[qkv(0), qkv(1), qkv(2), cw(0), cw(4), cw(8), col(0), col(0), col(0), ANY_SPEC],
        out_specs=[pl.BlockSpec((t, QKV), lambda h: (0, COL_GDN // QKV + h)), cw(0), cw(0), cw(0)],
        out_shape=[_sds(dproj.shape, BF)] + [_sds((CONV_K, D_GDN))] * 3,
        input_output_aliases={9: 0}, compiler_params=_params("parallel"),
    )(proj, proj, proj, convw, convw, convw, dqn, dkn, dcv, dproj)


def _mix_out(fox_n, gdn_o, proj, gnw, w_out, x, pmw, plw, after):
    t = x.shape[0]
    tm = min(MATMUL_BLOCK, t)

    def body(fn_ref, go_ref, gz_ref, gnw_ref, w_ref, x_ref, pmw_ref, plw_ref, x1_ref, h2_ref, mixed_ref, omix_ref,
             h2t_ref):
        omix_ref[:, 0:D_FOX] = fn_ref[...]
        for hd in range(N_GDN_HEADS):
            cs = slice(hd * LANES, (hd + 1) * LANES)
            go = go_ref[:, cs]
            r = lax.rsqrt(jnp.mean(go * go, axis=-1, keepdims=True) + EPS)
            gz = gz_ref[:, cs]
            omix_ref[:, D_FOX + hd * LANES:D_FOX + (hd + 1) * LANES] = (
                go * r * gnw_ref[...] * (gz * _sigmoid(gz))).astype(BF)
        mixed = jnp.dot(omix_ref[...], w_ref[...], preferred_element_type=F32)
        mixed_ref[...] = mixed
        r2 = lax.rsqrt(jnp.mean(mixed * mixed, axis=-1, keepdims=True) + EPS)
        x1 = x_ref[...] + mixed * r2 * pmw_ref[...]
        x1_ref[...] = x1
        r3 = lax.rsqrt(jnp.mean(x1 * x1, axis=-1, keepdims=True) + EPS)
        h2 = x1 * r3 * plw_ref[...]
        h2_ref[...] = h2.astype(BF)
        h2t_ref[...] = h2.T.astype(BF)

    tok = lambda w: pl.BlockSpec((tm, w), lambda i: (i, 0))
    vec = lambda w: pl.BlockSpec((1, w), lambda i: (0, 0))
    return pl.pallas_call(
        _ordered(body), name="mix_out", grid=(t // tm,),
        in_specs=[ANY_SPEC, tok(D_FOX), tok(D_GDN), pl.BlockSpec((tm, D_GDN), lambda i: (i, COL_GZ // D_GDN)), vec(LANES),
                  pl.BlockSpec((D_MODEL, D_MODEL), lambda i: (0, 0)), tok(D_MODEL), vec(D_MODEL), vec(D_MODEL)],
        out_specs=[tok(D_MODEL)] * 4 + [pl.BlockSpec((D_MODEL, tm), lambda i: (0, i))],
        out_shape=[_sds((t, D_MODEL)), _sds((t, D_MODEL), BF), _sds((t, D_MODEL)), _sds((t, D_MODEL), BF),
                   _sds((D_MODEL, t), BF)],
        compiler_params=_params("parallel"),
    )(after, fox_n, gdn_o, proj, gnw, w_out, x, pmw, plw)


def _out_bwd(dmixed, w_out, o_fox, gdn_o, proj, fnw, gnw, after):
    t = dmixed.shape[0]
    tm = min(MATMUL_BLOCK, t)

    def body(dm_ref, w_ref, of_ref, go_ref, gz_ref, fnw_ref, gnw_ref, dof_ref, dgo_ref, dgz_ref, dfw_ref, dgw_ref):
        i = pl.program_id(0)

        @pl.when(i == 0)
        def _():
            dfw_ref[...] = jnp.zeros_like(dfw_ref)
            dgw_ref[...] = jnp.zeros_like(dgw_ref)

        domix = _mm_nt(dm_ref[...], w_ref[...])
        first = _iota((1, LANES), 1) < FOX_HEAD_DIM
        dfw = jnp.zeros((1, LANES), F32)
        dgw = jnp.zeros((1, LANES), F32)
        for pr in range(N_FOX_HEADS // 2):
            cs = slice(pr * LANES, (pr + 1) * LANES)
            o = of_ref[:, cs]
            dfn = domix[:, cs]
            o2 = o * o
            s0 = jnp.sum(jnp.where(first, o2, 0.0), axis=-1, keepdims=True)
            s1 = jnp.sum(jnp.where(first, 0.0, o2), axis=-1, keepdims=True)
            r = lax.rsqrt(jnp.where(first, s0, s1) * (1.0 / FOX_HEAD_DIM) + EPS)
            oh = o * r
            dfw = dfw + jnp.sum(dfn * oh, axis=0, keepdims=True)
            doh = dfn * fnw_ref[...]
            pr_ = doh * oh
            m0 = jnp.sum(jnp.where(first, pr_, 0.0), axis=-1, keepdims=True)
            m1 = jnp.sum(jnp.where(first, 0.0, pr_), axis=-1, keepdims=True)
            dof_ref[:, cs] = r * (doh - oh * jnp.where(first, m0, m1) * (1.0 / FOX_HEAD_DIM))
        for hd in range(N_GDN_HEADS):
            cs = slice(hd * LANES, (hd + 1) * LANES)
            go = go_ref[:, cs]
            gz = gz_ref[:, cs]
            dgated = domix[:, D_FOX + hd * LANES:D_FOX + (hd + 1) * LANES]
            r = lax.rsqrt(jnp.mean(go * go, axis=-1, keepdims=True) + EPS)
            goh = go * r
            sg = _sigmoid(gz)
            sz = gz * sg
            gn = goh * gnw_ref[...]
            dgn = dgated * sz
            dgz_ref[:, cs] = (dgated * gn * sg * (1.0 + gz * (1.0 - sg))).astype(BF)
            dgw = dgw + jnp.sum(dgn * goh, axis=0, keepdims=True)
            dgh = dgn * gnw_ref[...]
            dgo_ref[:, cs] = r * (dgh - goh * jnp.mean(dgh * goh, axis=-1, keepdims=True))
        dfw_ref[...] += dfw + pltpu.roll(dfw, FOX_HEAD_DIM, 1)
        dgw_ref[...] += dgw

    tok = lambda w: pl.BlockSpec((tm, w), lambda i: (i, 0))
    vec = lambda w: pl.BlockSpec((1, w), lambda i: (0, 0))
    return pl.pallas_call(
        _ordered(body), name="out_bwd", grid=(t // tm,),
        in_specs=[ANY_SPEC, tok(D_MODEL), pl.BlockSpec((D_MODEL, D_MODEL), lambda i: (0, 0)), tok(D_FOX), tok(D_GDN),
                  pl.BlockSpec((tm, D_GDN), lambda i: (i, COL_GZ // D_GDN)), vec(LANES), vec(LANES)],
        out_specs=[tok(D_FOX), tok(D_GDN), pl.BlockSpec((tm, D_GDN), lambda i: (i, COL_GZ // D_GDN)), vec(LANES),
                   vec(LANES)],
        out_shape=[_sds((t, D_FOX)), _sds((t, D_GDN)), _sds((t, PROJ_W), BF), _sds((1, LANES)), _sds((1, LANES))],
        compiler_params=_params("arbitrary"),
    )(after, dmixed, w_out, o_fox, gdn_o, proj, fnw, gnw)


def _mlp_up(h2, w_upt):
    t = h2.shape[0]
    tm = min(MATMUL_BLOCK, t)

    def body(h_ref, w_ref, up_ref):
        up_ref[...] = lax.dot_general(h_ref[...], w_ref[...], (((1,), (1,)), ((), ())),
                                      preferred_element_type=F32).astype(BF)

    return pl.pallas_call(
        body, name="mlp_up", grid=(t // tm,),
        in_specs=[pl.BlockSpec((tm, D_MODEL), lambda i: (i, 0)), pl.BlockSpec((D_FF, D_MODEL), lambda i: (0, 0))],
        out_specs=pl.BlockSpec((tm, D_FF), lambda i: (i, 0)), out_shape=_sds((t, D_FF), BF),
        compiler_params=_params("parallel"),
    )(h2, w_upt)


def _mlp_down_loss(up, w_down, x1, pw, target):
    t = up.shape[0]
    tm = min(MATMUL_BLOCK, t)

    def body(up_ref, w_ref, x1_ref, pw_ref, tg_ref, dy_ref, dx2_ref, loss_ref, dpw_ref):
        i = pl.program_id(0)

        @pl.when(i == 0)
        def _():
            loss_ref[...] = jnp.zeros_like(loss_ref)
            dpw_ref[...] = jnp.zeros_like(dpw_ref)

        u = jnp.maximum(up_ref[...].astype(F32), 0.0)
        y = jnp.dot((u * u).astype(BF), w_ref[...], preferred_element_type=F32)
        r = lax.rsqrt(jnp.mean(y * y, axis=-1, keepdims=True) + EPS)
        yh = y * r
        pw = pw_ref[...]
        err = x1_ref[...] + yh * pw - tg_ref[...]
        part = jnp.sum(jnp.sum(err * err, axis=-1, keepdims=True), axis=0, keepdims=True) * (0.5 / D_MODEL)
        loss_ref[...] += jnp.broadcast_to(part, loss_ref.shape)
        dx2 = err * (1.0 / D_MODEL)
        dx2_ref[...] = dx2
        dpw_ref[...] += jnp.sum(dx2 * yh, axis=0, keepdims=True)
        dyh = dx2 * pw
        dy_ref[...] = (r * (dyh - yh * jnp.mean(dyh * yh, axis=-1, keepdims=True))).astype(BF)

    tok = lambda w: pl.BlockSpec((tm, w), lambda i: (i, 0))
    vec = lambda w: pl.BlockSpec((1, w), lambda i: (0, 0))
    return pl.pallas_call(
        body, name="mlp_down_loss", grid=(t // tm,),
        in_specs=[tok(D_FF), pl.BlockSpec((D_FF, D_MODEL), lambda i: (0, 0)), tok(D_MODEL), vec(D_MODEL), tok(D_MODEL)],
        out_specs=[tok(D_MODEL), tok(D_MODEL), vec(LANES), vec(D_MODEL)],
        out_shape=[_sds((t, D_MODEL), BF), _sds((t, D_MODEL)), _sds((1, LANES)), _sds((1, D_MODEL))],
        compiler_params=_params("arbitrary"),
    )(up, w_down, x1, pw, target)


def _mlp_bwd_act(dy, w_down, up):
    t = dy.shape[0]
    tm = min(MATMUL_BLOCK, t)

    def body(dy_ref, w_ref, up_ref, dup_ref):
        da = lax.dot_general(dy_ref[...], w_ref[...], (((1,), (1,)), ((), ())), preferred_element_type=F32)
        dup_ref[...] = (da * (2.0 * jnp.maximum(up_ref[...].astype(F32), 0.0))).astype(BF)

    return pl.pallas_call(
        body, name="mlp_bwd_act", grid=(t // tm,),
        in_specs=[pl.BlockSpec((tm, D_MODEL), lambda i: (i, 0)), pl.BlockSpec((D_FF, D_MODEL), lambda i: (0, 0)),
                  pl.BlockSpec((tm, D_FF), lambda i: (i, 0))],
        out_specs=pl.BlockSpec((tm, D_FF), lambda i: (i, 0)), out_shape=_sds((t, D_FF), BF),
        compiler_params=_params("parallel"),
    )(dy, w_down, up)


def _mlp_bwd_in(dup, w_up, x1, plw, dx2, mixed, pmw, after):
    t = dup.shape[0]
    tm = min(MATMUL_BLOCK, t)

    def body(dup_ref, w_ref, x1_ref, plw_ref, dx2_ref, mx_ref, pmw_ref, dx1_ref, dmixed_ref, dplw_ref, dpmw_ref):
        i = pl.program_id(0)

        @pl.when(i == 0)
        def _():
            dplw_ref[...] = jnp.zeros_like(dplw_ref)
            dpmw_ref[...] = jnp.zeros_like(dpmw_ref)

        dh = jnp.dot(dup_ref[...], w_ref[...], preferred_element_type=F32)
        x1 = x1_ref[...]
        r = lax.rsqrt(jnp.mean(x1 * x1, axis=-1, keepdims=True) + EPS)
        xh = x1 * r
        dplw_ref[...] += jnp.sum(dh * xh, axis=0, keepdims=True)
        dxh = dh * plw_ref[...]
        dx1 = dx2_ref[...] + r * (dxh - xh * jnp.mean(dxh * xh, axis=-1, keepdims=True))
        dx1_ref[...] = dx1
        mx = mx_ref[...]
        r2 = lax.rsqrt(jnp.mean(mx * mx, axis=-1, keepdims=True) + EPS)
        mh = mx * r2
        dpmw_ref[...] += jnp.sum(dx1 * mh, axis=0, keepdims=True)
        dmh = dx1 * pmw_ref[...]
        dmixed_ref[...] = (r2 * (dmh - mh * jnp.mean(dmh * mh, axis=-1, keepdims=True))).astype(BF)

    tok = lambda w: pl.BlockSpec((tm, w), lambda i: (i, 0))
    vec = lambda w: pl.BlockSpec((1, w), lambda i: (0, 0))
    return pl.pallas_call(
        _ordered(body), name="mlp_bwd_in", grid=(t // tm,),
        in_specs=[ANY_SPEC, tok(D_FF), pl.BlockSpec((D_FF, D_MODEL), lambda i: (0, 0)), tok(D_MODEL),
                  vec(D_MODEL), tok(D_MODEL), tok(D_MODEL), vec(D_MODEL)],
        out_specs=[tok(D_MODEL), tok(D_MODEL), vec(D_MODEL), vec(D_MODEL)],
        out_shape=[_sds((t, D_MODEL)), _sds((t, D_MODEL), BF), _sds((1, D_MODEL)), _sds((1, D_MODEL))],
        compiler_params=_params("arbitrary"),
    )(after, dup, w_up, x1, plw, dx2, mixed, pmw)


def _wgrad(a, b, a_cols, split=1, a_fn=None, a_block0=0, name="wgrad"):
    t, b_cols = b.shape
    n_a = (a.shape[1] - a_block0 * a_cols) // a_cols if a_block0 else a.shape[1] // a_cols

    def body(a_ref, b_ref, o_ref):
        av = a_ref[...]
        if a_fn is not None:
            av = a_fn(av)
        o_ref[...] = _mm_tn(av, b_ref[...]).astype(BF).reshape(o_ref.shape)

    return pl.pallas_call(
        body, name=name, grid=(n_a,),
        in_specs=[pl.BlockSpec((t, a_cols), lambda i: (0, i + a_block0)), pl.BlockSpec((t, b_cols), lambda i: (0, 0))],
        out_specs=pl.BlockSpec((split, a_cols // split, b_cols), lambda i: (i, 0, 0)),
        out_shape=_sds((n_a * split, a_cols // split, b_cols), BF),
        compiler_params=_params("parallel"),
    )(a, b)


def _wgrad_pre_t(at, b, b_cols, name):
    rows, t = at.shape
    n_b = b.shape[1] // b_cols

    def body(a_ref, b_ref, o_ref):
        o_ref[0] = jnp.dot(a_ref[...], b_ref[...], preferred_element_type=F32).astype(BF)

    return pl.pallas_call(
        body, name=name, grid=(n_b,),
        in_specs=[pl.BlockSpec((rows, t), lambda j: (0, 0)), pl.BlockSpec((t, b_cols), lambda j: (0, j))],
        out_specs=pl.BlockSpec((1, rows, b_cols), lambda j: (j, 0, 0)), out_shape=_sds((n_b, rows, b_cols), BF),
        compiler_params=_params("parallel"),
    )(at, b)


def _small_bwd(proj, fb, al, dtb, dcq, dckt, dbe, dge, dproj):
    t = proj.shape[0]

    def body(sm_ref, fb_ref, al_ref, dtb_ref, dcq_ref, dckt_ref, dbe_ref, dge_ref, _, dsm_ref, dvec_ref):
        s = sm_ref[...]
        lane = _iota((1, LANES), 1)
        dcum = dcq_ref[...] - dckt_ref[...].T
        row = _iota((t, LANES), 0)
        step = 1
        while step < t:
            dcum = dcum + _shift_up(dcum, step, row)
            step *= 2
        dff = dcum * _sigmoid(-(s + fb_ref[...]))
        dbeta = jnp.zeros((t, LANES), F32)
        dg = jnp.zeros((t, LANES), F32)
        for hd in range(N_GDN_HEADS):
            dbeta = jnp.where(lane == SM_GB + hd, dbe_ref[:, hd * LANES:hd * LANES + 1], dbeta)
            dg = jnp.where(lane == SM_GA + hd, dge_ref[:, hd * LANES:hd * LANES + 1], dg)
        beta = _sigmoid(s)
        dgb = dbeta * beta * (1.0 - beta)
        za = s + dtb_ref[...]
        nea = -jnp.exp(al_ref[...])
        dga = dg * nea * _sigmoid(za)
        is_f = lane < SM_GB
        is_b = (lane >= SM_GB) & (lane < SM_GA)
        is_a = (lane >= SM_GA) & (lane < SM_GA + 4)
        dsm_ref[...] = jnp.where(is_f, dff, jnp.where(is_b, dgb, jnp.where(is_a, dga, 0.0))).astype(BF)
        dvec_ref[...] = jnp.zeros_like(dvec_ref)
        dvec_ref[0:1, :] = jnp.sum(jnp.where(is_f, dff, 0.0), axis=0, keepdims=True)
        dvec_ref[1:2, :] = jnp.sum(jnp.where(is_a, dg * nea * _softplus(za), 0.0), axis=0, keepdims=True)
        dvec_ref[2:3, :] = jnp.sum(jnp.where(is_a, dga, 0.0), axis=0, keepdims=True)

    vec = pl.BlockSpec((1, LANES), lambda i: (0, 0))
    full = lambda r, c: pl.BlockSpec((r, c), lambda i: (0, 0))
    small = pl.BlockSpec((t, LANES), lambda i: (0, COL_SMALL // LANES))
    return pl.pallas_call(
        body, name="small_bwd", grid=(1,),
        in_specs=[small, vec, vec, vec, full(t, LANES), full(LANES, t), full(t, 512), full(t, 512), ANY_SPEC],
        out_specs=[small, full(8, LANES)], out_shape=[_sds(dproj.shape, BF), _sds((8, LANES))],
        input_output_aliases={8: 0}, compiler_params=_params("arbitrary"),
    )(proj, fb, al, dtb, dcq, dckt, dbe, dge, dproj)


def _in_bwd(dproj, wt_al, x, nw, dx1, after):
    t = x.shape[0]
    tm = min(MATMUL_BLOCK, t)

    def body(dp_ref, w_ref, x_ref, nw_ref, dx1_ref, dx_ref, dnw_ref):
        i = pl.program_id(0)

        @pl.when(i == 0)
        def _():
            dnw_ref[...] = jnp.zeros_like(dnw_ref)

        dh = jnp.dot(dp_ref[...], w_ref[...], preferred_element_type=F32)
        xv = x_ref[...]
        r = lax.rsqrt(jnp.mean(xv * xv, axis=-1, keepdims=True) + EPS)
        xh = xv * r
        dnw_ref[...] += jnp.sum(dh * xh, axis=0, keepdims=True)
        dxh = dh * nw_ref[...]
        dx_ref[...] = dx1_ref[...] + r * (dxh - xh * jnp.mean(dxh * xh, axis=-1, keepdims=True))

    tok = lambda w: pl.BlockSpec((tm, w), lambda i: (i, 0))
    vec = lambda w: pl.BlockSpec((1, w), lambda i: (0, 0))
    return pl.pallas_call(
        _ordered(body), name="in_bwd", grid=(t // tm,),
        in_specs=[ANY_SPEC, tok(PROJ_W), pl.BlockSpec((PROJ_W, D_MODEL), lambda i: (0, 0)), tok(D_MODEL), vec(D_MODEL),
                  tok(D_MODEL)],
        out_specs=[tok(D_MODEL), vec(D_MODEL)], out_shape=[_sds((t, D_MODEL)), _sds((1, D_MODEL))],
        compiler_params=_params("arbitrary"),
    )(after, dproj, wt_al, x, nw, dx1)


def _row(v, width=None):
    v = v.reshape(1, -1).astype(F32)
    if width is not None and v.shape[1] < width:
        v = jnp.pad(v, ((0, 0), (0, width - v.shape[1])))
    return v


def _lane_vec(v, first):
    return jnp.pad(v.astype(F32), (first, LANES - first - v.shape[0])).reshape(1, LANES)


def _local_step(x, target, wt_al, started, late_weights, on_grads, convw, pre_mix_norm, fox_f_bias, fox_out_norm,
                gdn_a_log, gdn_dt_bias, gdn_out_norm, post_mix_norm, pre_mlp_norm, post_mlp_norm):
    t = x.shape[0]
    nch = t // CHUNK
    nw, pmw, plw, pw = _row(pre_mix_norm), _row(post_mix_norm), _row(pre_mlp_norm), _row(post_mlp_norm)
    fb, al, dtb = _lane_vec(fox_f_bias, SM_FF), _lane_vec(gdn_a_log, SM_GA), _lane_vec(gdn_dt_bias, SM_GA)
    fnw = _row(jnp.tile(fox_out_norm, 2))
    gnw = _row(gdn_out_norm)

    proj, h = _norm_proj(x, nw, wt_al, started)
    cumt, beta, g = _small_prep(proj, fb, al, dtb)
    qn, kn, cv, gc, be, mmat, amat = _gdn_prep(proj, convw, beta, g)
    n_prob = N_GDN_HEADS * nch
    m3 = mmat.reshape(n_prob, CHUNK, CHUNK)
    if n_prob < LANES:
        m3 = jnp.pad(m3, ((0, LANES - n_prob), (0, 0), (0, 0)))
    tinv = _tri_inverse(m3)[:n_prob].reshape(N_GDN_HEADS, nch, CHUNK, CHUNK)
    gdn_o, s_all, vn_all = _gdn_scan(qn, kn, cv, be, gc, tinv, amat)
    token = late_weights("mlp_relay", gdn_o)
    o_fox, lse, fox_n = _fox_fwd(proj, cumt, fnw, token)
    w_out = late_weights("w_out", fox_n)
    x1, h2, mixed, omix, h2t = _mix_out(fox_n, gdn_o, proj, gnw, w_out, x, pmw, plw, token)
    w_up, w_down = late_weights("mlp", h2)
    up = _mlp_up(h2, w_up)
    dy, dx2, loss, d_pw = _mlp_down_loss(up, w_down, x1, pw, target)

    dup = _mlp_bwd_act(dy, w_down, up)
    relu2 = lambda u: jnp.square(jnp.maximum(u.astype(F32), 0.0))
    g_down = _wgrad(up, dy, D_FF // N_DEV, a_fn=relu2, name="wgrad_down")
    g_up = _wgrad_pre_t(h2t, dup, D_FF // N_DEV, name="wgrad_up")
    token = on_grads("mlp", (g_up, g_down))
    dx1, dmixed, d_plw, d_pmw = _mlp_bwd_in(dup, w_up, x1, plw, dx2, mixed, pmw, token)
    token = on_grads("w_out", _wgrad(omix, dmixed, 512, split=4, name="wgrad_out"))
    do_fox, dgo, dproj, d_fnw, d_gnw = _out_bwd(dmixed, w_out, o_fox, gdn_o, proj, fnw, gnw, token)
    dproj, dcq, dckt = _fox_bwd(proj, cumt, lse, o_fox, do_fox, dproj)
    dqn, dkn, dcv, dbe, dge = _gdn_bwd(qn, kn, cv, be, gc, tinv, amat, s_all, vn_all, dgo)
    dproj, dwq, dwk, dwv = _gdn_bwd_conv(proj, convw, dqn, dkn, dcv, dproj)
    dproj, dvec = _small_bwd(proj, fb, al, dtb, dcq, dckt, dbe, dge, dproj)
    g_main = _wgrad(dproj, h, WGRAD_IN_ROWS, name="wgrad_in")
    g_tail = _wgrad(dproj, h, LANES, a_block0=COL_SMALL // LANES, name="wgrad_in_small")
    token = on_grads("w_in", (g_main, g_tail))
    grad_x, d_nw = _in_bwd(dproj, wt_al, x, nw, dx1, token)
    small = dict(norms=(d_nw, d_pmw, d_plw, d_pw), fox_out_norm=d_fnw, gdn_out_norm=d_gnw, loss=loss, vectors=dvec,
                 conv=(dwq, dwk, dwv))
    return grad_x, small


MESH_IDS = pl.DeviceIdType.MESH
CHIP_FLIPS = ((0, 0), (1, 0), (0, 1), (1, 1))


def _place():
    return lax.axis_index("x"), lax.axis_index("y"), lax.axis_index("c")


def _all_gather(blocks, later, dtype):
    n, k = len(blocks), len(later)

    def body(*refs):
        ins, shards, outs = refs[:n], refs[n:n + k], refs[n + k:2 * n + k]
        zones, to_send = refs[2 * n + k:2 * n + 2 * k], refs[2 * n + 2 * k:2 * n + 3 * k]
        stage_in, stage_out = refs[2 * n + 3 * k:2 * n + 4 * k], refs[2 * n + 4 * k:2 * n + 5 * k]
        send_sems, recv_sems, local_sems, late_sems = refs[2 * n + 5 * k:]
        x, y, c = _place()
        sibling = (x, y, 1 - c)
        chips = [(x ^ fx, y ^ fy) for fx, fy in CHIP_FLIPS[1:]]

        def slot(out, px, py, pc):
            return out.at[4 * px + 2 * py + pc]

        def copy(a, k, block, to, src=None):
            return pltpu.make_async_remote_copy(
                src_ref=slot(outs[a], *block) if src is None else src, dst_ref=slot(outs[a], *block),
                send_sem=send_sems.at[a, k], recv_sem=recv_sems.at[a, k], device_id=to, device_id_type=MESH_IDS)

        pending = []
        for a in range(n):
            mine = pltpu.make_async_copy(ins[a], slot(outs[a], x, y, c), local_sems.at[a])
            mine.start()
            pending.append(mine)
        sends = []
        for a in range(n):
            first = [copy(a, 0, (x, y, c), sibling, src=ins[a])]
            first += [copy(a, 1 + j, (x, y, c), (*chip, c), src=ins[a]) for j, chip in enumerate(chips)]
            for cp in first:
                cp.start()
            sends += first
        loads = [pltpu.make_async_copy(shards[a], stage_in[a], late_sems.at[a, 0]) for a in range(k)]
        for cp in loads:
            cp.start()
        for a, (_, transposed) in enumerate(later):
            loads[a].wait()
            val = stage_in[a][...]
            stage_out[a][...] = (val.T if transposed else val).astype(dtype)
            for j, dst in enumerate((slot(zones[a], x, y, c), to_send[a])):
                cp = pltpu.make_async_copy(stage_out[a], dst, late_sems.at[a, 1 + j])
                cp.start()
                pending.append(cp)
        for a in range(n):
            for j, chip in enumerate(chips):
                copy(a, 1 + j, (*chip, c), (x, y, c)).wait_recv()
                fwd = copy(a, 4 + j, (*chip, c), sibling)
                fwd.start()
                sends.append(fwd)
        for a in range(n):
            copy(a, 0, sibling, (x, y, c)).wait_recv()
            for j, chip in enumerate(chips):
                copy(a, 4 + j, (*chip, 1 - c), (x, y, c)).wait_recv()
        for cp in sends:
            cp.wait_send()
        for cp in pending:
            cp.wait()

    shapes = [s_.shape[::-1] if transposed else s_.shape for s_, transposed in later]
    out = pl.pallas_call(
        body, name="all_gather_weights", in_specs=[ANY_SPEC] * (n + k), out_specs=[ANY_SPEC] * (n + 2 * k),
        out_shape=[_sds((N_DEV,) + b.shape, b.dtype) for b in blocks] + [_sds((N_DEV,) + sh, dtype) for sh in shapes]
        + [_sds(sh, dtype) for sh in shapes],
        scratch_shapes=[pltpu.VMEM(s_.shape, s_.dtype) for s_, _ in later] + [pltpu.VMEM(sh, dtype) for sh in shapes]
        + [pltpu.SemaphoreType.DMA((n, 7)), pltpu.SemaphoreType.DMA((n, 7)), pltpu.SemaphoreType.DMA((n,)),
           pltpu.SemaphoreType.DMA((k, 3))],
        compiler_params=pltpu.CompilerParams(vmem_limit_bytes=VMEM_LIMIT, has_side_effects=True),
    )(*blocks, *[s_ for s_, _ in later])
    return out[:n], out[n:n + k], out[n + k:]


def _adamw(w, g, m, v):
    m = ADAM_B1 * m + (1.0 - ADAM_B1) * g
    v = ADAM_B2 * v + (1.0 - ADAM_B2) * (g * g)
    m_hat = m / (1.0 - ADAM_B1 ** ADAM_STEP)
    v_hat = v / (1.0 - ADAM_B2 ** ADAM_STEP)
    return -ADAM_LR * (m_hat / (jnp.sqrt(v_hat) + ADAM_EPS) + ADAM_WD * w), m, v


def _pair_reduce(g, name):
    _, r, c_ = g.shape
    n = len(CHIP_FLIPS)

    def body(g_ref, out_ref, sib_buf, send_sems, recv_sems):
        x, y, c = _place()
        chips = [(x ^ fx, y ^ fy) for fx, fy in CHIP_FLIPS]
        piece = lambda chip, core: g_ref.at[4 * chip[0] + 2 * chip[1] + core]
        copies = [pltpu.make_async_remote_copy(
            src_ref=piece(chip, 1 - c), dst_ref=sib_buf.at[j], send_sem=send_sems.at[j], recv_sem=recv_sems.at[j],
            device_id=(x, y, 1 - c), device_id_type=MESH_IDS) for j, chip in enumerate(chips)]
        for cp in copies:
            cp.start()
        for j, chip in enumerate(chips):
            copies[j].wait_recv()
            out_ref[j] = (piece(chip, c)[...].astype(F32) + sib_buf[j].astype(F32)).astype(BF)
        for cp in copies:
            cp.wait_send()

    return pl.pallas_call(
        body, name=name, in_specs=[VMEM_SPEC], out_specs=VMEM_SPEC, out_shape=_sds((n, r, c_), BF),
        scratch_shapes=[pltpu.VMEM((n, r, c_), BF), pltpu.SemaphoreType.DMA((n,)), pltpu.SemaphoreType.DMA((n,))],
        compiler_params=pltpu.CompilerParams(vmem_limit_bytes=VMEM_LIMIT, has_side_effects=True),
    )(g)


HBM_SPEC = pl.BlockSpec(memory_space=pltpu.HBM)
SEM_SPEC = pl.BlockSpec(memory_space=pltpu.SEMAPHORE)
DATAFLOW = pltpu.SideEffectType.DATAFLOW_SIDE_EFFECTING


def _peers():
    x, y, c = _place()
    return 4 * x + 2 * y + c, [(x ^ (k >> 2), y ^ ((k >> 1) & 1), c ^ (k & 1)) for k in range(1, N_DEV)]


def _peer_index(peer):
    return 4 * peer[0] + 2 * peer[1] + peer[2]


def _exchange_start(srcs, zones, pieces, name, chips=False):
    n = len(srcs)
    fresh = zones is None
    if fresh:
        slots = len(CHIP_FLIPS) if chips else N_DEV
        zones = [_sds((slots,) + (v.shape[1:] if pieces else v.shape), v.dtype) for v in srcs]
    n_in = n if fresh else 2 * n
    among_chips = list(chips) if isinstance(chips, (list, tuple)) else [chips] * n

    def body(*refs):
        ins, sems, token = refs[:n], refs[n_in:n_in + 2 * n], refs[-1]
        zs = refs[n_in + 3 * n:n_in + 4 * n] if fresh else refs[n:2 * n]
        me, peers = _peers()
        x, y, c = _place()
        for a in range(n):
            if among_chips[a] and pieces:
                routes = [((x ^ fx, y ^ fy, c), j, j) for j, (fx, fy) in enumerate(CHIP_FLIPS) if j]
            elif among_chips[a]:
                routes = [((x ^ fx, y ^ fy, c), None, me) for fx, fy in CHIP_FLIPS[1:]]
            else:
                routes = [(peer, _peer_index(peer) if pieces else None, me) for peer in peers]
            for peer, src_slot, dst_slot in routes:
                pltpu.make_async_remote_copy(
                    src_ref=ins[a] if src_slot is None else ins[a].at[src_slot], dst_ref=zs[a].at[dst_slot],
                    send_sem=sems[2 * a], recv_sem=sems[2 * a + 1], device_id=peer, device_id_type=MESH_IDS).start()
        token[...] = jnp.zeros_like(token)

    hbm = lambda v: pltpu.with_memory_space_constraint(v, pltpu.HBM)
    out = pl.pallas_call(
        body, name=name,
        out_shape=tuple([pltpu.SemaphoreType.DMA(())] * (2 * n) + [pltpu.HBM(v.shape, v.dtype) for v in srcs]
                        + [pltpu.HBM(z.shape, z.dtype) for z in zones] + [_sds((8, LANES))]),
        in_specs=[HBM_SPEC] * n_in, out_specs=tuple([SEM_SPEC] * (2 * n) + [HBM_SPEC] * (2 * n) + [VMEM_SPEC]),
        input_output_aliases={i: 2 * n + i for i in range(n_in)},
        compiler_params=pltpu.CompilerParams(has_side_effects=DATAFLOW),
    )(*[hbm(v) for v in srcs], *([] if fresh else [hbm(z) for z in zones]))
    return out[:2 * n], out[2 * n:3 * n], out[3 * n:4 * n], out[-1]


def _relay_start(zones, name):
    n = len(zones)

    def body(*refs):
        zs, sems, token = refs[:n], refs[n:3 * n], refs[-1]
        x, y, c = _place()
        for fx, fy in CHIP_FLIPS:
            slot = 4 * (x ^ fx) + 2 * (y ^ fy) + c
            for a in range(n):
                pltpu.make_async_remote_copy(
                    src_ref=zs[a].at[slot], dst_ref=zs[a].at[slot], send_sem=sems[2 * a], recv_sem=sems[2 * a + 1],
                    device_id=(x, y, 1 - c), device_id_type=MESH_IDS).start()
        token[...] = jnp.zeros_like(token)

    out = pl.pallas_call(
        body, name=name,
        out_shape=tuple([pltpu.SemaphoreType.DMA(())] * (2 * n) + [pltpu.HBM(z.shape, z.dtype) for z in zones]
                        + [_sds((8, LANES))]),
        in_specs=[HBM_SPEC] * n, out_specs=tuple([SEM_SPEC] * (2 * n) + [HBM_SPEC] * n + [VMEM_SPEC]),
        input_output_aliases={i: 2 * n + i for i in range(n)},
        compiler_params=pltpu.CompilerParams(has_side_effects=DATAFLOW),
    )(*[pltpu.with_memory_space_constraint(z, pltpu.HBM) for z in zones])
    return out[:2 * n], [], out[2 * n:3 * n], out[-1]


def _exchange_wait(sems, srcs, zones, after, name, chips=False, n_copies=None):
    n, n_src = len(zones), len(srcs)
    after = list(after) if isinstance(after, (list, tuple)) else [after]
    n_copies = n_copies or (len(CHIP_FLIPS) - 1 if chips else N_DEV - 1)

    def body(*refs):
        zs, sm = refs[n_src:n_src + n], refs[n_src + n:n_src + 3 * n]
        me, peers = _peers()
        for a in range(n):
            seven = zs[a].at[pl.ds(0, n_copies)]
            cp = pltpu.make_async_remote_copy(src_ref=seven, dst_ref=seven, send_sem=sm[2 * a], recv_sem=sm[2 * a + 1],
                                              device_id=peers[0], device_id_type=MESH_IDS)
            cp.wait_send()
            cp.wait_recv()

    out = pl.pallas_call(
        body, name=name, out_shape=tuple([pltpu.HBM(v.shape, v.dtype) for v in srcs] + [pltpu.HBM(z.shape, z.dtype) for z in zones]),
        in_specs=[HBM_SPEC] * (n_src + n) + [SEM_SPEC] * (2 * n) + [ANY_SPEC] * len(after),
        out_specs=tuple([HBM_SPEC] * (n_src + n)), input_output_aliases={i: i for i in range(n_src + n)},
        compiler_params=pltpu.CompilerParams(has_side_effects=DATAFLOW),
    )(*srcs, *zones, *sems, *after)
    return out[:n_src], out[n_src:]


def _sum_adamw(zone, own, w, m, v, name, chips=False):
    n_slots, r, c_ = zone.shape
    rb = next((b for b in (256, 128) if r % b == 0), r)

    def body(me_ref, z_ref, own_ref, w_ref, m_ref, v_ref, grad_ref, delta_ref, nm_ref, nv_ref):
        total = None
        for d in range(n_slots):
            part = jnp.where(me_ref[0] == d, own_ref[0], z_ref[d]).astype(F32)
            total = part if total is None else total + part
        grad_ref[...] = total
        delta_ref[...], nm_ref[...], nv_ref[...] = _adamw(w_ref[...], total, m_ref[...], v_ref[...])

    x, y, c = _place()
    mine = 0 * x if chips else 4 * x + 2 * y + c
    blk = pl.BlockSpec((rb, c_), lambda i, me_ref: (i, 0))
    return pl.pallas_call(
        body, name=name,
        grid_spec=pltpu.PrefetchScalarGridSpec(
            num_scalar_prefetch=1, grid=(r // rb,),
            in_specs=[pl.BlockSpec((n_slots, rb, c_), lambda i, me_ref: (0, i, 0)),
                      pl.BlockSpec((1, rb, c_), lambda i, me_ref: (me_ref[0], i, 0)), blk, blk, blk],
            out_specs=[blk] * 4),
        out_shape=[_sds((r, c_))] * 4, compiler_params=_params("parallel"),
    )(mine.astype(jnp.int32).reshape(1), zone, own, w, m, v)


SMALL_NORMS = ("pre_mix_norm", "post_mix_norm", "pre_mlp_norm", "post_mlp_norm")
SMALL_ORDER = SMALL_NORMS + ("fox_out_norm", "gdn_out_norm", "fox_f_bias", "gdn_a_log", "gdn_dt_bias", "gdn_conv_w")
CONV_SLAB_ROWS, CONV_SLAB_LANES = 8, 256


def _small_pack(small):
    def body(n0, n1, n2, n3, fnw_ref, gnw_ref, loss_ref, vec_ref, out_ref):
        out_ref[...] = jnp.zeros_like(out_ref)
        for i, ref in enumerate((n0, n1, n2, n3)):
            out_ref[i:i + 1, :] = ref[...]
        out_ref[4:5, 0:LANES] = fnw_ref[...]
        out_ref[4:5, LANES:2 * LANES] = gnw_ref[...]
        out_ref[4:5, 2 * LANES:3 * LANES] = loss_ref[...]
        out_ref[5:8, 0:LANES] = vec_ref[0:3, :]

    return pl.pallas_call(body, name="small_pack", in_specs=[VMEM_SPEC] * 8, out_specs=VMEM_SPEC,
                          out_shape=_sds((8, D_MODEL)))(*small["norms"], small["fox_out_norm"], small["gdn_out_norm"],
                                                        small["loss"], small["vectors"])


def _conv_slabs(dconv):
    blocks = dconv.reshape(CONV_K, N_DEV, -1).transpose(1, 0, 2)
    blocks = jnp.pad(blocks, ((0, 0), (0, CONV_SLAB_ROWS - CONV_K), (0, CONV_SLAB_LANES - blocks.shape[2])))
    return blocks.reshape(N_DEV * CONV_SLAB_ROWS, CONV_SLAB_LANES)


def _small_update(zone, conv_zone, own, own_conv, w, m, v):
    n = len(SMALL_ORDER)
    n_conv = w["gdn_conv_w"].shape[1]

    def body(me_ref, z_ref, zc_ref, own_ref, ownc_ref, *refs):
        params, loss_ref, outs, (tot, totc) = refs[:3 * n], refs[3 * n], refs[3 * n + 1:7 * n + 1], refs[-2:]
        total, total_c = None, None
        for d in range(N_DEV):
            part = jnp.where(me_ref[0] == d, own_ref[...], z_ref[d])
            part_c = jnp.where(me_ref[0] == d, ownc_ref[...], zc_ref[d])
            total, total_c = (part, part_c) if d == 0 else (total + part, total_c + part_c)
        tot[...] = total
        totc[...] = total_c
        loss_ref[...] = tot[4, 2 * LANES:2 * LANES + 1]
        mine = totc[pl.ds(pl.multiple_of(me_ref[0] * CONV_SLAB_ROWS, CONV_SLAB_ROWS), CONV_SLAB_ROWS), :]
        g = dict(zip(SMALL_NORMS, (tot[0], tot[1], tot[2], tot[3])))
        g.update(fox_out_norm=tot[4, 0:FOX_HEAD_DIM], gdn_out_norm=tot[4, LANES:LANES + GDN_HEAD_DIM],
                 fox_f_bias=tot[5, SM_FF:SM_FF + N_FOX_HEADS], gdn_a_log=tot[6, SM_GA:SM_GA + N_GDN_HEADS],
                 gdn_dt_bias=tot[7, SM_GA:SM_GA + N_GDN_HEADS], gdn_conv_w=mine[0:CONV_K, 0:n_conv])
        for i, name in enumerate(SMALL_ORDER):
            w_ref, m_ref, v_ref = params[3 * i:3 * i + 3]
            outs[4 * i][...] = g[name]
            outs[4 * i + 1][...], outs[4 * i + 2][...], outs[4 * i + 3][...] = _adamw(w_ref[...], g[name], m_ref[...],
                                                                                     v_ref[...])

    x, y, c = _place()
    operands = [a[name] for name in SMALL_ORDER for a in (w, m, v)]
    out = pl.pallas_call(
        body, name="small_update",
        in_specs=[pl.BlockSpec(memory_space=pltpu.SMEM)] + [VMEM_SPEC] * (4 + 3 * n), out_specs=[VMEM_SPEC] * (1 + 4 * n),
        out_shape=[_sds((1,))] + [_sds(w[name].shape) for name in SMALL_ORDER for _ in range(4)],
        scratch_shapes=[pltpu.VMEM(zone.shape[1:], F32), pltpu.VMEM(conv_zone.shape[1:], F32)],
    )((4 * x + 2 * y + c).astype(jnp.int32).reshape(1), zone, conv_zone, own, own_conv, *operands)
    return out[0][0], {name: out[1 + 4 * i:5 + 4 * i] for i, name in enumerate(SMALL_ORDER)}


def _native_rows():
    groups = []
    for first, n_groups in ((0, N_FOX_HEADS // 2), (D_FOX * 3 + N_FOX_HEADS, N_GDN_HEADS)):
        for g in range(n_groups):
            groups += [(first + part * n_groups * LANES + g * LANES, first + part * n_groups * LANES + (g + 1) * LANES)
                       for part in range(3)]
    return tuple(groups) + ((3088, 3600), (1536, 1544), (3080, 3088))


NATIVE_ROWS = _native_rows()


W_IN_PIECE = D_PROJ // N_DEV
WGRAD_IN_ROWS = 512
SHUFFLE_LANES = 256


def _to_aligned_moves():
    moves, o = [], 0
    for lo, hi in NATIVE_ROWS:
        r = lo
        while r < hi:
            d = r // W_IN_PIECE
            k = min(hi, (d + 1) * W_IN_PIECE) - r
            moves.append((0, d, r - d * W_IN_PIECE, 0, o, k))
            r, o = r + k, o + k
    return moves


def _from_aligned_moves():
    moves = []
    for _, d, a, _, o, k in _to_aligned_moves():
        while k:
            n = min(k, WGRAD_IN_ROWS - o % WGRAD_IN_ROWS) if o < COL_SMALL else k
            moves.append((0, o // WGRAD_IN_ROWS, o % WGRAD_IN_ROWS, d, a, n) if o < COL_SMALL else
                         (1, 0, o - COL_SMALL, d, a, n))
            o, a, k = o + n, a + n, k - n
    return moves


def _shuffle_rows(srcs, moves, out_shape, name):
    c = srcs[0].shape[-1]

    def body(*refs):
        s_refs, o_ref, s_f, o_f = refs[:len(srcs)], refs[len(srcs)], refs[len(srcs) + 1:-1], refs[-1]
        for s_ref, f in zip(s_refs, s_f):
            f[...] = s_ref[...].astype(F32)
        o_f[...] = jnp.zeros_like(o_f)
        for i, ss, so, ds, do, k in moves:
            o_f[ds, pl.ds(do, k), :] = s_f[i][ss, pl.ds(so, k), :]
        o_ref[...] = o_f[...].astype(BF)

    blk = lambda shape: pl.BlockSpec(tuple(shape[:-1]) + (SHUFFLE_LANES,), lambda j: (0, 0, j))
    scratch = lambda shape: pltpu.VMEM(tuple(shape[:-1]) + (SHUFFLE_LANES,), F32)
    return pl.pallas_call(
        body, name=name, grid=(c // SHUFFLE_LANES,), in_specs=[blk(s.shape) for s in srcs], out_specs=blk(out_shape),
        out_shape=_sds(out_shape, BF), scratch_shapes=[scratch(s.shape) for s in srcs] + [scratch(out_shape)],
        compiler_params=_params("parallel"),
    )(*srcs)


def _cols_from_pieces(p):
    return p.transpose(1, 0, 2).reshape(p.shape[1], -1)


WEIGHT_ORDER = ("pre_mix_norm", "w_in", "fox_f_bias", "fox_out_norm", "gdn_conv_w", "gdn_a_log", "gdn_dt_bias",
                "gdn_out_norm", "w_out", "post_mix_norm", "pre_mlp_norm", "w_up", "w_down", "post_mlp_norm")


def kernel(x, pre_mix_norm, w_in, fox_f_bias, fox_out_norm, gdn_conv_w, gdn_a_log, gdn_dt_bias, gdn_out_norm, w_out, post_mix_norm, pre_mlp_norm, w_up, w_down, post_mlp_norm, loss_target, m_pre_mix_norm, m_w_in, m_fox_f_bias, m_fox_out_norm, m_gdn_conv_w, m_gdn_a_log, m_gdn_dt_bias, m_gdn_out_norm, m_w_out, m_post_mix_norm, m_pre_mlp_norm, m_w_up, m_w_down, m_post_mlp_norm, v_pre_mix_norm, v_w_in, v_fox_f_bias, v_fox_out_norm, v_gdn_conv_w, v_gdn_a_log, v_gdn_dt_bias, v_gdn_out_norm, v_w_out, v_post_mix_norm, v_pre_mlp_norm, v_w_up, v_w_down, v_post_mlp_norm):
    w = dict(pre_mix_norm=pre_mix_norm, w_in=w_in, fox_f_bias=fox_f_bias, fox_out_norm=fox_out_norm,
             gdn_conv_w=gdn_conv_w, gdn_a_log=gdn_a_log, gdn_dt_bias=gdn_dt_bias, gdn_out_norm=gdn_out_norm, w_out=w_out,
             post_mix_norm=post_mix_norm, pre_mlp_norm=pre_mlp_norm, w_up=w_up, w_down=w_down, post_mlp_norm=post_mlp_norm)
    mom = dict(pre_mix_norm=m_pre_mix_norm, w_in=m_w_in, fox_f_bias=m_fox_f_bias, fox_out_norm=m_fox_out_norm,
               gdn_conv_w=m_gdn_conv_w, gdn_a_log=m_gdn_a_log, gdn_dt_bias=m_gdn_dt_bias, gdn_out_norm=m_gdn_out_norm,
               w_out=m_w_out, post_mix_norm=m_post_mix_norm, pre_mlp_norm=m_pre_mlp_norm, w_up=m_w_up, w_down=m_w_down,
               post_mlp_norm=m_post_mlp_norm)
    var = dict(pre_mix_norm=v_pre_mix_norm, w_in=v_w_in, fox_f_bias=v_fox_f_bias, fox_out_norm=v_fox_out_norm,
               gdn_conv_w=v_gdn_conv_w, gdn_a_log=v_gdn_a_log, gdn_dt_bias=v_gdn_dt_bias, gdn_out_norm=v_gdn_out_norm,
               w_out=v_w_out, post_mix_norm=v_post_mix_norm, pre_mlp_norm=v_pre_mlp_norm, w_up=v_w_up, w_down=v_w_down,
               post_mlp_norm=v_post_mlp_norm)

    (win_g, conv_g), zones, shards = _all_gather([w_in.T.astype(BF), gdn_conv_w],
                                                 [(w_out, False), (w_up, True), (w_down, False)], BF)
    wt_al = _shuffle_rows([win_g], _to_aligned_moves(), (1, PROJ_W, D_MODEL), "w_in_to_aligned")[0]
    convw = _cols_from_pieces(conv_g)
    sems, shards, zones, after = _exchange_start(shards, zones, False, "gather_start", chips=True)
    gathers = dict(hop=(sems, shards, zones))

    def late_weights(name, after):
        if name == "mlp_relay":
            sems, shards, zones = gathers.pop("hop")
            _, zones = _exchange_wait(sems, shards, zones, after, "gather_wait", chips=True)
            sems, _, zones, token = _relay_start(zones, "gather_relay")
            gathers.update(w_out=(sems[:2], zones[:1]), mlp=(sems[2:], zones[1:]))
            return token
        sems, zones = gathers[name]
        _, got = _exchange_wait(sems, [], zones, after, "gather_" + name + "_done", n_copies=len(CHIP_FLIPS))
        if name == "w_out":
            return got[0].reshape(D_MODEL, D_MODEL)
        return got[0].reshape(D_FF, D_MODEL), got[1].reshape(D_FF, D_MODEL)

    scatters = {}

    def on_grads(name, g):
        if name == "mlp":
            scatters["mlp"] = list(g)
            return g[0]
        if name == "w_out":
            sems, srcs, zones, token = _exchange_start(scatters["mlp"] + [g], None, True, "scatter_mlp_w_out_start")
            scatters["mlp"] = (sems[:4], srcs[:2], zones[:2], token)
            scatters["w_out"] = (sems[4:], srcs[2:], zones[2:], token)
            return token
        g = _shuffle_rows(list(g), _from_aligned_moves(), (N_DEV, W_IN_PIECE, D_MODEL), "w_in_grad_from_aligned")
        scatters[name] = _exchange_start([_pair_reduce(g, "pair_reduce_w_in")], None, True, "scatter_w_in_start", chips=True)
        return scatters[name][3]

    grad_x, small = _local_step(
        x[0], loss_target[0], wt_al, after, late_weights, on_grads, convw, pre_mix_norm,
        fox_f_bias, fox_out_norm, gdn_a_log, gdn_dt_bias, gdn_out_norm, post_mix_norm, pre_mlp_norm, post_mlp_norm)
    slabs = [_small_pack(small), _conv_slabs(jnp.concatenate(small["conv"], axis=1))]
    scatters["small"] = _exchange_start(slabs, None, False, "small_start")

    grads, delta, new_m, new_v = {}, {}, {}, {}
    after = scatters["small"][3]
    for name, members in (("mlp", ("w_up", "w_down")), ("w_out", ("w_out",)), ("small", ()), ("w_in", ("w_in",))):
        sems, srcs, zones, _ = scatters[name]
        srcs, zones = _exchange_wait(sems, srcs, zones, after, "scatter_" + name + "_wait", chips=name == "w_in")
        if name == "small":
            loss, updated = _small_update(*zones, *srcs, w, mom, var)
            for n, res in updated.items():
                grads[n], delta[n], new_m[n], new_v[n] = res
            after = grads["pre_mix_norm"]
        for n, zone, own in zip(members, zones, srcs):
            if n == "w_in":
                res = _sum_adamw(zone, own, w[n].T, mom[n].T, var[n].T, "adamw_" + n, chips=True)
                grads[n], delta[n], new_m[n], new_v[n] = [r.T for r in res]
            else:
                grads[n], delta[n], new_m[n], new_v[n] = _sum_adamw(zone, own, w[n], mom[n], var[n], "adamw_" + n)
        if members:
            after = [grads[n] for n in members]

    return (loss, grad_x[None], *[grads[n] for n in WEIGHT_ORDER], *[delta[n] for n in WEIGHT_ORDER],
            *[new_m[n] for n in WEIGHT_ORDER], *[new_v[n] for n in WEIGHT_ORDER])
```

```python
import jax
import jax.numpy as jnp
from jax import lax
from jax.experimental import pallas as pl
from jax.experimental.pallas import tpu as pltpu

F32 = jnp.float32
BF = jnp.bfloat16

D_MODEL = 1024
N_FOX_HEADS, FOX_HEAD_DIM = 8, 64
N_GDN_HEADS, GDN_HEAD_DIM = 4, 128
D_FOX = N_FOX_HEADS * FOX_HEAD_DIM
D_GDN = N_GDN_HEADS * GDN_HEAD_DIM
CHUNK = 64
CONV_K = 4
D_FF = 4 * D_MODEL
EPS = 1e-6
D_PROJ = 3600
N_DEV = 8

PROJ_W = 3712
COL_FOX, COL_GDN, COL_GZ, COL_SMALL = 0, 1536, 3072, 3584
LANES = 128
QKV = 3 * LANES
SM_FF, SM_GB, SM_GA = 0, 8, 12

ADAM_LR, ADAM_B1, ADAM_B2, ADAM_EPS, ADAM_WD, ADAM_STEP = 0.001, 0.9, 0.999, 1e-08, 0.01, 10

TOKEN_BLOCK = 256
MATMUL_BLOCK = 512
TRI_ROWS = 4
FOX_SCALE = FOX_HEAD_DIM ** -0.5
GDN_QSCALE = GDN_HEAD_DIM ** -0.5
NEG_BIG = -1e30
VMEM_LIMIT = 56 * 1024 * 1024

VMEM_SPEC = pl.BlockSpec(memory_space=pltpu.VMEM)
ANY_SPEC = pl.BlockSpec(memory_space=pl.ANY)


def _sds(shape, dtype=F32):
    return jax.ShapeDtypeStruct(shape, dtype)


def _params(*sem):
    return pltpu.CompilerParams(dimension_semantics=sem if sem else None, vmem_limit_bytes=VMEM_LIMIT)


def _ordered(body):
    def ordered(_, *refs):
        body(*refs)

    return ordered


def _mm(a, b):
    return jnp.dot(a.astype(BF), b.astype(BF), preferred_element_type=F32)


def _mm_nt(a, b):
    return lax.dot_general(a.astype(BF), b.astype(BF), (((1,), (1,)), ((), ())), preferred_element_type=F32)


def _mm_tn(a, b):
    return lax.dot_general(a.astype(BF), b.astype(BF), (((0,), (0,)), ((), ())), preferred_element_type=F32)


def _sigmoid(x):
    return 1.0 / (1.0 + jnp.exp(-x))


def _softplus(x):
    return jnp.maximum(x, 0.0) + jnp.log1p(jnp.exp(-jnp.abs(x)))


def _iota(shape, dim):
    return lax.broadcasted_iota(jnp.int32, shape, dim)


def _shift_down(x, s, row):
    return jnp.where(row >= s, pltpu.roll(x, s, 0), 0.0)


def _shift_up(x, s, row):
    n = x.shape[0]
    return jnp.where(row < n - s, pltpu.roll(x, n - s, 0), 0.0)


def _norm_proj(x, nw, wt_al, after):
    t = x.shape[0]

    def body(x_ref, nw_ref, w_ref, proj_ref, h_ref):
        xv = x_ref[...]
        r = lax.rsqrt(jnp.mean(xv * xv, axis=-1, keepdims=True) + EPS)
        h = (xv * r * nw_ref[...]).astype(BF)
        h_ref[...] = h
        proj_ref[...] = lax.dot_general(h, w_ref[...], (((1,), (1,)), ((), ())), preferred_element_type=F32)

    tm = min(MATMUL_BLOCK, t)
    return pl.pallas_call(
        _ordered(body), name="norm_proj", grid=(t // tm,),
        in_specs=[ANY_SPEC, pl.BlockSpec((tm, D_MODEL), lambda i: (i, 0)), pl.BlockSpec((1, D_MODEL), lambda i: (0, 0)),
                  pl.BlockSpec((PROJ_W, D_MODEL), lambda i: (0, 0))],
        out_specs=[pl.BlockSpec((tm, PROJ_W), lambda i: (i, 0)), pl.BlockSpec((tm, D_MODEL), lambda i: (i, 0))],
        out_shape=[_sds((t, PROJ_W)), _sds((t, D_MODEL), BF)],
        compiler_params=_params("parallel"),
    )(after, x, nw, wt_al)


def _lane_column(x, lane):
    return jnp.sum(jnp.where(_iota((1, LANES), 1) == lane, x, 0.0), axis=-1, keepdims=True)


def _small_prep(proj, fb, al, dtb):
    t = proj.shape[0]

    def body(sm_ref, fb_ref, al_ref, dtb_ref, cumt_ref, beta_ref, g_ref):
        s = sm_ref[...]
        z = s + fb_ref[...]
        cum = jnp.minimum(z, 0.0) - jnp.log1p(jnp.exp(-jnp.abs(z)))
        row = _iota((t, LANES), 0)
        step = 1
        while step < t:
            cum = cum + _shift_down(cum, step, row)
            step *= 2
        cumt_ref[...] = cum.T
        beta_ref[...] = _sigmoid(s)
        g_ref[...] = -jnp.exp(al_ref[...]) * _softplus(s + dtb_ref[...])

    vec = pl.BlockSpec((1, LANES), lambda i: (0, 0))
    tok = pl.BlockSpec((t, LANES), lambda i: (0, 0))
    return pl.pallas_call(
        body, name="small_prep", grid=(1,),
        in_specs=[pl.BlockSpec((t, LANES), lambda i: (0, COL_SMALL // LANES)), vec, vec, vec],
        out_specs=[pl.BlockSpec((LANES, t), lambda i: (0, 0)), tok, tok],
        out_shape=[_sds((LANES, t)), _sds((t, LANES)), _sds((t, LANES))],
        compiler_params=_params("arbitrary"),
    )(proj, fb, al, dtb)


def _fox_stack(x, first):
    return jnp.concatenate([jnp.where(first, x, 0.0), jnp.where(first, 0.0, x)], axis=0).astype(BF)


def _fox_unstack(y, first):
    n = y.shape[0] // 2
    return jnp.where(first, y[:n], y[n:])


def _fox_logits(q2_i, kb, cumt_ref, pair, i, tq):
    klen = (i + 1) * tq
    s = lax.dot_general(q2_i, kb[:klen], (((1,), (1,)), ((), ())), preferred_element_type=F32)
    upper = _iota((2 * tq, 1), 0) < tq
    s = s - jnp.where(upper, cumt_ref[pl.ds(2 * pair, 1), 0:klen], cumt_ref[pl.ds(2 * pair + 1, 1), 0:klen])
    causal = _iota((2 * tq, tq), 1) <= _iota((2 * tq, tq), 0) % tq
    parts = [(s[:, :klen - tq], 0, klen - tq)] if i else []
    return parts + [(jnp.where(causal, s[:, klen - tq:], NEG_BIG), klen - tq, klen)]


def _fox_fwd(proj, cumt, fnw, after):
    t = proj.shape[0]
    tq = min(TOKEN_BLOCK, t // 2)
    nq = t // tq

    def body(q_ref, k_ref, v_ref, cumt_ref, fnw_ref, o_ref, lse_ref, fn_ref):
        j = pl.program_id(0)
        first = _iota((1, LANES), 1) < FOX_HEAD_DIM
        kb = k_ref[...].astype(BF)
        vb = v_ref[...].astype(BF)
        for i in range(nq):
            rows = slice(i * tq, (i + 1) * tq)
            q2 = _fox_stack(q_ref[rows, :] * FOX_SCALE, first)
            parts = _fox_logits(q2, kb, cumt_ref, j, i, tq)
            m = jnp.max(parts[-1][0], axis=-1, keepdims=True)
            if i:
                m = jnp.maximum(m, jnp.max(parts[0][0], axis=-1, keepdims=True))
            l = jnp.zeros((2 * tq, 1), F32)
            o = jnp.zeros((2 * tq, LANES), F32)
            for s, lo, hi in parts:
                p = jnp.exp(s - m)
                l = l + jnp.sum(p, axis=-1, keepdims=True)
                o = o + jnp.dot(p.astype(BF), vb[lo:hi], preferred_element_type=F32)
            o_acc = _fox_unstack(o / l, first)
            lse_acc = _fox_unstack(jnp.broadcast_to(m + jnp.log(l), (2 * tq, LANES)), first)
            o_ref[rows, :] = o_acc
            lse_ref[rows, :] = lse_acc
            o2 = o_acc * o_acc
            s0 = jnp.sum(jnp.where(first, o2, 0.0), axis=-1, keepdims=True)
            s1 = jnp.sum(jnp.where(first, 0.0, o2), axis=-1, keepdims=True)
            r = lax.rsqrt(jnp.where(first, s0, s1) * (1.0 / FOX_HEAD_DIM) + EPS)
            fn_ref[rows, :] = (o_acc * r * fnw_ref[...]).astype(BF)

    qkv = lambda k: pl.BlockSpec((t, LANES), lambda j: (0, COL_FOX // LANES + 3 * j + k))
    pair = pl.BlockSpec((t, LANES), lambda j: (0, j))
    return pl.pallas_call(
        _ordered(body), name="fox_fwd", grid=(N_FOX_HEADS // 2,),
        in_specs=[ANY_SPEC, qkv(0), qkv(1), qkv(2), pl.BlockSpec((LANES, t), lambda j: (0, 0)),
                  pl.BlockSpec((1, LANES), lambda j: (0, 0))],
        out_specs=[pair, pair, pair],
        out_shape=[_sds((t, D_FOX)), _sds((t, D_FOX)), _sds((t, D_FOX), BF)],
        compiler_params=_params("parallel"),
    )(after, proj, proj, proj, cumt, fnw)


def _fox_bwd(proj, cumt, lse, o, do, dproj):
    t = proj.shape[0]
    tq = min(TOKEN_BLOCK, t // 2)
    nq = t // tq

    def body(q_ref, k_ref, v_ref, cumt_ref, lse_ref, o_ref, do_ref, _, dqkv_ref, dcq_ref, dckt_ref, dk_s, dv_s):
        j = pl.program_id(0)

        @pl.when(j == 0)
        def _():
            dcq_ref[...] = jnp.zeros_like(dcq_ref)
            dckt_ref[...] = jnp.zeros_like(dckt_ref)

        lane = _iota((1, LANES), 1)

        first = _iota((1, LANES), 1) < FOX_HEAD_DIM
        kb = k_ref[...].astype(BF)
        vb = v_ref[...].astype(BF)
        dk_s[...] = jnp.zeros_like(dk_s)
        dv_s[...] = jnp.zeros_like(dv_s)
        for i in range(nq):
            rows = slice(i * tq, (i + 1) * tq)
            do_i = do_ref[rows, :]
            prod = do_i * o_ref[rows, :]
            lse_i = lse_ref[rows, :]
            q2 = _fox_stack(q_ref[rows, :] * FOX_SCALE, first)
            do2 = _fox_stack(do_i, first)
            delta = jnp.concatenate([jnp.sum(jnp.where(first, prod, 0.0), axis=-1, keepdims=True),
                                     jnp.sum(jnp.where(first, 0.0, prod), axis=-1, keepdims=True)], axis=0)
            lse2 = jnp.concatenate([lse_i[:, 0:1], lse_i[:, FOX_HEAD_DIM:FOX_HEAD_DIM + 1]], axis=0)
            dq2 = jnp.zeros((2 * tq, LANES), F32)
            dcq2 = jnp.zeros((2 * tq, 1), F32)
            for s, lo, hi in _fox_logits(q2, kb, cumt_ref, j, i, tq):
                p = jnp.exp(s - lse2)
                ds = p * (_mm_nt(do2, vb[lo:hi]) - delta)
                dsb = ds.astype(BF)
                dq2 = dq2 + jnp.dot(dsb, kb[lo:hi], preferred_element_type=F32)
                dk_s[lo:hi, :] += _mm_tn(dsb, q2)
                dv_s[lo:hi, :] += _mm_tn(p, do2)
                dcq2 = dcq2 + jnp.sum(ds, axis=-1, keepdims=True)
                dckt_ref[pl.ds(2 * j, 1), lo:hi] += jnp.sum(ds[:tq], axis=0, keepdims=True)
                dckt_ref[pl.ds(2 * j + 1, 1), lo:hi] += jnp.sum(ds[tq:], axis=0, keepdims=True)
            dqkv_ref[rows, 0:LANES] = (_fox_unstack(dq2, first) * FOX_SCALE).astype(BF)
            dcq_ref[rows, :] += jnp.where(lane == 2 * j, dcq2[:tq], jnp.where(lane == 2 * j + 1, dcq2[tq:], 0.0))
        dqkv_ref[:, LANES:2 * LANES] = dk_s[...].astype(BF)
        dqkv_ref[:, 2 * LANES:QKV] = dv_s[...].astype(BF)

    qkv = lambda k: pl.BlockSpec((t, LANES), lambda j: (0, COL_FOX // LANES + 3 * j + k))
    pair = pl.BlockSpec((t, LANES), lambda j: (0, j))
    rows128 = pl.BlockSpec((LANES, t), lambda j: (0, 0))
    return pl.pallas_call(
        body, name="fox_bwd", grid=(N_FOX_HEADS // 2,),
        in_specs=[qkv(0), qkv(1), qkv(2), rows128, pair, pair, pair, ANY_SPEC],
        out_specs=[pl.BlockSpec((t, QKV), lambda j: (0, COL_FOX // QKV + j)),
                   pl.BlockSpec((t, LANES), lambda j: (0, 0)), rows128],
        out_shape=[_sds(dproj.shape, BF), _sds((t, LANES)), _sds((LANES, t))],
        scratch_shapes=[pltpu.VMEM((t, LANES), F32), pltpu.VMEM((t, LANES), F32)],
        input_output_aliases={7: 0}, compiler_params=_params("arbitrary"),
    )(proj, proj, proj, cumt, lse, o, do, dproj)


def _conv(x, w, row):
    return (w[3:4, :] * x + w[2:3, :] * _shift_down(x, 1, row) + w[1:2, :] * _shift_down(x, 2, row)
            + w[0:1, :] * _shift_down(x, 3, row))


def _chunk_decay(gc_c):
    gi = gc_c[:, 0:CHUNK]
    gj = gc_c.T[0:CHUNK, :]
    ri = _iota((CHUNK, CHUNK), 0)
    cj = _iota((CHUNK, CHUNK), 1)
    return jnp.where(ri >= cj, jnp.exp(jnp.minimum(gi - gj, 0.0)), 0.0), ri > cj


def _gdn_specs(t):
    col = lambda off: pl.BlockSpec((t, LANES), lambda h: (0, off + h))
    cw = lambda off: pl.BlockSpec((CONV_K, LANES), lambda h: (0, off + h))
    mat = pl.BlockSpec((1, t // CHUNK, CHUNK, CHUNK), lambda h: (h, 0, 0, 0))
    qkv = lambda k: pl.BlockSpec((t, LANES), lambda h: (0, COL_GDN // LANES + 3 * h + k))
    return col, cw, mat, qkv


def _gdn_prep(proj, convw, beta, g):
    t = proj.shape[0]
    nch = t // CHUNK

    def body(xq_ref, xk_ref, xv_ref, wq_ref, wk_ref, wv_ref, beta_ref, g_ref,
             qn_ref, kn_ref, cv_ref, gc_ref, be_ref, m_ref, a_ref):
        row = _iota((t, LANES), 0)
        hd = pl.program_id(0)
        be_ref[...] = jnp.broadcast_to(_lane_column(beta_ref[...], SM_GB + hd), (t, LANES))

        def act(x_ref, w_ref):
            y = _conv(x_ref[...], w_ref[...], row)
            return y * _sigmoid(y)

        cq = act(xq_ref, wq_ref)
        ck = act(xk_ref, wk_ref)
        cv_ref[...] = act(xv_ref, wv_ref)
        qn_ref[...] = cq * lax.rsqrt(jnp.sum(cq * cq, axis=-1, keepdims=True) + EPS) * GDN_QSCALE
        kn_ref[...] = ck * lax.rsqrt(jnp.sum(ck * ck, axis=-1, keepdims=True) + EPS)
        gc = jnp.broadcast_to(_lane_column(g_ref[...], SM_GA + hd), (t, LANES))
        pos = row % CHUNK
        step = 1
        while step < CHUNK:
            gc = gc + jnp.where(pos >= step, pltpu.roll(gc, step, 0), 0.0)
            step *= 2
        gc_ref[...] = gc

        group = 4 if nch % 4 == 0 else 1

        def chunks(gi, carry):
            ns = [gi * group + c for c in range(group)]
            sls = [pl.ds(pl.multiple_of(n * CHUNK, CHUNK), CHUNK) for n in ns]
            ks = [kn_ref[sl, :] for sl in sls]
            kk = [_mm_nt(k_c * be_ref[sl, :], k_c) for k_c, sl in zip(ks, sls)]
            qk = [_mm_nt(qn_ref[sl, :], k_c) for k_c, sl in zip(ks, sls)]
            for c, n in enumerate(ns):
                decay, strict = _chunk_decay(gc_ref[sls[c], :])
                m_ref[0, n] = jnp.where(strict, kk[c] * decay, 0.0)
                a_ref[0, n] = qk[c] * decay
            return carry

        lax.fori_loop(0, nch // group, chunks, 0)

    col, cw, mat, qkv = _gdn_specs(t)
    return pl.pallas_call(
        body, name="gdn_prep", grid=(N_GDN_HEADS,),
        in_specs=[qkv(0), qkv(1), qkv(2), cw(0), cw(4), cw(8)] + [pl.BlockSpec((t, LANES), lambda h: (0, 0))] * 2,
        out_specs=[col(0), col(0), col(0), col(0), col(0), mat, mat],
        out_shape=[_sds((t, D_GDN))] * 5 + [_sds((N_GDN_HEADS, nch, CHUNK, CHUNK))] * 2,
        compiler_params=_params("parallel"),
    )(proj, proj, proj, convw, convw, convw, beta, g)


def _tri_inverse(m3):
    assert m3.shape == (LANES, CHUNK, CHUNK)

    def body(m_ref, t_ref, ms, ts):
        for i in range(CHUNK):
            ms[i * CHUNK:(i + 1) * CHUNK, :] = m_ref[:, i, :].T
        cidx = _iota((CHUNK, LANES), 0)

        def t_row(j):
            return ts[pl.ds(pl.multiple_of(j * CHUNK, CHUNK), CHUNK), :]

        def outer(ib, carry):
            i0 = ib * TRI_ROWS

            def inner(pair, accs):
                for jj in (2 * pair, 2 * pair + 1):
                    earlier = t_row(jj)
                    accs = tuple(acc - ms[pl.ds((i0 + r) * CHUNK + jj, 1), :] * earlier for r, acc in enumerate(accs))
                return accs

            accs = list(lax.fori_loop(
                0, i0 // 2, inner, tuple(jnp.where(cidx == i0 + r, 1.0, 0.0).astype(F32) for r in range(TRI_ROWS))))
            for r in range(TRI_ROWS):
                for q in range(r):
                    accs[r] = accs[r] - ms[pl.ds((i0 + r) * CHUNK + i0 + q, 1), :] * accs[q]
                ts[pl.ds(pl.multiple_of((i0 + r) * CHUNK, CHUNK), CHUNK), :] = accs[r]
            return carry

        lax.fori_loop(0, CHUNK // TRI_ROWS, outer, 0)
        for i in range(CHUNK):
            t_ref[:, i, :] = ts[i * CHUNK:(i + 1) * CHUNK, :].T

    return pl.pallas_call(
        body, name="tri_inverse", in_specs=[VMEM_SPEC], out_specs=VMEM_SPEC,
        out_shape=_sds((LANES, CHUNK, CHUNK)),
        scratch_shapes=[pltpu.VMEM((CHUNK * CHUNK, LANES), F32), pltpu.VMEM((CHUNK * CHUNK, LANES), F32)],
        compiler_params=_params(),
    )(m3)


def _gdn_chunk_terms(q, k, v, b, gcc):
    eg = jnp.exp(gcc)
    last = gcc[CHUNK - 1:CHUNK, :]
    egl = jnp.exp(last - gcc)
    gl = jnp.exp(last)
    kb = k * b
    return eg, egl, gl, kb, v * b, kb * eg, q * eg, k * egl


GDN_BLOCK_CHUNKS = 4


def _gdn_block_specs(t, reverse):
    cb = GDN_BLOCK_CHUNKS
    nb = t // (cb * CHUNK)
    idx = (lambda i: nb - 1 - i) if reverse else (lambda i: i)
    tok = pl.BlockSpec((cb * CHUNK, D_GDN), lambda i: (idx(i), 0))
    mat = pl.BlockSpec((N_GDN_HEADS, cb, CHUNK, CHUNK), lambda i: (0, idx(i), 0, 0))
    state = pl.BlockSpec((N_GDN_HEADS, cb, GDN_HEAD_DIM, GDN_HEAD_DIM), lambda i: (0, idx(i), 0, 0))
    return nb, tok, mat, state


def _gdn_scan(qn, kn, cv, be, gc, tinv, amat):
    t = qn.shape[0]
    nch = t // CHUNK

    def body(q_ref, k_ref, v_ref, b_ref, gc_ref, t_ref, a_ref, o_ref, sall_ref, vn_ref, s_scr):
        @pl.when(pl.program_id(0) == 0)
        def _():
            s_scr[...] = jnp.zeros_like(s_scr)

        heads = range(N_GDN_HEADS)
        cols = [slice(hd * LANES, (hd + 1) * LANES) for hd in heads]
        s = [s_scr[hd] for hd in heads]
        for cc in range(GDN_BLOCK_CHUNKS):
            rs = slice(cc * CHUNK, (cc + 1) * CHUNK)
            terms = [_gdn_chunk_terms(q_ref[rs, cs], k_ref[rs, cs], v_ref[rs, cs], b_ref[rs, cs], gc_ref[rs, cs])
                     for cs in cols]
            for hd in heads:
                sall_ref[hd, cc] = s[hd]
            uw = [_mm(t_ref[hd, cc], jnp.concatenate([terms[hd][4], terms[hd][5]], axis=1)) for hd in heads]
            ws_qs = [_mm(jnp.concatenate([uw[hd][:, LANES:], terms[hd][6]], axis=0), s[hd]) for hd in heads]
            vn = [uw[hd][:, :LANES] - ws_qs[hd][:CHUNK] for hd in heads]
            a_vn = [_mm(a_ref[hd, cc], vn[hd]) for hd in heads]
            kd_vn = [_mm_tn(terms[hd][7], vn[hd]) for hd in heads]
            for hd in heads:
                vn_ref[rs, cols[hd]] = vn[hd]
                o_ref[rs, cols[hd]] = ws_qs[hd][CHUNK:] + a_vn[hd]
                s[hd] = s[hd] * terms[hd][2] + kd_vn[hd]
        for hd in heads:
            s_scr[hd] = s[hd]

    nb, tok, mat, state = _gdn_block_specs(t, False)
    return pl.pallas_call(
        body, name="gdn_scan", grid=(nb,),
        in_specs=[tok] * 5 + [mat, mat], out_specs=[tok, state, tok],
        out_shape=[_sds((t, D_GDN)), _sds((N_GDN_HEADS, nch, GDN_HEAD_DIM, GDN_HEAD_DIM)), _sds((t, D_GDN))],
        scratch_shapes=[pltpu.VMEM((N_GDN_HEADS, GDN_HEAD_DIM, GDN_HEAD_DIM), F32)],
        compiler_params=_params("arbitrary"),
    )(qn, kn, cv, be, gc, tinv, amat)


def _gdn_bwd(qn, kn, cv, be, gc, tinv, amat, s_all, vn_all, do):
    t = qn.shape[0]

    def body(q_ref, k_ref, v_ref, b_ref, gc_ref, t_ref, a_ref, sall_ref, vn_ref, do_ref,
             dq_ref, dk_ref, dv_ref, db_ref, dg_ref, ds_scr):
        @pl.when(pl.program_id(0) == 0)
        def _():
            ds_scr[...] = jnp.zeros_like(ds_scr)

        lastrow = _iota((CHUNK, LANES), 0) == CHUNK - 1
        heads = range(N_GDN_HEADS)
        cols = [slice(hd * LANES, (hd + 1) * LANES) for hd in heads]
        each = lambda fn: [fn(hd) for hd in heads]
        rows_cat = lambda x, y: jnp.concatenate([x, y], axis=0)
        lane_cat = lambda x, y: jnp.concatenate([x, y], axis=1)
        dsp = each(lambda hd: ds_scr[hd])
        for cc in reversed(range(GDN_BLOCK_CHUNKS)):
            rs = slice(cc * CHUNK, (cc + 1) * CHUNK)
            q = each(lambda hd: q_ref[rs, cols[hd]])
            k = each(lambda hd: k_ref[rs, cols[hd]])
            v = each(lambda hd: v_ref[rs, cols[hd]])
            b = each(lambda hd: b_ref[rs, cols[hd]])
            gcc = each(lambda hd: gc_ref[rs, cols[hd]])
            do_c = each(lambda hd: do_ref[rs, cols[hd]])
            vn = each(lambda hd: vn_ref[rs, cols[hd]])
            tn = each(lambda hd: t_ref[hd, cc])
            st = each(lambda hd: sall_ref[hd, cc])
            terms = each(lambda hd: _gdn_chunk_terms(q[hd], k[hd], v[hd], b[hd], gcc[hd]))
            eg, egl, gl, kb, vb, kbg, qd, kd = [[terms[hd][i] for hd in heads] for i in range(8)]
            w = each(lambda hd: _mm(tn[hd], kbg[hd]))
            a_do = each(lambda hd: _mm_tn(a_ref[hd, cc], do_c[hd]))
            kd_ds = each(lambda hd: _mm(kd[hd], dsp[hd]))
            da = each(lambda hd: _mm_nt(do_c[hd], vn[hd]))
            dkd = each(lambda hd: _mm_nt(vn[hd], dsp[hd]))
            by_k = each(lambda hd: _mm_nt(rows_cat(kb[hd], q[hd]), k[hd]))
            dgl = each(lambda hd: jnp.sum(jnp.sum(dsp[hd] * st[hd], axis=-1, keepdims=True), axis=0, keepdims=True))
            dvn = each(lambda hd: a_do[hd] + kd_ds[hd])
            do_dvn = each(lambda hd: rows_cat(do_c[hd], dvn[hd]))
            by_s = each(lambda hd: _mm_nt(do_dvn[hd], st[hd]))
            dqd = each(lambda hd: by_s[hd][:CHUNK])
            dvn_dw = each(lambda hd: lane_cat(dvn[hd], -by_s[hd][CHUNK:]))
            dsp = each(lambda hd: _mm_tn(rows_cat(qd[hd], -w[hd]), do_dvn[hd]) + gl[hd] * dsp[hd])
            dt = each(lambda hd: _mm_nt(dvn_dw[hd], lane_cat(vb[hd], kbg[hd])))
            by_t = each(lambda hd: _mm_tn(tn[hd], dvn_dw[hd]))
            tt_dt = each(lambda hd: _mm_tn(tn[hd], dt[hd]))
            dm_raw = each(lambda hd: _mm_nt(tt_dt[hd], tn[hd]))
            masks = each(lambda hd: _chunk_decay(gcc[hd]))
            dkk = each(lambda hd: jnp.where(masks[hd][1], -dm_raw[hd], 0.0) * masks[hd][0])
            dqk = each(lambda hd: da[hd] * masks[hd][0])
            dqk_dkk = each(lambda hd: rows_cat(dqk[hd], dkk[hd]))
            on_k = each(lambda hd: _mm(dqk_dkk[hd], k[hd]))
            dk_mm = each(lambda hd: _mm_tn(dqk_dkk[hd], rows_cat(q[hd], kb[hd])))
            for hd in heads:
                cs = cols[hd]
                dvb, dkbg = by_t[hd][:, :LANES], by_t[hd][:, LANES:]
                gmat = dkk[hd] * by_k[hd][:CHUNK] + dqk[hd] * by_k[hd][CHUNK:]
                dq_ref[rs, cs] = dqd[hd] * eg[hd] + on_k[hd][:CHUNK]
                dkb = on_k[hd][CHUNK:] + dkbg * eg[hd]
                dk_ref[rs, cs] = dkd[hd] * egl[hd] + dk_mm[hd] + dkb * b[hd]
                db = jnp.sum(dkb * k[hd], axis=-1, keepdims=True) + jnp.sum(dvb * v[hd], axis=-1, keepdims=True)
                db_ref[rs, cs] = jnp.broadcast_to(db, (CHUNK, LANES))
                dv_ref[rs, cs] = dvb * b[hd]
                dkd_kd = jnp.sum(dkd[hd] * kd[hd], axis=-1, keepdims=True)
                col_sums = jnp.sum(lane_cat(gmat, jnp.zeros_like(gmat)).T, axis=-1, keepdims=True)
                dgc = (jnp.sum(gmat, axis=-1, keepdims=True) - col_sums[:CHUNK]
                       + jnp.sum(dqd[hd] * qd[hd], axis=-1, keepdims=True)
                       + jnp.sum(dkbg * kbg[hd], axis=-1, keepdims=True) - dkd_kd)
                extra = jnp.sum(dkd_kd, axis=0, keepdims=True) + dgl[hd] * gl[hd]
                dg_ref[rs, cs] = dgc + jnp.where(lastrow, extra, 0.0)
        for hd in heads:
            ds_scr[hd] = dsp[hd]
        dg = dg_ref[...]
        row = _iota(dg.shape, 0)
        pos = row % CHUNK
        step = 1
        while step < CHUNK:
            dg = dg + jnp.where(pos < CHUNK - step, pltpu.roll(dg, dg.shape[0] - step, 0), 0.0)
            step *= 2
        dg_ref[...] = dg

    nb, tok, mat, state = _gdn_block_specs(t, True)
    return pl.pallas_call(
        body, name="gdn_bwd", grid=(nb,),
        in_specs=[tok] * 5 + [mat, mat, state, tok, tok], out_specs=[tok] * 5, out_shape=[_sds((t, D_GDN))] * 5,
        scratch_shapes=[pltpu.VMEM((N_GDN_HEADS, GDN_HEAD_DIM, GDN_HEAD_DIM), F32)],
        compiler_params=_params("arbitrary"),
    )(qn, kn, cv, be, gc, tinv, amat, s_all, vn_all, do)


def _gdn_bwd_conv(proj, convw, dqn, dkn, dcv, dproj):
    t = proj.shape[0]

    def body(xq_ref, xk_ref, xv_ref, wq_ref, wk_ref, wv_ref, dq_ref, dk_ref, dv_ref, _,
             dqkv_ref, dwq_ref, dwk_ref, dwv_ref):
        row = _iota((t, LANES), 0)

        def one(x_ref, w_ref, d_ref, k, dw_ref, scale):
            x = x_ref[...]
            w = w_ref[...]
            y = _conv(x, w, row)
            sg = _sigmoid(y)
            dc = d_ref[...]
            if scale is not None:
                c = y * sg
                r = lax.rsqrt(jnp.sum(c * c, axis=-1, keepdims=True) + EPS)
                ch = c * r
                dc = scale * r * (dc - ch * jnp.sum(dc * ch, axis=-1, keepdims=True))
            dy = dc * sg * (1.0 + y * (1.0 - sg))
            dqkv_ref[:, k * LANES:(k + 1) * LANES] = (
                w[3:4, :] * dy + w[2:3, :] * _shift_up(dy, 1, row) + w[1:2, :] * _shift_up(dy, 2, row)
                + w[0:1, :] * _shift_up(dy, 3, row)).astype(BF)
            for jj in range(CONV_K):
                xs = x if jj == CONV_K - 1 else _shift_down(x, CONV_K - 1 - jj, row)
                dw_ref[jj:jj + 1, :] = jnp.sum(dy * xs, axis=0, keepdims=True)

        one(xq_ref, wq_ref, dq_ref, 0, dwq_ref, GDN_QSCALE)
        one(xk_ref, wk_ref, dk_ref, 1, dwk_ref, 1.0)
        one(xv_ref, wv_ref, dv_ref, 2, dwv_ref, None)

    col, cw, _, qkv = _gdn_specs(t)
    return pl.pallas_call(
        body, name="gdn_bwd_conv", grid=(N_GDN_HEADS,),
        in_specs=[qkv(0), qkv(1), qkv(2), cw(0), cw(4), cw(8), col(0), col(0), col(0), ANY_SPEC],
        out_specs=[pl.BlockSpec((t, QKV), lambda h: (0, COL_GDN // QKV + h)), cw(0), cw(0), cw(0)],
        out_shape=[_sds(dproj.shape, BF)] + [_sds((CONV_K, D_GDN))] * 3,
        input_output_aliases={9: 0}, compiler_params=_params("parallel"),
    )(proj, proj, proj, convw, convw, convw, dqn, dkn, dcv, dproj)


def _mix_out(fox_n, gdn_o, proj, gnw, w_out, x, pmw, plw, after):
    t = x.shape[0]
    tm = min(MATMUL_BLOCK, t)

    def body(fn_ref, go_ref, gz_ref, gnw_ref, w_ref, x_ref, pmw_ref, plw_ref, x1_ref, h2_ref, mixed_ref, omix_ref,
             h2t_ref):
        omix_ref[:, 0:D_FOX] = fn_ref[...]
        for hd in range(N_GDN_HEADS):
            cs = slice(hd * LANES, (hd + 1) * LANES)
            go = go_ref[:, cs]
            r = lax.rsqrt(jnp.mean(go * go, axis=-1, keepdims=True) + EPS)
            gz = gz_ref[:, cs]
            omix_ref[:, D_FOX + hd * LANES:D_FOX + (hd + 1) * LANES] = (
                go * r * gnw_ref[...] * (gz * _sigmoid(gz))).astype(BF)
        mixed = jnp.dot(omix_ref[...], w_ref[...], preferred_element_type=F32)
        mixed_ref[...] = mixed
        r2 = lax.rsqrt(jnp.mean(mixed * mixed, axis=-1, keepdims=True) + EPS)
        x1 = x_ref[...] + mixed * r2 * pmw_ref[...]
        x1_ref[...] = x1
        r3 = lax.rsqrt(jnp.mean(x1 * x1, axis=-1, keepdims=True) + EPS)
        h2 = x1 * r3 * plw_ref[...]
        h2_ref[...] = h2.astype(BF)
        h2t_ref[...] = h2.T.astype(BF)

    tok = lambda w: pl.BlockSpec((tm, w), lambda i: (i, 0))
    vec = lambda w: pl.BlockSpec((1, w), lambda i: (0, 0))
    return pl.pallas_call(
        _ordered(body), name="mix_out", grid=(t // tm,),
        in_specs=[ANY_SPEC, tok(D_FOX), tok(D_GDN), pl.BlockSpec((tm, D_GDN), lambda i: (i, COL_GZ // D_GDN)), vec(LANES),
                  pl.BlockSpec((D_MODEL, D_MODEL), lambda i: (0, 0)), tok(D_MODEL), vec(D_MODEL), vec(D_MODEL)],
        out_specs=[tok(D_MODEL)] * 4 + [pl.BlockSpec((D_MODEL, tm), lambda i: (0, i))],
        out_shape=[_sds((t, D_MODEL)), _sds((t, D_MODEL), BF), _sds((t, D_MODEL)), _sds((t, D_MODEL), BF),
                   _sds((D_MODEL, t), BF)],
        compiler_params=_params("parallel"),
    )(after, fox_n, gdn_o, proj, gnw, w_out, x, pmw, plw)


def _out_bwd(dmixed, w_out, o_fox, gdn_o, proj, fnw, gnw, after):
    t = dmixed.shape[0]
    tm = min(MATMUL_BLOCK, t)

    def body(dm_ref, w_ref, of_ref, go_ref, gz_ref, fnw_ref, gnw_ref, dof_ref, dgo_ref, dgz_ref, dfw_ref, dgw_ref):
        i = pl.program_id(0)

        @pl.when(i == 0)
        def _():
            dfw_ref[...] = jnp.zeros_like(dfw_ref)
            dgw_ref[...] = jnp.zeros_like(dgw_ref)

        domix = _mm_nt(dm_ref[...], w_ref[...])
        first = _iota((1, LANES), 1) < FOX_HEAD_DIM
        dfw = jnp.zeros((1, LANES), F32)
        dgw = jnp.zeros((1, LANES), F32)
        for pr in range(N_FOX_HEADS // 2):
            cs = slice(pr * LANES, (pr + 1) * LANES)
            o = of_ref[:, cs]
            dfn = domix[:, cs]
            o2 = o * o
            s0 = jnp.sum(jnp.where(first, o2, 0.0), axis=-1, keepdims=True)
            s1 = jnp.sum(jnp.where(first, 0.0, o2), axis=-1, keepdims=True)
            r = lax.rsqrt(jnp.where(first, s0, s1) * (1.0 / FOX_HEAD_DIM) + EPS)
            oh = o * r
            dfw = dfw + jnp.sum(dfn * oh, axis=0, keepdims=True)
            doh = dfn * fnw_ref[...]
            pr_ = doh * oh
            m0 = jnp.sum(jnp.where(first, pr_, 0.0), axis=-1, keepdims=True)
            m1 = jnp.sum(jnp.where(first, 0.0, pr_), axis=-1, keepdims=True)
            dof_ref[:, cs] = r * (doh - oh * jnp.where(first, m0, m1) * (1.0 / FOX_HEAD_DIM))
        for hd in range(N_GDN_HEADS):
            cs = slice(hd * LANES, (hd + 1) * LANES)
            go = go_ref[:, cs]
            gz = gz_ref[:, cs]
            dgated = domix[:, D_FOX + hd * LANES:D_FOX + (hd + 1) * LANES]
            r = lax.rsqrt(jnp.mean(go * go, axis=-1, keepdims=True) + EPS)
            goh = go * r
            sg = _sigmoid(gz)
            sz = gz * sg
            gn = goh * gnw_ref[...]
            dgn = dgated * sz
            dgz_ref[:, cs] = (dgated * gn * sg * (1.0 + gz * (1.0 - sg))).astype(BF)
            dgw = dgw + jnp.sum(dgn * goh, axis=0, keepdims=True)
            dgh = dgn * gnw_ref[...]
            dgo_ref[:, cs] = r * (dgh - goh * jnp.mean(dgh * goh, axis=-1, keepdims=True))
        dfw_ref[...] += dfw + pltpu.roll(dfw, FOX_HEAD_DIM, 1)
        dgw_ref[...] += dgw

    tok = lambda w: pl.BlockSpec((tm, w), lambda i: (i, 0))
    vec = lambda w: pl.BlockSpec((1, w), lambda i: (0, 0))
    return pl.pallas_call(
        _ordered(body), name="out_bwd", grid=(t // tm,),
        in_specs=[ANY_SPEC, tok(D_MODEL), pl.BlockSpec((D_MODEL, D_MODEL), lambda i: (0, 0)), tok(D_FOX), tok(D_GDN),
                  pl.BlockSpec((tm, D_GDN), lambda i: (i, COL_GZ // D_GDN)), vec(LANES), vec(LANES)],
        out_specs=[tok(D_FOX), tok(D_GDN), pl.BlockSpec((tm, D_GDN), lambda i: (i, COL_GZ // D_GDN)), vec(LANES),
                   vec(LANES)],
        out_shape=[_sds((t, D_FOX)), _sds((t, D_GDN)), _sds((t, PROJ_W), BF), _sds((1, LANES)), _sds((1, LANES))],
        compiler_params=_params("arbitrary"),
    )(after, dmixed, w_out, o_fox, gdn_o, proj, fnw, gnw)


def _mlp_up(h2, w_upt):
    t = h2.shape[0]
    tm = min(MATMUL_BLOCK, t)

    def body(h_ref, w_ref, up_ref):
        up_ref[...] = lax.dot_general(h_ref[...], w_ref[...], (((1,), (1,)), ((), ())),
                                      preferred_element_type=F32).astype(BF)

    return pl.pallas_call(
        body, name="mlp_up", grid=(t // tm,),
        in_specs=[pl.BlockSpec((tm, D_MODEL), lambda i: (i, 0)), pl.BlockSpec((D_FF, D_MODEL), lambda i: (0, 0))],
        out_specs=pl.BlockSpec((tm, D_FF), lambda i: (i, 0)), out_shape=_sds((t, D_FF), BF),
        compiler_params=_params("parallel"),
    )(h2, w_upt)


def _mlp_down_loss(up, w_down, x1, pw, target):
    t = up.shape[0]
    tm = min(MATMUL_BLOCK, t)

    def body(up_ref, w_ref, x1_ref, pw_ref, tg_ref, dy_ref, dx2_ref, loss_ref, dpw_ref):
        i = pl.program_id(0)

        @pl.when(i == 0)
        def _():
            loss_ref[...] = jnp.zeros_like(loss_ref)
            dpw_ref[...] = jnp.zeros_like(dpw_ref)

        u = jnp.maximum(up_ref[...].astype(F32), 0.0)
        y = jnp.dot((u * u).astype(BF), w_ref[...], preferred_element_type=F32)
        r = lax.rsqrt(jnp.mean(y * y, axis=-1, keepdims=True) + EPS)
        yh = y * r
        pw = pw_ref[...]
        err = x1_ref[...] + yh * pw - tg_ref[...]
        part = jnp.sum(jnp.sum(err * err, axis=-1, keepdims=True), axis=0, keepdims=True) * (0.5 / D_MODEL)
        loss_ref[...] += jnp.broadcast_to(part, loss_ref.shape)
        dx2 = err * (1.0 / D_MODEL)
        dx2_ref[...] = dx2
        dpw_ref[...] += jnp.sum(dx2 * yh, axis=0, keepdims=True)
        dyh = dx2 * pw
        dy_ref[...] = (r * (dyh - yh * jnp.mean(dyh * yh, axis=-1, keepdims=True))).astype(BF)

    tok = lambda w: pl.BlockSpec((tm, w), lambda i: (i, 0))
    vec = lambda w: pl.BlockSpec((1, w), lambda i: (0, 0))
    return pl.pallas_call(
        body, name="mlp_down_loss", grid=(t // tm,),
        in_specs=[tok(D_FF), pl.BlockSpec((D_FF, D_MODEL), lambda i: (0, 0)), tok(D_MODEL), vec(D_MODEL), tok(D_MODEL)],
        out_specs=[tok(D_MODEL), tok(D_MODEL), vec(LANES), vec(D_MODEL)],
        out_shape=[_sds((t, D_MODEL), BF), _sds((t, D_MODEL)), _sds((1, LANES)), _sds((1, D_MODEL))],
        compiler_params=_params("arbitrary"),
    )(up, w_down, x1, pw, target)


def _mlp_bwd_act(dy, w_down, up):
    t = dy.shape[0]
    tm = min(MATMUL_BLOCK, t)

    def body(dy_ref, w_ref, up_ref, dup_ref):
        da = lax.dot_general(dy_ref[...], w_ref[...], (((1,), (1,)), ((), ())), preferred_element_type=F32)
        dup_ref[...] = (da * (2.0 * jnp.maximum(up_ref[...].astype(F32), 0.0))).astype(BF)

    return pl.pallas_call(
        body, name="mlp_bwd_act", grid=(t // tm,),
        in_specs=[pl.BlockSpec((tm, D_MODEL), lambda i: (i, 0)), pl.BlockSpec((D_FF, D_MODEL), lambda i: (0, 0)),
                  pl.BlockSpec((tm, D_FF), lambda i: (i, 0))],
        out_specs=pl.BlockSpec((tm, D_FF), lambda i: (i, 0)), out_shape=_sds((t, D_FF), BF),
        compiler_params=_params("parallel"),
    )(dy, w_down, up)


def _mlp_bwd_in(dup, w_up, x1, plw, dx2, mixed, pmw, after):
    t = dup.shape[0]
    tm = min(MATMUL_BLOCK, t)

    def body(dup_ref, w_ref, x1_ref, plw_ref, dx2_ref, mx_ref, pmw_ref, dx1_ref, dmixed_ref, dplw_ref, dpmw_ref):
        i = pl.program_id(0)

        @pl.when(i == 0)
        def _():
            dplw_ref[...] = jnp.zeros_like(dplw_ref)
            dpmw_ref[...] = jnp.zeros_like(dpmw_ref)

        dh = jnp.dot(dup_ref[...], w_ref[...], preferred_element_type=F32)
        x1 = x1_ref[...]
        r = lax.rsqrt(jnp.mean(x1 * x1, axis=-1, keepdims=True) + EPS)
        xh = x1 * r
        dplw_ref[...] += jnp.sum(dh * xh, axis=0, keepdims=True)
        dxh = dh * plw_ref[...]
        dx1 = dx2_ref[...] + r * (dxh - xh * jnp.mean(dxh * xh, axis=-1, keepdims=True))
        dx1_ref[...] = dx1
        mx = mx_ref[...]
        r2 = lax.rsqrt(jnp.mean(mx * mx, axis=-1, keepdims=True) + EPS)
        mh = mx * r2
        dpmw_ref[...] += jnp.sum(dx1 * mh, axis=0, keepdims=True)
        dmh = dx1 * pmw_ref[...]
        dmixed_ref[...] = (r2 * (dmh - mh * jnp.mean(dmh * mh, axis=-1, keepdims=True))).astype(BF)

    tok = lambda w: pl.BlockSpec((tm, w), lambda i: (i, 0))
    vec = lambda w: pl.BlockSpec((1, w), lambda i: (0, 0))
    return pl.pallas_call(
        _ordered(body), name="mlp_bwd_in", grid=(t // tm,),
        in_specs=[ANY_SPEC, tok(D_FF), pl.BlockSpec((D_FF, D_MODEL), lambda i: (0, 0)), tok(D_MODEL),
                  vec(D_MODEL), tok(D_MODEL), tok(D_MODEL), vec(D_MODEL)],
        out_specs=[tok(D_MODEL), tok(D_MODEL), vec(D_MODEL), vec(D_MODEL)],
        out_shape=[_sds((t, D_MODEL)), _sds((t, D_MODEL), BF), _sds((1, D_MODEL)), _sds((1, D_MODEL))],
        compiler_params=_params("arbitrary"),
    )(after, dup, w_up, x1, plw, dx2, mixed, pmw)


def _wgrad(a, b, a_cols, split=1, a_fn=None, a_block0=0, name="wgrad"):
    t, b_cols = b.shape
    n_a = (a.shape[1] - a_block0 * a_cols) // a_cols if a_block0 else a.shape[1] // a_cols

    def body(a_ref, b_ref, o_ref):
        av = a_ref[...]
        if a_fn is not None:
            av = a_fn(av)
        o_ref[...] = _mm_tn(av, b_ref[...]).astype(BF).reshape(o_ref.shape)

    return pl.pallas_call(
        body, name=name, grid=(n_a,),
        in_specs=[pl.BlockSpec((t, a_cols), lambda i: (0, i + a_block0)), pl.BlockSpec((t, b_cols), lambda i: (0, 0))],
        out_specs=pl.BlockSpec((split, a_cols // split, b_cols), lambda i: (i, 0, 0)),
        out_shape=_sds((n_a * split, a_cols // split, b_cols), BF),
        compiler_params=_params("parallel"),
    )(a, b)


def _wgrad_pre_t(at, b, b_cols, name):
    rows, t = at.shape
    n_b = b.shape[1] // b_cols

    def body(a_ref, b_ref, o_ref):
        o_ref[0] = jnp.dot(a_ref[...], b_ref[...], preferred_element_type=F32).astype(BF)

    return pl.pallas_call(
        body, name=name, grid=(n_b,),
        in_specs=[pl.BlockSpec((rows, t), lambda j: (0, 0)), pl.BlockSpec((t, b_cols), lambda j: (0, j))],
        out_specs=pl.BlockSpec((1, rows, b_cols), lambda j: (j, 0, 0)), out_shape=_sds((n_b, rows, b_cols), BF),
        compiler_params=_params("parallel"),
    )(at, b)


def _small_bwd(proj, fb, al, dtb, dcq, dckt, dbe, dge, dproj):
    t = proj.shape[0]

    def body(sm_ref, fb_ref, al_ref, dtb_ref, dcq_ref, dckt_ref, dbe_ref, dge_ref, _, dsm_ref, dvec_ref):
        s = sm_ref[...]
        lane = _iota((1, LANES), 1)
        dcum = dcq_ref[...] - dckt_ref[...].T
        row = _iota((t, LANES), 0)
        step = 1
        while step < t:
            dcum = dcum + _shift_up(dcum, step, row)
            step *= 2
        dff = dcum * _sigmoid(-(s + fb_ref[...]))
        dbeta = jnp.zeros((t, LANES), F32)
        dg = jnp.zeros((t, LANES), F32)
        for hd in range(N_GDN_HEADS):
            dbeta = jnp.where(lane == SM_GB + hd, dbe_ref[:, hd * LANES:hd * LANES + 1], dbeta)
            dg = jnp.where(lane == SM_GA + hd, dge_ref[:, hd * LANES:hd * LANES + 1], dg)
        beta = _sigmoid(s)
        dgb = dbeta * beta * (1.0 - beta)
        za = s + dtb_ref[...]
        nea = -jnp.exp(al_ref[...])
        dga = dg * nea * _sigmoid(za)
        is_f = lane < SM_GB
        is_b = (lane >= SM_GB) & (lane < SM_GA)
        is_a = (lane >= SM_GA) & (lane < SM_GA + 4)
        dsm_ref[...] = jnp.where(is_f, dff, jnp.where(is_b, dgb, jnp.where(is_a, dga, 0.0))).astype(BF)
        dvec_ref[...] = jnp.zeros_like(dvec_ref)
        dvec_ref[0:1, :] = jnp.sum(jnp.where(is_f, dff, 0.0), axis=0, keepdims=True)
        dvec_ref[1:2, :] = jnp.sum(jnp.where(is_a, dg * nea * _softplus(za), 0.0), axis=0, keepdims=True)
        dvec_ref[2:3, :] = jnp.sum(jnp.where(is_a, dga, 0.0), axis=0, keepdims=True)

    vec = pl.BlockSpec((1, LANES), lambda i: (0, 0))
    full = lambda r, c: pl.BlockSpec((r, c), lambda i: (0, 0))
    small = pl.BlockSpec((t, LANES), lambda i: (0, COL_SMALL // LANES))
    return pl.pallas_call(
        body, name="small_bwd", grid=(1,),
        in_specs=[small, vec, vec, vec, full(t, LANES), full(LANES, t), full(t, 512), full(t, 512), ANY_SPEC],
        out_specs=[small, full(8, LANES)], out_shape=[_sds(dproj.shape, BF), _sds((8, LANES))],
        input_output_aliases={8: 0}, compiler_params=_params("arbitrary"),
    )(proj, fb, al, dtb, dcq, dckt, dbe, dge, dproj)


def _in_bwd(dproj, wt_al, x, nw, dx1, after):
    t = x.shape[0]
    tm = min(MATMUL_BLOCK, t)

    def body(dp_ref, w_ref, x_ref, nw_ref, dx1_ref, dx_ref, dnw_ref):
        i = pl.program_id(0)

        @pl.when(i == 0)
        def _():
            dnw_ref[...] = jnp.zeros_like(dnw_ref)

        dh = jnp.dot(dp_ref[...], w_ref[...], preferred_element_type=F32)
        xv = x_ref[...]
        r = lax.rsqrt(jnp.mean(xv * xv, axis=-1, keepdims=True) + EPS)
        xh = xv * r
        dnw_ref[...] += jnp.sum(dh * xh, axis=0, keepdims=True)
        dxh = dh * nw_ref[...]
        dx_ref[...] = dx1_ref[...] + r * (dxh - xh * jnp.mean(dxh * xh, axis=-1, keepdims=True))

    tok = lambda w: pl.BlockSpec((tm, w), lambda i: (i, 0))
    vec = lambda w: pl.BlockSpec((1, w), lambda i: (0, 0))
    return pl.pallas_call(
        _ordered(body), name="in_bwd", grid=(t // tm,),
        in_specs=[ANY_SPEC, tok(PROJ_W), pl.BlockSpec((PROJ_W, D_MODEL), lambda i: (0, 0)), tok(D_MODEL), vec(D_MODEL),
                  tok(D_MODEL)],
        out_specs=[tok(D_MODEL), vec(D_MODEL)], out_shape=[_sds((t, D_MODEL)), _sds((1, D_MODEL))],
        compiler_params=_params("arbitrary"),
    )(after, dproj, wt_al, x, nw, dx1)


def _row(v, width=None):
    v = v.reshape(1, -1).astype(F32)
    if width is not None and v.shape[1] < width:
        v = jnp.pad(v, ((0, 0), (0, width - v.shape[1])))
    return v


def _lane_vec(v, first):
    return jnp.pad(v.astype(F32), (first, LANES - first - v.shape[0])).reshape(1, LANES)


def _local_step(x, target, wt_al, started, late_weights, on_grads, convw, pre_mix_norm, fox_f_bias, fox_out_norm,
                gdn_a_log, gdn_dt_bias, gdn_out_norm, post_mix_norm, pre_mlp_norm, post_mlp_norm):
    t = x.shape[0]
    nch = t // CHUNK
    nw, pmw, plw, pw = _row(pre_mix_norm), _row(post_mix_norm), _row(pre_mlp_norm), _row(post_mlp_norm)
    fb, al, dtb = _lane_vec(fox_f_bias, SM_FF), _lane_vec(gdn_a_log, SM_GA), _lane_vec(gdn_dt_bias, SM_GA)
    fnw = _row(jnp.tile(fox_out_norm, 2))
    gnw = _row(gdn_out_norm)

    proj, h = _norm_proj(x, nw, wt_al, started)
    cumt, beta, g = _small_prep(proj, fb, al, dtb)
    qn, kn, cv, gc, be, mmat, amat = _gdn_prep(proj, convw, beta, g)
    n_prob = N_GDN_HEADS * nch
    m3 = mmat.reshape(n_prob, CHUNK, CHUNK)
    if n_prob < LANES:
        m3 = jnp.pad(m3, ((0, LANES - n_prob), (0, 0), (0, 0)))
    tinv = _tri_inverse(m3)[:n_prob].reshape(N_GDN_HEADS, nch, CHUNK, CHUNK)
    gdn_o, s_all, vn_all = _gdn_scan(qn, kn, cv, be, gc, tinv, amat)
    token = late_weights("mlp_relay", gdn_o)
    o_fox, lse, fox_n = _fox_fwd(proj, cumt, fnw, token)
    w_out = late_weights("w_out", fox_n)
    x1, h2, mixed, omix, h2t = _mix_out(fox_n, gdn_o, proj, gnw, w_out, x, pmw, plw, token)
    w_up, w_down = late_weights("mlp", h2)
    up = _mlp_up(h2, w_up)
    dy, dx2, loss, d_pw = _mlp_down_loss(up, w_down, x1, pw, target)

    dup = _mlp_bwd_act(dy, w_down, up)
    relu2 = lambda u: jnp.square(jnp.maximum(u.astype(F32), 0.0))
    g_down = _wgrad(up, dy, D_FF // N_DEV, a_fn=relu2, name="wgrad_down")
    g_up = _wgrad_pre_t(h2t, dup, D_FF // N_DEV, name="wgrad_up")
    token = on_grads("mlp", (g_up, g_down))
    dx1, dmixed, d_plw, d_pmw = _mlp_bwd_in(dup, w_up, x1, plw, dx2, mixed, pmw, token)
    token = on_grads("w_out", _wgrad(omix, dmixed, 512, split=4, name="wgrad_out"))
    do_fox, dgo, dproj, d_fnw, d_gnw = _out_bwd(dmixed, w_out, o_fox, gdn_o, proj, fnw, gnw, token)
    dproj, dcq, dckt = _fox_bwd(proj, cumt, lse, o_fox, do_fox, dproj)
    dqn, dkn, dcv, dbe, dge = _gdn_bwd(qn, kn, cv, be, gc, tinv, amat, s_all, vn_all, dgo)
    dproj, dwq, dwk, dwv = _gdn_bwd_conv(proj, convw, dqn, dkn, dcv, dproj)
    dproj, dvec = _small_bwd(proj, fb, al, dtb, dcq, dckt, dbe, dge, dproj)
    g_main = _wgrad(dproj, h, WGRAD_IN_ROWS, name="wgrad_in")
    g_tail = _wgrad(dproj, h, LANES, a_block0=COL_SMALL // LANES, name="wgrad_in_small")
    token = on_grads("w_in", (g_main, g_tail))
    grad_x, d_nw = _in_bwd(dproj, wt_al, x, nw, dx1, token)
    small = dict(norms=(d_nw, d_pmw, d_plw, d_pw), fox_out_norm=d_fnw, gdn_out_norm=d_gnw, loss=loss, vectors=dvec,
                 conv=(dwq, dwk, dwv))
    return grad_x, small


MESH_IDS = pl.DeviceIdType.MESH
CHIP_FLIPS = ((0, 0), (1, 0), (0, 1), (1, 1))


def _place():
    return lax.axis_index("x"), lax.axis_index("y"), lax.axis_index("c")


def _all_gather(blocks, later, dtype):
    n, k = len(blocks), len(later)

    def body(*refs):
        ins, shards, outs = refs[:n], refs[n:n + k], refs[n + k:2 * n + k]
        zones, to_send = refs[2 * n + k:2 * n + 2 * k], refs[2 * n + 2 * k:2 * n + 3 * k]
        stage_in, stage_out = refs[2 * n + 3 * k:2 * n + 4 * k], refs[2 * n + 4 * k:2 * n + 5 * k]
        send_sems, recv_sems, local_sems, late_sems = refs[2 * n + 5 * k:]
        x, y, c = _place()
        sibling = (x, y, 1 - c)
        chips = [(x ^ fx, y ^ fy) for fx, fy in CHIP_FLIPS[1:]]

        def slot(out, px, py, pc):
            return out.at[4 * px + 2 * py + pc]

        def copy(a, k, block, to, src=None):
            return pltpu.make_async_remote_copy(
                src_ref=slot(outs[a], *block) if src is None else src, dst_ref=slot(outs[a], *block),
                send_sem=send_sems.at[a, k], recv_sem=recv_sems.at[a, k], device_id=to, device_id_type=MESH_IDS)

        pending = []
        for a in range(n):
            mine = pltpu.make_async_copy(ins[a], slot(outs[a], x, y, c), local_sems.at[a])
            mine.start()
            pending.append(mine)
        sends = []
        for a in range(n):
            first = [copy(a, 1 + j, (x, y, c), (*chip, c), src=ins[a]) for j, chip in enumerate(chips)][::-1]
            first.append(copy(a, 0, (x, y, c), sibling, src=ins[a]))
            for cp in first:
                cp.start()
            sends += first
        loads = [pltpu.make_async_copy(shards[a], stage_in[a], late_sems.at[a, 0]) for a in range(k)]
        for cp in loads:
            cp.start()
        for a, (_, transposed) in enumerate(later):
            loads[a].wait()
            val = stage_in[a][...]
            stage_out[a][...] = (val.T if transposed else val).astype(dtype)
            for j, dst in enumerate((slot(zones[a], x, y, c), to_send[a])):
                cp = pltpu.make_async_copy(stage_out[a], dst, late_sems.at[a, 1 + j])
                cp.start()
                pending.append(cp)
        for a in range(n):
            for j, chip in reversed(list(enumerate(chips))):
                copy(a, 1 + j, (*chip, c), (x, y, c)).wait_recv()
                fwd = copy(a, 4 + j, (*chip, c), sibling)
                fwd.start()
                sends.append(fwd)
        for a in range(n):
            copy(a, 0, sibling, (x, y, c)).wait_recv()
            for j, chip in enumerate(chips):
                copy(a, 4 + j, (*chip, 1 - c), (x, y, c)).wait_recv()
        for cp in sends:
            cp.wait_send()
        for cp in pending:
            cp.wait()

    shapes = [s_.shape[::-1] if transposed else s_.shape for s_, transposed in later]
    out = pl.pallas_call(
        body, name="all_gather_weights", in_specs=[ANY_SPEC] * (n + k), out_specs=[ANY_SPEC] * (n + 2 * k),
        out_shape=[_sds((N_DEV,) + b.shape, b.dtype) for b in blocks] + [_sds((N_DEV,) + sh, dtype) for sh in shapes]
        + [_sds(sh, dtype) for sh in shapes],
        scratch_shapes=[pltpu.VMEM(s_.shape, s_.dtype) for s_, _ in later] + [pltpu.VMEM(sh, dtype) for sh in shapes]
        + [pltpu.SemaphoreType.DMA((n, 7)), pltpu.SemaphoreType.DMA((n, 7)), pltpu.SemaphoreType.DMA((n,)),
           pltpu.SemaphoreType.DMA((k, 3))],
        compiler_params=pltpu.CompilerParams(vmem_limit_bytes=VMEM_LIMIT, has_side_effects=True),
    )(*blocks, *[s_ for s_, _ in later])
    return out[:n], out[n:n + k], out[n + k:]


def _adamw(w, g, m, v):
    m = ADAM_B1 * m + (1.0 - ADAM_B1) * g
    v = ADAM_B2 * v + (1.0 - ADAM_B2) * (g * g)
    m_hat = m / (1.0 - ADAM_B1 ** ADAM_STEP)
    v_hat = v / (1.0 - ADAM_B2 ** ADAM_STEP)
    return -ADAM_LR * (m_hat / (jnp.sqrt(v_hat) + ADAM_EPS) + ADAM_WD * w), m, v


def _pair_reduce(g, name):
    _, r, c_ = g.shape
    n = len(CHIP_FLIPS)

    def body(g_ref, out_ref, sib_buf, send_sems, recv_sems):
        x, y, c = _place()
        chips = [(x ^ fx, y ^ fy) for fx, fy in CHIP_FLIPS]
        piece = lambda chip, core: g_ref.at[4 * chip[0] + 2 * chip[1] + core]
        copies = [pltpu.make_async_remote_copy(
            src_ref=piece(chip, 1 - c), dst_ref=sib_buf.at[j], send_sem=send_sems.at[j], recv_sem=recv_sems.at[j],
            device_id=(x, y, 1 - c), device_id_type=MESH_IDS) for j, chip in enumerate(chips)]
        for cp in copies:
            cp.start()
        for j, chip in enumerate(chips):
            copies[j].wait_recv()
            out_ref[j] = (piece(chip, c)[...].astype(F32) + sib_buf[j].astype(F32)).astype(BF)
        for cp in copies:
            cp.wait_send()

    return pl.pallas_call(
        body, name=name, in_specs=[VMEM_SPEC], out_specs=VMEM_SPEC, out_shape=_sds((n, r, c_), BF),
        scratch_shapes=[pltpu.VMEM((n, r, c_), BF), pltpu.SemaphoreType.DMA((n,)), pltpu.SemaphoreType.DMA((n,))],
        compiler_params=pltpu.CompilerParams(vmem_limit_bytes=VMEM_LIMIT, has_side_effects=True),
    )(g)


HBM_SPEC = pl.BlockSpec(memory_space=pltpu.HBM)
SEM_SPEC = pl.BlockSpec(memory_space=pltpu.SEMAPHORE)
DATAFLOW = pltpu.SideEffectType.DATAFLOW_SIDE_EFFECTING


def _peers():
    x, y, c = _place()
    return 4 * x + 2 * y + c, [(x ^ (k >> 2), y ^ ((k >> 1) & 1), c ^ (k & 1)) for k in range(1, N_DEV)]


def _peer_index(peer):
    return 4 * peer[0] + 2 * peer[1] + peer[2]


def _exchange_start(srcs, zones, pieces, name, chips=False):
    n = len(srcs)
    fresh = zones is None
    if fresh:
        slots = len(CHIP_FLIPS) if chips else N_DEV
        zones = [_sds((slots,) + (v.shape[1:] if pieces else v.shape), v.dtype) for v in srcs]
    n_in = n if fresh else 2 * n
    among_chips = list(chips) if isinstance(chips, (list, tuple)) else [chips] * n

    def body(*refs):
        ins, sems, token = refs[:n], refs[n_in:n_in + 2 * n], refs[-1]
        zs = refs[n_in + 3 * n:n_in + 4 * n] if fresh else refs[n:2 * n]
        me, peers = _peers()
        x, y, c = _place()
        for a in range(n):
            if among_chips[a] and pieces:
                routes = [((x ^ fx, y ^ fy, c), j, j) for j, (fx, fy) in enumerate(CHIP_FLIPS) if j]
            elif among_chips[a]:
                routes = [((x ^ fx, y ^ fy, c), None, me) for fx, fy in CHIP_FLIPS[1:]]
            else:
                routes = [(peer, _peer_index(peer) if pieces else None, me) for peer in peers]
            for peer, src_slot, dst_slot in routes:
                pltpu.make_async_remote_copy(
                    src_ref=ins[a] if src_slot is None else ins[a].at[src_slot], dst_ref=zs[a].at[dst_slot],
                    send_sem=sems[2 * a], recv_sem=sems[2 * a + 1], device_id=peer, device_id_type=MESH_IDS).start()
        token[...] = jnp.zeros_like(token)

    hbm = lambda v: pltpu.with_memory_space_constraint(v, pltpu.HBM)
    out = pl.pallas_call(
        body, name=name,
        out_shape=tuple([pltpu.SemaphoreType.DMA(())] * (2 * n) + [pltpu.HBM(v.shape, v.dtype) for v in srcs]
                        + [pltpu.HBM(z.shape, z.dtype) for z in zones] + [_sds((8, LANES))]),
        in_specs=[HBM_SPEC] * n_in, out_specs=tuple([SEM_SPEC] * (2 * n) + [HBM_SPEC] * (2 * n) + [VMEM_SPEC]),
        input_output_aliases={i: 2 * n + i for i in range(n_in)},
        compiler_params=pltpu.CompilerParams(has_side_effects=DATAFLOW),
    )(*[hbm(v) for v in srcs], *([] if fresh else [hbm(z) for z in zones]))
    return out[:2 * n], out[2 * n:3 * n], out[3 * n:4 * n], out[-1]


def _relay_start(zones, name):
    n = len(zones)

    def body(*refs):
        zs, sems, token = refs[:n], refs[n:3 * n], refs[-1]
        x, y, c = _place()
        for fx, fy in CHIP_FLIPS:
            slot = 4 * (x ^ fx) + 2 * (y ^ fy) + c
            for a in range(n):
                pltpu.make_async_remote_copy(
                    src_ref=zs[a].at[slot], dst_ref=zs[a].at[slot], send_sem=sems[2 * a], recv_sem=sems[2 * a + 1],
                    device_id=(x, y, 1 - c), device_id_type=MESH_IDS).start()
        token[...] = jnp.zeros_like(token)

    out = pl.pallas_call(
        body, name=name,
        out_shape=tuple([pltpu.SemaphoreType.DMA(())] * (2 * n) + [pltpu.HBM(z.shape, z.dtype) for z in zones]
                        + [_sds((8, LANES))]),
        in_specs=[HBM_SPEC] * n, out_specs=tuple([SEM_SPEC] * (2 * n) + [HBM_SPEC] * n + [VMEM_SPEC]),
        input_output_aliases={i: 2 * n + i for i in range(n)},
        compiler_params=pltpu.CompilerParams(has_side_effects=DATAFLOW),
    )(*[pltpu.with_memory_space_constraint(z, pltpu.HBM) for z in zones])
    return out[:2 * n], [], out[2 * n:3 * n], out[-1]


def _exchange_wait(sems, srcs, zones, after, name, chips=False, n_copies=None):
    n, n_src = len(zones), len(srcs)
    after = list(after) if isinstance(after, (list, tuple)) else [after]
    n_copies = n_copies or (len(CHIP_FLIPS) - 1 if chips else N_DEV - 1)

    def body(*refs):
        zs, sm = refs[n_src:n_src + n], refs[n_src + n:n_src + 3 * n]
        me, peers = _peers()
        for a in range(n):
            seven = zs[a].at[pl.ds(0, n_copies)]
            cp = pltpu.make_async_remote_copy(src_ref=seven, dst_ref=seven, send_sem=sm[2 * a], recv_sem=sm[2 * a + 1],
                                              device_id=peers[0], device_id_type=MESH_IDS)
            cp.wait_send()
            cp.wait_recv()

    out = pl.pallas_call(
        body, name=name, out_shape=tuple([pltpu.HBM(v.shape, v.dtype) for v in srcs] + [pltpu.HBM(z.shape, z.dtype) for z in zones]),
        in_specs=[HBM_SPEC] * (n_src + n) + [SEM_SPEC] * (2 * n) + [ANY_SPEC] * len(after),
        out_specs=tuple([HBM_SPEC] * (n_src + n)), input_output_aliases={i: i for i in range(n_src + n)},
        compiler_params=pltpu.CompilerParams(has_side_effects=DATAFLOW),
    )(*srcs, *zones, *sems, *after)
    return out[:n_src], out[n_src:]


def _sum_adamw(zone, own, w, m, v, name, chips=False):
    n_slots, r, c_ = zone.shape
    rb = next((b for b in (256, 128) if r % b == 0), r)

    def body(me_ref, z_ref, own_ref, w_ref, m_ref, v_ref, grad_ref, delta_ref, nm_ref, nv_ref):
        total = None
        for d in range(n_slots):
            part = jnp.where(me_ref[0] == d, own_ref[0], z_ref[d]).astype(F32)
            total = part if total is None else total + part
        grad_ref[...] = total
        delta_ref[...], nm_ref[...], nv_ref[...] = _adamw(w_ref[...], total, m_ref[...], v_ref[...])

    x, y, c = _place()
    mine = 0 * x if chips else 4 * x + 2 * y + c
    blk = pl.BlockSpec((rb, c_), lambda i, me_ref: (i, 0))
    return pl.pallas_call(
        body, name=name,
        grid_spec=pltpu.PrefetchScalarGridSpec(
            num_scalar_prefetch=1, grid=(r // rb,),
            in_specs=[pl.BlockSpec((n_slots, rb, c_), lambda i, me_ref: (0, i, 0)),
                      pl.BlockSpec((1, rb, c_), lambda i, me_ref: (me_ref[0], i, 0)), blk, blk, blk],
            out_specs=[blk] * 4),
        out_shape=[_sds((r, c_))] * 4, compiler_params=_params("parallel"),
    )(mine.astype(jnp.int32).reshape(1), zone, own, w, m, v)


SMALL_NORMS = ("pre_mix_norm", "post_mix_norm", "pre_mlp_norm", "post_mlp_norm")
SMALL_ORDER = SMALL_NORMS + ("fox_out_norm", "gdn_out_norm", "fox_f_bias", "gdn_a_log", "gdn_dt_bias", "gdn_conv_w")
CONV_SLAB_ROWS, CONV_SLAB_LANES = 8, 256


def _small_pack(small):
    def body(n0, n1, n2, n3, fnw_ref, gnw_ref, loss_ref, vec_ref, out_ref):
        out_ref[...] = jnp.zeros_like(out_ref)
        for i, ref in enumerate((n0, n1, n2, n3)):
            out_ref[i:i + 1, :] = ref[...]
        out_ref[4:5, 0:LANES] = fnw_ref[...]
        out_ref[4:5, LANES:2 * LANES] = gnw_ref[...]
        out_ref[4:5, 2 * LANES:3 * LANES] = loss_ref[...]
        out_ref[5:8, 0:LANES] = vec_ref[0:3, :]

    return pl.pallas_call(body, name="small_pack", in_specs=[VMEM_SPEC] * 8, out_specs=VMEM_SPEC,
                          out_shape=_sds((8, D_MODEL)))(*small["norms"], small["fox_out_norm"], small["gdn_out_norm"],
                                                        small["loss"], small["vectors"])


def _conv_slabs(dconv):
    blocks = dconv.reshape(CONV_K, N_DEV, -1).transpose(1, 0, 2)
    blocks = jnp.pad(blocks, ((0, 0), (0, CONV_SLAB_ROWS - CONV_K), (0, CONV_SLAB_LANES - blocks.shape[2])))
    return blocks.reshape(N_DEV * CONV_SLAB_ROWS, CONV_SLAB_LANES)


def _small_update(zone, conv_zone, own, own_conv, w, m, v):
    n = len(SMALL_ORDER)
    n_conv = w["gdn_conv_w"].shape[1]

    def body(me_ref, z_ref, zc_ref, own_ref, ownc_ref, *refs):
        params, loss_ref, outs, (tot, totc) = refs[:3 * n], refs[3 * n], refs[3 * n + 1:7 * n + 1], refs[-2:]
        total, total_c = None, None
        for d in range(N_DEV):
            part = jnp.where(me_ref[0] == d, own_ref[...], z_ref[d])
            part_c = jnp.where(me_ref[0] == d, ownc_ref[...], zc_ref[d])
            total, total_c = (part, part_c) if d == 0 else (total + part, total_c + part_c)
        tot[...] = total
        totc[...] = total_c
        loss_ref[...] = tot[4, 2 * LANES:2 * LANES + 1]
        mine = totc[pl.ds(pl.multiple_of(me_ref[0] * CONV_SLAB_ROWS, CONV_SLAB_ROWS), CONV_SLAB_ROWS), :]
        g = dict(zip(SMALL_NORMS, (tot[0], tot[1], tot[2], tot[3])))
        g.update(fox_out_norm=tot[4, 0:FOX_HEAD_DIM], gdn_out_norm=tot[4, LANES:LANES + GDN_HEAD_DIM],
                 fox_f_bias=tot[5, SM_FF:SM_FF + N_FOX_HEADS], gdn_a_log=tot[6, SM_GA:SM_GA + N_GDN_HEADS],
                 gdn_dt_bias=tot[7, SM_GA:SM_GA + N_GDN_HEADS], gdn_conv_w=mine[0:CONV_K, 0:n_conv])
        for i, name in enumerate(SMALL_ORDER):
            w_ref, m_ref, v_ref = params[3 * i:3 * i + 3]
            outs[4 * i][...] = g[name]
            outs[4 * i + 1][...], outs[4 * i + 2][...], outs[4 * i + 3][...] = _adamw(w_ref[...], g[name], m_ref[...],
                                                                                     v_ref[...])

    x, y, c = _place()
    operands = [a[name] for name in SMALL_ORDER for a in (w, m, v)]
    out = pl.pallas_call(
        body, name="small_update",
        in_specs=[pl.BlockSpec(memory_space=pltpu.SMEM)] + [VMEM_SPEC] * (4 + 3 * n), out_specs=[VMEM_SPEC] * (1 + 4 * n),
        out_shape=[_sds((1,))] + [_sds(w[name].shape) for name in SMALL_ORDER for _ in range(4)],
        scratch_shapes=[pltpu.VMEM(zone.shape[1:], F32), pltpu.VMEM(conv_zone.shape[1:], F32)],
    )((4 * x + 2 * y + c).astype(jnp.int32).reshape(1), zone, conv_zone, own, own_conv, *operands)
    return out[0][0], {name: out[1 + 4 * i:5 + 4 * i] for i, name in enumerate(SMALL_ORDER)}


def _native_rows():
    groups = []
    for first, n_groups in ((0, N_FOX_HEADS // 2), (D_FOX * 3 + N_FOX_HEADS, N_GDN_HEADS)):
        for g in range(n_groups):
            groups += [(first + part * n_groups * LANES + g * LANES, first + part * n_groups * LANES + (g + 1) * LANES)
                       for part in range(3)]
    return tuple(groups) + ((3088, 3600), (1536, 1544), (3080, 3088))


NATIVE_ROWS = _native_rows()


W_IN_PIECE = D_PROJ // N_DEV
WGRAD_IN_ROWS = 512
SHUFFLE_LANES = 256


def _to_aligned_moves():
    moves, o = [], 0
    for lo, hi in NATIVE_ROWS:
        r = lo
        while r < hi:
            d = r // W_IN_PIECE
            k = min(hi, (d + 1) * W_IN_PIECE) - r
            moves.append((0, d, r - d * W_IN_PIECE, 0, o, k))
            r, o = r + k, o + k
    return moves


def _from_aligned_moves():
    moves = []
    for _, d, a, _, o, k in _to_aligned_moves():
        while k:
            n = min(k, WGRAD_IN_ROWS - o % WGRAD_IN_ROWS) if o < COL_SMALL else k
            moves.append((0, o // WGRAD_IN_ROWS, o % WGRAD_IN_ROWS, d, a, n) if o < COL_SMALL else
                         (1, 0, o - COL_SMALL, d, a, n))
            o, a, k = o + n, a + n, k - n
    return moves


def _shuffle_rows(srcs, moves, out_shape, name):
    c = srcs[0].shape[-1]

    def body(*refs):
        s_refs, o_ref, s_f, o_f = refs[:len(srcs)], refs[len(srcs)], refs[len(srcs) + 1:-1], refs[-1]
        for s_ref, f in zip(s_refs, s_f):
            f[...] = s_ref[...].astype(F32)
        o_f[...] = jnp.zeros_like(o_f)
        for i, ss, so, ds, do, k in moves:
            o_f[ds, pl.ds(do, k), :] = s_f[i][ss, pl.ds(so, k), :]
        o_ref[...] = o_f[...].astype(BF)

    blk = lambda shape: pl.BlockSpec(tuple(shape[:-1]) + (SHUFFLE_LANES,), lambda j: (0, 0, j))
    scratch = lambda shape: pltpu.VMEM(tuple(shape[:-1]) + (SHUFFLE_LANES,), F32)
    return pl.pallas_call(
        body, name=name, grid=(c // SHUFFLE_LANES,), in_specs=[blk(s.shape) for s in srcs], out_specs=blk(out_shape),
        out_shape=_sds(out_shape, BF), scratch_shapes=[scratch(s.shape) for s in srcs] + [scratch(out_shape)],
        compiler_params=_params("parallel"),
    )(*srcs)


def _cols_from_pieces(p):
    return p.transpose(1, 0, 2).reshape(p.shape[1], -1)


WEIGHT_ORDER = ("pre_mix_norm", "w_in", "fox_f_bias", "fox_out_norm", "gdn_conv_w", "gdn_a_log", "gdn_dt_bias",
                "gdn_out_norm", "w_out", "post_mix_norm", "pre_mlp_norm", "w_up", "w_down", "post_mlp_norm")


def kernel(x, pre_mix_norm, w_in, fox_f_bias, fox_out_norm, gdn_conv_w, gdn_a_log, gdn_dt_bias, gdn_out_norm, w_out, post_mix_norm, pre_mlp_norm, w_up, w_down, post_mlp_norm, loss_target, m_pre_mix_norm, m_w_in, m_fox_f_bias, m_fox_out_norm, m_gdn_conv_w, m_gdn_a_log, m_gdn_dt_bias, m_gdn_out_norm, m_w_out, m_post_mix_norm, m_pre_mlp_norm, m_w_up, m_w_down, m_post_mlp_norm, v_pre_mix_norm, v_w_in, v_fox_f_bias, v_fox_out_norm, v_gdn_conv_w, v_gdn_a_log, v_gdn_dt_bias, v_gdn_out_norm, v_w_out, v_post_mix_norm, v_pre_mlp_norm, v_w_up, v_w_down, v_post_mlp_norm):
    w = dict(pre_mix_norm=pre_mix_norm, w_in=w_in, fox_f_bias=fox_f_bias, fox_out_norm=fox_out_norm,
             gdn_conv_w=gdn_conv_w, gdn_a_log=gdn_a_log, gdn_dt_bias=gdn_dt_bias, gdn_out_norm=gdn_out_norm, w_out=w_out,
             post_mix_norm=post_mix_norm, pre_mlp_norm=pre_mlp_norm, w_up=w_up, w_down=w_down, post_mlp_norm=post_mlp_norm)
    mom = dict(pre_mix_norm=m_pre_mix_norm, w_in=m_w_in, fox_f_bias=m_fox_f_bias, fox_out_norm=m_fox_out_norm,
               gdn_conv_w=m_gdn_conv_w, gdn_a_log=m_gdn_a_log, gdn_dt_bias=m_gdn_dt_bias, gdn_out_norm=m_gdn_out_norm,
               w_out=m_w_out, post_mix_norm=m_post_mix_norm, pre_mlp_norm=m_pre_mlp_norm, w_up=m_w_up, w_down=m_w_down,
               post_mlp_norm=m_post_mlp_norm)
    var = dict(pre_mix_norm=v_pre_mix_norm, w_in=v_w_in, fox_f_bias=v_fox_f_bias, fox_out_norm=v_fox_out_norm,
               gdn_conv_w=v_gdn_conv_w, gdn_a_log=v_gdn_a_log, gdn_dt_bias=v_gdn_dt_bias, gdn_out_norm=v_gdn_out_norm,
               w_out=v_w_out, post_mix_norm=v_post_mix_norm, pre_mlp_norm=v_pre_mlp_norm, w_up=v_w_up, w_down=v_w_down,
               post_mlp_norm=v_post_mlp_norm)

    (win_g, conv_g), zones, shards = _all_gather([w_in.T.astype(BF), gdn_conv_w],
                                                 [(w_out, False), (w_up, True), (w_down, False)], BF)
    wt_al = _shuffle_rows([win_g], _to_aligned_moves(), (1, PROJ_W, D_MODEL), "w_in_to_aligned")[0]
    convw = _cols_from_pieces(conv_g)
    sems, shards, zones, after = _exchange_start(shards, zones, False, "gather_start", chips=True)
    gathers = dict(hop=(sems, shards, zones))

    def late_weights(name, after):
        if name == "mlp_relay":
            sems, shards, zones = gathers.pop("hop")
            _, zones = _exchange_wait(sems, shards, zones, after, "gather_wait", chips=True)
            sems, _, zones, token = _relay_start(zones, "gather_relay")
            gathers.update(w_out=(sems[:2], zones[:1]), mlp=(sems[2:], zones[1:]))
            return token
        sems, zones = gathers[name]
        _, got = _exchange_wait(sems, [], zones, after, "gather_" + name + "_done", n_copies=len(CHIP_FLIPS))
        if name == "w_out":
            return got[0].reshape(D_MODEL, D_MODEL)
        return got[0].reshape(D_FF, D_MODEL), got[1].reshape(D_FF, D_MODEL)

    scatters = {}

    def on_grads(name, g):
        if name == "mlp":
            scatters["mlp"] = list(g)
            return g[0]
        if name == "w_out":
            sems, srcs, zones, token = _exchange_start(scatters["mlp"] + [g], None, True, "scatter_mlp_w_out_start")
            scatters["mlp"] = (sems[:4], srcs[:2], zones[:2], token)
            scatters["w_out"] = (sems[4:], srcs[2:], zones[2:], token)
            return token
        g = _shuffle_rows(list(g), _from_aligned_moves(), (N_DEV, W_IN_PIECE, D_MODEL), "w_in_grad_from_aligned")
        scatters[name] = _exchange_start([_pair_reduce(g, "pair_reduce_w_in")], None, True, "scatter_w_in_start", chips=True)
        return scatters[name][3]

    grad_x, small = _local_step(
        x[0], loss_target[0], wt_al, after, late_weights, on_grads, convw, pre_mix_norm,
        fox_f_bias, fox_out_norm, gdn_a_log, gdn_dt_bias, gdn_out_norm, post_mix_norm, pre_mlp_norm, post_mlp_norm)
    slabs = [_small_pack(small), _conv_slabs(jnp.concatenate(small["conv"], axis=1))]
    scatters["small"] = _exchange_start(slabs, None, False, "small_start")

    grads, delta, new_m, new_v = {}, {}, {}, {}
    after = scatters["small"][3]
    for name, members in (("mlp", ("w_up", "w_down")), ("w_out", ("w_out",)), ("small", ()), ("w_in", ("w_in",))):
        sems, srcs, zones, _ = scatters[name]
        srcs, zones = _exchange_wait(sems, srcs, zones, after, "scatter_" + name + "_wait", chips=name == "w_in")
        if name == "small":
            loss, updated = _small_update(*zones, *srcs, w, mom, var)
            for n, res in updated.items():
                grads[n], delta[n], new_m[n], new_v[n] = res
            after = grads["pre_mix_norm"]
        for n, zone, own in zip(members, zones, srcs):
            if n == "w_in":
                res = _sum_adamw(zone, own, w[n].T, mom[n].T, var[n].T, "adamw_" + n, chips=True)
                grads[n], delta[n], new_m[n], new_v[n] = [r.T for r in res]
            else:
                grads[n], delta[n], new_m[n], new_v[n] = _sum_adamw(zone, own, w[n], mom[n], var[n], "adamw_" + n)
        if members:
            after = [grads[n] for n in members]

    return (loss, grad_x[None], *[grads[n] for n in WEIGHT_ORDER], *[delta[n] for n in WEIGHT_ORDER],
            *[new_m[n] for n in WEIGHT_ORDER], *[new_v[n] for n in WEIGHT_ORDER])
```

```python
import jax
import jax.numpy as jnp
from jax import lax
from jax.experimental import pallas as pl
from jax.experimental.pallas import tpu as pltpu

F32 = jnp.float32
BF = jnp.bfloat16

D_MODEL = 1024
N_FOX_HEADS, FOX_HEAD_DIM = 8, 64
N_GDN_HEADS, GDN_HEAD_DIM = 4, 128
D_FOX = N_FOX_HEADS * FOX_HEAD_DIM
D_GDN = N_GDN_HEADS * GDN_HEAD_DIM
CHUNK = 64
CONV_K = 4
D_FF = 4 * D_MODEL
EPS = 1e-6
D_PROJ = 3600
N_DEV = 8

PROJ_W = 3712
COL_FOX, COL_GDN, COL_GZ, COL_SMALL = 0, 1536, 3072, 3584
LANES = 128
QKV = 3 * LANES
SM_FF, SM_GB, SM_GA = 0, 8, 12

ADAM_LR, ADAM_B1, ADAM_B2, ADAM_EPS, ADAM_WD, ADAM_STEP = 0.001, 0.9, 0.999, 1e-08, 0.01, 10

TOKEN_BLOCK = 256
MATMUL_BLOCK = 512
TRI_ROWS = 4
FOX_SCALE = FOX_HEAD_DIM ** -0.5
GDN_QSCALE = GDN_HEAD_DIM ** -0.5
NEG_BIG = -1e30
VMEM_LIMIT = 56 * 1024 * 1024

VMEM_SPEC = pl.BlockSpec(memory_space=pltpu.VMEM)
ANY_SPEC = pl.BlockSpec(memory_space=pl.ANY)


def _sds(shape, dtype=F32):
    return jax.ShapeDtypeStruct(shape, dtype)


def _params(*sem):
    return pltpu.CompilerParams(dimension_semantics=sem if sem else None, vmem_limit_bytes=VMEM_LIMIT)


def _ordered(body):
    def ordered(_, *refs):
        body(*refs)

    return ordered


def _mm(a, b):
    return jnp.dot(a.astype(BF), b.astype(BF), preferred_element_type=F32)


def _mm_nt(a, b):
    return lax.dot_general(a.astype(BF), b.astype(BF), (((1,), (1,)), ((), ())), preferred_element_type=F32)


def _mm_tn(a, b):
    return lax.dot_general(a.astype(BF), b.astype(BF), (((0,), (0,)), ((), ())), preferred_element_type=F32)


def _sigmoid(x):
    return 1.0 / (1.0 + jnp.exp(-x))


def _softplus(x):
    return jnp.maximum(x, 0.0) + jnp.log1p(jnp.exp(-jnp.abs(x)))


def _iota(shape, dim):
    return lax.broadcasted_iota(jnp.int32, shape, dim)


def _shift_down(x, s, row):
    return jnp.where(row >= s, pltpu.roll(x, s, 0), 0.0)


def _shift_up(x, s, row):
    n = x.shape[0]
    return jnp.where(row < n - s, pltpu.roll(x, n - s, 0), 0.0)


def _norm_proj(x, nw, wt_al, after):
    t = x.shape[0]

    def body(x_ref, nw_ref, w_ref, proj_ref, h_ref):
        xv = x_ref[...]
        r = lax.rsqrt(jnp.mean(xv * xv, axis=-1, keepdims=True) + EPS)
        h = (xv * r * nw_ref[...]).astype(BF)
        h_ref[...] = h
        proj_ref[...] = lax.dot_general(h, w_ref[...], (((1,), (1,)), ((), ())), preferred_element_type=F32)

    tm = min(MATMUL_BLOCK, t)
    return pl.pallas_call(
        _ordered(body), name="norm_proj", grid=(t // tm,),
        in_specs=[ANY_SPEC, pl.BlockSpec((tm, D_MODEL), lambda i: (i, 0)), pl.BlockSpec((1, D_MODEL), lambda i: (0, 0)),
                  pl.BlockSpec((PROJ_W, D_MODEL), lambda i: (0, 0))],
        out_specs=[pl.BlockSpec((tm, PROJ_W), lambda i: (i, 0)), pl.BlockSpec((tm, D_MODEL), lambda i: (i, 0))],
        out_shape=[_sds((t, PROJ_W)), _sds((t, D_MODEL), BF)],
        compiler_params=_params("parallel"),
    )(after, x, nw, wt_al)


def _lane_column(x, lane):
    return jnp.sum(jnp.where(_iota((1, LANES), 1) == lane, x, 0.0), axis=-1, keepdims=True)


def _small_prep(proj, fb, al, dtb):
    t = proj.shape[0]

    def body(sm_ref, fb_ref, al_ref, dtb_ref, cumt_ref, beta_ref, g_ref):
        s = sm_ref[...]
        z = s + fb_ref[...]
        cum = jnp.minimum(z, 0.0) - jnp.log1p(jnp.exp(-jnp.abs(z)))
        row = _iota((t, LANES), 0)
        step = 1
        while step < t:
            cum = cum + _shift_down(cum, step, row)
            step *= 2
        cumt_ref[...] = cum.T
        beta_ref[...] = _sigmoid(s)
        g_ref[...] = -jnp.exp(al_ref[...]) * _softplus(s + dtb_ref[...])

    vec = pl.BlockSpec((1, LANES), lambda i: (0, 0))
    tok = pl.BlockSpec((t, LANES), lambda i: (0, 0))
    return pl.pallas_call(
        body, name="small_prep", grid=(1,),
        in_specs=[pl.BlockSpec((t, LANES), lambda i: (0, COL_SMALL // LANES)), vec, vec, vec],
        out_specs=[pl.BlockSpec((LANES, t), lambda i: (0, 0)), tok, tok],
        out_shape=[_sds((LANES, t)), _sds((t, LANES)), _sds((t, LANES))],
        compiler_params=_params("arbitrary"),
    )(proj, fb, al, dtb)


def _fox_stack(x, first):
    return jnp.concatenate([jnp.where(first, x, 0.0), jnp.where(first, 0.0, x)], axis=0).astype(BF)


def _fox_unstack(y, first):
    n = y.shape[0] // 2
    return jnp.where(first, y[:n], y[n:])


def _fox_logits(q2_i, kb, cumt_ref, pair, i, tq):
    klen = (i + 1) * tq
    s = lax.dot_general(q2_i, kb[:klen], (((1,), (1,)), ((), ())), preferred_element_type=F32)
    upper = _iota((2 * tq, 1), 0) < tq
    s = s - jnp.where(upper, cumt_ref[pl.ds(2 * pair, 1), 0:klen], cumt_ref[pl.ds(2 * pair + 1, 1), 0:klen])
    causal = _iota((2 * tq, tq), 1) <= _iota((2 * tq, tq), 0) % tq
    parts = [(s[:, :klen - tq], 0, klen - tq)] if i else []
    return parts + [(jnp.where(causal, s[:, klen - tq:], NEG_BIG), klen - tq, klen)]


def _fox_fwd(proj, cumt, fnw, after):
    t = proj.shape[0]
    tq = min(TOKEN_BLOCK, t // 2)
    nq = t // tq

    def body(q_ref, k_ref, v_ref, cumt_ref, fnw_ref, o_ref, lse_ref, fn_ref):
        j = pl.program_id(0)
        first = _iota((1, LANES), 1) < FOX_HEAD_DIM
        kb = k_ref[...].astype(BF)
        vb = v_ref[...].astype(BF)
        for i in range(nq):
            rows = slice(i * tq, (i + 1) * tq)
            q2 = _fox_stack(q_ref[rows, :] * FOX_SCALE, first)
            parts = _fox_logits(q2, kb, cumt_ref, j, i, tq)
            m = jnp.max(parts[-1][0], axis=-1, keepdims=True)
            if i:
                m = jnp.maximum(m, jnp.max(parts[0][0], axis=-1, keepdims=True))
            l = jnp.zeros((2 * tq, 1), F32)
            o = jnp.zeros((2 * tq, LANES), F32)
            for s, lo, hi in parts:
                p = jnp.exp(s - m)
                l = l + jnp.sum(p, axis=-1, keepdims=True)
                o = o + jnp.dot(p.astype(BF), vb[lo:hi], preferred_element_type=F32)
            o_acc = _fox_unstack(o / l, first)
            lse_acc = _fox_unstack(jnp.broadcast_to(m + jnp.log(l), (2 * tq, LANES)), first)
            o_ref[rows, :] = o_acc
            lse_ref[rows, :] = lse_acc
            o2 = o_acc * o_acc
            s0 = jnp.sum(jnp.where(first, o2, 0.0), axis=-1, keepdims=True)
            s1 = jnp.sum(jnp.where(first, 0.0, o2), axis=-1, keepdims=True)
            r = lax.rsqrt(jnp.where(first, s0, s1) * (1.0 / FOX_HEAD_DIM) + EPS)
            fn_ref[rows, :] = (o_acc * r * fnw_ref[...]).astype(BF)

    qkv = lambda k: pl.BlockSpec((t, LANES), lambda j: (0, COL_FOX // LANES + 3 * j + k))
    pair = pl.BlockSpec((t, LANES), lambda j: (0, j))
    return pl.pallas_call(
        _ordered(body), name="fox_fwd", grid=(N_FOX_HEADS // 2,),
        in_specs=[ANY_SPEC, qkv(0), qkv(1), qkv(2), pl.BlockSpec((LANES, t), lambda j: (0, 0)),
                  pl.BlockSpec((1, LANES), lambda j: (0, 0))],
        out_specs=[pair, pair, pair],
        out_shape=[_sds((t, D_FOX)), _sds((t, D_FOX)), _sds((t, D_FOX), BF)],
        compiler_params=_params("parallel"),
    )(after, proj, proj, proj, cumt, fnw)


def _fox_bwd(proj, cumt, lse, o, do, dproj, after):
    t = proj.shape[0]
    tq = min(TOKEN_BLOCK, t // 2)
    nq = t // tq

    def body(q_ref, k_ref, v_ref, cumt_ref, lse_ref, o_ref, do_ref, _, dqkv_ref, dcq_ref, dckt_ref, dk_s, dv_s):
        j = pl.program_id(0)

        @pl.when(j == 0)
        def _():
            dcq_ref[...] = jnp.zeros_like(dcq_ref)
            dckt_ref[...] = jnp.zeros_like(dckt_ref)

        lane = _iota((1, LANES), 1)

        first = _iota((1, LANES), 1) < FOX_HEAD_DIM
        kb = k_ref[...].astype(BF)
        vb = v_ref[...].astype(BF)
        dk_s[...] = jnp.zeros_like(dk_s)
        dv_s[...] = jnp.zeros_like(dv_s)
        for i in range(nq):
            rows = slice(i * tq, (i + 1) * tq)
            do_i = do_ref[rows, :]
            prod = do_i * o_ref[rows, :]
            lse_i = lse_ref[rows, :]
            q2 = _fox_stack(q_ref[rows, :] * FOX_SCALE, first)
            do2 = _fox_stack(do_i, first)
            delta = jnp.concatenate([jnp.sum(jnp.where(first, prod, 0.0), axis=-1, keepdims=True),
                                     jnp.sum(jnp.where(first, 0.0, prod), axis=-1, keepdims=True)], axis=0)
            lse2 = jnp.concatenate([lse_i[:, 0:1], lse_i[:, FOX_HEAD_DIM:FOX_HEAD_DIM + 1]], axis=0)
            dq2 = jnp.zeros((2 * tq, LANES), F32)
            dcq2 = jnp.zeros((2 * tq, 1), F32)
            for s, lo, hi in _fox_logits(q2, kb, cumt_ref, j, i, tq):
                p = jnp.exp(s - lse2)
                ds = p * (_mm_nt(do2, vb[lo:hi]) - delta)
                dsb = ds.astype(BF)
                dq2 = dq2 + jnp.dot(dsb, kb[lo:hi], preferred_element_type=F32)
                dk_s[lo:hi, :] += _mm_tn(dsb, q2)
                dv_s[lo:hi, :] += _mm_tn(p, do2)
                dcq2 = dcq2 + jnp.sum(ds, axis=-1, keepdims=True)
                dckt_ref[pl.ds(2 * j, 1), lo:hi] += jnp.sum(ds[:tq], axis=0, keepdims=True)
                dckt_ref[pl.ds(2 * j + 1, 1), lo:hi] += jnp.sum(ds[tq:], axis=0, keepdims=True)
            dqkv_ref[rows, 0:LANES] = (_fox_unstack(dq2, first) * FOX_SCALE).astype(BF)
            dcq_ref[rows, :] += jnp.where(lane == 2 * j, dcq2[:tq], jnp.where(lane == 2 * j + 1, dcq2[tq:], 0.0))
        dqkv_ref[:, LANES:2 * LANES] = dk_s[...].astype(BF)
        dqkv_ref[:, 2 * LANES:QKV] = dv_s[...].astype(BF)

    qkv = lambda k: pl.BlockSpec((t, LANES), lambda j: (0, COL_FOX // LANES + 3 * j + k))
    pair = pl.BlockSpec((t, LANES), lambda j: (0, j))
    rows128 = pl.BlockSpec((LANES, t), lambda j: (0, 0))
    return pl.pallas_call(
        _ordered(body), name="fox_bwd", grid=(N_FOX_HEADS // 2,),
        in_specs=[ANY_SPEC, qkv(0), qkv(1), qkv(2), rows128, pair, pair, pair, ANY_SPEC],
        out_specs=[pl.BlockSpec((t, QKV), lambda j: (0, COL_FOX // QKV + j)),
                   pl.BlockSpec((t, LANES), lambda j: (0, 0)), rows128],
        out_shape=[_sds(dproj.shape, BF), _sds((t, LANES)), _sds((LANES, t))],
        scratch_shapes=[pltpu.VMEM((t, LANES), F32), pltpu.VMEM((t, LANES), F32)],
        input_output_aliases={8: 0}, compiler_params=_params("arbitrary"),
    )(after, proj, proj, proj, cumt, lse, o, do, dproj)


def _conv(x, w, row):
    return (w[3:4, :] * x + w[2:3, :] * _shift_down(x, 1, row) + w[1:2, :] * _shift_down(x, 2, row)
            + w[0:1, :] * _shift_down(x, 3, row))


def _chunk_decay(gc_c):
    gi = gc_c[:, 0:CHUNK]
    gj = gc_c.T[0:CHUNK, :]
    ri = _iota((CHUNK, CHUNK), 0)
    cj = _iota((CHUNK, CHUNK), 1)
    return jnp.where(ri >= cj, jnp.exp(jnp.minimum(gi - gj, 0.0)), 0.0), ri > cj


def _gdn_specs(t):
    col = lambda off: pl.BlockSpec((t, LANES), lambda h: (0, off + h))
    cw = lambda off: pl.BlockSpec((CONV_K, LANES), lambda h: (0, off + h))
    mat = pl.BlockSpec((1, t // CHUNK, CHUNK, CHUNK), lambda h: (h, 0, 0, 0))
    qkv = lambda k: pl.BlockSpec((t, LANES), lambda h: (0, COL_GDN // LANES + 3 * h + k))
    return col, cw, mat, qkv


def _gdn_prep(proj, convw, beta, g):
    t = proj.shape[0]
    nch = t // CHUNK

    def body(xq_ref, xk_ref, xv_ref, wq_ref, wk_ref, wv_ref, beta_ref, g_ref,
             qn_ref, kn_ref, cv_ref, gc_ref, be_ref, m_ref, a_ref):
        row = _iota((t, LANES), 0)
        hd = pl.program_id(0)
        be_ref[...] = jnp.broadcast_to(_lane_column(beta_ref[...], SM_GB + hd), (t, LANES))

        def act(x_ref, w_ref):
            y = _conv(x_ref[...], w_ref[...], row)
            return y * _sigmoid(y)

        cq = act(xq_ref, wq_ref)
        ck = act(xk_ref, wk_ref)
        cv_ref[...] = act(xv_ref, wv_ref)
        qn_ref[...] = cq * lax.rsqrt(jnp.sum(cq * cq, axis=-1, keepdims=True) + EPS) * GDN_QSCALE
        kn_ref[...] = ck * lax.rsqrt(jnp.sum(ck * ck, axis=-1, keepdims=True) + EPS)
        gc = jnp.broadcast_to(_lane_column(g_ref[...], SM_GA + hd), (t, LANES))
        pos = row % CHUNK
        step = 1
        while step < CHUNK:
            gc = gc + jnp.where(pos >= step, pltpu.roll(gc, step, 0), 0.0)
            step *= 2
        gc_ref[...] = gc

        group = 4 if nch % 4 == 0 else 1

        def chunks(gi, carry):
            ns = [gi * group + c for c in range(group)]
            sls = [pl.ds(pl.multiple_of(n * CHUNK, CHUNK), CHUNK) for n in ns]
            ks = [kn_ref[sl, :] for sl in sls]
            kk = [_mm_nt(k_c * be_ref[sl, :], k_c) for k_c, sl in zip(ks, sls)]
            qk = [_mm_nt(qn_ref[sl, :], k_c) for k_c, sl in zip(ks, sls)]
            for c, n in enumerate(ns):
                decay, strict = _chunk_decay(gc_ref[sls[c], :])
                m_ref[0, n] = jnp.where(strict, kk[c] * decay, 0.0)
                a_ref[0, n] = qk[c] * decay
            return carry

        lax.fori_loop(0, nch // group, chunks, 0)

    col, cw, mat, qkv = _gdn_specs(t)
    return pl.pallas_call(
        body, name="gdn_prep", grid=(N_GDN_HEADS,),
        in_specs=[qkv(0), qkv(1), qkv(2), cw(0), cw(4), cw(8)] + [pl.BlockSpec((t, LANES), lambda h: (0, 0))] * 2,
        out_specs=[col(0), col(0), col(0), col(0), col(0), mat, mat],
        out_shape=[_sds((t, D_GDN))] * 5 + [_sds((N_GDN_HEADS, nch, CHUNK, CHUNK))] * 2,
        compiler_params=_params("parallel"),
    )(proj, proj, proj, convw, convw, convw, beta, g)


def _tri_inverse(m3):
    assert m3.shape == (LANES, CHUNK, CHUNK)

    def body(m_ref, t_ref, ms, ts):
        for i in range(CHUNK):
            ms[i * CHUNK:(i + 1) * CHUNK, :] = m_ref[:, i, :].T
        cidx = _iota((CHUNK, LANES), 0)

        def t_row(j):
            return ts[pl.ds(pl.multiple_of(j * CHUNK, CHUNK), CHUNK), :]

        def outer(ib, carry):
            i0 = ib * TRI_ROWS

            def inner(group, accs):
                for jj in range(TRI_ROWS):
                    jj = group * TRI_ROWS + jj
                    earlier = t_row(jj)
                    accs = tuple(acc - ms[pl.ds((i0 + r) * CHUNK + jj, 1), :] * earlier for r, acc in enumerate(accs))
                return accs

            accs = list(lax.fori_loop(
                0, ib, inner, tuple(jnp.where(cidx == i0 + r, 1.0, 0.0).astype(F32) for r in range(TRI_ROWS))))
            for r in range(TRI_ROWS):
                for q in range(r):
                    accs[r] = accs[r] - ms[pl.ds((i0 + r) * CHUNK + i0 + q, 1), :] * accs[q]
                ts[pl.ds(pl.multiple_of((i0 + r) * CHUNK, CHUNK), CHUNK), :] = accs[r]
            return carry

        lax.fori_loop(0, CHUNK // TRI_ROWS, outer, 0)
        for i in range(CHUNK):
            t_ref[:, i, :] = ts[i * CHUNK:(i + 1) * CHUNK, :].T

    return pl.pallas_call(
        body, name="tri_inverse", in_specs=[VMEM_SPEC], out_specs=VMEM_SPEC,
        out_shape=_sds((LANES, CHUNK, CHUNK)),
        scratch_shapes=[pltpu.VMEM((CHUNK * CHUNK, LANES), F32), pltpu.VMEM((CHUNK * CHUNK, LANES), F32)],
        compiler_params=_params(),
    )(m3)


def _gdn_chunk_terms(q, k, v, b, gcc):
    eg = jnp.exp(gcc)
    last = gcc[CHUNK - 1:CHUNK, :]
    egl = jnp.exp(last - gcc)
    gl = jnp.exp(last)
    kb = k * b
    return eg, egl, gl, kb, v * b, kb * eg, q * eg, k * egl


GDN_BLOCK_CHUNKS = 4


def _gdn_block_specs(t, reverse):
    cb = GDN_BLOCK_CHUNKS
    nb = t // (cb * CHUNK)
    idx = (lambda i: nb - 1 - i) if reverse else (lambda i: i)
    tok = pl.BlockSpec((cb * CHUNK, D_GDN), lambda i: (idx(i), 0))
    mat = pl.BlockSpec((N_GDN_HEADS, cb, CHUNK, CHUNK), lambda i: (0, idx(i), 0, 0))
    state = pl.BlockSpec((N_GDN_HEADS, cb, GDN_HEAD_DIM, GDN_HEAD_DIM), lambda i: (0, idx(i), 0, 0))
    return nb, tok, mat, state


def _gdn_scan(qn, kn, cv, be, gc, tinv, amat):
    t = qn.shape[0]
    nch = t // CHUNK

    def body(q_ref, k_ref, v_ref, b_ref, gc_ref, t_ref, a_ref, o_ref, sall_ref, vn_ref, s_scr):
        @pl.when(pl.program_id(0) == 0)
        def _():
            s_scr[...] = jnp.zeros_like(s_scr)

        heads = range(N_GDN_HEADS)
        cols = [slice(hd * LANES, (hd + 1) * LANES) for hd in heads]
        s = [s_scr[hd] for hd in heads]
        for cc in range(GDN_BLOCK_CHUNKS):
            rs = slice(cc * CHUNK, (cc + 1) * CHUNK)
            terms = [_gdn_chunk_terms(q_ref[rs, cs], k_ref[rs, cs], v_ref[rs, cs], b_ref[rs, cs], gc_ref[rs, cs])
                     for cs in cols]
            for hd in heads:
                sall_ref[hd, cc] = s[hd]
            uw = [_mm(t_ref[hd, cc], jnp.concatenate([terms[hd][4], terms[hd][5]], axis=1)) for hd in heads]
            ws_qs = [_mm(jnp.concatenate([uw[hd][:, LANES:], terms[hd][6]], axis=0), s[hd]) for hd in heads]
            vn = [uw[hd][:, :LANES] - ws_qs[hd][:CHUNK] for hd in heads]
            a_vn = [_mm(a_ref[hd, cc], vn[hd]) for hd in heads]
            kd_vn = [_mm_tn(terms[hd][7], vn[hd]) for hd in heads]
            for hd in heads:
                vn_ref[rs, cols[hd]] = vn[hd]
                o_ref[rs, cols[hd]] = ws_qs[hd][CHUNK:] + a_vn[hd]
                s[hd] = s[hd] * terms[hd][2] + kd_vn[hd]
        for hd in heads:
            s_scr[hd] = s[hd]

    nb, tok, mat, state = _gdn_block_specs(t, False)
    return pl.pallas_call(
        body, name="gdn_scan", grid=(nb,),
        in_specs=[tok] * 5 + [mat, mat], out_specs=[tok, state, tok],
        out_shape=[_sds((t, D_GDN)), _sds((N_GDN_HEADS, nch, GDN_HEAD_DIM, GDN_HEAD_DIM)), _sds((t, D_GDN))],
        scratch_shapes=[pltpu.VMEM((N_GDN_HEADS, GDN_HEAD_DIM, GDN_HEAD_DIM), F32)],
        compiler_params=_params("arbitrary"),
    )(qn, kn, cv, be, gc, tinv, amat)


def _gdn_bwd(qn, kn, cv, be, gc, tinv, amat, s_all, vn_all, do, after):
    t = qn.shape[0]

    def body(q_ref, k_ref, v_ref, b_ref, gc_ref, t_ref, a_ref, sall_ref, vn_ref, do_ref,
             dq_ref, dk_ref, dv_ref, db_ref, dg_ref, ds_scr):
        @pl.when(pl.program_id(0) == 0)
        def _():
            ds_scr[...] = jnp.zeros_like(ds_scr)

        lastrow = _iota((CHUNK, LANES), 0) == CHUNK - 1
        heads = range(N_GDN_HEADS)
        cols = [slice(hd * LANES, (hd + 1) * LANES) for hd in heads]
        each = lambda fn: [fn(hd) for hd in heads]
        rows_cat = lambda x, y: jnp.concatenate([x, y], axis=0)
        lane_cat = lambda x, y: jnp.concatenate([x, y], axis=1)
        dsp = each(lambda hd: ds_scr[hd])
        for cc in reversed(range(GDN_BLOCK_CHUNKS)):
            rs = slice(cc * CHUNK, (cc + 1) * CHUNK)
            q = each(lambda hd: q_ref[rs, cols[hd]])
            k = each(lambda hd: k_ref[rs, cols[hd]])
            v = each(lambda hd: v_ref[rs, cols[hd]])
            b = each(lambda hd: b_ref[rs, cols[hd]])
            gcc = each(lambda hd: gc_ref[rs, cols[hd]])
            do_c = each(lambda hd: do_ref[rs, cols[hd]])
            vn = each(lambda hd: vn_ref[rs, cols[hd]])
            tn = each(lambda hd: t_ref[hd, cc])
            st = each(lambda hd: sall_ref[hd, cc])
            terms = each(lambda hd: _gdn_chunk_terms(q[hd], k[hd], v[hd], b[hd], gcc[hd]))
            eg, egl, gl, kb, vb, kbg, qd, kd = [[terms[hd][i] for hd in heads] for i in range(8)]
            w = each(lambda hd: _mm(tn[hd], kbg[hd]))
            a_do = each(lambda hd: _mm_tn(a_ref[hd, cc], do_c[hd]))
            kd_ds = each(lambda hd: _mm(kd[hd], dsp[hd]))
            da = each(lambda hd: _mm_nt(do_c[hd], vn[hd]))
            dkd = each(lambda hd: _mm_nt(vn[hd], dsp[hd]))
            by_k = each(lambda hd: _mm_nt(rows_cat(kb[hd], q[hd]), k[hd]))
            dgl = each(lambda hd: jnp.sum(jnp.sum(dsp[hd] * st[hd], axis=-1, keepdims=True), axis=0, keepdims=True))
            dvn = each(lambda hd: a_do[hd] + kd_ds[hd])
            do_dvn = each(lambda hd: rows_cat(do_c[hd], dvn[hd]))
            by_s = each(lambda hd: _mm_nt(do_dvn[hd], st[hd]))
            dqd = each(lambda hd: by_s[hd][:CHUNK])
            dvn_dw = each(lambda hd: lane_cat(dvn[hd], -by_s[hd][CHUNK:]))
            dsp = each(lambda hd: _mm_tn(rows_cat(qd[hd], -w[hd]), do_dvn[hd]) + gl[hd] * dsp[hd])
            dt = each(lambda hd: _mm_nt(dvn_dw[hd], lane_cat(vb[hd], kbg[hd])))
            by_t = each(lambda hd: _mm_tn(tn[hd], dvn_dw[hd]))
            tt_dt = each(lambda hd: _mm_tn(tn[hd], dt[hd]))
            dm_raw = each(lambda hd: _mm_nt(tt_dt[hd], tn[hd]))
            masks = each(lambda hd: _chunk_decay(gcc[hd]))
            dkk = each(lambda hd: jnp.where(masks[hd][1], -dm_raw[hd], 0.0) * masks[hd][0])
            dqk = each(lambda hd: da[hd] * masks[hd][0])
            dqk_dkk = each(lambda hd: rows_cat(dqk[hd], dkk[hd]))
            on_k = each(lambda hd: _mm(dqk_dkk[hd], k[hd]))
            dk_mm = each(lambda hd: _mm_tn(dqk_dkk[hd], rows_cat(q[hd], kb[hd])))
            for hd in heads:
                cs = cols[hd]
                dvb, dkbg = by_t[hd][:, :LANES], by_t[hd][:, LANES:]
                gmat = dkk[hd] * by_k[hd][:CHUNK] + dqk[hd] * by_k[hd][CHUNK:]
                dq_ref[rs, cs] = dqd[hd] * eg[hd] + on_k[hd][:CHUNK]
                dkb = on_k[hd][CHUNK:] + dkbg * eg[hd]
                dk_ref[rs, cs] = dkd[hd] * egl[hd] + dk_mm[hd] + dkb * b[hd]
                db = jnp.sum(dkb * k[hd], axis=-1, keepdims=True) + jnp.sum(dvb * v[hd], axis=-1, keepdims=True)
                db_ref[rs, cs] = jnp.broadcast_to(db, (CHUNK, LANES))
                dv_ref[rs, cs] = dvb * b[hd]
                dkd_kd = jnp.sum(dkd[hd] * kd[hd], axis=-1, keepdims=True)
                col_sums = jnp.sum(lane_cat(gmat, jnp.zeros_like(gmat)).T, axis=-1, keepdims=True)
                dgc = (jnp.sum(gmat, axis=-1, keepdims=True) - col_sums[:CHUNK]
                       + jnp.sum(dqd[hd] * qd[hd], axis=-1, keepdims=True)
                       + jnp.sum(dkbg * kbg[hd], axis=-1, keepdims=True) - dkd_kd)
                extra = jnp.sum(dkd_kd, axis=0, keepdims=True) + dgl[hd] * gl[hd]
                dg_ref[rs, cs] = dgc + jnp.where(lastrow, extra, 0.0)
        for hd in heads:
            ds_scr[hd] = dsp[hd]
        dg = dg_ref[...]
        row = _iota(dg.shape, 0)
        pos = row % CHUNK
        step = 1
        while step < CHUNK:
            dg = dg + jnp.where(pos < CHUNK - step, pltpu.roll(dg, dg.shape[0] - step, 0), 0.0)
            step *= 2
        dg_ref[...] = dg

    nb, tok, mat, state = _gdn_block_specs(t, True)
    return pl.pallas_call(
        _ordered(body), name="gdn_bwd", grid=(nb,),
        in_specs=[ANY_SPEC] + [tok] * 5 + [mat, mat, state, tok, tok], out_specs=[tok] * 5,
        out_shape=[_sds((t, D_GDN))] * 5,
        scratch_shapes=[pltpu.VMEM((N_GDN_HEADS, GDN_HEAD_DIM, GDN_HEAD_DIM), F32)],
        compiler_params=_params("arbitrary"),
    )(after, qn, kn, cv, be, gc, tinv, amat, s_all, vn_all, do)


def _gdn_bwd_conv(proj, convw, dqn, dkn, dcv, dproj):
    t = proj.shape[0]

    def body(xq_ref, xk_ref, xv_ref, wq_ref, wk_ref, wv_ref, dq_ref, dk_ref, dv_ref, _,
             dqkv_ref, dwq_ref, dwk_ref, dwv_ref):
        row = _iota((t, LANES), 0)

        def one(x_ref, w_ref, d_ref, k, dw_ref, scale):
            x = x_ref[...]
            w = w_ref[...]
            y = _conv(x, w, row)
            sg = _sigmoid(y)
            dc = d_ref[...]
            if scale is not None:
                c = y * sg
                r = lax.rsqrt(jnp.sum(c * c, axis=-1, keepdims=True) + EPS)
                ch = c * r
                dc = scale * r * (dc - ch * jnp.sum(dc * ch, axis=-1, keepdims=True))
            dy = dc * sg * (1.0 + y * (1.0 - sg))
            dqkv_ref[:, k * LANES:(k + 1) * LANES] = (
                w[3:4, :] * dy + w[2:3, :] * _shift_up(dy, 1, row) + w[1:2, :] * _shift_up(dy, 2, row)
                + w[0:1, :] * _shift_up(dy, 3, row)).astype(BF)
            for jj in range(CONV_K):
                xs = x if jj == CONV_K - 1 else _shift_down(x, CONV_K - 1 - jj, row)
                dw_ref[jj:jj + 1, :] = jnp.sum(dy * xs, axis=0, keepdims=True)

        one(xq_ref, wq_ref, dq_ref, 0, dwq_ref, GDN_QSCALE)
        one(xk_ref, wk_ref, dk_ref, 1, dwk_ref, 1.0)
        one(xv_ref, wv_ref, dv_ref, 2, dwv_ref, None)

    col, cw, _, qkv = _gdn_specs(t)
    return pl.pallas_call(
        body, name="gdn_bwd_conv", grid=(N_GDN_HEADS,),
        in_specs=[qkv(0), qkv(1), qkv(2), cw(0), cw(4), cw(8), col(0), col(0), col(0), ANY_SPEC],
        out_specs=[pl.BlockSpec((t, QKV), lambda h: (0, COL_GDN // QKV + h)), cw(0), cw(0), cw(0)],
        out_shape=[_sds(dproj.shape, BF)] + [_sds((CONV_K, D_GDN))] * 3,
        input_output_aliases={9: 0}, compiler_params=_params("parallel"),
    )(proj, proj, proj, convw, convw, convw, dqn, dkn, dcv, dproj)


def _mix_out(fox_n, gdn_o, proj, gnw, w_out, x, pmw, plw, after):
    t = x.shape[0]
    tm = min(MATMUL_BLOCK, t)

    def body(fn_ref, go_ref, gz_ref, gnw_ref, w_ref, x_ref, pmw_ref, plw_ref, x1_ref, h2_ref, mixed_ref, omix_ref,
             h2t_ref):
        omix_ref[:, 0:D_FOX] = fn_ref[...]
        for hd in range(N_GDN_HEADS):
            cs = slice(hd * LANES, (hd + 1) * LANES)
            go = go_ref[:, cs]
            r = lax.rsqrt(jnp.mean(go * go, axis=-1, keepdims=True) + EPS)
            gz = gz_ref[:, cs]
            omix_ref[:, D_FOX + hd * LANES:D_FOX + (hd + 1) * LANES] = (
                go * r * gnw_ref[...] * (gz * _sigmoid(gz))).astype(BF)
        mixed = jnp.dot(omix_ref[...], w_ref[...], preferred_element_type=F32)
        mixed_ref[...] = mixed
        r2 = lax.rsqrt(jnp.mean(mixed * mixed, axis=-1, keepdims=True) + EPS)
        x1 = x_ref[...] + mixed * r2 * pmw_ref[...]
        x1_ref[...] = x1
        r3 = lax.rsqrt(jnp.mean(x1 * x1, axis=-1, keepdims=True) + EPS)
        h2 = x1 * r3 * plw_ref[...]
        h2_ref[...] = h2.astype(BF)
        h2t_ref[...] = h2.T.astype(BF)

    tok = lambda w: pl.BlockSpec((tm, w), lambda i: (i, 0))
    vec = lambda w: pl.BlockSpec((1, w), lambda i: (0, 0))
    return pl.pallas_call(
        _ordered(body), name="mix_out", grid=(t // tm,),
        in_specs=[ANY_SPEC, tok(D_FOX), tok(D_GDN), pl.BlockSpec((tm, D_GDN), lambda i: (i, COL_GZ // D_GDN)), vec(LANES),
                  pl.BlockSpec((D_MODEL, D_MODEL), lambda i: (0, 0)), tok(D_MODEL), vec(D_MODEL), vec(D_MODEL)],
        out_specs=[tok(D_MODEL)] * 4 + [pl.BlockSpec((D_MODEL, tm), lambda i: (0, i))],
        out_shape=[_sds((t, D_MODEL)), _sds((t, D_MODEL), BF), _sds((t, D_MODEL)), _sds((t, D_MODEL), BF),
                   _sds((D_MODEL, t), BF)],
        compiler_params=_params("parallel"),
    )(after, fox_n, gdn_o, proj, gnw, w_out, x, pmw, plw)


def _out_bwd(dmixed, w_out, o_fox, gdn_o, proj, fnw, gnw, after):
    t = dmixed.shape[0]
    tm = min(MATMUL_BLOCK, t)

    def body(dm_ref, w_ref, of_ref, go_ref, gz_ref, fnw_ref, gnw_ref, dof_ref, dgo_ref, dgz_ref, dfw_ref, dgw_ref):
        i = pl.program_id(0)

        @pl.when(i == 0)
        def _():
            dfw_ref[...] = jnp.zeros_like(dfw_ref)
            dgw_ref[...] = jnp.zeros_like(dgw_ref)

        domix = _mm_nt(dm_ref[...], w_ref[...])
        first = _iota((1, LANES), 1) < FOX_HEAD_DIM
        dfw = jnp.zeros((1, LANES), F32)
        dgw = jnp.zeros((1, LANES), F32)
        for pr in range(N_FOX_HEADS // 2):
            cs = slice(pr * LANES, (pr + 1) * LANES)
            o = of_ref[:, cs]
            dfn = domix[:, cs]
            o2 = o * o
            s0 = jnp.sum(jnp.where(first, o2, 0.0), axis=-1, keepdims=True)
            s1 = jnp.sum(jnp.where(first, 0.0, o2), axis=-1, keepdims=True)
            r = lax.rsqrt(jnp.where(first, s0, s1) * (1.0 / FOX_HEAD_DIM) + EPS)
            oh = o * r
            dfw = dfw + jnp.sum(dfn * oh, axis=0, keepdims=True)
            doh = dfn * fnw_ref[...]
            pr_ = doh * oh
            m0 = jnp.sum(jnp.where(first, pr_, 0.0), axis=-1, keepdims=True)
            m1 = jnp.sum(jnp.where(first, 0.0, pr_), axis=-1, keepdims=True)
            dof_ref[:, cs] = r * (doh - oh * jnp.where(first, m0, m1) * (1.0 / FOX_HEAD_DIM))
        for hd in range(N_GDN_HEADS):
            cs = slice(hd * LANES, (hd + 1) * LANES)
            go = go_ref[:, cs]
            gz = gz_ref[:, cs]
            dgated = domix[:, D_FOX + hd * LANES:D_FOX + (hd + 1) * LANES]
            r = lax.rsqrt(jnp.mean(go * go, axis=-1, keepdims=True) + EPS)
            goh = go * r
            sg = _sigmoid(gz)
            sz = gz * sg
            gn = goh * gnw_ref[...]
            dgn = dgated * sz
            dgz_ref[:, cs] = (dgated * gn * sg * (1.0 + gz * (1.0 - sg))).astype(BF)
            dgw = dgw + jnp.sum(dgn * goh, axis=0, keepdims=True)
            dgh = dgn * gnw_ref[...]
            dgo_ref[:, cs] = r * (dgh - goh * jnp.mean(dgh * goh, axis=-1, keepdims=True))
        dfw_ref[...] += dfw + pltpu.roll(dfw, FOX_HEAD_DIM, 1)
        dgw_ref[...] += dgw

    tok = lambda w: pl.BlockSpec((tm, w), lambda i: (i, 0))
    vec = lambda w: pl.BlockSpec((1, w), lambda i: (0, 0))
    return pl.pallas_call(
        _ordered(body), name="out_bwd", grid=(t // tm,),
        in_specs=[ANY_SPEC, tok(D_MODEL), pl.BlockSpec((D_MODEL, D_MODEL), lambda i: (0, 0)), tok(D_FOX), tok(D_GDN),
                  pl.BlockSpec((tm, D_GDN), lambda i: (i, COL_GZ // D_GDN)), vec(LANES), vec(LANES)],
        out_specs=[tok(D_FOX), tok(D_GDN), pl.BlockSpec((tm, D_GDN), lambda i: (i, COL_GZ // D_GDN)), vec(LANES),
                   vec(LANES)],
        out_shape=[_sds((t, D_FOX)), _sds((t, D_GDN)), _sds((t, PROJ_W), BF), _sds((1, LANES)), _sds((1, LANES))],
        compiler_params=_params("arbitrary"),
    )(after, dmixed, w_out, o_fox, gdn_o, proj, fnw, gnw)


def _mlp_up(h2, w_upt):
    t = h2.shape[0]
    tm = min(MATMUL_BLOCK, t)

    def body(h_ref, w_ref, up_ref):
        up_ref[...] = lax.dot_general(h_ref[...], w_ref[...], (((1,), (1,)), ((), ())),
                                      preferred_element_type=F32).astype(BF)

    return pl.pallas_call(
        body, name="mlp_up", grid=(t // tm,),
        in_specs=[pl.BlockSpec((tm, D_MODEL), lambda i: (i, 0)), pl.BlockSpec((D_FF, D_MODEL), lambda i: (0, 0))],
        out_specs=pl.BlockSpec((tm, D_FF), lambda i: (i, 0)), out_shape=_sds((t, D_FF), BF),
        compiler_params=_params("parallel"),
    )(h2, w_upt)


def _mlp_down_loss(up, w_down, x1, pw, target):
    t = up.shape[0]
    tm = min(MATMUL_BLOCK, t)

    def body(up_ref, w_ref, x1_ref, pw_ref, tg_ref, dy_ref, dx2_ref, loss_ref, dpw_ref):
        i = pl.program_id(0)

        @pl.when(i == 0)
        def _():
            loss_ref[...] = jnp.zeros_like(loss_ref)
            dpw_ref[...] = jnp.zeros_like(dpw_ref)

        u = jnp.maximum(up_ref[...].astype(F32), 0.0)
        y = jnp.dot((u * u).astype(BF), w_ref[...], preferred_element_type=F32)
        r = lax.rsqrt(jnp.mean(y * y, axis=-1, keepdims=True) + EPS)
        yh = y * r
        pw = pw_ref[...]
        err = x1_ref[...] + yh * pw - tg_ref[...]
        part = jnp.sum(jnp.sum(err * err, axis=-1, keepdims=True), axis=0, keepdims=True) * (0.5 / D_MODEL)
        loss_ref[...] += jnp.broadcast_to(part, loss_ref.shape)
        dx2 = err * (1.0 / D_MODEL)
        dx2_ref[...] = dx2
        dpw_ref[...] += jnp.sum(dx2 * yh, axis=0, keepdims=True)
        dyh = dx2 * pw
        dy_ref[...] = (r * (dyh - yh * jnp.mean(dyh * yh, axis=-1, keepdims=True))).astype(BF)

    tok = lambda w: pl.BlockSpec((tm, w), lambda i: (i, 0))
    vec = lambda w: pl.BlockSpec((1, w), lambda i: (0, 0))
    return pl.pallas_call(
        body, name="mlp_down_loss", grid=(t // tm,),
        in_specs=[tok(D_FF), pl.BlockSpec((D_FF, D_MODEL), lambda i: (0, 0)), tok(D_MODEL), vec(D_MODEL), tok(D_MODEL)],
        out_specs=[tok(D_MODEL), tok(D_MODEL), vec(LANES), vec(D_MODEL)],
        out_shape=[_sds((t, D_MODEL), BF), _sds((t, D_MODEL)), _sds((1, LANES)), _sds((1, D_MODEL))],
        compiler_params=_params("arbitrary"),
    )(up, w_down, x1, pw, target)


def _mlp_bwd_act(dy, w_down, up):
    t = dy.shape[0]
    tm = min(MATMUL_BLOCK, t)

    def body(dy_ref, w_ref, up_ref, dup_ref):
        da = lax.dot_general(dy_ref[...], w_ref[...], (((1,), (1,)), ((), ())), preferred_element_type=F32)
        dup_ref[...] = (da * (2.0 * jnp.maximum(up_ref[...].astype(F32), 0.0))).astype(BF)

    return pl.pallas_call(
        body, name="mlp_bwd_act", grid=(t // tm,),
        in_specs=[pl.BlockSpec((tm, D_MODEL), lambda i: (i, 0)), pl.BlockSpec((D_FF, D_MODEL), lambda i: (0, 0)),
                  pl.BlockSpec((tm, D_FF), lambda i: (i, 0))],
        out_specs=pl.BlockSpec((tm, D_FF), lambda i: (i, 0)), out_shape=_sds((t, D_FF), BF),
        compiler_params=_params("parallel"),
    )(dy, w_down, up)


def _mlp_bwd_in(dup, w_up, x1, plw, dx2, mixed, pmw, after):
    t = dup.shape[0]
    tm = min(MATMUL_BLOCK, t)

    def body(dup_ref, w_ref, x1_ref, plw_ref, dx2_ref, mx_ref, pmw_ref, dx1_ref, dmixed_ref, dplw_ref, dpmw_ref):
        i = pl.program_id(0)

        @pl.when(i == 0)
        def _():
            dplw_ref[...] = jnp.zeros_like(dplw_ref)
            dpmw_ref[...] = jnp.zeros_like(dpmw_ref)

        dh = jnp.dot(dup_ref[...], w_ref[...], preferred_element_type=F32)
        x1 = x1_ref[...]
        r = lax.rsqrt(jnp.mean(x1 * x1, axis=-1, keepdims=True) + EPS)
        xh = x1 * r
        dplw_ref[...] += jnp.sum(dh * xh, axis=0, keepdims=True)
        dxh = dh * plw_ref[...]
        dx1 = dx2_ref[...] + r * (dxh - xh * jnp.mean(dxh * xh, axis=-1, keepdims=True))
        dx1_ref[...] = dx1
        mx = mx_ref[...]
        r2 = lax.rsqrt(jnp.mean(mx * mx, axis=-1, keepdims=True) + EPS)
        mh = mx * r2
        dpmw_ref[...] += jnp.sum(dx1 * mh, axis=0, keepdims=True)
        dmh = dx1 * pmw_ref[...]
        dmixed_ref[...] = (r2 * (dmh - mh * jnp.mean(dmh * mh, axis=-1, keepdims=True))).astype(BF)

    tok = lambda w: pl.BlockSpec((tm, w), lambda i: (i, 0))
    vec = lambda w: pl.BlockSpec((1, w), lambda i: (0, 0))
    return pl.pallas_call(
        _ordered(body), name="mlp_bwd_in", grid=(t // tm,),
        in_specs=[ANY_SPEC, tok(D_FF), pl.BlockSpec((D_FF, D_MODEL), lambda i: (0, 0)), tok(D_MODEL),
                  vec(D_MODEL), tok(D_MODEL), tok(D_MODEL), vec(D_MODEL)],
        out_specs=[tok(D_MODEL), tok(D_MODEL), vec(D_MODEL), vec(D_MODEL)],
        out_shape=[_sds((t, D_MODEL)), _sds((t, D_MODEL), BF), _sds((1, D_MODEL)), _sds((1, D_MODEL))],
        compiler_params=_params("arbitrary"),
    )(after, dup, w_up, x1, plw, dx2, mixed, pmw)


def _wgrad(a, b, a_cols, split=1, a_fn=None, a_block0=0, name="wgrad"):
    t, b_cols = b.shape
    n_a = (a.shape[1] - a_block0 * a_cols) // a_cols if a_block0 else a.shape[1] // a_cols

    def body(a_ref, b_ref, o_ref):
        av = a_ref[...]
        if a_fn is not None:
            av = a_fn(av)
        o_ref[...] = _mm_tn(av, b_ref[...]).astype(BF).reshape(o_ref.shape)

    return pl.pallas_call(
        body, name=name, grid=(n_a,),
        in_specs=[pl.BlockSpec((t, a_cols), lambda i: (0, i + a_block0)), pl.BlockSpec((t, b_cols), lambda i: (0, 0))],
        out_specs=pl.BlockSpec((split, a_cols // split, b_cols), lambda i: (i, 0, 0)),
        out_shape=_sds((n_a * split, a_cols // split, b_cols), BF),
        compiler_params=_params("parallel"),
    )(a, b)


def _wgrad_pre_t(at, b, b_cols, name):
    rows, t = at.shape
    n_b = b.shape[1] // b_cols

    def body(a_ref, b_ref, o_ref):
        o_ref[0] = jnp.dot(a_ref[...], b_ref[...], preferred_element_type=F32).astype(BF)

    return pl.pallas_call(
        body, name=name, grid=(n_b,),
        in_specs=[pl.BlockSpec((rows, t), lambda j: (0, 0)), pl.BlockSpec((t, b_cols), lambda j: (0, j))],
        out_specs=pl.BlockSpec((1, rows, b_cols), lambda j: (j, 0, 0)), out_shape=_sds((n_b, rows, b_cols), BF),
        compiler_params=_params("parallel"),
    )(at, b)


def _small_bwd(proj, fb, al, dtb, dcq, dckt, dbe, dge, dproj):
    t = proj.shape[0]

    def body(sm_ref, fb_ref, al_ref, dtb_ref, dcq_ref, dckt_ref, dbe_ref, dge_ref, _, dsm_ref, dvec_ref):
        s = sm_ref[...]
        lane = _iota((1, LANES), 1)
        dcum = dcq_ref[...] - dckt_ref[...].T
        row = _iota((t, LANES), 0)
        step = 1
        while step < t:
            dcum = dcum + _shift_up(dcum, step, row)
            step *= 2
        dff = dcum * _sigmoid(-(s + fb_ref[...]))
        dbeta = jnp.zeros((t, LANES), F32)
        dg = jnp.zeros((t, LANES), F32)
        for hd in range(N_GDN_HEADS):
            dbeta = jnp.where(lane == SM_GB + hd, dbe_ref[:, hd * LANES:hd * LANES + 1], dbeta)
            dg = jnp.where(lane == SM_GA + hd, dge_ref[:, hd * LANES:hd * LANES + 1], dg)
        beta = _sigmoid(s)
        dgb = dbeta * beta * (1.0 - beta)
        za = s + dtb_ref[...]
        nea = -jnp.exp(al_ref[...])
        dga = dg * nea * _sigmoid(za)
        is_f = lane < SM_GB
        is_b = (lane >= SM_GB) & (lane < SM_GA)
        is_a = (lane >= SM_GA) & (lane < SM_GA + 4)
        dsm_ref[...] = jnp.where(is_f, dff, jnp.where(is_b, dgb, jnp.where(is_a, dga, 0.0))).astype(BF)
        dvec_ref[...] = jnp.zeros_like(dvec_ref)
        dvec_ref[0:1, :] = jnp.sum(jnp.where(is_f, dff, 0.0), axis=0, keepdims=True)
        dvec_ref[1:2, :] = jnp.sum(jnp.where(is_a, dg * nea * _softplus(za), 0.0), axis=0, keepdims=True)
        dvec_ref[2:3, :] = jnp.sum(jnp.where(is_a, dga, 0.0), axis=0, keepdims=True)

    vec = pl.BlockSpec((1, LANES), lambda i: (0, 0))
    full = lambda r, c: pl.BlockSpec((r, c), lambda i: (0, 0))
    small = pl.BlockSpec((t, LANES), lambda i: (0, COL_SMALL // LANES))
    return pl.pallas_call(
        body, name="small_bwd", grid=(1,),
        in_specs=[small, vec, vec, vec, full(t, LANES), full(LANES, t), full(t, 512), full(t, 512), ANY_SPEC],
        out_specs=[small, full(8, LANES)], out_shape=[_sds(dproj.shape, BF), _sds((8, LANES))],
        input_output_aliases={8: 0}, compiler_params=_params("arbitrary"),
    )(proj, fb, al, dtb, dcq, dckt, dbe, dge, dproj)


def _in_bwd(dproj, wt_al, x, nw, dx1, after):
    t = x.shape[0]
    tm = min(MATMUL_BLOCK, t)

    def body(dp_ref, w_ref, x_ref, nw_ref, dx1_ref, dx_ref, dnw_ref):
        i = pl.program_id(0)

        @pl.when(i == 0)
        def _():
            dnw_ref[...] = jnp.zeros_like(dnw_ref)

        dh = jnp.dot(dp_ref[...], w_ref[...], preferred_element_type=F32)
        xv = x_ref[...]
        r = lax.rsqrt(jnp.mean(xv * xv, axis=-1, keepdims=True) + EPS)
        xh = xv * r
        dnw_ref[...] += jnp.sum(dh * xh, axis=0, keepdims=True)
        dxh = dh * nw_ref[...]
        dx_ref[...] = dx1_ref[...] + r * (dxh - xh * jnp.mean(dxh * xh, axis=-1, keepdims=True))

    tok = lambda w: pl.BlockSpec((tm, w), lambda i: (i, 0))
    vec = lambda w: pl.BlockSpec((1, w), lambda i: (0, 0))
    return pl.pallas_call(
        _ordered(body), name="in_bwd", grid=(t // tm,),
        in_specs=[ANY_SPEC, tok(PROJ_W), pl.BlockSpec((PROJ_W, D_MODEL), lambda i: (0, 0)), tok(D_MODEL), vec(D_MODEL),
                  tok(D_MODEL)],
        out_specs=[tok(D_MODEL), vec(D_MODEL)], out_shape=[_sds((t, D_MODEL)), _sds((1, D_MODEL))],
        compiler_params=_params("arbitrary"),
    )(after, dproj, wt_al, x, nw, dx1)


def _row(v, width=None):
    v = v.reshape(1, -1).astype(F32)
    if width is not None and v.shape[1] < width:
        v = jnp.pad(v, ((0, 0), (0, width - v.shape[1])))
    return v


def _lane_vec(v, first):
    return jnp.pad(v.astype(F32), (first, LANES - first - v.shape[0])).reshape(1, LANES)


def _local_step(x, target, wt_al, started, late_weights, on_grads, convw, pre_mix_norm, fox_f_bias, fox_out_norm,
                gdn_a_log, gdn_dt_bias, gdn_out_norm, post_mix_norm, pre_mlp_norm, post_mlp_norm):
    t = x.shape[0]
    nch = t // CHUNK
    nw, pmw, plw, pw = _row(pre_mix_norm), _row(post_mix_norm), _row(pre_mlp_norm), _row(post_mlp_norm)
    fb, al, dtb = _lane_vec(fox_f_bias, SM_FF), _lane_vec(gdn_a_log, SM_GA), _lane_vec(gdn_dt_bias, SM_GA)
    fnw = _row(jnp.tile(fox_out_norm, 2))
    gnw = _row(gdn_out_norm)

    proj, h = _norm_proj(x, nw, wt_al, started)
    cumt, beta, g = _small_prep(proj, fb, al, dtb)
    qn, kn, cv, gc, be, mmat, amat = _gdn_prep(proj, convw, beta, g)
    n_prob = N_GDN_HEADS * nch
    m3 = mmat.reshape(n_prob, CHUNK, CHUNK)
    if n_prob < LANES:
        m3 = jnp.pad(m3, ((0, LANES - n_prob), (0, 0), (0, 0)))
    tinv = _tri_inverse(m3)[:n_prob].reshape(N_GDN_HEADS, nch, CHUNK, CHUNK)
    gdn_o, s_all, vn_all = _gdn_scan(qn, kn, cv, be, gc, tinv, amat)
    token = late_weights("mlp_relay", gdn_o)
    o_fox, lse, fox_n = _fox_fwd(proj, cumt, fnw, token)
    w_out = late_weights("w_out", fox_n)
    x1, h2, mixed, omix, h2t = _mix_out(fox_n, gdn_o, proj, gnw, w_out, x, pmw, plw, token)
    w_up, w_down = late_weights("mlp", h2)
    up = _mlp_up(h2, w_up)
    dy, dx2, loss, d_pw = _mlp_down_loss(up, w_down, x1, pw, target)

    dup = _mlp_bwd_act(dy, w_down, up)
    relu2 = lambda u: jnp.square(jnp.maximum(u.astype(F32), 0.0))
    g_down = _wgrad(up, dy, D_FF // N_DEV, a_fn=relu2, name="wgrad_down")
    g_up = _wgrad_pre_t(h2t, dup, D_FF // N_DEV, name="wgrad_up")
    token = on_grads("mlp", (g_up, g_down))
    dx1, dmixed, d_plw, d_pmw = _mlp_bwd_in(dup, w_up, x1, plw, dx2, mixed, pmw, token)
    g_out = _wgrad(omix, dmixed, 512, split=4, name="wgrad_out")
    do_fox, dgo, dproj, d_fnw, d_gnw = _out_bwd(dmixed, w_out, o_fox, gdn_o, proj, fnw, gnw, g_out)
    token = on_grads("w_out", g_out)
    dqn, dkn, dcv, dbe, dge = _gdn_bwd(qn, kn, cv, be, gc, tinv, amat, s_all, vn_all, dgo, token)
    dproj, dcq, dckt = _fox_bwd(proj, cumt, lse, o_fox, do_fox, dproj, token)
    dproj, dwq, dwk, dwv = _gdn_bwd_conv(proj, convw, dqn, dkn, dcv, dproj)
    dproj, dvec = _small_bwd(proj, fb, al, dtb, dcq, dckt, dbe, dge, dproj)
    g_main = _wgrad(dproj, h, WGRAD_IN_ROWS, name="wgrad_in")
    g_tail = _wgrad(dproj, h, LANES, a_block0=COL_SMALL // LANES, name="wgrad_in_small")
    token = on_grads("w_in", (g_main, g_tail))
    grad_x, d_nw = _in_bwd(dproj, wt_al, x, nw, dx1, token)
    small = dict(norms=(d_nw, d_pmw, d_plw, d_pw), fox_out_norm=d_fnw, gdn_out_norm=d_gnw, loss=loss, vectors=dvec,
                 conv=(dwq, dwk, dwv))
    return grad_x, small


MESH_IDS = pl.DeviceIdType.MESH
CHIP_FLIPS = ((0, 0), (1, 0), (0, 1), (1, 1))


def _place():
    return lax.axis_index("x"), lax.axis_index("y"), lax.axis_index("c")


def _all_gather(blocks, later, dtype):
    n, k = len(blocks), len(later)

    def body(*refs):
        ins, shards, outs = refs[:n], refs[n:n + k], refs[n + k:2 * n + k]
        zones, to_send = refs[2 * n + k:2 * n + 2 * k], refs[2 * n + 2 * k:2 * n + 3 * k]
        stage_in, stage_out = refs[2 * n + 3 * k:2 * n + 4 * k], refs[2 * n + 4 * k:2 * n + 5 * k]
        send_sems, recv_sems, local_sems, late_sems = refs[2 * n + 5 * k:]
        x, y, c = _place()
        sibling = (x, y, 1 - c)
        chips = [(x ^ fx, y ^ fy) for fx, fy in CHIP_FLIPS[1:]]

        def slot(out, px, py, pc):
            return out.at[4 * px + 2 * py + pc]

        def copy(a, k, block, to, src=None):
            return pltpu.make_async_remote_copy(
                src_ref=slot(outs[a], *block) if src is None else src, dst_ref=slot(outs[a], *block),
                send_sem=send_sems.at[a, k], recv_sem=recv_sems.at[a, k], device_id=to, device_id_type=MESH_IDS)

        pending = []
        for a in range(n):
            mine = pltpu.make_async_copy(ins[a], slot(outs[a], x, y, c), local_sems.at[a])
            mine.start()
            pending.append(mine)
        sends = []
        for a in range(n):
            first = [copy(a, 1 + j, (x, y, c), (*chip, c), src=ins[a]) for j, chip in enumerate(chips)][::-1]
            first.append(copy(a, 0, (x, y, c), sibling, src=ins[a]))
            for cp in first:
                cp.start()
            sends += first
        loads = [pltpu.make_async_copy(shards[a], stage_in[a], late_sems.at[a, 0]) for a in range(k)]
        for cp in loads:
            cp.start()
        for a, (_, transposed) in enumerate(later):
            loads[a].wait()
            val = stage_in[a][...]
            stage_out[a][...] = (val.T if transposed else val).astype(dtype)
            for j, dst in enumerate((slot(zones[a], x, y, c), to_send[a])):
                cp = pltpu.make_async_copy(stage_out[a], dst, late_sems.at[a, 1 + j])
                cp.start()
                pending.append(cp)
        for a in range(n):
            for j, chip in reversed(list(enumerate(chips))):
                copy(a, 1 + j, (*chip, c), (x, y, c)).wait_recv()
                fwd = copy(a, 4 + j, (*chip, c), sibling)
                fwd.start()
                sends.append(fwd)
        for a in range(n):
            copy(a, 0, sibling, (x, y, c)).wait_recv()
            for j, chip in enumerate(chips):
                copy(a, 4 + j, (*chip, 1 - c), (x, y, c)).wait_recv()
        for cp in sends:
            cp.wait_send()
        for cp in pending:
            cp.wait()

    shapes = [s_.shape[::-1] if transposed else s_.shape for s_, transposed in later]
    out = pl.pallas_call(
        body, name="all_gather_weights", in_specs=[ANY_SPEC] * (n + k), out_specs=[ANY_SPEC] * (n + 2 * k),
        out_shape=[_sds((N_DEV,) + b.shape, b.dtype) for b in blocks] + [_sds((N_DEV,) + sh, dtype) for sh in shapes]
        + [_sds(sh, dtype) for sh in shapes],
        scratch_shapes=[pltpu.VMEM(s_.shape, s_.dtype) for s_, _ in later] + [pltpu.VMEM(sh, dtype) for sh in shapes]
        + [pltpu.SemaphoreType.DMA((n, 7)), pltpu.SemaphoreType.DMA((n, 7)), pltpu.SemaphoreType.DMA((n,)),
           pltpu.SemaphoreType.DMA((k, 3))],
        compiler_params=pltpu.CompilerParams(vmem_limit_bytes=VMEM_LIMIT, has_side_effects=True),
    )(*blocks, *[s_ for s_, _ in later])
    return out[:n], out[n:n + k], out[n + k:]


def _adamw(w, g, m, v):
    m = ADAM_B1 * m + (1.0 - ADAM_B1) * g
    v = ADAM_B2 * v + (1.0 - ADAM_B2) * (g * g)
    m_hat = m / (1.0 - ADAM_B1 ** ADAM_STEP)
    v_hat = v / (1.0 - ADAM_B2 ** ADAM_STEP)
    return -ADAM_LR * (m_hat / (jnp.sqrt(v_hat) + ADAM_EPS) + ADAM_WD * w), m, v


def _pair_reduce(g, name):
    _, r, c_ = g.shape
    n = len(CHIP_FLIPS)

    def body(g_ref, out_ref, sib_buf, send_sems, recv_sems):
        x, y, c = _place()
        chips = [(x ^ fx, y ^ fy) for fx, fy in CHIP_FLIPS]
        piece = lambda chip, core: g_ref.at[4 * chip[0] + 2 * chip[1] + core]
        copies = [pltpu.make_async_remote_copy(
            src_ref=piece(chip, 1 - c), dst_ref=sib_buf.at[j], send_sem=send_sems.at[j], recv_sem=recv_sems.at[j],
            device_id=(x, y, 1 - c), device_id_type=MESH_IDS) for j, chip in enumerate(chips)]
        for cp in copies:
            cp.start()
        for j, chip in enumerate(chips):
            copies[j].wait_recv()
            out_ref[j] = (piece(chip, c)[...].astype(F32) + sib_buf[j].astype(F32)).astype(BF)
        for cp in copies:
            cp.wait_send()

    return pl.pallas_call(
        body, name=name, in_specs=[VMEM_SPEC], out_specs=VMEM_SPEC, out_shape=_sds((n, r, c_), BF),
        scratch_shapes=[pltpu.VMEM((n, r, c_), BF), pltpu.SemaphoreType.DMA((n,)), pltpu.SemaphoreType.DMA((n,))],
        compiler_params=pltpu.CompilerParams(vmem_limit_bytes=VMEM_LIMIT, has_side_effects=True),
    )(g)


HBM_SPEC = pl.BlockSpec(memory_space=pltpu.HBM)
SEM_SPEC = pl.BlockSpec(memory_space=pltpu.SEMAPHORE)
DATAFLOW = pltpu.SideEffectType.DATAFLOW_SIDE_EFFECTING


def _peers():
    x, y, c = _place()
    return 4 * x + 2 * y + c, [(x ^ (k >> 2), y ^ ((k >> 1) & 1), c ^ (k & 1)) for k in range(1, N_DEV)]


def _peer_index(peer):
    return 4 * peer[0] + 2 * peer[1] + peer[2]


def _exchange_start(srcs, zones, pieces, name, chips=False):
    n = len(srcs)
    fresh = zones is None
    if fresh:
        slots = len(CHIP_FLIPS) if chips else N_DEV
        zones = [_sds((slots,) + (v.shape[1:] if pieces else v.shape), v.dtype) for v in srcs]
    n_in = n if fresh else 2 * n
    among_chips = list(chips) if isinstance(chips, (list, tuple)) else [chips] * n

    def body(*refs):
        ins, sems, token = refs[:n], refs[n_in:n_in + 2 * n], refs[-1]
        zs = refs[n_in + 3 * n:n_in + 4 * n] if fresh else refs[n:2 * n]
        me, peers = _peers()
        x, y, c = _place()
        for a in range(n):
            if among_chips[a] and pieces:
                routes = [((x ^ fx, y ^ fy, c), j, j) for j, (fx, fy) in enumerate(CHIP_FLIPS) if j]
            elif among_chips[a]:
                routes = [((x ^ fx, y ^ fy, c), None, me) for fx, fy in CHIP_FLIPS[1:]]
            else:
                routes = [(peer, _peer_index(peer) if pieces else None, me) for peer in peers]
            for peer, src_slot, dst_slot in routes:
                pltpu.make_async_remote_copy(
                    src_ref=ins[a] if src_slot is None else ins[a].at[src_slot], dst_ref=zs[a].at[dst_slot],
                    send_sem=sems[2 * a], recv_sem=sems[2 * a + 1], device_id=peer, device_id_type=MESH_IDS).start()
        token[...] = jnp.zeros_like(token)

    hbm = lambda v: pltpu.with_memory_space_constraint(v, pltpu.HBM)
    out = pl.pallas_call(
        body, name=name,
        out_shape=tuple([pltpu.SemaphoreType.DMA(())] * (2 * n) + [pltpu.HBM(v.shape, v.dtype) for v in srcs]
                        + [pltpu.HBM(z.shape, z.dtype) for z in zones] + [_sds((8, LANES))]),
        in_specs=[HBM_SPEC] * n_in, out_specs=tuple([SEM_SPEC] * (2 * n) + [HBM_SPEC] * (2 * n) + [VMEM_SPEC]),
        input_output_aliases={i: 2 * n + i for i in range(n_in)},
        compiler_params=pltpu.CompilerParams(has_side_effects=DATAFLOW),
    )(*[hbm(v) for v in srcs], *([] if fresh else [hbm(z) for z in zones]))
    return out[:2 * n], out[2 * n:3 * n], out[3 * n:4 * n], out[-1]


def _relay_start(zones, name):
    n = len(zones)

    def body(*refs):
        zs, sems, token = refs[:n], refs[n:3 * n], refs[-1]
        x, y, c = _place()
        for fx, fy in CHIP_FLIPS:
            slot = 4 * (x ^ fx) + 2 * (y ^ fy) + c
            for a in range(n):
                pltpu.make_async_remote_copy(
                    src_ref=zs[a].at[slot], dst_ref=zs[a].at[slot], send_sem=sems[2 * a], recv_sem=sems[2 * a + 1],
                    device_id=(x, y, 1 - c), device_id_type=MESH_IDS).start()
        token[...] = jnp.zeros_like(token)

    out = pl.pallas_call(
        body, name=name,
        out_shape=tuple([pltpu.SemaphoreType.DMA(())] * (2 * n) + [pltpu.HBM(z.shape, z.dtype) for z in zones]
                        + [_sds((8, LANES))]),
        in_specs=[HBM_SPEC] * n, out_specs=tuple([SEM_SPEC] * (2 * n) + [HBM_SPEC] * n + [VMEM_SPEC]),
        input_output_aliases={i: 2 * n + i for i in range(n)},
        compiler_params=pltpu.CompilerParams(has_side_effects=DATAFLOW),
    )(*[pltpu.with_memory_space_constraint(z, pltpu.HBM) for z in zones])
    return out[:2 * n], [], out[2 * n:3 * n], out[-1]


def _exchange_wait(sems, srcs, zones, after, name, chips=False, n_copies=None):
    n, n_src = len(zones), len(srcs)
    after = list(after) if isinstance(after, (list, tuple)) else [after]
    n_copies = n_copies or (len(CHIP_FLIPS) - 1 if chips else N_DEV - 1)

    def body(*refs):
        zs, sm = refs[n_src:n_src + n], refs[n_src + n:n_src + 3 * n]
        me, peers = _peers()
        for a in range(n):
            seven = zs[a].at[pl.ds(0, n_copies)]
            cp = pltpu.make_async_remote_copy(src_ref=seven, dst_ref=seven, send_sem=sm[2 * a], recv_sem=sm[2 * a + 1],
                                              device_id=peers[0], device_id_type=MESH_IDS)
            cp.wait_send()
            cp.wait_recv()

    out = pl.pallas_call(
        body, name=name, out_shape=tuple([pltpu.HBM(v.shape, v.dtype) for v in srcs] + [pltpu.HBM(z.shape, z.dtype) for z in zones]),
        in_specs=[HBM_SPEC] * (n_src + n) + [SEM_SPEC] * (2 * n) + [ANY_SPEC] * len(after),
        out_specs=tuple([HBM_SPEC] * (n_src + n)), input_output_aliases={i: i for i in range(n_src + n)},
        compiler_params=pltpu.CompilerParams(has_side_effects=DATAFLOW),
    )(*srcs, *zones, *sems, *after)
    return out[:n_src], out[n_src:]


def _sum_adamw(zone, own, w, m, v, name, chips=False):
    n_slots, r, c_ = zone.shape
    rb = next((b for b in (256, 128) if r % b == 0), r)

    def body(me_ref, z_ref, own_ref, w_ref, m_ref, v_ref, grad_ref, delta_ref, nm_ref, nv_ref):
        total = None
        for d in range(n_slots):
            part = jnp.where(me_ref[0] == d, own_ref[0], z_ref[d]).astype(F32)
            total = part if total is None else total + part
        grad_ref[...] = total
        delta_ref[...], nm_ref[...], nv_ref[...] = _adamw(w_ref[...], total, m_ref[...], v_ref[...])

    x, y, c = _place()
    mine = 0 * x if chips else 4 * x + 2 * y + c
    blk = pl.BlockSpec((rb, c_), lambda i, me_ref: (i, 0))
    return pl.pallas_call(
        body, name=name,
        grid_spec=pltpu.PrefetchScalarGridSpec(
            num_scalar_prefetch=1, grid=(r // rb,),
            in_specs=[pl.BlockSpec((n_slots, rb, c_), lambda i, me_ref: (0, i, 0)),
                      pl.BlockSpec((1, rb, c_), lambda i, me_ref: (me_ref[0], i, 0)), blk, blk, blk],
            out_specs=[blk] * 4),
        out_shape=[_sds((r, c_))] * 4, compiler_params=_params("parallel"),
    )(mine.astype(jnp.int32).reshape(1), zone, own, w, m, v)


SMALL_NORMS = ("pre_mix_norm", "post_mix_norm", "pre_mlp_norm", "post_mlp_norm")
SMALL_ORDER = SMALL_NORMS + ("fox_out_norm", "gdn_out_norm", "fox_f_bias", "gdn_a_log", "gdn_dt_bias", "gdn_conv_w")
CONV_SLAB_ROWS, CONV_SLAB_LANES = 8, 256


def _small_pack(small):
    def body(n0, n1, n2, n3, fnw_ref, gnw_ref, loss_ref, vec_ref, out_ref):
        out_ref[...] = jnp.zeros_like(out_ref)
        for i, ref in enumerate((n0, n1, n2, n3)):
            out_ref[i:i + 1, :] = ref[...]
        out_ref[4:5, 0:LANES] = fnw_ref[...]
        out_ref[4:5, LANES:2 * LANES] = gnw_ref[...]
        out_ref[4:5, 2 * LANES:3 * LANES] = loss_ref[...]
        out_ref[5:8, 0:LANES] = vec_ref[0:3, :]

    return pl.pallas_call(body, name="small_pack", in_specs=[VMEM_SPEC] * 8, out_specs=VMEM_SPEC,
                          out_shape=_sds((8, D_MODEL)))(*small["norms"], small["fox_out_norm"], small["gdn_out_norm"],
                                                        small["loss"], small["vectors"])


def _conv_slabs(dconv):
    blocks = dconv.reshape(CONV_K, N_DEV, -1).transpose(1, 0, 2)
    blocks = jnp.pad(blocks, ((0, 0), (0, CONV_SLAB_ROWS - CONV_K), (0, CONV_SLAB_LANES - blocks.shape[2])))
    return blocks.reshape(N_DEV * CONV_SLAB_ROWS, CONV_SLAB_LANES)


def _small_update(zone, conv_zone, own, own_conv, w, m, v):
    n = len(SMALL_ORDER)
    n_conv = w["gdn_conv_w"].shape[1]

    def body(me_ref, z_ref, zc_ref, own_ref, ownc_ref, *refs):
        params, loss_ref, outs, (tot, totc) = refs[:3 * n], refs[3 * n], refs[3 * n + 1:7 * n + 1], refs[-2:]
        total, total_c = None, None
        for d in range(N_DEV):
            part = jnp.where(me_ref[0] == d, own_ref[...], z_ref[d])
            part_c = jnp.where(me_ref[0] == d, ownc_ref[...], zc_ref[d])
            total, total_c = (part, part_c) if d == 0 else (total + part, total_c + part_c)
        tot[...] = total
        totc[...] = total_c
        loss_ref[...] = tot[4, 2 * LANES:2 * LANES + 1]
        mine = totc[pl.ds(pl.multiple_of(me_ref[0] * CONV_SLAB_ROWS, CONV_SLAB_ROWS), CONV_SLAB_ROWS), :]
        g = dict(zip(SMALL_NORMS, (tot[0], tot[1], tot[2], tot[3])))
        g.update(fox_out_norm=tot[4, 0:FOX_HEAD_DIM], gdn_out_norm=tot[4, LANES:LANES + GDN_HEAD_DIM],
                 fox_f_bias=tot[5, SM_FF:SM_FF + N_FOX_HEADS], gdn_a_log=tot[6, SM_GA:SM_GA + N_GDN_HEADS],
                 gdn_dt_bias=tot[7, SM_GA:SM_GA + N_GDN_HEADS], gdn_conv_w=mine[0:CONV_K, 0:n_conv])
        for i, name in enumerate(SMALL_ORDER):
            w_ref, m_ref, v_ref = params[3 * i:3 * i + 3]
            outs[4 * i][...] = g[name]
            outs[4 * i + 1][...], outs[4 * i + 2][...], outs[4 * i + 3][...] = _adamw(w_ref[...], g[name], m_ref[...],
                                                                                     v_ref[...])

    x, y, c = _place()
    operands = [a[name] for name in SMALL_ORDER for a in (w, m, v)]
    out = pl.pallas_call(
        body, name="small_update",
        in_specs=[pl.BlockSpec(memory_space=pltpu.SMEM)] + [VMEM_SPEC] * (4 + 3 * n), out_specs=[VMEM_SPEC] * (1 + 4 * n),
        out_shape=[_sds((1,))] + [_sds(w[name].shape) for name in SMALL_ORDER for _ in range(4)],
        scratch_shapes=[pltpu.VMEM(zone.shape[1:], F32), pltpu.VMEM(conv_zone.shape[1:], F32)],
    )((4 * x + 2 * y + c).astype(jnp.int32).reshape(1), zone, conv_zone, own, own_conv, *operands)
    return out[0][0], {name: out[1 + 4 * i:5 + 4 * i] for i, name in enumerate(SMALL_ORDER)}


def _native_rows():
    groups = []
    for first, n_groups in ((0, N_FOX_HEADS // 2), (D_FOX * 3 + N_FOX_HEADS, N_GDN_HEADS)):
        for g in range(n_groups):
            groups += [(first + part * n_groups * LANES + g * LANES, first + part * n_groups * LANES + (g + 1) * LANES)
                       for part in range(3)]
    return tuple(groups) + ((3088, 3600), (1536, 1544), (3080, 3088))


NATIVE_ROWS = _native_rows()


W_IN_PIECE = D_PROJ // N_DEV
WGRAD_IN_ROWS = 512
SHUFFLE_LANES = 256


def _to_aligned_moves():
    moves, o = [], 0
    for lo, hi in NATIVE_ROWS:
        r = lo
        while r < hi:
            d = r // W_IN_PIECE
            k = min(hi, (d + 1) * W_IN_PIECE) - r
            moves.append((0, d, r - d * W_IN_PIECE, 0, o, k))
            r, o = r + k, o + k
    return moves


def _from_aligned_moves():
    moves = []
    for _, d, a, _, o, k in _to_aligned_moves():
        while k:
            n = min(k, WGRAD_IN_ROWS - o % WGRAD_IN_ROWS) if o < COL_SMALL else k
            moves.append((0, o // WGRAD_IN_ROWS, o % WGRAD_IN_ROWS, d, a, n) if o < COL_SMALL else
                         (1, 0, o - COL_SMALL, d, a, n))
            o, a, k = o + n, a + n, k - n
    return moves


def _shuffle_rows(srcs, moves, out_shape, name):
    c = srcs[0].shape[-1]

    def body(*refs):
        s_refs, o_ref, s_f, o_f = refs[:len(srcs)], refs[len(srcs)], refs[len(srcs) + 1:-1], refs[-1]
        for s_ref, f in zip(s_refs, s_f):
            f[...] = s_ref[...].astype(F32)
        o_f[...] = jnp.zeros_like(o_f)
        for i, ss, so, ds, do, k in moves:
            o_f[ds, pl.ds(do, k), :] = s_f[i][ss, pl.ds(so, k), :]
        o_ref[...] = o_f[...].astype(BF)

    blk = lambda shape: pl.BlockSpec(tuple(shape[:-1]) + (SHUFFLE_LANES,), lambda j: (0, 0, j))
    scratch = lambda shape: pltpu.VMEM(tuple(shape[:-1]) + (SHUFFLE_LANES,), F32)
    return pl.pallas_call(
        body, name=name, grid=(c // SHUFFLE_LANES,), in_specs=[blk(s.shape) for s in srcs], out_specs=blk(out_shape),
        out_shape=_sds(out_shape, BF), scratch_shapes=[scratch(s.shape) for s in srcs] + [scratch(out_shape)],
        compiler_params=_params("parallel"),
    )(*srcs)


def _cols_from_pieces(p):
    return p.transpose(1, 0, 2).reshape(p.shape[1], -1)


WEIGHT_ORDER = ("pre_mix_norm", "w_in", "fox_f_bias", "fox_out_norm", "gdn_conv_w", "gdn_a_log", "gdn_dt_bias",
                "gdn_out_norm", "w_out", "post_mix_norm", "pre_mlp_norm", "w_up", "w_down", "post_mlp_norm")


def kernel(x, pre_mix_norm, w_in, fox_f_bias, fox_out_norm, gdn_conv_w, gdn_a_log, gdn_dt_bias, gdn_out_norm, w_out, post_mix_norm, pre_mlp_norm, w_up, w_down, post_mlp_norm, loss_target, m_pre_mix_norm, m_w_in, m_fox_f_bias, m_fox_out_norm, m_gdn_conv_w, m_gdn_a_log, m_gdn_dt_bias, m_gdn_out_norm, m_w_out, m_post_mix_norm, m_pre_mlp_norm, m_w_up, m_w_down, m_post_mlp_norm, v_pre_mix_norm, v_w_in, v_fox_f_bias, v_fox_out_norm, v_gdn_conv_w, v_gdn_a_log, v_gdn_dt_bias, v_gdn_out_norm, v_w_out, v_post_mix_norm, v_pre_mlp_norm, v_w_up, v_w_down, v_post_mlp_norm):
    w = dict(pre_mix_norm=pre_mix_norm, w_in=w_in, fox_f_bias=fox_f_bias, fox_out_norm=fox_out_norm,
             gdn_conv_w=gdn_conv_w, gdn_a_log=gdn_a_log, gdn_dt_bias=gdn_dt_bias, gdn_out_norm=gdn_out_norm, w_out=w_out,
             post_mix_norm=post_mix_norm, pre_mlp_norm=pre_mlp_norm, w_up=w_up, w_down=w_down, post_mlp_norm=post_mlp_norm)
    mom = dict(pre_mix_norm=m_pre_mix_norm, w_in=m_w_in, fox_f_bias=m_fox_f_bias, fox_out_norm=m_fox_out_norm,
               gdn_conv_w=m_gdn_conv_w, gdn_a_log=m_gdn_a_log, gdn_dt_bias=m_gdn_dt_bias, gdn_out_norm=m_gdn_out_norm,
               w_out=m_w_out, post_mix_norm=m_post_mix_norm, pre_mlp_norm=m_pre_mlp_norm, w_up=m_w_up, w_down=m_w_down,
               post_mlp_norm=m_post_mlp_norm)
    var = dict(pre_mix_norm=v_pre_mix_norm, w_in=v_w_in, fox_f_bias=v_fox_f_bias, fox_out_norm=v_fox_out_norm,
               gdn_conv_w=v_gdn_conv_w, gdn_a_log=v_gdn_a_log, gdn_dt_bias=v_gdn_dt_bias, gdn_out_norm=v_gdn_out_norm,
               w_out=v_w_out, post_mix_norm=v_post_mix_norm, pre_mlp_norm=v_pre_mlp_norm, w_up=v_w_up, w_down=v_w_down,
               post_mlp_norm=v_post_mlp_norm)

    (win_g, conv_g), zones, shards = _all_gather([w_in.T.astype(BF), gdn_conv_w],
                                                 [(w_out, False), (w_up, True), (w_down, False)], BF)
    wt_al = _shuffle_rows([win_g], _to_aligned_moves(), (1, PROJ_W, D_MODEL), "w_in_to_aligned")[0]
    convw = _cols_from_pieces(conv_g)
    sems, shards, zones, after = _exchange_start(shards, zones, False, "gather_start", chips=True)
    gathers = dict(hop=(sems, shards, zones))

    def late_weights(name, after):
        if name == "mlp_relay":
            sems, shards, zones = gathers.pop("hop")
            _, zones = _exchange_wait(sems, shards, zones, after, "gather_wait", chips=True)
            sems, _, zones, token = _relay_start(zones, "gather_relay")
            gathers.update(w_out=(sems[:2], zones[:1]), mlp=(sems[2:], zones[1:]))
            return token
        sems, zones = gathers[name]
        _, got = _exchange_wait(sems, [], zones, after, "gather_" + name + "_done", n_copies=len(CHIP_FLIPS))
        if name == "w_out":
            return got[0].reshape(D_MODEL, D_MODEL)
        return got[0].reshape(D_FF, D_MODEL), got[1].reshape(D_FF, D_MODEL)

    scatters = {}

    def on_grads(name, g):
        if name == "mlp":
            scatters["mlp"] = list(g)
            return g[0]
        if name == "w_out":
            sems, srcs, zones, token = _exchange_start(scatters["mlp"] + [g], None, True, "scatter_mlp_w_out_start")
            scatters["mlp"] = (sems[:4], srcs[:2], zones[:2], token)
            scatters["w_out"] = (sems[4:], srcs[2:], zones[2:], token)
            return token
        g = _shuffle_rows(list(g), _from_aligned_moves(), (N_DEV, W_IN_PIECE, D_MODEL), "w_in_grad_from_aligned")
        scatters[name] = _exchange_start([_pair_reduce(g, "pair_reduce_w_in")], None, True, "scatter_w_in_start", chips=True)
        return scatters[name][3]

    grad_x, small = _local_step(
        x[0], loss_target[0], wt_al, after, late_weights, on_grads, convw, pre_mix_norm,
        fox_f_bias, fox_out_norm, gdn_a_log, gdn_dt_bias, gdn_out_norm, post_mix_norm, pre_mlp_norm, post_mlp_norm)
    slabs = [_small_pack(small), _conv_slabs(jnp.concatenate(small["conv"], axis=1))]
    scatters["small"] = _exchange_start(slabs, None, False, "small_start")

    grads, delta, new_m, new_v = {}, {}, {}, {}
    after = scatters["small"][3]
    for name, members in (("mlp", ("w_up", "w_down")), ("w_out", ("w_out",)), ("small", ()), ("w_in", ("w_in",))):
        sems, srcs, zones, _ = scatters[name]
        srcs, zones = _exchange_wait(sems, srcs, zones, after, "scatter_" + name + "_wait", chips=name == "w_in")
        if name == "small":
            loss, updated = _small_update(*zones, *srcs, w, mom, var)
            for n, res in updated.items():
                grads[n], delta[n], new_m[n], new_v[n] = res
            after = grads["pre_mix_norm"]
        for n, zone, own in zip(members, zones, srcs):
            if n == "w_in":
                res = _sum_adamw(zone, own, w[n].T, mom[n].T, var[n].T, "adamw_" + n, chips=True)
                grads[n], delta[n], new_m[n], new_v[n] = [r.T for r in res]
            else:
                grads[n], delta[n], new_m[n], new_v[n] = _sum_adamw(zone, own, w[n], mom[n], var[n], "adamw_" + n)
        if members:
            after = [grads[n] for n in members]

    return (loss, grad_x[None], *[grads[n] for n in WEIGHT_ORDER], *[delta[n] for n in WEIGHT_ORDER],
            *[new_m[n] for n in WEIGHT_ORDER], *[new_v[n] for n in WEIGHT_ORDER])
```

```python
import jax
import jax.numpy as jnp
from jax import lax
from jax.experimental import pallas as pl
from jax.experimental.pallas import tpu as pltpu

F32 = jnp.float32
BF = jnp.bfloat16

D_MODEL = 1024
N_FOX_HEADS, FOX_HEAD_DIM = 8, 64
N_GDN_HEADS, GDN_HEAD_DIM = 4, 128
D_FOX = N_FOX_HEADS * FOX_HEAD_DIM
D_GDN = N_GDN_HEADS * GDN_HEAD_DIM
CHUNK = 64
CONV_K = 4
D_FF = 4 * D_MODEL
EPS = 1e-6
D_PROJ = 3600
N_DEV = 8

PROJ_W = 3712
COL_FOX, COL_GDN, COL_GZ, COL_SMALL = 0, 1536, 3072, 3584
LANES = 128
QKV = 3 * LANES
SM_FF, SM_GB, SM_GA = 0, 8, 12

ADAM_LR, ADAM_B1, ADAM_B2, ADAM_EPS, ADAM_WD, ADAM_STEP = 0.001, 0.9, 0.999, 1e-08, 0.01, 10

TOKEN_BLOCK = 256
MATMUL_BLOCK = 512
TRI_ROWS = 4
FOX_SCALE = FOX_HEAD_DIM ** -0.5
GDN_QSCALE = GDN_HEAD_DIM ** -0.5
NEG_BIG = -1e30
VMEM_LIMIT = 56 * 1024 * 1024

VMEM_SPEC = pl.BlockSpec(memory_space=pltpu.VMEM)
ANY_SPEC = pl.BlockSpec(memory_space=pl.ANY)


def _sds(shape, dtype=F32):
    return jax.ShapeDtypeStruct(shape, dtype)


def _params(*sem):
    return pltpu.CompilerParams(dimension_semantics=sem if sem else None, vmem_limit_bytes=VMEM_LIMIT)


def _ordered(body):
    def ordered(_, *refs):
        body(*refs)

    return ordered


def _mm(a, b):
    return jnp.dot(a.astype(BF), b.astype(BF), preferred_element_type=F32)


def _mm_nt(a, b):
    return lax.dot_general(a.astype(BF), b.astype(BF), (((1,), (1,)), ((), ())), preferred_element_type=F32)


def _mm_tn(a, b):
    return lax.dot_general(a.astype(BF), b.astype(BF), (((0,), (0,)), ((), ())), preferred_element_type=F32)


def _sigmoid(x):
    return 1.0 / (1.0 + jnp.exp(-x))


def _softplus(x):
    return jnp.maximum(x, 0.0) + jnp.log1p(jnp.exp(-jnp.abs(x)))


def _iota(shape, dim):
    return lax.broadcasted_iota(jnp.int32, shape, dim)


def _shift_down(x, s, row):
    return jnp.where(row >= s, pltpu.roll(x, s, 0), 0.0)


def _shift_up(x, s, row):
    n = x.shape[0]
    return jnp.where(row < n - s, pltpu.roll(x, n - s, 0), 0.0)


def _norm_proj(x, nw, wt_al, after):
    t = x.shape[0]

    def body(x_ref, nw_ref, w_ref, proj_ref, h_ref):
        xv = x_ref[...]
        r = lax.rsqrt(jnp.mean(xv * xv, axis=-1, keepdims=True) + EPS)
        h = (xv * r * nw_ref[...]).astype(BF)
        h_ref[...] = h
        proj_ref[...] = lax.dot_general(h, w_ref[...], (((1,), (1,)), ((), ())), preferred_element_type=F32)

    tm = min(MATMUL_BLOCK, t)
    return pl.pallas_call(
        _ordered(body), name="norm_proj", grid=(t // tm,),
        in_specs=[ANY_SPEC, pl.BlockSpec((tm, D_MODEL), lambda i: (i, 0)), pl.BlockSpec((1, D_MODEL), lambda i: (0, 0)),
                  pl.BlockSpec((PROJ_W, D_MODEL), lambda i: (0, 0))],
        out_specs=[pl.BlockSpec((tm, PROJ_W), lambda i: (i, 0)), pl.BlockSpec((tm, D_MODEL), lambda i: (i, 0))],
        out_shape=[_sds((t, PROJ_W)), _sds((t, D_MODEL), BF)],
        compiler_params=_params("parallel"),
    )(after, x, nw, wt_al)


def _lane_column(x, lane):
    return jnp.sum(jnp.where(_iota((1, LANES), 1) == lane, x, 0.0), axis=-1, keepdims=True)


def _small_prep(proj, fb, al, dtb):
    t = proj.shape[0]

    def body(sm_ref, fb_ref, al_ref, dtb_ref, cumt_ref, beta_ref, g_ref):
        s = sm_ref[...]
        z = s + fb_ref[...]
        cum = jnp.minimum(z, 0.0) - jnp.log1p(jnp.exp(-jnp.abs(z)))
        row = _iota((t, LANES), 0)
        step = 1
        while step < t:
            cum = cum + _shift_down(cum, step, row)
            step *= 2
        cumt_ref[...] = cum.T
        beta_ref[...] = _sigmoid(s)
        g_ref[...] = -jnp.exp(al_ref[...]) * _softplus(s + dtb_ref[...])

    vec = pl.BlockSpec((1, LANES), lambda i: (0, 0))
    tok = pl.BlockSpec((t, LANES), lambda i: (0, 0))
    return pl.pallas_call(
        body, name="small_prep", grid=(1,),
        in_specs=[pl.BlockSpec((t, LANES), lambda i: (0, COL_SMALL // LANES)), vec, vec, vec],
        out_specs=[pl.BlockSpec((LANES, t), lambda i: (0, 0)), tok, tok],
        out_shape=[_sds((LANES, t)), _sds((t, LANES)), _sds((t, LANES))],
        compiler_params=_params("arbitrary"),
    )(proj, fb, al, dtb)


def _fox_stack(x, first):
    return jnp.concatenate([jnp.where(first, x, 0.0), jnp.where(first, 0.0, x)], axis=0).astype(BF)


def _fox_unstack(y, first):
    n = y.shape[0] // 2
    return jnp.where(first, y[:n], y[n:])


def _fox_logits(q2_i, kb, cumt_ref, pair, i, tq):
    klen = (i + 1) * tq
    s = lax.dot_general(q2_i, kb[:klen], (((1,), (1,)), ((), ())), preferred_element_type=F32)
    upper = _iota((2 * tq, 1), 0) < tq
    s = s - jnp.where(upper, cumt_ref[pl.ds(2 * pair, 1), 0:klen], cumt_ref[pl.ds(2 * pair + 1, 1), 0:klen])
    causal = _iota((2 * tq, tq), 1) <= _iota((2 * tq, tq), 0) % tq
    parts = [(s[:, :klen - tq], 0, klen - tq)] if i else []
    return parts + [(jnp.where(causal, s[:, klen - tq:], NEG_BIG), klen - tq, klen)]


def _fox_fwd(proj, cumt, fnw, after):
    t = proj.shape[0]
    tq = min(TOKEN_BLOCK, t // 2)
    nq = t // tq

    def body(q_ref, k_ref, v_ref, cumt_ref, fnw_ref, o_ref, lse_ref, fn_ref):
        j = pl.program_id(0)
        first = _iota((1, LANES), 1) < FOX_HEAD_DIM
        kb = k_ref[...].astype(BF)
        vb = v_ref[...].astype(BF)
        for i in range(nq):
            rows = slice(i * tq, (i + 1) * tq)
            q2 = _fox_stack(q_ref[rows, :] * FOX_SCALE, first)
            parts = _fox_logits(q2, kb, cumt_ref, j, i, tq)
            m = jnp.max(parts[-1][0], axis=-1, keepdims=True)
            if i:
                m = jnp.maximum(m, jnp.max(parts[0][0], axis=-1, keepdims=True))
            l = jnp.zeros((2 * tq, 1), F32)
            o = jnp.zeros((2 * tq, LANES), F32)
            for s, lo, hi in parts:
                p = jnp.exp(s - m)
                l = l + jnp.sum(p, axis=-1, keepdims=True)
                o = o + jnp.dot(p.astype(BF), vb[lo:hi], preferred_element_type=F32)
            o_acc = _fox_unstack(o / l, first)
            lse_acc = _fox_unstack(jnp.broadcast_to(m + jnp.log(l), (2 * tq, LANES)), first)
            o_ref[rows, :] = o_acc
            lse_ref[rows, :] = lse_acc
            o2 = o_acc * o_acc
            s0 = jnp.sum(jnp.where(first, o2, 0.0), axis=-1, keepdims=True)
            s1 = jnp.sum(jnp.where(first, 0.0, o2), axis=-1, keepdims=True)
            r = lax.rsqrt(jnp.where(first, s0, s1) * (1.0 / FOX_HEAD_DIM) + EPS)
            fn_ref[rows, :] = (o_acc * r * fnw_ref[...]).astype(BF)

    qkv = lambda k: pl.BlockSpec((t, LANES), lambda j: (0, COL_FOX // LANES + 3 * j + k))
    pair = pl.BlockSpec((t, LANES), lambda j: (0, j))
    return pl.pallas_call(
        _ordered(body), name="fox_fwd", grid=(N_FOX_HEADS // 2,),
        in_specs=[ANY_SPEC, qkv(0), qkv(1), qkv(2), pl.BlockSpec((LANES, t), lambda j: (0, 0)),
                  pl.BlockSpec((1, LANES), lambda j: (0, 0))],
        out_specs=[pair, pair, pair],
        out_shape=[_sds((t, D_FOX)), _sds((t, D_FOX)), _sds((t, D_FOX), BF)],
        compiler_params=_params("parallel"),
    )(after, proj, proj, proj, cumt, fnw)


def _fox_bwd(proj, cumt, lse, o, do, dproj, after):
    t = proj.shape[0]
    tq = min(TOKEN_BLOCK, t // 2)
    nq = t // tq

    def body(q_ref, k_ref, v_ref, cumt_ref, lse_ref, o_ref, do_ref, _, dqkv_ref, dcq_ref, dckt_ref, dk_s, dv_s):
        j = pl.program_id(0)

        @pl.when(j == 0)
        def _():
            dcq_ref[...] = jnp.zeros_like(dcq_ref)
            dckt_ref[...] = jnp.zeros_like(dckt_ref)

        lane = _iota((1, LANES), 1)

        first = _iota((1, LANES), 1) < FOX_HEAD_DIM
        kb = k_ref[...].astype(BF)
        vb = v_ref[...].astype(BF)
        dk_s[...] = jnp.zeros_like(dk_s)
        dv_s[...] = jnp.zeros_like(dv_s)
        for i in range(nq):
            rows = slice(i * tq, (i + 1) * tq)
            do_i = do_ref[rows, :]
            prod = do_i * o_ref[rows, :]
            lse_i = lse_ref[rows, :]
            q2 = _fox_stack(q_ref[rows, :] * FOX_SCALE, first)
            do2 = _fox_stack(do_i, first)
            delta = jnp.concatenate([jnp.sum(jnp.where(first, prod, 0.0), axis=-1, keepdims=True),
                                     jnp.sum(jnp.where(first, 0.0, prod), axis=-1, keepdims=True)], axis=0)
            lse2 = jnp.concatenate([lse_i[:, 0:1], lse_i[:, FOX_HEAD_DIM:FOX_HEAD_DIM + 1]], axis=0)
            dq2 = jnp.zeros((2 * tq, LANES), F32)
            dcq2 = jnp.zeros((2 * tq, 1), F32)
            for s, lo, hi in _fox_logits(q2, kb, cumt_ref, j, i, tq):
                p = jnp.exp(s - lse2)
                ds = p * (_mm_nt(do2, vb[lo:hi]) - delta)
                dsb = ds.astype(BF)
                dq2 = dq2 + jnp.dot(dsb, kb[lo:hi], preferred_element_type=F32)
                dk_s[lo:hi, :] += _mm_tn(dsb, q2)
                dv_s[lo:hi, :] += _mm_tn(p, do2)
                dcq2 = dcq2 + jnp.sum(ds, axis=-1, keepdims=True)
                dckt_ref[pl.ds(2 * j, 1), lo:hi] += jnp.sum(ds[:tq], axis=0, keepdims=True)
                dckt_ref[pl.ds(2 * j + 1, 1), lo:hi] += jnp.sum(ds[tq:], axis=0, keepdims=True)
            dqkv_ref[rows, 0:LANES] = (_fox_unstack(dq2, first) * FOX_SCALE).astype(BF)
            dcq_ref[rows, :] += jnp.where(lane == 2 * j, dcq2[:tq], jnp.where(lane == 2 * j + 1, dcq2[tq:], 0.0))
        dqkv_ref[:, LANES:2 * LANES] = dk_s[...].astype(BF)
        dqkv_ref[:, 2 * LANES:QKV] = dv_s[...].astype(BF)

    qkv = lambda k: pl.BlockSpec((t, LANES), lambda j: (0, COL_FOX // LANES + 3 * j + k))
    pair = pl.BlockSpec((t, LANES), lambda j: (0, j))
    rows128 = pl.BlockSpec((LANES, t), lambda j: (0, 0))
    return pl.pallas_call(
        _ordered(body), name="fox_bwd", grid=(N_FOX_HEADS // 2,),
        in_specs=[ANY_SPEC, qkv(0), qkv(1), qkv(2), rows128, pair, pair, pair, ANY_SPEC],
        out_specs=[pl.BlockSpec((t, QKV), lambda j: (0, COL_FOX // QKV + j)),
                   pl.BlockSpec((t, LANES), lambda j: (0, 0)), rows128],
        out_shape=[_sds(dproj.shape, BF), _sds((t, LANES)), _sds((LANES, t))],
        scratch_shapes=[pltpu.VMEM((t, LANES), F32), pltpu.VMEM((t, LANES), F32)],
        input_output_aliases={8: 0}, compiler_params=_params("arbitrary"),
    )(after, proj, proj, proj, cumt, lse, o, do, dproj)


def _conv(x, w, row):
    return (w[3:4, :] * x + w[2:3, :] * _shift_down(x, 1, row) + w[1:2, :] * _shift_down(x, 2, row)
            + w[0:1, :] * _shift_down(x, 3, row))


def _chunk_decay(gc_c):
    gi = gc_c[:, 0:CHUNK]
    gj = gc_c.T[0:CHUNK, :]
    ri = _iota((CHUNK, CHUNK), 0)
    cj = _iota((CHUNK, CHUNK), 1)
    return jnp.where(ri >= cj, jnp.exp(jnp.minimum(gi - gj, 0.0)), 0.0), ri > cj


def _gdn_specs(t):
    col = lambda off: pl.BlockSpec((t, LANES), lambda h: (0, off + h))
    cw = lambda off: pl.BlockSpec((CONV_K, LANES), lambda h: (0, off + h))
    mat = pl.BlockSpec((1, t // CHUNK, CHUNK, CHUNK), lambda h: (h, 0, 0, 0))
    qkv = lambda k: pl.BlockSpec((t, LANES), lambda h: (0, COL_GDN // LANES + 3 * h + k))
    return col, cw, mat, qkv


def _gdn_prep(proj, convw, beta, g):
    t = proj.shape[0]
    nch = t // CHUNK

    def body(xq_ref, xk_ref, xv_ref, wq_ref, wk_ref, wv_ref, beta_ref, g_ref,
             qn_ref, kn_ref, cv_ref, gc_ref, be_ref, m_ref, a_ref):
        row = _iota((t, LANES), 0)
        hd = pl.program_id(0)
        be_ref[...] = jnp.broadcast_to(_lane_column(beta_ref[...], SM_GB + hd), (t, LANES))

        def act(x_ref, w_ref):
            y = _conv(x_ref[...], w_ref[...], row)
            return y * _sigmoid(y)

        cq = act(xq_ref, wq_ref)
        ck = act(xk_ref, wk_ref)
        cv_ref[...] = act(xv_ref, wv_ref)
        qn_ref[...] = cq * lax.rsqrt(jnp.sum(cq * cq, axis=-1, keepdims=True) + EPS) * GDN_QSCALE
        kn_ref[...] = ck * lax.rsqrt(jnp.sum(ck * ck, axis=-1, keepdims=True) + EPS)
        gc = jnp.broadcast_to(_lane_column(g_ref[...], SM_GA + hd), (t, LANES))
        pos = row % CHUNK
        step = 1
        while step < CHUNK:
            gc = gc + jnp.where(pos >= step, pltpu.roll(gc, step, 0), 0.0)
            step *= 2
        gc_ref[...] = gc

        group = 4 if nch % 4 == 0 else 1

        def chunks(gi, carry):
            ns = [gi * group + c for c in range(group)]
            sls = [pl.ds(pl.multiple_of(n * CHUNK, CHUNK), CHUNK) for n in ns]
            ks = [kn_ref[sl, :] for sl in sls]
            kk = [_mm_nt(k_c * be_ref[sl, :], k_c) for k_c, sl in zip(ks, sls)]
            qk = [_mm_nt(qn_ref[sl, :], k_c) for k_c, sl in zip(ks, sls)]
            for c, n in enumerate(ns):
                decay, strict = _chunk_decay(gc_ref[sls[c], :])
                m_ref[0, n] = jnp.where(strict, kk[c] * decay, 0.0)
                a_ref[0, n] = qk[c] * decay
            return carry

        lax.fori_loop(0, nch // group, chunks, 0)

    col, cw, mat, qkv = _gdn_specs(t)
    return pl.pallas_call(
        body, name="gdn_prep", grid=(N_GDN_HEADS,),
        in_specs=[qkv(0), qkv(1), qkv(2), cw(0), cw(4), cw(8)] + [pl.BlockSpec((t, LANES), lambda h: (0, 0))] * 2,
        out_specs=[col(0), col(0), col(0), col(0), col(0), mat, mat],
        out_shape=[_sds((t, D_GDN))] * 5 + [_sds((N_GDN_HEADS, nch, CHUNK, CHUNK))] * 2,
        compiler_params=_params("parallel"),
    )(proj, proj, proj, convw, convw, convw, beta, g)


def _tri_inverse(m3):
    assert m3.shape == (LANES, CHUNK, CHUNK)

    def body(m_ref, t_ref, ms, ts):
        for i in range(CHUNK):
            ms[i * CHUNK:(i + 1) * CHUNK, :] = m_ref[:, i, :].T
        cidx = _iota((CHUNK, LANES), 0)

        def t_row(j):
            return ts[pl.ds(pl.multiple_of(j * CHUNK, CHUNK), CHUNK), :]

        def outer(ib, carry):
            i0 = ib * TRI_ROWS

            def inner(group, accs):
                for jj in range(TRI_ROWS):
                    jj = group * TRI_ROWS + jj
                    earlier = t_row(jj)
                    accs = tuple(acc - ms[pl.ds((i0 + r) * CHUNK + jj, 1), :] * earlier for r, acc in enumerate(accs))
                return accs

            accs = list(lax.fori_loop(
                0, ib, inner, tuple(jnp.where(cidx == i0 + r, 1.0, 0.0).astype(F32) for r in range(TRI_ROWS))))
            for r in range(TRI_ROWS):
                for q in range(r):
                    accs[r] = accs[r] - ms[pl.ds((i0 + r) * CHUNK + i0 + q, 1), :] * accs[q]
                ts[pl.ds(pl.multiple_of((i0 + r) * CHUNK, CHUNK), CHUNK), :] = accs[r]
            return carry

        lax.fori_loop(0, CHUNK // TRI_ROWS, outer, 0)
        for i in range(CHUNK):
            t_ref[:, i, :] = ts[i * CHUNK:(i + 1) * CHUNK, :].T

    return pl.pallas_call(
        body, name="tri_inverse", in_specs=[VMEM_SPEC], out_specs=VMEM_SPEC,
        out_shape=_sds((LANES, CHUNK, CHUNK)),
        scratch_shapes=[pltpu.VMEM((CHUNK * CHUNK, LANES), F32), pltpu.VMEM((CHUNK * CHUNK, LANES), F32)],
        compiler_params=_params(),
    )(m3)


def _gdn_chunk_terms(q, k, v, b, gcc):
    eg = jnp.exp(gcc)
    last = gcc[CHUNK - 1:CHUNK, :]
    egl = jnp.exp(last - gcc)
    gl = jnp.exp(last)
    kb = k * b
    return eg, egl, gl, kb, v * b, kb * eg, q * eg, k * egl


GDN_BLOCK_CHUNKS = 4


def _gdn_block_specs(t, reverse):
    cb = GDN_BLOCK_CHUNKS
    nb = t // (cb * CHUNK)
    idx = (lambda i: nb - 1 - i) if reverse else (lambda i: i)
    tok = pl.BlockSpec((cb * CHUNK, D_GDN), lambda i: (idx(i), 0))
    mat = pl.BlockSpec((N_GDN_HEADS, cb, CHUNK, CHUNK), lambda i: (0, idx(i), 0, 0))
    state = pl.BlockSpec((N_GDN_HEADS, cb, GDN_HEAD_DIM, GDN_HEAD_DIM), lambda i: (0, idx(i), 0, 0))
    return nb, tok, mat, state


def _gdn_scan(qn, kn, cv, be, gc, tinv, amat):
    t = qn.shape[0]
    nch = t // CHUNK

    def body(q_ref, k_ref, v_ref, b_ref, gc_ref, t_ref, a_ref, o_ref, sall_ref, vn_ref, s_scr):
        @pl.when(pl.program_id(0) == 0)
        def _():
            s_scr[...] = jnp.zeros_like(s_scr)

        heads = range(N_GDN_HEADS)
        cols = [slice(hd * LANES, (hd + 1) * LANES) for hd in heads]
        s = [s_scr[hd] for hd in heads]
        for cc in range(GDN_BLOCK_CHUNKS):
            rs = slice(cc * CHUNK, (cc + 1) * CHUNK)
            terms = [_gdn_chunk_terms(q_ref[rs, cs], k_ref[rs, cs], v_ref[rs, cs], b_ref[rs, cs], gc_ref[rs, cs])
                     for cs in cols]
            for hd in heads:
                sall_ref[hd, cc] = s[hd]
            uw = [_mm(t_ref[hd, cc], jnp.concatenate([terms[hd][4], terms[hd][5]], axis=1)) for hd in heads]
            ws_qs = [_mm(jnp.concatenate([uw[hd][:, LANES:], terms[hd][6]], axis=0), s[hd]) for hd in heads]
            vn = [uw[hd][:, :LANES] - ws_qs[hd][:CHUNK] for hd in heads]
            a_vn = [_mm(a_ref[hd, cc], vn[hd]) for hd in heads]
            kd_vn = [_mm_tn(terms[hd][7], vn[hd]) for hd in heads]
            for hd in heads:
                vn_ref[rs, cols[hd]] = vn[hd]
                o_ref[rs, cols[hd]] = ws_qs[hd][CHUNK:] + a_vn[hd]
                s[hd] = s[hd] * terms[hd][2] + kd_vn[hd]
        for hd in heads:
            s_scr[hd] = s[hd]

    nb, tok, mat, state = _gdn_block_specs(t, False)
    return pl.pallas_call(
        body, name="gdn_scan", grid=(nb,),
        in_specs=[tok] * 5 + [mat, mat], out_specs=[tok, state, tok],
        out_shape=[_sds((t, D_GDN)), _sds((N_GDN_HEADS, nch, GDN_HEAD_DIM, GDN_HEAD_DIM)), _sds((t, D_GDN))],
        scratch_shapes=[pltpu.VMEM((N_GDN_HEADS, GDN_HEAD_DIM, GDN_HEAD_DIM), F32)],
        compiler_params=_params("arbitrary"),
    )(qn, kn, cv, be, gc, tinv, amat)


def _gdn_bwd(qn, kn, cv, be, gc, tinv, amat, s_all, vn_all, do, after):
    t = qn.shape[0]

    def body(q_ref, k_ref, v_ref, b_ref, gc_ref, t_ref, a_ref, sall_ref, vn_ref, do_ref,
             dq_ref, dk_ref, dv_ref, db_ref, dg_ref, ds_scr):
        @pl.when(pl.program_id(0) == 0)
        def _():
            ds_scr[...] = jnp.zeros_like(ds_scr)

        lastrow = _iota((CHUNK, LANES), 0) == CHUNK - 1
        heads = range(N_GDN_HEADS)
        cols = [slice(hd * LANES, (hd + 1) * LANES) for hd in heads]
        each = lambda fn: [fn(hd) for hd in heads]
        rows_cat = lambda x, y: jnp.concatenate([x, y], axis=0)
        lane_cat = lambda x, y: jnp.concatenate([x, y], axis=1)
        dsp = each(lambda hd: ds_scr[hd])
        for cc in reversed(range(GDN_BLOCK_CHUNKS)):
            rs = slice(cc * CHUNK, (cc + 1) * CHUNK)
            q = each(lambda hd: q_ref[rs, cols[hd]])
            k = each(lambda hd: k_ref[rs, cols[hd]])
            v = each(lambda hd: v_ref[rs, cols[hd]])
            b = each(lambda hd: b_ref[rs, cols[hd]])
            gcc = each(lambda hd: gc_ref[rs, cols[hd]])
            do_c = each(lambda hd: do_ref[rs, cols[hd]])
            vn = each(lambda hd: vn_ref[rs, cols[hd]])
            tn = each(lambda hd: t_ref[hd, cc])
            st = each(lambda hd: sall_ref[hd, cc])
            terms = each(lambda hd: _gdn_chunk_terms(q[hd], k[hd], v[hd], b[hd], gcc[hd]))
            eg, egl, gl, kb, vb, kbg, qd, kd = [[terms[hd][i] for hd in heads] for i in range(8)]
            w = each(lambda hd: _mm(tn[hd], kbg[hd]))
            a_do = each(lambda hd: _mm_tn(a_ref[hd, cc], do_c[hd]))
            kd_ds = each(lambda hd: _mm(kd[hd], dsp[hd]))
            da = each(lambda hd: _mm_nt(do_c[hd], vn[hd]))
            dkd = each(lambda hd: _mm_nt(vn[hd], dsp[hd]))
            by_k = each(lambda hd: _mm_nt(rows_cat(kb[hd], q[hd]), k[hd]))
            dgl = each(lambda hd: jnp.sum(jnp.sum(dsp[hd] * st[hd], axis=-1, keepdims=True), axis=0, keepdims=True))
            dvn = each(lambda hd: a_do[hd] + kd_ds[hd])
            do_dvn = each(lambda hd: rows_cat(do_c[hd], dvn[hd]))
            by_s = each(lambda hd: _mm_nt(do_dvn[hd], st[hd]))
            dqd = each(lambda hd: by_s[hd][:CHUNK])
            dvn_dw = each(lambda hd: lane_cat(dvn[hd], -by_s[hd][CHUNK:]))
            dsp = each(lambda hd: _mm_tn(rows_cat(qd[hd], -w[hd]), do_dvn[hd]) + gl[hd] * dsp[hd])
            dt = each(lambda hd: _mm_nt(dvn_dw[hd], lane_cat(vb[hd], kbg[hd])))
            by_t = each(lambda hd: _mm_tn(tn[hd], dvn_dw[hd]))
            tt_dt = each(lambda hd: _mm_tn(tn[hd], dt[hd]))
            dm_raw = each(lambda hd: _mm_nt(tt_dt[hd], tn[hd]))
            masks = each(lambda hd: _chunk_decay(gcc[hd]))
            dkk = each(lambda hd: jnp.where(masks[hd][1], -dm_raw[hd], 0.0) * masks[hd][0])
            dqk = each(lambda hd: da[hd] * masks[hd][0])
            dqk_dkk = each(lambda hd: rows_cat(dqk[hd], dkk[hd]))
            on_k = each(lambda hd: _mm(dqk_dkk[hd], k[hd]))
            dk_mm = each(lambda hd: _mm_tn(dqk_dkk[hd], rows_cat(q[hd], kb[hd])))
            for hd in heads:
                cs = cols[hd]
                dvb, dkbg = by_t[hd][:, :LANES], by_t[hd][:, LANES:]
                gmat = dkk[hd] * by_k[hd][:CHUNK] + dqk[hd] * by_k[hd][CHUNK:]
                dq_ref[rs, cs] = dqd[hd] * eg[hd] + on_k[hd][:CHUNK]
                dkb = on_k[hd][CHUNK:] + dkbg * eg[hd]
                dk_ref[rs, cs] = dkd[hd] * egl[hd] + dk_mm[hd] + dkb * b[hd]
                db = jnp.sum(dkb * k[hd], axis=-1, keepdims=True) + jnp.sum(dvb * v[hd], axis=-1, keepdims=True)
                db_ref[rs, cs] = jnp.broadcast_to(db, (CHUNK, LANES))
                dv_ref[rs, cs] = dvb * b[hd]
                dkd_kd = jnp.sum(dkd[hd] * kd[hd], axis=-1, keepdims=True)
                col_sums = jnp.sum(lane_cat(gmat, jnp.zeros_like(gmat)).T, axis=-1, keepdims=True)
                dgc = (jnp.sum(gmat, axis=-1, keepdims=True) - col_sums[:CHUNK]
                       + jnp.sum(dqd[hd] * qd[hd], axis=-1, keepdims=True)
                       + jnp.sum(dkbg * kbg[hd], axis=-1, keepdims=True) - dkd_kd)
                extra = jnp.sum(dkd_kd, axis=0, keepdims=True) + dgl[hd] * gl[hd]
                dg_ref[rs, cs] = dgc + jnp.where(lastrow, extra, 0.0)
        for hd in heads:
            ds_scr[hd] = dsp[hd]
        dg = dg_ref[...]
        row = _iota(dg.shape, 0)
        pos = row % CHUNK
        step = 1
        while step < CHUNK:
            dg = dg + jnp.where(pos < CHUNK - step, pltpu.roll(dg, dg.shape[0] - step, 0), 0.0)
            step *= 2
        dg_ref[...] = dg

    nb, tok, mat, state = _gdn_block_specs(t, True)
    return pl.pallas_call(
        _ordered(body), name="gdn_bwd", grid=(nb,),
        in_specs=[ANY_SPEC] + [tok] * 5 + [mat, mat, state, tok, tok], out_specs=[tok] * 5,
        out_shape=[_sds((t, D_GDN))] * 5,
        scratch_shapes=[pltpu.VMEM((N_GDN_HEADS, GDN_HEAD_DIM, GDN_HEAD_DIM), F32)],
        compiler_params=_params("arbitrary"),
    )(after, qn, kn, cv, be, gc, tinv, amat, s_all, vn_all, do)


def _gdn_bwd_conv(proj, convw, dqn, dkn, dcv, dproj):
    t = proj.shape[0]

    def body(xq_ref, xk_ref, xv_ref, wq_ref, wk_ref, wv_ref, dq_ref, dk_ref, dv_ref, _,
             dqkv_ref, dwq_ref, dwk_ref, dwv_ref):
        row = _iota((t, LANES), 0)

        def one(x_ref, w_ref, d_ref, k, dw_ref, scale):
            x = x_ref[...]
            w = w_ref[...]
            y = _conv(x, w, row)
            sg = _sigmoid(y)
            dc = d_ref[...]
            if scale is not None:
                c = y * sg
                r = lax.rsqrt(jnp.sum(c * c, axis=-1, keepdims=True) + EPS)
                ch = c * r
                dc = scale * r * (dc - ch * jnp.sum(dc * ch, axis=-1, keepdims=True))
            dy = dc * sg * (1.0 + y * (1.0 - sg))
            dqkv_ref[:, k * LANES:(k + 1) * LANES] = (
                w[3:4, :] * dy + w[2:3, :] * _shift_up(dy, 1, row) + w[1:2, :] * _shift_up(dy, 2, row)
                + w[0:1, :] * _shift_up(dy, 3, row)).astype(BF)
            for jj in range(CONV_K):
                xs = x if jj == CONV_K - 1 else _shift_down(x, CONV_K - 1 - jj, row)
                dw_ref[jj:jj + 1, :] = jnp.sum(dy * xs, axis=0, keepdims=True)

        one(xq_ref, wq_ref, dq_ref, 0, dwq_ref, GDN_QSCALE)
        one(xk_ref, wk_ref, dk_ref, 1, dwk_ref, 1.0)
        one(xv_ref, wv_ref, dv_ref, 2, dwv_ref, None)

    col, cw, _, qkv = _gdn_specs(t)
    return pl.pallas_call(
        body, name="gdn_bwd_conv", grid=(N_GDN_HEADS,),
        in_specs=[qkv(0), qkv(1), qkv(2), cw(0), cw(4), cw(8), col(0), col(0), col(0), ANY_SPEC],
        out_specs=[pl.BlockSpec((t, QKV), lambda h: (0, COL_GDN // QKV + h)), cw(0), cw(0), cw(0)],
        out_shape=[_sds(dproj.shape, BF)] + [_sds((CONV_K, D_GDN))] * 3,
        input_output_aliases={9: 0}, compiler_params=_params("parallel"),
    )(proj, proj, proj, convw, convw, convw, dqn, dkn, dcv, dproj)


def _mix_out(fox_n, gdn_o, proj, gnw, w_out, x, pmw, plw, after):
    t = x.shape[0]
    tm = min(MATMUL_BLOCK, t)

    def body(fn_ref, go_ref, gz_ref, gnw_ref, w_ref, x_ref, pmw_ref, plw_ref, x1_ref, h2_ref, mixed_ref, omix_ref,
             h2t_ref):
        omix_ref[:, 0:D_FOX] = fn_ref[...]
        for hd in range(N_GDN_HEADS):
            cs = slice(hd * LANES, (hd + 1) * LANES)
            go = go_ref[:, cs]
            r = lax.rsqrt(jnp.mean(go * go, axis=-1, keepdims=True) + EPS)
            gz = gz_ref[:, cs]
            omix_ref[:, D_FOX + hd * LANES:D_FOX + (hd + 1) * LANES] = (
                go * r * gnw_ref[...] * (gz * _sigmoid(gz))).astype(BF)
        mixed = jnp.dot(omix_ref[...], w_ref[...], preferred_element_type=F32)
        mixed_ref[...] = mixed
        r2 = lax.rsqrt(jnp.mean(mixed * mixed, axis=-1, keepdims=True) + EPS)
        x1 = x_ref[...] + mixed * r2 * pmw_ref[...]
        x1_ref[...] = x1
        r3 = lax.rsqrt(jnp.mean(x1 * x1, axis=-1, keepdims=True) + EPS)
        h2 = x1 * r3 * plw_ref[...]
        h2_ref[...] = h2.astype(BF)
        h2t_ref[...] = h2.T.astype(BF)

    tok = lambda w: pl.BlockSpec((tm, w), lambda i: (i, 0))
    vec = lambda w: pl.BlockSpec((1, w), lambda i: (0, 0))
    return pl.pallas_call(
        _ordered(body), name="mix_out", grid=(t // tm,),
        in_specs=[ANY_SPEC, tok(D_FOX), tok(D_GDN), pl.BlockSpec((tm, D_GDN), lambda i: (i, COL_GZ // D_GDN)), vec(LANES),
                  pl.BlockSpec((D_MODEL, D_MODEL), lambda i: (0, 0)), tok(D_MODEL), vec(D_MODEL), vec(D_MODEL)],
        out_specs=[tok(D_MODEL)] * 4 + [pl.BlockSpec((D_MODEL, tm), lambda i: (0, i))],
        out_shape=[_sds((t, D_MODEL)), _sds((t, D_MODEL), BF), _sds((t, D_MODEL)), _sds((t, D_MODEL), BF),
                   _sds((D_MODEL, t), BF)],
        compiler_params=_params("parallel"),
    )(after, fox_n, gdn_o, proj, gnw, w_out, x, pmw, plw)


def _out_bwd(dmixed, w_out, o_fox, gdn_o, proj, fnw, gnw, after):
    t = dmixed.shape[0]
    tm = min(MATMUL_BLOCK, t)

    def body(dm_ref, w_ref, of_ref, go_ref, gz_ref, fnw_ref, gnw_ref, dof_ref, dgo_ref, dgz_ref, dfw_ref, dgw_ref):
        i = pl.program_id(0)

        @pl.when(i == 0)
        def _():
            dfw_ref[...] = jnp.zeros_like(dfw_ref)
            dgw_ref[...] = jnp.zeros_like(dgw_ref)

        domix = _mm_nt(dm_ref[...], w_ref[...])
        first = _iota((1, LANES), 1) < FOX_HEAD_DIM
        dfw = jnp.zeros((1, LANES), F32)
        dgw = jnp.zeros((1, LANES), F32)
        for pr in range(N_FOX_HEADS // 2):
            cs = slice(pr * LANES, (pr + 1) * LANES)
            o = of_ref[:, cs]
            dfn = domix[:, cs]
            o2 = o * o
            s0 = jnp.sum(jnp.where(first, o2, 0.0), axis=-1, keepdims=True)
            s1 = jnp.sum(jnp.where(first, 0.0, o2), axis=-1, keepdims=True)
            r = lax.rsqrt(jnp.where(first, s0, s1) * (1.0 / FOX_HEAD_DIM) + EPS)
            oh = o * r
            dfw = dfw + jnp.sum(dfn * oh, axis=0, keepdims=True)
            doh = dfn * fnw_ref[...]
            pr_ = doh * oh
            m0 = jnp.sum(jnp.where(first, pr_, 0.0), axis=-1, keepdims=True)
            m1 = jnp.sum(jnp.where(first, 0.0, pr_), axis=-1, keepdims=True)
            dof_ref[:, cs] = r * (doh - oh * jnp.where(first, m0, m1) * (1.0 / FOX_HEAD_DIM))
        for hd in range(N_GDN_HEADS):
            cs = slice(hd * LANES, (hd + 1) * LANES)
            go = go_ref[:, cs]
            gz = gz_ref[:, cs]
            dgated = domix[:, D_FOX + hd * LANES:D_FOX + (hd + 1) * LANES]
            r = lax.rsqrt(jnp.mean(go * go, axis=-1, keepdims=True) + EPS)
            goh = go * r
            sg = _sigmoid(gz)
            sz = gz * sg
            gn = goh * gnw_ref[...]
            dgn = dgated * sz
            dgz_ref[:, cs] = (dgated * gn * sg * (1.0 + gz * (1.0 - sg))).astype(BF)
            dgw = dgw + jnp.sum(dgn * goh, axis=0, keepdims=True)
            dgh = dgn * gnw_ref[...]
            dgo_ref[:, cs] = r * (dgh - goh * jnp.mean(dgh * goh, axis=-1, keepdims=True))
        dfw_ref[...] += dfw + pltpu.roll(dfw, FOX_HEAD_DIM, 1)
        dgw_ref[...] += dgw

    tok = lambda w: pl.BlockSpec((tm, w), lambda i: (i, 0))
    vec = lambda w: pl.BlockSpec((1, w), lambda i: (0, 0))
    return pl.pallas_call(
        _ordered(body), name="out_bwd", grid=(t // tm,),
        in_specs=[ANY_SPEC, tok(D_MODEL), pl.BlockSpec((D_MODEL, D_MODEL), lambda i: (0, 0)), tok(D_FOX), tok(D_GDN),
                  pl.BlockSpec((tm, D_GDN), lambda i: (i, COL_GZ // D_GDN)), vec(LANES), vec(LANES)],
        out_specs=[tok(D_FOX), tok(D_GDN), pl.BlockSpec((tm, D_GDN), lambda i: (i, COL_GZ // D_GDN)), vec(LANES),
                   vec(LANES)],
        out_shape=[_sds((t, D_FOX)), _sds((t, D_GDN)), _sds((t, PROJ_W), BF), _sds((1, LANES)), _sds((1, LANES))],
        compiler_params=_params("arbitrary"),
    )(after, dmixed, w_out, o_fox, gdn_o, proj, fnw, gnw)


def _mlp_up(h2, w_upt):
    t = h2.shape[0]
    tm = min(MATMUL_BLOCK, t)

    def body(h_ref, w_ref, up_ref):
        up_ref[...] = lax.dot_general(h_ref[...], w_ref[...], (((1,), (1,)), ((), ())),
                                      preferred_element_type=F32).astype(BF)

    return pl.pallas_call(
        body, name="mlp_up", grid=(t // tm,),
        in_specs=[pl.BlockSpec((tm, D_MODEL), lambda i: (i, 0)), pl.BlockSpec((D_FF, D_MODEL), lambda i: (0, 0))],
        out_specs=pl.BlockSpec((tm, D_FF), lambda i: (i, 0)), out_shape=_sds((t, D_FF), BF),
        compiler_params=_params("parallel"),
    )(h2, w_upt)


def _mlp_down_loss(up, w_down, x1, pw, target):
    t = up.shape[0]
    tm = min(MATMUL_BLOCK, t)

    def body(up_ref, w_ref, x1_ref, pw_ref, tg_ref, dy_ref, dx2_ref, loss_ref, dpw_ref):
        i = pl.program_id(0)

        @pl.when(i == 0)
        def _():
            loss_ref[...] = jnp.zeros_like(loss_ref)
            dpw_ref[...] = jnp.zeros_like(dpw_ref)

        u = jnp.maximum(up_ref[...].astype(F32), 0.0)
        y = jnp.dot((u * u).astype(BF), w_ref[...], preferred_element_type=F32)
        r = lax.rsqrt(jnp.mean(y * y, axis=-1, keepdims=True) + EPS)
        yh = y * r
        pw = pw_ref[...]
        err = x1_ref[...] + yh * pw - tg_ref[...]
        part = jnp.sum(jnp.sum(err * err, axis=-1, keepdims=True), axis=0, keepdims=True) * (0.5 / D_MODEL)
        loss_ref[...] += jnp.broadcast_to(part, loss_ref.shape)
        dx2 = err * (1.0 / D_MODEL)
        dx2_ref[...] = dx2
        dpw_ref[...] += jnp.sum(dx2 * yh, axis=0, keepdims=True)
        dyh = dx2 * pw
        dy_ref[...] = (r * (dyh - yh * jnp.mean(dyh * yh, axis=-1, keepdims=True))).astype(BF)

    tok = lambda w: pl.BlockSpec((tm, w), lambda i: (i, 0))
    vec = lambda w: pl.BlockSpec((1, w), lambda i: (0, 0))
    return pl.pallas_call(
        body, name="mlp_down_loss", grid=(t // tm,),
        in_specs=[tok(D_FF), pl.BlockSpec((D_FF, D_MODEL), lambda i: (0, 0)), tok(D_MODEL), vec(D_MODEL), tok(D_MODEL)],
        out_specs=[tok(D_MODEL), tok(D_MODEL), vec(LANES), vec(D_MODEL)],
        out_shape=[_sds((t, D_MODEL), BF), _sds((t, D_MODEL)), _sds((1, LANES)), _sds((1, D_MODEL))],
        compiler_params=_params("arbitrary"),
    )(up, w_down, x1, pw, target)


def _mlp_bwd_act(dy, w_down, up):
    t = dy.shape[0]
    tm = min(MATMUL_BLOCK, t)

    def body(dy_ref, w_ref, up_ref, dup_ref):
        da = lax.dot_general(dy_ref[...], w_ref[...], (((1,), (1,)), ((), ())), preferred_element_type=F32)
        dup_ref[...] = (da * (2.0 * jnp.maximum(up_ref[...].astype(F32), 0.0))).astype(BF)

    return pl.pallas_call(
        body, name="mlp_bwd_act", grid=(t // tm,),
        in_specs=[pl.BlockSpec((tm, D_MODEL), lambda i: (i, 0)), pl.BlockSpec((D_FF, D_MODEL), lambda i: (0, 0)),
                  pl.BlockSpec((tm, D_FF), lambda i: (i, 0))],
        out_specs=pl.BlockSpec((tm, D_FF), lambda i: (i, 0)), out_shape=_sds((t, D_FF), BF),
        compiler_params=_params("parallel"),
    )(dy, w_down, up)


def _mlp_bwd_in(dup, w_up, x1, plw, dx2, mixed, pmw, after):
    t = dup.shape[0]
    tm = min(MATMUL_BLOCK, t)

    def body(dup_ref, w_ref, x1_ref, plw_ref, dx2_ref, mx_ref, pmw_ref, dx1_ref, dmixed_ref, dplw_ref, dpmw_ref):
        i = pl.program_id(0)

        @pl.when(i == 0)
        def _():
            dplw_ref[...] = jnp.zeros_like(dplw_ref)
            dpmw_ref[...] = jnp.zeros_like(dpmw_ref)

        dh = jnp.dot(dup_ref[...], w_ref[...], preferred_element_type=F32)
        x1 = x1_ref[...]
        r = lax.rsqrt(jnp.mean(x1 * x1, axis=-1, keepdims=True) + EPS)
        xh = x1 * r
        dplw_ref[...] += jnp.sum(dh * xh, axis=0, keepdims=True)
        dxh = dh * plw_ref[...]
        dx1 = dx2_ref[...] + r * (dxh - xh * jnp.mean(dxh * xh, axis=-1, keepdims=True))
        dx1_ref[...] = dx1
        mx = mx_ref[...]
        r2 = lax.rsqrt(jnp.mean(mx * mx, axis=-1, keepdims=True) + EPS)
        mh = mx * r2
        dpmw_ref[...] += jnp.sum(dx1 * mh, axis=0, keepdims=True)
        dmh = dx1 * pmw_ref[...]
        dmixed_ref[...] = (r2 * (dmh - mh * jnp.mean(dmh * mh, axis=-1, keepdims=True))).astype(BF)

    tok = lambda w: pl.BlockSpec((tm, w), lambda i: (i, 0))
    vec = lambda w: pl.BlockSpec((1, w), lambda i: (0, 0))
    return pl.pallas_call(
        _ordered(body), name="mlp_bwd_in", grid=(t // tm,),
        in_specs=[ANY_SPEC, tok(D_FF), pl.BlockSpec((D_FF, D_MODEL), lambda i: (0, 0)), tok(D_MODEL),
                  vec(D_MODEL), tok(D_MODEL), tok(D_MODEL), vec(D_MODEL)],
        out_specs=[tok(D_MODEL), tok(D_MODEL), vec(D_MODEL), vec(D_MODEL)],
        out_shape=[_sds((t, D_MODEL)), _sds((t, D_MODEL), BF), _sds((1, D_MODEL)), _sds((1, D_MODEL))],
        compiler_params=_params("arbitrary"),
    )(after, dup, w_up, x1, plw, dx2, mixed, pmw)


def _wgrad(a, b, a_cols, split=1, a_fn=None, name="wgrad"):
    t, b_cols = b.shape
    n_a = a.shape[1] // a_cols

    def body(a_ref, b_ref, o_ref):
        av = a_ref[...]
        if a_fn is not None:
            av = a_fn(av)
        o_ref[...] = _mm_tn(av, b_ref[...]).astype(BF).reshape(o_ref.shape)

    return pl.pallas_call(
        body, name=name, grid=(n_a,),
        in_specs=[pl.BlockSpec((t, a_cols), lambda i: (0, i)), pl.BlockSpec((t, b_cols), lambda i: (0, 0))],
        out_specs=pl.BlockSpec((split, a_cols // split, b_cols), lambda i: (i, 0, 0)),
        out_shape=_sds((n_a * split, a_cols // split, b_cols), BF),
        compiler_params=_params("parallel"),
    )(a, b)


def _wgrad_pre_t(at, b, b_cols, name):
    rows, t = at.shape
    n_b = b.shape[1] // b_cols

    def body(a_ref, b_ref, o_ref):
        o_ref[0] = jnp.dot(a_ref[...], b_ref[...], preferred_element_type=F32).astype(BF)

    return pl.pallas_call(
        body, name=name, grid=(n_b,),
        in_specs=[pl.BlockSpec((rows, t), lambda j: (0, 0)), pl.BlockSpec((t, b_cols), lambda j: (0, j))],
        out_specs=pl.BlockSpec((1, rows, b_cols), lambda j: (j, 0, 0)), out_shape=_sds((n_b, rows, b_cols), BF),
        compiler_params=_params("parallel"),
    )(at, b)


def _small_bwd(proj, fb, al, dtb, dcq, dckt, dbe, dge, h, dproj):
    t = proj.shape[0]

    def body(sm_ref, fb_ref, al_ref, dtb_ref, dcq_ref, dckt_ref, dbe_ref, dge_ref, h_ref, _, dsm_ref, dvec_ref,
             gw_ref):
        s = sm_ref[...]
        lane = _iota((1, LANES), 1)
        dcum = dcq_ref[...] - dckt_ref[...].T
        row = _iota((t, LANES), 0)
        step = 1
        while step < t:
            dcum = dcum + _shift_up(dcum, step, row)
            step *= 2
        dff = dcum * _sigmoid(-(s + fb_ref[...]))
        dbeta = jnp.zeros((t, LANES), F32)
        dg = jnp.zeros((t, LANES), F32)
        for hd in range(N_GDN_HEADS):
            dbeta = jnp.where(lane == SM_GB + hd, dbe_ref[:, hd * LANES:hd * LANES + 1], dbeta)
            dg = jnp.where(lane == SM_GA + hd, dge_ref[:, hd * LANES:hd * LANES + 1], dg)
        beta = _sigmoid(s)
        dgb = dbeta * beta * (1.0 - beta)
        za = s + dtb_ref[...]
        nea = -jnp.exp(al_ref[...])
        dga = dg * nea * _sigmoid(za)
        is_f = lane < SM_GB
        is_b = (lane >= SM_GB) & (lane < SM_GA)
        is_a = (lane >= SM_GA) & (lane < SM_GA + 4)
        dsm = jnp.where(is_f, dff, jnp.where(is_b, dgb, jnp.where(is_a, dga, 0.0))).astype(BF)
        dsm_ref[...] = dsm
        gw_ref[0] = _mm_tn(dsm, h_ref[...]).astype(BF)
        dvec_ref[...] = jnp.zeros_like(dvec_ref)
        dvec_ref[0:1, :] = jnp.sum(jnp.where(is_f, dff, 0.0), axis=0, keepdims=True)
        dvec_ref[1:2, :] = jnp.sum(jnp.where(is_a, dg * nea * _softplus(za), 0.0), axis=0, keepdims=True)
        dvec_ref[2:3, :] = jnp.sum(jnp.where(is_a, dga, 0.0), axis=0, keepdims=True)

    vec = pl.BlockSpec((1, LANES), lambda i: (0, 0))
    full = lambda r, c: pl.BlockSpec((r, c), lambda i: (0, 0))
    small = pl.BlockSpec((t, LANES), lambda i: (0, COL_SMALL // LANES))
    return pl.pallas_call(
        body, name="small_bwd", grid=(1,),
        in_specs=[small, vec, vec, vec, full(t, LANES), full(LANES, t), full(t, 512), full(t, 512), full(t, D_MODEL),
                  ANY_SPEC],
        out_specs=[small, full(8, LANES), pl.BlockSpec((1, LANES, D_MODEL), lambda i: (0, 0, 0))],
        out_shape=[_sds(dproj.shape, BF), _sds((8, LANES)), _sds((1, LANES, D_MODEL), BF)],
        input_output_aliases={9: 0}, compiler_params=_params("arbitrary"),
    )(proj, fb, al, dtb, dcq, dckt, dbe, dge, h, dproj)


def _in_bwd(dproj, wt_al, x, nw, dx1, after):
    t = x.shape[0]
    tm = min(MATMUL_BLOCK, t)

    def body(dp_ref, w_ref, x_ref, nw_ref, dx1_ref, dx_ref, dnw_ref):
        i = pl.program_id(0)

        @pl.when(i == 0)
        def _():
            dnw_ref[...] = jnp.zeros_like(dnw_ref)

        dh = jnp.dot(dp_ref[...], w_ref[...], preferred_element_type=F32)
        xv = x_ref[...]
        r = lax.rsqrt(jnp.mean(xv * xv, axis=-1, keepdims=True) + EPS)
        xh = xv * r
        dnw_ref[...] += jnp.sum(dh * xh, axis=0, keepdims=True)
        dxh = dh * nw_ref[...]
        dx_ref[...] = dx1_ref[...] + r * (dxh - xh * jnp.mean(dxh * xh, axis=-1, keepdims=True))

    tok = lambda w: pl.BlockSpec((tm, w), lambda i: (i, 0))
    vec = lambda w: pl.BlockSpec((1, w), lambda i: (0, 0))
    return pl.pallas_call(
        _ordered(body), name="in_bwd", grid=(t // tm,),
        in_specs=[ANY_SPEC, tok(PROJ_W), pl.BlockSpec((PROJ_W, D_MODEL), lambda i: (0, 0)), tok(D_MODEL), vec(D_MODEL),
                  tok(D_MODEL)],
        out_specs=[tok(D_MODEL), vec(D_MODEL)], out_shape=[_sds((t, D_MODEL)), _sds((1, D_MODEL))],
        compiler_params=_params("arbitrary"),
    )(after, dproj, wt_al, x, nw, dx1)


def _row(v, width=None):
    v = v.reshape(1, -1).astype(F32)
    if width is not None and v.shape[1] < width:
        v = jnp.pad(v, ((0, 0), (0, width - v.shape[1])))
    return v


def _lane_vec(v, first):
    return jnp.pad(v.astype(F32), (first, LANES - first - v.shape[0])).reshape(1, LANES)


def _local_step(x, target, wt_al, started, late_weights, on_grads, convw, pre_mix_norm, fox_f_bias, fox_out_norm,
                gdn_a_log, gdn_dt_bias, gdn_out_norm, post_mix_norm, pre_mlp_norm, post_mlp_norm):
    t = x.shape[0]
    nch = t // CHUNK
    nw, pmw, plw, pw = _row(pre_mix_norm), _row(post_mix_norm), _row(pre_mlp_norm), _row(post_mlp_norm)
    fb, al, dtb = _lane_vec(fox_f_bias, SM_FF), _lane_vec(gdn_a_log, SM_GA), _lane_vec(gdn_dt_bias, SM_GA)
    fnw = _row(jnp.tile(fox_out_norm, 2))
    gnw = _row(gdn_out_norm)

    proj, h = _norm_proj(x, nw, wt_al, started)
    cumt, beta, g = _small_prep(proj, fb, al, dtb)
    qn, kn, cv, gc, be, mmat, amat = _gdn_prep(proj, convw, beta, g)
    n_prob = N_GDN_HEADS * nch
    m3 = mmat.reshape(n_prob, CHUNK, CHUNK)
    if n_prob < LANES:
        m3 = jnp.pad(m3, ((0, LANES - n_prob), (0, 0), (0, 0)))
    tinv = _tri_inverse(m3)[:n_prob].reshape(N_GDN_HEADS, nch, CHUNK, CHUNK)
    gdn_o, s_all, vn_all = _gdn_scan(qn, kn, cv, be, gc, tinv, amat)
    token = late_weights("mlp_relay", gdn_o)
    o_fox, lse, fox_n = _fox_fwd(proj, cumt, fnw, token)
    w_out = late_weights("w_out", fox_n)
    x1, h2, mixed, omix, h2t = _mix_out(fox_n, gdn_o, proj, gnw, w_out, x, pmw, plw, token)
    w_up, w_down = late_weights("mlp", h2)
    up = _mlp_up(h2, w_up)
    dy, dx2, loss, d_pw = _mlp_down_loss(up, w_down, x1, pw, target)

    dup = _mlp_bwd_act(dy, w_down, up)
    relu2 = lambda u: jnp.square(jnp.maximum(u.astype(F32), 0.0))
    g_down = _wgrad(up, dy, D_FF // N_DEV, a_fn=relu2, name="wgrad_down")
    g_up = _wgrad_pre_t(h2t, dup, D_FF // N_DEV, name="wgrad_up")
    token = on_grads("mlp", (g_up, g_down))
    dx1, dmixed, d_plw, d_pmw = _mlp_bwd_in(dup, w_up, x1, plw, dx2, mixed, pmw, token)
    g_out = _wgrad(omix, dmixed, 512, split=4, name="wgrad_out")
    do_fox, dgo, dproj, d_fnw, d_gnw = _out_bwd(dmixed, w_out, o_fox, gdn_o, proj, fnw, gnw, g_out)
    token = on_grads("w_out", g_out)
    dqn, dkn, dcv, dbe, dge = _gdn_bwd(qn, kn, cv, be, gc, tinv, amat, s_all, vn_all, dgo, token)
    dproj, dcq, dckt = _fox_bwd(proj, cumt, lse, o_fox, do_fox, dproj, token)
    dproj, dwq, dwk, dwv = _gdn_bwd_conv(proj, convw, dqn, dkn, dcv, dproj)
    dproj, dvec, g_tail = _small_bwd(proj, fb, al, dtb, dcq, dckt, dbe, dge, h, dproj)
    g_main = _wgrad(dproj, h, WGRAD_IN_ROWS, name="wgrad_in")
    token = on_grads("w_in", (g_main, g_tail))
    grad_x, d_nw = _in_bwd(dproj, wt_al, x, nw, dx1, token)
    small = dict(norms=(d_nw, d_pmw, d_plw, d_pw), fox_out_norm=d_fnw, gdn_out_norm=d_gnw, loss=loss, vectors=dvec,
                 conv=(dwq, dwk, dwv))
    return grad_x, small


MESH_IDS = pl.DeviceIdType.MESH
CHIP_FLIPS = ((0, 0), (1, 0), (0, 1), (1, 1))


def _place():
    return lax.axis_index("x"), lax.axis_index("y"), lax.axis_index("c")


def _all_gather(blocks, later, dtype):
    n, k = len(blocks), len(later)

    def body(*refs):
        ins, shards, outs = refs[:n], refs[n:n + k], refs[n + k:2 * n + k]
        zones, to_send = refs[2 * n + k:2 * n + 2 * k], refs[2 * n + 2 * k:2 * n + 3 * k]
        stage_in, stage_out = refs[2 * n + 3 * k:2 * n + 4 * k], refs[2 * n + 4 * k:2 * n + 5 * k]
        send_sems, recv_sems, local_sems, late_sems = refs[2 * n + 5 * k:]
        x, y, c = _place()
        sibling = (x, y, 1 - c)
        chips = [(x ^ fx, y ^ fy) for fx, fy in CHIP_FLIPS[1:]]

        def slot(out, px, py, pc):
            return out.at[4 * px + 2 * py + pc]

        def copy(a, k, block, to, src=None):
            return pltpu.make_async_remote_copy(
                src_ref=slot(outs[a], *block) if src is None else src, dst_ref=slot(outs[a], *block),
                send_sem=send_sems.at[a, k], recv_sem=recv_sems.at[a, k], device_id=to, device_id_type=MESH_IDS)

        pending = []
        for a in range(n):
            mine = pltpu.make_async_copy(ins[a], slot(outs[a], x, y, c), local_sems.at[a])
            mine.start()
            pending.append(mine)
        sends = []
        for a in range(n):
            first = [copy(a, 1 + j, (x, y, c), (*chip, c), src=ins[a]) for j, chip in enumerate(chips)][::-1]
            first.append(copy(a, 0, (x, y, c), sibling, src=ins[a]))
            for cp in first:
                cp.start()
            sends += first
        loads = [pltpu.make_async_copy(shards[a], stage_in[a], late_sems.at[a, 0]) for a in range(k)]
        for cp in loads:
            cp.start()
        for a, (_, transposed) in enumerate(later):
            loads[a].wait()
            val = stage_in[a][...]
            stage_out[a][...] = (val.T if transposed else val).astype(dtype)
            for j, dst in enumerate((slot(zones[a], x, y, c), to_send[a])):
                cp = pltpu.make_async_copy(stage_out[a], dst, late_sems.at[a, 1 + j])
                cp.start()
                pending.append(cp)
        for a in range(n):
            for j, chip in reversed(list(enumerate(chips))):
                copy(a, 1 + j, (*chip, c), (x, y, c)).wait_recv()
                fwd = copy(a, 4 + j, (*chip, c), sibling)
                fwd.start()
                sends.append(fwd)
        for a in range(n):
            copy(a, 0, sibling, (x, y, c)).wait_recv()
            for j, chip in enumerate(chips):
                copy(a, 4 + j, (*chip, 1 - c), (x, y, c)).wait_recv()
        for cp in sends:
            cp.wait_send()
        for cp in pending:
            cp.wait()

    shapes = [s_.shape[::-1] if transposed else s_.shape for s_, transposed in later]
    out = pl.pallas_call(
        body, name="all_gather_weights", in_specs=[ANY_SPEC] * (n + k), out_specs=[ANY_SPEC] * (n + 2 * k),
        out_shape=[_sds((N_DEV,) + b.shape, b.dtype) for b in blocks] + [_sds((N_DEV,) + sh, dtype) for sh in shapes]
        + [_sds(sh, dtype) for sh in shapes],
        scratch_shapes=[pltpu.VMEM(s_.shape, s_.dtype) for s_, _ in later] + [pltpu.VMEM(sh, dtype) for sh in shapes]
        + [pltpu.SemaphoreType.DMA((n, 7)), pltpu.SemaphoreType.DMA((n, 7)), pltpu.SemaphoreType.DMA((n,)),
           pltpu.SemaphoreType.DMA((k, 3))],
        compiler_params=pltpu.CompilerParams(vmem_limit_bytes=VMEM_LIMIT, has_side_effects=True),
    )(*blocks, *[s_ for s_, _ in later])
    return out[:n], out[n:n + k], out[n + k:]


def _adamw(w, g, m, v):
    m = ADAM_B1 * m + (1.0 - ADAM_B1) * g
    v = ADAM_B2 * v + (1.0 - ADAM_B2) * (g * g)
    m_hat = m / (1.0 - ADAM_B1 ** ADAM_STEP)
    v_hat = v / (1.0 - ADAM_B2 ** ADAM_STEP)
    return -ADAM_LR * (m_hat / (jnp.sqrt(v_hat) + ADAM_EPS) + ADAM_WD * w), m, v


def _pair_reduce(g, name):
    _, r, c_ = g.shape
    n = len(CHIP_FLIPS)

    def body(g_ref, out_ref, sib_buf, send_sems, recv_sems):
        x, y, c = _place()
        chips = [(x ^ fx, y ^ fy) for fx, fy in CHIP_FLIPS]
        piece = lambda chip, core: g_ref.at[4 * chip[0] + 2 * chip[1] + core]
        copies = [pltpu.make_async_remote_copy(
            src_ref=piece(chip, 1 - c), dst_ref=sib_buf.at[j], send_sem=send_sems.at[j], recv_sem=recv_sems.at[j],
            device_id=(x, y, 1 - c), device_id_type=MESH_IDS) for j, chip in enumerate(chips)]
        for cp in copies:
            cp.start()
        for j, chip in enumerate(chips):
            copies[j].wait_recv()
            out_ref[j] = (piece(chip, c)[...].astype(F32) + sib_buf[j].astype(F32)).astype(BF)
        for cp in copies:
            cp.wait_send()

    return pl.pallas_call(
        body, name=name, in_specs=[VMEM_SPEC], out_specs=VMEM_SPEC, out_shape=_sds((n, r, c_), BF),
        scratch_shapes=[pltpu.VMEM((n, r, c_), BF), pltpu.SemaphoreType.DMA((n,)), pltpu.SemaphoreType.DMA((n,))],
        compiler_params=pltpu.CompilerParams(vmem_limit_bytes=VMEM_LIMIT, has_side_effects=True),
    )(g)


HBM_SPEC = pl.BlockSpec(memory_space=pltpu.HBM)
SEM_SPEC = pl.BlockSpec(memory_space=pltpu.SEMAPHORE)
DATAFLOW = pltpu.SideEffectType.DATAFLOW_SIDE_EFFECTING


def _peers():
    x, y, c = _place()
    return 4 * x + 2 * y + c, [(x ^ (k >> 2), y ^ ((k >> 1) & 1), c ^ (k & 1)) for k in range(1, N_DEV)]


def _peer_index(peer):
    return 4 * peer[0] + 2 * peer[1] + peer[2]


def _exchange_start(srcs, zones, pieces, name, chips=False):
    n = len(srcs)
    fresh = zones is None
    if fresh:
        slots = len(CHIP_FLIPS) if chips else N_DEV
        zones = [_sds((slots,) + (v.shape[1:] if pieces else v.shape), v.dtype) for v in srcs]
    n_in = n if fresh else 2 * n
    among_chips = list(chips) if isinstance(chips, (list, tuple)) else [chips] * n

    def body(*refs):
        ins, sems, token = refs[:n], refs[n_in:n_in + 2 * n], refs[-1]
        zs = refs[n_in + 3 * n:n_in + 4 * n] if fresh else refs[n:2 * n]
        me, peers = _peers()
        x, y, c = _place()
        for a in range(n):
            if among_chips[a] and pieces:
                routes = [((x ^ fx, y ^ fy, c), j, j) for j, (fx, fy) in enumerate(CHIP_FLIPS) if j]
            elif among_chips[a]:
                routes = [((x ^ fx, y ^ fy, c), None, me) for fx, fy in CHIP_FLIPS[1:]]
            else:
                routes = [(peer, _peer_index(peer) if pieces else None, me) for peer in peers]
            for peer, src_slot, dst_slot in routes:
                pltpu.make_async_remote_copy(
                    src_ref=ins[a] if src_slot is None else ins[a].at[src_slot], dst_ref=zs[a].at[dst_slot],
                    send_sem=sems[2 * a], recv_sem=sems[2 * a + 1], device_id=peer, device_id_type=MESH_IDS).start()
        token[...] = jnp.zeros_like(token)

    hbm = lambda v: pltpu.with_memory_space_constraint(v, pltpu.HBM)
    out = pl.pallas_call(
        body, name=name,
        out_shape=tuple([pltpu.SemaphoreType.DMA(())] * (2 * n) + [pltpu.HBM(v.shape, v.dtype) for v in srcs]
                        + [pltpu.HBM(z.shape, z.dtype) for z in zones] + [_sds((8, LANES))]),
        in_specs=[HBM_SPEC] * n_in, out_specs=tuple([SEM_SPEC] * (2 * n) + [HBM_SPEC] * (2 * n) + [VMEM_SPEC]),
        input_output_aliases={i: 2 * n + i for i in range(n_in)},
        compiler_params=pltpu.CompilerParams(has_side_effects=DATAFLOW),
    )(*[hbm(v) for v in srcs], *([] if fresh else [hbm(z) for z in zones]))
    return out[:2 * n], out[2 * n:3 * n], out[3 * n:4 * n], out[-1]


def _relay_start(zones, name):
    n = len(zones)

    def body(*refs):
        zs, sems, token = refs[:n], refs[n:3 * n], refs[-1]
        x, y, c = _place()
        for fx, fy in CHIP_FLIPS:
            slot = 4 * (x ^ fx) + 2 * (y ^ fy) + c
            for a in range(n):
                pltpu.make_async_remote_copy(
                    src_ref=zs[a].at[slot], dst_ref=zs[a].at[slot], send_sem=sems[2 * a], recv_sem=sems[2 * a + 1],
                    device_id=(x, y, 1 - c), device_id_type=MESH_IDS).start()
        token[...] = jnp.zeros_like(token)

    out = pl.pallas_call(
        body, name=name,
        out_shape=tuple([pltpu.SemaphoreType.DMA(())] * (2 * n) + [pltpu.HBM(z.shape, z.dtype) for z in zones]
                        + [_sds((8, LANES))]),
        in_specs=[HBM_SPEC] * n, out_specs=tuple([SEM_SPEC] * (2 * n) + [HBM_SPEC] * n + [VMEM_SPEC]),
        input_output_aliases={i: 2 * n + i for i in range(n)},
        compiler_params=pltpu.CompilerParams(has_side_effects=DATAFLOW),
    )(*[pltpu.with_memory_space_constraint(z, pltpu.HBM) for z in zones])
    return out[:2 * n], [], out[2 * n:3 * n], out[-1]


def _exchange_wait(sems, srcs, zones, after, name, chips=False, n_copies=None):
    n, n_src = len(zones), len(srcs)
    after = list(after) if isinstance(after, (list, tuple)) else [after]
    n_copies = n_copies or (len(CHIP_FLIPS) - 1 if chips else N_DEV - 1)

    def body(*refs):
        zs, sm = refs[n_src:n_src + n], refs[n_src + n:n_src + 3 * n]
        me, peers = _peers()
        for a in range(n):
            seven = zs[a].at[pl.ds(0, n_copies)]
            cp = pltpu.make_async_remote_copy(src_ref=seven, dst_ref=seven, send_sem=sm[2 * a], recv_sem=sm[2 * a + 1],
                                              device_id=peers[0], device_id_type=MESH_IDS)
            cp.wait_send()
            cp.wait_recv()

    out = pl.pallas_call(
        body, name=name, out_shape=tuple([pltpu.HBM(v.shape, v.dtype) for v in srcs] + [pltpu.HBM(z.shape, z.dtype) for z in zones]),
        in_specs=[HBM_SPEC] * (n_src + n) + [SEM_SPEC] * (2 * n) + [ANY_SPEC] * len(after),
        out_specs=tuple([HBM_SPEC] * (n_src + n)), input_output_aliases={i: i for i in range(n_src + n)},
        compiler_params=pltpu.CompilerParams(has_side_effects=DATAFLOW),
    )(*srcs, *zones, *sems, *after)
    return out[:n_src], out[n_src:]


def _sum_adamw(zone, own, w, m, v, name, chips=False):
    n_slots, r, c_ = zone.shape
    rb = next((b for b in (256, 128) if r % b == 0), r)

    def body(me_ref, z_ref, own_ref, w_ref, m_ref, v_ref, grad_ref, delta_ref, nm_ref, nv_ref):
        total = None
        for d in range(n_slots):
            part = jnp.where(me_ref[0] == d, own_ref[0], z_ref[d]).astype(F32)
            total = part if total is None else total + part
        grad_ref[...] = total
        delta_ref[...], nm_ref[...], nv_ref[...] = _adamw(w_ref[...], total, m_ref[...], v_ref[...])

    x, y, c = _place()
    mine = 0 * x if chips else 4 * x + 2 * y + c
    blk = pl.BlockSpec((rb, c_), lambda i, me_ref: (i, 0))
    return pl.pallas_call(
        body, name=name,
        grid_spec=pltpu.PrefetchScalarGridSpec(
            num_scalar_prefetch=1, grid=(r // rb,),
            in_specs=[pl.BlockSpec((n_slots, rb, c_), lambda i, me_ref: (0, i, 0)),
                      pl.BlockSpec((1, rb, c_), lambda i, me_ref: (me_ref[0], i, 0)), blk, blk, blk],
            out_specs=[blk] * 4),
        out_shape=[_sds((r, c_))] * 4, compiler_params=_params("parallel"),
    )(mine.astype(jnp.int32).reshape(1), zone, own, w, m, v)


SMALL_NORMS = ("pre_mix_norm", "post_mix_norm", "pre_mlp_norm", "post_mlp_norm")
SMALL_ORDER = SMALL_NORMS + ("fox_out_norm", "gdn_out_norm", "fox_f_bias", "gdn_a_log", "gdn_dt_bias", "gdn_conv_w")
CONV_SLAB_ROWS, CONV_SLAB_LANES = 8, 256


def _small_pack(small):
    def body(n0, n1, n2, n3, fnw_ref, gnw_ref, loss_ref, vec_ref, out_ref):
        out_ref[...] = jnp.zeros_like(out_ref)
        for i, ref in enumerate((n0, n1, n2, n3)):
            out_ref[i:i + 1, :] = ref[...]
        out_ref[4:5, 0:LANES] = fnw_ref[...]
        out_ref[4:5, LANES:2 * LANES] = gnw_ref[...]
        out_ref[4:5, 2 * LANES:3 * LANES] = loss_ref[...]
        out_ref[5:8, 0:LANES] = vec_ref[0:3, :]

    return pl.pallas_call(body, name="small_pack", in_specs=[VMEM_SPEC] * 8, out_specs=VMEM_SPEC,
                          out_shape=_sds((8, D_MODEL)))(*small["norms"], small["fox_out_norm"], small["gdn_out_norm"],
                                                        small["loss"], small["vectors"])


def _conv_slabs(dconv):
    blocks = dconv.reshape(CONV_K, N_DEV, -1).transpose(1, 0, 2)
    blocks = jnp.pad(blocks, ((0, 0), (0, CONV_SLAB_ROWS - CONV_K), (0, CONV_SLAB_LANES - blocks.shape[2])))
    return blocks.reshape(N_DEV * CONV_SLAB_ROWS, CONV_SLAB_LANES)


def _small_update(zone, conv_zone, own, own_conv, w, m, v):
    n = len(SMALL_ORDER)
    n_conv = w["gdn_conv_w"].shape[1]

    def body(me_ref, z_ref, zc_ref, own_ref, ownc_ref, *refs):
        params, loss_ref, outs, (tot, totc) = refs[:3 * n], refs[3 * n], refs[3 * n + 1:7 * n + 1], refs[-2:]
        total, total_c = None, None
        for d in range(N_DEV):
            part = jnp.where(me_ref[0] == d, own_ref[...], z_ref[d])
            part_c = jnp.where(me_ref[0] == d, ownc_ref[...], zc_ref[d])
            total, total_c = (part, part_c) if d == 0 else (total + part, total_c + part_c)
        tot[...] = total
        totc[...] = total_c
        loss_ref[...] = tot[4, 2 * LANES:2 * LANES + 1]
        mine = totc[pl.ds(pl.multiple_of(me_ref[0] * CONV_SLAB_ROWS, CONV_SLAB_ROWS), CONV_SLAB_ROWS), :]
        g = dict(zip(SMALL_NORMS, (tot[0], tot[1], tot[2], tot[3])))
        g.update(fox_out_norm=tot[4, 0:FOX_HEAD_DIM], gdn_out_norm=tot[4, LANES:LANES + GDN_HEAD_DIM],
                 fox_f_bias=tot[5, SM_FF:SM_FF + N_FOX_HEADS], gdn_a_log=tot[6, SM_GA:SM_GA + N_GDN_HEADS],
                 gdn_dt_bias=tot[7, SM_GA:SM_GA + N_GDN_HEADS], gdn_conv_w=mine[0:CONV_K, 0:n_conv])
        for i, name in enumerate(SMALL_ORDER):
            w_ref, m_ref, v_ref = params[3 * i:3 * i + 3]
            outs[4 * i][...] = g[name]
            outs[4 * i + 1][...], outs[4 * i + 2][...], outs[4 * i + 3][...] = _adamw(w_ref[...], g[name], m_ref[...],
                                                                                     v_ref[...])

    x, y, c = _place()
    operands = [a[name] for name in SMALL_ORDER for a in (w, m, v)]
    out = pl.pallas_call(
        body, name="small_update",
        in_specs=[pl.BlockSpec(memory_space=pltpu.SMEM)] + [VMEM_SPEC] * (4 + 3 * n), out_specs=[VMEM_SPEC] * (1 + 4 * n),
        out_shape=[_sds((1,))] + [_sds(w[name].shape) for name in SMALL_ORDER for _ in range(4)],
        scratch_shapes=[pltpu.VMEM(zone.shape[1:], F32), pltpu.VMEM(conv_zone.shape[1:], F32)],
    )((4 * x + 2 * y + c).astype(jnp.int32).reshape(1), zone, conv_zone, own, own_conv, *operands)
    return out[0][0], {name: out[1 + 4 * i:5 + 4 * i] for i, name in enumerate(SMALL_ORDER)}


def _native_rows():
    groups = []
    for first, n_groups in ((0, N_FOX_HEADS // 2), (D_FOX * 3 + N_FOX_HEADS, N_GDN_HEADS)):
        for g in range(n_groups):
            groups += [(first + part * n_groups * LANES + g * LANES, first + part * n_groups * LANES + (g + 1) * LANES)
                       for part in range(3)]
    return tuple(groups) + ((3088, 3600), (1536, 1544), (3080, 3088))


NATIVE_ROWS = _native_rows()


W_IN_PIECE = D_PROJ // N_DEV
WGRAD_IN_ROWS = 512
SHUFFLE_LANES = 256


def _to_aligned_moves():
    moves, o = [], 0
    for lo, hi in NATIVE_ROWS:
        r = lo
        while r < hi:
            d = r // W_IN_PIECE
            k = min(hi, (d + 1) * W_IN_PIECE) - r
            moves.append((0, d, r - d * W_IN_PIECE, 0, o, k))
            r, o = r + k, o + k
    return moves


def _from_aligned_moves():
    moves = []
    for _, d, a, _, o, k in _to_aligned_moves():
        while k:
            n = min(k, WGRAD_IN_ROWS - o % WGRAD_IN_ROWS) if o < COL_SMALL else k
            moves.append((0, o // WGRAD_IN_ROWS, o % WGRAD_IN_ROWS, d, a, n) if o < COL_SMALL else
                         (1, 0, o - COL_SMALL, d, a, n))
            o, a, k = o + n, a + n, k - n
    return moves


def _shuffle_rows(srcs, moves, out_shape, name):
    c = srcs[0].shape[-1]

    def body(*refs):
        s_refs, o_ref, s_f, o_f = refs[:len(srcs)], refs[len(srcs)], refs[len(srcs) + 1:-1], refs[-1]
        for s_ref, f in zip(s_refs, s_f):
            f[...] = s_ref[...].astype(F32)
        o_f[...] = jnp.zeros_like(o_f)
        for i, ss, so, ds, do, k in moves:
            o_f[ds, pl.ds(do, k), :] = s_f[i][ss, pl.ds(so, k), :]
        o_ref[...] = o_f[...].astype(BF)

    blk = lambda shape: pl.BlockSpec(tuple(shape[:-1]) + (SHUFFLE_LANES,), lambda j: (0, 0, j))
    scratch = lambda shape: pltpu.VMEM(tuple(shape[:-1]) + (SHUFFLE_LANES,), F32)
    return pl.pallas_call(
        body, name=name, grid=(c // SHUFFLE_LANES,), in_specs=[blk(s.shape) for s in srcs], out_specs=blk(out_shape),
        out_shape=_sds(out_shape, BF), scratch_shapes=[scratch(s.shape) for s in srcs] + [scratch(out_shape)],
        compiler_params=_params("parallel"),
    )(*srcs)


def _cols_from_pieces(p):
    return p.transpose(1, 0, 2).reshape(p.shape[1], -1)


WEIGHT_ORDER = ("pre_mix_norm", "w_in", "fox_f_bias", "fox_out_norm", "gdn_conv_w", "gdn_a_log", "gdn_dt_bias",
                "gdn_out_norm", "w_out", "post_mix_norm", "pre_mlp_norm", "w_up", "w_down", "post_mlp_norm")


def kernel(x, pre_mix_norm, w_in, fox_f_bias, fox_out_norm, gdn_conv_w, gdn_a_log, gdn_dt_bias, gdn_out_norm, w_out, post_mix_norm, pre_mlp_norm, w_up, w_down, post_mlp_norm, loss_target, m_pre_mix_norm, m_w_in, m_fox_f_bias, m_fox_out_norm, m_gdn_conv_w, m_gdn_a_log, m_gdn_dt_bias, m_gdn_out_norm, m_w_out, m_post_mix_norm, m_pre_mlp_norm, m_w_up, m_w_down, m_post_mlp_norm, v_pre_mix_norm, v_w_in, v_fox_f_bias, v_fox_out_norm, v_gdn_conv_w, v_gdn_a_log, v_gdn_dt_bias, v_gdn_out_norm, v_w_out, v_post_mix_norm, v_pre_mlp_norm, v_w_up, v_w_down, v_post_mlp_norm):
    w = dict(pre_mix_norm=pre_mix_norm, w_in=w_in, fox_f_bias=fox_f_bias, fox_out_norm=fox_out_norm,
             gdn_conv_w=gdn_conv_w, gdn_a_log=gdn_a_log, gdn_dt_bias=gdn_dt_bias, gdn_out_norm=gdn_out_norm, w_out=w_out,
             post_mix_norm=post_mix_norm, pre_mlp_norm=pre_mlp_norm, w_up=w_up, w_down=w_down, post_mlp_norm=post_mlp_norm)
    mom = dict(pre_mix_norm=m_pre_mix_norm, w_in=m_w_in, fox_f_bias=m_fox_f_bias, fox_out_norm=m_fox_out_norm,
               gdn_conv_w=m_gdn_conv_w, gdn_a_log=m_gdn_a_log, gdn_dt_bias=m_gdn_dt_bias, gdn_out_norm=m_gdn_out_norm,
               w_out=m_w_out, post_mix_norm=m_post_mix_norm, pre_mlp_norm=m_pre_mlp_norm, w_up=m_w_up, w_down=m_w_down,
               post_mlp_norm=m_post_mlp_norm)
    var = dict(pre_mix_norm=v_pre_mix_norm, w_in=v_w_in, fox_f_bias=v_fox_f_bias, fox_out_norm=v_fox_out_norm,
               gdn_conv_w=v_gdn_conv_w, gdn_a_log=v_gdn_a_log, gdn_dt_bias=v_gdn_dt_bias, gdn_out_norm=v_gdn_out_norm,
               w_out=v_w_out, post_mix_norm=v_post_mix_norm, pre_mlp_norm=v_pre_mlp_norm, w_up=v_w_up, w_down=v_w_down,
               post_mlp_norm=v_post_mlp_norm)

    (win_g, conv_g), zones, shards = _all_gather([w_in.T.astype(BF), gdn_conv_w],
                                                 [(w_out, False), (w_up, True), (w_down, False)], BF)
    wt_al = _shuffle_rows([win_g], _to_aligned_moves(), (1, PROJ_W, D_MODEL), "w_in_to_aligned")[0]
    convw = _cols_from_pieces(conv_g)
    sems, shards, zones, after = _exchange_start(shards, zones, False, "gather_start", chips=True)
    gathers = dict(hop=(sems, shards, zones))

    def late_weights(name, after):
        if name == "mlp_relay":
            sems, shards, zones = gathers.pop("hop")
            _, zones = _exchange_wait(sems, shards, zones, after, "gather_wait", chips=True)
            sems, _, zones, token = _relay_start(zones, "gather_relay")
            gathers.update(w_out=(sems[:2], zones[:1]), mlp=(sems[2:], zones[1:]))
            return token
        sems, zones = gathers[name]
        _, got = _exchange_wait(sems, [], zones, after, "gather_" + name + "_done", n_copies=len(CHIP_FLIPS))
        if name == "w_out":
            return got[0].reshape(D_MODEL, D_MODEL)
        return got[0].reshape(D_FF, D_MODEL), got[1].reshape(D_FF, D_MODEL)

    scatters = {}

    def on_grads(name, g):
        if name == "mlp":
            scatters["mlp"] = list(g)
            return g[0]
        if name == "w_out":
            sems, srcs, zones, token = _exchange_start(scatters["mlp"] + [g], None, True, "scatter_mlp_w_out_start")
            scatters["mlp"] = (sems[:4], srcs[:2], zones[:2], token)
            scatters["w_out"] = (sems[4:], srcs[2:], zones[2:], token)
            return token
        g = _shuffle_rows(list(g), _from_aligned_moves(), (N_DEV, W_IN_PIECE, D_MODEL), "w_in_grad_from_aligned")
        scatters[name] = _exchange_start([_pair_reduce(g, "pair_reduce_w_in")], None, True, "scatter_w_in_start", chips=True)
        return scatters[name][3]

    grad_x, small = _local_step(
        x[0], loss_target[0], wt_al, after, late_weights, on_grads, convw, pre_mix_norm,
        fox_f_bias, fox_out_norm, gdn_a_log, gdn_dt_bias, gdn_out_norm, post_mix_norm, pre_mlp_norm, post_mlp_norm)
    slabs = [_small_pack(small), _conv_slabs(jnp.concatenate(small["conv"], axis=1))]
    scatters["small"] = _exchange_start(slabs, None, False, "small_start")

    grads, delta, new_m, new_v = {}, {}, {}, {}
    after = scatters["small"][3]
    for name, members in (("mlp", ("w_up", "w_down")), ("w_out", ("w_out",)), ("small", ()), ("w_in", ("w_in",))):
        sems, srcs, zones, _ = scatters[name]
        srcs, zones = _exchange_wait(sems, srcs, zones, after, "scatter_" + name + "_wait", chips=name == "w_in")
        if name == "small":
            loss, updated = _small_update(*zones, *srcs, w, mom, var)
            for n, res in updated.items():
                grads[n], delta[n], new_m[n], new_v[n] = res
            after = grads["pre_mix_norm"]
        for n, zone, own in zip(members, zones, srcs):
            if n == "w_in":
                res = _sum_adamw(zone, own, w[n].T, mom[n].T, var[n].T, "adamw_" + n, chips=True)
                grads[n], delta[n], new_m[n], new_v[n] = [r.T for r in res]
            else:
                grads[n], delta[n], new_m[n], new_v[n] = _sum_adamw(zone, own, w[n], mom[n], var[n], "adamw_" + n)
        if members:
            after = [grads[n] for n in members]

    return (loss, grad_x[None], *[grads[n] for n in WEIGHT_ORDER], *[delta[n] for n in WEIGHT_ORDER],
            *[new_m[n] for n in WEIGHT_ORDER], *[new_v[n] for n in WEIGHT_ORDER])
```

```python
import jax
import jax.numpy as jnp
from jax import lax
from jax.experimental import pallas as pl
from jax.experimental.pallas import tpu as pltpu

F32 = jnp.float32
BF = jnp.bfloat16

D_MODEL = 1024
N_FOX_HEADS, FOX_HEAD_DIM = 8, 64
N_GDN_HEADS, GDN_HEAD_DIM = 4, 128
D_FOX = N_FOX_HEADS * FOX_HEAD_DIM
D_GDN = N_GDN_HEADS * GDN_HEAD_DIM
CHUNK = 64
CONV_K = 4
D_FF = 4 * D_MODEL
EPS = 1e-6
D_PROJ = 3600
N_DEV = 8

PROJ_W = 3712
COL_FOX, COL_GDN, COL_GZ, COL_SMALL = 0, 1536, 3072, 3584
LANES = 128
QKV = 3 * LANES
SM_FF, SM_GB, SM_GA = 0, 8, 12

ADAM_LR, ADAM_B1, ADAM_B2, ADAM_EPS, ADAM_WD, ADAM_STEP = 0.001, 0.9, 0.999, 1e-08, 0.01, 10

TOKEN_BLOCK = 256
MATMUL_BLOCK = 512
TRI_ROWS = 4
FOX_SCALE = FOX_HEAD_DIM ** -0.5
GDN_QSCALE = GDN_HEAD_DIM ** -0.5
NEG_BIG = -1e30
VMEM_LIMIT = 56 * 1024 * 1024

VMEM_SPEC = pl.BlockSpec(memory_space=pltpu.VMEM)
ANY_SPEC = pl.BlockSpec(memory_space=pl.ANY)


def _sds(shape, dtype=F32):
    return jax.ShapeDtypeStruct(shape, dtype)


def _params(*sem):
    return pltpu.CompilerParams(dimension_semantics=sem if sem else None, vmem_limit_bytes=VMEM_LIMIT)


def _ordered(body):
    def ordered(_, *refs):
        body(*refs)

    return ordered


def _mm(a, b):
    return jnp.dot(a.astype(BF), b.astype(BF), preferred_element_type=F32)


def _mm_nt(a, b):
    return lax.dot_general(a.astype(BF), b.astype(BF), (((1,), (1,)), ((), ())), preferred_element_type=F32)


def _mm_tn(a, b):
    return lax.dot_general(a.astype(BF), b.astype(BF), (((0,), (0,)), ((), ())), preferred_element_type=F32)


def _sigmoid(x):
    return 1.0 / (1.0 + jnp.exp(-x))


def _softplus(x):
    return jnp.maximum(x, 0.0) + jnp.log1p(jnp.exp(-jnp.abs(x)))


def _iota(shape, dim):
    return lax.broadcasted_iota(jnp.int32, shape, dim)


def _shift_down(x, s, row):
    return jnp.where(row >= s, pltpu.roll(x, s, 0), 0.0)


def _shift_up(x, s, row):
    n = x.shape[0]
    return jnp.where(row < n - s, pltpu.roll(x, n - s, 0), 0.0)


def _norm_proj(x, nw, wt_al, after):
    t = x.shape[0]

    def body(x_ref, nw_ref, w_ref, proj_ref, h_ref):
        xv = x_ref[...]
        r = lax.rsqrt(jnp.mean(xv * xv, axis=-1, keepdims=True) + EPS)
        h = (xv * r * nw_ref[...]).astype(BF)
        h_ref[...] = h
        proj_ref[...] = lax.dot_general(h, w_ref[...], (((1,), (1,)), ((), ())), preferred_element_type=F32)

    tm = min(MATMUL_BLOCK, t)
    return pl.pallas_call(
        _ordered(body), name="norm_proj", grid=(t // tm,),
        in_specs=[ANY_SPEC, pl.BlockSpec((tm, D_MODEL), lambda i: (i, 0)), pl.BlockSpec((1, D_MODEL), lambda i: (0, 0)),
                  pl.BlockSpec((PROJ_W, D_MODEL), lambda i: (0, 0))],
        out_specs=[pl.BlockSpec((tm, PROJ_W), lambda i: (i, 0)), pl.BlockSpec((tm, D_MODEL), lambda i: (i, 0))],
        out_shape=[_sds((t, PROJ_W)), _sds((t, D_MODEL), BF)],
        compiler_params=_params("parallel"),
    )(after, x, nw, wt_al)


def _lane_column(x, lane):
    return jnp.sum(jnp.where(_iota((1, LANES), 1) == lane, x, 0.0), axis=-1, keepdims=True)


def _small_prep(proj, fb, al, dtb):
    t = proj.shape[0]

    def body(sm_ref, fb_ref, al_ref, dtb_ref, cumt_ref, beta_ref, g_ref):
        s = sm_ref[...]
        z = s + fb_ref[...]
        cum = jnp.minimum(z, 0.0) - jnp.log1p(jnp.exp(-jnp.abs(z)))
        row = _iota((t, LANES), 0)
        step = 1
        while step < t:
            cum = cum + _shift_down(cum, step, row)
            step *= 2
        cumt_ref[...] = cum.T
        beta_ref[...] = _sigmoid(s)
        g_ref[...] = -jnp.exp(al_ref[...]) * _softplus(s + dtb_ref[...])

    vec = pl.BlockSpec((1, LANES), lambda i: (0, 0))
    tok = pl.BlockSpec((t, LANES), lambda i: (0, 0))
    return pl.pallas_call(
        body, name="small_prep", grid=(1,),
        in_specs=[pl.BlockSpec((t, LANES), lambda i: (0, COL_SMALL // LANES)), vec, vec, vec],
        out_specs=[pl.BlockSpec((LANES, t), lambda i: (0, 0)), tok, tok],
        out_shape=[_sds((LANES, t)), _sds((t, LANES)), _sds((t, LANES))],
        compiler_params=_params("arbitrary"),
    )(proj, fb, al, dtb)


def _fox_stack(x, first):
    return jnp.concatenate([jnp.where(first, x, 0.0), jnp.where(first, 0.0, x)], axis=0).astype(BF)


def _fox_unstack(y, first):
    n = y.shape[0] // 2
    return jnp.where(first, y[:n], y[n:])


def _fox_logits(q2_i, kb, cumt_ref, pair, i, tq):
    klen = (i + 1) * tq
    s = lax.dot_general(q2_i, kb[:klen], (((1,), (1,)), ((), ())), preferred_element_type=F32)
    upper = _iota((2 * tq, 1), 0) < tq
    s = s - jnp.where(upper, cumt_ref[pl.ds(2 * pair, 1), 0:klen], cumt_ref[pl.ds(2 * pair + 1, 1), 0:klen])
    causal = _iota((2 * tq, tq), 1) <= _iota((2 * tq, tq), 0) % tq
    parts = [(s[:, :klen - tq], 0, klen - tq)] if i else []
    return parts + [(jnp.where(causal, s[:, klen - tq:], NEG_BIG), klen - tq, klen)]


def _fox_fwd(proj, cumt, fnw, after):
    t = proj.shape[0]
    tq = min(TOKEN_BLOCK, t // 2)
    nq = t // tq

    def body(q_ref, k_ref, v_ref, cumt_ref, fnw_ref, o_ref, lse_ref, fn_ref):
        j = pl.program_id(0)
        first = _iota((1, LANES), 1) < FOX_HEAD_DIM
        kb = k_ref[...].astype(BF)
        vb = v_ref[...].astype(BF)
        for i in range(nq):
            rows = slice(i * tq, (i + 1) * tq)
            q2 = _fox_stack(q_ref[rows, :] * FOX_SCALE, first)
            parts = _fox_logits(q2, kb, cumt_ref, j, i, tq)
            m = jnp.max(parts[-1][0], axis=-1, keepdims=True)
            if i:
                m = jnp.maximum(m, jnp.max(parts[0][0], axis=-1, keepdims=True))
            l = jnp.zeros((2 * tq, 1), F32)
            o = jnp.zeros((2 * tq, LANES), F32)
            for s, lo, hi in parts:
                p = jnp.exp(s - m)
                l = l + jnp.sum(p, axis=-1, keepdims=True)
                o = o + jnp.dot(p.astype(BF), vb[lo:hi], preferred_element_type=F32)
            o_acc = _fox_unstack(o / l, first)
            lse_acc = _fox_unstack(jnp.broadcast_to(m + jnp.log(l), (2 * tq, LANES)), first)
            o_ref[rows, :] = o_acc
            lse_ref[rows, :] = lse_acc
            o2 = o_acc * o_acc
            s0 = jnp.sum(jnp.where(first, o2, 0.0), axis=-1, keepdims=True)
            s1 = jnp.sum(jnp.where(first, 0.0, o2), axis=-1, keepdims=True)
            r = lax.rsqrt(jnp.where(first, s0, s1) * (1.0 / FOX_HEAD_DIM) + EPS)
            fn_ref[rows, :] = (o_acc * r * fnw_ref[...]).astype(BF)

    qkv = lambda k: pl.BlockSpec((t, LANES), lambda j: (0, COL_FOX // LANES + 3 * j + k))
    pair = pl.BlockSpec((t, LANES), lambda j: (0, j))
    return pl.pallas_call(
        _ordered(body), name="fox_fwd", grid=(N_FOX_HEADS // 2,),
        in_specs=[ANY_SPEC, qkv(0), qkv(1), qkv(2), pl.BlockSpec((LANES, t), lambda j: (0, 0)),
                  pl.BlockSpec((1, LANES), lambda j: (0, 0))],
        out_specs=[pair, pair, pair],
        out_shape=[_sds((t, D_FOX)), _sds((t, D_FOX)), _sds((t, D_FOX), BF)],
        compiler_params=_params("parallel"),
    )(after, proj, proj, proj, cumt, fnw)


def _fox_bwd(proj, cumt, lse, o, do, dproj, after):
    t = proj.shape[0]
    tq = min(TOKEN_BLOCK, t // 2)
    nq = t // tq

    def body(q_ref, k_ref, v_ref, cumt_ref, lse_ref, o_ref, do_ref, _, dqkv_ref, dcq_ref, dckt_ref, dk_s, dv_s):
        j = pl.program_id(0)

        @pl.when(j == 0)
        def _():
            dcq_ref[...] = jnp.zeros_like(dcq_ref)
            dckt_ref[...] = jnp.zeros_like(dckt_ref)

        lane = _iota((1, LANES), 1)

        first = _iota((1, LANES), 1) < FOX_HEAD_DIM
        kb = k_ref[...].astype(BF)
        vb = v_ref[...].astype(BF)
        dk_s[...] = jnp.zeros_like(dk_s)
        dv_s[...] = jnp.zeros_like(dv_s)
        for i in range(nq):
            rows = slice(i * tq, (i + 1) * tq)
            do_i = do_ref[rows, :]
            prod = do_i * o_ref[rows, :]
            lse_i = lse_ref[rows, :]
            q2 = _fox_stack(q_ref[rows, :] * FOX_SCALE, first)
            do2 = _fox_stack(do_i, first)
            delta = jnp.concatenate([jnp.sum(jnp.where(first, prod, 0.0), axis=-1, keepdims=True),
                                     jnp.sum(jnp.where(first, 0.0, prod), axis=-1, keepdims=True)], axis=0)
            lse2 = jnp.concatenate([lse_i[:, 0:1], lse_i[:, FOX_HEAD_DIM:FOX_HEAD_DIM + 1]], axis=0)
            dq2 = jnp.zeros((2 * tq, LANES), F32)
            dcq2 = jnp.zeros((2 * tq, 1), F32)
            for s, lo, hi in _fox_logits(q2, kb, cumt_ref, j, i, tq):
                p = jnp.exp(s - lse2)
                ds = p * (_mm_nt(do2, vb[lo:hi]) - delta)
                dsb = ds.astype(BF)
                dq2 = dq2 + jnp.dot(dsb, kb[lo:hi], preferred_element_type=F32)
                dk_s[lo:hi, :] += _mm_tn(dsb, q2)
                dv_s[lo:hi, :] += _mm_tn(p, do2)
                dcq2 = dcq2 + jnp.sum(ds, axis=-1, keepdims=True)
                dckt_ref[pl.ds(2 * j, 1), lo:hi] += jnp.sum(ds[:tq], axis=0, keepdims=True)
                dckt_ref[pl.ds(2 * j + 1, 1), lo:hi] += jnp.sum(ds[tq:], axis=0, keepdims=True)
            dqkv_ref[rows, 0:LANES] = (_fox_unstack(dq2, first) * FOX_SCALE).astype(BF)
            dcq_ref[rows, :] += jnp.where(lane == 2 * j, dcq2[:tq], jnp.where(lane == 2 * j + 1, dcq2[tq:], 0.0))
        dqkv_ref[:, LANES:2 * LANES] = dk_s[...].astype(BF)
        dqkv_ref[:, 2 * LANES:QKV] = dv_s[...].astype(BF)

    qkv = lambda k: pl.BlockSpec((t, LANES), lambda j: (0, COL_FOX // LANES + 3 * j + k))
    pair = pl.BlockSpec((t, LANES), lambda j: (0, j))
    rows128 = pl.BlockSpec((LANES, t), lambda j: (0, 0))
    return pl.pallas_call(
        _ordered(body), name="fox_bwd", grid=(N_FOX_HEADS // 2,),
        in_specs=[ANY_SPEC, qkv(0), qkv(1), qkv(2), rows128, pair, pair, pair, ANY_SPEC],
        out_specs=[pl.BlockSpec((t, QKV), lambda j: (0, COL_FOX // QKV + j)),
                   pl.BlockSpec((t, LANES), lambda j: (0, 0)), rows128],
        out_shape=[_sds(dproj.shape, BF), _sds((t, LANES)), _sds((LANES, t))],
        scratch_shapes=[pltpu.VMEM((t, LANES), F32), pltpu.VMEM((t, LANES), F32)],
        input_output_aliases={8: 0}, compiler_params=_params("arbitrary"),
    )(after, proj, proj, proj, cumt, lse, o, do, dproj)


def _conv(x, w, row):
    return (w[3:4, :] * x + w[2:3, :] * _shift_down(x, 1, row) + w[1:2, :] * _shift_down(x, 2, row)
            + w[0:1, :] * _shift_down(x, 3, row))


def _chunk_decay(gc_c):
    gi = gc_c[:, 0:CHUNK]
    gj = gc_c.T[0:CHUNK, :]
    ri = _iota((CHUNK, CHUNK), 0)
    cj = _iota((CHUNK, CHUNK), 1)
    return jnp.where(ri >= cj, jnp.exp(jnp.minimum(gi - gj, 0.0)), 0.0), ri > cj


def _gdn_specs(t):
    col = lambda off: pl.BlockSpec((t, LANES), lambda h: (0, off + h))
    cw = lambda off: pl.BlockSpec((CONV_K, LANES), lambda h: (0, off + h))
    mat = pl.BlockSpec((1, t // CHUNK, CHUNK, CHUNK), lambda h: (h, 0, 0, 0))
    qkv = lambda k: pl.BlockSpec((t, LANES), lambda h: (0, COL_GDN // LANES + 3 * h + k))
    return col, cw, mat, qkv


def _gdn_prep(proj, convw, beta, g):
    t = proj.shape[0]
    nch = t // CHUNK

    def body(xq_ref, xk_ref, xv_ref, wq_ref, wk_ref, wv_ref, beta_ref, g_ref,
             qn_ref, kn_ref, cv_ref, gc_ref, be_ref, m_ref, a_ref):
        row = _iota((t, LANES), 0)
        hd = pl.program_id(0)
        be_ref[...] = jnp.broadcast_to(_lane_column(beta_ref[...], SM_GB + hd), (t, LANES))

        def act(x_ref, w_ref):
            y = _conv(x_ref[...], w_ref[...], row)
            return y * _sigmoid(y)

        cq = act(xq_ref, wq_ref)
        ck = act(xk_ref, wk_ref)
        cv_ref[...] = act(xv_ref, wv_ref)
        qn_ref[...] = cq * lax.rsqrt(jnp.sum(cq * cq, axis=-1, keepdims=True) + EPS) * GDN_QSCALE
        kn_ref[...] = ck * lax.rsqrt(jnp.sum(ck * ck, axis=-1, keepdims=True) + EPS)
        gc = jnp.broadcast_to(_lane_column(g_ref[...], SM_GA + hd), (t, LANES))
        pos = row % CHUNK
        step = 1
        while step < CHUNK:
            gc = gc + jnp.where(pos >= step, pltpu.roll(gc, step, 0), 0.0)
            step *= 2
        gc_ref[...] = gc

        group = 4 if nch % 4 == 0 else 1

        def chunks(gi, carry):
            ns = [gi * group + c for c in range(group)]
            sls = [pl.ds(pl.multiple_of(n * CHUNK, CHUNK), CHUNK) for n in ns]
            ks = [kn_ref[sl, :] for sl in sls]
            kk = [_mm_nt(k_c * be_ref[sl, :], k_c) for k_c, sl in zip(ks, sls)]
            qk = [_mm_nt(qn_ref[sl, :], k_c) for k_c, sl in zip(ks, sls)]
            for c, n in enumerate(ns):
                decay, strict = _chunk_decay(gc_ref[sls[c], :])
                m_ref[0, n] = jnp.where(strict, kk[c] * decay, 0.0)
                a_ref[0, n] = qk[c] * decay
            return carry

        lax.fori_loop(0, nch // group, chunks, 0)

    col, cw, mat, qkv = _gdn_specs(t)
    return pl.pallas_call(
        body, name="gdn_prep", grid=(N_GDN_HEADS,),
        in_specs=[qkv(0), qkv(1), qkv(2), cw(0), cw(4), cw(8)] + [pl.BlockSpec((t, LANES), lambda h: (0, 0))] * 2,
        out_specs=[col(0), col(0), col(0), col(0), col(0), mat, mat],
        out_shape=[_sds((t, D_GDN))] * 5 + [_sds((N_GDN_HEADS, nch, CHUNK, CHUNK))] * 2,
        compiler_params=_params("parallel"),
    )(proj, proj, proj, convw, convw, convw, beta, g)


def _tri_inverse(m3):
    assert m3.shape == (LANES, CHUNK, CHUNK)

    def body(m_ref, t_ref, ms, ts):
        for i in range(CHUNK):
            ms[i * CHUNK:(i + 1) * CHUNK, :] = m_ref[:, i, :].T
        cidx = _iota((CHUNK, LANES), 0)

        def t_row(j):
            return ts[pl.ds(pl.multiple_of(j * CHUNK, CHUNK), CHUNK), :]

        def outer(ib, carry):
            i0 = ib * TRI_ROWS

            def inner(group, accs):
                for jj in range(TRI_ROWS):
                    jj = group * TRI_ROWS + jj
                    earlier = t_row(jj)
                    accs = tuple(acc - ms[pl.ds((i0 + r) * CHUNK + jj, 1), :] * earlier for r, acc in enumerate(accs))
                return accs

            accs = list(lax.fori_loop(
                0, ib, inner, tuple(jnp.where(cidx == i0 + r, 1.0, 0.0).astype(F32) for r in range(TRI_ROWS))))
            for r in range(TRI_ROWS):
                for q in range(r):
                    accs[r] = accs[r] - ms[pl.ds((i0 + r) * CHUNK + i0 + q, 1), :] * accs[q]
                ts[pl.ds(pl.multiple_of((i0 + r) * CHUNK, CHUNK), CHUNK), :] = accs[r]
            return carry

        lax.fori_loop(0, CHUNK // TRI_ROWS, outer, 0)
        for i in range(CHUNK):
            t_ref[:, i, :] = ts[i * CHUNK:(i + 1) * CHUNK, :].T

    return pl.pallas_call(
        body, name="tri_inverse", in_specs=[VMEM_SPEC], out_specs=VMEM_SPEC,
        out_shape=_sds((LANES, CHUNK, CHUNK)),
        scratch_shapes=[pltpu.VMEM((CHUNK * CHUNK, LANES), F32), pltpu.VMEM((CHUNK * CHUNK, LANES), F32)],
        compiler_params=_params(),
    )(m3)


def _gdn_chunk_terms(q, k, v, b, gcc):
    eg = jnp.exp(gcc)
    last = gcc[CHUNK - 1:CHUNK, :]
    egl = jnp.exp(last - gcc)
    gl = jnp.exp(last)
    kb = k * b
    return eg, egl, gl, kb, v * b, kb * eg, q * eg, k * egl


GDN_BLOCK_CHUNKS = 4


def _gdn_block_specs(t, reverse):
    cb = GDN_BLOCK_CHUNKS
    nb = t // (cb * CHUNK)
    idx = (lambda i: nb - 1 - i) if reverse else (lambda i: i)
    tok = pl.BlockSpec((cb * CHUNK, D_GDN), lambda i: (idx(i), 0))
    mat = pl.BlockSpec((N_GDN_HEADS, cb, CHUNK, CHUNK), lambda i: (0, idx(i), 0, 0))
    state = pl.BlockSpec((N_GDN_HEADS, cb, GDN_HEAD_DIM, GDN_HEAD_DIM), lambda i: (0, idx(i), 0, 0))
    return nb, tok, mat, state


def _gdn_scan(qn, kn, cv, be, gc, tinv, amat):
    t = qn.shape[0]
    nch = t // CHUNK

    def body(q_ref, k_ref, v_ref, b_ref, gc_ref, t_ref, a_ref, o_ref, sall_ref, vn_ref, s_scr):
        @pl.when(pl.program_id(0) == 0)
        def _():
            s_scr[...] = jnp.zeros_like(s_scr)

        heads = range(N_GDN_HEADS)
        cols = [slice(hd * LANES, (hd + 1) * LANES) for hd in heads]
        s = [s_scr[hd] for hd in heads]
        for cc in range(GDN_BLOCK_CHUNKS):
            rs = slice(cc * CHUNK, (cc + 1) * CHUNK)
            terms = [_gdn_chunk_terms(q_ref[rs, cs], k_ref[rs, cs], v_ref[rs, cs], b_ref[rs, cs], gc_ref[rs, cs])
                     for cs in cols]
            for hd in heads:
                sall_ref[hd, cc] = s[hd]
            uw = [_mm(t_ref[hd, cc], jnp.concatenate([terms[hd][4], terms[hd][5]], axis=1)) for hd in heads]
            ws_qs = [_mm(jnp.concatenate([uw[hd][:, LANES:], terms[hd][6]], axis=0), s[hd]) for hd in heads]
            vn = [uw[hd][:, :LANES] - ws_qs[hd][:CHUNK] for hd in heads]
            a_vn = [_mm(a_ref[hd, cc], vn[hd]) for hd in heads]
            kd_vn = [_mm_tn(terms[hd][7], vn[hd]) for hd in heads]
            for hd in heads:
                vn_ref[rs, cols[hd]] = vn[hd]
                o_ref[rs, cols[hd]] = ws_qs[hd][CHUNK:] + a_vn[hd]
                s[hd] = s[hd] * terms[hd][2] + kd_vn[hd]
        for hd in heads:
            s_scr[hd] = s[hd]

    nb, tok, mat, state = _gdn_block_specs(t, False)
    return pl.pallas_call(
        body, name="gdn_scan", grid=(nb,),
        in_specs=[tok] * 5 + [mat, mat], out_specs=[tok, state, tok],
        out_shape=[_sds((t, D_GDN)), _sds((N_GDN_HEADS, nch, GDN_HEAD_DIM, GDN_HEAD_DIM)), _sds((t, D_GDN))],
        scratch_shapes=[pltpu.VMEM((N_GDN_HEADS, GDN_HEAD_DIM, GDN_HEAD_DIM), F32)],
        compiler_params=_params("arbitrary"),
    )(qn, kn, cv, be, gc, tinv, amat)


def _gdn_bwd(qn, kn, cv, be, gc, tinv, amat, s_all, vn_all, do, after):
    t = qn.shape[0]

    def body(q_ref, k_ref, v_ref, b_ref, gc_ref, t_ref, a_ref, sall_ref, vn_ref, do_ref,
             dq_ref, dk_ref, dv_ref, db_ref, dg_ref, ds_scr):
        @pl.when(pl.program_id(0) == 0)
        def _():
            ds_scr[...] = jnp.zeros_like(ds_scr)

        lastrow = _iota((CHUNK, LANES), 0) == CHUNK - 1
        heads = range(N_GDN_HEADS)
        cols = [slice(hd * LANES, (hd + 1) * LANES) for hd in heads]
        each = lambda fn: [fn(hd) for hd in heads]
        rows_cat = lambda x, y: jnp.concatenate([x, y], axis=0)
        lane_cat = lambda x, y: jnp.concatenate([x, y], axis=1)
        dsp = each(lambda hd: ds_scr[hd])
        for cc in reversed(range(GDN_BLOCK_CHUNKS)):
            rs = slice(cc * CHUNK, (cc + 1) * CHUNK)
            q = each(lambda hd: q_ref[rs, cols[hd]])
            k = each(lambda hd: k_ref[rs, cols[hd]])
            v = each(lambda hd: v_ref[rs, cols[hd]])
            b = each(lambda hd: b_ref[rs, cols[hd]])
            gcc = each(lambda hd: gc_ref[rs, cols[hd]])
            do_c = each(lambda hd: do_ref[rs, cols[hd]])
            vn = each(lambda hd: vn_ref[rs, cols[hd]])
            tn = each(lambda hd: t_ref[hd, cc])
            st = each(lambda hd: sall_ref[hd, cc])
            terms = each(lambda hd: _gdn_chunk_terms(q[hd], k[hd], v[hd], b[hd], gcc[hd]))
            eg, egl, gl, kb, vb, kbg, qd, kd = [[terms[hd][i] for hd in heads] for i in range(8)]
            w = each(lambda hd: _mm(tn[hd], kbg[hd]))
            a_do = each(lambda hd: _mm_tn(a_ref[hd, cc], do_c[hd]))
            kd_ds = each(lambda hd: _mm(kd[hd], dsp[hd]))
            da = each(lambda hd: _mm_nt(do_c[hd], vn[hd]))
            dkd = each(lambda hd: _mm_nt(vn[hd], dsp[hd]))
            by_k = each(lambda hd: _mm_nt(rows_cat(kb[hd], q[hd]), k[hd]))
            dgl = each(lambda hd: jnp.sum(jnp.sum(dsp[hd] * st[hd], axis=-1, keepdims=True), axis=0, keepdims=True))
            dvn = each(lambda hd: a_do[hd] + kd_ds[hd])
            do_dvn = each(lambda hd: rows_cat(do_c[hd], dvn[hd]))
            by_s = each(lambda hd: _mm_nt(do_dvn[hd], st[hd]))
            dqd = each(lambda hd: by_s[hd][:CHUNK])
            dvn_dw = each(lambda hd: lane_cat(dvn[hd], -by_s[hd][CHUNK:]))
            dsp = each(lambda hd: _mm_tn(rows_cat(qd[hd], -w[hd]), do_dvn[hd]) + gl[hd] * dsp[hd])
            dt = each(lambda hd: _mm_nt(dvn_dw[hd], lane_cat(vb[hd], kbg[hd])))
            by_t = each(lambda hd: _mm_tn(tn[hd], dvn_dw[hd]))
            tt_dt = each(lambda hd: _mm_tn(tn[hd], dt[hd]))
            dm_raw = each(lambda hd: _mm_nt(tt_dt[hd], tn[hd]))
            masks = each(lambda hd: _chunk_decay(gcc[hd]))
            dkk = each(lambda hd: jnp.where(masks[hd][1], -dm_raw[hd], 0.0) * masks[hd][0])
            dqk = each(lambda hd: da[hd] * masks[hd][0])
            dqk_dkk = each(lambda hd: rows_cat(dqk[hd], dkk[hd]))
            on_k = each(lambda hd: _mm(dqk_dkk[hd], k[hd]))
            dk_mm = each(lambda hd: _mm_tn(dqk_dkk[hd], rows_cat(q[hd], kb[hd])))
            for hd in heads:
                cs = cols[hd]
                dvb, dkbg = by_t[hd][:, :LANES], by_t[hd][:, LANES:]
                gmat = dkk[hd] * by_k[hd][:CHUNK] + dqk[hd] * by_k[hd][CHUNK:]
                dq_ref[rs, cs] = dqd[hd] * eg[hd] + on_k[hd][:CHUNK]
                dkb = on_k[hd][CHUNK:] + dkbg * eg[hd]
                dk_ref[rs, cs] = dkd[hd] * egl[hd] + dk_mm[hd] + dkb * b[hd]
                db = jnp.sum(dkb * k[hd], axis=-1, keepdims=True) + jnp.sum(dvb * v[hd], axis=-1, keepdims=True)
                db_ref[rs, cs] = jnp.broadcast_to(db, (CHUNK, LANES))
                dv_ref[rs, cs] = dvb * b[hd]
                dkd_kd = jnp.sum(dkd[hd] * kd[hd], axis=-1, keepdims=True)
                col_sums = jnp.sum(lane_cat(gmat, jnp.zeros_like(gmat)).T, axis=-1, keepdims=True)
                dgc = (jnp.sum(gmat, axis=-1, keepdims=True) - col_sums[:CHUNK]
                       + jnp.sum(dqd[hd] * qd[hd], axis=-1, keepdims=True)
                       + jnp.sum(dkbg * kbg[hd], axis=-1, keepdims=True) - dkd_kd)
                extra = jnp.sum(dkd_kd, axis=0, keepdims=True) + dgl[hd] * gl[hd]
                dg_ref[rs, cs] = dgc + jnp.where(lastrow, extra, 0.0)
        for hd in heads:
            ds_scr[hd] = dsp[hd]
        dg = dg_ref[...]
        row = _iota(dg.shape, 0)
        pos = row % CHUNK
        step = 1
        while step < CHUNK:
            dg = dg + jnp.where(pos < CHUNK - step, pltpu.roll(dg, dg.shape[0] - step, 0), 0.0)
            step *= 2
        dg_ref[...] = dg

    nb, tok, mat, state = _gdn_block_specs(t, True)
    return pl.pallas_call(
        _ordered(body), name="gdn_bwd", grid=(nb,),
        in_specs=[ANY_SPEC] + [tok] * 5 + [mat, mat, state, tok, tok], out_specs=[tok] * 5,
        out_shape=[_sds((t, D_GDN))] * 5,
        scratch_shapes=[pltpu.VMEM((N_GDN_HEADS, GDN_HEAD_DIM, GDN_HEAD_DIM), F32)],
        compiler_params=_params("arbitrary"),
    )(after, qn, kn, cv, be, gc, tinv, amat, s_all, vn_all, do)


def _gdn_bwd_conv(proj, convw, dqn, dkn, dcv, dproj):
    t = proj.shape[0]

    def body(xq_ref, xk_ref, xv_ref, wq_ref, wk_ref, wv_ref, dq_ref, dk_ref, dv_ref, _,
             dqkv_ref, dwq_ref, dwk_ref, dwv_ref):
        row = _iota((t, LANES), 0)

        def one(x_ref, w_ref, d_ref, k, dw_ref, scale):
            x = x_ref[...]
            w = w_ref[...]
            y = _conv(x, w, row)
            sg = _sigmoid(y)
            dc = d_ref[...]
            if scale is not None:
                c = y * sg
                r = lax.rsqrt(jnp.sum(c * c, axis=-1, keepdims=True) + EPS)
                ch = c * r
                dc = scale * r * (dc - ch * jnp.sum(dc * ch, axis=-1, keepdims=True))
            dy = dc * sg * (1.0 + y * (1.0 - sg))
            dqkv_ref[:, k * LANES:(k + 1) * LANES] = (
                w[3:4, :] * dy + w[2:3, :] * _shift_up(dy, 1, row) + w[1:2, :] * _shift_up(dy, 2, row)
                + w[0:1, :] * _shift_up(dy, 3, row)).astype(BF)
            for jj in range(CONV_K):
                xs = x if jj == CONV_K - 1 else _shift_down(x, CONV_K - 1 - jj, row)
                dw_ref[jj:jj + 1, :] = jnp.sum(dy * xs, axis=0, keepdims=True)

        one(xq_ref, wq_ref, dq_ref, 0, dwq_ref, GDN_QSCALE)
        one(xk_ref, wk_ref, dk_ref, 1, dwk_ref, 1.0)
        one(xv_ref, wv_ref, dv_ref, 2, dwv_ref, None)

    col, cw, _, qkv = _gdn_specs(t)
    return pl.pallas_call(
        body, name="gdn_bwd_conv", grid=(N_GDN_HEADS,),
        in_specs=[qkv(0), qkv(1), qkv(2), cw(0), cw(4), cw(8), col(0), col(0), col(0), ANY_SPEC],
        out_specs=[pl.BlockSpec((t, QKV), lambda h: (0, COL_GDN // QKV + h)), cw(0), cw(0), cw(0)],
        out_shape=[_sds(dproj.shape, BF)] + [_sds((CONV_K, D_GDN))] * 3,
        input_output_aliases={9: 0}, compiler_params=_params("parallel"),
    )(proj, proj, proj, convw, convw, convw, dqn, dkn, dcv, dproj)


def _mix_out(fox_n, gdn_o, proj, gnw, w_out, x, pmw, plw, after):
    t = x.shape[0]
    tm = min(MATMUL_BLOCK, t)

    def body(fn_ref, go_ref, gz_ref, gnw_ref, w_ref, x_ref, pmw_ref, plw_ref, x1_ref, h2_ref, mixed_ref, omix_ref,
             h2t_ref):
        omix_ref[:, 0:D_FOX] = fn_ref[...]
        for hd in range(N_GDN_HEADS):
            cs = slice(hd * LANES, (hd + 1) * LANES)
            go = go_ref[:, cs]
            r = lax.rsqrt(jnp.mean(go * go, axis=-1, keepdims=True) + EPS)
            gz = gz_ref[:, cs]
            omix_ref[:, D_FOX + hd * LANES:D_FOX + (hd + 1) * LANES] = (
                go * r * gnw_ref[...] * (gz * _sigmoid(gz))).astype(BF)
        mixed = jnp.dot(omix_ref[...], w_ref[...], preferred_element_type=F32)
        mixed_ref[...] = mixed
        r2 = lax.rsqrt(jnp.mean(mixed * mixed, axis=-1, keepdims=True) + EPS)
        x1 = x_ref[...] + mixed * r2 * pmw_ref[...]
        x1_ref[...] = x1
        r3 = lax.rsqrt(jnp.mean(x1 * x1, axis=-1, keepdims=True) + EPS)
        h2 = x1 * r3 * plw_ref[...]
        h2_ref[...] = h2.astype(BF)
        h2t_ref[...] = h2.T.astype(BF)

    tok = lambda w: pl.BlockSpec((tm, w), lambda i: (i, 0))
    vec = lambda w: pl.BlockSpec((1, w), lambda i: (0, 0))
    return pl.pallas_call(
        _ordered(body), name="mix_out", grid=(t // tm,),
        in_specs=[ANY_SPEC, tok(D_FOX), tok(D_GDN), pl.BlockSpec((tm, D_GDN), lambda i: (i, COL_GZ // D_GDN)), vec(LANES),
                  pl.BlockSpec((D_MODEL, D_MODEL), lambda i: (0, 0)), tok(D_MODEL), vec(D_MODEL), vec(D_MODEL)],
        out_specs=[tok(D_MODEL)] * 4 + [pl.BlockSpec((D_MODEL, tm), lambda i: (0, i))],
        out_shape=[_sds((t, D_MODEL)), _sds((t, D_MODEL), BF), _sds((t, D_MODEL)), _sds((t, D_MODEL), BF),
                   _sds((D_MODEL, t), BF)],
        compiler_params=_params("parallel"),
    )(after, fox_n, gdn_o, proj, gnw, w_out, x, pmw, plw)


def _out_bwd(dmixed, w_out, o_fox, gdn_o, proj, fnw, gnw, after):
    t = dmixed.shape[0]
    tm = min(MATMUL_BLOCK, t)

    def body(dm_ref, w_ref, of_ref, go_ref, gz_ref, fnw_ref, gnw_ref, dof_ref, dgo_ref, dgz_ref, dfw_ref, dgw_ref):
        i = pl.program_id(0)

        @pl.when(i == 0)
        def _():
            dfw_ref[...] = jnp.zeros_like(dfw_ref)
            dgw_ref[...] = jnp.zeros_like(dgw_ref)

        domix = _mm_nt(dm_ref[...], w_ref[...])
        first = _iota((1, LANES), 1) < FOX_HEAD_DIM
        dfw = jnp.zeros((1, LANES), F32)
        dgw = jnp.zeros((1, LANES), F32)
        for pr in range(N_FOX_HEADS // 2):
            cs = slice(pr * LANES, (pr + 1) * LANES)
            o = of_ref[:, cs]
            dfn = domix[:, cs]
            o2 = o * o
            s0 = jnp.sum(jnp.where(first, o2, 0.0), axis=-1, keepdims=True)
            s1 = jnp.sum(jnp.where(first, 0.0, o2), axis=-1, keepdims=True)
            r = lax.rsqrt(jnp.where(first, s0, s1) * (1.0 / FOX_HEAD_DIM) + EPS)
            oh = o * r
            dfw = dfw + jnp.sum(dfn * oh, axis=0, keepdims=True)
            doh = dfn * fnw_ref[...]
            pr_ = doh * oh
            m0 = jnp.sum(jnp.where(first, pr_, 0.0), axis=-1, keepdims=True)
            m1 = jnp.sum(jnp.where(first, 0.0, pr_), axis=-1, keepdims=True)
            dof_ref[:, cs] = r * (doh - oh * jnp.where(first, m0, m1) * (1.0 / FOX_HEAD_DIM))
        for hd in range(N_GDN_HEADS):
            cs = slice(hd * LANES, (hd + 1) * LANES)
            go = go_ref[:, cs]
            gz = gz_ref[:, cs]
            dgated = domix[:, D_FOX + hd * LANES:D_FOX + (hd + 1) * LANES]
            r = lax.rsqrt(jnp.mean(go * go, axis=-1, keepdims=True) + EPS)
            goh = go * r
            sg = _sigmoid(gz)
            sz = gz * sg
            gn = goh * gnw_ref[...]
            dgn = dgated * sz
            dgz_ref[:, cs] = (dgated * gn * sg * (1.0 + gz * (1.0 - sg))).astype(BF)
            dgw = dgw + jnp.sum(dgn * goh, axis=0, keepdims=True)
            dgh = dgn * gnw_ref[...]
            dgo_ref[:, cs] = r * (dgh - goh * jnp.mean(dgh * goh, axis=-1, keepdims=True))
        dfw_ref[...] += dfw + pltpu.roll(dfw, FOX_HEAD_DIM, 1)
        dgw_ref[...] += dgw

    tok = lambda w: pl.BlockSpec((tm, w), lambda i: (i, 0))
    vec = lambda w: pl.BlockSpec((1, w), lambda i: (0, 0))
    return pl.pallas_call(
        _ordered(body), name="out_bwd", grid=(t // tm,),
        in_specs=[ANY_SPEC, tok(D_MODEL), pl.BlockSpec((D_MODEL, D_MODEL), lambda i: (0, 0)), tok(D_FOX), tok(D_GDN),
                  pl.BlockSpec((tm, D_GDN), lambda i: (i, COL_GZ // D_GDN)), vec(LANES), vec(LANES)],
        out_specs=[tok(D_FOX), tok(D_GDN), pl.BlockSpec((tm, D_GDN), lambda i: (i, COL_GZ // D_GDN)), vec(LANES),
                   vec(LANES)],
        out_shape=[_sds((t, D_FOX)), _sds((t, D_GDN)), _sds((t, PROJ_W), BF), _sds((1, LANES)), _sds((1, LANES))],
        compiler_params=_params("arbitrary"),
    )(after, dmixed, w_out, o_fox, gdn_o, proj, fnw, gnw)


WEIGHT_CHUNKS = 4


def _streamed_weight(w_hbm, w_ref, sems):
    rows = w_ref.shape[0] // WEIGHT_CHUNKS
    chunks = [pl.ds(c * rows, rows) for c in range(WEIGHT_CHUNKS)]
    copies = [pltpu.make_async_copy(w_hbm.at[ch], w_ref.at[ch], sems.at[c]) for c, ch in enumerate(chunks)]
    first = pl.program_id(0) == 0

    @pl.when(first)
    def _():
        for cp in copies:
            cp.start()

    for c in range(WEIGHT_CHUNKS):
        pl.when(first)(copies[c].wait)
        yield slice(c * rows, (c + 1) * rows)


def _mlp_up(h2, w_upt):
    t = h2.shape[0]
    tm = min(MATMUL_BLOCK, t)

    def body(h_ref, w_hbm, up_ref, w_ref, sems):
        for rows in _streamed_weight(w_hbm, w_ref, sems):
            up_ref[:, rows] = lax.dot_general(h_ref[...], w_ref[rows, :], (((1,), (1,)), ((), ())),
                                              preferred_element_type=F32).astype(BF)

    return pl.pallas_call(
        body, name="mlp_up", grid=(t // tm,),
        in_specs=[pl.BlockSpec((tm, D_MODEL), lambda i: (i, 0)), ANY_SPEC],
        out_specs=pl.BlockSpec((tm, D_FF), lambda i: (i, 0)), out_shape=_sds((t, D_FF), BF),
        scratch_shapes=[pltpu.VMEM((D_FF, D_MODEL), BF), pltpu.SemaphoreType.DMA((WEIGHT_CHUNKS,))],
        compiler_params=_params("arbitrary"),
    )(h2, w_upt)


def _mlp_down_loss(up, w_down, x1, pw, target):
    t = up.shape[0]
    tm = min(MATMUL_BLOCK, t)

    def body(up_ref, w_ref, x1_ref, pw_ref, tg_ref, dy_ref, dx2_ref, loss_ref, dpw_ref):
        i = pl.program_id(0)

        @pl.when(i == 0)
        def _():
            loss_ref[...] = jnp.zeros_like(loss_ref)
            dpw_ref[...] = jnp.zeros_like(dpw_ref)

        u = jnp.maximum(up_ref[...].astype(F32), 0.0)
        y = jnp.dot((u * u).astype(BF), w_ref[...], preferred_element_type=F32)
        r = lax.rsqrt(jnp.mean(y * y, axis=-1, keepdims=True) + EPS)
        yh = y * r
        pw = pw_ref[...]
        err = x1_ref[...] + yh * pw - tg_ref[...]
        part = jnp.sum(jnp.sum(err * err, axis=-1, keepdims=True), axis=0, keepdims=True) * (0.5 / D_MODEL)
        loss_ref[...] += jnp.broadcast_to(part, loss_ref.shape)
        dx2 = err * (1.0 / D_MODEL)
        dx2_ref[...] = dx2
        dpw_ref[...] += jnp.sum(dx2 * yh, axis=0, keepdims=True)
        dyh = dx2 * pw
        dy_ref[...] = (r * (dyh - yh * jnp.mean(dyh * yh, axis=-1, keepdims=True))).astype(BF)

    tok = lambda w: pl.BlockSpec((tm, w), lambda i: (i, 0))
    vec = lambda w: pl.BlockSpec((1, w), lambda i: (0, 0))
    return pl.pallas_call(
        body, name="mlp_down_loss", grid=(t // tm,),
        in_specs=[tok(D_FF), pl.BlockSpec((D_FF, D_MODEL), lambda i: (0, 0)), tok(D_MODEL), vec(D_MODEL), tok(D_MODEL)],
        out_specs=[tok(D_MODEL), tok(D_MODEL), vec(LANES), vec(D_MODEL)],
        out_shape=[_sds((t, D_MODEL), BF), _sds((t, D_MODEL)), _sds((1, LANES)), _sds((1, D_MODEL))],
        compiler_params=_params("arbitrary"),
    )(up, w_down, x1, pw, target)


def _mlp_bwd_act(dy, w_down, up):
    t = dy.shape[0]
    tm = min(MATMUL_BLOCK, t)

    def body(dy_ref, w_hbm, up_ref, dup_ref, w_ref, sems):
        for rows in _streamed_weight(w_hbm, w_ref, sems):
            da = lax.dot_general(dy_ref[...], w_ref[rows, :], (((1,), (1,)), ((), ())), preferred_element_type=F32)
            dup_ref[:, rows] = (da * (2.0 * jnp.maximum(up_ref[:, rows].astype(F32), 0.0))).astype(BF)

    return pl.pallas_call(
        body, name="mlp_bwd_act", grid=(t // tm,),
        in_specs=[pl.BlockSpec((tm, D_MODEL), lambda i: (i, 0)), ANY_SPEC, pl.BlockSpec((tm, D_FF), lambda i: (i, 0))],
        out_specs=pl.BlockSpec((tm, D_FF), lambda i: (i, 0)), out_shape=_sds((t, D_FF), BF),
        scratch_shapes=[pltpu.VMEM((D_FF, D_MODEL), BF), pltpu.SemaphoreType.DMA((WEIGHT_CHUNKS,))],
        compiler_params=_params("arbitrary"),
    )(dy, w_down, up)


def _mlp_bwd_in(dup, w_up, x1, plw, dx2, mixed, pmw, after):
    t = dup.shape[0]
    tm = min(MATMUL_BLOCK, t)

    def body(dup_ref, w_ref, x1_ref, plw_ref, dx2_ref, mx_ref, pmw_ref, dx1_ref, dmixed_ref, dplw_ref, dpmw_ref):
        i = pl.program_id(0)

        @pl.when(i == 0)
        def _():
            dplw_ref[...] = jnp.zeros_like(dplw_ref)
            dpmw_ref[...] = jnp.zeros_like(dpmw_ref)

        dh = jnp.dot(dup_ref[...], w_ref[...], preferred_element_type=F32)
        x1 = x1_ref[...]
        r = lax.rsqrt(jnp.mean(x1 * x1, axis=-1, keepdims=True) + EPS)
        xh = x1 * r
        dplw_ref[...] += jnp.sum(dh * xh, axis=0, keepdims=True)
        dxh = dh * plw_ref[...]
        dx1 = dx2_ref[...] + r * (dxh - xh * jnp.mean(dxh * xh, axis=-1, keepdims=True))
        dx1_ref[...] = dx1
        mx = mx_ref[...]
        r2 = lax.rsqrt(jnp.mean(mx * mx, axis=-1, keepdims=True) + EPS)
        mh = mx * r2
        dpmw_ref[...] += jnp.sum(dx1 * mh, axis=0, keepdims=True)
        dmh = dx1 * pmw_ref[...]
        dmixed_ref[...] = (r2 * (dmh - mh * jnp.mean(dmh * mh, axis=-1, keepdims=True))).astype(BF)

    tok = lambda w: pl.BlockSpec((tm, w), lambda i: (i, 0))
    vec = lambda w: pl.BlockSpec((1, w), lambda i: (0, 0))
    return pl.pallas_call(
        _ordered(body), name="mlp_bwd_in", grid=(t // tm,),
        in_specs=[ANY_SPEC, tok(D_FF), pl.BlockSpec((D_FF, D_MODEL), lambda i: (0, 0)), tok(D_MODEL),
                  vec(D_MODEL), tok(D_MODEL), tok(D_MODEL), vec(D_MODEL)],
        out_specs=[tok(D_MODEL), tok(D_MODEL), vec(D_MODEL), vec(D_MODEL)],
        out_shape=[_sds((t, D_MODEL)), _sds((t, D_MODEL), BF), _sds((1, D_MODEL)), _sds((1, D_MODEL))],
        compiler_params=_params("arbitrary"),
    )(after, dup, w_up, x1, plw, dx2, mixed, pmw)


def _wgrad(a, b, a_cols, split=1, a_fn=None, name="wgrad"):
    t, b_cols = b.shape
    n_a = a.shape[1] // a_cols

    def body(a_ref, b_ref, o_ref):
        av = a_ref[...]
        if a_fn is not None:
            av = a_fn(av)
        o_ref[...] = _mm_tn(av, b_ref[...]).astype(BF).reshape(o_ref.shape)

    return pl.pallas_call(
        body, name=name, grid=(n_a,),
        in_specs=[pl.BlockSpec((t, a_cols), lambda i: (0, i)), pl.BlockSpec((t, b_cols), lambda i: (0, 0))],
        out_specs=pl.BlockSpec((split, a_cols // split, b_cols), lambda i: (i, 0, 0)),
        out_shape=_sds((n_a * split, a_cols // split, b_cols), BF),
        compiler_params=_params("parallel"),
    )(a, b)


def _wgrad_pre_t(at, b, b_cols, name):
    rows, t = at.shape
    n_b = b.shape[1] // b_cols

    def body(a_ref, b_ref, o_ref):
        o_ref[0] = jnp.dot(a_ref[...], b_ref[...], preferred_element_type=F32).astype(BF)

    return pl.pallas_call(
        body, name=name, grid=(n_b,),
        in_specs=[pl.BlockSpec((rows, t), lambda j: (0, 0)), pl.BlockSpec((t, b_cols), lambda j: (0, j))],
        out_specs=pl.BlockSpec((1, rows, b_cols), lambda j: (j, 0, 0)), out_shape=_sds((n_b, rows, b_cols), BF),
        compiler_params=_params("parallel"),
    )(at, b)


def _small_bwd(proj, fb, al, dtb, dcq, dckt, dbe, dge, h, dproj):
    t = proj.shape[0]

    def body(sm_ref, fb_ref, al_ref, dtb_ref, dcq_ref, dckt_ref, dbe_ref, dge_ref, h_ref, _, dsm_ref, dvec_ref,
             gw_ref):
        s = sm_ref[...]
        lane = _iota((1, LANES), 1)
        dcum = dcq_ref[...] - dckt_ref[...].T
        row = _iota((t, LANES), 0)
        step = 1
        while step < t:
            dcum = dcum + _shift_up(dcum, step, row)
            step *= 2
        dff = dcum * _sigmoid(-(s + fb_ref[...]))
        dbeta = jnp.zeros((t, LANES), F32)
        dg = jnp.zeros((t, LANES), F32)
        for hd in range(N_GDN_HEADS):
            dbeta = jnp.where(lane == SM_GB + hd, dbe_ref[:, hd * LANES:hd * LANES + 1], dbeta)
            dg = jnp.where(lane == SM_GA + hd, dge_ref[:, hd * LANES:hd * LANES + 1], dg)
        beta = _sigmoid(s)
        dgb = dbeta * beta * (1.0 - beta)
        za = s + dtb_ref[...]
        nea = -jnp.exp(al_ref[...])
        dga = dg * nea * _sigmoid(za)
        is_f = lane < SM_GB
        is_b = (lane >= SM_GB) & (lane < SM_GA)
        is_a = (lane >= SM_GA) & (lane < SM_GA + 4)
        dsm = jnp.where(is_f, dff, jnp.where(is_b, dgb, jnp.where(is_a, dga, 0.0))).astype(BF)
        dsm_ref[...] = dsm
        gw_ref[0] = _mm_tn(dsm, h_ref[...]).astype(BF)
        dvec_ref[...] = jnp.zeros_like(dvec_ref)
        dvec_ref[0:1, :] = jnp.sum(jnp.where(is_f, dff, 0.0), axis=0, keepdims=True)
        dvec_ref[1:2, :] = jnp.sum(jnp.where(is_a, dg * nea * _softplus(za), 0.0), axis=0, keepdims=True)
        dvec_ref[2:3, :] = jnp.sum(jnp.where(is_a, dga, 0.0), axis=0, keepdims=True)

    vec = pl.BlockSpec((1, LANES), lambda i: (0, 0))
    full = lambda r, c: pl.BlockSpec((r, c), lambda i: (0, 0))
    small = pl.BlockSpec((t, LANES), lambda i: (0, COL_SMALL // LANES))
    return pl.pallas_call(
        body, name="small_bwd", grid=(1,),
        in_specs=[small, vec, vec, vec, full(t, LANES), full(LANES, t), full(t, 512), full(t, 512), full(t, D_MODEL),
                  ANY_SPEC],
        out_specs=[small, full(8, LANES), pl.BlockSpec((1, LANES, D_MODEL), lambda i: (0, 0, 0))],
        out_shape=[_sds(dproj.shape, BF), _sds((8, LANES)), _sds((1, LANES, D_MODEL), BF)],
        input_output_aliases={9: 0}, compiler_params=_params("arbitrary"),
    )(proj, fb, al, dtb, dcq, dckt, dbe, dge, h, dproj)


def _in_bwd(dproj, wt_al, x, nw, dx1, after):
    t = x.shape[0]
    tm = min(MATMUL_BLOCK, t)

    def body(dp_ref, w_ref, x_ref, nw_ref, dx1_ref, dx_ref, dnw_ref):
        i = pl.program_id(0)

        @pl.when(i == 0)
        def _():
            dnw_ref[...] = jnp.zeros_like(dnw_ref)

        dh = jnp.dot(dp_ref[...], w_ref[...], preferred_element_type=F32)
        xv = x_ref[...]
        r = lax.rsqrt(jnp.mean(xv * xv, axis=-1, keepdims=True) + EPS)
        xh = xv * r
        dnw_ref[...] += jnp.sum(dh * xh, axis=0, keepdims=True)
        dxh = dh * nw_ref[...]
        dx_ref[...] = dx1_ref[...] + r * (dxh - xh * jnp.mean(dxh * xh, axis=-1, keepdims=True))

    tok = lambda w: pl.BlockSpec((tm, w), lambda i: (i, 0))
    vec = lambda w: pl.BlockSpec((1, w), lambda i: (0, 0))
    return pl.pallas_call(
        _ordered(body), name="in_bwd", grid=(t // tm,),
        in_specs=[ANY_SPEC, tok(PROJ_W), pl.BlockSpec((PROJ_W, D_MODEL), lambda i: (0, 0)), tok(D_MODEL), vec(D_MODEL),
                  tok(D_MODEL)],
        out_specs=[tok(D_MODEL), vec(D_MODEL)], out_shape=[_sds((t, D_MODEL)), _sds((1, D_MODEL))],
        compiler_params=_params("arbitrary"),
    )(after, dproj, wt_al, x, nw, dx1)


def _row(v, width=None):
    v = v.reshape(1, -1).astype(F32)
    if width is not None and v.shape[1] < width:
        v = jnp.pad(v, ((0, 0), (0, width - v.shape[1])))
    return v


def _lane_vec(v, first):
    return jnp.pad(v.astype(F32), (first, LANES - first - v.shape[0])).reshape(1, LANES)


def _local_step(x, target, wt_al, started, late_weights, on_grads, convw, pre_mix_norm, fox_f_bias, fox_out_norm,
                gdn_a_log, gdn_dt_bias, gdn_out_norm, post_mix_norm, pre_mlp_norm, post_mlp_norm):
    t = x.shape[0]
    nch = t // CHUNK
    nw, pmw, plw, pw = _row(pre_mix_norm), _row(post_mix_norm), _row(pre_mlp_norm), _row(post_mlp_norm)
    fb, al, dtb = _lane_vec(fox_f_bias, SM_FF), _lane_vec(gdn_a_log, SM_GA), _lane_vec(gdn_dt_bias, SM_GA)
    fnw = _row(jnp.tile(fox_out_norm, 2))
    gnw = _row(gdn_out_norm)

    proj, h = _norm_proj(x, nw, wt_al, started)
    cumt, beta, g = _small_prep(proj, fb, al, dtb)
    qn, kn, cv, gc, be, mmat, amat = _gdn_prep(proj, convw, beta, g)
    n_prob = N_GDN_HEADS * nch
    m3 = mmat.reshape(n_prob, CHUNK, CHUNK)
    if n_prob < LANES:
        m3 = jnp.pad(m3, ((0, LANES - n_prob), (0, 0), (0, 0)))
    tinv = _tri_inverse(m3)[:n_prob].reshape(N_GDN_HEADS, nch, CHUNK, CHUNK)
    gdn_o, s_all, vn_all = _gdn_scan(qn, kn, cv, be, gc, tinv, amat)
    token = late_weights("mlp_relay", gdn_o)
    o_fox, lse, fox_n = _fox_fwd(proj, cumt, fnw, token)
    w_out = late_weights("w_out", fox_n)
    x1, h2, mixed, omix, h2t = _mix_out(fox_n, gdn_o, proj, gnw, w_out, x, pmw, plw, token)
    w_up, w_down = late_weights("mlp", h2)
    up = _mlp_up(h2, w_up)
    dy, dx2, loss, d_pw = _mlp_down_loss(up, w_down, x1, pw, target)

    dup = _mlp_bwd_act(dy, w_down, up)
    relu2 = lambda u: jnp.square(jnp.maximum(u.astype(F32), 0.0))
    g_down = _wgrad(up, dy, D_FF // N_DEV, a_fn=relu2, name="wgrad_down")
    g_up = _wgrad_pre_t(h2t, dup, D_FF // N_DEV, name="wgrad_up")
    token = on_grads("mlp", (g_up, g_down))
    dx1, dmixed, d_plw, d_pmw = _mlp_bwd_in(dup, w_up, x1, plw, dx2, mixed, pmw, token)
    g_out = _wgrad(omix, dmixed, 512, split=4, name="wgrad_out")
    do_fox, dgo, dproj, d_fnw, d_gnw = _out_bwd(dmixed, w_out, o_fox, gdn_o, proj, fnw, gnw, g_out)
    token = on_grads("w_out", g_out)
    dqn, dkn, dcv, dbe, dge = _gdn_bwd(qn, kn, cv, be, gc, tinv, amat, s_all, vn_all, dgo, token)
    dproj, dcq, dckt = _fox_bwd(proj, cumt, lse, o_fox, do_fox, dproj, token)
    dproj, dwq, dwk, dwv = _gdn_bwd_conv(proj, convw, dqn, dkn, dcv, dproj)
    dproj, dvec, g_tail = _small_bwd(proj, fb, al, dtb, dcq, dckt, dbe, dge, h, dproj)
    g_main = _wgrad(dproj, h, WGRAD_IN_ROWS, name="wgrad_in")
    token = on_grads("w_in", (g_main, g_tail))
    grad_x, d_nw = _in_bwd(dproj, wt_al, x, nw, dx1, token)
    small = dict(norms=(d_nw, d_pmw, d_plw, d_pw), fox_out_norm=d_fnw, gdn_out_norm=d_gnw, loss=loss, vectors=dvec,
                 conv=(dwq, dwk, dwv))
    return grad_x, small


MESH_IDS = pl.DeviceIdType.MESH
CHIP_FLIPS = ((0, 0), (1, 0), (0, 1), (1, 1))


def _place():
    return lax.axis_index("x"), lax.axis_index("y"), lax.axis_index("c")


def _all_gather(blocks, later, dtype):
    n, k = len(blocks), len(later)

    def body(*refs):
        ins, shards, outs = refs[:n], refs[n:n + k], refs[n + k:2 * n + k]
        zones, to_send = refs[2 * n + k:2 * n + 2 * k], refs[2 * n + 2 * k:2 * n + 3 * k]
        stage_in, stage_out = refs[2 * n + 3 * k:2 * n + 4 * k], refs[2 * n + 4 * k:2 * n + 5 * k]
        send_sems, recv_sems, local_sems, late_sems = refs[2 * n + 5 * k:]
        x, y, c = _place()
        sibling = (x, y, 1 - c)
        chips = [(x ^ fx, y ^ fy) for fx, fy in CHIP_FLIPS[1:]]

        def slot(out, px, py, pc):
            return out.at[4 * px + 2 * py + pc]

        def copy(a, k, block, to, src=None):
            return pltpu.make_async_remote_copy(
                src_ref=slot(outs[a], *block) if src is None else src, dst_ref=slot(outs[a], *block),
                send_sem=send_sems.at[a, k], recv_sem=recv_sems.at[a, k], device_id=to, device_id_type=MESH_IDS)

        pending = []
        for a in range(n):
            mine = pltpu.make_async_copy(ins[a], slot(outs[a], x, y, c), local_sems.at[a])
            mine.start()
            pending.append(mine)
        sends = []
        for a in range(n):
            first = [copy(a, 1 + j, (x, y, c), (*chip, c), src=ins[a]) for j, chip in enumerate(chips)][::-1]
            first.append(copy(a, 0, (x, y, c), sibling, src=ins[a]))
            for cp in first:
                cp.start()
            sends += first
        loads = [pltpu.make_async_copy(shards[a], stage_in[a], late_sems.at[a, 0]) for a in range(k)]
        for cp in loads:
            cp.start()
        for a, (_, transposed) in enumerate(later):
            loads[a].wait()
            val = stage_in[a][...]
            stage_out[a][...] = (val.T if transposed else val).astype(dtype)
            for j, dst in enumerate((slot(zones[a], x, y, c), to_send[a])):
                cp = pltpu.make_async_copy(stage_out[a], dst, late_sems.at[a, 1 + j])
                cp.start()
                pending.append(cp)
        for a in range(n):
            for j, chip in reversed(list(enumerate(chips))):
                copy(a, 1 + j, (*chip, c), (x, y, c)).wait_recv()
                fwd = copy(a, 4 + j, (*chip, c), sibling)
                fwd.start()
                sends.append(fwd)
        for a in range(n):
            copy(a, 0, sibling, (x, y, c)).wait_recv()
            for j, chip in enumerate(chips):
                copy(a, 4 + j, (*chip, 1 - c), (x, y, c)).wait_recv()
        for cp in sends:
            cp.wait_send()
        for cp in pending:
            cp.wait()

    shapes = [s_.shape[::-1] if transposed else s_.shape for s_, transposed in later]
    out = pl.pallas_call(
        body, name="all_gather_weights", in_specs=[ANY_SPEC] * (n + k), out_specs=[ANY_SPEC] * (n + 2 * k),
        out_shape=[_sds((N_DEV,) + b.shape, b.dtype) for b in blocks] + [_sds((N_DEV,) + sh, dtype) for sh in shapes]
        + [_sds(sh, dtype) for sh in shapes],
        scratch_shapes=[pltpu.VMEM(s_.shape, s_.dtype) for s_, _ in later] + [pltpu.VMEM(sh, dtype) for sh in shapes]
        + [pltpu.SemaphoreType.DMA((n, 7)), pltpu.SemaphoreType.DMA((n, 7)), pltpu.SemaphoreType.DMA((n,)),
           pltpu.SemaphoreType.DMA((k, 3))],
        compiler_params=pltpu.CompilerParams(vmem_limit_bytes=VMEM_LIMIT, has_side_effects=True),
    )(*blocks, *[s_ for s_, _ in later])
    return out[:n], out[n:n + k], out[n + k:]


def _adamw(w, g, m, v):
    m = ADAM_B1 * m + (1.0 - ADAM_B1) * g
    v = ADAM_B2 * v + (1.0 - ADAM_B2) * (g * g)
    m_hat = m / (1.0 - ADAM_B1 ** ADAM_STEP)
    v_hat = v / (1.0 - ADAM_B2 ** ADAM_STEP)
    return -ADAM_LR * (m_hat / (jnp.sqrt(v_hat) + ADAM_EPS) + ADAM_WD * w), m, v


def _pair_reduce(g, name):
    _, r, c_ = g.shape
    n = len(CHIP_FLIPS)

    def body(g_ref, out_ref, sib_buf, send_sems, recv_sems):
        x, y, c = _place()
        chips = [(x ^ fx, y ^ fy) for fx, fy in CHIP_FLIPS]
        piece = lambda chip, core: g_ref.at[4 * chip[0] + 2 * chip[1] + core]
        copies = [pltpu.make_async_remote_copy(
            src_ref=piece(chip, 1 - c), dst_ref=sib_buf.at[j], send_sem=send_sems.at[j], recv_sem=recv_sems.at[j],
            device_id=(x, y, 1 - c), device_id_type=MESH_IDS) for j, chip in enumerate(chips)]
        for cp in copies:
            cp.start()
        for j, chip in enumerate(chips):
            copies[j].wait_recv()
            out_ref[j] = (piece(chip, c)[...].astype(F32) + sib_buf[j].astype(F32)).astype(BF)
        for cp in copies:
            cp.wait_send()

    return pl.pallas_call(
        body, name=name, in_specs=[VMEM_SPEC], out_specs=VMEM_SPEC, out_shape=_sds((n, r, c_), BF),
        scratch_shapes=[pltpu.VMEM((n, r, c_), BF), pltpu.SemaphoreType.DMA((n,)), pltpu.SemaphoreType.DMA((n,))],
        compiler_params=pltpu.CompilerParams(vmem_limit_bytes=VMEM_LIMIT, has_side_effects=True),
    )(g)


HBM_SPEC = pl.BlockSpec(memory_space=pltpu.HBM)
SEM_SPEC = pl.BlockSpec(memory_space=pltpu.SEMAPHORE)
DATAFLOW = pltpu.SideEffectType.DATAFLOW_SIDE_EFFECTING


def _peers():
    x, y, c = _place()
    return 4 * x + 2 * y + c, [(x ^ (k >> 2), y ^ ((k >> 1) & 1), c ^ (k & 1)) for k in range(1, N_DEV)]


def _peer_index(peer):
    return 4 * peer[0] + 2 * peer[1] + peer[2]


def _exchange_start(srcs, zones, pieces, name, chips=False):
    n = len(srcs)
    fresh = zones is None
    if fresh:
        slots = len(CHIP_FLIPS) if chips else N_DEV
        zones = [_sds((slots,) + (v.shape[1:] if pieces else v.shape), v.dtype) for v in srcs]
    n_in = n if fresh else 2 * n
    among_chips = list(chips) if isinstance(chips, (list, tuple)) else [chips] * n

    def body(*refs):
        ins, sems, token = refs[:n], refs[n_in:n_in + 2 * n], refs[-1]
        zs = refs[n_in + 3 * n:n_in + 4 * n] if fresh else refs[n:2 * n]
        me, peers = _peers()
        x, y, c = _place()
        for a in range(n):
            if among_chips[a] and pieces:
                routes = [((x ^ fx, y ^ fy, c), j, j) for j, (fx, fy) in enumerate(CHIP_FLIPS) if j]
            elif among_chips[a]:
                routes = [((x ^ fx, y ^ fy, c), None, me) for fx, fy in CHIP_FLIPS[1:]]
            else:
                routes = [(peer, _peer_index(peer) if pieces else None, me) for peer in peers]
            for peer, src_slot, dst_slot in routes:
                pltpu.make_async_remote_copy(
                    src_ref=ins[a] if src_slot is None else ins[a].at[src_slot], dst_ref=zs[a].at[dst_slot],
                    send_sem=sems[2 * a], recv_sem=sems[2 * a + 1], device_id=peer, device_id_type=MESH_IDS).start()
        token[...] = jnp.zeros_like(token)

    hbm = lambda v: pltpu.with_memory_space_constraint(v, pltpu.HBM)
    out = pl.pallas_call(
        body, name=name,
        out_shape=tuple([pltpu.SemaphoreType.DMA(())] * (2 * n) + [pltpu.HBM(v.shape, v.dtype) for v in srcs]
                        + [pltpu.HBM(z.shape, z.dtype) for z in zones] + [_sds((8, LANES))]),
        in_specs=[HBM_SPEC] * n_in, out_specs=tuple([SEM_SPEC] * (2 * n) + [HBM_SPEC] * (2 * n) + [VMEM_SPEC]),
        input_output_aliases={i: 2 * n + i for i in range(n_in)},
        compiler_params=pltpu.CompilerParams(has_side_effects=DATAFLOW),
    )(*[hbm(v) for v in srcs], *([] if fresh else [hbm(z) for z in zones]))
    return out[:2 * n], out[2 * n:3 * n], out[3 * n:4 * n], out[-1]


def _relay_start(zones, name):
    n = len(zones)

    def body(*refs):
        zs, sems, token = refs[:n], refs[n:3 * n], refs[-1]
        x, y, c = _place()
        for fx, fy in CHIP_FLIPS:
            slot = 4 * (x ^ fx) + 2 * (y ^ fy) + c
            for a in range(n):
                pltpu.make_async_remote_copy(
                    src_ref=zs[a].at[slot], dst_ref=zs[a].at[slot], send_sem=sems[2 * a], recv_sem=sems[2 * a + 1],
                    device_id=(x, y, 1 - c), device_id_type=MESH_IDS).start()
        token[...] = jnp.zeros_like(token)

    out = pl.pallas_call(
        body, name=name,
        out_shape=tuple([pltpu.SemaphoreType.DMA(())] * (2 * n) + [pltpu.HBM(z.shape, z.dtype) for z in zones]
                        + [_sds((8, LANES))]),
        in_specs=[HBM_SPEC] * n, out_specs=tuple([SEM_SPEC] * (2 * n) + [HBM_SPEC] * n + [VMEM_SPEC]),
        input_output_aliases={i: 2 * n + i for i in range(n)},
        compiler_params=pltpu.CompilerParams(has_side_effects=DATAFLOW),
    )(*[pltpu.with_memory_space_constraint(z, pltpu.HBM) for z in zones])
    return out[:2 * n], [], out[2 * n:3 * n], out[-1]


def _exchange_wait(sems, srcs, zones, after, name, chips=False, n_copies=None):
    n, n_src = len(zones), len(srcs)
    after = list(after) if isinstance(after, (list, tuple)) else [after]
    n_copies = n_copies or (len(CHIP_FLIPS) - 1 if chips else N_DEV - 1)

    def body(*refs):
        zs, sm = refs[n_src:n_src + n], refs[n_src + n:n_src + 3 * n]
        me, peers = _peers()
        for a in range(n):
            seven = zs[a].at[pl.ds(0, n_copies)]
            cp = pltpu.make_async_remote_copy(src_ref=seven, dst_ref=seven, send_sem=sm[2 * a], recv_sem=sm[2 * a + 1],
                                              device_id=peers[0], device_id_type=MESH_IDS)
            cp.wait_send()
            cp.wait_recv()

    out = pl.pallas_call(
        body, name=name, out_shape=tuple([pltpu.HBM(v.shape, v.dtype) for v in srcs] + [pltpu.HBM(z.shape, z.dtype) for z in zones]),
        in_specs=[HBM_SPEC] * (n_src + n) + [SEM_SPEC] * (2 * n) + [ANY_SPEC] * len(after),
        out_specs=tuple([HBM_SPEC] * (n_src + n)), input_output_aliases={i: i for i in range(n_src + n)},
        compiler_params=pltpu.CompilerParams(has_side_effects=DATAFLOW),
    )(*srcs, *zones, *sems, *after)
    return out[:n_src], out[n_src:]


def _sum_adamw(zone, own, w, m, v, name, chips=False):
    n_slots, r, c_ = zone.shape
    rb = next((b for b in (256, 128) if r % b == 0), r)

    def body(me_ref, z_ref, own_ref, w_ref, m_ref, v_ref, grad_ref, delta_ref, nm_ref, nv_ref):
        total = None
        for d in range(n_slots):
            part = jnp.where(me_ref[0] == d, own_ref[0], z_ref[d]).astype(F32)
            total = part if total is None else total + part
        grad_ref[...] = total
        delta_ref[...], nm_ref[...], nv_ref[...] = _adamw(w_ref[...], total, m_ref[...], v_ref[...])

    x, y, c = _place()
    mine = 0 * x if chips else 4 * x + 2 * y + c
    blk = pl.BlockSpec((rb, c_), lambda i, me_ref: (i, 0))
    return pl.pallas_call(
        body, name=name,
        grid_spec=pltpu.PrefetchScalarGridSpec(
            num_scalar_prefetch=1, grid=(r // rb,),
            in_specs=[pl.BlockSpec((n_slots, rb, c_), lambda i, me_ref: (0, i, 0)),
                      pl.BlockSpec((1, rb, c_), lambda i, me_ref: (me_ref[0], i, 0)), blk, blk, blk],
            out_specs=[blk] * 4),
        out_shape=[_sds((r, c_))] * 4, compiler_params=_params("parallel"),
    )(mine.astype(jnp.int32).reshape(1), zone, own, w, m, v)


SMALL_NORMS = ("pre_mix_norm", "post_mix_norm", "pre_mlp_norm", "post_mlp_norm")
SMALL_ORDER = SMALL_NORMS + ("fox_out_norm", "gdn_out_norm", "fox_f_bias", "gdn_a_log", "gdn_dt_bias", "gdn_conv_w")
CONV_SLAB_ROWS, CONV_SLAB_LANES = 8, 256


def _small_pack(small):
    def body(n0, n1, n2, n3, fnw_ref, gnw_ref, loss_ref, vec_ref, out_ref):
        out_ref[...] = jnp.zeros_like(out_ref)
        for i, ref in enumerate((n0, n1, n2, n3)):
            out_ref[i:i + 1, :] = ref[...]
        out_ref[4:5, 0:LANES] = fnw_ref[...]
        out_ref[4:5, LANES:2 * LANES] = gnw_ref[...]
        out_ref[4:5, 2 * LANES:3 * LANES] = loss_ref[...]
        out_ref[5:8, 0:LANES] = vec_ref[0:3, :]

    return pl.pallas_call(body, name="small_pack", in_specs=[VMEM_SPEC] * 8, out_specs=VMEM_SPEC,
                          out_shape=_sds((8, D_MODEL)))(*small["norms"], small["fox_out_norm"], small["gdn_out_norm"],
                                                        small["loss"], small["vectors"])


def _conv_slabs(dconv):
    blocks = dconv.reshape(CONV_K, N_DEV, -1).transpose(1, 0, 2)
    blocks = jnp.pad(blocks, ((0, 0), (0, CONV_SLAB_ROWS - CONV_K), (0, CONV_SLAB_LANES - blocks.shape[2])))
    return blocks.reshape(N_DEV * CONV_SLAB_ROWS, CONV_SLAB_LANES)


def _small_update(zone, conv_zone, own, own_conv, w, m, v):
    n = len(SMALL_ORDER)
    n_conv = w["gdn_conv_w"].shape[1]

    def body(me_ref, z_ref, zc_ref, own_ref, ownc_ref, *refs):
        params, loss_ref, outs, (tot, totc) = refs[:3 * n], refs[3 * n], refs[3 * n + 1:7 * n + 1], refs[-2:]
        total, total_c = None, None
        for d in range(N_DEV):
            part = jnp.where(me_ref[0] == d, own_ref[...], z_ref[d])
            part_c = jnp.where(me_ref[0] == d, ownc_ref[...], zc_ref[d])
            total, total_c = (part, part_c) if d == 0 else (total + part, total_c + part_c)
        tot[...] = total
        totc[...] = total_c
        loss_ref[...] = tot[4, 2 * LANES:2 * LANES + 1]
        mine = totc[pl.ds(pl.multiple_of(me_ref[0] * CONV_SLAB_ROWS, CONV_SLAB_ROWS), CONV_SLAB_ROWS), :]
        g = dict(zip(SMALL_NORMS, (tot[0], tot[1], tot[2], tot[3])))
        g.update(fox_out_norm=tot[4, 0:FOX_HEAD_DIM], gdn_out_norm=tot[4, LANES:LANES + GDN_HEAD_DIM],
                 fox_f_bias=tot[5, SM_FF:SM_FF + N_FOX_HEADS], gdn_a_log=tot[6, SM_GA:SM_GA + N_GDN_HEADS],
                 gdn_dt_bias=tot[7, SM_GA:SM_GA + N_GDN_HEADS], gdn_conv_w=mine[0:CONV_K, 0:n_conv])
        for i, name in enumerate(SMALL_ORDER):
            w_ref, m_ref, v_ref = params[3 * i:3 * i + 3]
            outs[4 * i][...] = g[name]
            outs[4 * i + 1][...], outs[4 * i + 2][...], outs[4 * i + 3][...] = _adamw(w_ref[...], g[name], m_ref[...],
                                                                                     v_ref[...])

    x, y, c = _place()
    operands = [a[name] for name in SMALL_ORDER for a in (w, m, v)]
    out = pl.pallas_call(
        body, name="small_update",
        in_specs=[pl.BlockSpec(memory_space=pltpu.SMEM)] + [VMEM_SPEC] * (4 + 3 * n), out_specs=[VMEM_SPEC] * (1 + 4 * n),
        out_shape=[_sds((1,))] + [_sds(w[name].shape) for name in SMALL_ORDER for _ in range(4)],
        scratch_shapes=[pltpu.VMEM(zone.shape[1:], F32), pltpu.VMEM(conv_zone.shape[1:], F32)],
    )((4 * x + 2 * y + c).astype(jnp.int32).reshape(1), zone, conv_zone, own, own_conv, *operands)
    return out[0][0], {name: out[1 + 4 * i:5 + 4 * i] for i, name in enumerate(SMALL_ORDER)}


def _native_rows():
    groups = []
    for first, n_groups in ((0, N_FOX_HEADS // 2), (D_FOX * 3 + N_FOX_HEADS, N_GDN_HEADS)):
        for g in range(n_groups):
            groups += [(first + part * n_groups * LANES + g * LANES, first + part * n_groups * LANES + (g + 1) * LANES)
                       for part in range(3)]
    return tuple(groups) + ((3088, 3600), (1536, 1544), (3080, 3088))


NATIVE_ROWS = _native_rows()


W_IN_PIECE = D_PROJ // N_DEV
WGRAD_IN_ROWS = 512
SHUFFLE_LANES = 256


def _to_aligned_moves():
    moves, o = [], 0
    for lo, hi in NATIVE_ROWS:
        r = lo
        while r < hi:
            d = r // W_IN_PIECE
            k = min(hi, (d + 1) * W_IN_PIECE) - r
            moves.append((0, d, r - d * W_IN_PIECE, 0, o, k))
            r, o = r + k, o + k
    return moves


def _from_aligned_moves():
    moves = []
    for _, d, a, _, o, k in _to_aligned_moves():
        while k:
            n = min(k, WGRAD_IN_ROWS - o % WGRAD_IN_ROWS) if o < COL_SMALL else k
            moves.append((0, o // WGRAD_IN_ROWS, o % WGRAD_IN_ROWS, d, a, n) if o < COL_SMALL else
                         (1, 0, o - COL_SMALL, d, a, n))
            o, a, k = o + n, a + n, k - n
    return moves


def _shuffle_rows(srcs, moves, out_shape, name):
    c = srcs[0].shape[-1]

    def body(*refs):
        s_refs, o_ref, s_f, o_f = refs[:len(srcs)], refs[len(srcs)], refs[len(srcs) + 1:-1], refs[-1]
        for s_ref, f in zip(s_refs, s_f):
            f[...] = s_ref[...].astype(F32)
        o_f[...] = jnp.zeros_like(o_f)
        for i, ss, so, ds, do, k in moves:
            o_f[ds, pl.ds(do, k), :] = s_f[i][ss, pl.ds(so, k), :]
        o_ref[...] = o_f[...].astype(BF)

    blk = lambda shape: pl.BlockSpec(tuple(shape[:-1]) + (SHUFFLE_LANES,), lambda j: (0, 0, j))
    scratch = lambda shape: pltpu.VMEM(tuple(shape[:-1]) + (SHUFFLE_LANES,), F32)
    return pl.pallas_call(
        body, name=name, grid=(c // SHUFFLE_LANES,), in_specs=[blk(s.shape) for s in srcs], out_specs=blk(out_shape),
        out_shape=_sds(out_shape, BF), scratch_shapes=[scratch(s.shape) for s in srcs] + [scratch(out_shape)],
        compiler_params=_params("parallel"),
    )(*srcs)


def _cols_from_pieces(p):
    return p.transpose(1, 0, 2).reshape(p.shape[1], -1)


WEIGHT_ORDER = ("pre_mix_norm", "w_in", "fox_f_bias", "fox_out_norm", "gdn_conv_w", "gdn_a_log", "gdn_dt_bias",
                "gdn_out_norm", "w_out", "post_mix_norm", "pre_mlp_norm", "w_up", "w_down", "post_mlp_norm")


def kernel(x, pre_mix_norm, w_in, fox_f_bias, fox_out_norm, gdn_conv_w, gdn_a_log, gdn_dt_bias, gdn_out_norm, w_out, post_mix_norm, pre_mlp_norm, w_up, w_down, post_mlp_norm, loss_target, m_pre_mix_norm, m_w_in, m_fox_f_bias, m_fox_out_norm, m_gdn_conv_w, m_gdn_a_log, m_gdn_dt_bias, m_gdn_out_norm, m_w_out, m_post_mix_norm, m_pre_mlp_norm, m_w_up, m_w_down, m_post_mlp_norm, v_pre_mix_norm, v_w_in, v_fox_f_bias, v_fox_out_norm, v_gdn_conv_w, v_gdn_a_log, v_gdn_dt_bias, v_gdn_out_norm, v_w_out, v_post_mix_norm, v_pre_mlp_norm, v_w_up, v_w_down, v_post_mlp_norm):
    w = dict(pre_mix_norm=pre_mix_norm, w_in=w_in, fox_f_bias=fox_f_bias, fox_out_norm=fox_out_norm,
             gdn_conv_w=gdn_conv_w, gdn_a_log=gdn_a_log, gdn_dt_bias=gdn_dt_bias, gdn_out_norm=gdn_out_norm, w_out=w_out,
             post_mix_norm=post_mix_norm, pre_mlp_norm=pre_mlp_norm, w_up=w_up, w_down=w_down, post_mlp_norm=post_mlp_norm)
    mom = dict(pre_mix_norm=m_pre_mix_norm, w_in=m_w_in, fox_f_bias=m_fox_f_bias, fox_out_norm=m_fox_out_norm,
               gdn_conv_w=m_gdn_conv_w, gdn_a_log=m_gdn_a_log, gdn_dt_bias=m_gdn_dt_bias, gdn_out_norm=m_gdn_out_norm,
               w_out=m_w_out, post_mix_norm=m_post_mix_norm, pre_mlp_norm=m_pre_mlp_norm, w_up=m_w_up, w_down=m_w_down,
               post_mlp_norm=m_post_mlp_norm)
    var = dict(pre_mix_norm=v_pre_mix_norm, w_in=v_w_in, fox_f_bias=v_fox_f_bias, fox_out_norm=v_fox_out_norm,
               gdn_conv_w=v_gdn_conv_w, gdn_a_log=v_gdn_a_log, gdn_dt_bias=v_gdn_dt_bias, gdn_out_norm=v_gdn_out_norm,
               w_out=v_w_out, post_mix_norm=v_post_mix_norm, pre_mlp_norm=v_pre_mlp_norm, w_up=v_w_up, w_down=v_w_down,
               post_mlp_norm=v_post_mlp_norm)

    (win_g, conv_g), zones, shards = _all_gather([w_in.T.astype(BF), gdn_conv_w],
                                                 [(w_out, False), (w_up, True), (w_down, False)], BF)
    wt_al = _shuffle_rows([win_g], _to_aligned_moves(), (1, PROJ_W, D_MODEL), "w_in_to_aligned")[0]
    convw = _cols_from_pieces(conv_g)
    sems, shards, zones, after = _exchange_start(shards, zones, False, "gather_start", chips=True)
    gathers = dict(hop=(sems, shards, zones))

    def late_weights(name, after):
        if name == "mlp_relay":
            sems, shards, zones = gathers.pop("hop")
            _, zones = _exchange_wait(sems, shards, zones, after, "gather_wait", chips=True)
            sems, _, zones, token = _relay_start(zones, "gather_relay")
            gathers.update(w_out=(sems[:2], zones[:1]), mlp=(sems[2:], zones[1:]))
            return token
        sems, zones = gathers[name]
        _, got = _exchange_wait(sems, [], zones, after, "gather_" + name + "_done", n_copies=len(CHIP_FLIPS))
        if name == "w_out":
            return got[0].reshape(D_MODEL, D_MODEL)
        return got[0].reshape(D_FF, D_MODEL), got[1].reshape(D_FF, D_MODEL)

    scatters = {}

    def on_grads(name, g):
        if name == "mlp":
            scatters["mlp"] = list(g)
            return g[0]
        if name == "w_out":
            sems, srcs, zones, token = _exchange_start(scatters["mlp"] + [g], None, True, "scatter_mlp_w_out_start")
            scatters["mlp"] = (sems[:4], srcs[:2], zones[:2], token)
            scatters["w_out"] = (sems[4:], srcs[2:], zones[2:], token)
            return token
        g = _shuffle_rows(list(g), _from_aligned_moves(), (N_DEV, W_IN_PIECE, D_MODEL), "w_in_grad_from_aligned")
        scatters[name] = _exchange_start([_pair_reduce(g, "pair_reduce_w_in")], None, True, "scatter_w_in_start", chips=True)
        return scatters[name][3]

    grad_x, small = _local_step(
        x[0], loss_target[0], wt_al, after, late_weights, on_grads, convw, pre_mix_norm,
        fox_f_bias, fox_out_norm, gdn_a_log, gdn_dt_bias, gdn_out_norm, post_mix_norm, pre_mlp_norm, post_mlp_norm)
    slabs = [_small_pack(small), _conv_slabs(jnp.concatenate(small["conv"], axis=1))]
    scatters["small"] = _exchange_start(slabs, None, False, "small_start")

    grads, delta, new_m, new_v = {}, {}, {}, {}
    after = scatters["small"][3]
    for name, members in (("mlp", ("w_up", "w_down")), ("w_out", ("w_out",)), ("small", ()), ("w_in", ("w_in",))):
        sems, srcs, zones, _ = scatters[name]
        srcs, zones = _exchange_wait(sems, srcs, zones, after, "scatter_" + name + "_wait", chips=name == "w_in")
        if name == "small":
            loss, updated = _small_update(*zones, *srcs, w, mom, var)
            for n, res in updated.items():
                grads[n], delta[n], new_m[n], new_v[n] = res
            after = grads["pre_mix_norm"]
        for n, zone, own in zip(members, zones, srcs):
            if n == "w_in":
                res = _sum_adamw(zone, own, w[n].T, mom[n].T, var[n].T, "adamw_" + n, chips=True)
                grads[n], delta[n], new_m[n], new_v[n] = [r.T for r in res]
            else:
                grads[n], delta[n], new_m[n], new_v[n] = _sum_adamw(zone, own, w[n], mom[n], var[n], "adamw_" + n)
        if members:
            after = [grads[n] for n in members]

    return (loss, grad_x[None], *[grads[n] for n in WEIGHT_ORDER], *[delta[n] for n in WEIGHT_ORDER],
            *[new_m[n] for n in WEIGHT_ORDER], *[new_v[n] for n in WEIGHT_ORDER])
```

```python
import jax
import jax.numpy as jnp
from jax import lax
from jax.experimental import pallas as pl
from jax.experimental.pallas import tpu as pltpu

F32 = jnp.float32
BF = jnp.bfloat16

D_MODEL = 1024
N_FOX_HEADS, FOX_HEAD_DIM = 8, 64
N_GDN_HEADS, GDN_HEAD_DIM = 4, 128
D_FOX = N_FOX_HEADS * FOX_HEAD_DIM
D_GDN = N_GDN_HEADS * GDN_HEAD_DIM
CHUNK = 64
CONV_K = 4
D_FF = 4 * D_MODEL
EPS = 1e-6
D_PROJ = 3600
N_DEV = 8

PROJ_W = 3712
COL_FOX, COL_GDN, COL_GZ, COL_SMALL = 0, 1536, 3072, 3584
LANES = 128
QKV = 3 * LANES
SM_FF, SM_GB, SM_GA = 0, 8, 12

ADAM_LR, ADAM_B1, ADAM_B2, ADAM_EPS, ADAM_WD, ADAM_STEP = 0.001, 0.9, 0.999, 1e-08, 0.01, 10

TOKEN_BLOCK = 256
MATMUL_BLOCK = 512
TRI_ROWS = 4
FOX_SCALE = FOX_HEAD_DIM ** -0.5
GDN_QSCALE = GDN_HEAD_DIM ** -0.5
NEG_BIG = -1e30
VMEM_LIMIT = 56 * 1024 * 1024

VMEM_SPEC = pl.BlockSpec(memory_space=pltpu.VMEM)
ANY_SPEC = pl.BlockSpec(memory_space=pl.ANY)


def _sds(shape, dtype=F32):
    return jax.ShapeDtypeStruct(shape, dtype)


def _params(*sem):
    return pltpu.CompilerParams(dimension_semantics=sem if sem else None, vmem_limit_bytes=VMEM_LIMIT)


def _ordered(body):
    def ordered(_, *refs):
        body(*refs)

    return ordered


def _mm(a, b):
    return jnp.dot(a.astype(BF), b.astype(BF), preferred_element_type=F32)


def _mm_nt(a, b):
    return lax.dot_general(a.astype(BF), b.astype(BF), (((1,), (1,)), ((), ())), preferred_element_type=F32)


def _mm_tn(a, b):
    return lax.dot_general(a.astype(BF), b.astype(BF), (((0,), (0,)), ((), ())), preferred_element_type=F32)


def _sigmoid(x):
    return 1.0 / (1.0 + jnp.exp(-x))


def _softplus(x):
    return jnp.maximum(x, 0.0) + jnp.log1p(jnp.exp(-jnp.abs(x)))


def _iota(shape, dim):
    return lax.broadcasted_iota(jnp.int32, shape, dim)


def _shift_down(x, s, row):
    return jnp.where(row >= s, pltpu.roll(x, s, 0), 0.0)


def _shift_up(x, s, row):
    n = x.shape[0]
    return jnp.where(row < n - s, pltpu.roll(x, n - s, 0), 0.0)


def _norm_proj(x, nw, wt_al, after):
    t = x.shape[0]

    def body(x_ref, nw_ref, w_ref, proj_ref, h_ref):
        xv = x_ref[...]
        r = lax.rsqrt(jnp.mean(xv * xv, axis=-1, keepdims=True) + EPS)
        h = (xv * r * nw_ref[...]).astype(BF)
        h_ref[...] = h
        proj_ref[...] = lax.dot_general(h, w_ref[...], (((1,), (1,)), ((), ())), preferred_element_type=F32)

    tm = min(MATMUL_BLOCK, t)
    return pl.pallas_call(
        _ordered(body), name="norm_proj", grid=(t // tm,),
        in_specs=[ANY_SPEC, pl.BlockSpec((tm, D_MODEL), lambda i: (i, 0)), pl.BlockSpec((1, D_MODEL), lambda i: (0, 0)),
                  pl.BlockSpec((PROJ_W, D_MODEL), lambda i: (0, 0))],
        out_specs=[pl.BlockSpec((tm, PROJ_W), lambda i: (i, 0)), pl.BlockSpec((tm, D_MODEL), lambda i: (i, 0))],
        out_shape=[_sds((t, PROJ_W)), _sds((t, D_MODEL), BF)],
        compiler_params=_params("parallel"),
    )(after, x, nw, wt_al)


def _lane_column(x, lane):
    return jnp.sum(jnp.where(_iota((1, LANES), 1) == lane, x, 0.0), axis=-1, keepdims=True)


def _small_prep(proj, fb, al, dtb):
    t = proj.shape[0]

    def body(sm_ref, fb_ref, al_ref, dtb_ref, cumt_ref, beta_ref, g_ref):
        s = sm_ref[...]
        z = s + fb_ref[...]
        cum = jnp.minimum(z, 0.0) - jnp.log1p(jnp.exp(-jnp.abs(z)))
        row = _iota((t, LANES), 0)
        step = 1
        while step < t:
            cum = cum + _shift_down(cum, step, row)
            step *= 2
        cumt_ref[...] = cum.T
        beta_ref[...] = _sigmoid(s)
        g_ref[...] = -jnp.exp(al_ref[...]) * _softplus(s + dtb_ref[...])

    vec = pl.BlockSpec((1, LANES), lambda i: (0, 0))
    tok = pl.BlockSpec((t, LANES), lambda i: (0, 0))
    return pl.pallas_call(
        body, name="small_prep", grid=(1,),
        in_specs=[pl.BlockSpec((t, LANES), lambda i: (0, COL_SMALL // LANES)), vec, vec, vec],
        out_specs=[pl.BlockSpec((LANES, t), lambda i: (0, 0)), tok, tok],
        out_shape=[_sds((LANES, t)), _sds((t, LANES)), _sds((t, LANES))],
        compiler_params=_params("arbitrary"),
    )(proj, fb, al, dtb)


def _fox_stack(x, first):
    return jnp.concatenate([jnp.where(first, x, 0.0), jnp.where(first, 0.0, x)], axis=0).astype(BF)


def _fox_unstack(y, first):
    n = y.shape[0] // 2
    return jnp.where(first, y[:n], y[n:])


def _fox_logits(q2_i, kb, cumt_ref, pair, i, tq):
    klen = (i + 1) * tq
    s = lax.dot_general(q2_i, kb[:klen], (((1,), (1,)), ((), ())), preferred_element_type=F32)
    upper = _iota((2 * tq, 1), 0) < tq
    s = s - jnp.where(upper, cumt_ref[pl.ds(2 * pair, 1), 0:klen], cumt_ref[pl.ds(2 * pair + 1, 1), 0:klen])
    causal = _iota((2 * tq, tq), 1) <= _iota((2 * tq, tq), 0) % tq
    parts = [(s[:, :klen - tq], 0, klen - tq)] if i else []
    return parts + [(jnp.where(causal, s[:, klen - tq:], NEG_BIG), klen - tq, klen)]


def _fox_fwd(proj, cumt, fnw, after):
    t = proj.shape[0]
    tq = min(TOKEN_BLOCK, t // 2)
    nq = t // tq

    def body(q_ref, k_ref, v_ref, cumt_ref, fnw_ref, o_ref, lse_ref, fn_ref):
        j = pl.program_id(0)
        first = _iota((1, LANES), 1) < FOX_HEAD_DIM
        kb = k_ref[...].astype(BF)
        vb = v_ref[...].astype(BF)
        for i in range(nq):
            rows = slice(i * tq, (i + 1) * tq)
            q2 = _fox_stack(q_ref[rows, :] * FOX_SCALE, first)
            parts = _fox_logits(q2, kb, cumt_ref, j, i, tq)
            m = jnp.max(parts[-1][0], axis=-1, keepdims=True)
            if i:
                m = jnp.maximum(m, jnp.max(parts[0][0], axis=-1, keepdims=True))
            l = jnp.zeros((2 * tq, 1), F32)
            o = jnp.zeros((2 * tq, LANES), F32)
            for s, lo, hi in parts:
                p = jnp.exp(s - m)
                l = l + jnp.sum(p, axis=-1, keepdims=True)
                o = o + jnp.dot(p.astype(BF), vb[lo:hi], preferred_element_type=F32)
            o_acc = _fox_unstack(o / l, first)
            lse_acc = _fox_unstack(jnp.broadcast_to(m + jnp.log(l), (2 * tq, LANES)), first)
            o_ref[rows, :] = o_acc
            lse_ref[rows, :] = lse_acc
            o2 = o_acc * o_acc
            s0 = jnp.sum(jnp.where(first, o2, 0.0), axis=-1, keepdims=True)
            s1 = jnp.sum(jnp.where(first, 0.0, o2), axis=-1, keepdims=True)
            r = lax.rsqrt(jnp.where(first, s0, s1) * (1.0 / FOX_HEAD_DIM) + EPS)
            fn_ref[rows, :] = (o_acc * r * fnw_ref[...]).astype(BF)

    qkv = lambda k: pl.BlockSpec((t, LANES), lambda j: (0, COL_FOX // LANES + 3 * j + k))
    pair = pl.BlockSpec((t, LANES), lambda j: (0, j))
    return pl.pallas_call(
        _ordered(body), name="fox_fwd", grid=(N_FOX_HEADS // 2,),
        in_specs=[ANY_SPEC, qkv(0), qkv(1), qkv(2), pl.BlockSpec((LANES, t), lambda j: (0, 0)),
                  pl.BlockSpec((1, LANES), lambda j: (0, 0))],
        out_specs=[pair, pair, pair],
        out_shape=[_sds((t, D_FOX)), _sds((t, D_FOX)), _sds((t, D_FOX), BF)],
        compiler_params=_params("parallel"),
    )(after, proj, proj, proj, cumt, fnw)


def _fox_bwd(proj, cumt, lse, o, do, dproj, after):
    t = proj.shape[0]
    tq = min(TOKEN_BLOCK, t // 2)
    nq = t // tq

    def body(q_ref, k_ref, v_ref, cumt_ref, lse_ref, o_ref, do_ref, _, dqkv_ref, dcq_ref, dckt_ref, dk_s, dv_s):
        j = pl.program_id(0)

        @pl.when(j == 0)
        def _():
            dcq_ref[...] = jnp.zeros_like(dcq_ref)
            dckt_ref[...] = jnp.zeros_like(dckt_ref)

        lane = _iota((1, LANES), 1)

        first = _iota((1, LANES), 1) < FOX_HEAD_DIM
        kb = k_ref[...].astype(BF)
        vb = v_ref[...].astype(BF)
        dk_s[...] = jnp.zeros_like(dk_s)
        dv_s[...] = jnp.zeros_like(dv_s)
        for i in range(nq):
            rows = slice(i * tq, (i + 1) * tq)
            do_i = do_ref[rows, :]
            prod = do_i * o_ref[rows, :]
            lse_i = lse_ref[rows, :]
            q2 = _fox_stack(q_ref[rows, :] * FOX_SCALE, first)
            do2 = _fox_stack(do_i, first)
            delta = jnp.concatenate([jnp.sum(jnp.where(first, prod, 0.0), axis=-1, keepdims=True),
                                     jnp.sum(jnp.where(first, 0.0, prod), axis=-1, keepdims=True)], axis=0)
            lse2 = jnp.concatenate([lse_i[:, 0:1], lse_i[:, FOX_HEAD_DIM:FOX_HEAD_DIM + 1]], axis=0)
            dq2 = jnp.zeros((2 * tq, LANES), F32)
            dcq2 = jnp.zeros((2 * tq, 1), F32)
            for s, lo, hi in _fox_logits(q2, kb, cumt_ref, j, i, tq):
                p = jnp.exp(s - lse2)
                ds = p * (_mm_nt(do2, vb[lo:hi]) - delta)
                dsb = ds.astype(BF)
                dq2 = dq2 + jnp.dot(dsb, kb[lo:hi], preferred_element_type=F32)
                dk_s[lo:hi, :] += _mm_tn(dsb, q2)
                dv_s[lo:hi, :] += _mm_tn(p, do2)
                dcq2 = dcq2 + jnp.sum(ds, axis=-1, keepdims=True)
                dckt_ref[pl.ds(2 * j, 1), lo:hi] += jnp.sum(ds[:tq], axis=0, keepdims=True)
                dckt_ref[pl.ds(2 * j + 1, 1), lo:hi] += jnp.sum(ds[tq:], axis=0, keepdims=True)
            dqkv_ref[rows, 0:LANES] = (_fox_unstack(dq2, first) * FOX_SCALE).astype(BF)
            dcq_ref[rows, :] += jnp.where(lane == 2 * j, dcq2[:tq], jnp.where(lane == 2 * j + 1, dcq2[tq:], 0.0))
        dqkv_ref[:, LANES:2 * LANES] = dk_s[...].astype(BF)
        dqkv_ref[:, 2 * LANES:QKV] = dv_s[...].astype(BF)

    qkv = lambda k: pl.BlockSpec((t, LANES), lambda j: (0, COL_FOX // LANES + 3 * j + k))
    pair = pl.BlockSpec((t, LANES), lambda j: (0, j))
    rows128 = pl.BlockSpec((LANES, t), lambda j: (0, 0))
    return pl.pallas_call(
        _ordered(body), name="fox_bwd", grid=(N_FOX_HEADS // 2,),
        in_specs=[ANY_SPEC, qkv(0), qkv(1), qkv(2), rows128, pair, pair, pair, ANY_SPEC],
        out_specs=[pl.BlockSpec((t, QKV), lambda j: (0, COL_FOX // QKV + j)),
                   pl.BlockSpec((t, LANES), lambda j: (0, 0)), rows128],
        out_shape=[_sds(dproj.shape, BF), _sds((t, LANES)), _sds((LANES, t))],
        scratch_shapes=[pltpu.VMEM((t, LANES), F32), pltpu.VMEM((t, LANES), F32)],
        input_output_aliases={8: 0}, compiler_params=_params("arbitrary"),
    )(after, proj, proj, proj, cumt, lse, o, do, dproj)


def _conv(x, w, row):
    return (w[3:4, :] * x + w[2:3, :] * _shift_down(x, 1, row) + w[1:2, :] * _shift_down(x, 2, row)
            + w[0:1, :] * _shift_down(x, 3, row))


def _chunk_decay(gc_c):
    gi = gc_c[:, 0:CHUNK]
    gj = gc_c.T[0:CHUNK, :]
    ri = _iota((CHUNK, CHUNK), 0)
    cj = _iota((CHUNK, CHUNK), 1)
    return jnp.where(ri >= cj, jnp.exp(jnp.minimum(gi - gj, 0.0)), 0.0), ri > cj


def _gdn_specs(t):
    col = lambda off: pl.BlockSpec((t, LANES), lambda h: (0, off + h))
    cw = lambda off: pl.BlockSpec((CONV_K, LANES), lambda h: (0, off + h))
    mat = pl.BlockSpec((1, t // CHUNK, CHUNK, CHUNK), lambda h: (h, 0, 0, 0))
    qkv = lambda k: pl.BlockSpec((t, LANES), lambda h: (0, COL_GDN // LANES + 3 * h + k))
    return col, cw, mat, qkv


def _gdn_prep(proj, convw, beta, g):
    t = proj.shape[0]
    nch = t // CHUNK

    def body(xq_ref, xk_ref, xv_ref, wq_ref, wk_ref, wv_ref, beta_ref, g_ref,
             qn_ref, kn_ref, cv_ref, gc_ref, be_ref, m_ref, a_ref):
        row = _iota((t, LANES), 0)
        hd = pl.program_id(0)
        be_ref[...] = jnp.broadcast_to(_lane_column(beta_ref[...], SM_GB + hd), (t, LANES))

        def act(x_ref, w_ref):
            y = _conv(x_ref[...], w_ref[...], row)
            return y * _sigmoid(y)

        cq = act(xq_ref, wq_ref)
        ck = act(xk_ref, wk_ref)
        cv_ref[...] = act(xv_ref, wv_ref)
        qn_ref[...] = cq * lax.rsqrt(jnp.sum(cq * cq, axis=-1, keepdims=True) + EPS) * GDN_QSCALE
        kn_ref[...] = ck * lax.rsqrt(jnp.sum(ck * ck, axis=-1, keepdims=True) + EPS)
        gc = jnp.broadcast_to(_lane_column(g_ref[...], SM_GA + hd), (t, LANES))
        pos = row % CHUNK
        step = 1
        while step < CHUNK:
            gc = gc + jnp.where(pos >= step, pltpu.roll(gc, step, 0), 0.0)
            step *= 2
        gc_ref[...] = gc

        group = 4 if nch % 4 == 0 else 1

        def chunks(gi, carry):
            ns = [gi * group + c for c in range(group)]
            sls = [pl.ds(pl.multiple_of(n * CHUNK, CHUNK), CHUNK) for n in ns]
            ks = [kn_ref[sl, :] for sl in sls]
            kk = [_mm_nt(k_c * be_ref[sl, :], k_c) for k_c, sl in zip(ks, sls)]
            qk = [_mm_nt(qn_ref[sl, :], k_c) for k_c, sl in zip(ks, sls)]
            for c, n in enumerate(ns):
                decay, strict = _chunk_decay(gc_ref[sls[c], :])
                m_ref[0, n] = jnp.where(strict, kk[c] * decay, 0.0)
                a_ref[0, n] = qk[c] * decay
            return carry

        lax.fori_loop(0, nch // group, chunks, 0)

    col, cw, mat, qkv = _gdn_specs(t)
    return pl.pallas_call(
        body, name="gdn_prep", grid=(N_GDN_HEADS,),
        in_specs=[qkv(0), qkv(1), qkv(2), cw(0), cw(4), cw(8)] + [pl.BlockSpec((t, LANES), lambda h: (0, 0))] * 2,
        out_specs=[col(0), col(0), col(0), col(0), col(0), mat, mat],
        out_shape=[_sds((t, D_GDN))] * 5 + [_sds((N_GDN_HEADS, nch, CHUNK, CHUNK))] * 2,
        compiler_params=_params("parallel"),
    )(proj, proj, proj, convw, convw, convw, beta, g)


def _tri_inverse(m3):
    assert m3.shape == (LANES, CHUNK, CHUNK)

    def body(m_ref, t_ref, ms, ts):
        for i in range(CHUNK):
            ms[i * CHUNK:(i + 1) * CHUNK, :] = m_ref[:, i, :].T
        cidx = _iota((CHUNK, LANES), 0)

        def t_row(j):
            return ts[pl.ds(pl.multiple_of(j * CHUNK, CHUNK), CHUNK), :]

        def outer(ib, carry):
            i0 = ib * TRI_ROWS

            def inner(group, accs):
                for jj in range(TRI_ROWS):
                    jj = group * TRI_ROWS + jj
                    earlier = t_row(jj)
                    accs = tuple(acc - ms[pl.ds((i0 + r) * CHUNK + jj, 1), :] * earlier for r, acc in enumerate(accs))
                return accs

            accs = list(lax.fori_loop(
                0, ib, inner, tuple(jnp.where(cidx == i0 + r, 1.0, 0.0).astype(F32) for r in range(TRI_ROWS))))
            for r in range(TRI_ROWS):
                for q in range(r):
                    accs[r] = accs[r] - ms[pl.ds((i0 + r) * CHUNK + i0 + q, 1), :] * accs[q]
                ts[pl.ds(pl.multiple_of((i0 + r) * CHUNK, CHUNK), CHUNK), :] = accs[r]
            return carry

        lax.fori_loop(0, CHUNK // TRI_ROWS, outer, 0)
        for i in range(CHUNK):
            t_ref[:, i, :] = ts[i * CHUNK:(i + 1) * CHUNK, :].T

    return pl.pallas_call(
        body, name="tri_inverse", in_specs=[VMEM_SPEC], out_specs=VMEM_SPEC,
        out_shape=_sds((LANES, CHUNK, CHUNK)),
        scratch_shapes=[pltpu.VMEM((CHUNK * CHUNK, LANES), F32), pltpu.VMEM((CHUNK * CHUNK, LANES), F32)],
        compiler_params=_params(),
    )(m3)


def _gdn_chunk_terms(q, k, v, b, gcc):
    eg = jnp.exp(gcc)
    last = gcc[CHUNK - 1:CHUNK, :]
    egl = jnp.exp(last - gcc)
    gl = jnp.exp(last)
    kb = k * b
    return eg, egl, gl, kb, v * b, kb * eg, q * eg, k * egl


GDN_BLOCK_CHUNKS = 4


def _gdn_block_specs(t, reverse):
    cb = GDN_BLOCK_CHUNKS
    nb = t // (cb * CHUNK)
    idx = (lambda i: nb - 1 - i) if reverse else (lambda i: i)
    tok = pl.BlockSpec((cb * CHUNK, D_GDN), lambda i: (idx(i), 0))
    mat = pl.BlockSpec((N_GDN_HEADS, cb, CHUNK, CHUNK), lambda i: (0, idx(i), 0, 0))
    state = pl.BlockSpec((N_GDN_HEADS, cb, GDN_HEAD_DIM, GDN_HEAD_DIM), lambda i: (0, idx(i), 0, 0))
    return nb, tok, mat, state


def _gdn_scan(qn, kn, cv, be, gc, tinv, amat):
    t = qn.shape[0]
    nch = t // CHUNK

    def body(q_ref, k_ref, v_ref, b_ref, gc_ref, t_ref, a_ref, o_ref, sall_ref, vn_ref, s_scr):
        @pl.when(pl.program_id(0) == 0)
        def _():
            s_scr[...] = jnp.zeros_like(s_scr)

        heads = range(N_GDN_HEADS)
        cols = [slice(hd * LANES, (hd + 1) * LANES) for hd in heads]
        s = [s_scr[hd] for hd in heads]
        for cc in range(GDN_BLOCK_CHUNKS):
            rs = slice(cc * CHUNK, (cc + 1) * CHUNK)
            terms = [_gdn_chunk_terms(q_ref[rs, cs], k_ref[rs, cs], v_ref[rs, cs], b_ref[rs, cs], gc_ref[rs, cs])
                     for cs in cols]
            for hd in heads:
                sall_ref[hd, cc] = s[hd]
            uw = [_mm(t_ref[hd, cc], jnp.concatenate([terms[hd][4], terms[hd][5]], axis=1)) for hd in heads]
            ws_qs = [_mm(jnp.concatenate([uw[hd][:, LANES:], terms[hd][6]], axis=0), s[hd]) for hd in heads]
            vn = [uw[hd][:, :LANES] - ws_qs[hd][:CHUNK] for hd in heads]
            a_vn = [_mm(a_ref[hd, cc], vn[hd]) for hd in heads]
            kd_vn = [_mm_tn(terms[hd][7], vn[hd]) for hd in heads]
            for hd in heads:
                vn_ref[rs, cols[hd]] = vn[hd]
                o_ref[rs, cols[hd]] = ws_qs[hd][CHUNK:] + a_vn[hd]
                s[hd] = s[hd] * terms[hd][2] + kd_vn[hd]
        for hd in heads:
            s_scr[hd] = s[hd]

    nb, tok, mat, state = _gdn_block_specs(t, False)
    return pl.pallas_call(
        body, name="gdn_scan", grid=(nb,),
        in_specs=[tok] * 5 + [mat, mat], out_specs=[tok, state, tok],
        out_shape=[_sds((t, D_GDN)), _sds((N_GDN_HEADS, nch, GDN_HEAD_DIM, GDN_HEAD_DIM)), _sds((t, D_GDN))],
        scratch_shapes=[pltpu.VMEM((N_GDN_HEADS, GDN_HEAD_DIM, GDN_HEAD_DIM), F32)],
        compiler_params=_params("arbitrary"),
    )(qn, kn, cv, be, gc, tinv, amat)


def _gdn_bwd(qn, kn, cv, be, gc, tinv, amat, s_all, vn_all, do, after):
    t = qn.shape[0]

    def body(q_ref, k_ref, v_ref, b_ref, gc_ref, t_ref, a_ref, sall_ref, vn_ref, do_ref,
             dq_ref, dk_ref, dv_ref, db_ref, dg_ref, ds_scr):
        @pl.when(pl.program_id(0) == 0)
        def _():
            ds_scr[...] = jnp.zeros_like(ds_scr)

        lastrow = _iota((CHUNK, LANES), 0) == CHUNK - 1
        heads = range(N_GDN_HEADS)
        cols = [slice(hd * LANES, (hd + 1) * LANES) for hd in heads]
        each = lambda fn: [fn(hd) for hd in heads]
        rows_cat = lambda x, y: jnp.concatenate([x, y], axis=0)
        lane_cat = lambda x, y: jnp.concatenate([x, y], axis=1)
        dsp = each(lambda hd: ds_scr[hd])
        for cc in reversed(range(GDN_BLOCK_CHUNKS)):
            rs = slice(cc * CHUNK, (cc + 1) * CHUNK)
            q = each(lambda hd: q_ref[rs, cols[hd]])
            k = each(lambda hd: k_ref[rs, cols[hd]])
            v = each(lambda hd: v_ref[rs, cols[hd]])
            b = each(lambda hd: b_ref[rs, cols[hd]])
            gcc = each(lambda hd: gc_ref[rs, cols[hd]])
            do_c = each(lambda hd: do_ref[rs, cols[hd]])
            vn = each(lambda hd: vn_ref[rs, cols[hd]])
            tn = each(lambda hd: t_ref[hd, cc])
            st = each(lambda hd: sall_ref[hd, cc])
            terms = each(lambda hd: _gdn_chunk_terms(q[hd], k[hd], v[hd], b[hd], gcc[hd]))
            eg, egl, gl, kb, vb, kbg, qd, kd = [[terms[hd][i] for hd in heads] for i in range(8)]
            w = each(lambda hd: _mm(tn[hd], kbg[hd]))
            a_do = each(lambda hd: _mm_tn(a_ref[hd, cc], do_c[hd]))
            kd_ds = each(lambda hd: _mm(kd[hd], dsp[hd]))
            da = each(lambda hd: _mm_nt(do_c[hd], vn[hd]))
            dkd = each(lambda hd: _mm_nt(vn[hd], dsp[hd]))
            by_k = each(lambda hd: _mm_nt(rows_cat(kb[hd], q[hd]), k[hd]))
            dgl = each(lambda hd: jnp.sum(jnp.sum(dsp[hd] * st[hd], axis=-1, keepdims=True), axis=0, keepdims=True))
            dvn = each(lambda hd: a_do[hd] + kd_ds[hd])
            do_dvn = each(lambda hd: rows_cat(do_c[hd], dvn[hd]))
            by_s = each(lambda hd: _mm_nt(do_dvn[hd], st[hd]))
            dqd = each(lambda hd: by_s[hd][:CHUNK])
            dvn_dw = each(lambda hd: lane_cat(dvn[hd], -by_s[hd][CHUNK:]))
            dsp = each(lambda hd: _mm_tn(rows_cat(qd[hd], -w[hd]), do_dvn[hd]) + gl[hd] * dsp[hd])
            dt = each(lambda hd: _mm_nt(dvn_dw[hd], lane_cat(vb[hd], kbg[hd])))
            by_t = each(lambda hd: _mm_tn(tn[hd], dvn_dw[hd]))
            tt_dt = each(lambda hd: _mm_tn(tn[hd], dt[hd]))
            dm_raw = each(lambda hd: _mm_nt(tt_dt[hd], tn[hd]))
            masks = each(lambda hd: _chunk_decay(gcc[hd]))
            dkk = each(lambda hd: jnp.where(masks[hd][1], -dm_raw[hd], 0.0) * masks[hd][0])
            dqk = each(lambda hd: da[hd] * masks[hd][0])
            dqk_dkk = each(lambda hd: rows_cat(dqk[hd], dkk[hd]))
            on_k = each(lambda hd: _mm(dqk_dkk[hd], k[hd]))
            dk_mm = each(lambda hd: _mm_tn(dqk_dkk[hd], rows_cat(q[hd], kb[hd])))
            for hd in heads:
                cs = cols[hd]
                dvb, dkbg = by_t[hd][:, :LANES], by_t[hd][:, LANES:]
                gmat = dkk[hd] * by_k[hd][:CHUNK] + dqk[hd] * by_k[hd][CHUNK:]
                dq_ref[rs, cs] = dqd[hd] * eg[hd] + on_k[hd][:CHUNK]
                dkb = on_k[hd][CHUNK:] + dkbg * eg[hd]
                dk_ref[rs, cs] = dkd[hd] * egl[hd] + dk_mm[hd] + dkb * b[hd]
                db = jnp.sum(dkb * k[hd], axis=-1, keepdims=True) + jnp.sum(dvb * v[hd], axis=-1, keepdims=True)
                db_ref[rs, cs] = jnp.broadcast_to(db, (CHUNK, LANES))
                dv_ref[rs, cs] = dvb * b[hd]
                dkd_kd = jnp.sum(dkd[hd] * kd[hd], axis=-1, keepdims=True)
                col_sums = jnp.sum(lane_cat(gmat, jnp.zeros_like(gmat)).T, axis=-1, keepdims=True)
                dgc = (jnp.sum(gmat, axis=-1, keepdims=True) - col_sums[:CHUNK]
                       + jnp.sum(dqd[hd] * qd[hd], axis=-1, keepdims=True)
                       + jnp.sum(dkbg * kbg[hd], axis=-1, keepdims=True) - dkd_kd)
                extra = jnp.sum(dkd_kd, axis=0, keepdims=True) + dgl[hd] * gl[hd]
                dg_ref[rs, cs] = dgc + jnp.where(lastrow, extra, 0.0)
        for hd in heads:
            ds_scr[hd] = dsp[hd]
        dg = dg_ref[...]
        row = _iota(dg.shape, 0)
        pos = row % CHUNK
        step = 1
        while step < CHUNK:
            dg = dg + jnp.where(pos < CHUNK - step, pltpu.roll(dg, dg.shape[0] - step, 0), 0.0)
            step *= 2
        dg_ref[...] = dg

    nb, tok, mat, state = _gdn_block_specs(t, True)
    return pl.pallas_call(
        _ordered(body), name="gdn_bwd", grid=(nb,),
        in_specs=[ANY_SPEC] + [tok] * 5 + [mat, mat, state, tok, tok], out_specs=[tok] * 5,
        out_shape=[_sds((t, D_GDN))] * 5,
        scratch_shapes=[pltpu.VMEM((N_GDN_HEADS, GDN_HEAD_DIM, GDN_HEAD_DIM), F32)],
        compiler_params=_params("arbitrary"),
    )(after, qn, kn, cv, be, gc, tinv, amat, s_all, vn_all, do)


def _gdn_bwd_conv(proj, convw, dqn, dkn, dcv, dproj):
    t = proj.shape[0]

    def body(xq_ref, xk_ref, xv_ref, wq_ref, wk_ref, wv_ref, dq_ref, dk_ref, dv_ref, _,
             dqkv_ref, dwq_ref, dwk_ref, dwv_ref):
        row = _iota((t, LANES), 0)

        def one(x_ref, w_ref, d_ref, k, dw_ref, scale):
            x = x_ref[...]
            w = w_ref[...]
            y = _conv(x, w, row)
            sg = _sigmoid(y)
            dc = d_ref[...]
            if scale is not None:
                c = y * sg
                r = lax.rsqrt(jnp.sum(c * c, axis=-1, keepdims=True) + EPS)
                ch = c * r
                dc = scale * r * (dc - ch * jnp.sum(dc * ch, axis=-1, keepdims=True))
            dy = dc * sg * (1.0 + y * (1.0 - sg))
            dqkv_ref[:, k * LANES:(k + 1) * LANES] = (
                w[3:4, :] * dy + w[2:3, :] * _shift_up(dy, 1, row) + w[1:2, :] * _shift_up(dy, 2, row)
                + w[0:1, :] * _shift_up(dy, 3, row)).astype(BF)
            for jj in range(CONV_K):
                xs = x if jj == CONV_K - 1 else _shift_down(x, CONV_K - 1 - jj, row)
                dw_ref[jj:jj + 1, :] = jnp.sum(dy * xs, axis=0, keepdims=True)

        one(xq_ref, wq_ref, dq_ref, 0, dwq_ref, GDN_QSCALE)
        one(xk_ref, wk_ref, dk_ref, 1, dwk_ref, 1.0)
        one(xv_ref, wv_ref, dv_ref, 2, dwv_ref, None)

    col, cw, _, qkv = _gdn_specs(t)
    return pl.pallas_call(
        body, name="gdn_bwd_conv", grid=(N_GDN_HEADS,),
        in_specs=[qkv(0), qkv(1), qkv(2), cw(0), cw(4), cw(8), col(0), col(0), col(0), ANY_SPEC],
        out_specs=[pl.BlockSpec((t, QKV), lambda h: (0, COL_GDN // QKV + h)), cw(0), cw(0), cw(0)],
        out_shape=[_sds(dproj.shape, BF)] + [_sds((CONV_K, D_GDN))] * 3,
        input_output_aliases={9: 0}, compiler_params=_params("parallel"),
    )(proj, proj, proj, convw, convw, convw, dqn, dkn, dcv, dproj)


def _mix_out(fox_n, gdn_o, proj, gnw, w_out, x, pmw, plw, after):
    t = x.shape[0]
    tm = min(MATMUL_BLOCK, t)

    def body(fn_ref, go_ref, gz_ref, gnw_ref, w_ref, x_ref, pmw_ref, plw_ref, x1_ref, h2_ref, mixed_ref, omix_ref,
             h2t_ref):
        omix_ref[:, 0:D_FOX] = fn_ref[...]
        for hd in range(N_GDN_HEADS):
            cs = slice(hd * LANES, (hd + 1) * LANES)
            go = go_ref[:, cs]
            r = lax.rsqrt(jnp.mean(go * go, axis=-1, keepdims=True) + EPS)
            gz = gz_ref[:, cs]
            omix_ref[:, D_FOX + hd * LANES:D_FOX + (hd + 1) * LANES] = (
                go * r * gnw_ref[...] * (gz * _sigmoid(gz))).astype(BF)
        mixed = jnp.dot(omix_ref[...], w_ref[...], preferred_element_type=F32)
        mixed_ref[...] = mixed
        r2 = lax.rsqrt(jnp.mean(mixed * mixed, axis=-1, keepdims=True) + EPS)
        x1 = x_ref[...] + mixed * r2 * pmw_ref[...]
        x1_ref[...] = x1
        r3 = lax.rsqrt(jnp.mean(x1 * x1, axis=-1, keepdims=True) + EPS)
        h2 = x1 * r3 * plw_ref[...]
        h2_ref[...] = h2.astype(BF)
        h2t_ref[...] = h2.T.astype(BF)

    tok = lambda w: pl.BlockSpec((tm, w), lambda i: (i, 0))
    vec = lambda w: pl.BlockSpec((1, w), lambda i: (0, 0))
    return pl.pallas_call(
        _ordered(body), name="mix_out", grid=(t // tm,),
        in_specs=[ANY_SPEC, tok(D_FOX), tok(D_GDN), pl.BlockSpec((tm, D_GDN), lambda i: (i, COL_GZ // D_GDN)), vec(LANES),
                  pl.BlockSpec((D_MODEL, D_MODEL), lambda i: (0, 0)), tok(D_MODEL), vec(D_MODEL), vec(D_MODEL)],
        out_specs=[tok(D_MODEL)] * 4 + [pl.BlockSpec((D_MODEL, tm), lambda i: (0, i))],
        out_shape=[_sds((t, D_MODEL)), _sds((t, D_MODEL), BF), _sds((t, D_MODEL)), _sds((t, D_MODEL), BF),
                   _sds((D_MODEL, t), BF)],
        compiler_params=_params("parallel"),
    )(after, fox_n, gdn_o, proj, gnw, w_out, x, pmw, plw)


def _out_bwd(dmixed, w_out, o_fox, gdn_o, proj, fnw, gnw, after):
    t = dmixed.shape[0]
    tm = min(MATMUL_BLOCK, t)

    def body(dm_ref, w_ref, of_ref, go_ref, gz_ref, fnw_ref, gnw_ref, dof_ref, dgo_ref, dgz_ref, dfw_ref, dgw_ref):
        i = pl.program_id(0)

        @pl.when(i == 0)
        def _():
            dfw_ref[...] = jnp.zeros_like(dfw_ref)
            dgw_ref[...] = jnp.zeros_like(dgw_ref)

        domix = _mm_nt(dm_ref[...], w_ref[...])
        first = _iota((1, LANES), 1) < FOX_HEAD_DIM
        dfw = jnp.zeros((1, LANES), F32)
        dgw = jnp.zeros((1, LANES), F32)
        for pr in range(N_FOX_HEADS // 2):
            cs = slice(pr * LANES, (pr + 1) * LANES)
            o = of_ref[:, cs]
            dfn = domix[:, cs]
            o2 = o * o
            s0 = jnp.sum(jnp.where(first, o2, 0.0), axis=-1, keepdims=True)
            s1 = jnp.sum(jnp.where(first, 0.0, o2), axis=-1, keepdims=True)
            r = lax.rsqrt(jnp.where(first, s0, s1) * (1.0 / FOX_HEAD_DIM) + EPS)
            oh = o * r
            dfw = dfw + jnp.sum(dfn * oh, axis=0, keepdims=True)
            doh = dfn * fnw_ref[...]
            pr_ = doh * oh
            m0 = jnp.sum(jnp.where(first, pr_, 0.0), axis=-1, keepdims=True)
            m1 = jnp.sum(jnp.where(first, 0.0, pr_), axis=-1, keepdims=True)
            dof_ref[:, cs] = r * (doh - oh * jnp.where(first, m0, m1) * (1.0 / FOX_HEAD_DIM))
        for hd in range(N_GDN_HEADS):
            cs = slice(hd * LANES, (hd + 1) * LANES)
            go = go_ref[:, cs]
            gz = gz_ref[:, cs]
            dgated = domix[:, D_FOX + hd * LANES:D_FOX + (hd + 1) * LANES]
            r = lax.rsqrt(jnp.mean(go * go, axis=-1, keepdims=True) + EPS)
            goh = go * r
            sg = _sigmoid(gz)
            sz = gz * sg
            gn = goh * gnw_ref[...]
            dgn = dgated * sz
            dgz_ref[:, cs] = (dgated * gn * sg * (1.0 + gz * (1.0 - sg))).astype(BF)
            dgw = dgw + jnp.sum(dgn * goh, axis=0, keepdims=True)
            dgh = dgn * gnw_ref[...]
            dgo_ref[:, cs] = r * (dgh - goh * jnp.mean(dgh * goh, axis=-1, keepdims=True))
        dfw_ref[...] += dfw + pltpu.roll(dfw, FOX_HEAD_DIM, 1)
        dgw_ref[...] += dgw

    tok = lambda w: pl.BlockSpec((tm, w), lambda i: (i, 0))
    vec = lambda w: pl.BlockSpec((1, w), lambda i: (0, 0))
    return pl.pallas_call(
        _ordered(body), name="out_bwd", grid=(t // tm,),
        in_specs=[ANY_SPEC, tok(D_MODEL), pl.BlockSpec((D_MODEL, D_MODEL), lambda i: (0, 0)), tok(D_FOX), tok(D_GDN),
                  pl.BlockSpec((tm, D_GDN), lambda i: (i, COL_GZ // D_GDN)), vec(LANES), vec(LANES)],
        out_specs=[tok(D_FOX), tok(D_GDN), pl.BlockSpec((tm, D_GDN), lambda i: (i, COL_GZ // D_GDN)), vec(LANES),
                   vec(LANES)],
        out_shape=[_sds((t, D_FOX)), _sds((t, D_GDN)), _sds((t, PROJ_W), BF), _sds((1, LANES)), _sds((1, LANES))],
        compiler_params=_params("arbitrary"),
    )(after, dmixed, w_out, o_fox, gdn_o, proj, fnw, gnw)


def _mlp_up(h2, w_upt):
    t = h2.shape[0]
    tm = min(MATMUL_BLOCK, t)

    def body(h_ref, w_ref, up_ref):
        up_ref[...] = lax.dot_general(h_ref[...], w_ref[...], (((1,), (1,)), ((), ())),
                                      preferred_element_type=F32).astype(BF)

    return pl.pallas_call(
        body, name="mlp_up", grid=(t // tm,),
        in_specs=[pl.BlockSpec((tm, D_MODEL), lambda i: (i, 0)), pl.BlockSpec((D_FF, D_MODEL), lambda i: (0, 0))],
        out_specs=pl.BlockSpec((tm, D_FF), lambda i: (i, 0)), out_shape=_sds((t, D_FF), BF),
        compiler_params=_params("parallel"),
    )(h2, w_upt)


def _mlp_down_loss(up, w_down, x1, pw, target):
    t = up.shape[0]
    tm = min(MATMUL_BLOCK, t)

    def body(up_ref, w_ref, x1_ref, pw_ref, tg_ref, dy_ref, dx2_ref, loss_ref, dpw_ref):
        i = pl.program_id(0)

        @pl.when(i == 0)
        def _():
            loss_ref[...] = jnp.zeros_like(loss_ref)
            dpw_ref[...] = jnp.zeros_like(dpw_ref)

        u = jnp.maximum(up_ref[...].astype(F32), 0.0)
        y = jnp.dot((u * u).astype(BF), w_ref[...], preferred_element_type=F32)
        r = lax.rsqrt(jnp.mean(y * y, axis=-1, keepdims=True) + EPS)
        yh = y * r
        pw = pw_ref[...]
        err = x1_ref[...] + yh * pw - tg_ref[...]
        part = jnp.sum(jnp.sum(err * err, axis=-1, keepdims=True), axis=0, keepdims=True) * (0.5 / D_MODEL)
        loss_ref[...] += jnp.broadcast_to(part, loss_ref.shape)
        dx2 = err * (1.0 / D_MODEL)
        dx2_ref[...] = dx2
        dpw_ref[...] += jnp.sum(dx2 * yh, axis=0, keepdims=True)
        dyh = dx2 * pw
        dy_ref[...] = (r * (dyh - yh * jnp.mean(dyh * yh, axis=-1, keepdims=True))).astype(BF)

    tok = lambda w: pl.BlockSpec((tm, w), lambda i: (i, 0))
    vec = lambda w: pl.BlockSpec((1, w), lambda i: (0, 0))
    return pl.pallas_call(
        body, name="mlp_down_loss", grid=(t // tm,),
        in_specs=[tok(D_FF), pl.BlockSpec((D_FF, D_MODEL), lambda i: (0, 0)), tok(D_MODEL), vec(D_MODEL), tok(D_MODEL)],
        out_specs=[tok(D_MODEL), tok(D_MODEL), vec(LANES), vec(D_MODEL)],
        out_shape=[_sds((t, D_MODEL), BF), _sds((t, D_MODEL)), _sds((1, LANES)), _sds((1, D_MODEL))],
        compiler_params=_params("arbitrary"),
    )(up, w_down, x1, pw, target)


def _mlp_bwd_act(dy, w_down, up):
    t = dy.shape[0]
    tm = min(MATMUL_BLOCK, t)

    def body(dy_ref, w_ref, up_ref, dup_ref):
        da = lax.dot_general(dy_ref[...], w_ref[...], (((1,), (1,)), ((), ())), preferred_element_type=F32)
        dup_ref[...] = (da * (2.0 * jnp.maximum(up_ref[...].astype(F32), 0.0))).astype(BF)

    return pl.pallas_call(
        body, name="mlp_bwd_act", grid=(t // tm,),
        in_specs=[pl.BlockSpec((tm, D_MODEL), lambda i: (i, 0)), pl.BlockSpec((D_FF, D_MODEL), lambda i: (0, 0)),
                  pl.BlockSpec((tm, D_FF), lambda i: (i, 0))],
        out_specs=pl.BlockSpec((tm, D_FF), lambda i: (i, 0)), out_shape=_sds((t, D_FF), BF),
        compiler_params=_params("parallel"),
    )(dy, w_down, up)


def _mlp_bwd_in(dup, w_up, x1, plw, dx2, mixed, pmw, after):
    t = dup.shape[0]
    tm = min(MATMUL_BLOCK, t)

    def body(dup_ref, w_ref, x1_ref, plw_ref, dx2_ref, mx_ref, pmw_ref, dx1_ref, dmixed_ref, dplw_ref, dpmw_ref):
        i = pl.program_id(0)

        @pl.when(i == 0)
        def _():
            dplw_ref[...] = jnp.zeros_like(dplw_ref)
            dpmw_ref[...] = jnp.zeros_like(dpmw_ref)

        dh = jnp.dot(dup_ref[...], w_ref[...], preferred_element_type=F32)
        x1 = x1_ref[...]
        r = lax.rsqrt(jnp.mean(x1 * x1, axis=-1, keepdims=True) + EPS)
        xh = x1 * r
        dplw_ref[...] += jnp.sum(dh * xh, axis=0, keepdims=True)
        dxh = dh * plw_ref[...]
        dx1 = dx2_ref[...] + r * (dxh - xh * jnp.mean(dxh * xh, axis=-1, keepdims=True))
        dx1_ref[...] = dx1
        mx = mx_ref[...]
        r2 = lax.rsqrt(jnp.mean(mx * mx, axis=-1, keepdims=True) + EPS)
        mh = mx * r2
        dpmw_ref[...] += jnp.sum(dx1 * mh, axis=0, keepdims=True)
        dmh = dx1 * pmw_ref[...]
        dmixed_ref[...] = (r2 * (dmh - mh * jnp.mean(dmh * mh, axis=-1, keepdims=True))).astype(BF)

    tok = lambda w: pl.BlockSpec((tm, w), lambda i: (i, 0))
    vec = lambda w: pl.BlockSpec((1, w), lambda i: (0, 0))
    return pl.pallas_call(
        _ordered(body), name="mlp_bwd_in", grid=(t // tm,),
        in_specs=[ANY_SPEC, tok(D_FF), pl.BlockSpec((D_FF, D_MODEL), lambda i: (0, 0)), tok(D_MODEL),
                  vec(D_MODEL), tok(D_MODEL), tok(D_MODEL), vec(D_MODEL)],
        out_specs=[tok(D_MODEL), tok(D_MODEL), vec(D_MODEL), vec(D_MODEL)],
        out_shape=[_sds((t, D_MODEL)), _sds((t, D_MODEL), BF), _sds((1, D_MODEL)), _sds((1, D_MODEL))],
        compiler_params=_params("arbitrary"),
    )(after, dup, w_up, x1, plw, dx2, mixed, pmw)


def _wgrad(a, b, a_cols, split=1, a_fn=None, name="wgrad"):
    t, b_cols = b.shape
    n_a = a.shape[1] // a_cols

    def body(a_ref, b_ref, o_ref):
        av = a_ref[...]
        if a_fn is not None:
            av = a_fn(av)
        o_ref[...] = _mm_tn(av, b_ref[...]).astype(BF).reshape(o_ref.shape)

    return pl.pallas_call(
        body, name=name, grid=(n_a,),
        in_specs=[pl.BlockSpec((t, a_cols), lambda i: (0, i)), pl.BlockSpec((t, b_cols), lambda i: (0, 0))],
        out_specs=pl.BlockSpec((split, a_cols // split, b_cols), lambda i: (i, 0, 0)),
        out_shape=_sds((n_a * split, a_cols // split, b_cols), BF),
        compiler_params=_params("parallel"),
    )(a, b)


def _wgrad_pre_t(at, b, b_cols, name):
    rows, t = at.shape
    n_b = b.shape[1] // b_cols

    def body(a_ref, b_ref, o_ref):
        o_ref[0] = jnp.dot(a_ref[...], b_ref[...], preferred_element_type=F32).astype(BF)

    return pl.pallas_call(
        body, name=name, grid=(n_b,),
        in_specs=[pl.BlockSpec((rows, t), lambda j: (0, 0)), pl.BlockSpec((t, b_cols), lambda j: (0, j))],
        out_specs=pl.BlockSpec((1, rows, b_cols), lambda j: (j, 0, 0)), out_shape=_sds((n_b, rows, b_cols), BF),
        compiler_params=_params("parallel"),
    )(at, b)


def _small_bwd(proj, fb, al, dtb, dcq, dckt, dbe, dge, h, dproj):
    t = proj.shape[0]

    def body(sm_ref, fb_ref, al_ref, dtb_ref, dcq_ref, dckt_ref, dbe_ref, dge_ref, h_ref, _, dsm_ref, dvec_ref,
             gw_ref):
        s = sm_ref[...]
        lane = _iota((1, LANES), 1)
        dcum = dcq_ref[...] - dckt_ref[...].T
        row = _iota((t, LANES), 0)
        step = 1
        while step < t:
            dcum = dcum + _shift_up(dcum, step, row)
            step *= 2
        dff = dcum * _sigmoid(-(s + fb_ref[...]))
        dbeta = jnp.zeros((t, LANES), F32)
        dg = jnp.zeros((t, LANES), F32)
        for hd in range(N_GDN_HEADS):
            dbeta = jnp.where(lane == SM_GB + hd, dbe_ref[:, hd * LANES:hd * LANES + 1], dbeta)
            dg = jnp.where(lane == SM_GA + hd, dge_ref[:, hd * LANES:hd * LANES + 1], dg)
        beta = _sigmoid(s)
        dgb = dbeta * beta * (1.0 - beta)
        za = s + dtb_ref[...]
        nea = -jnp.exp(al_ref[...])
        dga = dg * nea * _sigmoid(za)
        is_f = lane < SM_GB
        is_b = (lane >= SM_GB) & (lane < SM_GA)
        is_a = (lane >= SM_GA) & (lane < SM_GA + 4)
        dsm = jnp.where(is_f, dff, jnp.where(is_b, dgb, jnp.where(is_a, dga, 0.0))).astype(BF)
        dsm_ref[...] = dsm
        gw_ref[0] = _mm_tn(dsm, h_ref[...]).astype(BF)
        dvec_ref[...] = jnp.zeros_like(dvec_ref)
        dvec_ref[0:1, :] = jnp.sum(jnp.where(is_f, dff, 0.0), axis=0, keepdims=True)
        dvec_ref[1:2, :] = jnp.sum(jnp.where(is_a, dg * nea * _softplus(za), 0.0), axis=0, keepdims=True)
        dvec_ref[2:3, :] = jnp.sum(jnp.where(is_a, dga, 0.0), axis=0, keepdims=True)

    vec = pl.BlockSpec((1, LANES), lambda i: (0, 0))
    full = lambda r, c: pl.BlockSpec((r, c), lambda i: (0, 0))
    small = pl.BlockSpec((t, LANES), lambda i: (0, COL_SMALL // LANES))
    return pl.pallas_call(
        body, name="small_bwd", grid=(1,),
        in_specs=[small, vec, vec, vec, full(t, LANES), full(LANES, t), full(t, 512), full(t, 512), full(t, D_MODEL),
                  ANY_SPEC],
        out_specs=[small, full(8, LANES), pl.BlockSpec((1, LANES, D_MODEL), lambda i: (0, 0, 0))],
        out_shape=[_sds(dproj.shape, BF), _sds((8, LANES)), _sds((1, LANES, D_MODEL), BF)],
        input_output_aliases={9: 0}, compiler_params=_params("arbitrary"),
    )(proj, fb, al, dtb, dcq, dckt, dbe, dge, h, dproj)


def _in_bwd(dproj, wt_al, x, nw, dx1, after):
    t = x.shape[0]
    tm = min(MATMUL_BLOCK, t)

    def body(dp_ref, w_ref, x_ref, nw_ref, dx1_ref, dx_ref, dnw_ref):
        i = pl.program_id(0)

        @pl.when(i == 0)
        def _():
            dnw_ref[...] = jnp.zeros_like(dnw_ref)

        dh = jnp.dot(dp_ref[...], w_ref[...], preferred_element_type=F32)
        xv = x_ref[...]
        r = lax.rsqrt(jnp.mean(xv * xv, axis=-1, keepdims=True) + EPS)
        xh = xv * r
        dnw_ref[...] += jnp.sum(dh * xh, axis=0, keepdims=True)
        dxh = dh * nw_ref[...]
        dx_ref[...] = dx1_ref[...] + r * (dxh - xh * jnp.mean(dxh * xh, axis=-1, keepdims=True))

    tok = lambda w: pl.BlockSpec((tm, w), lambda i: (i, 0))
    vec = lambda w: pl.BlockSpec((1, w), lambda i: (0, 0))
    return pl.pallas_call(
        _ordered(body), name="in_bwd", grid=(t // tm,),
        in_specs=[ANY_SPEC, tok(PROJ_W), pl.BlockSpec((PROJ_W, D_MODEL), lambda i: (0, 0)), tok(D_MODEL), vec(D_MODEL),
                  tok(D_MODEL)],
        out_specs=[tok(D_MODEL), vec(D_MODEL)], out_shape=[_sds((t, D_MODEL)), _sds((1, D_MODEL))],
        compiler_params=_params("arbitrary"),
    )(after, dproj, wt_al, x, nw, dx1)


def _row(v, width=None):
    v = v.reshape(1, -1).astype(F32)
    if width is not None and v.shape[1] < width:
        v = jnp.pad(v, ((0, 0), (0, width - v.shape[1])))
    return v


def _lane_vec(v, first):
    return jnp.pad(v.astype(F32), (first, LANES - first - v.shape[0])).reshape(1, LANES)


def _local_step(x, target, wt_al, started, late_weights, on_grads, convw, pre_mix_norm, fox_f_bias, fox_out_norm,
                gdn_a_log, gdn_dt_bias, gdn_out_norm, post_mix_norm, pre_mlp_norm, post_mlp_norm):
    t = x.shape[0]
    nch = t // CHUNK
    nw, pmw, plw, pw = _row(pre_mix_norm), _row(post_mix_norm), _row(pre_mlp_norm), _row(post_mlp_norm)
    fb, al, dtb = _lane_vec(fox_f_bias, SM_FF), _lane_vec(gdn_a_log, SM_GA), _lane_vec(gdn_dt_bias, SM_GA)
    fnw = _row(jnp.tile(fox_out_norm, 2))
    gnw = _row(gdn_out_norm)

    proj, h = _norm_proj(x, nw, wt_al, started)
    cumt, beta, g = _small_prep(proj, fb, al, dtb)
    qn, kn, cv, gc, be, mmat, amat = _gdn_prep(proj, convw, beta, g)
    n_prob = N_GDN_HEADS * nch
    m3 = mmat.reshape(n_prob, CHUNK, CHUNK)
    if n_prob < LANES:
        m3 = jnp.pad(m3, ((0, LANES - n_prob), (0, 0), (0, 0)))
    tinv = _tri_inverse(m3)[:n_prob].reshape(N_GDN_HEADS, nch, CHUNK, CHUNK)
    gdn_o, s_all, vn_all = _gdn_scan(qn, kn, cv, be, gc, tinv, amat)
    token = late_weights("mlp_relay", gdn_o)
    o_fox, lse, fox_n = _fox_fwd(proj, cumt, fnw, token)
    w_out = late_weights("w_out", fox_n)
    x1, h2, mixed, omix, h2t = _mix_out(fox_n, gdn_o, proj, gnw, w_out, x, pmw, plw, token)
    w_up, w_down = late_weights("mlp", h2)
    up = _mlp_up(h2, w_up)
    dy, dx2, loss, d_pw = _mlp_down_loss(up, w_down, x1, pw, target)

    dup = _mlp_bwd_act(dy, w_down, up)
    relu2 = lambda u: jnp.square(jnp.maximum(u.astype(F32), 0.0))
    g_down = _wgrad(up, dy, D_FF // N_DEV, a_fn=relu2, name="wgrad_down")
    g_up = _wgrad_pre_t(h2t, dup, D_FF // N_DEV, name="wgrad_up")
    token = on_grads("mlp", (g_up, g_down))
    dx1, dmixed, d_plw, d_pmw = _mlp_bwd_in(dup, w_up, x1, plw, dx2, mixed, pmw, token)
    g_out = _wgrad(omix, dmixed, 512, split=4, name="wgrad_out")
    do_fox, dgo, dproj, d_fnw, d_gnw = _out_bwd(dmixed, w_out, o_fox, gdn_o, proj, fnw, gnw, g_out)
    token = on_grads("w_out", g_out)
    dqn, dkn, dcv, dbe, dge = _gdn_bwd(qn, kn, cv, be, gc, tinv, amat, s_all, vn_all, dgo, token)
    dproj, dcq, dckt = _fox_bwd(proj, cumt, lse, o_fox, do_fox, dproj, token)
    dproj, dwq, dwk, dwv = _gdn_bwd_conv(proj, convw, dqn, dkn, dcv, dproj)
    dproj, dvec, g_tail = _small_bwd(proj, fb, al, dtb, dcq, dckt, dbe, dge, h, dproj)
    g_main = _wgrad(dproj, h, WGRAD_IN_ROWS, name="wgrad_in")
    token = on_grads("w_in", (g_main, g_tail))
    grad_x, d_nw = _in_bwd(dproj, wt_al, x, nw, dx1, token)
    small = dict(norms=(d_nw, d_pmw, d_plw, d_pw), fox_out_norm=d_fnw, gdn_out_norm=d_gnw, loss=loss, vectors=dvec,
                 conv=(dwq, dwk, dwv))
    return grad_x, small


MESH_IDS = pl.DeviceIdType.MESH
CHIP_FLIPS = ((0, 0), (1, 0), (0, 1), (1, 1))


def _place():
    return lax.axis_index("x"), lax.axis_index("y"), lax.axis_index("c")


def _all_gather(blocks, later, dtype):
    n, k = len(blocks), len(later)

    def body(*refs):
        ins, shards, outs = refs[:n], refs[n:n + k], refs[n + k:2 * n + k]
        zones, to_send = refs[2 * n + k:2 * n + 2 * k], refs[2 * n + 2 * k:2 * n + 3 * k]
        stage_in, stage_out = refs[2 * n + 3 * k:2 * n + 4 * k], refs[2 * n + 4 * k:2 * n + 5 * k]
        send_sems, recv_sems, local_sems, late_sems = refs[2 * n + 5 * k:]
        x, y, c = _place()
        sibling = (x, y, 1 - c)
        chips = [(x ^ fx, y ^ fy) for fx, fy in CHIP_FLIPS[1:]]

        def slot(out, px, py, pc):
            return out.at[4 * px + 2 * py + pc]

        def copy(a, k, block, to, src=None):
            return pltpu.make_async_remote_copy(
                src_ref=slot(outs[a], *block) if src is None else src, dst_ref=slot(outs[a], *block),
                send_sem=send_sems.at[a, k], recv_sem=recv_sems.at[a, k], device_id=to, device_id_type=MESH_IDS)

        pending = []
        for a in range(n):
            mine = pltpu.make_async_copy(ins[a], slot(outs[a], x, y, c), local_sems.at[a])
            mine.start()
            pending.append(mine)
        sends = []
        for a in range(n):
            first = [copy(a, 1 + j, (x, y, c), (*chip, c), src=ins[a]) for j, chip in enumerate(chips)][::-1]
            first.append(copy(a, 0, (x, y, c), sibling, src=ins[a]))
            for cp in first:
                cp.start()
            sends += first
        loads = [pltpu.make_async_copy(shards[a], stage_in[a], late_sems.at[a, 0]) for a in range(k)]
        for cp in loads:
            cp.start()
        for a, (_, transposed) in enumerate(later):
            loads[a].wait()
            val = stage_in[a][...]
            stage_out[a][...] = (val.T if transposed else val).astype(dtype)
            for j, dst in enumerate((slot(zones[a], x, y, c), to_send[a])):
                cp = pltpu.make_async_copy(stage_out[a], dst, late_sems.at[a, 1 + j])
                cp.start()
                pending.append(cp)
        for a in range(n):
            for j, chip in reversed(list(enumerate(chips))):
                copy(a, 1 + j, (*chip, c), (x, y, c)).wait_recv()
                fwd = copy(a, 4 + j, (*chip, c), sibling)
                fwd.start()
                sends.append(fwd)
        for a in range(n):
            copy(a, 0, sibling, (x, y, c)).wait_recv()
            for j, chip in enumerate(chips):
                copy(a, 4 + j, (*chip, 1 - c), (x, y, c)).wait_recv()
        for cp in sends:
            cp.wait_send()
        for cp in pending:
            cp.wait()

    shapes = [s_.shape[::-1] if transposed else s_.shape for s_, transposed in later]
    out = pl.pallas_call(
        body, name="all_gather_weights", in_specs=[ANY_SPEC] * (n + k), out_specs=[ANY_SPEC] * (n + 2 * k),
        out_shape=[_sds((N_DEV,) + b.shape, b.dtype) for b in blocks] + [_sds((N_DEV,) + sh, dtype) for sh in shapes]
        + [_sds(sh, dtype) for sh in shapes],
        scratch_shapes=[pltpu.VMEM(s_.shape, s_.dtype) for s_, _ in later] + [pltpu.VMEM(sh, dtype) for sh in shapes]
        + [pltpu.SemaphoreType.DMA((n, 7)), pltpu.SemaphoreType.DMA((n, 7)), pltpu.SemaphoreType.DMA((n,)),
           pltpu.SemaphoreType.DMA((k, 3))],
        compiler_params=pltpu.CompilerParams(vmem_limit_bytes=VMEM_LIMIT, has_side_effects=True),
    )(*blocks, *[s_ for s_, _ in later])
    return out[:n], out[n:n + k], out[n + k:]


def _adamw(w, g, m, v):
    m = ADAM_B1 * m + (1.0 - ADAM_B1) * g
    v = ADAM_B2 * v + (1.0 - ADAM_B2) * (g * g)
    m_hat = m / (1.0 - ADAM_B1 ** ADAM_STEP)
    v_hat = v / (1.0 - ADAM_B2 ** ADAM_STEP)
    return -ADAM_LR * (m_hat / (jnp.sqrt(v_hat) + ADAM_EPS) + ADAM_WD * w), m, v


def _pair_reduce(g, name):
    _, r, c_ = g.shape
    n = len(CHIP_FLIPS)

    def body(g_ref, out_ref, own_buf, sib_buf, send_sems, recv_sems, own_sems):
        x, y, c = _place()
        chips = [(x ^ fx, y ^ fy) for fx, fy in CHIP_FLIPS]
        piece = lambda chip, core: g_ref.at[4 * chip[0] + 2 * chip[1] + core]
        copies = [pltpu.make_async_remote_copy(
            src_ref=piece(chip, 1 - c), dst_ref=sib_buf.at[j], send_sem=send_sems.at[j], recv_sem=recv_sems.at[j],
            device_id=(x, y, 1 - c), device_id_type=MESH_IDS) for j, chip in enumerate(chips)]
        loads = [pltpu.make_async_copy(piece(chip, c), own_buf.at[j], own_sems.at[j]) for j, chip in enumerate(chips)]
        for cp in copies + loads:
            cp.start()
        for j in range(n):
            loads[j].wait()
            copies[j].wait_recv()
            out_ref[j] = (own_buf[j].astype(F32) + sib_buf[j].astype(F32)).astype(BF)
        for cp in copies:
            cp.wait_send()

    return pl.pallas_call(
        body, name=name, in_specs=[ANY_SPEC], out_specs=VMEM_SPEC, out_shape=_sds((n, r, c_), BF),
        scratch_shapes=[pltpu.VMEM((n, r, c_), BF), pltpu.VMEM((n, r, c_), BF), pltpu.SemaphoreType.DMA((n,)),
                        pltpu.SemaphoreType.DMA((n,)), pltpu.SemaphoreType.DMA((n,))],
        compiler_params=pltpu.CompilerParams(vmem_limit_bytes=VMEM_LIMIT, has_side_effects=True),
    )(g)


HBM_SPEC = pl.BlockSpec(memory_space=pltpu.HBM)
SEM_SPEC = pl.BlockSpec(memory_space=pltpu.SEMAPHORE)
DATAFLOW = pltpu.SideEffectType.DATAFLOW_SIDE_EFFECTING


def _peers():
    x, y, c = _place()
    return 4 * x + 2 * y + c, [(x ^ (k >> 2), y ^ ((k >> 1) & 1), c ^ (k & 1)) for k in range(1, N_DEV)]


def _peer_index(peer):
    return 4 * peer[0] + 2 * peer[1] + peer[2]


def _exchange_start(srcs, zones, pieces, name, chips=False):
    n = len(srcs)
    fresh = zones is None
    if fresh:
        slots = len(CHIP_FLIPS) if chips else N_DEV
        zones = [_sds((slots,) + (v.shape[1:] if pieces else v.shape), v.dtype) for v in srcs]
    n_in = n if fresh else 2 * n
    among_chips = list(chips) if isinstance(chips, (list, tuple)) else [chips] * n

    def body(*refs):
        ins, sems, token = refs[:n], refs[n_in:n_in + 2 * n], refs[-1]
        zs = refs[n_in + 3 * n:n_in + 4 * n] if fresh else refs[n:2 * n]
        me, peers = _peers()
        x, y, c = _place()
        for a in range(n):
            if among_chips[a] and pieces:
                routes = [((x ^ fx, y ^ fy, c), j, j) for j, (fx, fy) in enumerate(CHIP_FLIPS) if j]
            elif among_chips[a]:
                routes = [((x ^ fx, y ^ fy, c), None, me) for fx, fy in CHIP_FLIPS[1:]]
            else:
                routes = [(peer, _peer_index(peer) if pieces else None, me) for peer in peers]
            for peer, src_slot, dst_slot in routes:
                pltpu.make_async_remote_copy(
                    src_ref=ins[a] if src_slot is None else ins[a].at[src_slot], dst_ref=zs[a].at[dst_slot],
                    send_sem=sems[2 * a], recv_sem=sems[2 * a + 1], device_id=peer, device_id_type=MESH_IDS).start()
        token[...] = jnp.zeros_like(token)

    hbm = lambda v: pltpu.with_memory_space_constraint(v, pltpu.HBM)
    out = pl.pallas_call(
        body, name=name,
        out_shape=tuple([pltpu.SemaphoreType.DMA(())] * (2 * n) + [pltpu.HBM(v.shape, v.dtype) for v in srcs]
                        + [pltpu.HBM(z.shape, z.dtype) for z in zones] + [_sds((8, LANES))]),
        in_specs=[HBM_SPEC] * n_in, out_specs=tuple([SEM_SPEC] * (2 * n) + [HBM_SPEC] * (2 * n) + [VMEM_SPEC]),
        input_output_aliases={i: 2 * n + i for i in range(n_in)},
        compiler_params=pltpu.CompilerParams(has_side_effects=DATAFLOW),
    )(*[hbm(v) for v in srcs], *([] if fresh else [hbm(z) for z in zones]))
    return out[:2 * n], out[2 * n:3 * n], out[3 * n:4 * n], out[-1]


def _relay_start(zones, name):
    n = len(zones)

    def body(*refs):
        zs, sems, token = refs[:n], refs[n:3 * n], refs[-1]
        x, y, c = _place()
        for fx, fy in CHIP_FLIPS:
            slot = 4 * (x ^ fx) + 2 * (y ^ fy) + c
            for a in range(n):
                pltpu.make_async_remote_copy(
                    src_ref=zs[a].at[slot], dst_ref=zs[a].at[slot], send_sem=sems[2 * a], recv_sem=sems[2 * a + 1],
                    device_id=(x, y, 1 - c), device_id_type=MESH_IDS).start()
        token[...] = jnp.zeros_like(token)

    out = pl.pallas_call(
        body, name=name,
        out_shape=tuple([pltpu.SemaphoreType.DMA(())] * (2 * n) + [pltpu.HBM(z.shape, z.dtype) for z in zones]
                        + [_sds((8, LANES))]),
        in_specs=[HBM_SPEC] * n, out_specs=tuple([SEM_SPEC] * (2 * n) + [HBM_SPEC] * n + [VMEM_SPEC]),
        input_output_aliases={i: 2 * n + i for i in range(n)},
        compiler_params=pltpu.CompilerParams(has_side_effects=DATAFLOW),
    )(*[pltpu.with_memory_space_constraint(z, pltpu.HBM) for z in zones])
    return out[:2 * n], [], out[2 * n:3 * n], out[-1]


def _exchange_wait(sems, srcs, zones, after, name, chips=False, n_copies=None):
    n, n_src = len(zones), len(srcs)
    after = list(after) if isinstance(after, (list, tuple)) else [after]
    n_copies = n_copies or (len(CHIP_FLIPS) - 1 if chips else N_DEV - 1)

    def body(*refs):
        zs, sm = refs[n_src:n_src + n], refs[n_src + n:n_src + 3 * n]
        me, peers = _peers()
        for a in range(n):
            seven = zs[a].at[pl.ds(0, n_copies)]
            cp = pltpu.make_async_remote_copy(src_ref=seven, dst_ref=seven, send_sem=sm[2 * a], recv_sem=sm[2 * a + 1],
                                              device_id=peers[0], device_id_type=MESH_IDS)
            cp.wait_send()
            cp.wait_recv()

    out = pl.pallas_call(
        body, name=name, out_shape=tuple([pltpu.HBM(v.shape, v.dtype) for v in srcs] + [pltpu.HBM(z.shape, z.dtype) for z in zones]),
        in_specs=[HBM_SPEC] * (n_src + n) + [SEM_SPEC] * (2 * n) + [ANY_SPEC] * len(after),
        out_specs=tuple([HBM_SPEC] * (n_src + n)), input_output_aliases={i: i for i in range(n_src + n)},
        compiler_params=pltpu.CompilerParams(has_side_effects=DATAFLOW),
    )(*srcs, *zones, *sems, *after)
    return out[:n_src], out[n_src:]


def _sum_adamw(zone, own, w, m, v, name, chips=False):
    n_slots, r, c_ = zone.shape
    rb = next((b for b in (256, 128) if r % b == 0), r)

    def body(me_ref, z_ref, own_ref, w_ref, m_ref, v_ref, grad_ref, delta_ref, nm_ref, nv_ref):
        total = None
        for d in range(n_slots):
            part = jnp.where(me_ref[0] == d, own_ref[0], z_ref[d]).astype(F32)
            total = part if total is None else total + part
        grad_ref[...] = total
        delta_ref[...], nm_ref[...], nv_ref[...] = _adamw(w_ref[...], total, m_ref[...], v_ref[...])

    x, y, c = _place()
    mine = 0 * x if chips else 4 * x + 2 * y + c
    blk = pl.BlockSpec((rb, c_), lambda i, me_ref: (i, 0))
    return pl.pallas_call(
        body, name=name,
        grid_spec=pltpu.PrefetchScalarGridSpec(
            num_scalar_prefetch=1, grid=(r // rb,),
            in_specs=[pl.BlockSpec((n_slots, rb, c_), lambda i, me_ref: (0, i, 0)),
                      pl.BlockSpec((1, rb, c_), lambda i, me_ref: (me_ref[0], i, 0)), blk, blk, blk],
            out_specs=[blk] * 4),
        out_shape=[_sds((r, c_))] * 4, compiler_params=_params("parallel"),
    )(mine.astype(jnp.int32).reshape(1), zone, own, w, m, v)


SMALL_NORMS = ("pre_mix_norm", "post_mix_norm", "pre_mlp_norm", "post_mlp_norm")
SMALL_ORDER = SMALL_NORMS + ("fox_out_norm", "gdn_out_norm", "fox_f_bias", "gdn_a_log", "gdn_dt_bias", "gdn_conv_w")
CONV_SLAB_ROWS, CONV_SLAB_LANES = 8, 256


def _small_pack(small):
    def body(n0, n1, n2, n3, fnw_ref, gnw_ref, loss_ref, vec_ref, out_ref):
        out_ref[...] = jnp.zeros_like(out_ref)
        for i, ref in enumerate((n0, n1, n2, n3)):
            out_ref[i:i + 1, :] = ref[...]
        out_ref[4:5, 0:LANES] = fnw_ref[...]
        out_ref[4:5, LANES:2 * LANES] = gnw_ref[...]
        out_ref[4:5, 2 * LANES:3 * LANES] = loss_ref[...]
        out_ref[5:8, 0:LANES] = vec_ref[0:3, :]

    return pl.pallas_call(body, name="small_pack", in_specs=[VMEM_SPEC] * 8, out_specs=VMEM_SPEC,
                          out_shape=_sds((8, D_MODEL)))(*small["norms"], small["fox_out_norm"], small["gdn_out_norm"],
                                                        small["loss"], small["vectors"])


def _conv_slabs(dconv):
    blocks = dconv.reshape(CONV_K, N_DEV, -1).transpose(1, 0, 2)
    blocks = jnp.pad(blocks, ((0, 0), (0, CONV_SLAB_ROWS - CONV_K), (0, CONV_SLAB_LANES - blocks.shape[2])))
    return blocks.reshape(N_DEV * CONV_SLAB_ROWS, CONV_SLAB_LANES)


def _small_update(zone, conv_zone, own, own_conv, w, m, v):
    n = len(SMALL_ORDER)
    n_conv = w["gdn_conv_w"].shape[1]

    def body(me_ref, z_ref, zc_ref, own_ref, ownc_ref, *refs):
        params, loss_ref, outs, (tot, totc) = refs[:3 * n], refs[3 * n], refs[3 * n + 1:7 * n + 1], refs[-2:]
        total, total_c = None, None
        for d in range(N_DEV):
            part = jnp.where(me_ref[0] == d, own_ref[...], z_ref[d])
            part_c = jnp.where(me_ref[0] == d, ownc_ref[...], zc_ref[d])
            total, total_c = (part, part_c) if d == 0 else (total + part, total_c + part_c)
        tot[...] = total
        totc[...] = total_c
        loss_ref[...] = tot[4, 2 * LANES:2 * LANES + 1]
        mine = totc[pl.ds(pl.multiple_of(me_ref[0] * CONV_SLAB_ROWS, CONV_SLAB_ROWS), CONV_SLAB_ROWS), :]
        g = dict(zip(SMALL_NORMS, (tot[0], tot[1], tot[2], tot[3])))
        g.update(fox_out_norm=tot[4, 0:FOX_HEAD_DIM], gdn_out_norm=tot[4, LANES:LANES + GDN_HEAD_DIM],
                 fox_f_bias=tot[5, SM_FF:SM_FF + N_FOX_HEADS], gdn_a_log=tot[6, SM_GA:SM_GA + N_GDN_HEADS],
                 gdn_dt_bias=tot[7, SM_GA:SM_GA + N_GDN_HEADS], gdn_conv_w=mine[0:CONV_K, 0:n_conv])
        for i, name in enumerate(SMALL_ORDER):
            w_ref, m_ref, v_ref = params[3 * i:3 * i + 3]
            outs[4 * i][...] = g[name]
            outs[4 * i + 1][...], outs[4 * i + 2][...], outs[4 * i + 3][...] = _adamw(w_ref[...], g[name], m_ref[...],
                                                                                     v_ref[...])

    x, y, c = _place()
    operands = [a[name] for name in SMALL_ORDER for a in (w, m, v)]
    out = pl.pallas_call(
        body, name="small_update",
        in_specs=[pl.BlockSpec(memory_space=pltpu.SMEM)] + [VMEM_SPEC] * (4 + 3 * n), out_specs=[VMEM_SPEC] * (1 + 4 * n),
        out_shape=[_sds((1,))] + [_sds(w[name].shape) for name in SMALL_ORDER for _ in range(4)],
        scratch_shapes=[pltpu.VMEM(zone.shape[1:], F32), pltpu.VMEM(conv_zone.shape[1:], F32)],
    )((4 * x + 2 * y + c).astype(jnp.int32).reshape(1), zone, conv_zone, own, own_conv, *operands)
    return out[0][0], {name: out[1 + 4 * i:5 + 4 * i] for i, name in enumerate(SMALL_ORDER)}


def _native_rows():
    groups = []
    for first, n_groups in ((0, N_FOX_HEADS // 2), (D_FOX * 3 + N_FOX_HEADS, N_GDN_HEADS)):
        for g in range(n_groups):
            groups += [(first + part * n_groups * LANES + g * LANES, first + part * n_groups * LANES + (g + 1) * LANES)
                       for part in range(3)]
    return tuple(groups) + ((3088, 3600), (1536, 1544), (3080, 3088))


NATIVE_ROWS = _native_rows()


W_IN_PIECE = D_PROJ // N_DEV
WGRAD_IN_ROWS = 512
SHUFFLE_LANES = 256


def _to_aligned_moves():
    moves, o = [], 0
    for lo, hi in NATIVE_ROWS:
        r = lo
        while r < hi:
            d = r // W_IN_PIECE
            k = min(hi, (d + 1) * W_IN_PIECE) - r
            moves.append((0, d, r - d * W_IN_PIECE, 0, o, k))
            r, o = r + k, o + k
    return moves


def _from_aligned_moves():
    moves = []
    for _, d, a, _, o, k in _to_aligned_moves():
        while k:
            n = min(k, WGRAD_IN_ROWS - o % WGRAD_IN_ROWS) if o < COL_SMALL else k
            moves.append((0, o // WGRAD_IN_ROWS, o % WGRAD_IN_ROWS, d, a, n) if o < COL_SMALL else
                         (1, 0, o - COL_SMALL, d, a, n))
            o, a, k = o + n, a + n, k - n
    return moves


def _shuffle_rows(srcs, moves, out_shape, name):
    c = srcs[0].shape[-1]

    def body(*refs):
        s_refs, o_ref, s_f, o_f = refs[:len(srcs)], refs[len(srcs)], refs[len(srcs) + 1:-1], refs[-1]
        for s_ref, f in zip(s_refs, s_f):
            f[...] = s_ref[...].astype(F32)
        o_f[...] = jnp.zeros_like(o_f)
        for i, ss, so, ds, do, k in moves:
            o_f[ds, pl.ds(do, k), :] = s_f[i][ss, pl.ds(so, k), :]
        o_ref[...] = o_f[...].astype(BF)

    blk = lambda shape: pl.BlockSpec(tuple(shape[:-1]) + (SHUFFLE_LANES,), lambda j: (0, 0, j))
    scratch = lambda shape: pltpu.VMEM(tuple(shape[:-1]) + (SHUFFLE_LANES,), F32)
    return pl.pallas_call(
        body, name=name, grid=(c // SHUFFLE_LANES,), in_specs=[blk(s.shape) for s in srcs], out_specs=blk(out_shape),
        out_shape=_sds(out_shape, BF), scratch_shapes=[scratch(s.shape) for s in srcs] + [scratch(out_shape)],
        compiler_params=_params("parallel"),
    )(*srcs)


def _cols_from_pieces(p):
    return p.transpose(1, 0, 2).reshape(p.shape[1], -1)


WEIGHT_ORDER = ("pre_mix_norm", "w_in", "fox_f_bias", "fox_out_norm", "gdn_conv_w", "gdn_a_log", "gdn_dt_bias",
                "gdn_out_norm", "w_out", "post_mix_norm", "pre_mlp_norm", "w_up", "w_down", "post_mlp_norm")


def kernel(x, pre_mix_norm, w_in, fox_f_bias, fox_out_norm, gdn_conv_w, gdn_a_log, gdn_dt_bias, gdn_out_norm, w_out, post_mix_norm, pre_mlp_norm, w_up, w_down, post_mlp_norm, loss_target, m_pre_mix_norm, m_w_in, m_fox_f_bias, m_fox_out_norm, m_gdn_conv_w, m_gdn_a_log, m_gdn_dt_bias, m_gdn_out_norm, m_w_out, m_post_mix_norm, m_pre_mlp_norm, m_w_up, m_w_down, m_post_mlp_norm, v_pre_mix_norm, v_w_in, v_fox_f_bias, v_fox_out_norm, v_gdn_conv_w, v_gdn_a_log, v_gdn_dt_bias, v_gdn_out_norm, v_w_out, v_post_mix_norm, v_pre_mlp_norm, v_w_up, v_w_down, v_post_mlp_norm):
    w = dict(pre_mix_norm=pre_mix_norm, w_in=w_in, fox_f_bias=fox_f_bias, fox_out_norm=fox_out_norm,
             gdn_conv_w=gdn_conv_w, gdn_a_log=gdn_a_log, gdn_dt_bias=gdn_dt_bias, gdn_out_norm=gdn_out_norm, w_out=w_out,
             post_mix_norm=post_mix_norm, pre_mlp_norm=pre_mlp_norm, w_up=w_up, w_down=w_down, post_mlp_norm=post_mlp_norm)
    mom = dict(pre_mix_norm=m_pre_mix_norm, w_in=m_w_in, fox_f_bias=m_fox_f_bias, fox_out_norm=m_fox_out_norm,
               gdn_conv_w=m_gdn_conv_w, gdn_a_log=m_gdn_a_log, gdn_dt_bias=m_gdn_dt_bias, gdn_out_norm=m_gdn_out_norm,
               w_out=m_w_out, post_mix_norm=m_post_mix_norm, pre_mlp_norm=m_pre_mlp_norm, w_up=m_w_up, w_down=m_w_down,
               post_mlp_norm=m_post_mlp_norm)
    var = dict(pre_mix_norm=v_pre_mix_norm, w_in=v_w_in, fox_f_bias=v_fox_f_bias, fox_out_norm=v_fox_out_norm,
               gdn_conv_w=v_gdn_conv_w, gdn_a_log=v_gdn_a_log, gdn_dt_bias=v_gdn_dt_bias, gdn_out_norm=v_gdn_out_norm,
               w_out=v_w_out, post_mix_norm=v_post_mix_norm, pre_mlp_norm=v_pre_mlp_norm, w_up=v_w_up, w_down=v_w_down,
               post_mlp_norm=v_post_mlp_norm)

    (win_g, conv_g), zones, shards = _all_gather([w_in.T.astype(BF), gdn_conv_w],
                                                 [(w_out, False), (w_up, True), (w_down, False)], BF)
    wt_al = _shuffle_rows([win_g], _to_aligned_moves(), (1, PROJ_W, D_MODEL), "w_in_to_aligned")[0]
    convw = _cols_from_pieces(conv_g)
    sems, shards, zones, after = _exchange_start(shards, zones, False, "gather_start", chips=True)
    gathers = dict(hop=(sems, shards, zones))

    def late_weights(name, after):
        if name == "mlp_relay":
            sems, shards, zones = gathers.pop("hop")
            _, zones = _exchange_wait(sems, shards, zones, after, "gather_wait", chips=True)
            sems, _, zones, token = _relay_start(zones, "gather_relay")
            gathers.update(w_out=(sems[:2], zones[:1]), mlp=(sems[2:], zones[1:]))
            return token
        sems, zones = gathers[name]
        _, got = _exchange_wait(sems, [], zones, after, "gather_" + name + "_done", n_copies=len(CHIP_FLIPS))
        if name == "w_out":
            return got[0].reshape(D_MODEL, D_MODEL)
        return got[0].reshape(D_FF, D_MODEL), got[1].reshape(D_FF, D_MODEL)

    scatters = {}

    def on_grads(name, g):
        if name == "mlp":
            scatters["mlp"] = list(g)
            return g[0]
        if name == "w_out":
            sems, srcs, zones, token = _exchange_start(scatters["mlp"] + [g], None, True, "scatter_mlp_w_out_start")
            scatters["mlp"] = (sems[:4], srcs[:2], zones[:2], token)
            scatters["w_out"] = (sems[4:], srcs[2:], zones[2:], token)
            return token
        g = _shuffle_rows(list(g), _from_aligned_moves(), (N_DEV, W_IN_PIECE, D_MODEL), "w_in_grad_from_aligned")
        scatters[name] = _exchange_start([_pair_reduce(g, "pair_reduce_w_in")], None, True, "scatter_w_in_start", chips=True)
        return scatters[name][3]

    grad_x, small = _local_step(
        x[0], loss_target[0], wt_al, after, late_weights, on_grads, convw, pre_mix_norm,
        fox_f_bias, fox_out_norm, gdn_a_log, gdn_dt_bias, gdn_out_norm, post_mix_norm, pre_mlp_norm, post_mlp_norm)
    slabs = [_small_pack(small), _conv_slabs(jnp.concatenate(small["conv"], axis=1))]
    scatters["small"] = _exchange_start(slabs, None, False, "small_start")

    grads, delta, new_m, new_v = {}, {}, {}, {}
    after = scatters["small"][3]
    for name, members in (("mlp", ("w_up", "w_down")), ("w_out", ("w_out",)), ("small", ()), ("w_in", ("w_in",))):
        sems, srcs, zones, _ = scatters[name]
        srcs, zones = _exchange_wait(sems, srcs, zones, after, "scatter_" + name + "_wait", chips=name == "w_in")
        if name == "small":
            loss, updated = _small_update(*zones, *srcs, w, mom, var)
            for n, res in updated.items():
                grads[n], delta[n], new_m[n], new_v[n] = res
            after = grads["pre_mix_norm"]
        for n, zone, own in zip(members, zones, srcs):
            if n == "w_in":
                res = _sum_adamw(zone, own, w[n].T, mom[n].T, var[n].T, "adamw_" + n, chips=True)
                grads[n], delta[n], new_m[n], new_v[n] = [r.T for r in res]
            else:
                grads[n], delta[n], new_m[n], new_v[n] = _sum_adamw(zone, own, w[n], mom[n], var[n], "adamw_" + n)
        if members:
            after = [grads[n] for n in members]

    return (loss, grad_x[None], *[grads[n] for n in WEIGHT_ORDER], *[delta[n] for n in WEIGHT_ORDER],
            *[new_m[n] for n in WEIGHT_ORDER], *[new_v[n] for n in WEIGHT_ORDER])
```
